```python
import math
import jax, jax.numpy as jnp
from jax import lax
import numpy as np

D_MODEL = 1024
BATCH = 8
SEQ = 4096
DEPTH = 1

SB_HEADS = 8
SB_HEAD_DIM = 64
SB_WIDTH = SB_HEADS * SB_HEAD_DIM
MLA_HEADS = 8
MLA_NOPE_DIM = 64
MLA_ROPE_DIM = 32
MLA_V_DIM = 64
MLA_Q_RANK = 384
MLA_KV_RANK = 256
MLA_WIDTH = MLA_HEADS * MLA_V_DIM
MLA_QK_DIM = MLA_NOPE_DIM + MLA_ROPE_DIM

Q_BLOCK = 128
ROPE_BASE = 10000.0
EPS = 1e-6

SPLITS = (SB_WIDTH, SB_WIDTH, SB_WIDTH, SB_WIDTH,
          MLA_Q_RANK, MLA_KV_RANK, MLA_ROPE_DIM, MLA_WIDTH,
          D_MODEL, D_MODEL)
IN_WIDTH = int(sum(SPLITS))
SPLIT_POINTS = tuple(int(v) for v in np.cumsum(SPLITS)[:-1])

kernel_name = "hybrid_stickbreaking_mla_adaln_block"


def rmsnorm(x, g):
    xf = x.astype(jnp.float32)
    y = xf * lax.rsqrt(jnp.mean(xf * xf, axis=-1, keepdims=True) + EPS)
    return (y * g.astype(jnp.float32)).astype(x.dtype)


def rope(x, positions):
    r = x.shape[-1]
    inv_freq = ROPE_BASE ** (-jnp.arange(0, r, 2, dtype=jnp.float32) / r)
    ang = positions.astype(jnp.float32)[:, :, None, None] * inv_freq
    cos, sin = jnp.cos(ang), jnp.sin(ang)
    xf = x.astype(jnp.float32)
    x1, x2 = xf[..., : r // 2], xf[..., r // 2:]
    return jnp.concatenate([x1 * cos - x2 * sin, x1 * sin + x2 * cos], axis=-1)


def to_blocks(t):
    b, h, s, d = t.shape
    return t.reshape(b, h, s // Q_BLOCK, Q_BLOCK, d).transpose(2, 0, 1, 3, 4)


def from_blocks(t):
    n, b, h, q, d = t.shape
    return t.transpose(1, 2, 0, 3, 4).reshape(b, h, n * q, d)


def stick_breaking_attention(q, k, v):
    s_len = k.shape[2]
    scale = 1.0 / math.sqrt(q.shape[-1])
    key_idx = jnp.arange(s_len)

    def block(args):
        qb, i = args
        z = jnp.einsum('bhqd,bhkd->bhqk', qb, k) * scale
        q_idx = i * Q_BLOCK + jnp.arange(Q_BLOCK)
        strict = key_idx[None, :] < q_idx[:, None]
        log_one_minus = jnp.where(strict, jax.nn.log_sigmoid(-z), 0.0)
        suffix = lax.cumsum(log_one_minus, axis=3, reverse=True) - log_one_minus
        w = jnp.where(strict, jnp.exp(jax.nn.log_sigmoid(z) + suffix), 0.0)
        return jnp.einsum('bhqk,bhkd->bhqd', w, v)

    n_blk = s_len // Q_BLOCK
    out = lax.map(block, (to_blocks(q), jnp.arange(n_blk)))
    return from_blocks(out)


def latent_attention(q_nope, q_pe, k_nope, k_pe, v):
    s_len = k_nope.shape[2]
    scale = 1.0 / math.sqrt(MLA_QK_DIM)
    key_idx = jnp.arange(s_len)

    def block(args):
        qn, qp, i = args
        sc = (jnp.einsum('bhqd,bhkd->bhqk', qn, k_nope)
              + jnp.einsum('bhqr,bkr->bhqk', qp, k_pe)) * scale
        q_idx = i * Q_BLOCK + jnp.arange(Q_BLOCK)
        causal = key_idx[None, :] <= q_idx[:, None]
        p = jax.nn.softmax(jnp.where(causal, sc, -jnp.inf), axis=-1)
        return jnp.einsum('bhqk,bhkd->bhqd', p, v)

    n_blk = s_len // Q_BLOCK
    out = lax.map(block, (to_blocks(q_nope), to_blocks(q_pe), jnp.arange(n_blk)))
    return from_blocks(out)


def split_heads(t, h):
    b, s, w = t.shape
    return t.reshape(b, s, h, w // h).transpose(0, 2, 1, 3)


def merge_heads(t):
    b, h, s, d = t.shape
    return t.transpose(0, 2, 1, 3).reshape(b, s, h * d)


def _fwd_setup_inputs(seed: int = 0) -> dict:
    key = jax.random.key(seed)
    ks = jax.random.split(key, 16)
    f32 = jnp.float32

    def nrm(k, shape, fan_in, mult=1.0):
        return jax.random.normal(k, shape, f32) * (mult * fan_in ** -0.5)

    x = jax.random.normal(ks[0], (BATCH, SEQ, D_MODEL), f32)
    c = jax.random.normal(ks[1], (BATCH, D_MODEL), f32)
    positions = jnp.broadcast_to(jnp.arange(SEQ, dtype=jnp.int32), (BATCH, SEQ))
    w_ada = nrm(ks[2], (DEPTH, D_MODEL, 3 * D_MODEL), D_MODEL, 0.2)
    b_ada = jax.random.normal(ks[3], (DEPTH, 3 * D_MODEL), f32) * 0.02
    norm_gain = 1.0 + 0.02 * jax.random.normal(ks[4], (DEPTH, D_MODEL), f32)
    w_in = nrm(ks[5], (DEPTH, D_MODEL, IN_WIDTH), D_MODEL)
    q_norm_gain = 1.0 + 0.02 * jax.random.normal(ks[6], (DEPTH, MLA_Q_RANK), f32)
    w_uq = nrm(ks[7], (DEPTH, MLA_Q_RANK, MLA_HEADS * MLA_QK_DIM), MLA_Q_RANK)
    kv_norm_gain = 1.0 + 0.02 * jax.random.normal(ks[8], (DEPTH, MLA_KV_RANK), f32)
    w_ukv = nrm(ks[9], (DEPTH, MLA_KV_RANK, MLA_HEADS * (MLA_NOPE_DIM + MLA_V_DIM)), MLA_KV_RANK)
    w_branch_a = nrm(ks[10], (DEPTH, SB_WIDTH, D_MODEL), SB_WIDTH)
    w_branch_b = nrm(ks[11], (DEPTH, MLA_WIDTH, D_MODEL), MLA_WIDTH)
    w_out = nrm(ks[12], (DEPTH, D_MODEL, D_MODEL), D_MODEL)
    final_norm_gain = 1.0 + 0.02 * jax.random.normal(ks[13], (D_MODEL,), f32)
    return {"x": x, "c": c, "positions": positions, "w_ada": w_ada, "b_ada": b_ada,
            "norm_gain": norm_gain, "w_in": w_in, "q_norm_gain": q_norm_gain, "w_uq": w_uq,
            "kv_norm_gain": kv_norm_gain, "w_ukv": w_ukv, "w_branch_a": w_branch_a,
            "w_branch_b": w_branch_b, "w_out": w_out, "final_norm_gain": final_norm_gain}


def _fwd_reference(x, c, positions, w_ada, b_ada, norm_gain, w_in, q_norm_gain, w_uq,
              kv_norm_gain, w_ukv, w_branch_a, w_branch_b, w_out, final_norm_gain):
    b, s, _ = x.shape
    f32 = jnp.float32
    for l in range(DEPTH):
        mod = c @ w_ada[l] + b_ada[l]
        shift, scale, gate = jnp.split(mod, 3, axis=-1)
        h = rmsnorm(x, norm_gain[l]) * (1.0 + scale[:, None, :]) + shift[:, None, :]

        proj = h @ w_in[l]
        (sb_q, sb_k, sb_v, sb_z, c_q, c_kv, k_rot, mla_z, g_a, g_b) = jnp.split(proj, SPLIT_POINTS, axis=-1)

        o_a = stick_breaking_attention(split_heads(sb_q, SB_HEADS).astype(f32),
                                       split_heads(sb_k, SB_HEADS).astype(f32),
                                       split_heads(sb_v, SB_HEADS).astype(f32))
        o_a = merge_heads(o_a).astype(x.dtype)
        y_a = (o_a * jax.nn.silu(sb_z)) @ w_branch_a[l]

        q = (rmsnorm(c_q, q_norm_gain[l]) @ w_uq[l]).reshape(b, s, MLA_HEADS, MLA_QK_DIM)
        q_nope = q[..., :MLA_NOPE_DIM].astype(f32)
        q_pe = rope(q[..., MLA_NOPE_DIM:], positions)
        kv = (rmsnorm(c_kv, kv_norm_gain[l]) @ w_ukv[l]).reshape(b, s, MLA_HEADS, MLA_NOPE_DIM + MLA_V_DIM)
        k_nope = kv[..., :MLA_NOPE_DIM].astype(f32)
        v_b = kv[..., MLA_NOPE_DIM:].astype(f32)
        k_pe = rope(k_rot[:, :, None, :], positions)[:, :, 0, :]
        o_b = latent_attention(q_nope.transpose(0, 2, 1, 3), q_pe.transpose(0, 2, 1, 3),
                               k_nope.transpose(0, 2, 1, 3), k_pe, v_b.transpose(0, 2, 1, 3))
        o_b = merge_heads(o_b).astype(x.dtype)
        y_b = (o_b * jax.nn.silu(mla_z)) @ w_branch_b[l]

        merged = jax.nn.sigmoid(g_a) * y_a + jax.nn.sigmoid(g_b) * y_b
        x = x + gate[:, None, :] * (merged @ w_out[l])
    return rmsnorm(x, final_norm_gain)


import jax as _jax
import jax.numpy as _jnp

TWIN_FORMAT = 'train_step'
FWD_PARAMS = ['x', 'c', 'positions', 'w_ada', 'b_ada', 'norm_gain', 'w_in', 'q_norm_gain', 'w_uq', 'kv_norm_gain', 'w_ukv', 'w_branch_a', 'w_branch_b', 'w_out', 'final_norm_gain']
TWIN_WEIGHTS = ['w_ada', 'b_ada', 'norm_gain', 'w_in', 'q_norm_gain', 'w_uq', 'kv_norm_gain', 'w_ukv', 'w_branch_a', 'w_branch_b', 'w_out', 'final_norm_gain']
TWIN_DIFF_INPUT = 'x'
TWIN_INPUTS = ['x', 'c', 'positions', 'w_ada', 'b_ada', 'norm_gain', 'w_in', 'q_norm_gain', 'w_uq', 'kv_norm_gain', 'w_ukv', 'w_branch_a', 'w_branch_b', 'w_out', 'final_norm_gain', 'loss_target', 'm_w_ada', 'm_b_ada', 'm_norm_gain', 'm_w_in', 'm_q_norm_gain', 'm_w_uq', 'm_kv_norm_gain', 'm_w_ukv', 'm_w_branch_a', 'm_w_branch_b', 'm_w_out', 'm_final_norm_gain', 'v_w_ada', 'v_b_ada', 'v_norm_gain', 'v_w_in', 'v_q_norm_gain', 'v_w_uq', 'v_kv_norm_gain', 'v_w_ukv', 'v_w_branch_a', 'v_w_branch_b', 'v_w_out', 'v_final_norm_gain']
TWIN_OUTPUTS = ['loss', 'grad_x', 'grad_w_ada', 'grad_b_ada', 'grad_norm_gain', 'grad_w_in', 'grad_q_norm_gain', 'grad_w_uq', 'grad_kv_norm_gain', 'grad_w_ukv', 'grad_w_branch_a', 'grad_w_branch_b', 'grad_w_out', 'grad_final_norm_gain', 'delta_w_ada', 'delta_b_ada', 'delta_norm_gain', 'delta_w_in', 'delta_q_norm_gain', 'delta_w_uq', 'delta_kv_norm_gain', 'delta_w_ukv', 'delta_w_branch_a', 'delta_w_branch_b', 'delta_w_out', 'delta_final_norm_gain', 'new_m_w_ada', 'new_m_b_ada', 'new_m_norm_gain', 'new_m_w_in', 'new_m_q_norm_gain', 'new_m_w_uq', 'new_m_kv_norm_gain', 'new_m_w_ukv', 'new_m_w_branch_a', 'new_m_w_branch_b', 'new_m_w_out', 'new_m_final_norm_gain', 'new_v_w_ada', 'new_v_b_ada', 'new_v_norm_gain', 'new_v_w_in', 'new_v_q_norm_gain', 'new_v_w_uq', 'new_v_kv_norm_gain', 'new_v_w_ukv', 'new_v_w_branch_a', 'new_v_w_branch_b', 'new_v_w_out', 'new_v_final_norm_gain']
TWIN_LEAF_KINDS = {'loss': 'loss', 'grad_x': 'grad_x', 'grad_w_ada': 'grad_w', 'grad_b_ada': 'grad_w', 'grad_norm_gain': 'grad_w', 'grad_w_in': 'grad_w', 'grad_q_norm_gain': 'grad_w', 'grad_w_uq': 'grad_w', 'grad_kv_norm_gain': 'grad_w', 'grad_w_ukv': 'grad_w', 'grad_w_branch_a': 'grad_w', 'grad_w_branch_b': 'grad_w', 'grad_w_out': 'grad_w', 'grad_final_norm_gain': 'grad_w', 'delta_w_ada': 'delta_w', 'delta_b_ada': 'delta_w', 'delta_norm_gain': 'delta_w', 'delta_w_in': 'delta_w', 'delta_q_norm_gain': 'delta_w', 'delta_w_uq': 'delta_w', 'delta_kv_norm_gain': 'delta_w', 'delta_w_ukv': 'delta_w', 'delta_w_branch_a': 'delta_w', 'delta_w_branch_b': 'delta_w', 'delta_w_out': 'delta_w', 'delta_final_norm_gain': 'delta_w', 'new_m_w_ada': 'new_m', 'new_m_b_ada': 'new_m', 'new_m_norm_gain': 'new_m', 'new_m_w_in': 'new_m', 'new_m_q_norm_gain': 'new_m', 'new_m_w_uq': 'new_m', 'new_m_kv_norm_gain': 'new_m', 'new_m_w_ukv': 'new_m', 'new_m_w_branch_a': 'new_m', 'new_m_w_branch_b': 'new_m', 'new_m_w_out': 'new_m', 'new_m_final_norm_gain': 'new_m', 'new_v_w_ada': 'new_v', 'new_v_b_ada': 'new_v', 'new_v_norm_gain': 'new_v', 'new_v_w_in': 'new_v', 'new_v_q_norm_gain': 'new_v', 'new_v_w_uq': 'new_v', 'new_v_kv_norm_gain': 'new_v', 'new_v_w_ukv': 'new_v', 'new_v_w_branch_a': 'new_v', 'new_v_w_branch_b': 'new_v', 'new_v_w_out': 'new_v', 'new_v_final_norm_gain': 'new_v'}


def _forward(args):
    return _fwd_reference(*[args[k] for k in FWD_PARAMS])


def _output_shape():
    def fwd():
        inp = _fwd_setup_inputs(0)
        return _fwd_reference(*[inp[k] for k in FWD_PARAMS])
    out = _jax.eval_shape(fwd)
    return out.shape, out.dtype

N_MICROBATCH = 1
ADAM_LR = 0.001
ADAM_B1 = 0.9
ADAM_B2 = 0.999
ADAM_EPS = 1e-08
ADAM_WD = 0.01
ADAM_STEP = 10
PER_EXAMPLE_BATCH_AXIS = {'x': 0, 'c': 0, 'positions': 0, 'loss_target': 0}
SHARED_INPUTS = []
_WEIGHT_DTYPES = {'w_ada': _jnp.float32, 'b_ada': _jnp.float32, 'norm_gain': _jnp.float32, 'w_in': _jnp.float32, 'q_norm_gain': _jnp.float32, 'w_uq': _jnp.float32, 'kv_norm_gain': _jnp.float32, 'w_ukv': _jnp.float32, 'w_branch_a': _jnp.float32, 'w_branch_b': _jnp.float32, 'w_out': _jnp.float32, 'final_norm_gain': _jnp.float32}
MOMENT_SCALE = {'w_ada': 3.024684e-02, 'b_ada': 3.081229e-02, 'norm_gain': 1.476536e-02, 'w_in': 6.646815e-03, 'q_norm_gain': 3.681540e-03, 'w_uq': 2.481346e-03, 'kv_norm_gain': 7.782797e-03, 'w_ukv': 3.931351e-03, 'w_branch_a': 8.484417e-03, 'w_branch_b': 3.521253e-03, 'w_out': 9.178269e-03, 'final_norm_gain': 3.200744e+01}


def _to_microbatches(a, axis):
    t = _jnp.moveaxis(a, axis, 0)
    t = t.reshape((N_MICROBATCH, t.shape[0] // N_MICROBATCH) + t.shape[1:])
    return _jnp.moveaxis(t, 1, axis + 1)


def setup_inputs(seed: int = 0) -> dict:
    inp = _fwd_setup_inputs(seed)
    key = _jax.random.fold_in(_jax.random.key(seed), 7919)
    shape, _ = _output_shape()
    out = dict(inp)
    out["loss_target"] = _jax.random.normal(_jax.random.fold_in(key, 0), shape, _jnp.float32)
    for i, name in enumerate(TWIN_WEIGHTS):
        w = inp[name].astype(_jnp.float32)
        if MOMENT_SCALE is None:
            s = _jnp.sqrt(_jnp.mean(_jnp.square(w)) + 1e-30)
        else:
            s = MOMENT_SCALE[name]
        km, kv = _jax.random.split(_jax.random.fold_in(key, i + 1))
        out[name] = w
        out["m_" + name] = s * _jax.random.normal(km, w.shape, _jnp.float32)
        out["v_" + name] = (s * s) * _jax.random.uniform(kv, w.shape, _jnp.float32, 0.5, 1.5)
    if N_MICROBATCH > 1:
        for name, axis in PER_EXAMPLE_BATCH_AXIS.items():
            out[name] = _to_microbatches(out[name], axis)
    return {'x': out['x'], 'c': out['c'], 'positions': out['positions'], 'w_ada': out['w_ada'], 'b_ada': out['b_ada'], 'norm_gain': out['norm_gain'], 'w_in': out['w_in'], 'q_norm_gain': out['q_norm_gain'], 'w_uq': out['w_uq'], 'kv_norm_gain': out['kv_norm_gain'], 'w_ukv': out['w_ukv'], 'w_branch_a': out['w_branch_a'], 'w_branch_b': out['w_branch_b'], 'w_out': out['w_out'], 'final_norm_gain': out['final_norm_gain'], 'loss_target': out['loss_target'], 'm_w_ada': out['m_w_ada'], 'm_b_ada': out['m_b_ada'], 'm_norm_gain': out['m_norm_gain'], 'm_w_in': out['m_w_in'], 'm_q_norm_gain': out['m_q_norm_gain'], 'm_w_uq': out['m_w_uq'], 'm_kv_norm_gain': out['m_kv_norm_gain'], 'm_w_ukv': out['m_w_ukv'], 'm_w_branch_a': out['m_w_branch_a'], 'm_w_branch_b': out['m_w_branch_b'], 'm_w_out': out['m_w_out'], 'm_final_norm_gain': out['m_final_norm_gain'], 'v_w_ada': out['v_w_ada'], 'v_b_ada': out['v_b_ada'], 'v_norm_gain': out['v_norm_gain'], 'v_w_in': out['v_w_in'], 'v_q_norm_gain': out['v_q_norm_gain'], 'v_w_uq': out['v_w_uq'], 'v_kv_norm_gain': out['v_kv_norm_gain'], 'v_w_ukv': out['v_w_ukv'], 'v_w_branch_a': out['v_w_branch_a'], 'v_w_branch_b': out['v_w_branch_b'], 'v_w_out': out['v_w_out'], 'v_final_norm_gain': out['v_final_norm_gain']}


def _loss(weights, diff, rest, loss_target):
    with _jax.named_scope("forward"):
        args = {**rest, TWIN_DIFF_INPUT: diff, **{k: w.astype(_WEIGHT_DTYPES[k]) for k, w in weights.items()}}
        y = _forward(args)
    with _jax.named_scope("loss_head"):
        err = _jnp.square(y.astype(_jnp.float32) - loss_target)
        return 0.5 * _jnp.sum(_jnp.mean(err, axis=-1)) if err.ndim else 0.5 * err


def _adamw(w, g, m, v):
    m = ADAM_B1 * m + (1.0 - ADAM_B1) * g
    v = ADAM_B2 * v + (1.0 - ADAM_B2) * _jnp.square(g)
    m_hat = m / (1.0 - ADAM_B1 ** ADAM_STEP)
    v_hat = v / (1.0 - ADAM_B2 ** ADAM_STEP)
    delta = -ADAM_LR * (m_hat / (_jnp.sqrt(v_hat) + ADAM_EPS) + ADAM_WD * w)
    return delta, m, v


def reference(x, c, positions, w_ada, b_ada, norm_gain, w_in, q_norm_gain, w_uq, kv_norm_gain, w_ukv, w_branch_a, w_branch_b, w_out, final_norm_gain, loss_target, m_w_ada, m_b_ada, m_norm_gain, m_w_in, m_q_norm_gain, m_w_uq, m_kv_norm_gain, m_w_ukv, m_w_branch_a, m_w_branch_b, m_w_out, m_final_norm_gain, v_w_ada, v_b_ada, v_norm_gain, v_w_in, v_q_norm_gain, v_w_uq, v_kv_norm_gain, v_w_ukv, v_w_branch_a, v_w_branch_b, v_w_out, v_final_norm_gain):
    given = dict(x=x, c=c, positions=positions, w_ada=w_ada, b_ada=b_ada, norm_gain=norm_gain, w_in=w_in, q_norm_gain=q_norm_gain, w_uq=w_uq, kv_norm_gain=kv_norm_gain, w_ukv=w_ukv, w_branch_a=w_branch_a, w_branch_b=w_branch_b, w_out=w_out, final_norm_gain=final_norm_gain, loss_target=loss_target, m_w_ada=m_w_ada, m_b_ada=m_b_ada, m_norm_gain=m_norm_gain, m_w_in=m_w_in, m_q_norm_gain=m_q_norm_gain, m_w_uq=m_w_uq, m_kv_norm_gain=m_kv_norm_gain, m_w_ukv=m_w_ukv, m_w_branch_a=m_w_branch_a, m_w_branch_b=m_w_branch_b, m_w_out=m_w_out, m_final_norm_gain=m_final_norm_gain, v_w_ada=v_w_ada, v_b_ada=v_b_ada, v_norm_gain=v_norm_gain, v_w_in=v_w_in, v_q_norm_gain=v_q_norm_gain, v_w_uq=v_w_uq, v_kv_norm_gain=v_kv_norm_gain, v_w_ukv=v_w_ukv, v_w_branch_a=v_w_branch_a, v_w_branch_b=v_w_branch_b, v_w_out=v_w_out, v_final_norm_gain=v_final_norm_gain)
    weights = {n: given[n] for n in TWIN_WEIGHTS}
    shared = {n: given[n] for n in SHARED_INPUTS}
    per_example = {n: given[n] for n in ['x', 'c', 'positions']}
    grad_fn = _jax.value_and_grad(_loss, argnums=(0, 1))

    def one_microbatch(ex, loss_target):
        ex = dict(ex)
        diff = ex.pop(TWIN_DIFF_INPUT)
        return grad_fn(weights, diff, {**shared, **ex}, loss_target)

    if N_MICROBATCH == 1:
        loss, (grad_w, grad_x) = one_microbatch(per_example, given["loss_target"])
    else:
        def body(carry, xs):
            loss_sum, grad_sum = carry
            l_k, (gw_k, gx_k) = one_microbatch(xs[0], xs[1])
            with _jax.named_scope("update"):
                return (loss_sum + l_k, _jax.tree.map(_jnp.add, grad_sum, gw_k)), gx_k

        init = (_jnp.zeros((), _jnp.float32), _jax.tree.map(_jnp.zeros_like, weights))
        (loss, grad_w), grad_x = _jax.lax.scan(body, init, (per_example, given["loss_target"]))
    with _jax.named_scope("update"):
        delta_w, new_m, new_v = {}, {}, {}
        for n in TWIN_WEIGHTS:
            delta_w[n], new_m[n], new_v[n] = _adamw(weights[n], grad_w[n], given["m_" + n], given["v_" + n])
    return (loss, grad_x, *[grad_w[n] for n in TWIN_WEIGHTS], *[delta_w[n] for n in TWIN_WEIGHTS],
            *[new_m[n] for n in TWIN_WEIGHTS], *[new_v[n] for n in TWIN_WEIGHTS])
```

```python
import functools
import math

import jax
import jax.numpy as jnp
from jax import lax
from jax.experimental import pallas as pl
from jax.experimental.pallas import tpu as pltpu

F32 = jnp.float32
BF16 = jnp.bfloat16

D_MODEL = 1024
SB_WIDTH = 512
MLA_WIDTH = 512
Q_RANK = 384
KV_RANK = 256
ROPE_DIM = 32
N_HEADS = 8
IN_WIDTH = 5280
EPS = 1e-6
ROPE_BASE = 10000.0
MLA_SCALE = 1.0 / math.sqrt(96.0)

ADAM_LR = 0.001
ADAM_B1 = 0.9
ADAM_B2 = 0.999
ADAM_EPS = 1e-08
ADAM_WD = 0.01
ADAM_STEP = 10

O_SQ, O_SK, O_SV, O_SZ, O_CQ, O_CKV, O_MZ, O_GA, O_GB, O_KR, O_END = (
    0, 512, 1024, 1536, 2048, 2432, 2688, 3200, 4224, 5248, 5504)
W_INT = 5632

N_CHIPS = 4
N_DEV = 8
LANES = 128
PACK_ROWS = (10560, 576, 512, 1024, 1024, 2048)
PACK_TOTAL = sum(PACK_ROWS)
HALF_ROWS = PACK_TOTAL // 2
SV_COLS = 768

ROW_TILE = 256
ATT_TILE = 256
VMEM_LIMIT = 56 * 1024 * 1024

MESH = pl.DeviceIdType.MESH


def _dot(a, b):
    return lax.dot_general(a, b, (((1,), (0,)), ((), ())), preferred_element_type=F32)


def _dot_nt(a, b):
    return lax.dot_general(a, b, (((1,), (1,)), ((), ())), preferred_element_type=F32)


def _dot_tn(a, b):
    return lax.dot_general(a, b, (((0,), (0,)), ((), ())), preferred_element_type=F32)


def _sigmoid(z):
    return 1.0 / (1.0 + jnp.exp(-z))


def _params(sem=None):
    if sem is None:
        return pltpu.CompilerParams(vmem_limit_bytes=VMEM_LIMIT)
    return pltpu.CompilerParams(dimension_semantics=sem, vmem_limit_bytes=VMEM_LIMIT)


def _rows(tm, n):
    return pl.BlockSpec((tm, n), lambda i: (i, 0))


def _whole(shape):
    nd = len(shape)
    return pl.BlockSpec(shape, lambda i: (0,) * nd)


def _sds(shape, dtype):
    return jax.ShapeDtypeStruct(shape, dtype)


def _flip(v, d):
    return 1 - v if d else v


def _gather_call(c_row, w_ada_sh, pack):
    def body(c_ref, wada_ref, pk_ref, mg_ref, cg_ref, gw_ref,
             cv, ssem_c, rsem_c, ssem_m, rsem_m, ssem_w, rsem_w, lsem):
        x, y, c = lax.axis_index("x"), lax.axis_index("y"), lax.axis_index("c")
        me = 4 * x + 2 * y + c
        chip = 2 * x + y
        rel3 = [(1, 0), (0, 1), (1, 1)]

        wcopies = []
        for j, (dx, dy) in enumerate(rel3):
            cp = pltpu.make_async_remote_copy(
                src_ref=pk_ref, dst_ref=gw_ref.at[chip], send_sem=ssem_w.at[j], recv_sem=rsem_w.at[j],
                device_id=(_flip(x, dx), _flip(y, dy), c), device_id_type=MESH)
            cp.start()
            wcopies.append(cp)
        own = pltpu.make_async_copy(pk_ref, gw_ref.at[chip], lsem)
        own.start()

        cv[me] = c_ref[...]
        ccopies = []
        for r in range(1, N_DEV):
            dx, dy, dc = (r >> 2) & 1, (r >> 1) & 1, r & 1
            cp = pltpu.make_async_remote_copy(
                src_ref=c_ref, dst_ref=cv.at[me], send_sem=ssem_c.at[r - 1], recv_sem=rsem_c.at[r - 1],
                device_id=(_flip(x, dx), _flip(y, dy), _flip(c, dc)), device_id_type=MESH)
            cp.start()
            ccopies.append(cp)
        for r in range(1, N_DEV):
            dx, dy, dc = (r >> 2) & 1, (r >> 1) & 1, r & 1
            src = 4 * _flip(x, dx) + 2 * _flip(y, dy) + _flip(c, dc)
            pltpu.make_async_remote_copy(
                src_ref=c_ref, dst_ref=cv.at[src], send_sem=ssem_c.at[r - 1], recv_sem=rsem_c.at[r - 1],
                device_id=(x, y, c), device_id_type=MESH).wait_recv()
        rows = lax.broadcasted_iota(jnp.int32, (N_DEV, D_MODEL), 0)
        call = jnp.zeros((N_DEV, D_MODEL), F32)
        for b in range(N_DEV):
            call = jnp.where(rows == b, jnp.broadcast_to(cv[b], (N_DEV, D_MODEL)), call)
        cg_ref[...] = call

        mg_ref[chip] = lax.dot_general(call, wada_ref[...], (((1,), (0,)), ((), ())),
                                       precision=lax.Precision.HIGHEST, preferred_element_type=F32)
        mcopies = []
        for j, (dx, dy) in enumerate(rel3):
            cp = pltpu.make_async_remote_copy(
                src_ref=mg_ref.at[chip], dst_ref=mg_ref.at[chip], send_sem=ssem_m.at[j], recv_sem=rsem_m.at[j],
                device_id=(_flip(x, dx), _flip(y, dy), c), device_id_type=MESH)
            cp.start()
            mcopies.append(cp)
        for j, (dx, dy) in enumerate(rel3):
            src_chip = 2 * _flip(x, dx) + _flip(y, dy)
            pltpu.make_async_remote_copy(
                src_ref=mg_ref.at[src_chip], dst_ref=mg_ref.at[src_chip], send_sem=ssem_m.at[j],
                recv_sem=rsem_m.at[j], device_id=(x, y, c), device_id_type=MESH).wait_recv()
            pltpu.make_async_remote_copy(
                src_ref=pk_ref, dst_ref=gw_ref.at[src_chip], send_sem=ssem_w.at[j], recv_sem=rsem_w.at[j],
                device_id=(x, y, c), device_id_type=MESH).wait_recv()
        for cp in ccopies + mcopies + wcopies:
            cp.wait_send()
        own.wait()

    vmem = pl.BlockSpec(memory_space=pltpu.VMEM)
    hbm = pl.BlockSpec(memory_space=pl.ANY)
    return pl.pallas_call(
        body, name="gather_fwd",
        out_shape=(_sds((N_CHIPS, N_DEV, 768), F32), _sds((N_DEV, D_MODEL), F32),
                   _sds((N_CHIPS, PACK_TOTAL, LANES), BF16)),
        in_specs=[vmem, vmem, hbm], out_specs=(vmem, vmem, hbm),
        scratch_shapes=[
            pltpu.VMEM((N_DEV, 1, D_MODEL), F32),
            pltpu.SemaphoreType.DMA((N_DEV - 1,)), pltpu.SemaphoreType.DMA((N_DEV - 1,)),
            pltpu.SemaphoreType.DMA((3,)), pltpu.SemaphoreType.DMA((3,)),
            pltpu.SemaphoreType.DMA((3,)), pltpu.SemaphoreType.DMA((3,)),
            pltpu.SemaphoreType.DMA,
        ],
        compiler_params=_params(),
    )(c_row, w_ada_sh, pack)


def _scatter_call(gpack, sv):
    def body(g_ref, sv_ref, land_ref, svg_ref, ssem_g, rsem_g, ssem_s, rsem_s, lsem):
        x, y, c = lax.axis_index("x"), lax.axis_index("y"), lax.axis_index("c")
        me = 4 * x + 2 * y + c
        copies = []
        for r in range(1, N_DEV):
            dx, dy, dc = (r >> 2) & 1, (r >> 1) & 1, r & 1
            tx, ty, tc = _flip(x, dx), _flip(y, dy), _flip(c, dc)
            tgt = 4 * tx + 2 * ty + tc
            cp = pltpu.make_async_remote_copy(
                src_ref=g_ref.at[tgt], dst_ref=land_ref.at[me], send_sem=ssem_g.at[r - 1],
                recv_sem=rsem_g.at[r - 1], device_id=(tx, ty, tc), device_id_type=MESH)
            cp.start()
            copies.append(cp)
            cp = pltpu.make_async_remote_copy(
                src_ref=sv_ref, dst_ref=svg_ref.at[me], send_sem=ssem_s.at[r - 1],
                recv_sem=rsem_s.at[r - 1], device_id=(tx, ty, tc), device_id_type=MESH)
            cp.start()
            copies.append(cp)
        own = pltpu.make_async_copy(g_ref.at[me], land_ref.at[me], lsem)
        own.start()
        svg_ref[me] = sv_ref[...]
        for r in range(1, N_DEV):
            dx, dy, dc = (r >> 2) & 1, (r >> 1) & 1, r & 1
            src = 4 * _flip(x, dx) + 2 * _flip(y, dy) + _flip(c, dc)
            pltpu.make_async_remote_copy(
                src_ref=g_ref.at[src], dst_ref=land_ref.at[src], send_sem=ssem_g.at[r - 1],
                recv_sem=rsem_g.at[r - 1], device_id=(x, y, c), device_id_type=MESH).wait_recv()
            pltpu.make_async_remote_copy(
                src_ref=sv_ref, dst_ref=svg_ref.at[src], send_sem=ssem_s.at[r - 1],
                recv_sem=rsem_s.at[r - 1], device_id=(x, y, c), device_id_type=MESH).wait_recv()
        for cp in copies:
            cp.wait_send()
        own.wait()

    vmem = pl.BlockSpec(memory_space=pltpu.VMEM)
    hbm = pl.BlockSpec(memory_space=pl.ANY)
    return pl.pallas_call(
        body, name="grad_scatter",
        out_shape=(_sds((N_DEV, HALF_ROWS, LANES), F32), _sds((N_DEV, 8, SV_COLS), F32)),
        in_specs=[hbm, vmem], out_specs=(hbm, vmem),
        scratch_shapes=[
            pltpu.SemaphoreType.DMA((N_DEV - 1,)), pltpu.SemaphoreType.DMA((N_DEV - 1,)),
            pltpu.SemaphoreType.DMA((N_DEV - 1,)), pltpu.SemaphoreType.DMA((N_DEV - 1,)),
            pltpu.SemaphoreType.DMA,
        ],
        compiler_params=_params(),
    )(gpack, sv)


def _sum_call(land):
    rows = land.shape[1]
    tr = 984

    def body(l_ref, o_ref):
        acc = l_ref[0]
        for d in range(1, N_DEV):
            acc = acc + l_ref[d]
        o_ref[...] = acc

    return pl.pallas_call(
        body, name="grad_sum", grid=(rows // tr,),
        out_shape=_sds((rows, LANES), F32),
        in_specs=[pl.BlockSpec((N_DEV, tr, LANES), lambda i: (0, i, 0))],
        out_specs=pl.BlockSpec((tr, LANES), lambda i: (i, 0)),
        compiler_params=_params(("parallel",)),
    )(land)


def _sibling_call(red):
    def body(r_ref, full_ref, ssem, rsem, lsem):
        x, y, c = lax.axis_index("x"), lax.axis_index("y"), lax.axis_index("c")
        cp = pltpu.make_async_remote_copy(
            src_ref=r_ref, dst_ref=full_ref.at[c], send_sem=ssem, recv_sem=rsem,
            device_id=(x, y, 1 - c), device_id_type=MESH)
        cp.start()
        own = pltpu.make_async_copy(r_ref, full_ref.at[c], lsem)
        own.start()
        pltpu.make_async_remote_copy(
            src_ref=r_ref, dst_ref=full_ref.at[1 - c], send_sem=ssem, recv_sem=rsem,
            device_id=(x, y, c), device_id_type=MESH).wait_recv()
        cp.wait_send()
        own.wait()

    hbm = pl.BlockSpec(memory_space=pl.ANY)
    return pl.pallas_call(
        body, name="grad_sibling",
        out_shape=_sds((2,) + red.shape, F32),
        in_specs=[hbm], out_specs=hbm,
        scratch_shapes=[pltpu.SemaphoreType.DMA, pltpu.SemaphoreType.DMA, pltpu.SemaphoreType.DMA],
        compiler_params=_params(),
    )(red)


def _inproj_call(x, shift, scale, g1, w_int, w_q, w_kv, qg, kvg, cos256, sin256):
    s_len = x.shape[0]
    tm = min(ROW_TILE, s_len)

    def body(x_ref, sh_ref, sc_ref, g1_ref, w_ref, wq_ref, wkv_ref, qg_ref, kvg_ref, cos_ref, sin_ref,
             h_ref, sq_ref, sk_ref, sv_ref, sz_ref, cq_ref, ckv_ref, mz_ref, ga_ref, gb_ref, kpt_ref,
             qn_ref, qp_ref, kn_ref, vv_ref):
        xt = x_ref[...]
        r = lax.rsqrt(jnp.mean(xt * xt, axis=-1, keepdims=True) + EPS)
        h = (xt * r * g1_ref[...]) * (1.0 + sc_ref[...]) + sh_ref[...]
        hb = h.astype(BF16)
        h_ref[...] = hb

        def seg(a, b):
            return _dot(hb, w_ref[:, a:b])

        sq_ref[...] = seg(O_SQ, O_SK).astype(BF16)
        sk_ref[...] = seg(O_SK, O_SV).astype(BF16)
        sv_ref[...] = seg(O_SV, O_SZ).astype(BF16)
        sz_ref[...] = seg(O_SZ, O_CQ)
        mz_ref[...] = seg(O_MZ, O_GA)
        ga_ref[...] = seg(O_GA, O_GB)
        gb_ref[...] = seg(O_GB, O_KR)
        cos = cos_ref[...]
        sin = sin_ref[...]
        kr = seg(O_KR, O_END)
        kpt_ref[...] = (kr[:, :128] * cos[:, :128] + kr[:, 128:] * sin[:, :128]).astype(BF16)

        cq = seg(O_CQ, O_CKV)
        cq_ref[...] = cq
        rq = lax.rsqrt(jnp.mean(cq * cq, axis=-1, keepdims=True) + EPS)
        cqn = (cq * rq * qg_ref[...]).astype(BF16)
        qa = _dot(cqn, wq_ref[...])
        qn_ref[...] = qa[:, :512].astype(BF16)
        qp_ref[...] = (qa[:, 512:768] * cos + qa[:, 768:] * sin).astype(BF16)

        ckv = seg(O_CKV, O_MZ)
        ckv_ref[...] = ckv
        rk = lax.rsqrt(jnp.mean(ckv * ckv, axis=-1, keepdims=True) + EPS)
        ckvn = (ckv * rk * kvg_ref[...]).astype(BF16)
        kva = _dot(ckvn, wkv_ref[...])
        kn_ref[...] = kva[:, :512].astype(BF16)
        vv_ref[...] = kva[:, 512:].astype(BF16)

    outs = [
        (D_MODEL, BF16), (512, BF16), (512, BF16), (512, BF16), (512, F32), (Q_RANK, F32), (KV_RANK, F32),
        (512, F32), (D_MODEL, F32), (D_MODEL, F32), (128, BF16), (512, BF16), (256, BF16), (512, BF16), (512, BF16),
    ]
    return pl.pallas_call(
        body, name="inproj", grid=(s_len // tm,),
        out_shape=tuple(_sds((s_len, n), dt) for n, dt in outs),
        in_specs=[_rows(tm, D_MODEL), _whole((1, D_MODEL)), _whole((1, D_MODEL)), _whole((1, D_MODEL)),
                  _whole((D_MODEL, W_INT)), _whole((Q_RANK, 1024)), _whole((KV_RANK, 1024)),
                  _whole((1, Q_RANK)), _whole((1, KV_RANK)), _rows(tm, 256), _rows(tm, 256)],
        out_specs=tuple(_rows(tm, n) for n, _ in outs),
        compiler_params=_params(("parallel",)),
    )(x, shift, scale, g1, w_int, w_q, w_kv, qg, kvg, cos256, sin256)


def _softplus(z):
    return jnp.maximum(z, 0.0) + jnp.log(1.0 + jnp.exp(-jnp.abs(z)))


def _split_bf16(a):
    hi = a.astype(BF16)
    lo = (a - hi.astype(F32)).astype(BF16)
    return hi, lo


def _sb_fwd_call(q, k, v):
    s_len = q.shape[0]
    t = min(ATT_TILE, s_len)
    nq = s_len // t

    def body(q_ref, k_ref, v_ref, o_ref, lt_ref):
        i = pl.program_id(1)
        q2 = q_ref[...]
        lane = lax.broadcasted_iota(jnp.int32, (1, 128), 1)
        row = lax.broadcasted_iota(jnp.int32, (t, t), 0)
        col = lax.broadcasted_iota(jnp.int32, (t, t), 1)
        later = (row > col).astype(BF16)
        acc = jnp.zeros((t, 128), F32)
        for hh in range(2):
            hm = (lane // 64) == hh
            qm = jnp.where(hm, q2, jnp.zeros_like(q2))

            def step(jj, carry, qm=qm, hm=hm):
                run, acc = carry
                j = i - jj
                off = pl.multiple_of(j * t, t)
                kb = k_ref[pl.ds(off, t), :]
                vb = v_ref[pl.ds(off, t), :]
                z = _dot_nt(qm, kb)
                valid = (col + j * t) < (row + i * t)
                lg = -_softplus(z)
                lm = jnp.where(valid, lg, 0.0)
                hi, lo = _split_bf16(lm)
                suf = _dot(hi, later) + _dot(lo, later)
                w = jnp.where(valid, jnp.exp(z + lg + suf + run), 0.0)
                vm = jnp.where(hm, vb, jnp.zeros_like(vb))
                acc = acc + _dot(w.astype(BF16), vm)
                run = run + jnp.sum(lm, axis=1, keepdims=True)
                return run, acc

            run, acc = lax.fori_loop(0, i + 1, step, (jnp.zeros((t, 1), F32), acc))
            lt_ref[0, :, hh:hh + 1] = run
        o_ref[...] = acc

    return pl.pallas_call(
        body, name="sb_fwd", grid=(4, nq),
        out_shape=(_sds((s_len, SB_WIDTH), F32), _sds((4, s_len, 2), F32)),
        in_specs=[pl.BlockSpec((t, 128), lambda p, i: (i, p)),
                  pl.BlockSpec((s_len, 128), lambda p, i: (0, p)),
                  pl.BlockSpec((s_len, 128), lambda p, i: (0, p))],
        out_specs=(pl.BlockSpec((t, 128), lambda p, i: (i, p)),
                   pl.BlockSpec((1, t, 2), lambda p, i: (p, i, 0))),
        compiler_params=_params(("parallel", "parallel")),
    )(q, k, v)


def _sb_bwd_call(q, k, v, do, lt):
    s_len = q.shape[0]
    t = min(ATT_TILE, s_len)
    nq = s_len // t

    def body(q_ref, k_ref, v_ref, do_ref, lt_ref, dq_ref, dk_ref, dv_ref):
        i = pl.program_id(1)

        @pl.when(i == 0)
        def _():
            dk_ref[...] = jnp.zeros_like(dk_ref)
            dv_ref[...] = jnp.zeros_like(dv_ref)

        q2 = q_ref[...]
        do2 = do_ref[...].astype(BF16)
        lane = lax.broadcasted_iota(jnp.int32, (1, 128), 1)
        row = lax.broadcasted_iota(jnp.int32, (t, t), 0)
        col = lax.broadcasted_iota(jnp.int32, (t, t), 1)
        earlier = (row < col).astype(BF16)
        dq = jnp.zeros((t, 128), F32)
        for hh in range(2):
            hm = (lane // 64) == hh
            qm = jnp.where(hm, q2, jnp.zeros_like(q2))
            dom = jnp.where(hm, do2, jnp.zeros_like(do2))
            ltot = lt_ref[0, :, hh:hh + 1]

            def step(j, carry, qm=qm, dom=dom, hm=hm, ltot=ltot):
                lpre, ppre, dq = carry
                off = pl.multiple_of(j * t, t)
                kb = k_ref[pl.ds(off, t), :]
                vb = v_ref[pl.ds(off, t), :]
                z = _dot_nt(qm, kb)
                valid = (col + j * t) < (row + i * t)
                sp = _softplus(z)
                lg = -sp
                lm = jnp.where(valid, lg, 0.0)
                hi, lo = _split_bf16(lm)
                before = _dot(hi, earlier) + _dot(lo, earlier)
                between = ltot - (lpre + before + lm)
                a = jnp.where(valid, jnp.exp(z + lg + between), 0.0)
                da = _dot_nt(dom, vb)
                p = a * da
                phi, plo = _split_bf16(p)
                pbefore = ppre + _dot(phi, earlier) + _dot(plo, earlier)
                sig = jnp.exp(z - sp)
                dz = jnp.where(valid, p * (1.0 - sig) - sig * pbefore, 0.0).astype(BF16)
                km = jnp.where(hm, kb, jnp.zeros_like(kb))
                dq = dq + _dot(dz, km)
                dk_ref[pl.ds(off, t), :] += _dot_tn(dz, qm)
                dv_ref[pl.ds(off, t), :] += _dot_tn(a.astype(BF16), dom)
                lpre = lpre + jnp.sum(lm, axis=1, keepdims=True)
                ppre = ppre + jnp.sum(p, axis=1, keepdims=True)
                return lpre, ppre, dq

            zero = jnp.zeros((t, 1), F32)
            _, _, dq = lax.fori_loop(0, i + 1, step, (zero, zero, dq))
        dq_ref[...] = dq.astype(BF16)

    return pl.pallas_call(
        body, name="sb_bwd", grid=(4, nq),
        out_shape=(_sds((s_len, SB_WIDTH), BF16), _sds((s_len, SB_WIDTH), F32), _sds((s_len, SB_WIDTH), F32)),
        in_specs=[pl.BlockSpec((t, 128), lambda p, i: (i, p)),
                  pl.BlockSpec((s_len, 128), lambda p, i: (0, p)),
                  pl.BlockSpec((s_len, 128), lambda p, i: (0, p)),
                  pl.BlockSpec((t, 128), lambda p, i: (i, p)),
                  pl.BlockSpec((1, t, 2), lambda p, i: (p, i, 0))],
        out_specs=(pl.BlockSpec((t, 128), lambda p, i: (i, p)),
                   pl.BlockSpec((s_len, 128), lambda p, i: (0, p)),
                   pl.BlockSpec((s_len, 128), lambda p, i: (0, p))),
        compiler_params=_params(("parallel", "arbitrary")),
    )(q, k, v, do, lt)


def _mla_fwd_call(qn, qp, kn, kpt, v):
    s_len = qn.shape[0]
    t = min(ATT_TILE, s_len)
    nq = s_len // t

    def body(qn_ref, qp_ref, kn_ref, kpt_ref, v_ref, o_ref, lse_ref):
        i = pl.program_id(1)
        qn2 = qn_ref[...]
        qp2 = qp_ref[...]
        lane256 = lax.broadcasted_iota(jnp.int32, (1, 256), 1)
        lane128 = lax.broadcasted_iota(jnp.int32, (1, 128), 1)
        row = lax.broadcasted_iota(jnp.int32, (t, t), 0)
        col = lax.broadcasted_iota(jnp.int32, (t, t), 1)
        out = jnp.zeros((t, 256), F32)
        for hh in range(4):
            m64 = (lane256 // 64) == hh
            m32 = (lane128 // 32) == hh
            qc = jnp.concatenate([jnp.where(m64, qn2, jnp.zeros_like(qn2)),
                                  jnp.where(m32, qp2, jnp.zeros_like(qp2))], axis=1)

            def step(j, carry, qc=qc, m64=m64):
                m, l, acc = carry
                off = pl.multiple_of(j * t, t)
                kc = jnp.concatenate([kn_ref[pl.ds(off, t), :], kpt_ref[pl.ds(off, t), :]], axis=1)
                vb = v_ref[pl.ds(off, t), :]
                s = _dot_nt(qc, kc) * MLA_SCALE
                valid = (col + j * t) <= (row + i * t)
                s = jnp.where(valid, s, -1e30)
                mn = jnp.maximum(m, jnp.max(s, axis=1, keepdims=True))
                p = jnp.exp(s - mn)
                alpha = jnp.exp(m - mn)
                l = alpha * l + jnp.sum(p, axis=1, keepdims=True)
                vm = jnp.where(m64, vb, jnp.zeros_like(vb))
                acc = alpha * acc + _dot(p.astype(BF16), vm)
                return mn, l, acc

            m, l, acc = lax.fori_loop(
                0, i + 1, step,
                (jnp.full((t, 1), -1e30, F32), jnp.zeros((t, 1), F32), jnp.zeros((t, 256), F32)))
            out = out + acc / l
            lse_ref[0, :, hh:hh + 1] = m + jnp.log(l)
        o_ref[...] = out

    return pl.pallas_call(
        body, name="mla_fwd", grid=(2, nq),
        out_shape=(_sds((s_len, MLA_WIDTH), F32), _sds((2, s_len, 4), F32)),
        in_specs=[pl.BlockSpec((t, 256), lambda g, i: (i, g)),
                  pl.BlockSpec((t, 128), lambda g, i: (i, g)),
                  pl.BlockSpec((s_len, 256), lambda g, i: (0, g)),
                  pl.BlockSpec((s_len, 128), lambda g, i: (0, 0)),
                  pl.BlockSpec((s_len, 256), lambda g, i: (0, g))],
        out_specs=(pl.BlockSpec((t, 256), lambda g, i: (i, g)),
                   pl.BlockSpec((1, t, 4), lambda g, i: (g, i, 0))),
        compiler_params=_params(("parallel", "parallel")),
    )(qn, qp, kn, kpt, v)


def _mla_bwd_call(qn, qp, kn, kpt, v, o, do, lse):
    s_len = qn.shape[0]
    t = min(ATT_TILE, s_len)
    nq = s_len // t

    def body(qn_ref, qp_ref, kn_ref, kpt_ref, v_ref, o_ref, do_ref, lse_ref,
             dqn_ref, dqp_ref, dkn_ref, dkpt_ref, dv_ref):
        g = pl.program_id(0)
        i = pl.program_id(1)

        @pl.when(i == 0)
        def _():
            dkn_ref[...] = jnp.zeros_like(dkn_ref)
            dv_ref[...] = jnp.zeros_like(dv_ref)

        @pl.when((i == 0) & (g == 0))
        def _():
            dkpt_ref[...] = jnp.zeros_like(dkpt_ref)

        qn2 = qn_ref[...]
        qp2 = qp_ref[...]
        of = o_ref[...]
        dof = do_ref[...]
        dob = dof.astype(BF16)
        prod = dof * of
        lane256 = lax.broadcasted_iota(jnp.int32, (1, 256), 1)
        lane128 = lax.broadcasted_iota(jnp.int32, (1, 128), 1)
        row = lax.broadcasted_iota(jnp.int32, (t, t), 0)
        col = lax.broadcasted_iota(jnp.int32, (t, t), 1)
        dqc = jnp.zeros((t, 384), F32)
        for hh in range(4):
            m64 = (lane256 // 64) == hh
            m32 = (lane128 // 32) == hh
            qc = jnp.concatenate([jnp.where(m64, qn2, jnp.zeros_like(qn2)),
                                  jnp.where(m32, qp2, jnp.zeros_like(qp2))], axis=1)
            dom = jnp.where(m64, dob, jnp.zeros_like(dob))
            dsum = jnp.sum(jnp.where(m64, prod, 0.0), axis=1, keepdims=True)
            lse = lse_ref[0, :, hh:hh + 1]

            def step(j, dqc, qc=qc, dom=dom, dsum=dsum, lse=lse, m64=m64, m32=m32):
                off = pl.multiple_of(j * t, t)
                knb = kn_ref[pl.ds(off, t), :]
                kpb = kpt_ref[pl.ds(off, t), :]
                vb = v_ref[pl.ds(off, t), :]
                kc = jnp.concatenate([knb, kpb], axis=1)
                s = _dot_nt(qc, kc) * MLA_SCALE
                valid = (col + j * t) <= (row + i * t)
                p = jnp.exp(jnp.where(valid, s, -1e30) - lse)
                dp = _dot_nt(dom, vb)
                ds = (p * (dp - dsum) * MLA_SCALE).astype(BF16)
                kcm = jnp.concatenate([jnp.where(m64, knb, jnp.zeros_like(knb)),
                                       jnp.where(m32, kpb, jnp.zeros_like(kpb))], axis=1)
                dqc = dqc + _dot(ds, kcm)
                dkc = _dot_tn(ds, qc)
                dkn_ref[pl.ds(off, t), :] += dkc[:, :256]
                dkpt_ref[pl.ds(off, t), :] += dkc[:, 256:]
                dv_ref[pl.ds(off, t), :] += _dot_tn(p.astype(BF16), dom)
                return dqc

            dqc = lax.fori_loop(0, i + 1, step, dqc)
        dqn_ref[...] = dqc[:, :256].astype(BF16)
        dqp_ref[...] = dqc[:, 256:].astype(BF16)

    return pl.pallas_call(
        body, name="mla_bwd", grid=(2, nq),
        out_shape=(_sds((s_len, 512), BF16), _sds((s_len, 256), BF16), _sds((s_len, 512), F32),
                   _sds((s_len, 128), F32), _sds((s_len, 512), F32)),
        in_specs=[pl.BlockSpec((t, 256), lambda g, i: (i, g)),
                  pl.BlockSpec((t, 128), lambda g, i: (i, g)),
                  pl.BlockSpec((s_len, 256), lambda g, i: (0, g)),
                  pl.BlockSpec((s_len, 128), lambda g, i: (0, 0)),
                  pl.BlockSpec((s_len, 256), lambda g, i: (0, g)),
                  pl.BlockSpec((t, 256), lambda g, i: (i, g)),
                  pl.BlockSpec((t, 256), lambda g, i: (i, g)),
                  pl.BlockSpec((1, t, 4), lambda g, i: (g, i, 0))],
        out_specs=(pl.BlockSpec((t, 256), lambda g, i: (i, g)),
                   pl.BlockSpec((t, 128), lambda g, i: (i, g)),
                   pl.BlockSpec((s_len, 256), lambda g, i: (0, g)),
                   pl.BlockSpec((s_len, 128), lambda g, i: (0, 0)),
                   pl.BlockSpec((s_len, 256), lambda g, i: (0, g))),
        compiler_params=_params(("arbitrary", "arbitrary")),
    )(qn, qp, kn, kpt, v, o, do, lse)


def _post_call(x, tgt, oa, ob, sz, mz, ga, gb, gate, gf, wa, wb, wo, wat, wbt, wot):
    s_len = x.shape[0]
    tm = min(ROW_TILE, s_len)

    def body(x_ref, t_ref, oa_ref, ob_ref, sz_ref, mz_ref, ga_ref, gb_ref, gate_ref, gf_ref,
             wa_ref, wb_ref, wo_ref, wat_ref, wbt_ref, wot_ref,
             dx2_ref, doa_ref, dob_ref, dsz_ref, dmz_ref, dga_ref, dgb_ref,
             dwo_ref, dwa_ref, dwb_ref, dgf_ref, dgate_ref, loss_ref):
        @pl.when(pl.program_id(0) == 0)
        def _():
            dwo_ref[...] = jnp.zeros_like(dwo_ref)
            dwa_ref[...] = jnp.zeros_like(dwa_ref)
            dwb_ref[...] = jnp.zeros_like(dwb_ref)
            dgf_ref[...] = jnp.zeros_like(dgf_ref)
            dgate_ref[...] = jnp.zeros_like(dgate_ref)
            loss_ref[...] = jnp.zeros_like(loss_ref)

        gate = gate_ref[...]
        gf = gf_ref[...]
        oa = oa_ref[...]
        ob = ob_ref[...]
        sz = sz_ref[...]
        mz = mz_ref[...]
        sa = _sigmoid(sz)
        sb = _sigmoid(mz)
        silu_a = sz * sa
        silu_b = mz * sb
        ua = (oa * silu_a).astype(BF16)
        ub = (ob * silu_b).astype(BF16)
        ya = _dot(ua, wa_ref[...])
        yb = _dot(ub, wb_ref[...])
        sga = _sigmoid(ga_ref[...])
        sgb = _sigmoid(gb_ref[...])
        merged = (sga * ya + sgb * yb).astype(BF16)
        out = _dot(merged, wo_ref[...])
        x2 = x_ref[...] + gate * out
        r2 = lax.rsqrt(jnp.mean(x2 * x2, axis=-1, keepdims=True) + EPS)
        xhat = x2 * r2
        err = xhat * gf - t_ref[...]
        loss_ref[...] += 0.5 * jnp.sum(jnp.sum(err * err, axis=1, keepdims=True), axis=0, keepdims=True) / D_MODEL
        dy = err * (1.0 / D_MODEL)
        dgf_ref[...] += jnp.sum(dy * xhat, axis=0, keepdims=True)
        dxhat = dy * gf
        dx2 = r2 * (dxhat - xhat * jnp.mean(dxhat * xhat, axis=-1, keepdims=True))
        dx2_ref[...] = dx2
        dgate_ref[...] += jnp.sum(dx2 * out, axis=0, keepdims=True)
        dout = (dx2 * gate).astype(BF16)
        dmerged = _dot(dout, wot_ref[...])
        dwo_ref[...] += _dot_tn(merged, dout)
        dya = dmerged * sga
        dyb = dmerged * sgb
        dga_ref[...] = (dya * ya * (1.0 - sga)).astype(BF16)
        dgb_ref[...] = (dyb * yb * (1.0 - sgb)).astype(BF16)
        dyab = dya.astype(BF16)
        dybb = dyb.astype(BF16)
        dua = _dot(dyab, wat_ref[...])
        dub = _dot(dybb, wbt_ref[...])
        dwa_ref[...] += _dot_tn(ua, dyab)
        dwb_ref[...] += _dot_tn(ub, dybb)
        doa_ref[...] = dua * silu_a
        dob_ref[...] = dub * silu_b
        dsz_ref[...] = (dua * oa * (sa * (1.0 + sz * (1.0 - sa)))).astype(BF16)
        dmz_ref[...] = (dub * ob * (sb * (1.0 + mz * (1.0 - sb)))).astype(BF16)

    return pl.pallas_call(
        body, name="post", grid=(s_len // tm,),
        out_shape=(_sds((s_len, D_MODEL), F32), _sds((s_len, 512), F32), _sds((s_len, 512), F32),
                   _sds((s_len, 512), BF16), _sds((s_len, 512), BF16),
                   _sds((s_len, D_MODEL), BF16), _sds((s_len, D_MODEL), BF16),
                   _sds((D_MODEL, D_MODEL), F32), _sds((512, D_MODEL), F32), _sds((512, D_MODEL), F32),
                   _sds((1, D_MODEL), F32), _sds((1, D_MODEL), F32), _sds((1, 128), F32)),
        in_specs=[_rows(tm, D_MODEL), _rows(tm, D_MODEL), _rows(tm, 512), _rows(tm, 512), _rows(tm, 512),
                  _rows(tm, 512), _rows(tm, D_MODEL), _rows(tm, D_MODEL), _whole((1, D_MODEL)), _whole((1, D_MODEL)),
                  _whole((512, D_MODEL)), _whole((512, D_MODEL)), _whole((D_MODEL, D_MODEL)),
                  _whole((D_MODEL, 512)), _whole((D_MODEL, 512)), _whole((D_MODEL, D_MODEL))],
        out_specs=(_rows(tm, D_MODEL), _rows(tm, 512), _rows(tm, 512), _rows(tm, 512), _rows(tm, 512),
                   _rows(tm, D_MODEL), _rows(tm, D_MODEL),
                   _whole((D_MODEL, D_MODEL)), _whole((512, D_MODEL)), _whole((512, D_MODEL)),
                   _whole((1, D_MODEL)), _whole((1, D_MODEL)), _whole((1, 128))),
        compiler_params=_params(("arbitrary",)),
    )(x, tgt, oa, ob, sz, mz, ga, gb, gate, gf, wa, wb, wo, wat, wbt, wot)


def _bwdprep_call(dsq, dsk, dsv, dsz, dqn, dqp, dkn, dvv, dkpt, dmz, dga, dgb, cq, ckv, cos256, sin256,
                  qg, kvg, wqt, wkvt):
    s_len = cq.shape[0]
    tm = min(ROW_TILE, s_len)

    def body(dsq_ref, dsk_ref, dsv_ref, dsz_ref, dqn_ref, dqp_ref, dkn_ref, dvv_ref, dkpt_ref, dmz_ref,
             dga_ref, dgb_ref, cq_ref, ckv_ref, cos_ref, sin_ref, qg_ref, kvg_ref, wqt_ref, wkvt_ref,
             dp_ref, dwq_ref, dwkv_ref, dqg_ref, dkvg_ref):
        @pl.when(pl.program_id(0) == 0)
        def _():
            dwq_ref[...] = jnp.zeros_like(dwq_ref)
            dwkv_ref[...] = jnp.zeros_like(dwkv_ref)
            dqg_ref[...] = jnp.zeros_like(dqg_ref)
            dkvg_ref[...] = jnp.zeros_like(dkvg_ref)

        cos = cos_ref[...]
        sin = sin_ref[...]
        dp_ref[:, O_SQ:O_SK] = dsq_ref[...]
        dp_ref[:, O_SK:O_SV] = dsk_ref[...].astype(BF16)
        dp_ref[:, O_SV:O_SZ] = dsv_ref[...].astype(BF16)
        dp_ref[:, O_SZ:O_CQ] = dsz_ref[...]
        dp_ref[:, O_MZ:O_GA] = dmz_ref[...]
        dp_ref[:, O_GA:O_GB] = dga_ref[...]
        dp_ref[:, O_GB:O_KR] = dgb_ref[...]
        dkp = dkpt_ref[...]
        dp_ref[:, O_KR:O_KR + 128] = (dkp * cos[:, :128]).astype(BF16)
        dp_ref[:, O_KR + 128:O_END] = (dkp * sin[:, :128]).astype(BF16)
        dp_ref[:, O_END:W_INT] = jnp.zeros((tm, W_INT - O_END), BF16)

        cq = cq_ref[...]
        rq = lax.rsqrt(jnp.mean(cq * cq, axis=-1, keepdims=True) + EPS)
        cqh = cq * rq
        qg = qg_ref[...]
        cqn = (cqh * qg).astype(BF16)
        dqp = dqp_ref[...].astype(F32)
        dqa = jnp.concatenate([dqn_ref[...], (dqp * cos).astype(BF16), (dqp * sin).astype(BF16)], axis=1)
        dcqn = _dot(dqa, wqt_ref[...])
        dwq_ref[...] += _dot_tn(cqn, dqa)
        dqg_ref[...] += jnp.sum(dcqn * cqh, axis=0, keepdims=True)
        dh = dcqn * qg
        dcq = rq * (dh - cqh * jnp.mean(dh * cqh, axis=-1, keepdims=True))
        dp_ref[:, O_CQ:O_CKV] = dcq.astype(BF16)

        ckv = ckv_ref[...]
        rk = lax.rsqrt(jnp.mean(ckv * ckv, axis=-1, keepdims=True) + EPS)
        ckh = ckv * rk
        kvg = kvg_ref[...]
        ckvn = (ckh * kvg).astype(BF16)
        dkva = jnp.concatenate([dkn_ref[...].astype(BF16), dvv_ref[...].astype(BF16)], axis=1)
        dckvn = _dot(dkva, wkvt_ref[...])
        dwkv_ref[...] += _dot_tn(ckvn, dkva)
        dkvg_ref[...] += jnp.sum(dckvn * ckh, axis=0, keepdims=True)
        dh2 = dckvn * kvg
        dckv = rk * (dh2 - ckh * jnp.mean(dh2 * ckh, axis=-1, keepdims=True))
        dp_ref[:, O_CKV:O_MZ] = dckv.astype(BF16)

    return pl.pallas_call(
        body, name="bwdprep", grid=(s_len // tm,),
        out_shape=(_sds((s_len, W_INT), BF16), _sds((Q_RANK, 1024), F32), _sds((KV_RANK, 1024), F32),
                   _sds((1, Q_RANK), F32), _sds((1, KV_RANK), F32)),
        in_specs=[_rows(tm, 512), _rows(tm, 512), _rows(tm, 512), _rows(tm, 512), _rows(tm, 512), _rows(tm, 256),
                  _rows(tm, 512), _rows(tm, 512), _rows(tm, 128), _rows(tm, 512), _rows(tm, D_MODEL),
                  _rows(tm, D_MODEL), _rows(tm, Q_RANK), _rows(tm, KV_RANK), _rows(tm, 256), _rows(tm, 256),
                  _whole((1, Q_RANK)), _whole((1, KV_RANK)), _whole((1024, Q_RANK)), _whole((1024, KV_RANK))],
        out_specs=(_rows(tm, W_INT), _whole((Q_RANK, 1024)), _whole((KV_RANK, 1024)),
                   _whole((1, Q_RANK)), _whole((1, KV_RANK))),
        compiler_params=_params(("arbitrary",)),
    )(dsq, dsk, dsv, dsz, dqn, dqp, dkn, dvv, dkpt, dmz, dga, dgb, cq, ckv, cos256, sin256, qg, kvg, wqt, wkvt)


def _dh_call(dproj, w_int_t, x, dx2, scale, g1):
    s_len = x.shape[0]
    tm = min(ROW_TILE, s_len)

    def body(dp_ref, wt_ref, x_ref, dx2_ref, sc_ref, g1_ref, gx_ref, dsh_ref, dsc_ref, dg1_ref):
        @pl.when(pl.program_id(0) == 0)
        def _():
            dsh_ref[...] = jnp.zeros_like(dsh_ref)
            dsc_ref[...] = jnp.zeros_like(dsc_ref)
            dg1_ref[...] = jnp.zeros_like(dg1_ref)

        dh = _dot(dp_ref[...], wt_ref[...])
        xt = x_ref[...]
        r = lax.rsqrt(jnp.mean(xt * xt, axis=-1, keepdims=True) + EPS)
        xh = xt * r
        g1 = g1_ref[...]
        xg = xh * g1
        dsh_ref[...] += jnp.sum(dh, axis=0, keepdims=True)
        dsc_ref[...] += jnp.sum(dh * xg, axis=0, keepdims=True)
        dxg = dh * (1.0 + sc_ref[...])
        dg1_ref[...] += jnp.sum(dxg * xh, axis=0, keepdims=True)
        dxh = dxg * g1
        gx_ref[...] = dx2_ref[...] + r * (dxh - xh * jnp.mean(dxh * xh, axis=-1, keepdims=True))

    return pl.pallas_call(
        body, name="dh", grid=(s_len // tm,),
        out_shape=(_sds((s_len, D_MODEL), F32), _sds((1, D_MODEL), F32), _sds((1, D_MODEL), F32),
                   _sds((1, D_MODEL), F32)),
        in_specs=[_rows(tm, W_INT), _whole((W_INT, D_MODEL)), _rows(tm, D_MODEL), _rows(tm, D_MODEL),
                  _whole((1, D_MODEL)), _whole((1, D_MODEL))],
        out_specs=(_rows(tm, D_MODEL), _whole((1, D_MODEL)), _whole((1, D_MODEL)), _whole((1, D_MODEL))),
        compiler_params=_params(("arbitrary",)),
    )(dproj, w_int_t, x, dx2, scale, g1)


def _dwin_call(h, dproj):
    s_len = h.shape[0]
    tm = min(2 * ROW_TILE, s_len)
    nc = 4
    chunk = W_INT // nc

    def body(h_ref, dp_ref, dw_ref):
        @pl.when(pl.program_id(1) == 0)
        def _():
            dw_ref[...] = jnp.zeros_like(dw_ref)

        dw_ref[...] += _dot_tn(h_ref[...], dp_ref[...])

    return pl.pallas_call(
        body, name="dwin", grid=(nc, s_len // tm),
        out_shape=_sds((D_MODEL, nc * chunk), F32),
        in_specs=[pl.BlockSpec((tm, D_MODEL), lambda c, i: (i, 0)),
                  pl.BlockSpec((tm, chunk), lambda c, i: (i, c))],
        out_specs=pl.BlockSpec((D_MODEL, chunk), lambda c, i: (0, c)),
        compiler_params=_params(("parallel", "arbitrary")),
    )(h, dproj)


def _small_call(svg, ct, dmod_sh):
    def body(sv_ref, ct_ref, dm_ref, tot_ref, gwada_ref):
        acc = sv_ref[0:1, :]
        for d in range(1, N_DEV):
            acc = acc + sv_ref[d:d + 1, :]
        tot_ref[...] = acc
        gwada_ref[...] = lax.dot_general(ct_ref[...], dm_ref[...], (((1,), (0,)), ((), ())),
                                         precision=lax.Precision.HIGHEST, preferred_element_type=F32)

    vmem = pl.BlockSpec(memory_space=pltpu.VMEM)
    return pl.pallas_call(
        body, name="small_grads",
        out_shape=(_sds((1, 8 * SV_COLS), F32), _sds((D_MODEL, 768), F32)),
        in_specs=[vmem, vmem, vmem], out_specs=(vmem, vmem),
        compiler_params=_params(),
    )(svg, ct, dmod_sh)


def _adamw_tile_rows(rows, cols):
    budget = 1 << 20
    if rows * cols * 4 <= budget or rows % 8:
        return rows
    best = 8
    for tr in range(8, rows + 1, 8):
        if rows % tr == 0 and tr * cols * 4 <= budget:
            best = tr
    return best


def _adamw_call(name, w, g, m, v):
    rows, cols = w.shape
    tr = _adamw_tile_rows(rows, cols)

    def body(w_ref, g_ref, m_ref, v_ref, d_ref, nm_ref, nv_ref):
        gg = g_ref[...]
        m2 = ADAM_B1 * m_ref[...] + (1.0 - ADAM_B1) * gg
        v2 = ADAM_B2 * v_ref[...] + (1.0 - ADAM_B2) * (gg * gg)
        m_hat = m2 / (1.0 - ADAM_B1 ** ADAM_STEP)
        v_hat = v2 / (1.0 - ADAM_B2 ** ADAM_STEP)
        d_ref[...] = -ADAM_LR * (m_hat / (jnp.sqrt(v_hat) + ADAM_EPS) + ADAM_WD * w_ref[...])
        nm_ref[...] = m2
        nv_ref[...] = v2

    spec = pl.BlockSpec((tr, cols), lambda i: (i, 0))
    return pl.pallas_call(
        body, name="adamw_" + name, grid=(rows // tr,),
        out_shape=(_sds((rows, cols), F32),) * 3,
        in_specs=[spec] * 4, out_specs=(spec,) * 3,
        compiler_params=_params(("parallel",)),
    )(w, g, m, v)


def _swap_halves(w, group):
    r, n = w.shape
    return w.reshape(r, n // group, 2, group // 2)[:, :, ::-1, :].reshape(r, n)


def _pack_shards(parts):
    return jnp.concatenate([p.reshape(-1, LANES) for p in parts], axis=0)


def _unpack_chip_major(gw):
    offs = [0]
    for r in PACK_ROWS:
        offs.append(offs[-1] + r)

    def cols(i, rows, shard_cols):
        blk = gw[:, offs[i]:offs[i + 1]].reshape(N_CHIPS, rows, shard_cols)
        return blk.transpose(1, 0, 2).reshape(rows, N_CHIPS * shard_cols)

    w_in = cols(0, D_MODEL, 1320)
    w_uq = cols(1, Q_RANK, 192)
    w_ukv = cols(2, KV_RANK, 256)
    w_a = cols(3, 512, 256)
    w_b = cols(4, 512, 256)
    w_out = gw[:, offs[5]:offs[6]].reshape(D_MODEL, D_MODEL)
    return w_in, w_uq, w_ukv, w_a, w_b, w_out


def _internal_weights(w_in, w_uq, w_ukv):
    krot = w_in[:, 2688:2720]
    w_int = jnp.concatenate([
        w_in[:, 0:512] * jnp.asarray(0.125, w_in.dtype), w_in[:, 512:2048],
        w_in[:, 2048:2432], w_in[:, 2432:2688], w_in[:, 2720:3232], w_in[:, 3232:4256], w_in[:, 4256:5280],
        jnp.tile(krot, (1, 4)), jnp.tile(_swap_halves(krot, 32), (1, 4)),
        jnp.zeros((D_MODEL, W_INT - O_END), w_in.dtype)], axis=1)
    uq = w_uq.reshape(Q_RANK, N_HEADS, 96)
    wp = uq[:, :, 64:].reshape(Q_RANK, 256)
    w_q = jnp.concatenate([uq[:, :, :64].reshape(Q_RANK, 512), wp, _swap_halves(wp, 32)], axis=1)
    ukv = w_ukv.reshape(KV_RANK, N_HEADS, 128)
    w_kv = jnp.concatenate([ukv[:, :, :64].reshape(KV_RANK, 512), ukv[:, :, 64:].reshape(KV_RANK, 512)], axis=1)
    return w_int, w_q, w_kv


def _true_weight_grads(dwi, dwq, dwkv):
    dkr = dwi[:, O_KR:O_KR + 128].reshape(D_MODEL, 4, 32).sum(axis=1)
    dkr_sw = dwi[:, O_KR + 128:O_END].reshape(D_MODEL, 4, 32).sum(axis=1)
    dkrot = dkr + _swap_halves(dkr_sw, 32)
    g_in = jnp.concatenate([
        dwi[:, 0:512] * 0.125, dwi[:, 512:2048], dwi[:, O_CQ:O_CKV], dwi[:, O_CKV:O_MZ], dkrot,
        dwi[:, O_MZ:O_GA], dwi[:, O_GA:O_GB], dwi[:, O_GB:O_KR]], axis=1)
    dwp = dwq[:, 512:768] + _swap_halves(dwq[:, 768:1024], 32)
    g_uq = jnp.concatenate([dwq[:, :512].reshape(Q_RANK, N_HEADS, 64), dwp.reshape(Q_RANK, N_HEADS, 32)],
                           axis=2).reshape(Q_RANK, 768)
    g_ukv = jnp.concatenate([dwkv[:, :512].reshape(KV_RANK, N_HEADS, 64), dwkv[:, 512:].reshape(KV_RANK, N_HEADS, 64)],
                            axis=2).reshape(KV_RANK, 1024)
    return g_in, g_uq, g_ukv


def _chip_major(g, shard_cols):
    r = g.shape[0]
    return g.reshape(r, N_CHIPS, shard_cols).transpose(1, 0, 2).reshape(N_CHIPS, -1, LANES)


def kernel(x, c, positions, w_ada, b_ada, norm_gain, w_in, q_norm_gain, w_uq, kv_norm_gain, w_ukv, w_branch_a, w_branch_b, w_out, final_norm_gain, loss_target, m_w_ada, m_b_ada, m_norm_gain, m_w_in, m_q_norm_gain, m_w_uq, m_kv_norm_gain, m_w_ukv, m_w_branch_a, m_w_branch_b, m_w_out, m_final_norm_gain, v_w_ada, v_b_ada, v_norm_gain, v_w_in, v_q_norm_gain, v_w_uq, v_kv_norm_gain, v_w_ukv, v_w_branch_a, v_w_branch_b, v_w_out, v_final_norm_gain):
    ix, iy, ic = lax.axis_index("x"), lax.axis_index("y"), lax.axis_index("c")
    me = 4 * ix + 2 * iy + ic
    chip = 2 * ix + iy
    xs = x[0]
    tgt = loss_target[0]
    s_len = xs.shape[0]

    shards = (w_in[0], w_uq[0], w_ukv[0], w_branch_a[0], w_branch_b[0], w_out[0])
    pack = _pack_shards([s.astype(BF16) for s in shards])
    mg, call, gw = _gather_call(c, w_ada[0], pack)
    mod = mg.transpose(1, 0, 2).reshape(N_DEV, 3 * D_MODEL) + b_ada
    mod_me = lax.dynamic_slice_in_dim(mod, me, 1, axis=0)
    shift, scale, gate = mod_me[:, :D_MODEL], mod_me[:, D_MODEL:2 * D_MODEL], mod_me[:, 2 * D_MODEL:]

    f_in, f_uq, f_ukv, f_a, f_b, f_out = _unpack_chip_major(gw)
    w_int, w_q, w_kv = _internal_weights(f_in, f_uq, f_ukv)

    inv_freq = ROPE_BASE ** (-jnp.arange(0, ROPE_DIM, 2, dtype=F32) / ROPE_DIM)
    ang = positions[0].astype(F32)[:, None] * inv_freq
    cs, sn = jnp.cos(ang), jnp.sin(ang)
    cos256 = jnp.tile(jnp.concatenate([cs, cs], axis=1), (1, 8))
    sin256 = jnp.tile(jnp.concatenate([-sn, sn], axis=1), (1, 8))

    (h, sq, sk, sv, sz, cq, ckv, mz, ga, gb, kpt, qn, qp, kn, vv) = _inproj_call(
        xs, shift, scale, norm_gain, w_int, w_q, w_kv, q_norm_gain, kv_norm_gain, cos256, sin256)
    oa, lt = _sb_fwd_call(sq, sk, sv)
    ob, lse = _mla_fwd_call(qn, qp, kn, kpt, vv)

    gf = final_norm_gain.reshape(1, D_MODEL)
    (dx2, doa, dob, dsz, dmz, dga, dgb, dwo, dwa, dwb, dgf, dgate, loss_p) = _post_call(
        xs, tgt, oa, ob, sz, mz, ga, gb, gate, gf, f_a, f_b, f_out, f_a.T, f_b.T, f_out.T)

    dsq, dsk, dsv = _sb_bwd_call(sq, sk, sv, doa, lt)
    dqn, dqp, dkn, dkpt, dvv = _mla_bwd_call(qn, qp, kn, kpt, vv, ob, dob, lse)

    dproj, dwq, dwkv, dqg, dkvg = _bwdprep_call(
        dsq, dsk, dsv, dsz, dqn, dqp, dkn, dvv, dkpt, dmz, dga, dgb, cq, ckv, cos256, sin256,
        q_norm_gain, kv_norm_gain, w_q.T, w_kv.T)
    grad_x, dshift, dscale, dg1 = _dh_call(dproj, w_int.T, xs, dx2, scale, norm_gain)
    dwi = _dwin_call(h, dproj)
    g_in, g_uq, g_ukv = _true_weight_grads(dwi, dwq, dwkv)

    gpack = jnp.concatenate([
        _chip_major(g_in, 1320), _chip_major(g_uq, 192), _chip_major(g_ukv, 256),
        _chip_major(dwa, 256), _chip_major(dwb, 256), dwo.reshape(N_CHIPS, -1, LANES)], axis=1)
    gpack = gpack.reshape(N_DEV, HALF_ROWS, LANES)
    small = jnp.concatenate([
        dshift, dscale, dgate, dg1, dqg, dkvg, dgf, loss_p,
        jnp.zeros((1, 8 * SV_COLS - 5888), F32)], axis=1).reshape(8, SV_COLS)
    land, svg = _scatter_call(gpack, small)
    red = _sum_call(land)
    full = _sibling_call(red).reshape(PACK_TOTAL, LANES)
    offs = [0]
    for r in PACK_ROWS:
        offs.append(offs[-1] + r)
    gs_in = full[offs[0]:offs[1]].reshape(D_MODEL, 1320)
    gs_uq = full[offs[1]:offs[2]].reshape(Q_RANK, 192)
    gs_ukv = full[offs[2]:offs[3]].reshape(KV_RANK, 256)
    gs_a = full[offs[3]:offs[4]].reshape(512, 256)
    gs_b = full[offs[4]:offs[5]].reshape(512, 256)
    gs_out = full[offs[5]:offs[6]].reshape(256, D_MODEL)

    svm = svg.reshape(N_DEV, 8 * SV_COLS)
    dmod_sh = lax.dynamic_slice_in_dim(svm[:, :3 * D_MODEL], chip * 768, 768, axis=1)
    tot, gs_ada = _small_call(svm, call.T, dmod_sh)
    g_bada = tot[:, 0:3072]
    g_g1 = tot[:, 3072:4096]
    g_qg = tot[:, 4096:4480]
    g_kvg = tot[:, 4480:4736]
    g_gf = tot[:, 4736:5760]
    loss = tot[0, 5760]

    names = ["w_ada", "b_ada", "norm_gain", "w_in", "q_norm_gain", "w_uq", "kv_norm_gain", "w_ukv",
             "w_branch_a", "w_branch_b", "w_out", "final_norm_gain"]
    ws = [w_ada[0], b_ada, norm_gain, w_in[0], q_norm_gain, w_uq[0], kv_norm_gain, w_ukv[0],
          w_branch_a[0], w_branch_b[0], w_out[0], final_norm_gain.reshape(1, D_MODEL)]
    gs = [gs_ada, g_bada, g_g1, gs_in, g_qg, gs_uq, g_kvg, gs_ukv, gs_a, gs_b, gs_out, g_gf]
    ms = [m_w_ada[0], m_b_ada, m_norm_gain, m_w_in[0], m_q_norm_gain, m_w_uq[0], m_kv_norm_gain, m_w_ukv[0],
          m_w_branch_a[0], m_w_branch_b[0], m_w_out[0], m_final_norm_gain.reshape(1, D_MODEL)]
    vs = [v_w_ada[0], v_b_ada, v_norm_gain, v_w_in[0], v_q_norm_gain, v_w_uq[0], v_kv_norm_gain, v_w_ukv[0],
          v_w_branch_a[0], v_w_branch_b[0], v_w_out[0], v_final_norm_gain.reshape(1, D_MODEL)]
    refs = [w_ada, b_ada, norm_gain, w_in, q_norm_gain, w_uq, kv_norm_gain, w_ukv,
            w_branch_a, w_branch_b, w_out, final_norm_gain]
    grads, deltas, new_ms, new_vs = [], [], [], []
    for n, w_, g_, m_, v_, ref in zip(names, ws, gs, ms, vs, refs):
        d_, nm_, nv_ = _adamw_call(n, w_, g_, m_, v_)
        grads.append(g_.reshape(ref.shape))
        deltas.append(d_.reshape(ref.shape))
        new_ms.append(nm_.reshape(ref.shape))
        new_vs.append(nv_.reshape(ref.shape))

    return (loss, grad_x.reshape(x.shape), *grads, *deltas, *new_ms, *new_vs)
```

```python
import functools
import math

import jax
import jax.numpy as jnp
from jax import lax
from jax.experimental import pallas as pl
from jax.experimental.pallas import tpu as pltpu

F32 = jnp.float32
BF16 = jnp.bfloat16

D_MODEL = 1024
SB_WIDTH = 512
MLA_WIDTH = 512
Q_RANK = 384
KV_RANK = 256
ROPE_DIM = 32
N_HEADS = 8
IN_WIDTH = 5280
EPS = 1e-6
ROPE_BASE = 10000.0
MLA_SCALE = 1.0 / math.sqrt(96.0)

ADAM_LR = 0.001
ADAM_B1 = 0.9
ADAM_B2 = 0.999
ADAM_EPS = 1e-08
ADAM_WD = 0.01
ADAM_STEP = 10

O_SQ, O_SK, O_SV, O_SZ, O_CQ, O_CKV, O_MZ, O_GA, O_GB, O_KR, O_END = (
    0, 512, 1024, 1536, 2048, 2432, 2688, 3200, 4224, 5248, 5504)
W_INT = 5632

N_CHIPS = 4
N_DEV = 8
LANES = 128
PACK_ROWS = (10560, 576, 512, 1024, 1024, 2048)
PACK_TOTAL = sum(PACK_ROWS)
HALF_ROWS = PACK_TOTAL // 2
SV_COLS = 768

ROW_TILE = 256
ATT_TILE = 256
VMEM_LIMIT = 56 * 1024 * 1024

MESH = pl.DeviceIdType.MESH


def _dot(a, b):
    return lax.dot_general(a, b, (((1,), (0,)), ((), ())), preferred_element_type=F32)


def _dot_nt(a, b):
    return lax.dot_general(a, b, (((1,), (1,)), ((), ())), preferred_element_type=F32)


def _dot_tn(a, b):
    return lax.dot_general(a, b, (((0,), (0,)), ((), ())), preferred_element_type=F32)


def _sigmoid(z):
    return 1.0 / (1.0 + jnp.exp(-z))


def _params(sem=None):
    if sem is None:
        return pltpu.CompilerParams(vmem_limit_bytes=VMEM_LIMIT)
    return pltpu.CompilerParams(dimension_semantics=sem, vmem_limit_bytes=VMEM_LIMIT)


def _rows(tm, n):
    return pl.BlockSpec((tm, n), lambda i: (i, 0))


def _whole(shape):
    nd = len(shape)
    return pl.BlockSpec(shape, lambda i: (0,) * nd)


def _sds(shape, dtype):
    return jax.ShapeDtypeStruct(shape, dtype)


def _flip(v, d):
    return 1 - v if d else v


def _gather_call(c_row, w_ada_sh, pack):
    def body(c_ref, wada_ref, pk_ref, mg_ref, cg_ref, gw_ref,
             cv, ssem_c, rsem_c, ssem_m, rsem_m, ssem_w, rsem_w, lsem):
        x, y, c = lax.axis_index("x"), lax.axis_index("y"), lax.axis_index("c")
        me = 4 * x + 2 * y + c
        chip = 2 * x + y
        rel3 = [(1, 0), (0, 1), (1, 1)]

        wcopies = []
        for j, (dx, dy) in enumerate(rel3):
            cp = pltpu.make_async_remote_copy(
                src_ref=pk_ref, dst_ref=gw_ref.at[chip], send_sem=ssem_w.at[j], recv_sem=rsem_w.at[j],
                device_id=(_flip(x, dx), _flip(y, dy), c), device_id_type=MESH)
            cp.start()
            wcopies.append(cp)
        own = pltpu.make_async_copy(pk_ref, gw_ref.at[chip], lsem)
        own.start()

        cv[me] = c_ref[...]
        ccopies = []
        for r in range(1, N_DEV):
            dx, dy, dc = (r >> 2) & 1, (r >> 1) & 1, r & 1
            cp = pltpu.make_async_remote_copy(
                src_ref=c_ref, dst_ref=cv.at[me], send_sem=ssem_c.at[r - 1], recv_sem=rsem_c.at[r - 1],
                device_id=(_flip(x, dx), _flip(y, dy), _flip(c, dc)), device_id_type=MESH)
            cp.start()
            ccopies.append(cp)
        for r in range(1, N_DEV):
            dx, dy, dc = (r >> 2) & 1, (r >> 1) & 1, r & 1
            src = 4 * _flip(x, dx) + 2 * _flip(y, dy) + _flip(c, dc)
            pltpu.make_async_remote_copy(
                src_ref=c_ref, dst_ref=cv.at[src], send_sem=ssem_c.at[r - 1], recv_sem=rsem_c.at[r - 1],
                device_id=(x, y, c), device_id_type=MESH).wait_recv()
        rows = lax.broadcasted_iota(jnp.int32, (N_DEV, D_MODEL), 0)
        call = jnp.zeros((N_DEV, D_MODEL), F32)
        for b in range(N_DEV):
            call = jnp.where(rows == b, jnp.broadcast_to(cv[b], (N_DEV, D_MODEL)), call)
        cg_ref[...] = call

        mg_ref[chip] = lax.dot_general(call, wada_ref[...], (((1,), (0,)), ((), ())),
                                       precision=lax.Precision.HIGHEST, preferred_element_type=F32)
        mcopies = []
        for j, (dx, dy) in enumerate(rel3):
            cp = pltpu.make_async_remote_copy(
                src_ref=mg_ref.at[chip], dst_ref=mg_ref.at[chip], send_sem=ssem_m.at[j], recv_sem=rsem_m.at[j],
                device_id=(_flip(x, dx), _flip(y, dy), c), device_id_type=MESH)
            cp.start()
            mcopies.append(cp)
        for j, (dx, dy) in enumerate(rel3):
            src_chip = 2 * _flip(x, dx) + _flip(y, dy)
            pltpu.make_async_remote_copy(
                src_ref=mg_ref.at[src_chip], dst_ref=mg_ref.at[src_chip], send_sem=ssem_m.at[j],
                recv_sem=rsem_m.at[j], device_id=(x, y, c), device_id_type=MESH).wait_recv()
            pltpu.make_async_remote_copy(
                src_ref=pk_ref, dst_ref=gw_ref.at[src_chip], send_sem=ssem_w.at[j], recv_sem=rsem_w.at[j],
                device_id=(x, y, c), device_id_type=MESH).wait_recv()
        for cp in ccopies + mcopies + wcopies:
            cp.wait_send()
        own.wait()

    vmem = pl.BlockSpec(memory_space=pltpu.VMEM)
    hbm = pl.BlockSpec(memory_space=pl.ANY)
    return pl.pallas_call(
        body, name="gather_fwd",
        out_shape=(_sds((N_CHIPS, N_DEV, 768), F32), _sds((N_DEV, D_MODEL), F32),
                   _sds((N_CHIPS, PACK_TOTAL, LANES), BF16)),
        in_specs=[vmem, vmem, hbm], out_specs=(vmem, vmem, hbm),
        scratch_shapes=[
            pltpu.VMEM((N_DEV, 1, D_MODEL), F32),
            pltpu.SemaphoreType.DMA((N_DEV - 1,)), pltpu.SemaphoreType.DMA((N_DEV - 1,)),
            pltpu.SemaphoreType.DMA((3,)), pltpu.SemaphoreType.DMA((3,)),
            pltpu.SemaphoreType.DMA((3,)), pltpu.SemaphoreType.DMA((3,)),
            pltpu.SemaphoreType.DMA,
        ],
        compiler_params=_params(),
    )(c_row, w_ada_sh, pack)


def _scatter_call(gpack, sv):
    def body(g_ref, sv_ref, land_ref, svg_ref, ssem_g, rsem_g, ssem_s, rsem_s, lsem):
        x, y, c = lax.axis_index("x"), lax.axis_index("y"), lax.axis_index("c")
        me = 4 * x + 2 * y + c
        copies = []
        for r in range(1, N_DEV):
            dx, dy, dc = (r >> 2) & 1, (r >> 1) & 1, r & 1
            tx, ty, tc = _flip(x, dx), _flip(y, dy), _flip(c, dc)
            tgt = 4 * tx + 2 * ty + tc
            cp = pltpu.make_async_remote_copy(
                src_ref=g_ref.at[tgt], dst_ref=land_ref.at[me], send_sem=ssem_g.at[r - 1],
                recv_sem=rsem_g.at[r - 1], device_id=(tx, ty, tc), device_id_type=MESH)
            cp.start()
            copies.append(cp)
            cp = pltpu.make_async_remote_copy(
                src_ref=sv_ref, dst_ref=svg_ref.at[me], send_sem=ssem_s.at[r - 1],
                recv_sem=rsem_s.at[r - 1], device_id=(tx, ty, tc), device_id_type=MESH)
            cp.start()
            copies.append(cp)
        own = pltpu.make_async_copy(g_ref.at[me], land_ref.at[me], lsem)
        own.start()
        svg_ref[me] = sv_ref[...]
        for r in range(1, N_DEV):
            dx, dy, dc = (r >> 2) & 1, (r >> 1) & 1, r & 1
            src = 4 * _flip(x, dx) + 2 * _flip(y, dy) + _flip(c, dc)
            pltpu.make_async_remote_copy(
                src_ref=g_ref.at[src], dst_ref=land_ref.at[src], send_sem=ssem_g.at[r - 1],
                recv_sem=rsem_g.at[r - 1], device_id=(x, y, c), device_id_type=MESH).wait_recv()
            pltpu.make_async_remote_copy(
                src_ref=sv_ref, dst_ref=svg_ref.at[src], send_sem=ssem_s.at[r - 1],
                recv_sem=rsem_s.at[r - 1], device_id=(x, y, c), device_id_type=MESH).wait_recv()
        for cp in copies:
            cp.wait_send()
        own.wait()

    vmem = pl.BlockSpec(memory_space=pltpu.VMEM)
    hbm = pl.BlockSpec(memory_space=pl.ANY)
    return pl.pallas_call(
        body, name="grad_scatter",
        out_shape=(_sds((N_DEV, HALF_ROWS, LANES), F32), _sds((N_DEV, 8, SV_COLS), F32)),
        in_specs=[hbm, vmem], out_specs=(hbm, vmem),
        scratch_shapes=[
            pltpu.SemaphoreType.DMA((N_DEV - 1,)), pltpu.SemaphoreType.DMA((N_DEV - 1,)),
            pltpu.SemaphoreType.DMA((N_DEV - 1,)), pltpu.SemaphoreType.DMA((N_DEV - 1,)),
            pltpu.SemaphoreType.DMA,
        ],
        compiler_params=_params(),
    )(gpack, sv)


def _sum_call(land):
    rows = land.shape[1]
    tr = 984

    def body(l_ref, o_ref):
        acc = l_ref[0]
        for d in range(1, N_DEV):
            acc = acc + l_ref[d]
        o_ref[...] = acc

    return pl.pallas_call(
        body, name="grad_sum", grid=(rows // tr,),
        out_shape=_sds((rows, LANES), F32),
        in_specs=[pl.BlockSpec((N_DEV, tr, LANES), lambda i: (0, i, 0))],
        out_specs=pl.BlockSpec((tr, LANES), lambda i: (i, 0)),
        compiler_params=_params(("parallel",)),
    )(land)


def _sibling_call(red):
    def body(r_ref, full_ref, ssem, rsem, lsem):
        x, y, c = lax.axis_index("x"), lax.axis_index("y"), lax.axis_index("c")
        cp = pltpu.make_async_remote_copy(
            src_ref=r_ref, dst_ref=full_ref.at[c], send_sem=ssem, recv_sem=rsem,
            device_id=(x, y, 1 - c), device_id_type=MESH)
        cp.start()
        own = pltpu.make_async_copy(r_ref, full_ref.at[c], lsem)
        own.start()
        pltpu.make_async_remote_copy(
            src_ref=r_ref, dst_ref=full_ref.at[1 - c], send_sem=ssem, recv_sem=rsem,
            device_id=(x, y, c), device_id_type=MESH).wait_recv()
        cp.wait_send()
        own.wait()

    hbm = pl.BlockSpec(memory_space=pl.ANY)
    return pl.pallas_call(
        body, name="grad_sibling",
        out_shape=_sds((2,) + red.shape, F32),
        in_specs=[hbm], out_specs=hbm,
        scratch_shapes=[pltpu.SemaphoreType.DMA, pltpu.SemaphoreType.DMA, pltpu.SemaphoreType.DMA],
        compiler_params=_params(),
    )(red)


def _inproj_call(x, shift, scale, g1, w_int, w_q, w_kv, qg, kvg, cos256, sin256):
    s_len = x.shape[0]
    tm = min(ROW_TILE, s_len)

    def body(x_ref, sh_ref, sc_ref, g1_ref, w_ref, wq_ref, wkv_ref, qg_ref, kvg_ref, cos_ref, sin_ref,
             h_ref, sq_ref, sk_ref, sv_ref, sz_ref, cq_ref, ckv_ref, mz_ref, ga_ref, gb_ref, kpt_ref,
             qn_ref, qp_ref, kn_ref, vv_ref):
        xt = x_ref[...]
        r = lax.rsqrt(jnp.mean(xt * xt, axis=-1, keepdims=True) + EPS)
        h = (xt * r * g1_ref[...]) * (1.0 + sc_ref[...]) + sh_ref[...]
        hb = h.astype(BF16)
        h_ref[...] = hb

        def seg(a, b):
            return _dot(hb, w_ref[:, a:b])

        sq_ref[...] = seg(O_SQ, O_SK).astype(BF16)
        sk_ref[...] = seg(O_SK, O_SV).astype(BF16)
        sv_ref[...] = seg(O_SV, O_SZ).astype(BF16)
        sz_ref[...] = seg(O_SZ, O_CQ)
        mz_ref[...] = seg(O_MZ, O_GA)
        ga_ref[...] = seg(O_GA, O_GB)
        gb_ref[...] = seg(O_GB, O_KR)
        cos = cos_ref[...]
        sin = sin_ref[...]
        kr = seg(O_KR, O_END)
        kpt_ref[...] = (kr[:, :128] * cos[:, :128] + kr[:, 128:] * sin[:, :128]).astype(BF16)

        cq = seg(O_CQ, O_CKV)
        cq_ref[...] = cq
        rq = lax.rsqrt(jnp.mean(cq * cq, axis=-1, keepdims=True) + EPS)
        cqn = (cq * rq * qg_ref[...]).astype(BF16)
        qa = _dot(cqn, wq_ref[...])
        qn_ref[...] = qa[:, :512].astype(BF16)
        qp_ref[...] = (qa[:, 512:768] * cos + qa[:, 768:] * sin).astype(BF16)

        ckv = seg(O_CKV, O_MZ)
        ckv_ref[...] = ckv
        rk = lax.rsqrt(jnp.mean(ckv * ckv, axis=-1, keepdims=True) + EPS)
        ckvn = (ckv * rk * kvg_ref[...]).astype(BF16)
        kva = _dot(ckvn, wkv_ref[...])
        kn_ref[...] = kva[:, :512].astype(BF16)
        vv_ref[...] = kva[:, 512:].astype(BF16)

    outs = [
        (D_MODEL, BF16), (512, BF16), (512, BF16), (512, BF16), (512, F32), (Q_RANK, F32), (KV_RANK, F32),
        (512, F32), (D_MODEL, F32), (D_MODEL, F32), (128, BF16), (512, BF16), (256, BF16), (512, BF16), (512, BF16),
    ]
    return pl.pallas_call(
        body, name="inproj", grid=(s_len // tm,),
        out_shape=tuple(_sds((s_len, n), dt) for n, dt in outs),
        in_specs=[_rows(tm, D_MODEL), _whole((1, D_MODEL)), _whole((1, D_MODEL)), _whole((1, D_MODEL)),
                  _whole((D_MODEL, W_INT)), _whole((Q_RANK, 1024)), _whole((KV_RANK, 1024)),
                  _whole((1, Q_RANK)), _whole((1, KV_RANK)), _rows(tm, 256), _rows(tm, 256)],
        out_specs=tuple(_rows(tm, n) for n, _ in outs),
        compiler_params=_params(("parallel",)),
    )(x, shift, scale, g1, w_int, w_q, w_kv, qg, kvg, cos256, sin256)


def _softplus(z):
    return jnp.maximum(z, 0.0) + jnp.log(1.0 + jnp.exp(-jnp.abs(z)))


def _split_bf16(a):
    hi = a.astype(BF16)
    lo = (a - hi.astype(F32)).astype(BF16)
    return hi, lo


def _sb_fwd_call(q, k, v):
    s_len = q.shape[0]
    t = min(ATT_TILE, s_len)
    nq = s_len // t

    def body(q_ref, k_ref, v_ref, o_ref, lt_ref):
        i = pl.program_id(1)
        q2 = q_ref[...]
        lane = lax.broadcasted_iota(jnp.int32, (1, 128), 1)
        row = lax.broadcasted_iota(jnp.int32, (t, t), 0)
        col = lax.broadcasted_iota(jnp.int32, (t, t), 1)
        later = (row > col).astype(BF16)
        later2 = jnp.concatenate([later, later], axis=0)
        valid = col < row
        hms = [(lane // 64) == hh for hh in range(2)]
        qms = [jnp.where(hm, q2, jnp.zeros_like(q2)) for hm in hms]

        def block(j, carry, diag):
            runs, acc = list(carry[:2]), carry[2]
            off = pl.multiple_of(j * t, t)
            kb = k_ref[pl.ds(off, t), :]
            vb = v_ref[pl.ds(off, t), :]
            ws = []
            for hh in range(2):
                z = _dot_nt(qms[hh], kb)
                lg = -_softplus(z)
                lm = jnp.where(valid, lg, 0.0) if diag else lg
                hi, lo = _split_bf16(lm)
                suf = _dot(jnp.concatenate([hi, lo], axis=1), later2)
                w = jnp.exp(z + lg + suf + runs[hh])
                if diag:
                    w = jnp.where(valid, w, 0.0)
                ws.append(w.astype(BF16))
                runs[hh] = runs[hh] + jnp.sum(lm, axis=1, keepdims=True)
            vstack = jnp.concatenate([jnp.where(hm, vb, jnp.zeros_like(vb)) for hm in hms], axis=0)
            acc = acc + _dot(jnp.concatenate(ws, axis=1), vstack)
            return runs[0], runs[1], acc

        zero = jnp.zeros((t, 1), F32)
        carry = block(i, (zero, zero, jnp.zeros((t, 128), F32)), True)
        carry = lax.fori_loop(1, i + 1, lambda jj, cr: block(i - jj, cr, False), carry)
        lt_ref[0, :, 0:1] = carry[0]
        lt_ref[0, :, 1:2] = carry[1]
        o_ref[...] = carry[2]

    return pl.pallas_call(
        body, name="sb_fwd", grid=(4, nq),
        out_shape=(_sds((s_len, SB_WIDTH), F32), _sds((4, s_len, 2), F32)),
        in_specs=[pl.BlockSpec((t, 128), lambda p, i: (i, p)),
                  pl.BlockSpec((s_len, 128), lambda p, i: (0, p)),
                  pl.BlockSpec((s_len, 128), lambda p, i: (0, p))],
        out_specs=(pl.BlockSpec((t, 128), lambda p, i: (i, p)),
                   pl.BlockSpec((1, t, 2), lambda p, i: (p, i, 0))),
        compiler_params=_params(("parallel", "parallel")),
    )(q, k, v)


def _sb_bwd_call(q, k, v, do, lt):
    s_len = q.shape[0]
    t = min(ATT_TILE, s_len)
    nq = s_len // t

    def body(q_ref, k_ref, v_ref, do_ref, lt_ref, dq_ref, dk_ref, dv_ref):
        i = pl.program_id(1)

        @pl.when(i == 0)
        def _():
            dk_ref[...] = jnp.zeros_like(dk_ref)
            dv_ref[...] = jnp.zeros_like(dv_ref)

        q2 = q_ref[...]
        do2 = do_ref[...].astype(BF16)
        lane = lax.broadcasted_iota(jnp.int32, (1, 128), 1)
        row = lax.broadcasted_iota(jnp.int32, (t, t), 0)
        col = lax.broadcasted_iota(jnp.int32, (t, t), 1)
        earlier = (row < col).astype(BF16)
        earlier2 = jnp.concatenate([earlier, earlier], axis=0)
        valid = col < row
        hms = [(lane // 64) == hh for hh in range(2)]
        qms = [jnp.where(hm, q2, jnp.zeros_like(q2)) for hm in hms]
        doms = [jnp.where(hm, do2, jnp.zeros_like(do2)) for hm in hms]
        ltots = [lt_ref[0, :, hh:hh + 1] for hh in range(2)]
        qstack = jnp.concatenate(qms, axis=0)
        dostack = jnp.concatenate(doms, axis=0)

        def block(j, carry, diag):
            lpre, ppre, dq = list(carry[0:2]), list(carry[2:4]), carry[4]
            off = pl.multiple_of(j * t, t)
            kb = k_ref[pl.ds(off, t), :]
            vb = v_ref[pl.ds(off, t), :]
            dzs, avs = [], []
            for hh in range(2):
                z = _dot_nt(qms[hh], kb)
                sp = _softplus(z)
                lg = -sp
                lm = jnp.where(valid, lg, 0.0) if diag else lg
                hi, lo = _split_bf16(lm)
                before = _dot(jnp.concatenate([hi, lo], axis=1), earlier2)
                between = ltots[hh] - (lpre[hh] + before + lm)
                a = jnp.exp(z + lg + between)
                if diag:
                    a = jnp.where(valid, a, 0.0)
                p = a * _dot_nt(doms[hh], vb)
                phi, plo = _split_bf16(p)
                pbefore = ppre[hh] + _dot(jnp.concatenate([phi, plo], axis=1), earlier2)
                sig = jnp.exp(z - sp)
                dz = p - sig * (p + pbefore)
                if diag:
                    dz = jnp.where(valid, dz, 0.0)
                dzs.append(dz.astype(BF16))
                avs.append(a.astype(BF16))
                lpre[hh] = lpre[hh] + jnp.sum(lm, axis=1, keepdims=True)
                ppre[hh] = ppre[hh] + jnp.sum(p, axis=1, keepdims=True)
            kstack = jnp.concatenate([jnp.where(hm, kb, jnp.zeros_like(kb)) for hm in hms], axis=0)
            dq = dq + _dot(jnp.concatenate(dzs, axis=1), kstack)
            dk_ref[pl.ds(off, t), :] += _dot_tn(jnp.concatenate(dzs, axis=0), qstack)
            dv_ref[pl.ds(off, t), :] += _dot_tn(jnp.concatenate(avs, axis=0), dostack)
            return lpre[0], lpre[1], ppre[0], ppre[1], dq

        zero = jnp.zeros((t, 1), F32)
        carry = lax.fori_loop(0, i, lambda j, cr: block(j, cr, False),
                              (zero, zero, zero, zero, jnp.zeros((t, 128), F32)))
        carry = block(i, carry, True)
        dq_ref[...] = carry[4].astype(BF16)

    return pl.pallas_call(
        body, name="sb_bwd", grid=(4, nq),
        out_shape=(_sds((s_len, SB_WIDTH), BF16), _sds((s_len, SB_WIDTH), F32), _sds((s_len, SB_WIDTH), F32)),
        in_specs=[pl.BlockSpec((t, 128), lambda p, i: (i, p)),
                  pl.BlockSpec((s_len, 128), lambda p, i: (0, p)),
                  pl.BlockSpec((s_len, 128), lambda p, i: (0, p)),
                  pl.BlockSpec((t, 128), lambda p, i: (i, p)),
                  pl.BlockSpec((1, t, 2), lambda p, i: (p, i, 0))],
        out_specs=(pl.BlockSpec((t, 128), lambda p, i: (i, p)),
                   pl.BlockSpec((s_len, 128), lambda p, i: (0, p)),
                   pl.BlockSpec((s_len, 128), lambda p, i: (0, p))),
        compiler_params=_params(("parallel", "arbitrary")),
    )(q, k, v, do, lt)


def _mla_fwd_call(qn, qp, kn, kpt, v):
    s_len = qn.shape[0]
    t = min(ATT_TILE, s_len)
    nq = s_len // t

    def body(qn_ref, qp_ref, kn_ref, kpt_ref, v_ref, o_ref, lse_ref):
        i = pl.program_id(1)
        qn2 = qn_ref[...]
        qp2 = qp_ref[...]
        lane256 = lax.broadcasted_iota(jnp.int32, (1, 256), 1)
        lane128 = lax.broadcasted_iota(jnp.int32, (1, 128), 1)
        row = lax.broadcasted_iota(jnp.int32, (t, t), 0)
        col = lax.broadcasted_iota(jnp.int32, (t, t), 1)
        valid = col <= row
        m64s = [(lane256 // 64) == hh for hh in range(4)]
        m32s = [(lane128 // 32) == hh for hh in range(4)]
        qcs = [jnp.concatenate([jnp.where(m64s[hh], qn2, jnp.zeros_like(qn2)),
                                jnp.where(m32s[hh], qp2, jnp.zeros_like(qp2))], axis=1) for hh in range(4)]

        def by_head(vals):
            return jnp.where(m64s[0], vals[0], jnp.where(m64s[1], vals[1], jnp.where(m64s[2], vals[2], vals[3])))

        def block(j, carry, diag):
            ms, ls, acc = list(carry[0:4]), list(carry[4:8]), carry[8]
            off = pl.multiple_of(j * t, t)
            kc = jnp.concatenate([kn_ref[pl.ds(off, t), :], kpt_ref[pl.ds(off, t), :]], axis=1)
            vb = v_ref[pl.ds(off, t), :]
            ps, alphas = [], []
            for hh in range(4):
                s = _dot_nt(qcs[hh], kc) * MLA_SCALE
                if diag:
                    s = jnp.where(valid, s, -1e30)
                mn = jnp.maximum(ms[hh], jnp.max(s, axis=1, keepdims=True))
                p = jnp.exp(s - mn)
                alpha = jnp.exp(ms[hh] - mn)
                ls[hh] = alpha * ls[hh] + jnp.sum(p, axis=1, keepdims=True)
                ms[hh] = mn
                ps.append(p.astype(BF16))
                alphas.append(alpha)
            vstack = jnp.concatenate([jnp.where(m64, vb, jnp.zeros_like(vb)) for m64 in m64s], axis=0)
            acc = by_head(alphas) * acc + _dot(jnp.concatenate(ps, axis=1), vstack)
            return (*ms, *ls, acc)

        neg = jnp.full((t, 1), -1e30, F32)
        zero = jnp.zeros((t, 1), F32)
        carry = lax.fori_loop(0, i, lambda j, cr: block(j, cr, False),
                              (neg, neg, neg, neg, zero, zero, zero, zero, jnp.zeros((t, 256), F32)))
        carry = block(i, carry, True)
        o_ref[...] = carry[8] / by_head(list(carry[4:8]))
        for hh in range(4):
            lse_ref[0, :, hh:hh + 1] = carry[hh] + jnp.log(carry[4 + hh])

    return pl.pallas_call(
        body, name="mla_fwd", grid=(2, nq),
        out_shape=(_sds((s_len, MLA_WIDTH), F32), _sds((2, s_len, 4), F32)),
        in_specs=[pl.BlockSpec((t, 256), lambda g, i: (i, g)),
                  pl.BlockSpec((t, 128), lambda g, i: (i, g)),
                  pl.BlockSpec((s_len, 256), lambda g, i: (0, g)),
                  pl.BlockSpec((s_len, 128), lambda g, i: (0, 0)),
                  pl.BlockSpec((s_len, 256), lambda g, i: (0, g))],
        out_specs=(pl.BlockSpec((t, 256), lambda g, i: (i, g)),
                   pl.BlockSpec((1, t, 4), lambda g, i: (g, i, 0))),
        compiler_params=_params(("parallel", "parallel")),
    )(qn, qp, kn, kpt, v)


def _mla_bwd_call(qn, qp, kn, kpt, v, o, do, lse):
    s_len = qn.shape[0]
    t = min(ATT_TILE, s_len)
    nq = s_len // t

    def body(qn_ref, qp_ref, kn_ref, kpt_ref, v_ref, o_ref, do_ref, lse_ref,
             dqn_ref, dqp_ref, dkn_ref, dkpt_ref, dv_ref):
        g = pl.program_id(0)
        i = pl.program_id(1)

        @pl.when(i == 0)
        def _():
            dkn_ref[...] = jnp.zeros_like(dkn_ref)
            dv_ref[...] = jnp.zeros_like(dv_ref)

        @pl.when((i == 0) & (g == 0))
        def _():
            dkpt_ref[...] = jnp.zeros_like(dkpt_ref)

        qn2 = qn_ref[...]
        qp2 = qp_ref[...]
        of = o_ref[...]
        dof = do_ref[...]
        dob = dof.astype(BF16)
        prod = dof * of
        lane256 = lax.broadcasted_iota(jnp.int32, (1, 256), 1)
        lane128 = lax.broadcasted_iota(jnp.int32, (1, 128), 1)
        row = lax.broadcasted_iota(jnp.int32, (t, t), 0)
        col = lax.broadcasted_iota(jnp.int32, (t, t), 1)
        valid = col <= row
        m64s = [(lane256 // 64) == hh for hh in range(4)]
        m32s = [(lane128 // 32) == hh for hh in range(4)]
        qcs = [jnp.concatenate([jnp.where(m64s[hh], qn2, jnp.zeros_like(qn2)),
                                jnp.where(m32s[hh], qp2, jnp.zeros_like(qp2))], axis=1) for hh in range(4)]
        doms = [jnp.where(m64, dob, jnp.zeros_like(dob)) for m64 in m64s]
        dsums = [jnp.sum(jnp.where(m64, prod, 0.0), axis=1, keepdims=True) * MLA_SCALE for m64 in m64s]
        lses = [lse_ref[0, :, hh:hh + 1] for hh in range(4)]
        qstack = jnp.concatenate(qcs, axis=0)
        dostack = jnp.concatenate(doms, axis=0)

        def block(j, dqc, diag):
            off = pl.multiple_of(j * t, t)
            knb = kn_ref[pl.ds(off, t), :]
            kpb = kpt_ref[pl.ds(off, t), :]
            vb = v_ref[pl.ds(off, t), :]
            kc = jnp.concatenate([knb, kpb], axis=1)
            dss, pbs = [], []
            for hh in range(4):
                s = _dot_nt(qcs[hh], kc) * MLA_SCALE
                if diag:
                    s = jnp.where(valid, s, -1e30)
                p = jnp.exp(s - lses[hh])
                ds = p * (_dot_nt(doms[hh], vb) * MLA_SCALE - dsums[hh])
                dss.append(ds.astype(BF16))
                pbs.append(p.astype(BF16))
            kstack = jnp.concatenate(
                [jnp.concatenate([jnp.where(m64s[hh], knb, jnp.zeros_like(knb)),
                                  jnp.where(m32s[hh], kpb, jnp.zeros_like(kpb))], axis=1) for hh in range(4)], axis=0)
            dqc = dqc + _dot(jnp.concatenate(dss, axis=1), kstack)
            dkc = _dot_tn(jnp.concatenate(dss, axis=0), qstack)
            dkn_ref[pl.ds(off, t), :] += dkc[:, :256]
            dkpt_ref[pl.ds(off, t), :] += dkc[:, 256:]
            dv_ref[pl.ds(off, t), :] += _dot_tn(jnp.concatenate(pbs, axis=0), dostack)
            return dqc

        dqc = lax.fori_loop(0, i, lambda j, cr: block(j, cr, False), jnp.zeros((t, 384), F32))
        dqc = block(i, dqc, True)
        dqn_ref[...] = dqc[:, :256].astype(BF16)
        dqp_ref[...] = dqc[:, 256:].astype(BF16)

    return pl.pallas_call(
        body, name="mla_bwd", grid=(2, nq),
        out_shape=(_sds((s_len, 512), BF16), _sds((s_len, 256), BF16), _sds((s_len, 512), F32),
                   _sds((s_len, 128), F32), _sds((s_len, 512), F32)),
        in_specs=[pl.BlockSpec((t, 256), lambda g, i: (i, g)),
                  pl.BlockSpec((t, 128), lambda g, i: (i, g)),
                  pl.BlockSpec((s_len, 256), lambda g, i: (0, g)),
                  pl.BlockSpec((s_len, 128), lambda g, i: (0, 0)),
                  pl.BlockSpec((s_len, 256), lambda g, i: (0, g)),
                  pl.BlockSpec((t, 256), lambda g, i: (i, g)),
                  pl.BlockSpec((t, 256), lambda g, i: (i, g)),
                  pl.BlockSpec((1, t, 4), lambda g, i: (g, i, 0))],
        out_specs=(pl.BlockSpec((t, 256), lambda g, i: (i, g)),
                   pl.BlockSpec((t, 128), lambda g, i: (i, g)),
                   pl.BlockSpec((s_len, 256), lambda g, i: (0, g)),
                   pl.BlockSpec((s_len, 128), lambda g, i: (0, 0)),
                   pl.BlockSpec((s_len, 256), lambda g, i: (0, g))),
        compiler_params=_params(("arbitrary", "arbitrary")),
    )(qn, qp, kn, kpt, v, o, do, lse)


def _post_call(x, tgt, oa, ob, sz, mz, ga, gb, gate, gf, wa, wb, wo, wat, wbt, wot):
    s_len = x.shape[0]
    tm = min(ROW_TILE, s_len)

    def body(x_ref, t_ref, oa_ref, ob_ref, sz_ref, mz_ref, ga_ref, gb_ref, gate_ref, gf_ref,
             wa_ref, wb_ref, wo_ref, wat_ref, wbt_ref, wot_ref,
             dx2_ref, doa_ref, dob_ref, dsz_ref, dmz_ref, dga_ref, dgb_ref,
             dwo_ref, dwa_ref, dwb_ref, dgf_ref, dgate_ref, loss_ref):
        @pl.when(pl.program_id(0) == 0)
        def _():
            dwo_ref[...] = jnp.zeros_like(dwo_ref)
            dwa_ref[...] = jnp.zeros_like(dwa_ref)
            dwb_ref[...] = jnp.zeros_like(dwb_ref)
            dgf_ref[...] = jnp.zeros_like(dgf_ref)
            dgate_ref[...] = jnp.zeros_like(dgate_ref)
            loss_ref[...] = jnp.zeros_like(loss_ref)

        gate = gate_ref[...]
        gf = gf_ref[...]
        oa = oa_ref[...]
        ob = ob_ref[...]
        sz = sz_ref[...]
        mz = mz_ref[...]
        sa = _sigmoid(sz)
        sb = _sigmoid(mz)
        silu_a = sz * sa
        silu_b = mz * sb
        ua = (oa * silu_a).astype(BF16)
        ub = (ob * silu_b).astype(BF16)
        ya = _dot(ua, wa_ref[...])
        yb = _dot(ub, wb_ref[...])
        sga = _sigmoid(ga_ref[...])
        sgb = _sigmoid(gb_ref[...])
        merged = (sga * ya + sgb * yb).astype(BF16)
        out = _dot(merged, wo_ref[...])
        x2 = x_ref[...] + gate * out
        r2 = lax.rsqrt(jnp.mean(x2 * x2, axis=-1, keepdims=True) + EPS)
        xhat = x2 * r2
        err = xhat * gf - t_ref[...]
        loss_ref[...] += 0.5 * jnp.sum(jnp.sum(err * err, axis=1, keepdims=True), axis=0, keepdims=True) / D_MODEL
        dy = err * (1.0 / D_MODEL)
        dgf_ref[...] += jnp.sum(dy * xhat, axis=0, keepdims=True)
        dxhat = dy * gf
        dx2 = r2 * (dxhat - xhat * jnp.mean(dxhat * xhat, axis=-1, keepdims=True))
        dx2_ref[...] = dx2
        dgate_ref[...] += jnp.sum(dx2 * out, axis=0, keepdims=True)
        dout = (dx2 * gate).astype(BF16)
        dmerged = _dot(dout, wot_ref[...])
        dwo_ref[...] += _dot_tn(merged, dout)
        dya = dmerged * sga
        dyb = dmerged * sgb
        dga_ref[...] = (dya * ya * (1.0 - sga)).astype(BF16)
        dgb_ref[...] = (dyb * yb * (1.0 - sgb)).astype(BF16)
        dyab = dya.astype(BF16)
        dybb = dyb.astype(BF16)
        dua = _dot(dyab, wat_ref[...])
        dub = _dot(dybb, wbt_ref[...])
        dwa_ref[...] += _dot_tn(ua, dyab)
        dwb_ref[...] += _dot_tn(ub, dybb)
        doa_ref[...] = dua * silu_a
        dob_ref[...] = dub * silu_b
        dsz_ref[...] = (dua * oa * (sa * (1.0 + sz * (1.0 - sa)))).astype(BF16)
        dmz_ref[...] = (dub * ob * (sb * (1.0 + mz * (1.0 - sb)))).astype(BF16)

    return pl.pallas_call(
        body, name="post", grid=(s_len // tm,),
        out_shape=(_sds((s_len, D_MODEL), F32), _sds((s_len, 512), F32), _sds((s_len, 512), F32),
                   _sds((s_len, 512), BF16), _sds((s_len, 512), BF16),
                   _sds((s_len, D_MODEL), BF16), _sds((s_len, D_MODEL), BF16),
                   _sds((D_MODEL, D_MODEL), F32), _sds((512, D_MODEL), F32), _sds((512, D_MODEL), F32),
                   _sds((1, D_MODEL), F32), _sds((1, D_MODEL), F32), _sds((1, 128), F32)),
        in_specs=[_rows(tm, D_MODEL), _rows(tm, D_MODEL), _rows(tm, 512), _rows(tm, 512), _rows(tm, 512),
                  _rows(tm, 512), _rows(tm, D_MODEL), _rows(tm, D_MODEL), _whole((1, D_MODEL)), _whole((1, D_MODEL)),
                  _whole((512, D_MODEL)), _whole((512, D_MODEL)), _whole((D_MODEL, D_MODEL)),
                  _whole((D_MODEL, 512)), _whole((D_MODEL, 512)), _whole((D_MODEL, D_MODEL))],
        out_specs=(_rows(tm, D_MODEL), _rows(tm, 512), _rows(tm, 512), _rows(tm, 512), _rows(tm, 512),
                   _rows(tm, D_MODEL), _rows(tm, D_MODEL),
                   _whole((D_MODEL, D_MODEL)), _whole((512, D_MODEL)), _whole((512, D_MODEL)),
                   _whole((1, D_MODEL)), _whole((1, D_MODEL)), _whole((1, 128))),
        compiler_params=_params(("arbitrary",)),
    )(x, tgt, oa, ob, sz, mz, ga, gb, gate, gf, wa, wb, wo, wat, wbt, wot)


def _bwdprep_call(dsq, dsk, dsv, dsz, dqn, dqp, dkn, dvv, dkpt, dmz, dga, dgb, cq, ckv, cos256, sin256,
                  qg, kvg, wqt, wkvt):
    s_len = cq.shape[0]
    tm = min(ROW_TILE, s_len)

    def body(dsq_ref, dsk_ref, dsv_ref, dsz_ref, dqn_ref, dqp_ref, dkn_ref, dvv_ref, dkpt_ref, dmz_ref,
             dga_ref, dgb_ref, cq_ref, ckv_ref, cos_ref, sin_ref, qg_ref, kvg_ref, wqt_ref, wkvt_ref,
             dp_ref, dwq_ref, dwkv_ref, dqg_ref, dkvg_ref):
        @pl.when(pl.program_id(0) == 0)
        def _():
            dwq_ref[...] = jnp.zeros_like(dwq_ref)
            dwkv_ref[...] = jnp.zeros_like(dwkv_ref)
            dqg_ref[...] = jnp.zeros_like(dqg_ref)
            dkvg_ref[...] = jnp.zeros_like(dkvg_ref)

        cos = cos_ref[...]
        sin = sin_ref[...]
        dp_ref[:, O_SQ:O_SK] = dsq_ref[...]
        dp_ref[:, O_SK:O_SV] = dsk_ref[...].astype(BF16)
        dp_ref[:, O_SV:O_SZ] = dsv_ref[...].astype(BF16)
        dp_ref[:, O_SZ:O_CQ] = dsz_ref[...]
        dp_ref[:, O_MZ:O_GA] = dmz_ref[...]
        dp_ref[:, O_GA:O_GB] = dga_ref[...]
        dp_ref[:, O_GB:O_KR] = dgb_ref[...]
        dkp = dkpt_ref[...]
        dp_ref[:, O_KR:O_KR + 128] = (dkp * cos[:, :128]).astype(BF16)
        dp_ref[:, O_KR + 128:O_END] = (dkp * sin[:, :128]).astype(BF16)
        dp_ref[:, O_END:W_INT] = jnp.zeros((tm, W_INT - O_END), BF16)

        cq = cq_ref[...]
        rq = lax.rsqrt(jnp.mean(cq * cq, axis=-1, keepdims=True) + EPS)
        cqh = cq * rq
        qg = qg_ref[...]
        cqn = (cqh * qg).astype(BF16)
        dqp = dqp_ref[...].astype(F32)
        dqa = jnp.concatenate([dqn_ref[...], (dqp * cos).astype(BF16), (dqp * sin).astype(BF16)], axis=1)
        dcqn = _dot(dqa, wqt_ref[...])
        dwq_ref[...] += _dot_tn(cqn, dqa)
        dqg_ref[...] += jnp.sum(dcqn * cqh, axis=0, keepdims=True)
        dh = dcqn * qg
        dcq = rq * (dh - cqh * jnp.mean(dh * cqh, axis=-1, keepdims=True))
        dp_ref[:, O_CQ:O_CKV] = dcq.astype(BF16)

        ckv = ckv_ref[...]
        rk = lax.rsqrt(jnp.mean(ckv * ckv, axis=-1, keepdims=True) + EPS)
        ckh = ckv * rk
        kvg = kvg_ref[...]
        ckvn = (ckh * kvg).astype(BF16)
        dkva = jnp.concatenate([dkn_ref[...].astype(BF16), dvv_ref[...].astype(BF16)], axis=1)
        dckvn = _dot(dkva, wkvt_ref[...])
        dwkv_ref[...] += _dot_tn(ckvn, dkva)
        dkvg_ref[...] += jnp.sum(dckvn * ckh, axis=0, keepdims=True)
        dh2 = dckvn * kvg
        dckv = rk * (dh2 - ckh * jnp.mean(dh2 * ckh, axis=-1, keepdims=True))
        dp_ref[:, O_CKV:O_MZ] = dckv.astype(BF16)

    return pl.pallas_call(
        body, name="bwdprep", grid=(s_len // tm,),
        out_shape=(_sds((s_len, W_INT), BF16), _sds((Q_RANK, 1024), F32), _sds((KV_RANK, 1024), F32),
                   _sds((1, Q_RANK), F32), _sds((1, KV_RANK), F32)),
        in_specs=[_rows(tm, 512), _rows(tm, 512), _rows(tm, 512), _rows(tm, 512), _rows(tm, 512), _rows(tm, 256),
                  _rows(tm, 512), _rows(tm, 512), _rows(tm, 128), _rows(tm, 512), _rows(tm, D_MODEL),
                  _rows(tm, D_MODEL), _rows(tm, Q_RANK), _rows(tm, KV_RANK), _rows(tm, 256), _rows(tm, 256),
                  _whole((1, Q_RANK)), _whole((1, KV_RANK)), _whole((1024, Q_RANK)), _whole((1024, KV_RANK))],
        out_specs=(_rows(tm, W_INT), _whole((Q_RANK, 1024)), _whole((KV_RANK, 1024)),
                   _whole((1, Q_RANK)), _whole((1, KV_RANK))),
        compiler_params=_params(("arbitrary",)),
    )(dsq, dsk, dsv, dsz, dqn, dqp, dkn, dvv, dkpt, dmz, dga, dgb, cq, ckv, cos256, sin256, qg, kvg, wqt, wkvt)


def _dh_call(dproj, w_int_t, x, dx2, scale, g1):
    s_len = x.shape[0]
    tm = min(ROW_TILE, s_len)

    def body(dp_ref, wt_ref, x_ref, dx2_ref, sc_ref, g1_ref, gx_ref, dsh_ref, dsc_ref, dg1_ref):
        @pl.when(pl.program_id(0) == 0)
        def _():
            dsh_ref[...] = jnp.zeros_like(dsh_ref)
            dsc_ref[...] = jnp.zeros_like(dsc_ref)
            dg1_ref[...] = jnp.zeros_like(dg1_ref)

        dh = _dot(dp_ref[...], wt_ref[...])
        xt = x_ref[...]
        r = lax.rsqrt(jnp.mean(xt * xt, axis=-1, keepdims=True) + EPS)
        xh = xt * r
        g1 = g1_ref[...]
        xg = xh * g1
        dsh_ref[...] += jnp.sum(dh, axis=0, keepdims=True)
        dsc_ref[...] += jnp.sum(dh * xg, axis=0, keepdims=True)
        dxg = dh * (1.0 + sc_ref[...])
        dg1_ref[...] += jnp.sum(dxg * xh, axis=0, keepdims=True)
        dxh = dxg * g1
        gx_ref[...] = dx2_ref[...] + r * (dxh - xh * jnp.mean(dxh * xh, axis=-1, keepdims=True))

    return pl.pallas_call(
        body, name="dh", grid=(s_len // tm,),
        out_shape=(_sds((s_len, D_MODEL), F32), _sds((1, D_MODEL), F32), _sds((1, D_MODEL), F32),
                   _sds((1, D_MODEL), F32)),
        in_specs=[_rows(tm, W_INT), _whole((W_INT, D_MODEL)), _rows(tm, D_MODEL), _rows(tm, D_MODEL),
                  _whole((1, D_MODEL)), _whole((1, D_MODEL))],
        out_specs=(_rows(tm, D_MODEL), _whole((1, D_MODEL)), _whole((1, D_MODEL)), _whole((1, D_MODEL))),
        compiler_params=_params(("arbitrary",)),
    )(dproj, w_int_t, x, dx2, scale, g1)


def _dwin_call(h, dproj):
    s_len = h.shape[0]
    tm = min(2 * ROW_TILE, s_len)
    nc = 4
    chunk = W_INT // nc

    def body(h_ref, dp_ref, dw_ref):
        @pl.when(pl.program_id(1) == 0)
        def _():
            dw_ref[...] = jnp.zeros_like(dw_ref)

        dw_ref[...] += _dot_tn(h_ref[...], dp_ref[...])

    return pl.pallas_call(
        body, name="dwin", grid=(nc, s_len // tm),
        out_shape=_sds((D_MODEL, nc * chunk), F32),
        in_specs=[pl.BlockSpec((tm, D_MODEL), lambda c, i: (i, 0)),
                  pl.BlockSpec((tm, chunk), lambda c, i: (i, c))],
        out_specs=pl.BlockSpec((D_MODEL, chunk), lambda c, i: (0, c)),
        compiler_params=_params(("parallel", "arbitrary")),
    )(h, dproj)


def _small_call(svg, ct, dmod_sh):
    def body(sv_ref, ct_ref, dm_ref, tot_ref, gwada_ref):
        acc = sv_ref[0:1, :]
        for d in range(1, N_DEV):
            acc = acc + sv_ref[d:d + 1, :]
        tot_ref[...] = acc
        gwada_ref[...] = lax.dot_general(ct_ref[...], dm_ref[...], (((1,), (0,)), ((), ())),
                                         precision=lax.Precision.HIGHEST, preferred_element_type=F32)

    vmem = pl.BlockSpec(memory_space=pltpu.VMEM)
    return pl.pallas_call(
        body, name="small_grads",
        out_shape=(_sds((1, 8 * SV_COLS), F32), _sds((D_MODEL, 768), F32)),
        in_specs=[vmem, vmem, vmem], out_specs=(vmem, vmem),
        compiler_params=_params(),
    )(svg, ct, dmod_sh)


def _adamw_tile_rows(rows, cols):
    budget = 1 << 20
    if rows * cols * 4 <= budget or rows % 8:
        return rows
    best = 8
    for tr in range(8, rows + 1, 8):
        if rows % tr == 0 and tr * cols * 4 <= budget:
            best = tr
    return best


def _adamw_call(name, w, g, m, v):
    rows, cols = w.shape
    tr = _adamw_tile_rows(rows, cols)

    def body(w_ref, g_ref, m_ref, v_ref, d_ref, nm_ref, nv_ref):
        gg = g_ref[...]
        m2 = ADAM_B1 * m_ref[...] + (1.0 - ADAM_B1) * gg
        v2 = ADAM_B2 * v_ref[...] + (1.0 - ADAM_B2) * (gg * gg)
        m_hat = m2 / (1.0 - ADAM_B1 ** ADAM_STEP)
        v_hat = v2 / (1.0 - ADAM_B2 ** ADAM_STEP)
        d_ref[...] = -ADAM_LR * (m_hat / (jnp.sqrt(v_hat) + ADAM_EPS) + ADAM_WD * w_ref[...])
        nm_ref[...] = m2
        nv_ref[...] = v2

    spec = pl.BlockSpec((tr, cols), lambda i: (i, 0))
    return pl.pallas_call(
        body, name="adamw_" + name, grid=(rows // tr,),
        out_shape=(_sds((rows, cols), F32),) * 3,
        in_specs=[spec] * 4, out_specs=(spec,) * 3,
        compiler_params=_params(("parallel",)),
    )(w, g, m, v)


def _swap_halves(w, group):
    r, n = w.shape
    return w.reshape(r, n // group, 2, group // 2)[:, :, ::-1, :].reshape(r, n)


def _pack_shards(parts):
    return jnp.concatenate([p.reshape(-1, LANES) for p in parts], axis=0)


def _unpack_chip_major(gw):
    offs = [0]
    for r in PACK_ROWS:
        offs.append(offs[-1] + r)

    def cols(i, rows, shard_cols):
        blk = gw[:, offs[i]:offs[i + 1]].reshape(N_CHIPS, rows, shard_cols)
        return blk.transpose(1, 0, 2).reshape(rows, N_CHIPS * shard_cols)

    w_in = cols(0, D_MODEL, 1320)
    w_uq = cols(1, Q_RANK, 192)
    w_ukv = cols(2, KV_RANK, 256)
    w_a = cols(3, 512, 256)
    w_b = cols(4, 512, 256)
    w_out = gw[:, offs[5]:offs[6]].reshape(D_MODEL, D_MODEL)
    return w_in, w_uq, w_ukv, w_a, w_b, w_out


def _internal_weights(w_in, w_uq, w_ukv):
    krot = w_in[:, 2688:2720]
    w_int = jnp.concatenate([
        w_in[:, 0:512] * jnp.asarray(0.125, w_in.dtype), w_in[:, 512:2048],
        w_in[:, 2048:2432], w_in[:, 2432:2688], w_in[:, 2720:3232], w_in[:, 3232:4256], w_in[:, 4256:5280],
        jnp.tile(krot, (1, 4)), jnp.tile(_swap_halves(krot, 32), (1, 4)),
        jnp.zeros((D_MODEL, W_INT - O_END), w_in.dtype)], axis=1)
    uq = w_uq.reshape(Q_RANK, N_HEADS, 96)
    wp = uq[:, :, 64:].reshape(Q_RANK, 256)
    w_q = jnp.concatenate([uq[:, :, :64].reshape(Q_RANK, 512), wp, _swap_halves(wp, 32)], axis=1)
    ukv = w_ukv.reshape(KV_RANK, N_HEADS, 128)
    w_kv = jnp.concatenate([ukv[:, :, :64].reshape(KV_RANK, 512), ukv[:, :, 64:].reshape(KV_RANK, 512)], axis=1)
    return w_int, w_q, w_kv


def _true_weight_grads(dwi, dwq, dwkv):
    dkr = dwi[:, O_KR:O_KR + 128].reshape(D_MODEL, 4, 32).sum(axis=1)
    dkr_sw = dwi[:, O_KR + 128:O_END].reshape(D_MODEL, 4, 32).sum(axis=1)
    dkrot = dkr + _swap_halves(dkr_sw, 32)
    g_in = jnp.concatenate([
        dwi[:, 0:512] * 0.125, dwi[:, 512:2048], dwi[:, O_CQ:O_CKV], dwi[:, O_CKV:O_MZ], dkrot,
        dwi[:, O_MZ:O_GA], dwi[:, O_GA:O_GB], dwi[:, O_GB:O_KR]], axis=1)
    dwp = dwq[:, 512:768] + _swap_halves(dwq[:, 768:1024], 32)
    g_uq = jnp.concatenate([dwq[:, :512].reshape(Q_RANK, N_HEADS, 64), dwp.reshape(Q_RANK, N_HEADS, 32)],
                           axis=2).reshape(Q_RANK, 768)
    g_ukv = jnp.concatenate([dwkv[:, :512].reshape(KV_RANK, N_HEADS, 64), dwkv[:, 512:].reshape(KV_RANK, N_HEADS, 64)],
                            axis=2).reshape(KV_RANK, 1024)
    return g_in, g_uq, g_ukv


def _chip_major(g, shard_cols):
    r = g.shape[0]
    return g.reshape(r, N_CHIPS, shard_cols).transpose(1, 0, 2).reshape(N_CHIPS, -1, LANES)


def kernel(x, c, positions, w_ada, b_ada, norm_gain, w_in, q_norm_gain, w_uq, kv_norm_gain, w_ukv, w_branch_a, w_branch_b, w_out, final_norm_gain, loss_target, m_w_ada, m_b_ada, m_norm_gain, m_w_in, m_q_norm_gain, m_w_uq, m_kv_norm_gain, m_w_ukv, m_w_branch_a, m_w_branch_b, m_w_out, m_final_norm_gain, v_w_ada, v_b_ada, v_norm_gain, v_w_in, v_q_norm_gain, v_w_uq, v_kv_norm_gain, v_w_ukv, v_w_branch_a, v_w_branch_b, v_w_out, v_final_norm_gain):
    ix, iy, ic = lax.axis_index("x"), lax.axis_index("y"), lax.axis_index("c")
    me = 4 * ix + 2 * iy + ic
    chip = 2 * ix + iy
    xs = x[0]
    tgt = loss_target[0]
    s_len = xs.shape[0]

    shards = (w_in[0], w_uq[0], w_ukv[0], w_branch_a[0], w_branch_b[0], w_out[0])
    pack = _pack_shards([s.astype(BF16) for s in shards])
    mg, call, gw = _gather_call(c, w_ada[0], pack)
    mod = mg.transpose(1, 0, 2).reshape(N_DEV, 3 * D_MODEL) + b_ada
    mod_me = lax.dynamic_slice_in_dim(mod, me, 1, axis=0)
    shift, scale, gate = mod_me[:, :D_MODEL], mod_me[:, D_MODEL:2 * D_MODEL], mod_me[:, 2 * D_MODEL:]

    f_in, f_uq, f_ukv, f_a, f_b, f_out = _unpack_chip_major(gw)
    w_int, w_q, w_kv = _internal_weights(f_in, f_uq, f_ukv)

    inv_freq = ROPE_BASE ** (-jnp.arange(0, ROPE_DIM, 2, dtype=F32) / ROPE_DIM)
    ang = positions[0].astype(F32)[:, None] * inv_freq
    cs, sn = jnp.cos(ang), jnp.sin(ang)
    cos256 = jnp.tile(jnp.concatenate([cs, cs], axis=1), (1, 8))
    sin256 = jnp.tile(jnp.concatenate([-sn, sn], axis=1), (1, 8))

    (h, sq, sk, sv, sz, cq, ckv, mz, ga, gb, kpt, qn, qp, kn, vv) = _inproj_call(
        xs, shift, scale, norm_gain, w_int, w_q, w_kv, q_norm_gain, kv_norm_gain, cos256, sin256)
    oa, lt = _sb_fwd_call(sq, sk, sv)
    ob, lse = _mla_fwd_call(qn, qp, kn, kpt, vv)

    gf = final_norm_gain.reshape(1, D_MODEL)
    (dx2, doa, dob, dsz, dmz, dga, dgb, dwo, dwa, dwb, dgf, dgate, loss_p) = _post_call(
        xs, tgt, oa, ob, sz, mz, ga, gb, gate, gf, f_a, f_b, f_out, f_a.T, f_b.T, f_out.T)

    dsq, dsk, dsv = _sb_bwd_call(sq, sk, sv, doa, lt)
    dqn, dqp, dkn, dkpt, dvv = _mla_bwd_call(qn, qp, kn, kpt, vv, ob, dob, lse)

    dproj, dwq, dwkv, dqg, dkvg = _bwdprep_call(
        dsq, dsk, dsv, dsz, dqn, dqp, dkn, dvv, dkpt, dmz, dga, dgb, cq, ckv, cos256, sin256,
        q_norm_gain, kv_norm_gain, w_q.T, w_kv.T)
    grad_x, dshift, dscale, dg1 = _dh_call(dproj, w_int.T, xs, dx2, scale, norm_gain)
    dwi = _dwin_call(h, dproj)
    g_in, g_uq, g_ukv = _true_weight_grads(dwi, dwq, dwkv)

    gpack = jnp.concatenate([
        _chip_major(g_in, 1320), _chip_major(g_uq, 192), _chip_major(g_ukv, 256),
        _chip_major(dwa, 256), _chip_major(dwb, 256), dwo.reshape(N_CHIPS, -1, LANES)], axis=1)
    gpack = gpack.reshape(N_DEV, HALF_ROWS, LANES)
    small = jnp.concatenate([
        dshift, dscale, dgate, dg1, dqg, dkvg, dgf, loss_p,
        jnp.zeros((1, 8 * SV_COLS - 5888), F32)], axis=1).reshape(8, SV_COLS)
    land, svg = _scatter_call(gpack, small)
    red = _sum_call(land)
    full = _sibling_call(red).reshape(PACK_TOTAL, LANES)
    offs = [0]
    for r in PACK_ROWS:
        offs.append(offs[-1] + r)
    gs_in = full[offs[0]:offs[1]].reshape(D_MODEL, 1320)
    gs_uq = full[offs[1]:offs[2]].reshape(Q_RANK, 192)
    gs_ukv = full[offs[2]:offs[3]].reshape(KV_RANK, 256)
    gs_a = full[offs[3]:offs[4]].reshape(512, 256)
    gs_b = full[offs[4]:offs[5]].reshape(512, 256)
    gs_out = full[offs[5]:offs[6]].reshape(256, D_MODEL)

    svm = svg.reshape(N_DEV, 8 * SV_COLS)
    dmod_sh = lax.dynamic_slice_in_dim(svm[:, :3 * D_MODEL], chip * 768, 768, axis=1)
    tot, gs_ada = _small_call(svm, call.T, dmod_sh)
    g_bada = tot[:, 0:3072]
    g_g1 = tot[:, 3072:4096]
    g_qg = tot[:, 4096:4480]
    g_kvg = tot[:, 4480:4736]
    g_gf = tot[:, 4736:5760]
    loss = tot[0, 5760]

    names = ["w_ada", "b_ada", "norm_gain", "w_in", "q_norm_gain", "w_uq", "kv_norm_gain", "w_ukv",
             "w_branch_a", "w_branch_b", "w_out", "final_norm_gain"]
    ws = [w_ada[0], b_ada, norm_gain, w_in[0], q_norm_gain, w_uq[0], kv_norm_gain, w_ukv[0],
          w_branch_a[0], w_branch_b[0], w_out[0], final_norm_gain.reshape(1, D_MODEL)]
    gs = [gs_ada, g_bada, g_g1, gs_in, g_qg, gs_uq, g_kvg, gs_ukv, gs_a, gs_b, gs_out, g_gf]
    ms = [m_w_ada[0], m_b_ada, m_norm_gain, m_w_in[0], m_q_norm_gain, m_w_uq[0], m_kv_norm_gain, m_w_ukv[0],
          m_w_branch_a[0], m_w_branch_b[0], m_w_out[0], m_final_norm_gain.reshape(1, D_MODEL)]
    vs = [v_w_ada[0], v_b_ada, v_norm_gain, v_w_in[0], v_q_norm_gain, v_w_uq[0], v_kv_norm_gain, v_w_ukv[0],
          v_w_branch_a[0], v_w_branch_b[0], v_w_out[0], v_final_norm_gain.reshape(1, D_MODEL)]
    refs = [w_ada, b_ada, norm_gain, w_in, q_norm_gain, w_uq, kv_norm_gain, w_ukv,
            w_branch_a, w_branch_b, w_out, final_norm_gain]
    grads, deltas, new_ms, new_vs = [], [], [], []
    for n, w_, g_, m_, v_, ref in zip(names, ws, gs, ms, vs, refs):
        d_, nm_, nv_ = _adamw_call(n, w_, g_, m_, v_)
        grads.append(g_.reshape(ref.shape))
        deltas.append(d_.reshape(ref.shape))
        new_ms.append(nm_.reshape(ref.shape))
        new_vs.append(nv_.reshape(ref.shape))

    return (loss, grad_x.reshape(x.shape), *grads, *deltas, *new_ms, *new_vs)
```

```python
import functools
import math

import jax
import jax.numpy as jnp
from jax import lax
from jax.experimental import pallas as pl
from jax.experimental.pallas import tpu as pltpu

F32 = jnp.float32
BF16 = jnp.bfloat16

D_MODEL = 1024
SB_WIDTH = 512
MLA_WIDTH = 512
Q_RANK = 384
KV_RANK = 256
ROPE_DIM = 32
N_HEADS = 8
IN_WIDTH = 5280
EPS = 1e-6
ROPE_BASE = 10000.0
MLA_SCALE = 1.0 / math.sqrt(96.0)

ADAM_LR = 0.001
ADAM_B1 = 0.9
ADAM_B2 = 0.999
ADAM_EPS = 1e-08
ADAM_WD = 0.01
ADAM_STEP = 10

O_SQ, O_SK, O_SV, O_SZ, O_CQ, O_CKV, O_MZ, O_GA, O_GB, O_KR, O_END = (
    0, 512, 1024, 1536, 2048, 2432, 2688, 3200, 4224, 5248, 5504)
W_INT = 5632

N_CHIPS = 4
N_DEV = 8
LANES = 128
PACK_ROWS = (10560, 576, 512, 1024, 1024, 2048)
PACK_TOTAL = sum(PACK_ROWS)
HALF_ROWS = PACK_TOTAL // 2
SV_COLS = 768

ROW_TILE = 256
ATT_TILE = 256
VMEM_LIMIT = 56 * 1024 * 1024

MESH = pl.DeviceIdType.MESH


def _dot(a, b):
    return lax.dot_general(a, b, (((1,), (0,)), ((), ())), preferred_element_type=F32)


def _dot_nt(a, b):
    return lax.dot_general(a, b, (((1,), (1,)), ((), ())), preferred_element_type=F32)


def _dot_tn(a, b):
    return lax.dot_general(a, b, (((0,), (0,)), ((), ())), preferred_element_type=F32)


def _sigmoid(z):
    return 1.0 / (1.0 + jnp.exp(-z))


def _params(sem=None):
    if sem is None:
        return pltpu.CompilerParams(vmem_limit_bytes=VMEM_LIMIT)
    return pltpu.CompilerParams(dimension_semantics=sem, vmem_limit_bytes=VMEM_LIMIT)


def _rows(tm, n):
    return pl.BlockSpec((tm, n), lambda i: (i, 0))


def _whole(shape):
    nd = len(shape)
    return pl.BlockSpec(shape, lambda i: (0,) * nd)


def _sds(shape, dtype):
    return jax.ShapeDtypeStruct(shape, dtype)


def _flip(v, d):
    return 1 - v if d else v


def _gather_call(c_row, w_ada_sh, pack):
    def body(c_ref, wada_ref, pk_ref, mg_ref, cg_ref, gw_ref,
             cv, ssem_c, rsem_c, ssem_m, rsem_m, ssem_w, rsem_w, ssem_f, rsem_f, lsem):
        x, y, c = lax.axis_index("x"), lax.axis_index("y"), lax.axis_index("c")
        me = 4 * x + 2 * y + c
        chip = 2 * x + y
        rel3 = [(1, 0), (0, 1), (1, 1)]

        wcopies = []
        for j, (dx, dy) in enumerate(rel3):
            cp = pltpu.make_async_remote_copy(
                src_ref=pk_ref.at[c], dst_ref=gw_ref.at[chip, c], send_sem=ssem_w.at[j], recv_sem=rsem_w.at[j],
                device_id=(_flip(x, dx), _flip(y, dy), c), device_id_type=MESH)
            cp.start()
            wcopies.append(cp)
        own = pltpu.make_async_copy(pk_ref, gw_ref.at[chip], lsem)
        own.start()

        cv[me] = c_ref[...]
        ccopies = []
        for r in range(1, N_DEV):
            dx, dy, dc = (r >> 2) & 1, (r >> 1) & 1, r & 1
            cp = pltpu.make_async_remote_copy(
                src_ref=c_ref, dst_ref=cv.at[me], send_sem=ssem_c.at[r - 1], recv_sem=rsem_c.at[r - 1],
                device_id=(_flip(x, dx), _flip(y, dy), _flip(c, dc)), device_id_type=MESH)
            cp.start()
            ccopies.append(cp)
        for r in range(1, N_DEV):
            dx, dy, dc = (r >> 2) & 1, (r >> 1) & 1, r & 1
            src = 4 * _flip(x, dx) + 2 * _flip(y, dy) + _flip(c, dc)
            pltpu.make_async_remote_copy(
                src_ref=c_ref, dst_ref=cv.at[src], send_sem=ssem_c.at[r - 1], recv_sem=rsem_c.at[r - 1],
                device_id=(x, y, c), device_id_type=MESH).wait_recv()
        rows = lax.broadcasted_iota(jnp.int32, (N_DEV, D_MODEL), 0)
        call = jnp.zeros((N_DEV, D_MODEL), F32)
        for b in range(N_DEV):
            call = jnp.where(rows == b, jnp.broadcast_to(cv[b], (N_DEV, D_MODEL)), call)
        cg_ref[...] = call

        mg_ref[chip] = lax.dot_general(call, wada_ref[...], (((1,), (0,)), ((), ())),
                                       precision=lax.Precision.HIGHEST, preferred_element_type=F32)
        mcopies = []
        for j, (dx, dy) in enumerate(rel3):
            cp = pltpu.make_async_remote_copy(
                src_ref=mg_ref.at[chip], dst_ref=mg_ref.at[chip], send_sem=ssem_m.at[j], recv_sem=rsem_m.at[j],
                device_id=(_flip(x, dx), _flip(y, dy), c), device_id_type=MESH)
            cp.start()
            mcopies.append(cp)
        for j, (dx, dy) in enumerate(rel3):
            src_chip = 2 * _flip(x, dx) + _flip(y, dy)
            pltpu.make_async_remote_copy(
                src_ref=mg_ref.at[src_chip], dst_ref=mg_ref.at[src_chip], send_sem=ssem_m.at[j],
                recv_sem=rsem_m.at[j], device_id=(x, y, c), device_id_type=MESH).wait_recv()
        fcopies = []
        for j, (dx, dy) in enumerate(rel3):
            src_chip = 2 * _flip(x, dx) + _flip(y, dy)
            pltpu.make_async_remote_copy(
                src_ref=pk_ref.at[c], dst_ref=gw_ref.at[src_chip, c], send_sem=ssem_w.at[j], recv_sem=rsem_w.at[j],
                device_id=(x, y, c), device_id_type=MESH).wait_recv()
            cp = pltpu.make_async_remote_copy(
                src_ref=gw_ref.at[src_chip, c], dst_ref=gw_ref.at[src_chip, c], send_sem=ssem_f.at[j],
                recv_sem=rsem_f.at[j], device_id=(x, y, 1 - c), device_id_type=MESH)
            cp.start()
            fcopies.append(cp)
        for j, (dx, dy) in enumerate(rel3):
            src_chip = 2 * _flip(x, dx) + _flip(y, dy)
            pltpu.make_async_remote_copy(
                src_ref=pk_ref.at[c], dst_ref=gw_ref.at[src_chip, 1 - c], send_sem=ssem_f.at[j],
                recv_sem=rsem_f.at[j], device_id=(x, y, c), device_id_type=MESH).wait_recv()
        for cp in ccopies + mcopies + wcopies + fcopies:
            cp.wait_send()
        own.wait()

    vmem = pl.BlockSpec(memory_space=pltpu.VMEM)
    return pl.pallas_call(
        body, name="gather_fwd",
        out_shape=(_sds((N_CHIPS, N_DEV, 768), F32), _sds((N_DEV, D_MODEL), F32),
                   _sds((N_CHIPS, 2, HALF_ROWS, LANES), BF16)),
        in_specs=[vmem, vmem, vmem], out_specs=(vmem, vmem, vmem),
        scratch_shapes=[
            pltpu.VMEM((N_DEV, 1, D_MODEL), F32),
            pltpu.SemaphoreType.DMA((N_DEV - 1,)), pltpu.SemaphoreType.DMA((N_DEV - 1,)),
            pltpu.SemaphoreType.DMA((3,)), pltpu.SemaphoreType.DMA((3,)),
            pltpu.SemaphoreType.DMA((3,)), pltpu.SemaphoreType.DMA((3,)),
            pltpu.SemaphoreType.DMA((3,)), pltpu.SemaphoreType.DMA((3,)),
            pltpu.SemaphoreType.DMA,
        ],
        compiler_params=_params(),
    )(c_row, w_ada_sh, pack)


SUM_ROWS = 656


def _reduce_call(gpack, sv):
    def body(g_ref, sv_ref, full_ref, svg_ref, land, ssem_g, rsem_g, ssem_s, rsem_s, ssem_x, rsem_x, lsem):
        x, y, c = lax.axis_index("x"), lax.axis_index("y"), lax.axis_index("c")
        me = 4 * x + 2 * y + c
        copies = []
        for r in range(1, N_DEV):
            dx, dy, dc = (r >> 2) & 1, (r >> 1) & 1, r & 1
            tx, ty, tc = _flip(x, dx), _flip(y, dy), _flip(c, dc)
            tgt = 4 * tx + 2 * ty + tc
            cp = pltpu.make_async_remote_copy(
                src_ref=g_ref.at[tgt], dst_ref=land.at[me], send_sem=ssem_g.at[r - 1],
                recv_sem=rsem_g.at[r - 1], device_id=(tx, ty, tc), device_id_type=MESH)
            cp.start()
            copies.append(cp)
            cp = pltpu.make_async_remote_copy(
                src_ref=sv_ref, dst_ref=svg_ref.at[me], send_sem=ssem_s.at[r - 1],
                recv_sem=rsem_s.at[r - 1], device_id=(tx, ty, tc), device_id_type=MESH)
            cp.start()
            copies.append(cp)
        own = pltpu.make_async_copy(g_ref.at[me], land.at[me], lsem)
        own.start()
        svg_ref[me] = sv_ref[...]
        for r in range(1, N_DEV):
            dx, dy, dc = (r >> 2) & 1, (r >> 1) & 1, r & 1
            src = 4 * _flip(x, dx) + 2 * _flip(y, dy) + _flip(c, dc)
            pltpu.make_async_remote_copy(
                src_ref=g_ref.at[src], dst_ref=land.at[src], send_sem=ssem_g.at[r - 1],
                recv_sem=rsem_g.at[r - 1], device_id=(x, y, c), device_id_type=MESH).wait_recv()
            pltpu.make_async_remote_copy(
                src_ref=sv_ref, dst_ref=svg_ref.at[src], send_sem=ssem_s.at[r - 1],
                recv_sem=rsem_s.at[r - 1], device_id=(x, y, c), device_id_type=MESH).wait_recv()
        own.wait()

        def sum_rows(i, carry):
            sl = pl.ds(pl.multiple_of(i * SUM_ROWS, 16), SUM_ROWS)
            acc = land[0, sl, :].astype(F32)
            for d in range(1, N_DEV):
                acc = acc + land[d, sl, :].astype(F32)
            full_ref[c, sl, :] = acc
            return carry

        lax.fori_loop(0, HALF_ROWS // SUM_ROWS, sum_rows, 0)
        swap = pltpu.make_async_remote_copy(
            src_ref=full_ref.at[c], dst_ref=full_ref.at[c], send_sem=ssem_x, recv_sem=rsem_x,
            device_id=(x, y, 1 - c), device_id_type=MESH)
        swap.start()
        pltpu.make_async_remote_copy(
            src_ref=full_ref.at[c], dst_ref=full_ref.at[1 - c], send_sem=ssem_x, recv_sem=rsem_x,
            device_id=(x, y, c), device_id_type=MESH).wait_recv()
        swap.wait_send()
        for cp in copies:
            cp.wait_send()

    vmem = pl.BlockSpec(memory_space=pltpu.VMEM)
    return pl.pallas_call(
        body, name="grad_reduce",
        out_shape=(_sds((2, HALF_ROWS, LANES), F32), _sds((N_DEV, 8, SV_COLS), F32)),
        in_specs=[vmem, vmem], out_specs=(vmem, vmem),
        scratch_shapes=[
            pltpu.VMEM((N_DEV, HALF_ROWS, LANES), BF16),
            pltpu.SemaphoreType.DMA((N_DEV - 1,)), pltpu.SemaphoreType.DMA((N_DEV - 1,)),
            pltpu.SemaphoreType.DMA((N_DEV - 1,)), pltpu.SemaphoreType.DMA((N_DEV - 1,)),
            pltpu.SemaphoreType.DMA, pltpu.SemaphoreType.DMA, pltpu.SemaphoreType.DMA,
        ],
        compiler_params=_params(),
    )(gpack, sv)


def _inproj_call(x, shift, scale, g1, w_int, w_q, w_kv, qg, kvg, cos256, sin256):
    s_len = x.shape[0]
    tm = min(ROW_TILE, s_len)

    def body(x_ref, sh_ref, sc_ref, g1_ref, w_ref, wq_ref, wkv_ref, qg_ref, kvg_ref, cos_ref, sin_ref,
             h_ref, sq_ref, sk_ref, sv_ref, sz_ref, cq_ref, ckv_ref, mz_ref, ga_ref, gb_ref, kpt_ref,
             qn_ref, qp_ref, kn_ref, vv_ref):
        xt = x_ref[...]
        r = lax.rsqrt(jnp.mean(xt * xt, axis=-1, keepdims=True) + EPS)
        h = (xt * r * g1_ref[...]) * (1.0 + sc_ref[...]) + sh_ref[...]
        hb = h.astype(BF16)
        h_ref[...] = hb

        def seg(a, b):
            return _dot(hb, w_ref[:, a:b])

        sq_ref[...] = seg(O_SQ, O_SK).astype(BF16)
        sk_ref[...] = seg(O_SK, O_SV).astype(BF16)
        sv_ref[...] = seg(O_SV, O_SZ).astype(BF16)
        sz_ref[...] = seg(O_SZ, O_CQ)
        mz_ref[...] = seg(O_MZ, O_GA)
        ga_ref[...] = seg(O_GA, O_GB)
        gb_ref[...] = seg(O_GB, O_KR)
        cos = cos_ref[...]
        sin = sin_ref[...]
        kr = seg(O_KR, O_END)
        kpt_ref[...] = (kr[:, :128] * cos[:, :128] + kr[:, 128:] * sin[:, :128]).astype(BF16)

        cq = seg(O_CQ, O_CKV)
        cq_ref[...] = cq
        rq = lax.rsqrt(jnp.mean(cq * cq, axis=-1, keepdims=True) + EPS)
        cqn = (cq * rq * qg_ref[...]).astype(BF16)
        qa = _dot(cqn, wq_ref[...])
        qn_ref[...] = qa[:, :512].astype(BF16)
        qp_ref[...] = (qa[:, 512:768] * cos + qa[:, 768:] * sin).astype(BF16)

        ckv = seg(O_CKV, O_MZ)
        ckv_ref[...] = ckv
        rk = lax.rsqrt(jnp.mean(ckv * ckv, axis=-1, keepdims=True) + EPS)
        ckvn = (ckv * rk * kvg_ref[...]).astype(BF16)
        kva = _dot(ckvn, wkv_ref[...])
        kn_ref[...] = kva[:, :512].astype(BF16)
        vv_ref[...] = kva[:, 512:].astype(BF16)

    outs = [
        (D_MODEL, BF16), (512, BF16), (512, BF16), (512, BF16), (512, F32), (Q_RANK, F32), (KV_RANK, F32),
        (512, F32), (D_MODEL, F32), (D_MODEL, F32), (128, BF16), (512, BF16), (256, BF16), (512, BF16), (512, BF16),
    ]
    return pl.pallas_call(
        body, name="inproj", grid=(s_len // tm,),
        out_shape=tuple(_sds((s_len, n), dt) for n, dt in outs),
        in_specs=[_rows(tm, D_MODEL), _whole((1, D_MODEL)), _whole((1, D_MODEL)), _whole((1, D_MODEL)),
                  _whole((D_MODEL, W_INT)), _whole((Q_RANK, 1024)), _whole((KV_RANK, 1024)),
                  _whole((1, Q_RANK)), _whole((1, KV_RANK)), _rows(tm, 256), _rows(tm, 256)],
        out_specs=tuple(_rows(tm, n) for n, _ in outs),
        compiler_params=_params(("parallel",)),
    )(x, shift, scale, g1, w_int, w_q, w_kv, qg, kvg, cos256, sin256)


def _softplus(z):
    return jnp.maximum(z, 0.0) + jnp.log(1.0 + jnp.exp(-jnp.abs(z)))


def _split_bf16(a):
    hi = a.astype(BF16)
    lo = (a - hi.astype(F32)).astype(BF16)
    return hi, lo


def _sb_fwd_call(q, k, v):
    s_len = q.shape[0]
    t = min(ATT_TILE, s_len)
    nq = s_len // t

    def body(q_ref, k_ref, v_ref, o_ref, lt_ref):
        i = pl.program_id(1)
        q2 = q_ref[...]
        lane = lax.broadcasted_iota(jnp.int32, (1, 128), 1)
        row = lax.broadcasted_iota(jnp.int32, (t, t), 0)
        col = lax.broadcasted_iota(jnp.int32, (t, t), 1)
        later = (row > col).astype(BF16)
        later2 = jnp.concatenate([later, later], axis=0)
        valid = col < row
        hms = [(lane // 64) == hh for hh in range(2)]
        qms = [jnp.where(hm, q2, jnp.zeros_like(q2)) for hm in hms]

        def block(j, carry, diag):
            runs, acc = list(carry[:2]), carry[2]
            off = pl.multiple_of(j * t, t)
            kb = k_ref[pl.ds(off, t), :]
            vb = v_ref[pl.ds(off, t), :]
            ws = []
            for hh in range(2):
                z = _dot_nt(qms[hh], kb)
                lg = -_softplus(z)
                lm = jnp.where(valid, lg, 0.0) if diag else lg
                hi, lo = _split_bf16(lm)
                suf = _dot(jnp.concatenate([hi, lo], axis=1), later2)
                w = jnp.exp(z + lg + suf + runs[hh])
                if diag:
                    w = jnp.where(valid, w, 0.0)
                ws.append(w.astype(BF16))
                runs[hh] = runs[hh] + jnp.sum(lm, axis=1, keepdims=True)
            vstack = jnp.concatenate([jnp.where(hm, vb, jnp.zeros_like(vb)) for hm in hms], axis=0)
            acc = acc + _dot(jnp.concatenate(ws, axis=1), vstack)
            return runs[0], runs[1], acc

        zero = jnp.zeros((t, 1), F32)
        carry = block(i, (zero, zero, jnp.zeros((t, 128), F32)), True)
        carry = lax.fori_loop(1, i + 1, lambda jj, cr: block(i - jj, cr, False), carry)
        lt_ref[0, :, 0:1] = carry[0]
        lt_ref[0, :, 1:2] = carry[1]
        o_ref[...] = carry[2]

    return pl.pallas_call(
        body, name="sb_fwd", grid=(4, nq),
        out_shape=(_sds((s_len, SB_WIDTH), F32), _sds((4, s_len, 2), F32)),
        in_specs=[pl.BlockSpec((t, 128), lambda p, i: (i, p)),
                  pl.BlockSpec((s_len, 128), lambda p, i: (0, p)),
                  pl.BlockSpec((s_len, 128), lambda p, i: (0, p))],
        out_specs=(pl.BlockSpec((t, 128), lambda p, i: (i, p)),
                   pl.BlockSpec((1, t, 2), lambda p, i: (p, i, 0))),
        compiler_params=_params(("parallel", "parallel")),
    )(q, k, v)


def _sb_bwd_call(q, k, v, do, lt):
    s_len = q.shape[0]
    t = min(ATT_TILE, s_len)
    nq = s_len // t

    def body(q_ref, k_ref, v_ref, do_ref, lt_ref, dq_ref, dk_ref, dv_ref):
        i = pl.program_id(1)

        @pl.when(i == 0)
        def _():
            dk_ref[...] = jnp.zeros_like(dk_ref)
            dv_ref[...] = jnp.zeros_like(dv_ref)

        q2 = q_ref[...]
        do2 = do_ref[...].astype(BF16)
        lane = lax.broadcasted_iota(jnp.int32, (1, 128), 1)
        row = lax.broadcasted_iota(jnp.int32, (t, t), 0)
        col = lax.broadcasted_iota(jnp.int32, (t, t), 1)
        earlier = (row < col).astype(BF16)
        earlier2 = jnp.concatenate([earlier, earlier], axis=0)
        valid = col < row
        hms = [(lane // 64) == hh for hh in range(2)]
        qms = [jnp.where(hm, q2, jnp.zeros_like(q2)) for hm in hms]
        doms = [jnp.where(hm, do2, jnp.zeros_like(do2)) for hm in hms]
        ltots = [lt_ref[0, :, hh:hh + 1] for hh in range(2)]
        qstack = jnp.concatenate(qms, axis=0)
        dostack = jnp.concatenate(doms, axis=0)

        def block(j, carry, diag):
            lpre, ppre, dq = list(carry[0:2]), list(carry[2:4]), carry[4]
            off = pl.multiple_of(j * t, t)
            kb = k_ref[pl.ds(off, t), :]
            vb = v_ref[pl.ds(off, t), :]
            dzs, avs = [], []
            for hh in range(2):
                z = _dot_nt(qms[hh], kb)
                sp = _softplus(z)
                lg = -sp
                lm = jnp.where(valid, lg, 0.0) if diag else lg
                hi, lo = _split_bf16(lm)
                before = _dot(jnp.concatenate([hi, lo], axis=1), earlier2)
                between = ltots[hh] - (lpre[hh] + before + lm)
                a = jnp.exp(z + lg + between)
                if diag:
                    a = jnp.where(valid, a, 0.0)
                p = a * _dot_nt(doms[hh], vb)
                phi, plo = _split_bf16(p)
                pbefore = ppre[hh] + _dot(jnp.concatenate([phi, plo], axis=1), earlier2)
                sig = jnp.exp(z - sp)
                dz = p - sig * (p + pbefore)
                if diag:
                    dz = jnp.where(valid, dz, 0.0)
                dzs.append(dz.astype(BF16))
                avs.append(a.astype(BF16))
                lpre[hh] = lpre[hh] + jnp.sum(lm, axis=1, keepdims=True)
                ppre[hh] = ppre[hh] + jnp.sum(p, axis=1, keepdims=True)
            kstack = jnp.concatenate([jnp.where(hm, kb, jnp.zeros_like(kb)) for hm in hms], axis=0)
            dq = dq + _dot(jnp.concatenate(dzs, axis=1), kstack)
            dk_ref[pl.ds(off, t), :] += _dot_tn(jnp.concatenate(dzs, axis=0), qstack)
            dv_ref[pl.ds(off, t), :] += _dot_tn(jnp.concatenate(avs, axis=0), dostack)
            return lpre[0], lpre[1], ppre[0], ppre[1], dq

        zero = jnp.zeros((t, 1), F32)
        carry = lax.fori_loop(0, i, lambda j, cr: block(j, cr, False),
                              (zero, zero, zero, zero, jnp.zeros((t, 128), F32)))
        carry = block(i, carry, True)
        dq_ref[...] = carry[4].astype(BF16)

    return pl.pallas_call(
        body, name="sb_bwd", grid=(4, nq),
        out_shape=(_sds((s_len, SB_WIDTH), BF16), _sds((s_len, SB_WIDTH), F32), _sds((s_len, SB_WIDTH), F32)),
        in_specs=[pl.BlockSpec((t, 128), lambda p, i: (i, p)),
                  pl.BlockSpec((s_len, 128), lambda p, i: (0, p)),
                  pl.BlockSpec((s_len, 128), lambda p, i: (0, p)),
                  pl.BlockSpec((t, 128), lambda p, i: (i, p)),
                  pl.BlockSpec((1, t, 2), lambda p, i: (p, i, 0))],
        out_specs=(pl.BlockSpec((t, 128), lambda p, i: (i, p)),
                   pl.BlockSpec((s_len, 128), lambda p, i: (0, p)),
                   pl.BlockSpec((s_len, 128), lambda p, i: (0, p))),
        compiler_params=_params(("parallel", "arbitrary")),
    )(q, k, v, do, lt)


def _mla_fwd_call(qn, qp, kn, kpt, v):
    s_len = qn.shape[0]
    t = min(ATT_TILE, s_len)
    nq = s_len // t

    def body(qn_ref, qp_ref, kn_ref, kpt_ref, v_ref, o_ref, lse_ref):
        i = pl.program_id(1)
        qn2 = qn_ref[...]
        qp2 = qp_ref[...]
        lane256 = lax.broadcasted_iota(jnp.int32, (1, 256), 1)
        lane128 = lax.broadcasted_iota(jnp.int32, (1, 128), 1)
        row = lax.broadcasted_iota(jnp.int32, (t, t), 0)
        col = lax.broadcasted_iota(jnp.int32, (t, t), 1)
        valid = col <= row
        m64s = [(lane256 // 64) == hh for hh in range(4)]
        m32s = [(lane128 // 32) == hh for hh in range(4)]
        qcs = [jnp.concatenate([jnp.where(m64s[hh], qn2, jnp.zeros_like(qn2)),
                                jnp.where(m32s[hh], qp2, jnp.zeros_like(qp2))], axis=1) for hh in range(4)]

        def by_head(vals):
            return jnp.where(m64s[0], vals[0], jnp.where(m64s[1], vals[1], jnp.where(m64s[2], vals[2], vals[3])))

        def block(j, carry, diag):
            ms, ls, acc = list(carry[0:4]), list(carry[4:8]), carry[8]
            off = pl.multiple_of(j * t, t)
            kc = jnp.concatenate([kn_ref[pl.ds(off, t), :], kpt_ref[pl.ds(off, t), :]], axis=1)
            vb = v_ref[pl.ds(off, t), :]
            ps, alphas = [], []
            for hh in range(4):
                s = _dot_nt(qcs[hh], kc) * MLA_SCALE
                if diag:
                    s = jnp.where(valid, s, -1e30)
                mn = jnp.maximum(ms[hh], jnp.max(s, axis=1, keepdims=True))
                p = jnp.exp(s - mn)
                alpha = jnp.exp(ms[hh] - mn)
                ls[hh] = alpha * ls[hh] + jnp.sum(p, axis=1, keepdims=True)
                ms[hh] = mn
                ps.append(p.astype(BF16))
                alphas.append(alpha)
            vstack = jnp.concatenate([jnp.where(m64, vb, jnp.zeros_like(vb)) for m64 in m64s], axis=0)
            acc = by_head(alphas) * acc + _dot(jnp.concatenate(ps, axis=1), vstack)
            return (*ms, *ls, acc)

        neg = jnp.full((t, 1), -1e30, F32)
        zero = jnp.zeros((t, 1), F32)
        carry = lax.fori_loop(0, i, lambda j, cr: block(j, cr, False),
                              (neg, neg, neg, neg, zero, zero, zero, zero, jnp.zeros((t, 256), F32)))
        carry = block(i, carry, True)
        o_ref[...] = carry[8] / by_head(list(carry[4:8]))
        for hh in range(4):
            lse_ref[0, :, hh:hh + 1] = carry[hh] + jnp.log(carry[4 + hh])

    return pl.pallas_call(
        body, name="mla_fwd", grid=(2, nq),
        out_shape=(_sds((s_len, MLA_WIDTH), F32), _sds((2, s_len, 4), F32)),
        in_specs=[pl.BlockSpec((t, 256), lambda g, i: (i, g)),
                  pl.BlockSpec((t, 128), lambda g, i: (i, g)),
                  pl.BlockSpec((s_len, 256), lambda g, i: (0, g)),
                  pl.BlockSpec((s_len, 128), lambda g, i: (0, 0)),
                  pl.BlockSpec((s_len, 256), lambda g, i: (0, g))],
        out_specs=(pl.BlockSpec((t, 256), lambda g, i: (i, g)),
                   pl.BlockSpec((1, t, 4), lambda g, i: (g, i, 0))),
        compiler_params=_params(("parallel", "parallel")),
    )(qn, qp, kn, kpt, v)


def _mla_bwd_call(qn, qp, kn, kpt, v, o, do, lse):
    s_len = qn.shape[0]
    t = min(ATT_TILE, s_len)
    nq = s_len // t

    def body(qn_ref, qp_ref, kn_ref, kpt_ref, v_ref, o_ref, do_ref, lse_ref,
             dqn_ref, dqp_ref, dkn_ref, dkpt_ref, dv_ref):
        g = pl.program_id(0)
        i = pl.program_id(1)

        @pl.when(i == 0)
        def _():
            dkn_ref[...] = jnp.zeros_like(dkn_ref)
            dv_ref[...] = jnp.zeros_like(dv_ref)

        @pl.when((i == 0) & (g == 0))
        def _():
            dkpt_ref[...] = jnp.zeros_like(dkpt_ref)

        qn2 = qn_ref[...]
        qp2 = qp_ref[...]
        of = o_ref[...]
        dof = do_ref[...]
        dob = dof.astype(BF16)
        prod = dof * of
        lane256 = lax.broadcasted_iota(jnp.int32, (1, 256), 1)
        lane128 = lax.broadcasted_iota(jnp.int32, (1, 128), 1)
        row = lax.broadcasted_iota(jnp.int32, (t, t), 0)
        col = lax.broadcasted_iota(jnp.int32, (t, t), 1)
        valid = col <= row
        m64s = [(lane256 // 64) == hh for hh in range(4)]
        m32s = [(lane128 // 32) == hh for hh in range(4)]
        qcs = [jnp.concatenate([jnp.where(m64s[hh], qn2, jnp.zeros_like(qn2)),
                                jnp.where(m32s[hh], qp2, jnp.zeros_like(qp2))], axis=1) for hh in range(4)]
        doms = [jnp.where(m64, dob, jnp.zeros_like(dob)) for m64 in m64s]
        dsums = [jnp.sum(jnp.where(m64, prod, 0.0), axis=1, keepdims=True) * MLA_SCALE for m64 in m64s]
        lses = [lse_ref[0, :, hh:hh + 1] for hh in range(4)]
        qstack = jnp.concatenate(qcs, axis=0)
        dostack = jnp.concatenate(doms, axis=0)

        def block(j, dqc, diag):
            off = pl.multiple_of(j * t, t)
            knb = kn_ref[pl.ds(off, t), :]
            kpb = kpt_ref[pl.ds(off, t), :]
            vb = v_ref[pl.ds(off, t), :]
            kc = jnp.concatenate([knb, kpb], axis=1)
            dss, pbs = [], []
            for hh in range(4):
                s = _dot_nt(qcs[hh], kc) * MLA_SCALE
                if diag:
                    s = jnp.where(valid, s, -1e30)
                p = jnp.exp(s - lses[hh])
                ds = p * (_dot_nt(doms[hh], vb) * MLA_SCALE - dsums[hh])
                dss.append(ds.astype(BF16))
                pbs.append(p.astype(BF16))
            kstack = jnp.concatenate(
                [jnp.concatenate([jnp.where(m64s[hh], knb, jnp.zeros_like(knb)),
                                  jnp.where(m32s[hh], kpb, jnp.zeros_like(kpb))], axis=1) for hh in range(4)], axis=0)
            dqc = dqc + _dot(jnp.concatenate(dss, axis=1), kstack)
            dkc = _dot_tn(jnp.concatenate(dss, axis=0), qstack)
            dkn_ref[pl.ds(off, t), :] += dkc[:, :256]
            dkpt_ref[pl.ds(off, t), :] += dkc[:, 256:]
            dv_ref[pl.ds(off, t), :] += _dot_tn(jnp.concatenate(pbs, axis=0), dostack)
            return dqc

        dqc = lax.fori_loop(0, i, lambda j, cr: block(j, cr, False), jnp.zeros((t, 384), F32))
        dqc = block(i, dqc, True)
        dqn_ref[...] = dqc[:, :256].astype(BF16)
        dqp_ref[...] = dqc[:, 256:].astype(BF16)

    return pl.pallas_call(
        body, name="mla_bwd", grid=(2, nq),
        out_shape=(_sds((s_len, 512), BF16), _sds((s_len, 256), BF16), _sds((s_len, 512), F32),
                   _sds((s_len, 128), F32), _sds((s_len, 512), F32)),
        in_specs=[pl.BlockSpec((t, 256), lambda g, i: (i, g)),
                  pl.BlockSpec((t, 128), lambda g, i: (i, g)),
                  pl.BlockSpec((s_len, 256), lambda g, i: (0, g)),
                  pl.BlockSpec((s_len, 128), lambda g, i: (0, 0)),
                  pl.BlockSpec((s_len, 256), lambda g, i: (0, g)),
                  pl.BlockSpec((t, 256), lambda g, i: (i, g)),
                  pl.BlockSpec((t, 256), lambda g, i: (i, g)),
                  pl.BlockSpec((1, t, 4), lambda g, i: (g, i, 0))],
        out_specs=(pl.BlockSpec((t, 256), lambda g, i: (i, g)),
                   pl.BlockSpec((t, 128), lambda g, i: (i, g)),
                   pl.BlockSpec((s_len, 256), lambda g, i: (0, g)),
                   pl.BlockSpec((s_len, 128), lambda g, i: (0, 0)),
                   pl.BlockSpec((s_len, 256), lambda g, i: (0, g))),
        compiler_params=_params(("arbitrary", "arbitrary")),
    )(qn, qp, kn, kpt, v, o, do, lse)


def _post_call(x, tgt, oa, ob, sz, mz, ga, gb, gate, gf, wa, wb, wo, wat, wbt, wot):
    s_len = x.shape[0]
    tm = min(ROW_TILE, s_len)

    def body(x_ref, t_ref, oa_ref, ob_ref, sz_ref, mz_ref, ga_ref, gb_ref, gate_ref, gf_ref,
             wa_ref, wb_ref, wo_ref, wat_ref, wbt_ref, wot_ref,
             dx2_ref, doa_ref, dob_ref, dsz_ref, dmz_ref, dga_ref, dgb_ref,
             dwo_ref, dwa_ref, dwb_ref, dgf_ref, dgate_ref, loss_ref):
        @pl.when(pl.program_id(0) == 0)
        def _():
            dwo_ref[...] = jnp.zeros_like(dwo_ref)
            dwa_ref[...] = jnp.zeros_like(dwa_ref)
            dwb_ref[...] = jnp.zeros_like(dwb_ref)
            dgf_ref[...] = jnp.zeros_like(dgf_ref)
            dgate_ref[...] = jnp.zeros_like(dgate_ref)
            loss_ref[...] = jnp.zeros_like(loss_ref)

        gate = gate_ref[...]
        gf = gf_ref[...]
        oa = oa_ref[...]
        ob = ob_ref[...]
        sz = sz_ref[...]
        mz = mz_ref[...]
        sa = _sigmoid(sz)
        sb = _sigmoid(mz)
        silu_a = sz * sa
        silu_b = mz * sb
        ua = (oa * silu_a).astype(BF16)
        ub = (ob * silu_b).astype(BF16)
        ya = _dot(ua, wa_ref[...])
        yb = _dot(ub, wb_ref[...])
        sga = _sigmoid(ga_ref[...])
        sgb = _sigmoid(gb_ref[...])
        merged = (sga * ya + sgb * yb).astype(BF16)
        out = _dot(merged, wo_ref[...])
        x2 = x_ref[...] + gate * out
        r2 = lax.rsqrt(jnp.mean(x2 * x2, axis=-1, keepdims=True) + EPS)
        xhat = x2 * r2
        err = xhat * gf - t_ref[...]
        loss_ref[...] += 0.5 * jnp.sum(jnp.sum(err * err, axis=1, keepdims=True), axis=0, keepdims=True) / D_MODEL
        dy = err * (1.0 / D_MODEL)
        dgf_ref[...] += jnp.sum(dy * xhat, axis=0, keepdims=True)
        dxhat = dy * gf
        dx2 = r2 * (dxhat - xhat * jnp.mean(dxhat * xhat, axis=-1, keepdims=True))
        dx2_ref[...] = dx2
        dgate_ref[...] += jnp.sum(dx2 * out, axis=0, keepdims=True)
        dout = (dx2 * gate).astype(BF16)
        dmerged = _dot(dout, wot_ref[...])
        dwo_ref[...] += _dot_tn(merged, dout)
        dya = dmerged * sga
        dyb = dmerged * sgb
        dga_ref[...] = (dya * ya * (1.0 - sga)).astype(BF16)
        dgb_ref[...] = (dyb * yb * (1.0 - sgb)).astype(BF16)
        dyab = dya.astype(BF16)
        dybb = dyb.astype(BF16)
        dua = _dot(dyab, wat_ref[...])
        dub = _dot(dybb, wbt_ref[...])
        dwa_ref[...] += _dot_tn(ua, dyab)
        dwb_ref[...] += _dot_tn(ub, dybb)
        doa_ref[...] = dua * silu_a
        dob_ref[...] = dub * silu_b
        dsz_ref[...] = (dua * oa * (sa * (1.0 + sz * (1.0 - sa)))).astype(BF16)
        dmz_ref[...] = (dub * ob * (sb * (1.0 + mz * (1.0 - sb)))).astype(BF16)

    return pl.pallas_call(
        body, name="post", grid=(s_len // tm,),
        out_shape=(_sds((s_len, D_MODEL), F32), _sds((s_len, 512), F32), _sds((s_len, 512), F32),
                   _sds((s_len, 512), BF16), _sds((s_len, 512), BF16),
                   _sds((s_len, D_MODEL), BF16), _sds((s_len, D_MODEL), BF16),
                   _sds((D_MODEL, D_MODEL), F32), _sds((512, D_MODEL), F32), _sds((512, D_MODEL), F32),
                   _sds((1, D_MODEL), F32), _sds((1, D_MODEL), F32), _sds((1, 128), F32)),
        in_specs=[_rows(tm, D_MODEL), _rows(tm, D_MODEL), _rows(tm, 512), _rows(tm, 512), _rows(tm, 512),
                  _rows(tm, 512), _rows(tm, D_MODEL), _rows(tm, D_MODEL), _whole((1, D_MODEL)), _whole((1, D_MODEL)),
                  _whole((512, D_MODEL)), _whole((512, D_MODEL)), _whole((D_MODEL, D_MODEL)),
                  _whole((D_MODEL, 512)), _whole((D_MODEL, 512)), _whole((D_MODEL, D_MODEL))],
        out_specs=(_rows(tm, D_MODEL), _rows(tm, 512), _rows(tm, 512), _rows(tm, 512), _rows(tm, 512),
                   _rows(tm, D_MODEL), _rows(tm, D_MODEL),
                   _whole((D_MODEL, D_MODEL)), _whole((512, D_MODEL)), _whole((512, D_MODEL)),
                   _whole((1, D_MODEL)), _whole((1, D_MODEL)), _whole((1, 128))),
        compiler_params=_params(("arbitrary",)),
    )(x, tgt, oa, ob, sz, mz, ga, gb, gate, gf, wa, wb, wo, wat, wbt, wot)


def _bwdprep_call(dsq, dsk, dsv, dsz, dqn, dqp, dkn, dvv, dkpt, dmz, dga, dgb, cq, ckv, cos256, sin256,
                  qg, kvg, wqt, wkvt):
    s_len = cq.shape[0]
    tm = min(ROW_TILE, s_len)

    def body(dsq_ref, dsk_ref, dsv_ref, dsz_ref, dqn_ref, dqp_ref, dkn_ref, dvv_ref, dkpt_ref, dmz_ref,
             dga_ref, dgb_ref, cq_ref, ckv_ref, cos_ref, sin_ref, qg_ref, kvg_ref, wqt_ref, wkvt_ref,
             dp_ref, dwq_ref, dwkv_ref, dqg_ref, dkvg_ref):
        @pl.when(pl.program_id(0) == 0)
        def _():
            dwq_ref[...] = jnp.zeros_like(dwq_ref)
            dwkv_ref[...] = jnp.zeros_like(dwkv_ref)
            dqg_ref[...] = jnp.zeros_like(dqg_ref)
            dkvg_ref[...] = jnp.zeros_like(dkvg_ref)

        cos = cos_ref[...]
        sin = sin_ref[...]
        dp_ref[:, O_SQ:O_SK] = dsq_ref[...]
        dp_ref[:, O_SK:O_SV] = dsk_ref[...].astype(BF16)
        dp_ref[:, O_SV:O_SZ] = dsv_ref[...].astype(BF16)
        dp_ref[:, O_SZ:O_CQ] = dsz_ref[...]
        dp_ref[:, O_MZ:O_GA] = dmz_ref[...]
        dp_ref[:, O_GA:O_GB] = dga_ref[...]
        dp_ref[:, O_GB:O_KR] = dgb_ref[...]
        dkp = dkpt_ref[...]
        dp_ref[:, O_KR:O_KR + 128] = (dkp * cos[:, :128]).astype(BF16)
        dp_ref[:, O_KR + 128:O_END] = (dkp * sin[:, :128]).astype(BF16)
        dp_ref[:, O_END:W_INT] = jnp.zeros((tm, W_INT - O_END), BF16)

        cq = cq_ref[...]
        rq = lax.rsqrt(jnp.mean(cq * cq, axis=-1, keepdims=True) + EPS)
        cqh = cq * rq
        qg = qg_ref[...]
        cqn = (cqh * qg).astype(BF16)
        dqp = dqp_ref[...].astype(F32)
        dqa = jnp.concatenate([dqn_ref[...], (dqp * cos).astype(BF16), (dqp * sin).astype(BF16)], axis=1)
        dcqn = _dot(dqa, wqt_ref[...])
        dwq_ref[...] += _dot_tn(cqn, dqa)
        dqg_ref[...] += jnp.sum(dcqn * cqh, axis=0, keepdims=True)
        dh = dcqn * qg
        dcq = rq * (dh - cqh * jnp.mean(dh * cqh, axis=-1, keepdims=True))
        dp_ref[:, O_CQ:O_CKV] = dcq.astype(BF16)

        ckv = ckv_ref[...]
        rk = lax.rsqrt(jnp.mean(ckv * ckv, axis=-1, keepdims=True) + EPS)
        ckh = ckv * rk
        kvg = kvg_ref[...]
        ckvn = (ckh * kvg).astype(BF16)
        dkva = jnp.concatenate([dkn_ref[...].astype(BF16), dvv_ref[...].astype(BF16)], axis=1)
        dckvn = _dot(dkva, wkvt_ref[...])
        dwkv_ref[...] += _dot_tn(ckvn, dkva)
        dkvg_ref[...] += jnp.sum(dckvn * ckh, axis=0, keepdims=True)
        dh2 = dckvn * kvg
        dckv = rk * (dh2 - ckh * jnp.mean(dh2 * ckh, axis=-1, keepdims=True))
        dp_ref[:, O_CKV:O_MZ] = dckv.astype(BF16)

    return pl.pallas_call(
        body, name="bwdprep", grid=(s_len // tm,),
        out_shape=(_sds((s_len, W_INT), BF16), _sds((Q_RANK, 1024), F32), _sds((KV_RANK, 1024), F32),
                   _sds((1, Q_RANK), F32), _sds((1, KV_RANK), F32)),
        in_specs=[_rows(tm, 512), _rows(tm, 512), _rows(tm, 512), _rows(tm, 512), _rows(tm, 512), _rows(tm, 256),
                  _rows(tm, 512), _rows(tm, 512), _rows(tm, 128), _rows(tm, 512), _rows(tm, D_MODEL),
                  _rows(tm, D_MODEL), _rows(tm, Q_RANK), _rows(tm, KV_RANK), _rows(tm, 256), _rows(tm, 256),
                  _whole((1, Q_RANK)), _whole((1, KV_RANK)), _whole((1024, Q_RANK)), _whole((1024, KV_RANK))],
        out_specs=(_rows(tm, W_INT), _whole((Q_RANK, 1024)), _whole((KV_RANK, 1024)),
                   _whole((1, Q_RANK)), _whole((1, KV_RANK))),
        compiler_params=_params(("arbitrary",)),
    )(dsq, dsk, dsv, dsz, dqn, dqp, dkn, dvv, dkpt, dmz, dga, dgb, cq, ckv, cos256, sin256, qg, kvg, wqt, wkvt)


def _dh_call(dproj, w_int_t, x, dx2, scale, g1):
    s_len = x.shape[0]
    tm = min(ROW_TILE, s_len)

    def body(dp_ref, wt_ref, x_ref, dx2_ref, sc_ref, g1_ref, gx_ref, dsh_ref, dsc_ref, dg1_ref):
        @pl.when(pl.program_id(0) == 0)
        def _():
            dsh_ref[...] = jnp.zeros_like(dsh_ref)
            dsc_ref[...] = jnp.zeros_like(dsc_ref)
            dg1_ref[...] = jnp.zeros_like(dg1_ref)

        dh = _dot(dp_ref[...], wt_ref[...])
        xt = x_ref[...]
        r = lax.rsqrt(jnp.mean(xt * xt, axis=-1, keepdims=True) + EPS)
        xh = xt * r
        g1 = g1_ref[...]
        xg = xh * g1
        dsh_ref[...] += jnp.sum(dh, axis=0, keepdims=True)
        dsc_ref[...] += jnp.sum(dh * xg, axis=0, keepdims=True)
        dxg = dh * (1.0 + sc_ref[...])
        dg1_ref[...] += jnp.sum(dxg * xh, axis=0, keepdims=True)
        dxh = dxg * g1
        gx_ref[...] = dx2_ref[...] + r * (dxh - xh * jnp.mean(dxh * xh, axis=-1, keepdims=True))

    return pl.pallas_call(
        body, name="dh", grid=(s_len // tm,),
        out_shape=(_sds((s_len, D_MODEL), F32), _sds((1, D_MODEL), F32), _sds((1, D_MODEL), F32),
                   _sds((1, D_MODEL), F32)),
        in_specs=[_rows(tm, W_INT), _whole((W_INT, D_MODEL)), _rows(tm, D_MODEL), _rows(tm, D_MODEL),
                  _whole((1, D_MODEL)), _whole((1, D_MODEL))],
        out_specs=(_rows(tm, D_MODEL), _whole((1, D_MODEL)), _whole((1, D_MODEL)), _whole((1, D_MODEL))),
        compiler_params=_params(("arbitrary",)),
    )(dproj, w_int_t, x, dx2, scale, g1)


def _dwin_call(h, dproj):
    s_len = h.shape[0]
    tm = min(2 * ROW_TILE, s_len)
    nc = 4
    chunk = W_INT // nc

    def body(h_ref, dp_ref, dw_ref):
        @pl.when(pl.program_id(1) == 0)
        def _():
            dw_ref[...] = jnp.zeros_like(dw_ref)

        dw_ref[...] += _dot_tn(h_ref[...], dp_ref[...])

    return pl.pallas_call(
        body, name="dwin", grid=(nc, s_len // tm),
        out_shape=_sds((D_MODEL, nc * chunk), F32),
        in_specs=[pl.BlockSpec((tm, D_MODEL), lambda c, i: (i, 0)),
                  pl.BlockSpec((tm, chunk), lambda c, i: (i, c))],
        out_specs=pl.BlockSpec((D_MODEL, chunk), lambda c, i: (0, c)),
        compiler_params=_params(("parallel", "arbitrary")),
    )(h, dproj)


def _small_call(svg, ct, dmod_sh):
    def body(sv_ref, ct_ref, dm_ref, tot_ref, gwada_ref):
        acc = sv_ref[0:1, :]
        for d in range(1, N_DEV):
            acc = acc + sv_ref[d:d + 1, :]
        tot_ref[...] = acc
        gwada_ref[...] = lax.dot_general(ct_ref[...], dm_ref[...], (((1,), (0,)), ((), ())),
                                         precision=lax.Precision.HIGHEST, preferred_element_type=F32)

    vmem = pl.BlockSpec(memory_space=pltpu.VMEM)
    return pl.pallas_call(
        body, name="small_grads",
        out_shape=(_sds((1, 8 * SV_COLS), F32), _sds((D_MODEL, 768), F32)),
        in_specs=[vmem, vmem, vmem], out_specs=(vmem, vmem),
        compiler_params=_params(),
    )(svg, ct, dmod_sh)


def _adamw_tile_rows(rows, cols):
    budget = 1 << 20
    if rows * cols * 4 <= budget or rows % 8:
        return rows
    best = 8
    for tr in range(8, rows + 1, 8):
        if rows % tr == 0 and tr * cols * 4 <= budget:
            best = tr
    return best


def _adamw_call(name, w, g, m, v):
    rows, cols = w.shape
    tr = _adamw_tile_rows(rows, cols)

    def body(w_ref, g_ref, m_ref, v_ref, d_ref, nm_ref, nv_ref):
        gg = g_ref[...]
        m2 = ADAM_B1 * m_ref[...] + (1.0 - ADAM_B1) * gg
        v2 = ADAM_B2 * v_ref[...] + (1.0 - ADAM_B2) * (gg * gg)
        m_hat = m2 / (1.0 - ADAM_B1 ** ADAM_STEP)
        v_hat = v2 / (1.0 - ADAM_B2 ** ADAM_STEP)
        d_ref[...] = -ADAM_LR * (m_hat / (jnp.sqrt(v_hat) + ADAM_EPS) + ADAM_WD * w_ref[...])
        nm_ref[...] = m2
        nv_ref[...] = v2

    spec = pl.BlockSpec((tr, cols), lambda i: (i, 0))
    return pl.pallas_call(
        body, name="adamw_" + name, grid=(rows // tr,),
        out_shape=(_sds((rows, cols), F32),) * 3,
        in_specs=[spec] * 4, out_specs=(spec,) * 3,
        compiler_params=_params(("parallel",)),
    )(w, g, m, v)


def _swap_halves(w, group):
    r, n = w.shape
    return w.reshape(r, n // group, 2, group // 2)[:, :, ::-1, :].reshape(r, n)


def _pack_shards(parts):
    return jnp.concatenate([p.reshape(-1, LANES) for p in parts], axis=0)


def _unpack_chip_major(gw):
    offs = [0]
    for r in PACK_ROWS:
        offs.append(offs[-1] + r)

    def cols(i, rows, shard_cols):
        blk = gw[:, offs[i]:offs[i + 1]].reshape(N_CHIPS, rows, shard_cols)
        return blk.transpose(1, 0, 2).reshape(rows, N_CHIPS * shard_cols)

    w_in = cols(0, D_MODEL, 1320)
    w_uq = cols(1, Q_RANK, 192)
    w_ukv = cols(2, KV_RANK, 256)
    w_a = cols(3, 512, 256)
    w_b = cols(4, 512, 256)
    w_out = gw[:, offs[5]:offs[6]].reshape(D_MODEL, D_MODEL)
    return w_in, w_uq, w_ukv, w_a, w_b, w_out


def _internal_weights(w_in, w_uq, w_ukv):
    krot = w_in[:, 2688:2720]
    w_int = jnp.concatenate([
        w_in[:, 0:512] * jnp.asarray(0.125, w_in.dtype), w_in[:, 512:2048],
        w_in[:, 2048:2432], w_in[:, 2432:2688], w_in[:, 2720:3232], w_in[:, 3232:4256], w_in[:, 4256:5280],
        jnp.tile(krot, (1, 4)), jnp.tile(_swap_halves(krot, 32), (1, 4)),
        jnp.zeros((D_MODEL, W_INT - O_END), w_in.dtype)], axis=1)
    uq = w_uq.reshape(Q_RANK, N_HEADS, 96)
    wp = uq[:, :, 64:].reshape(Q_RANK, 256)
    w_q = jnp.concatenate([uq[:, :, :64].reshape(Q_RANK, 512), wp, _swap_halves(wp, 32)], axis=1)
    ukv = w_ukv.reshape(KV_RANK, N_HEADS, 128)
    w_kv = jnp.concatenate([ukv[:, :, :64].reshape(KV_RANK, 512), ukv[:, :, 64:].reshape(KV_RANK, 512)], axis=1)
    return w_int, w_q, w_kv


def _true_weight_grads(dwi, dwq, dwkv):
    dkr = dwi[:, O_KR:O_KR + 128].reshape(D_MODEL, 4, 32).sum(axis=1)
    dkr_sw = dwi[:, O_KR + 128:O_END].reshape(D_MODEL, 4, 32).sum(axis=1)
    dkrot = dkr + _swap_halves(dkr_sw, 32)
    g_in = jnp.concatenate([
        dwi[:, 0:512] * 0.125, dwi[:, 512:2048], dwi[:, O_CQ:O_CKV], dwi[:, O_CKV:O_MZ], dkrot,
        dwi[:, O_MZ:O_GA], dwi[:, O_GA:O_GB], dwi[:, O_GB:O_KR]], axis=1)
    dwp = dwq[:, 512:768] + _swap_halves(dwq[:, 768:1024], 32)
    g_uq = jnp.concatenate([dwq[:, :512].reshape(Q_RANK, N_HEADS, 64), dwp.reshape(Q_RANK, N_HEADS, 32)],
                           axis=2).reshape(Q_RANK, 768)
    g_ukv = jnp.concatenate([dwkv[:, :512].reshape(KV_RANK, N_HEADS, 64), dwkv[:, 512:].reshape(KV_RANK, N_HEADS, 64)],
                            axis=2).reshape(KV_RANK, 1024)
    return g_in, g_uq, g_ukv


def _chip_major(g, shard_cols):
    r = g.shape[0]
    return g.reshape(r, N_CHIPS, shard_cols).transpose(1, 0, 2).reshape(N_CHIPS, -1, LANES)


def kernel(x, c, positions, w_ada, b_ada, norm_gain, w_in, q_norm_gain, w_uq, kv_norm_gain, w_ukv, w_branch_a, w_branch_b, w_out, final_norm_gain, loss_target, m_w_ada, m_b_ada, m_norm_gain, m_w_in, m_q_norm_gain, m_w_uq, m_kv_norm_gain, m_w_ukv, m_w_branch_a, m_w_branch_b, m_w_out, m_final_norm_gain, v_w_ada, v_b_ada, v_norm_gain, v_w_in, v_q_norm_gain, v_w_uq, v_kv_norm_gain, v_w_ukv, v_w_branch_a, v_w_branch_b, v_w_out, v_final_norm_gain):
    ix, iy, ic = lax.axis_index("x"), lax.axis_index("y"), lax.axis_index("c")
    me = 4 * ix + 2 * iy + ic
    chip = 2 * ix + iy
    xs = x[0]
    tgt = loss_target[0]
    s_len = xs.shape[0]

    shards = (w_in[0], w_uq[0], w_ukv[0], w_branch_a[0], w_branch_b[0], w_out[0])
    pack = _pack_shards([s.astype(BF16) for s in shards])
    mg, call, gw = _gather_call(c, w_ada[0], pack.reshape(2, HALF_ROWS, LANES))
    gw = gw.reshape(N_CHIPS, PACK_TOTAL, LANES)
    mod = mg.transpose(1, 0, 2).reshape(N_DEV, 3 * D_MODEL) + b_ada
    mod_me = lax.dynamic_slice_in_dim(mod, me, 1, axis=0)
    shift, scale, gate = mod_me[:, :D_MODEL], mod_me[:, D_MODEL:2 * D_MODEL], mod_me[:, 2 * D_MODEL:]

    f_in, f_uq, f_ukv, f_a, f_b, f_out = _unpack_chip_major(gw)
    w_int, w_q, w_kv = _internal_weights(f_in, f_uq, f_ukv)

    inv_freq = ROPE_BASE ** (-jnp.arange(0, ROPE_DIM, 2, dtype=F32) / ROPE_DIM)
    ang = positions[0].astype(F32)[:, None] * inv_freq
    cs, sn = jnp.cos(ang), jnp.sin(ang)
    cos256 = jnp.tile(jnp.concatenate([cs, cs], axis=1), (1, 8))
    sin256 = jnp.tile(jnp.concatenate([-sn, sn], axis=1), (1, 8))

    (h, sq, sk, sv, sz, cq, ckv, mz, ga, gb, kpt, qn, qp, kn, vv) = _inproj_call(
        xs, shift, scale, norm_gain, w_int, w_q, w_kv, q_norm_gain, kv_norm_gain, cos256, sin256)
    oa, lt = _sb_fwd_call(sq, sk, sv)
    ob, lse = _mla_fwd_call(qn, qp, kn, kpt, vv)

    gf = final_norm_gain.reshape(1, D_MODEL)
    (dx2, doa, dob, dsz, dmz, dga, dgb, dwo, dwa, dwb, dgf, dgate, loss_p) = _post_call(
        xs, tgt, oa, ob, sz, mz, ga, gb, gate, gf, f_a, f_b, f_out, f_a.T, f_b.T, f_out.T)

    dsq, dsk, dsv = _sb_bwd_call(sq, sk, sv, doa, lt)
    dqn, dqp, dkn, dkpt, dvv = _mla_bwd_call(qn, qp, kn, kpt, vv, ob, dob, lse)

    dproj, dwq, dwkv, dqg, dkvg = _bwdprep_call(
        dsq, dsk, dsv, dsz, dqn, dqp, dkn, dvv, dkpt, dmz, dga, dgb, cq, ckv, cos256, sin256,
        q_norm_gain, kv_norm_gain, w_q.T, w_kv.T)
    grad_x, dshift, dscale, dg1 = _dh_call(dproj, w_int.T, xs, dx2, scale, norm_gain)
    dwi = _dwin_call(h, dproj)
    g_in, g_uq, g_ukv = _true_weight_grads(dwi, dwq, dwkv)

    gpack = jnp.concatenate([
        _chip_major(g_in, 1320), _chip_major(g_uq, 192), _chip_major(g_ukv, 256),
        _chip_major(dwa, 256), _chip_major(dwb, 256), dwo.reshape(N_CHIPS, -1, LANES)], axis=1)
    gpack = gpack.reshape(N_DEV, HALF_ROWS, LANES)
    small = jnp.concatenate([
        dshift, dscale, dgate, dg1, dqg, dkvg, dgf, loss_p,
        jnp.zeros((1, 8 * SV_COLS - 5888), F32)], axis=1).reshape(8, SV_COLS)
    full, svg = _reduce_call(gpack.astype(BF16), small)
    full = full.reshape(PACK_TOTAL, LANES)
    offs = [0]
    for r in PACK_ROWS:
        offs.append(offs[-1] + r)
    gs_in = full[offs[0]:offs[1]].reshape(D_MODEL, 1320)
    gs_uq = full[offs[1]:offs[2]].reshape(Q_RANK, 192)
    gs_ukv = full[offs[2]:offs[3]].reshape(KV_RANK, 256)
    gs_a = full[offs[3]:offs[4]].reshape(512, 256)
    gs_b = full[offs[4]:offs[5]].reshape(512, 256)
    gs_out = full[offs[5]:offs[6]].reshape(256, D_MODEL)

    svm = svg.reshape(N_DEV, 8 * SV_COLS)
    dmod_sh = lax.dynamic_slice_in_dim(svm[:, :3 * D_MODEL], chip * 768, 768, axis=1)
    tot, gs_ada = _small_call(svm, call.T, dmod_sh)
    g_bada = tot[:, 0:3072]
    g_g1 = tot[:, 3072:4096]
    g_qg = tot[:, 4096:4480]
    g_kvg = tot[:, 4480:4736]
    g_gf = tot[:, 4736:5760]
    loss = tot[0, 5760]

    names = ["w_ada", "b_ada", "norm_gain", "w_in", "q_norm_gain", "w_uq", "kv_norm_gain", "w_ukv",
             "w_branch_a", "w_branch_b", "w_out", "final_norm_gain"]
    ws = [w_ada[0], b_ada, norm_gain, w_in[0], q_norm_gain, w_uq[0], kv_norm_gain, w_ukv[0],
          w_branch_a[0], w_branch_b[0], w_out[0], final_norm_gain.reshape(1, D_MODEL)]
    gs = [gs_ada, g_bada, g_g1, gs_in, g_qg, gs_uq, g_kvg, gs_ukv, gs_a, gs_b, gs_out, g_gf]
    ms = [m_w_ada[0], m_b_ada, m_norm_gain, m_w_in[0], m_q_norm_gain, m_w_uq[0], m_kv_norm_gain, m_w_ukv[0],
          m_w_branch_a[0], m_w_branch_b[0], m_w_out[0], m_final_norm_gain.reshape(1, D_MODEL)]
    vs = [v_w_ada[0], v_b_ada, v_norm_gain, v_w_in[0], v_q_norm_gain, v_w_uq[0], v_kv_norm_gain, v_w_ukv[0],
          v_w_branch_a[0], v_w_branch_b[0], v_w_out[0], v_final_norm_gain.reshape(1, D_MODEL)]
    refs = [w_ada, b_ada, norm_gain, w_in, q_norm_gain, w_uq, kv_norm_gain, w_ukv,
            w_branch_a, w_branch_b, w_out, final_norm_gain]
    grads, deltas, new_ms, new_vs = [], [], [], []
    for n, w_, g_, m_, v_, ref in zip(names, ws, gs, ms, vs, refs):
        d_, nm_, nv_ = _adamw_call(n, w_, g_, m_, v_)
        grads.append(g_.reshape(ref.shape))
        deltas.append(d_.reshape(ref.shape))
        new_ms.append(nm_.reshape(ref.shape))
        new_vs.append(nv_.reshape(ref.shape))

    return (loss, grad_x.reshape(x.shape), *grads, *deltas, *new_ms, *new_vs)
```

```python
import functools
import math

import jax
import jax.numpy as jnp
from jax import lax
from jax.experimental import pallas as pl
from jax.experimental.pallas import tpu as pltpu

F32 = jnp.float32
BF16 = jnp.bfloat16

D_MODEL = 1024
SB_WIDTH = 512
MLA_WIDTH = 512
Q_RANK = 384
KV_RANK = 256
ROPE_DIM = 32
N_HEADS = 8
IN_WIDTH = 5280
EPS = 1e-6
ROPE_BASE = 10000.0
MLA_SCALE = 1.0 / math.sqrt(96.0)

ADAM_LR = 0.001
ADAM_B1 = 0.9
ADAM_B2 = 0.999
ADAM_EPS = 1e-08
ADAM_WD = 0.01
ADAM_STEP = 10

O_SQ, O_SK, O_SV, O_SZ, O_CQ, O_CKV, O_MZ, O_GA, O_GB, O_KR, O_END = (
    0, 512, 1024, 1536, 2048, 2432, 2688, 3200, 4224, 5248, 5504)
W_INT = 5632

N_CHIPS = 4
N_DEV = 8
LANES = 128
PACK_ROWS = (10560, 576, 512, 1024, 1024, 2048)
PACK_TOTAL = sum(PACK_ROWS)
HALF_ROWS = PACK_TOTAL // 2
SV_COLS = 768

ROW_TILE = 256
ATT_TILE = 256
VMEM_LIMIT = 56 * 1024 * 1024

MESH = pl.DeviceIdType.MESH


def _dot(a, b):
    return lax.dot_general(a, b, (((1,), (0,)), ((), ())), preferred_element_type=F32)


def _dot_nt(a, b):
    return lax.dot_general(a, b, (((1,), (1,)), ((), ())), preferred_element_type=F32)


def _dot_tn(a, b):
    return lax.dot_general(a, b, (((0,), (0,)), ((), ())), preferred_element_type=F32)


def _sigmoid(z):
    return 1.0 / (1.0 + jnp.exp(-z))


def _params(sem=None):
    if sem is None:
        return pltpu.CompilerParams(vmem_limit_bytes=VMEM_LIMIT)
    return pltpu.CompilerParams(dimension_semantics=sem, vmem_limit_bytes=VMEM_LIMIT)


def _rows(tm, n):
    return pl.BlockSpec((tm, n), lambda i: (i, 0))


def _whole(shape):
    nd = len(shape)
    return pl.BlockSpec(shape, lambda i: (0,) * nd)


def _sds(shape, dtype):
    return jax.ShapeDtypeStruct(shape, dtype)


def _flip(v, d):
    return 1 - v if d else v


def _gather_call(c_row, w_ada_sh, pack):
    def body(c_ref, wada_ref, pk_ref, mg_ref, cg_ref, gw_ref,
             cv, ssem_c, rsem_c, ssem_m, rsem_m, ssem_w, rsem_w, ssem_f, rsem_f, lsem):
        x, y, c = lax.axis_index("x"), lax.axis_index("y"), lax.axis_index("c")
        me = 4 * x + 2 * y + c
        chip = 2 * x + y
        rel3 = [(1, 0), (0, 1), (1, 1)]

        wcopies = []
        for j, (dx, dy) in enumerate(rel3):
            cp = pltpu.make_async_remote_copy(
                src_ref=pk_ref.at[c], dst_ref=gw_ref.at[chip, c], send_sem=ssem_w.at[j], recv_sem=rsem_w.at[j],
                device_id=(_flip(x, dx), _flip(y, dy), c), device_id_type=MESH)
            cp.start()
            wcopies.append(cp)
        own = pltpu.make_async_copy(pk_ref, gw_ref.at[chip], lsem)
        own.start()

        cv[me] = c_ref[...]
        ccopies = []
        for r in range(1, N_DEV):
            dx, dy, dc = (r >> 2) & 1, (r >> 1) & 1, r & 1
            cp = pltpu.make_async_remote_copy(
                src_ref=c_ref, dst_ref=cv.at[me], send_sem=ssem_c.at[r - 1], recv_sem=rsem_c.at[r - 1],
                device_id=(_flip(x, dx), _flip(y, dy), _flip(c, dc)), device_id_type=MESH)
            cp.start()
            ccopies.append(cp)
        for r in range(1, N_DEV):
            dx, dy, dc = (r >> 2) & 1, (r >> 1) & 1, r & 1
            src = 4 * _flip(x, dx) + 2 * _flip(y, dy) + _flip(c, dc)
            pltpu.make_async_remote_copy(
                src_ref=c_ref, dst_ref=cv.at[src], send_sem=ssem_c.at[r - 1], recv_sem=rsem_c.at[r - 1],
                device_id=(x, y, c), device_id_type=MESH).wait_recv()
        rows = lax.broadcasted_iota(jnp.int32, (N_DEV, D_MODEL), 0)
        call = jnp.zeros((N_DEV, D_MODEL), F32)
        for b in range(N_DEV):
            call = jnp.where(rows == b, jnp.broadcast_to(cv[b], (N_DEV, D_MODEL)), call)
        cg_ref[...] = call

        mg_ref[chip] = lax.dot_general(call, wada_ref[...], (((1,), (0,)), ((), ())),
                                       precision=lax.Precision.HIGHEST, preferred_element_type=F32)
        mcopies = []
        for j, (dx, dy) in enumerate(rel3):
            cp = pltpu.make_async_remote_copy(
                src_ref=mg_ref.at[chip], dst_ref=mg_ref.at[chip], send_sem=ssem_m.at[j], recv_sem=rsem_m.at[j],
                device_id=(_flip(x, dx), _flip(y, dy), c), device_id_type=MESH)
            cp.start()
            mcopies.append(cp)
        for j, (dx, dy) in enumerate(rel3):
            src_chip = 2 * _flip(x, dx) + _flip(y, dy)
            pltpu.make_async_remote_copy(
                src_ref=mg_ref.at[src_chip], dst_ref=mg_ref.at[src_chip], send_sem=ssem_m.at[j],
                recv_sem=rsem_m.at[j], device_id=(x, y, c), device_id_type=MESH).wait_recv()
        fcopies = []
        for j, (dx, dy) in enumerate(rel3):
            src_chip = 2 * _flip(x, dx) + _flip(y, dy)
            pltpu.make_async_remote_copy(
                src_ref=pk_ref.at[c], dst_ref=gw_ref.at[src_chip, c], send_sem=ssem_w.at[j], recv_sem=rsem_w.at[j],
                device_id=(x, y, c), device_id_type=MESH).wait_recv()
            cp = pltpu.make_async_remote_copy(
                src_ref=gw_ref.at[src_chip, c], dst_ref=gw_ref.at[src_chip, c], send_sem=ssem_f.at[j],
                recv_sem=rsem_f.at[j], device_id=(x, y, 1 - c), device_id_type=MESH)
            cp.start()
            fcopies.append(cp)
        for j, (dx, dy) in enumerate(rel3):
            src_chip = 2 * _flip(x, dx) + _flip(y, dy)
            pltpu.make_async_remote_copy(
                src_ref=pk_ref.at[c], dst_ref=gw_ref.at[src_chip, 1 - c], send_sem=ssem_f.at[j],
                recv_sem=rsem_f.at[j], device_id=(x, y, c), device_id_type=MESH).wait_recv()
        for cp in ccopies + mcopies + wcopies + fcopies:
            cp.wait_send()
        own.wait()

    vmem = pl.BlockSpec(memory_space=pltpu.VMEM)
    return pl.pallas_call(
        body, name="gather_fwd",
        out_shape=(_sds((N_CHIPS, N_DEV, 768), F32), _sds((N_DEV, D_MODEL), F32),
                   _sds((N_CHIPS, 2, HALF_ROWS, LANES), BF16)),
        in_specs=[vmem, vmem, vmem], out_specs=(vmem, vmem, vmem),
        scratch_shapes=[
            pltpu.VMEM((N_DEV, 1, D_MODEL), F32),
            pltpu.SemaphoreType.DMA((N_DEV - 1,)), pltpu.SemaphoreType.DMA((N_DEV - 1,)),
            pltpu.SemaphoreType.DMA((3,)), pltpu.SemaphoreType.DMA((3,)),
            pltpu.SemaphoreType.DMA((3,)), pltpu.SemaphoreType.DMA((3,)),
            pltpu.SemaphoreType.DMA((3,)), pltpu.SemaphoreType.DMA((3,)),
            pltpu.SemaphoreType.DMA,
        ],
        compiler_params=_params(),
    )(c_row, w_ada_sh, pack)


SUM_ROWS = 656


def _reduce_call(gpack, sv):
    def body(g_ref, sv_ref, full_ref, svg_ref, land, ssem_g, rsem_g, ssem_s, rsem_s, ssem_x, rsem_x, lsem):
        x, y, c = lax.axis_index("x"), lax.axis_index("y"), lax.axis_index("c")
        me = 4 * x + 2 * y + c
        copies = []
        for r in range(1, N_DEV):
            dx, dy, dc = (r >> 2) & 1, (r >> 1) & 1, r & 1
            tx, ty, tc = _flip(x, dx), _flip(y, dy), _flip(c, dc)
            tgt = 4 * tx + 2 * ty + tc
            cp = pltpu.make_async_remote_copy(
                src_ref=g_ref.at[tgt], dst_ref=land.at[me], send_sem=ssem_g.at[r - 1],
                recv_sem=rsem_g.at[r - 1], device_id=(tx, ty, tc), device_id_type=MESH)
            cp.start()
            copies.append(cp)
            cp = pltpu.make_async_remote_copy(
                src_ref=sv_ref, dst_ref=svg_ref.at[me], send_sem=ssem_s.at[r - 1],
                recv_sem=rsem_s.at[r - 1], device_id=(tx, ty, tc), device_id_type=MESH)
            cp.start()
            copies.append(cp)
        own = pltpu.make_async_copy(g_ref.at[me], land.at[me], lsem)
        own.start()
        svg_ref[me] = sv_ref[...]
        for r in range(1, N_DEV):
            dx, dy, dc = (r >> 2) & 1, (r >> 1) & 1, r & 1
            src = 4 * _flip(x, dx) + 2 * _flip(y, dy) + _flip(c, dc)
            pltpu.make_async_remote_copy(
                src_ref=g_ref.at[src], dst_ref=land.at[src], send_sem=ssem_g.at[r - 1],
                recv_sem=rsem_g.at[r - 1], device_id=(x, y, c), device_id_type=MESH).wait_recv()
            pltpu.make_async_remote_copy(
                src_ref=sv_ref, dst_ref=svg_ref.at[src], send_sem=ssem_s.at[r - 1],
                recv_sem=rsem_s.at[r - 1], device_id=(x, y, c), device_id_type=MESH).wait_recv()
        own.wait()

        def sum_rows(i, carry):
            sl = pl.ds(pl.multiple_of(i * SUM_ROWS, 16), SUM_ROWS)
            acc = land[0, sl, :].astype(F32)
            for d in range(1, N_DEV):
                acc = acc + land[d, sl, :].astype(F32)
            full_ref[c, sl, :] = acc
            return carry

        lax.fori_loop(0, HALF_ROWS // SUM_ROWS, sum_rows, 0)
        swap = pltpu.make_async_remote_copy(
            src_ref=full_ref.at[c], dst_ref=full_ref.at[c], send_sem=ssem_x, recv_sem=rsem_x,
            device_id=(x, y, 1 - c), device_id_type=MESH)
        swap.start()
        pltpu.make_async_remote_copy(
            src_ref=full_ref.at[c], dst_ref=full_ref.at[1 - c], send_sem=ssem_x, recv_sem=rsem_x,
            device_id=(x, y, c), device_id_type=MESH).wait_recv()
        swap.wait_send()
        for cp in copies:
            cp.wait_send()

    vmem = pl.BlockSpec(memory_space=pltpu.VMEM)
    return pl.pallas_call(
        body, name="grad_reduce",
        out_shape=(_sds((2, HALF_ROWS, LANES), F32), _sds((N_DEV, 8, SV_COLS), F32)),
        in_specs=[vmem, vmem], out_specs=(vmem, vmem),
        scratch_shapes=[
            pltpu.VMEM((N_DEV, HALF_ROWS, LANES), BF16),
            pltpu.SemaphoreType.DMA((N_DEV - 1,)), pltpu.SemaphoreType.DMA((N_DEV - 1,)),
            pltpu.SemaphoreType.DMA((N_DEV - 1,)), pltpu.SemaphoreType.DMA((N_DEV - 1,)),
            pltpu.SemaphoreType.DMA, pltpu.SemaphoreType.DMA, pltpu.SemaphoreType.DMA,
        ],
        compiler_params=_params(),
    )(gpack, sv)


def _inproj_call(x, shift, scale, g1, w_int, w_q, w_kv, qg, kvg, cos256, sin256):
    s_len = x.shape[0]
    tm = min(ROW_TILE, s_len)

    def body(x_ref, sh_ref, sc_ref, g1_ref, w_ref, wq_ref, wkv_ref, qg_ref, kvg_ref, cos_ref, sin_ref,
             h_ref, sq_ref, sk_ref, sv_ref, sz_ref, cq_ref, ckv_ref, mz_ref, ga_ref, gb_ref, kpt_ref,
             qn_ref, qp_ref, kn_ref, vv_ref):
        xt = x_ref[...]
        r = lax.rsqrt(jnp.mean(xt * xt, axis=-1, keepdims=True) + EPS)
        h = (xt * r * g1_ref[...]) * (1.0 + sc_ref[...]) + sh_ref[...]
        hb = h.astype(BF16)
        h_ref[...] = hb

        def seg(a, b):
            return _dot(hb, w_ref[:, a:b])

        sq_ref[...] = seg(O_SQ, O_SK).astype(BF16)
        sk_ref[...] = seg(O_SK, O_SV).astype(BF16)
        sv_ref[...] = seg(O_SV, O_SZ).astype(BF16)
        sz_ref[...] = seg(O_SZ, O_CQ)
        mz_ref[...] = seg(O_MZ, O_GA)
        ga_ref[...] = seg(O_GA, O_GB)
        gb_ref[...] = seg(O_GB, O_KR)
        cos = cos_ref[...]
        sin = sin_ref[...]
        kr = seg(O_KR, O_END)
        kpt_ref[...] = (kr[:, :128] * cos[:, :128] + kr[:, 128:] * sin[:, :128]).astype(BF16)

        cq = seg(O_CQ, O_CKV)
        cq_ref[...] = cq
        rq = lax.rsqrt(jnp.mean(cq * cq, axis=-1, keepdims=True) + EPS)
        cqn = (cq * rq * qg_ref[...]).astype(BF16)
        qa = _dot(cqn, wq_ref[...])
        qn_ref[...] = qa[:, :512].astype(BF16)
        qp_ref[...] = (qa[:, 512:768] * cos + qa[:, 768:] * sin).astype(BF16)

        ckv = seg(O_CKV, O_MZ)
        ckv_ref[...] = ckv
        rk = lax.rsqrt(jnp.mean(ckv * ckv, axis=-1, keepdims=True) + EPS)
        ckvn = (ckv * rk * kvg_ref[...]).astype(BF16)
        kva = _dot(ckvn, wkv_ref[...])
        kn_ref[...] = kva[:, :512].astype(BF16)
        vv_ref[...] = kva[:, 512:].astype(BF16)

    outs = [
        (D_MODEL, BF16), (512, BF16), (512, BF16), (512, BF16), (512, F32), (Q_RANK, F32), (KV_RANK, F32),
        (512, F32), (D_MODEL, F32), (D_MODEL, F32), (128, BF16), (512, BF16), (256, BF16), (512, BF16), (512, BF16),
    ]
    return pl.pallas_call(
        body, name="inproj", grid=(s_len // tm,),
        out_shape=tuple(_sds((s_len, n), dt) for n, dt in outs),
        in_specs=[_rows(tm, D_MODEL), _whole((1, D_MODEL)), _whole((1, D_MODEL)), _whole((1, D_MODEL)),
                  _whole((D_MODEL, W_INT)), _whole((Q_RANK, 1024)), _whole((KV_RANK, 1024)),
                  _whole((1, Q_RANK)), _whole((1, KV_RANK)), _rows(tm, 256), _rows(tm, 256)],
        out_specs=tuple(_rows(tm, n) for n, _ in outs),
        compiler_params=_params(("parallel",)),
    )(x, shift, scale, g1, w_int, w_q, w_kv, qg, kvg, cos256, sin256)


def _softplus(z):
    return jnp.maximum(z, 0.0) + jnp.log(1.0 + jnp.exp(-jnp.abs(z)))


def _split_bf16(a):
    hi = a.astype(BF16)
    lo = (a - hi.astype(F32)).astype(BF16)
    return hi, lo


def _sb_fwd_call(q, k, v):
    s_len = q.shape[0]
    t = min(ATT_TILE, s_len)
    nq = s_len // t

    def body(q_ref, k_ref, v_ref, o_ref, lt_ref):
        i = pl.program_id(1)
        q2 = q_ref[...]
        lane = lax.broadcasted_iota(jnp.int32, (1, 128), 1)
        row = lax.broadcasted_iota(jnp.int32, (t, t), 0)
        col = lax.broadcasted_iota(jnp.int32, (t, t), 1)
        later = (row > col).astype(BF16)
        later2 = jnp.concatenate([later, later], axis=0)
        valid = col < row
        hms = [(lane // 64) == hh for hh in range(2)]
        qms = [jnp.where(hm, q2, jnp.zeros_like(q2)) for hm in hms]

        def block(j, carry, diag):
            runs, acc = list(carry[:2]), carry[2]
            off = pl.multiple_of(j * t, t)
            kb = k_ref[pl.ds(off, t), :]
            vb = v_ref[pl.ds(off, t), :]
            ws = []
            for hh in range(2):
                z = _dot_nt(qms[hh], kb)
                lg = -_softplus(z)
                lm = jnp.where(valid, lg, 0.0) if diag else lg
                hi, lo = _split_bf16(lm)
                suf = _dot(jnp.concatenate([hi, lo], axis=1), later2)
                w = jnp.exp(z + lg + suf + runs[hh])
                if diag:
                    w = jnp.where(valid, w, 0.0)
                ws.append(w.astype(BF16))
                runs[hh] = runs[hh] + jnp.sum(lm, axis=1, keepdims=True)
            vstack = jnp.concatenate([jnp.where(hm, vb, jnp.zeros_like(vb)) for hm in hms], axis=0)
            acc = acc + _dot(jnp.concatenate(ws, axis=1), vstack)
            return runs[0], runs[1], acc

        zero = jnp.zeros((t, 1), F32)
        carry = block(i, (zero, zero, jnp.zeros((t, 128), F32)), True)
        carry = lax.fori_loop(1, i + 1, lambda jj, cr: block(i - jj, cr, False), carry)
        lt_ref[0, :, 0:1] = carry[0]
        lt_ref[0, :, 1:2] = carry[1]
        o_ref[...] = carry[2]

    return pl.pallas_call(
        body, name="sb_fwd", grid=(4, nq),
        out_shape=(_sds((s_len, SB_WIDTH), F32), _sds((4, s_len, 2), F32)),
        in_specs=[pl.BlockSpec((t, 128), lambda p, i: (i, p)),
                  pl.BlockSpec((s_len, 128), lambda p, i: (0, p)),
                  pl.BlockSpec((s_len, 128), lambda p, i: (0, p))],
        out_specs=(pl.BlockSpec((t, 128), lambda p, i: (i, p)),
                   pl.BlockSpec((1, t, 2), lambda p, i: (p, i, 0))),
        compiler_params=_params(("parallel", "parallel")),
    )(q, k, v)


def _sb_bwd_call(q, k, v, do, lt):
    s_len = q.shape[0]
    t = min(ATT_TILE, s_len)
    nq = s_len // t

    def body(q_ref, k_ref, v_ref, do_ref, lt_ref, dq_ref, dk_ref, dv_ref):
        i = pl.program_id(1)

        @pl.when(i == 0)
        def _():
            dk_ref[...] = jnp.zeros_like(dk_ref)
            dv_ref[...] = jnp.zeros_like(dv_ref)

        q2 = q_ref[...]
        do2 = do_ref[...].astype(BF16)
        lane = lax.broadcasted_iota(jnp.int32, (1, 128), 1)
        row = lax.broadcasted_iota(jnp.int32, (t, t), 0)
        col = lax.broadcasted_iota(jnp.int32, (t, t), 1)
        earlier = (row < col).astype(BF16)
        earlier2 = jnp.concatenate([earlier, earlier], axis=0)
        valid = col < row
        hms = [(lane // 64) == hh for hh in range(2)]
        qms = [jnp.where(hm, q2, jnp.zeros_like(q2)) for hm in hms]
        doms = [jnp.where(hm, do2, jnp.zeros_like(do2)) for hm in hms]
        ltots = [lt_ref[0, :, hh:hh + 1] for hh in range(2)]
        qstack = jnp.concatenate(qms, axis=0)
        dostack = jnp.concatenate(doms, axis=0)

        def block(j, carry, diag):
            lpre, ppre, dq = list(carry[0:2]), list(carry[2:4]), carry[4]
            off = pl.multiple_of(j * t, t)
            kb = k_ref[pl.ds(off, t), :]
            vb = v_ref[pl.ds(off, t), :]
            dzs, avs = [], []
            for hh in range(2):
                z = _dot_nt(qms[hh], kb)
                sp = _softplus(z)
                lg = -sp
                lm = jnp.where(valid, lg, 0.0) if diag else lg
                hi, lo = _split_bf16(lm)
                before = _dot(jnp.concatenate([hi, lo], axis=1), earlier2)
                between = ltots[hh] - (lpre[hh] + before + lm)
                a = jnp.exp(z + lg + between)
                if diag:
                    a = jnp.where(valid, a, 0.0)
                p = a * _dot_nt(doms[hh], vb)
                phi, plo = _split_bf16(p)
                pbefore = ppre[hh] + _dot(jnp.concatenate([phi, plo], axis=1), earlier2)
                sig = jnp.exp(z - sp)
                dz = p - sig * (p + pbefore)
                if diag:
                    dz = jnp.where(valid, dz, 0.0)
                dzs.append(dz.astype(BF16))
                avs.append(a.astype(BF16))
                lpre[hh] = lpre[hh] + jnp.sum(lm, axis=1, keepdims=True)
                ppre[hh] = ppre[hh] + jnp.sum(p, axis=1, keepdims=True)
            kstack = jnp.concatenate([jnp.where(hm, kb, jnp.zeros_like(kb)) for hm in hms], axis=0)
            dq = dq + _dot(jnp.concatenate(dzs, axis=1), kstack)
            dk_ref[pl.ds(off, t), :] += _dot_tn(jnp.concatenate(dzs, axis=0), qstack)
            dv_ref[pl.ds(off, t), :] += _dot_tn(jnp.concatenate(avs, axis=0), dostack)
            return lpre[0], lpre[1], ppre[0], ppre[1], dq

        zero = jnp.zeros((t, 1), F32)
        carry = lax.fori_loop(0, i, lambda j, cr: block(j, cr, False),
                              (zero, zero, zero, zero, jnp.zeros((t, 128), F32)))
        carry = block(i, carry, True)
        dq_ref[...] = carry[4].astype(BF16)

    return pl.pallas_call(
        body, name="sb_bwd", grid=(4, nq),
        out_shape=(_sds((s_len, SB_WIDTH), BF16), _sds((s_len, SB_WIDTH), F32), _sds((s_len, SB_WIDTH), F32)),
        in_specs=[pl.BlockSpec((t, 128), lambda p, i: (i, p)),
                  pl.BlockSpec((s_len, 128), lambda p, i: (0, p)),
                  pl.BlockSpec((s_len, 128), lambda p, i: (0, p)),
                  pl.BlockSpec((t, 128), lambda p, i: (i, p)),
                  pl.BlockSpec((1, t, 2), lambda p, i: (p, i, 0))],
        out_specs=(pl.BlockSpec((t, 128), lambda p, i: (i, p)),
                   pl.BlockSpec((s_len, 128), lambda p, i: (0, p)),
                   pl.BlockSpec((s_len, 128), lambda p, i: (0, p))),
        compiler_params=_params(("parallel", "arbitrary")),
    )(q, k, v, do, lt)


def _mla_fwd_call(qn, qp, kn, kpt, v):
    s_len = qn.shape[0]
    t = min(ATT_TILE, s_len)
    nq = s_len // t

    def body(qn_ref, qp_ref, kn_ref, kpt_ref, v_ref, o_ref, lse_ref):
        i = pl.program_id(1)
        qn2 = qn_ref[...]
        qp2 = qp_ref[...]
        lane256 = lax.broadcasted_iota(jnp.int32, (1, 256), 1)
        lane128 = lax.broadcasted_iota(jnp.int32, (1, 128), 1)
        row = lax.broadcasted_iota(jnp.int32, (t, t), 0)
        col = lax.broadcasted_iota(jnp.int32, (t, t), 1)
        valid = col <= row
        m64s = [(lane256 // 64) == hh for hh in range(4)]
        m32s = [(lane128 // 32) == hh for hh in range(4)]
        qcs = [jnp.concatenate([jnp.where(m64s[hh], qn2, jnp.zeros_like(qn2)),
                                jnp.where(m32s[hh], qp2, jnp.zeros_like(qp2))], axis=1) for hh in range(4)]

        def by_head(vals):
            return jnp.where(m64s[0], vals[0], jnp.where(m64s[1], vals[1], jnp.where(m64s[2], vals[2], vals[3])))

        def block(j, carry, diag):
            ms, ls, acc = list(carry[0:4]), list(carry[4:8]), carry[8]
            off = pl.multiple_of(j * t, t)
            kc = jnp.concatenate([kn_ref[pl.ds(off, t), :], kpt_ref[pl.ds(off, t), :]], axis=1)
            vb = v_ref[pl.ds(off, t), :]
            ps, alphas = [], []
            for hh in range(4):
                s = _dot_nt(qcs[hh], kc) * MLA_SCALE
                if diag:
                    s = jnp.where(valid, s, -1e30)
                mn = jnp.maximum(ms[hh], jnp.max(s, axis=1, keepdims=True))
                p = jnp.exp(s - mn)
                alpha = jnp.exp(ms[hh] - mn)
                ls[hh] = alpha * ls[hh] + jnp.sum(p, axis=1, keepdims=True)
                ms[hh] = mn
                ps.append(p.astype(BF16))
                alphas.append(alpha)
            vstack = jnp.concatenate([jnp.where(m64, vb, jnp.zeros_like(vb)) for m64 in m64s], axis=0)
            acc = by_head(alphas) * acc + _dot(jnp.concatenate(ps, axis=1), vstack)
            return (*ms, *ls, acc)

        neg = jnp.full((t, 1), -1e30, F32)
        zero = jnp.zeros((t, 1), F32)
        carry = lax.fori_loop(0, i, lambda j, cr: block(j, cr, False),
                              (neg, neg, neg, neg, zero, zero, zero, zero, jnp.zeros((t, 256), F32)))
        carry = block(i, carry, True)
        o_ref[...] = carry[8] / by_head(list(carry[4:8]))
        for hh in range(4):
            lse_ref[0, :, hh:hh + 1] = carry[hh] + jnp.log(carry[4 + hh])

    return pl.pallas_call(
        body, name="mla_fwd", grid=(2, nq),
        out_shape=(_sds((s_len, MLA_WIDTH), F32), _sds((2, s_len, 4), F32)),
        in_specs=[pl.BlockSpec((t, 256), lambda g, i: (i, g)),
                  pl.BlockSpec((t, 128), lambda g, i: (i, g)),
                  pl.BlockSpec((s_len, 256), lambda g, i: (0, g)),
                  pl.BlockSpec((s_len, 128), lambda g, i: (0, 0)),
                  pl.BlockSpec((s_len, 256), lambda g, i: (0, g))],
        out_specs=(pl.BlockSpec((t, 256), lambda g, i: (i, g)),
                   pl.BlockSpec((1, t, 4), lambda g, i: (g, i, 0))),
        compiler_params=_params(("parallel", "parallel")),
    )(qn, qp, kn, kpt, v)


def _mla_bwd_call(qn, qp, kn, kpt, v, o, do, lse):
    s_len = qn.shape[0]
    t = min(ATT_TILE, s_len)
    nq = s_len // t

    def body(qn_ref, qp_ref, kn_ref, kpt_ref, v_ref, o_ref, do_ref, lse_ref,
             dqn_ref, dqp_ref, dkn_ref, dkpt_ref, dv_ref):
        g = pl.program_id(0)
        i = pl.program_id(1)

        @pl.when(i == 0)
        def _():
            dkn_ref[...] = jnp.zeros_like(dkn_ref)
            dv_ref[...] = jnp.zeros_like(dv_ref)

        @pl.when((i == 0) & (g == 0))
        def _():
            dkpt_ref[...] = jnp.zeros_like(dkpt_ref)

        qn2 = qn_ref[...]
        qp2 = qp_ref[...]
        of = o_ref[...]
        dof = do_ref[...]
        dob = dof.astype(BF16)
        prod = dof * of
        lane256 = lax.broadcasted_iota(jnp.int32, (1, 256), 1)
        lane128 = lax.broadcasted_iota(jnp.int32, (1, 128), 1)
        row = lax.broadcasted_iota(jnp.int32, (t, t), 0)
        col = lax.broadcasted_iota(jnp.int32, (t, t), 1)
        valid = col <= row
        m64s = [(lane256 // 64) == hh for hh in range(4)]
        m32s = [(lane128 // 32) == hh for hh in range(4)]
        qcs = [jnp.concatenate([jnp.where(m64s[hh], qn2, jnp.zeros_like(qn2)),
                                jnp.where(m32s[hh], qp2, jnp.zeros_like(qp2))], axis=1) for hh in range(4)]
        doms = [jnp.where(m64, dob, jnp.zeros_like(dob)) for m64 in m64s]
        dsums = [jnp.sum(jnp.where(m64, prod, 0.0), axis=1, keepdims=True) * MLA_SCALE for m64 in m64s]
        lses = [lse_ref[0, :, hh:hh + 1] for hh in range(4)]
        qstack = jnp.concatenate(qcs, axis=0)
        dostack = jnp.concatenate(doms, axis=0)

        def block(j, dqc, diag):
            off = pl.multiple_of(j * t, t)
            knb = kn_ref[pl.ds(off, t), :]
            kpb = kpt_ref[pl.ds(off, t), :]
            vb = v_ref[pl.ds(off, t), :]
            kc = jnp.concatenate([knb, kpb], axis=1)
            dss, pbs = [], []
            for hh in range(4):
                s = _dot_nt(qcs[hh], kc) * MLA_SCALE
                if diag:
                    s = jnp.where(valid, s, -1e30)
                p = jnp.exp(s - lses[hh])
                ds = p * (_dot_nt(doms[hh], vb) * MLA_SCALE - dsums[hh])
                dss.append(ds.astype(BF16))
                pbs.append(p.astype(BF16))
            kstack = jnp.concatenate(
                [jnp.concatenate([jnp.where(m64s[hh], knb, jnp.zeros_like(knb)),
                                  jnp.where(m32s[hh], kpb, jnp.zeros_like(kpb))], axis=1) for hh in range(4)], axis=0)
            dqc = dqc + _dot(jnp.concatenate(dss, axis=1), kstack)
            dkc = _dot_tn(jnp.concatenate(dss, axis=0), qstack)
            dkn_ref[pl.ds(off, t), :] += dkc[:, :256]
            dkpt_ref[pl.ds(off, t), :] += dkc[:, 256:]
            dv_ref[pl.ds(off, t), :] += _dot_tn(jnp.concatenate(pbs, axis=0), dostack)
            return dqc

        dqc = lax.fori_loop(0, i, lambda j, cr: block(j, cr, False), jnp.zeros((t, 384), F32))
        dqc = block(i, dqc, True)
        dqn_ref[...] = dqc[:, :256].astype(BF16)
        dqp_ref[...] = dqc[:, 256:].astype(BF16)

    return pl.pallas_call(
        body, name="mla_bwd", grid=(2, nq),
        out_shape=(_sds((s_len, 512), BF16), _sds((s_len, 256), BF16), _sds((s_len, 512), F32),
                   _sds((s_len, 128), F32), _sds((s_len, 512), F32)),
        in_specs=[pl.BlockSpec((t, 256), lambda g, i: (i, g)),
                  pl.BlockSpec((t, 128), lambda g, i: (i, g)),
                  pl.BlockSpec((s_len, 256), lambda g, i: (0, g)),
                  pl.BlockSpec((s_len, 128), lambda g, i: (0, 0)),
                  pl.BlockSpec((s_len, 256), lambda g, i: (0, g)),
                  pl.BlockSpec((t, 256), lambda g, i: (i, g)),
                  pl.BlockSpec((t, 256), lambda g, i: (i, g)),
                  pl.BlockSpec((1, t, 4), lambda g, i: (g, i, 0))],
        out_specs=(pl.BlockSpec((t, 256), lambda g, i: (i, g)),
                   pl.BlockSpec((t, 128), lambda g, i: (i, g)),
                   pl.BlockSpec((s_len, 256), lambda g, i: (0, g)),
                   pl.BlockSpec((s_len, 128), lambda g, i: (0, 0)),
                   pl.BlockSpec((s_len, 256), lambda g, i: (0, g))),
        compiler_params=_params(("arbitrary", "arbitrary")),
    )(qn, qp, kn, kpt, v, o, do, lse)


Z_CLAMP = 80.0


def _softplus_clamped(z):
    zc = jnp.minimum(z, Z_CLAMP)
    return zc, jnp.log(1.0 + jnp.exp(zc))


def _tri_sum(a, tri, tri2, passes):
    if passes == 1:
        return _dot(a.astype(BF16), tri)
    hi, lo = _split_bf16(a)
    return _dot(jnp.concatenate([hi, lo], axis=1), tri2)


def _sb4_fwd_call(q, k, v):
    s_len = q.shape[0]
    t = min(ATT_TILE, s_len)
    nq = s_len // t

    def body(q_ref, k_ref, v_ref, o_ref, lt_ref):
        i = pl.program_id(1)
        q2 = q_ref[...]
        lane = lax.broadcasted_iota(jnp.int32, (1, 256), 1)
        row = lax.broadcasted_iota(jnp.int32, (t, t), 0)
        col = lax.broadcasted_iota(jnp.int32, (t, t), 1)
        later = (row > col).astype(BF16)
        later2 = jnp.concatenate([later, later], axis=0)
        valid = col < row
        hms = [(lane // 64) == hh for hh in range(4)]
        qms = [jnp.where(hm, q2, jnp.zeros_like(q2)) for hm in hms]

        def block(j, carry, diag):
            runs, acc = list(carry[:4]), carry[4]
            off = pl.multiple_of(j * t, t)
            kb = k_ref[pl.ds(off, t), :]
            vb = v_ref[pl.ds(off, t), :]
            ws = []
            for hh in range(4):
                zc, sp = _softplus_clamped(_dot_nt(qms[hh], kb))
                lm = jnp.where(valid, sp, 0.0) if diag else sp
                suf = _tri_sum(lm, later, later2, 1)
                w = jnp.exp(zc - sp - suf - runs[hh])
                if diag:
                    w = jnp.where(valid, w, 0.0)
                ws.append(w.astype(BF16))
                runs[hh] = runs[hh] + jnp.sum(lm, axis=1, keepdims=True)
            vstack = jnp.concatenate([jnp.where(hm, vb, jnp.zeros_like(vb)) for hm in hms], axis=0)
            acc = acc + _dot(jnp.concatenate(ws, axis=1), vstack)
            return (*runs, acc)

        zero = jnp.zeros((t, 1), F32)
        carry = block(i, (zero, zero, zero, zero, jnp.zeros((t, 256), F32)), True)
        carry = lax.fori_loop(1, i + 1, lambda jj, cr: block(i - jj, cr, False), carry)
        for hh in range(4):
            lt_ref[0, :, hh:hh + 1] = carry[hh]
        o_ref[...] = carry[4]

    return pl.pallas_call(
        body, name="sb_fwd", grid=(2, nq),
        out_shape=(_sds((s_len, SB_WIDTH), F32), _sds((2, s_len, 4), F32)),
        in_specs=[pl.BlockSpec((t, 256), lambda g, i: (i, g)),
                  pl.BlockSpec((s_len, 256), lambda g, i: (0, g)),
                  pl.BlockSpec((s_len, 256), lambda g, i: (0, g))],
        out_specs=(pl.BlockSpec((t, 256), lambda g, i: (i, g)),
                   pl.BlockSpec((1, t, 4), lambda g, i: (g, i, 0))),
        compiler_params=_params(("parallel", "parallel")),
    )(q, k, v)


def _sb4_bwd_call(q, k, v, do, lt):
    s_len = q.shape[0]
    t = min(ATT_TILE, s_len)
    nq = s_len // t

    def body(q_ref, k_ref, v_ref, do_ref, lt_ref, dq_ref, dk_ref, dv_ref):
        i = pl.program_id(1)

        @pl.when(i == 0)
        def _():
            dk_ref[...] = jnp.zeros_like(dk_ref)
            dv_ref[...] = jnp.zeros_like(dv_ref)

        q2 = q_ref[...]
        do2 = do_ref[...].astype(BF16)
        lane = lax.broadcasted_iota(jnp.int32, (1, 256), 1)
        row = lax.broadcasted_iota(jnp.int32, (t, t), 0)
        col = lax.broadcasted_iota(jnp.int32, (t, t), 1)
        earlier = (row < col).astype(BF16)
        earlier2 = jnp.concatenate([earlier, earlier], axis=0)
        later = (row > col).astype(BF16)
        later2 = jnp.concatenate([later, later], axis=0)
        valid = col < row
        hms = [(lane // 64) == hh for hh in range(4)]
        qms = [jnp.where(hm, q2, jnp.zeros_like(q2)) for hm in hms]
        doms = [jnp.where(hm, do2, jnp.zeros_like(do2)) for hm in hms]
        ltots = [lt_ref[0, :, hh:hh + 1] for hh in range(4)]
        qstack = jnp.concatenate(qms, axis=0)
        dostack = jnp.concatenate(doms, axis=0)

        def block(j, carry, diag):
            lpre, ppre, dq = list(carry[0:4]), list(carry[4:8]), carry[8]
            off = pl.multiple_of(j * t, t)
            kb = k_ref[pl.ds(off, t), :]
            vb = v_ref[pl.ds(off, t), :]
            dzs, avs = [], []
            for hh in range(4):
                zc, sp = _softplus_clamped(_dot_nt(qms[hh], kb))
                lsig = zc - sp
                lm = jnp.where(valid, sp, 0.0) if diag else sp
                rowsum = jnp.sum(lm, axis=1, keepdims=True)
                between = _tri_sum(lm, later, later2, 1) + ((ltots[hh] - lpre[hh]) - rowsum)
                a = jnp.exp(lsig - between)
                if diag:
                    a = jnp.where(valid, a, 0.0)
                p = a * _dot_nt(doms[hh], vb)
                pbefore = ppre[hh] + _tri_sum(p, earlier, earlier2, 1)
                dz = p - jnp.exp(lsig) * (p + pbefore)
                if diag:
                    dz = jnp.where(valid, dz, 0.0)
                dzs.append(dz.astype(BF16))
                avs.append(a.astype(BF16))
                lpre[hh] = lpre[hh] + rowsum
                ppre[hh] = ppre[hh] + jnp.sum(p, axis=1, keepdims=True)
            kstack = jnp.concatenate([jnp.where(hm, kb, jnp.zeros_like(kb)) for hm in hms], axis=0)
            dq = dq + _dot(jnp.concatenate(dzs, axis=1), kstack)
            dk_ref[pl.ds(off, t), :] += _dot_tn(jnp.concatenate(dzs, axis=0), qstack)
            dv_ref[pl.ds(off, t), :] += _dot_tn(jnp.concatenate(avs, axis=0), dostack)
            return (*lpre, *ppre, dq)

        zero = jnp.zeros((t, 1), F32)
        carry = lax.fori_loop(0, i, lambda j, cr: block(j, cr, False),
                              (zero,) * 8 + (jnp.zeros((t, 256), F32),))
        carry = block(i, carry, True)
        dq_ref[...] = carry[8].astype(BF16)

    return pl.pallas_call(
        body, name="sb_bwd", grid=(2, nq),
        out_shape=(_sds((s_len, SB_WIDTH), BF16), _sds((s_len, SB_WIDTH), F32), _sds((s_len, SB_WIDTH), F32)),
        in_specs=[pl.BlockSpec((t, 256), lambda g, i: (i, g)),
                  pl.BlockSpec((s_len, 256), lambda g, i: (0, g)),
                  pl.BlockSpec((s_len, 256), lambda g, i: (0, g)),
                  pl.BlockSpec((t, 256), lambda g, i: (i, g)),
                  pl.BlockSpec((1, t, 4), lambda g, i: (g, i, 0))],
        out_specs=(pl.BlockSpec((t, 256), lambda g, i: (i, g)),
                   pl.BlockSpec((s_len, 256), lambda g, i: (0, g)),
                   pl.BlockSpec((s_len, 256), lambda g, i: (0, g))),
        compiler_params=_params(("parallel", "arbitrary")),
    )(q, k, v, do, lt)


def _mla4_fwd_call(qn, qp, kn, kpt, v):
    s_len = qn.shape[0]
    t = min(ATT_TILE, s_len)
    nq = s_len // t

    def body(qn_ref, qp_ref, kn_ref, kpt_ref, v_ref, o_ref, lse_ref):
        i = pl.program_id(1)
        qn2 = qn_ref[...]
        qp2 = qp_ref[...]
        lane256 = lax.broadcasted_iota(jnp.int32, (1, 256), 1)
        lane128 = lax.broadcasted_iota(jnp.int32, (1, 128), 1)
        row = lax.broadcasted_iota(jnp.int32, (t, t), 0)
        col = lax.broadcasted_iota(jnp.int32, (t, t), 1)
        valid = col <= row
        m64s = [(lane256 // 64) == hh for hh in range(4)]
        half = [(lane128 // 64) == u for u in range(2)]
        m32s = [(lane128 // 32) == hh for hh in range(4)]
        qcs = []
        for hh in range(4):
            qpair = qn2[:, 128 * (hh // 2):128 * (hh // 2) + 128]
            qcs.append(jnp.concatenate([jnp.where(half[hh % 2], qpair, jnp.zeros_like(qpair)),
                                        jnp.where(m32s[hh], qp2, jnp.zeros_like(qp2))], axis=1))

        def by_head(vals):
            return jnp.where(m64s[0], vals[0], jnp.where(m64s[1], vals[1], jnp.where(m64s[2], vals[2], vals[3])))

        def block(j, carry, diag):
            ms, ls, acc = list(carry[0:4]), list(carry[4:8]), carry[8]
            off = pl.multiple_of(j * t, t)
            knb = kn_ref[pl.ds(off, t), :]
            kpb = kpt_ref[pl.ds(off, t), :]
            vb = v_ref[pl.ds(off, t), :]
            kcs = [jnp.concatenate([knb[:, 128 * pp:128 * pp + 128], kpb], axis=1) for pp in range(2)]
            ps, alphas = [], []
            for hh in range(4):
                s = _dot_nt(qcs[hh], kcs[hh // 2]) * MLA_SCALE
                if diag:
                    s = jnp.where(valid, s, -1e30)
                mn = jnp.maximum(ms[hh], jnp.max(s, axis=1, keepdims=True))
                p = jnp.exp(s - mn)
                alpha = jnp.exp(ms[hh] - mn)
                ls[hh] = alpha * ls[hh] + jnp.sum(p, axis=1, keepdims=True)
                ms[hh] = mn
                ps.append(p.astype(BF16))
                alphas.append(alpha)
            pvs = []
            for pp in range(2):
                vpair = vb[:, 128 * pp:128 * pp + 128]
                vstack = jnp.concatenate([jnp.where(hf, vpair, jnp.zeros_like(vpair)) for hf in half], axis=0)
                pvs.append(_dot(jnp.concatenate(ps[2 * pp:2 * pp + 2], axis=1), vstack))
            acc = by_head(alphas) * acc + jnp.concatenate(pvs, axis=1)
            return (*ms, *ls, acc)

        neg = jnp.full((t, 1), -1e30, F32)
        zero = jnp.zeros((t, 1), F32)
        carry = lax.fori_loop(0, i, lambda j, cr: block(j, cr, False),
                              (neg,) * 4 + (zero,) * 4 + (jnp.zeros((t, 256), F32),))
        carry = block(i, carry, True)
        o_ref[...] = carry[8] / by_head(list(carry[4:8]))
        for hh in range(4):
            lse_ref[0, :, hh:hh + 1] = carry[hh] + jnp.log(carry[4 + hh])

    return pl.pallas_call(
        body, name="mla_fwd", grid=(2, nq),
        out_shape=(_sds((s_len, MLA_WIDTH), F32), _sds((2, s_len, 4), F32)),
        in_specs=[pl.BlockSpec((t, 256), lambda g, i: (i, g)),
                  pl.BlockSpec((t, 128), lambda g, i: (i, g)),
                  pl.BlockSpec((s_len, 256), lambda g, i: (0, g)),
                  pl.BlockSpec((s_len, 128), lambda g, i: (0, 0)),
                  pl.BlockSpec((s_len, 256), lambda g, i: (0, g))],
        out_specs=(pl.BlockSpec((t, 256), lambda g, i: (i, g)),
                   pl.BlockSpec((1, t, 4), lambda g, i: (g, i, 0))),
        compiler_params=_params(("parallel", "parallel")),
    )(qn, qp, kn, kpt, v)


def _mla4_bwd_call(qn, qp, kn, kpt, v, o, do, lse):
    s_len = qn.shape[0]
    t = min(ATT_TILE, s_len)
    nq = s_len // t

    def body(qn_ref, qp_ref, kn_ref, kpt_ref, v_ref, o_ref, do_ref, lse_ref,
             dqn_ref, dqp_ref, dkn_ref, dkpt_ref, dv_ref):
        g = pl.program_id(0)
        i = pl.program_id(1)

        @pl.when(i == 0)
        def _():
            dkn_ref[...] = jnp.zeros_like(dkn_ref)
            dv_ref[...] = jnp.zeros_like(dv_ref)

        @pl.when((i == 0) & (g == 0))
        def _():
            dkpt_ref[...] = jnp.zeros_like(dkpt_ref)

        qn2 = qn_ref[...]
        qp2 = qp_ref[...]
        dof = do_ref[...]
        dob = dof.astype(BF16)
        prod = dof * o_ref[...]
        lane256 = lax.broadcasted_iota(jnp.int32, (1, 256), 1)
        lane128 = lax.broadcasted_iota(jnp.int32, (1, 128), 1)
        row = lax.broadcasted_iota(jnp.int32, (t, t), 0)
        col = lax.broadcasted_iota(jnp.int32, (t, t), 1)
        valid = col <= row
        m64s = [(lane256 // 64) == hh for hh in range(4)]
        half = [(lane128 // 64) == u for u in range(2)]
        m32s = [(lane128 // 32) == hh for hh in range(4)]
        qcs, doms = [], []
        for hh in range(4):
            sl = slice(128 * (hh // 2), 128 * (hh // 2) + 128)
            qpair = qn2[:, sl]
            dpair = dob[:, sl]
            qcs.append(jnp.concatenate([jnp.where(half[hh % 2], qpair, jnp.zeros_like(qpair)),
                                        jnp.where(m32s[hh], qp2, jnp.zeros_like(qp2))], axis=1))
            doms.append(jnp.where(half[hh % 2], dpair, jnp.zeros_like(dpair)))
        dsums = [jnp.sum(jnp.where(m64, prod, 0.0), axis=1, keepdims=True) * MLA_SCALE for m64 in m64s]
        lses = [lse_ref[0, :, hh:hh + 1] for hh in range(4)]
        qstacks = [jnp.concatenate(qcs[2 * pp:2 * pp + 2], axis=0) for pp in range(2)]
        dostacks = [jnp.concatenate(doms[2 * pp:2 * pp + 2], axis=0) for pp in range(2)]

        def block(j, carry, diag):
            dqn, dqp = carry
            off = pl.multiple_of(j * t, t)
            knb = kn_ref[pl.ds(off, t), :]
            kpb = kpt_ref[pl.ds(off, t), :]
            vb = v_ref[pl.ds(off, t), :]
            dqn_parts, dkn_parts, dv_parts = [], [], []
            dkp = None
            for pp in range(2):
                sl = slice(128 * pp, 128 * pp + 128)
                knp = knb[:, sl]
                vpair = vb[:, sl]
                kc = jnp.concatenate([knp, kpb], axis=1)
                dss, pbs, kcms = [], [], []
                for u in range(2):
                    hh = 2 * pp + u
                    s = _dot_nt(qcs[hh], kc) * MLA_SCALE
                    if diag:
                        s = jnp.where(valid, s, -1e30)
                    p = jnp.exp(s - lses[hh])
                    ds = p * (_dot_nt(doms[hh], vpair) * MLA_SCALE - dsums[hh])
                    dss.append(ds.astype(BF16))
                    pbs.append(p.astype(BF16))
                    kcms.append(jnp.concatenate([jnp.where(half[u], knp, jnp.zeros_like(knp)),
                                                 jnp.where(m32s[hh], kpb, jnp.zeros_like(kpb))], axis=1))
                dqc = _dot(jnp.concatenate(dss, axis=1), jnp.concatenate(kcms, axis=0))
                dqn_parts.append(dqc[:, :128])
                dqp = dqp + dqc[:, 128:]
                dkc = _dot_tn(jnp.concatenate(dss, axis=0), qstacks[pp])
                dkn_parts.append(dkc[:, :128])
                dkp = dkc[:, 128:] if dkp is None else dkp + dkc[:, 128:]
                dv_parts.append(_dot_tn(jnp.concatenate(pbs, axis=0), dostacks[pp]))
            dqn = dqn + jnp.concatenate(dqn_parts, axis=1)
            dkn_ref[pl.ds(off, t), :] += jnp.concatenate(dkn_parts, axis=1)
            dkpt_ref[pl.ds(off, t), :] += dkp
            dv_ref[pl.ds(off, t), :] += jnp.concatenate(dv_parts, axis=1)
            return dqn, dqp

        carry = lax.fori_loop(0, i, lambda j, cr: block(j, cr, False),
                              (jnp.zeros((t, 256), F32), jnp.zeros((t, 128), F32)))
        dqn, dqp = block(i, carry, True)
        dqn_ref[...] = dqn.astype(BF16)
        dqp_ref[...] = dqp.astype(BF16)

    return pl.pallas_call(
        body, name="mla_bwd", grid=(2, nq),
        out_shape=(_sds((s_len, 512), BF16), _sds((s_len, 256), BF16), _sds((s_len, 512), F32),
                   _sds((s_len, 128), F32), _sds((s_len, 512), F32)),
        in_specs=[pl.BlockSpec((t, 256), lambda g, i: (i, g)),
                  pl.BlockSpec((t, 128), lambda g, i: (i, g)),
                  pl.BlockSpec((s_len, 256), lambda g, i: (0, g)),
                  pl.BlockSpec((s_len, 128), lambda g, i: (0, 0)),
                  pl.BlockSpec((s_len, 256), lambda g, i: (0, g)),
                  pl.BlockSpec((t, 256), lambda g, i: (i, g)),
                  pl.BlockSpec((t, 256), lambda g, i: (i, g)),
                  pl.BlockSpec((1, t, 4), lambda g, i: (g, i, 0))],
        out_specs=(pl.BlockSpec((t, 256), lambda g, i: (i, g)),
                   pl.BlockSpec((t, 128), lambda g, i: (i, g)),
                   pl.BlockSpec((s_len, 256), lambda g, i: (0, g)),
                   pl.BlockSpec((s_len, 128), lambda g, i: (0, 0)),
                   pl.BlockSpec((s_len, 256), lambda g, i: (0, g))),
        compiler_params=_params(("arbitrary", "arbitrary")),
    )(qn, qp, kn, kpt, v, o, do, lse)


def _post_call(x, tgt, oa, ob, sz, mz, ga, gb, gate, gf, wa, wb, wo, wat, wbt, wot):
    s_len = x.shape[0]
    tm = min(ROW_TILE, s_len)

    def body(x_ref, t_ref, oa_ref, ob_ref, sz_ref, mz_ref, ga_ref, gb_ref, gate_ref, gf_ref,
             wa_ref, wb_ref, wo_ref, wat_ref, wbt_ref, wot_ref,
             dx2_ref, doa_ref, dob_ref, dsz_ref, dmz_ref, dga_ref, dgb_ref,
             dwo_ref, dwa_ref, dwb_ref, dgf_ref, dgate_ref, loss_ref):
        @pl.when(pl.program_id(0) == 0)
        def _():
            dwo_ref[...] = jnp.zeros_like(dwo_ref)
            dwa_ref[...] = jnp.zeros_like(dwa_ref)
            dwb_ref[...] = jnp.zeros_like(dwb_ref)
            dgf_ref[...] = jnp.zeros_like(dgf_ref)
            dgate_ref[...] = jnp.zeros_like(dgate_ref)
            loss_ref[...] = jnp.zeros_like(loss_ref)

        gate = gate_ref[...]
        gf = gf_ref[...]
        oa = oa_ref[...]
        ob = ob_ref[...]
        sz = sz_ref[...]
        mz = mz_ref[...]
        sa = _sigmoid(sz)
        sb = _sigmoid(mz)
        silu_a = sz * sa
        silu_b = mz * sb
        ua = (oa * silu_a).astype(BF16)
        ub = (ob * silu_b).astype(BF16)
        ya = _dot(ua, wa_ref[...])
        yb = _dot(ub, wb_ref[...])
        sga = _sigmoid(ga_ref[...])
        sgb = _sigmoid(gb_ref[...])
        merged = (sga * ya + sgb * yb).astype(BF16)
        out = _dot(merged, wo_ref[...])
        x2 = x_ref[...] + gate * out
        r2 = lax.rsqrt(jnp.mean(x2 * x2, axis=-1, keepdims=True) + EPS)
        xhat = x2 * r2
        err = xhat * gf - t_ref[...]
        loss_ref[...] += 0.5 * jnp.sum(jnp.sum(err * err, axis=1, keepdims=True), axis=0, keepdims=True) / D_MODEL
        dy = err * (1.0 / D_MODEL)
        dgf_ref[...] += jnp.sum(dy * xhat, axis=0, keepdims=True)
        dxhat = dy * gf
        dx2 = r2 * (dxhat - xhat * jnp.mean(dxhat * xhat, axis=-1, keepdims=True))
        dx2_ref[...] = dx2
        dgate_ref[...] += jnp.sum(dx2 * out, axis=0, keepdims=True)
        dout = (dx2 * gate).astype(BF16)
        dmerged = _dot(dout, wot_ref[...])
        dwo_ref[...] += _dot_tn(merged, dout)
        dya = dmerged * sga
        dyb = dmerged * sgb
        dga_ref[...] = (dya * ya * (1.0 - sga)).astype(BF16)
        dgb_ref[...] = (dyb * yb * (1.0 - sgb)).astype(BF16)
        dyab = dya.astype(BF16)
        dybb = dyb.astype(BF16)
        dua = _dot(dyab, wat_ref[...])
        dub = _dot(dybb, wbt_ref[...])
        dwa_ref[...] += _dot_tn(ua, dyab)
        dwb_ref[...] += _dot_tn(ub, dybb)
        doa_ref[...] = dua * silu_a
        dob_ref[...] = dub * silu_b
        dsz_ref[...] = (dua * oa * (sa * (1.0 + sz * (1.0 - sa)))).astype(BF16)
        dmz_ref[...] = (dub * ob * (sb * (1.0 + mz * (1.0 - sb)))).astype(BF16)

    return pl.pallas_call(
        body, name="post", grid=(s_len // tm,),
        out_shape=(_sds((s_len, D_MODEL), F32), _sds((s_len, 512), F32), _sds((s_len, 512), F32),
                   _sds((s_len, 512), BF16), _sds((s_len, 512), BF16),
                   _sds((s_len, D_MODEL), BF16), _sds((s_len, D_MODEL), BF16),
                   _sds((D_MODEL, D_MODEL), F32), _sds((512, D_MODEL), F32), _sds((512, D_MODEL), F32),
                   _sds((1, D_MODEL), F32), _sds((1, D_MODEL), F32), _sds((1, 128), F32)),
        in_specs=[_rows(tm, D_MODEL), _rows(tm, D_MODEL), _rows(tm, 512), _rows(tm, 512), _rows(tm, 512),
                  _rows(tm, 512), _rows(tm, D_MODEL), _rows(tm, D_MODEL), _whole((1, D_MODEL)), _whole((1, D_MODEL)),
                  _whole((512, D_MODEL)), _whole((512, D_MODEL)), _whole((D_MODEL, D_MODEL)),
                  _whole((D_MODEL, 512)), _whole((D_MODEL, 512)), _whole((D_MODEL, D_MODEL))],
        out_specs=(_rows(tm, D_MODEL), _rows(tm, 512), _rows(tm, 512), _rows(tm, 512), _rows(tm, 512),
                   _rows(tm, D_MODEL), _rows(tm, D_MODEL),
                   _whole((D_MODEL, D_MODEL)), _whole((512, D_MODEL)), _whole((512, D_MODEL)),
                   _whole((1, D_MODEL)), _whole((1, D_MODEL)), _whole((1, 128))),
        compiler_params=_params(("arbitrary",)),
    )(x, tgt, oa, ob, sz, mz, ga, gb, gate, gf, wa, wb, wo, wat, wbt, wot)


def _bwdprep_call(dsq, dsk, dsv, dsz, dqn, dqp, dkn, dvv, dkpt, dmz, dga, dgb, cq, ckv, cos256, sin256,
                  qg, kvg, wqt, wkvt):
    s_len = cq.shape[0]
    tm = min(ROW_TILE, s_len)

    def body(dsq_ref, dsk_ref, dsv_ref, dsz_ref, dqn_ref, dqp_ref, dkn_ref, dvv_ref, dkpt_ref, dmz_ref,
             dga_ref, dgb_ref, cq_ref, ckv_ref, cos_ref, sin_ref, qg_ref, kvg_ref, wqt_ref, wkvt_ref,
             dp_ref, dwq_ref, dwkv_ref, dqg_ref, dkvg_ref):
        @pl.when(pl.program_id(0) == 0)
        def _():
            dwq_ref[...] = jnp.zeros_like(dwq_ref)
            dwkv_ref[...] = jnp.zeros_like(dwkv_ref)
            dqg_ref[...] = jnp.zeros_like(dqg_ref)
            dkvg_ref[...] = jnp.zeros_like(dkvg_ref)

        cos = cos_ref[...]
        sin = sin_ref[...]
        dp_ref[:, O_SQ:O_SK] = dsq_ref[...]
        dp_ref[:, O_SK:O_SV] = dsk_ref[...].astype(BF16)
        dp_ref[:, O_SV:O_SZ] = dsv_ref[...].astype(BF16)
        dp_ref[:, O_SZ:O_CQ] = dsz_ref[...]
        dp_ref[:, O_MZ:O_GA] = dmz_ref[...]
        dp_ref[:, O_GA:O_GB] = dga_ref[...]
        dp_ref[:, O_GB:O_KR] = dgb_ref[...]
        dkp = dkpt_ref[...]
        dp_ref[:, O_KR:O_KR + 128] = (dkp * cos[:, :128]).astype(BF16)
        dp_ref[:, O_KR + 128:O_END] = (dkp * sin[:, :128]).astype(BF16)
        dp_ref[:, O_END:W_INT] = jnp.zeros((tm, W_INT - O_END), BF16)

        cq = cq_ref[...]
        rq = lax.rsqrt(jnp.mean(cq * cq, axis=-1, keepdims=True) + EPS)
        cqh = cq * rq
        qg = qg_ref[...]
        cqn = (cqh * qg).astype(BF16)
        dqp = dqp_ref[...].astype(F32)
        dqa = jnp.concatenate([dqn_ref[...], (dqp * cos).astype(BF16), (dqp * sin).astype(BF16)], axis=1)
        dcqn = _dot(dqa, wqt_ref[...])
        dwq_ref[...] += _dot_tn(cqn, dqa)
        dqg_ref[...] += jnp.sum(dcqn * cqh, axis=0, keepdims=True)
        dh = dcqn * qg
        dcq = rq * (dh - cqh * jnp.mean(dh * cqh, axis=-1, keepdims=True))
        dp_ref[:, O_CQ:O_CKV] = dcq.astype(BF16)

        ckv = ckv_ref[...]
        rk = lax.rsqrt(jnp.mean(ckv * ckv, axis=-1, keepdims=True) + EPS)
        ckh = ckv * rk
        kvg = kvg_ref[...]
        ckvn = (ckh * kvg).astype(BF16)
        dkva = jnp.concatenate([dkn_ref[...].astype(BF16), dvv_ref[...].astype(BF16)], axis=1)
        dckvn = _dot(dkva, wkvt_ref[...])
        dwkv_ref[...] += _dot_tn(ckvn, dkva)
        dkvg_ref[...] += jnp.sum(dckvn * ckh, axis=0, keepdims=True)
        dh2 = dckvn * kvg
        dckv = rk * (dh2 - ckh * jnp.mean(dh2 * ckh, axis=-1, keepdims=True))
        dp_ref[:, O_CKV:O_MZ] = dckv.astype(BF16)

    return pl.pallas_call(
        body, name="bwdprep", grid=(s_len // tm,),
        out_shape=(_sds((s_len, W_INT), BF16), _sds((Q_RANK, 1024), F32), _sds((KV_RANK, 1024), F32),
                   _sds((1, Q_RANK), F32), _sds((1, KV_RANK), F32)),
        in_specs=[_rows(tm, 512), _rows(tm, 512), _rows(tm, 512), _rows(tm, 512), _rows(tm, 512), _rows(tm, 256),
                  _rows(tm, 512), _rows(tm, 512), _rows(tm, 128), _rows(tm, 512), _rows(tm, D_MODEL),
                  _rows(tm, D_MODEL), _rows(tm, Q_RANK), _rows(tm, KV_RANK), _rows(tm, 256), _rows(tm, 256),
                  _whole((1, Q_RANK)), _whole((1, KV_RANK)), _whole((1024, Q_RANK)), _whole((1024, KV_RANK))],
        out_specs=(_rows(tm, W_INT), _whole((Q_RANK, 1024)), _whole((KV_RANK, 1024)),
                   _whole((1, Q_RANK)), _whole((1, KV_RANK))),
        compiler_params=_params(("arbitrary",)),
    )(dsq, dsk, dsv, dsz, dqn, dqp, dkn, dvv, dkpt, dmz, dga, dgb, cq, ckv, cos256, sin256, qg, kvg, wqt, wkvt)


def _dh_call(dproj, w_int_t, x, dx2, scale, g1):
    s_len = x.shape[0]
    tm = min(ROW_TILE, s_len)

    def body(dp_ref, wt_ref, x_ref, dx2_ref, sc_ref, g1_ref, gx_ref, dsh_ref, dsc_ref, dg1_ref):
        @pl.when(pl.program_id(0) == 0)
        def _():
            dsh_ref[...] = jnp.zeros_like(dsh_ref)
            dsc_ref[...] = jnp.zeros_like(dsc_ref)
            dg1_ref[...] = jnp.zeros_like(dg1_ref)

        dh = _dot(dp_ref[...], wt_ref[...])
        xt = x_ref[...]
        r = lax.rsqrt(jnp.mean(xt * xt, axis=-1, keepdims=True) + EPS)
        xh = xt * r
        g1 = g1_ref[...]
        xg = xh * g1
        dsh_ref[...] += jnp.sum(dh, axis=0, keepdims=True)
        dsc_ref[...] += jnp.sum(dh * xg, axis=0, keepdims=True)
        dxg = dh * (1.0 + sc_ref[...])
        dg1_ref[...] += jnp.sum(dxg * xh, axis=0, keepdims=True)
        dxh = dxg * g1
        gx_ref[...] = dx2_ref[...] + r * (dxh - xh * jnp.mean(dxh * xh, axis=-1, keepdims=True))

    return pl.pallas_call(
        body, name="dh", grid=(s_len // tm,),
        out_shape=(_sds((s_len, D_MODEL), F32), _sds((1, D_MODEL), F32), _sds((1, D_MODEL), F32),
                   _sds((1, D_MODEL), F32)),
        in_specs=[_rows(tm, W_INT), _whole((W_INT, D_MODEL)), _rows(tm, D_MODEL), _rows(tm, D_MODEL),
                  _whole((1, D_MODEL)), _whole((1, D_MODEL))],
        out_specs=(_rows(tm, D_MODEL), _whole((1, D_MODEL)), _whole((1, D_MODEL)), _whole((1, D_MODEL))),
        compiler_params=_params(("arbitrary",)),
    )(dproj, w_int_t, x, dx2, scale, g1)


def _dwin_call(h, dproj):
    s_len = h.shape[0]
    tm = min(2 * ROW_TILE, s_len)
    nc = 4
    chunk = W_INT // nc

    def body(h_ref, dp_ref, dw_ref):
        @pl.when(pl.program_id(1) == 0)
        def _():
            dw_ref[...] = jnp.zeros_like(dw_ref)

        dw_ref[...] += _dot_tn(h_ref[...], dp_ref[...])

    return pl.pallas_call(
        body, name="dwin", grid=(nc, s_len // tm),
        out_shape=_sds((D_MODEL, nc * chunk), F32),
        in_specs=[pl.BlockSpec((tm, D_MODEL), lambda c, i: (i, 0)),
                  pl.BlockSpec((tm, chunk), lambda c, i: (i, c))],
        out_specs=pl.BlockSpec((D_MODEL, chunk), lambda c, i: (0, c)),
        compiler_params=_params(("parallel", "arbitrary")),
    )(h, dproj)


def _small_call(svg, ct, dmod_sh):
    def body(sv_ref, ct_ref, dm_ref, tot_ref, gwada_ref):
        acc = sv_ref[0:1, :]
        for d in range(1, N_DEV):
            acc = acc + sv_ref[d:d + 1, :]
        tot_ref[...] = acc
        gwada_ref[...] = lax.dot_general(ct_ref[...], dm_ref[...], (((1,), (0,)), ((), ())),
                                         precision=lax.Precision.HIGHEST, preferred_element_type=F32)

    vmem = pl.BlockSpec(memory_space=pltpu.VMEM)
    return pl.pallas_call(
        body, name="small_grads",
        out_shape=(_sds((1, 8 * SV_COLS), F32), _sds((D_MODEL, 768), F32)),
        in_specs=[vmem, vmem, vmem], out_specs=(vmem, vmem),
        compiler_params=_params(),
    )(svg, ct, dmod_sh)


def _adamw_tile_rows(rows, cols):
    budget = 1 << 20
    if rows * cols * 4 <= budget or rows % 8:
        return rows
    best = 8
    for tr in range(8, rows + 1, 8):
        if rows % tr == 0 and tr * cols * 4 <= budget:
            best = tr
    return best


def _adamw_call(name, w, g, m, v):
    rows, cols = w.shape
    tr = _adamw_tile_rows(rows, cols)

    def body(w_ref, g_ref, m_ref, v_ref, d_ref, nm_ref, nv_ref):
        gg = g_ref[...]
        m2 = ADAM_B1 * m_ref[...] + (1.0 - ADAM_B1) * gg
        v2 = ADAM_B2 * v_ref[...] + (1.0 - ADAM_B2) * (gg * gg)
        m_hat = m2 / (1.0 - ADAM_B1 ** ADAM_STEP)
        v_hat = v2 / (1.0 - ADAM_B2 ** ADAM_STEP)
        d_ref[...] = -ADAM_LR * (m_hat / (jnp.sqrt(v_hat) + ADAM_EPS) + ADAM_WD * w_ref[...])
        nm_ref[...] = m2
        nv_ref[...] = v2

    spec = pl.BlockSpec((tr, cols), lambda i: (i, 0))
    return pl.pallas_call(
        body, name="adamw_" + name, grid=(rows // tr,),
        out_shape=(_sds((rows, cols), F32),) * 3,
        in_specs=[spec] * 4, out_specs=(spec,) * 3,
        compiler_params=_params(("parallel",)),
    )(w, g, m, v)


def _swap_halves(w, group):
    r, n = w.shape
    return w.reshape(r, n // group, 2, group // 2)[:, :, ::-1, :].reshape(r, n)


def _pack_shards(parts):
    return jnp.concatenate([p.reshape(-1, LANES) for p in parts], axis=0)


def _unpack_chip_major(gw):
    offs = [0]
    for r in PACK_ROWS:
        offs.append(offs[-1] + r)

    def cols(i, rows, shard_cols):
        blk = gw[:, offs[i]:offs[i + 1]].reshape(N_CHIPS, rows, shard_cols)
        return blk.transpose(1, 0, 2).reshape(rows, N_CHIPS * shard_cols)

    w_in = cols(0, D_MODEL, 1320)
    w_uq = cols(1, Q_RANK, 192)
    w_ukv = cols(2, KV_RANK, 256)
    w_a = cols(3, 512, 256)
    w_b = cols(4, 512, 256)
    w_out = gw[:, offs[5]:offs[6]].reshape(D_MODEL, D_MODEL)
    return w_in, w_uq, w_ukv, w_a, w_b, w_out


def _internal_weights(w_in, w_uq, w_ukv):
    krot = w_in[:, 2688:2720]
    w_int = jnp.concatenate([
        w_in[:, 0:512] * jnp.asarray(0.125, w_in.dtype), w_in[:, 512:2048],
        w_in[:, 2048:2432], w_in[:, 2432:2688], w_in[:, 2720:3232], w_in[:, 3232:4256], w_in[:, 4256:5280],
        jnp.tile(krot, (1, 4)), jnp.tile(_swap_halves(krot, 32), (1, 4)),
        jnp.zeros((D_MODEL, W_INT - O_END), w_in.dtype)], axis=1)
    uq = w_uq.reshape(Q_RANK, N_HEADS, 96)
    wp = uq[:, :, 64:].reshape(Q_RANK, 256)
    w_q = jnp.concatenate([uq[:, :, :64].reshape(Q_RANK, 512), wp, _swap_halves(wp, 32)], axis=1)
    ukv = w_ukv.reshape(KV_RANK, N_HEADS, 128)
    w_kv = jnp.concatenate([ukv[:, :, :64].reshape(KV_RANK, 512), ukv[:, :, 64:].reshape(KV_RANK, 512)], axis=1)
    return w_int, w_q, w_kv


def _true_weight_grads(dwi, dwq, dwkv):
    dkr = dwi[:, O_KR:O_KR + 128].reshape(D_MODEL, 4, 32).sum(axis=1)
    dkr_sw = dwi[:, O_KR + 128:O_END].reshape(D_MODEL, 4, 32).sum(axis=1)
    dkrot = dkr + _swap_halves(dkr_sw, 32)
    g_in = jnp.concatenate([
        dwi[:, 0:512] * 0.125, dwi[:, 512:2048], dwi[:, O_CQ:O_CKV], dwi[:, O_CKV:O_MZ], dkrot,
        dwi[:, O_MZ:O_GA], dwi[:, O_GA:O_GB], dwi[:, O_GB:O_KR]], axis=1)
    dwp = dwq[:, 512:768] + _swap_halves(dwq[:, 768:1024], 32)
    g_uq = jnp.concatenate([dwq[:, :512].reshape(Q_RANK, N_HEADS, 64), dwp.reshape(Q_RANK, N_HEADS, 32)],
                           axis=2).reshape(Q_RANK, 768)
    g_ukv = jnp.concatenate([dwkv[:, :512].reshape(KV_RANK, N_HEADS, 64), dwkv[:, 512:].reshape(KV_RANK, N_HEADS, 64)],
                            axis=2).reshape(KV_RANK, 1024)
    return g_in, g_uq, g_ukv


def _chip_major(g, shard_cols):
    r = g.shape[0]
    return g.reshape(r, N_CHIPS, shard_cols).transpose(1, 0, 2).reshape(N_CHIPS, -1, LANES)


def kernel(x, c, positions, w_ada, b_ada, norm_gain, w_in, q_norm_gain, w_uq, kv_norm_gain, w_ukv, w_branch_a, w_branch_b, w_out, final_norm_gain, loss_target, m_w_ada, m_b_ada, m_norm_gain, m_w_in, m_q_norm_gain, m_w_uq, m_kv_norm_gain, m_w_ukv, m_w_branch_a, m_w_branch_b, m_w_out, m_final_norm_gain, v_w_ada, v_b_ada, v_norm_gain, v_w_in, v_q_norm_gain, v_w_uq, v_kv_norm_gain, v_w_ukv, v_w_branch_a, v_w_branch_b, v_w_out, v_final_norm_gain):
    ix, iy, ic = lax.axis_index("x"), lax.axis_index("y"), lax.axis_index("c")
    me = 4 * ix + 2 * iy + ic
    chip = 2 * ix + iy
    xs = x[0]
    tgt = loss_target[0]
    s_len = xs.shape[0]

    shards = (w_in[0], w_uq[0], w_ukv[0], w_branch_a[0], w_branch_b[0], w_out[0])
    pack = _pack_shards([s.astype(BF16) for s in shards])
    mg, call, gw = _gather_call(c, w_ada[0], pack.reshape(2, HALF_ROWS, LANES))
    gw = gw.reshape(N_CHIPS, PACK_TOTAL, LANES)
    mod = mg.transpose(1, 0, 2).reshape(N_DEV, 3 * D_MODEL) + b_ada
    mod_me = lax.dynamic_slice_in_dim(mod, me, 1, axis=0)
    shift, scale, gate = mod_me[:, :D_MODEL], mod_me[:, D_MODEL:2 * D_MODEL], mod_me[:, 2 * D_MODEL:]

    f_in, f_uq, f_ukv, f_a, f_b, f_out = _unpack_chip_major(gw)
    w_int, w_q, w_kv = _internal_weights(f_in, f_uq, f_ukv)

    inv_freq = ROPE_BASE ** (-jnp.arange(0, ROPE_DIM, 2, dtype=F32) / ROPE_DIM)
    ang = positions[0].astype(F32)[:, None] * inv_freq
    cs, sn = jnp.cos(ang), jnp.sin(ang)
    cos256 = jnp.tile(jnp.concatenate([cs, cs], axis=1), (1, 8))
    sin256 = jnp.tile(jnp.concatenate([-sn, sn], axis=1), (1, 8))

    (h, sq, sk, sv, sz, cq, ckv, mz, ga, gb, kpt, qn, qp, kn, vv) = _inproj_call(
        xs, shift, scale, norm_gain, w_int, w_q, w_kv, q_norm_gain, kv_norm_gain, cos256, sin256)
    oa, lt = _sb4_fwd_call(sq, sk, sv)
    ob, lse = _mla4_fwd_call(qn, qp, kn, kpt, vv)

    gf = final_norm_gain.reshape(1, D_MODEL)
    (dx2, doa, dob, dsz, dmz, dga, dgb, dwo, dwa, dwb, dgf, dgate, loss_p) = _post_call(
        xs, tgt, oa, ob, sz, mz, ga, gb, gate, gf, f_a, f_b, f_out, f_a.T, f_b.T, f_out.T)

    dsq, dsk, dsv = _sb4_bwd_call(sq, sk, sv, doa, lt)
    dqn, dqp, dkn, dkpt, dvv = _mla4_bwd_call(qn, qp, kn, kpt, vv, ob, dob, lse)

    dproj, dwq, dwkv, dqg, dkvg = _bwdprep_call(
        dsq, dsk, dsv, dsz, dqn, dqp, dkn, dvv, dkpt, dmz, dga, dgb, cq, ckv, cos256, sin256,
        q_norm_gain, kv_norm_gain, w_q.T, w_kv.T)
    grad_x, dshift, dscale, dg1 = _dh_call(dproj, w_int.T, xs, dx2, scale, norm_gain)
    dwi = _dwin_call(h, dproj)
    g_in, g_uq, g_ukv = _true_weight_grads(dwi, dwq, dwkv)

    gpack = jnp.concatenate([
        _chip_major(g_in, 1320), _chip_major(g_uq, 192), _chip_major(g_ukv, 256),
        _chip_major(dwa, 256), _chip_major(dwb, 256), dwo.reshape(N_CHIPS, -1, LANES)], axis=1)
    gpack = gpack.reshape(N_DEV, HALF_ROWS, LANES)
    small = jnp.concatenate([
        dshift, dscale, dgate, dg1, dqg, dkvg, dgf, loss_p,
        jnp.zeros((1, 8 * SV_COLS - 5888), F32)], axis=1).reshape(8, SV_COLS)
    full, svg = _reduce_call(gpack.astype(BF16), small)
    full = full.reshape(PACK_TOTAL, LANES)
    offs = [0]
    for r in PACK_ROWS:
        offs.append(offs[-1] + r)
    gs_in = full[offs[0]:offs[1]].reshape(D_MODEL, 1320)
    gs_uq = full[offs[1]:offs[2]].reshape(Q_RANK, 192)
    gs_ukv = full[offs[2]:offs[3]].reshape(KV_RANK, 256)
    gs_a = full[offs[3]:offs[4]].reshape(512, 256)
    gs_b = full[offs[4]:offs[5]].reshape(512, 256)
    gs_out = full[offs[5]:offs[6]].reshape(256, D_MODEL)

    svm = svg.reshape(N_DEV, 8 * SV_COLS)
    dmod_sh = lax.dynamic_slice_in_dim(svm[:, :3 * D_MODEL], chip * 768, 768, axis=1)
    tot, gs_ada = _small_call(svm, call.T, dmod_sh)
    g_bada = tot[:, 0:3072]
    g_g1 = tot[:, 3072:4096]
    g_qg = tot[:, 4096:4480]
    g_kvg = tot[:, 4480:4736]
    g_gf = tot[:, 4736:5760]
    loss = tot[0, 5760]

    names = ["w_ada", "b_ada", "norm_gain", "w_in", "q_norm_gain", "w_uq", "kv_norm_gain", "w_ukv",
             "w_branch_a", "w_branch_b", "w_out", "final_norm_gain"]
    ws = [w_ada[0], b_ada, norm_gain, w_in[0], q_norm_gain, w_uq[0], kv_norm_gain, w_ukv[0],
          w_branch_a[0], w_branch_b[0], w_out[0], final_norm_gain.reshape(1, D_MODEL)]
    gs = [gs_ada, g_bada, g_g1, gs_in, g_qg, gs_uq, g_kvg, gs_ukv, gs_a, gs_b, gs_out, g_gf]
    ms = [m_w_ada[0], m_b_ada, m_norm_gain, m_w_in[0], m_q_norm_gain, m_w_uq[0], m_kv_norm_gain, m_w_ukv[0],
          m_w_branch_a[0], m_w_branch_b[0], m_w_out[0], m_final_norm_gain.reshape(1, D_MODEL)]
    vs = [v_w_ada[0], v_b_ada, v_norm_gain, v_w_in[0], v_q_norm_gain, v_w_uq[0], v_kv_norm_gain, v_w_ukv[0],
          v_w_branch_a[0], v_w_branch_b[0], v_w_out[0], v_final_norm_gain.reshape(1, D_MODEL)]
    refs = [w_ada, b_ada, norm_gain, w_in, q_norm_gain, w_uq, kv_norm_gain, w_ukv,
            w_branch_a, w_branch_b, w_out, final_norm_gain]
    grads, deltas, new_ms, new_vs = [], [], [], []
    for n, w_, g_, m_, v_, ref in zip(names, ws, gs, ms, vs, refs):
        d_, nm_, nv_ = _adamw_call(n, w_, g_, m_, v_)
        grads.append(g_.reshape(ref.shape))
        deltas.append(d_.reshape(ref.shape))
        new_ms.append(nm_.reshape(ref.shape))
        new_vs.append(nv_.reshape(ref.shape))

    return (loss, grad_x.reshape(x.shape), *grads, *deltas, *new_ms, *new_vs)
```

```python
import functools
import math

import jax
import jax.numpy as jnp
from jax import lax
from jax.experimental import pallas as pl
from jax.experimental.pallas import tpu as pltpu

F32 = jnp.float32
BF16 = jnp.bfloat16

D_MODEL = 1024
SB_WIDTH = 512
MLA_WIDTH = 512
Q_RANK = 384
KV_RANK = 256
ROPE_DIM = 32
N_HEADS = 8
IN_WIDTH = 5280
EPS = 1e-6
ROPE_BASE = 10000.0
MLA_SCALE = 1.0 / math.sqrt(96.0)

ADAM_LR = 0.001
ADAM_B1 = 0.9
ADAM_B2 = 0.999
ADAM_EPS = 1e-08
ADAM_WD = 0.01
ADAM_STEP = 10

O_SQ, O_SK, O_SV, O_SZ, O_CQ, O_CKV, O_MZ, O_GA, O_GB, O_KR, O_END = (
    0, 512, 1024, 1536, 2048, 2432, 2688, 3200, 4224, 5248, 5504)
W_INT = 5632

N_CHIPS = 4
N_DEV = 8
LANES = 128
PACK_ROWS = (10560, 576, 512, 1024, 1024, 2048)
PACK_TOTAL = sum(PACK_ROWS)
HALF_ROWS = PACK_TOTAL // 2
SV_COLS = 768

ROW_TILE = 256
ATT_TILE = 256
VMEM_LIMIT = 56 * 1024 * 1024

MESH = pl.DeviceIdType.MESH


def _dot(a, b):
    return lax.dot_general(a, b, (((1,), (0,)), ((), ())), preferred_element_type=F32)


def _dot_nt(a, b):
    return lax.dot_general(a, b, (((1,), (1,)), ((), ())), preferred_element_type=F32)


def _dot_tn(a, b):
    return lax.dot_general(a, b, (((0,), (0,)), ((), ())), preferred_element_type=F32)


def _sigmoid(z):
    return 1.0 / (1.0 + jnp.exp(-z))


def _params(sem=None):
    if sem is None:
        return pltpu.CompilerParams(vmem_limit_bytes=VMEM_LIMIT)
    return pltpu.CompilerParams(dimension_semantics=sem, vmem_limit_bytes=VMEM_LIMIT)


def _rows(tm, n):
    return pl.BlockSpec((tm, n), lambda i: (i, 0))


def _whole(shape):
    nd = len(shape)
    return pl.BlockSpec(shape, lambda i: (0,) * nd)


def _sds(shape, dtype):
    return jax.ShapeDtypeStruct(shape, dtype)


def _flip(v, d):
    return 1 - v if d else v


def _gather_call(c_row, w_ada_sh, pack):
    def body(c_ref, wada_ref, pk_ref, mg_ref, cg_ref, gw_ref,
             cv, ssem_c, rsem_c, ssem_m, rsem_m, ssem_w, rsem_w, ssem_f, rsem_f, lsem):
        x, y, c = lax.axis_index("x"), lax.axis_index("y"), lax.axis_index("c")
        me = 4 * x + 2 * y + c
        chip = 2 * x + y
        rel3 = [(1, 0), (0, 1), (1, 1)]

        wcopies = []
        for j, (dx, dy) in enumerate(rel3):
            cp = pltpu.make_async_remote_copy(
                src_ref=pk_ref.at[c], dst_ref=gw_ref.at[chip, c], send_sem=ssem_w.at[j], recv_sem=rsem_w.at[j],
                device_id=(_flip(x, dx), _flip(y, dy), c), device_id_type=MESH)
            cp.start()
            wcopies.append(cp)
        own = pltpu.make_async_copy(pk_ref, gw_ref.at[chip], lsem)
        own.start()

        cv[me] = c_ref[...]
        ccopies = []
        for r in range(1, N_DEV):
            dx, dy, dc = (r >> 2) & 1, (r >> 1) & 1, r & 1
            cp = pltpu.make_async_remote_copy(
                src_ref=c_ref, dst_ref=cv.at[me], send_sem=ssem_c.at[r - 1], recv_sem=rsem_c.at[r - 1],
                device_id=(_flip(x, dx), _flip(y, dy), _flip(c, dc)), device_id_type=MESH)
            cp.start()
            ccopies.append(cp)
        for r in range(1, N_DEV):
            dx, dy, dc = (r >> 2) & 1, (r >> 1) & 1, r & 1
            src = 4 * _flip(x, dx) + 2 * _flip(y, dy) + _flip(c, dc)
            pltpu.make_async_remote_copy(
                src_ref=c_ref, dst_ref=cv.at[src], send_sem=ssem_c.at[r - 1], recv_sem=rsem_c.at[r - 1],
                device_id=(x, y, c), device_id_type=MESH).wait_recv()
        rows = lax.broadcasted_iota(jnp.int32, (N_DEV, D_MODEL), 0)
        call = jnp.zeros((N_DEV, D_MODEL), F32)
        for b in range(N_DEV):
            call = jnp.where(rows == b, jnp.broadcast_to(cv[b], (N_DEV, D_MODEL)), call)
        cg_ref[...] = call

        mg_ref[chip] = lax.dot_general(call, wada_ref[...], (((1,), (0,)), ((), ())),
                                       precision=lax.Precision.HIGHEST, preferred_element_type=F32)
        mcopies = []
        for j, (dx, dy) in enumerate(rel3):
            cp = pltpu.make_async_remote_copy(
                src_ref=mg_ref.at[chip], dst_ref=mg_ref.at[chip], send_sem=ssem_m.at[j], recv_sem=rsem_m.at[j],
                device_id=(_flip(x, dx), _flip(y, dy), c), device_id_type=MESH)
            cp.start()
            mcopies.append(cp)
        for j, (dx, dy) in enumerate(rel3):
            src_chip = 2 * _flip(x, dx) + _flip(y, dy)
            pltpu.make_async_remote_copy(
                src_ref=mg_ref.at[src_chip], dst_ref=mg_ref.at[src_chip], send_sem=ssem_m.at[j],
                recv_sem=rsem_m.at[j], device_id=(x, y, c), device_id_type=MESH).wait_recv()
        fcopies = []
        for j, (dx, dy) in enumerate(rel3):
            src_chip = 2 * _flip(x, dx) + _flip(y, dy)
            pltpu.make_async_remote_copy(
                src_ref=pk_ref.at[c], dst_ref=gw_ref.at[src_chip, c], send_sem=ssem_w.at[j], recv_sem=rsem_w.at[j],
                device_id=(x, y, c), device_id_type=MESH).wait_recv()
            cp = pltpu.make_async_remote_copy(
                src_ref=gw_ref.at[src_chip, c], dst_ref=gw_ref.at[src_chip, c], send_sem=ssem_f.at[j],
                recv_sem=rsem_f.at[j], device_id=(x, y, 1 - c), device_id_type=MESH)
            cp.start()
            fcopies.append(cp)
        for j, (dx, dy) in enumerate(rel3):
            src_chip = 2 * _flip(x, dx) + _flip(y, dy)
            pltpu.make_async_remote_copy(
                src_ref=pk_ref.at[c], dst_ref=gw_ref.at[src_chip, 1 - c], send_sem=ssem_f.at[j],
                recv_sem=rsem_f.at[j], device_id=(x, y, c), device_id_type=MESH).wait_recv()
        for cp in ccopies + mcopies + wcopies + fcopies:
            cp.wait_send()
        own.wait()

    vmem = pl.BlockSpec(memory_space=pltpu.VMEM)
    return pl.pallas_call(
        body, name="gather_fwd",
        out_shape=(_sds((N_CHIPS, N_DEV, 768), F32), _sds((N_DEV, D_MODEL), F32),
                   _sds((N_CHIPS, 2, HALF_ROWS, LANES), BF16)),
        in_specs=[vmem, vmem, vmem], out_specs=(vmem, vmem, vmem),
        scratch_shapes=[
            pltpu.VMEM((N_DEV, 1, D_MODEL), F32),
            pltpu.SemaphoreType.DMA((N_DEV - 1,)), pltpu.SemaphoreType.DMA((N_DEV - 1,)),
            pltpu.SemaphoreType.DMA((3,)), pltpu.SemaphoreType.DMA((3,)),
            pltpu.SemaphoreType.DMA((3,)), pltpu.SemaphoreType.DMA((3,)),
            pltpu.SemaphoreType.DMA((3,)), pltpu.SemaphoreType.DMA((3,)),
            pltpu.SemaphoreType.DMA,
        ],
        compiler_params=_params(),
    )(c_row, w_ada_sh, pack)


SUM_ROWS = 656


def _reduce_call(gpack, sv):
    def body(g_ref, sv_ref, full_ref, svg_ref, land, ssem_g, rsem_g, ssem_s, rsem_s, ssem_x, rsem_x, lsem):
        x, y, c = lax.axis_index("x"), lax.axis_index("y"), lax.axis_index("c")
        me = 4 * x + 2 * y + c
        copies = []
        for r in range(1, N_DEV):
            dx, dy, dc = (r >> 2) & 1, (r >> 1) & 1, r & 1
            tx, ty, tc = _flip(x, dx), _flip(y, dy), _flip(c, dc)
            tgt = 4 * tx + 2 * ty + tc
            cp = pltpu.make_async_remote_copy(
                src_ref=g_ref.at[tgt], dst_ref=land.at[me], send_sem=ssem_g.at[r - 1],
                recv_sem=rsem_g.at[r - 1], device_id=(tx, ty, tc), device_id_type=MESH)
            cp.start()
            copies.append(cp)
            cp = pltpu.make_async_remote_copy(
                src_ref=sv_ref, dst_ref=svg_ref.at[me], send_sem=ssem_s.at[r - 1],
                recv_sem=rsem_s.at[r - 1], device_id=(tx, ty, tc), device_id_type=MESH)
            cp.start()
            copies.append(cp)
        own = pltpu.make_async_copy(g_ref.at[me], land.at[me], lsem)
        own.start()
        svg_ref[me] = sv_ref[...]
        for r in range(1, N_DEV):
            dx, dy, dc = (r >> 2) & 1, (r >> 1) & 1, r & 1
            src = 4 * _flip(x, dx) + 2 * _flip(y, dy) + _flip(c, dc)
            pltpu.make_async_remote_copy(
                src_ref=g_ref.at[src], dst_ref=land.at[src], send_sem=ssem_g.at[r - 1],
                recv_sem=rsem_g.at[r - 1], device_id=(x, y, c), device_id_type=MESH).wait_recv()
            pltpu.make_async_remote_copy(
                src_ref=sv_ref, dst_ref=svg_ref.at[src], send_sem=ssem_s.at[r - 1],
                recv_sem=rsem_s.at[r - 1], device_id=(x, y, c), device_id_type=MESH).wait_recv()
        own.wait()

        def sum_rows(i, carry):
            sl = pl.ds(pl.multiple_of(i * SUM_ROWS, 16), SUM_ROWS)
            acc = land[0, sl, :].astype(F32)
            for d in range(1, N_DEV):
                acc = acc + land[d, sl, :].astype(F32)
            full_ref[c, sl, :] = acc
            return carry

        lax.fori_loop(0, HALF_ROWS // SUM_ROWS, sum_rows, 0)
        swap = pltpu.make_async_remote_copy(
            src_ref=full_ref.at[c], dst_ref=full_ref.at[c], send_sem=ssem_x, recv_sem=rsem_x,
            device_id=(x, y, 1 - c), device_id_type=MESH)
        swap.start()
        pltpu.make_async_remote_copy(
            src_ref=full_ref.at[c], dst_ref=full_ref.at[1 - c], send_sem=ssem_x, recv_sem=rsem_x,
            device_id=(x, y, c), device_id_type=MESH).wait_recv()
        swap.wait_send()
        for cp in copies:
            cp.wait_send()

    vmem = pl.BlockSpec(memory_space=pltpu.VMEM)
    return pl.pallas_call(
        body, name="grad_reduce",
        out_shape=(_sds((2, HALF_ROWS, LANES), F32), _sds((N_DEV, 8, SV_COLS), F32)),
        in_specs=[vmem, vmem], out_specs=(vmem, vmem),
        scratch_shapes=[
            pltpu.VMEM((N_DEV, HALF_ROWS, LANES), BF16),
            pltpu.SemaphoreType.DMA((N_DEV - 1,)), pltpu.SemaphoreType.DMA((N_DEV - 1,)),
            pltpu.SemaphoreType.DMA((N_DEV - 1,)), pltpu.SemaphoreType.DMA((N_DEV - 1,)),
            pltpu.SemaphoreType.DMA, pltpu.SemaphoreType.DMA, pltpu.SemaphoreType.DMA,
        ],
        compiler_params=_params(),
    )(gpack, sv)


def _inproj_call(x, shift, scale, g1, w_int, w_q, w_kv, qg, kvg, cos256, sin256):
    s_len = x.shape[0]
    tm = min(ROW_TILE, s_len)

    def body(x_ref, sh_ref, sc_ref, g1_ref, w_ref, wq_ref, wkv_ref, qg_ref, kvg_ref, cos_ref, sin_ref,
             h_ref, sq_ref, sk_ref, sv_ref, sz_ref, cq_ref, ckv_ref, mz_ref, ga_ref, gb_ref, kpt_ref,
             qn_ref, qp_ref, kn_ref, vv_ref):
        xt = x_ref[...]
        r = lax.rsqrt(jnp.mean(xt * xt, axis=-1, keepdims=True) + EPS)
        h = (xt * r * g1_ref[...]) * (1.0 + sc_ref[...]) + sh_ref[...]
        hb = h.astype(BF16)
        h_ref[...] = hb

        def seg(a, b):
            return _dot(hb, w_ref[:, a:b])

        sq_ref[...] = seg(O_SQ, O_SK).astype(BF16)
        sk_ref[...] = seg(O_SK, O_SV).astype(BF16)
        sv_ref[...] = seg(O_SV, O_SZ).astype(BF16)
        sz_ref[...] = seg(O_SZ, O_CQ)
        mz_ref[...] = seg(O_MZ, O_GA)
        ga_ref[...] = seg(O_GA, O_GB)
        gb_ref[...] = seg(O_GB, O_KR)
        cos = cos_ref[...]
        sin = sin_ref[...]
        kr = seg(O_KR, O_END)
        kpt_ref[...] = (kr[:, :128] * cos[:, :128] + kr[:, 128:] * sin[:, :128]).astype(BF16)

        cq = seg(O_CQ, O_CKV)
        cq_ref[...] = cq
        rq = lax.rsqrt(jnp.mean(cq * cq, axis=-1, keepdims=True) + EPS)
        cqn = (cq * rq * qg_ref[...]).astype(BF16)
        qa = _dot(cqn, wq_ref[...])
        qn_ref[...] = qa[:, :512].astype(BF16)
        qp_ref[...] = (qa[:, 512:768] * cos + qa[:, 768:] * sin).astype(BF16)

        ckv = seg(O_CKV, O_MZ)
        ckv_ref[...] = ckv
        rk = lax.rsqrt(jnp.mean(ckv * ckv, axis=-1, keepdims=True) + EPS)
        ckvn = (ckv * rk * kvg_ref[...]).astype(BF16)
        kva = _dot(ckvn, wkv_ref[...])
        kn_ref[...] = kva[:, :512].astype(BF16)
        vv_ref[...] = kva[:, 512:].astype(BF16)

    outs = [
        (D_MODEL, BF16), (512, BF16), (512, BF16), (512, BF16), (512, F32), (Q_RANK, F32), (KV_RANK, F32),
        (512, F32), (D_MODEL, F32), (D_MODEL, F32), (128, BF16), (512, BF16), (256, BF16), (512, BF16), (512, BF16),
    ]
    return pl.pallas_call(
        body, name="inproj", grid=(s_len // tm,),
        out_shape=tuple(_sds((s_len, n), dt) for n, dt in outs),
        in_specs=[_rows(tm, D_MODEL), _whole((1, D_MODEL)), _whole((1, D_MODEL)), _whole((1, D_MODEL)),
                  _whole((D_MODEL, W_INT)), _whole((Q_RANK, 1024)), _whole((KV_RANK, 1024)),
                  _whole((1, Q_RANK)), _whole((1, KV_RANK)), _rows(tm, 256), _rows(tm, 256)],
        out_specs=tuple(_rows(tm, n) for n, _ in outs),
        compiler_params=_params(("parallel",)),
    )(x, shift, scale, g1, w_int, w_q, w_kv, qg, kvg, cos256, sin256)


def _softplus(z):
    return jnp.maximum(z, 0.0) + jnp.log(1.0 + jnp.exp(-jnp.abs(z)))


def _split_bf16(a):
    hi = a.astype(BF16)
    lo = (a - hi.astype(F32)).astype(BF16)
    return hi, lo


def _sb_fwd_call(q, k, v):
    s_len = q.shape[0]
    t = min(ATT_TILE, s_len)
    nq = s_len // t

    def body(q_ref, k_ref, v_ref, o_ref, lt_ref):
        i = pl.program_id(1)
        q2 = q_ref[...]
        lane = lax.broadcasted_iota(jnp.int32, (1, 128), 1)
        row = lax.broadcasted_iota(jnp.int32, (t, t), 0)
        col = lax.broadcasted_iota(jnp.int32, (t, t), 1)
        later = (row > col).astype(BF16)
        later2 = jnp.concatenate([later, later], axis=0)
        valid = col < row
        hms = [(lane // 64) == hh for hh in range(2)]
        qms = [jnp.where(hm, q2, jnp.zeros_like(q2)) for hm in hms]

        def block(j, carry, diag):
            runs, acc = list(carry[:2]), carry[2]
            off = pl.multiple_of(j * t, t)
            kb = k_ref[pl.ds(off, t), :]
            vb = v_ref[pl.ds(off, t), :]
            ws = []
            for hh in range(2):
                z = _dot_nt(qms[hh], kb)
                lg = -_softplus(z)
                lm = jnp.where(valid, lg, 0.0) if diag else lg
                hi, lo = _split_bf16(lm)
                suf = _dot(jnp.concatenate([hi, lo], axis=1), later2)
                w = jnp.exp(z + lg + suf + runs[hh])
                if diag:
                    w = jnp.where(valid, w, 0.0)
                ws.append(w.astype(BF16))
                runs[hh] = runs[hh] + jnp.sum(lm, axis=1, keepdims=True)
            vstack = jnp.concatenate([jnp.where(hm, vb, jnp.zeros_like(vb)) for hm in hms], axis=0)
            acc = acc + _dot(jnp.concatenate(ws, axis=1), vstack)
            return runs[0], runs[1], acc

        zero = jnp.zeros((t, 1), F32)
        carry = block(i, (zero, zero, jnp.zeros((t, 128), F32)), True)
        carry = lax.fori_loop(1, i + 1, lambda jj, cr: block(i - jj, cr, False), carry)
        lt_ref[0, :, 0:1] = carry[0]
        lt_ref[0, :, 1:2] = carry[1]
        o_ref[...] = carry[2]

    return pl.pallas_call(
        body, name="sb_fwd", grid=(4, nq),
        out_shape=(_sds((s_len, SB_WIDTH), F32), _sds((4, s_len, 2), F32)),
        in_specs=[pl.BlockSpec((t, 128), lambda p, i: (i, p)),
                  pl.BlockSpec((s_len, 128), lambda p, i: (0, p)),
                  pl.BlockSpec((s_len, 128), lambda p, i: (0, p))],
        out_specs=(pl.BlockSpec((t, 128), lambda p, i: (i, p)),
                   pl.BlockSpec((1, t, 2), lambda p, i: (p, i, 0))),
        compiler_params=_params(("parallel", "parallel")),
    )(q, k, v)


def _sb_bwd_call(q, k, v, do, lt):
    s_len = q.shape[0]
    t = min(ATT_TILE, s_len)
    nq = s_len // t

    def body(q_ref, k_ref, v_ref, do_ref, lt_ref, dq_ref, dk_ref, dv_ref):
        i = pl.program_id(1)

        @pl.when(i == 0)
        def _():
            dk_ref[...] = jnp.zeros_like(dk_ref)
            dv_ref[...] = jnp.zeros_like(dv_ref)

        q2 = q_ref[...]
        do2 = do_ref[...].astype(BF16)
        lane = lax.broadcasted_iota(jnp.int32, (1, 128), 1)
        row = lax.broadcasted_iota(jnp.int32, (t, t), 0)
        col = lax.broadcasted_iota(jnp.int32, (t, t), 1)
        earlier = (row < col).astype(BF16)
        earlier2 = jnp.concatenate([earlier, earlier], axis=0)
        valid = col < row
        hms = [(lane // 64) == hh for hh in range(2)]
        qms = [jnp.where(hm, q2, jnp.zeros_like(q2)) for hm in hms]
        doms = [jnp.where(hm, do2, jnp.zeros_like(do2)) for hm in hms]
        ltots = [lt_ref[0, :, hh:hh + 1] for hh in range(2)]
        qstack = jnp.concatenate(qms, axis=0)
        dostack = jnp.concatenate(doms, axis=0)

        def block(j, carry, diag):
            lpre, ppre, dq = list(carry[0:2]), list(carry[2:4]), carry[4]
            off = pl.multiple_of(j * t, t)
            kb = k_ref[pl.ds(off, t), :]
            vb = v_ref[pl.ds(off, t), :]
            dzs, avs = [], []
            for hh in range(2):
                z = _dot_nt(qms[hh], kb)
                sp = _softplus(z)
                lg = -sp
                lm = jnp.where(valid, lg, 0.0) if diag else lg
                hi, lo = _split_bf16(lm)
                before = _dot(jnp.concatenate([hi, lo], axis=1), earlier2)
                between = ltots[hh] - (lpre[hh] + before + lm)
                a = jnp.exp(z + lg + between)
                if diag:
                    a = jnp.where(valid, a, 0.0)
                p = a * _dot_nt(doms[hh], vb)
                phi, plo = _split_bf16(p)
                pbefore = ppre[hh] + _dot(jnp.concatenate([phi, plo], axis=1), earlier2)
                sig = jnp.exp(z - sp)
                dz = p - sig * (p + pbefore)
                if diag:
                    dz = jnp.where(valid, dz, 0.0)
                dzs.append(dz.astype(BF16))
                avs.append(a.astype(BF16))
                lpre[hh] = lpre[hh] + jnp.sum(lm, axis=1, keepdims=True)
                ppre[hh] = ppre[hh] + jnp.sum(p, axis=1, keepdims=True)
            kstack = jnp.concatenate([jnp.where(hm, kb, jnp.zeros_like(kb)) for hm in hms], axis=0)
            dq = dq + _dot(jnp.concatenate(dzs, axis=1), kstack)
            dk_ref[pl.ds(off, t), :] += _dot_tn(jnp.concatenate(dzs, axis=0), qstack)
            dv_ref[pl.ds(off, t), :] += _dot_tn(jnp.concatenate(avs, axis=0), dostack)
            return lpre[0], lpre[1], ppre[0], ppre[1], dq

        zero = jnp.zeros((t, 1), F32)
        carry = lax.fori_loop(0, i, lambda j, cr: block(j, cr, False),
                              (zero, zero, zero, zero, jnp.zeros((t, 128), F32)))
        carry = block(i, carry, True)
        dq_ref[...] = carry[4].astype(BF16)

    return pl.pallas_call(
        body, name="sb_bwd", grid=(4, nq),
        out_shape=(_sds((s_len, SB_WIDTH), BF16), _sds((s_len, SB_WIDTH), F32), _sds((s_len, SB_WIDTH), F32)),
        in_specs=[pl.BlockSpec((t, 128), lambda p, i: (i, p)),
                  pl.BlockSpec((s_len, 128), lambda p, i: (0, p)),
                  pl.BlockSpec((s_len, 128), lambda p, i: (0, p)),
                  pl.BlockSpec((t, 128), lambda p, i: (i, p)),
                  pl.BlockSpec((1, t, 2), lambda p, i: (p, i, 0))],
        out_specs=(pl.BlockSpec((t, 128), lambda p, i: (i, p)),
                   pl.BlockSpec((s_len, 128), lambda p, i: (0, p)),
                   pl.BlockSpec((s_len, 128), lambda p, i: (0, p))),
        compiler_params=_params(("parallel", "arbitrary")),
    )(q, k, v, do, lt)


def _mla_fwd_call(qn, qp, kn, kpt, v):
    s_len = qn.shape[0]
    t = min(ATT_TILE, s_len)
    nq = s_len // t

    def body(qn_ref, qp_ref, kn_ref, kpt_ref, v_ref, o_ref, lse_ref):
        i = pl.program_id(1)
        qn2 = qn_ref[...]
        qp2 = qp_ref[...]
        lane256 = lax.broadcasted_iota(jnp.int32, (1, 256), 1)
        lane128 = lax.broadcasted_iota(jnp.int32, (1, 128), 1)
        row = lax.broadcasted_iota(jnp.int32, (t, t), 0)
        col = lax.broadcasted_iota(jnp.int32, (t, t), 1)
        valid = col <= row
        m64s = [(lane256 // 64) == hh for hh in range(4)]
        m32s = [(lane128 // 32) == hh for hh in range(4)]
        qcs = [jnp.concatenate([jnp.where(m64s[hh], qn2, jnp.zeros_like(qn2)),
                                jnp.where(m32s[hh], qp2, jnp.zeros_like(qp2))], axis=1) for hh in range(4)]

        def by_head(vals):
            return jnp.where(m64s[0], vals[0], jnp.where(m64s[1], vals[1], jnp.where(m64s[2], vals[2], vals[3])))

        def block(j, carry, diag):
            ms, ls, acc = list(carry[0:4]), list(carry[4:8]), carry[8]
            off = pl.multiple_of(j * t, t)
            kc = jnp.concatenate([kn_ref[pl.ds(off, t), :], kpt_ref[pl.ds(off, t), :]], axis=1)
            vb = v_ref[pl.ds(off, t), :]
            ps, alphas = [], []
            for hh in range(4):
                s = _dot_nt(qcs[hh], kc) * MLA_SCALE
                if diag:
                    s = jnp.where(valid, s, -1e30)
                mn = jnp.maximum(ms[hh], jnp.max(s, axis=1, keepdims=True))
                p = jnp.exp(s - mn)
                alpha = jnp.exp(ms[hh] - mn)
                ls[hh] = alpha * ls[hh] + jnp.sum(p, axis=1, keepdims=True)
                ms[hh] = mn
                ps.append(p.astype(BF16))
                alphas.append(alpha)
            vstack = jnp.concatenate([jnp.where(m64, vb, jnp.zeros_like(vb)) for m64 in m64s], axis=0)
            acc = by_head(alphas) * acc + _dot(jnp.concatenate(ps, axis=1), vstack)
            return (*ms, *ls, acc)

        neg = jnp.full((t, 1), -1e30, F32)
        zero = jnp.zeros((t, 1), F32)
        carry = lax.fori_loop(0, i, lambda j, cr: block(j, cr, False),
                              (neg, neg, neg, neg, zero, zero, zero, zero, jnp.zeros((t, 256), F32)))
        carry = block(i, carry, True)
        o_ref[...] = carry[8] / by_head(list(carry[4:8]))
        for hh in range(4):
            lse_ref[0, :, hh:hh + 1] = carry[hh] + jnp.log(carry[4 + hh])

    return pl.pallas_call(
        body, name="mla_fwd", grid=(2, nq),
        out_shape=(_sds((s_len, MLA_WIDTH), F32), _sds((2, s_len, 4), F32)),
        in_specs=[pl.BlockSpec((t, 256), lambda g, i: (i, g)),
                  pl.BlockSpec((t, 128), lambda g, i: (i, g)),
                  pl.BlockSpec((s_len, 256), lambda g, i: (0, g)),
                  pl.BlockSpec((s_len, 128), lambda g, i: (0, 0)),
                  pl.BlockSpec((s_len, 256), lambda g, i: (0, g))],
        out_specs=(pl.BlockSpec((t, 256), lambda g, i: (i, g)),
                   pl.BlockSpec((1, t, 4), lambda g, i: (g, i, 0))),
        compiler_params=_params(("parallel", "parallel")),
    )(qn, qp, kn, kpt, v)


def _mla_bwd_call(qn, qp, kn, kpt, v, o, do, lse):
    s_len = qn.shape[0]
    t = min(ATT_TILE, s_len)
    nq = s_len // t

    def body(qn_ref, qp_ref, kn_ref, kpt_ref, v_ref, o_ref, do_ref, lse_ref,
             dqn_ref, dqp_ref, dkn_ref, dkpt_ref, dv_ref):
        g = pl.program_id(0)
        i = pl.program_id(1)

        @pl.when(i == 0)
        def _():
            dkn_ref[...] = jnp.zeros_like(dkn_ref)
            dv_ref[...] = jnp.zeros_like(dv_ref)

        @pl.when((i == 0) & (g == 0))
        def _():
            dkpt_ref[...] = jnp.zeros_like(dkpt_ref)

        qn2 = qn_ref[...]
        qp2 = qp_ref[...]
        of = o_ref[...]
        dof = do_ref[...]
        dob = dof.astype(BF16)
        prod = dof * of
        lane256 = lax.broadcasted_iota(jnp.int32, (1, 256), 1)
        lane128 = lax.broadcasted_iota(jnp.int32, (1, 128), 1)
        row = lax.broadcasted_iota(jnp.int32, (t, t), 0)
        col = lax.broadcasted_iota(jnp.int32, (t, t), 1)
        valid = col <= row
        m64s = [(lane256 // 64) == hh for hh in range(4)]
        m32s = [(lane128 // 32) == hh for hh in range(4)]
        qcs = [jnp.concatenate([jnp.where(m64s[hh], qn2, jnp.zeros_like(qn2)),
                                jnp.where(m32s[hh], qp2, jnp.zeros_like(qp2))], axis=1) for hh in range(4)]
        doms = [jnp.where(m64, dob, jnp.zeros_like(dob)) for m64 in m64s]
        dsums = [jnp.sum(jnp.where(m64, prod, 0.0), axis=1, keepdims=True) * MLA_SCALE for m64 in m64s]
        lses = [lse_ref[0, :, hh:hh + 1] for hh in range(4)]
        qstack = jnp.concatenate(qcs, axis=0)
        dostack = jnp.concatenate(doms, axis=0)

        def block(j, dqc, diag):
            off = pl.multiple_of(j * t, t)
            knb = kn_ref[pl.ds(off, t), :]
            kpb = kpt_ref[pl.ds(off, t), :]
            vb = v_ref[pl.ds(off, t), :]
            kc = jnp.concatenate([knb, kpb], axis=1)
            dss, pbs = [], []
            for hh in range(4):
                s = _dot_nt(qcs[hh], kc) * MLA_SCALE
                if diag:
                    s = jnp.where(valid, s, -1e30)
                p = jnp.exp(s - lses[hh])
                ds = p * (_dot_nt(doms[hh], vb) * MLA_SCALE - dsums[hh])
                dss.append(ds.astype(BF16))
                pbs.append(p.astype(BF16))
            kstack = jnp.concatenate(
                [jnp.concatenate([jnp.where(m64s[hh], knb, jnp.zeros_like(knb)),
                                  jnp.where(m32s[hh], kpb, jnp.zeros_like(kpb))], axis=1) for hh in range(4)], axis=0)
            dqc = dqc + _dot(jnp.concatenate(dss, axis=1), kstack)
            dkc = _dot_tn(jnp.concatenate(dss, axis=0), qstack)
            dkn_ref[pl.ds(off, t), :] += dkc[:, :256]
            dkpt_ref[pl.ds(off, t), :] += dkc[:, 256:]
            dv_ref[pl.ds(off, t), :] += _dot_tn(jnp.concatenate(pbs, axis=0), dostack)
            return dqc

        dqc = lax.fori_loop(0, i, lambda j, cr: block(j, cr, False), jnp.zeros((t, 384), F32))
        dqc = block(i, dqc, True)
        dqn_ref[...] = dqc[:, :256].astype(BF16)
        dqp_ref[...] = dqc[:, 256:].astype(BF16)

    return pl.pallas_call(
        body, name="mla_bwd", grid=(2, nq),
        out_shape=(_sds((s_len, 512), BF16), _sds((s_len, 256), BF16), _sds((s_len, 512), F32),
                   _sds((s_len, 128), F32), _sds((s_len, 512), F32)),
        in_specs=[pl.BlockSpec((t, 256), lambda g, i: (i, g)),
                  pl.BlockSpec((t, 128), lambda g, i: (i, g)),
                  pl.BlockSpec((s_len, 256), lambda g, i: (0, g)),
                  pl.BlockSpec((s_len, 128), lambda g, i: (0, 0)),
                  pl.BlockSpec((s_len, 256), lambda g, i: (0, g)),
                  pl.BlockSpec((t, 256), lambda g, i: (i, g)),
                  pl.BlockSpec((t, 256), lambda g, i: (i, g)),
                  pl.BlockSpec((1, t, 4), lambda g, i: (g, i, 0))],
        out_specs=(pl.BlockSpec((t, 256), lambda g, i: (i, g)),
                   pl.BlockSpec((t, 128), lambda g, i: (i, g)),
                   pl.BlockSpec((s_len, 256), lambda g, i: (0, g)),
                   pl.BlockSpec((s_len, 128), lambda g, i: (0, 0)),
                   pl.BlockSpec((s_len, 256), lambda g, i: (0, g))),
        compiler_params=_params(("arbitrary", "arbitrary")),
    )(qn, qp, kn, kpt, v, o, do, lse)


Z_CLAMP = 80.0


def _softplus_clamped(z):
    zc = jnp.minimum(z, Z_CLAMP)
    return zc, jnp.log(1.0 + jnp.exp(zc))


def _tri_sum(a, tri, tri2, passes):
    if passes == 1:
        return _dot(a.astype(BF16), tri)
    hi, lo = _split_bf16(a)
    return _dot(jnp.concatenate([hi, lo], axis=1), tri2)


def _sb4_fwd_call(q, k, v):
    s_len = q.shape[0]
    t = min(ATT_TILE, s_len)
    nq = s_len // t

    def body(q_ref, k_ref, v_ref, o_ref, lt_ref):
        i = pl.program_id(1)
        q2 = q_ref[...]
        lane = lax.broadcasted_iota(jnp.int32, (1, 256), 1)
        row = lax.broadcasted_iota(jnp.int32, (t, t), 0)
        col = lax.broadcasted_iota(jnp.int32, (t, t), 1)
        later = (row > col).astype(BF16)
        later2 = jnp.concatenate([later, later], axis=0)
        valid = col < row
        hms = [(lane // 64) == hh for hh in range(4)]
        qms = [jnp.where(hm, q2, jnp.zeros_like(q2)) for hm in hms]

        def block(j, carry, diag):
            runs, acc = list(carry[:4]), carry[4]
            off = pl.multiple_of(j * t, t)
            kb = k_ref[pl.ds(off, t), :]
            vb = v_ref[pl.ds(off, t), :]
            ws = []
            for hh in range(4):
                zc, sp = _softplus_clamped(_dot_nt(qms[hh], kb))
                lm = jnp.where(valid, sp, 0.0) if diag else sp
                suf = _tri_sum(lm, later, later2, 1)
                w = jnp.exp(zc - sp - suf - runs[hh])
                if diag:
                    w = jnp.where(valid, w, 0.0)
                ws.append(w.astype(BF16))
                runs[hh] = runs[hh] + jnp.sum(lm, axis=1, keepdims=True)
            vstack = jnp.concatenate([jnp.where(hm, vb, jnp.zeros_like(vb)) for hm in hms], axis=0)
            acc = acc + _dot(jnp.concatenate(ws, axis=1), vstack)
            return (*runs, acc)

        zero = jnp.zeros((t, 1), F32)
        carry = block(i, (zero, zero, zero, zero, jnp.zeros((t, 256), F32)), True)
        carry = lax.fori_loop(1, i + 1, lambda jj, cr: block(i - jj, cr, False), carry)
        for hh in range(4):
            lt_ref[0, :, hh:hh + 1] = carry[hh]
        o_ref[...] = carry[4]

    return pl.pallas_call(
        body, name="sb_fwd", grid=(2, nq),
        out_shape=(_sds((s_len, SB_WIDTH), F32), _sds((2, s_len, 4), F32)),
        in_specs=[pl.BlockSpec((t, 256), lambda g, i: (i, g)),
                  pl.BlockSpec((s_len, 256), lambda g, i: (0, g)),
                  pl.BlockSpec((s_len, 256), lambda g, i: (0, g))],
        out_specs=(pl.BlockSpec((t, 256), lambda g, i: (i, g)),
                   pl.BlockSpec((1, t, 4), lambda g, i: (g, i, 0))),
        compiler_params=_params(("parallel", "parallel")),
    )(q, k, v)


def _sb4_bwd_call(q, k, v, do, lt):
    s_len = q.shape[0]
    t = min(ATT_TILE, s_len)
    nq = s_len // t

    def body(q_ref, k_ref, v_ref, do_ref, lt_ref, dq_ref, dk_ref, dv_ref):
        i = pl.program_id(1)

        @pl.when(i == 0)
        def _():
            dk_ref[...] = jnp.zeros_like(dk_ref)
            dv_ref[...] = jnp.zeros_like(dv_ref)

        q2 = q_ref[...]
        do2 = do_ref[...].astype(BF16)
        lane = lax.broadcasted_iota(jnp.int32, (1, 256), 1)
        row = lax.broadcasted_iota(jnp.int32, (t, t), 0)
        col = lax.broadcasted_iota(jnp.int32, (t, t), 1)
        earlier = (row < col).astype(BF16)
        earlier2 = jnp.concatenate([earlier, earlier], axis=0)
        later = (row > col).astype(BF16)
        later2 = jnp.concatenate([later, later], axis=0)
        valid = col < row
        hms = [(lane // 64) == hh for hh in range(4)]
        qms = [jnp.where(hm, q2, jnp.zeros_like(q2)) for hm in hms]
        doms = [jnp.where(hm, do2, jnp.zeros_like(do2)) for hm in hms]
        ltots = [lt_ref[0, :, hh:hh + 1] for hh in range(4)]
        q2t = jnp.transpose(q2.astype(F32))
        do2t = jnp.transpose(do_ref[...])
        subl = lax.broadcasted_iota(jnp.int32, (256, 1), 0)
        qtstack = jnp.concatenate(
            [jnp.where((subl // 64) == hh, q2t, 0.0).astype(BF16) for hh in range(4)], axis=1)
        dotstack = jnp.concatenate(
            [jnp.where((subl // 64) == hh, do2t, 0.0).astype(BF16) for hh in range(4)], axis=1)

        def block(j, carry, diag):
            lpre, ppre, dq = list(carry[0:4]), list(carry[4:8]), carry[8]
            off = pl.multiple_of(j * t, t)
            kb = k_ref[pl.ds(off, t), :]
            vb = v_ref[pl.ds(off, t), :]
            dzs, avs = [], []
            for hh in range(4):
                zc, sp = _softplus_clamped(_dot_nt(qms[hh], kb))
                lsig = zc - sp
                lm = jnp.where(valid, sp, 0.0) if diag else sp
                rowsum = jnp.sum(lm, axis=1, keepdims=True)
                between = _tri_sum(lm, later, later2, 1) + ((ltots[hh] - lpre[hh]) - rowsum)
                a = jnp.exp(lsig - between)
                if diag:
                    a = jnp.where(valid, a, 0.0)
                p = a * _dot_nt(doms[hh], vb)
                pbefore = ppre[hh] + _tri_sum(p, earlier, earlier2, 1)
                dz = p - jnp.exp(lsig) * (p + pbefore)
                if diag:
                    dz = jnp.where(valid, dz, 0.0)
                dzs.append(dz.astype(BF16))
                avs.append(a.astype(BF16))
                lpre[hh] = lpre[hh] + rowsum
                ppre[hh] = ppre[hh] + jnp.sum(p, axis=1, keepdims=True)
            kstack = jnp.concatenate([jnp.where(hm, kb, jnp.zeros_like(kb)) for hm in hms], axis=0)
            dq = dq + _dot(jnp.concatenate(dzs, axis=1), kstack)
            dk_ref[:, pl.ds(off, t)] += _dot(qtstack, jnp.concatenate(dzs, axis=0))
            dv_ref[:, pl.ds(off, t)] += _dot(dotstack, jnp.concatenate(avs, axis=0))
            return (*lpre, *ppre, dq)

        zero = jnp.zeros((t, 1), F32)
        carry = lax.fori_loop(0, i, lambda j, cr: block(j, cr, False),
                              (zero,) * 8 + (jnp.zeros((t, 256), F32),))
        carry = block(i, carry, True)
        dq_ref[...] = carry[8].astype(BF16)

    return pl.pallas_call(
        body, name="sb_bwd", grid=(2, nq),
        out_shape=(_sds((s_len, SB_WIDTH), BF16), _sds((SB_WIDTH, s_len), F32), _sds((SB_WIDTH, s_len), F32)),
        in_specs=[pl.BlockSpec((t, 256), lambda g, i: (i, g)),
                  pl.BlockSpec((s_len, 256), lambda g, i: (0, g)),
                  pl.BlockSpec((s_len, 256), lambda g, i: (0, g)),
                  pl.BlockSpec((t, 256), lambda g, i: (i, g)),
                  pl.BlockSpec((1, t, 4), lambda g, i: (g, i, 0))],
        out_specs=(pl.BlockSpec((t, 256), lambda g, i: (i, g)),
                   pl.BlockSpec((256, s_len), lambda g, i: (g, 0)),
                   pl.BlockSpec((256, s_len), lambda g, i: (g, 0))),
        compiler_params=_params(("parallel", "arbitrary")),
    )(q, k, v, do, lt)


def _mla4_fwd_call(qn, qp, kn, kpt, v):
    s_len = qn.shape[0]
    t = min(ATT_TILE, s_len)
    nq = s_len // t

    def body(qn_ref, qp_ref, kn_ref, kpt_ref, v_ref, o_ref, lse_ref):
        i = pl.program_id(1)
        qn2 = qn_ref[...]
        qp2 = qp_ref[...]
        lane256 = lax.broadcasted_iota(jnp.int32, (1, 256), 1)
        lane128 = lax.broadcasted_iota(jnp.int32, (1, 128), 1)
        row = lax.broadcasted_iota(jnp.int32, (t, t), 0)
        col = lax.broadcasted_iota(jnp.int32, (t, t), 1)
        valid = col <= row
        m64s = [(lane256 // 64) == hh for hh in range(4)]
        half = [(lane128 // 64) == u for u in range(2)]
        m32s = [(lane128 // 32) == hh for hh in range(4)]
        qcs = []
        for hh in range(4):
            qpair = qn2[:, 128 * (hh // 2):128 * (hh // 2) + 128]
            qcs.append(jnp.concatenate([jnp.where(half[hh % 2], qpair, jnp.zeros_like(qpair)),
                                        jnp.where(m32s[hh], qp2, jnp.zeros_like(qp2))], axis=1))

        def by_head(vals):
            return jnp.where(m64s[0], vals[0], jnp.where(m64s[1], vals[1], jnp.where(m64s[2], vals[2], vals[3])))

        def block(j, carry, diag):
            ms, ls, acc = list(carry[0:4]), list(carry[4:8]), carry[8]
            off = pl.multiple_of(j * t, t)
            knb = kn_ref[pl.ds(off, t), :]
            kpb = kpt_ref[pl.ds(off, t), :]
            vb = v_ref[pl.ds(off, t), :]
            kcs = [jnp.concatenate([knb[:, 128 * pp:128 * pp + 128], kpb], axis=1) for pp in range(2)]
            ps, alphas = [], []
            for hh in range(4):
                s = _dot_nt(qcs[hh], kcs[hh // 2]) * MLA_SCALE
                if diag:
                    s = jnp.where(valid, s, -1e30)
                mn = jnp.maximum(ms[hh], jnp.max(s, axis=1, keepdims=True))
                p = jnp.exp(s - mn)
                alpha = jnp.exp(ms[hh] - mn)
                ls[hh] = alpha * ls[hh] + jnp.sum(p, axis=1, keepdims=True)
                ms[hh] = mn
                ps.append(p.astype(BF16))
                alphas.append(alpha)
            pvs = []
            for pp in range(2):
                vpair = vb[:, 128 * pp:128 * pp + 128]
                vstack = jnp.concatenate([jnp.where(hf, vpair, jnp.zeros_like(vpair)) for hf in half], axis=0)
                pvs.append(_dot(jnp.concatenate(ps[2 * pp:2 * pp + 2], axis=1), vstack))
            acc = by_head(alphas) * acc + jnp.concatenate(pvs, axis=1)
            return (*ms, *ls, acc)

        neg = jnp.full((t, 1), -1e30, F32)
        zero = jnp.zeros((t, 1), F32)
        carry = lax.fori_loop(0, i, lambda j, cr: block(j, cr, False),
                              (neg,) * 4 + (zero,) * 4 + (jnp.zeros((t, 256), F32),))
        carry = block(i, carry, True)
        o_ref[...] = carry[8] / by_head(list(carry[4:8]))
        for hh in range(4):
            lse_ref[0, :, hh:hh + 1] = carry[hh] + jnp.log(carry[4 + hh])

    return pl.pallas_call(
        body, name="mla_fwd", grid=(2, nq),
        out_shape=(_sds((s_len, MLA_WIDTH), F32), _sds((2, s_len, 4), F32)),
        in_specs=[pl.BlockSpec((t, 256), lambda g, i: (i, g)),
                  pl.BlockSpec((t, 128), lambda g, i: (i, g)),
                  pl.BlockSpec((s_len, 256), lambda g, i: (0, g)),
                  pl.BlockSpec((s_len, 128), lambda g, i: (0, 0)),
                  pl.BlockSpec((s_len, 256), lambda g, i: (0, g))],
        out_specs=(pl.BlockSpec((t, 256), lambda g, i: (i, g)),
                   pl.BlockSpec((1, t, 4), lambda g, i: (g, i, 0))),
        compiler_params=_params(("parallel", "parallel")),
    )(qn, qp, kn, kpt, v)


def _mla4_bwd_call(qn, qp, kn, kpt, v, o, do, lse):
    s_len = qn.shape[0]
    t = min(ATT_TILE, s_len)
    nq = s_len // t

    def body(qn_ref, qp_ref, kn_ref, kpt_ref, v_ref, o_ref, do_ref, lse_ref,
             dqn_ref, dqp_ref, dkn_ref, dkpt_ref, dv_ref):
        g = pl.program_id(0)
        i = pl.program_id(1)

        @pl.when(i == 0)
        def _():
            dkn_ref[...] = jnp.zeros_like(dkn_ref)
            dv_ref[...] = jnp.zeros_like(dv_ref)

        @pl.when((i == 0) & (g == 0))
        def _():
            dkpt_ref[...] = jnp.zeros_like(dkpt_ref)

        qn2 = qn_ref[...]
        qp2 = qp_ref[...]
        dof = do_ref[...]
        dob = dof.astype(BF16)
        prod = dof * o_ref[...]
        lane256 = lax.broadcasted_iota(jnp.int32, (1, 256), 1)
        lane128 = lax.broadcasted_iota(jnp.int32, (1, 128), 1)
        row = lax.broadcasted_iota(jnp.int32, (t, t), 0)
        col = lax.broadcasted_iota(jnp.int32, (t, t), 1)
        valid = col <= row
        m64s = [(lane256 // 64) == hh for hh in range(4)]
        half = [(lane128 // 64) == u for u in range(2)]
        m32s = [(lane128 // 32) == hh for hh in range(4)]
        qcs, doms = [], []
        for hh in range(4):
            sl = slice(128 * (hh // 2), 128 * (hh // 2) + 128)
            qpair = qn2[:, sl]
            dpair = dob[:, sl]
            qcs.append(jnp.concatenate([jnp.where(half[hh % 2], qpair, jnp.zeros_like(qpair)),
                                        jnp.where(m32s[hh], qp2, jnp.zeros_like(qp2))], axis=1))
            doms.append(jnp.where(half[hh % 2], dpair, jnp.zeros_like(dpair)))
        dsums = [jnp.sum(jnp.where(m64, prod, 0.0), axis=1, keepdims=True) * MLA_SCALE for m64 in m64s]
        lses = [lse_ref[0, :, hh:hh + 1] for hh in range(4)]
        qn2t = jnp.transpose(qn2.astype(F32))
        qp2t = jnp.transpose(qp2.astype(F32))
        do2t = jnp.transpose(dof)
        sub128 = lax.broadcasted_iota(jnp.int32, (128, 1), 0)
        qtstacks, dotstacks = [], []
        for pp in range(2):
            qts, dts = [], []
            for u in range(2):
                hh = 2 * pp + u
                qts.append(jnp.concatenate(
                    [jnp.where((sub128 // 64) == u, qn2t[128 * pp:128 * pp + 128, :], 0.0),
                     jnp.where((sub128 // 32) == hh, qp2t, 0.0)], axis=0).astype(BF16))
                dts.append(jnp.where((sub128 // 64) == u, do2t[128 * pp:128 * pp + 128, :], 0.0).astype(BF16))
            qtstacks.append(jnp.concatenate(qts, axis=1))
            dotstacks.append(jnp.concatenate(dts, axis=1))

        def block(j, carry, diag):
            dqn, dqp = carry
            off = pl.multiple_of(j * t, t)
            knb = kn_ref[pl.ds(off, t), :]
            kpb = kpt_ref[pl.ds(off, t), :]
            vb = v_ref[pl.ds(off, t), :]
            dqn_parts = []
            dkp = None
            for pp in range(2):
                sl = slice(128 * pp, 128 * pp + 128)
                knp = knb[:, sl]
                vpair = vb[:, sl]
                kc = jnp.concatenate([knp, kpb], axis=1)
                dss, pbs, kcms = [], [], []
                for u in range(2):
                    hh = 2 * pp + u
                    s = _dot_nt(qcs[hh], kc) * MLA_SCALE
                    if diag:
                        s = jnp.where(valid, s, -1e30)
                    p = jnp.exp(s - lses[hh])
                    ds = p * (_dot_nt(doms[hh], vpair) * MLA_SCALE - dsums[hh])
                    dss.append(ds.astype(BF16))
                    pbs.append(p.astype(BF16))
                    kcms.append(jnp.concatenate([jnp.where(half[u], knp, jnp.zeros_like(knp)),
                                                 jnp.where(m32s[hh], kpb, jnp.zeros_like(kpb))], axis=1))
                dqc = _dot(jnp.concatenate(dss, axis=1), jnp.concatenate(kcms, axis=0))
                dqn_parts.append(dqc[:, :128])
                dqp = dqp + dqc[:, 128:]
                dkc = _dot(qtstacks[pp], jnp.concatenate(dss, axis=0))
                dkn_ref[128 * pp:128 * pp + 128, pl.ds(off, t)] += dkc[:128, :]
                dkp = dkc[128:, :] if dkp is None else dkp + dkc[128:, :]
                dv_ref[128 * pp:128 * pp + 128, pl.ds(off, t)] += _dot(dotstacks[pp], jnp.concatenate(pbs, axis=0))
            dqn = dqn + jnp.concatenate(dqn_parts, axis=1)
            dkpt_ref[:, pl.ds(off, t)] += dkp
            return dqn, dqp

        carry = lax.fori_loop(0, i, lambda j, cr: block(j, cr, False),
                              (jnp.zeros((t, 256), F32), jnp.zeros((t, 128), F32)))
        dqn, dqp = block(i, carry, True)
        dqn_ref[...] = dqn.astype(BF16)
        dqp_ref[...] = dqp.astype(BF16)

    return pl.pallas_call(
        body, name="mla_bwd", grid=(2, nq),
        out_shape=(_sds((s_len, 512), BF16), _sds((s_len, 256), BF16), _sds((512, s_len), F32),
                   _sds((128, s_len), F32), _sds((512, s_len), F32)),
        in_specs=[pl.BlockSpec((t, 256), lambda g, i: (i, g)),
                  pl.BlockSpec((t, 128), lambda g, i: (i, g)),
                  pl.BlockSpec((s_len, 256), lambda g, i: (0, g)),
                  pl.BlockSpec((s_len, 128), lambda g, i: (0, 0)),
                  pl.BlockSpec((s_len, 256), lambda g, i: (0, g)),
                  pl.BlockSpec((t, 256), lambda g, i: (i, g)),
                  pl.BlockSpec((t, 256), lambda g, i: (i, g)),
                  pl.BlockSpec((1, t, 4), lambda g, i: (g, i, 0))],
        out_specs=(pl.BlockSpec((t, 256), lambda g, i: (i, g)),
                   pl.BlockSpec((t, 128), lambda g, i: (i, g)),
                   pl.BlockSpec((256, s_len), lambda g, i: (g, 0)),
                   pl.BlockSpec((128, s_len), lambda g, i: (0, 0)),
                   pl.BlockSpec((256, s_len), lambda g, i: (g, 0))),
        compiler_params=_params(("arbitrary", "arbitrary")),
    )(qn, qp, kn, kpt, v, o, do, lse)


def _post_call(x, tgt, oa, ob, sz, mz, ga, gb, gate, gf, wa, wb, wo, wat, wbt, wot):
    s_len = x.shape[0]
    tm = min(ROW_TILE, s_len)

    def body(x_ref, t_ref, oa_ref, ob_ref, sz_ref, mz_ref, ga_ref, gb_ref, gate_ref, gf_ref,
             wa_ref, wb_ref, wo_ref, wat_ref, wbt_ref, wot_ref,
             dx2_ref, doa_ref, dob_ref, dsz_ref, dmz_ref, dga_ref, dgb_ref,
             dwo_ref, dwa_ref, dwb_ref, dgf_ref, dgate_ref, loss_ref):
        @pl.when(pl.program_id(0) == 0)
        def _():
            dwo_ref[...] = jnp.zeros_like(dwo_ref)
            dwa_ref[...] = jnp.zeros_like(dwa_ref)
            dwb_ref[...] = jnp.zeros_like(dwb_ref)
            dgf_ref[...] = jnp.zeros_like(dgf_ref)
            dgate_ref[...] = jnp.zeros_like(dgate_ref)
            loss_ref[...] = jnp.zeros_like(loss_ref)

        gate = gate_ref[...]
        gf = gf_ref[...]
        oa = oa_ref[...]
        ob = ob_ref[...]
        sz = sz_ref[...]
        mz = mz_ref[...]
        sa = _sigmoid(sz)
        sb = _sigmoid(mz)
        silu_a = sz * sa
        silu_b = mz * sb
        ua = (oa * silu_a).astype(BF16)
        ub = (ob * silu_b).astype(BF16)
        ya = _dot(ua, wa_ref[...])
        yb = _dot(ub, wb_ref[...])
        sga = _sigmoid(ga_ref[...])
        sgb = _sigmoid(gb_ref[...])
        merged = (sga * ya + sgb * yb).astype(BF16)
        out = _dot(merged, wo_ref[...])
        x2 = x_ref[...] + gate * out
        r2 = lax.rsqrt(jnp.mean(x2 * x2, axis=-1, keepdims=True) + EPS)
        xhat = x2 * r2
        err = xhat * gf - t_ref[...]
        loss_ref[...] += 0.5 * jnp.sum(jnp.sum(err * err, axis=1, keepdims=True), axis=0, keepdims=True) / D_MODEL
        dy = err * (1.0 / D_MODEL)
        dgf_ref[...] += jnp.sum(dy * xhat, axis=0, keepdims=True)
        dxhat = dy * gf
        dx2 = r2 * (dxhat - xhat * jnp.mean(dxhat * xhat, axis=-1, keepdims=True))
        dx2_ref[...] = dx2
        dgate_ref[...] += jnp.sum(dx2 * out, axis=0, keepdims=True)
        dout = (dx2 * gate).astype(BF16)
        dmerged = _dot(dout, wot_ref[...])
        dwo_ref[...] += _dot_tn(merged, dout)
        dya = dmerged * sga
        dyb = dmerged * sgb
        dga_ref[...] = (dya * ya * (1.0 - sga)).astype(BF16)
        dgb_ref[...] = (dyb * yb * (1.0 - sgb)).astype(BF16)
        dyab = dya.astype(BF16)
        dybb = dyb.astype(BF16)
        dua = _dot(dyab, wat_ref[...])
        dub = _dot(dybb, wbt_ref[...])
        dwa_ref[...] += _dot_tn(ua, dyab)
        dwb_ref[...] += _dot_tn(ub, dybb)
        doa_ref[...] = dua * silu_a
        dob_ref[...] = dub * silu_b
        dsz_ref[...] = (dua * oa * (sa * (1.0 + sz * (1.0 - sa)))).astype(BF16)
        dmz_ref[...] = (dub * ob * (sb * (1.0 + mz * (1.0 - sb)))).astype(BF16)

    return pl.pallas_call(
        body, name="post", grid=(s_len // tm,),
        out_shape=(_sds((s_len, D_MODEL), F32), _sds((s_len, 512), F32), _sds((s_len, 512), F32),
                   _sds((s_len, 512), BF16), _sds((s_len, 512), BF16),
                   _sds((s_len, D_MODEL), BF16), _sds((s_len, D_MODEL), BF16),
                   _sds((D_MODEL, D_MODEL), F32), _sds((512, D_MODEL), F32), _sds((512, D_MODEL), F32),
                   _sds((1, D_MODEL), F32), _sds((1, D_MODEL), F32), _sds((1, 128), F32)),
        in_specs=[_rows(tm, D_MODEL), _rows(tm, D_MODEL), _rows(tm, 512), _rows(tm, 512), _rows(tm, 512),
                  _rows(tm, 512), _rows(tm, D_MODEL), _rows(tm, D_MODEL), _whole((1, D_MODEL)), _whole((1, D_MODEL)),
                  _whole((512, D_MODEL)), _whole((512, D_MODEL)), _whole((D_MODEL, D_MODEL)),
                  _whole((D_MODEL, 512)), _whole((D_MODEL, 512)), _whole((D_MODEL, D_MODEL))],
        out_specs=(_rows(tm, D_MODEL), _rows(tm, 512), _rows(tm, 512), _rows(tm, 512), _rows(tm, 512),
                   _rows(tm, D_MODEL), _rows(tm, D_MODEL),
                   _whole((D_MODEL, D_MODEL)), _whole((512, D_MODEL)), _whole((512, D_MODEL)),
                   _whole((1, D_MODEL)), _whole((1, D_MODEL)), _whole((1, 128))),
        compiler_params=_params(("arbitrary",)),
    )(x, tgt, oa, ob, sz, mz, ga, gb, gate, gf, wa, wb, wo, wat, wbt, wot)


def _bwdprep_call(dsq, dsk, dsv, dsz, dqn, dqp, dkn, dvv, dkpt, dmz, dga, dgb, cq, ckv, cos256, sin256,
                  qg, kvg, wqt, wkvt):
    s_len = cq.shape[0]
    tm = min(ROW_TILE, s_len)

    def body(dsq_ref, dsk_ref, dsv_ref, dsz_ref, dqn_ref, dqp_ref, dkn_ref, dvv_ref, dkpt_ref, dmz_ref,
             dga_ref, dgb_ref, cq_ref, ckv_ref, cos_ref, sin_ref, qg_ref, kvg_ref, wqt_ref, wkvt_ref,
             dp_ref, dwq_ref, dwkv_ref, dqg_ref, dkvg_ref):
        @pl.when(pl.program_id(0) == 0)
        def _():
            dwq_ref[...] = jnp.zeros_like(dwq_ref)
            dwkv_ref[...] = jnp.zeros_like(dwkv_ref)
            dqg_ref[...] = jnp.zeros_like(dqg_ref)
            dkvg_ref[...] = jnp.zeros_like(dkvg_ref)

        cos = cos_ref[...]
        sin = sin_ref[...]
        dp_ref[:, O_SQ:O_SK] = dsq_ref[...]
        dp_ref[:, O_SK:O_SV] = dsk_ref[...].astype(BF16)
        dp_ref[:, O_SV:O_SZ] = dsv_ref[...].astype(BF16)
        dp_ref[:, O_SZ:O_CQ] = dsz_ref[...]
        dp_ref[:, O_MZ:O_GA] = dmz_ref[...]
        dp_ref[:, O_GA:O_GB] = dga_ref[...]
        dp_ref[:, O_GB:O_KR] = dgb_ref[...]
        dkp = dkpt_ref[...]
        dp_ref[:, O_KR:O_KR + 128] = (dkp * cos[:, :128]).astype(BF16)
        dp_ref[:, O_KR + 128:O_END] = (dkp * sin[:, :128]).astype(BF16)
        dp_ref[:, O_END:W_INT] = jnp.zeros((tm, W_INT - O_END), BF16)

        cq = cq_ref[...]
        rq = lax.rsqrt(jnp.mean(cq * cq, axis=-1, keepdims=True) + EPS)
        cqh = cq * rq
        qg = qg_ref[...]
        cqn = (cqh * qg).astype(BF16)
        dqp = dqp_ref[...].astype(F32)
        dqa = jnp.concatenate([dqn_ref[...], (dqp * cos).astype(BF16), (dqp * sin).astype(BF16)], axis=1)
        dcqn = _dot(dqa, wqt_ref[...])
        dwq_ref[...] += _dot_tn(cqn, dqa)
        dqg_ref[...] += jnp.sum(dcqn * cqh, axis=0, keepdims=True)
        dh = dcqn * qg
        dcq = rq * (dh - cqh * jnp.mean(dh * cqh, axis=-1, keepdims=True))
        dp_ref[:, O_CQ:O_CKV] = dcq.astype(BF16)

        ckv = ckv_ref[...]
        rk = lax.rsqrt(jnp.mean(ckv * ckv, axis=-1, keepdims=True) + EPS)
        ckh = ckv * rk
        kvg = kvg_ref[...]
        ckvn = (ckh * kvg).astype(BF16)
        dkva = jnp.concatenate([dkn_ref[...].astype(BF16), dvv_ref[...].astype(BF16)], axis=1)
        dckvn = _dot(dkva, wkvt_ref[...])
        dwkv_ref[...] += _dot_tn(ckvn, dkva)
        dkvg_ref[...] += jnp.sum(dckvn * ckh, axis=0, keepdims=True)
        dh2 = dckvn * kvg
        dckv = rk * (dh2 - ckh * jnp.mean(dh2 * ckh, axis=-1, keepdims=True))
        dp_ref[:, O_CKV:O_MZ] = dckv.astype(BF16)

    return pl.pallas_call(
        body, name="bwdprep", grid=(s_len // tm,),
        out_shape=(_sds((s_len, W_INT), BF16), _sds((Q_RANK, 1024), F32), _sds((KV_RANK, 1024), F32),
                   _sds((1, Q_RANK), F32), _sds((1, KV_RANK), F32)),
        in_specs=[_rows(tm, 512), _rows(tm, 512), _rows(tm, 512), _rows(tm, 512), _rows(tm, 512), _rows(tm, 256),
                  _rows(tm, 512), _rows(tm, 512), _rows(tm, 128), _rows(tm, 512), _rows(tm, D_MODEL),
                  _rows(tm, D_MODEL), _rows(tm, Q_RANK), _rows(tm, KV_RANK), _rows(tm, 256), _rows(tm, 256),
                  _whole((1, Q_RANK)), _whole((1, KV_RANK)), _whole((1024, Q_RANK)), _whole((1024, KV_RANK))],
        out_specs=(_rows(tm, W_INT), _whole((Q_RANK, 1024)), _whole((KV_RANK, 1024)),
                   _whole((1, Q_RANK)), _whole((1, KV_RANK))),
        compiler_params=_params(("arbitrary",)),
    )(dsq, dsk, dsv, dsz, dqn, dqp, dkn, dvv, dkpt, dmz, dga, dgb, cq, ckv, cos256, sin256, qg, kvg, wqt, wkvt)


def _dh_call(dproj, w_int_t, x, dx2, scale, g1):
    s_len = x.shape[0]
    tm = min(ROW_TILE, s_len)

    def body(dp_ref, wt_ref, x_ref, dx2_ref, sc_ref, g1_ref, gx_ref, dsh_ref, dsc_ref, dg1_ref):
        @pl.when(pl.program_id(0) == 0)
        def _():
            dsh_ref[...] = jnp.zeros_like(dsh_ref)
            dsc_ref[...] = jnp.zeros_like(dsc_ref)
            dg1_ref[...] = jnp.zeros_like(dg1_ref)

        dh = _dot(dp_ref[...], wt_ref[...])
        xt = x_ref[...]
        r = lax.rsqrt(jnp.mean(xt * xt, axis=-1, keepdims=True) + EPS)
        xh = xt * r
        g1 = g1_ref[...]
        xg = xh * g1
        dsh_ref[...] += jnp.sum(dh, axis=0, keepdims=True)
        dsc_ref[...] += jnp.sum(dh * xg, axis=0, keepdims=True)
        dxg = dh * (1.0 + sc_ref[...])
        dg1_ref[...] += jnp.sum(dxg * xh, axis=0, keepdims=True)
        dxh = dxg * g1
        gx_ref[...] = dx2_ref[...] + r * (dxh - xh * jnp.mean(dxh * xh, axis=-1, keepdims=True))

    return pl.pallas_call(
        body, name="dh", grid=(s_len // tm,),
        out_shape=(_sds((s_len, D_MODEL), F32), _sds((1, D_MODEL), F32), _sds((1, D_MODEL), F32),
                   _sds((1, D_MODEL), F32)),
        in_specs=[_rows(tm, W_INT), _whole((W_INT, D_MODEL)), _rows(tm, D_MODEL), _rows(tm, D_MODEL),
                  _whole((1, D_MODEL)), _whole((1, D_MODEL))],
        out_specs=(_rows(tm, D_MODEL), _whole((1, D_MODEL)), _whole((1, D_MODEL)), _whole((1, D_MODEL))),
        compiler_params=_params(("arbitrary",)),
    )(dproj, w_int_t, x, dx2, scale, g1)


def _dwin_call(h, dproj):
    s_len = h.shape[0]
    tm = min(2 * ROW_TILE, s_len)
    nc = 4
    chunk = W_INT // nc

    def body(h_ref, dp_ref, dw_ref):
        @pl.when(pl.program_id(1) == 0)
        def _():
            dw_ref[...] = jnp.zeros_like(dw_ref)

        dw_ref[...] += _dot_tn(h_ref[...], dp_ref[...])

    return pl.pallas_call(
        body, name="dwin", grid=(nc, s_len // tm),
        out_shape=_sds((D_MODEL, nc * chunk), F32),
        in_specs=[pl.BlockSpec((tm, D_MODEL), lambda c, i: (i, 0)),
                  pl.BlockSpec((tm, chunk), lambda c, i: (i, c))],
        out_specs=pl.BlockSpec((D_MODEL, chunk), lambda c, i: (0, c)),
        compiler_params=_params(("parallel", "arbitrary")),
    )(h, dproj)


def _small_call(svg, ct, dmod_sh):
    def body(sv_ref, ct_ref, dm_ref, tot_ref, gwada_ref):
        acc = sv_ref[0:1, :]
        for d in range(1, N_DEV):
            acc = acc + sv_ref[d:d + 1, :]
        tot_ref[...] = acc
        gwada_ref[...] = lax.dot_general(ct_ref[...], dm_ref[...], (((1,), (0,)), ((), ())),
                                         precision=lax.Precision.HIGHEST, preferred_element_type=F32)

    vmem = pl.BlockSpec(memory_space=pltpu.VMEM)
    return pl.pallas_call(
        body, name="small_grads",
        out_shape=(_sds((1, 8 * SV_COLS), F32), _sds((D_MODEL, 768), F32)),
        in_specs=[vmem, vmem, vmem], out_specs=(vmem, vmem),
        compiler_params=_params(),
    )(svg, ct, dmod_sh)


def _adamw_tile_rows(rows, cols):
    budget = 2 << 20
    if rows * cols * 4 <= budget or rows % 8:
        return rows
    best = 8
    for tr in range(8, rows + 1, 8):
        if rows % tr == 0 and tr * cols * 4 <= budget:
            best = tr
    return best


def _adamw_call(name, w, g, m, v):
    rows, cols = w.shape
    tr = _adamw_tile_rows(rows, cols)

    def body(w_ref, g_ref, m_ref, v_ref, d_ref, nm_ref, nv_ref):
        gg = g_ref[...]
        m2 = ADAM_B1 * m_ref[...] + (1.0 - ADAM_B1) * gg
        v2 = ADAM_B2 * v_ref[...] + (1.0 - ADAM_B2) * (gg * gg)
        m_hat = m2 / (1.0 - ADAM_B1 ** ADAM_STEP)
        v_hat = v2 / (1.0 - ADAM_B2 ** ADAM_STEP)
        d_ref[...] = -ADAM_LR * (m_hat / (jnp.sqrt(v_hat) + ADAM_EPS) + ADAM_WD * w_ref[...])
        nm_ref[...] = m2
        nv_ref[...] = v2

    spec = pl.BlockSpec((tr, cols), lambda i: (i, 0))
    return pl.pallas_call(
        body, name="adamw_" + name, grid=(rows // tr,),
        out_shape=(_sds((rows, cols), F32),) * 3,
        in_specs=[spec] * 4, out_specs=(spec,) * 3,
        compiler_params=_params(("parallel",)),
    )(w, g, m, v)


IN_SHARD = IN_WIDTH // N_CHIPS
HALF_D = D_MODEL // 2
SMALL_ROWS = (576, 512, 1024, 1024, 2048)
SMALL_TOTAL = sum(SMALL_ROWS)
SMALL_HALF = SMALL_TOTAL // 2
SMALL_SUM_ROWS = 432


def _gather2_call(c_row, w_ada_sh, pack_in, pack_small):
    def body(c_ref, wada_ref, pki_ref, pks_ref, mg_ref, cg_ref, gwi_ref, gws_ref,
             cv, ssem_c, rsem_c, ssem_m, rsem_m, ssem_w, rsem_w, ssem_f, rsem_f, lsem):
        x, y, c = lax.axis_index("x"), lax.axis_index("y"), lax.axis_index("c")
        me = 4 * x + 2 * y + c
        chip = 2 * x + y
        rel3 = [(1, 0), (0, 1), (1, 1)]
        packs = [(pki_ref, gwi_ref), (pks_ref, gws_ref)]

        sends = []
        for j, (dx, dy) in enumerate(rel3):
            for a, (pk, gw) in enumerate(packs):
                cp = pltpu.make_async_remote_copy(
                    src_ref=pk.at[c], dst_ref=gw.at[chip, c], send_sem=ssem_w.at[j, a], recv_sem=rsem_w.at[j, a],
                    device_id=(_flip(x, dx), _flip(y, dy), c), device_id_type=MESH)
                cp.start()
                sends.append(cp)
        owns = []
        for a, (pk, gw) in enumerate(packs):
            own = pltpu.make_async_copy(pk, gw.at[chip], lsem.at[a])
            own.start()
            owns.append(own)

        cv[me] = c_ref[...]
        for r in range(1, N_DEV):
            dx, dy, dc = (r >> 2) & 1, (r >> 1) & 1, r & 1
            cp = pltpu.make_async_remote_copy(
                src_ref=c_ref, dst_ref=cv.at[me], send_sem=ssem_c.at[r - 1], recv_sem=rsem_c.at[r - 1],
                device_id=(_flip(x, dx), _flip(y, dy), _flip(c, dc)), device_id_type=MESH)
            cp.start()
            sends.append(cp)
        for r in range(1, N_DEV):
            dx, dy, dc = (r >> 2) & 1, (r >> 1) & 1, r & 1
            src = 4 * _flip(x, dx) + 2 * _flip(y, dy) + _flip(c, dc)
            pltpu.make_async_remote_copy(
                src_ref=c_ref, dst_ref=cv.at[src], send_sem=ssem_c.at[r - 1], recv_sem=rsem_c.at[r - 1],
                device_id=(x, y, c), device_id_type=MESH).wait_recv()
        rows = lax.broadcasted_iota(jnp.int32, (N_DEV, D_MODEL), 0)
        call = jnp.zeros((N_DEV, D_MODEL), F32)
        for b in range(N_DEV):
            call = jnp.where(rows == b, jnp.broadcast_to(cv[b], (N_DEV, D_MODEL)), call)
        cg_ref[...] = call

        mg_ref[chip] = lax.dot_general(call, wada_ref[...], (((1,), (0,)), ((), ())),
                                       precision=lax.Precision.HIGHEST, preferred_element_type=F32)
        for j, (dx, dy) in enumerate(rel3):
            cp = pltpu.make_async_remote_copy(
                src_ref=mg_ref.at[chip], dst_ref=mg_ref.at[chip], send_sem=ssem_m.at[j], recv_sem=rsem_m.at[j],
                device_id=(_flip(x, dx), _flip(y, dy), c), device_id_type=MESH)
            cp.start()
            sends.append(cp)
        for j, (dx, dy) in enumerate(rel3):
            src_chip = 2 * _flip(x, dx) + _flip(y, dy)
            pltpu.make_async_remote_copy(
                src_ref=mg_ref.at[src_chip], dst_ref=mg_ref.at[src_chip], send_sem=ssem_m.at[j],
                recv_sem=rsem_m.at[j], device_id=(x, y, c), device_id_type=MESH).wait_recv()
        for j, (dx, dy) in enumerate(rel3):
            src_chip = 2 * _flip(x, dx) + _flip(y, dy)
            for a, (pk, gw) in enumerate(packs):
                pltpu.make_async_remote_copy(
                    src_ref=pk.at[c], dst_ref=gw.at[src_chip, c], send_sem=ssem_w.at[j, a],
                    recv_sem=rsem_w.at[j, a], device_id=(x, y, c), device_id_type=MESH).wait_recv()
                cp = pltpu.make_async_remote_copy(
                    src_ref=gw.at[src_chip, c], dst_ref=gw.at[src_chip, c], send_sem=ssem_f.at[j, a],
                    recv_sem=rsem_f.at[j, a], device_id=(x, y, 1 - c), device_id_type=MESH)
                cp.start()
                sends.append(cp)
        for j, (dx, dy) in enumerate(rel3):
            src_chip = 2 * _flip(x, dx) + _flip(y, dy)
            for a, (pk, gw) in enumerate(packs):
                pltpu.make_async_remote_copy(
                    src_ref=pk.at[c], dst_ref=gw.at[src_chip, 1 - c], send_sem=ssem_f.at[j, a],
                    recv_sem=rsem_f.at[j, a], device_id=(x, y, c), device_id_type=MESH).wait_recv()
        for cp in sends:
            cp.wait_send()
        for own in owns:
            own.wait()

    vmem = pl.BlockSpec(memory_space=pltpu.VMEM)
    return pl.pallas_call(
        body, name="gather_fwd",
        out_shape=(_sds((N_CHIPS, N_DEV, 768), F32), _sds((N_DEV, D_MODEL), F32),
                   _sds((N_CHIPS, 2, IN_SHARD, HALF_D), BF16), _sds((N_CHIPS, 2, SMALL_HALF, LANES), BF16)),
        in_specs=[vmem, vmem, vmem, vmem], out_specs=(vmem, vmem, vmem, vmem),
        scratch_shapes=[
            pltpu.VMEM((N_DEV, 1, D_MODEL), F32),
            pltpu.SemaphoreType.DMA((N_DEV - 1,)), pltpu.SemaphoreType.DMA((N_DEV - 1,)),
            pltpu.SemaphoreType.DMA((3,)), pltpu.SemaphoreType.DMA((3,)),
            pltpu.SemaphoreType.DMA((3, 2)), pltpu.SemaphoreType.DMA((3, 2)),
            pltpu.SemaphoreType.DMA((3, 2)), pltpu.SemaphoreType.DMA((3, 2)),
            pltpu.SemaphoreType.DMA((2,)),
        ],
        compiler_params=_params(),
    )(c_row, w_ada_sh, pack_in, pack_small)


def _reduce2_call(g_in, g_small, sv):
    def body(gi_ref, gs_ref, sv_ref, fi_ref, fs_ref, svg_ref, land_i, land_s,
             ssem_g, rsem_g, ssem_s, rsem_s, ssem_x, rsem_x, lsem):
        x, y, c = lax.axis_index("x"), lax.axis_index("y"), lax.axis_index("c")
        me = 4 * x + 2 * y + c
        pieces = [(gi_ref, land_i), (gs_ref, land_s)]
        copies = []
        for r in range(1, N_DEV):
            dx, dy, dc = (r >> 2) & 1, (r >> 1) & 1, r & 1
            tx, ty, tc = _flip(x, dx), _flip(y, dy), _flip(c, dc)
            tgt = 4 * tx + 2 * ty + tc
            for a, (g, land) in enumerate(pieces):
                cp = pltpu.make_async_remote_copy(
                    src_ref=g.at[tgt], dst_ref=land.at[me], send_sem=ssem_g.at[r - 1, a],
                    recv_sem=rsem_g.at[r - 1, a], device_id=(tx, ty, tc), device_id_type=MESH)
                cp.start()
                copies.append(cp)
            cp = pltpu.make_async_remote_copy(
                src_ref=sv_ref, dst_ref=svg_ref.at[me], send_sem=ssem_s.at[r - 1],
                recv_sem=rsem_s.at[r - 1], device_id=(tx, ty, tc), device_id_type=MESH)
            cp.start()
            copies.append(cp)
        owns = []
        for a, (g, land) in enumerate(pieces):
            own = pltpu.make_async_copy(g.at[me], land.at[me], lsem.at[a])
            own.start()
            owns.append(own)
        svg_ref[me] = sv_ref[...]
        for r in range(1, N_DEV):
            dx, dy, dc = (r >> 2) & 1, (r >> 1) & 1, r & 1
            src = 4 * _flip(x, dx) + 2 * _flip(y, dy) + _flip(c, dc)
            for a, (g, land) in enumerate(pieces):
                pltpu.make_async_remote_copy(
                    src_ref=g.at[src], dst_ref=land.at[src], send_sem=ssem_g.at[r - 1, a],
                    recv_sem=rsem_g.at[r - 1, a], device_id=(x, y, c), device_id_type=MESH).wait_recv()
            pltpu.make_async_remote_copy(
                src_ref=sv_ref, dst_ref=svg_ref.at[src], send_sem=ssem_s.at[r - 1],
                recv_sem=rsem_s.at[r - 1], device_id=(x, y, c), device_id_type=MESH).wait_recv()
        for own in owns:
            own.wait()

        for qd in range(HALF_D // LANES):
            sl = slice(LANES * qd, LANES * qd + LANES)
            acc = land_i[0, :, sl].astype(F32)
            for d in range(1, N_DEV):
                acc = acc + land_i[d, :, sl].astype(F32)
            fi_ref[c, :, sl] = acc

        def sum_rows(i, carry):
            sl = pl.ds(pl.multiple_of(i * SMALL_SUM_ROWS, 16), SMALL_SUM_ROWS)
            acc = land_s[0, sl, :].astype(F32)
            for d in range(1, N_DEV):
                acc = acc + land_s[d, sl, :].astype(F32)
            fs_ref[c, sl, :] = acc
            return carry

        lax.fori_loop(0, SMALL_HALF // SMALL_SUM_ROWS, sum_rows, 0)
        swaps = []
        for a, f in enumerate((fi_ref, fs_ref)):
            cp = pltpu.make_async_remote_copy(
                src_ref=f.at[c], dst_ref=f.at[c], send_sem=ssem_x.at[a], recv_sem=rsem_x.at[a],
                device_id=(x, y, 1 - c), device_id_type=MESH)
            cp.start()
            swaps.append(cp)
        for a, f in enumerate((fi_ref, fs_ref)):
            pltpu.make_async_remote_copy(
                src_ref=f.at[c], dst_ref=f.at[1 - c], send_sem=ssem_x.at[a], recv_sem=rsem_x.at[a],
                device_id=(x, y, c), device_id_type=MESH).wait_recv()
        for cp in swaps + copies:
            cp.wait_send()

    vmem = pl.BlockSpec(memory_space=pltpu.VMEM)
    return pl.pallas_call(
        body, name="grad_reduce",
        out_shape=(_sds((2, IN_SHARD, HALF_D), F32), _sds((2, SMALL_HALF, LANES), F32),
                   _sds((N_DEV, 8, SV_COLS), F32)),
        in_specs=[vmem, vmem, vmem], out_specs=(vmem, vmem, vmem),
        scratch_shapes=[
            pltpu.VMEM((N_DEV, IN_SHARD, HALF_D), BF16), pltpu.VMEM((N_DEV, SMALL_HALF, LANES), BF16),
            pltpu.SemaphoreType.DMA((N_DEV - 1, 2)), pltpu.SemaphoreType.DMA((N_DEV - 1, 2)),
            pltpu.SemaphoreType.DMA((N_DEV - 1,)), pltpu.SemaphoreType.DMA((N_DEV - 1,)),
            pltpu.SemaphoreType.DMA((2,)), pltpu.SemaphoreType.DMA((2,)), pltpu.SemaphoreType.DMA((2,)),
        ],
        compiler_params=_params(),
    )(g_in, g_small, sv)


def _dwin_t_call(h, dproj):
    s_len = h.shape[0]
    tm = min(2 * ROW_TILE, s_len)
    nrow = s_len // tm
    nc = 4
    chunk = W_INT // nc

    def body(h_ref, dp_ref, dw_ref, acc):
        i = pl.program_id(1)

        @pl.when(i == 0)
        def _():
            acc[...] = jnp.zeros_like(acc)

        acc[...] += _dot_tn(dp_ref[...], h_ref[...])

        @pl.when(i == nrow - 1)
        def _():
            dw_ref[...] = acc[...].astype(BF16)

    return pl.pallas_call(
        body, name="dwin", grid=(nc, nrow),
        out_shape=_sds((W_INT, D_MODEL), BF16),
        in_specs=[pl.BlockSpec((tm, D_MODEL), lambda c, i: (i, 0)),
                  pl.BlockSpec((tm, chunk), lambda c, i: (i, c))],
        out_specs=pl.BlockSpec((chunk, D_MODEL), lambda c, i: (c, 0)),
        scratch_shapes=[pltpu.VMEM((chunk, D_MODEL), F32)],
        compiler_params=_params(("parallel", "arbitrary")),
    )(h, dproj)


def _swap_rows(w, group):
    r, n = w.shape
    return w.reshape(r // group, 2, group // 2, n)[:, ::-1].reshape(r, n)


def _internal_weights_t(w_in_t, w_uq, w_ukv):
    krot_t = w_in_t[2688:2720]
    w_int_t = jnp.concatenate([
        w_in_t[0:512] * jnp.asarray(0.125, w_in_t.dtype), w_in_t[512:2048],
        w_in_t[2048:2432], w_in_t[2432:2688], w_in_t[2720:3232], w_in_t[3232:4256], w_in_t[4256:5280],
        jnp.tile(krot_t, (4, 1)), jnp.tile(_swap_rows(krot_t, 32), (4, 1)),
        jnp.zeros((W_INT - O_END, D_MODEL), w_in_t.dtype)], axis=0)
    uq = w_uq.reshape(Q_RANK, N_HEADS, 96)
    wp = uq[:, :, 64:].reshape(Q_RANK, 256)
    w_q = jnp.concatenate([uq[:, :, :64].reshape(Q_RANK, 512), wp, _swap_halves(wp, 32)], axis=1)
    ukv = w_ukv.reshape(KV_RANK, N_HEADS, 128)
    w_kv = jnp.concatenate([ukv[:, :, :64].reshape(KV_RANK, 512), ukv[:, :, 64:].reshape(KV_RANK, 512)], axis=1)
    return w_int_t, w_q, w_kv


def _true_weight_grads_t(dwi_t, dwq, dwkv):
    dkr = dwi_t[O_KR:O_KR + 128].astype(F32).reshape(4, 32, D_MODEL).sum(axis=0)
    dkr_sw = dwi_t[O_KR + 128:O_END].astype(F32).reshape(4, 32, D_MODEL).sum(axis=0)
    dkrot_t = (dkr + _swap_rows(dkr_sw, 32)).astype(dwi_t.dtype)
    g_in_t = jnp.concatenate([
        dwi_t[0:512] * jnp.asarray(0.125, dwi_t.dtype), dwi_t[512:2048], dwi_t[O_CQ:O_CKV], dwi_t[O_CKV:O_MZ],
        dkrot_t, dwi_t[O_MZ:O_GA], dwi_t[O_GA:O_GB], dwi_t[O_GB:O_KR]], axis=0)
    dwp = dwq[:, 512:768] + _swap_halves(dwq[:, 768:1024], 32)
    g_uq = jnp.concatenate([dwq[:, :512].reshape(Q_RANK, N_HEADS, 64), dwp.reshape(Q_RANK, N_HEADS, 32)],
                           axis=2).reshape(Q_RANK, 768)
    g_ukv = jnp.concatenate([dwkv[:, :512].reshape(KV_RANK, N_HEADS, 64), dwkv[:, 512:].reshape(KV_RANK, N_HEADS, 64)],
                            axis=2).reshape(KV_RANK, 1024)
    return g_in_t, g_uq, g_ukv


def _swap_halves(w, group):
    r, n = w.shape
    return w.reshape(r, n // group, 2, group // 2)[:, :, ::-1, :].reshape(r, n)


def _pack_shards(parts):
    return jnp.concatenate([p.reshape(-1, LANES) for p in parts], axis=0)


def _unpack_chip_major(gw):
    offs = [0]
    for r in PACK_ROWS:
        offs.append(offs[-1] + r)

    def cols(i, rows, shard_cols):
        blk = gw[:, offs[i]:offs[i + 1]].reshape(N_CHIPS, rows, shard_cols)
        return blk.transpose(1, 0, 2).reshape(rows, N_CHIPS * shard_cols)

    w_in = cols(0, D_MODEL, 1320)
    w_uq = cols(1, Q_RANK, 192)
    w_ukv = cols(2, KV_RANK, 256)
    w_a = cols(3, 512, 256)
    w_b = cols(4, 512, 256)
    w_out = gw[:, offs[5]:offs[6]].reshape(D_MODEL, D_MODEL)
    return w_in, w_uq, w_ukv, w_a, w_b, w_out


def _unpack_small(gw):
    offs = [0]
    for r in SMALL_ROWS:
        offs.append(offs[-1] + r)

    def cols(i, rows, shard_cols):
        blk = gw[:, offs[i]:offs[i + 1]].reshape(N_CHIPS, rows, shard_cols)
        return blk.transpose(1, 0, 2).reshape(rows, N_CHIPS * shard_cols)

    return (cols(0, Q_RANK, 192), cols(1, KV_RANK, 256), cols(2, 512, 256), cols(3, 512, 256),
            gw[:, offs[4]:offs[5]].reshape(D_MODEL, D_MODEL))


def _internal_weights(w_in, w_uq, w_ukv):
    krot = w_in[:, 2688:2720]
    w_int = jnp.concatenate([
        w_in[:, 0:512] * jnp.asarray(0.125, w_in.dtype), w_in[:, 512:2048],
        w_in[:, 2048:2432], w_in[:, 2432:2688], w_in[:, 2720:3232], w_in[:, 3232:4256], w_in[:, 4256:5280],
        jnp.tile(krot, (1, 4)), jnp.tile(_swap_halves(krot, 32), (1, 4)),
        jnp.zeros((D_MODEL, W_INT - O_END), w_in.dtype)], axis=1)
    uq = w_uq.reshape(Q_RANK, N_HEADS, 96)
    wp = uq[:, :, 64:].reshape(Q_RANK, 256)
    w_q = jnp.concatenate([uq[:, :, :64].reshape(Q_RANK, 512), wp, _swap_halves(wp, 32)], axis=1)
    ukv = w_ukv.reshape(KV_RANK, N_HEADS, 128)
    w_kv = jnp.concatenate([ukv[:, :, :64].reshape(KV_RANK, 512), ukv[:, :, 64:].reshape(KV_RANK, 512)], axis=1)
    return w_int, w_q, w_kv


def _true_weight_grads(dwi, dwq, dwkv):
    dkr = dwi[:, O_KR:O_KR + 128].reshape(D_MODEL, 4, 32).sum(axis=1)
    dkr_sw = dwi[:, O_KR + 128:O_END].reshape(D_MODEL, 4, 32).sum(axis=1)
    dkrot = dkr + _swap_halves(dkr_sw, 32)
    g_in = jnp.concatenate([
        dwi[:, 0:512] * 0.125, dwi[:, 512:2048], dwi[:, O_CQ:O_CKV], dwi[:, O_CKV:O_MZ], dkrot,
        dwi[:, O_MZ:O_GA], dwi[:, O_GA:O_GB], dwi[:, O_GB:O_KR]], axis=1)
    dwp = dwq[:, 512:768] + _swap_halves(dwq[:, 768:1024], 32)
    g_uq = jnp.concatenate([dwq[:, :512].reshape(Q_RANK, N_HEADS, 64), dwp.reshape(Q_RANK, N_HEADS, 32)],
                           axis=2).reshape(Q_RANK, 768)
    g_ukv = jnp.concatenate([dwkv[:, :512].reshape(KV_RANK, N_HEADS, 64), dwkv[:, 512:].reshape(KV_RANK, N_HEADS, 64)],
                            axis=2).reshape(KV_RANK, 1024)
    return g_in, g_uq, g_ukv


def _chip_major(g, shard_cols):
    r = g.shape[0]
    return g.reshape(r, N_CHIPS, shard_cols).transpose(1, 0, 2).reshape(N_CHIPS, -1, LANES)


def kernel(x, c, positions, w_ada, b_ada, norm_gain, w_in, q_norm_gain, w_uq, kv_norm_gain, w_ukv, w_branch_a, w_branch_b, w_out, final_norm_gain, loss_target, m_w_ada, m_b_ada, m_norm_gain, m_w_in, m_q_norm_gain, m_w_uq, m_kv_norm_gain, m_w_ukv, m_w_branch_a, m_w_branch_b, m_w_out, m_final_norm_gain, v_w_ada, v_b_ada, v_norm_gain, v_w_in, v_q_norm_gain, v_w_uq, v_kv_norm_gain, v_w_ukv, v_w_branch_a, v_w_branch_b, v_w_out, v_final_norm_gain):
    ix, iy, ic = lax.axis_index("x"), lax.axis_index("y"), lax.axis_index("c")
    me = 4 * ix + 2 * iy + ic
    chip = 2 * ix + iy
    xs = x[0]
    tgt = loss_target[0]
    s_len = xs.shape[0]

    w_in_t = jnp.swapaxes(w_in[0], 0, 1)
    pack_in = w_in_t.astype(BF16).reshape(IN_SHARD, 2, HALF_D).transpose(1, 0, 2)
    small_shards = (w_uq[0], w_ukv[0], w_branch_a[0], w_branch_b[0], w_out[0])
    pack_small = _pack_shards([s.astype(BF16) for s in small_shards]).reshape(2, SMALL_HALF, LANES)
    mg, call, gw_in, gw_small = _gather2_call(c, w_ada[0], pack_in, pack_small)
    mod = mg.transpose(1, 0, 2).reshape(N_DEV, 3 * D_MODEL) + b_ada
    mod_me = lax.dynamic_slice_in_dim(mod, me, 1, axis=0)
    shift, scale, gate = mod_me[:, :D_MODEL], mod_me[:, D_MODEL:2 * D_MODEL], mod_me[:, 2 * D_MODEL:]

    f_in_t = gw_in.transpose(0, 2, 1, 3).reshape(IN_WIDTH, D_MODEL)
    f_uq, f_ukv, f_a, f_b, f_out = _unpack_small(gw_small.reshape(N_CHIPS, SMALL_TOTAL, LANES))
    w_int_t, w_q, w_kv = _internal_weights_t(f_in_t, f_uq, f_ukv)
    w_int = w_int_t.T

    inv_freq = ROPE_BASE ** (-jnp.arange(0, ROPE_DIM, 2, dtype=F32) / ROPE_DIM)
    ang = positions[0].astype(F32)[:, None] * inv_freq
    cs, sn = jnp.cos(ang), jnp.sin(ang)
    cos256 = jnp.tile(jnp.concatenate([cs, cs], axis=1), (1, 8))
    sin256 = jnp.tile(jnp.concatenate([-sn, sn], axis=1), (1, 8))

    (h, sq, sk, sv, sz, cq, ckv, mz, ga, gb, kpt, qn, qp, kn, vv) = _inproj_call(
        xs, shift, scale, norm_gain, w_int, w_q, w_kv, q_norm_gain, kv_norm_gain, cos256, sin256)
    oa, lt = _sb4_fwd_call(sq, sk, sv)
    ob, lse = _mla4_fwd_call(qn, qp, kn, kpt, vv)

    gf = final_norm_gain.reshape(1, D_MODEL)
    (dx2, doa, dob, dsz, dmz, dga, dgb, dwo, dwa, dwb, dgf, dgate, loss_p) = _post_call(
        xs, tgt, oa, ob, sz, mz, ga, gb, gate, gf, f_a, f_b, f_out, f_a.T, f_b.T, f_out.T)

    dsq, dsk_t, dsv_t = _sb4_bwd_call(sq, sk, sv, doa, lt)
    dqn, dqp, dkn_t, dkpt_t, dvv_t = _mla4_bwd_call(qn, qp, kn, kpt, vv, ob, dob, lse)
    dsk, dsv, dkn, dkpt, dvv = dsk_t.T, dsv_t.T, dkn_t.T, dkpt_t.T, dvv_t.T

    dproj, dwq, dwkv, dqg, dkvg = _bwdprep_call(
        dsq, dsk, dsv, dsz, dqn, dqp, dkn, dvv, dkpt, dmz, dga, dgb, cq, ckv, cos256, sin256,
        q_norm_gain, kv_norm_gain, w_q.T, w_kv.T)
    grad_x, dshift, dscale, dg1 = _dh_call(dproj, w_int_t, xs, dx2, scale, norm_gain)
    dwi_t = _dwin_t_call(h, dproj)
    g_in_t, g_uq, g_ukv = _true_weight_grads_t(dwi_t, dwq, dwkv)

    g_in_pieces = g_in_t.reshape(N_CHIPS, IN_SHARD, 2, HALF_D).transpose(0, 2, 1, 3).reshape(N_DEV, IN_SHARD, HALF_D)
    g_small = jnp.concatenate([
        _chip_major(g_uq, 192), _chip_major(g_ukv, 256), _chip_major(dwa, 256), _chip_major(dwb, 256),
        dwo.reshape(N_CHIPS, -1, LANES)], axis=1).astype(BF16).reshape(N_DEV, SMALL_HALF, LANES)
    small = jnp.concatenate([
        dshift, dscale, dgate, dg1, dqg, dkvg, dgf, loss_p,
        jnp.zeros((1, 8 * SV_COLS - 5888), F32)], axis=1).reshape(8, SV_COLS)
    full_in, full_small, svg = _reduce2_call(g_in_pieces, g_small, small)
    gs_in_t = full_in.transpose(1, 0, 2).reshape(IN_SHARD, D_MODEL)
    full = full_small.reshape(SMALL_TOTAL, LANES)
    offs = [0]
    for r in SMALL_ROWS:
        offs.append(offs[-1] + r)
    gs_uq = full[offs[0]:offs[1]].reshape(Q_RANK, 192)
    gs_ukv = full[offs[1]:offs[2]].reshape(KV_RANK, 256)
    gs_a = full[offs[2]:offs[3]].reshape(512, 256)
    gs_b = full[offs[3]:offs[4]].reshape(512, 256)
    gs_out = full[offs[4]:offs[5]].reshape(256, D_MODEL)

    svm = svg.reshape(N_DEV, 8 * SV_COLS)
    dmod_sh = lax.dynamic_slice_in_dim(svm[:, :3 * D_MODEL], chip * 768, 768, axis=1)
    tot, gs_ada = _small_call(svm, call.T, dmod_sh)
    g_bada = tot[:, 0:3072]
    g_g1 = tot[:, 3072:4096]
    g_qg = tot[:, 4096:4480]
    g_kvg = tot[:, 4480:4736]
    g_gf = tot[:, 4736:5760]
    loss = tot[0, 5760]

    names = ["w_ada", "b_ada", "norm_gain", "w_in", "q_norm_gain", "w_uq", "kv_norm_gain", "w_ukv",
             "w_branch_a", "w_branch_b", "w_out", "final_norm_gain"]
    ws = [w_ada[0], b_ada, norm_gain, w_in_t, q_norm_gain, w_uq[0], kv_norm_gain, w_ukv[0],
          w_branch_a[0], w_branch_b[0], w_out[0], final_norm_gain.reshape(1, D_MODEL)]
    gs = [gs_ada, g_bada, g_g1, gs_in_t, g_qg, gs_uq, g_kvg, gs_ukv, gs_a, gs_b, gs_out, g_gf]
    ms = [m_w_ada[0], m_b_ada, m_norm_gain, jnp.swapaxes(m_w_in[0], 0, 1), m_q_norm_gain, m_w_uq[0],
          m_kv_norm_gain, m_w_ukv[0], m_w_branch_a[0], m_w_branch_b[0], m_w_out[0],
          m_final_norm_gain.reshape(1, D_MODEL)]
    vs = [v_w_ada[0], v_b_ada, v_norm_gain, jnp.swapaxes(v_w_in[0], 0, 1), v_q_norm_gain, v_w_uq[0],
          v_kv_norm_gain, v_w_ukv[0], v_w_branch_a[0], v_w_branch_b[0], v_w_out[0],
          v_final_norm_gain.reshape(1, D_MODEL)]
    refs = [w_ada, b_ada, norm_gain, w_in, q_norm_gain, w_uq, kv_norm_gain, w_ukv,
            w_branch_a, w_branch_b, w_out, final_norm_gain]
    grads, deltas, new_ms, new_vs = [], [], [], []
    for n, w_, g_, m_, v_, ref in zip(names, ws, gs, ms, vs, refs):
        outs = (g_,) + _adamw_call(n, w_, g_, m_, v_)
        if n == "w_in":
            outs = tuple(jnp.swapaxes(o_, 0, 1) for o_ in outs)
        for lst, o_ in zip((grads, deltas, new_ms, new_vs), outs):
            lst.append(o_.reshape(ref.shape))

    return (loss, grad_x.reshape(x.shape), *grads, *deltas, *new_ms, *new_vs)
```

```python
import functools
import math

import jax
import jax.numpy as jnp
from jax import lax
from jax.experimental import pallas as pl
from jax.experimental.pallas import tpu as pltpu

F32 = jnp.float32
BF16 = jnp.bfloat16

D_MODEL = 1024
SB_WIDTH = 512
MLA_WIDTH = 512
Q_RANK = 384
KV_RANK = 256
ROPE_DIM = 32
N_HEADS = 8
IN_WIDTH = 5280
EPS = 1e-6
ROPE_BASE = 10000.0
MLA_SCALE = 1.0 / math.sqrt(96.0)

ADAM_LR = 0.001
ADAM_B1 = 0.9
ADAM_B2 = 0.999
ADAM_EPS = 1e-08
ADAM_WD = 0.01
ADAM_STEP = 10

O_SQ, O_SK, O_SV, O_SZ, O_CQ, O_CKV, O_MZ, O_GA, O_GB, O_KR, O_END = (
    0, 512, 1024, 1536, 2048, 2432, 2688, 3200, 4224, 5248, 5504)
W_INT = 5632

N_CHIPS = 4
N_DEV = 8
LANES = 128
PACK_ROWS = (10560, 576, 512, 1024, 1024, 2048)
PACK_TOTAL = sum(PACK_ROWS)
HALF_ROWS = PACK_TOTAL // 2
SV_COLS = 768

ROW_TILE = 256
ATT_TILE = 256
VMEM_LIMIT = 56 * 1024 * 1024

MESH = pl.DeviceIdType.MESH


def _dot(a, b):
    return lax.dot_general(a, b, (((1,), (0,)), ((), ())), preferred_element_type=F32)


def _dot_nt(a, b):
    return lax.dot_general(a, b, (((1,), (1,)), ((), ())), preferred_element_type=F32)


def _dot_tn(a, b):
    return lax.dot_general(a, b, (((0,), (0,)), ((), ())), preferred_element_type=F32)


def _sigmoid(z):
    return 1.0 / (1.0 + jnp.exp(-z))


def _params(sem=None):
    if sem is None:
        return pltpu.CompilerParams(vmem_limit_bytes=VMEM_LIMIT)
    return pltpu.CompilerParams(dimension_semantics=sem, vmem_limit_bytes=VMEM_LIMIT)


def _rows(tm, n):
    return pl.BlockSpec((tm, n), lambda i: (i, 0))


def _cols(n, tm):
    return pl.BlockSpec((n, tm), lambda i: (0, i))


def _whole(shape):
    nd = len(shape)
    return pl.BlockSpec(shape, lambda i: (0,) * nd)


def _sds(shape, dtype):
    return jax.ShapeDtypeStruct(shape, dtype)


def _flip(v, d):
    return 1 - v if d else v


def _gather_call(c_row, w_ada_sh, pack):
    def body(c_ref, wada_ref, pk_ref, mg_ref, cg_ref, gw_ref,
             cv, ssem_c, rsem_c, ssem_m, rsem_m, ssem_w, rsem_w, ssem_f, rsem_f, lsem):
        x, y, c = lax.axis_index("x"), lax.axis_index("y"), lax.axis_index("c")
        me = 4 * x + 2 * y + c
        chip = 2 * x + y
        rel3 = [(1, 0), (0, 1), (1, 1)]

        wcopies = []
        for j, (dx, dy) in enumerate(rel3):
            cp = pltpu.make_async_remote_copy(
                src_ref=pk_ref.at[c], dst_ref=gw_ref.at[chip, c], send_sem=ssem_w.at[j], recv_sem=rsem_w.at[j],
                device_id=(_flip(x, dx), _flip(y, dy), c), device_id_type=MESH)
            cp.start()
            wcopies.append(cp)
        own = pltpu.make_async_copy(pk_ref, gw_ref.at[chip], lsem)
        own.start()

        cv[me] = c_ref[...]
        ccopies = []
        for r in range(1, N_DEV):
            dx, dy, dc = (r >> 2) & 1, (r >> 1) & 1, r & 1
            cp = pltpu.make_async_remote_copy(
                src_ref=c_ref, dst_ref=cv.at[me], send_sem=ssem_c.at[r - 1], recv_sem=rsem_c.at[r - 1],
                device_id=(_flip(x, dx), _flip(y, dy), _flip(c, dc)), device_id_type=MESH)
            cp.start()
            ccopies.append(cp)
        for r in range(1, N_DEV):
            dx, dy, dc = (r >> 2) & 1, (r >> 1) & 1, r & 1
            src = 4 * _flip(x, dx) + 2 * _flip(y, dy) + _flip(c, dc)
            pltpu.make_async_remote_copy(
                src_ref=c_ref, dst_ref=cv.at[src], send_sem=ssem_c.at[r - 1], recv_sem=rsem_c.at[r - 1],
                device_id=(x, y, c), device_id_type=MESH).wait_recv()
        rows = lax.broadcasted_iota(jnp.int32, (N_DEV, D_MODEL), 0)
        call = jnp.zeros((N_DEV, D_MODEL), F32)
        for b in range(N_DEV):
            call = jnp.where(rows == b, jnp.broadcast_to(cv[b], (N_DEV, D_MODEL)), call)
        cg_ref[...] = call

        mg_ref[chip] = lax.dot_general(call, wada_ref[...], (((1,), (0,)), ((), ())),
                                       precision=lax.Precision.HIGHEST, preferred_element_type=F32)
        mcopies = []
        for j, (dx, dy) in enumerate(rel3):
            cp = pltpu.make_async_remote_copy(
                src_ref=mg_ref.at[chip], dst_ref=mg_ref.at[chip], send_sem=ssem_m.at[j], recv_sem=rsem_m.at[j],
                device_id=(_flip(x, dx), _flip(y, dy), c), device_id_type=MESH)
            cp.start()
            mcopies.append(cp)
        for j, (dx, dy) in enumerate(rel3):
            src_chip = 2 * _flip(x, dx) + _flip(y, dy)
            pltpu.make_async_remote_copy(
                src_ref=mg_ref.at[src_chip], dst_ref=mg_ref.at[src_chip], send_sem=ssem_m.at[j],
                recv_sem=rsem_m.at[j], device_id=(x, y, c), device_id_type=MESH).wait_recv()
        fcopies = []
        for j, (dx, dy) in enumerate(rel3):
            src_chip = 2 * _flip(x, dx) + _flip(y, dy)
            pltpu.make_async_remote_copy(
                src_ref=pk_ref.at[c], dst_ref=gw_ref.at[src_chip, c], send_sem=ssem_w.at[j], recv_sem=rsem_w.at[j],
                device_id=(x, y, c), device_id_type=MESH).wait_recv()
            cp = pltpu.make_async_remote_copy(
                src_ref=gw_ref.at[src_chip, c], dst_ref=gw_ref.at[src_chip, c], send_sem=ssem_f.at[j],
                recv_sem=rsem_f.at[j], device_id=(x, y, 1 - c), device_id_type=MESH)
            cp.start()
            fcopies.append(cp)
        for j, (dx, dy) in enumerate(rel3):
            src_chip = 2 * _flip(x, dx) + _flip(y, dy)
            pltpu.make_async_remote_copy(
                src_ref=pk_ref.at[c], dst_ref=gw_ref.at[src_chip, 1 - c], send_sem=ssem_f.at[j],
                recv_sem=rsem_f.at[j], device_id=(x, y, c), device_id_type=MESH).wait_recv()
        for cp in ccopies + mcopies + wcopies + fcopies:
            cp.wait_send()
        own.wait()

    vmem = pl.BlockSpec(memory_space=pltpu.VMEM)
    return pl.pallas_call(
        body, name="gather_fwd",
        out_shape=(_sds((N_CHIPS, N_DEV, 768), F32), _sds((N_DEV, D_MODEL), F32),
                   _sds((N_CHIPS, 2, HALF_ROWS, LANES), BF16)),
        in_specs=[vmem, vmem, vmem], out_specs=(vmem, vmem, vmem),
        scratch_shapes=[
            pltpu.VMEM((N_DEV, 1, D_MODEL), F32),
            pltpu.SemaphoreType.DMA((N_DEV - 1,)), pltpu.SemaphoreType.DMA((N_DEV - 1,)),
            pltpu.SemaphoreType.DMA((3,)), pltpu.SemaphoreType.DMA((3,)),
            pltpu.SemaphoreType.DMA((3,)), pltpu.SemaphoreType.DMA((3,)),
            pltpu.SemaphoreType.DMA((3,)), pltpu.SemaphoreType.DMA((3,)),
            pltpu.SemaphoreType.DMA,
        ],
        compiler_params=_params(),
    )(c_row, w_ada_sh, pack)


SUM_ROWS = 656


def _reduce_call(gpack, sv):
    def body(g_ref, sv_ref, full_ref, svg_ref, land, ssem_g, rsem_g, ssem_s, rsem_s, ssem_x, rsem_x, lsem):
        x, y, c = lax.axis_index("x"), lax.axis_index("y"), lax.axis_index("c")
        me = 4 * x + 2 * y + c
        copies = []
        for r in range(1, N_DEV):
            dx, dy, dc = (r >> 2) & 1, (r >> 1) & 1, r & 1
            tx, ty, tc = _flip(x, dx), _flip(y, dy), _flip(c, dc)
            tgt = 4 * tx + 2 * ty + tc
            cp = pltpu.make_async_remote_copy(
                src_ref=g_ref.at[tgt], dst_ref=land.at[me], send_sem=ssem_g.at[r - 1],
                recv_sem=rsem_g.at[r - 1], device_id=(tx, ty, tc), device_id_type=MESH)
            cp.start()
            copies.append(cp)
            cp = pltpu.make_async_remote_copy(
                src_ref=sv_ref, dst_ref=svg_ref.at[me], send_sem=ssem_s.at[r - 1],
                recv_sem=rsem_s.at[r - 1], device_id=(tx, ty, tc), device_id_type=MESH)
            cp.start()
            copies.append(cp)
        own = pltpu.make_async_copy(g_ref.at[me], land.at[me], lsem)
        own.start()
        svg_ref[me] = sv_ref[...]
        for r in range(1, N_DEV):
            dx, dy, dc = (r >> 2) & 1, (r >> 1) & 1, r & 1
            src = 4 * _flip(x, dx) + 2 * _flip(y, dy) + _flip(c, dc)
            pltpu.make_async_remote_copy(
                src_ref=g_ref.at[src], dst_ref=land.at[src], send_sem=ssem_g.at[r - 1],
                recv_sem=rsem_g.at[r - 1], device_id=(x, y, c), device_id_type=MESH).wait_recv()
            pltpu.make_async_remote_copy(
                src_ref=sv_ref, dst_ref=svg_ref.at[src], send_sem=ssem_s.at[r - 1],
                recv_sem=rsem_s.at[r - 1], device_id=(x, y, c), device_id_type=MESH).wait_recv()
        own.wait()

        def sum_rows(i, carry):
            sl = pl.ds(pl.multiple_of(i * SUM_ROWS, 16), SUM_ROWS)
            acc = land[0, sl, :].astype(F32)
            for d in range(1, N_DEV):
                acc = acc + land[d, sl, :].astype(F32)
            full_ref[c, sl, :] = acc
            return carry

        lax.fori_loop(0, HALF_ROWS // SUM_ROWS, sum_rows, 0)
        swap = pltpu.make_async_remote_copy(
            src_ref=full_ref.at[c], dst_ref=full_ref.at[c], send_sem=ssem_x, recv_sem=rsem_x,
            device_id=(x, y, 1 - c), device_id_type=MESH)
        swap.start()
        pltpu.make_async_remote_copy(
            src_ref=full_ref.at[c], dst_ref=full_ref.at[1 - c], send_sem=ssem_x, recv_sem=rsem_x,
            device_id=(x, y, c), device_id_type=MESH).wait_recv()
        swap.wait_send()
        for cp in copies:
            cp.wait_send()

    vmem = pl.BlockSpec(memory_space=pltpu.VMEM)
    return pl.pallas_call(
        body, name="grad_reduce",
        out_shape=(_sds((2, HALF_ROWS, LANES), F32), _sds((N_DEV, 8, SV_COLS), F32)),
        in_specs=[vmem, vmem], out_specs=(vmem, vmem),
        scratch_shapes=[
            pltpu.VMEM((N_DEV, HALF_ROWS, LANES), BF16),
            pltpu.SemaphoreType.DMA((N_DEV - 1,)), pltpu.SemaphoreType.DMA((N_DEV - 1,)),
            pltpu.SemaphoreType.DMA((N_DEV - 1,)), pltpu.SemaphoreType.DMA((N_DEV - 1,)),
            pltpu.SemaphoreType.DMA, pltpu.SemaphoreType.DMA, pltpu.SemaphoreType.DMA,
        ],
        compiler_params=_params(),
    )(gpack, sv)


def _inproj_call(x, shift, scale, g1, w_int, w_q, w_kv, qg, kvg, cos256, sin256):
    s_len = x.shape[0]
    tm = min(ROW_TILE, s_len)

    def body(x_ref, sh_ref, sc_ref, g1_ref, w_ref, wq_ref, wkv_ref, qg_ref, kvg_ref, cos_ref, sin_ref,
             h_ref, sq_ref, sk_ref, sv_ref, sz_ref, cq_ref, ckv_ref, mz_ref, ga_ref, gb_ref, kpt_ref,
             qn_ref, qp_ref, kn_ref, vv_ref):
        xt = x_ref[...]
        r = lax.rsqrt(jnp.mean(xt * xt, axis=-1, keepdims=True) + EPS)
        h = (xt * r * g1_ref[...]) * (1.0 + sc_ref[...]) + sh_ref[...]
        hb = h.astype(BF16)
        h_ref[...] = hb

        def seg(a, b):
            return _dot_nt(hb, w_ref[a:b, :])

        sq_ref[...] = seg(O_SQ, O_SK).astype(BF16)
        sk_ref[...] = seg(O_SK, O_SV).astype(BF16)
        sv_ref[...] = seg(O_SV, O_SZ).astype(BF16)
        sz_ref[...] = seg(O_SZ, O_CQ)
        mz_ref[...] = seg(O_MZ, O_GA)
        ga_ref[...] = seg(O_GA, O_GB)
        gb_ref[...] = seg(O_GB, O_KR)
        cos = cos_ref[...]
        sin = sin_ref[...]
        kr = seg(O_KR, O_END)
        kpt_ref[...] = (kr[:, :128] * cos[:, :128] + kr[:, 128:] * sin[:, :128]).astype(BF16)

        cq = seg(O_CQ, O_CKV)
        cq_ref[...] = cq
        rq = lax.rsqrt(jnp.mean(cq * cq, axis=-1, keepdims=True) + EPS)
        cqn = (cq * rq * qg_ref[...]).astype(BF16)
        qa = _dot(cqn, wq_ref[...])
        qn_ref[...] = qa[:, :512].astype(BF16)
        qp_ref[...] = (qa[:, 512:768] * cos + qa[:, 768:] * sin).astype(BF16)

        ckv = seg(O_CKV, O_MZ)
        ckv_ref[...] = ckv
        rk = lax.rsqrt(jnp.mean(ckv * ckv, axis=-1, keepdims=True) + EPS)
        ckvn = (ckv * rk * kvg_ref[...]).astype(BF16)
        kva = _dot(ckvn, wkv_ref[...])
        kn_ref[...] = kva[:, :512].astype(BF16)
        vv_ref[...] = kva[:, 512:].astype(BF16)

    outs = [
        (D_MODEL, BF16), (512, BF16), (512, BF16), (512, BF16), (512, F32), (Q_RANK, F32), (KV_RANK, F32),
        (512, F32), (D_MODEL, F32), (D_MODEL, F32), (128, BF16), (512, BF16), (256, BF16), (512, BF16), (512, BF16),
    ]
    return pl.pallas_call(
        body, name="inproj", grid=(s_len // tm,),
        out_shape=tuple(_sds((s_len, n), dt) for n, dt in outs),
        in_specs=[_rows(tm, D_MODEL), _whole((1, D_MODEL)), _whole((1, D_MODEL)), _whole((1, D_MODEL)),
                  _whole((W_INT, D_MODEL)), _whole((Q_RANK, 1024)), _whole((KV_RANK, 1024)),
                  _whole((1, Q_RANK)), _whole((1, KV_RANK)), _rows(tm, 256), _rows(tm, 256)],
        out_specs=tuple(_rows(tm, n) for n, _ in outs),
        compiler_params=_params(("parallel",)),
    )(x, shift, scale, g1, w_int, w_q, w_kv, qg, kvg, cos256, sin256)


def _softplus(z):
    return jnp.maximum(z, 0.0) + jnp.log(1.0 + jnp.exp(-jnp.abs(z)))


def _split_bf16(a):
    hi = a.astype(BF16)
    lo = (a - hi.astype(F32)).astype(BF16)
    return hi, lo


def _sb_fwd_call(q, k, v):
    s_len = q.shape[0]
    t = min(ATT_TILE, s_len)
    nq = s_len // t

    def body(q_ref, k_ref, v_ref, o_ref, lt_ref):
        i = pl.program_id(1)
        q2 = q_ref[...]
        lane = lax.broadcasted_iota(jnp.int32, (1, 128), 1)
        row = lax.broadcasted_iota(jnp.int32, (t, t), 0)
        col = lax.broadcasted_iota(jnp.int32, (t, t), 1)
        later = (row > col).astype(BF16)
        later2 = jnp.concatenate([later, later], axis=0)
        valid = col < row
        hms = [(lane // 64) == hh for hh in range(2)]
        qms = [jnp.where(hm, q2, jnp.zeros_like(q2)) for hm in hms]

        def block(j, carry, diag):
            runs, acc = list(carry[:2]), carry[2]
            off = pl.multiple_of(j * t, t)
            kb = k_ref[pl.ds(off, t), :]
            vb = v_ref[pl.ds(off, t), :]
            ws = []
            for hh in range(2):
                z = _dot_nt(qms[hh], kb)
                lg = -_softplus(z)
                lm = jnp.where(valid, lg, 0.0) if diag else lg
                hi, lo = _split_bf16(lm)
                suf = _dot(jnp.concatenate([hi, lo], axis=1), later2)
                w = jnp.exp(z + lg + suf + runs[hh])
                if diag:
                    w = jnp.where(valid, w, 0.0)
                ws.append(w.astype(BF16))
                runs[hh] = runs[hh] + jnp.sum(lm, axis=1, keepdims=True)
            vstack = jnp.concatenate([jnp.where(hm, vb, jnp.zeros_like(vb)) for hm in hms], axis=0)
            acc = acc + _dot(jnp.concatenate(ws, axis=1), vstack)
            return runs[0], runs[1], acc

        zero = jnp.zeros((t, 1), F32)
        carry = block(i, (zero, zero, jnp.zeros((t, 128), F32)), True)
        carry = lax.fori_loop(1, i + 1, lambda jj, cr: block(i - jj, cr, False), carry)
        lt_ref[0, :, 0:1] = carry[0]
        lt_ref[0, :, 1:2] = carry[1]
        o_ref[...] = carry[2]

    return pl.pallas_call(
        body, name="sb_fwd", grid=(4, nq),
        out_shape=(_sds((s_len, SB_WIDTH), F32), _sds((4, s_len, 2), F32)),
        in_specs=[pl.BlockSpec((t, 128), lambda p, i: (i, p)),
                  pl.BlockSpec((s_len, 128), lambda p, i: (0, p)),
                  pl.BlockSpec((s_len, 128), lambda p, i: (0, p))],
        out_specs=(pl.BlockSpec((t, 128), lambda p, i: (i, p)),
                   pl.BlockSpec((1, t, 2), lambda p, i: (p, i, 0))),
        compiler_params=_params(("parallel", "parallel")),
    )(q, k, v)


def _sb_bwd_call(q, k, v, do, lt):
    s_len = q.shape[0]
    t = min(ATT_TILE, s_len)
    nq = s_len // t

    def body(q_ref, k_ref, v_ref, do_ref, lt_ref, dq_ref, dk_ref, dv_ref):
        i = pl.program_id(1)

        @pl.when(i == 0)
        def _():
            dk_ref[...] = jnp.zeros_like(dk_ref)
            dv_ref[...] = jnp.zeros_like(dv_ref)

        q2 = q_ref[...]
        do2 = do_ref[...].astype(BF16)
        lane = lax.broadcasted_iota(jnp.int32, (1, 128), 1)
        row = lax.broadcasted_iota(jnp.int32, (t, t), 0)
        col = lax.broadcasted_iota(jnp.int32, (t, t), 1)
        earlier = (row < col).astype(BF16)
        earlier2 = jnp.concatenate([earlier, earlier], axis=0)
        valid = col < row
        hms = [(lane // 64) == hh for hh in range(2)]
        qms = [jnp.where(hm, q2, jnp.zeros_like(q2)) for hm in hms]
        doms = [jnp.where(hm, do2, jnp.zeros_like(do2)) for hm in hms]
        ltots = [lt_ref[0, :, hh:hh + 1] for hh in range(2)]
        qstack = jnp.concatenate(qms, axis=0)
        dostack = jnp.concatenate(doms, axis=0)

        def block(j, carry, diag):
            lpre, ppre, dq = list(carry[0:2]), list(carry[2:4]), carry[4]
            off = pl.multiple_of(j * t, t)
            kb = k_ref[pl.ds(off, t), :]
            vb = v_ref[pl.ds(off, t), :]
            dzs, avs = [], []
            for hh in range(2):
                z = _dot_nt(qms[hh], kb)
                sp = _softplus(z)
                lg = -sp
                lm = jnp.where(valid, lg, 0.0) if diag else lg
                hi, lo = _split_bf16(lm)
                before = _dot(jnp.concatenate([hi, lo], axis=1), earlier2)
                between = ltots[hh] - (lpre[hh] + before + lm)
                a = jnp.exp(z + lg + between)
                if diag:
                    a = jnp.where(valid, a, 0.0)
                p = a * _dot_nt(doms[hh], vb)
                phi, plo = _split_bf16(p)
                pbefore = ppre[hh] + _dot(jnp.concatenate([phi, plo], axis=1), earlier2)
                sig = jnp.exp(z - sp)
                dz = p - sig * (p + pbefore)
                if diag:
                    dz = jnp.where(valid, dz, 0.0)
                dzs.append(dz.astype(BF16))
                avs.append(a.astype(BF16))
                lpre[hh] = lpre[hh] + jnp.sum(lm, axis=1, keepdims=True)
                ppre[hh] = ppre[hh] + jnp.sum(p, axis=1, keepdims=True)
            kstack = jnp.concatenate([jnp.where(hm, kb, jnp.zeros_like(kb)) for hm in hms], axis=0)
            dq = dq + _dot(jnp.concatenate(dzs, axis=1), kstack)
            dk_ref[pl.ds(off, t), :] += _dot_tn(jnp.concatenate(dzs, axis=0), qstack)
            dv_ref[pl.ds(off, t), :] += _dot_tn(jnp.concatenate(avs, axis=0), dostack)
            return lpre[0], lpre[1], ppre[0], ppre[1], dq

        zero = jnp.zeros((t, 1), F32)
        carry = lax.fori_loop(0, i, lambda j, cr: block(j, cr, False),
                              (zero, zero, zero, zero, jnp.zeros((t, 128), F32)))
        carry = block(i, carry, True)
        dq_ref[...] = carry[4].astype(BF16)

    return pl.pallas_call(
        body, name="sb_bwd", grid=(4, nq),
        out_shape=(_sds((s_len, SB_WIDTH), BF16), _sds((s_len, SB_WIDTH), F32), _sds((s_len, SB_WIDTH), F32)),
        in_specs=[pl.BlockSpec((t, 128), lambda p, i: (i, p)),
                  pl.BlockSpec((s_len, 128), lambda p, i: (0, p)),
                  pl.BlockSpec((s_len, 128), lambda p, i: (0, p)),
                  pl.BlockSpec((t, 128), lambda p, i: (i, p)),
                  pl.BlockSpec((1, t, 2), lambda p, i: (p, i, 0))],
        out_specs=(pl.BlockSpec((t, 128), lambda p, i: (i, p)),
                   pl.BlockSpec((s_len, 128), lambda p, i: (0, p)),
                   pl.BlockSpec((s_len, 128), lambda p, i: (0, p))),
        compiler_params=_params(("parallel", "arbitrary")),
    )(q, k, v, do, lt)


def _mla_fwd_call(qn, qp, kn, kpt, v):
    s_len = qn.shape[0]
    t = min(ATT_TILE, s_len)
    nq = s_len // t

    def body(qn_ref, qp_ref, kn_ref, kpt_ref, v_ref, o_ref, lse_ref):
        i = pl.program_id(1)
        qn2 = qn_ref[...]
        qp2 = qp_ref[...]
        lane256 = lax.broadcasted_iota(jnp.int32, (1, 256), 1)
        lane128 = lax.broadcasted_iota(jnp.int32, (1, 128), 1)
        row = lax.broadcasted_iota(jnp.int32, (t, t), 0)
        col = lax.broadcasted_iota(jnp.int32, (t, t), 1)
        valid = col <= row
        m64s = [(lane256 // 64) == hh for hh in range(4)]
        m32s = [(lane128 // 32) == hh for hh in range(4)]
        qcs = [jnp.concatenate([jnp.where(m64s[hh], qn2, jnp.zeros_like(qn2)),
                                jnp.where(m32s[hh], qp2, jnp.zeros_like(qp2))], axis=1) for hh in range(4)]

        def by_head(vals):
            return jnp.where(m64s[0], vals[0], jnp.where(m64s[1], vals[1], jnp.where(m64s[2], vals[2], vals[3])))

        def block(j, carry, diag):
            ms, ls, acc = list(carry[0:4]), list(carry[4:8]), carry[8]
            off = pl.multiple_of(j * t, t)
            kc = jnp.concatenate([kn_ref[pl.ds(off, t), :], kpt_ref[pl.ds(off, t), :]], axis=1)
            vb = v_ref[pl.ds(off, t), :]
            ps, alphas = [], []
            for hh in range(4):
                s = _dot_nt(qcs[hh], kc) * MLA_SCALE
                if diag:
                    s = jnp.where(valid, s, -1e30)
                mn = jnp.maximum(ms[hh], jnp.max(s, axis=1, keepdims=True))
                p = jnp.exp(s - mn)
                alpha = jnp.exp(ms[hh] - mn)
                ls[hh] = alpha * ls[hh] + jnp.sum(p, axis=1, keepdims=True)
                ms[hh] = mn
                ps.append(p.astype(BF16))
                alphas.append(alpha)
            vstack = jnp.concatenate([jnp.where(m64, vb, jnp.zeros_like(vb)) for m64 in m64s], axis=0)
            acc = by_head(alphas) * acc + _dot(jnp.concatenate(ps, axis=1), vstack)
            return (*ms, *ls, acc)

        neg = jnp.full((t, 1), -1e30, F32)
        zero = jnp.zeros((t, 1), F32)
        carry = lax.fori_loop(0, i, lambda j, cr: block(j, cr, False),
                              (neg, neg, neg, neg, zero, zero, zero, zero, jnp.zeros((t, 256), F32)))
        carry = block(i, carry, True)
        o_ref[...] = carry[8] / by_head(list(carry[4:8]))
        for hh in range(4):
            lse_ref[0, :, hh:hh + 1] = carry[hh] + jnp.log(carry[4 + hh])

    return pl.pallas_call(
        body, name="mla_fwd", grid=(2, nq),
        out_shape=(_sds((s_len, MLA_WIDTH), F32), _sds((2, s_len, 4), F32)),
        in_specs=[pl.BlockSpec((t, 256), lambda g, i: (i, g)),
                  pl.BlockSpec((t, 128), lambda g, i: (i, g)),
                  pl.BlockSpec((s_len, 256), lambda g, i: (0, g)),
                  pl.BlockSpec((s_len, 128), lambda g, i: (0, 0)),
                  pl.BlockSpec((s_len, 256), lambda g, i: (0, g))],
        out_specs=(pl.BlockSpec((t, 256), lambda g, i: (i, g)),
                   pl.BlockSpec((1, t, 4), lambda g, i: (g, i, 0))),
        compiler_params=_params(("parallel", "parallel")),
    )(qn, qp, kn, kpt, v)


def _mla_bwd_call(qn, qp, kn, kpt, v, o, do, lse):
    s_len = qn.shape[0]
    t = min(ATT_TILE, s_len)
    nq = s_len // t

    def body(qn_ref, qp_ref, kn_ref, kpt_ref, v_ref, o_ref, do_ref, lse_ref,
             dqn_ref, dqp_ref, dkn_ref, dkpt_ref, dv_ref):
        g = pl.program_id(0)
        i = pl.program_id(1)

        @pl.when(i == 0)
        def _():
            dkn_ref[...] = jnp.zeros_like(dkn_ref)
            dv_ref[...] = jnp.zeros_like(dv_ref)

        @pl.when((i == 0) & (g == 0))
        def _():
            dkpt_ref[...] = jnp.zeros_like(dkpt_ref)

        qn2 = qn_ref[...]
        qp2 = qp_ref[...]
        of = o_ref[...]
        dof = do_ref[...]
        dob = dof.astype(BF16)
        prod = dof * of
        lane256 = lax.broadcasted_iota(jnp.int32, (1, 256), 1)
        lane128 = lax.broadcasted_iota(jnp.int32, (1, 128), 1)
        row = lax.broadcasted_iota(jnp.int32, (t, t), 0)
        col = lax.broadcasted_iota(jnp.int32, (t, t), 1)
        valid = col <= row
        m64s = [(lane256 // 64) == hh for hh in range(4)]
        m32s = [(lane128 // 32) == hh for hh in range(4)]
        qcs = [jnp.concatenate([jnp.where(m64s[hh], qn2, jnp.zeros_like(qn2)),
                                jnp.where(m32s[hh], qp2, jnp.zeros_like(qp2))], axis=1) for hh in range(4)]
        doms = [jnp.where(m64, dob, jnp.zeros_like(dob)) for m64 in m64s]
        dsums = [jnp.sum(jnp.where(m64, prod, 0.0), axis=1, keepdims=True) * MLA_SCALE for m64 in m64s]
        lses = [lse_ref[0, :, hh:hh + 1] for hh in range(4)]
        qstack = jnp.concatenate(qcs, axis=0)
        dostack = jnp.concatenate(doms, axis=0)

        def block(j, dqc, diag):
            off = pl.multiple_of(j * t, t)
            knb = kn_ref[pl.ds(off, t), :]
            kpb = kpt_ref[pl.ds(off, t), :]
            vb = v_ref[pl.ds(off, t), :]
            kc = jnp.concatenate([knb, kpb], axis=1)
            dss, pbs = [], []
            for hh in range(4):
                s = _dot_nt(qcs[hh], kc) * MLA_SCALE
                if diag:
                    s = jnp.where(valid, s, -1e30)
                p = jnp.exp(s - lses[hh])
                ds = p * (_dot_nt(doms[hh], vb) * MLA_SCALE - dsums[hh])
                dss.append(ds.astype(BF16))
                pbs.append(p.astype(BF16))
            kstack = jnp.concatenate(
                [jnp.concatenate([jnp.where(m64s[hh], knb, jnp.zeros_like(knb)),
                                  jnp.where(m32s[hh], kpb, jnp.zeros_like(kpb))], axis=1) for hh in range(4)], axis=0)
            dqc = dqc + _dot(jnp.concatenate(dss, axis=1), kstack)
            dkc = _dot_tn(jnp.concatenate(dss, axis=0), qstack)
            dkn_ref[pl.ds(off, t), :] += dkc[:, :256]
            dkpt_ref[pl.ds(off, t), :] += dkc[:, 256:]
            dv_ref[pl.ds(off, t), :] += _dot_tn(jnp.concatenate(pbs, axis=0), dostack)
            return dqc

        dqc = lax.fori_loop(0, i, lambda j, cr: block(j, cr, False), jnp.zeros((t, 384), F32))
        dqc = block(i, dqc, True)
        dqn_ref[...] = dqc[:, :256].astype(BF16)
        dqp_ref[...] = dqc[:, 256:].astype(BF16)

    return pl.pallas_call(
        body, name="mla_bwd", grid=(2, nq),
        out_shape=(_sds((s_len, 512), BF16), _sds((s_len, 256), BF16), _sds((s_len, 512), F32),
                   _sds((s_len, 128), F32), _sds((s_len, 512), F32)),
        in_specs=[pl.BlockSpec((t, 256), lambda g, i: (i, g)),
                  pl.BlockSpec((t, 128), lambda g, i: (i, g)),
                  pl.BlockSpec((s_len, 256), lambda g, i: (0, g)),
                  pl.BlockSpec((s_len, 128), lambda g, i: (0, 0)),
                  pl.BlockSpec((s_len, 256), lambda g, i: (0, g)),
                  pl.BlockSpec((t, 256), lambda g, i: (i, g)),
                  pl.BlockSpec((t, 256), lambda g, i: (i, g)),
                  pl.BlockSpec((1, t, 4), lambda g, i: (g, i, 0))],
        out_specs=(pl.BlockSpec((t, 256), lambda g, i: (i, g)),
                   pl.BlockSpec((t, 128), lambda g, i: (i, g)),
                   pl.BlockSpec((s_len, 256), lambda g, i: (0, g)),
                   pl.BlockSpec((s_len, 128), lambda g, i: (0, 0)),
                   pl.BlockSpec((s_len, 256), lambda g, i: (0, g))),
        compiler_params=_params(("arbitrary", "arbitrary")),
    )(qn, qp, kn, kpt, v, o, do, lse)


Z_CLAMP = 80.0


def _softplus_clamped(z):
    zc = jnp.minimum(z, Z_CLAMP)
    return zc, jnp.log(1.0 + jnp.exp(zc))


def _tri_sum(a, tri, tri2, passes):
    if passes == 1:
        return _dot(a.astype(BF16), tri)
    hi, lo = _split_bf16(a)
    return _dot(jnp.concatenate([hi, lo], axis=1), tri2)


def _sb4_fwd_call(q, k, v):
    s_len = q.shape[0]
    t = min(ATT_TILE, s_len)
    nq = s_len // t

    def body(q_ref, k_ref, v_ref, o_ref, lt_ref):
        i = pl.program_id(1)
        q2 = q_ref[...]
        lane = lax.broadcasted_iota(jnp.int32, (1, 256), 1)
        row = lax.broadcasted_iota(jnp.int32, (t, t), 0)
        col = lax.broadcasted_iota(jnp.int32, (t, t), 1)
        later = (row > col).astype(BF16)
        later2 = jnp.concatenate([later, later], axis=0)
        valid = col < row
        hms = [(lane // 64) == hh for hh in range(4)]
        qms = [jnp.where(hm, q2, jnp.zeros_like(q2)) for hm in hms]

        def block(j, carry, diag):
            runs, acc = list(carry[:4]), carry[4]
            off = pl.multiple_of(j * t, t)
            kb = k_ref[pl.ds(off, t), :]
            vb = v_ref[pl.ds(off, t), :]
            ws = []
            for hh in range(4):
                zc, sp = _softplus_clamped(_dot_nt(qms[hh], kb))
                lm = jnp.where(valid, sp, 0.0) if diag else sp
                suf = _tri_sum(lm, later, later2, 1)
                w = jnp.exp(zc - sp - suf - runs[hh])
                if diag:
                    w = jnp.where(valid, w, 0.0)
                ws.append(w.astype(BF16))
                runs[hh] = runs[hh] + jnp.sum(lm, axis=1, keepdims=True)
            vstack = jnp.concatenate([jnp.where(hm, vb, jnp.zeros_like(vb)) for hm in hms], axis=0)
            acc = acc + _dot(jnp.concatenate(ws, axis=1), vstack)
            return (*runs, acc)

        zero = jnp.zeros((t, 1), F32)
        carry = block(i, (zero, zero, zero, zero, jnp.zeros((t, 256), F32)), True)
        carry = lax.fori_loop(1, i + 1, lambda jj, cr: block(i - jj, cr, False), carry)
        for hh in range(4):
            lt_ref[0, :, hh:hh + 1] = carry[hh]
        o_ref[...] = carry[4]

    return pl.pallas_call(
        body, name="sb_fwd", grid=(2, nq),
        out_shape=(_sds((s_len, SB_WIDTH), F32), _sds((2, s_len, 4), F32)),
        in_specs=[pl.BlockSpec((t, 256), lambda g, i: (i, g)),
                  pl.BlockSpec((s_len, 256), lambda g, i: (0, g)),
                  pl.BlockSpec((s_len, 256), lambda g, i: (0, g))],
        out_specs=(pl.BlockSpec((t, 256), lambda g, i: (i, g)),
                   pl.BlockSpec((1, t, 4), lambda g, i: (g, i, 0))),
        compiler_params=_params(("parallel", "parallel")),
    )(q, k, v)


def _sb4_bwd_call(q, k, v, do, lt):
    s_len = q.shape[0]
    t = min(ATT_TILE, s_len)
    nq = s_len // t

    def body(q_ref, k_ref, v_ref, do_ref, lt_ref, dq_ref, dk_ref, dv_ref):
        i = pl.program_id(1)

        @pl.when(i == 0)
        def _():
            dk_ref[...] = jnp.zeros_like(dk_ref)
            dv_ref[...] = jnp.zeros_like(dv_ref)

        q2 = q_ref[...]
        do2 = do_ref[...].astype(BF16)
        lane = lax.broadcasted_iota(jnp.int32, (1, 256), 1)
        row = lax.broadcasted_iota(jnp.int32, (t, t), 0)
        col = lax.broadcasted_iota(jnp.int32, (t, t), 1)
        earlier = (row < col).astype(BF16)
        earlier2 = jnp.concatenate([earlier, earlier], axis=0)
        later = (row > col).astype(BF16)
        later2 = jnp.concatenate([later, later], axis=0)
        valid = col < row
        hms = [(lane // 64) == hh for hh in range(4)]
        qms = [jnp.where(hm, q2, jnp.zeros_like(q2)) for hm in hms]
        doms = [jnp.where(hm, do2, jnp.zeros_like(do2)) for hm in hms]
        ltots = [lt_ref[0, :, hh:hh + 1] for hh in range(4)]
        q2t = jnp.transpose(q2.astype(F32))
        do2t = jnp.transpose(do_ref[...])
        subl = lax.broadcasted_iota(jnp.int32, (256, 1), 0)
        qtstack = jnp.concatenate(
            [jnp.where((subl // 64) == hh, q2t, 0.0).astype(BF16) for hh in range(4)], axis=1)
        dotstack = jnp.concatenate(
            [jnp.where((subl // 64) == hh, do2t, 0.0).astype(BF16) for hh in range(4)], axis=1)

        def block(j, carry, diag):
            lpre, ppre, dq = list(carry[0:4]), list(carry[4:8]), carry[8]
            off = pl.multiple_of(j * t, t)
            kb = k_ref[pl.ds(off, t), :]
            vb = v_ref[pl.ds(off, t), :]
            dzs, avs = [], []
            for hh in range(4):
                zc, sp = _softplus_clamped(_dot_nt(qms[hh], kb))
                lsig = zc - sp
                lm = jnp.where(valid, sp, 0.0) if diag else sp
                rowsum = jnp.sum(lm, axis=1, keepdims=True)
                between = _tri_sum(lm, later, later2, 1) + ((ltots[hh] - lpre[hh]) - rowsum)
                a = jnp.exp(lsig - between)
                if diag:
                    a = jnp.where(valid, a, 0.0)
                p = a * _dot_nt(doms[hh], vb)
                pbefore = ppre[hh] + _tri_sum(p, earlier, earlier2, 1)
                dz = p - jnp.exp(lsig) * (p + pbefore)
                if diag:
                    dz = jnp.where(valid, dz, 0.0)
                dzs.append(dz.astype(BF16))
                avs.append(a.astype(BF16))
                lpre[hh] = lpre[hh] + rowsum
                ppre[hh] = ppre[hh] + jnp.sum(p, axis=1, keepdims=True)
            kstack = jnp.concatenate([jnp.where(hm, kb, jnp.zeros_like(kb)) for hm in hms], axis=0)
            dq = dq + _dot(jnp.concatenate(dzs, axis=1), kstack)
            dk_ref[:, pl.ds(off, t)] += _dot(qtstack, jnp.concatenate(dzs, axis=0))
            dv_ref[:, pl.ds(off, t)] += _dot(dotstack, jnp.concatenate(avs, axis=0))
            return (*lpre, *ppre, dq)

        zero = jnp.zeros((t, 1), F32)
        carry = lax.fori_loop(0, i, lambda j, cr: block(j, cr, False),
                              (zero,) * 8 + (jnp.zeros((t, 256), F32),))
        carry = block(i, carry, True)
        dq_ref[...] = carry[8].astype(BF16)

    return pl.pallas_call(
        body, name="sb_bwd", grid=(2, nq),
        out_shape=(_sds((s_len, SB_WIDTH), BF16), _sds((SB_WIDTH, s_len), F32), _sds((SB_WIDTH, s_len), F32)),
        in_specs=[pl.BlockSpec((t, 256), lambda g, i: (i, g)),
                  pl.BlockSpec((s_len, 256), lambda g, i: (0, g)),
                  pl.BlockSpec((s_len, 256), lambda g, i: (0, g)),
                  pl.BlockSpec((t, 256), lambda g, i: (i, g)),
                  pl.BlockSpec((1, t, 4), lambda g, i: (g, i, 0))],
        out_specs=(pl.BlockSpec((t, 256), lambda g, i: (i, g)),
                   pl.BlockSpec((256, s_len), lambda g, i: (g, 0)),
                   pl.BlockSpec((256, s_len), lambda g, i: (g, 0))),
        compiler_params=_params(("parallel", "arbitrary")),
    )(q, k, v, do, lt)


def _mla4_fwd_call(qn, qp, kn, kpt, v):
    s_len = qn.shape[0]
    t = min(ATT_TILE, s_len)
    nq = s_len // t

    def body(qn_ref, qp_ref, kn_ref, kpt_ref, v_ref, o_ref, lse_ref):
        i = pl.program_id(1)
        qn2 = qn_ref[...]
        qp2 = qp_ref[...]
        lane256 = lax.broadcasted_iota(jnp.int32, (1, 256), 1)
        lane128 = lax.broadcasted_iota(jnp.int32, (1, 128), 1)
        row = lax.broadcasted_iota(jnp.int32, (t, t), 0)
        col = lax.broadcasted_iota(jnp.int32, (t, t), 1)
        valid = col <= row
        m64s = [(lane256 // 64) == hh for hh in range(4)]
        half = [(lane128 // 64) == u for u in range(2)]
        m32s = [(lane128 // 32) == hh for hh in range(4)]
        qcs = []
        for hh in range(4):
            qpair = qn2[:, 128 * (hh // 2):128 * (hh // 2) + 128]
            qcs.append(jnp.concatenate([jnp.where(half[hh % 2], qpair, jnp.zeros_like(qpair)),
                                        jnp.where(m32s[hh], qp2, jnp.zeros_like(qp2))], axis=1))

        def by_head(vals):
            return jnp.where(m64s[0], vals[0], jnp.where(m64s[1], vals[1], jnp.where(m64s[2], vals[2], vals[3])))

        def block(j, carry, diag):
            ms, ls, acc = list(carry[0:4]), list(carry[4:8]), carry[8]
            off = pl.multiple_of(j * t, t)
            knb = kn_ref[pl.ds(off, t), :]
            kpb = kpt_ref[pl.ds(off, t), :]
            vb = v_ref[pl.ds(off, t), :]
            kcs = [jnp.concatenate([knb[:, 128 * pp:128 * pp + 128], kpb], axis=1) for pp in range(2)]
            ps, alphas = [], []
            for hh in range(4):
                s = _dot_nt(qcs[hh], kcs[hh // 2]) * MLA_SCALE
                if diag:
                    s = jnp.where(valid, s, -1e30)
                mn = jnp.maximum(ms[hh], jnp.max(s, axis=1, keepdims=True))
                p = jnp.exp(s - mn)
                alpha = jnp.exp(ms[hh] - mn)
                ls[hh] = alpha * ls[hh] + jnp.sum(p, axis=1, keepdims=True)
                ms[hh] = mn
                ps.append(p.astype(BF16))
                alphas.append(alpha)
            pvs = []
            for pp in range(2):
                vpair = vb[:, 128 * pp:128 * pp + 128]
                vstack = jnp.concatenate([jnp.where(hf, vpair, jnp.zeros_like(vpair)) for hf in half], axis=0)
                pvs.append(_dot(jnp.concatenate(ps[2 * pp:2 * pp + 2], axis=1), vstack))
            acc = by_head(alphas) * acc + jnp.concatenate(pvs, axis=1)
            return (*ms, *ls, acc)

        neg = jnp.full((t, 1), -1e30, F32)
        zero = jnp.zeros((t, 1), F32)
        carry = lax.fori_loop(0, i, lambda j, cr: block(j, cr, False),
                              (neg,) * 4 + (zero,) * 4 + (jnp.zeros((t, 256), F32),))
        carry = block(i, carry, True)
        o_ref[...] = carry[8] / by_head(list(carry[4:8]))
        for hh in range(4):
            lse_ref[0, :, hh:hh + 1] = carry[hh] + jnp.log(carry[4 + hh])

    return pl.pallas_call(
        body, name="mla_fwd", grid=(2, nq),
        out_shape=(_sds((s_len, MLA_WIDTH), F32), _sds((2, s_len, 4), F32)),
        in_specs=[pl.BlockSpec((t, 256), lambda g, i: (i, g)),
                  pl.BlockSpec((t, 128), lambda g, i: (i, g)),
                  pl.BlockSpec((s_len, 256), lambda g, i: (0, g)),
                  pl.BlockSpec((s_len, 128), lambda g, i: (0, 0)),
                  pl.BlockSpec((s_len, 256), lambda g, i: (0, g))],
        out_specs=(pl.BlockSpec((t, 256), lambda g, i: (i, g)),
                   pl.BlockSpec((1, t, 4), lambda g, i: (g, i, 0))),
        compiler_params=_params(("parallel", "parallel")),
    )(qn, qp, kn, kpt, v)


def _mla4_bwd_call(qn, qp, kn, kpt, v, o, do, lse):
    s_len = qn.shape[0]
    t = min(ATT_TILE, s_len)
    nq = s_len // t

    def body(qn_ref, qp_ref, kn_ref, kpt_ref, v_ref, o_ref, do_ref, lse_ref,
             dqn_ref, dqp_ref, dkn_ref, dkpt_ref, dv_ref):
        g = pl.program_id(0)
        i = pl.program_id(1)

        @pl.when(i == 0)
        def _():
            dkn_ref[...] = jnp.zeros_like(dkn_ref)
            dv_ref[...] = jnp.zeros_like(dv_ref)

        @pl.when((i == 0) & (g == 0))
        def _():
            dkpt_ref[...] = jnp.zeros_like(dkpt_ref)

        qn2 = qn_ref[...]
        qp2 = qp_ref[...]
        dof = do_ref[...]
        dob = dof.astype(BF16)
        prod = dof * o_ref[...]
        lane256 = lax.broadcasted_iota(jnp.int32, (1, 256), 1)
        lane128 = lax.broadcasted_iota(jnp.int32, (1, 128), 1)
        row = lax.broadcasted_iota(jnp.int32, (t, t), 0)
        col = lax.broadcasted_iota(jnp.int32, (t, t), 1)
        valid = col <= row
        m64s = [(lane256 // 64) == hh for hh in range(4)]
        half = [(lane128 // 64) == u for u in range(2)]
        m32s = [(lane128 // 32) == hh for hh in range(4)]
        qcs, doms = [], []
        for hh in range(4):
            sl = slice(128 * (hh // 2), 128 * (hh // 2) + 128)
            qpair = qn2[:, sl]
            dpair = dob[:, sl]
            qcs.append(jnp.concatenate([jnp.where(half[hh % 2], qpair, jnp.zeros_like(qpair)),
                                        jnp.where(m32s[hh], qp2, jnp.zeros_like(qp2))], axis=1))
            doms.append(jnp.where(half[hh % 2], dpair, jnp.zeros_like(dpair)))
        dsums = [jnp.sum(jnp.where(m64, prod, 0.0), axis=1, keepdims=True) * MLA_SCALE for m64 in m64s]
        lses = [lse_ref[0, :, hh:hh + 1] for hh in range(4)]
        qn2t = jnp.transpose(qn2.astype(F32))
        qp2t = jnp.transpose(qp2.astype(F32))
        do2t = jnp.transpose(dof)
        sub128 = lax.broadcasted_iota(jnp.int32, (128, 1), 0)
        qtstacks, dotstacks = [], []
        for pp in range(2):
            qts, dts = [], []
            for u in range(2):
                hh = 2 * pp + u
                qts.append(jnp.concatenate(
                    [jnp.where((sub128 // 64) == u, qn2t[128 * pp:128 * pp + 128, :], 0.0),
                     jnp.where((sub128 // 32) == hh, qp2t, 0.0)], axis=0).astype(BF16))
                dts.append(jnp.where((sub128 // 64) == u, do2t[128 * pp:128 * pp + 128, :], 0.0).astype(BF16))
            qtstacks.append(jnp.concatenate(qts, axis=1))
            dotstacks.append(jnp.concatenate(dts, axis=1))

        def block(j, carry, diag):
            dqn, dqp = carry
            off = pl.multiple_of(j * t, t)
            knb = kn_ref[pl.ds(off, t), :]
            kpb = kpt_ref[pl.ds(off, t), :]
            vb = v_ref[pl.ds(off, t), :]
            dqn_parts = []
            dkp = None
            for pp in range(2):
                sl = slice(128 * pp, 128 * pp + 128)
                knp = knb[:, sl]
                vpair = vb[:, sl]
                kc = jnp.concatenate([knp, kpb], axis=1)
                dss, pbs, kcms = [], [], []
                for u in range(2):
                    hh = 2 * pp + u
                    s = _dot_nt(qcs[hh], kc) * MLA_SCALE
                    if diag:
                        s = jnp.where(valid, s, -1e30)
                    p = jnp.exp(s - lses[hh])
                    ds = p * (_dot_nt(doms[hh], vpair) * MLA_SCALE - dsums[hh])
                    dss.append(ds.astype(BF16))
                    pbs.append(p.astype(BF16))
                    kcms.append(jnp.concatenate([jnp.where(half[u], knp, jnp.zeros_like(knp)),
                                                 jnp.where(m32s[hh], kpb, jnp.zeros_like(kpb))], axis=1))
                dqc = _dot(jnp.concatenate(dss, axis=1), jnp.concatenate(kcms, axis=0))
                dqn_parts.append(dqc[:, :128])
                dqp = dqp + dqc[:, 128:]
                dkc = _dot(qtstacks[pp], jnp.concatenate(dss, axis=0))
                dkn_ref[128 * pp:128 * pp + 128, pl.ds(off, t)] += dkc[:128, :]
                dkp = dkc[128:, :] if dkp is None else dkp + dkc[128:, :]
                dv_ref[128 * pp:128 * pp + 128, pl.ds(off, t)] += _dot(dotstacks[pp], jnp.concatenate(pbs, axis=0))
            dqn = dqn + jnp.concatenate(dqn_parts, axis=1)
            dkpt_ref[:, pl.ds(off, t)] += dkp
            return dqn, dqp

        carry = lax.fori_loop(0, i, lambda j, cr: block(j, cr, False),
                              (jnp.zeros((t, 256), F32), jnp.zeros((t, 128), F32)))
        dqn, dqp = block(i, carry, True)
        dqn_ref[...] = dqn.astype(BF16)
        dqp_ref[...] = dqp.astype(BF16)

    return pl.pallas_call(
        body, name="mla_bwd", grid=(2, nq),
        out_shape=(_sds((s_len, 512), BF16), _sds((s_len, 256), BF16), _sds((512, s_len), F32),
                   _sds((128, s_len), F32), _sds((512, s_len), F32)),
        in_specs=[pl.BlockSpec((t, 256), lambda g, i: (i, g)),
                  pl.BlockSpec((t, 128), lambda g, i: (i, g)),
                  pl.BlockSpec((s_len, 256), lambda g, i: (0, g)),
                  pl.BlockSpec((s_len, 128), lambda g, i: (0, 0)),
                  pl.BlockSpec((s_len, 256), lambda g, i: (0, g)),
                  pl.BlockSpec((t, 256), lambda g, i: (i, g)),
                  pl.BlockSpec((t, 256), lambda g, i: (i, g)),
                  pl.BlockSpec((1, t, 4), lambda g, i: (g, i, 0))],
        out_specs=(pl.BlockSpec((t, 256), lambda g, i: (i, g)),
                   pl.BlockSpec((t, 128), lambda g, i: (i, g)),
                   pl.BlockSpec((256, s_len), lambda g, i: (g, 0)),
                   pl.BlockSpec((128, s_len), lambda g, i: (0, 0)),
                   pl.BlockSpec((256, s_len), lambda g, i: (g, 0))),
        compiler_params=_params(("arbitrary", "arbitrary")),
    )(qn, qp, kn, kpt, v, o, do, lse)


def _post_call(x, tgt, oa, ob, sz, mz, ga, gb, gate, gf, wa, wb, wo, wat, wbt, wot):
    s_len = x.shape[0]
    tm = min(ROW_TILE, s_len)

    def body(x_ref, t_ref, oa_ref, ob_ref, sz_ref, mz_ref, ga_ref, gb_ref, gate_ref, gf_ref,
             wa_ref, wb_ref, wo_ref, wat_ref, wbt_ref, wot_ref,
             dx2_ref, doa_ref, dob_ref, dsz_ref, dmz_ref, dga_ref, dgb_ref,
             dwo_ref, dwa_ref, dwb_ref, dgf_ref, dgate_ref, loss_ref):
        @pl.when(pl.program_id(0) == 0)
        def _():
            dwo_ref[...] = jnp.zeros_like(dwo_ref)
            dwa_ref[...] = jnp.zeros_like(dwa_ref)
            dwb_ref[...] = jnp.zeros_like(dwb_ref)
            dgf_ref[...] = jnp.zeros_like(dgf_ref)
            dgate_ref[...] = jnp.zeros_like(dgate_ref)
            loss_ref[...] = jnp.zeros_like(loss_ref)

        gate = gate_ref[...]
        gf = gf_ref[...]
        oa = oa_ref[...]
        ob = ob_ref[...]
        sz = sz_ref[...]
        mz = mz_ref[...]
        sa = _sigmoid(sz)
        sb = _sigmoid(mz)
        silu_a = sz * sa
        silu_b = mz * sb
        ua = (oa * silu_a).astype(BF16)
        ub = (ob * silu_b).astype(BF16)
        ya = _dot(ua, wa_ref[...])
        yb = _dot(ub, wb_ref[...])
        sga = _sigmoid(ga_ref[...])
        sgb = _sigmoid(gb_ref[...])
        merged = (sga * ya + sgb * yb).astype(BF16)
        out = _dot(merged, wo_ref[...])
        x2 = x_ref[...] + gate * out
        r2 = lax.rsqrt(jnp.mean(x2 * x2, axis=-1, keepdims=True) + EPS)
        xhat = x2 * r2
        err = xhat * gf - t_ref[...]
        loss_ref[...] += 0.5 * jnp.sum(jnp.sum(err * err, axis=1, keepdims=True), axis=0, keepdims=True) / D_MODEL
        dy = err * (1.0 / D_MODEL)
        dgf_ref[...] += jnp.sum(dy * xhat, axis=0, keepdims=True)
        dxhat = dy * gf
        dx2 = r2 * (dxhat - xhat * jnp.mean(dxhat * xhat, axis=-1, keepdims=True))
        dx2_ref[...] = dx2
        dgate_ref[...] += jnp.sum(dx2 * out, axis=0, keepdims=True)
        dout = (dx2 * gate).astype(BF16)
        dmerged = _dot(dout, wot_ref[...])
        dwo_ref[...] += _dot_tn(merged, dout)
        dya = dmerged * sga
        dyb = dmerged * sgb
        dga_ref[...] = (dya * ya * (1.0 - sga)).astype(BF16)
        dgb_ref[...] = (dyb * yb * (1.0 - sgb)).astype(BF16)
        dyab = dya.astype(BF16)
        dybb = dyb.astype(BF16)
        dua = _dot(dyab, wat_ref[...])
        dub = _dot(dybb, wbt_ref[...])
        dwa_ref[...] += _dot_tn(ua, dyab)
        dwb_ref[...] += _dot_tn(ub, dybb)
        doa_ref[...] = dua * silu_a
        dob_ref[...] = dub * silu_b
        dsz_ref[...] = (dua * oa * (sa * (1.0 + sz * (1.0 - sa)))).astype(BF16)
        dmz_ref[...] = (dub * ob * (sb * (1.0 + mz * (1.0 - sb)))).astype(BF16)

    return pl.pallas_call(
        body, name="post", grid=(s_len // tm,),
        out_shape=(_sds((s_len, D_MODEL), F32), _sds((s_len, 512), F32), _sds((s_len, 512), F32),
                   _sds((s_len, 512), BF16), _sds((s_len, 512), BF16),
                   _sds((s_len, D_MODEL), BF16), _sds((s_len, D_MODEL), BF16),
                   _sds((D_MODEL, D_MODEL), F32), _sds((512, D_MODEL), F32), _sds((512, D_MODEL), F32),
                   _sds((1, D_MODEL), F32), _sds((1, D_MODEL), F32), _sds((1, 128), F32)),
        in_specs=[_rows(tm, D_MODEL), _rows(tm, D_MODEL), _rows(tm, 512), _rows(tm, 512), _rows(tm, 512),
                  _rows(tm, 512), _rows(tm, D_MODEL), _rows(tm, D_MODEL), _whole((1, D_MODEL)), _whole((1, D_MODEL)),
                  _whole((512, D_MODEL)), _whole((512, D_MODEL)), _whole((D_MODEL, D_MODEL)),
                  _whole((D_MODEL, 512)), _whole((D_MODEL, 512)), _whole((D_MODEL, D_MODEL))],
        out_specs=(_rows(tm, D_MODEL), _rows(tm, 512), _rows(tm, 512), _rows(tm, 512), _rows(tm, 512),
                   _rows(tm, D_MODEL), _rows(tm, D_MODEL),
                   _whole((D_MODEL, D_MODEL)), _whole((512, D_MODEL)), _whole((512, D_MODEL)),
                   _whole((1, D_MODEL)), _whole((1, D_MODEL)), _whole((1, 128))),
        compiler_params=_params(("arbitrary",)),
    )(x, tgt, oa, ob, sz, mz, ga, gb, gate, gf, wa, wb, wo, wat, wbt, wot)


def _bwdprep_call(dsq, dsk, dsv, dsz, dqn, dqp, dkn, dvv, dkpt, dmz, dga, dgb, cq, ckv, cos256, sin256,
                  qg, kvg, wqt, wkvt):
    s_len = cq.shape[0]
    tm = min(ROW_TILE, s_len)

    def body(dsq_ref, dsk_ref, dsv_ref, dsz_ref, dqn_ref, dqp_ref, dkn_ref, dvv_ref, dkpt_ref, dmz_ref,
             dga_ref, dgb_ref, cq_ref, ckv_ref, cos_ref, sin_ref, qg_ref, kvg_ref, wqt_ref, wkvt_ref,
             dp_ref, dwq_ref, dwkv_ref, dqg_ref, dkvg_ref):
        @pl.when(pl.program_id(0) == 0)
        def _():
            dwq_ref[...] = jnp.zeros_like(dwq_ref)
            dwkv_ref[...] = jnp.zeros_like(dwkv_ref)
            dqg_ref[...] = jnp.zeros_like(dqg_ref)
            dkvg_ref[...] = jnp.zeros_like(dkvg_ref)

        cos = cos_ref[...]
        sin = sin_ref[...]
        dp_ref[:, O_SQ:O_SK] = dsq_ref[...]
        dp_ref[:, O_SK:O_SV] = jnp.transpose(dsk_ref[...]).astype(BF16)
        dp_ref[:, O_SV:O_SZ] = jnp.transpose(dsv_ref[...]).astype(BF16)
        dp_ref[:, O_SZ:O_CQ] = dsz_ref[...]
        dp_ref[:, O_MZ:O_GA] = dmz_ref[...]
        dp_ref[:, O_GA:O_GB] = dga_ref[...]
        dp_ref[:, O_GB:O_KR] = dgb_ref[...]
        dkp = jnp.transpose(dkpt_ref[...])
        dp_ref[:, O_KR:O_KR + 128] = (dkp * cos[:, :128]).astype(BF16)
        dp_ref[:, O_KR + 128:O_END] = (dkp * sin[:, :128]).astype(BF16)
        dp_ref[:, O_END:W_INT] = jnp.zeros((tm, W_INT - O_END), BF16)

        cq = cq_ref[...]
        rq = lax.rsqrt(jnp.mean(cq * cq, axis=-1, keepdims=True) + EPS)
        cqh = cq * rq
        qg = qg_ref[...]
        cqn = (cqh * qg).astype(BF16)
        dqp = dqp_ref[...].astype(F32)
        dqa = jnp.concatenate([dqn_ref[...], (dqp * cos).astype(BF16), (dqp * sin).astype(BF16)], axis=1)
        dcqn = _dot(dqa, wqt_ref[...])
        dwq_ref[...] += _dot_tn(cqn, dqa)
        dqg_ref[...] += jnp.sum(dcqn * cqh, axis=0, keepdims=True)
        dh = dcqn * qg
        dcq = rq * (dh - cqh * jnp.mean(dh * cqh, axis=-1, keepdims=True))
        dp_ref[:, O_CQ:O_CKV] = dcq.astype(BF16)

        ckv = ckv_ref[...]
        rk = lax.rsqrt(jnp.mean(ckv * ckv, axis=-1, keepdims=True) + EPS)
        ckh = ckv * rk
        kvg = kvg_ref[...]
        ckvn = (ckh * kvg).astype(BF16)
        dkva = jnp.concatenate([jnp.transpose(dkn_ref[...]).astype(BF16),
                                jnp.transpose(dvv_ref[...]).astype(BF16)], axis=1)
        dckvn = _dot(dkva, wkvt_ref[...])
        dwkv_ref[...] += _dot_tn(ckvn, dkva)
        dkvg_ref[...] += jnp.sum(dckvn * ckh, axis=0, keepdims=True)
        dh2 = dckvn * kvg
        dckv = rk * (dh2 - ckh * jnp.mean(dh2 * ckh, axis=-1, keepdims=True))
        dp_ref[:, O_CKV:O_MZ] = dckv.astype(BF16)

    return pl.pallas_call(
        body, name="bwdprep", grid=(s_len // tm,),
        out_shape=(_sds((s_len, W_INT), BF16), _sds((Q_RANK, 1024), F32), _sds((KV_RANK, 1024), F32),
                   _sds((1, Q_RANK), F32), _sds((1, KV_RANK), F32)),
        in_specs=[_rows(tm, 512), _cols(512, tm), _cols(512, tm), _rows(tm, 512), _rows(tm, 512), _rows(tm, 256),
                  _cols(512, tm), _cols(512, tm), _cols(128, tm), _rows(tm, 512), _rows(tm, D_MODEL),
                  _rows(tm, D_MODEL), _rows(tm, Q_RANK), _rows(tm, KV_RANK), _rows(tm, 256), _rows(tm, 256),
                  _whole((1, Q_RANK)), _whole((1, KV_RANK)), _whole((1024, Q_RANK)), _whole((1024, KV_RANK))],
        out_specs=(_rows(tm, W_INT), _whole((Q_RANK, 1024)), _whole((KV_RANK, 1024)),
                   _whole((1, Q_RANK)), _whole((1, KV_RANK))),
        compiler_params=_params(("arbitrary",)),
    )(dsq, dsk, dsv, dsz, dqn, dqp, dkn, dvv, dkpt, dmz, dga, dgb, cq, ckv, cos256, sin256, qg, kvg, wqt, wkvt)


def _dh_call(dproj, w_int_t, x, dx2, scale, g1):
    s_len = x.shape[0]
    tm = min(ROW_TILE, s_len)

    def body(dp_ref, wt_ref, x_ref, dx2_ref, sc_ref, g1_ref, gx_ref, dsh_ref, dsc_ref, dg1_ref):
        @pl.when(pl.program_id(0) == 0)
        def _():
            dsh_ref[...] = jnp.zeros_like(dsh_ref)
            dsc_ref[...] = jnp.zeros_like(dsc_ref)
            dg1_ref[...] = jnp.zeros_like(dg1_ref)

        dh = _dot(dp_ref[...], wt_ref[...])
        xt = x_ref[...]
        r = lax.rsqrt(jnp.mean(xt * xt, axis=-1, keepdims=True) + EPS)
        xh = xt * r
        g1 = g1_ref[...]
        xg = xh * g1
        dsh_ref[...] += jnp.sum(dh, axis=0, keepdims=True)
        dsc_ref[...] += jnp.sum(dh * xg, axis=0, keepdims=True)
        dxg = dh * (1.0 + sc_ref[...])
        dg1_ref[...] += jnp.sum(dxg * xh, axis=0, keepdims=True)
        dxh = dxg * g1
        gx_ref[...] = dx2_ref[...] + r * (dxh - xh * jnp.mean(dxh * xh, axis=-1, keepdims=True))

    return pl.pallas_call(
        body, name="dh", grid=(s_len // tm,),
        out_shape=(_sds((s_len, D_MODEL), F32), _sds((1, D_MODEL), F32), _sds((1, D_MODEL), F32),
                   _sds((1, D_MODEL), F32)),
        in_specs=[_rows(tm, W_INT), _whole((W_INT, D_MODEL)), _rows(tm, D_MODEL), _rows(tm, D_MODEL),
                  _whole((1, D_MODEL)), _whole((1, D_MODEL))],
        out_specs=(_rows(tm, D_MODEL), _whole((1, D_MODEL)), _whole((1, D_MODEL)), _whole((1, D_MODEL))),
        compiler_params=_params(("arbitrary",)),
    )(dproj, w_int_t, x, dx2, scale, g1)


def _dwin_call(h, dproj):
    s_len = h.shape[0]
    tm = min(2 * ROW_TILE, s_len)
    nc = 4
    chunk = W_INT // nc

    def body(h_ref, dp_ref, dw_ref):
        @pl.when(pl.program_id(1) == 0)
        def _():
            dw_ref[...] = jnp.zeros_like(dw_ref)

        dw_ref[...] += _dot_tn(h_ref[...], dp_ref[...])

    return pl.pallas_call(
        body, name="dwin", grid=(nc, s_len // tm),
        out_shape=_sds((D_MODEL, nc * chunk), F32),
        in_specs=[pl.BlockSpec((tm, D_MODEL), lambda c, i: (i, 0)),
                  pl.BlockSpec((tm, chunk), lambda c, i: (i, c))],
        out_specs=pl.BlockSpec((D_MODEL, chunk), lambda c, i: (0, c)),
        compiler_params=_params(("parallel", "arbitrary")),
    )(h, dproj)


def _small_call(svg, ct, dmod_sh):
    def body(sv_ref, ct_ref, dm_ref, tot_ref, gwada_ref):
        acc = sv_ref[0:1, :]
        for d in range(1, N_DEV):
            acc = acc + sv_ref[d:d + 1, :]
        tot_ref[...] = acc
        gwada_ref[...] = lax.dot_general(ct_ref[...], dm_ref[...], (((1,), (0,)), ((), ())),
                                         precision=lax.Precision.HIGHEST, preferred_element_type=F32)

    vmem = pl.BlockSpec(memory_space=pltpu.VMEM)
    return pl.pallas_call(
        body, name="small_grads",
        out_shape=(_sds((1, 8 * SV_COLS), F32), _sds((D_MODEL, 768), F32)),
        in_specs=[vmem, vmem, vmem], out_specs=(vmem, vmem),
        compiler_params=_params(),
    )(svg, ct, dmod_sh)


def _adamw_tile_rows(rows, cols):
    budget = 2 << 20
    if rows * cols * 4 <= budget or rows % 8:
        return rows
    best = 8
    for tr in range(8, rows + 1, 8):
        if rows % tr == 0 and tr * cols * 4 <= budget:
            best = tr
    return best


def _adamw_call(name, w, g, m, v):
    rows, cols = w.shape
    tr = _adamw_tile_rows(rows, cols)

    def body(w_ref, g_ref, m_ref, v_ref, d_ref, nm_ref, nv_ref):
        gg = g_ref[...]
        m2 = ADAM_B1 * m_ref[...] + (1.0 - ADAM_B1) * gg
        v2 = ADAM_B2 * v_ref[...] + (1.0 - ADAM_B2) * (gg * gg)
        m_hat = m2 / (1.0 - ADAM_B1 ** ADAM_STEP)
        v_hat = v2 / (1.0 - ADAM_B2 ** ADAM_STEP)
        d_ref[...] = -ADAM_LR * (m_hat / (jnp.sqrt(v_hat) + ADAM_EPS) + ADAM_WD * w_ref[...])
        nm_ref[...] = m2
        nv_ref[...] = v2

    spec = pl.BlockSpec((tr, cols), lambda i: (i, 0))
    return pl.pallas_call(
        body, name="adamw_" + name, grid=(rows // tr,),
        out_shape=(_sds((rows, cols), F32),) * 3,
        in_specs=[spec] * 4, out_specs=(spec,) * 3,
        compiler_params=_params(("parallel",)),
    )(w, g, m, v)


IN_SHARD = IN_WIDTH // N_CHIPS
HALF_D = D_MODEL // 2
SMALL_ROWS = (576, 512, 1024, 1024, 2048)
SMALL_TOTAL = sum(SMALL_ROWS)
SMALL_HALF = SMALL_TOTAL // 2
SMALL_SUM_ROWS = 432


def _gather2_call(c_row, w_ada_sh, pack_in, pack_small):
    def body(c_ref, wada_ref, pki_ref, pks_ref, mg_ref, cg_ref, gwi_ref, gws_ref,
             cv, ssem_c, rsem_c, ssem_m, rsem_m, ssem_w, rsem_w, ssem_f, rsem_f, lsem):
        x, y, c = lax.axis_index("x"), lax.axis_index("y"), lax.axis_index("c")
        me = 4 * x + 2 * y + c
        chip = 2 * x + y
        rel3 = [(1, 0), (0, 1), (1, 1)]
        packs = [(pki_ref, gwi_ref), (pks_ref, gws_ref)]

        sends = []
        for j, (dx, dy) in enumerate(rel3):
            for a, (pk, gw) in enumerate(packs):
                cp = pltpu.make_async_remote_copy(
                    src_ref=pk.at[c], dst_ref=gw.at[chip, c], send_sem=ssem_w.at[j, a], recv_sem=rsem_w.at[j, a],
                    device_id=(_flip(x, dx), _flip(y, dy), c), device_id_type=MESH)
                cp.start()
                sends.append(cp)
        owns = []
        for a, (pk, gw) in enumerate(packs):
            own = pltpu.make_async_copy(pk, gw.at[chip], lsem.at[a])
            own.start()
            owns.append(own)

        cv[me] = c_ref[...]
        for r in range(1, N_DEV):
            dx, dy, dc = (r >> 2) & 1, (r >> 1) & 1, r & 1
            cp = pltpu.make_async_remote_copy(
                src_ref=c_ref, dst_ref=cv.at[me], send_sem=ssem_c.at[r - 1], recv_sem=rsem_c.at[r - 1],
                device_id=(_flip(x, dx), _flip(y, dy), _flip(c, dc)), device_id_type=MESH)
            cp.start()
            sends.append(cp)
        for r in range(1, N_DEV):
            dx, dy, dc = (r >> 2) & 1, (r >> 1) & 1, r & 1
            src = 4 * _flip(x, dx) + 2 * _flip(y, dy) + _flip(c, dc)
            pltpu.make_async_remote_copy(
                src_ref=c_ref, dst_ref=cv.at[src], send_sem=ssem_c.at[r - 1], recv_sem=rsem_c.at[r - 1],
                device_id=(x, y, c), device_id_type=MESH).wait_recv()
        rows = lax.broadcasted_iota(jnp.int32, (N_DEV, D_MODEL), 0)
        call = jnp.zeros((N_DEV, D_MODEL), F32)
        for b in range(N_DEV):
            call = jnp.where(rows == b, jnp.broadcast_to(cv[b], (N_DEV, D_MODEL)), call)
        cg_ref[...] = call

        mg_ref[chip] = lax.dot_general(call, wada_ref[...], (((1,), (0,)), ((), ())),
                                       precision=lax.Precision.HIGHEST, preferred_element_type=F32)
        for j, (dx, dy) in enumerate(rel3):
            cp = pltpu.make_async_remote_copy(
                src_ref=mg_ref.at[chip], dst_ref=mg_ref.at[chip], send_sem=ssem_m.at[j], recv_sem=rsem_m.at[j],
                device_id=(_flip(x, dx), _flip(y, dy), c), device_id_type=MESH)
            cp.start()
            sends.append(cp)
        for j, (dx, dy) in enumerate(rel3):
            src_chip = 2 * _flip(x, dx) + _flip(y, dy)
            pltpu.make_async_remote_copy(
                src_ref=mg_ref.at[src_chip], dst_ref=mg_ref.at[src_chip], send_sem=ssem_m.at[j],
                recv_sem=rsem_m.at[j], device_id=(x, y, c), device_id_type=MESH).wait_recv()
        for j, (dx, dy) in enumerate(rel3):
            src_chip = 2 * _flip(x, dx) + _flip(y, dy)
            for a, (pk, gw) in enumerate(packs):
                pltpu.make_async_remote_copy(
                    src_ref=pk.at[c], dst_ref=gw.at[src_chip, c], send_sem=ssem_w.at[j, a],
                    recv_sem=rsem_w.at[j, a], device_id=(x, y, c), device_id_type=MESH).wait_recv()
                cp = pltpu.make_async_remote_copy(
                    src_ref=gw.at[src_chip, c], dst_ref=gw.at[src_chip, c], send_sem=ssem_f.at[j, a],
                    recv_sem=rsem_f.at[j, a], device_id=(x, y, 1 - c), device_id_type=MESH)
                cp.start()
                sends.append(cp)
        for j, (dx, dy) in enumerate(rel3):
            src_chip = 2 * _flip(x, dx) + _flip(y, dy)
            for a, (pk, gw) in enumerate(packs):
                pltpu.make_async_remote_copy(
                    src_ref=pk.at[c], dst_ref=gw.at[src_chip, 1 - c], send_sem=ssem_f.at[j, a],
                    recv_sem=rsem_f.at[j, a], device_id=(x, y, c), device_id_type=MESH).wait_recv()
        for cp in sends:
            cp.wait_send()
        for own in owns:
            own.wait()

    vmem = pl.BlockSpec(memory_space=pltpu.VMEM)
    return pl.pallas_call(
        body, name="gather_fwd",
        out_shape=(_sds((N_CHIPS, N_DEV, 768), F32), _sds((N_DEV, D_MODEL), F32),
                   _sds((N_CHIPS, 2, IN_SHARD, HALF_D), BF16), _sds((N_CHIPS, 2, SMALL_HALF, LANES), BF16)),
        in_specs=[vmem, vmem, vmem, vmem], out_specs=(vmem, vmem, vmem, vmem),
        scratch_shapes=[
            pltpu.VMEM((N_DEV, 1, D_MODEL), F32),
            pltpu.SemaphoreType.DMA((N_DEV - 1,)), pltpu.SemaphoreType.DMA((N_DEV - 1,)),
            pltpu.SemaphoreType.DMA((3,)), pltpu.SemaphoreType.DMA((3,)),
            pltpu.SemaphoreType.DMA((3, 2)), pltpu.SemaphoreType.DMA((3, 2)),
            pltpu.SemaphoreType.DMA((3, 2)), pltpu.SemaphoreType.DMA((3, 2)),
            pltpu.SemaphoreType.DMA((2,)),
        ],
        compiler_params=_params(),
    )(c_row, w_ada_sh, pack_in, pack_small)


def _reduce2_call(g_in, g_small, sv):
    def body(gi_ref, gs_ref, sv_ref, fi_ref, fs_ref, svg_ref, land_i, land_s,
             ssem_g, rsem_g, ssem_s, rsem_s, ssem_x, rsem_x, lsem):
        x, y, c = lax.axis_index("x"), lax.axis_index("y"), lax.axis_index("c")
        me = 4 * x + 2 * y + c
        pieces = [(gi_ref, land_i), (gs_ref, land_s)]
        copies = []
        for r in range(1, N_DEV):
            dx, dy, dc = (r >> 2) & 1, (r >> 1) & 1, r & 1
            tx, ty, tc = _flip(x, dx), _flip(y, dy), _flip(c, dc)
            tgt = 4 * tx + 2 * ty + tc
            for a, (g, land) in enumerate(pieces):
                cp = pltpu.make_async_remote_copy(
                    src_ref=g.at[tgt], dst_ref=land.at[me], send_sem=ssem_g.at[r - 1, a],
                    recv_sem=rsem_g.at[r - 1, a], device_id=(tx, ty, tc), device_id_type=MESH)
                cp.start()
                copies.append(cp)
            cp = pltpu.make_async_remote_copy(
                src_ref=sv_ref, dst_ref=svg_ref.at[me], send_sem=ssem_s.at[r - 1],
                recv_sem=rsem_s.at[r - 1], device_id=(tx, ty, tc), device_id_type=MESH)
            cp.start()
            copies.append(cp)
        owns = []
        for a, (g, land) in enumerate(pieces):
            own = pltpu.make_async_copy(g.at[me], land.at[me], lsem.at[a])
            own.start()
            owns.append(own)
        svg_ref[me] = sv_ref[...]
        for r in range(1, N_DEV):
            dx, dy, dc = (r >> 2) & 1, (r >> 1) & 1, r & 1
            src = 4 * _flip(x, dx) + 2 * _flip(y, dy) + _flip(c, dc)
            for a, (g, land) in enumerate(pieces):
                pltpu.make_async_remote_copy(
                    src_ref=g.at[src], dst_ref=land.at[src], send_sem=ssem_g.at[r - 1, a],
                    recv_sem=rsem_g.at[r - 1, a], device_id=(x, y, c), device_id_type=MESH).wait_recv()
            pltpu.make_async_remote_copy(
                src_ref=sv_ref, dst_ref=svg_ref.at[src], send_sem=ssem_s.at[r - 1],
                recv_sem=rsem_s.at[r - 1], device_id=(x, y, c), device_id_type=MESH).wait_recv()
        for own in owns:
            own.wait()

        for qd in range(HALF_D // LANES):
            sl = slice(LANES * qd, LANES * qd + LANES)
            acc = land_i[0, :, sl].astype(F32)
            for d in range(1, N_DEV):
                acc = acc + land_i[d, :, sl].astype(F32)
            fi_ref[c, :, sl] = acc

        def sum_rows(i, carry):
            sl = pl.ds(pl.multiple_of(i * SMALL_SUM_ROWS, 16), SMALL_SUM_ROWS)
            acc = land_s[0, sl, :].astype(F32)
            for d in range(1, N_DEV):
                acc = acc + land_s[d, sl, :].astype(F32)
            fs_ref[c, sl, :] = acc
            return carry

        lax.fori_loop(0, SMALL_HALF // SMALL_SUM_ROWS, sum_rows, 0)
        swaps = []
        for a, f in enumerate((fi_ref, fs_ref)):
            cp = pltpu.make_async_remote_copy(
                src_ref=f.at[c], dst_ref=f.at[c], send_sem=ssem_x.at[a], recv_sem=rsem_x.at[a],
                device_id=(x, y, 1 - c), device_id_type=MESH)
            cp.start()
            swaps.append(cp)
        for a, f in enumerate((fi_ref, fs_ref)):
            pltpu.make_async_remote_copy(
                src_ref=f.at[c], dst_ref=f.at[1 - c], send_sem=ssem_x.at[a], recv_sem=rsem_x.at[a],
                device_id=(x, y, c), device_id_type=MESH).wait_recv()
        for cp in swaps + copies:
            cp.wait_send()

    vmem = pl.BlockSpec(memory_space=pltpu.VMEM)
    return pl.pallas_call(
        body, name="grad_reduce",
        out_shape=(_sds((2, IN_SHARD, HALF_D), F32), _sds((2, SMALL_HALF, LANES), F32),
                   _sds((N_DEV, 8, SV_COLS), F32)),
        in_specs=[vmem, vmem, vmem], out_specs=(vmem, vmem, vmem),
        scratch_shapes=[
            pltpu.VMEM((N_DEV, IN_SHARD, HALF_D), BF16), pltpu.VMEM((N_DEV, SMALL_HALF, LANES), BF16),
            pltpu.SemaphoreType.DMA((N_DEV - 1, 2)), pltpu.SemaphoreType.DMA((N_DEV - 1, 2)),
            pltpu.SemaphoreType.DMA((N_DEV - 1,)), pltpu.SemaphoreType.DMA((N_DEV - 1,)),
            pltpu.SemaphoreType.DMA((2,)), pltpu.SemaphoreType.DMA((2,)), pltpu.SemaphoreType.DMA((2,)),
        ],
        compiler_params=_params(),
    )(g_in, g_small, sv)


def _dwin_t_call(h, dproj):
    s_len = h.shape[0]
    tm = min(2 * ROW_TILE, s_len)
    nrow = s_len // tm
    nc = 4
    chunk = W_INT // nc

    def body(h_ref, dp_ref, dw_ref, acc):
        i = pl.program_id(1)

        @pl.when(i == 0)
        def _():
            acc[...] = jnp.zeros_like(acc)

        acc[...] += _dot_tn(dp_ref[...], h_ref[...])

        @pl.when(i == nrow - 1)
        def _():
            dw_ref[...] = acc[...].astype(BF16)

    return pl.pallas_call(
        body, name="dwin", grid=(nc, nrow),
        out_shape=_sds((W_INT, D_MODEL), BF16),
        in_specs=[pl.BlockSpec((tm, D_MODEL), lambda c, i: (i, 0)),
                  pl.BlockSpec((tm, chunk), lambda c, i: (i, c))],
        out_specs=pl.BlockSpec((chunk, D_MODEL), lambda c, i: (c, 0)),
        scratch_shapes=[pltpu.VMEM((chunk, D_MODEL), F32)],
        compiler_params=_params(("parallel", "arbitrary")),
    )(h, dproj)


def _swap_rows(w, group):
    r, n = w.shape
    return w.reshape(r // group, 2, group // 2, n)[:, ::-1].reshape(r, n)


def _internal_weights_t(w_in_t, w_uq, w_ukv):
    krot_t = w_in_t[2688:2720]
    w_int_t = jnp.concatenate([
        w_in_t[0:512] * jnp.asarray(0.125, w_in_t.dtype), w_in_t[512:2048],
        w_in_t[2048:2432], w_in_t[2432:2688], w_in_t[2720:3232], w_in_t[3232:4256], w_in_t[4256:5280],
        jnp.tile(krot_t, (4, 1)), jnp.tile(_swap_rows(krot_t, 32), (4, 1)),
        jnp.zeros((W_INT - O_END, D_MODEL), w_in_t.dtype)], axis=0)
    uq = w_uq.reshape(Q_RANK, N_HEADS, 96)
    wp = uq[:, :, 64:].reshape(Q_RANK, 256)
    w_q = jnp.concatenate([uq[:, :, :64].reshape(Q_RANK, 512), wp, _swap_halves(wp, 32)], axis=1)
    ukv = w_ukv.reshape(KV_RANK, N_HEADS, 128)
    w_kv = jnp.concatenate([ukv[:, :, :64].reshape(KV_RANK, 512), ukv[:, :, 64:].reshape(KV_RANK, 512)], axis=1)
    return w_int_t, w_q, w_kv


def _true_weight_grads_t(dwi_t, dwq, dwkv):
    dkr = dwi_t[O_KR:O_KR + 128].astype(F32).reshape(4, 32, D_MODEL).sum(axis=0)
    dkr_sw = dwi_t[O_KR + 128:O_END].astype(F32).reshape(4, 32, D_MODEL).sum(axis=0)
    dkrot_t = (dkr + _swap_rows(dkr_sw, 32)).astype(dwi_t.dtype)
    g_in_t = jnp.concatenate([
        dwi_t[0:512] * jnp.asarray(0.125, dwi_t.dtype), dwi_t[512:2048], dwi_t[O_CQ:O_CKV], dwi_t[O_CKV:O_MZ],
        dkrot_t, dwi_t[O_MZ:O_GA], dwi_t[O_GA:O_GB], dwi_t[O_GB:O_KR]], axis=0)
    dwp = dwq[:, 512:768] + _swap_halves(dwq[:, 768:1024], 32)
    g_uq = jnp.concatenate([dwq[:, :512].reshape(Q_RANK, N_HEADS, 64), dwp.reshape(Q_RANK, N_HEADS, 32)],
                           axis=2).reshape(Q_RANK, 768)
    g_ukv = jnp.concatenate([dwkv[:, :512].reshape(KV_RANK, N_HEADS, 64), dwkv[:, 512:].reshape(KV_RANK, N_HEADS, 64)],
                            axis=2).reshape(KV_RANK, 1024)
    return g_in_t, g_uq, g_ukv


def _swap_halves(w, group):
    r, n = w.shape
    return w.reshape(r, n // group, 2, group // 2)[:, :, ::-1, :].reshape(r, n)


def _pack_shards(parts):
    return jnp.concatenate([p.reshape(-1, LANES) for p in parts], axis=0)


def _unpack_chip_major(gw):
    offs = [0]
    for r in PACK_ROWS:
        offs.append(offs[-1] + r)

    def cols(i, rows, shard_cols):
        blk = gw[:, offs[i]:offs[i + 1]].reshape(N_CHIPS, rows, shard_cols)
        return blk.transpose(1, 0, 2).reshape(rows, N_CHIPS * shard_cols)

    w_in = cols(0, D_MODEL, 1320)
    w_uq = cols(1, Q_RANK, 192)
    w_ukv = cols(2, KV_RANK, 256)
    w_a = cols(3, 512, 256)
    w_b = cols(4, 512, 256)
    w_out = gw[:, offs[5]:offs[6]].reshape(D_MODEL, D_MODEL)
    return w_in, w_uq, w_ukv, w_a, w_b, w_out


def _unpack_small(gw):
    offs = [0]
    for r in SMALL_ROWS:
        offs.append(offs[-1] + r)

    def cols(i, rows, shard_cols):
        blk = gw[:, offs[i]:offs[i + 1]].reshape(N_CHIPS, rows, shard_cols)
        return blk.transpose(1, 0, 2).reshape(rows, N_CHIPS * shard_cols)

    return (cols(0, Q_RANK, 192), cols(1, KV_RANK, 256), cols(2, 512, 256), cols(3, 512, 256),
            gw[:, offs[4]:offs[5]].reshape(D_MODEL, D_MODEL))


def _internal_weights(w_in, w_uq, w_ukv):
    krot = w_in[:, 2688:2720]
    w_int = jnp.concatenate([
        w_in[:, 0:512] * jnp.asarray(0.125, w_in.dtype), w_in[:, 512:2048],
        w_in[:, 2048:2432], w_in[:, 2432:2688], w_in[:, 2720:3232], w_in[:, 3232:4256], w_in[:, 4256:5280],
        jnp.tile(krot, (1, 4)), jnp.tile(_swap_halves(krot, 32), (1, 4)),
        jnp.zeros((D_MODEL, W_INT - O_END), w_in.dtype)], axis=1)
    uq = w_uq.reshape(Q_RANK, N_HEADS, 96)
    wp = uq[:, :, 64:].reshape(Q_RANK, 256)
    w_q = jnp.concatenate([uq[:, :, :64].reshape(Q_RANK, 512), wp, _swap_halves(wp, 32)], axis=1)
    ukv = w_ukv.reshape(KV_RANK, N_HEADS, 128)
    w_kv = jnp.concatenate([ukv[:, :, :64].reshape(KV_RANK, 512), ukv[:, :, 64:].reshape(KV_RANK, 512)], axis=1)
    return w_int, w_q, w_kv


def _true_weight_grads(dwi, dwq, dwkv):
    dkr = dwi[:, O_KR:O_KR + 128].reshape(D_MODEL, 4, 32).sum(axis=1)
    dkr_sw = dwi[:, O_KR + 128:O_END].reshape(D_MODEL, 4, 32).sum(axis=1)
    dkrot = dkr + _swap_halves(dkr_sw, 32)
    g_in = jnp.concatenate([
        dwi[:, 0:512] * 0.125, dwi[:, 512:2048], dwi[:, O_CQ:O_CKV], dwi[:, O_CKV:O_MZ], dkrot,
        dwi[:, O_MZ:O_GA], dwi[:, O_GA:O_GB], dwi[:, O_GB:O_KR]], axis=1)
    dwp = dwq[:, 512:768] + _swap_halves(dwq[:, 768:1024], 32)
    g_uq = jnp.concatenate([dwq[:, :512].reshape(Q_RANK, N_HEADS, 64), dwp.reshape(Q_RANK, N_HEADS, 32)],
                           axis=2).reshape(Q_RANK, 768)
    g_ukv = jnp.concatenate([dwkv[:, :512].reshape(KV_RANK, N_HEADS, 64), dwkv[:, 512:].reshape(KV_RANK, N_HEADS, 64)],
                            axis=2).reshape(KV_RANK, 1024)
    return g_in, g_uq, g_ukv


def _chip_major(g, shard_cols):
    r = g.shape[0]
    return g.reshape(r, N_CHIPS, shard_cols).transpose(1, 0, 2).reshape(N_CHIPS, -1, LANES)


def kernel(x, c, positions, w_ada, b_ada, norm_gain, w_in, q_norm_gain, w_uq, kv_norm_gain, w_ukv, w_branch_a, w_branch_b, w_out, final_norm_gain, loss_target, m_w_ada, m_b_ada, m_norm_gain, m_w_in, m_q_norm_gain, m_w_uq, m_kv_norm_gain, m_w_ukv, m_w_branch_a, m_w_branch_b, m_w_out, m_final_norm_gain, v_w_ada, v_b_ada, v_norm_gain, v_w_in, v_q_norm_gain, v_w_uq, v_kv_norm_gain, v_w_ukv, v_w_branch_a, v_w_branch_b, v_w_out, v_final_norm_gain):
    ix, iy, ic = lax.axis_index("x"), lax.axis_index("y"), lax.axis_index("c")
    me = 4 * ix + 2 * iy + ic
    chip = 2 * ix + iy
    xs = x[0]
    tgt = loss_target[0]
    s_len = xs.shape[0]

    w_in_t = jnp.swapaxes(w_in[0], 0, 1)
    w_in_tb = w_in_t.astype(BF16)
    pack_in = jnp.stack([w_in_tb[:, :HALF_D], w_in_tb[:, HALF_D:]], axis=0)
    small_shards = (w_uq[0], w_ukv[0], w_branch_a[0], w_branch_b[0], w_out[0])
    pack_small = _pack_shards([s.astype(BF16) for s in small_shards]).reshape(2, SMALL_HALF, LANES)
    mg, call, gw_in, gw_small = _gather2_call(c, w_ada[0], pack_in, pack_small)
    mod = mg.transpose(1, 0, 2).reshape(N_DEV, 3 * D_MODEL) + b_ada
    mod_me = lax.dynamic_slice_in_dim(mod, me, 1, axis=0)
    shift, scale, gate = mod_me[:, :D_MODEL], mod_me[:, D_MODEL:2 * D_MODEL], mod_me[:, 2 * D_MODEL:]

    f_in_t = jnp.concatenate([gw_in[:, 0], gw_in[:, 1]], axis=2).reshape(IN_WIDTH, D_MODEL)
    f_uq, f_ukv, f_a, f_b, f_out = _unpack_small(gw_small.reshape(N_CHIPS, SMALL_TOTAL, LANES))
    w_int_t, w_q, w_kv = _internal_weights_t(f_in_t, f_uq, f_ukv)

    inv_freq = ROPE_BASE ** (-jnp.arange(0, ROPE_DIM, 2, dtype=F32) / ROPE_DIM)
    ang = positions[0].astype(F32)[:, None] * inv_freq
    cs, sn = jnp.cos(ang), jnp.sin(ang)
    cos256 = jnp.tile(jnp.concatenate([cs, cs], axis=1), (1, 8))
    sin256 = jnp.tile(jnp.concatenate([-sn, sn], axis=1), (1, 8))

    (h, sq, sk, sv, sz, cq, ckv, mz, ga, gb, kpt, qn, qp, kn, vv) = _inproj_call(
        xs, shift, scale, norm_gain, w_int_t, w_q, w_kv, q_norm_gain, kv_norm_gain, cos256, sin256)
    oa, lt = _sb4_fwd_call(sq, sk, sv)
    ob, lse = _mla4_fwd_call(qn, qp, kn, kpt, vv)

    gf = final_norm_gain.reshape(1, D_MODEL)
    (dx2, doa, dob, dsz, dmz, dga, dgb, dwo, dwa, dwb, dgf, dgate, loss_p) = _post_call(
        xs, tgt, oa, ob, sz, mz, ga, gb, gate, gf, f_a, f_b, f_out, f_a.T, f_b.T, f_out.T)

    dsq, dsk_t, dsv_t = _sb4_bwd_call(sq, sk, sv, doa, lt)
    dqn, dqp, dkn_t, dkpt_t, dvv_t = _mla4_bwd_call(qn, qp, kn, kpt, vv, ob, dob, lse)

    dproj, dwq, dwkv, dqg, dkvg = _bwdprep_call(
        dsq, dsk_t, dsv_t, dsz, dqn, dqp, dkn_t, dvv_t, dkpt_t, dmz, dga, dgb, cq, ckv, cos256, sin256,
        q_norm_gain, kv_norm_gain, w_q.T, w_kv.T)
    grad_x, dshift, dscale, dg1 = _dh_call(dproj, w_int_t, xs, dx2, scale, norm_gain)
    dwi_t = _dwin_t_call(h, dproj)
    g_in_t, g_uq, g_ukv = _true_weight_grads_t(dwi_t, dwq, dwkv)

    g_in_c = g_in_t.reshape(N_CHIPS, IN_SHARD, D_MODEL)
    g_in_pieces = jnp.stack([g_in_c[:, :, :HALF_D], g_in_c[:, :, HALF_D:]], axis=1).reshape(N_DEV, IN_SHARD, HALF_D)
    g_small = jnp.concatenate([
        _chip_major(g_uq, 192), _chip_major(g_ukv, 256), _chip_major(dwa, 256), _chip_major(dwb, 256),
        dwo.reshape(N_CHIPS, -1, LANES)], axis=1).astype(BF16).reshape(N_DEV, SMALL_HALF, LANES)
    small = jnp.concatenate([
        dshift, dscale, dgate, dg1, dqg, dkvg, dgf, loss_p,
        jnp.zeros((1, 8 * SV_COLS - 5888), F32)], axis=1).reshape(8, SV_COLS)
    full_in, full_small, svg = _reduce2_call(g_in_pieces, g_small, small)
    gs_in_t = jnp.concatenate([full_in[0], full_in[1]], axis=1)
    full = full_small.reshape(SMALL_TOTAL, LANES)
    offs = [0]
    for r in SMALL_ROWS:
        offs.append(offs[-1] + r)
    gs_uq = full[offs[0]:offs[1]].reshape(Q_RANK, 192)
    gs_ukv = full[offs[1]:offs[2]].reshape(KV_RANK, 256)
    gs_a = full[offs[2]:offs[3]].reshape(512, 256)
    gs_b = full[offs[3]:offs[4]].reshape(512, 256)
    gs_out = full[offs[4]:offs[5]].reshape(256, D_MODEL)

    svm = svg.reshape(N_DEV, 8 * SV_COLS)
    dmod_sh = lax.dynamic_slice_in_dim(svm[:, :3 * D_MODEL], chip * 768, 768, axis=1)
    tot, gs_ada = _small_call(svm, call.T, dmod_sh)
    g_bada = tot[:, 0:3072]
    g_g1 = tot[:, 3072:4096]
    g_qg = tot[:, 4096:4480]
    g_kvg = tot[:, 4480:4736]
    g_gf = tot[:, 4736:5760]
    loss = tot[0, 5760]

    names = ["w_ada", "b_ada", "norm_gain", "w_in", "q_norm_gain", "w_uq", "kv_norm_gain", "w_ukv",
             "w_branch_a", "w_branch_b", "w_out", "final_norm_gain"]
    ws = [w_ada[0], b_ada, norm_gain, w_in_t, q_norm_gain, w_uq[0], kv_norm_gain, w_ukv[0],
          w_branch_a[0], w_branch_b[0], w_out[0], final_norm_gain.reshape(1, D_MODEL)]
    gs = [gs_ada, g_bada, g_g1, gs_in_t, g_qg, gs_uq, g_kvg, gs_ukv, gs_a, gs_b, gs_out, g_gf]
    ms = [m_w_ada[0], m_b_ada, m_norm_gain, jnp.swapaxes(m_w_in[0], 0, 1), m_q_norm_gain, m_w_uq[0],
          m_kv_norm_gain, m_w_ukv[0], m_w_branch_a[0], m_w_branch_b[0], m_w_out[0],
          m_final_norm_gain.reshape(1, D_MODEL)]
    vs = [v_w_ada[0], v_b_ada, v_norm_gain, jnp.swapaxes(v_w_in[0], 0, 1), v_q_norm_gain, v_w_uq[0],
          v_kv_norm_gain, v_w_ukv[0], v_w_branch_a[0], v_w_branch_b[0], v_w_out[0],
          v_final_norm_gain.reshape(1, D_MODEL)]
    refs = [w_ada, b_ada, norm_gain, w_in, q_norm_gain, w_uq, kv_norm_gain, w_ukv,
            w_branch_a, w_branch_b, w_out, final_norm_gain]
    grads, deltas, new_ms, new_vs = [], [], [], []
    for n, w_, g_, m_, v_, ref in zip(names, ws, gs, ms, vs, refs):
        outs = (g_,) + _adamw_call(n, w_, g_, m_, v_)
        if n == "w_in":
            outs = tuple(jnp.swapaxes(o_, 0, 1) for o_ in outs)
        for lst, o_ in zip((grads, deltas, new_ms, new_vs), outs):
            lst.append(o_.reshape(ref.shape))

    return (loss, grad_x.reshape(x.shape), *grads, *deltas, *new_ms, *new_vs)
```

```python
import functools
import math

import jax
import jax.numpy as jnp
from jax import lax
from jax.experimental import pallas as pl
from jax.experimental.pallas import tpu as pltpu

F32 = jnp.float32
BF16 = jnp.bfloat16

D_MODEL = 1024
SB_WIDTH = 512
MLA_WIDTH = 512
Q_RANK = 384
KV_RANK = 256
ROPE_DIM = 32
N_HEADS = 8
IN_WIDTH = 5280
EPS = 1e-6
ROPE_BASE = 10000.0
MLA_SCALE = 1.0 / math.sqrt(96.0)

ADAM_LR = 0.001
ADAM_B1 = 0.9
ADAM_B2 = 0.999
ADAM_EPS = 1e-08
ADAM_WD = 0.01
ADAM_STEP = 10

O_SQ, O_SK, O_SV, O_SZ, O_CQ, O_CKV, O_MZ, O_GA, O_GB, O_KR, O_END = (
    0, 512, 1024, 1536, 2048, 2432, 2688, 3200, 4224, 5248, 5504)
W_INT = 5632

N_CHIPS = 4
N_DEV = 8
LANES = 128
PACK_ROWS = (10560, 576, 512, 1024, 1024, 2048)
PACK_TOTAL = sum(PACK_ROWS)
HALF_ROWS = PACK_TOTAL // 2
SV_COLS = 768

ROW_TILE = 256
ATT_TILE = 256
ATT_Q_TILES = 2
VMEM_LIMIT = 56 * 1024 * 1024

MESH = pl.DeviceIdType.MESH


def _dot(a, b):
    return lax.dot_general(a, b, (((1,), (0,)), ((), ())), preferred_element_type=F32)


def _dot_nt(a, b):
    return lax.dot_general(a, b, (((1,), (1,)), ((), ())), preferred_element_type=F32)


def _dot_tn(a, b):
    return lax.dot_general(a, b, (((0,), (0,)), ((), ())), preferred_element_type=F32)


def _sigmoid(z):
    return 1.0 / (1.0 + jnp.exp(-z))


def _params(sem=None):
    if sem is None:
        return pltpu.CompilerParams(vmem_limit_bytes=VMEM_LIMIT)
    return pltpu.CompilerParams(dimension_semantics=sem, vmem_limit_bytes=VMEM_LIMIT)


def _rows(tm, n):
    return pl.BlockSpec((tm, n), lambda i: (i, 0))


def _cols(n, tm):
    return pl.BlockSpec((n, tm), lambda i: (0, i))


def _whole(shape):
    nd = len(shape)
    return pl.BlockSpec(shape, lambda i: (0,) * nd)


def _sds(shape, dtype):
    return jax.ShapeDtypeStruct(shape, dtype)


def _flip(v, d):
    return 1 - v if d else v


def _gather_call(c_row, w_ada_sh, pack):
    def body(c_ref, wada_ref, pk_ref, mg_ref, cg_ref, gw_ref,
             cv, ssem_c, rsem_c, ssem_m, rsem_m, ssem_w, rsem_w, ssem_f, rsem_f, lsem):
        x, y, c = lax.axis_index("x"), lax.axis_index("y"), lax.axis_index("c")
        me = 4 * x + 2 * y + c
        chip = 2 * x + y
        rel3 = [(1, 0), (0, 1), (1, 1)]

        wcopies = []
        for j, (dx, dy) in enumerate(rel3):
            cp = pltpu.make_async_remote_copy(
                src_ref=pk_ref.at[c], dst_ref=gw_ref.at[chip, c], send_sem=ssem_w.at[j], recv_sem=rsem_w.at[j],
                device_id=(_flip(x, dx), _flip(y, dy), c), device_id_type=MESH)
            cp.start()
            wcopies.append(cp)
        own = pltpu.make_async_copy(pk_ref, gw_ref.at[chip], lsem)
        own.start()

        cv[me] = c_ref[...]
        ccopies = []
        for r in range(1, N_DEV):
            dx, dy, dc = (r >> 2) & 1, (r >> 1) & 1, r & 1
            cp = pltpu.make_async_remote_copy(
                src_ref=c_ref, dst_ref=cv.at[me], send_sem=ssem_c.at[r - 1], recv_sem=rsem_c.at[r - 1],
                device_id=(_flip(x, dx), _flip(y, dy), _flip(c, dc)), device_id_type=MESH)
            cp.start()
            ccopies.append(cp)
        for r in range(1, N_DEV):
            dx, dy, dc = (r >> 2) & 1, (r >> 1) & 1, r & 1
            src = 4 * _flip(x, dx) + 2 * _flip(y, dy) + _flip(c, dc)
            pltpu.make_async_remote_copy(
                src_ref=c_ref, dst_ref=cv.at[src], send_sem=ssem_c.at[r - 1], recv_sem=rsem_c.at[r - 1],
                device_id=(x, y, c), device_id_type=MESH).wait_recv()
        rows = lax.broadcasted_iota(jnp.int32, (N_DEV, D_MODEL), 0)
        call = jnp.zeros((N_DEV, D_MODEL), F32)
        for b in range(N_DEV):
            call = jnp.where(rows == b, jnp.broadcast_to(cv[b], (N_DEV, D_MODEL)), call)
        cg_ref[...] = call

        mg_ref[chip] = lax.dot_general(call, wada_ref[...], (((1,), (0,)), ((), ())),
                                       precision=lax.Precision.HIGHEST, preferred_element_type=F32)
        mcopies = []
        for j, (dx, dy) in enumerate(rel3):
            cp = pltpu.make_async_remote_copy(
                src_ref=mg_ref.at[chip], dst_ref=mg_ref.at[chip], send_sem=ssem_m.at[j], recv_sem=rsem_m.at[j],
                device_id=(_flip(x, dx), _flip(y, dy), c), device_id_type=MESH)
            cp.start()
            mcopies.append(cp)
        for j, (dx, dy) in enumerate(rel3):
            src_chip = 2 * _flip(x, dx) + _flip(y, dy)
            pltpu.make_async_remote_copy(
                src_ref=mg_ref.at[src_chip], dst_ref=mg_ref.at[src_chip], send_sem=ssem_m.at[j],
                recv_sem=rsem_m.at[j], device_id=(x, y, c), device_id_type=MESH).wait_recv()
        fcopies = []
        for j, (dx, dy) in enumerate(rel3):
            src_chip = 2 * _flip(x, dx) + _flip(y, dy)
            pltpu.make_async_remote_copy(
                src_ref=pk_ref.at[c], dst_ref=gw_ref.at[src_chip, c], send_sem=ssem_w.at[j], recv_sem=rsem_w.at[j],
                device_id=(x, y, c), device_id_type=MESH).wait_recv()
            cp = pltpu.make_async_remote_copy(
                src_ref=gw_ref.at[src_chip, c], dst_ref=gw_ref.at[src_chip, c], send_sem=ssem_f.at[j],
                recv_sem=rsem_f.at[j], device_id=(x, y, 1 - c), device_id_type=MESH)
            cp.start()
            fcopies.append(cp)
        for j, (dx, dy) in enumerate(rel3):
            src_chip = 2 * _flip(x, dx) + _flip(y, dy)
            pltpu.make_async_remote_copy(
                src_ref=pk_ref.at[c], dst_ref=gw_ref.at[src_chip, 1 - c], send_sem=ssem_f.at[j],
                recv_sem=rsem_f.at[j], device_id=(x, y, c), device_id_type=MESH).wait_recv()
        for cp in ccopies + mcopies + wcopies + fcopies:
            cp.wait_send()
        own.wait()

    vmem = pl.BlockSpec(memory_space=pltpu.VMEM)
    return pl.pallas_call(
        body, name="gather_fwd",
        out_shape=(_sds((N_CHIPS, N_DEV, 768), F32), _sds((N_DEV, D_MODEL), F32),
                   _sds((N_CHIPS, 2, HALF_ROWS, LANES), BF16)),
        in_specs=[vmem, vmem, vmem], out_specs=(vmem, vmem, vmem),
        scratch_shapes=[
            pltpu.VMEM((N_DEV, 1, D_MODEL), F32),
            pltpu.SemaphoreType.DMA((N_DEV - 1,)), pltpu.SemaphoreType.DMA((N_DEV - 1,)),
            pltpu.SemaphoreType.DMA((3,)), pltpu.SemaphoreType.DMA((3,)),
            pltpu.SemaphoreType.DMA((3,)), pltpu.SemaphoreType.DMA((3,)),
            pltpu.SemaphoreType.DMA((3,)), pltpu.SemaphoreType.DMA((3,)),
            pltpu.SemaphoreType.DMA,
        ],
        compiler_params=_params(),
    )(c_row, w_ada_sh, pack)


SUM_ROWS = 656


def _reduce_call(gpack, sv):
    def body(g_ref, sv_ref, full_ref, svg_ref, land, ssem_g, rsem_g, ssem_s, rsem_s, ssem_x, rsem_x, lsem):
        x, y, c = lax.axis_index("x"), lax.axis_index("y"), lax.axis_index("c")
        me = 4 * x + 2 * y + c
        copies = []
        for r in range(1, N_DEV):
            dx, dy, dc = (r >> 2) & 1, (r >> 1) & 1, r & 1
            tx, ty, tc = _flip(x, dx), _flip(y, dy), _flip(c, dc)
            tgt = 4 * tx + 2 * ty + tc
            cp = pltpu.make_async_remote_copy(
                src_ref=g_ref.at[tgt], dst_ref=land.at[me], send_sem=ssem_g.at[r - 1],
                recv_sem=rsem_g.at[r - 1], device_id=(tx, ty, tc), device_id_type=MESH)
            cp.start()
            copies.append(cp)
            cp = pltpu.make_async_remote_copy(
                src_ref=sv_ref, dst_ref=svg_ref.at[me], send_sem=ssem_s.at[r - 1],
                recv_sem=rsem_s.at[r - 1], device_id=(tx, ty, tc), device_id_type=MESH)
            cp.start()
            copies.append(cp)
        own = pltpu.make_async_copy(g_ref.at[me], land.at[me], lsem)
        own.start()
        svg_ref[me] = sv_ref[...]
        for r in range(1, N_DEV):
            dx, dy, dc = (r >> 2) & 1, (r >> 1) & 1, r & 1
            src = 4 * _flip(x, dx) + 2 * _flip(y, dy) + _flip(c, dc)
            pltpu.make_async_remote_copy(
                src_ref=g_ref.at[src], dst_ref=land.at[src], send_sem=ssem_g.at[r - 1],
                recv_sem=rsem_g.at[r - 1], device_id=(x, y, c), device_id_type=MESH).wait_recv()
            pltpu.make_async_remote_copy(
                src_ref=sv_ref, dst_ref=svg_ref.at[src], send_sem=ssem_s.at[r - 1],
                recv_sem=rsem_s.at[r - 1], device_id=(x, y, c), device_id_type=MESH).wait_recv()
        own.wait()

        def sum_rows(i, carry):
            sl = pl.ds(pl.multiple_of(i * SUM_ROWS, 16), SUM_ROWS)
            acc = land[0, sl, :].astype(F32)
            for d in range(1, N_DEV):
                acc = acc + land[d, sl, :].astype(F32)
            full_ref[c, sl, :] = acc
            return carry

        lax.fori_loop(0, HALF_ROWS // SUM_ROWS, sum_rows, 0)
        swap = pltpu.make_async_remote_copy(
            src_ref=full_ref.at[c], dst_ref=full_ref.at[c], send_sem=ssem_x, recv_sem=rsem_x,
            device_id=(x, y, 1 - c), device_id_type=MESH)
        swap.start()
        pltpu.make_async_remote_copy(
            src_ref=full_ref.at[c], dst_ref=full_ref.at[1 - c], send_sem=ssem_x, recv_sem=rsem_x,
            device_id=(x, y, c), device_id_type=MESH).wait_recv()
        swap.wait_send()
        for cp in copies:
            cp.wait_send()

    vmem = pl.BlockSpec(memory_space=pltpu.VMEM)
    return pl.pallas_call(
        body, name="grad_reduce",
        out_shape=(_sds((2, HALF_ROWS, LANES), F32), _sds((N_DEV, 8, SV_COLS), F32)),
        in_specs=[vmem, vmem], out_specs=(vmem, vmem),
        scratch_shapes=[
            pltpu.VMEM((N_DEV, HALF_ROWS, LANES), BF16),
            pltpu.SemaphoreType.DMA((N_DEV - 1,)), pltpu.SemaphoreType.DMA((N_DEV - 1,)),
            pltpu.SemaphoreType.DMA((N_DEV - 1,)), pltpu.SemaphoreType.DMA((N_DEV - 1,)),
            pltpu.SemaphoreType.DMA, pltpu.SemaphoreType.DMA, pltpu.SemaphoreType.DMA,
        ],
        compiler_params=_params(),
    )(gpack, sv)


def _inproj_call(x, shift, scale, g1, w_int, w_q, w_kv, qg, kvg, cos256, sin256):
    s_len = x.shape[0]
    tm = min(ROW_TILE, s_len)

    def body(x_ref, sh_ref, sc_ref, g1_ref, w_ref, wq_ref, wkv_ref, qg_ref, kvg_ref, cos_ref, sin_ref,
             h_ref, sq_ref, sk_ref, sv_ref, sz_ref, cq_ref, ckv_ref, mz_ref, ga_ref, gb_ref, kpt_ref,
             qn_ref, qp_ref, kn_ref, vv_ref):
        xt = x_ref[...]
        r = lax.rsqrt(jnp.mean(xt * xt, axis=-1, keepdims=True) + EPS)
        h = (xt * r * g1_ref[...]) * (1.0 + sc_ref[...]) + sh_ref[...]
        hb = h.astype(BF16)
        h_ref[...] = hb

        def seg(a, b):
            return _dot_nt(hb, w_ref[a:b, :])

        sq_ref[...] = seg(O_SQ, O_SK).astype(BF16)
        sk_ref[...] = seg(O_SK, O_SV).astype(BF16)
        sv_ref[...] = seg(O_SV, O_SZ).astype(BF16)
        sz_ref[...] = seg(O_SZ, O_CQ)
        mz_ref[...] = seg(O_MZ, O_GA)
        ga_ref[...] = seg(O_GA, O_GB)
        gb_ref[...] = seg(O_GB, O_KR)
        cos = cos_ref[...]
        sin = sin_ref[...]
        kr = seg(O_KR, O_END)
        kpt_ref[...] = (kr[:, :128] * cos[:, :128] + kr[:, 128:] * sin[:, :128]).astype(BF16)

        cq = seg(O_CQ, O_CKV)
        cq_ref[...] = cq
        rq = lax.rsqrt(jnp.mean(cq * cq, axis=-1, keepdims=True) + EPS)
        cqn = (cq * rq * qg_ref[...]).astype(BF16)
        qa = _dot(cqn, wq_ref[...])
        qn_ref[...] = qa[:, :512].astype(BF16)
        qp_ref[...] = (qa[:, 512:768] * cos + qa[:, 768:] * sin).astype(BF16)

        ckv = seg(O_CKV, O_MZ)
        ckv_ref[...] = ckv
        rk = lax.rsqrt(jnp.mean(ckv * ckv, axis=-1, keepdims=True) + EPS)
        ckvn = (ckv * rk * kvg_ref[...]).astype(BF16)
        kva = _dot(ckvn, wkv_ref[...])
        kn_ref[...] = kva[:, :512].astype(BF16)
        vv_ref[...] = kva[:, 512:].astype(BF16)

    outs = [
        (D_MODEL, BF16), (512, BF16), (512, BF16), (512, BF16), (512, F32), (Q_RANK, F32), (KV_RANK, F32),
        (512, F32), (D_MODEL, F32), (D_MODEL, F32), (128, BF16), (512, BF16), (256, BF16), (512, BF16), (512, BF16),
    ]
    return pl.pallas_call(
        body, name="inproj", grid=(s_len // tm,),
        out_shape=tuple(_sds((s_len, n), dt) for n, dt in outs),
        in_specs=[_rows(tm, D_MODEL), _whole((1, D_MODEL)), _whole((1, D_MODEL)), _whole((1, D_MODEL)),
                  _whole((W_INT, D_MODEL)), _whole((Q_RANK, 1024)), _whole((KV_RANK, 1024)),
                  _whole((1, Q_RANK)), _whole((1, KV_RANK)), _rows(tm, 256), _rows(tm, 256)],
        out_specs=tuple(_rows(tm, n) for n, _ in outs),
        compiler_params=_params(("parallel",)),
    )(x, shift, scale, g1, w_int, w_q, w_kv, qg, kvg, cos256, sin256)


def _softplus(z):
    return jnp.maximum(z, 0.0) + jnp.log(1.0 + jnp.exp(-jnp.abs(z)))


def _split_bf16(a):
    hi = a.astype(BF16)
    lo = (a - hi.astype(F32)).astype(BF16)
    return hi, lo


def _sb_fwd_call(q, k, v):
    s_len = q.shape[0]
    tk = min(ATT_TILE, s_len)
    tq = min(ATT_Q_TILES * tk, s_len)
    r = tq // tk
    nq = s_len // tq

    def body(q_ref, k_ref, v_ref, o_ref, lt_ref):
        i = pl.program_id(1)
        q2 = q_ref[...]
        lane = lax.broadcasted_iota(jnp.int32, (1, 128), 1)
        krow = lax.broadcasted_iota(jnp.int32, (tk, tk), 0)
        kcol = lax.broadcasted_iota(jnp.int32, (tk, tk), 1)
        row = lax.broadcasted_iota(jnp.int32, (tq, tk), 0)
        col = lax.broadcasted_iota(jnp.int32, (tq, tk), 1)
        later = (krow > kcol).astype(BF16)
        later2 = jnp.concatenate([later, later], axis=0)
        valids = [col + u * tk < row for u in range(r)]
        hms = [(lane // 64) == hh for hh in range(2)]
        qms = [jnp.where(hm, q2, jnp.zeros_like(q2)) for hm in hms]

        def block(j, carry, valid):
            runs, acc = list(carry[:2]), carry[2]
            off = pl.multiple_of(j * tk, tk)
            kb = k_ref[pl.ds(off, tk), :]
            vb = v_ref[pl.ds(off, tk), :]
            ws = []
            for hh in range(2):
                z = _dot_nt(qms[hh], kb)
                lg = -_softplus(z)
                lm = jnp.where(valid, lg, 0.0) if valid is not None else lg
                hi, lo = _split_bf16(lm)
                suf = _dot(jnp.concatenate([hi, lo], axis=1), later2)
                w = jnp.exp(z + lg + suf + runs[hh])
                if valid is not None:
                    w = jnp.where(valid, w, 0.0)
                ws.append(w.astype(BF16))
                runs[hh] = runs[hh] + jnp.sum(lm, axis=1, keepdims=True)
            vstack = jnp.concatenate([jnp.where(hm, vb, jnp.zeros_like(vb)) for hm in hms], axis=0)
            acc = acc + _dot(jnp.concatenate(ws, axis=1), vstack)
            return runs[0], runs[1], acc

        zero = jnp.zeros((tq,1), F32)
        carry = block(i, (zero, zero, jnp.zeros((tq,128), F32)), True)
        carry = lax.fori_loop(1, i + 1, lambda jj, cr: block(i - jj, cr, False), carry)
        lt_ref[0, :, 0:1] = carry[0]
        lt_ref[0, :, 1:2] = carry[1]
        o_ref[...] = carry[2]

    return pl.pallas_call(
        body, name="sb_fwd", grid=(4, nq),
        out_shape=(_sds((s_len, SB_WIDTH), F32), _sds((4, s_len, 2), F32)),
        in_specs=[pl.BlockSpec((tq,128), lambda p, i: (i, p)),
                  pl.BlockSpec((s_len, 128), lambda p, i: (0, p)),
                  pl.BlockSpec((s_len, 128), lambda p, i: (0, p))],
        out_specs=(pl.BlockSpec((tq,128), lambda p, i: (i, p)),
                   pl.BlockSpec((1, t, 2), lambda p, i: (p, i, 0))),
        compiler_params=_params(("parallel", "parallel")),
    )(q, k, v)


def _sb_bwd_call(q, k, v, do, lt):
    s_len = q.shape[0]
    tk = min(ATT_TILE, s_len)
    tq = min(ATT_Q_TILES * tk, s_len)
    r = tq // tk
    nq = s_len // tq

    def body(q_ref, k_ref, v_ref, do_ref, lt_ref, dq_ref, dk_ref, dv_ref):
        i = pl.program_id(1)

        @pl.when(i == 0)
        def _():
            dk_ref[...] = jnp.zeros_like(dk_ref)
            dv_ref[...] = jnp.zeros_like(dv_ref)

        q2 = q_ref[...]
        do2 = do_ref[...].astype(BF16)
        lane = lax.broadcasted_iota(jnp.int32, (1, 128), 1)
        krow = lax.broadcasted_iota(jnp.int32, (tk, tk), 0)
        kcol = lax.broadcasted_iota(jnp.int32, (tk, tk), 1)
        row = lax.broadcasted_iota(jnp.int32, (tq, tk), 0)
        col = lax.broadcasted_iota(jnp.int32, (tq, tk), 1)
        earlier = (krow < kcol).astype(BF16)
        earlier2 = jnp.concatenate([earlier, earlier], axis=0)
        valids = [col + u * tk < row for u in range(r)]
        hms = [(lane // 64) == hh for hh in range(2)]
        qms = [jnp.where(hm, q2, jnp.zeros_like(q2)) for hm in hms]
        doms = [jnp.where(hm, do2, jnp.zeros_like(do2)) for hm in hms]
        ltots = [lt_ref[0, :, hh:hh + 1] for hh in range(2)]
        qstack = jnp.concatenate(qms, axis=0)
        dostack = jnp.concatenate(doms, axis=0)

        def block(j, carry, valid):
            lpre, ppre, dq = list(carry[0:2]), list(carry[2:4]), carry[4]
            off = pl.multiple_of(j * tk, tk)
            kb = k_ref[pl.ds(off, tk), :]
            vb = v_ref[pl.ds(off, tk), :]
            dzs, avs = [], []
            for hh in range(2):
                z = _dot_nt(qms[hh], kb)
                sp = _softplus(z)
                lg = -sp
                lm = jnp.where(valid, lg, 0.0) if valid is not None else lg
                hi, lo = _split_bf16(lm)
                before = _dot(jnp.concatenate([hi, lo], axis=1), earlier2)
                between = ltots[hh] - (lpre[hh] + before + lm)
                a = jnp.exp(z + lg + between)
                if valid is not None:
                    a = jnp.where(valid, a, 0.0)
                p = a * _dot_nt(doms[hh], vb)
                phi, plo = _split_bf16(p)
                pbefore = ppre[hh] + _dot(jnp.concatenate([phi, plo], axis=1), earlier2)
                sig = jnp.exp(z - sp)
                dz = p - sig * (p + pbefore)
                if valid is not None:
                    dz = jnp.where(valid, dz, 0.0)
                dzs.append(dz.astype(BF16))
                avs.append(a.astype(BF16))
                lpre[hh] = lpre[hh] + jnp.sum(lm, axis=1, keepdims=True)
                ppre[hh] = ppre[hh] + jnp.sum(p, axis=1, keepdims=True)
            kstack = jnp.concatenate([jnp.where(hm, kb, jnp.zeros_like(kb)) for hm in hms], axis=0)
            dq = dq + _dot(jnp.concatenate(dzs, axis=1), kstack)
            dk_ref[pl.ds(off, tk), :] += _dot_tn(jnp.concatenate(dzs, axis=0), qstack)
            dv_ref[pl.ds(off, tk), :] += _dot_tn(jnp.concatenate(avs, axis=0), dostack)
            return lpre[0], lpre[1], ppre[0], ppre[1], dq

        zero = jnp.zeros((tq,1), F32)
        carry = lax.fori_loop(0, i, lambda j, cr: block(j, cr, False),
                              (zero, zero, zero, zero, jnp.zeros((tq,128), F32)))
        carry = block(i, carry, True)
        dq_ref[...] = carry[4].astype(BF16)

    return pl.pallas_call(
        body, name="sb_bwd", grid=(4, nq),
        out_shape=(_sds((s_len, SB_WIDTH), BF16), _sds((s_len, SB_WIDTH), F32), _sds((s_len, SB_WIDTH), F32)),
        in_specs=[pl.BlockSpec((tq,128), lambda p, i: (i, p)),
                  pl.BlockSpec((s_len, 128), lambda p, i: (0, p)),
                  pl.BlockSpec((s_len, 128), lambda p, i: (0, p)),
                  pl.BlockSpec((tq,128), lambda p, i: (i, p)),
                  pl.BlockSpec((1, t, 2), lambda p, i: (p, i, 0))],
        out_specs=(pl.BlockSpec((tq,128), lambda p, i: (i, p)),
                   pl.BlockSpec((s_len, 128), lambda p, i: (0, p)),
                   pl.BlockSpec((s_len, 128), lambda p, i: (0, p))),
        compiler_params=_params(("parallel", "arbitrary")),
    )(q, k, v, do, lt)


def _mla_fwd_call(qn, qp, kn, kpt, v):
    s_len = qn.shape[0]
    tk = min(ATT_TILE, s_len)
    tq = min(ATT_Q_TILES * tk, s_len)
    r = tq // tk
    nq = s_len // tq

    def body(qn_ref, qp_ref, kn_ref, kpt_ref, v_ref, o_ref, lse_ref):
        i = pl.program_id(1)
        qn2 = qn_ref[...]
        qp2 = qp_ref[...]
        lane256 = lax.broadcasted_iota(jnp.int32, (1, 256), 1)
        lane128 = lax.broadcasted_iota(jnp.int32, (1, 128), 1)
        krow = lax.broadcasted_iota(jnp.int32, (tk, tk), 0)
        kcol = lax.broadcasted_iota(jnp.int32, (tk, tk), 1)
        row = lax.broadcasted_iota(jnp.int32, (tq, tk), 0)
        col = lax.broadcasted_iota(jnp.int32, (tq, tk), 1)
        valids = [col + u * tk <= row for u in range(r)]
        m64s = [(lane256 // 64) == hh for hh in range(4)]
        m32s = [(lane128 // 32) == hh for hh in range(4)]
        qcs = [jnp.concatenate([jnp.where(m64s[hh], qn2, jnp.zeros_like(qn2)),
                                jnp.where(m32s[hh], qp2, jnp.zeros_like(qp2))], axis=1) for hh in range(4)]

        def by_head(vals):
            return jnp.where(m64s[0], vals[0], jnp.where(m64s[1], vals[1], jnp.where(m64s[2], vals[2], vals[3])))

        def block(j, carry, valid):
            ms, ls, acc = list(carry[0:4]), list(carry[4:8]), carry[8]
            off = pl.multiple_of(j * tk, tk)
            kc = jnp.concatenate([kn_ref[pl.ds(off, tk), :], kpt_ref[pl.ds(off, tk), :]], axis=1)
            vb = v_ref[pl.ds(off, tk), :]
            ps, alphas = [], []
            for hh in range(4):
                s = _dot_nt(qcs[hh], kc) * MLA_SCALE
                if valid is not None:
                    s = jnp.where(valid, s, -1e30)
                mn = jnp.maximum(ms[hh], jnp.max(s, axis=1, keepdims=True))
                p = jnp.exp(s - mn)
                alpha = jnp.exp(ms[hh] - mn)
                ls[hh] = alpha * ls[hh] + jnp.sum(p, axis=1, keepdims=True)
                ms[hh] = mn
                ps.append(p.astype(BF16))
                alphas.append(alpha)
            vstack = jnp.concatenate([jnp.where(m64, vb, jnp.zeros_like(vb)) for m64 in m64s], axis=0)
            acc = by_head(alphas) * acc + _dot(jnp.concatenate(ps, axis=1), vstack)
            return (*ms, *ls, acc)

        neg = jnp.full((tq, 1), -1e30, F32)
        zero = jnp.zeros((tq,1), F32)
        carry = lax.fori_loop(0, i, lambda j, cr: block(j, cr, False),
                              (neg, neg, neg, neg, zero, zero, zero, zero, jnp.zeros((tq,256), F32)))
        carry = block(i, carry, True)
        o_ref[...] = carry[8] / by_head(list(carry[4:8]))
        for hh in range(4):
            lse_ref[0, :, hh:hh + 1] = carry[hh] + jnp.log(carry[4 + hh])

    return pl.pallas_call(
        body, name="mla_fwd", grid=(2, nq),
        out_shape=(_sds((s_len, MLA_WIDTH), F32), _sds((2, s_len, 4), F32)),
        in_specs=[pl.BlockSpec((tq,256), lambda g, i: (i, g)),
                  pl.BlockSpec((tq,128), lambda g, i: (i, g)),
                  pl.BlockSpec((s_len, 256), lambda g, i: (0, g)),
                  pl.BlockSpec((s_len, 128), lambda g, i: (0, 0)),
                  pl.BlockSpec((s_len, 256), lambda g, i: (0, g))],
        out_specs=(pl.BlockSpec((tq,256), lambda g, i: (i, g)),
                   pl.BlockSpec((1, tq, 4), lambda g, i: (g, i, 0))),
        compiler_params=_params(("parallel", "parallel")),
    )(qn, qp, kn, kpt, v)


def _mla_bwd_call(qn, qp, kn, kpt, v, o, do, lse):
    s_len = qn.shape[0]
    tk = min(ATT_TILE, s_len)
    tq = min(ATT_Q_TILES * tk, s_len)
    r = tq // tk
    nq = s_len // tq

    def body(qn_ref, qp_ref, kn_ref, kpt_ref, v_ref, o_ref, do_ref, lse_ref,
             dqn_ref, dqp_ref, dkn_ref, dkpt_ref, dv_ref):
        g = pl.program_id(0)
        i = pl.program_id(1)

        @pl.when(i == 0)
        def _():
            dkn_ref[...] = jnp.zeros_like(dkn_ref)
            dv_ref[...] = jnp.zeros_like(dv_ref)

        @pl.when((i == 0) & (g == 0))
        def _():
            dkpt_ref[...] = jnp.zeros_like(dkpt_ref)

        qn2 = qn_ref[...]
        qp2 = qp_ref[...]
        of = o_ref[...]
        dof = do_ref[...]
        dob = dof.astype(BF16)
        prod = dof * of
        lane256 = lax.broadcasted_iota(jnp.int32, (1, 256), 1)
        lane128 = lax.broadcasted_iota(jnp.int32, (1, 128), 1)
        krow = lax.broadcasted_iota(jnp.int32, (tk, tk), 0)
        kcol = lax.broadcasted_iota(jnp.int32, (tk, tk), 1)
        row = lax.broadcasted_iota(jnp.int32, (tq, tk), 0)
        col = lax.broadcasted_iota(jnp.int32, (tq, tk), 1)
        valids = [col + u * tk <= row for u in range(r)]
        m64s = [(lane256 // 64) == hh for hh in range(4)]
        m32s = [(lane128 // 32) == hh for hh in range(4)]
        qcs = [jnp.concatenate([jnp.where(m64s[hh], qn2, jnp.zeros_like(qn2)),
                                jnp.where(m32s[hh], qp2, jnp.zeros_like(qp2))], axis=1) for hh in range(4)]
        doms = [jnp.where(m64, dob, jnp.zeros_like(dob)) for m64 in m64s]
        dsums = [jnp.sum(jnp.where(m64, prod, 0.0), axis=1, keepdims=True) * MLA_SCALE for m64 in m64s]
        lses = [lse_ref[0, :, hh:hh + 1] for hh in range(4)]
        qstack = jnp.concatenate(qcs, axis=0)
        dostack = jnp.concatenate(doms, axis=0)

        def block(j, dqc, diag):
            off = pl.multiple_of(j * tk, tk)
            knb = kn_ref[pl.ds(off, tk), :]
            kpb = kpt_ref[pl.ds(off, tk), :]
            vb = v_ref[pl.ds(off, tk), :]
            kc = jnp.concatenate([knb, kpb], axis=1)
            dss, pbs = [], []
            for hh in range(4):
                s = _dot_nt(qcs[hh], kc) * MLA_SCALE
                if valid is not None:
                    s = jnp.where(valid, s, -1e30)
                p = jnp.exp(s - lses[hh])
                ds = p * (_dot_nt(doms[hh], vb) * MLA_SCALE - dsums[hh])
                dss.append(ds.astype(BF16))
                pbs.append(p.astype(BF16))
            kstack = jnp.concatenate(
                [jnp.concatenate([jnp.where(m64s[hh], knb, jnp.zeros_like(knb)),
                                  jnp.where(m32s[hh], kpb, jnp.zeros_like(kpb))], axis=1) for hh in range(4)], axis=0)
            dqc = dqc + _dot(jnp.concatenate(dss, axis=1), kstack)
            dkc = _dot_tn(jnp.concatenate(dss, axis=0), qstack)
            dkn_ref[pl.ds(off, tk), :] += dkc[:, :256]
            dkpt_ref[pl.ds(off, tk), :] += dkc[:, 256:]
            dv_ref[pl.ds(off, tk), :] += _dot_tn(jnp.concatenate(pbs, axis=0), dostack)
            return dqc

        dqc = lax.fori_loop(0, i, lambda j, cr: block(j, cr, False), jnp.zeros((tq,384), F32))
        dqc = block(i, dqc, True)
        dqn_ref[...] = dqc[:, :256].astype(BF16)
        dqp_ref[...] = dqc[:, 256:].astype(BF16)

    return pl.pallas_call(
        body, name="mla_bwd", grid=(2, nq),
        out_shape=(_sds((s_len, 512), BF16), _sds((s_len, 256), BF16), _sds((s_len, 512), F32),
                   _sds((s_len, 128), F32), _sds((s_len, 512), F32)),
        in_specs=[pl.BlockSpec((tq,256), lambda g, i: (i, g)),
                  pl.BlockSpec((tq,128), lambda g, i: (i, g)),
                  pl.BlockSpec((s_len, 256), lambda g, i: (0, g)),
                  pl.BlockSpec((s_len, 128), lambda g, i: (0, 0)),
                  pl.BlockSpec((s_len, 256), lambda g, i: (0, g)),
                  pl.BlockSpec((tq,256), lambda g, i: (i, g)),
                  pl.BlockSpec((tq,256), lambda g, i: (i, g)),
                  pl.BlockSpec((1, tq, 4), lambda g, i: (g, i, 0))],
        out_specs=(pl.BlockSpec((tq,256), lambda g, i: (i, g)),
                   pl.BlockSpec((tq,128), lambda g, i: (i, g)),
                   pl.BlockSpec((s_len, 256), lambda g, i: (0, g)),
                   pl.BlockSpec((s_len, 128), lambda g, i: (0, 0)),
                   pl.BlockSpec((s_len, 256), lambda g, i: (0, g))),
        compiler_params=_params(("arbitrary", "arbitrary")),
    )(qn, qp, kn, kpt, v, o, do, lse)


Z_CLAMP = 80.0


def _softplus_clamped(z):
    zc = jnp.minimum(z, Z_CLAMP)
    return zc, jnp.log(1.0 + jnp.exp(zc))


def _tri_sum(a, tri, tri2, passes):
    if passes == 1:
        return _dot(a.astype(BF16), tri)
    hi, lo = _split_bf16(a)
    return _dot(jnp.concatenate([hi, lo], axis=1), tri2)


def _sb4_fwd_call(q, k, v):
    s_len = q.shape[0]
    tk = min(ATT_TILE, s_len)
    tq = min(ATT_Q_TILES * tk, s_len)
    r = tq // tk
    nq = s_len // tq

    def body(q_ref, k_ref, v_ref, o_ref, lt_ref):
        i = pl.program_id(1)
        q2 = q_ref[...]
        lane = lax.broadcasted_iota(jnp.int32, (1, 256), 1)
        krow = lax.broadcasted_iota(jnp.int32, (tk, tk), 0)
        kcol = lax.broadcasted_iota(jnp.int32, (tk, tk), 1)
        row = lax.broadcasted_iota(jnp.int32, (tq, tk), 0)
        col = lax.broadcasted_iota(jnp.int32, (tq, tk), 1)
        later = (krow > kcol).astype(BF16)
        later2 = jnp.concatenate([later, later], axis=0)
        valids = [col + u * tk < row for u in range(r)]
        hms = [(lane // 64) == hh for hh in range(4)]
        qms = [jnp.where(hm, q2, jnp.zeros_like(q2)) for hm in hms]

        def block(j, carry, valid):
            runs, acc = list(carry[:4]), carry[4]
            off = pl.multiple_of(j * tk, tk)
            kb = k_ref[pl.ds(off, tk), :]
            vb = v_ref[pl.ds(off, tk), :]
            ws = []
            for hh in range(4):
                zc, sp = _softplus_clamped(_dot_nt(qms[hh], kb))
                lm = jnp.where(valid, sp, 0.0) if valid is not None else sp
                suf = _tri_sum(lm, later, later2, 1)
                w = jnp.exp(zc - sp - suf - runs[hh])
                if valid is not None:
                    w = jnp.where(valid, w, 0.0)
                ws.append(w.astype(BF16))
                runs[hh] = runs[hh] + jnp.sum(lm, axis=1, keepdims=True)
            vstack = jnp.concatenate([jnp.where(hm, vb, jnp.zeros_like(vb)) for hm in hms], axis=0)
            acc = acc + _dot(jnp.concatenate(ws, axis=1), vstack)
            return (*runs, acc)

        zero = jnp.zeros((tq,1), F32)
        carry = (zero, zero, zero, zero, jnp.zeros((tq, 256), F32))
        for u in reversed(range(r)):
            carry = block(i * r + u, carry, valids[u])
        carry = lax.fori_loop(0, i * r, lambda jj, cr: block(i * r - 1 - jj, cr, None), carry)
        for hh in range(4):
            lt_ref[0, :, hh:hh + 1] = carry[hh]
        o_ref[...] = carry[4]

    return pl.pallas_call(
        body, name="sb_fwd", grid=(2, nq),
        out_shape=(_sds((s_len, SB_WIDTH), F32), _sds((2, s_len, 4), F32)),
        in_specs=[pl.BlockSpec((tq,256), lambda g, i: (i, g)),
                  pl.BlockSpec((s_len, 256), lambda g, i: (0, g)),
                  pl.BlockSpec((s_len, 256), lambda g, i: (0, g))],
        out_specs=(pl.BlockSpec((tq,256), lambda g, i: (i, g)),
                   pl.BlockSpec((1, tq, 4), lambda g, i: (g, i, 0))),
        compiler_params=_params(("parallel", "parallel")),
    )(q, k, v)


def _sb4_bwd_call(q, k, v, do, lt):
    s_len = q.shape[0]
    tk = min(ATT_TILE, s_len)
    tq = min(ATT_Q_TILES * tk, s_len)
    r = tq // tk
    nq = s_len // tq

    def body(q_ref, k_ref, v_ref, do_ref, lt_ref, dq_ref, dk_ref, dv_ref):
        i = pl.program_id(1)

        @pl.when(i == 0)
        def _():
            dk_ref[...] = jnp.zeros_like(dk_ref)
            dv_ref[...] = jnp.zeros_like(dv_ref)

        q2 = q_ref[...]
        do2 = do_ref[...].astype(BF16)
        lane = lax.broadcasted_iota(jnp.int32, (1, 256), 1)
        krow = lax.broadcasted_iota(jnp.int32, (tk, tk), 0)
        kcol = lax.broadcasted_iota(jnp.int32, (tk, tk), 1)
        row = lax.broadcasted_iota(jnp.int32, (tq, tk), 0)
        col = lax.broadcasted_iota(jnp.int32, (tq, tk), 1)
        earlier = (krow < kcol).astype(BF16)
        earlier2 = jnp.concatenate([earlier, earlier], axis=0)
        later = (krow > kcol).astype(BF16)
        later2 = jnp.concatenate([later, later], axis=0)
        valids = [col + u * tk < row for u in range(r)]
        hms = [(lane // 64) == hh for hh in range(4)]
        qms = [jnp.where(hm, q2, jnp.zeros_like(q2)) for hm in hms]
        doms = [jnp.where(hm, do2, jnp.zeros_like(do2)) for hm in hms]
        ltots = [lt_ref[0, :, hh:hh + 1] for hh in range(4)]
        q2t = jnp.transpose(q2.astype(F32))
        do2t = jnp.transpose(do_ref[...])
        subl = lax.broadcasted_iota(jnp.int32, (256, 1), 0)
        qtstack = jnp.concatenate(
            [jnp.where((subl // 64) == hh, q2t, 0.0).astype(BF16) for hh in range(4)], axis=1)
        dotstack = jnp.concatenate(
            [jnp.where((subl // 64) == hh, do2t, 0.0).astype(BF16) for hh in range(4)], axis=1)

        def block(j, carry, valid):
            lpre, ppre, dq = list(carry[0:4]), list(carry[4:8]), carry[8]
            off = pl.multiple_of(j * tk, tk)
            kb = k_ref[pl.ds(off, tk), :]
            vb = v_ref[pl.ds(off, tk), :]
            dzs, avs = [], []
            for hh in range(4):
                zc, sp = _softplus_clamped(_dot_nt(qms[hh], kb))
                lsig = zc - sp
                lm = jnp.where(valid, sp, 0.0) if valid is not None else sp
                rowsum = jnp.sum(lm, axis=1, keepdims=True)
                between = _tri_sum(lm, later, later2, 1) + ((ltots[hh] - lpre[hh]) - rowsum)
                a = jnp.exp(lsig - between)
                if valid is not None:
                    a = jnp.where(valid, a, 0.0)
                p = a * _dot_nt(doms[hh], vb)
                pbefore = ppre[hh] + _tri_sum(p, earlier, earlier2, 1)
                dz = p - jnp.exp(lsig) * (p + pbefore)
                if valid is not None:
                    dz = jnp.where(valid, dz, 0.0)
                dzs.append(dz.astype(BF16))
                avs.append(a.astype(BF16))
                lpre[hh] = lpre[hh] + rowsum
                ppre[hh] = ppre[hh] + jnp.sum(p, axis=1, keepdims=True)
            kstack = jnp.concatenate([jnp.where(hm, kb, jnp.zeros_like(kb)) for hm in hms], axis=0)
            dq = dq + _dot(jnp.concatenate(dzs, axis=1), kstack)
            dk_ref[:, pl.ds(off, tk)] += _dot(qtstack, jnp.concatenate(dzs, axis=0))
            dv_ref[:, pl.ds(off, tk)] += _dot(dotstack, jnp.concatenate(avs, axis=0))
            return (*lpre, *ppre, dq)

        zero = jnp.zeros((tq,1), F32)
        carry = lax.fori_loop(0, i * r, lambda j, cr: block(j, cr, None),
                              (zero,) * 8 + (jnp.zeros((tq, 256), F32),))
        for u in range(r):
            carry = block(i * r + u, carry, valids[u])
        dq_ref[...] = carry[8].astype(BF16)

    return pl.pallas_call(
        body, name="sb_bwd", grid=(2, nq),
        out_shape=(_sds((s_len, SB_WIDTH), BF16), _sds((SB_WIDTH, s_len), F32), _sds((SB_WIDTH, s_len), F32)),
        in_specs=[pl.BlockSpec((tq,256), lambda g, i: (i, g)),
                  pl.BlockSpec((s_len, 256), lambda g, i: (0, g)),
                  pl.BlockSpec((s_len, 256), lambda g, i: (0, g)),
                  pl.BlockSpec((tq,256), lambda g, i: (i, g)),
                  pl.BlockSpec((1, tq, 4), lambda g, i: (g, i, 0))],
        out_specs=(pl.BlockSpec((tq,256), lambda g, i: (i, g)),
                   pl.BlockSpec((256, s_len), lambda g, i: (g, 0)),
                   pl.BlockSpec((256, s_len), lambda g, i: (g, 0))),
        compiler_params=_params(("parallel", "arbitrary")),
    )(q, k, v, do, lt)


def _mla4_fwd_call(qn, qp, kn, kpt, v):
    s_len = qn.shape[0]
    tk = min(ATT_TILE, s_len)
    tq = min(ATT_Q_TILES * tk, s_len)
    r = tq // tk
    nq = s_len // tq

    def body(qn_ref, qp_ref, kn_ref, kpt_ref, v_ref, o_ref, lse_ref):
        i = pl.program_id(1)
        qn2 = qn_ref[...]
        qp2 = qp_ref[...]
        lane256 = lax.broadcasted_iota(jnp.int32, (1, 256), 1)
        lane128 = lax.broadcasted_iota(jnp.int32, (1, 128), 1)
        krow = lax.broadcasted_iota(jnp.int32, (tk, tk), 0)
        kcol = lax.broadcasted_iota(jnp.int32, (tk, tk), 1)
        row = lax.broadcasted_iota(jnp.int32, (tq, tk), 0)
        col = lax.broadcasted_iota(jnp.int32, (tq, tk), 1)
        valids = [col + u * tk <= row for u in range(r)]
        m64s = [(lane256 // 64) == hh for hh in range(4)]
        half = [(lane128 // 64) == u for u in range(2)]
        m32s = [(lane128 // 32) == hh for hh in range(4)]
        qcs = []
        for hh in range(4):
            qpair = qn2[:, 128 * (hh // 2):128 * (hh // 2) + 128]
            qcs.append(jnp.concatenate([jnp.where(half[hh % 2], qpair, jnp.zeros_like(qpair)),
                                        jnp.where(m32s[hh], qp2, jnp.zeros_like(qp2))], axis=1))

        def by_head(vals):
            return jnp.where(m64s[0], vals[0], jnp.where(m64s[1], vals[1], jnp.where(m64s[2], vals[2], vals[3])))

        def block(j, carry, valid):
            ms, ls, acc = list(carry[0:4]), list(carry[4:8]), carry[8]
            off = pl.multiple_of(j * tk, tk)
            knb = kn_ref[pl.ds(off, tk), :]
            kpb = kpt_ref[pl.ds(off, tk), :]
            vb = v_ref[pl.ds(off, tk), :]
            kcs = [jnp.concatenate([knb[:, 128 * pp:128 * pp + 128], kpb], axis=1) for pp in range(2)]
            ps, alphas = [], []
            for hh in range(4):
                s = _dot_nt(qcs[hh], kcs[hh // 2]) * MLA_SCALE
                if valid is not None:
                    s = jnp.where(valid, s, -1e30)
                mn = jnp.maximum(ms[hh], jnp.max(s, axis=1, keepdims=True))
                p = jnp.exp(s - mn)
                alpha = jnp.exp(ms[hh] - mn)
                ls[hh] = alpha * ls[hh] + jnp.sum(p, axis=1, keepdims=True)
                ms[hh] = mn
                ps.append(p.astype(BF16))
                alphas.append(alpha)
            pvs = []
            for pp in range(2):
                vpair = vb[:, 128 * pp:128 * pp + 128]
                vstack = jnp.concatenate([jnp.where(hf, vpair, jnp.zeros_like(vpair)) for hf in half], axis=0)
                pvs.append(_dot(jnp.concatenate(ps[2 * pp:2 * pp + 2], axis=1), vstack))
            acc = by_head(alphas) * acc + jnp.concatenate(pvs, axis=1)
            return (*ms, *ls, acc)

        neg = jnp.full((tq, 1), -1e30, F32)
        zero = jnp.zeros((tq,1), F32)
        carry = lax.fori_loop(0, i * r, lambda j, cr: block(j, cr, None),
                              (neg,) * 4 + (zero,) * 4 + (jnp.zeros((tq, 256), F32),))
        for u in range(r):
            carry = block(i * r + u, carry, valids[u])
        o_ref[...] = carry[8] / by_head(list(carry[4:8]))
        for hh in range(4):
            lse_ref[0, :, hh:hh + 1] = carry[hh] + jnp.log(carry[4 + hh])

    return pl.pallas_call(
        body, name="mla_fwd", grid=(2, nq),
        out_shape=(_sds((s_len, MLA_WIDTH), F32), _sds((2, s_len, 4), F32)),
        in_specs=[pl.BlockSpec((tq,256), lambda g, i: (i, g)),
                  pl.BlockSpec((tq,128), lambda g, i: (i, g)),
                  pl.BlockSpec((s_len, 256), lambda g, i: (0, g)),
                  pl.BlockSpec((s_len, 128), lambda g, i: (0, 0)),
                  pl.BlockSpec((s_len, 256), lambda g, i: (0, g))],
        out_specs=(pl.BlockSpec((tq,256), lambda g, i: (i, g)),
                   pl.BlockSpec((1, tq, 4), lambda g, i: (g, i, 0))),
        compiler_params=_params(("parallel", "parallel")),
    )(qn, qp, kn, kpt, v)


def _mla4_bwd_call(qn, qp, kn, kpt, v, o, do, lse):
    s_len = qn.shape[0]
    tk = min(ATT_TILE, s_len)
    tq = min(ATT_Q_TILES * tk, s_len)
    r = tq // tk
    nq = s_len // tq

    def body(qn_ref, qp_ref, kn_ref, kpt_ref, v_ref, o_ref, do_ref, lse_ref,
             dqn_ref, dqp_ref, dkn_ref, dkpt_ref, dv_ref):
        g = pl.program_id(0)
        i = pl.program_id(1)

        @pl.when(i == 0)
        def _():
            dkn_ref[...] = jnp.zeros_like(dkn_ref)
            dv_ref[...] = jnp.zeros_like(dv_ref)

        @pl.when((i == 0) & (g == 0))
        def _():
            dkpt_ref[...] = jnp.zeros_like(dkpt_ref)

        qn2 = qn_ref[...]
        qp2 = qp_ref[...]
        dof = do_ref[...]
        dob = dof.astype(BF16)
        prod = dof * o_ref[...]
        lane256 = lax.broadcasted_iota(jnp.int32, (1, 256), 1)
        lane128 = lax.broadcasted_iota(jnp.int32, (1, 128), 1)
        krow = lax.broadcasted_iota(jnp.int32, (tk, tk), 0)
        kcol = lax.broadcasted_iota(jnp.int32, (tk, tk), 1)
        row = lax.broadcasted_iota(jnp.int32, (tq, tk), 0)
        col = lax.broadcasted_iota(jnp.int32, (tq, tk), 1)
        valids = [col + u * tk <= row for u in range(r)]
        m64s = [(lane256 // 64) == hh for hh in range(4)]
        half = [(lane128 // 64) == u for u in range(2)]
        m32s = [(lane128 // 32) == hh for hh in range(4)]
        qcs, doms = [], []
        for hh in range(4):
            sl = slice(128 * (hh // 2), 128 * (hh // 2) + 128)
            qpair = qn2[:, sl]
            dpair = dob[:, sl]
            qcs.append(jnp.concatenate([jnp.where(half[hh % 2], qpair, jnp.zeros_like(qpair)),
                                        jnp.where(m32s[hh], qp2, jnp.zeros_like(qp2))], axis=1))
            doms.append(jnp.where(half[hh % 2], dpair, jnp.zeros_like(dpair)))
        dsums = [jnp.sum(jnp.where(m64, prod, 0.0), axis=1, keepdims=True) * MLA_SCALE for m64 in m64s]
        lses = [lse_ref[0, :, hh:hh + 1] for hh in range(4)]
        qn2t = jnp.transpose(qn2.astype(F32))
        qp2t = jnp.transpose(qp2.astype(F32))
        do2t = jnp.transpose(dof)
        sub128 = lax.broadcasted_iota(jnp.int32, (128, 1), 0)
        qtstacks, dotstacks = [], []
        for pp in range(2):
            qts, dts = [], []
            for u in range(2):
                hh = 2 * pp + u
                qts.append(jnp.concatenate(
                    [jnp.where((sub128 // 64) == u, qn2t[128 * pp:128 * pp + 128, :], 0.0),
                     jnp.where((sub128 // 32) == hh, qp2t, 0.0)], axis=0).astype(BF16))
                dts.append(jnp.where((sub128 // 64) == u, do2t[128 * pp:128 * pp + 128, :], 0.0).astype(BF16))
            qtstacks.append(jnp.concatenate(qts, axis=1))
            dotstacks.append(jnp.concatenate(dts, axis=1))

        def block(j, carry, valid):
            dqn, dqp = carry
            off = pl.multiple_of(j * tk, tk)
            knb = kn_ref[pl.ds(off, tk), :]
            kpb = kpt_ref[pl.ds(off, tk), :]
            vb = v_ref[pl.ds(off, tk), :]
            dqn_parts = []
            dkp = None
            for pp in range(2):
                sl = slice(128 * pp, 128 * pp + 128)
                knp = knb[:, sl]
                vpair = vb[:, sl]
                kc = jnp.concatenate([knp, kpb], axis=1)
                dss, pbs, kcms = [], [], []
                for u in range(2):
                    hh = 2 * pp + u
                    s = _dot_nt(qcs[hh], kc) * MLA_SCALE
                    if valid is not None:
                        s = jnp.where(valid, s, -1e30)
                    p = jnp.exp(s - lses[hh])
                    ds = p * (_dot_nt(doms[hh], vpair) * MLA_SCALE - dsums[hh])
                    dss.append(ds.astype(BF16))
                    pbs.append(p.astype(BF16))
                    kcms.append(jnp.concatenate([jnp.where(half[u], knp, jnp.zeros_like(knp)),
                                                 jnp.where(m32s[hh], kpb, jnp.zeros_like(kpb))], axis=1))
                dqc = _dot(jnp.concatenate(dss, axis=1), jnp.concatenate(kcms, axis=0))
                dqn_parts.append(dqc[:, :128])
                dqp = dqp + dqc[:, 128:]
                dkc = _dot(qtstacks[pp], jnp.concatenate(dss, axis=0))
                dkn_ref[128 * pp:128 * pp + 128, pl.ds(off, tk)] += dkc[:128, :]
                dkp = dkc[128:, :] if dkp is None else dkp + dkc[128:, :]
                dv_ref[128 * pp:128 * pp + 128, pl.ds(off, tk)] += _dot(dotstacks[pp], jnp.concatenate(pbs, axis=0))
            dqn = dqn + jnp.concatenate(dqn_parts, axis=1)
            dkpt_ref[:, pl.ds(off, tk)] += dkp
            return dqn, dqp

        carry = lax.fori_loop(0, i * r, lambda j, cr: block(j, cr, None),
                              (jnp.zeros((tq, 256), F32), jnp.zeros((tq, 128), F32)))
        for u in range(r):
            carry = block(i * r + u, carry, valids[u])
        dqn, dqp = carry
        dqn_ref[...] = dqn.astype(BF16)
        dqp_ref[...] = dqp.astype(BF16)

    return pl.pallas_call(
        body, name="mla_bwd", grid=(2, nq),
        out_shape=(_sds((s_len, 512), BF16), _sds((s_len, 256), BF16), _sds((512, s_len), F32),
                   _sds((128, s_len), F32), _sds((512, s_len), F32)),
        in_specs=[pl.BlockSpec((tq,256), lambda g, i: (i, g)),
                  pl.BlockSpec((tq,128), lambda g, i: (i, g)),
                  pl.BlockSpec((s_len, 256), lambda g, i: (0, g)),
                  pl.BlockSpec((s_len, 128), lambda g, i: (0, 0)),
                  pl.BlockSpec((s_len, 256), lambda g, i: (0, g)),
                  pl.BlockSpec((tq,256), lambda g, i: (i, g)),
                  pl.BlockSpec((tq,256), lambda g, i: (i, g)),
                  pl.BlockSpec((1, tq, 4), lambda g, i: (g, i, 0))],
        out_specs=(pl.BlockSpec((tq,256), lambda g, i: (i, g)),
                   pl.BlockSpec((tq,128), lambda g, i: (i, g)),
                   pl.BlockSpec((256, s_len), lambda g, i: (g, 0)),
                   pl.BlockSpec((128, s_len), lambda g, i: (0, 0)),
                   pl.BlockSpec((256, s_len), lambda g, i: (g, 0))),
        compiler_params=_params(("arbitrary", "arbitrary")),
    )(qn, qp, kn, kpt, v, o, do, lse)


def _post_call(x, tgt, oa, ob, sz, mz, ga, gb, gate, gf, wa, wb, wo, wat, wbt, wot):
    s_len = x.shape[0]
    tm = min(ROW_TILE, s_len)

    def body(x_ref, t_ref, oa_ref, ob_ref, sz_ref, mz_ref, ga_ref, gb_ref, gate_ref, gf_ref,
             wa_ref, wb_ref, wo_ref, wat_ref, wbt_ref, wot_ref,
             dx2_ref, doa_ref, dob_ref, dsz_ref, dmz_ref, dga_ref, dgb_ref,
             dwo_ref, dwa_ref, dwb_ref, dgf_ref, dgate_ref, loss_ref):
        @pl.when(pl.program_id(0) == 0)
        def _():
            dwo_ref[...] = jnp.zeros_like(dwo_ref)
            dwa_ref[...] = jnp.zeros_like(dwa_ref)
            dwb_ref[...] = jnp.zeros_like(dwb_ref)
            dgf_ref[...] = jnp.zeros_like(dgf_ref)
            dgate_ref[...] = jnp.zeros_like(dgate_ref)
            loss_ref[...] = jnp.zeros_like(loss_ref)

        gate = gate_ref[...]
        gf = gf_ref[...]
        oa = oa_ref[...]
        ob = ob_ref[...]
        sz = sz_ref[...]
        mz = mz_ref[...]
        sa = _sigmoid(sz)
        sb = _sigmoid(mz)
        silu_a = sz * sa
        silu_b = mz * sb
        ua = (oa * silu_a).astype(BF16)
        ub = (ob * silu_b).astype(BF16)
        ya = _dot(ua, wa_ref[...])
        yb = _dot(ub, wb_ref[...])
        sga = _sigmoid(ga_ref[...])
        sgb = _sigmoid(gb_ref[...])
        merged = (sga * ya + sgb * yb).astype(BF16)
        out = _dot(merged, wo_ref[...])
        x2 = x_ref[...] + gate * out
        r2 = lax.rsqrt(jnp.mean(x2 * x2, axis=-1, keepdims=True) + EPS)
        xhat = x2 * r2
        err = xhat * gf - t_ref[...]
        loss_ref[...] += 0.5 * jnp.sum(jnp.sum(err * err, axis=1, keepdims=True), axis=0, keepdims=True) / D_MODEL
        dy = err * (1.0 / D_MODEL)
        dgf_ref[...] += jnp.sum(dy * xhat, axis=0, keepdims=True)
        dxhat = dy * gf
        dx2 = r2 * (dxhat - xhat * jnp.mean(dxhat * xhat, axis=-1, keepdims=True))
        dx2_ref[...] = dx2
        dgate_ref[...] += jnp.sum(dx2 * out, axis=0, keepdims=True)
        dout = (dx2 * gate).astype(BF16)
        dmerged = _dot(dout, wot_ref[...])
        dwo_ref[...] += _dot_tn(merged, dout)
        dya = dmerged * sga
        dyb = dmerged * sgb
        dga_ref[...] = (dya * ya * (1.0 - sga)).astype(BF16)
        dgb_ref[...] = (dyb * yb * (1.0 - sgb)).astype(BF16)
        dyab = dya.astype(BF16)
        dybb = dyb.astype(BF16)
        dua = _dot(dyab, wat_ref[...])
        dub = _dot(dybb, wbt_ref[...])
        dwa_ref[...] += _dot_tn(ua, dyab)
        dwb_ref[...] += _dot_tn(ub, dybb)
        doa_ref[...] = dua * silu_a
        dob_ref[...] = dub * silu_b
        dsz_ref[...] = (dua * oa * (sa * (1.0 + sz * (1.0 - sa)))).astype(BF16)
        dmz_ref[...] = (dub * ob * (sb * (1.0 + mz * (1.0 - sb)))).astype(BF16)

    return pl.pallas_call(
        body, name="post", grid=(s_len // tm,),
        out_shape=(_sds((s_len, D_MODEL), F32), _sds((s_len, 512), F32), _sds((s_len, 512), F32),
                   _sds((s_len, 512), BF16), _sds((s_len, 512), BF16),
                   _sds((s_len, D_MODEL), BF16), _sds((s_len, D_MODEL), BF16),
                   _sds((D_MODEL, D_MODEL), F32), _sds((512, D_MODEL), F32), _sds((512, D_MODEL), F32),
                   _sds((1, D_MODEL), F32), _sds((1, D_MODEL), F32), _sds((1, 128), F32)),
        in_specs=[_rows(tm, D_MODEL), _rows(tm, D_MODEL), _rows(tm, 512), _rows(tm, 512), _rows(tm, 512),
                  _rows(tm, 512), _rows(tm, D_MODEL), _rows(tm, D_MODEL), _whole((1, D_MODEL)), _whole((1, D_MODEL)),
                  _whole((512, D_MODEL)), _whole((512, D_MODEL)), _whole((D_MODEL, D_MODEL)),
                  _whole((D_MODEL, 512)), _whole((D_MODEL, 512)), _whole((D_MODEL, D_MODEL))],
        out_specs=(_rows(tm, D_MODEL), _rows(tm, 512), _rows(tm, 512), _rows(tm, 512), _rows(tm, 512),
                   _rows(tm, D_MODEL), _rows(tm, D_MODEL),
                   _whole((D_MODEL, D_MODEL)), _whole((512, D_MODEL)), _whole((512, D_MODEL)),
                   _whole((1, D_MODEL)), _whole((1, D_MODEL)), _whole((1, 128))),
        compiler_params=_params(("arbitrary",)),
    )(x, tgt, oa, ob, sz, mz, ga, gb, gate, gf, wa, wb, wo, wat, wbt, wot)


def _bwdprep_call(dsq, dsk, dsv, dsz, dqn, dqp, dkn, dvv, dkpt, dmz, dga, dgb, cq, ckv, cos256, sin256,
                  qg, kvg, wqt, wkvt):
    s_len = cq.shape[0]
    tm = min(ROW_TILE, s_len)

    def body(dsq_ref, dsk_ref, dsv_ref, dsz_ref, dqn_ref, dqp_ref, dkn_ref, dvv_ref, dkpt_ref, dmz_ref,
             dga_ref, dgb_ref, cq_ref, ckv_ref, cos_ref, sin_ref, qg_ref, kvg_ref, wqt_ref, wkvt_ref,
             dp_ref, dwq_ref, dwkv_ref, dqg_ref, dkvg_ref):
        @pl.when(pl.program_id(0) == 0)
        def _():
            dwq_ref[...] = jnp.zeros_like(dwq_ref)
            dwkv_ref[...] = jnp.zeros_like(dwkv_ref)
            dqg_ref[...] = jnp.zeros_like(dqg_ref)
            dkvg_ref[...] = jnp.zeros_like(dkvg_ref)

        cos = cos_ref[...]
        sin = sin_ref[...]
        dp_ref[:, O_SQ:O_SK] = dsq_ref[...]
        dp_ref[:, O_SK:O_SV] = jnp.transpose(dsk_ref[...]).astype(BF16)
        dp_ref[:, O_SV:O_SZ] = jnp.transpose(dsv_ref[...]).astype(BF16)
        dp_ref[:, O_SZ:O_CQ] = dsz_ref[...]
        dp_ref[:, O_MZ:O_GA] = dmz_ref[...]
        dp_ref[:, O_GA:O_GB] = dga_ref[...]
        dp_ref[:, O_GB:O_KR] = dgb_ref[...]
        dkp = jnp.transpose(dkpt_ref[...])
        dp_ref[:, O_KR:O_KR + 128] = (dkp * cos[:, :128]).astype(BF16)
        dp_ref[:, O_KR + 128:O_END] = (dkp * sin[:, :128]).astype(BF16)
        dp_ref[:, O_END:W_INT] = jnp.zeros((tm, W_INT - O_END), BF16)

        cq = cq_ref[...]
        rq = lax.rsqrt(jnp.mean(cq * cq, axis=-1, keepdims=True) + EPS)
        cqh = cq * rq
        qg = qg_ref[...]
        cqn = (cqh * qg).astype(BF16)
        dqp = dqp_ref[...].astype(F32)
        dqa = jnp.concatenate([dqn_ref[...], (dqp * cos).astype(BF16), (dqp * sin).astype(BF16)], axis=1)
        dcqn = _dot(dqa, wqt_ref[...])
        dwq_ref[...] += _dot_tn(cqn, dqa)
        dqg_ref[...] += jnp.sum(dcqn * cqh, axis=0, keepdims=True)
        dh = dcqn * qg
        dcq = rq * (dh - cqh * jnp.mean(dh * cqh, axis=-1, keepdims=True))
        dp_ref[:, O_CQ:O_CKV] = dcq.astype(BF16)

        ckv = ckv_ref[...]
        rk = lax.rsqrt(jnp.mean(ckv * ckv, axis=-1, keepdims=True) + EPS)
        ckh = ckv * rk
        kvg = kvg_ref[...]
        ckvn = (ckh * kvg).astype(BF16)
        dkva = jnp.concatenate([jnp.transpose(dkn_ref[...]).astype(BF16),
                                jnp.transpose(dvv_ref[...]).astype(BF16)], axis=1)
        dckvn = _dot(dkva, wkvt_ref[...])
        dwkv_ref[...] += _dot_tn(ckvn, dkva)
        dkvg_ref[...] += jnp.sum(dckvn * ckh, axis=0, keepdims=True)
        dh2 = dckvn * kvg
        dckv = rk * (dh2 - ckh * jnp.mean(dh2 * ckh, axis=-1, keepdims=True))
        dp_ref[:, O_CKV:O_MZ] = dckv.astype(BF16)

    return pl.pallas_call(
        body, name="bwdprep", grid=(s_len // tm,),
        out_shape=(_sds((s_len, W_INT), BF16), _sds((Q_RANK, 1024), F32), _sds((KV_RANK, 1024), F32),
                   _sds((1, Q_RANK), F32), _sds((1, KV_RANK), F32)),
        in_specs=[_rows(tm, 512), _cols(512, tm), _cols(512, tm), _rows(tm, 512), _rows(tm, 512), _rows(tm, 256),
                  _cols(512, tm), _cols(512, tm), _cols(128, tm), _rows(tm, 512), _rows(tm, D_MODEL),
                  _rows(tm, D_MODEL), _rows(tm, Q_RANK), _rows(tm, KV_RANK), _rows(tm, 256), _rows(tm, 256),
                  _whole((1, Q_RANK)), _whole((1, KV_RANK)), _whole((1024, Q_RANK)), _whole((1024, KV_RANK))],
        out_specs=(_rows(tm, W_INT), _whole((Q_RANK, 1024)), _whole((KV_RANK, 1024)),
                   _whole((1, Q_RANK)), _whole((1, KV_RANK))),
        compiler_params=_params(("arbitrary",)),
    )(dsq, dsk, dsv, dsz, dqn, dqp, dkn, dvv, dkpt, dmz, dga, dgb, cq, ckv, cos256, sin256, qg, kvg, wqt, wkvt)


def _dh_call(dproj, w_int_t, x, dx2, scale, g1):
    s_len = x.shape[0]
    tm = min(ROW_TILE, s_len)

    def body(dp_ref, wt_ref, x_ref, dx2_ref, sc_ref, g1_ref, gx_ref, dsh_ref, dsc_ref, dg1_ref):
        @pl.when(pl.program_id(0) == 0)
        def _():
            dsh_ref[...] = jnp.zeros_like(dsh_ref)
            dsc_ref[...] = jnp.zeros_like(dsc_ref)
            dg1_ref[...] = jnp.zeros_like(dg1_ref)

        dh = _dot(dp_ref[...], wt_ref[...])
        xt = x_ref[...]
        r = lax.rsqrt(jnp.mean(xt * xt, axis=-1, keepdims=True) + EPS)
        xh = xt * r
        g1 = g1_ref[...]
        xg = xh * g1
        dsh_ref[...] += jnp.sum(dh, axis=0, keepdims=True)
        dsc_ref[...] += jnp.sum(dh * xg, axis=0, keepdims=True)
        dxg = dh * (1.0 + sc_ref[...])
        dg1_ref[...] += jnp.sum(dxg * xh, axis=0, keepdims=True)
        dxh = dxg * g1
        gx_ref[...] = dx2_ref[...] + r * (dxh - xh * jnp.mean(dxh * xh, axis=-1, keepdims=True))

    return pl.pallas_call(
        body, name="dh", grid=(s_len // tm,),
        out_shape=(_sds((s_len, D_MODEL), F32), _sds((1, D_MODEL), F32), _sds((1, D_MODEL), F32),
                   _sds((1, D_MODEL), F32)),
        in_specs=[_rows(tm, W_INT), _whole((W_INT, D_MODEL)), _rows(tm, D_MODEL), _rows(tm, D_MODEL),
                  _whole((1, D_MODEL)), _whole((1, D_MODEL))],
        out_specs=(_rows(tm, D_MODEL), _whole((1, D_MODEL)), _whole((1, D_MODEL)), _whole((1, D_MODEL))),
        compiler_params=_params(("arbitrary",)),
    )(dproj, w_int_t, x, dx2, scale, g1)


def _dwin_call(h, dproj):
    s_len = h.shape[0]
    tm = min(2 * ROW_TILE, s_len)
    nc = 4
    chunk = W_INT // nc

    def body(h_ref, dp_ref, dw_ref):
        @pl.when(pl.program_id(1) == 0)
        def _():
            dw_ref[...] = jnp.zeros_like(dw_ref)

        dw_ref[...] += _dot_tn(h_ref[...], dp_ref[...])

    return pl.pallas_call(
        body, name="dwin", grid=(nc, s_len // tm),
        out_shape=_sds((D_MODEL, nc * chunk), F32),
        in_specs=[pl.BlockSpec((tm, D_MODEL), lambda c, i: (i, 0)),
                  pl.BlockSpec((tm, chunk), lambda c, i: (i, c))],
        out_specs=pl.BlockSpec((D_MODEL, chunk), lambda c, i: (0, c)),
        compiler_params=_params(("parallel", "arbitrary")),
    )(h, dproj)


def _small_call(svg, ct, dmod_sh):
    def body(sv_ref, ct_ref, dm_ref, tot_ref, gwada_ref):
        acc = sv_ref[0:1, :]
        for d in range(1, N_DEV):
            acc = acc + sv_ref[d:d + 1, :]
        tot_ref[...] = acc
        gwada_ref[...] = lax.dot_general(ct_ref[...], dm_ref[...], (((1,), (0,)), ((), ())),
                                         precision=lax.Precision.HIGHEST, preferred_element_type=F32)

    vmem = pl.BlockSpec(memory_space=pltpu.VMEM)
    return pl.pallas_call(
        body, name="small_grads",
        out_shape=(_sds((1, 8 * SV_COLS), F32), _sds((D_MODEL, 768), F32)),
        in_specs=[vmem, vmem, vmem], out_specs=(vmem, vmem),
        compiler_params=_params(),
    )(svg, ct, dmod_sh)


def _adamw_tile_rows(rows, cols):
    budget = 2 << 20
    if rows * cols * 4 <= budget or rows % 8:
        return rows
    best = 8
    for tr in range(8, rows + 1, 8):
        if rows % tr == 0 and tr * cols * 4 <= budget:
            best = tr
    return best


def _adamw_call(name, w, g, m, v):
    rows, cols = w.shape
    tr = _adamw_tile_rows(rows, cols)

    def body(w_ref, g_ref, m_ref, v_ref, d_ref, nm_ref, nv_ref):
        gg = g_ref[...]
        m2 = ADAM_B1 * m_ref[...] + (1.0 - ADAM_B1) * gg
        v2 = ADAM_B2 * v_ref[...] + (1.0 - ADAM_B2) * (gg * gg)
        m_hat = m2 / (1.0 - ADAM_B1 ** ADAM_STEP)
        v_hat = v2 / (1.0 - ADAM_B2 ** ADAM_STEP)
        d_ref[...] = -ADAM_LR * (m_hat / (jnp.sqrt(v_hat) + ADAM_EPS) + ADAM_WD * w_ref[...])
        nm_ref[...] = m2
        nv_ref[...] = v2

    spec = pl.BlockSpec((tr, cols), lambda i: (i, 0))
    return pl.pallas_call(
        body, name="adamw_" + name, grid=(rows // tr,),
        out_shape=(_sds((rows, cols), F32),) * 3,
        in_specs=[spec] * 4, out_specs=(spec,) * 3,
        compiler_params=_params(("parallel",)),
    )(w, g, m, v)


IN_SHARD = IN_WIDTH // N_CHIPS
HALF_D = D_MODEL // 2
SMALL_ROWS = (576, 512, 1024, 1024, 2048)
SMALL_TOTAL = sum(SMALL_ROWS)
SMALL_HALF = SMALL_TOTAL // 2
SMALL_SUM_ROWS = 432


def _gather2_call(c_row, w_ada_sh, pack_in, pack_small):
    def body(c_ref, wada_ref, pki_ref, pks_ref, mg_ref, cg_ref, gwi_ref, gws_ref,
             cv, ssem_c, rsem_c, ssem_m, rsem_m, ssem_w, rsem_w, ssem_f, rsem_f, lsem):
        x, y, c = lax.axis_index("x"), lax.axis_index("y"), lax.axis_index("c")
        me = 4 * x + 2 * y + c
        chip = 2 * x + y
        rel3 = [(1, 0), (0, 1), (1, 1)]
        packs = [(pki_ref, gwi_ref), (pks_ref, gws_ref)]

        sends = []
        for j, (dx, dy) in enumerate(rel3):
            for a, (pk, gw) in enumerate(packs):
                cp = pltpu.make_async_remote_copy(
                    src_ref=pk.at[c], dst_ref=gw.at[chip, c], send_sem=ssem_w.at[j, a], recv_sem=rsem_w.at[j, a],
                    device_id=(_flip(x, dx), _flip(y, dy), c), device_id_type=MESH)
                cp.start()
                sends.append(cp)
        owns = []
        for a, (pk, gw) in enumerate(packs):
            own = pltpu.make_async_copy(pk, gw.at[chip], lsem.at[a])
            own.start()
            owns.append(own)

        cv[me] = c_ref[...]
        for r in range(1, N_DEV):
            dx, dy, dc = (r >> 2) & 1, (r >> 1) & 1, r & 1
            cp = pltpu.make_async_remote_copy(
                src_ref=c_ref, dst_ref=cv.at[me], send_sem=ssem_c.at[r - 1], recv_sem=rsem_c.at[r - 1],
                device_id=(_flip(x, dx), _flip(y, dy), _flip(c, dc)), device_id_type=MESH)
            cp.start()
            sends.append(cp)
        for r in range(1, N_DEV):
            dx, dy, dc = (r >> 2) & 1, (r >> 1) & 1, r & 1
            src = 4 * _flip(x, dx) + 2 * _flip(y, dy) + _flip(c, dc)
            pltpu.make_async_remote_copy(
                src_ref=c_ref, dst_ref=cv.at[src], send_sem=ssem_c.at[r - 1], recv_sem=rsem_c.at[r - 1],
                device_id=(x, y, c), device_id_type=MESH).wait_recv()
        rows = lax.broadcasted_iota(jnp.int32, (N_DEV, D_MODEL), 0)
        call = jnp.zeros((N_DEV, D_MODEL), F32)
        for b in range(N_DEV):
            call = jnp.where(rows == b, jnp.broadcast_to(cv[b], (N_DEV, D_MODEL)), call)
        cg_ref[...] = call

        mg_ref[chip] = lax.dot_general(call, wada_ref[...], (((1,), (0,)), ((), ())),
                                       precision=lax.Precision.HIGHEST, preferred_element_type=F32)
        for j, (dx, dy) in enumerate(rel3):
            cp = pltpu.make_async_remote_copy(
                src_ref=mg_ref.at[chip], dst_ref=mg_ref.at[chip], send_sem=ssem_m.at[j], recv_sem=rsem_m.at[j],
                device_id=(_flip(x, dx), _flip(y, dy), c), device_id_type=MESH)
            cp.start()
            sends.append(cp)
        for j, (dx, dy) in enumerate(rel3):
            src_chip = 2 * _flip(x, dx) + _flip(y, dy)
            pltpu.make_async_remote_copy(
                src_ref=mg_ref.at[src_chip], dst_ref=mg_ref.at[src_chip], send_sem=ssem_m.at[j],
                recv_sem=rsem_m.at[j], device_id=(x, y, c), device_id_type=MESH).wait_recv()
        for j, (dx, dy) in enumerate(rel3):
            src_chip = 2 * _flip(x, dx) + _flip(y, dy)
            for a, (pk, gw) in enumerate(packs):
                pltpu.make_async_remote_copy(
                    src_ref=pk.at[c], dst_ref=gw.at[src_chip, c], send_sem=ssem_w.at[j, a],
                    recv_sem=rsem_w.at[j, a], device_id=(x, y, c), device_id_type=MESH).wait_recv()
                cp = pltpu.make_async_remote_copy(
                    src_ref=gw.at[src_chip, c], dst_ref=gw.at[src_chip, c], send_sem=ssem_f.at[j, a],
                    recv_sem=rsem_f.at[j, a], device_id=(x, y, 1 - c), device_id_type=MESH)
                cp.start()
                sends.append(cp)
        for j, (dx, dy) in enumerate(rel3):
            src_chip = 2 * _flip(x, dx) + _flip(y, dy)
            for a, (pk, gw) in enumerate(packs):
                pltpu.make_async_remote_copy(
                    src_ref=pk.at[c], dst_ref=gw.at[src_chip, 1 - c], send_sem=ssem_f.at[j, a],
                    recv_sem=rsem_f.at[j, a], device_id=(x, y, c), device_id_type=MESH).wait_recv()
        for cp in sends:
            cp.wait_send()
        for own in owns:
            own.wait()

    vmem = pl.BlockSpec(memory_space=pltpu.VMEM)
    return pl.pallas_call(
        body, name="gather_fwd",
        out_shape=(_sds((N_CHIPS, N_DEV, 768), F32), _sds((N_DEV, D_MODEL), F32),
                   _sds((N_CHIPS, 2, IN_SHARD, HALF_D), BF16), _sds((N_CHIPS, 2, SMALL_HALF, LANES), BF16)),
        in_specs=[vmem, vmem, vmem, vmem], out_specs=(vmem, vmem, vmem, vmem),
        scratch_shapes=[
            pltpu.VMEM((N_DEV, 1, D_MODEL), F32),
            pltpu.SemaphoreType.DMA((N_DEV - 1,)), pltpu.SemaphoreType.DMA((N_DEV - 1,)),
            pltpu.SemaphoreType.DMA((3,)), pltpu.SemaphoreType.DMA((3,)),
            pltpu.SemaphoreType.DMA((3, 2)), pltpu.SemaphoreType.DMA((3, 2)),
            pltpu.SemaphoreType.DMA((3, 2)), pltpu.SemaphoreType.DMA((3, 2)),
            pltpu.SemaphoreType.DMA((2,)),
        ],
        compiler_params=_params(),
    )(c_row, w_ada_sh, pack_in, pack_small)


def _reduce2_call(g_in, g_small, sv):
    def body(gi_ref, gs_ref, sv_ref, fi_ref, fs_ref, svg_ref, land_i, land_s,
             ssem_g, rsem_g, ssem_s, rsem_s, ssem_x, rsem_x, lsem):
        x, y, c = lax.axis_index("x"), lax.axis_index("y"), lax.axis_index("c")
        me = 4 * x + 2 * y + c
        pieces = [(gi_ref, land_i), (gs_ref, land_s)]
        copies = []
        for r in range(1, N_DEV):
            dx, dy, dc = (r >> 2) & 1, (r >> 1) & 1, r & 1
            tx, ty, tc = _flip(x, dx), _flip(y, dy), _flip(c, dc)
            tgt = 4 * tx + 2 * ty + tc
            for a, (g, land) in enumerate(pieces):
                cp = pltpu.make_async_remote_copy(
                    src_ref=g.at[tgt], dst_ref=land.at[me], send_sem=ssem_g.at[r - 1, a],
                    recv_sem=rsem_g.at[r - 1, a], device_id=(tx, ty, tc), device_id_type=MESH)
                cp.start()
                copies.append(cp)
            cp = pltpu.make_async_remote_copy(
                src_ref=sv_ref, dst_ref=svg_ref.at[me], send_sem=ssem_s.at[r - 1],
                recv_sem=rsem_s.at[r - 1], device_id=(tx, ty, tc), device_id_type=MESH)
            cp.start()
            copies.append(cp)
        owns = []
        for a, (g, land) in enumerate(pieces):
            own = pltpu.make_async_copy(g.at[me], land.at[me], lsem.at[a])
            own.start()
            owns.append(own)
        svg_ref[me] = sv_ref[...]
        for r in range(1, N_DEV):
            dx, dy, dc = (r >> 2) & 1, (r >> 1) & 1, r & 1
            src = 4 * _flip(x, dx) + 2 * _flip(y, dy) + _flip(c, dc)
            for a, (g, land) in enumerate(pieces):
                pltpu.make_async_remote_copy(
                    src_ref=g.at[src], dst_ref=land.at[src], send_sem=ssem_g.at[r - 1, a],
                    recv_sem=rsem_g.at[r - 1, a], device_id=(x, y, c), device_id_type=MESH).wait_recv()
            pltpu.make_async_remote_copy(
                src_ref=sv_ref, dst_ref=svg_ref.at[src], send_sem=ssem_s.at[r - 1],
                recv_sem=rsem_s.at[r - 1], device_id=(x, y, c), device_id_type=MESH).wait_recv()
        for own in owns:
            own.wait()

        for qd in range(HALF_D // LANES):
            sl = slice(LANES * qd, LANES * qd + LANES)
            acc = land_i[0, :, sl].astype(F32)
            for d in range(1, N_DEV):
                acc = acc + land_i[d, :, sl].astype(F32)
            fi_ref[c, :, sl] = acc

        def sum_rows(i, carry):
            sl = pl.ds(pl.multiple_of(i * SMALL_SUM_ROWS, 16), SMALL_SUM_ROWS)
            acc = land_s[0, sl, :].astype(F32)
            for d in range(1, N_DEV):
                acc = acc + land_s[d, sl, :].astype(F32)
            fs_ref[c, sl, :] = acc
            return carry

        lax.fori_loop(0, SMALL_HALF // SMALL_SUM_ROWS, sum_rows, 0)
        swaps = []
        for a, f in enumerate((fi_ref, fs_ref)):
            cp = pltpu.make_async_remote_copy(
                src_ref=f.at[c], dst_ref=f.at[c], send_sem=ssem_x.at[a], recv_sem=rsem_x.at[a],
                device_id=(x, y, 1 - c), device_id_type=MESH)
            cp.start()
            swaps.append(cp)
        for a, f in enumerate((fi_ref, fs_ref)):
            pltpu.make_async_remote_copy(
                src_ref=f.at[c], dst_ref=f.at[1 - c], send_sem=ssem_x.at[a], recv_sem=rsem_x.at[a],
                device_id=(x, y, c), device_id_type=MESH).wait_recv()
        for cp in swaps + copies:
            cp.wait_send()

    vmem = pl.BlockSpec(memory_space=pltpu.VMEM)
    return pl.pallas_call(
        body, name="grad_reduce",
        out_shape=(_sds((2, IN_SHARD, HALF_D), F32), _sds((2, SMALL_HALF, LANES), F32),
                   _sds((N_DEV, 8, SV_COLS), F32)),
        in_specs=[vmem, vmem, vmem], out_specs=(vmem, vmem, vmem),
        scratch_shapes=[
            pltpu.VMEM((N_DEV, IN_SHARD, HALF_D), BF16), pltpu.VMEM((N_DEV, SMALL_HALF, LANES), BF16),
            pltpu.SemaphoreType.DMA((N_DEV - 1, 2)), pltpu.SemaphoreType.DMA((N_DEV - 1, 2)),
            pltpu.SemaphoreType.DMA((N_DEV - 1,)), pltpu.SemaphoreType.DMA((N_DEV - 1,)),
            pltpu.SemaphoreType.DMA((2,)), pltpu.SemaphoreType.DMA((2,)), pltpu.SemaphoreType.DMA((2,)),
        ],
        compiler_params=_params(),
    )(g_in, g_small, sv)


def _dwin_t_call(h, dproj):
    s_len = h.shape[0]
    tm = min(2 * ROW_TILE, s_len)
    nrow = s_len // tm
    nc = 4
    chunk = W_INT // nc

    def body(h_ref, dp_ref, dw_ref, acc):
        i = pl.program_id(1)

        @pl.when(i == 0)
        def _():
            acc[...] = jnp.zeros_like(acc)

        acc[...] += _dot_tn(dp_ref[...], h_ref[...])

        @pl.when(i == nrow - 1)
        def _():
            dw_ref[...] = acc[...].astype(BF16)

    return pl.pallas_call(
        body, name="dwin", grid=(nc, nrow),
        out_shape=_sds((W_INT, D_MODEL), BF16),
        in_specs=[pl.BlockSpec((tm, D_MODEL), lambda c, i: (i, 0)),
                  pl.BlockSpec((tm, chunk), lambda c, i: (i, c))],
        out_specs=pl.BlockSpec((chunk, D_MODEL), lambda c, i: (c, 0)),
        scratch_shapes=[pltpu.VMEM((chunk, D_MODEL), F32)],
        compiler_params=_params(("parallel", "arbitrary")),
    )(h, dproj)


def _swap_rows(w, group):
    r, n = w.shape
    return w.reshape(r // group, 2, group // 2, n)[:, ::-1].reshape(r, n)


def _internal_weights_t(w_in_t, w_uq, w_ukv):
    krot_t = w_in_t[2688:2720]
    w_int_t = jnp.concatenate([
        w_in_t[0:512] * jnp.asarray(0.125, w_in_t.dtype), w_in_t[512:2048],
        w_in_t[2048:2432], w_in_t[2432:2688], w_in_t[2720:3232], w_in_t[3232:4256], w_in_t[4256:5280],
        jnp.tile(krot_t, (4, 1)), jnp.tile(_swap_rows(krot_t, 32), (4, 1)),
        jnp.zeros((W_INT - O_END, D_MODEL), w_in_t.dtype)], axis=0)
    uq = w_uq.reshape(Q_RANK, N_HEADS, 96)
    wp = uq[:, :, 64:].reshape(Q_RANK, 256)
    w_q = jnp.concatenate([uq[:, :, :64].reshape(Q_RANK, 512), wp, _swap_halves(wp, 32)], axis=1)
    ukv = w_ukv.reshape(KV_RANK, N_HEADS, 128)
    w_kv = jnp.concatenate([ukv[:, :, :64].reshape(KV_RANK, 512), ukv[:, :, 64:].reshape(KV_RANK, 512)], axis=1)
    return w_int_t, w_q, w_kv


def _true_weight_grads_t(dwi_t, dwq, dwkv):
    dkr = dwi_t[O_KR:O_KR + 128].astype(F32).reshape(4, 32, D_MODEL).sum(axis=0)
    dkr_sw = dwi_t[O_KR + 128:O_END].astype(F32).reshape(4, 32, D_MODEL).sum(axis=0)
    dkrot_t = (dkr + _swap_rows(dkr_sw, 32)).astype(dwi_t.dtype)
    g_in_t = jnp.concatenate([
        dwi_t[0:512] * jnp.asarray(0.125, dwi_t.dtype), dwi_t[512:2048], dwi_t[O_CQ:O_CKV], dwi_t[O_CKV:O_MZ],
        dkrot_t, dwi_t[O_MZ:O_GA], dwi_t[O_GA:O_GB], dwi_t[O_GB:O_KR]], axis=0)
    dwp = dwq[:, 512:768] + _swap_halves(dwq[:, 768:1024], 32)
    g_uq = jnp.concatenate([dwq[:, :512].reshape(Q_RANK, N_HEADS, 64), dwp.reshape(Q_RANK, N_HEADS, 32)],
                           axis=2).reshape(Q_RANK, 768)
    g_ukv = jnp.concatenate([dwkv[:, :512].reshape(KV_RANK, N_HEADS, 64), dwkv[:, 512:].reshape(KV_RANK, N_HEADS, 64)],
                            axis=2).reshape(KV_RANK, 1024)
    return g_in_t, g_uq, g_ukv


def _swap_halves(w, group):
    r, n = w.shape
    return w.reshape(r, n // group, 2, group // 2)[:, :, ::-1, :].reshape(r, n)


def _pack_shards(parts):
    return jnp.concatenate([p.reshape(-1, LANES) for p in parts], axis=0)


def _unpack_chip_major(gw):
    offs = [0]
    for r in PACK_ROWS:
        offs.append(offs[-1] + r)

    def cols(i, rows, shard_cols):
        blk = gw[:, offs[i]:offs[i + 1]].reshape(N_CHIPS, rows, shard_cols)
        return blk.transpose(1, 0, 2).reshape(rows, N_CHIPS * shard_cols)

    w_in = cols(0, D_MODEL, 1320)
    w_uq = cols(1, Q_RANK, 192)
    w_ukv = cols(2, KV_RANK, 256)
    w_a = cols(3, 512, 256)
    w_b = cols(4, 512, 256)
    w_out = gw[:, offs[5]:offs[6]].reshape(D_MODEL, D_MODEL)
    return w_in, w_uq, w_ukv, w_a, w_b, w_out


def _unpack_small(gw):
    offs = [0]
    for r in SMALL_ROWS:
        offs.append(offs[-1] + r)

    def cols(i, rows, shard_cols):
        blk = gw[:, offs[i]:offs[i + 1]].reshape(N_CHIPS, rows, shard_cols)
        return blk.transpose(1, 0, 2).reshape(rows, N_CHIPS * shard_cols)

    return (cols(0, Q_RANK, 192), cols(1, KV_RANK, 256), cols(2, 512, 256), cols(3, 512, 256),
            gw[:, offs[4]:offs[5]].reshape(D_MODEL, D_MODEL))


def _internal_weights(w_in, w_uq, w_ukv):
    krot = w_in[:, 2688:2720]
    w_int = jnp.concatenate([
        w_in[:, 0:512] * jnp.asarray(0.125, w_in.dtype), w_in[:, 512:2048],
        w_in[:, 2048:2432], w_in[:, 2432:2688], w_in[:, 2720:3232], w_in[:, 3232:4256], w_in[:, 4256:5280],
        jnp.tile(krot, (1, 4)), jnp.tile(_swap_halves(krot, 32), (1, 4)),
        jnp.zeros((D_MODEL, W_INT - O_END), w_in.dtype)], axis=1)
    uq = w_uq.reshape(Q_RANK, N_HEADS, 96)
    wp = uq[:, :, 64:].reshape(Q_RANK, 256)
    w_q = jnp.concatenate([uq[:, :, :64].reshape(Q_RANK, 512), wp, _swap_halves(wp, 32)], axis=1)
    ukv = w_ukv.reshape(KV_RANK, N_HEADS, 128)
    w_kv = jnp.concatenate([ukv[:, :, :64].reshape(KV_RANK, 512), ukv[:, :, 64:].reshape(KV_RANK, 512)], axis=1)
    return w_int, w_q, w_kv


def _true_weight_grads(dwi, dwq, dwkv):
    dkr = dwi[:, O_KR:O_KR + 128].reshape(D_MODEL, 4, 32).sum(axis=1)
    dkr_sw = dwi[:, O_KR + 128:O_END].reshape(D_MODEL, 4, 32).sum(axis=1)
    dkrot = dkr + _swap_halves(dkr_sw, 32)
    g_in = jnp.concatenate([
        dwi[:, 0:512] * 0.125, dwi[:, 512:2048], dwi[:, O_CQ:O_CKV], dwi[:, O_CKV:O_MZ], dkrot,
        dwi[:, O_MZ:O_GA], dwi[:, O_GA:O_GB], dwi[:, O_GB:O_KR]], axis=1)
    dwp = dwq[:, 512:768] + _swap_halves(dwq[:, 768:1024], 32)
    g_uq = jnp.concatenate([dwq[:, :512].reshape(Q_RANK, N_HEADS, 64), dwp.reshape(Q_RANK, N_HEADS, 32)],
                           axis=2).reshape(Q_RANK, 768)
    g_ukv = jnp.concatenate([dwkv[:, :512].reshape(KV_RANK, N_HEADS, 64), dwkv[:, 512:].reshape(KV_RANK, N_HEADS, 64)],
                            axis=2).reshape(KV_RANK, 1024)
    return g_in, g_uq, g_ukv


def _chip_major(g, shard_cols):
    r = g.shape[0]
    return g.reshape(r, N_CHIPS, shard_cols).transpose(1, 0, 2).reshape(N_CHIPS, -1, LANES)


def kernel(x, c, positions, w_ada, b_ada, norm_gain, w_in, q_norm_gain, w_uq, kv_norm_gain, w_ukv, w_branch_a, w_branch_b, w_out, final_norm_gain, loss_target, m_w_ada, m_b_ada, m_norm_gain, m_w_in, m_q_norm_gain, m_w_uq, m_kv_norm_gain, m_w_ukv, m_w_branch_a, m_w_branch_b, m_w_out, m_final_norm_gain, v_w_ada, v_b_ada, v_norm_gain, v_w_in, v_q_norm_gain, v_w_uq, v_kv_norm_gain, v_w_ukv, v_w_branch_a, v_w_branch_b, v_w_out, v_final_norm_gain):
    ix, iy, ic = lax.axis_index("x"), lax.axis_index("y"), lax.axis_index("c")
    me = 4 * ix + 2 * iy + ic
    chip = 2 * ix + iy
    xs = x[0]
    tgt = loss_target[0]
    s_len = xs.shape[0]

    w_in_t = jnp.swapaxes(w_in[0], 0, 1)
    w_in_tb = w_in_t.astype(BF16)
    pack_in = jnp.stack([w_in_tb[:, :HALF_D], w_in_tb[:, HALF_D:]], axis=0)
    small_shards = (w_uq[0], w_ukv[0], w_branch_a[0], w_branch_b[0], w_out[0])
    pack_small = _pack_shards([s.astype(BF16) for s in small_shards]).reshape(2, SMALL_HALF, LANES)
    mg, call, gw_in, gw_small = _gather2_call(c, w_ada[0], pack_in, pack_small)
    mod = mg.transpose(1, 0, 2).reshape(N_DEV, 3 * D_MODEL) + b_ada
    mod_me = lax.dynamic_slice_in_dim(mod, me, 1, axis=0)
    shift, scale, gate = mod_me[:, :D_MODEL], mod_me[:, D_MODEL:2 * D_MODEL], mod_me[:, 2 * D_MODEL:]

    f_in_t = jnp.concatenate([gw_in[:, 0], gw_in[:, 1]], axis=2).reshape(IN_WIDTH, D_MODEL)
    f_uq, f_ukv, f_a, f_b, f_out = _unpack_small(gw_small.reshape(N_CHIPS, SMALL_TOTAL, LANES))
    w_int_t, w_q, w_kv = _internal_weights_t(f_in_t, f_uq, f_ukv)

    inv_freq = ROPE_BASE ** (-jnp.arange(0, ROPE_DIM, 2, dtype=F32) / ROPE_DIM)
    ang = positions[0].astype(F32)[:, None] * inv_freq
    cs, sn = jnp.cos(ang), jnp.sin(ang)
    cos256 = jnp.tile(jnp.concatenate([cs, cs], axis=1), (1, 8))
    sin256 = jnp.tile(jnp.concatenate([-sn, sn], axis=1), (1, 8))

    (h, sq, sk, sv, sz, cq, ckv, mz, ga, gb, kpt, qn, qp, kn, vv) = _inproj_call(
        xs, shift, scale, norm_gain, w_int_t, w_q, w_kv, q_norm_gain, kv_norm_gain, cos256, sin256)
    oa, lt = _sb4_fwd_call(sq, sk, sv)
    ob, lse = _mla4_fwd_call(qn, qp, kn, kpt, vv)

    gf = final_norm_gain.reshape(1, D_MODEL)
    (dx2, doa, dob, dsz, dmz, dga, dgb, dwo, dwa, dwb, dgf, dgate, loss_p) = _post_call(
        xs, tgt, oa, ob, sz, mz, ga, gb, gate, gf, f_a, f_b, f_out, f_a.T, f_b.T, f_out.T)

    dsq, dsk_t, dsv_t = _sb4_bwd_call(sq, sk, sv, doa, lt)
    dqn, dqp, dkn_t, dkpt_t, dvv_t = _mla4_bwd_call(qn, qp, kn, kpt, vv, ob, dob, lse)

    dproj, dwq, dwkv, dqg, dkvg = _bwdprep_call(
        dsq, dsk_t, dsv_t, dsz, dqn, dqp, dkn_t, dvv_t, dkpt_t, dmz, dga, dgb, cq, ckv, cos256, sin256,
        q_norm_gain, kv_norm_gain, w_q.T, w_kv.T)
    grad_x, dshift, dscale, dg1 = _dh_call(dproj, w_int_t, xs, dx2, scale, norm_gain)
    dwi_t = _dwin_t_call(h, dproj)
    g_in_t, g_uq, g_ukv = _true_weight_grads_t(dwi_t, dwq, dwkv)

    g_in_c = g_in_t.reshape(N_CHIPS, IN_SHARD, D_MODEL)
    g_in_pieces = jnp.stack([g_in_c[:, :, :HALF_D], g_in_c[:, :, HALF_D:]], axis=1).reshape(N_DEV, IN_SHARD, HALF_D)
    g_small = jnp.concatenate([
        _chip_major(g_uq, 192), _chip_major(g_ukv, 256), _chip_major(dwa, 256), _chip_major(dwb, 256),
        dwo.reshape(N_CHIPS, -1, LANES)], axis=1).astype(BF16).reshape(N_DEV, SMALL_HALF, LANES)
    small = jnp.concatenate([
        dshift, dscale, dgate, dg1, dqg, dkvg, dgf, loss_p,
        jnp.zeros((1, 8 * SV_COLS - 5888), F32)], axis=1).reshape(8, SV_COLS)
    full_in, full_small, svg = _reduce2_call(g_in_pieces, g_small, small)
    gs_in_t = jnp.concatenate([full_in[0], full_in[1]], axis=1)
    full = full_small.reshape(SMALL_TOTAL, LANES)
    offs = [0]
    for r in SMALL_ROWS:
        offs.append(offs[-1] + r)
    gs_uq = full[offs[0]:offs[1]].reshape(Q_RANK, 192)
    gs_ukv = full[offs[1]:offs[2]].reshape(KV_RANK, 256)
    gs_a = full[offs[2]:offs[3]].reshape(512, 256)
    gs_b = full[offs[3]:offs[4]].reshape(512, 256)
    gs_out = full[offs[4]:offs[5]].reshape(256, D_MODEL)

    svm = svg.reshape(N_DEV, 8 * SV_COLS)
    dmod_sh = lax.dynamic_slice_in_dim(svm[:, :3 * D_MODEL], chip * 768, 768, axis=1)
    tot, gs_ada = _small_call(svm, call.T, dmod_sh)
    g_bada = tot[:, 0:3072]
    g_g1 = tot[:, 3072:4096]
    g_qg = tot[:, 4096:4480]
    g_kvg = tot[:, 4480:4736]
    g_gf = tot[:, 4736:5760]
    loss = tot[0, 5760]

    names = ["w_ada", "b_ada", "norm_gain", "w_in", "q_norm_gain", "w_uq", "kv_norm_gain", "w_ukv",
             "w_branch_a", "w_branch_b", "w_out", "final_norm_gain"]
    ws = [w_ada[0], b_ada, norm_gain, w_in_t, q_norm_gain, w_uq[0], kv_norm_gain, w_ukv[0],
          w_branch_a[0], w_branch_b[0], w_out[0], final_norm_gain.reshape(1, D_MODEL)]
    gs = [gs_ada, g_bada, g_g1, gs_in_t, g_qg, gs_uq, g_kvg, gs_ukv, gs_a, gs_b, gs_out, g_gf]
    ms = [m_w_ada[0], m_b_ada, m_norm_gain, jnp.swapaxes(m_w_in[0], 0, 1), m_q_norm_gain, m_w_uq[0],
          m_kv_norm_gain, m_w_ukv[0], m_w_branch_a[0], m_w_branch_b[0], m_w_out[0],
          m_final_norm_gain.reshape(1, D_MODEL)]
    vs = [v_w_ada[0], v_b_ada, v_norm_gain, jnp.swapaxes(v_w_in[0], 0, 1), v_q_norm_gain, v_w_uq[0],
          v_kv_norm_gain, v_w_ukv[0], v_w_branch_a[0], v_w_branch_b[0], v_w_out[0],
          v_final_norm_gain.reshape(1, D_MODEL)]
    refs = [w_ada, b_ada, norm_gain, w_in, q_norm_gain, w_uq, kv_norm_gain, w_ukv,
            w_branch_a, w_branch_b, w_out, final_norm_gain]
    grads, deltas, new_ms, new_vs = [], [], [], []
    for n, w_, g_, m_, v_, ref in zip(names, ws, gs, ms, vs, refs):
        outs = (g_,) + _adamw_call(n, w_, g_, m_, v_)
        if n == "w_in":
            outs = tuple(jnp.swapaxes(o_, 0, 1) for o_ in outs)
        for lst, o_ in zip((grads, deltas, new_ms, new_vs), outs):
            lst.append(o_.reshape(ref.shape))

    return (loss, grad_x.reshape(x.shape), *grads, *deltas, *new_ms, *new_vs)
```

```python
import functools
import math

import jax
import jax.numpy as jnp
from jax import lax
from jax.experimental import pallas as pl
from jax.experimental.pallas import tpu as pltpu

F32 = jnp.float32
BF16 = jnp.bfloat16

D_MODEL = 1024
SB_WIDTH = 512
MLA_WIDTH = 512
Q_RANK = 384
KV_RANK = 256
ROPE_DIM = 32
N_HEADS = 8
IN_WIDTH = 5280
EPS = 1e-6
ROPE_BASE = 10000.0
MLA_SCALE = 1.0 / math.sqrt(96.0)

ADAM_LR = 0.001
ADAM_B1 = 0.9
ADAM_B2 = 0.999
ADAM_EPS = 1e-08
ADAM_WD = 0.01
ADAM_STEP = 10

O_SQ, O_SK, O_SV, O_SZ, O_CQ, O_CKV, O_MZ, O_GA, O_GB, O_KR, O_END = (
    0, 512, 1024, 1536, 2048, 2432, 2688, 3200, 4224, 5248, 5504)
W_INT = 5632

N_CHIPS = 4
N_DEV = 8
LANES = 128
PACK_ROWS = (10560, 576, 512, 1024, 1024, 2048)
PACK_TOTAL = sum(PACK_ROWS)
HALF_ROWS = PACK_TOTAL // 2
SV_COLS = 768

ROW_TILE = 256
ATT_TILE = 256
ATT_Q_TILES = 2
VMEM_LIMIT = 56 * 1024 * 1024

MESH = pl.DeviceIdType.MESH


def _dot(a, b):
    return lax.dot_general(a, b, (((1,), (0,)), ((), ())), preferred_element_type=F32)


def _dot_nt(a, b):
    return lax.dot_general(a, b, (((1,), (1,)), ((), ())), preferred_element_type=F32)


def _dot_tn(a, b):
    return lax.dot_general(a, b, (((0,), (0,)), ((), ())), preferred_element_type=F32)


def _sigmoid(z):
    return 1.0 / (1.0 + jnp.exp(-z))


def _params(sem=None):
    if sem is None:
        return pltpu.CompilerParams(vmem_limit_bytes=VMEM_LIMIT)
    return pltpu.CompilerParams(dimension_semantics=sem, vmem_limit_bytes=VMEM_LIMIT)


def _rows(tm, n):
    return pl.BlockSpec((tm, n), lambda i: (i, 0))


def _cols(n, tm):
    return pl.BlockSpec((n, tm), lambda i: (0, i))


def _whole(shape):
    nd = len(shape)
    return pl.BlockSpec(shape, lambda i: (0,) * nd)


def _sds(shape, dtype):
    return jax.ShapeDtypeStruct(shape, dtype)


def _flip(v, d):
    return 1 - v if d else v


def _gather_call(c_row, w_ada_sh, pack):
    def body(c_ref, wada_ref, pk_ref, mg_ref, cg_ref, gw_ref,
             cv, ssem_c, rsem_c, ssem_m, rsem_m, ssem_w, rsem_w, ssem_f, rsem_f, lsem):
        x, y, c = lax.axis_index("x"), lax.axis_index("y"), lax.axis_index("c")
        me = 4 * x + 2 * y + c
        chip = 2 * x + y
        rel3 = [(1, 0), (0, 1), (1, 1)]

        wcopies = []
        for j, (dx, dy) in enumerate(rel3):
            cp = pltpu.make_async_remote_copy(
                src_ref=pk_ref.at[c], dst_ref=gw_ref.at[chip, c], send_sem=ssem_w.at[j], recv_sem=rsem_w.at[j],
                device_id=(_flip(x, dx), _flip(y, dy), c), device_id_type=MESH)
            cp.start()
            wcopies.append(cp)
        own = pltpu.make_async_copy(pk_ref, gw_ref.at[chip], lsem)
        own.start()

        cv[me] = c_ref[...]
        ccopies = []
        for r in range(1, N_DEV):
            dx, dy, dc = (r >> 2) & 1, (r >> 1) & 1, r & 1
            cp = pltpu.make_async_remote_copy(
                src_ref=c_ref, dst_ref=cv.at[me], send_sem=ssem_c.at[r - 1], recv_sem=rsem_c.at[r - 1],
                device_id=(_flip(x, dx), _flip(y, dy), _flip(c, dc)), device_id_type=MESH)
            cp.start()
            ccopies.append(cp)
        for r in range(1, N_DEV):
            dx, dy, dc = (r >> 2) & 1, (r >> 1) & 1, r & 1
            src = 4 * _flip(x, dx) + 2 * _flip(y, dy) + _flip(c, dc)
            pltpu.make_async_remote_copy(
                src_ref=c_ref, dst_ref=cv.at[src], send_sem=ssem_c.at[r - 1], recv_sem=rsem_c.at[r - 1],
                device_id=(x, y, c), device_id_type=MESH).wait_recv()
        rows = lax.broadcasted_iota(jnp.int32, (N_DEV, D_MODEL), 0)
        call = jnp.zeros((N_DEV, D_MODEL), F32)
        for b in range(N_DEV):
            call = jnp.where(rows == b, jnp.broadcast_to(cv[b], (N_DEV, D_MODEL)), call)
        cg_ref[...] = call

        mg_ref[chip] = lax.dot_general(call, wada_ref[...], (((1,), (0,)), ((), ())),
                                       precision=lax.Precision.HIGHEST, preferred_element_type=F32)
        mcopies = []
        for j, (dx, dy) in enumerate(rel3):
            cp = pltpu.make_async_remote_copy(
                src_ref=mg_ref.at[chip], dst_ref=mg_ref.at[chip], send_sem=ssem_m.at[j], recv_sem=rsem_m.at[j],
                device_id=(_flip(x, dx), _flip(y, dy), c), device_id_type=MESH)
            cp.start()
            mcopies.append(cp)
        for j, (dx, dy) in enumerate(rel3):
            src_chip = 2 * _flip(x, dx) + _flip(y, dy)
            pltpu.make_async_remote_copy(
                src_ref=mg_ref.at[src_chip], dst_ref=mg_ref.at[src_chip], send_sem=ssem_m.at[j],
                recv_sem=rsem_m.at[j], device_id=(x, y, c), device_id_type=MESH).wait_recv()
        fcopies = []
        for j, (dx, dy) in enumerate(rel3):
            src_chip = 2 * _flip(x, dx) + _flip(y, dy)
            pltpu.make_async_remote_copy(
                src_ref=pk_ref.at[c], dst_ref=gw_ref.at[src_chip, c], send_sem=ssem_w.at[j], recv_sem=rsem_w.at[j],
                device_id=(x, y, c), device_id_type=MESH).wait_recv()
            cp = pltpu.make_async_remote_copy(
                src_ref=gw_ref.at[src_chip, c], dst_ref=gw_ref.at[src_chip, c], send_sem=ssem_f.at[j],
                recv_sem=rsem_f.at[j], device_id=(x, y, 1 - c), device_id_type=MESH)
            cp.start()
            fcopies.append(cp)
        for j, (dx, dy) in enumerate(rel3):
            src_chip = 2 * _flip(x, dx) + _flip(y, dy)
            pltpu.make_async_remote_copy(
                src_ref=pk_ref.at[c], dst_ref=gw_ref.at[src_chip, 1 - c], send_sem=ssem_f.at[j],
                recv_sem=rsem_f.at[j], device_id=(x, y, c), device_id_type=MESH).wait_recv()
        for cp in ccopies + mcopies + wcopies + fcopies:
            cp.wait_send()
        own.wait()

    vmem = pl.BlockSpec(memory_space=pltpu.VMEM)
    return pl.pallas_call(
        body, name="gather_fwd",
        out_shape=(_sds((N_CHIPS, N_DEV, 768), F32), _sds((N_DEV, D_MODEL), F32),
                   _sds((N_CHIPS, 2, HALF_ROWS, LANES), BF16)),
        in_specs=[vmem, vmem, vmem], out_specs=(vmem, vmem, vmem),
        scratch_shapes=[
            pltpu.VMEM((N_DEV, 1, D_MODEL), F32),
            pltpu.SemaphoreType.DMA((N_DEV - 1,)), pltpu.SemaphoreType.DMA((N_DEV - 1,)),
            pltpu.SemaphoreType.DMA((3,)), pltpu.SemaphoreType.DMA((3,)),
            pltpu.SemaphoreType.DMA((3,)), pltpu.SemaphoreType.DMA((3,)),
            pltpu.SemaphoreType.DMA((3,)), pltpu.SemaphoreType.DMA((3,)),
            pltpu.SemaphoreType.DMA,
        ],
        compiler_params=_params(),
    )(c_row, w_ada_sh, pack)


SUM_ROWS = 656


def _reduce_call(gpack, sv):
    def body(g_ref, sv_ref, full_ref, svg_ref, land, ssem_g, rsem_g, ssem_s, rsem_s, ssem_x, rsem_x, lsem):
        x, y, c = lax.axis_index("x"), lax.axis_index("y"), lax.axis_index("c")
        me = 4 * x + 2 * y + c
        copies = []
        for r in range(1, N_DEV):
            dx, dy, dc = (r >> 2) & 1, (r >> 1) & 1, r & 1
            tx, ty, tc = _flip(x, dx), _flip(y, dy), _flip(c, dc)
            tgt = 4 * tx + 2 * ty + tc
            cp = pltpu.make_async_remote_copy(
                src_ref=g_ref.at[tgt], dst_ref=land.at[me], send_sem=ssem_g.at[r - 1],
                recv_sem=rsem_g.at[r - 1], device_id=(tx, ty, tc), device_id_type=MESH)
            cp.start()
            copies.append(cp)
            cp = pltpu.make_async_remote_copy(
                src_ref=sv_ref, dst_ref=svg_ref.at[me], send_sem=ssem_s.at[r - 1],
                recv_sem=rsem_s.at[r - 1], device_id=(tx, ty, tc), device_id_type=MESH)
            cp.start()
            copies.append(cp)
        own = pltpu.make_async_copy(g_ref.at[me], land.at[me], lsem)
        own.start()
        svg_ref[me] = sv_ref[...]
        for r in range(1, N_DEV):
            dx, dy, dc = (r >> 2) & 1, (r >> 1) & 1, r & 1
            src = 4 * _flip(x, dx) + 2 * _flip(y, dy) + _flip(c, dc)
            pltpu.make_async_remote_copy(
                src_ref=g_ref.at[src], dst_ref=land.at[src], send_sem=ssem_g.at[r - 1],
                recv_sem=rsem_g.at[r - 1], device_id=(x, y, c), device_id_type=MESH).wait_recv()
            pltpu.make_async_remote_copy(
                src_ref=sv_ref, dst_ref=svg_ref.at[src], send_sem=ssem_s.at[r - 1],
                recv_sem=rsem_s.at[r - 1], device_id=(x, y, c), device_id_type=MESH).wait_recv()
        own.wait()

        def sum_rows(i, carry):
            sl = pl.ds(pl.multiple_of(i * SUM_ROWS, 16), SUM_ROWS)
            acc = land[0, sl, :].astype(F32)
            for d in range(1, N_DEV):
                acc = acc + land[d, sl, :].astype(F32)
            full_ref[c, sl, :] = acc
            return carry

        lax.fori_loop(0, HALF_ROWS // SUM_ROWS, sum_rows, 0)
        swap = pltpu.make_async_remote_copy(
            src_ref=full_ref.at[c], dst_ref=full_ref.at[c], send_sem=ssem_x, recv_sem=rsem_x,
            device_id=(x, y, 1 - c), device_id_type=MESH)
        swap.start()
        pltpu.make_async_remote_copy(
            src_ref=full_ref.at[c], dst_ref=full_ref.at[1 - c], send_sem=ssem_x, recv_sem=rsem_x,
            device_id=(x, y, c), device_id_type=MESH).wait_recv()
        swap.wait_send()
        for cp in copies:
            cp.wait_send()

    vmem = pl.BlockSpec(memory_space=pltpu.VMEM)
    return pl.pallas_call(
        body, name="grad_reduce",
        out_shape=(_sds((2, HALF_ROWS, LANES), F32), _sds((N_DEV, 8, SV_COLS), F32)),
        in_specs=[vmem, vmem], out_specs=(vmem, vmem),
        scratch_shapes=[
            pltpu.VMEM((N_DEV, HALF_ROWS, LANES), BF16),
            pltpu.SemaphoreType.DMA((N_DEV - 1,)), pltpu.SemaphoreType.DMA((N_DEV - 1,)),
            pltpu.SemaphoreType.DMA((N_DEV - 1,)), pltpu.SemaphoreType.DMA((N_DEV - 1,)),
            pltpu.SemaphoreType.DMA, pltpu.SemaphoreType.DMA, pltpu.SemaphoreType.DMA,
        ],
        compiler_params=_params(),
    )(gpack, sv)


def _inproj_call(x, shift, scale, g1, w_int, w_q, w_kv, qg, kvg, cos256, sin256):
    s_len = x.shape[0]
    tm = min(ROW_TILE, s_len)

    def body(x_ref, sh_ref, sc_ref, g1_ref, w_ref, wq_ref, wkv_ref, qg_ref, kvg_ref, cos_ref, sin_ref,
             h_ref, sq_ref, sk_ref, sv_ref, sz_ref, cq_ref, ckv_ref, mz_ref, ga_ref, gb_ref, kpt_ref,
             qn_ref, qp_ref, kn_ref, vv_ref):
        xt = x_ref[...]
        r = lax.rsqrt(jnp.mean(xt * xt, axis=-1, keepdims=True) + EPS)
        h = (xt * r * g1_ref[...]) * (1.0 + sc_ref[...]) + sh_ref[...]
        hb = h.astype(BF16)
        h_ref[...] = hb

        def seg(a, b):
            return _dot_nt(hb, w_ref[a:b, :])

        sq_ref[...] = seg(O_SQ, O_SK).astype(BF16)
        sk_ref[...] = seg(O_SK, O_SV).astype(BF16)
        sv_ref[...] = seg(O_SV, O_SZ).astype(BF16)
        sz_ref[...] = seg(O_SZ, O_CQ)
        mz_ref[...] = seg(O_MZ, O_GA)
        ga_ref[...] = seg(O_GA, O_GB)
        gb_ref[...] = seg(O_GB, O_KR)
        cos = cos_ref[...]
        sin = sin_ref[...]
        kr = seg(O_KR, O_END)
        kpt_ref[...] = (kr[:, :128] * cos[:, :128] + kr[:, 128:] * sin[:, :128]).astype(BF16)

        cq = seg(O_CQ, O_CKV)
        cq_ref[...] = cq
        rq = lax.rsqrt(jnp.mean(cq * cq, axis=-1, keepdims=True) + EPS)
        cqn = (cq * rq * qg_ref[...]).astype(BF16)
        qa = _dot(cqn, wq_ref[...])
        qn_ref[...] = qa[:, :512].astype(BF16)
        qp_ref[...] = (qa[:, 512:768] * cos + qa[:, 768:] * sin).astype(BF16)

        ckv = seg(O_CKV, O_MZ)
        ckv_ref[...] = ckv
        rk = lax.rsqrt(jnp.mean(ckv * ckv, axis=-1, keepdims=True) + EPS)
        ckvn = (ckv * rk * kvg_ref[...]).astype(BF16)
        kva = _dot(ckvn, wkv_ref[...])
        kn_ref[...] = kva[:, :512].astype(BF16)
        vv_ref[...] = kva[:, 512:].astype(BF16)

    outs = [
        (D_MODEL, BF16), (512, BF16), (512, BF16), (512, BF16), (512, F32), (Q_RANK, F32), (KV_RANK, F32),
        (512, F32), (D_MODEL, F32), (D_MODEL, F32), (128, BF16), (512, BF16), (256, BF16), (512, BF16), (512, BF16),
    ]
    return pl.pallas_call(
        body, name="inproj", grid=(s_len // tm,),
        out_shape=tuple(_sds((s_len, n), dt) for n, dt in outs),
        in_specs=[_rows(tm, D_MODEL), _whole((1, D_MODEL)), _whole((1, D_MODEL)), _whole((1, D_MODEL)),
                  _whole((W_INT, D_MODEL)), _whole((Q_RANK, 1024)), _whole((KV_RANK, 1024)),
                  _whole((1, Q_RANK)), _whole((1, KV_RANK)), _rows(tm, 256), _rows(tm, 256)],
        out_specs=tuple(_rows(tm, n) for n, _ in outs),
        compiler_params=_params(("parallel",)),
    )(x, shift, scale, g1, w_int, w_q, w_kv, qg, kvg, cos256, sin256)


def _softplus(z):
    return jnp.maximum(z, 0.0) + jnp.log(1.0 + jnp.exp(-jnp.abs(z)))


def _split_bf16(a):
    hi = a.astype(BF16)
    lo = (a - hi.astype(F32)).astype(BF16)
    return hi, lo


def _sb_fwd_call(q, k, v):
    s_len = q.shape[0]
    tk = min(ATT_TILE, s_len)
    tq = min(ATT_Q_TILES * tk, s_len)
    r = tq // tk
    nq = s_len // tq

    def body(q_ref, k_ref, v_ref, o_ref, lt_ref):
        i = pl.program_id(1)
        q2 = q_ref[...]
        lane = lax.broadcasted_iota(jnp.int32, (1, 128), 1)
        krow = lax.broadcasted_iota(jnp.int32, (tk, tk), 0)
        kcol = lax.broadcasted_iota(jnp.int32, (tk, tk), 1)
        row = lax.broadcasted_iota(jnp.int32, (tq, tk), 0)
        col = lax.broadcasted_iota(jnp.int32, (tq, tk), 1)
        later = (krow > kcol).astype(BF16)
        later2 = jnp.concatenate([later, later], axis=0)
        valids = [col + u * tk < row for u in range(r)]
        hms = [(lane // 64) == hh for hh in range(2)]
        qms = [jnp.where(hm, q2, jnp.zeros_like(q2)) for hm in hms]

        def block(j, carry, valid):
            runs, acc = list(carry[:2]), carry[2]
            off = pl.multiple_of(j * tk, tk)
            kb = k_ref[pl.ds(off, tk), :]
            vb = v_ref[pl.ds(off, tk), :]
            ws = []
            for hh in range(2):
                z = _dot_nt(qms[hh], kb)
                lg = -_softplus(z)
                lm = jnp.where(valid, lg, 0.0) if valid is not None else lg
                hi, lo = _split_bf16(lm)
                suf = _dot(jnp.concatenate([hi, lo], axis=1), later2)
                w = jnp.exp(z + lg + suf + runs[hh])
                if valid is not None:
                    w = jnp.where(valid, w, 0.0)
                ws.append(w.astype(BF16))
                runs[hh] = runs[hh] + jnp.sum(lm, axis=1, keepdims=True)
            vstack = jnp.concatenate([jnp.where(hm, vb, jnp.zeros_like(vb)) for hm in hms], axis=0)
            acc = acc + _dot(jnp.concatenate(ws, axis=1), vstack)
            return runs[0], runs[1], acc

        zero = jnp.zeros((tq,1), F32)
        carry = block(i, (zero, zero, jnp.zeros((tq,128), F32)), True)
        carry = lax.fori_loop(1, i + 1, lambda jj, cr: block(i - jj, cr, False), carry)
        lt_ref[0, :, 0:1] = carry[0]
        lt_ref[0, :, 1:2] = carry[1]
        o_ref[...] = carry[2]

    return pl.pallas_call(
        body, name="sb_fwd", grid=(4, nq),
        out_shape=(_sds((s_len, SB_WIDTH), F32), _sds((4, s_len, 2), F32)),
        in_specs=[pl.BlockSpec((tq,128), lambda p, i: (i, p)),
                  pl.BlockSpec((s_len, 128), lambda p, i: (0, p)),
                  pl.BlockSpec((s_len, 128), lambda p, i: (0, p))],
        out_specs=(pl.BlockSpec((tq,128), lambda p, i: (i, p)),
                   pl.BlockSpec((1, t, 2), lambda p, i: (p, i, 0))),
        compiler_params=_params(("parallel", "parallel")),
    )(q, k, v)


def _sb_bwd_call(q, k, v, do, lt):
    s_len = q.shape[0]
    tk = min(ATT_TILE, s_len)
    tq = min(ATT_Q_TILES * tk, s_len)
    r = tq // tk
    nq = s_len // tq

    def body(q_ref, k_ref, v_ref, do_ref, lt_ref, dq_ref, dk_ref, dv_ref):
        i = pl.program_id(1)

        @pl.when(i == 0)
        def _():
            dk_ref[...] = jnp.zeros_like(dk_ref)
            dv_ref[...] = jnp.zeros_like(dv_ref)

        q2 = q_ref[...]
        do2 = do_ref[...].astype(BF16)
        lane = lax.broadcasted_iota(jnp.int32, (1, 128), 1)
        krow = lax.broadcasted_iota(jnp.int32, (tk, tk), 0)
        kcol = lax.broadcasted_iota(jnp.int32, (tk, tk), 1)
        row = lax.broadcasted_iota(jnp.int32, (tq, tk), 0)
        col = lax.broadcasted_iota(jnp.int32, (tq, tk), 1)
        earlier = (krow < kcol).astype(BF16)
        earlier2 = jnp.concatenate([earlier, earlier], axis=0)
        valids = [col + u * tk < row for u in range(r)]
        hms = [(lane // 64) == hh for hh in range(2)]
        qms = [jnp.where(hm, q2, jnp.zeros_like(q2)) for hm in hms]
        doms = [jnp.where(hm, do2, jnp.zeros_like(do2)) for hm in hms]
        ltots = [lt_ref[0, :, hh:hh + 1] for hh in range(2)]
        qstack = jnp.concatenate(qms, axis=0)
        dostack = jnp.concatenate(doms, axis=0)

        def block(j, carry, valid):
            lpre, ppre, dq = list(carry[0:2]), list(carry[2:4]), carry[4]
            off = pl.multiple_of(j * tk, tk)
            kb = k_ref[pl.ds(off, tk), :]
            vb = v_ref[pl.ds(off, tk), :]
            dzs, avs = [], []
            for hh in range(2):
                z = _dot_nt(qms[hh], kb)
                sp = _softplus(z)
                lg = -sp
                lm = jnp.where(valid, lg, 0.0) if valid is not None else lg
                hi, lo = _split_bf16(lm)
                before = _dot(jnp.concatenate([hi, lo], axis=1), earlier2)
                between = ltots[hh] - (lpre[hh] + before + lm)
                a = jnp.exp(z + lg + between)
                if valid is not None:
                    a = jnp.where(valid, a, 0.0)
                p = a * _dot_nt(doms[hh], vb)
                phi, plo = _split_bf16(p)
                pbefore = ppre[hh] + _dot(jnp.concatenate([phi, plo], axis=1), earlier2)
                sig = jnp.exp(z - sp)
                dz = p - sig * (p + pbefore)
                if valid is not None:
                    dz = jnp.where(valid, dz, 0.0)
                dzs.append(dz.astype(BF16))
                avs.append(a.astype(BF16))
                lpre[hh] = lpre[hh] + jnp.sum(lm, axis=1, keepdims=True)
                ppre[hh] = ppre[hh] + jnp.sum(p, axis=1, keepdims=True)
            kstack = jnp.concatenate([jnp.where(hm, kb, jnp.zeros_like(kb)) for hm in hms], axis=0)
            dq = dq + _dot(jnp.concatenate(dzs, axis=1), kstack)
            dk_ref[pl.ds(off, tk), :] += _dot_tn(jnp.concatenate(dzs, axis=0), qstack)
            dv_ref[pl.ds(off, tk), :] += _dot_tn(jnp.concatenate(avs, axis=0), dostack)
            return lpre[0], lpre[1], ppre[0], ppre[1], dq

        zero = jnp.zeros((tq,1), F32)
        carry = lax.fori_loop(0, i, lambda j, cr: block(j, cr, False),
                              (zero, zero, zero, zero, jnp.zeros((tq,128), F32)))
        carry = block(i, carry, True)
        dq_ref[...] = carry[4].astype(BF16)

    return pl.pallas_call(
        body, name="sb_bwd", grid=(4, nq),
        out_shape=(_sds((s_len, SB_WIDTH), BF16), _sds((s_len, SB_WIDTH), F32), _sds((s_len, SB_WIDTH), F32)),
        in_specs=[pl.BlockSpec((tq,128), lambda p, i: (i, p)),
                  pl.BlockSpec((s_len, 128), lambda p, i: (0, p)),
                  pl.BlockSpec((s_len, 128), lambda p, i: (0, p)),
                  pl.BlockSpec((tq,128), lambda p, i: (i, p)),
                  pl.BlockSpec((1, t, 2), lambda p, i: (p, i, 0))],
        out_specs=(pl.BlockSpec((tq,128), lambda p, i: (i, p)),
                   pl.BlockSpec((s_len, 128), lambda p, i: (0, p)),
                   pl.BlockSpec((s_len, 128), lambda p, i: (0, p))),
        compiler_params=_params(("parallel", "arbitrary")),
    )(q, k, v, do, lt)


def _mla_fwd_call(qn, qp, kn, kpt, v):
    s_len = qn.shape[0]
    tk = min(ATT_TILE, s_len)
    tq = min(ATT_Q_TILES * tk, s_len)
    r = tq // tk
    nq = s_len // tq

    def body(qn_ref, qp_ref, kn_ref, kpt_ref, v_ref, o_ref, lse_ref):
        i = pl.program_id(1)
        qn2 = qn_ref[...]
        qp2 = qp_ref[...]
        lane256 = lax.broadcasted_iota(jnp.int32, (1, 256), 1)
        lane128 = lax.broadcasted_iota(jnp.int32, (1, 128), 1)
        krow = lax.broadcasted_iota(jnp.int32, (tk, tk), 0)
        kcol = lax.broadcasted_iota(jnp.int32, (tk, tk), 1)
        row = lax.broadcasted_iota(jnp.int32, (tq, tk), 0)
        col = lax.broadcasted_iota(jnp.int32, (tq, tk), 1)
        valids = [col + u * tk <= row for u in range(r)]
        m64s = [(lane256 // 64) == hh for hh in range(4)]
        m32s = [(lane128 // 32) == hh for hh in range(4)]
        qcs = [jnp.concatenate([jnp.where(m64s[hh], qn2, jnp.zeros_like(qn2)),
                                jnp.where(m32s[hh], qp2, jnp.zeros_like(qp2))], axis=1) for hh in range(4)]

        def by_head(vals):
            return jnp.where(m64s[0], vals[0], jnp.where(m64s[1], vals[1], jnp.where(m64s[2], vals[2], vals[3])))

        def block(j, carry, valid):
            ms, ls, acc = list(carry[0:4]), list(carry[4:8]), carry[8]
            off = pl.multiple_of(j * tk, tk)
            kc = jnp.concatenate([kn_ref[pl.ds(off, tk), :], kpt_ref[pl.ds(off, tk), :]], axis=1)
            vb = v_ref[pl.ds(off, tk), :]
            ps, alphas = [], []
            for hh in range(4):
                s = _dot_nt(qcs[hh], kc) * MLA_SCALE
                if valid is not None:
                    s = jnp.where(valid, s, -1e30)
                mn = jnp.maximum(ms[hh], jnp.max(s, axis=1, keepdims=True))
                p = jnp.exp(s - mn)
                alpha = jnp.exp(ms[hh] - mn)
                ls[hh] = alpha * ls[hh] + jnp.sum(p, axis=1, keepdims=True)
                ms[hh] = mn
                ps.append(p.astype(BF16))
                alphas.append(alpha)
            vstack = jnp.concatenate([jnp.where(m64, vb, jnp.zeros_like(vb)) for m64 in m64s], axis=0)
            acc = by_head(alphas) * acc + _dot(jnp.concatenate(ps, axis=1), vstack)
            return (*ms, *ls, acc)

        neg = jnp.full((tq, 1), -1e30, F32)
        zero = jnp.zeros((tq,1), F32)
        carry = lax.fori_loop(0, i, lambda j, cr: block(j, cr, False),
                              (neg, neg, neg, neg, zero, zero, zero, zero, jnp.zeros((tq,256), F32)))
        carry = block(i, carry, True)
        o_ref[...] = carry[8] / by_head(list(carry[4:8]))
        for hh in range(4):
            lse_ref[0, :, hh:hh + 1] = carry[hh] + jnp.log(carry[4 + hh])

    return pl.pallas_call(
        body, name="mla_fwd", grid=(2, nq),
        out_shape=(_sds((s_len, MLA_WIDTH), F32), _sds((2, s_len, 4), F32)),
        in_specs=[pl.BlockSpec((tq,256), lambda g, i: (i, g)),
                  pl.BlockSpec((tq,128), lambda g, i: (i, g)),
                  pl.BlockSpec((s_len, 256), lambda g, i: (0, g)),
                  pl.BlockSpec((s_len, 128), lambda g, i: (0, 0)),
                  pl.BlockSpec((s_len, 256), lambda g, i: (0, g))],
        out_specs=(pl.BlockSpec((tq,256), lambda g, i: (i, g)),
                   pl.BlockSpec((1, tq, 4), lambda g, i: (g, i, 0))),
        compiler_params=_params(("parallel", "parallel")),
    )(qn, qp, kn, kpt, v)


def _mla_bwd_call(qn, qp, kn, kpt, v, o, do, lse):
    s_len = qn.shape[0]
    tk = min(ATT_TILE, s_len)
    tq = min(ATT_Q_TILES * tk, s_len)
    r = tq // tk
    nq = s_len // tq

    def body(qn_ref, qp_ref, kn_ref, kpt_ref, v_ref, o_ref, do_ref, lse_ref,
             dqn_ref, dqp_ref, dkn_ref, dkpt_ref, dv_ref):
        g = pl.program_id(0)
        i = pl.program_id(1)

        @pl.when(i == 0)
        def _():
            dkn_ref[...] = jnp.zeros_like(dkn_ref)
            dv_ref[...] = jnp.zeros_like(dv_ref)

        @pl.when((i == 0) & (g == 0))
        def _():
            dkpt_ref[...] = jnp.zeros_like(dkpt_ref)

        qn2 = qn_ref[...]
        qp2 = qp_ref[...]
        of = o_ref[...]
        dof = do_ref[...]
        dob = dof.astype(BF16)
        prod = dof * of
        lane256 = lax.broadcasted_iota(jnp.int32, (1, 256), 1)
        lane128 = lax.broadcasted_iota(jnp.int32, (1, 128), 1)
        krow = lax.broadcasted_iota(jnp.int32, (tk, tk), 0)
        kcol = lax.broadcasted_iota(jnp.int32, (tk, tk), 1)
        row = lax.broadcasted_iota(jnp.int32, (tq, tk), 0)
        col = lax.broadcasted_iota(jnp.int32, (tq, tk), 1)
        valids = [col + u * tk <= row for u in range(r)]
        m64s = [(lane256 // 64) == hh for hh in range(4)]
        m32s = [(lane128 // 32) == hh for hh in range(4)]
        qcs = [jnp.concatenate([jnp.where(m64s[hh], qn2, jnp.zeros_like(qn2)),
                                jnp.where(m32s[hh], qp2, jnp.zeros_like(qp2))], axis=1) for hh in range(4)]
        doms = [jnp.where(m64, dob, jnp.zeros_like(dob)) for m64 in m64s]
        dsums = [jnp.sum(jnp.where(m64, prod, 0.0), axis=1, keepdims=True) * MLA_SCALE for m64 in m64s]
        lses = [lse_ref[0, :, hh:hh + 1] for hh in range(4)]
        qstack = jnp.concatenate(qcs, axis=0)
        dostack = jnp.concatenate(doms, axis=0)

        def block(j, dqc, diag):
            off = pl.multiple_of(j * tk, tk)
            knb = kn_ref[pl.ds(off, tk), :]
            kpb = kpt_ref[pl.ds(off, tk), :]
            vb = v_ref[pl.ds(off, tk), :]
            kc = jnp.concatenate([knb, kpb], axis=1)
            dss, pbs = [], []
            for hh in range(4):
                s = _dot_nt(qcs[hh], kc) * MLA_SCALE
                if valid is not None:
                    s = jnp.where(valid, s, -1e30)
                p = jnp.exp(s - lses[hh])
                ds = p * (_dot_nt(doms[hh], vb) * MLA_SCALE - dsums[hh])
                dss.append(ds.astype(BF16))
                pbs.append(p.astype(BF16))
            kstack = jnp.concatenate(
                [jnp.concatenate([jnp.where(m64s[hh], knb, jnp.zeros_like(knb)),
                                  jnp.where(m32s[hh], kpb, jnp.zeros_like(kpb))], axis=1) for hh in range(4)], axis=0)
            dqc = dqc + _dot(jnp.concatenate(dss, axis=1), kstack)
            dkc = _dot_tn(jnp.concatenate(dss, axis=0), qstack)
            dkn_ref[pl.ds(off, tk), :] += dkc[:, :256]
            dkpt_ref[pl.ds(off, tk), :] += dkc[:, 256:]
            dv_ref[pl.ds(off, tk), :] += _dot_tn(jnp.concatenate(pbs, axis=0), dostack)
            return dqc

        dqc = lax.fori_loop(0, i, lambda j, cr: block(j, cr, False), jnp.zeros((tq,384), F32))
        dqc = block(i, dqc, True)
        dqn_ref[...] = dqc[:, :256].astype(BF16)
        dqp_ref[...] = dqc[:, 256:].astype(BF16)

    return pl.pallas_call(
        body, name="mla_bwd", grid=(2, nq),
        out_shape=(_sds((s_len, 512), BF16), _sds((s_len, 256), BF16), _sds((s_len, 512), F32),
                   _sds((s_len, 128), F32), _sds((s_len, 512), F32)),
        in_specs=[pl.BlockSpec((tq,256), lambda g, i: (i, g)),
                  pl.BlockSpec((tq,128), lambda g, i: (i, g)),
                  pl.BlockSpec((s_len, 256), lambda g, i: (0, g)),
                  pl.BlockSpec((s_len, 128), lambda g, i: (0, 0)),
                  pl.BlockSpec((s_len, 256), lambda g, i: (0, g)),
                  pl.BlockSpec((tq,256), lambda g, i: (i, g)),
                  pl.BlockSpec((tq,256), lambda g, i: (i, g)),
                  pl.BlockSpec((1, tq, 4), lambda g, i: (g, i, 0))],
        out_specs=(pl.BlockSpec((tq,256), lambda g, i: (i, g)),
                   pl.BlockSpec((tq,128), lambda g, i: (i, g)),
                   pl.BlockSpec((s_len, 256), lambda g, i: (0, g)),
                   pl.BlockSpec((s_len, 128), lambda g, i: (0, 0)),
                   pl.BlockSpec((s_len, 256), lambda g, i: (0, g))),
        compiler_params=_params(("arbitrary", "arbitrary")),
    )(qn, qp, kn, kpt, v, o, do, lse)


Z_CLAMP = 80.0


def _softplus_clamped(z):
    zc = jnp.minimum(z, Z_CLAMP)
    return zc, jnp.log(1.0 + jnp.exp(zc))


def _tri_sum(a, tri, tri2, passes):
    if passes == 1:
        return _dot(a.astype(BF16), tri)
    hi, lo = _split_bf16(a)
    return _dot(jnp.concatenate([hi, lo], axis=1), tri2)


def _sb4_fwd_call(q, k, v):
    s_len = q.shape[0]
    tk = min(ATT_TILE, s_len)
    tq = min(ATT_Q_TILES * tk, s_len)
    r = tq // tk
    nq = s_len // tq

    def body(q_ref, k_ref, v_ref, o_ref, lt_ref):
        i = pl.program_id(1)
        q2 = q_ref[...]
        lane = lax.broadcasted_iota(jnp.int32, (1, 256), 1)
        krow = lax.broadcasted_iota(jnp.int32, (tk, tk), 0)
        kcol = lax.broadcasted_iota(jnp.int32, (tk, tk), 1)
        row = lax.broadcasted_iota(jnp.int32, (tq, tk), 0)
        col = lax.broadcasted_iota(jnp.int32, (tq, tk), 1)
        later = (krow > kcol).astype(BF16)
        later2 = jnp.concatenate([later, later], axis=0)
        valids = [col + u * tk < row for u in range(r)]
        hms = [(lane // 64) == hh for hh in range(4)]
        qms = [jnp.where(hm, q2, jnp.zeros_like(q2)) for hm in hms]

        def block(j, carry, valid):
            runs, acc = list(carry[:4]), carry[4]
            off = pl.multiple_of(j * tk, tk)
            kb = k_ref[pl.ds(off, tk), :]
            vb = v_ref[pl.ds(off, tk), :]
            ws = []
            for hh in range(4):
                zc, sp = _softplus_clamped(_dot_nt(qms[hh], kb))
                lm = jnp.where(valid, sp, 0.0) if valid is not None else sp
                suf = _tri_sum(lm, later, later2, 1)
                w = jnp.exp(zc - sp - suf - runs[hh])
                if valid is not None:
                    w = jnp.where(valid, w, 0.0)
                ws.append(w.astype(BF16))
                runs[hh] = runs[hh] + jnp.sum(lm, axis=1, keepdims=True)
            vstack = jnp.concatenate([jnp.where(hm, vb, jnp.zeros_like(vb)) for hm in hms], axis=0)
            acc = acc + _dot(jnp.concatenate(ws, axis=1), vstack)
            return (*runs, acc)

        zero = jnp.zeros((tq,1), F32)
        carry = (zero, zero, zero, zero, jnp.zeros((tq, 256), F32))
        for u in reversed(range(r)):
            carry = block(i * r + u, carry, valids[u])
        carry = lax.fori_loop(0, i * r, lambda jj, cr: block(i * r - 1 - jj, cr, None), carry)
        for hh in range(4):
            lt_ref[0, :, hh:hh + 1] = carry[hh]
        o_ref[...] = carry[4]

    return pl.pallas_call(
        body, name="sb_fwd", grid=(2, nq),
        out_shape=(_sds((s_len, SB_WIDTH), F32), _sds((2, s_len, 4), F32)),
        in_specs=[pl.BlockSpec((tq,256), lambda g, i: (i, g)),
                  pl.BlockSpec((s_len, 256), lambda g, i: (0, g)),
                  pl.BlockSpec((s_len, 256), lambda g, i: (0, g))],
        out_specs=(pl.BlockSpec((tq,256), lambda g, i: (i, g)),
                   pl.BlockSpec((1, tq, 4), lambda g, i: (g, i, 0))),
        compiler_params=_params(("parallel", "parallel")),
    )(q, k, v)


def _sb4_bwd_call(q, k, v, do, lt):
    s_len = q.shape[0]
    tk = min(ATT_TILE, s_len)
    tq = min(ATT_Q_TILES * tk, s_len)
    r = tq // tk
    nq = s_len // tq

    def body(q_ref, k_ref, v_ref, do_ref, lt_ref, dq_ref, dk_ref, dv_ref):
        i = pl.program_id(1)

        @pl.when(i == 0)
        def _():
            dk_ref[...] = jnp.zeros_like(dk_ref)
            dv_ref[...] = jnp.zeros_like(dv_ref)

        q2 = q_ref[...]
        do2 = do_ref[...].astype(BF16)
        lane = lax.broadcasted_iota(jnp.int32, (1, 256), 1)
        krow = lax.broadcasted_iota(jnp.int32, (tk, tk), 0)
        kcol = lax.broadcasted_iota(jnp.int32, (tk, tk), 1)
        row = lax.broadcasted_iota(jnp.int32, (tq, tk), 0)
        col = lax.broadcasted_iota(jnp.int32, (tq, tk), 1)
        earlier = (krow < kcol).astype(BF16)
        earlier2 = jnp.concatenate([earlier, earlier], axis=0)
        later = (krow > kcol).astype(BF16)
        later2 = jnp.concatenate([later, later], axis=0)
        valids = [col + u * tk < row for u in range(r)]
        hms = [(lane // 64) == hh for hh in range(4)]
        qms = [jnp.where(hm, q2, jnp.zeros_like(q2)) for hm in hms]
        doms = [jnp.where(hm, do2, jnp.zeros_like(do2)) for hm in hms]
        ltots = [lt_ref[0, :, hh:hh + 1] for hh in range(4)]
        q2t = jnp.transpose(q2.astype(F32))
        do2t = jnp.transpose(do_ref[...])
        subl = lax.broadcasted_iota(jnp.int32, (256, 1), 0)
        qtstack = jnp.concatenate(
            [jnp.where((subl // 64) == hh, q2t, 0.0).astype(BF16) for hh in range(4)], axis=1)
        dotstack = jnp.concatenate(
            [jnp.where((subl // 64) == hh, do2t, 0.0).astype(BF16) for hh in range(4)], axis=1)

        def block(j, carry, valid):
            lpre, ppre, dq = list(carry[0:4]), list(carry[4:8]), carry[8]
            off = pl.multiple_of(j * tk, tk)
            kb = k_ref[pl.ds(off, tk), :]
            vb = v_ref[pl.ds(off, tk), :]
            dzs, avs = [], []
            for hh in range(4):
                zc, sp = _softplus_clamped(_dot_nt(qms[hh], kb))
                lsig = zc - sp
                lm = jnp.where(valid, sp, 0.0) if valid is not None else sp
                rowsum = jnp.sum(lm, axis=1, keepdims=True)
                between = _tri_sum(lm, later, later2, 1) + ((ltots[hh] - lpre[hh]) - rowsum)
                a = jnp.exp(lsig - between)
                if valid is not None:
                    a = jnp.where(valid, a, 0.0)
                p = a * _dot_nt(doms[hh], vb)
                pbefore = ppre[hh] + _tri_sum(p, earlier, earlier2, 1)
                dz = p - jnp.exp(lsig) * (p + pbefore)
                if valid is not None:
                    dz = jnp.where(valid, dz, 0.0)
                dzs.append(dz.astype(BF16))
                avs.append(a.astype(BF16))
                lpre[hh] = lpre[hh] + rowsum
                ppre[hh] = ppre[hh] + jnp.sum(p, axis=1, keepdims=True)
            kstack = jnp.concatenate([jnp.where(hm, kb, jnp.zeros_like(kb)) for hm in hms], axis=0)
            dq = dq + _dot(jnp.concatenate(dzs, axis=1), kstack)
            dk_ref[:, pl.ds(off, tk)] += _dot(qtstack, jnp.concatenate(dzs, axis=0))
            dv_ref[:, pl.ds(off, tk)] += _dot(dotstack, jnp.concatenate(avs, axis=0))
            return (*lpre, *ppre, dq)

        zero = jnp.zeros((tq,1), F32)
        carry = lax.fori_loop(0, i * r, lambda j, cr: block(j, cr, None),
                              (zero,) * 8 + (jnp.zeros((tq, 256), F32),))
        for u in range(r):
            carry = block(i * r + u, carry, valids[u])
        dq_ref[...] = carry[8].astype(BF16)

    return pl.pallas_call(
        body, name="sb_bwd", grid=(2, nq),
        out_shape=(_sds((s_len, SB_WIDTH), BF16), _sds((SB_WIDTH, s_len), F32), _sds((SB_WIDTH, s_len), F32)),
        in_specs=[pl.BlockSpec((tq,256), lambda g, i: (i, g)),
                  pl.BlockSpec((s_len, 256), lambda g, i: (0, g)),
                  pl.BlockSpec((s_len, 256), lambda g, i: (0, g)),
                  pl.BlockSpec((tq,256), lambda g, i: (i, g)),
                  pl.BlockSpec((1, tq, 4), lambda g, i: (g, i, 0))],
        out_specs=(pl.BlockSpec((tq,256), lambda g, i: (i, g)),
                   pl.BlockSpec((256, s_len), lambda g, i: (g, 0)),
                   pl.BlockSpec((256, s_len), lambda g, i: (g, 0))),
        compiler_params=_params(("parallel", "arbitrary")),
    )(q, k, v, do, lt)


def _mla4_fwd_call(qn, qp, kn, kpt, v):
    s_len = qn.shape[0]
    tk = min(ATT_TILE, s_len)
    tq = min(ATT_Q_TILES * tk, s_len)
    r = tq // tk
    nq = s_len // tq

    def body(qn_ref, qp_ref, kn_ref, kpt_ref, v_ref, o_ref, lse_ref):
        i = pl.program_id(1)
        qn2 = qn_ref[...]
        qp2 = qp_ref[...]
        lane256 = lax.broadcasted_iota(jnp.int32, (1, 256), 1)
        lane128 = lax.broadcasted_iota(jnp.int32, (1, 128), 1)
        krow = lax.broadcasted_iota(jnp.int32, (tk, tk), 0)
        kcol = lax.broadcasted_iota(jnp.int32, (tk, tk), 1)
        row = lax.broadcasted_iota(jnp.int32, (tq, tk), 0)
        col = lax.broadcasted_iota(jnp.int32, (tq, tk), 1)
        valids = [col + u * tk <= row for u in range(r)]
        m64s = [(lane256 // 64) == hh for hh in range(4)]
        half = [(lane128 // 64) == u for u in range(2)]
        m32s = [(lane128 // 32) == hh for hh in range(4)]
        qcs = []
        for hh in range(4):
            qpair = qn2[:, 128 * (hh // 2):128 * (hh // 2) + 128]
            qcs.append(jnp.concatenate([jnp.where(half[hh % 2], qpair, jnp.zeros_like(qpair)),
                                        jnp.where(m32s[hh], qp2, jnp.zeros_like(qp2))], axis=1))

        def by_head(vals):
            return jnp.where(m64s[0], vals[0], jnp.where(m64s[1], vals[1], jnp.where(m64s[2], vals[2], vals[3])))

        def block(j, carry, valid):
            ms, ls, acc = list(carry[0:4]), list(carry[4:8]), carry[8]
            off = pl.multiple_of(j * tk, tk)
            knb = kn_ref[pl.ds(off, tk), :]
            kpb = kpt_ref[pl.ds(off, tk), :]
            vb = v_ref[pl.ds(off, tk), :]
            kcs = [jnp.concatenate([knb[:, 128 * pp:128 * pp + 128], kpb], axis=1) for pp in range(2)]
            ps, alphas = [], []
            for hh in range(4):
                s = _dot_nt(qcs[hh], kcs[hh // 2]) * MLA_SCALE
                if valid is not None:
                    s = jnp.where(valid, s, -1e30)
                mn = jnp.maximum(ms[hh], jnp.max(s, axis=1, keepdims=True))
                p = jnp.exp(s - mn)
                alpha = jnp.exp(ms[hh] - mn)
                ls[hh] = alpha * ls[hh] + jnp.sum(p, axis=1, keepdims=True)
                ms[hh] = mn
                ps.append(p.astype(BF16))
                alphas.append(alpha)
            pvs = []
            for pp in range(2):
                vpair = vb[:, 128 * pp:128 * pp + 128]
                vstack = jnp.concatenate([jnp.where(hf, vpair, jnp.zeros_like(vpair)) for hf in half], axis=0)
                pvs.append(_dot(jnp.concatenate(ps[2 * pp:2 * pp + 2], axis=1), vstack))
            acc = by_head(alphas) * acc + jnp.concatenate(pvs, axis=1)
            return (*ms, *ls, acc)

        neg = jnp.full((tq, 1), -1e30, F32)
        zero = jnp.zeros((tq,1), F32)
        carry = lax.fori_loop(0, i * r, lambda j, cr: block(j, cr, None),
                              (neg,) * 4 + (zero,) * 4 + (jnp.zeros((tq, 256), F32),))
        for u in range(r):
            carry = block(i * r + u, carry, valids[u])
        o_ref[...] = carry[8] / by_head(list(carry[4:8]))
        for hh in range(4):
            lse_ref[0, :, hh:hh + 1] = carry[hh] + jnp.log(carry[4 + hh])

    return pl.pallas_call(
        body, name="mla_fwd", grid=(2, nq),
        out_shape=(_sds((s_len, MLA_WIDTH), F32), _sds((2, s_len, 4), F32)),
        in_specs=[pl.BlockSpec((tq,256), lambda g, i: (i, g)),
                  pl.BlockSpec((tq,128), lambda g, i: (i, g)),
                  pl.BlockSpec((s_len, 256), lambda g, i: (0, g)),
                  pl.BlockSpec((s_len, 128), lambda g, i: (0, 0)),
                  pl.BlockSpec((s_len, 256), lambda g, i: (0, g))],
        out_specs=(pl.BlockSpec((tq,256), lambda g, i: (i, g)),
                   pl.BlockSpec((1, tq, 4), lambda g, i: (g, i, 0))),
        compiler_params=_params(("parallel", "parallel")),
    )(qn, qp, kn, kpt, v)


def _mla4_bwd_call(qn, qp, kn, kpt, v, o, do, lse):
    s_len = qn.shape[0]
    tk = min(ATT_TILE, s_len)
    tq = min(ATT_Q_TILES * tk, s_len)
    r = tq // tk
    nq = s_len // tq

    def body(qn_ref, qp_ref, kn_ref, kpt_ref, v_ref, o_ref, do_ref, lse_ref,
             dqn_ref, dqp_ref, dkn_ref, dkpt_ref, dv_ref):
        g = pl.program_id(0)
        i = pl.program_id(1)

        @pl.when(i == 0)
        def _():
            dkn_ref[...] = jnp.zeros_like(dkn_ref)
            dv_ref[...] = jnp.zeros_like(dv_ref)

        @pl.when((i == 0) & (g == 0))
        def _():
            dkpt_ref[...] = jnp.zeros_like(dkpt_ref)

        qn2 = qn_ref[...]
        qp2 = qp_ref[...]
        dof = do_ref[...]
        dob = dof.astype(BF16)
        prod = dof * o_ref[...]
        lane256 = lax.broadcasted_iota(jnp.int32, (1, 256), 1)
        lane128 = lax.broadcasted_iota(jnp.int32, (1, 128), 1)
        krow = lax.broadcasted_iota(jnp.int32, (tk, tk), 0)
        kcol = lax.broadcasted_iota(jnp.int32, (tk, tk), 1)
        row = lax.broadcasted_iota(jnp.int32, (tq, tk), 0)
        col = lax.broadcasted_iota(jnp.int32, (tq, tk), 1)
        valids = [col + u * tk <= row for u in range(r)]
        m64s = [(lane256 // 64) == hh for hh in range(4)]
        half = [(lane128 // 64) == u for u in range(2)]
        m32s = [(lane128 // 32) == hh for hh in range(4)]
        qcs, doms = [], []
        for hh in range(4):
            sl = slice(128 * (hh // 2), 128 * (hh // 2) + 128)
            qpair = qn2[:, sl]
            dpair = dob[:, sl]
            qcs.append(jnp.concatenate([jnp.where(half[hh % 2], qpair, jnp.zeros_like(qpair)),
                                        jnp.where(m32s[hh], qp2, jnp.zeros_like(qp2))], axis=1))
            doms.append(jnp.where(half[hh % 2], dpair, jnp.zeros_like(dpair)))
        dsums = [jnp.sum(jnp.where(m64, prod, 0.0), axis=1, keepdims=True) * MLA_SCALE for m64 in m64s]
        lses = [lse_ref[0, :, hh:hh + 1] for hh in range(4)]
        qn2t = jnp.transpose(qn2.astype(F32))
        qp2t = jnp.transpose(qp2.astype(F32))
        do2t = jnp.transpose(dof)
        sub128 = lax.broadcasted_iota(jnp.int32, (128, 1), 0)
        qtstacks, dotstacks = [], []
        for pp in range(2):
            qts, dts = [], []
            for u in range(2):
                hh = 2 * pp + u
                qts.append(jnp.concatenate(
                    [jnp.where((sub128 // 64) == u, qn2t[128 * pp:128 * pp + 128, :], 0.0),
                     jnp.where((sub128 // 32) == hh, qp2t, 0.0)], axis=0).astype(BF16))
                dts.append(jnp.where((sub128 // 64) == u, do2t[128 * pp:128 * pp + 128, :], 0.0).astype(BF16))
            qtstacks.append(jnp.concatenate(qts, axis=1))
            dotstacks.append(jnp.concatenate(dts, axis=1))

        def block(j, carry, valid):
            dqn, dqp = carry
            off = pl.multiple_of(j * tk, tk)
            knb = kn_ref[pl.ds(off, tk), :]
            kpb = kpt_ref[pl.ds(off, tk), :]
            vb = v_ref[pl.ds(off, tk), :]
            dqn_parts = []
            dkp = None
            for pp in range(2):
                sl = slice(128 * pp, 128 * pp + 128)
                knp = knb[:, sl]
                vpair = vb[:, sl]
                kc = jnp.concatenate([knp, kpb], axis=1)
                dss, pbs, kcms = [], [], []
                for u in range(2):
                    hh = 2 * pp + u
                    s = _dot_nt(qcs[hh], kc) * MLA_SCALE
                    if valid is not None:
                        s = jnp.where(valid, s, -1e30)
                    p = jnp.exp(s - lses[hh])
                    ds = p * (_dot_nt(doms[hh], vpair) * MLA_SCALE - dsums[hh])
                    dss.append(ds.astype(BF16))
                    pbs.append(p.astype(BF16))
                    kcms.append(jnp.concatenate([jnp.where(half[u], knp, jnp.zeros_like(knp)),
                                                 jnp.where(m32s[hh], kpb, jnp.zeros_like(kpb))], axis=1))
                dqc = _dot(jnp.concatenate(dss, axis=1), jnp.concatenate(kcms, axis=0))
                dqn_parts.append(dqc[:, :128])
                dqp = dqp + dqc[:, 128:]
                dkc = _dot(qtstacks[pp], jnp.concatenate(dss, axis=0))
                dkn_ref[128 * pp:128 * pp + 128, pl.ds(off, tk)] += dkc[:128, :]
                dkp = dkc[128:, :] if dkp is None else dkp + dkc[128:, :]
                dv_ref[128 * pp:128 * pp + 128, pl.ds(off, tk)] += _dot(dotstacks[pp], jnp.concatenate(pbs, axis=0))
            dqn = dqn + jnp.concatenate(dqn_parts, axis=1)
            dkpt_ref[:, pl.ds(off, tk)] += dkp
            return dqn, dqp

        carry = lax.fori_loop(0, i * r, lambda j, cr: block(j, cr, None),
                              (jnp.zeros((tq, 256), F32), jnp.zeros((tq, 128), F32)))
        for u in range(r):
            carry = block(i * r + u, carry, valids[u])
        dqn, dqp = carry
        dqn_ref[...] = dqn.astype(BF16)
        dqp_ref[...] = dqp.astype(BF16)

    return pl.pallas_call(
        body, name="mla_bwd", grid=(2, nq),
        out_shape=(_sds((s_len, 512), BF16), _sds((s_len, 256), BF16), _sds((512, s_len), F32),
                   _sds((128, s_len), F32), _sds((512, s_len), F32)),
        in_specs=[pl.BlockSpec((tq,256), lambda g, i: (i, g)),
                  pl.BlockSpec((tq,128), lambda g, i: (i, g)),
                  pl.BlockSpec((s_len, 256), lambda g, i: (0, g)),
                  pl.BlockSpec((s_len, 128), lambda g, i: (0, 0)),
                  pl.BlockSpec((s_len, 256), lambda g, i: (0, g)),
                  pl.BlockSpec((tq,256), lambda g, i: (i, g)),
                  pl.BlockSpec((tq,256), lambda g, i: (i, g)),
                  pl.BlockSpec((1, tq, 4), lambda g, i: (g, i, 0))],
        out_specs=(pl.BlockSpec((tq,256), lambda g, i: (i, g)),
                   pl.BlockSpec((tq,128), lambda g, i: (i, g)),
                   pl.BlockSpec((256, s_len), lambda g, i: (g, 0)),
                   pl.BlockSpec((128, s_len), lambda g, i: (0, 0)),
                   pl.BlockSpec((256, s_len), lambda g, i: (g, 0))),
        compiler_params=_params(("arbitrary", "arbitrary")),
    )(qn, qp, kn, kpt, v, o, do, lse)


def _post_call(x, tgt, oa, ob, sz, mz, ga, gb, gate, gf, wa, wb, wo, wat, wbt, wot):
    s_len = x.shape[0]
    tm = min(ROW_TILE, s_len)

    def body(x_ref, t_ref, oa_ref, ob_ref, sz_ref, mz_ref, ga_ref, gb_ref, gate_ref, gf_ref,
             wa_ref, wb_ref, wo_ref, wat_ref, wbt_ref, wot_ref,
             dx2_ref, doa_ref, dob_ref, dsz_ref, dmz_ref, dga_ref, dgb_ref,
             dwo_ref, dwa_ref, dwb_ref, dgf_ref, dgate_ref, loss_ref):
        @pl.when(pl.program_id(0) == 0)
        def _():
            dwo_ref[...] = jnp.zeros_like(dwo_ref)
            dwa_ref[...] = jnp.zeros_like(dwa_ref)
            dwb_ref[...] = jnp.zeros_like(dwb_ref)
            dgf_ref[...] = jnp.zeros_like(dgf_ref)
            dgate_ref[...] = jnp.zeros_like(dgate_ref)
            loss_ref[...] = jnp.zeros_like(loss_ref)

        gate = gate_ref[...]
        gf = gf_ref[...]
        oa = oa_ref[...]
        ob = ob_ref[...]
        sz = sz_ref[...]
        mz = mz_ref[...]
        sa = _sigmoid(sz)
        sb = _sigmoid(mz)
        silu_a = sz * sa
        silu_b = mz * sb
        ua = (oa * silu_a).astype(BF16)
        ub = (ob * silu_b).astype(BF16)
        ya = _dot(ua, wa_ref[...])
        yb = _dot(ub, wb_ref[...])
        sga = _sigmoid(ga_ref[...])
        sgb = _sigmoid(gb_ref[...])
        merged = (sga * ya + sgb * yb).astype(BF16)
        out = _dot(merged, wo_ref[...])
        x2 = x_ref[...] + gate * out
        r2 = lax.rsqrt(jnp.mean(x2 * x2, axis=-1, keepdims=True) + EPS)
        xhat = x2 * r2
        err = xhat * gf - t_ref[...]
        loss_ref[...] += 0.5 * jnp.sum(jnp.sum(err * err, axis=1, keepdims=True), axis=0, keepdims=True) / D_MODEL
        dy = err * (1.0 / D_MODEL)
        dgf_ref[...] += jnp.sum(dy * xhat, axis=0, keepdims=True)
        dxhat = dy * gf
        dx2 = r2 * (dxhat - xhat * jnp.mean(dxhat * xhat, axis=-1, keepdims=True))
        dx2_ref[...] = dx2
        dgate_ref[...] += jnp.sum(dx2 * out, axis=0, keepdims=True)
        dout = (dx2 * gate).astype(BF16)
        dmerged = _dot(dout, wot_ref[...])
        dwo_ref[...] += _dot_tn(merged, dout)
        dya = dmerged * sga
        dyb = dmerged * sgb
        dga_ref[...] = (dya * ya * (1.0 - sga)).astype(BF16)
        dgb_ref[...] = (dyb * yb * (1.0 - sgb)).astype(BF16)
        dyab = dya.astype(BF16)
        dybb = dyb.astype(BF16)
        dua = _dot(dyab, wat_ref[...])
        dub = _dot(dybb, wbt_ref[...])
        dwa_ref[...] += _dot_tn(ua, dyab)
        dwb_ref[...] += _dot_tn(ub, dybb)
        doa_ref[...] = dua * silu_a
        dob_ref[...] = dub * silu_b
        dsz_ref[...] = (dua * oa * (sa * (1.0 + sz * (1.0 - sa)))).astype(BF16)
        dmz_ref[...] = (dub * ob * (sb * (1.0 + mz * (1.0 - sb)))).astype(BF16)

    return pl.pallas_call(
        body, name="post", grid=(s_len // tm,),
        out_shape=(_sds((s_len, D_MODEL), F32), _sds((s_len, 512), F32), _sds((s_len, 512), F32),
                   _sds((s_len, 512), BF16), _sds((s_len, 512), BF16),
                   _sds((s_len, D_MODEL), BF16), _sds((s_len, D_MODEL), BF16),
                   _sds((D_MODEL, D_MODEL), F32), _sds((512, D_MODEL), F32), _sds((512, D_MODEL), F32),
                   _sds((1, D_MODEL), F32), _sds((1, D_MODEL), F32), _sds((1, 128), F32)),
        in_specs=[_rows(tm, D_MODEL), _rows(tm, D_MODEL), _rows(tm, 512), _rows(tm, 512), _rows(tm, 512),
                  _rows(tm, 512), _rows(tm, D_MODEL), _rows(tm, D_MODEL), _whole((1, D_MODEL)), _whole((1, D_MODEL)),
                  _whole((512, D_MODEL)), _whole((512, D_MODEL)), _whole((D_MODEL, D_MODEL)),
                  _whole((D_MODEL, 512)), _whole((D_MODEL, 512)), _whole((D_MODEL, D_MODEL))],
        out_specs=(_rows(tm, D_MODEL), _rows(tm, 512), _rows(tm, 512), _rows(tm, 512), _rows(tm, 512),
                   _rows(tm, D_MODEL), _rows(tm, D_MODEL),
                   _whole((D_MODEL, D_MODEL)), _whole((512, D_MODEL)), _whole((512, D_MODEL)),
                   _whole((1, D_MODEL)), _whole((1, D_MODEL)), _whole((1, 128))),
        compiler_params=_params(("arbitrary",)),
    )(x, tgt, oa, ob, sz, mz, ga, gb, gate, gf, wa, wb, wo, wat, wbt, wot)


def _bwdprep_call(dsq, dsk, dsv, dsz, dqn, dqp, dkn, dvv, dkpt, dmz, dga, dgb, cq, ckv, cos256, sin256,
                  qg, kvg, wqt, wkvt):
    s_len = cq.shape[0]
    tm = min(ROW_TILE, s_len)

    def body(dsq_ref, dsk_ref, dsv_ref, dsz_ref, dqn_ref, dqp_ref, dkn_ref, dvv_ref, dkpt_ref, dmz_ref,
             dga_ref, dgb_ref, cq_ref, ckv_ref, cos_ref, sin_ref, qg_ref, kvg_ref, wqt_ref, wkvt_ref,
             dp_ref, dwq_ref, dwkv_ref, dqg_ref, dkvg_ref):
        @pl.when(pl.program_id(0) == 0)
        def _():
            dwq_ref[...] = jnp.zeros_like(dwq_ref)
            dwkv_ref[...] = jnp.zeros_like(dwkv_ref)
            dqg_ref[...] = jnp.zeros_like(dqg_ref)
            dkvg_ref[...] = jnp.zeros_like(dkvg_ref)

        cos = cos_ref[...]
        sin = sin_ref[...]
        dp_ref[:, O_SQ:O_SK] = dsq_ref[...]
        dp_ref[:, O_SK:O_SV] = jnp.transpose(dsk_ref[...]).astype(BF16)
        dp_ref[:, O_SV:O_SZ] = jnp.transpose(dsv_ref[...]).astype(BF16)
        dp_ref[:, O_SZ:O_CQ] = dsz_ref[...]
        dp_ref[:, O_MZ:O_GA] = dmz_ref[...]
        dp_ref[:, O_GA:O_GB] = dga_ref[...]
        dp_ref[:, O_GB:O_KR] = dgb_ref[...]
        dkp = jnp.transpose(dkpt_ref[...])
        dp_ref[:, O_KR:O_KR + 128] = (dkp * cos[:, :128]).astype(BF16)
        dp_ref[:, O_KR + 128:O_END] = (dkp * sin[:, :128]).astype(BF16)
        dp_ref[:, O_END:W_INT] = jnp.zeros((tm, W_INT - O_END), BF16)

        cq = cq_ref[...]
        rq = lax.rsqrt(jnp.mean(cq * cq, axis=-1, keepdims=True) + EPS)
        cqh = cq * rq
        qg = qg_ref[...]
        cqn = (cqh * qg).astype(BF16)
        dqp = dqp_ref[...].astype(F32)
        dqa = jnp.concatenate([dqn_ref[...], (dqp * cos).astype(BF16), (dqp * sin).astype(BF16)], axis=1)
        dcqn = _dot(dqa, wqt_ref[...])
        dwq_ref[...] += _dot_tn(cqn, dqa)
        dqg_ref[...] += jnp.sum(dcqn * cqh, axis=0, keepdims=True)
        dh = dcqn * qg
        dcq = rq * (dh - cqh * jnp.mean(dh * cqh, axis=-1, keepdims=True))
        dp_ref[:, O_CQ:O_CKV] = dcq.astype(BF16)

        ckv = ckv_ref[...]
        rk = lax.rsqrt(jnp.mean(ckv * ckv, axis=-1, keepdims=True) + EPS)
        ckh = ckv * rk
        kvg = kvg_ref[...]
        ckvn = (ckh * kvg).astype(BF16)
        dkva = jnp.concatenate([jnp.transpose(dkn_ref[...]).astype(BF16),
                                jnp.transpose(dvv_ref[...]).astype(BF16)], axis=1)
        dckvn = _dot(dkva, wkvt_ref[...])
        dwkv_ref[...] += _dot_tn(ckvn, dkva)
        dkvg_ref[...] += jnp.sum(dckvn * ckh, axis=0, keepdims=True)
        dh2 = dckvn * kvg
        dckv = rk * (dh2 - ckh * jnp.mean(dh2 * ckh, axis=-1, keepdims=True))
        dp_ref[:, O_CKV:O_MZ] = dckv.astype(BF16)

    return pl.pallas_call(
        body, name="bwdprep", grid=(s_len // tm,),
        out_shape=(_sds((s_len, W_INT), BF16), _sds((Q_RANK, 1024), F32), _sds((KV_RANK, 1024), F32),
                   _sds((1, Q_RANK), F32), _sds((1, KV_RANK), F32)),
        in_specs=[_rows(tm, 512), _cols(512, tm), _cols(512, tm), _rows(tm, 512), _rows(tm, 512), _rows(tm, 256),
                  _cols(512, tm), _cols(512, tm), _cols(128, tm), _rows(tm, 512), _rows(tm, D_MODEL),
                  _rows(tm, D_MODEL), _rows(tm, Q_RANK), _rows(tm, KV_RANK), _rows(tm, 256), _rows(tm, 256),
                  _whole((1, Q_RANK)), _whole((1, KV_RANK)), _whole((1024, Q_RANK)), _whole((1024, KV_RANK))],
        out_specs=(_rows(tm, W_INT), _whole((Q_RANK, 1024)), _whole((KV_RANK, 1024)),
                   _whole((1, Q_RANK)), _whole((1, KV_RANK))),
        compiler_params=_params(("arbitrary",)),
    )(dsq, dsk, dsv, dsz, dqn, dqp, dkn, dvv, dkpt, dmz, dga, dgb, cq, ckv, cos256, sin256, qg, kvg, wqt, wkvt)


def _dh_call(dproj, w_int_t, x, dx2, scale, g1):
    s_len = x.shape[0]
    tm = min(ROW_TILE, s_len)

    def body(dp_ref, wt_ref, x_ref, dx2_ref, sc_ref, g1_ref, gx_ref, dsh_ref, dsc_ref, dg1_ref):
        @pl.when(pl.program_id(0) == 0)
        def _():
            dsh_ref[...] = jnp.zeros_like(dsh_ref)
            dsc_ref[...] = jnp.zeros_like(dsc_ref)
            dg1_ref[...] = jnp.zeros_like(dg1_ref)

        dh = _dot(dp_ref[...], wt_ref[...])
        xt = x_ref[...]
        r = lax.rsqrt(jnp.mean(xt * xt, axis=-1, keepdims=True) + EPS)
        xh = xt * r
        g1 = g1_ref[...]
        xg = xh * g1
        dsh_ref[...] += jnp.sum(dh, axis=0, keepdims=True)
        dsc_ref[...] += jnp.sum(dh * xg, axis=0, keepdims=True)
        dxg = dh * (1.0 + sc_ref[...])
        dg1_ref[...] += jnp.sum(dxg * xh, axis=0, keepdims=True)
        dxh = dxg * g1
        gx_ref[...] = dx2_ref[...] + r * (dxh - xh * jnp.mean(dxh * xh, axis=-1, keepdims=True))

    return pl.pallas_call(
        body, name="dh", grid=(s_len // tm,),
        out_shape=(_sds((s_len, D_MODEL), F32), _sds((1, D_MODEL), F32), _sds((1, D_MODEL), F32),
                   _sds((1, D_MODEL), F32)),
        in_specs=[_rows(tm, W_INT), _whole((W_INT, D_MODEL)), _rows(tm, D_MODEL), _rows(tm, D_MODEL),
                  _whole((1, D_MODEL)), _whole((1, D_MODEL))],
        out_specs=(_rows(tm, D_MODEL), _whole((1, D_MODEL)), _whole((1, D_MODEL)), _whole((1, D_MODEL))),
        compiler_params=_params(("arbitrary",)),
    )(dproj, w_int_t, x, dx2, scale, g1)


def _dwin_call(h, dproj):
    s_len = h.shape[0]
    tm = min(2 * ROW_TILE, s_len)
    nc = 4
    chunk = W_INT // nc

    def body(h_ref, dp_ref, dw_ref):
        @pl.when(pl.program_id(1) == 0)
        def _():
            dw_ref[...] = jnp.zeros_like(dw_ref)

        dw_ref[...] += _dot_tn(h_ref[...], dp_ref[...])

    return pl.pallas_call(
        body, name="dwin", grid=(nc, s_len // tm),
        out_shape=_sds((D_MODEL, nc * chunk), F32),
        in_specs=[pl.BlockSpec((tm, D_MODEL), lambda c, i: (i, 0)),
                  pl.BlockSpec((tm, chunk), lambda c, i: (i, c))],
        out_specs=pl.BlockSpec((D_MODEL, chunk), lambda c, i: (0, c)),
        compiler_params=_params(("parallel", "arbitrary")),
    )(h, dproj)


def _small_call(svg, ct, dmod_sh):
    def body(sv_ref, ct_ref, dm_ref, tot_ref, gwada_ref):
        acc = sv_ref[0:1, :]
        for d in range(1, N_DEV):
            acc = acc + sv_ref[d:d + 1, :]
        tot_ref[...] = acc
        gwada_ref[...] = lax.dot_general(ct_ref[...], dm_ref[...], (((1,), (0,)), ((), ())),
                                         precision=lax.Precision.HIGHEST, preferred_element_type=F32)

    vmem = pl.BlockSpec(memory_space=pltpu.VMEM)
    return pl.pallas_call(
        body, name="small_grads",
        out_shape=(_sds((1, 8 * SV_COLS), F32), _sds((D_MODEL, 768), F32)),
        in_specs=[vmem, vmem, vmem], out_specs=(vmem, vmem),
        compiler_params=_params(),
    )(svg, ct, dmod_sh)


def _adamw_tile_rows(rows, cols):
    budget = 2 << 20
    if rows * cols * 4 <= budget or rows % 8:
        return rows
    best = 8
    for tr in range(8, rows + 1, 8):
        if rows % tr == 0 and tr * cols * 4 <= budget:
            best = tr
    return best


def _adamw_call(name, w, g, m, v):
    rows, cols = w.shape
    tr = _adamw_tile_rows(rows, cols)

    def body(w_ref, g_ref, m_ref, v_ref, d_ref, nm_ref, nv_ref):
        gg = g_ref[...]
        m2 = ADAM_B1 * m_ref[...] + (1.0 - ADAM_B1) * gg
        v2 = ADAM_B2 * v_ref[...] + (1.0 - ADAM_B2) * (gg * gg)
        m_hat = m2 / (1.0 - ADAM_B1 ** ADAM_STEP)
        v_hat = v2 / (1.0 - ADAM_B2 ** ADAM_STEP)
        d_ref[...] = -ADAM_LR * (m_hat / (jnp.sqrt(v_hat) + ADAM_EPS) + ADAM_WD * w_ref[...])
        nm_ref[...] = m2
        nv_ref[...] = v2

    spec = pl.BlockSpec((tr, cols), lambda i: (i, 0))
    return pl.pallas_call(
        body, name="adamw_" + name, grid=(rows // tr,),
        out_shape=(_sds((rows, cols), F32),) * 3,
        in_specs=[spec] * 4, out_specs=(spec,) * 3,
        compiler_params=_params(("parallel",)),
    )(w, g, m, v)


IN_SHARD = IN_WIDTH // N_CHIPS
HALF_D = D_MODEL // 2
SMALL_ROWS = (576, 512, 1024, 1024, 2048)
SMALL_TOTAL = sum(SMALL_ROWS)
SMALL_HALF = SMALL_TOTAL // 2
SMALL_SUM_ROWS = 432


def _gather2_call(c_row, w_ada_sh, pack_in, pack_small):
    def body(c_ref, wada_ref, pki_ref, pks_ref, mg_ref, cg_ref, gwi_ref, gws_ref,
             cv, ssem_c, rsem_c, ssem_m, rsem_m, ssem_w, rsem_w, ssem_f, rsem_f, lsem):
        x, y, c = lax.axis_index("x"), lax.axis_index("y"), lax.axis_index("c")
        me = 4 * x + 2 * y + c
        chip = 2 * x + y
        rel3 = [(1, 0), (0, 1), (1, 1)]
        packs = [(pki_ref, gwi_ref), (pks_ref, gws_ref)]

        sends = []
        for j, (dx, dy) in enumerate(rel3):
            for a, (pk, gw) in enumerate(packs):
                cp = pltpu.make_async_remote_copy(
                    src_ref=pk.at[c], dst_ref=gw.at[chip, c], send_sem=ssem_w.at[j, a], recv_sem=rsem_w.at[j, a],
                    device_id=(_flip(x, dx), _flip(y, dy), c), device_id_type=MESH)
                cp.start()
                sends.append(cp)
        owns = []
        for a, (pk, gw) in enumerate(packs):
            own = pltpu.make_async_copy(pk, gw.at[chip], lsem.at[a])
            own.start()
            owns.append(own)

        cv[me] = c_ref[...]
        for r in range(1, N_DEV):
            dx, dy, dc = (r >> 2) & 1, (r >> 1) & 1, r & 1
            cp = pltpu.make_async_remote_copy(
                src_ref=c_ref, dst_ref=cv.at[me], send_sem=ssem_c.at[r - 1], recv_sem=rsem_c.at[r - 1],
                device_id=(_flip(x, dx), _flip(y, dy), _flip(c, dc)), device_id_type=MESH)
            cp.start()
            sends.append(cp)
        for r in range(1, N_DEV):
            dx, dy, dc = (r >> 2) & 1, (r >> 1) & 1, r & 1
            src = 4 * _flip(x, dx) + 2 * _flip(y, dy) + _flip(c, dc)
            pltpu.make_async_remote_copy(
                src_ref=c_ref, dst_ref=cv.at[src], send_sem=ssem_c.at[r - 1], recv_sem=rsem_c.at[r - 1],
                device_id=(x, y, c), device_id_type=MESH).wait_recv()
        rows = lax.broadcasted_iota(jnp.int32, (N_DEV, D_MODEL), 0)
        call = jnp.zeros((N_DEV, D_MODEL), F32)
        for b in range(N_DEV):
            call = jnp.where(rows == b, jnp.broadcast_to(cv[b], (N_DEV, D_MODEL)), call)
        cg_ref[...] = call

        mg_ref[chip] = lax.dot_general(call, wada_ref[...], (((1,), (0,)), ((), ())),
                                       precision=lax.Precision.HIGHEST, preferred_element_type=F32)
        for j, (dx, dy) in enumerate(rel3):
            cp = pltpu.make_async_remote_copy(
                src_ref=mg_ref.at[chip], dst_ref=mg_ref.at[chip], send_sem=ssem_m.at[j], recv_sem=rsem_m.at[j],
                device_id=(_flip(x, dx), _flip(y, dy), c), device_id_type=MESH)
            cp.start()
            sends.append(cp)
        for j, (dx, dy) in enumerate(rel3):
            src_chip = 2 * _flip(x, dx) + _flip(y, dy)
            pltpu.make_async_remote_copy(
                src_ref=mg_ref.at[src_chip], dst_ref=mg_ref.at[src_chip], send_sem=ssem_m.at[j],
                recv_sem=rsem_m.at[j], device_id=(x, y, c), device_id_type=MESH).wait_recv()
        for j, (dx, dy) in enumerate(rel3):
            src_chip = 2 * _flip(x, dx) + _flip(y, dy)
            for a, (pk, gw) in enumerate(packs):
                pltpu.make_async_remote_copy(
                    src_ref=pk.at[c], dst_ref=gw.at[src_chip, c], send_sem=ssem_w.at[j, a],
                    recv_sem=rsem_w.at[j, a], device_id=(x, y, c), device_id_type=MESH).wait_recv()
                cp = pltpu.make_async_remote_copy(
                    src_ref=gw.at[src_chip, c], dst_ref=gw.at[src_chip, c], send_sem=ssem_f.at[j, a],
                    recv_sem=rsem_f.at[j, a], device_id=(x, y, 1 - c), device_id_type=MESH)
                cp.start()
                sends.append(cp)
        for j, (dx, dy) in enumerate(rel3):
            src_chip = 2 * _flip(x, dx) + _flip(y, dy)
            for a, (pk, gw) in enumerate(packs):
                pltpu.make_async_remote_copy(
                    src_ref=pk.at[c], dst_ref=gw.at[src_chip, 1 - c], send_sem=ssem_f.at[j, a],
                    recv_sem=rsem_f.at[j, a], device_id=(x, y, c), device_id_type=MESH).wait_recv()
        for cp in sends:
            cp.wait_send()
        for own in owns:
            own.wait()

    vmem = pl.BlockSpec(memory_space=pltpu.VMEM)
    return pl.pallas_call(
        body, name="gather_fwd",
        out_shape=(_sds((N_CHIPS, N_DEV, 768), F32), _sds((N_DEV, D_MODEL), F32),
                   _sds((N_CHIPS, 2, IN_SHARD, HALF_D), BF16), _sds((N_CHIPS, 2, SMALL_HALF, LANES), BF16)),
        in_specs=[vmem, vmem, vmem, vmem], out_specs=(vmem, vmem, vmem, vmem),
        scratch_shapes=[
            pltpu.VMEM((N_DEV, 1, D_MODEL), F32),
            pltpu.SemaphoreType.DMA((N_DEV - 1,)), pltpu.SemaphoreType.DMA((N_DEV - 1,)),
            pltpu.SemaphoreType.DMA((3,)), pltpu.SemaphoreType.DMA((3,)),
            pltpu.SemaphoreType.DMA((3, 2)), pltpu.SemaphoreType.DMA((3, 2)),
            pltpu.SemaphoreType.DMA((3, 2)), pltpu.SemaphoreType.DMA((3, 2)),
            pltpu.SemaphoreType.DMA((2,)),
        ],
        compiler_params=_params(),
    )(c_row, w_ada_sh, pack_in, pack_small)


def _reduce2_call(g_in, g_small, sv):
    def body(gi_ref, gs_ref, sv_ref, fi_ref, fs_ref, svg_ref, land_i, land_s,
             ssem_g, rsem_g, ssem_s, rsem_s, ssem_x, rsem_x, lsem):
        x, y, c = lax.axis_index("x"), lax.axis_index("y"), lax.axis_index("c")
        me = 4 * x + 2 * y + c
        pieces = [(gi_ref, land_i), (gs_ref, land_s)]
        copies = []
        for r in range(1, N_DEV):
            dx, dy, dc = (r >> 2) & 1, (r >> 1) & 1, r & 1
            tx, ty, tc = _flip(x, dx), _flip(y, dy), _flip(c, dc)
            tgt = 4 * tx + 2 * ty + tc
            for a, (g, land) in enumerate(pieces):
                cp = pltpu.make_async_remote_copy(
                    src_ref=g.at[tgt], dst_ref=land.at[me], send_sem=ssem_g.at[r - 1, a],
                    recv_sem=rsem_g.at[r - 1, a], device_id=(tx, ty, tc), device_id_type=MESH)
                cp.start()
                copies.append(cp)
            cp = pltpu.make_async_remote_copy(
                src_ref=sv_ref, dst_ref=svg_ref.at[me], send_sem=ssem_s.at[r - 1],
                recv_sem=rsem_s.at[r - 1], device_id=(tx, ty, tc), device_id_type=MESH)
            cp.start()
            copies.append(cp)
        owns = []
        for a, (g, land) in enumerate(pieces):
            own = pltpu.make_async_copy(g.at[me], land.at[me], lsem.at[a])
            own.start()
            owns.append(own)
        svg_ref[me] = sv_ref[...]
        for r in range(1, N_DEV):
            dx, dy, dc = (r >> 2) & 1, (r >> 1) & 1, r & 1
            src = 4 * _flip(x, dx) + 2 * _flip(y, dy) + _flip(c, dc)
            for a, (g, land) in enumerate(pieces):
                pltpu.make_async_remote_copy(
                    src_ref=g.at[src], dst_ref=land.at[src], send_sem=ssem_g.at[r - 1, a],
                    recv_sem=rsem_g.at[r - 1, a], device_id=(x, y, c), device_id_type=MESH).wait_recv()
            pltpu.make_async_remote_copy(
                src_ref=sv_ref, dst_ref=svg_ref.at[src], send_sem=ssem_s.at[r - 1],
                recv_sem=rsem_s.at[r - 1], device_id=(x, y, c), device_id_type=MESH).wait_recv()
        for own in owns:
            own.wait()

        for qd in range(HALF_D // LANES):
            sl = slice(LANES * qd, LANES * qd + LANES)
            acc = land_i[0, :, sl].astype(F32)
            for d in range(1, N_DEV):
                acc = acc + land_i[d, :, sl].astype(F32)
            fi_ref[c, :, sl] = acc

        def sum_rows(i, carry):
            sl = pl.ds(pl.multiple_of(i * SMALL_SUM_ROWS, 16), SMALL_SUM_ROWS)
            acc = land_s[0, sl, :].astype(F32)
            for d in range(1, N_DEV):
                acc = acc + land_s[d, sl, :].astype(F32)
            fs_ref[c, sl, :] = acc
            return carry

        lax.fori_loop(0, SMALL_HALF // SMALL_SUM_ROWS, sum_rows, 0)
        swaps = []
        for a, f in enumerate((fi_ref, fs_ref)):
            cp = pltpu.make_async_remote_copy(
                src_ref=f.at[c], dst_ref=f.at[c], send_sem=ssem_x.at[a], recv_sem=rsem_x.at[a],
                device_id=(x, y, 1 - c), device_id_type=MESH)
            cp.start()
            swaps.append(cp)
        for a, f in enumerate((fi_ref, fs_ref)):
            pltpu.make_async_remote_copy(
                src_ref=f.at[c], dst_ref=f.at[1 - c], send_sem=ssem_x.at[a], recv_sem=rsem_x.at[a],
                device_id=(x, y, c), device_id_type=MESH).wait_recv()
        for cp in swaps + copies:
            cp.wait_send()

    vmem = pl.BlockSpec(memory_space=pltpu.VMEM)
    return pl.pallas_call(
        body, name="grad_reduce",
        out_shape=(_sds((2, IN_SHARD, HALF_D), F32), _sds((2, SMALL_HALF, LANES), F32),
                   _sds((N_DEV, 8, SV_COLS), F32)),
        in_specs=[vmem, vmem, vmem], out_specs=(vmem, vmem, vmem),
        scratch_shapes=[
            pltpu.VMEM((N_DEV, IN_SHARD, HALF_D), BF16), pltpu.VMEM((N_DEV, SMALL_HALF, LANES), BF16),
            pltpu.SemaphoreType.DMA((N_DEV - 1, 2)), pltpu.SemaphoreType.DMA((N_DEV - 1, 2)),
            pltpu.SemaphoreType.DMA((N_DEV - 1,)), pltpu.SemaphoreType.DMA((N_DEV - 1,)),
            pltpu.SemaphoreType.DMA((2,)), pltpu.SemaphoreType.DMA((2,)), pltpu.SemaphoreType.DMA((2,)),
        ],
        compiler_params=_params(),
    )(g_in, g_small, sv)


def _reduce3_call(g_in, g_small, sv):
    def body(gi_ref, gs_ref, sv_ref, fi_ref, fs_ref, svg_ref, pair_i, pair_s, send_i, send_s, land_i, land_s,
             ssem_p, rsem_p, ssem_g, rsem_g, ssem_s, rsem_s, ssem_x, rsem_x):
        x, y, c = lax.axis_index("x"), lax.axis_index("y"), lax.axis_index("c")
        me = 4 * x + 2 * y + c
        chip = 2 * x + y
        rel3 = [(1, 0), (0, 1), (1, 1)]
        payloads = [(gi_ref, pair_i, send_i, land_i, fi_ref), (gs_ref, pair_s, send_s, land_s, fs_ref)]
        copies = []

        for k in range(N_CHIPS):
            for a, (g, pair, _, _, _) in enumerate(payloads):
                cp = pltpu.make_async_remote_copy(
                    src_ref=g.at[2 * k + 1 - c], dst_ref=pair.at[k], send_sem=ssem_p.at[k, a],
                    recv_sem=rsem_p.at[k, a], device_id=(x, y, 1 - c), device_id_type=MESH)
                cp.start()
                copies.append(cp)

        for r in range(1, N_DEV):
            dx, dy, dc = (r >> 2) & 1, (r >> 1) & 1, r & 1
            cp = pltpu.make_async_remote_copy(
                src_ref=sv_ref, dst_ref=svg_ref.at[me], send_sem=ssem_s.at[r - 1], recv_sem=rsem_s.at[r - 1],
                device_id=(_flip(x, dx), _flip(y, dy), _flip(c, dc)), device_id_type=MESH)
            cp.start()
            copies.append(cp)
        svg_ref[me] = sv_ref[...]

        for k in range(N_CHIPS):
            for a, (g, pair, _, _, _) in enumerate(payloads):
                pltpu.make_async_remote_copy(
                    src_ref=g.at[2 * k + c], dst_ref=pair.at[k], send_sem=ssem_p.at[k, a],
                    recv_sem=rsem_p.at[k, a], device_id=(x, y, c), device_id_type=MESH).wait_recv()

        def pair_sum(k, store_in, store_small):
            for qd in range(HALF_D // LANES):
                sl = slice(LANES * qd, LANES * qd + LANES)
                store_in(sl, gi_ref[2 * k + c, :, sl].astype(F32) + pair_i[k, :, sl].astype(F32))

            def rows(i, carry):
                sl = pl.ds(pl.multiple_of(i * SMALL_SUM_ROWS, 16), SMALL_SUM_ROWS)
                store_small(sl, gs_ref[2 * k + c, sl, :].astype(F32) + pair_s[k, sl, :].astype(F32))
                return carry

            lax.fori_loop(0, SMALL_HALF // SMALL_SUM_ROWS, rows, 0)

        for j, (dx, dy) in enumerate(rel3):
            tx, ty = _flip(x, dx), _flip(y, dy)

            def put_in(sl, val, j=j):
                send_i[j, :, sl] = val.astype(BF16)

            def put_small(sl, val, j=j):
                send_s[j, sl, :] = val.astype(BF16)

            pair_sum(2 * tx + ty, put_in, put_small)
            for a, (_, _, send, land, _) in enumerate(payloads):
                cp = pltpu.make_async_remote_copy(
                    src_ref=send.at[j], dst_ref=land.at[j], send_sem=ssem_g.at[j, a], recv_sem=rsem_g.at[j, a],
                    device_id=(tx, ty, c), device_id_type=MESH)
                cp.start()
                copies.append(cp)

        def own_in(sl, val):
            fi_ref[c, :, sl] = val

        def own_small(sl, val):
            fs_ref[c, sl, :] = val

        pair_sum(chip, own_in, own_small)
        for j in range(3):
            for a, (_, _, send, land, _) in enumerate(payloads):
                pltpu.make_async_remote_copy(
                    src_ref=send.at[j], dst_ref=land.at[j], send_sem=ssem_g.at[j, a], recv_sem=rsem_g.at[j, a],
                    device_id=(x, y, c), device_id_type=MESH).wait_recv()
            for qd in range(HALF_D // LANES):
                sl = slice(LANES * qd, LANES * qd + LANES)
                fi_ref[c, :, sl] += land_i[j, :, sl].astype(F32)

            def add_rows(i, carry, j=j):
                sl = pl.ds(pl.multiple_of(i * SMALL_SUM_ROWS, 16), SMALL_SUM_ROWS)
                fs_ref[c, sl, :] += land_s[j, sl, :].astype(F32)
                return carry

            lax.fori_loop(0, SMALL_HALF // SMALL_SUM_ROWS, add_rows, 0)

        for a, f in enumerate((fi_ref, fs_ref)):
            cp = pltpu.make_async_remote_copy(
                src_ref=f.at[c], dst_ref=f.at[c], send_sem=ssem_x.at[a], recv_sem=rsem_x.at[a],
                device_id=(x, y, 1 - c), device_id_type=MESH)
            cp.start()
            copies.append(cp)
        for a, f in enumerate((fi_ref, fs_ref)):
            pltpu.make_async_remote_copy(
                src_ref=f.at[c], dst_ref=f.at[1 - c], send_sem=ssem_x.at[a], recv_sem=rsem_x.at[a],
                device_id=(x, y, c), device_id_type=MESH).wait_recv()
        for r in range(1, N_DEV):
            dx, dy, dc = (r >> 2) & 1, (r >> 1) & 1, r & 1
            src = 4 * _flip(x, dx) + 2 * _flip(y, dy) + _flip(c, dc)
            pltpu.make_async_remote_copy(
                src_ref=sv_ref, dst_ref=svg_ref.at[src], send_sem=ssem_s.at[r - 1],
                recv_sem=rsem_s.at[r - 1], device_id=(x, y, c), device_id_type=MESH).wait_recv()
        for cp in copies:
            cp.wait_send()

    vmem = pl.BlockSpec(memory_space=pltpu.VMEM)
    return pl.pallas_call(
        body, name="grad_reduce",
        out_shape=(_sds((2, IN_SHARD, HALF_D), F32), _sds((2, SMALL_HALF, LANES), F32),
                   _sds((N_DEV, 8, SV_COLS), F32)),
        in_specs=[vmem, vmem, vmem], out_specs=(vmem, vmem, vmem),
        scratch_shapes=[
            pltpu.VMEM((N_CHIPS, IN_SHARD, HALF_D), BF16), pltpu.VMEM((N_CHIPS, SMALL_HALF, LANES), BF16),
            pltpu.VMEM((3, IN_SHARD, HALF_D), BF16), pltpu.VMEM((3, SMALL_HALF, LANES), BF16),
            pltpu.VMEM((3, IN_SHARD, HALF_D), BF16), pltpu.VMEM((3, SMALL_HALF, LANES), BF16),
            pltpu.SemaphoreType.DMA((N_CHIPS, 2)), pltpu.SemaphoreType.DMA((N_CHIPS, 2)),
            pltpu.SemaphoreType.DMA((3, 2)), pltpu.SemaphoreType.DMA((3, 2)),
            pltpu.SemaphoreType.DMA((N_DEV - 1,)), pltpu.SemaphoreType.DMA((N_DEV - 1,)),
            pltpu.SemaphoreType.DMA((2,)), pltpu.SemaphoreType.DMA((2,)),
        ],
        compiler_params=_params(),
    )(g_in, g_small, sv)


def _dwin_t_call(h, dproj):
    s_len = h.shape[0]
    tm = min(2 * ROW_TILE, s_len)
    nrow = s_len // tm
    nc = 4
    chunk = W_INT // nc

    def body(h_ref, dp_ref, dw_ref, acc):
        i = pl.program_id(1)

        @pl.when(i == 0)
        def _():
            acc[...] = jnp.zeros_like(acc)

        acc[...] += _dot_tn(dp_ref[...], h_ref[...])

        @pl.when(i == nrow - 1)
        def _():
            dw_ref[...] = acc[...].astype(BF16)

    return pl.pallas_call(
        body, name="dwin", grid=(nc, nrow),
        out_shape=_sds((W_INT, D_MODEL), BF16),
        in_specs=[pl.BlockSpec((tm, D_MODEL), lambda c, i: (i, 0)),
                  pl.BlockSpec((tm, chunk), lambda c, i: (i, c))],
        out_specs=pl.BlockSpec((chunk, D_MODEL), lambda c, i: (c, 0)),
        scratch_shapes=[pltpu.VMEM((chunk, D_MODEL), F32)],
        compiler_params=_params(("parallel", "arbitrary")),
    )(h, dproj)


def _swap_rows(w, group):
    r, n = w.shape
    return w.reshape(r // group, 2, group // 2, n)[:, ::-1].reshape(r, n)


def _internal_weights_t(w_in_t, w_uq, w_ukv):
    krot_t = w_in_t[2688:2720]
    w_int_t = jnp.concatenate([
        w_in_t[0:512] * jnp.asarray(0.125, w_in_t.dtype), w_in_t[512:2048],
        w_in_t[2048:2432], w_in_t[2432:2688], w_in_t[2720:3232], w_in_t[3232:4256], w_in_t[4256:5280],
        jnp.tile(krot_t, (4, 1)), jnp.tile(_swap_rows(krot_t, 32), (4, 1)),
        jnp.zeros((W_INT - O_END, D_MODEL), w_in_t.dtype)], axis=0)
    uq = w_uq.reshape(Q_RANK, N_HEADS, 96)
    wp = uq[:, :, 64:].reshape(Q_RANK, 256)
    w_q = jnp.concatenate([uq[:, :, :64].reshape(Q_RANK, 512), wp, _swap_halves(wp, 32)], axis=1)
    ukv = w_ukv.reshape(KV_RANK, N_HEADS, 128)
    w_kv = jnp.concatenate([ukv[:, :, :64].reshape(KV_RANK, 512), ukv[:, :, 64:].reshape(KV_RANK, 512)], axis=1)
    return w_int_t, w_q, w_kv


def _true_weight_grads_t(dwi_t, dwq, dwkv):
    dkr = dwi_t[O_KR:O_KR + 128].astype(F32).reshape(4, 32, D_MODEL).sum(axis=0)
    dkr_sw = dwi_t[O_KR + 128:O_END].astype(F32).reshape(4, 32, D_MODEL).sum(axis=0)
    dkrot_t = (dkr + _swap_rows(dkr_sw, 32)).astype(dwi_t.dtype)
    g_in_t = jnp.concatenate([
        dwi_t[0:512] * jnp.asarray(0.125, dwi_t.dtype), dwi_t[512:2048], dwi_t[O_CQ:O_CKV], dwi_t[O_CKV:O_MZ],
        dkrot_t, dwi_t[O_MZ:O_GA], dwi_t[O_GA:O_GB], dwi_t[O_GB:O_KR]], axis=0)
    dwp = dwq[:, 512:768] + _swap_halves(dwq[:, 768:1024], 32)
    g_uq = jnp.concatenate([dwq[:, :512].reshape(Q_RANK, N_HEADS, 64), dwp.reshape(Q_RANK, N_HEADS, 32)],
                           axis=2).reshape(Q_RANK, 768)
    g_ukv = jnp.concatenate([dwkv[:, :512].reshape(KV_RANK, N_HEADS, 64), dwkv[:, 512:].reshape(KV_RANK, N_HEADS, 64)],
                            axis=2).reshape(KV_RANK, 1024)
    return g_in_t, g_uq, g_ukv


def _swap_halves(w, group):
    r, n = w.shape
    return w.reshape(r, n // group, 2, group // 2)[:, :, ::-1, :].reshape(r, n)


def _pack_shards(parts):
    return jnp.concatenate([p.reshape(-1, LANES) for p in parts], axis=0)


def _unpack_chip_major(gw):
    offs = [0]
    for r in PACK_ROWS:
        offs.append(offs[-1] + r)

    def cols(i, rows, shard_cols):
        blk = gw[:, offs[i]:offs[i + 1]].reshape(N_CHIPS, rows, shard_cols)
        return blk.transpose(1, 0, 2).reshape(rows, N_CHIPS * shard_cols)

    w_in = cols(0, D_MODEL, 1320)
    w_uq = cols(1, Q_RANK, 192)
    w_ukv = cols(2, KV_RANK, 256)
    w_a = cols(3, 512, 256)
    w_b = cols(4, 512, 256)
    w_out = gw[:, offs[5]:offs[6]].reshape(D_MODEL, D_MODEL)
    return w_in, w_uq, w_ukv, w_a, w_b, w_out


def _unpack_small(gw):
    offs = [0]
    for r in SMALL_ROWS:
        offs.append(offs[-1] + r)

    def cols(i, rows, shard_cols):
        blk = gw[:, offs[i]:offs[i + 1]].reshape(N_CHIPS, rows, shard_cols)
        return blk.transpose(1, 0, 2).reshape(rows, N_CHIPS * shard_cols)

    return (cols(0, Q_RANK, 192), cols(1, KV_RANK, 256), cols(2, 512, 256), cols(3, 512, 256),
            gw[:, offs[4]:offs[5]].reshape(D_MODEL, D_MODEL))


def _internal_weights(w_in, w_uq, w_ukv):
    krot = w_in[:, 2688:2720]
    w_int = jnp.concatenate([
        w_in[:, 0:512] * jnp.asarray(0.125, w_in.dtype), w_in[:, 512:2048],
        w_in[:, 2048:2432], w_in[:, 2432:2688], w_in[:, 2720:3232], w_in[:, 3232:4256], w_in[:, 4256:5280],
        jnp.tile(krot, (1, 4)), jnp.tile(_swap_halves(krot, 32), (1, 4)),
        jnp.zeros((D_MODEL, W_INT - O_END), w_in.dtype)], axis=1)
    uq = w_uq.reshape(Q_RANK, N_HEADS, 96)
    wp = uq[:, :, 64:].reshape(Q_RANK, 256)
    w_q = jnp.concatenate([uq[:, :, :64].reshape(Q_RANK, 512), wp, _swap_halves(wp, 32)], axis=1)
    ukv = w_ukv.reshape(KV_RANK, N_HEADS, 128)
    w_kv = jnp.concatenate([ukv[:, :, :64].reshape(KV_RANK, 512), ukv[:, :, 64:].reshape(KV_RANK, 512)], axis=1)
    return w_int, w_q, w_kv


def _true_weight_grads(dwi, dwq, dwkv):
    dkr = dwi[:, O_KR:O_KR + 128].reshape(D_MODEL, 4, 32).sum(axis=1)
    dkr_sw = dwi[:, O_KR + 128:O_END].reshape(D_MODEL, 4, 32).sum(axis=1)
    dkrot = dkr + _swap_halves(dkr_sw, 32)
    g_in = jnp.concatenate([
        dwi[:, 0:512] * 0.125, dwi[:, 512:2048], dwi[:, O_CQ:O_CKV], dwi[:, O_CKV:O_MZ], dkrot,
        dwi[:, O_MZ:O_GA], dwi[:, O_GA:O_GB], dwi[:, O_GB:O_KR]], axis=1)
    dwp = dwq[:, 512:768] + _swap_halves(dwq[:, 768:1024], 32)
    g_uq = jnp.concatenate([dwq[:, :512].reshape(Q_RANK, N_HEADS, 64), dwp.reshape(Q_RANK, N_HEADS, 32)],
                           axis=2).reshape(Q_RANK, 768)
    g_ukv = jnp.concatenate([dwkv[:, :512].reshape(KV_RANK, N_HEADS, 64), dwkv[:, 512:].reshape(KV_RANK, N_HEADS, 64)],
                            axis=2).reshape(KV_RANK, 1024)
    return g_in, g_uq, g_ukv


def _chip_major(g, shard_cols):
    r = g.shape[0]
    return g.reshape(r, N_CHIPS, shard_cols).transpose(1, 0, 2).reshape(N_CHIPS, -1, LANES)


def kernel(x, c, positions, w_ada, b_ada, norm_gain, w_in, q_norm_gain, w_uq, kv_norm_gain, w_ukv, w_branch_a, w_branch_b, w_out, final_norm_gain, loss_target, m_w_ada, m_b_ada, m_norm_gain, m_w_in, m_q_norm_gain, m_w_uq, m_kv_norm_gain, m_w_ukv, m_w_branch_a, m_w_branch_b, m_w_out, m_final_norm_gain, v_w_ada, v_b_ada, v_norm_gain, v_w_in, v_q_norm_gain, v_w_uq, v_kv_norm_gain, v_w_ukv, v_w_branch_a, v_w_branch_b, v_w_out, v_final_norm_gain):
    ix, iy, ic = lax.axis_index("x"), lax.axis_index("y"), lax.axis_index("c")
    me = 4 * ix + 2 * iy + ic
    chip = 2 * ix + iy
    xs = x[0]
    tgt = loss_target[0]
    s_len = xs.shape[0]

    w_in_t = jnp.swapaxes(w_in[0], 0, 1)
    w_in_tb = w_in_t.astype(BF16)
    pack_in = jnp.stack([w_in_tb[:, :HALF_D], w_in_tb[:, HALF_D:]], axis=0)
    small_shards = (w_uq[0], w_ukv[0], w_branch_a[0], w_branch_b[0], w_out[0])
    pack_small = _pack_shards([s.astype(BF16) for s in small_shards]).reshape(2, SMALL_HALF, LANES)
    mg, call, gw_in, gw_small = _gather2_call(c, w_ada[0], pack_in, pack_small)
    mod = mg.transpose(1, 0, 2).reshape(N_DEV, 3 * D_MODEL) + b_ada
    mod_me = lax.dynamic_slice_in_dim(mod, me, 1, axis=0)
    shift, scale, gate = mod_me[:, :D_MODEL], mod_me[:, D_MODEL:2 * D_MODEL], mod_me[:, 2 * D_MODEL:]

    f_in_t = jnp.concatenate([gw_in[:, 0], gw_in[:, 1]], axis=2).reshape(IN_WIDTH, D_MODEL)
    f_uq, f_ukv, f_a, f_b, f_out = _unpack_small(gw_small.reshape(N_CHIPS, SMALL_TOTAL, LANES))
    w_int_t, w_q, w_kv = _internal_weights_t(f_in_t, f_uq, f_ukv)

    inv_freq = ROPE_BASE ** (-jnp.arange(0, ROPE_DIM, 2, dtype=F32) / ROPE_DIM)
    ang = positions[0].astype(F32)[:, None] * inv_freq
    cs, sn = jnp.cos(ang), jnp.sin(ang)
    cos256 = jnp.tile(jnp.concatenate([cs, cs], axis=1), (1, 8))
    sin256 = jnp.tile(jnp.concatenate([-sn, sn], axis=1), (1, 8))

    (h, sq, sk, sv, sz, cq, ckv, mz, ga, gb, kpt, qn, qp, kn, vv) = _inproj_call(
        xs, shift, scale, norm_gain, w_int_t, w_q, w_kv, q_norm_gain, kv_norm_gain, cos256, sin256)
    oa, lt = _sb4_fwd_call(sq, sk, sv)
    ob, lse = _mla4_fwd_call(qn, qp, kn, kpt, vv)

    gf = final_norm_gain.reshape(1, D_MODEL)
    (dx2, doa, dob, dsz, dmz, dga, dgb, dwo, dwa, dwb, dgf, dgate, loss_p) = _post_call(
        xs, tgt, oa, ob, sz, mz, ga, gb, gate, gf, f_a, f_b, f_out, f_a.T, f_b.T, f_out.T)

    dsq, dsk_t, dsv_t = _sb4_bwd_call(sq, sk, sv, doa, lt)
    dqn, dqp, dkn_t, dkpt_t, dvv_t = _mla4_bwd_call(qn, qp, kn, kpt, vv, ob, dob, lse)

    dproj, dwq, dwkv, dqg, dkvg = _bwdprep_call(
        dsq, dsk_t, dsv_t, dsz, dqn, dqp, dkn_t, dvv_t, dkpt_t, dmz, dga, dgb, cq, ckv, cos256, sin256,
        q_norm_gain, kv_norm_gain, w_q.T, w_kv.T)
    grad_x, dshift, dscale, dg1 = _dh_call(dproj, w_int_t, xs, dx2, scale, norm_gain)
    dwi_t = _dwin_t_call(h, dproj)
    g_in_t, g_uq, g_ukv = _true_weight_grads_t(dwi_t, dwq, dwkv)

    g_in_c = g_in_t.reshape(N_CHIPS, IN_SHARD, D_MODEL)
    g_in_pieces = jnp.stack([g_in_c[:, :, :HALF_D], g_in_c[:, :, HALF_D:]], axis=1).reshape(N_DEV, IN_SHARD, HALF_D)
    g_small = jnp.concatenate([
        _chip_major(g_uq, 192), _chip_major(g_ukv, 256), _chip_major(dwa, 256), _chip_major(dwb, 256),
        dwo.reshape(N_CHIPS, -1, LANES)], axis=1).astype(BF16).reshape(N_DEV, SMALL_HALF, LANES)
    small = jnp.concatenate([
        dshift, dscale, dgate, dg1, dqg, dkvg, dgf, loss_p,
        jnp.zeros((1, 8 * SV_COLS - 5888), F32)], axis=1).reshape(8, SV_COLS)
    full_in, full_small, svg = _reduce3_call(g_in_pieces, g_small, small)
    gs_in_t = jnp.concatenate([full_in[0], full_in[1]], axis=1)
    full = full_small.reshape(SMALL_TOTAL, LANES)
    offs = [0]
    for r in SMALL_ROWS:
        offs.append(offs[-1] + r)
    gs_uq = full[offs[0]:offs[1]].reshape(Q_RANK, 192)
    gs_ukv = full[offs[1]:offs[2]].reshape(KV_RANK, 256)
    gs_a = full[offs[2]:offs[3]].reshape(512, 256)
    gs_b = full[offs[3]:offs[4]].reshape(512, 256)
    gs_out = full[offs[4]:offs[5]].reshape(256, D_MODEL)

    svm = svg.reshape(N_DEV, 8 * SV_COLS)
    dmod_sh = lax.dynamic_slice_in_dim(svm[:, :3 * D_MODEL], chip * 768, 768, axis=1)
    tot, gs_ada = _small_call(svm, call.T, dmod_sh)
    g_bada = tot[:, 0:3072]
    g_g1 = tot[:, 3072:4096]
    g_qg = tot[:, 4096:4480]
    g_kvg = tot[:, 4480:4736]
    g_gf = tot[:, 4736:5760]
    loss = tot[0, 5760]

    names = ["w_ada", "b_ada", "norm_gain", "w_in", "q_norm_gain", "w_uq", "kv_norm_gain", "w_ukv",
             "w_branch_a", "w_branch_b", "w_out", "final_norm_gain"]
    ws = [w_ada[0], b_ada, norm_gain, w_in_t, q_norm_gain, w_uq[0], kv_norm_gain, w_ukv[0],
          w_branch_a[0], w_branch_b[0], w_out[0], final_norm_gain.reshape(1, D_MODEL)]
    gs = [gs_ada, g_bada, g_g1, gs_in_t, g_qg, gs_uq, g_kvg, gs_ukv, gs_a, gs_b, gs_out, g_gf]
    ms = [m_w_ada[0], m_b_ada, m_norm_gain, jnp.swapaxes(m_w_in[0], 0, 1), m_q_norm_gain, m_w_uq[0],
          m_kv_norm_gain, m_w_ukv[0], m_w_branch_a[0], m_w_branch_b[0], m_w_out[0],
          m_final_norm_gain.reshape(1, D_MODEL)]
    vs = [v_w_ada[0], v_b_ada, v_norm_gain, jnp.swapaxes(v_w_in[0], 0, 1), v_q_norm_gain, v_w_uq[0],
          v_kv_norm_gain, v_w_ukv[0], v_w_branch_a[0], v_w_branch_b[0], v_w_out[0],
          v_final_norm_gain.reshape(1, D_MODEL)]
    refs = [w_ada, b_ada, norm_gain, w_in, q_norm_gain, w_uq, kv_norm_gain, w_ukv,
            w_branch_a, w_branch_b, w_out, final_norm_gain]
    grads, deltas, new_ms, new_vs = [], [], [], []
    for n, w_, g_, m_, v_, ref in zip(names, ws, gs, ms, vs, refs):
        outs = (g_,) + _adamw_call(n, w_, g_, m_, v_)
        if n == "w_in":
            outs = tuple(jnp.swapaxes(o_, 0, 1) for o_ in outs)
        for lst, o_ in zip((grads, deltas, new_ms, new_vs), outs):
            lst.append(o_.reshape(ref.shape))

    return (loss, grad_x.reshape(x.shape), *grads, *deltas, *new_ms, *new_vs)
```

```python
import functools
import math

import jax
import jax.numpy as jnp
from jax import lax
from jax.experimental import pallas as pl
from jax.experimental.pallas import tpu as pltpu

F32 = jnp.float32
BF16 = jnp.bfloat16

D_MODEL = 1024
SB_WIDTH = 512
MLA_WIDTH = 512
Q_RANK = 384
KV_RANK = 256
ROPE_DIM = 32
N_HEADS = 8
IN_WIDTH = 5280
EPS = 1e-6
ROPE_BASE = 10000.0
MLA_SCALE = 1.0 / math.sqrt(96.0)

ADAM_LR = 0.001
ADAM_B1 = 0.9
ADAM_B2 = 0.999
ADAM_EPS = 1e-08
ADAM_WD = 0.01
ADAM_STEP = 10

O_SQ, O_SK, O_SV, O_SZ, O_CQ, O_CKV, O_MZ, O_GA, O_GB, O_KR, O_END = (
    0, 512, 1024, 1536, 2048, 2432, 2688, 3200, 4224, 5248, 5504)
W_INT = 5632

N_CHIPS = 4
N_DEV = 8
LANES = 128
PACK_ROWS = (10560, 576, 512, 1024, 1024, 2048)
PACK_TOTAL = sum(PACK_ROWS)
HALF_ROWS = PACK_TOTAL // 2
SV_COLS = 768

ROW_TILE = 256
ATT_TILE = 256
ATT_Q_TILES = 2
MLA_KEY_TILE = 512
VMEM_LIMIT = 56 * 1024 * 1024

MESH = pl.DeviceIdType.MESH


def _dot(a, b):
    return lax.dot_general(a, b, (((1,), (0,)), ((), ())), preferred_element_type=F32)


def _dot_nt(a, b):
    return lax.dot_general(a, b, (((1,), (1,)), ((), ())), preferred_element_type=F32)


def _dot_tn(a, b):
    return lax.dot_general(a, b, (((0,), (0,)), ((), ())), preferred_element_type=F32)


def _sigmoid(z):
    return 1.0 / (1.0 + jnp.exp(-z))


def _params(sem=None):
    if sem is None:
        return pltpu.CompilerParams(vmem_limit_bytes=VMEM_LIMIT)
    return pltpu.CompilerParams(dimension_semantics=sem, vmem_limit_bytes=VMEM_LIMIT)


def _rows(tm, n):
    return pl.BlockSpec((tm, n), lambda i: (i, 0))


def _cols(n, tm):
    return pl.BlockSpec((n, tm), lambda i: (0, i))


def _whole(shape):
    nd = len(shape)
    return pl.BlockSpec(shape, lambda i: (0,) * nd)


def _sds(shape, dtype):
    return jax.ShapeDtypeStruct(shape, dtype)


def _flip(v, d):
    return 1 - v if d else v


def _gather_call(c_row, w_ada_sh, pack):
    def body(c_ref, wada_ref, pk_ref, mg_ref, cg_ref, gw_ref,
             cv, ssem_c, rsem_c, ssem_m, rsem_m, ssem_w, rsem_w, ssem_f, rsem_f, lsem):
        x, y, c = lax.axis_index("x"), lax.axis_index("y"), lax.axis_index("c")
        me = 4 * x + 2 * y + c
        chip = 2 * x + y
        rel3 = [(1, 0), (0, 1), (1, 1)]

        wcopies = []
        for j, (dx, dy) in enumerate(rel3):
            cp = pltpu.make_async_remote_copy(
                src_ref=pk_ref.at[c], dst_ref=gw_ref.at[chip, c], send_sem=ssem_w.at[j], recv_sem=rsem_w.at[j],
                device_id=(_flip(x, dx), _flip(y, dy), c), device_id_type=MESH)
            cp.start()
            wcopies.append(cp)
        own = pltpu.make_async_copy(pk_ref, gw_ref.at[chip], lsem)
        own.start()

        cv[me] = c_ref[...]
        ccopies = []
        for r in range(1, N_DEV):
            dx, dy, dc = (r >> 2) & 1, (r >> 1) & 1, r & 1
            cp = pltpu.make_async_remote_copy(
                src_ref=c_ref, dst_ref=cv.at[me], send_sem=ssem_c.at[r - 1], recv_sem=rsem_c.at[r - 1],
                device_id=(_flip(x, dx), _flip(y, dy), _flip(c, dc)), device_id_type=MESH)
            cp.start()
            ccopies.append(cp)
        for r in range(1, N_DEV):
            dx, dy, dc = (r >> 2) & 1, (r >> 1) & 1, r & 1
            src = 4 * _flip(x, dx) + 2 * _flip(y, dy) + _flip(c, dc)
            pltpu.make_async_remote_copy(
                src_ref=c_ref, dst_ref=cv.at[src], send_sem=ssem_c.at[r - 1], recv_sem=rsem_c.at[r - 1],
                device_id=(x, y, c), device_id_type=MESH).wait_recv()
        rows = lax.broadcasted_iota(jnp.int32, (N_DEV, D_MODEL), 0)
        call = jnp.zeros((N_DEV, D_MODEL), F32)
        for b in range(N_DEV):
            call = jnp.where(rows == b, jnp.broadcast_to(cv[b], (N_DEV, D_MODEL)), call)
        cg_ref[...] = call

        mg_ref[chip] = lax.dot_general(call, wada_ref[...], (((1,), (0,)), ((), ())),
                                       precision=lax.Precision.HIGHEST, preferred_element_type=F32)
        mcopies = []
        for j, (dx, dy) in enumerate(rel3):
            cp = pltpu.make_async_remote_copy(
                src_ref=mg_ref.at[chip], dst_ref=mg_ref.at[chip], send_sem=ssem_m.at[j], recv_sem=rsem_m.at[j],
                device_id=(_flip(x, dx), _flip(y, dy), c), device_id_type=MESH)
            cp.start()
            mcopies.append(cp)
        for j, (dx, dy) in enumerate(rel3):
            src_chip = 2 * _flip(x, dx) + _flip(y, dy)
            pltpu.make_async_remote_copy(
                src_ref=mg_ref.at[src_chip], dst_ref=mg_ref.at[src_chip], send_sem=ssem_m.at[j],
                recv_sem=rsem_m.at[j], device_id=(x, y, c), device_id_type=MESH).wait_recv()
        fcopies = []
        for j, (dx, dy) in enumerate(rel3):
            src_chip = 2 * _flip(x, dx) + _flip(y, dy)
            pltpu.make_async_remote_copy(
                src_ref=pk_ref.at[c], dst_ref=gw_ref.at[src_chip, c], send_sem=ssem_w.at[j], recv_sem=rsem_w.at[j],
                device_id=(x, y, c), device_id_type=MESH).wait_recv()
            cp = pltpu.make_async_remote_copy(
                src_ref=gw_ref.at[src_chip, c], dst_ref=gw_ref.at[src_chip, c], send_sem=ssem_f.at[j],
                recv_sem=rsem_f.at[j], device_id=(x, y, 1 - c), device_id_type=MESH)
            cp.start()
            fcopies.append(cp)
        for j, (dx, dy) in enumerate(rel3):
            src_chip = 2 * _flip(x, dx) + _flip(y, dy)
            pltpu.make_async_remote_copy(
                src_ref=pk_ref.at[c], dst_ref=gw_ref.at[src_chip, 1 - c], send_sem=ssem_f.at[j],
                recv_sem=rsem_f.at[j], device_id=(x, y, c), device_id_type=MESH).wait_recv()
        for cp in ccopies + mcopies + wcopies + fcopies:
            cp.wait_send()
        own.wait()

    vmem = pl.BlockSpec(memory_space=pltpu.VMEM)
    return pl.pallas_call(
        body, name="gather_fwd",
        out_shape=(_sds((N_CHIPS, N_DEV, 768), F32), _sds((N_DEV, D_MODEL), F32),
                   _sds((N_CHIPS, 2, HALF_ROWS, LANES), BF16)),
        in_specs=[vmem, vmem, vmem], out_specs=(vmem, vmem, vmem),
        scratch_shapes=[
            pltpu.VMEM((N_DEV, 1, D_MODEL), F32),
            pltpu.SemaphoreType.DMA((N_DEV - 1,)), pltpu.SemaphoreType.DMA((N_DEV - 1,)),
            pltpu.SemaphoreType.DMA((3,)), pltpu.SemaphoreType.DMA((3,)),
            pltpu.SemaphoreType.DMA((3,)), pltpu.SemaphoreType.DMA((3,)),
            pltpu.SemaphoreType.DMA((3,)), pltpu.SemaphoreType.DMA((3,)),
            pltpu.SemaphoreType.DMA,
        ],
        compiler_params=_params(),
    )(c_row, w_ada_sh, pack)


SUM_ROWS = 656


def _reduce_call(gpack, sv):
    def body(g_ref, sv_ref, full_ref, svg_ref, land, ssem_g, rsem_g, ssem_s, rsem_s, ssem_x, rsem_x, lsem):
        x, y, c = lax.axis_index("x"), lax.axis_index("y"), lax.axis_index("c")
        me = 4 * x + 2 * y + c
        copies = []
        for r in range(1, N_DEV):
            dx, dy, dc = (r >> 2) & 1, (r >> 1) & 1, r & 1
            tx, ty, tc = _flip(x, dx), _flip(y, dy), _flip(c, dc)
            tgt = 4 * tx + 2 * ty + tc
            cp = pltpu.make_async_remote_copy(
                src_ref=g_ref.at[tgt], dst_ref=land.at[me], send_sem=ssem_g.at[r - 1],
                recv_sem=rsem_g.at[r - 1], device_id=(tx, ty, tc), device_id_type=MESH)
            cp.start()
            copies.append(cp)
            cp = pltpu.make_async_remote_copy(
                src_ref=sv_ref, dst_ref=svg_ref.at[me], send_sem=ssem_s.at[r - 1],
                recv_sem=rsem_s.at[r - 1], device_id=(tx, ty, tc), device_id_type=MESH)
            cp.start()
            copies.append(cp)
        own = pltpu.make_async_copy(g_ref.at[me], land.at[me], lsem)
        own.start()
        svg_ref[me] = sv_ref[...]
        for r in range(1, N_DEV):
            dx, dy, dc = (r >> 2) & 1, (r >> 1) & 1, r & 1
            src = 4 * _flip(x, dx) + 2 * _flip(y, dy) + _flip(c, dc)
            pltpu.make_async_remote_copy(
                src_ref=g_ref.at[src], dst_ref=land.at[src], send_sem=ssem_g.at[r - 1],
                recv_sem=rsem_g.at[r - 1], device_id=(x, y, c), device_id_type=MESH).wait_recv()
            pltpu.make_async_remote_copy(
                src_ref=sv_ref, dst_ref=svg_ref.at[src], send_sem=ssem_s.at[r - 1],
                recv_sem=rsem_s.at[r - 1], device_id=(x, y, c), device_id_type=MESH).wait_recv()
        own.wait()

        def sum_rows(i, carry):
            sl = pl.ds(pl.multiple_of(i * SUM_ROWS, 16), SUM_ROWS)
            acc = land[0, sl, :].astype(F32)
            for d in range(1, N_DEV):
                acc = acc + land[d, sl, :].astype(F32)
            full_ref[c, sl, :] = acc
            return carry

        lax.fori_loop(0, HALF_ROWS // SUM_ROWS, sum_rows, 0)
        swap = pltpu.make_async_remote_copy(
            src_ref=full_ref.at[c], dst_ref=full_ref.at[c], send_sem=ssem_x, recv_sem=rsem_x,
            device_id=(x, y, 1 - c), device_id_type=MESH)
        swap.start()
        pltpu.make_async_remote_copy(
            src_ref=full_ref.at[c], dst_ref=full_ref.at[1 - c], send_sem=ssem_x, recv_sem=rsem_x,
            device_id=(x, y, c), device_id_type=MESH).wait_recv()
        swap.wait_send()
        for cp in copies:
            cp.wait_send()

    vmem = pl.BlockSpec(memory_space=pltpu.VMEM)
    return pl.pallas_call(
        body, name="grad_reduce",
        out_shape=(_sds((2, HALF_ROWS, LANES), F32), _sds((N_DEV, 8, SV_COLS), F32)),
        in_specs=[vmem, vmem], out_specs=(vmem, vmem),
        scratch_shapes=[
            pltpu.VMEM((N_DEV, HALF_ROWS, LANES), BF16),
            pltpu.SemaphoreType.DMA((N_DEV - 1,)), pltpu.SemaphoreType.DMA((N_DEV - 1,)),
            pltpu.SemaphoreType.DMA((N_DEV - 1,)), pltpu.SemaphoreType.DMA((N_DEV - 1,)),
            pltpu.SemaphoreType.DMA, pltpu.SemaphoreType.DMA, pltpu.SemaphoreType.DMA,
        ],
        compiler_params=_params(),
    )(gpack, sv)


def _inproj_call(x, shift, scale, g1, w_int, w_q, w_kv, qg, kvg, cos256, sin256):
    s_len = x.shape[0]
    tm = min(ROW_TILE, s_len)

    def body(x_ref, sh_ref, sc_ref, g1_ref, w_ref, wq_ref, wkv_ref, qg_ref, kvg_ref, cos_ref, sin_ref,
             h_ref, sq_ref, sk_ref, sv_ref, sz_ref, cq_ref, ckv_ref, mz_ref, ga_ref, gb_ref, kpt_ref,
             qn_ref, qp_ref, kn_ref, vv_ref):
        xt = x_ref[...]
        r = lax.rsqrt(jnp.mean(xt * xt, axis=-1, keepdims=True) + EPS)
        h = (xt * r * g1_ref[...]) * (1.0 + sc_ref[...]) + sh_ref[...]
        hb = h.astype(BF16)
        h_ref[...] = hb

        def seg(a, b):
            return _dot_nt(hb, w_ref[a:b, :])

        sq_ref[...] = seg(O_SQ, O_SK).astype(BF16)
        sk_ref[...] = seg(O_SK, O_SV).astype(BF16)
        sv_ref[...] = seg(O_SV, O_SZ).astype(BF16)
        sz_ref[...] = seg(O_SZ, O_CQ)
        mz_ref[...] = seg(O_MZ, O_GA)
        ga_ref[...] = seg(O_GA, O_GB)
        gb_ref[...] = seg(O_GB, O_KR)
        cos = cos_ref[...]
        sin = sin_ref[...]
        kr = seg(O_KR, O_END)
        kpt_ref[...] = (kr[:, :128] * cos[:, :128] + kr[:, 128:] * sin[:, :128]).astype(BF16)

        cq = seg(O_CQ, O_CKV)
        cq_ref[...] = cq
        rq = lax.rsqrt(jnp.mean(cq * cq, axis=-1, keepdims=True) + EPS)
        cqn = (cq * rq * qg_ref[...]).astype(BF16)
        qa = _dot(cqn, wq_ref[...])
        qn_ref[...] = qa[:, :512].astype(BF16)
        qp_ref[...] = (qa[:, 512:768] * cos + qa[:, 768:] * sin).astype(BF16)

        ckv = seg(O_CKV, O_MZ)
        ckv_ref[...] = ckv
        rk = lax.rsqrt(jnp.mean(ckv * ckv, axis=-1, keepdims=True) + EPS)
        ckvn = (ckv * rk * kvg_ref[...]).astype(BF16)
        kva = _dot(ckvn, wkv_ref[...])
        kn_ref[...] = kva[:, :512].astype(BF16)
        vv_ref[...] = kva[:, 512:].astype(BF16)

    outs = [
        (D_MODEL, BF16), (512, BF16), (512, BF16), (512, BF16), (512, F32), (Q_RANK, F32), (KV_RANK, F32),
        (512, F32), (D_MODEL, F32), (D_MODEL, F32), (128, BF16), (512, BF16), (256, BF16), (512, BF16), (512, BF16),
    ]
    return pl.pallas_call(
        body, name="inproj", grid=(s_len // tm,),
        out_shape=tuple(_sds((s_len, n), dt) for n, dt in outs),
        in_specs=[_rows(tm, D_MODEL), _whole((1, D_MODEL)), _whole((1, D_MODEL)), _whole((1, D_MODEL)),
                  _whole((W_INT, D_MODEL)), _whole((Q_RANK, 1024)), _whole((KV_RANK, 1024)),
                  _whole((1, Q_RANK)), _whole((1, KV_RANK)), _rows(tm, 256), _rows(tm, 256)],
        out_specs=tuple(_rows(tm, n) for n, _ in outs),
        compiler_params=_params(("parallel",)),
    )(x, shift, scale, g1, w_int, w_q, w_kv, qg, kvg, cos256, sin256)


def _softplus(z):
    return jnp.maximum(z, 0.0) + jnp.log(1.0 + jnp.exp(-jnp.abs(z)))


def _split_bf16(a):
    hi = a.astype(BF16)
    lo = (a - hi.astype(F32)).astype(BF16)
    return hi, lo


def _sb_fwd_call(q, k, v):
    s_len = q.shape[0]
    tk = min(ATT_TILE, s_len)
    tq = min(ATT_Q_TILES * tk, s_len)
    r = tq // tk
    nq = s_len // tq

    def body(q_ref, k_ref, v_ref, o_ref, lt_ref):
        i = pl.program_id(1)
        q2 = q_ref[...]
        lane = lax.broadcasted_iota(jnp.int32, (1, 128), 1)
        krow = lax.broadcasted_iota(jnp.int32, (tk, tk), 0)
        kcol = lax.broadcasted_iota(jnp.int32, (tk, tk), 1)
        row = lax.broadcasted_iota(jnp.int32, (tq, tk), 0)
        col = lax.broadcasted_iota(jnp.int32, (tq, tk), 1)
        later = (krow > kcol).astype(BF16)
        later2 = jnp.concatenate([later, later], axis=0)
        valids = [col + u * tk < row for u in range(r)]
        hms = [(lane // 64) == hh for hh in range(2)]
        qms = [jnp.where(hm, q2, jnp.zeros_like(q2)) for hm in hms]

        def block(j, carry, valid):
            runs, acc = list(carry[:2]), carry[2]
            off = pl.multiple_of(j * tk, tk)
            kb = k_ref[pl.ds(off, tk), :]
            vb = v_ref[pl.ds(off, tk), :]
            ws = []
            for hh in range(2):
                z = _dot_nt(qms[hh], kb)
                lg = -_softplus(z)
                lm = jnp.where(valid, lg, 0.0) if valid is not None else lg
                hi, lo = _split_bf16(lm)
                suf = _dot(jnp.concatenate([hi, lo], axis=1), later2)
                w = jnp.exp(z + lg + suf + runs[hh])
                if valid is not None:
                    w = jnp.where(valid, w, 0.0)
                ws.append(w.astype(BF16))
                runs[hh] = runs[hh] + jnp.sum(lm, axis=1, keepdims=True)
            vstack = jnp.concatenate([jnp.where(hm, vb, jnp.zeros_like(vb)) for hm in hms], axis=0)
            acc = acc + _dot(jnp.concatenate(ws, axis=1), vstack)
            return runs[0], runs[1], acc

        zero = jnp.zeros((tq,1), F32)
        carry = block(i, (zero, zero, jnp.zeros((tq,128), F32)), True)
        carry = lax.fori_loop(1, i + 1, lambda jj, cr: block(i - jj, cr, False), carry)
        lt_ref[0, :, 0:1] = carry[0]
        lt_ref[0, :, 1:2] = carry[1]
        o_ref[...] = carry[2]

    return pl.pallas_call(
        body, name="sb_fwd", grid=(4, nq),
        out_shape=(_sds((s_len, SB_WIDTH), F32), _sds((4, s_len, 2), F32)),
        in_specs=[pl.BlockSpec((tq,128), lambda p, i: (i, p)),
                  pl.BlockSpec((s_len, 128), lambda p, i: (0, p)),
                  pl.BlockSpec((s_len, 128), lambda p, i: (0, p))],
        out_specs=(pl.BlockSpec((tq,128), lambda p, i: (i, p)),
                   pl.BlockSpec((1, t, 2), lambda p, i: (p, i, 0))),
        compiler_params=_params(("parallel", "parallel")),
    )(q, k, v)


def _sb_bwd_call(q, k, v, do, lt):
    s_len = q.shape[0]
    tk = min(ATT_TILE, s_len)
    tq = min(ATT_Q_TILES * tk, s_len)
    r = tq // tk
    nq = s_len // tq

    def body(q_ref, k_ref, v_ref, do_ref, lt_ref, dq_ref, dk_ref, dv_ref):
        i = pl.program_id(1)

        @pl.when(i == 0)
        def _():
            dk_ref[...] = jnp.zeros_like(dk_ref)
            dv_ref[...] = jnp.zeros_like(dv_ref)

        q2 = q_ref[...]
        do2 = do_ref[...].astype(BF16)
        lane = lax.broadcasted_iota(jnp.int32, (1, 128), 1)
        krow = lax.broadcasted_iota(jnp.int32, (tk, tk), 0)
        kcol = lax.broadcasted_iota(jnp.int32, (tk, tk), 1)
        row = lax.broadcasted_iota(jnp.int32, (tq, tk), 0)
        col = lax.broadcasted_iota(jnp.int32, (tq, tk), 1)
        earlier = (krow < kcol).astype(BF16)
        earlier2 = jnp.concatenate([earlier, earlier], axis=0)
        valids = [col + u * tk < row for u in range(r)]
        hms = [(lane // 64) == hh for hh in range(2)]
        qms = [jnp.where(hm, q2, jnp.zeros_like(q2)) for hm in hms]
        doms = [jnp.where(hm, do2, jnp.zeros_like(do2)) for hm in hms]
        ltots = [lt_ref[0, :, hh:hh + 1] for hh in range(2)]
        qstack = jnp.concatenate(qms, axis=0)
        dostack = jnp.concatenate(doms, axis=0)

        def block(j, carry, valid):
            lpre, ppre, dq = list(carry[0:2]), list(carry[2:4]), carry[4]
            off = pl.multiple_of(j * tk, tk)
            kb = k_ref[pl.ds(off, tk), :]
            vb = v_ref[pl.ds(off, tk), :]
            dzs, avs = [], []
            for hh in range(2):
                z = _dot_nt(qms[hh], kb)
                sp = _softplus(z)
                lg = -sp
                lm = jnp.where(valid, lg, 0.0) if valid is not None else lg
                hi, lo = _split_bf16(lm)
                before = _dot(jnp.concatenate([hi, lo], axis=1), earlier2)
                between = ltots[hh] - (lpre[hh] + before + lm)
                a = jnp.exp(z + lg + between)
                if valid is not None:
                    a = jnp.where(valid, a, 0.0)
                p = a * _dot_nt(doms[hh], vb)
                phi, plo = _split_bf16(p)
                pbefore = ppre[hh] + _dot(jnp.concatenate([phi, plo], axis=1), earlier2)
                sig = jnp.exp(z - sp)
                dz = p - sig * (p + pbefore)
                if valid is not None:
                    dz = jnp.where(valid, dz, 0.0)
                dzs.append(dz.astype(BF16))
                avs.append(a.astype(BF16))
                lpre[hh] = lpre[hh] + jnp.sum(lm, axis=1, keepdims=True)
                ppre[hh] = ppre[hh] + jnp.sum(p, axis=1, keepdims=True)
            kstack = jnp.concatenate([jnp.where(hm, kb, jnp.zeros_like(kb)) for hm in hms], axis=0)
            dq = dq + _dot(jnp.concatenate(dzs, axis=1), kstack)
            dk_ref[pl.ds(off, tk), :] += _dot_tn(jnp.concatenate(dzs, axis=0), qstack)
            dv_ref[pl.ds(off, tk), :] += _dot_tn(jnp.concatenate(avs, axis=0), dostack)
            return lpre[0], lpre[1], ppre[0], ppre[1], dq

        zero = jnp.zeros((tq,1), F32)
        carry = lax.fori_loop(0, i, lambda j, cr: block(j, cr, False),
                              (zero, zero, zero, zero, jnp.zeros((tq,128), F32)))
        carry = block(i, carry, True)
        dq_ref[...] = carry[4].astype(BF16)

    return pl.pallas_call(
        body, name="sb_bwd", grid=(4, nq),
        out_shape=(_sds((s_len, SB_WIDTH), BF16), _sds((s_len, SB_WIDTH), F32), _sds((s_len, SB_WIDTH), F32)),
        in_specs=[pl.BlockSpec((tq,128), lambda p, i: (i, p)),
                  pl.BlockSpec((s_len, 128), lambda p, i: (0, p)),
                  pl.BlockSpec((s_len, 128), lambda p, i: (0, p)),
                  pl.BlockSpec((tq,128), lambda p, i: (i, p)),
                  pl.BlockSpec((1, t, 2), lambda p, i: (p, i, 0))],
        out_specs=(pl.BlockSpec((tq,128), lambda p, i: (i, p)),
                   pl.BlockSpec((s_len, 128), lambda p, i: (0, p)),
                   pl.BlockSpec((s_len, 128), lambda p, i: (0, p))),
        compiler_params=_params(("parallel", "arbitrary")),
    )(q, k, v, do, lt)


def _mla_fwd_call(qn, qp, kn, kpt, v):
    s_len = qn.shape[0]
    tk = min(ATT_TILE, s_len)
    tq = min(ATT_Q_TILES * tk, s_len)
    r = tq // tk
    nq = s_len // tq

    def body(qn_ref, qp_ref, kn_ref, kpt_ref, v_ref, o_ref, lse_ref):
        i = pl.program_id(1)
        qn2 = qn_ref[...]
        qp2 = qp_ref[...]
        lane256 = lax.broadcasted_iota(jnp.int32, (1, 256), 1)
        lane128 = lax.broadcasted_iota(jnp.int32, (1, 128), 1)
        krow = lax.broadcasted_iota(jnp.int32, (tk, tk), 0)
        kcol = lax.broadcasted_iota(jnp.int32, (tk, tk), 1)
        row = lax.broadcasted_iota(jnp.int32, (tq, tk), 0)
        col = lax.broadcasted_iota(jnp.int32, (tq, tk), 1)
        valids = [col + u * tk <= row for u in range(r)]
        m64s = [(lane256 // 64) == hh for hh in range(4)]
        m32s = [(lane128 // 32) == hh for hh in range(4)]
        qcs = [jnp.concatenate([jnp.where(m64s[hh], qn2, jnp.zeros_like(qn2)),
                                jnp.where(m32s[hh], qp2, jnp.zeros_like(qp2))], axis=1) for hh in range(4)]

        def by_head(vals):
            return jnp.where(m64s[0], vals[0], jnp.where(m64s[1], vals[1], jnp.where(m64s[2], vals[2], vals[3])))

        def block(j, carry, valid):
            ms, ls, acc = list(carry[0:4]), list(carry[4:8]), carry[8]
            off = pl.multiple_of(j * tk, tk)
            kc = jnp.concatenate([kn_ref[pl.ds(off, tk), :], kpt_ref[pl.ds(off, tk), :]], axis=1)
            vb = v_ref[pl.ds(off, tk), :]
            ps, alphas = [], []
            for hh in range(4):
                s = _dot_nt(qcs[hh], kc) * MLA_SCALE
                if valid is not None:
                    s = jnp.where(valid, s, -1e30)
                mn = jnp.maximum(ms[hh], jnp.max(s, axis=1, keepdims=True))
                p = jnp.exp(s - mn)
                alpha = jnp.exp(ms[hh] - mn)
                ls[hh] = alpha * ls[hh] + jnp.sum(p, axis=1, keepdims=True)
                ms[hh] = mn
                ps.append(p.astype(BF16))
                alphas.append(alpha)
            vstack = jnp.concatenate([jnp.where(m64, vb, jnp.zeros_like(vb)) for m64 in m64s], axis=0)
            acc = by_head(alphas) * acc + _dot(jnp.concatenate(ps, axis=1), vstack)
            return (*ms, *ls, acc)

        neg = jnp.full((tq, 1), -1e30, F32)
        zero = jnp.zeros((tq,1), F32)
        carry = lax.fori_loop(0, i, lambda j, cr: block(j, cr, False),
                              (neg, neg, neg, neg, zero, zero, zero, zero, jnp.zeros((tq,256), F32)))
        carry = block(i, carry, True)
        o_ref[...] = carry[8] / by_head(list(carry[4:8]))
        for hh in range(4):
            lse_ref[0, :, hh:hh + 1] = carry[hh] + jnp.log(carry[4 + hh])

    return pl.pallas_call(
        body, name="mla_fwd", grid=(2, nq),
        out_shape=(_sds((s_len, MLA_WIDTH), F32), _sds((2, s_len, 4), F32)),
        in_specs=[pl.BlockSpec((tq,256), lambda g, i: (i, g)),
                  pl.BlockSpec((tq,128), lambda g, i: (i, g)),
                  pl.BlockSpec((s_len, 256), lambda g, i: (0, g)),
                  pl.BlockSpec((s_len, 128), lambda g, i: (0, 0)),
                  pl.BlockSpec((s_len, 256), lambda g, i: (0, g))],
        out_specs=(pl.BlockSpec((tq,256), lambda g, i: (i, g)),
                   pl.BlockSpec((1, tq, 4), lambda g, i: (g, i, 0))),
        compiler_params=_params(("parallel", "parallel")),
    )(qn, qp, kn, kpt, v)


def _mla_bwd_call(qn, qp, kn, kpt, v, o, do, lse):
    s_len = qn.shape[0]
    tk = min(MLA_KEY_TILE, s_len)
    tq = min(ATT_Q_TILES * ATT_TILE, s_len)
    r = tq // tk
    nq = s_len // tq

    def body(qn_ref, qp_ref, kn_ref, kpt_ref, v_ref, o_ref, do_ref, lse_ref,
             dqn_ref, dqp_ref, dkn_ref, dkpt_ref, dv_ref):
        g = pl.program_id(0)
        i = pl.program_id(1)

        @pl.when(i == 0)
        def _():
            dkn_ref[...] = jnp.zeros_like(dkn_ref)
            dv_ref[...] = jnp.zeros_like(dv_ref)

        @pl.when((i == 0) & (g == 0))
        def _():
            dkpt_ref[...] = jnp.zeros_like(dkpt_ref)

        qn2 = qn_ref[...]
        qp2 = qp_ref[...]
        of = o_ref[...]
        dof = do_ref[...]
        dob = dof.astype(BF16)
        prod = dof * of
        lane256 = lax.broadcasted_iota(jnp.int32, (1, 256), 1)
        lane128 = lax.broadcasted_iota(jnp.int32, (1, 128), 1)
        krow = lax.broadcasted_iota(jnp.int32, (tk, tk), 0)
        kcol = lax.broadcasted_iota(jnp.int32, (tk, tk), 1)
        row = lax.broadcasted_iota(jnp.int32, (tq, tk), 0)
        col = lax.broadcasted_iota(jnp.int32, (tq, tk), 1)
        valids = [col + u * tk <= row for u in range(r)]
        m64s = [(lane256 // 64) == hh for hh in range(4)]
        m32s = [(lane128 // 32) == hh for hh in range(4)]
        qcs = [jnp.concatenate([jnp.where(m64s[hh], qn2, jnp.zeros_like(qn2)),
                                jnp.where(m32s[hh], qp2, jnp.zeros_like(qp2))], axis=1) for hh in range(4)]
        doms = [jnp.where(m64, dob, jnp.zeros_like(dob)) for m64 in m64s]
        dsums = [jnp.sum(jnp.where(m64, prod, 0.0), axis=1, keepdims=True) * MLA_SCALE for m64 in m64s]
        lses = [lse_ref[0, :, hh:hh + 1] for hh in range(4)]
        qstack = jnp.concatenate(qcs, axis=0)
        dostack = jnp.concatenate(doms, axis=0)

        def block(j, dqc, diag):
            off = pl.multiple_of(j * tk, tk)
            knb = kn_ref[pl.ds(off, tk), :]
            kpb = kpt_ref[pl.ds(off, tk), :]
            vb = v_ref[pl.ds(off, tk), :]
            kc = jnp.concatenate([knb, kpb], axis=1)
            dss, pbs = [], []
            for hh in range(4):
                s = _dot_nt(qcs[hh], kc) * MLA_SCALE
                if valid is not None:
                    s = jnp.where(valid, s, -1e30)
                p = jnp.exp(s - lses[hh])
                ds = p * (_dot_nt(doms[hh], vb) * MLA_SCALE - dsums[hh])
                dss.append(ds.astype(BF16))
                pbs.append(p.astype(BF16))
            kstack = jnp.concatenate(
                [jnp.concatenate([jnp.where(m64s[hh], knb, jnp.zeros_like(knb)),
                                  jnp.where(m32s[hh], kpb, jnp.zeros_like(kpb))], axis=1) for hh in range(4)], axis=0)
            dqc = dqc + _dot(jnp.concatenate(dss, axis=1), kstack)
            dkc = _dot_tn(jnp.concatenate(dss, axis=0), qstack)
            dkn_ref[pl.ds(off, tk), :] += dkc[:, :256]
            dkpt_ref[pl.ds(off, tk), :] += dkc[:, 256:]
            dv_ref[pl.ds(off, tk), :] += _dot_tn(jnp.concatenate(pbs, axis=0), dostack)
            return dqc

        dqc = lax.fori_loop(0, i, lambda j, cr: block(j, cr, False), jnp.zeros((tq,384), F32))
        dqc = block(i, dqc, True)
        dqn_ref[...] = dqc[:, :256].astype(BF16)
        dqp_ref[...] = dqc[:, 256:].astype(BF16)

    return pl.pallas_call(
        body, name="mla_bwd", grid=(2, nq),
        out_shape=(_sds((s_len, 512), BF16), _sds((s_len, 256), BF16), _sds((s_len, 512), F32),
                   _sds((s_len, 128), F32), _sds((s_len, 512), F32)),
        in_specs=[pl.BlockSpec((tq,256), lambda g, i: (i, g)),
                  pl.BlockSpec((tq,128), lambda g, i: (i, g)),
                  pl.BlockSpec((s_len, 256), lambda g, i: (0, g)),
                  pl.BlockSpec((s_len, 128), lambda g, i: (0, 0)),
                  pl.BlockSpec((s_len, 256), lambda g, i: (0, g)),
                  pl.BlockSpec((tq,256), lambda g, i: (i, g)),
                  pl.BlockSpec((tq,256), lambda g, i: (i, g)),
                  pl.BlockSpec((1, tq, 4), lambda g, i: (g, i, 0))],
        out_specs=(pl.BlockSpec((tq,256), lambda g, i: (i, g)),
                   pl.BlockSpec((tq,128), lambda g, i: (i, g)),
                   pl.BlockSpec((s_len, 256), lambda g, i: (0, g)),
                   pl.BlockSpec((s_len, 128), lambda g, i: (0, 0)),
                   pl.BlockSpec((s_len, 256), lambda g, i: (0, g))),
        compiler_params=_params(("arbitrary", "arbitrary")),
    )(qn, qp, kn, kpt, v, o, do, lse)


Z_CLAMP = 80.0


def _softplus_clamped(z):
    zc = jnp.minimum(z, Z_CLAMP)
    return zc, jnp.log(1.0 + jnp.exp(zc))


def _tri_sum(a, tri, tri2, passes):
    if passes == 1:
        return _dot(a.astype(BF16), tri)
    hi, lo = _split_bf16(a)
    return _dot(jnp.concatenate([hi, lo], axis=1), tri2)


def _sb4_fwd_call(q, k, v):
    s_len = q.shape[0]
    tk = min(ATT_TILE, s_len)
    tq = min(ATT_Q_TILES * tk, s_len)
    r = tq // tk
    nq = s_len // tq

    def body(q_ref, k_ref, v_ref, o_ref, lt_ref):
        i = pl.program_id(1)
        q2 = q_ref[...]
        lane = lax.broadcasted_iota(jnp.int32, (1, 256), 1)
        krow = lax.broadcasted_iota(jnp.int32, (tk, tk), 0)
        kcol = lax.broadcasted_iota(jnp.int32, (tk, tk), 1)
        row = lax.broadcasted_iota(jnp.int32, (tq, tk), 0)
        col = lax.broadcasted_iota(jnp.int32, (tq, tk), 1)
        later = (krow > kcol).astype(BF16)
        later2 = jnp.concatenate([later, later], axis=0)
        valids = [col + u * tk < row for u in range(r)]
        hms = [(lane // 64) == hh for hh in range(4)]
        qms = [jnp.where(hm, q2, jnp.zeros_like(q2)) for hm in hms]

        def block(j, carry, valid):
            runs, acc = list(carry[:4]), carry[4]
            off = pl.multiple_of(j * tk, tk)
            kb = k_ref[pl.ds(off, tk), :]
            vb = v_ref[pl.ds(off, tk), :]
            ws = []
            for hh in range(4):
                zc, sp = _softplus_clamped(_dot_nt(qms[hh], kb))
                lm = jnp.where(valid, sp, 0.0) if valid is not None else sp
                suf = _tri_sum(lm, later, later2, 1)
                w = jnp.exp(zc - sp - suf - runs[hh])
                if valid is not None:
                    w = jnp.where(valid, w, 0.0)
                ws.append(w.astype(BF16))
                runs[hh] = runs[hh] + jnp.sum(lm, axis=1, keepdims=True)
            vstack = jnp.concatenate([jnp.where(hm, vb, jnp.zeros_like(vb)) for hm in hms], axis=0)
            acc = acc + _dot(jnp.concatenate(ws, axis=1), vstack)
            return (*runs, acc)

        zero = jnp.zeros((tq,1), F32)
        carry = (zero, zero, zero, zero, jnp.zeros((tq, 256), F32))
        for u in reversed(range(r)):
            carry = block(i * r + u, carry, valids[u])
        carry = lax.fori_loop(0, i * r, lambda jj, cr: block(i * r - 1 - jj, cr, None), carry)
        for hh in range(4):
            lt_ref[0, :, hh:hh + 1] = carry[hh]
        o_ref[...] = carry[4]

    return pl.pallas_call(
        body, name="sb_fwd", grid=(2, nq),
        out_shape=(_sds((s_len, SB_WIDTH), F32), _sds((2, s_len, 4), F32)),
        in_specs=[pl.BlockSpec((tq,256), lambda g, i: (i, g)),
                  pl.BlockSpec((s_len, 256), lambda g, i: (0, g)),
                  pl.BlockSpec((s_len, 256), lambda g, i: (0, g))],
        out_specs=(pl.BlockSpec((tq,256), lambda g, i: (i, g)),
                   pl.BlockSpec((1, tq, 4), lambda g, i: (g, i, 0))),
        compiler_params=_params(("parallel", "parallel")),
    )(q, k, v)


def _sb4_bwd_call(q, k, v, do, lt):
    s_len = q.shape[0]
    tk = min(ATT_TILE, s_len)
    tq = min(ATT_Q_TILES * tk, s_len)
    r = tq // tk
    nq = s_len // tq

    def body(q_ref, k_ref, v_ref, do_ref, lt_ref, dq_ref, dk_ref, dv_ref):
        i = pl.program_id(1)

        @pl.when(i == 0)
        def _():
            dk_ref[...] = jnp.zeros_like(dk_ref)
            dv_ref[...] = jnp.zeros_like(dv_ref)

        q2 = q_ref[...]
        do2 = do_ref[...].astype(BF16)
        lane = lax.broadcasted_iota(jnp.int32, (1, 256), 1)
        krow = lax.broadcasted_iota(jnp.int32, (tk, tk), 0)
        kcol = lax.broadcasted_iota(jnp.int32, (tk, tk), 1)
        row = lax.broadcasted_iota(jnp.int32, (tq, tk), 0)
        col = lax.broadcasted_iota(jnp.int32, (tq, tk), 1)
        earlier = (krow < kcol).astype(BF16)
        earlier2 = jnp.concatenate([earlier, earlier], axis=0)
        later = (krow > kcol).astype(BF16)
        later2 = jnp.concatenate([later, later], axis=0)
        valids = [col + u * tk < row for u in range(r)]
        hms = [(lane // 64) == hh for hh in range(4)]
        qms = [jnp.where(hm, q2, jnp.zeros_like(q2)) for hm in hms]
        doms = [jnp.where(hm, do2, jnp.zeros_like(do2)) for hm in hms]
        ltots = [lt_ref[0, :, hh:hh + 1] for hh in range(4)]
        q2t = jnp.transpose(q2.astype(F32))
        do2t = jnp.transpose(do_ref[...])
        subl = lax.broadcasted_iota(jnp.int32, (256, 1), 0)
        qtstack = jnp.concatenate(
            [jnp.where((subl // 64) == hh, q2t, 0.0).astype(BF16) for hh in range(4)], axis=1)
        dotstack = jnp.concatenate(
            [jnp.where((subl // 64) == hh, do2t, 0.0).astype(BF16) for hh in range(4)], axis=1)

        def block(j, carry, valid):
            lpre, ppre, dq = list(carry[0:4]), list(carry[4:8]), carry[8]
            off = pl.multiple_of(j * tk, tk)
            kb = k_ref[pl.ds(off, tk), :]
            vb = v_ref[pl.ds(off, tk), :]
            dzs, avs = [], []
            for hh in range(4):
                zc, sp = _softplus_clamped(_dot_nt(qms[hh], kb))
                lsig = zc - sp
                lm = jnp.where(valid, sp, 0.0) if valid is not None else sp
                rowsum = jnp.sum(lm, axis=1, keepdims=True)
                between = _tri_sum(lm, later, later2, 1) + ((ltots[hh] - lpre[hh]) - rowsum)
                a = jnp.exp(lsig - between)
                if valid is not None:
                    a = jnp.where(valid, a, 0.0)
                p = a * _dot_nt(doms[hh], vb)
                pbefore = ppre[hh] + _tri_sum(p, earlier, earlier2, 1)
                dz = p - jnp.exp(lsig) * (p + pbefore)
                if valid is not None:
                    dz = jnp.where(valid, dz, 0.0)
                dzs.append(dz.astype(BF16))
                avs.append(a.astype(BF16))
                lpre[hh] = lpre[hh] + rowsum
                ppre[hh] = ppre[hh] + jnp.sum(p, axis=1, keepdims=True)
            kstack = jnp.concatenate([jnp.where(hm, kb, jnp.zeros_like(kb)) for hm in hms], axis=0)
            dq = dq + _dot(jnp.concatenate(dzs, axis=1), kstack)
            dk_ref[:, pl.ds(off, tk)] += _dot(qtstack, jnp.concatenate(dzs, axis=0))
            dv_ref[:, pl.ds(off, tk)] += _dot(dotstack, jnp.concatenate(avs, axis=0))
            return (*lpre, *ppre, dq)

        zero = jnp.zeros((tq,1), F32)
        carry = lax.fori_loop(0, i * r, lambda j, cr: block(j, cr, None),
                              (zero,) * 8 + (jnp.zeros((tq, 256), F32),))
        for u in range(r):
            carry = block(i * r + u, carry, valids[u])
        dq_ref[...] = carry[8].astype(BF16)

    return pl.pallas_call(
        body, name="sb_bwd", grid=(2, nq),
        out_shape=(_sds((s_len, SB_WIDTH), BF16), _sds((SB_WIDTH, s_len), F32), _sds((SB_WIDTH, s_len), F32)),
        in_specs=[pl.BlockSpec((tq,256), lambda g, i: (i, g)),
                  pl.BlockSpec((s_len, 256), lambda g, i: (0, g)),
                  pl.BlockSpec((s_len, 256), lambda g, i: (0, g)),
                  pl.BlockSpec((tq,256), lambda g, i: (i, g)),
                  pl.BlockSpec((1, tq, 4), lambda g, i: (g, i, 0))],
        out_specs=(pl.BlockSpec((tq,256), lambda g, i: (i, g)),
                   pl.BlockSpec((256, s_len), lambda g, i: (g, 0)),
                   pl.BlockSpec((256, s_len), lambda g, i: (g, 0))),
        compiler_params=_params(("parallel", "arbitrary")),
    )(q, k, v, do, lt)


def _mla4_fwd_call(qn, qp, kn, kpt, v):
    s_len = qn.shape[0]
    tk = min(MLA_KEY_TILE, s_len)
    tq = min(ATT_Q_TILES * ATT_TILE, s_len)
    r = tq // tk
    nq = s_len // tq

    def body(qn_ref, qp_ref, kn_ref, kpt_ref, v_ref, o_ref, lse_ref):
        i = pl.program_id(1)
        qn2 = qn_ref[...]
        qp2 = qp_ref[...]
        lane256 = lax.broadcasted_iota(jnp.int32, (1, 256), 1)
        lane128 = lax.broadcasted_iota(jnp.int32, (1, 128), 1)
        krow = lax.broadcasted_iota(jnp.int32, (tk, tk), 0)
        kcol = lax.broadcasted_iota(jnp.int32, (tk, tk), 1)
        row = lax.broadcasted_iota(jnp.int32, (tq, tk), 0)
        col = lax.broadcasted_iota(jnp.int32, (tq, tk), 1)
        valids = [col + u * tk <= row for u in range(r)]
        m64s = [(lane256 // 64) == hh for hh in range(4)]
        half = [(lane128 // 64) == u for u in range(2)]
        m32s = [(lane128 // 32) == hh for hh in range(4)]
        qcs = []
        for hh in range(4):
            qpair = qn2[:, 128 * (hh // 2):128 * (hh // 2) + 128]
            qcs.append(jnp.concatenate([jnp.where(half[hh % 2], qpair, jnp.zeros_like(qpair)),
                                        jnp.where(m32s[hh], qp2, jnp.zeros_like(qp2))], axis=1))

        def by_head(vals):
            return jnp.where(m64s[0], vals[0], jnp.where(m64s[1], vals[1], jnp.where(m64s[2], vals[2], vals[3])))

        def block(j, carry, valid):
            ms, ls, acc = list(carry[0:4]), list(carry[4:8]), carry[8]
            off = pl.multiple_of(j * tk, tk)
            knb = kn_ref[pl.ds(off, tk), :]
            kpb = kpt_ref[pl.ds(off, tk), :]
            vb = v_ref[pl.ds(off, tk), :]
            kcs = [jnp.concatenate([knb[:, 128 * pp:128 * pp + 128], kpb], axis=1) for pp in range(2)]
            ps, alphas = [], []
            for hh in range(4):
                s = _dot_nt(qcs[hh], kcs[hh // 2]) * MLA_SCALE
                if valid is not None:
                    s = jnp.where(valid, s, -1e30)
                mn = jnp.maximum(ms[hh], jnp.max(s, axis=1, keepdims=True))
                p = jnp.exp(s - mn)
                alpha = jnp.exp(ms[hh] - mn)
                ls[hh] = alpha * ls[hh] + jnp.sum(p, axis=1, keepdims=True)
                ms[hh] = mn
                ps.append(p.astype(BF16))
                alphas.append(alpha)
            pvs = []
            for pp in range(2):
                vpair = vb[:, 128 * pp:128 * pp + 128]
                vstack = jnp.concatenate([jnp.where(hf, vpair, jnp.zeros_like(vpair)) for hf in half], axis=0)
                pvs.append(_dot(jnp.concatenate(ps[2 * pp:2 * pp + 2], axis=1), vstack))
            acc = by_head(alphas) * acc + jnp.concatenate(pvs, axis=1)
            return (*ms, *ls, acc)

        neg = jnp.full((tq, 1), -1e30, F32)
        zero = jnp.zeros((tq,1), F32)
        carry = lax.fori_loop(0, i * r, lambda j, cr: block(j, cr, None),
                              (neg,) * 4 + (zero,) * 4 + (jnp.zeros((tq, 256), F32),))
        for u in range(r):
            carry = block(i * r + u, carry, valids[u])
        o_ref[...] = carry[8] / by_head(list(carry[4:8]))
        for hh in range(4):
            lse_ref[0, :, hh:hh + 1] = carry[hh] + jnp.log(carry[4 + hh])

    return pl.pallas_call(
        body, name="mla_fwd", grid=(2, nq),
        out_shape=(_sds((s_len, MLA_WIDTH), F32), _sds((2, s_len, 4), F32)),
        in_specs=[pl.BlockSpec((tq,256), lambda g, i: (i, g)),
                  pl.BlockSpec((tq,128), lambda g, i: (i, g)),
                  pl.BlockSpec((s_len, 256), lambda g, i: (0, g)),
                  pl.BlockSpec((s_len, 128), lambda g, i: (0, 0)),
                  pl.BlockSpec((s_len, 256), lambda g, i: (0, g))],
        out_specs=(pl.BlockSpec((tq,256), lambda g, i: (i, g)),
                   pl.BlockSpec((1, tq, 4), lambda g, i: (g, i, 0))),
        compiler_params=_params(("parallel", "parallel")),
    )(qn, qp, kn, kpt, v)


def _mla4_bwd_call(qn, qp, kn, kpt, v, o, do, lse):
    s_len = qn.shape[0]
    tk = min(MLA_KEY_TILE, s_len)
    tq = min(ATT_Q_TILES * ATT_TILE, s_len)
    r = tq // tk
    nq = s_len // tq

    def body(qn_ref, qp_ref, kn_ref, kpt_ref, v_ref, o_ref, do_ref, lse_ref,
             dqn_ref, dqp_ref, dkn_ref, dkpt_ref, dv_ref):
        g = pl.program_id(0)
        i = pl.program_id(1)

        @pl.when(i == 0)
        def _():
            dkn_ref[...] = jnp.zeros_like(dkn_ref)
            dv_ref[...] = jnp.zeros_like(dv_ref)

        @pl.when((i == 0) & (g == 0))
        def _():
            dkpt_ref[...] = jnp.zeros_like(dkpt_ref)

        qn2 = qn_ref[...]
        qp2 = qp_ref[...]
        dof = do_ref[...]
        dob = dof.astype(BF16)
        prod = dof * o_ref[...]
        lane256 = lax.broadcasted_iota(jnp.int32, (1, 256), 1)
        lane128 = lax.broadcasted_iota(jnp.int32, (1, 128), 1)
        krow = lax.broadcasted_iota(jnp.int32, (tk, tk), 0)
        kcol = lax.broadcasted_iota(jnp.int32, (tk, tk), 1)
        row = lax.broadcasted_iota(jnp.int32, (tq, tk), 0)
        col = lax.broadcasted_iota(jnp.int32, (tq, tk), 1)
        valids = [col + u * tk <= row for u in range(r)]
        m64s = [(lane256 // 64) == hh for hh in range(4)]
        half = [(lane128 // 64) == u for u in range(2)]
        m32s = [(lane128 // 32) == hh for hh in range(4)]
        qcs, doms = [], []
        for hh in range(4):
            sl = slice(128 * (hh // 2), 128 * (hh // 2) + 128)
            qpair = qn2[:, sl]
            dpair = dob[:, sl]
            qcs.append(jnp.concatenate([jnp.where(half[hh % 2], qpair, jnp.zeros_like(qpair)),
                                        jnp.where(m32s[hh], qp2, jnp.zeros_like(qp2))], axis=1))
            doms.append(jnp.where(half[hh % 2], dpair, jnp.zeros_like(dpair)))
        dsums = [jnp.sum(jnp.where(m64, prod, 0.0), axis=1, keepdims=True) * MLA_SCALE for m64 in m64s]
        lses = [lse_ref[0, :, hh:hh + 1] for hh in range(4)]
        qn2t = jnp.transpose(qn2.astype(F32))
        qp2t = jnp.transpose(qp2.astype(F32))
        do2t = jnp.transpose(dof)
        sub128 = lax.broadcasted_iota(jnp.int32, (128, 1), 0)
        qtstacks, dotstacks = [], []
        for pp in range(2):
            qts, dts = [], []
            for u in range(2):
                hh = 2 * pp + u
                qts.append(jnp.concatenate(
                    [jnp.where((sub128 // 64) == u, qn2t[128 * pp:128 * pp + 128, :], 0.0),
                     jnp.where((sub128 // 32) == hh, qp2t, 0.0)], axis=0).astype(BF16))
                dts.append(jnp.where((sub128 // 64) == u, do2t[128 * pp:128 * pp + 128, :], 0.0).astype(BF16))
            qtstacks.append(jnp.concatenate(qts, axis=1))
            dotstacks.append(jnp.concatenate(dts, axis=1))

        def block(j, carry, valid):
            dqn, dqp = carry
            off = pl.multiple_of(j * tk, tk)
            knb = kn_ref[pl.ds(off, tk), :]
            kpb = kpt_ref[pl.ds(off, tk), :]
            vb = v_ref[pl.ds(off, tk), :]
            dqn_parts = []
            dkp = None
            for pp in range(2):
                sl = slice(128 * pp, 128 * pp + 128)
                knp = knb[:, sl]
                vpair = vb[:, sl]
                kc = jnp.concatenate([knp, kpb], axis=1)
                dss, pbs, kcms = [], [], []
                for u in range(2):
                    hh = 2 * pp + u
                    s = _dot_nt(qcs[hh], kc) * MLA_SCALE
                    if valid is not None:
                        s = jnp.where(valid, s, -1e30)
                    p = jnp.exp(s - lses[hh])
                    ds = p * (_dot_nt(doms[hh], vpair) * MLA_SCALE - dsums[hh])
                    dss.append(ds.astype(BF16))
                    pbs.append(p.astype(BF16))
                    kcms.append(jnp.concatenate([jnp.where(half[u], knp, jnp.zeros_like(knp)),
                                                 jnp.where(m32s[hh], kpb, jnp.zeros_like(kpb))], axis=1))
                dqc = _dot(jnp.concatenate(dss, axis=1), jnp.concatenate(kcms, axis=0))
                dqn_parts.append(dqc[:, :128])
                dqp = dqp + dqc[:, 128:]
                dkc = _dot(qtstacks[pp], jnp.concatenate(dss, axis=0))
                dkn_ref[128 * pp:128 * pp + 128, pl.ds(off, tk)] += dkc[:128, :]
                dkp = dkc[128:, :] if dkp is None else dkp + dkc[128:, :]
                dv_ref[128 * pp:128 * pp + 128, pl.ds(off, tk)] += _dot(dotstacks[pp], jnp.concatenate(pbs, axis=0))
            dqn = dqn + jnp.concatenate(dqn_parts, axis=1)
            dkpt_ref[:, pl.ds(off, tk)] += dkp
            return dqn, dqp

        carry = lax.fori_loop(0, i * r, lambda j, cr: block(j, cr, None),
                              (jnp.zeros((tq, 256), F32), jnp.zeros((tq, 128), F32)))
        for u in range(r):
            carry = block(i * r + u, carry, valids[u])
        dqn, dqp = carry
        dqn_ref[...] = dqn.astype(BF16)
        dqp_ref[...] = dqp.astype(BF16)

    return pl.pallas_call(
        body, name="mla_bwd", grid=(2, nq),
        out_shape=(_sds((s_len, 512), BF16), _sds((s_len, 256), BF16), _sds((512, s_len), F32),
                   _sds((128, s_len), F32), _sds((512, s_len), F32)),
        in_specs=[pl.BlockSpec((tq,256), lambda g, i: (i, g)),
                  pl.BlockSpec((tq,128), lambda g, i: (i, g)),
                  pl.BlockSpec((s_len, 256), lambda g, i: (0, g)),
                  pl.BlockSpec((s_len, 128), lambda g, i: (0, 0)),
                  pl.BlockSpec((s_len, 256), lambda g, i: (0, g)),
                  pl.BlockSpec((tq,256), lambda g, i: (i, g)),
                  pl.BlockSpec((tq,256), lambda g, i: (i, g)),
                  pl.BlockSpec((1, tq, 4), lambda g, i: (g, i, 0))],
        out_specs=(pl.BlockSpec((tq,256), lambda g, i: (i, g)),
                   pl.BlockSpec((tq,128), lambda g, i: (i, g)),
                   pl.BlockSpec((256, s_len), lambda g, i: (g, 0)),
                   pl.BlockSpec((128, s_len), lambda g, i: (0, 0)),
                   pl.BlockSpec((256, s_len), lambda g, i: (g, 0))),
        compiler_params=_params(("arbitrary", "arbitrary")),
    )(qn, qp, kn, kpt, v, o, do, lse)


def _post_call(x, tgt, oa, ob, sz, mz, ga, gb, gate, gf, wa, wb, wo, wat, wbt, wot):
    s_len = x.shape[0]
    tm = min(ROW_TILE, s_len)

    def body(x_ref, t_ref, oa_ref, ob_ref, sz_ref, mz_ref, ga_ref, gb_ref, gate_ref, gf_ref,
             wa_ref, wb_ref, wo_ref, wat_ref, wbt_ref, wot_ref,
             dx2_ref, doa_ref, dob_ref, dsz_ref, dmz_ref, dga_ref, dgb_ref,
             dwo_ref, dwa_ref, dwb_ref, dgf_ref, dgate_ref, loss_ref):
        @pl.when(pl.program_id(0) == 0)
        def _():
            dwo_ref[...] = jnp.zeros_like(dwo_ref)
            dwa_ref[...] = jnp.zeros_like(dwa_ref)
            dwb_ref[...] = jnp.zeros_like(dwb_ref)
            dgf_ref[...] = jnp.zeros_like(dgf_ref)
            dgate_ref[...] = jnp.zeros_like(dgate_ref)
            loss_ref[...] = jnp.zeros_like(loss_ref)

        gate = gate_ref[...]
        gf = gf_ref[...]
        oa = oa_ref[...]
        ob = ob_ref[...]
        sz = sz_ref[...]
        mz = mz_ref[...]
        sa = _sigmoid(sz)
        sb = _sigmoid(mz)
        silu_a = sz * sa
        silu_b = mz * sb
        ua = (oa * silu_a).astype(BF16)
        ub = (ob * silu_b).astype(BF16)
        ya = _dot(ua, wa_ref[...])
        yb = _dot(ub, wb_ref[...])
        sga = _sigmoid(ga_ref[...])
        sgb = _sigmoid(gb_ref[...])
        merged = (sga * ya + sgb * yb).astype(BF16)
        out = _dot(merged, wo_ref[...])
        x2 = x_ref[...] + gate * out
        r2 = lax.rsqrt(jnp.mean(x2 * x2, axis=-1, keepdims=True) + EPS)
        xhat = x2 * r2
        err = xhat * gf - t_ref[...]
        loss_ref[...] += 0.5 * jnp.sum(jnp.sum(err * err, axis=1, keepdims=True), axis=0, keepdims=True) / D_MODEL
        dy = err * (1.0 / D_MODEL)
        dgf_ref[...] += jnp.sum(dy * xhat, axis=0, keepdims=True)
        dxhat = dy * gf
        dx2 = r2 * (dxhat - xhat * jnp.mean(dxhat * xhat, axis=-1, keepdims=True))
        dx2_ref[...] = dx2
        dgate_ref[...] += jnp.sum(dx2 * out, axis=0, keepdims=True)
        dout = (dx2 * gate).astype(BF16)
        dmerged = _dot(dout, wot_ref[...])
        dwo_ref[...] += _dot_tn(merged, dout)
        dya = dmerged * sga
        dyb = dmerged * sgb
        dga_ref[...] = (dya * ya * (1.0 - sga)).astype(BF16)
        dgb_ref[...] = (dyb * yb * (1.0 - sgb)).astype(BF16)
        dyab = dya.astype(BF16)
        dybb = dyb.astype(BF16)
        dua = _dot(dyab, wat_ref[...])
        dub = _dot(dybb, wbt_ref[...])
        dwa_ref[...] += _dot_tn(ua, dyab)
        dwb_ref[...] += _dot_tn(ub, dybb)
        doa_ref[...] = dua * silu_a
        dob_ref[...] = dub * silu_b
        dsz_ref[...] = (dua * oa * (sa * (1.0 + sz * (1.0 - sa)))).astype(BF16)
        dmz_ref[...] = (dub * ob * (sb * (1.0 + mz * (1.0 - sb)))).astype(BF16)

    return pl.pallas_call(
        body, name="post", grid=(s_len // tm,),
        out_shape=(_sds((s_len, D_MODEL), F32), _sds((s_len, 512), F32), _sds((s_len, 512), F32),
                   _sds((s_len, 512), BF16), _sds((s_len, 512), BF16),
                   _sds((s_len, D_MODEL), BF16), _sds((s_len, D_MODEL), BF16),
                   _sds((D_MODEL, D_MODEL), F32), _sds((512, D_MODEL), F32), _sds((512, D_MODEL), F32),
                   _sds((1, D_MODEL), F32), _sds((1, D_MODEL), F32), _sds((1, 128), F32)),
        in_specs=[_rows(tm, D_MODEL), _rows(tm, D_MODEL), _rows(tm, 512), _rows(tm, 512), _rows(tm, 512),
                  _rows(tm, 512), _rows(tm, D_MODEL), _rows(tm, D_MODEL), _whole((1, D_MODEL)), _whole((1, D_MODEL)),
                  _whole((512, D_MODEL)), _whole((512, D_MODEL)), _whole((D_MODEL, D_MODEL)),
                  _whole((D_MODEL, 512)), _whole((D_MODEL, 512)), _whole((D_MODEL, D_MODEL))],
        out_specs=(_rows(tm, D_MODEL), _rows(tm, 512), _rows(tm, 512), _rows(tm, 512), _rows(tm, 512),
                   _rows(tm, D_MODEL), _rows(tm, D_MODEL),
                   _whole((D_MODEL, D_MODEL)), _whole((512, D_MODEL)), _whole((512, D_MODEL)),
                   _whole((1, D_MODEL)), _whole((1, D_MODEL)), _whole((1, 128))),
        compiler_params=_params(("arbitrary",)),
    )(x, tgt, oa, ob, sz, mz, ga, gb, gate, gf, wa, wb, wo, wat, wbt, wot)


def _bwdprep_call(dsq, dsk, dsv, dsz, dqn, dqp, dkn, dvv, dkpt, dmz, dga, dgb, cq, ckv, cos256, sin256,
                  qg, kvg, wqt, wkvt):
    s_len = cq.shape[0]
    tm = min(ROW_TILE, s_len)

    def body(dsq_ref, dsk_ref, dsv_ref, dsz_ref, dqn_ref, dqp_ref, dkn_ref, dvv_ref, dkpt_ref, dmz_ref,
             dga_ref, dgb_ref, cq_ref, ckv_ref, cos_ref, sin_ref, qg_ref, kvg_ref, wqt_ref, wkvt_ref,
             dp_ref, dwq_ref, dwkv_ref, dqg_ref, dkvg_ref):
        @pl.when(pl.program_id(0) == 0)
        def _():
            dwq_ref[...] = jnp.zeros_like(dwq_ref)
            dwkv_ref[...] = jnp.zeros_like(dwkv_ref)
            dqg_ref[...] = jnp.zeros_like(dqg_ref)
            dkvg_ref[...] = jnp.zeros_like(dkvg_ref)

        cos = cos_ref[...]
        sin = sin_ref[...]
        dp_ref[:, O_SQ:O_SK] = dsq_ref[...]
        dp_ref[:, O_SK:O_SV] = jnp.transpose(dsk_ref[...]).astype(BF16)
        dp_ref[:, O_SV:O_SZ] = jnp.transpose(dsv_ref[...]).astype(BF16)
        dp_ref[:, O_SZ:O_CQ] = dsz_ref[...]
        dp_ref[:, O_MZ:O_GA] = dmz_ref[...]
        dp_ref[:, O_GA:O_GB] = dga_ref[...]
        dp_ref[:, O_GB:O_KR] = dgb_ref[...]
        dkp = jnp.transpose(dkpt_ref[...])
        dp_ref[:, O_KR:O_KR + 128] = (dkp * cos[:, :128]).astype(BF16)
        dp_ref[:, O_KR + 128:O_END] = (dkp * sin[:, :128]).astype(BF16)
        dp_ref[:, O_END:W_INT] = jnp.zeros((tm, W_INT - O_END), BF16)

        cq = cq_ref[...]
        rq = lax.rsqrt(jnp.mean(cq * cq, axis=-1, keepdims=True) + EPS)
        cqh = cq * rq
        qg = qg_ref[...]
        cqn = (cqh * qg).astype(BF16)
        dqp = dqp_ref[...].astype(F32)
        dqa = jnp.concatenate([dqn_ref[...], (dqp * cos).astype(BF16), (dqp * sin).astype(BF16)], axis=1)
        dcqn = _dot(dqa, wqt_ref[...])
        dwq_ref[...] += _dot_tn(cqn, dqa)
        dqg_ref[...] += jnp.sum(dcqn * cqh, axis=0, keepdims=True)
        dh = dcqn * qg
        dcq = rq * (dh - cqh * jnp.mean(dh * cqh, axis=-1, keepdims=True))
        dp_ref[:, O_CQ:O_CKV] = dcq.astype(BF16)

        ckv = ckv_ref[...]
        rk = lax.rsqrt(jnp.mean(ckv * ckv, axis=-1, keepdims=True) + EPS)
        ckh = ckv * rk
        kvg = kvg_ref[...]
        ckvn = (ckh * kvg).astype(BF16)
        dkva = jnp.concatenate([jnp.transpose(dkn_ref[...]).astype(BF16),
                                jnp.transpose(dvv_ref[...]).astype(BF16)], axis=1)
        dckvn = _dot(dkva, wkvt_ref[...])
        dwkv_ref[...] += _dot_tn(ckvn, dkva)
        dkvg_ref[...] += jnp.sum(dckvn * ckh, axis=0, keepdims=True)
        dh2 = dckvn * kvg
        dckv = rk * (dh2 - ckh * jnp.mean(dh2 * ckh, axis=-1, keepdims=True))
        dp_ref[:, O_CKV:O_MZ] = dckv.astype(BF16)

    return pl.pallas_call(
        body, name="bwdprep", grid=(s_len // tm,),
        out_shape=(_sds((s_len, W_INT), BF16), _sds((Q_RANK, 1024), F32), _sds((KV_RANK, 1024), F32),
                   _sds((1, Q_RANK), F32), _sds((1, KV_RANK), F32)),
        in_specs=[_rows(tm, 512), _cols(512, tm), _cols(512, tm), _rows(tm, 512), _rows(tm, 512), _rows(tm, 256),
                  _cols(512, tm), _cols(512, tm), _cols(128, tm), _rows(tm, 512), _rows(tm, D_MODEL),
                  _rows(tm, D_MODEL), _rows(tm, Q_RANK), _rows(tm, KV_RANK), _rows(tm, 256), _rows(tm, 256),
                  _whole((1, Q_RANK)), _whole((1, KV_RANK)), _whole((1024, Q_RANK)), _whole((1024, KV_RANK))],
        out_specs=(_rows(tm, W_INT), _whole((Q_RANK, 1024)), _whole((KV_RANK, 1024)),
                   _whole((1, Q_RANK)), _whole((1, KV_RANK))),
        compiler_params=_params(("arbitrary",)),
    )(dsq, dsk, dsv, dsz, dqn, dqp, dkn, dvv, dkpt, dmz, dga, dgb, cq, ckv, cos256, sin256, qg, kvg, wqt, wkvt)


def _dh_call(dproj, w_int_t, x, dx2, scale, g1):
    s_len = x.shape[0]
    tm = min(ROW_TILE, s_len)

    def body(dp_ref, wt_ref, x_ref, dx2_ref, sc_ref, g1_ref, gx_ref, dsh_ref, dsc_ref, dg1_ref):
        @pl.when(pl.program_id(0) == 0)
        def _():
            dsh_ref[...] = jnp.zeros_like(dsh_ref)
            dsc_ref[...] = jnp.zeros_like(dsc_ref)
            dg1_ref[...] = jnp.zeros_like(dg1_ref)

        dh = _dot(dp_ref[...], wt_ref[...])
        xt = x_ref[...]
        r = lax.rsqrt(jnp.mean(xt * xt, axis=-1, keepdims=True) + EPS)
        xh = xt * r
        g1 = g1_ref[...]
        xg = xh * g1
        dsh_ref[...] += jnp.sum(dh, axis=0, keepdims=True)
        dsc_ref[...] += jnp.sum(dh * xg, axis=0, keepdims=True)
        dxg = dh * (1.0 + sc_ref[...])
        dg1_ref[...] += jnp.sum(dxg * xh, axis=0, keepdims=True)
        dxh = dxg * g1
        gx_ref[...] = dx2_ref[...] + r * (dxh - xh * jnp.mean(dxh * xh, axis=-1, keepdims=True))

    return pl.pallas_call(
        body, name="dh", grid=(s_len // tm,),
        out_shape=(_sds((s_len, D_MODEL), F32), _sds((1, D_MODEL), F32), _sds((1, D_MODEL), F32),
                   _sds((1, D_MODEL), F32)),
        in_specs=[_rows(tm, W_INT), _whole((W_INT, D_MODEL)), _rows(tm, D_MODEL), _rows(tm, D_MODEL),
                  _whole((1, D_MODEL)), _whole((1, D_MODEL))],
        out_specs=(_rows(tm, D_MODEL), _whole((1, D_MODEL)), _whole((1, D_MODEL)), _whole((1, D_MODEL))),
        compiler_params=_params(("arbitrary",)),
    )(dproj, w_int_t, x, dx2, scale, g1)


def _dwin_call(h, dproj):
    s_len = h.shape[0]
    tm = min(2 * ROW_TILE, s_len)
    nc = 4
    chunk = W_INT // nc

    def body(h_ref, dp_ref, dw_ref):
        @pl.when(pl.program_id(1) == 0)
        def _():
            dw_ref[...] = jnp.zeros_like(dw_ref)

        dw_ref[...] += _dot_tn(h_ref[...], dp_ref[...])

    return pl.pallas_call(
        body, name="dwin", grid=(nc, s_len // tm),
        out_shape=_sds((D_MODEL, nc * chunk), F32),
        in_specs=[pl.BlockSpec((tm, D_MODEL), lambda c, i: (i, 0)),
                  pl.BlockSpec((tm, chunk), lambda c, i: (i, c))],
        out_specs=pl.BlockSpec((D_MODEL, chunk), lambda c, i: (0, c)),
        compiler_params=_params(("parallel", "arbitrary")),
    )(h, dproj)


def _small_call(svg, ct, dmod_sh):
    def body(sv_ref, ct_ref, dm_ref, tot_ref, gwada_ref):
        acc = sv_ref[0:1, :]
        for d in range(1, N_DEV):
            acc = acc + sv_ref[d:d + 1, :]
        tot_ref[...] = acc
        gwada_ref[...] = lax.dot_general(ct_ref[...], dm_ref[...], (((1,), (0,)), ((), ())),
                                         precision=lax.Precision.HIGHEST, preferred_element_type=F32)

    vmem = pl.BlockSpec(memory_space=pltpu.VMEM)
    return pl.pallas_call(
        body, name="small_grads",
        out_shape=(_sds((1, 8 * SV_COLS), F32), _sds((D_MODEL, 768), F32)),
        in_specs=[vmem, vmem, vmem], out_specs=(vmem, vmem),
        compiler_params=_params(),
    )(svg, ct, dmod_sh)


def _adamw_tile_rows(rows, cols):
    budget = 2 << 20
    if rows * cols * 4 <= budget or rows % 8:
        return rows
    best = 8
    for tr in range(8, rows + 1, 8):
        if rows % tr == 0 and tr * cols * 4 <= budget:
            best = tr
    return best


def _adamw_call(name, w, g, m, v):
    rows, cols = w.shape
    tr = _adamw_tile_rows(rows, cols)

    def body(w_ref, g_ref, m_ref, v_ref, d_ref, nm_ref, nv_ref):
        gg = g_ref[...]
        m2 = ADAM_B1 * m_ref[...] + (1.0 - ADAM_B1) * gg
        v2 = ADAM_B2 * v_ref[...] + (1.0 - ADAM_B2) * (gg * gg)
        m_hat = m2 / (1.0 - ADAM_B1 ** ADAM_STEP)
        v_hat = v2 / (1.0 - ADAM_B2 ** ADAM_STEP)
        d_ref[...] = -ADAM_LR * (m_hat / (jnp.sqrt(v_hat) + ADAM_EPS) + ADAM_WD * w_ref[...])
        nm_ref[...] = m2
        nv_ref[...] = v2

    spec = pl.BlockSpec((tr, cols), lambda i: (i, 0))
    return pl.pallas_call(
        body, name="adamw_" + name, grid=(rows // tr,),
        out_shape=(_sds((rows, cols), F32),) * 3,
        in_specs=[spec] * 4, out_specs=(spec,) * 3,
        compiler_params=_params(("parallel",)),
    )(w, g, m, v)


IN_SHARD = IN_WIDTH // N_CHIPS
HALF_D = D_MODEL // 2
SMALL_ROWS = (576, 512, 1024, 1024, 2048)
SMALL_TOTAL = sum(SMALL_ROWS)
SMALL_HALF = SMALL_TOTAL // 2
SMALL_SUM_ROWS = 432


def _gather2_call(c_row, w_ada_sh, pack_in, pack_small):
    def body(c_ref, wada_ref, pki_ref, pks_ref, mg_ref, cg_ref, gwi_ref, gws_ref,
             cv, ssem_c, rsem_c, ssem_m, rsem_m, ssem_w, rsem_w, ssem_f, rsem_f, lsem):
        x, y, c = lax.axis_index("x"), lax.axis_index("y"), lax.axis_index("c")
        me = 4 * x + 2 * y + c
        chip = 2 * x + y
        rel3 = [(1, 0), (0, 1), (1, 1)]
        packs = [(pki_ref, gwi_ref), (pks_ref, gws_ref)]

        sends = []
        for j, (dx, dy) in enumerate(rel3):
            for a, (pk, gw) in enumerate(packs):
                cp = pltpu.make_async_remote_copy(
                    src_ref=pk.at[c], dst_ref=gw.at[chip, c], send_sem=ssem_w.at[j, a], recv_sem=rsem_w.at[j, a],
                    device_id=(_flip(x, dx), _flip(y, dy), c), device_id_type=MESH)
                cp.start()
                sends.append(cp)
        owns = []
        for a, (pk, gw) in enumerate(packs):
            own = pltpu.make_async_copy(pk, gw.at[chip], lsem.at[a])
            own.start()
            owns.append(own)

        cv[me] = c_ref[...]
        for r in range(1, N_DEV):
            dx, dy, dc = (r >> 2) & 1, (r >> 1) & 1, r & 1
            cp = pltpu.make_async_remote_copy(
                src_ref=c_ref, dst_ref=cv.at[me], send_sem=ssem_c.at[r - 1], recv_sem=rsem_c.at[r - 1],
                device_id=(_flip(x, dx), _flip(y, dy), _flip(c, dc)), device_id_type=MESH)
            cp.start()
            sends.append(cp)
        for r in range(1, N_DEV):
            dx, dy, dc = (r >> 2) & 1, (r >> 1) & 1, r & 1
            src = 4 * _flip(x, dx) + 2 * _flip(y, dy) + _flip(c, dc)
            pltpu.make_async_remote_copy(
                src_ref=c_ref, dst_ref=cv.at[src], send_sem=ssem_c.at[r - 1], recv_sem=rsem_c.at[r - 1],
                device_id=(x, y, c), device_id_type=MESH).wait_recv()
        rows = lax.broadcasted_iota(jnp.int32, (N_DEV, D_MODEL), 0)
        call = jnp.zeros((N_DEV, D_MODEL), F32)
        for b in range(N_DEV):
            call = jnp.where(rows == b, jnp.broadcast_to(cv[b], (N_DEV, D_MODEL)), call)
        cg_ref[...] = call

        mg_ref[chip] = lax.dot_general(call, wada_ref[...], (((1,), (0,)), ((), ())),
                                       precision=lax.Precision.HIGHEST, preferred_element_type=F32)
        for j, (dx, dy) in enumerate(rel3):
            cp = pltpu.make_async_remote_copy(
                src_ref=mg_ref.at[chip], dst_ref=mg_ref.at[chip], send_sem=ssem_m.at[j], recv_sem=rsem_m.at[j],
                device_id=(_flip(x, dx), _flip(y, dy), c), device_id_type=MESH)
            cp.start()
            sends.append(cp)
        for j, (dx, dy) in enumerate(rel3):
            src_chip = 2 * _flip(x, dx) + _flip(y, dy)
            pltpu.make_async_remote_copy(
                src_ref=mg_ref.at[src_chip], dst_ref=mg_ref.at[src_chip], send_sem=ssem_m.at[j],
                recv_sem=rsem_m.at[j], device_id=(x, y, c), device_id_type=MESH).wait_recv()
        for j, (dx, dy) in enumerate(rel3):
            src_chip = 2 * _flip(x, dx) + _flip(y, dy)
            for a, (pk, gw) in enumerate(packs):
                pltpu.make_async_remote_copy(
                    src_ref=pk.at[c], dst_ref=gw.at[src_chip, c], send_sem=ssem_w.at[j, a],
                    recv_sem=rsem_w.at[j, a], device_id=(x, y, c), device_id_type=MESH).wait_recv()
                cp = pltpu.make_async_remote_copy(
                    src_ref=gw.at[src_chip, c], dst_ref=gw.at[src_chip, c], send_sem=ssem_f.at[j, a],
                    recv_sem=rsem_f.at[j, a], device_id=(x, y, 1 - c), device_id_type=MESH)
                cp.start()
                sends.append(cp)
        for j, (dx, dy) in enumerate(rel3):
            src_chip = 2 * _flip(x, dx) + _flip(y, dy)
            for a, (pk, gw) in enumerate(packs):
                pltpu.make_async_remote_copy(
                    src_ref=pk.at[c], dst_ref=gw.at[src_chip, 1 - c], send_sem=ssem_f.at[j, a],
                    recv_sem=rsem_f.at[j, a], device_id=(x, y, c), device_id_type=MESH).wait_recv()
        for cp in sends:
            cp.wait_send()
        for own in owns:
            own.wait()

    vmem = pl.BlockSpec(memory_space=pltpu.VMEM)
    return pl.pallas_call(
        body, name="gather_fwd",
        out_shape=(_sds((N_CHIPS, N_DEV, 768), F32), _sds((N_DEV, D_MODEL), F32),
                   _sds((N_CHIPS, 2, IN_SHARD, HALF_D), BF16), _sds((N_CHIPS, 2, SMALL_HALF, LANES), BF16)),
        in_specs=[vmem, vmem, vmem, vmem], out_specs=(vmem, vmem, vmem, vmem),
        scratch_shapes=[
            pltpu.VMEM((N_DEV, 1, D_MODEL), F32),
            pltpu.SemaphoreType.DMA((N_DEV - 1,)), pltpu.SemaphoreType.DMA((N_DEV - 1,)),
            pltpu.SemaphoreType.DMA((3,)), pltpu.SemaphoreType.DMA((3,)),
            pltpu.SemaphoreType.DMA((3, 2)), pltpu.SemaphoreType.DMA((3, 2)),
            pltpu.SemaphoreType.DMA((3, 2)), pltpu.SemaphoreType.DMA((3, 2)),
            pltpu.SemaphoreType.DMA((2,)),
        ],
        compiler_params=_params(),
    )(c_row, w_ada_sh, pack_in, pack_small)


def _reduce2_call(g_in, g_small, sv):
    def body(gi_ref, gs_ref, sv_ref, fi_ref, fs_ref, svg_ref, land_i, land_s,
             ssem_g, rsem_g, ssem_s, rsem_s, ssem_x, rsem_x, lsem):
        x, y, c = lax.axis_index("x"), lax.axis_index("y"), lax.axis_index("c")
        me = 4 * x + 2 * y + c
        pieces = [(gi_ref, land_i), (gs_ref, land_s)]
        copies = []
        for r in range(1, N_DEV):
            dx, dy, dc = (r >> 2) & 1, (r >> 1) & 1, r & 1
            tx, ty, tc = _flip(x, dx), _flip(y, dy), _flip(c, dc)
            tgt = 4 * tx + 2 * ty + tc
            for a, (g, land) in enumerate(pieces):
                cp = pltpu.make_async_remote_copy(
                    src_ref=g.at[tgt], dst_ref=land.at[me], send_sem=ssem_g.at[r - 1, a],
                    recv_sem=rsem_g.at[r - 1, a], device_id=(tx, ty, tc), device_id_type=MESH)
                cp.start()
                copies.append(cp)
            cp = pltpu.make_async_remote_copy(
                src_ref=sv_ref, dst_ref=svg_ref.at[me], send_sem=ssem_s.at[r - 1],
                recv_sem=rsem_s.at[r - 1], device_id=(tx, ty, tc), device_id_type=MESH)
            cp.start()
            copies.append(cp)
        owns = []
        for a, (g, land) in enumerate(pieces):
            own = pltpu.make_async_copy(g.at[me], land.at[me], lsem.at[a])
            own.start()
            owns.append(own)
        svg_ref[me] = sv_ref[...]
        for r in range(1, N_DEV):
            dx, dy, dc = (r >> 2) & 1, (r >> 1) & 1, r & 1
            src = 4 * _flip(x, dx) + 2 * _flip(y, dy) + _flip(c, dc)
            for a, (g, land) in enumerate(pieces):
                pltpu.make_async_remote_copy(
                    src_ref=g.at[src], dst_ref=land.at[src], send_sem=ssem_g.at[r - 1, a],
                    recv_sem=rsem_g.at[r - 1, a], device_id=(x, y, c), device_id_type=MESH).wait_recv()
            pltpu.make_async_remote_copy(
                src_ref=sv_ref, dst_ref=svg_ref.at[src], send_sem=ssem_s.at[r - 1],
                recv_sem=rsem_s.at[r - 1], device_id=(x, y, c), device_id_type=MESH).wait_recv()
        for own in owns:
            own.wait()

        for qd in range(HALF_D // LANES):
            sl = slice(LANES * qd, LANES * qd + LANES)
            acc = land_i[0, :, sl].astype(F32)
            for d in range(1, N_DEV):
                acc = acc + land_i[d, :, sl].astype(F32)
            fi_ref[c, :, sl] = acc

        def sum_rows(i, carry):
            sl = pl.ds(pl.multiple_of(i * SMALL_SUM_ROWS, 16), SMALL_SUM_ROWS)
            acc = land_s[0, sl, :].astype(F32)
            for d in range(1, N_DEV):
                acc = acc + land_s[d, sl, :].astype(F32)
            fs_ref[c, sl, :] = acc
            return carry

        lax.fori_loop(0, SMALL_HALF // SMALL_SUM_ROWS, sum_rows, 0)
        swaps = []
        for a, f in enumerate((fi_ref, fs_ref)):
            cp = pltpu.make_async_remote_copy(
                src_ref=f.at[c], dst_ref=f.at[c], send_sem=ssem_x.at[a], recv_sem=rsem_x.at[a],
                device_id=(x, y, 1 - c), device_id_type=MESH)
            cp.start()
            swaps.append(cp)
        for a, f in enumerate((fi_ref, fs_ref)):
            pltpu.make_async_remote_copy(
                src_ref=f.at[c], dst_ref=f.at[1 - c], send_sem=ssem_x.at[a], recv_sem=rsem_x.at[a],
                device_id=(x, y, c), device_id_type=MESH).wait_recv()
        for cp in swaps + copies:
            cp.wait_send()

    vmem = pl.BlockSpec(memory_space=pltpu.VMEM)
    return pl.pallas_call(
        body, name="grad_reduce",
        out_shape=(_sds((2, IN_SHARD, HALF_D), F32), _sds((2, SMALL_HALF, LANES), F32),
                   _sds((N_DEV, 8, SV_COLS), F32)),
        in_specs=[vmem, vmem, vmem], out_specs=(vmem, vmem, vmem),
        scratch_shapes=[
            pltpu.VMEM((N_DEV, IN_SHARD, HALF_D), BF16), pltpu.VMEM((N_DEV, SMALL_HALF, LANES), BF16),
            pltpu.SemaphoreType.DMA((N_DEV - 1, 2)), pltpu.SemaphoreType.DMA((N_DEV - 1, 2)),
            pltpu.SemaphoreType.DMA((N_DEV - 1,)), pltpu.SemaphoreType.DMA((N_DEV - 1,)),
            pltpu.SemaphoreType.DMA((2,)), pltpu.SemaphoreType.DMA((2,)), pltpu.SemaphoreType.DMA((2,)),
        ],
        compiler_params=_params(),
    )(g_in, g_small, sv)


def _reduce3_call(g_in, g_small, sv):
    def body(gi_ref, gs_ref, sv_ref, fi_ref, fs_ref, svg_ref, pair_i, pair_s, send_i, send_s, land_i, land_s,
             ssem_p, rsem_p, ssem_g, rsem_g, ssem_s, rsem_s, ssem_x, rsem_x):
        x, y, c = lax.axis_index("x"), lax.axis_index("y"), lax.axis_index("c")
        me = 4 * x + 2 * y + c
        chip = 2 * x + y
        rel3 = [(1, 0), (0, 1), (1, 1)]
        payloads = [(gi_ref, pair_i, send_i, land_i, fi_ref), (gs_ref, pair_s, send_s, land_s, fs_ref)]
        copies = []

        for k in range(N_CHIPS):
            for a, (g, pair, _, _, _) in enumerate(payloads):
                cp = pltpu.make_async_remote_copy(
                    src_ref=g.at[2 * k + 1 - c], dst_ref=pair.at[k], send_sem=ssem_p.at[k, a],
                    recv_sem=rsem_p.at[k, a], device_id=(x, y, 1 - c), device_id_type=MESH)
                cp.start()
                copies.append(cp)

        for r in range(1, N_DEV):
            dx, dy, dc = (r >> 2) & 1, (r >> 1) & 1, r & 1
            cp = pltpu.make_async_remote_copy(
                src_ref=sv_ref, dst_ref=svg_ref.at[me], send_sem=ssem_s.at[r - 1], recv_sem=rsem_s.at[r - 1],
                device_id=(_flip(x, dx), _flip(y, dy), _flip(c, dc)), device_id_type=MESH)
            cp.start()
            copies.append(cp)
        svg_ref[me] = sv_ref[...]

        for k in range(N_CHIPS):
            for a, (g, pair, _, _, _) in enumerate(payloads):
                pltpu.make_async_remote_copy(
                    src_ref=g.at[2 * k + c], dst_ref=pair.at[k], send_sem=ssem_p.at[k, a],
                    recv_sem=rsem_p.at[k, a], device_id=(x, y, c), device_id_type=MESH).wait_recv()

        def pair_sum(k, store_in, store_small):
            for qd in range(HALF_D // LANES):
                sl = slice(LANES * qd, LANES * qd + LANES)
                store_in(sl, gi_ref[2 * k + c, :, sl].astype(F32) + pair_i[k, :, sl].astype(F32))

            def rows(i, carry):
                sl = pl.ds(pl.multiple_of(i * SMALL_SUM_ROWS, 16), SMALL_SUM_ROWS)
                store_small(sl, gs_ref[2 * k + c, sl, :].astype(F32) + pair_s[k, sl, :].astype(F32))
                return carry

            lax.fori_loop(0, SMALL_HALF // SMALL_SUM_ROWS, rows, 0)

        for j, (dx, dy) in enumerate(rel3):
            tx, ty = _flip(x, dx), _flip(y, dy)

            def put_in(sl, val, j=j):
                send_i[j, :, sl] = val.astype(BF16)

            def put_small(sl, val, j=j):
                send_s[j, sl, :] = val.astype(BF16)

            pair_sum(2 * tx + ty, put_in, put_small)
            for a, (_, _, send, land, _) in enumerate(payloads):
                cp = pltpu.make_async_remote_copy(
                    src_ref=send.at[j], dst_ref=land.at[j], send_sem=ssem_g.at[j, a], recv_sem=rsem_g.at[j, a],
                    device_id=(tx, ty, c), device_id_type=MESH)
                cp.start()
                copies.append(cp)

        def own_in(sl, val):
            fi_ref[c, :, sl] = val

        def own_small(sl, val):
            fs_ref[c, sl, :] = val

        pair_sum(chip, own_in, own_small)
        for j in range(3):
            for a, (_, _, send, land, _) in enumerate(payloads):
                pltpu.make_async_remote_copy(
                    src_ref=send.at[j], dst_ref=land.at[j], send_sem=ssem_g.at[j, a], recv_sem=rsem_g.at[j, a],
                    device_id=(x, y, c), device_id_type=MESH).wait_recv()
            for qd in range(HALF_D // LANES):
                sl = slice(LANES * qd, LANES * qd + LANES)
                fi_ref[c, :, sl] += land_i[j, :, sl].astype(F32)

            def add_rows(i, carry, j=j):
                sl = pl.ds(pl.multiple_of(i * SMALL_SUM_ROWS, 16), SMALL_SUM_ROWS)
                fs_ref[c, sl, :] += land_s[j, sl, :].astype(F32)
                return carry

            lax.fori_loop(0, SMALL_HALF // SMALL_SUM_ROWS, add_rows, 0)

        for a, f in enumerate((fi_ref, fs_ref)):
            cp = pltpu.make_async_remote_copy(
                src_ref=f.at[c], dst_ref=f.at[c], send_sem=ssem_x.at[a], recv_sem=rsem_x.at[a],
                device_id=(x, y, 1 - c), device_id_type=MESH)
            cp.start()
            copies.append(cp)
        for a, f in enumerate((fi_ref, fs_ref)):
            pltpu.make_async_remote_copy(
                src_ref=f.at[c], dst_ref=f.at[1 - c], send_sem=ssem_x.at[a], recv_sem=rsem_x.at[a],
                device_id=(x, y, c), device_id_type=MESH).wait_recv()
        for r in range(1, N_DEV):
            dx, dy, dc = (r >> 2) & 1, (r >> 1) & 1, r & 1
            src = 4 * _flip(x, dx) + 2 * _flip(y, dy) + _flip(c, dc)
            pltpu.make_async_remote_copy(
                src_ref=sv_ref, dst_ref=svg_ref.at[src], send_sem=ssem_s.at[r - 1],
                recv_sem=rsem_s.at[r - 1], device_id=(x, y, c), device_id_type=MESH).wait_recv()
        for cp in copies:
            cp.wait_send()

    vmem = pl.BlockSpec(memory_space=pltpu.VMEM)
    return pl.pallas_call(
        body, name="grad_reduce",
        out_shape=(_sds((2, IN_SHARD, HALF_D), F32), _sds((2, SMALL_HALF, LANES), F32),
                   _sds((N_DEV, 8, SV_COLS), F32)),
        in_specs=[vmem, vmem, vmem], out_specs=(vmem, vmem, vmem),
        scratch_shapes=[
            pltpu.VMEM((N_CHIPS, IN_SHARD, HALF_D), BF16), pltpu.VMEM((N_CHIPS, SMALL_HALF, LANES), BF16),
            pltpu.VMEM((3, IN_SHARD, HALF_D), BF16), pltpu.VMEM((3, SMALL_HALF, LANES), BF16),
            pltpu.VMEM((3, IN_SHARD, HALF_D), BF16), pltpu.VMEM((3, SMALL_HALF, LANES), BF16),
            pltpu.SemaphoreType.DMA((N_CHIPS, 2)), pltpu.SemaphoreType.DMA((N_CHIPS, 2)),
            pltpu.SemaphoreType.DMA((3, 2)), pltpu.SemaphoreType.DMA((3, 2)),
            pltpu.SemaphoreType.DMA((N_DEV - 1,)), pltpu.SemaphoreType.DMA((N_DEV - 1,)),
            pltpu.SemaphoreType.DMA((2,)), pltpu.SemaphoreType.DMA((2,)),
        ],
        compiler_params=_params(),
    )(g_in, g_small, sv)


def _dwin_t_call(h, dproj):
    s_len = h.shape[0]
    tm = min(4 * ROW_TILE, s_len)
    nrow = s_len // tm
    nc = 4
    chunk = W_INT // nc

    def body(h_ref, dp_ref, dw_ref, acc):
        i = pl.program_id(1)

        @pl.when(i == 0)
        def _():
            acc[...] = jnp.zeros_like(acc)

        acc[...] += _dot_tn(dp_ref[...], h_ref[...])

        @pl.when(i == nrow - 1)
        def _():
            dw_ref[...] = acc[...].astype(BF16)

    return pl.pallas_call(
        body, name="dwin", grid=(nc, nrow),
        out_shape=_sds((W_INT, D_MODEL), BF16),
        in_specs=[pl.BlockSpec((tm, D_MODEL), lambda c, i: (i, 0)),
                  pl.BlockSpec((tm, chunk), lambda c, i: (i, c))],
        out_specs=pl.BlockSpec((chunk, D_MODEL), lambda c, i: (c, 0)),
        scratch_shapes=[pltpu.VMEM((chunk, D_MODEL), F32)],
        compiler_params=_params(("parallel", "arbitrary")),
    )(h, dproj)


def _swap_rows(w, group):
    r, n = w.shape
    return w.reshape(r // group, 2, group // 2, n)[:, ::-1].reshape(r, n)


def _internal_weights_t(w_in_t, w_uq, w_ukv):
    krot_t = w_in_t[2688:2720]
    w_int_t = jnp.concatenate([
        w_in_t[0:512] * jnp.asarray(0.125, w_in_t.dtype), w_in_t[512:2048],
        w_in_t[2048:2432], w_in_t[2432:2688], w_in_t[2720:3232], w_in_t[3232:4256], w_in_t[4256:5280],
        jnp.tile(krot_t, (4, 1)), jnp.tile(_swap_rows(krot_t, 32), (4, 1)),
        jnp.zeros((W_INT - O_END, D_MODEL), w_in_t.dtype)], axis=0)
    uq = w_uq.reshape(Q_RANK, N_HEADS, 96)
    wp = uq[:, :, 64:].reshape(Q_RANK, 256)
    w_q = jnp.concatenate([uq[:, :, :64].reshape(Q_RANK, 512), wp, _swap_halves(wp, 32)], axis=1)
    ukv = w_ukv.reshape(KV_RANK, N_HEADS, 128)
    w_kv = jnp.concatenate([ukv[:, :, :64].reshape(KV_RANK, 512), ukv[:, :, 64:].reshape(KV_RANK, 512)], axis=1)
    return w_int_t, w_q, w_kv


def _true_weight_grads_t(dwi_t, dwq, dwkv):
    dkr = dwi_t[O_KR:O_KR + 128].astype(F32).reshape(4, 32, D_MODEL).sum(axis=0)
    dkr_sw = dwi_t[O_KR + 128:O_END].astype(F32).reshape(4, 32, D_MODEL).sum(axis=0)
    dkrot_t = (dkr + _swap_rows(dkr_sw, 32)).astype(dwi_t.dtype)
    g_in_t = jnp.concatenate([
        dwi_t[0:512] * jnp.asarray(0.125, dwi_t.dtype), dwi_t[512:2048], dwi_t[O_CQ:O_CKV], dwi_t[O_CKV:O_MZ],
        dkrot_t, dwi_t[O_MZ:O_GA], dwi_t[O_GA:O_GB], dwi_t[O_GB:O_KR]], axis=0)
    dwp = dwq[:, 512:768] + _swap_halves(dwq[:, 768:1024], 32)
    g_uq = jnp.concatenate([dwq[:, :512].reshape(Q_RANK, N_HEADS, 64), dwp.reshape(Q_RANK, N_HEADS, 32)],
                           axis=2).reshape(Q_RANK, 768)
    g_ukv = jnp.concatenate([dwkv[:, :512].reshape(KV_RANK, N_HEADS, 64), dwkv[:, 512:].reshape(KV_RANK, N_HEADS, 64)],
                            axis=2).reshape(KV_RANK, 1024)
    return g_in_t, g_uq, g_ukv


def _swap_halves(w, group):
    r, n = w.shape
    return w.reshape(r, n // group, 2, group // 2)[:, :, ::-1, :].reshape(r, n)


def _pack_shards(parts):
    return jnp.concatenate([p.reshape(-1, LANES) for p in parts], axis=0)


def _unpack_chip_major(gw):
    offs = [0]
    for r in PACK_ROWS:
        offs.append(offs[-1] + r)

    def cols(i, rows, shard_cols):
        blk = gw[:, offs[i]:offs[i + 1]].reshape(N_CHIPS, rows, shard_cols)
        return blk.transpose(1, 0, 2).reshape(rows, N_CHIPS * shard_cols)

    w_in = cols(0, D_MODEL, 1320)
    w_uq = cols(1, Q_RANK, 192)
    w_ukv = cols(2, KV_RANK, 256)
    w_a = cols(3, 512, 256)
    w_b = cols(4, 512, 256)
    w_out = gw[:, offs[5]:offs[6]].reshape(D_MODEL, D_MODEL)
    return w_in, w_uq, w_ukv, w_a, w_b, w_out


def _unpack_small(gw):
    offs = [0]
    for r in SMALL_ROWS:
        offs.append(offs[-1] + r)

    def cols(i, rows, shard_cols):
        blk = gw[:, offs[i]:offs[i + 1]].reshape(N_CHIPS, rows, shard_cols)
        return blk.transpose(1, 0, 2).reshape(rows, N_CHIPS * shard_cols)

    return (cols(0, Q_RANK, 192), cols(1, KV_RANK, 256), cols(2, 512, 256), cols(3, 512, 256),
            gw[:, offs[4]:offs[5]].reshape(D_MODEL, D_MODEL))


def _internal_weights(w_in, w_uq, w_ukv):
    krot = w_in[:, 2688:2720]
    w_int = jnp.concatenate([
        w_in[:, 0:512] * jnp.asarray(0.125, w_in.dtype), w_in[:, 512:2048],
        w_in[:, 2048:2432], w_in[:, 2432:2688], w_in[:, 2720:3232], w_in[:, 3232:4256], w_in[:, 4256:5280],
        jnp.tile(krot, (1, 4)), jnp.tile(_swap_halves(krot, 32), (1, 4)),
        jnp.zeros((D_MODEL, W_INT - O_END), w_in.dtype)], axis=1)
    uq = w_uq.reshape(Q_RANK, N_HEADS, 96)
    wp = uq[:, :, 64:].reshape(Q_RANK, 256)
    w_q = jnp.concatenate([uq[:, :, :64].reshape(Q_RANK, 512), wp, _swap_halves(wp, 32)], axis=1)
    ukv = w_ukv.reshape(KV_RANK, N_HEADS, 128)
    w_kv = jnp.concatenate([ukv[:, :, :64].reshape(KV_RANK, 512), ukv[:, :, 64:].reshape(KV_RANK, 512)], axis=1)
    return w_int, w_q, w_kv


def _true_weight_grads(dwi, dwq, dwkv):
    dkr = dwi[:, O_KR:O_KR + 128].reshape(D_MODEL, 4, 32).sum(axis=1)
    dkr_sw = dwi[:, O_KR + 128:O_END].reshape(D_MODEL, 4, 32).sum(axis=1)
    dkrot = dkr + _swap_halves(dkr_sw, 32)
    g_in = jnp.concatenate([
        dwi[:, 0:512] * 0.125, dwi[:, 512:2048], dwi[:, O_CQ:O_CKV], dwi[:, O_CKV:O_MZ], dkrot,
        dwi[:, O_MZ:O_GA], dwi[:, O_GA:O_GB], dwi[:, O_GB:O_KR]], axis=1)
    dwp = dwq[:, 512:768] + _swap_halves(dwq[:, 768:1024], 32)
    g_uq = jnp.concatenate([dwq[:, :512].reshape(Q_RANK, N_HEADS, 64), dwp.reshape(Q_RANK, N_HEADS, 32)],
                           axis=2).reshape(Q_RANK, 768)
    g_ukv = jnp.concatenate([dwkv[:, :512].reshape(KV_RANK, N_HEADS, 64), dwkv[:, 512:].reshape(KV_RANK, N_HEADS, 64)],
                            axis=2).reshape(KV_RANK, 1024)
    return g_in, g_uq, g_ukv


def _chip_major(g, shard_cols):
    r = g.shape[0]
    return g.reshape(r, N_CHIPS, shard_cols).transpose(1, 0, 2).reshape(N_CHIPS, -1, LANES)


def kernel(x, c, positions, w_ada, b_ada, norm_gain, w_in, q_norm_gain, w_uq, kv_norm_gain, w_ukv, w_branch_a, w_branch_b, w_out, final_norm_gain, loss_target, m_w_ada, m_b_ada, m_norm_gain, m_w_in, m_q_norm_gain, m_w_uq, m_kv_norm_gain, m_w_ukv, m_w_branch_a, m_w_branch_b, m_w_out, m_final_norm_gain, v_w_ada, v_b_ada, v_norm_gain, v_w_in, v_q_norm_gain, v_w_uq, v_kv_norm_gain, v_w_ukv, v_w_branch_a, v_w_branch_b, v_w_out, v_final_norm_gain):
    ix, iy, ic = lax.axis_index("x"), lax.axis_index("y"), lax.axis_index("c")
    me = 4 * ix + 2 * iy + ic
    chip = 2 * ix + iy
    xs = x[0]
    tgt = loss_target[0]
    s_len = xs.shape[0]

    w_in_t = jnp.swapaxes(w_in[0], 0, 1)
    w_in_tb = w_in_t.astype(BF16)
    pack_in = jnp.stack([w_in_tb[:, :HALF_D], w_in_tb[:, HALF_D:]], axis=0)
    small_shards = (w_uq[0], w_ukv[0], w_branch_a[0], w_branch_b[0], w_out[0])
    pack_small = _pack_shards([s.astype(BF16) for s in small_shards]).reshape(2, SMALL_HALF, LANES)
    mg, call, gw_in, gw_small = _gather2_call(c, w_ada[0], pack_in, pack_small)
    mod = mg.transpose(1, 0, 2).reshape(N_DEV, 3 * D_MODEL) + b_ada
    mod_me = lax.dynamic_slice_in_dim(mod, me, 1, axis=0)
    shift, scale, gate = mod_me[:, :D_MODEL], mod_me[:, D_MODEL:2 * D_MODEL], mod_me[:, 2 * D_MODEL:]

    f_in_t = jnp.concatenate([gw_in[:, 0], gw_in[:, 1]], axis=2).reshape(IN_WIDTH, D_MODEL)
    f_uq, f_ukv, f_a, f_b, f_out = _unpack_small(gw_small.reshape(N_CHIPS, SMALL_TOTAL, LANES))
    w_int_t, w_q, w_kv = _internal_weights_t(f_in_t, f_uq, f_ukv)

    inv_freq = ROPE_BASE ** (-jnp.arange(0, ROPE_DIM, 2, dtype=F32) / ROPE_DIM)
    ang = positions[0].astype(F32)[:, None] * inv_freq
    cs, sn = jnp.cos(ang), jnp.sin(ang)
    cos256 = jnp.tile(jnp.concatenate([cs, cs], axis=1), (1, 8))
    sin256 = jnp.tile(jnp.concatenate([-sn, sn], axis=1), (1, 8))

    (h, sq, sk, sv, sz, cq, ckv, mz, ga, gb, kpt, qn, qp, kn, vv) = _inproj_call(
        xs, shift, scale, norm_gain, w_int_t, w_q, w_kv, q_norm_gain, kv_norm_gain, cos256, sin256)
    oa, lt = _sb4_fwd_call(sq, sk, sv)
    ob, lse = _mla4_fwd_call(qn, qp, kn, kpt, vv)

    gf = final_norm_gain.reshape(1, D_MODEL)
    (dx2, doa, dob, dsz, dmz, dga, dgb, dwo, dwa, dwb, dgf, dgate, loss_p) = _post_call(
        xs, tgt, oa, ob, sz, mz, ga, gb, gate, gf, f_a, f_b, f_out, f_a.T, f_b.T, f_out.T)

    dsq, dsk_t, dsv_t = _sb4_bwd_call(sq, sk, sv, doa, lt)
    dqn, dqp, dkn_t, dkpt_t, dvv_t = _mla4_bwd_call(qn, qp, kn, kpt, vv, ob, dob, lse)

    dproj, dwq, dwkv, dqg, dkvg = _bwdprep_call(
        dsq, dsk_t, dsv_t, dsz, dqn, dqp, dkn_t, dvv_t, dkpt_t, dmz, dga, dgb, cq, ckv, cos256, sin256,
        q_norm_gain, kv_norm_gain, w_q.T, w_kv.T)
    grad_x, dshift, dscale, dg1 = _dh_call(dproj, w_int_t, xs, dx2, scale, norm_gain)
    dwi_t = _dwin_t_call(h, dproj)
    g_in_t, g_uq, g_ukv = _true_weight_grads_t(dwi_t, dwq, dwkv)

    g_in_c = g_in_t.reshape(N_CHIPS, IN_SHARD, D_MODEL)
    g_in_pieces = jnp.stack([g_in_c[:, :, :HALF_D], g_in_c[:, :, HALF_D:]], axis=1).reshape(N_DEV, IN_SHARD, HALF_D)
    g_small = jnp.concatenate([
        _chip_major(g_uq, 192), _chip_major(g_ukv, 256), _chip_major(dwa, 256), _chip_major(dwb, 256),
        dwo.reshape(N_CHIPS, -1, LANES)], axis=1).astype(BF16).reshape(N_DEV, SMALL_HALF, LANES)
    small = jnp.concatenate([
        dshift, dscale, dgate, dg1, dqg, dkvg, dgf, loss_p,
        jnp.zeros((1, 8 * SV_COLS - 5888), F32)], axis=1).reshape(8, SV_COLS)
    full_in, full_small, svg = _reduce3_call(g_in_pieces, g_small, small)
    gs_in_t = jnp.concatenate([full_in[0], full_in[1]], axis=1)
    full = full_small.reshape(SMALL_TOTAL, LANES)
    offs = [0]
    for r in SMALL_ROWS:
        offs.append(offs[-1] + r)
    gs_uq = full[offs[0]:offs[1]].reshape(Q_RANK, 192)
    gs_ukv = full[offs[1]:offs[2]].reshape(KV_RANK, 256)
    gs_a = full[offs[2]:offs[3]].reshape(512, 256)
    gs_b = full[offs[3]:offs[4]].reshape(512, 256)
    gs_out = full[offs[4]:offs[5]].reshape(256, D_MODEL)

    svm = svg.reshape(N_DEV, 8 * SV_COLS)
    dmod_sh = lax.dynamic_slice_in_dim(svm[:, :3 * D_MODEL], chip * 768, 768, axis=1)
    tot, gs_ada = _small_call(svm, call.T, dmod_sh)
    g_bada = tot[:, 0:3072]
    g_g1 = tot[:, 3072:4096]
    g_qg = tot[:, 4096:4480]
    g_kvg = tot[:, 4480:4736]
    g_gf = tot[:, 4736:5760]
    loss = tot[0, 5760]

    names = ["w_ada", "b_ada", "norm_gain", "w_in", "q_norm_gain", "w_uq", "kv_norm_gain", "w_ukv",
             "w_branch_a", "w_branch_b", "w_out", "final_norm_gain"]
    ws = [w_ada[0], b_ada, norm_gain, w_in_t, q_norm_gain, w_uq[0], kv_norm_gain, w_ukv[0],
          w_branch_a[0], w_branch_b[0], w_out[0], final_norm_gain.reshape(1, D_MODEL)]
    gs = [gs_ada, g_bada, g_g1, gs_in_t, g_qg, gs_uq, g_kvg, gs_ukv, gs_a, gs_b, gs_out, g_gf]
    ms = [m_w_ada[0], m_b_ada, m_norm_gain, jnp.swapaxes(m_w_in[0], 0, 1), m_q_norm_gain, m_w_uq[0],
          m_kv_norm_gain, m_w_ukv[0], m_w_branch_a[0], m_w_branch_b[0], m_w_out[0],
          m_final_norm_gain.reshape(1, D_MODEL)]
    vs = [v_w_ada[0], v_b_ada, v_norm_gain, jnp.swapaxes(v_w_in[0], 0, 1), v_q_norm_gain, v_w_uq[0],
          v_kv_norm_gain, v_w_ukv[0], v_w_branch_a[0], v_w_branch_b[0], v_w_out[0],
          v_final_norm_gain.reshape(1, D_MODEL)]
    refs = [w_ada, b_ada, norm_gain, w_in, q_norm_gain, w_uq, kv_norm_gain, w_ukv,
            w_branch_a, w_branch_b, w_out, final_norm_gain]
    grads, deltas, new_ms, new_vs = [], [], [], []
    for n, w_, g_, m_, v_, ref in zip(names, ws, gs, ms, vs, refs):
        outs = (g_,) + _adamw_call(n, w_, g_, m_, v_)
        if n == "w_in":
            outs = tuple(jnp.swapaxes(o_, 0, 1) for o_ in outs)
        for lst, o_ in zip((grads, deltas, new_ms, new_vs), outs):
            lst.append(o_.reshape(ref.shape))

    return (loss, grad_x.reshape(x.shape), *grads, *deltas, *new_ms, *new_vs)
```

```python
import math

import jax
import jax.numpy as jnp
from jax import lax
from jax.experimental import pallas as pl
from jax.experimental.pallas import tpu as pltpu

F32 = jnp.float32
BF16 = jnp.bfloat16

D_MODEL = 1024
SB_WIDTH = 512
MLA_WIDTH = 512
Q_RANK = 384
KV_RANK = 256
ROPE_DIM = 32
N_HEADS = 8
IN_WIDTH = 5280
EPS = 1e-6
ROPE_BASE = 10000.0
MLA_SCALE = 1.0 / math.sqrt(96.0)

ADAM_LR = 0.001
ADAM_B1 = 0.9
ADAM_B2 = 0.999
ADAM_EPS = 1e-08
ADAM_WD = 0.01
ADAM_STEP = 10

O_SQ, O_SK, O_SV, O_SZ, O_CQ, O_CKV, O_MZ, O_GA, O_GB, O_KR, O_END = (
    0, 512, 1024, 1536, 2048, 2432, 2688, 3200, 4224, 5248, 5504)
W_INT = 5632

N_CHIPS = 4
N_DEV = 8
LANES = 128
SV_COLS = 768

ROW_TILE = 256
ATT_TILE = 256
ATT_Q_TILES = 2
MLA_KEY_TILE = 512
VMEM_LIMIT = 56 * 1024 * 1024

MESH = pl.DeviceIdType.MESH


def _dot(a, b):
    return lax.dot_general(a, b, (((1,), (0,)), ((), ())), preferred_element_type=F32)


def _dot_nt(a, b):
    return lax.dot_general(a, b, (((1,), (1,)), ((), ())), preferred_element_type=F32)


def _dot_tn(a, b):
    return lax.dot_general(a, b, (((0,), (0,)), ((), ())), preferred_element_type=F32)


def _sigmoid(z):
    return 1.0 / (1.0 + jnp.exp(-z))


def _params(sem=None):
    if sem is None:
        return pltpu.CompilerParams(vmem_limit_bytes=VMEM_LIMIT)
    return pltpu.CompilerParams(dimension_semantics=sem, vmem_limit_bytes=VMEM_LIMIT)


def _rows(tm, n):
    return pl.BlockSpec((tm, n), lambda i: (i, 0))


def _cols(n, tm):
    return pl.BlockSpec((n, tm), lambda i: (0, i))


def _whole(shape):
    nd = len(shape)
    return pl.BlockSpec(shape, lambda i: (0,) * nd)


def _sds(shape, dtype):
    return jax.ShapeDtypeStruct(shape, dtype)


def _flip(v, d):
    return 1 - v if d else v


def _inproj_call(x, shift, scale, g1, w_int, w_q, w_kv, qg, kvg, cos256, sin256):
    s_len = x.shape[0]
    tm = min(ROW_TILE, s_len)

    def body(x_ref, sh_ref, sc_ref, g1_ref, w_ref, wq_ref, wkv_ref, qg_ref, kvg_ref, cos_ref, sin_ref,
             h_ref, sq_ref, sk_ref, sv_ref, sz_ref, cq_ref, ckv_ref, mz_ref, ga_ref, gb_ref, kpt_ref,
             qn_ref, qp_ref, kn_ref, vv_ref):
        xt = x_ref[...]
        r = lax.rsqrt(jnp.mean(xt * xt, axis=-1, keepdims=True) + EPS)
        h = (xt * r * g1_ref[...]) * (1.0 + sc_ref[...]) + sh_ref[...]
        hb = h.astype(BF16)
        h_ref[...] = hb

        def seg(a, b):
            return _dot_nt(hb, w_ref[a:b, :])

        sq_ref[...] = seg(O_SQ, O_SK).astype(BF16)
        sk_ref[...] = seg(O_SK, O_SV).astype(BF16)
        sv_ref[...] = seg(O_SV, O_SZ).astype(BF16)
        sz_ref[...] = seg(O_SZ, O_CQ)
        mz_ref[...] = seg(O_MZ, O_GA)
        ga_ref[...] = seg(O_GA, O_GB)
        gb_ref[...] = seg(O_GB, O_KR)
        cos = cos_ref[...]
        sin = sin_ref[...]
        kr = seg(O_KR, O_END)
        kpt_ref[...] = (kr[:, :128] * cos[:, :128] + kr[:, 128:] * sin[:, :128]).astype(BF16)

        cq = seg(O_CQ, O_CKV)
        cq_ref[...] = cq
        rq = lax.rsqrt(jnp.mean(cq * cq, axis=-1, keepdims=True) + EPS)
        cqn = (cq * rq * qg_ref[...]).astype(BF16)
        qa = _dot(cqn, wq_ref[...])
        qn_ref[...] = qa[:, :512].astype(BF16)
        qp_ref[...] = (qa[:, 512:768] * cos + qa[:, 768:] * sin).astype(BF16)

        ckv = seg(O_CKV, O_MZ)
        ckv_ref[...] = ckv
        rk = lax.rsqrt(jnp.mean(ckv * ckv, axis=-1, keepdims=True) + EPS)
        ckvn = (ckv * rk * kvg_ref[...]).astype(BF16)
        kva = _dot(ckvn, wkv_ref[...])
        kn_ref[...] = kva[:, :512].astype(BF16)
        vv_ref[...] = kva[:, 512:].astype(BF16)

    outs = [
        (D_MODEL, BF16), (512, BF16), (512, BF16), (512, BF16), (512, F32), (Q_RANK, F32), (KV_RANK, F32),
        (512, F32), (D_MODEL, F32), (D_MODEL, F32), (128, BF16), (512, BF16), (256, BF16), (512, BF16), (512, BF16),
    ]
    return pl.pallas_call(
        body, name="inproj", grid=(s_len // tm,),
        out_shape=tuple(_sds((s_len, n), dt) for n, dt in outs),
        in_specs=[_rows(tm, D_MODEL), _whole((1, D_MODEL)), _whole((1, D_MODEL)), _whole((1, D_MODEL)),
                  _whole((W_INT, D_MODEL)), _whole((Q_RANK, 1024)), _whole((KV_RANK, 1024)),
                  _whole((1, Q_RANK)), _whole((1, KV_RANK)), _rows(tm, 256), _rows(tm, 256)],
        out_specs=tuple(_rows(tm, n) for n, _ in outs),
        compiler_params=_params(("parallel",)),
    )(x, shift, scale, g1, w_int, w_q, w_kv, qg, kvg, cos256, sin256)


Z_CLAMP = 80.0


def _softplus_clamped(z):
    zc = jnp.minimum(z, Z_CLAMP)
    return zc, jnp.log(1.0 + jnp.exp(zc))


def _tri_sum(a, tri):
    return _dot(a.astype(BF16), tri)


def _sb_fwd_call(q, k, v):
    s_len = q.shape[0]
    tk = min(ATT_TILE, s_len)
    tq = min(ATT_Q_TILES * tk, s_len)
    r = tq // tk
    nq = s_len // tq

    def body(q_ref, k_ref, v_ref, o_ref, lt_ref):
        i = pl.program_id(1)
        q2 = q_ref[...]
        lane = lax.broadcasted_iota(jnp.int32, (1, 256), 1)
        krow = lax.broadcasted_iota(jnp.int32, (tk, tk), 0)
        kcol = lax.broadcasted_iota(jnp.int32, (tk, tk), 1)
        row = lax.broadcasted_iota(jnp.int32, (tq, tk), 0)
        col = lax.broadcasted_iota(jnp.int32, (tq, tk), 1)
        later = (krow > kcol).astype(BF16)
        valids = [col + u * tk < row for u in range(r)]
        hms = [(lane // 64) == hh for hh in range(4)]
        qms = [jnp.where(hm, q2, jnp.zeros_like(q2)) for hm in hms]

        def block(j, carry, valid):
            runs, acc = list(carry[:4]), carry[4]
            off = pl.multiple_of(j * tk, tk)
            kb = k_ref[pl.ds(off, tk), :]
            vb = v_ref[pl.ds(off, tk), :]
            ws = []
            for hh in range(4):
                zc, sp = _softplus_clamped(_dot_nt(qms[hh], kb))
                lm = jnp.where(valid, sp, 0.0) if valid is not None else sp
                suf = _tri_sum(lm, later)
                w = jnp.exp(zc - sp - suf - runs[hh])
                if valid is not None:
                    w = jnp.where(valid, w, 0.0)
                ws.append(w.astype(BF16))
                runs[hh] = runs[hh] + jnp.sum(lm, axis=1, keepdims=True)
            vstack = jnp.concatenate([jnp.where(hm, vb, jnp.zeros_like(vb)) for hm in hms], axis=0)
            acc = acc + _dot(jnp.concatenate(ws, axis=1), vstack)
            return (*runs, acc)

        zero = jnp.zeros((tq, 1), F32)
        carry = (zero, zero, zero, zero, jnp.zeros((tq, 256), F32))
        for u in reversed(range(r)):
            carry = block(i * r + u, carry, valids[u])
        carry = lax.fori_loop(0, i * r, lambda jj, cr: block(i * r - 1 - jj, cr, None), carry)
        for hh in range(4):
            lt_ref[0, :, hh:hh + 1] = carry[hh]
        o_ref[...] = carry[4]

    return pl.pallas_call(
        body, name="sb_fwd", grid=(2, nq),
        out_shape=(_sds((s_len, SB_WIDTH), F32), _sds((2, s_len, 4), F32)),
        in_specs=[pl.BlockSpec((tq, 256), lambda g, i: (i, g)),
                  pl.BlockSpec((s_len, 256), lambda g, i: (0, g)),
                  pl.BlockSpec((s_len, 256), lambda g, i: (0, g))],
        out_specs=(pl.BlockSpec((tq, 256), lambda g, i: (i, g)),
                   pl.BlockSpec((1, tq, 4), lambda g, i: (g, i, 0))),
        compiler_params=_params(("parallel", "parallel")),
    )(q, k, v)


def _sb_bwd_call(q, k, v, do, lt):
    s_len = q.shape[0]
    tk = min(ATT_TILE, s_len)
    tq = min(ATT_Q_TILES * tk, s_len)
    r = tq // tk
    nq = s_len // tq

    def body(q_ref, k_ref, v_ref, do_ref, lt_ref, dq_ref, dk_ref, dv_ref):
        i = pl.program_id(1)

        @pl.when(i == 0)
        def _():
            dk_ref[...] = jnp.zeros_like(dk_ref)
            dv_ref[...] = jnp.zeros_like(dv_ref)

        q2 = q_ref[...]
        do2 = do_ref[...].astype(BF16)
        lane = lax.broadcasted_iota(jnp.int32, (1, 256), 1)
        krow = lax.broadcasted_iota(jnp.int32, (tk, tk), 0)
        kcol = lax.broadcasted_iota(jnp.int32, (tk, tk), 1)
        row = lax.broadcasted_iota(jnp.int32, (tq, tk), 0)
        col = lax.broadcasted_iota(jnp.int32, (tq, tk), 1)
        earlier = (krow < kcol).astype(BF16)
        later = (krow > kcol).astype(BF16)
        valids = [col + u * tk < row for u in range(r)]
        hms = [(lane // 64) == hh for hh in range(4)]
        qms = [jnp.where(hm, q2, jnp.zeros_like(q2)) for hm in hms]
        doms = [jnp.where(hm, do2, jnp.zeros_like(do2)) for hm in hms]
        ltots = [lt_ref[0, :, hh:hh + 1] for hh in range(4)]
        q2t = jnp.transpose(q2.astype(F32))
        do2t = jnp.transpose(do_ref[...])
        subl = lax.broadcasted_iota(jnp.int32, (256, 1), 0)
        qtstack = jnp.concatenate(
            [jnp.where((subl // 64) == hh, q2t, 0.0).astype(BF16) for hh in range(4)], axis=1)
        dotstack = jnp.concatenate(
            [jnp.where((subl // 64) == hh, do2t, 0.0).astype(BF16) for hh in range(4)], axis=1)

        def block(j, carry, valid):
            lpre, ppre, dq = list(carry[0:4]), list(carry[4:8]), carry[8]
            off = pl.multiple_of(j * tk, tk)
            kb = k_ref[pl.ds(off, tk), :]
            vb = v_ref[pl.ds(off, tk), :]
            dzs, avs = [], []
            for hh in range(4):
                zc, sp = _softplus_clamped(_dot_nt(qms[hh], kb))
                lsig = zc - sp
                lm = jnp.where(valid, sp, 0.0) if valid is not None else sp
                rowsum = jnp.sum(lm, axis=1, keepdims=True)
                between = _tri_sum(lm, later) + ((ltots[hh] - lpre[hh]) - rowsum)
                a = jnp.exp(lsig - between)
                if valid is not None:
                    a = jnp.where(valid, a, 0.0)
                p = a * _dot_nt(doms[hh], vb)
                pbefore = ppre[hh] + _tri_sum(p, earlier)
                dz = p - jnp.exp(lsig) * (p + pbefore)
                if valid is not None:
                    dz = jnp.where(valid, dz, 0.0)
                dzs.append(dz.astype(BF16))
                avs.append(a.astype(BF16))
                lpre[hh] = lpre[hh] + rowsum
                ppre[hh] = ppre[hh] + jnp.sum(p, axis=1, keepdims=True)
            kstack = jnp.concatenate([jnp.where(hm, kb, jnp.zeros_like(kb)) for hm in hms], axis=0)
            dq = dq + _dot(jnp.concatenate(dzs, axis=1), kstack)
            dk_ref[:, pl.ds(off, tk)] += _dot(qtstack, jnp.concatenate(dzs, axis=0))
            dv_ref[:, pl.ds(off, tk)] += _dot(dotstack, jnp.concatenate(avs, axis=0))
            return (*lpre, *ppre, dq)

        zero = jnp.zeros((tq, 1), F32)
        carry = lax.fori_loop(0, i * r, lambda j, cr: block(j, cr, None),
                              (zero,) * 8 + (jnp.zeros((tq, 256), F32),))
        for u in range(r):
            carry = block(i * r + u, carry, valids[u])
        dq_ref[...] = carry[8].astype(BF16)

    return pl.pallas_call(
        body, name="sb_bwd", grid=(2, nq),
        out_shape=(_sds((s_len, SB_WIDTH), BF16), _sds((SB_WIDTH, s_len), F32), _sds((SB_WIDTH, s_len), F32)),
        in_specs=[pl.BlockSpec((tq, 256), lambda g, i: (i, g)),
                  pl.BlockSpec((s_len, 256), lambda g, i: (0, g)),
                  pl.BlockSpec((s_len, 256), lambda g, i: (0, g)),
                  pl.BlockSpec((tq, 256), lambda g, i: (i, g)),
                  pl.BlockSpec((1, tq, 4), lambda g, i: (g, i, 0))],
        out_specs=(pl.BlockSpec((tq, 256), lambda g, i: (i, g)),
                   pl.BlockSpec((256, s_len), lambda g, i: (g, 0)),
                   pl.BlockSpec((256, s_len), lambda g, i: (g, 0))),
        compiler_params=_params(("parallel", "arbitrary")),
    )(q, k, v, do, lt)


def _mla_fwd_call(qn, qp, kn, kpt, v):
    s_len = qn.shape[0]
    tk = min(MLA_KEY_TILE, s_len)
    tq = min(ATT_Q_TILES * ATT_TILE, s_len)
    r = tq // tk
    nq = s_len // tq

    def body(qn_ref, qp_ref, kn_ref, kpt_ref, v_ref, o_ref, lse_ref):
        i = pl.program_id(1)
        qn2 = qn_ref[...]
        qp2 = qp_ref[...]
        lane256 = lax.broadcasted_iota(jnp.int32, (1, 256), 1)
        lane128 = lax.broadcasted_iota(jnp.int32, (1, 128), 1)
        krow = lax.broadcasted_iota(jnp.int32, (tk, tk), 0)
        kcol = lax.broadcasted_iota(jnp.int32, (tk, tk), 1)
        row = lax.broadcasted_iota(jnp.int32, (tq, tk), 0)
        col = lax.broadcasted_iota(jnp.int32, (tq, tk), 1)
        valids = [col + u * tk <= row for u in range(r)]
        m64s = [(lane256 // 64) == hh for hh in range(4)]
        half = [(lane128 // 64) == u for u in range(2)]
        m32s = [(lane128 // 32) == hh for hh in range(4)]
        qcs = []
        for hh in range(4):
            qpair = qn2[:, 128 * (hh // 2):128 * (hh // 2) + 128]
            qcs.append(jnp.concatenate([jnp.where(half[hh % 2], qpair, jnp.zeros_like(qpair)),
                                        jnp.where(m32s[hh], qp2, jnp.zeros_like(qp2))], axis=1))

        def by_head(vals):
            return jnp.where(m64s[0], vals[0], jnp.where(m64s[1], vals[1], jnp.where(m64s[2], vals[2], vals[3])))

        def block(j, carry, valid):
            ms, ls, acc = list(carry[0:4]), list(carry[4:8]), carry[8]
            off = pl.multiple_of(j * tk, tk)
            knb = kn_ref[pl.ds(off, tk), :]
            kpb = kpt_ref[pl.ds(off, tk), :]
            vb = v_ref[pl.ds(off, tk), :]
            kcs = [jnp.concatenate([knb[:, 128 * pp:128 * pp + 128], kpb], axis=1) for pp in range(2)]
            ps, alphas = [], []
            for hh in range(4):
                s = _dot_nt(qcs[hh], kcs[hh // 2]) * MLA_SCALE
                if valid is not None:
                    s = jnp.where(valid, s, -1e30)
                mn = jnp.maximum(ms[hh], jnp.max(s, axis=1, keepdims=True))
                p = jnp.exp(s - mn)
                alpha = jnp.exp(ms[hh] - mn)
                ls[hh] = alpha * ls[hh] + jnp.sum(p, axis=1, keepdims=True)
                ms[hh] = mn
                ps.append(p.astype(BF16))
                alphas.append(alpha)
            pvs = []
            for pp in range(2):
                vpair = vb[:, 128 * pp:128 * pp + 128]
                vstack = jnp.concatenate([jnp.where(hf, vpair, jnp.zeros_like(vpair)) for hf in half], axis=0)
                pvs.append(_dot(jnp.concatenate(ps[2 * pp:2 * pp + 2], axis=1), vstack))
            acc = by_head(alphas) * acc + jnp.concatenate(pvs, axis=1)
            return (*ms, *ls, acc)

        neg = jnp.full((tq, 1), -1e30, F32)
        zero = jnp.zeros((tq, 1), F32)
        carry = lax.fori_loop(0, i * r, lambda j, cr: block(j, cr, None),
                              (neg,) * 4 + (zero,) * 4 + (jnp.zeros((tq, 256), F32),))
        for u in range(r):
            carry = block(i * r + u, carry, valids[u])
        o_ref[...] = carry[8] / by_head(list(carry[4:8]))
        for hh in range(4):
            lse_ref[0, :, hh:hh + 1] = carry[hh] + jnp.log(carry[4 + hh])

    return pl.pallas_call(
        body, name="mla_fwd", grid=(2, nq),
        out_shape=(_sds((s_len, MLA_WIDTH), F32), _sds((2, s_len, 4), F32)),
        in_specs=[pl.BlockSpec((tq, 256), lambda g, i: (i, g)),
                  pl.BlockSpec((tq, 128), lambda g, i: (i, g)),
                  pl.BlockSpec((s_len, 256), lambda g, i: (0, g)),
                  pl.BlockSpec((s_len, 128), lambda g, i: (0, 0)),
                  pl.BlockSpec((s_len, 256), lambda g, i: (0, g))],
        out_specs=(pl.BlockSpec((tq, 256), lambda g, i: (i, g)),
                   pl.BlockSpec((1, tq, 4), lambda g, i: (g, i, 0))),
        compiler_params=_params(("parallel", "parallel")),
    )(qn, qp, kn, kpt, v)


def _mla_bwd_call(qn, qp, kn, kpt, v, o, do, lse):
    s_len = qn.shape[0]
    tk = min(MLA_KEY_TILE, s_len)
    tq = min(ATT_Q_TILES * ATT_TILE, s_len)
    r = tq // tk
    nq = s_len // tq

    def body(qn_ref, qp_ref, kn_ref, kpt_ref, v_ref, o_ref, do_ref, lse_ref,
             dqn_ref, dqp_ref, dkn_ref, dkpt_ref, dv_ref):
        g = pl.program_id(0)
        i = pl.program_id(1)

        @pl.when(i == 0)
        def _():
            dkn_ref[...] = jnp.zeros_like(dkn_ref)
            dv_ref[...] = jnp.zeros_like(dv_ref)

        @pl.when((i == 0) & (g == 0))
        def _():
            dkpt_ref[...] = jnp.zeros_like(dkpt_ref)

        qn2 = qn_ref[...]
        qp2 = qp_ref[...]
        dof = do_ref[...]
        dob = dof.astype(BF16)
        prod = dof * o_ref[...]
        lane256 = lax.broadcasted_iota(jnp.int32, (1, 256), 1)
        lane128 = lax.broadcasted_iota(jnp.int32, (1, 128), 1)
        krow = lax.broadcasted_iota(jnp.int32, (tk, tk), 0)
        kcol = lax.broadcasted_iota(jnp.int32, (tk, tk), 1)
        row = lax.broadcasted_iota(jnp.int32, (tq, tk), 0)
        col = lax.broadcasted_iota(jnp.int32, (tq, tk), 1)
        valids = [col + u * tk <= row for u in range(r)]
        m64s = [(lane256 // 64) == hh for hh in range(4)]
        half = [(lane128 // 64) == u for u in range(2)]
        m32s = [(lane128 // 32) == hh for hh in range(4)]
        qcs, doms = [], []
        for hh in range(4):
            sl = slice(128 * (hh // 2), 128 * (hh // 2) + 128)
            qpair = qn2[:, sl]
            dpair = dob[:, sl]
            qcs.append(jnp.concatenate([jnp.where(half[hh % 2], qpair, jnp.zeros_like(qpair)),
                                        jnp.where(m32s[hh], qp2, jnp.zeros_like(qp2))], axis=1))
            doms.append(jnp.where(half[hh % 2], dpair, jnp.zeros_like(dpair)))
        dsums = [jnp.sum(jnp.where(m64, prod, 0.0), axis=1, keepdims=True) * MLA_SCALE for m64 in m64s]
        lses = [lse_ref[0, :, hh:hh + 1] for hh in range(4)]
        qn2t = jnp.transpose(qn2.astype(F32))
        qp2t = jnp.transpose(qp2.astype(F32))
        do2t = jnp.transpose(dof)
        sub128 = lax.broadcasted_iota(jnp.int32, (128, 1), 0)
        qtstacks, dotstacks = [], []
        for pp in range(2):
            qts, dts = [], []
            for u in range(2):
                hh = 2 * pp + u
                qts.append(jnp.concatenate(
                    [jnp.where((sub128 // 64) == u, qn2t[128 * pp:128 * pp + 128, :], 0.0),
                     jnp.where((sub128 // 32) == hh, qp2t, 0.0)], axis=0).astype(BF16))
                dts.append(jnp.where((sub128 // 64) == u, do2t[128 * pp:128 * pp + 128, :], 0.0).astype(BF16))
            qtstacks.append(jnp.concatenate(qts, axis=1))
            dotstacks.append(jnp.concatenate(dts, axis=1))

        def block(j, carry, valid):
            dqn, dqp = carry
            off = pl.multiple_of(j * tk, tk)
            knb = kn_ref[pl.ds(off, tk), :]
            kpb = kpt_ref[pl.ds(off, tk), :]
            vb = v_ref[pl.ds(off, tk), :]
            dqn_parts = []
            dkp = None
            for pp in range(2):
                sl = slice(128 * pp, 128 * pp + 128)
                knp = knb[:, sl]
                vpair = vb[:, sl]
                kc = jnp.concatenate([knp, kpb], axis=1)
                dss, pbs, kcms = [], [], []
                for u in range(2):
                    hh = 2 * pp + u
                    s = _dot_nt(qcs[hh], kc) * MLA_SCALE
                    if valid is not None:
                        s = jnp.where(valid, s, -1e30)
                    p = jnp.exp(s - lses[hh])
                    ds = p * (_dot_nt(doms[hh], vpair) * MLA_SCALE - dsums[hh])
                    dss.append(ds.astype(BF16))
                    pbs.append(p.astype(BF16))
                    kcms.append(jnp.concatenate([jnp.where(half[u], knp, jnp.zeros_like(knp)),
                                                 jnp.where(m32s[hh], kpb, jnp.zeros_like(kpb))], axis=1))
                dqc = _dot(jnp.concatenate(dss, axis=1), jnp.concatenate(kcms, axis=0))
                dqn_parts.append(dqc[:, :128])
                dqp = dqp + dqc[:, 128:]
                dkc = _dot(qtstacks[pp], jnp.concatenate(dss, axis=0))
                dkn_ref[128 * pp:128 * pp + 128, pl.ds(off, tk)] += dkc[:128, :]
                dkp = dkc[128:, :] if dkp is None else dkp + dkc[128:, :]
                dv_ref[128 * pp:128 * pp + 128, pl.ds(off, tk)] += _dot(dotstacks[pp], jnp.concatenate(pbs, axis=0))
            dqn = dqn + jnp.concatenate(dqn_parts, axis=1)
            dkpt_ref[:, pl.ds(off, tk)] += dkp
            return dqn, dqp

        carry = lax.fori_loop(0, i * r, lambda j, cr: block(j, cr, None),
                              (jnp.zeros((tq, 256), F32), jnp.zeros((tq, 128), F32)))
        for u in range(r):
            carry = block(i * r + u, carry, valids[u])
        dqn, dqp = carry
        dqn_ref[...] = dqn.astype(BF16)
        dqp_ref[...] = dqp.astype(BF16)

    return pl.pallas_call(
        body, name="mla_bwd", grid=(2, nq),
        out_shape=(_sds((s_len, 512), BF16), _sds((s_len, 256), BF16), _sds((512, s_len), F32),
                   _sds((128, s_len), F32), _sds((512, s_len), F32)),
        in_specs=[pl.BlockSpec((tq, 256), lambda g, i: (i, g)),
                  pl.BlockSpec((tq, 128), lambda g, i: (i, g)),
                  pl.BlockSpec((s_len, 256), lambda g, i: (0, g)),
                  pl.BlockSpec((s_len, 128), lambda g, i: (0, 0)),
                  pl.BlockSpec((s_len, 256), lambda g, i: (0, g)),
                  pl.BlockSpec((tq, 256), lambda g, i: (i, g)),
                  pl.BlockSpec((tq, 256), lambda g, i: (i, g)),
                  pl.BlockSpec((1, tq, 4), lambda g, i: (g, i, 0))],
        out_specs=(pl.BlockSpec((tq, 256), lambda g, i: (i, g)),
                   pl.BlockSpec((tq, 128), lambda g, i: (i, g)),
                   pl.BlockSpec((256, s_len), lambda g, i: (g, 0)),
                   pl.BlockSpec((128, s_len), lambda g, i: (0, 0)),
                   pl.BlockSpec((256, s_len), lambda g, i: (g, 0))),
        compiler_params=_params(("arbitrary", "arbitrary")),
    )(qn, qp, kn, kpt, v, o, do, lse)


def _post_call(x, tgt, oa, ob, sz, mz, ga, gb, gate, gf, wa, wb, wo, wat, wbt, wot):
    s_len = x.shape[0]
    tm = min(ROW_TILE, s_len)

    def body(x_ref, t_ref, oa_ref, ob_ref, sz_ref, mz_ref, ga_ref, gb_ref, gate_ref, gf_ref,
             wa_ref, wb_ref, wo_ref, wat_ref, wbt_ref, wot_ref,
             dx2_ref, doa_ref, dob_ref, dsz_ref, dmz_ref, dga_ref, dgb_ref,
             dwo_ref, dwa_ref, dwb_ref, dgf_ref, dgate_ref, loss_ref):
        @pl.when(pl.program_id(0) == 0)
        def _():
            dwo_ref[...] = jnp.zeros_like(dwo_ref)
            dwa_ref[...] = jnp.zeros_like(dwa_ref)
            dwb_ref[...] = jnp.zeros_like(dwb_ref)
            dgf_ref[...] = jnp.zeros_like(dgf_ref)
            dgate_ref[...] = jnp.zeros_like(dgate_ref)
            loss_ref[...] = jnp.zeros_like(loss_ref)

        gate = gate_ref[...]
        gf = gf_ref[...]
        oa = oa_ref[...]
        ob = ob_ref[...]
        sz = sz_ref[...]
        mz = mz_ref[...]
        sa = _sigmoid(sz)
        sb = _sigmoid(mz)
        silu_a = sz * sa
        silu_b = mz * sb
        ua = (oa * silu_a).astype(BF16)
        ub = (ob * silu_b).astype(BF16)
        ya = _dot(ua, wa_ref[...])
        yb = _dot(ub, wb_ref[...])
        sga = _sigmoid(ga_ref[...])
        sgb = _sigmoid(gb_ref[...])
        merged = (sga * ya + sgb * yb).astype(BF16)
        out = _dot(merged, wo_ref[...])
        x2 = x_ref[...] + gate * out
        r2 = lax.rsqrt(jnp.mean(x2 * x2, axis=-1, keepdims=True) + EPS)
        xhat = x2 * r2
        err = xhat * gf - t_ref[...]
        loss_ref[...] += 0.5 * jnp.sum(jnp.sum(err * err, axis=1, keepdims=True), axis=0, keepdims=True) / D_MODEL
        dy = err * (1.0 / D_MODEL)
        dgf_ref[...] += jnp.sum(dy * xhat, axis=0, keepdims=True)
        dxhat = dy * gf
        dx2 = r2 * (dxhat - xhat * jnp.mean(dxhat * xhat, axis=-1, keepdims=True))
        dx2_ref[...] = dx2
        dgate_ref[...] += jnp.sum(dx2 * out, axis=0, keepdims=True)
        dout = (dx2 * gate).astype(BF16)
        dmerged = _dot(dout, wot_ref[...])
        dwo_ref[...] += _dot_tn(merged, dout)
        dya = dmerged * sga
        dyb = dmerged * sgb
        dga_ref[...] = (dya * ya * (1.0 - sga)).astype(BF16)
        dgb_ref[...] = (dyb * yb * (1.0 - sgb)).astype(BF16)
        dyab = dya.astype(BF16)
        dybb = dyb.astype(BF16)
        dua = _dot(dyab, wat_ref[...])
        dub = _dot(dybb, wbt_ref[...])
        dwa_ref[...] += _dot_tn(ua, dyab)
        dwb_ref[...] += _dot_tn(ub, dybb)
        doa_ref[...] = dua * silu_a
        dob_ref[...] = dub * silu_b
        dsz_ref[...] = (dua * oa * (sa * (1.0 + sz * (1.0 - sa)))).astype(BF16)
        dmz_ref[...] = (dub * ob * (sb * (1.0 + mz * (1.0 - sb)))).astype(BF16)

    return pl.pallas_call(
        body, name="post", grid=(s_len // tm,),
        out_shape=(_sds((s_len, D_MODEL), F32), _sds((s_len, 512), F32), _sds((s_len, 512), F32),
                   _sds((s_len, 512), BF16), _sds((s_len, 512), BF16),
                   _sds((s_len, D_MODEL), BF16), _sds((s_len, D_MODEL), BF16),
                   _sds((D_MODEL, D_MODEL), F32), _sds((512, D_MODEL), F32), _sds((512, D_MODEL), F32),
                   _sds((1, D_MODEL), F32), _sds((1, D_MODEL), F32), _sds((1, 128), F32)),
        in_specs=[_rows(tm, D_MODEL), _rows(tm, D_MODEL), _rows(tm, 512), _rows(tm, 512), _rows(tm, 512),
                  _rows(tm, 512), _rows(tm, D_MODEL), _rows(tm, D_MODEL), _whole((1, D_MODEL)), _whole((1, D_MODEL)),
                  _whole((512, D_MODEL)), _whole((512, D_MODEL)), _whole((D_MODEL, D_MODEL)),
                  _whole((D_MODEL, 512)), _whole((D_MODEL, 512)), _whole((D_MODEL, D_MODEL))],
        out_specs=(_rows(tm, D_MODEL), _rows(tm, 512), _rows(tm, 512), _rows(tm, 512), _rows(tm, 512),
                   _rows(tm, D_MODEL), _rows(tm, D_MODEL),
                   _whole((D_MODEL, D_MODEL)), _whole((512, D_MODEL)), _whole((512, D_MODEL)),
                   _whole((1, D_MODEL)), _whole((1, D_MODEL)), _whole((1, 128))),
        compiler_params=_params(("arbitrary",)),
    )(x, tgt, oa, ob, sz, mz, ga, gb, gate, gf, wa, wb, wo, wat, wbt, wot)


def _bwdprep_call(dsq, dsk, dsv, dsz, dqn, dqp, dkn, dvv, dkpt, dmz, dga, dgb, cq, ckv, cos256, sin256,
                  qg, kvg, wqt, wkvt):
    s_len = cq.shape[0]
    tm = min(ROW_TILE, s_len)

    def body(dsq_ref, dsk_ref, dsv_ref, dsz_ref, dqn_ref, dqp_ref, dkn_ref, dvv_ref, dkpt_ref, dmz_ref,
             dga_ref, dgb_ref, cq_ref, ckv_ref, cos_ref, sin_ref, qg_ref, kvg_ref, wqt_ref, wkvt_ref,
             dp_ref, dwq_ref, dwkv_ref, dqg_ref, dkvg_ref):
        @pl.when(pl.program_id(0) == 0)
        def _():
            dwq_ref[...] = jnp.zeros_like(dwq_ref)
            dwkv_ref[...] = jnp.zeros_like(dwkv_ref)
            dqg_ref[...] = jnp.zeros_like(dqg_ref)
            dkvg_ref[...] = jnp.zeros_like(dkvg_ref)

        cos = cos_ref[...]
        sin = sin_ref[...]
        dp_ref[:, O_SQ:O_SK] = dsq_ref[...]
        dp_ref[:, O_SK:O_SV] = jnp.transpose(dsk_ref[...]).astype(BF16)
        dp_ref[:, O_SV:O_SZ] = jnp.transpose(dsv_ref[...]).astype(BF16)
        dp_ref[:, O_SZ:O_CQ] = dsz_ref[...]
        dp_ref[:, O_MZ:O_GA] = dmz_ref[...]
        dp_ref[:, O_GA:O_GB] = dga_ref[...]
        dp_ref[:, O_GB:O_KR] = dgb_ref[...]
        dkp = jnp.transpose(dkpt_ref[...])
        dp_ref[:, O_KR:O_KR + 128] = (dkp * cos[:, :128]).astype(BF16)
        dp_ref[:, O_KR + 128:O_END] = (dkp * sin[:, :128]).astype(BF16)
        dp_ref[:, O_END:W_INT] = jnp.zeros((tm, W_INT - O_END), BF16)

        cq = cq_ref[...]
        rq = lax.rsqrt(jnp.mean(cq * cq, axis=-1, keepdims=True) + EPS)
        cqh = cq * rq
        qg = qg_ref[...]
        cqn = (cqh * qg).astype(BF16)
        dqp = dqp_ref[...].astype(F32)
        dqa = jnp.concatenate([dqn_ref[...], (dqp * cos).astype(BF16), (dqp * sin).astype(BF16)], axis=1)
        dcqn = _dot(dqa, wqt_ref[...])
        dwq_ref[...] += _dot_tn(cqn, dqa)
        dqg_ref[...] += jnp.sum(dcqn * cqh, axis=0, keepdims=True)
        dh = dcqn * qg
        dcq = rq * (dh - cqh * jnp.mean(dh * cqh, axis=-1, keepdims=True))
        dp_ref[:, O_CQ:O_CKV] = dcq.astype(BF16)

        ckv = ckv_ref[...]
        rk = lax.rsqrt(jnp.mean(ckv * ckv, axis=-1, keepdims=True) + EPS)
        ckh = ckv * rk
        kvg = kvg_ref[...]
        ckvn = (ckh * kvg).astype(BF16)
        dkva = jnp.concatenate([jnp.transpose(dkn_ref[...]).astype(BF16),
                                jnp.transpose(dvv_ref[...]).astype(BF16)], axis=1)
        dckvn = _dot(dkva, wkvt_ref[...])
        dwkv_ref[...] += _dot_tn(ckvn, dkva)
        dkvg_ref[...] += jnp.sum(dckvn * ckh, axis=0, keepdims=True)
        dh2 = dckvn * kvg
        dckv = rk * (dh2 - ckh * jnp.mean(dh2 * ckh, axis=-1, keepdims=True))
        dp_ref[:, O_CKV:O_MZ] = dckv.astype(BF16)

    return pl.pallas_call(
        body, name="bwdprep", grid=(s_len // tm,),
        out_shape=(_sds((s_len, W_INT), BF16), _sds((Q_RANK, 1024), F32), _sds((KV_RANK, 1024), F32),
                   _sds((1, Q_RANK), F32), _sds((1, KV_RANK), F32)),
        in_specs=[_rows(tm, 512), _cols(512, tm), _cols(512, tm), _rows(tm, 512), _rows(tm, 512), _rows(tm, 256),
                  _cols(512, tm), _cols(512, tm), _cols(128, tm), _rows(tm, 512), _rows(tm, D_MODEL),
                  _rows(tm, D_MODEL), _rows(tm, Q_RANK), _rows(tm, KV_RANK), _rows(tm, 256), _rows(tm, 256),
                  _whole((1, Q_RANK)), _whole((1, KV_RANK)), _whole((1024, Q_RANK)), _whole((1024, KV_RANK))],
        out_specs=(_rows(tm, W_INT), _whole((Q_RANK, 1024)), _whole((KV_RANK, 1024)),
                   _whole((1, Q_RANK)), _whole((1, KV_RANK))),
        compiler_params=_params(("arbitrary",)),
    )(dsq, dsk, dsv, dsz, dqn, dqp, dkn, dvv, dkpt, dmz, dga, dgb, cq, ckv, cos256, sin256, qg, kvg, wqt, wkvt)


def _dh_call(dproj, w_int_t, x, dx2, scale, g1):
    s_len = x.shape[0]
    tm = min(2 * ROW_TILE, s_len)

    def body(dp_ref, wt_ref, x_ref, dx2_ref, sc_ref, g1_ref, gx_ref, dsh_ref, dsc_ref, dg1_ref):
        @pl.when(pl.program_id(0) == 0)
        def _():
            dsh_ref[...] = jnp.zeros_like(dsh_ref)
            dsc_ref[...] = jnp.zeros_like(dsc_ref)
            dg1_ref[...] = jnp.zeros_like(dg1_ref)

        dh = _dot(dp_ref[...], wt_ref[...])
        xt = x_ref[...]
        r = lax.rsqrt(jnp.mean(xt * xt, axis=-1, keepdims=True) + EPS)
        xh = xt * r
        g1 = g1_ref[...]
        xg = xh * g1
        dsh_ref[...] += jnp.sum(dh, axis=0, keepdims=True)
        dsc_ref[...] += jnp.sum(dh * xg, axis=0, keepdims=True)
        dxg = dh * (1.0 + sc_ref[...])
        dg1_ref[...] += jnp.sum(dxg * xh, axis=0, keepdims=True)
        dxh = dxg * g1
        gx_ref[...] = dx2_ref[...] + r * (dxh - xh * jnp.mean(dxh * xh, axis=-1, keepdims=True))

    return pl.pallas_call(
        body, name="dh", grid=(s_len // tm,),
        out_shape=(_sds((s_len, D_MODEL), F32), _sds((1, D_MODEL), F32), _sds((1, D_MODEL), F32),
                   _sds((1, D_MODEL), F32)),
        in_specs=[_rows(tm, W_INT), _whole((W_INT, D_MODEL)), _rows(tm, D_MODEL), _rows(tm, D_MODEL),
                  _whole((1, D_MODEL)), _whole((1, D_MODEL))],
        out_specs=(_rows(tm, D_MODEL), _whole((1, D_MODEL)), _whole((1, D_MODEL)), _whole((1, D_MODEL))),
        compiler_params=_params(("arbitrary",)),
    )(dproj, w_int_t, x, dx2, scale, g1)


def _small_call(svg, ct, dmod_sh):
    def body(sv_ref, ct_ref, dm_ref, tot_ref, gwada_ref):
        acc = sv_ref[0:1, :]
        for d in range(1, N_DEV):
            acc = acc + sv_ref[d:d + 1, :]
        tot_ref[...] = acc
        gwada_ref[...] = lax.dot_general(ct_ref[...], dm_ref[...], (((1,), (0,)), ((), ())),
                                         precision=lax.Precision.HIGHEST, preferred_element_type=F32)

    vmem = pl.BlockSpec(memory_space=pltpu.VMEM)
    return pl.pallas_call(
        body, name="small_grads",
        out_shape=(_sds((1, 8 * SV_COLS), F32), _sds((D_MODEL, 768), F32)),
        in_specs=[vmem, vmem, vmem], out_specs=(vmem, vmem),
        compiler_params=_params(),
    )(svg, ct, dmod_sh)


def _adamw_tile_rows(rows, cols):
    budget = 2 << 20
    if rows * cols * 4 <= budget or rows % 8:
        return rows
    best = 8
    for tr in range(8, rows + 1, 8):
        if rows % tr == 0 and tr * cols * 4 <= budget:
            best = tr
    return best


def _adamw_call(name, w, g, m, v):
    rows, cols = w.shape
    tr = _adamw_tile_rows(rows, cols)

    def body(w_ref, g_ref, m_ref, v_ref, d_ref, nm_ref, nv_ref):
        gg = g_ref[...]
        m2 = ADAM_B1 * m_ref[...] + (1.0 - ADAM_B1) * gg
        v2 = ADAM_B2 * v_ref[...] + (1.0 - ADAM_B2) * (gg * gg)
        m_hat = m2 / (1.0 - ADAM_B1 ** ADAM_STEP)
        v_hat = v2 / (1.0 - ADAM_B2 ** ADAM_STEP)
        d_ref[...] = -ADAM_LR * (m_hat / (jnp.sqrt(v_hat) + ADAM_EPS) + ADAM_WD * w_ref[...])
        nm_ref[...] = m2
        nv_ref[...] = v2

    spec = pl.BlockSpec((tr, cols), lambda i: (i, 0))
    return pl.pallas_call(
        body, name="adamw_" + name, grid=(rows // tr,),
        out_shape=(_sds((rows, cols), F32),) * 3,
        in_specs=[spec] * 4, out_specs=(spec,) * 3,
        compiler_params=_params(("parallel",)),
    )(w, g, m, v)


IN_SHARD = IN_WIDTH // N_CHIPS
HALF_D = D_MODEL // 2
SMALL_ROWS = (576, 512, 1024, 1024, 2048)
SMALL_TOTAL = sum(SMALL_ROWS)
SMALL_HALF = SMALL_TOTAL // 2
SMALL_SUM_ROWS = 432


def _gather_call(c_row, w_ada_sh, pack_in, pack_small):
    def body(c_ref, wada_ref, pki_ref, pks_ref, mg_ref, cg_ref, gwi_ref, gws_ref,
             cv, ssem_c, rsem_c, ssem_m, rsem_m, ssem_w, rsem_w, ssem_f, rsem_f, lsem):
        x, y, c = lax.axis_index("x"), lax.axis_index("y"), lax.axis_index("c")
        me = 4 * x + 2 * y + c
        chip = 2 * x + y
        rel3 = [(1, 0), (0, 1), (1, 1)]
        packs = [(pki_ref, gwi_ref), (pks_ref, gws_ref)]

        sends = []
        for j, (dx, dy) in enumerate(rel3):
            for a, (pk, gw) in enumerate(packs):
                cp = pltpu.make_async_remote_copy(
                    src_ref=pk.at[c], dst_ref=gw.at[chip, c], send_sem=ssem_w.at[j, a], recv_sem=rsem_w.at[j, a],
                    device_id=(_flip(x, dx), _flip(y, dy), c), device_id_type=MESH)
                cp.start()
                sends.append(cp)
        owns = []
        for a, (pk, gw) in enumerate(packs):
            own = pltpu.make_async_copy(pk, gw.at[chip], lsem.at[a])
            own.start()
            owns.append(own)

        cv[me] = c_ref[...]
        for r in range(1, N_DEV):
            dx, dy, dc = (r >> 2) & 1, (r >> 1) & 1, r & 1
            cp = pltpu.make_async_remote_copy(
                src_ref=c_ref, dst_ref=cv.at[me], send_sem=ssem_c.at[r - 1], recv_sem=rsem_c.at[r - 1],
                device_id=(_flip(x, dx), _flip(y, dy), _flip(c, dc)), device_id_type=MESH)
            cp.start()
            sends.append(cp)
        for r in range(1, N_DEV):
            dx, dy, dc = (r >> 2) & 1, (r >> 1) & 1, r & 1
            src = 4 * _flip(x, dx) + 2 * _flip(y, dy) + _flip(c, dc)
            pltpu.make_async_remote_copy(
                src_ref=c_ref, dst_ref=cv.at[src], send_sem=ssem_c.at[r - 1], recv_sem=rsem_c.at[r - 1],
                device_id=(x, y, c), device_id_type=MESH).wait_recv()
        rows = lax.broadcasted_iota(jnp.int32, (N_DEV, D_MODEL), 0)
        call = jnp.zeros((N_DEV, D_MODEL), F32)
        for b in range(N_DEV):
            call = jnp.where(rows == b, jnp.broadcast_to(cv[b], (N_DEV, D_MODEL)), call)
        cg_ref[...] = call

        mg_ref[chip] = lax.dot_general(call, wada_ref[...], (((1,), (0,)), ((), ())),
                                       precision=lax.Precision.HIGHEST, preferred_element_type=F32)
        for j, (dx, dy) in enumerate(rel3):
            cp = pltpu.make_async_remote_copy(
                src_ref=mg_ref.at[chip], dst_ref=mg_ref.at[chip], send_sem=ssem_m.at[j], recv_sem=rsem_m.at[j],
                device_id=(_flip(x, dx), _flip(y, dy), c), device_id_type=MESH)
            cp.start()
            sends.append(cp)
        for j, (dx, dy) in enumerate(rel3):
            src_chip = 2 * _flip(x, dx) + _flip(y, dy)
            pltpu.make_async_remote_copy(
                src_ref=mg_ref.at[src_chip], dst_ref=mg_ref.at[src_chip], send_sem=ssem_m.at[j],
                recv_sem=rsem_m.at[j], device_id=(x, y, c), device_id_type=MESH).wait_recv()
        for j, (dx, dy) in enumerate(rel3):
            src_chip = 2 * _flip(x, dx) + _flip(y, dy)
            for a, (pk, gw) in enumerate(packs):
                pltpu.make_async_remote_copy(
                    src_ref=pk.at[c], dst_ref=gw.at[src_chip, c], send_sem=ssem_w.at[j, a],
                    recv_sem=rsem_w.at[j, a], device_id=(x, y, c), device_id_type=MESH).wait_recv()
                cp = pltpu.make_async_remote_copy(
                    src_ref=gw.at[src_chip, c], dst_ref=gw.at[src_chip, c], send_sem=ssem_f.at[j, a],
                    recv_sem=rsem_f.at[j, a], device_id=(x, y, 1 - c), device_id_type=MESH)
                cp.start()
                sends.append(cp)
        for j, (dx, dy) in enumerate(rel3):
            src_chip = 2 * _flip(x, dx) + _flip(y, dy)
            for a, (pk, gw) in enumerate(packs):
                pltpu.make_async_remote_copy(
                    src_ref=pk.at[c], dst_ref=gw.at[src_chip, 1 - c], send_sem=ssem_f.at[j, a],
                    recv_sem=rsem_f.at[j, a], device_id=(x, y, c), device_id_type=MESH).wait_recv()
        for cp in sends:
            cp.wait_send()
        for own in owns:
            own.wait()

    vmem = pl.BlockSpec(memory_space=pltpu.VMEM)
    return pl.pallas_call(
        body, name="gather_fwd",
        out_shape=(_sds((N_CHIPS, N_DEV, 768), F32), _sds((N_DEV, D_MODEL), F32),
                   _sds((N_CHIPS, 2, IN_SHARD, HALF_D), BF16), _sds((N_CHIPS, 2, SMALL_HALF, LANES), BF16)),
        in_specs=[vmem, vmem, vmem, vmem], out_specs=(vmem, vmem, vmem, vmem),
        scratch_shapes=[
            pltpu.VMEM((N_DEV, 1, D_MODEL), F32),
            pltpu.SemaphoreType.DMA((N_DEV - 1,)), pltpu.SemaphoreType.DMA((N_DEV - 1,)),
            pltpu.SemaphoreType.DMA((3,)), pltpu.SemaphoreType.DMA((3,)),
            pltpu.SemaphoreType.DMA((3, 2)), pltpu.SemaphoreType.DMA((3, 2)),
            pltpu.SemaphoreType.DMA((3, 2)), pltpu.SemaphoreType.DMA((3, 2)),
            pltpu.SemaphoreType.DMA((2,)),
        ],
        compiler_params=_params(),
    )(c_row, w_ada_sh, pack_in, pack_small)


def _reduce_call(g_in, g_small, sv):
    def body(gi_ref, gs_ref, sv_ref, fi_ref, fs_ref, svg_ref, pair_i, pair_s, send_i, send_s, land_i, land_s,
             ssem_p, rsem_p, ssem_g, rsem_g, ssem_s, rsem_s, ssem_x, rsem_x):
        x, y, c = lax.axis_index("x"), lax.axis_index("y"), lax.axis_index("c")
        me = 4 * x + 2 * y + c
        chip = 2 * x + y
        rel3 = [(1, 0), (0, 1), (1, 1)]
        payloads = [(gi_ref, pair_i, send_i, land_i, fi_ref), (gs_ref, pair_s, send_s, land_s, fs_ref)]
        copies = []

        for k in range(N_CHIPS):
            for a, (g, pair, _, _, _) in enumerate(payloads):
                cp = pltpu.make_async_remote_copy(
                    src_ref=g.at[2 * k + 1 - c], dst_ref=pair.at[k], send_sem=ssem_p.at[k, a],
                    recv_sem=rsem_p.at[k, a], device_id=(x, y, 1 - c), device_id_type=MESH)
                cp.start()
                copies.append(cp)

        for r in range(1, N_DEV):
            dx, dy, dc = (r >> 2) & 1, (r >> 1) & 1, r & 1
            cp = pltpu.make_async_remote_copy(
                src_ref=sv_ref, dst_ref=svg_ref.at[me], send_sem=ssem_s.at[r - 1], recv_sem=rsem_s.at[r - 1],
                device_id=(_flip(x, dx), _flip(y, dy), _flip(c, dc)), device_id_type=MESH)
            cp.start()
            copies.append(cp)
        svg_ref[me] = sv_ref[...]

        for k in range(N_CHIPS):
            for a, (g, pair, _, _, _) in enumerate(payloads):
                pltpu.make_async_remote_copy(
                    src_ref=g.at[2 * k + c], dst_ref=pair.at[k], send_sem=ssem_p.at[k, a],
                    recv_sem=rsem_p.at[k, a], device_id=(x, y, c), device_id_type=MESH).wait_recv()

        def pair_sum(k, store_in, store_small):
            for qd in range(HALF_D // LANES):
                sl = slice(LANES * qd, LANES * qd + LANES)
                store_in(sl, gi_ref[2 * k + c, :, sl].astype(F32) + pair_i[k, :, sl].astype(F32))

            def rows(i, carry):
                sl = pl.ds(pl.multiple_of(i * SMALL_SUM_ROWS, 16), SMALL_SUM_ROWS)
                store_small(sl, gs_ref[2 * k + c, sl, :].astype(F32) + pair_s[k, sl, :].astype(F32))
                return carry

            lax.fori_loop(0, SMALL_HALF // SMALL_SUM_ROWS, rows, 0)

        for j, (dx, dy) in enumerate(rel3):
            tx, ty = _flip(x, dx), _flip(y, dy)

            def put_in(sl, val, j=j):
                send_i[j, :, sl] = val.astype(BF16)

            def put_small(sl, val, j=j):
                send_s[j, sl, :] = val.astype(BF16)

            pair_sum(2 * tx + ty, put_in, put_small)
            for a, (_, _, send, land, _) in enumerate(payloads):
                cp = pltpu.make_async_remote_copy(
                    src_ref=send.at[j], dst_ref=land.at[j], send_sem=ssem_g.at[j, a], recv_sem=rsem_g.at[j, a],
                    device_id=(tx, ty, c), device_id_type=MESH)
                cp.start()
                copies.append(cp)

        def own_in(sl, val):
            fi_ref[c, :, sl] = val

        def own_small(sl, val):
            fs_ref[c, sl, :] = val

        pair_sum(chip, own_in, own_small)
        for j in range(3):
            for a, (_, _, send, land, _) in enumerate(payloads):
                pltpu.make_async_remote_copy(
                    src_ref=send.at[j], dst_ref=land.at[j], send_sem=ssem_g.at[j, a], recv_sem=rsem_g.at[j, a],
                    device_id=(x, y, c), device_id_type=MESH).wait_recv()
            for qd in range(HALF_D // LANES):
                sl = slice(LANES * qd, LANES * qd + LANES)
                fi_ref[c, :, sl] += land_i[j, :, sl].astype(F32)

            def add_rows(i, carry, j=j):
                sl = pl.ds(pl.multiple_of(i * SMALL_SUM_ROWS, 16), SMALL_SUM_ROWS)
                fs_ref[c, sl, :] += land_s[j, sl, :].astype(F32)
                return carry

            lax.fori_loop(0, SMALL_HALF // SMALL_SUM_ROWS, add_rows, 0)

        for a, f in enumerate((fi_ref, fs_ref)):
            cp = pltpu.make_async_remote_copy(
                src_ref=f.at[c], dst_ref=f.at[c], send_sem=ssem_x.at[a], recv_sem=rsem_x.at[a],
                device_id=(x, y, 1 - c), device_id_type=MESH)
            cp.start()
            copies.append(cp)
        for a, f in enumerate((fi_ref, fs_ref)):
            pltpu.make_async_remote_copy(
                src_ref=f.at[c], dst_ref=f.at[1 - c], send_sem=ssem_x.at[a], recv_sem=rsem_x.at[a],
                device_id=(x, y, c), device_id_type=MESH).wait_recv()
        for r in range(1, N_DEV):
            dx, dy, dc = (r >> 2) & 1, (r >> 1) & 1, r & 1
            src = 4 * _flip(x, dx) + 2 * _flip(y, dy) + _flip(c, dc)
            pltpu.make_async_remote_copy(
                src_ref=sv_ref, dst_ref=svg_ref.at[src], send_sem=ssem_s.at[r - 1],
                recv_sem=rsem_s.at[r - 1], device_id=(x, y, c), device_id_type=MESH).wait_recv()
        for cp in copies:
            cp.wait_send()

    vmem = pl.BlockSpec(memory_space=pltpu.VMEM)
    return pl.pallas_call(
        body, name="grad_reduce",
        out_shape=(_sds((2, IN_SHARD, HALF_D), F32), _sds((2, SMALL_HALF, LANES), F32),
                   _sds((N_DEV, 8, SV_COLS), F32)),
        in_specs=[vmem, vmem, vmem], out_specs=(vmem, vmem, vmem),
        scratch_shapes=[
            pltpu.VMEM((N_CHIPS, IN_SHARD, HALF_D), BF16), pltpu.VMEM((N_CHIPS, SMALL_HALF, LANES), BF16),
            pltpu.VMEM((3, IN_SHARD, HALF_D), BF16), pltpu.VMEM((3, SMALL_HALF, LANES), BF16),
            pltpu.VMEM((3, IN_SHARD, HALF_D), BF16), pltpu.VMEM((3, SMALL_HALF, LANES), BF16),
            pltpu.SemaphoreType.DMA((N_CHIPS, 2)), pltpu.SemaphoreType.DMA((N_CHIPS, 2)),
            pltpu.SemaphoreType.DMA((3, 2)), pltpu.SemaphoreType.DMA((3, 2)),
            pltpu.SemaphoreType.DMA((N_DEV - 1,)), pltpu.SemaphoreType.DMA((N_DEV - 1,)),
            pltpu.SemaphoreType.DMA((2,)), pltpu.SemaphoreType.DMA((2,)),
        ],
        compiler_params=_params(),
    )(g_in, g_small, sv)


def _dwin_call(h, dproj):
    s_len = h.shape[0]
    tm = min(4 * ROW_TILE, s_len)
    nrow = s_len // tm
    nc = 4
    chunk = W_INT // nc

    def body(h_ref, dp_ref, dw_ref, acc):
        i = pl.program_id(1)

        @pl.when(i == 0)
        def _():
            acc[...] = jnp.zeros_like(acc)

        acc[...] += _dot_tn(dp_ref[...], h_ref[...])

        @pl.when(i == nrow - 1)
        def _():
            dw_ref[...] = acc[...].astype(BF16)

    return pl.pallas_call(
        body, name="dwin", grid=(nc, nrow),
        out_shape=_sds((W_INT, D_MODEL), BF16),
        in_specs=[pl.BlockSpec((tm, D_MODEL), lambda c, i: (i, 0)),
                  pl.BlockSpec((tm, chunk), lambda c, i: (i, c))],
        out_specs=pl.BlockSpec((chunk, D_MODEL), lambda c, i: (c, 0)),
        scratch_shapes=[pltpu.VMEM((chunk, D_MODEL), F32)],
        compiler_params=_params(("parallel", "arbitrary")),
    )(h, dproj)


def _swap_rows(w, group):
    r, n = w.shape
    return w.reshape(r // group, 2, group // 2, n)[:, ::-1].reshape(r, n)


def _internal_weights(w_in_t, w_uq, w_ukv):
    krot_t = w_in_t[2688:2720]
    w_int_t = jnp.concatenate([
        w_in_t[0:512] * jnp.asarray(0.125, w_in_t.dtype), w_in_t[512:2048],
        w_in_t[2048:2432], w_in_t[2432:2688], w_in_t[2720:3232], w_in_t[3232:4256], w_in_t[4256:5280],
        jnp.tile(krot_t, (4, 1)), jnp.tile(_swap_rows(krot_t, 32), (4, 1)),
        jnp.zeros((W_INT - O_END, D_MODEL), w_in_t.dtype)], axis=0)
    uq = w_uq.reshape(Q_RANK, N_HEADS, 96)
    wp = uq[:, :, 64:].reshape(Q_RANK, 256)
    w_q = jnp.concatenate([uq[:, :, :64].reshape(Q_RANK, 512), wp, _swap_halves(wp, 32)], axis=1)
    ukv = w_ukv.reshape(KV_RANK, N_HEADS, 128)
    w_kv = jnp.concatenate([ukv[:, :, :64].reshape(KV_RANK, 512), ukv[:, :, 64:].reshape(KV_RANK, 512)], axis=1)
    return w_int_t, w_q, w_kv


def _true_weight_grads(dwi_t, dwq, dwkv):
    dkr = dwi_t[O_KR:O_KR + 128].astype(F32).reshape(4, 32, D_MODEL).sum(axis=0)
    dkr_sw = dwi_t[O_KR + 128:O_END].astype(F32).reshape(4, 32, D_MODEL).sum(axis=0)
    dkrot_t = (dkr + _swap_rows(dkr_sw, 32)).astype(dwi_t.dtype)
    g_in_t = jnp.concatenate([
        dwi_t[0:512] * jnp.asarray(0.125, dwi_t.dtype), dwi_t[512:2048], dwi_t[O_CQ:O_CKV], dwi_t[O_CKV:O_MZ],
        dkrot_t, dwi_t[O_MZ:O_GA], dwi_t[O_GA:O_GB], dwi_t[O_GB:O_KR]], axis=0)
    dwp = dwq[:, 512:768] + _swap_halves(dwq[:, 768:1024], 32)
    g_uq = jnp.concatenate([dwq[:, :512].reshape(Q_RANK, N_HEADS, 64), dwp.reshape(Q_RANK, N_HEADS, 32)],
                           axis=2).reshape(Q_RANK, 768)
    g_ukv = jnp.concatenate([dwkv[:, :512].reshape(KV_RANK, N_HEADS, 64), dwkv[:, 512:].reshape(KV_RANK, N_HEADS, 64)],
                            axis=2).reshape(KV_RANK, 1024)
    return g_in_t, g_uq, g_ukv


def _swap_halves(w, group):
    r, n = w.shape
    return w.reshape(r, n // group, 2, group // 2)[:, :, ::-1, :].reshape(r, n)


def _pack_shards(parts):
    return jnp.concatenate([p.reshape(-1, LANES) for p in parts], axis=0)


def _unpack_small(gw):
    offs = [0]
    for r in SMALL_ROWS:
        offs.append(offs[-1] + r)

    def cols(i, rows, shard_cols):
        blk = gw[:, offs[i]:offs[i + 1]].reshape(N_CHIPS, rows, shard_cols)
        return blk.transpose(1, 0, 2).reshape(rows, N_CHIPS * shard_cols)

    return (cols(0, Q_RANK, 192), cols(1, KV_RANK, 256), cols(2, 512, 256), cols(3, 512, 256),
            gw[:, offs[4]:offs[5]].reshape(D_MODEL, D_MODEL))


def _chip_major(g, shard_cols):
    r = g.shape[0]
    return g.reshape(r, N_CHIPS, shard_cols).transpose(1, 0, 2).reshape(N_CHIPS, -1, LANES)


def kernel(x, c, positions, w_ada, b_ada, norm_gain, w_in, q_norm_gain, w_uq, kv_norm_gain, w_ukv, w_branch_a, w_branch_b, w_out, final_norm_gain, loss_target, m_w_ada, m_b_ada, m_norm_gain, m_w_in, m_q_norm_gain, m_w_uq, m_kv_norm_gain, m_w_ukv, m_w_branch_a, m_w_branch_b, m_w_out, m_final_norm_gain, v_w_ada, v_b_ada, v_norm_gain, v_w_in, v_q_norm_gain, v_w_uq, v_kv_norm_gain, v_w_ukv, v_w_branch_a, v_w_branch_b, v_w_out, v_final_norm_gain):
    ix, iy, ic = lax.axis_index("x"), lax.axis_index("y"), lax.axis_index("c")
    me = 4 * ix + 2 * iy + ic
    chip = 2 * ix + iy
    xs = x[0]
    tgt = loss_target[0]
    s_len = xs.shape[0]

    w_in_t = jnp.swapaxes(w_in[0], 0, 1)
    w_in_tb = w_in_t.astype(BF16)
    pack_in = jnp.stack([w_in_tb[:, :HALF_D], w_in_tb[:, HALF_D:]], axis=0)
    small_shards = (w_uq[0], w_ukv[0], w_branch_a[0], w_branch_b[0], w_out[0])
    pack_small = _pack_shards([s.astype(BF16) for s in small_shards]).reshape(2, SMALL_HALF, LANES)
    mg, call, gw_in, gw_small = _gather_call(c, w_ada[0], pack_in, pack_small)
    mod = mg.transpose(1, 0, 2).reshape(N_DEV, 3 * D_MODEL) + b_ada
    mod_me = lax.dynamic_slice_in_dim(mod, me, 1, axis=0)
    shift, scale, gate = mod_me[:, :D_MODEL], mod_me[:, D_MODEL:2 * D_MODEL], mod_me[:, 2 * D_MODEL:]

    f_in_t = jnp.concatenate([gw_in[:, 0], gw_in[:, 1]], axis=2).reshape(IN_WIDTH, D_MODEL)
    f_uq, f_ukv, f_a, f_b, f_out = _unpack_small(gw_small.reshape(N_CHIPS, SMALL_TOTAL, LANES))
    w_int_t, w_q, w_kv = _internal_weights(f_in_t, f_uq, f_ukv)

    inv_freq = ROPE_BASE ** (-jnp.arange(0, ROPE_DIM, 2, dtype=F32) / ROPE_DIM)
    ang = positions[0].astype(F32)[:, None] * inv_freq
    cs, sn = jnp.cos(ang), jnp.sin(ang)
    cos256 = jnp.tile(jnp.concatenate([cs, cs], axis=1), (1, 8))
    sin256 = jnp.tile(jnp.concatenate([-sn, sn], axis=1), (1, 8))

    (h, sq, sk, sv, sz, cq, ckv, mz, ga, gb, kpt, qn, qp, kn, vv) = _inproj_call(
        xs, shift, scale, norm_gain, w_int_t, w_q, w_kv, q_norm_gain, kv_norm_gain, cos256, sin256)
    oa, lt = _sb_fwd_call(sq, sk, sv)
    ob, lse = _mla_fwd_call(qn, qp, kn, kpt, vv)

    gf = final_norm_gain.reshape(1, D_MODEL)
    (dx2, doa, dob, dsz, dmz, dga, dgb, dwo, dwa, dwb, dgf, dgate, loss_p) = _post_call(
        xs, tgt, oa, ob, sz, mz, ga, gb, gate, gf, f_a, f_b, f_out, f_a.T, f_b.T, f_out.T)

    dsq, dsk_t, dsv_t = _sb_bwd_call(sq, sk, sv, doa, lt)
    dqn, dqp, dkn_t, dkpt_t, dvv_t = _mla_bwd_call(qn, qp, kn, kpt, vv, ob, dob, lse)

    dproj, dwq, dwkv, dqg, dkvg = _bwdprep_call(
        dsq, dsk_t, dsv_t, dsz, dqn, dqp, dkn_t, dvv_t, dkpt_t, dmz, dga, dgb, cq, ckv, cos256, sin256,
        q_norm_gain, kv_norm_gain, w_q.T, w_kv.T)
    grad_x, dshift, dscale, dg1 = _dh_call(dproj, w_int_t, xs, dx2, scale, norm_gain)
    dwi_t = _dwin_call(h, dproj)
    g_in_t, g_uq, g_ukv = _true_weight_grads(dwi_t, dwq, dwkv)

    g_in_c = g_in_t.reshape(N_CHIPS, IN_SHARD, D_MODEL)
    g_in_pieces = jnp.stack([g_in_c[:, :, :HALF_D], g_in_c[:, :, HALF_D:]], axis=1).reshape(N_DEV, IN_SHARD, HALF_D)
    g_small = jnp.concatenate([
        _chip_major(g_uq, 192), _chip_major(g_ukv, 256), _chip_major(dwa, 256), _chip_major(dwb, 256),
        dwo.reshape(N_CHIPS, -1, LANES)], axis=1).astype(BF16).reshape(N_DEV, SMALL_HALF, LANES)
    small = jnp.concatenate([
        dshift, dscale, dgate, dg1, dqg, dkvg, dgf, loss_p,
        jnp.zeros((1, 8 * SV_COLS - 5888), F32)], axis=1).reshape(8, SV_COLS)
    full_in, full_small, svg = _reduce_call(g_in_pieces, g_small, small)
    gs_in_t = jnp.concatenate([full_in[0], full_in[1]], axis=1)
    full = full_small.reshape(SMALL_TOTAL, LANES)
    offs = [0]
    for r in SMALL_ROWS:
        offs.append(offs[-1] + r)
    gs_uq = full[offs[0]:offs[1]].reshape(Q_RANK, 192)
    gs_ukv = full[offs[1]:offs[2]].reshape(KV_RANK, 256)
    gs_a = full[offs[2]:offs[3]].reshape(512, 256)
    gs_b = full[offs[3]:offs[4]].reshape(512, 256)
    gs_out = full[offs[4]:offs[5]].reshape(256, D_MODEL)

    svm = svg.reshape(N_DEV, 8 * SV_COLS)
    dmod_sh = lax.dynamic_slice_in_dim(svm[:, :3 * D_MODEL], chip * 768, 768, axis=1)
    tot, gs_ada = _small_call(svm, call.T, dmod_sh)
    g_bada = tot[:, 0:3072]
    g_g1 = tot[:, 3072:4096]
    g_qg = tot[:, 4096:4480]
    g_kvg = tot[:, 4480:4736]
    g_gf = tot[:, 4736:5760]
    loss = tot[0, 5760]

    names = ["w_ada", "b_ada", "norm_gain", "w_in", "q_norm_gain", "w_uq", "kv_norm_gain", "w_ukv",
             "w_branch_a", "w_branch_b", "w_out", "final_norm_gain"]
    ws = [w_ada[0], b_ada, norm_gain, w_in_t, q_norm_gain, w_uq[0], kv_norm_gain, w_ukv[0],
          w_branch_a[0], w_branch_b[0], w_out[0], final_norm_gain.reshape(1, D_MODEL)]
    gs = [gs_ada, g_bada, g_g1, gs_in_t, g_qg, gs_uq, g_kvg, gs_ukv, gs_a, gs_b, gs_out, g_gf]
    ms = [m_w_ada[0], m_b_ada, m_norm_gain, jnp.swapaxes(m_w_in[0], 0, 1), m_q_norm_gain, m_w_uq[0],
          m_kv_norm_gain, m_w_ukv[0], m_w_branch_a[0], m_w_branch_b[0], m_w_out[0],
          m_final_norm_gain.reshape(1, D_MODEL)]
    vs = [v_w_ada[0], v_b_ada, v_norm_gain, jnp.swapaxes(v_w_in[0], 0, 1), v_q_norm_gain, v_w_uq[0],
          v_kv_norm_gain, v_w_ukv[0], v_w_branch_a[0], v_w_branch_b[0], v_w_out[0],
          v_final_norm_gain.reshape(1, D_MODEL)]
    refs = [w_ada, b_ada, norm_gain, w_in, q_norm_gain, w_uq, kv_norm_gain, w_ukv,
            w_branch_a, w_branch_b, w_out, final_norm_gain]
    grads, deltas, new_ms, new_vs = [], [], [], []
    for n, w_, g_, m_, v_, ref in zip(names, ws, gs, ms, vs, refs):
        outs = (g_,) + _adamw_call(n, w_, g_, m_, v_)
        if n == "w_in":
            outs = tuple(jnp.swapaxes(o_, 0, 1) for o_ in outs)
        for lst, o_ in zip((grads, deltas, new_ms, new_vs), outs):
            lst.append(o_.reshape(ref.shape))

    return (loss, grad_x.reshape(x.shape), *grads, *deltas, *new_ms, *new_vs)
```

```python
import math

import jax
import jax.numpy as jnp
from jax import lax
from jax.experimental import pallas as pl
from jax.experimental.pallas import tpu as pltpu

F32 = jnp.float32
BF16 = jnp.bfloat16

D_MODEL = 1024
SB_WIDTH = 512
MLA_WIDTH = 512
Q_RANK = 384
KV_RANK = 256
ROPE_DIM = 32
N_HEADS = 8
IN_WIDTH = 5280
EPS = 1e-6
ROPE_BASE = 10000.0
MLA_SCALE = 1.0 / math.sqrt(96.0)

ADAM_LR = 0.001
ADAM_B1 = 0.9
ADAM_B2 = 0.999
ADAM_EPS = 1e-08
ADAM_WD = 0.01
ADAM_STEP = 10

O_SQ, O_SK, O_SV, O_SZ, O_CQ, O_CKV, O_MZ, O_GA, O_GB, O_KR, O_END = (
    0, 512, 1024, 1536, 2048, 2432, 2688, 3200, 4224, 5248, 5504)
W_INT = 5632

N_CHIPS = 4
N_DEV = 8
LANES = 128
SV_COLS = 768

ROW_TILE = 256
ATT_TILE = 256
ATT_Q_TILES = 2
FWD_Q_TILES = 4
MLA_KEY_TILE = 512
VMEM_LIMIT = 56 * 1024 * 1024

MESH = pl.DeviceIdType.MESH


def _dot(a, b):
    return lax.dot_general(a, b, (((1,), (0,)), ((), ())), preferred_element_type=F32)


def _dot_nt(a, b):
    return lax.dot_general(a, b, (((1,), (1,)), ((), ())), preferred_element_type=F32)


def _dot_tn(a, b):
    return lax.dot_general(a, b, (((0,), (0,)), ((), ())), preferred_element_type=F32)


def _sigmoid(z):
    return 1.0 / (1.0 + jnp.exp(-z))


def _params(sem=None):
    if sem is None:
        return pltpu.CompilerParams(vmem_limit_bytes=VMEM_LIMIT)
    return pltpu.CompilerParams(dimension_semantics=sem, vmem_limit_bytes=VMEM_LIMIT)


def _rows(tm, n):
    return pl.BlockSpec((tm, n), lambda i: (i, 0))


def _cols(n, tm):
    return pl.BlockSpec((n, tm), lambda i: (0, i))


def _whole(shape):
    nd = len(shape)
    return pl.BlockSpec(shape, lambda i: (0,) * nd)


def _sds(shape, dtype):
    return jax.ShapeDtypeStruct(shape, dtype)


def _flip(v, d):
    return 1 - v if d else v


def _inproj_call(x, shift, scale, g1, w_int, w_q, w_kv, qg, kvg, cos256, sin256):
    s_len = x.shape[0]
    tm = min(ROW_TILE, s_len)

    def body(x_ref, sh_ref, sc_ref, g1_ref, w_ref, wq_ref, wkv_ref, qg_ref, kvg_ref, cos_ref, sin_ref,
             h_ref, sq_ref, sk_ref, sv_ref, sz_ref, cq_ref, ckv_ref, mz_ref, ga_ref, gb_ref, kpt_ref,
             qn_ref, qp_ref, kn_ref, vv_ref):
        xt = x_ref[...]
        r = lax.rsqrt(jnp.mean(xt * xt, axis=-1, keepdims=True) + EPS)
        h = (xt * r * g1_ref[...]) * (1.0 + sc_ref[...]) + sh_ref[...]
        hb = h.astype(BF16)
        h_ref[...] = hb

        def seg(a, b):
            return _dot_nt(hb, w_ref[a:b, :])

        sq_ref[...] = seg(O_SQ, O_SK).astype(BF16)
        sk_ref[...] = seg(O_SK, O_SV).astype(BF16)
        sv_ref[...] = seg(O_SV, O_SZ).astype(BF16)
        sz_ref[...] = seg(O_SZ, O_CQ)
        mz_ref[...] = seg(O_MZ, O_GA)
        ga_ref[...] = seg(O_GA, O_GB)
        gb_ref[...] = seg(O_GB, O_KR)
        cos = cos_ref[...]
        sin = sin_ref[...]
        kr = seg(O_KR, O_END)
        kpt_ref[...] = (kr[:, :128] * cos[:, :128] + kr[:, 128:] * sin[:, :128]).astype(BF16)

        cq = seg(O_CQ, O_CKV)
        cq_ref[...] = cq
        rq = lax.rsqrt(jnp.mean(cq * cq, axis=-1, keepdims=True) + EPS)
        cqn = (cq * rq * qg_ref[...]).astype(BF16)
        qa = _dot(cqn, wq_ref[...])
        qn_ref[...] = qa[:, :512].astype(BF16)
        qp_ref[...] = (qa[:, 512:768] * cos + qa[:, 768:] * sin).astype(BF16)

        ckv = seg(O_CKV, O_MZ)
        ckv_ref[...] = ckv
        rk = lax.rsqrt(jnp.mean(ckv * ckv, axis=-1, keepdims=True) + EPS)
        ckvn = (ckv * rk * kvg_ref[...]).astype(BF16)
        kva = _dot(ckvn, wkv_ref[...])
        kn_ref[...] = kva[:, :512].astype(BF16)
        vv_ref[...] = kva[:, 512:].astype(BF16)

    outs = [
        (D_MODEL, BF16), (512, BF16), (512, BF16), (512, BF16), (512, F32), (Q_RANK, F32), (KV_RANK, F32),
        (512, F32), (D_MODEL, F32), (D_MODEL, F32), (128, BF16), (512, BF16), (256, BF16), (512, BF16), (512, BF16),
    ]
    return pl.pallas_call(
        body, name="inproj", grid=(s_len // tm,),
        out_shape=tuple(_sds((s_len, n), dt) for n, dt in outs),
        in_specs=[_rows(tm, D_MODEL), _whole((1, D_MODEL)), _whole((1, D_MODEL)), _whole((1, D_MODEL)),
                  _whole((W_INT, D_MODEL)), _whole((Q_RANK, 1024)), _whole((KV_RANK, 1024)),
                  _whole((1, Q_RANK)), _whole((1, KV_RANK)), _rows(tm, 256), _rows(tm, 256)],
        out_specs=tuple(_rows(tm, n) for n, _ in outs),
        compiler_params=_params(("parallel",)),
    )(x, shift, scale, g1, w_int, w_q, w_kv, qg, kvg, cos256, sin256)


Z_CLAMP = 80.0


def _softplus_clamped(z):
    zc = jnp.minimum(z, Z_CLAMP)
    return zc, jnp.log(1.0 + jnp.exp(zc))


def _tri_sum(a, tri):
    return _dot(a.astype(BF16), tri)


def _sb_fwd_call(q, k, v):
    s_len = q.shape[0]
    tk = min(ATT_TILE, s_len)
    tq = min(FWD_Q_TILES * ATT_TILE, s_len)
    r = tq // tk
    nq = s_len // tq

    def body(q_ref, k_ref, v_ref, o_ref, lt_ref):
        i = pl.program_id(1)
        q2 = q_ref[...]
        lane = lax.broadcasted_iota(jnp.int32, (1, 256), 1)
        krow = lax.broadcasted_iota(jnp.int32, (tk, tk), 0)
        kcol = lax.broadcasted_iota(jnp.int32, (tk, tk), 1)
        row = lax.broadcasted_iota(jnp.int32, (tq, tk), 0)
        col = lax.broadcasted_iota(jnp.int32, (tq, tk), 1)
        later = (krow > kcol).astype(BF16)
        valids = [col + u * tk < row for u in range(r)]
        hms = [(lane // 64) == hh for hh in range(4)]
        qms = [jnp.where(hm, q2, jnp.zeros_like(q2)) for hm in hms]

        def block(j, carry, valid):
            runs, acc = list(carry[:4]), carry[4]
            off = pl.multiple_of(j * tk, tk)
            kb = k_ref[pl.ds(off, tk), :]
            vb = v_ref[pl.ds(off, tk), :]
            ws = []
            for hh in range(4):
                zc, sp = _softplus_clamped(_dot_nt(qms[hh], kb))
                lm = jnp.where(valid, sp, 0.0) if valid is not None else sp
                suf = _tri_sum(lm, later)
                w = jnp.exp(zc - sp - suf - runs[hh])
                if valid is not None:
                    w = jnp.where(valid, w, 0.0)
                ws.append(w.astype(BF16))
                runs[hh] = runs[hh] + jnp.sum(lm, axis=1, keepdims=True)
            vstack = jnp.concatenate([jnp.where(hm, vb, jnp.zeros_like(vb)) for hm in hms], axis=0)
            acc = acc + _dot(jnp.concatenate(ws, axis=1), vstack)
            return (*runs, acc)

        zero = jnp.zeros((tq, 1), F32)
        carry = (zero, zero, zero, zero, jnp.zeros((tq, 256), F32))
        for u in reversed(range(r)):
            carry = block(i * r + u, carry, valids[u])
        carry = lax.fori_loop(0, i * r, lambda jj, cr: block(i * r - 1 - jj, cr, None), carry)
        for hh in range(4):
            lt_ref[0, :, hh:hh + 1] = carry[hh]
        o_ref[...] = carry[4]

    return pl.pallas_call(
        body, name="sb_fwd", grid=(2, nq),
        out_shape=(_sds((s_len, SB_WIDTH), F32), _sds((2, s_len, 4), F32)),
        in_specs=[pl.BlockSpec((tq, 256), lambda g, i: (i, g)),
                  pl.BlockSpec((s_len, 256), lambda g, i: (0, g)),
                  pl.BlockSpec((s_len, 256), lambda g, i: (0, g))],
        out_specs=(pl.BlockSpec((tq, 256), lambda g, i: (i, g)),
                   pl.BlockSpec((1, tq, 4), lambda g, i: (g, i, 0))),
        compiler_params=_params(("parallel", "parallel")),
    )(q, k, v)


def _sb_bwd_call(q, k, v, do, lt):
    s_len = q.shape[0]
    tk = min(ATT_TILE, s_len)
    tq = min(ATT_Q_TILES * tk, s_len)
    r = tq // tk
    nq = s_len // tq

    def body(q_ref, k_ref, v_ref, do_ref, lt_ref, dq_ref, dk_ref, dv_ref):
        i = pl.program_id(1)

        @pl.when(i == 0)
        def _():
            dk_ref[...] = jnp.zeros_like(dk_ref)
            dv_ref[...] = jnp.zeros_like(dv_ref)

        q2 = q_ref[...]
        do2 = do_ref[...].astype(BF16)
        lane = lax.broadcasted_iota(jnp.int32, (1, 256), 1)
        krow = lax.broadcasted_iota(jnp.int32, (tk, tk), 0)
        kcol = lax.broadcasted_iota(jnp.int32, (tk, tk), 1)
        row = lax.broadcasted_iota(jnp.int32, (tq, tk), 0)
        col = lax.broadcasted_iota(jnp.int32, (tq, tk), 1)
        earlier = (krow < kcol).astype(BF16)
        later = (krow > kcol).astype(BF16)
        valids = [col + u * tk < row for u in range(r)]
        hms = [(lane // 64) == hh for hh in range(4)]
        qms = [jnp.where(hm, q2, jnp.zeros_like(q2)) for hm in hms]
        doms = [jnp.where(hm, do2, jnp.zeros_like(do2)) for hm in hms]
        ltots = [lt_ref[0, :, hh:hh + 1] for hh in range(4)]
        q2t = jnp.transpose(q2.astype(F32))
        do2t = jnp.transpose(do_ref[...])
        subl = lax.broadcasted_iota(jnp.int32, (256, 1), 0)
        qtstack = jnp.concatenate(
            [jnp.where((subl // 64) == hh, q2t, 0.0).astype(BF16) for hh in range(4)], axis=1)
        dotstack = jnp.concatenate(
            [jnp.where((subl // 64) == hh, do2t, 0.0).astype(BF16) for hh in range(4)], axis=1)

        def block(j, carry, valid):
            lpre, ppre, dq = list(carry[0:4]), list(carry[4:8]), carry[8]
            off = pl.multiple_of(j * tk, tk)
            kb = k_ref[pl.ds(off, tk), :]
            vb = v_ref[pl.ds(off, tk), :]
            dzs, avs = [], []
            for hh in range(4):
                zc, sp = _softplus_clamped(_dot_nt(qms[hh], kb))
                lsig = zc - sp
                lm = jnp.where(valid, sp, 0.0) if valid is not None else sp
                rowsum = jnp.sum(lm, axis=1, keepdims=True)
                between = _tri_sum(lm, later) + ((ltots[hh] - lpre[hh]) - rowsum)
                a = jnp.exp(lsig - between)
                if valid is not None:
                    a = jnp.where(valid, a, 0.0)
                p = a * _dot_nt(doms[hh], vb)
                pbefore = ppre[hh] + _tri_sum(p, earlier)
                dz = p - jnp.exp(lsig) * (p + pbefore)
                if valid is not None:
                    dz = jnp.where(valid, dz, 0.0)
                dzs.append(dz.astype(BF16))
                avs.append(a.astype(BF16))
                lpre[hh] = lpre[hh] + rowsum
                ppre[hh] = ppre[hh] + jnp.sum(p, axis=1, keepdims=True)
            kstack = jnp.concatenate([jnp.where(hm, kb, jnp.zeros_like(kb)) for hm in hms], axis=0)
            dq = dq + _dot(jnp.concatenate(dzs, axis=1), kstack)
            dk_ref[:, pl.ds(off, tk)] += _dot(qtstack, jnp.concatenate(dzs, axis=0))
            dv_ref[:, pl.ds(off, tk)] += _dot(dotstack, jnp.concatenate(avs, axis=0))
            return (*lpre, *ppre, dq)

        zero = jnp.zeros((tq, 1), F32)
        carry = lax.fori_loop(0, i * r, lambda j, cr: block(j, cr, None),
                              (zero,) * 8 + (jnp.zeros((tq, 256), F32),))
        for u in range(r):
            carry = block(i * r + u, carry, valids[u])
        dq_ref[...] = carry[8].astype(BF16)

    return pl.pallas_call(
        body, name="sb_bwd", grid=(2, nq),
        out_shape=(_sds((s_len, SB_WIDTH), BF16), _sds((SB_WIDTH, s_len), F32), _sds((SB_WIDTH, s_len), F32)),
        in_specs=[pl.BlockSpec((tq, 256), lambda g, i: (i, g)),
                  pl.BlockSpec((s_len, 256), lambda g, i: (0, g)),
                  pl.BlockSpec((s_len, 256), lambda g, i: (0, g)),
                  pl.BlockSpec((tq, 256), lambda g, i: (i, g)),
                  pl.BlockSpec((1, tq, 4), lambda g, i: (g, i, 0))],
        out_specs=(pl.BlockSpec((tq, 256), lambda g, i: (i, g)),
                   pl.BlockSpec((256, s_len), lambda g, i: (g, 0)),
                   pl.BlockSpec((256, s_len), lambda g, i: (g, 0))),
        compiler_params=_params(("parallel", "arbitrary")),
    )(q, k, v, do, lt)


def _mla_fwd_call(qn, qp, kn, kpt, v):
    s_len = qn.shape[0]
    tk = min(MLA_KEY_TILE, s_len)
    tq = min(FWD_Q_TILES * ATT_TILE, s_len)
    r = tq // tk
    nq = s_len // tq

    def body(qn_ref, qp_ref, kn_ref, kpt_ref, v_ref, o_ref, lse_ref):
        i = pl.program_id(1)
        qn2 = qn_ref[...]
        qp2 = qp_ref[...]
        lane256 = lax.broadcasted_iota(jnp.int32, (1, 256), 1)
        lane128 = lax.broadcasted_iota(jnp.int32, (1, 128), 1)
        krow = lax.broadcasted_iota(jnp.int32, (tk, tk), 0)
        kcol = lax.broadcasted_iota(jnp.int32, (tk, tk), 1)
        row = lax.broadcasted_iota(jnp.int32, (tq, tk), 0)
        col = lax.broadcasted_iota(jnp.int32, (tq, tk), 1)
        valids = [col + u * tk <= row for u in range(r)]
        m64s = [(lane256 // 64) == hh for hh in range(4)]
        half = [(lane128 // 64) == u for u in range(2)]
        m32s = [(lane128 // 32) == hh for hh in range(4)]
        qcs = []
        for hh in range(4):
            qpair = qn2[:, 128 * (hh // 2):128 * (hh // 2) + 128]
            qcs.append(jnp.concatenate([jnp.where(half[hh % 2], qpair, jnp.zeros_like(qpair)),
                                        jnp.where(m32s[hh], qp2, jnp.zeros_like(qp2))], axis=1))

        def by_head(vals):
            return jnp.where(m64s[0], vals[0], jnp.where(m64s[1], vals[1], jnp.where(m64s[2], vals[2], vals[3])))

        def block(j, carry, valid):
            ms, ls, acc = list(carry[0:4]), list(carry[4:8]), carry[8]
            off = pl.multiple_of(j * tk, tk)
            knb = kn_ref[pl.ds(off, tk), :]
            kpb = kpt_ref[pl.ds(off, tk), :]
            vb = v_ref[pl.ds(off, tk), :]
            kcs = [jnp.concatenate([knb[:, 128 * pp:128 * pp + 128], kpb], axis=1) for pp in range(2)]
            ps, alphas = [], []
            for hh in range(4):
                s = _dot_nt(qcs[hh], kcs[hh // 2]) * MLA_SCALE
                if valid is not None:
                    s = jnp.where(valid, s, -1e30)
                mn = jnp.maximum(ms[hh], jnp.max(s, axis=1, keepdims=True))
                p = jnp.exp(s - mn)
                alpha = jnp.exp(ms[hh] - mn)
                ls[hh] = alpha * ls[hh] + jnp.sum(p, axis=1, keepdims=True)
                ms[hh] = mn
                ps.append(p.astype(BF16))
                alphas.append(alpha)
            pvs = []
            for pp in range(2):
                vpair = vb[:, 128 * pp:128 * pp + 128]
                vstack = jnp.concatenate([jnp.where(hf, vpair, jnp.zeros_like(vpair)) for hf in half], axis=0)
                pvs.append(_dot(jnp.concatenate(ps[2 * pp:2 * pp + 2], axis=1), vstack))
            acc = by_head(alphas) * acc + jnp.concatenate(pvs, axis=1)
            return (*ms, *ls, acc)

        neg = jnp.full((tq, 1), -1e30, F32)
        zero = jnp.zeros((tq, 1), F32)
        carry = lax.fori_loop(0, i * r, lambda j, cr: block(j, cr, None),
                              (neg,) * 4 + (zero,) * 4 + (jnp.zeros((tq, 256), F32),))
        for u in range(r):
            carry = block(i * r + u, carry, valids[u])
        o_ref[...] = carry[8] / by_head(list(carry[4:8]))
        for hh in range(4):
            lse_ref[0, :, hh:hh + 1] = carry[hh] + jnp.log(carry[4 + hh])

    return pl.pallas_call(
        body, name="mla_fwd", grid=(2, nq),
        out_shape=(_sds((s_len, MLA_WIDTH), F32), _sds((2, s_len, 4), F32)),
        in_specs=[pl.BlockSpec((tq, 256), lambda g, i: (i, g)),
                  pl.BlockSpec((tq, 128), lambda g, i: (i, g)),
                  pl.BlockSpec((s_len, 256), lambda g, i: (0, g)),
                  pl.BlockSpec((s_len, 128), lambda g, i: (0, 0)),
                  pl.BlockSpec((s_len, 256), lambda g, i: (0, g))],
        out_specs=(pl.BlockSpec((tq, 256), lambda g, i: (i, g)),
                   pl.BlockSpec((1, tq, 4), lambda g, i: (g, i, 0))),
        compiler_params=_params(("parallel", "parallel")),
    )(qn, qp, kn, kpt, v)


def _mla_bwd_call(qn, qp, kn, kpt, v, o, do, lse):
    s_len = qn.shape[0]
    tk = min(MLA_KEY_TILE, s_len)
    tq = min(ATT_Q_TILES * ATT_TILE, s_len)
    r = tq // tk
    nq = s_len // tq

    def body(qn_ref, qp_ref, kn_ref, kpt_ref, v_ref, o_ref, do_ref, lse_ref,
             dqn_ref, dqp_ref, dkn_ref, dkpt_ref, dv_ref):
        g = pl.program_id(0)
        i = pl.program_id(1)

        @pl.when(i == 0)
        def _():
            dkn_ref[...] = jnp.zeros_like(dkn_ref)
            dv_ref[...] = jnp.zeros_like(dv_ref)

        @pl.when((i == 0) & (g == 0))
        def _():
            dkpt_ref[...] = jnp.zeros_like(dkpt_ref)

        qn2 = qn_ref[...]
        qp2 = qp_ref[...]
        dof = do_ref[...]
        dob = dof.astype(BF16)
        prod = dof * o_ref[...]
        lane256 = lax.broadcasted_iota(jnp.int32, (1, 256), 1)
        lane128 = lax.broadcasted_iota(jnp.int32, (1, 128), 1)
        krow = lax.broadcasted_iota(jnp.int32, (tk, tk), 0)
        kcol = lax.broadcasted_iota(jnp.int32, (tk, tk), 1)
        row = lax.broadcasted_iota(jnp.int32, (tq, tk), 0)
        col = lax.broadcasted_iota(jnp.int32, (tq, tk), 1)
        valids = [col + u * tk <= row for u in range(r)]
        m64s = [(lane256 // 64) == hh for hh in range(4)]
        half = [(lane128 // 64) == u for u in range(2)]
        m32s = [(lane128 // 32) == hh for hh in range(4)]
        qcs, doms = [], []
        for hh in range(4):
            sl = slice(128 * (hh // 2), 128 * (hh // 2) + 128)
            qpair = qn2[:, sl]
            dpair = dob[:, sl]
            qcs.append(jnp.concatenate([jnp.where(half[hh % 2], qpair, jnp.zeros_like(qpair)),
                                        jnp.where(m32s[hh], qp2, jnp.zeros_like(qp2))], axis=1))
            doms.append(jnp.where(half[hh % 2], dpair, jnp.zeros_like(dpair)))
        dsums = [jnp.sum(jnp.where(m64, prod, 0.0), axis=1, keepdims=True) * MLA_SCALE for m64 in m64s]
        lses = [lse_ref[0, :, hh:hh + 1] for hh in range(4)]
        qn2t = jnp.transpose(qn2.astype(F32))
        qp2t = jnp.transpose(qp2.astype(F32))
        do2t = jnp.transpose(dof)
        sub128 = lax.broadcasted_iota(jnp.int32, (128, 1), 0)
        qtstacks, dotstacks = [], []
        for pp in range(2):
            qts, dts = [], []
            for u in range(2):
                hh = 2 * pp + u
                qts.append(jnp.concatenate(
                    [jnp.where((sub128 // 64) == u, qn2t[128 * pp:128 * pp + 128, :], 0.0),
                     jnp.where((sub128 // 32) == hh, qp2t, 0.0)], axis=0).astype(BF16))
                dts.append(jnp.where((sub128 // 64) == u, do2t[128 * pp:128 * pp + 128, :], 0.0).astype(BF16))
            qtstacks.append(jnp.concatenate(qts, axis=1))
            dotstacks.append(jnp.concatenate(dts, axis=1))

        def block(j, carry, valid):
            dqn, dqp = carry
            off = pl.multiple_of(j * tk, tk)
            knb = kn_ref[pl.ds(off, tk), :]
            kpb = kpt_ref[pl.ds(off, tk), :]
            vb = v_ref[pl.ds(off, tk), :]
            dqn_parts = []
            dkp = None
            for pp in range(2):
                sl = slice(128 * pp, 128 * pp + 128)
                knp = knb[:, sl]
                vpair = vb[:, sl]
                kc = jnp.concatenate([knp, kpb], axis=1)
                dss, pbs, kcms = [], [], []
                for u in range(2):
                    hh = 2 * pp + u
                    s = _dot_nt(qcs[hh], kc) * MLA_SCALE
                    if valid is not None:
                        s = jnp.where(valid, s, -1e30)
                    p = jnp.exp(s - lses[hh])
                    ds = p * (_dot_nt(doms[hh], vpair) * MLA_SCALE - dsums[hh])
                    dss.append(ds.astype(BF16))
                    pbs.append(p.astype(BF16))
                    kcms.append(jnp.concatenate([jnp.where(half[u], knp, jnp.zeros_like(knp)),
                                                 jnp.where(m32s[hh], kpb, jnp.zeros_like(kpb))], axis=1))
                dqc = _dot(jnp.concatenate(dss, axis=1), jnp.concatenate(kcms, axis=0))
                dqn_parts.append(dqc[:, :128])
                dqp = dqp + dqc[:, 128:]
                dkc = _dot(qtstacks[pp], jnp.concatenate(dss, axis=0))
                dkn_ref[128 * pp:128 * pp + 128, pl.ds(off, tk)] += dkc[:128, :]
                dkp = dkc[128:, :] if dkp is None else dkp + dkc[128:, :]
                dv_ref[128 * pp:128 * pp + 128, pl.ds(off, tk)] += _dot(dotstacks[pp], jnp.concatenate(pbs, axis=0))
            dqn = dqn + jnp.concatenate(dqn_parts, axis=1)
            dkpt_ref[:, pl.ds(off, tk)] += dkp
            return dqn, dqp

        carry = lax.fori_loop(0, i * r, lambda j, cr: block(j, cr, None),
                              (jnp.zeros((tq, 256), F32), jnp.zeros((tq, 128), F32)))
        for u in range(r):
            carry = block(i * r + u, carry, valids[u])
        dqn, dqp = carry
        dqn_ref[...] = dqn.astype(BF16)
        dqp_ref[...] = dqp.astype(BF16)

    return pl.pallas_call(
        body, name="mla_bwd", grid=(2, nq),
        out_shape=(_sds((s_len, 512), BF16), _sds((s_len, 256), BF16), _sds((512, s_len), F32),
                   _sds((128, s_len), F32), _sds((512, s_len), F32)),
        in_specs=[pl.BlockSpec((tq, 256), lambda g, i: (i, g)),
                  pl.BlockSpec((tq, 128), lambda g, i: (i, g)),
                  pl.BlockSpec((s_len, 256), lambda g, i: (0, g)),
                  pl.BlockSpec((s_len, 128), lambda g, i: (0, 0)),
                  pl.BlockSpec((s_len, 256), lambda g, i: (0, g)),
                  pl.BlockSpec((tq, 256), lambda g, i: (i, g)),
                  pl.BlockSpec((tq, 256), lambda g, i: (i, g)),
                  pl.BlockSpec((1, tq, 4), lambda g, i: (g, i, 0))],
        out_specs=(pl.BlockSpec((tq, 256), lambda g, i: (i, g)),
                   pl.BlockSpec((tq, 128), lambda g, i: (i, g)),
                   pl.BlockSpec((256, s_len), lambda g, i: (g, 0)),
                   pl.BlockSpec((128, s_len), lambda g, i: (0, 0)),
                   pl.BlockSpec((256, s_len), lambda g, i: (g, 0))),
        compiler_params=_params(("arbitrary", "arbitrary")),
    )(qn, qp, kn, kpt, v, o, do, lse)


def _post_call(x, tgt, oa, ob, sz, mz, ga, gb, gate, gf, wa, wb, wo, wat, wbt, wot):
    s_len = x.shape[0]
    tm = min(ROW_TILE, s_len)

    def body(x_ref, t_ref, oa_ref, ob_ref, sz_ref, mz_ref, ga_ref, gb_ref, gate_ref, gf_ref,
             wa_ref, wb_ref, wo_ref, wat_ref, wbt_ref, wot_ref,
             dx2_ref, doa_ref, dob_ref, dsz_ref, dmz_ref, dga_ref, dgb_ref,
             dwo_ref, dwa_ref, dwb_ref, dgf_ref, dgate_ref, loss_ref):
        @pl.when(pl.program_id(0) == 0)
        def _():
            dwo_ref[...] = jnp.zeros_like(dwo_ref)
            dwa_ref[...] = jnp.zeros_like(dwa_ref)
            dwb_ref[...] = jnp.zeros_like(dwb_ref)
            dgf_ref[...] = jnp.zeros_like(dgf_ref)
            dgate_ref[...] = jnp.zeros_like(dgate_ref)
            loss_ref[...] = jnp.zeros_like(loss_ref)

        gate = gate_ref[...]
        gf = gf_ref[...]
        oa = oa_ref[...]
        ob = ob_ref[...]
        sz = sz_ref[...]
        mz = mz_ref[...]
        sa = _sigmoid(sz)
        sb = _sigmoid(mz)
        silu_a = sz * sa
        silu_b = mz * sb
        ua = (oa * silu_a).astype(BF16)
        ub = (ob * silu_b).astype(BF16)
        ya = _dot(ua, wa_ref[...])
        yb = _dot(ub, wb_ref[...])
        sga = _sigmoid(ga_ref[...])
        sgb = _sigmoid(gb_ref[...])
        merged = (sga * ya + sgb * yb).astype(BF16)
        out = _dot(merged, wo_ref[...])
        x2 = x_ref[...] + gate * out
        r2 = lax.rsqrt(jnp.mean(x2 * x2, axis=-1, keepdims=True) + EPS)
        xhat = x2 * r2
        err = xhat * gf - t_ref[...]
        loss_ref[...] += 0.5 * jnp.sum(jnp.sum(err * err, axis=1, keepdims=True), axis=0, keepdims=True) / D_MODEL
        dy = err * (1.0 / D_MODEL)
        dgf_ref[...] += jnp.sum(dy * xhat, axis=0, keepdims=True)
        dxhat = dy * gf
        dx2 = r2 * (dxhat - xhat * jnp.mean(dxhat * xhat, axis=-1, keepdims=True))
        dx2_ref[...] = dx2
        dgate_ref[...] += jnp.sum(dx2 * out, axis=0, keepdims=True)
        dout = (dx2 * gate).astype(BF16)
        dmerged = _dot(dout, wot_ref[...])
        dwo_ref[...] += _dot_tn(merged, dout)
        dya = dmerged * sga
        dyb = dmerged * sgb
        dga_ref[...] = (dya * ya * (1.0 - sga)).astype(BF16)
        dgb_ref[...] = (dyb * yb * (1.0 - sgb)).astype(BF16)
        dyab = dya.astype(BF16)
        dybb = dyb.astype(BF16)
        dua = _dot(dyab, wat_ref[...])
        dub = _dot(dybb, wbt_ref[...])
        dwa_ref[...] += _dot_tn(ua, dyab)
        dwb_ref[...] += _dot_tn(ub, dybb)
        doa_ref[...] = dua * silu_a
        dob_ref[...] = dub * silu_b
        dsz_ref[...] = (dua * oa * (sa * (1.0 + sz * (1.0 - sa)))).astype(BF16)
        dmz_ref[...] = (dub * ob * (sb * (1.0 + mz * (1.0 - sb)))).astype(BF16)

    return pl.pallas_call(
        body, name="post", grid=(s_len // tm,),
        out_shape=(_sds((s_len, D_MODEL), F32), _sds((s_len, 512), F32), _sds((s_len, 512), F32),
                   _sds((s_len, 512), BF16), _sds((s_len, 512), BF16),
                   _sds((s_len, D_MODEL), BF16), _sds((s_len, D_MODEL), BF16),
                   _sds((D_MODEL, D_MODEL), F32), _sds((512, D_MODEL), F32), _sds((512, D_MODEL), F32),
                   _sds((1, D_MODEL), F32), _sds((1, D_MODEL), F32), _sds((1, 128), F32)),
        in_specs=[_rows(tm, D_MODEL), _rows(tm, D_MODEL), _rows(tm, 512), _rows(tm, 512), _rows(tm, 512),
                  _rows(tm, 512), _rows(tm, D_MODEL), _rows(tm, D_MODEL), _whole((1, D_MODEL)), _whole((1, D_MODEL)),
                  _whole((512, D_MODEL)), _whole((512, D_MODEL)), _whole((D_MODEL, D_MODEL)),
                  _whole((D_MODEL, 512)), _whole((D_MODEL, 512)), _whole((D_MODEL, D_MODEL))],
        out_specs=(_rows(tm, D_MODEL), _rows(tm, 512), _rows(tm, 512), _rows(tm, 512), _rows(tm, 512),
                   _rows(tm, D_MODEL), _rows(tm, D_MODEL),
                   _whole((D_MODEL, D_MODEL)), _whole((512, D_MODEL)), _whole((512, D_MODEL)),
                   _whole((1, D_MODEL)), _whole((1, D_MODEL)), _whole((1, 128))),
        compiler_params=_params(("arbitrary",)),
    )(x, tgt, oa, ob, sz, mz, ga, gb, gate, gf, wa, wb, wo, wat, wbt, wot)


def _bwdprep_call(dsq, dsk, dsv, dsz, dqn, dqp, dkn, dvv, dkpt, dmz, dga, dgb, cq, ckv, cos256, sin256,
                  qg, kvg, wqt, wkvt):
    s_len = cq.shape[0]
    tm = min(ROW_TILE, s_len)

    def body(dsq_ref, dsk_ref, dsv_ref, dsz_ref, dqn_ref, dqp_ref, dkn_ref, dvv_ref, dkpt_ref, dmz_ref,
             dga_ref, dgb_ref, cq_ref, ckv_ref, cos_ref, sin_ref, qg_ref, kvg_ref, wqt_ref, wkvt_ref,
             dp_ref, dwq_ref, dwkv_ref, dqg_ref, dkvg_ref):
        @pl.when(pl.program_id(0) == 0)
        def _():
            dwq_ref[...] = jnp.zeros_like(dwq_ref)
            dwkv_ref[...] = jnp.zeros_like(dwkv_ref)
            dqg_ref[...] = jnp.zeros_like(dqg_ref)
            dkvg_ref[...] = jnp.zeros_like(dkvg_ref)

        cos = cos_ref[...]
        sin = sin_ref[...]
        dp_ref[:, O_SQ:O_SK] = dsq_ref[...]
        dp_ref[:, O_SK:O_SV] = jnp.transpose(dsk_ref[...]).astype(BF16)
        dp_ref[:, O_SV:O_SZ] = jnp.transpose(dsv_ref[...]).astype(BF16)
        dp_ref[:, O_SZ:O_CQ] = dsz_ref[...]
        dp_ref[:, O_MZ:O_GA] = dmz_ref[...]
        dp_ref[:, O_GA:O_GB] = dga_ref[...]
        dp_ref[:, O_GB:O_KR] = dgb_ref[...]
        dkp = jnp.transpose(dkpt_ref[...])
        dp_ref[:, O_KR:O_KR + 128] = (dkp * cos[:, :128]).astype(BF16)
        dp_ref[:, O_KR + 128:O_END] = (dkp * sin[:, :128]).astype(BF16)
        dp_ref[:, O_END:W_INT] = jnp.zeros((tm, W_INT - O_END), BF16)

        cq = cq_ref[...]
        rq = lax.rsqrt(jnp.mean(cq * cq, axis=-1, keepdims=True) + EPS)
        cqh = cq * rq
        qg = qg_ref[...]
        cqn = (cqh * qg).astype(BF16)
        dqp = dqp_ref[...].astype(F32)
        dqa = jnp.concatenate([dqn_ref[...], (dqp * cos).astype(BF16), (dqp * sin).astype(BF16)], axis=1)
        dcqn = _dot(dqa, wqt_ref[...])
        dwq_ref[...] += _dot_tn(cqn, dqa)
        dqg_ref[...] += jnp.sum(dcqn * cqh, axis=0, keepdims=True)
        dh = dcqn * qg
        dcq = rq * (dh - cqh * jnp.mean(dh * cqh, axis=-1, keepdims=True))
        dp_ref[:, O_CQ:O_CKV] = dcq.astype(BF16)

        ckv = ckv_ref[...]
        rk = lax.rsqrt(jnp.mean(ckv * ckv, axis=-1, keepdims=True) + EPS)
        ckh = ckv * rk
        kvg = kvg_ref[...]
        ckvn = (ckh * kvg).astype(BF16)
        dkva = jnp.concatenate([jnp.transpose(dkn_ref[...]).astype(BF16),
                                jnp.transpose(dvv_ref[...]).astype(BF16)], axis=1)
        dckvn = _dot(dkva, wkvt_ref[...])
        dwkv_ref[...] += _dot_tn(ckvn, dkva)
        dkvg_ref[...] += jnp.sum(dckvn * ckh, axis=0, keepdims=True)
        dh2 = dckvn * kvg
        dckv = rk * (dh2 - ckh * jnp.mean(dh2 * ckh, axis=-1, keepdims=True))
        dp_ref[:, O_CKV:O_MZ] = dckv.astype(BF16)

    return pl.pallas_call(
        body, name="bwdprep", grid=(s_len // tm,),
        out_shape=(_sds((s_len, W_INT), BF16), _sds((Q_RANK, 1024), F32), _sds((KV_RANK, 1024), F32),
                   _sds((1, Q_RANK), F32), _sds((1, KV_RANK), F32)),
        in_specs=[_rows(tm, 512), _cols(512, tm), _cols(512, tm), _rows(tm, 512), _rows(tm, 512), _rows(tm, 256),
                  _cols(512, tm), _cols(512, tm), _cols(128, tm), _rows(tm, 512), _rows(tm, D_MODEL),
                  _rows(tm, D_MODEL), _rows(tm, Q_RANK), _rows(tm, KV_RANK), _rows(tm, 256), _rows(tm, 256),
                  _whole((1, Q_RANK)), _whole((1, KV_RANK)), _whole((1024, Q_RANK)), _whole((1024, KV_RANK))],
        out_specs=(_rows(tm, W_INT), _whole((Q_RANK, 1024)), _whole((KV_RANK, 1024)),
                   _whole((1, Q_RANK)), _whole((1, KV_RANK))),
        compiler_params=_params(("arbitrary",)),
    )(dsq, dsk, dsv, dsz, dqn, dqp, dkn, dvv, dkpt, dmz, dga, dgb, cq, ckv, cos256, sin256, qg, kvg, wqt, wkvt)


def _dh_call(dproj, w_int_t, x, dx2, scale, g1):
    s_len = x.shape[0]
    tm = min(2 * ROW_TILE, s_len)

    def body(dp_ref, wt_ref, x_ref, dx2_ref, sc_ref, g1_ref, gx_ref, dsh_ref, dsc_ref, dg1_ref):
        @pl.when(pl.program_id(0) == 0)
        def _():
            dsh_ref[...] = jnp.zeros_like(dsh_ref)
            dsc_ref[...] = jnp.zeros_like(dsc_ref)
            dg1_ref[...] = jnp.zeros_like(dg1_ref)

        dh = _dot(dp_ref[...], wt_ref[...])
        xt = x_ref[...]
        r = lax.rsqrt(jnp.mean(xt * xt, axis=-1, keepdims=True) + EPS)
        xh = xt * r
        g1 = g1_ref[...]
        xg = xh * g1
        dsh_ref[...] += jnp.sum(dh, axis=0, keepdims=True)
        dsc_ref[...] += jnp.sum(dh * xg, axis=0, keepdims=True)
        dxg = dh * (1.0 + sc_ref[...])
        dg1_ref[...] += jnp.sum(dxg * xh, axis=0, keepdims=True)
        dxh = dxg * g1
        gx_ref[...] = dx2_ref[...] + r * (dxh - xh * jnp.mean(dxh * xh, axis=-1, keepdims=True))

    return pl.pallas_call(
        body, name="dh", grid=(s_len // tm,),
        out_shape=(_sds((s_len, D_MODEL), F32), _sds((1, D_MODEL), F32), _sds((1, D_MODEL), F32),
                   _sds((1, D_MODEL), F32)),
        in_specs=[_rows(tm, W_INT), _whole((W_INT, D_MODEL)), _rows(tm, D_MODEL), _rows(tm, D_MODEL),
                  _whole((1, D_MODEL)), _whole((1, D_MODEL))],
        out_specs=(_rows(tm, D_MODEL), _whole((1, D_MODEL)), _whole((1, D_MODEL)), _whole((1, D_MODEL))),
        compiler_params=_params(("arbitrary",)),
    )(dproj, w_int_t, x, dx2, scale, g1)


def _small_call(svg, ct, dmod_sh):
    def body(sv_ref, ct_ref, dm_ref, tot_ref, gwada_ref):
        acc = sv_ref[0:1, :]
        for d in range(1, N_DEV):
            acc = acc + sv_ref[d:d + 1, :]
        tot_ref[...] = acc
        gwada_ref[...] = lax.dot_general(ct_ref[...], dm_ref[...], (((1,), (0,)), ((), ())),
                                         precision=lax.Precision.HIGHEST, preferred_element_type=F32)

    vmem = pl.BlockSpec(memory_space=pltpu.VMEM)
    return pl.pallas_call(
        body, name="small_grads",
        out_shape=(_sds((1, 8 * SV_COLS), F32), _sds((D_MODEL, 768), F32)),
        in_specs=[vmem, vmem, vmem], out_specs=(vmem, vmem),
        compiler_params=_params(),
    )(svg, ct, dmod_sh)


def _adamw_tile_rows(rows, cols):
    budget = 2 << 20
    if rows * cols * 4 <= budget or rows % 8:
        return rows
    best = 8
    for tr in range(8, rows + 1, 8):
        if rows % tr == 0 and tr * cols * 4 <= budget:
            best = tr
    return best


def _adamw_call(name, w, g, m, v):
    rows, cols = w.shape
    tr = _adamw_tile_rows(rows, cols)

    def body(w_ref, g_ref, m_ref, v_ref, d_ref, nm_ref, nv_ref):
        gg = g_ref[...]
        m2 = ADAM_B1 * m_ref[...] + (1.0 - ADAM_B1) * gg
        v2 = ADAM_B2 * v_ref[...] + (1.0 - ADAM_B2) * (gg * gg)
        m_hat = m2 / (1.0 - ADAM_B1 ** ADAM_STEP)
        v_hat = v2 / (1.0 - ADAM_B2 ** ADAM_STEP)
        d_ref[...] = -ADAM_LR * (m_hat / (jnp.sqrt(v_hat) + ADAM_EPS) + ADAM_WD * w_ref[...])
        nm_ref[...] = m2
        nv_ref[...] = v2

    spec = pl.BlockSpec((tr, cols), lambda i: (i, 0))
    return pl.pallas_call(
        body, name="adamw_" + name, grid=(rows // tr,),
        out_shape=(_sds((rows, cols), F32),) * 3,
        in_specs=[spec] * 4, out_specs=(spec,) * 3,
        compiler_params=_params(("parallel",)),
    )(w, g, m, v)


IN_SHARD = IN_WIDTH // N_CHIPS
HALF_D = D_MODEL // 2
SMALL_ROWS = (576, 512, 1024, 1024, 2048)
SMALL_TOTAL = sum(SMALL_ROWS)
SMALL_HALF = SMALL_TOTAL // 2
SMALL_SUM_ROWS = 432


def _gather_call(c_row, w_ada_sh, pack_in, pack_small):
    def body(c_ref, wada_ref, pki_ref, pks_ref, mg_ref, cg_ref, gwi_ref, gws_ref,
             cv, ssem_c, rsem_c, ssem_m, rsem_m, ssem_w, rsem_w, ssem_f, rsem_f, lsem):
        x, y, c = lax.axis_index("x"), lax.axis_index("y"), lax.axis_index("c")
        me = 4 * x + 2 * y + c
        chip = 2 * x + y
        rel3 = [(1, 0), (0, 1), (1, 1)]
        packs = [(pki_ref, gwi_ref), (pks_ref, gws_ref)]

        sends = []
        for j, (dx, dy) in enumerate(rel3):
            for a, (pk, gw) in enumerate(packs):
                cp = pltpu.make_async_remote_copy(
                    src_ref=pk.at[c], dst_ref=gw.at[chip, c], send_sem=ssem_w.at[j, a], recv_sem=rsem_w.at[j, a],
                    device_id=(_flip(x, dx), _flip(y, dy), c), device_id_type=MESH)
                cp.start()
                sends.append(cp)
        owns = []
        for a, (pk, gw) in enumerate(packs):
            own = pltpu.make_async_copy(pk, gw.at[chip], lsem.at[a])
            own.start()
            owns.append(own)

        cv[me] = c_ref[...]
        for r in range(1, N_DEV):
            dx, dy, dc = (r >> 2) & 1, (r >> 1) & 1, r & 1
            cp = pltpu.make_async_remote_copy(
                src_ref=c_ref, dst_ref=cv.at[me], send_sem=ssem_c.at[r - 1], recv_sem=rsem_c.at[r - 1],
                device_id=(_flip(x, dx), _flip(y, dy), _flip(c, dc)), device_id_type=MESH)
            cp.start()
            sends.append(cp)
        for r in range(1, N_DEV):
            dx, dy, dc = (r >> 2) & 1, (r >> 1) & 1, r & 1
            src = 4 * _flip(x, dx) + 2 * _flip(y, dy) + _flip(c, dc)
            pltpu.make_async_remote_copy(
                src_ref=c_ref, dst_ref=cv.at[src], send_sem=ssem_c.at[r - 1], recv_sem=rsem_c.at[r - 1],
                device_id=(x, y, c), device_id_type=MESH).wait_recv()
        rows = lax.broadcasted_iota(jnp.int32, (N_DEV, D_MODEL), 0)
        call = jnp.zeros((N_DEV, D_MODEL), F32)
        for b in range(N_DEV):
            call = jnp.where(rows == b, jnp.broadcast_to(cv[b], (N_DEV, D_MODEL)), call)
        cg_ref[...] = call

        mg_ref[chip] = lax.dot_general(call, wada_ref[...], (((1,), (0,)), ((), ())),
                                       precision=lax.Precision.HIGHEST, preferred_element_type=F32)
        for j, (dx, dy) in enumerate(rel3):
            cp = pltpu.make_async_remote_copy(
                src_ref=mg_ref.at[chip], dst_ref=mg_ref.at[chip], send_sem=ssem_m.at[j], recv_sem=rsem_m.at[j],
                device_id=(_flip(x, dx), _flip(y, dy), c), device_id_type=MESH)
            cp.start()
            sends.append(cp)
        for j, (dx, dy) in enumerate(rel3):
            src_chip = 2 * _flip(x, dx) + _flip(y, dy)
            pltpu.make_async_remote_copy(
                src_ref=mg_ref.at[src_chip], dst_ref=mg_ref.at[src_chip], send_sem=ssem_m.at[j],
                recv_sem=rsem_m.at[j], device_id=(x, y, c), device_id_type=MESH).wait_recv()
        for j, (dx, dy) in enumerate(rel3):
            src_chip = 2 * _flip(x, dx) + _flip(y, dy)
            for a, (pk, gw) in enumerate(packs):
                pltpu.make_async_remote_copy(
                    src_ref=pk.at[c], dst_ref=gw.at[src_chip, c], send_sem=ssem_w.at[j, a],
                    recv_sem=rsem_w.at[j, a], device_id=(x, y, c), device_id_type=MESH).wait_recv()
                cp = pltpu.make_async_remote_copy(
                    src_ref=gw.at[src_chip, c], dst_ref=gw.at[src_chip, c], send_sem=ssem_f.at[j, a],
                    recv_sem=rsem_f.at[j, a], device_id=(x, y, 1 - c), device_id_type=MESH)
                cp.start()
                sends.append(cp)
        for j, (dx, dy) in enumerate(rel3):
            src_chip = 2 * _flip(x, dx) + _flip(y, dy)
            for a, (pk, gw) in enumerate(packs):
                pltpu.make_async_remote_copy(
                    src_ref=pk.at[c], dst_ref=gw.at[src_chip, 1 - c], send_sem=ssem_f.at[j, a],
                    recv_sem=rsem_f.at[j, a], device_id=(x, y, c), device_id_type=MESH).wait_recv()
        for cp in sends:
            cp.wait_send()
        for own in owns:
            own.wait()

    vmem = pl.BlockSpec(memory_space=pltpu.VMEM)
    return pl.pallas_call(
        body, name="gather_fwd",
        out_shape=(_sds((N_CHIPS, N_DEV, 768), F32), _sds((N_DEV, D_MODEL), F32),
                   _sds((N_CHIPS, 2, IN_SHARD, HALF_D), BF16), _sds((N_CHIPS, 2, SMALL_HALF, LANES), BF16)),
        in_specs=[vmem, vmem, vmem, vmem], out_specs=(vmem, vmem, vmem, vmem),
        scratch_shapes=[
            pltpu.VMEM((N_DEV, 1, D_MODEL), F32),
            pltpu.SemaphoreType.DMA((N_DEV - 1,)), pltpu.SemaphoreType.DMA((N_DEV - 1,)),
            pltpu.SemaphoreType.DMA((3,)), pltpu.SemaphoreType.DMA((3,)),
            pltpu.SemaphoreType.DMA((3, 2)), pltpu.SemaphoreType.DMA((3, 2)),
            pltpu.SemaphoreType.DMA((3, 2)), pltpu.SemaphoreType.DMA((3, 2)),
            pltpu.SemaphoreType.DMA((2,)),
        ],
        compiler_params=_params(),
    )(c_row, w_ada_sh, pack_in, pack_small)


def _reduce_call(g_in, g_small, sv):
    def body(gi_ref, gs_ref, sv_ref, fi_ref, fs_ref, svg_ref, pair_i, pair_s, send_i, send_s, land_i, land_s,
             ssem_p, rsem_p, ssem_g, rsem_g, ssem_s, rsem_s, ssem_x, rsem_x):
        x, y, c = lax.axis_index("x"), lax.axis_index("y"), lax.axis_index("c")
        me = 4 * x + 2 * y + c
        chip = 2 * x + y
        rel3 = [(1, 0), (0, 1), (1, 1)]
        payloads = [(gi_ref, pair_i, send_i, land_i, fi_ref), (gs_ref, pair_s, send_s, land_s, fs_ref)]
        copies = []

        for k in range(N_CHIPS):
            for a, (g, pair, _, _, _) in enumerate(payloads):
                cp = pltpu.make_async_remote_copy(
                    src_ref=g.at[2 * k + 1 - c], dst_ref=pair.at[k], send_sem=ssem_p.at[k, a],
                    recv_sem=rsem_p.at[k, a], device_id=(x, y, 1 - c), device_id_type=MESH)
                cp.start()
                copies.append(cp)

        for r in range(1, N_DEV):
            dx, dy, dc = (r >> 2) & 1, (r >> 1) & 1, r & 1
            cp = pltpu.make_async_remote_copy(
                src_ref=sv_ref, dst_ref=svg_ref.at[me], send_sem=ssem_s.at[r - 1], recv_sem=rsem_s.at[r - 1],
                device_id=(_flip(x, dx), _flip(y, dy), _flip(c, dc)), device_id_type=MESH)
            cp.start()
            copies.append(cp)
        svg_ref[me] = sv_ref[...]

        for k in range(N_CHIPS):
            for a, (g, pair, _, _, _) in enumerate(payloads):
                pltpu.make_async_remote_copy(
                    src_ref=g.at[2 * k + c], dst_ref=pair.at[k], send_sem=ssem_p.at[k, a],
                    recv_sem=rsem_p.at[k, a], device_id=(x, y, c), device_id_type=MESH).wait_recv()

        def pair_sum(k, store_in, store_small):
            for qd in range(HALF_D // LANES):
                sl = slice(LANES * qd, LANES * qd + LANES)
                store_in(sl, gi_ref[2 * k + c, :, sl].astype(F32) + pair_i[k, :, sl].astype(F32))

            def rows(i, carry):
                sl = pl.ds(pl.multiple_of(i * SMALL_SUM_ROWS, 16), SMALL_SUM_ROWS)
                store_small(sl, gs_ref[2 * k + c, sl, :].astype(F32) + pair_s[k, sl, :].astype(F32))
                return carry

            lax.fori_loop(0, SMALL_HALF // SMALL_SUM_ROWS, rows, 0)

        for j, (dx, dy) in enumerate(rel3):
            tx, ty = _flip(x, dx), _flip(y, dy)

            def put_in(sl, val, j=j):
                send_i[j, :, sl] = val.astype(BF16)

            def put_small(sl, val, j=j):
                send_s[j, sl, :] = val.astype(BF16)

            pair_sum(2 * tx + ty, put_in, put_small)
            for a, (_, _, send, land, _) in enumerate(payloads):
                cp = pltpu.make_async_remote_copy(
                    src_ref=send.at[j], dst_ref=land.at[j], send_sem=ssem_g.at[j, a], recv_sem=rsem_g.at[j, a],
                    device_id=(tx, ty, c), device_id_type=MESH)
                cp.start()
                copies.append(cp)

        def own_in(sl, val):
            fi_ref[c, :, sl] = val

        def own_small(sl, val):
            fs_ref[c, sl, :] = val

        pair_sum(chip, own_in, own_small)
        for j in range(3):
            for a, (_, _, send, land, _) in enumerate(payloads):
                pltpu.make_async_remote_copy(
                    src_ref=send.at[j], dst_ref=land.at[j], send_sem=ssem_g.at[j, a], recv_sem=rsem_g.at[j, a],
                    device_id=(x, y, c), device_id_type=MESH).wait_recv()
            for qd in range(HALF_D // LANES):
                sl = slice(LANES * qd, LANES * qd + LANES)
                fi_ref[c, :, sl] += land_i[j, :, sl].astype(F32)

            def add_rows(i, carry, j=j):
                sl = pl.ds(pl.multiple_of(i * SMALL_SUM_ROWS, 16), SMALL_SUM_ROWS)
                fs_ref[c, sl, :] += land_s[j, sl, :].astype(F32)
                return carry

            lax.fori_loop(0, SMALL_HALF // SMALL_SUM_ROWS, add_rows, 0)

        for a, f in enumerate((fi_ref, fs_ref)):
            cp = pltpu.make_async_remote_copy(
                src_ref=f.at[c], dst_ref=f.at[c], send_sem=ssem_x.at[a], recv_sem=rsem_x.at[a],
                device_id=(x, y, 1 - c), device_id_type=MESH)
            cp.start()
            copies.append(cp)
        for a, f in enumerate((fi_ref, fs_ref)):
            pltpu.make_async_remote_copy(
                src_ref=f.at[c], dst_ref=f.at[1 - c], send_sem=ssem_x.at[a], recv_sem=rsem_x.at[a],
                device_id=(x, y, c), device_id_type=MESH).wait_recv()
        for r in range(1, N_DEV):
            dx, dy, dc = (r >> 2) & 1, (r >> 1) & 1, r & 1
            src = 4 * _flip(x, dx) + 2 * _flip(y, dy) + _flip(c, dc)
            pltpu.make_async_remote_copy(
                src_ref=sv_ref, dst_ref=svg_ref.at[src], send_sem=ssem_s.at[r - 1],
                recv_sem=rsem_s.at[r - 1], device_id=(x, y, c), device_id_type=MESH).wait_recv()
        for cp in copies:
            cp.wait_send()

    vmem = pl.BlockSpec(memory_space=pltpu.VMEM)
    return pl.pallas_call(
        body, name="grad_reduce",
        out_shape=(_sds((2, IN_SHARD, HALF_D), F32), _sds((2, SMALL_HALF, LANES), F32),
                   _sds((N_DEV, 8, SV_COLS), F32)),
        in_specs=[vmem, vmem, vmem], out_specs=(vmem, vmem, vmem),
        scratch_shapes=[
            pltpu.VMEM((N_CHIPS, IN_SHARD, HALF_D), BF16), pltpu.VMEM((N_CHIPS, SMALL_HALF, LANES), BF16),
            pltpu.VMEM((3, IN_SHARD, HALF_D), BF16), pltpu.VMEM((3, SMALL_HALF, LANES), BF16),
            pltpu.VMEM((3, IN_SHARD, HALF_D), BF16), pltpu.VMEM((3, SMALL_HALF, LANES), BF16),
            pltpu.SemaphoreType.DMA((N_CHIPS, 2)), pltpu.SemaphoreType.DMA((N_CHIPS, 2)),
            pltpu.SemaphoreType.DMA((3, 2)), pltpu.SemaphoreType.DMA((3, 2)),
            pltpu.SemaphoreType.DMA((N_DEV - 1,)), pltpu.SemaphoreType.DMA((N_DEV - 1,)),
            pltpu.SemaphoreType.DMA((2,)), pltpu.SemaphoreType.DMA((2,)),
        ],
        compiler_params=_params(),
    )(g_in, g_small, sv)


def _dwin_call(h, dproj):
    s_len = h.shape[0]
    tm = min(4 * ROW_TILE, s_len)
    nrow = s_len // tm
    nc = 4
    chunk = W_INT // nc

    def body(h_ref, dp_ref, dw_ref, acc):
        i = pl.program_id(1)

        @pl.when(i == 0)
        def _():
            acc[...] = jnp.zeros_like(acc)

        acc[...] += _dot_tn(dp_ref[...], h_ref[...])

        @pl.when(i == nrow - 1)
        def _():
            dw_ref[...] = acc[...].astype(BF16)

    return pl.pallas_call(
        body, name="dwin", grid=(nc, nrow),
        out_shape=_sds((W_INT, D_MODEL), BF16),
        in_specs=[pl.BlockSpec((tm, D_MODEL), lambda c, i: (i, 0)),
                  pl.BlockSpec((tm, chunk), lambda c, i: (i, c))],
        out_specs=pl.BlockSpec((chunk, D_MODEL), lambda c, i: (c, 0)),
        scratch_shapes=[pltpu.VMEM((chunk, D_MODEL), F32)],
        compiler_params=_params(("parallel", "arbitrary")),
    )(h, dproj)


def _swap_rows(w, group):
    r, n = w.shape
    return w.reshape(r // group, 2, group // 2, n)[:, ::-1].reshape(r, n)


def _internal_weights(w_in_t, w_uq, w_ukv):
    krot_t = w_in_t[2688:2720]
    w_int_t = jnp.concatenate([
        w_in_t[0:512] * jnp.asarray(0.125, w_in_t.dtype), w_in_t[512:2048],
        w_in_t[2048:2432], w_in_t[2432:2688], w_in_t[2720:3232], w_in_t[3232:4256], w_in_t[4256:5280],
        jnp.tile(krot_t, (4, 1)), jnp.tile(_swap_rows(krot_t, 32), (4, 1)),
        jnp.zeros((W_INT - O_END, D_MODEL), w_in_t.dtype)], axis=0)
    uq = w_uq.reshape(Q_RANK, N_HEADS, 96)
    wp = uq[:, :, 64:].reshape(Q_RANK, 256)
    w_q = jnp.concatenate([uq[:, :, :64].reshape(Q_RANK, 512), wp, _swap_halves(wp, 32)], axis=1)
    ukv = w_ukv.reshape(KV_RANK, N_HEADS, 128)
    w_kv = jnp.concatenate([ukv[:, :, :64].reshape(KV_RANK, 512), ukv[:, :, 64:].reshape(KV_RANK, 512)], axis=1)
    return w_int_t, w_q, w_kv


def _true_weight_grads(dwi_t, dwq, dwkv):
    dkr = dwi_t[O_KR:O_KR + 128].astype(F32).reshape(4, 32, D_MODEL).sum(axis=0)
    dkr_sw = dwi_t[O_KR + 128:O_END].astype(F32).reshape(4, 32, D_MODEL).sum(axis=0)
    dkrot_t = (dkr + _swap_rows(dkr_sw, 32)).astype(dwi_t.dtype)
    g_in_t = jnp.concatenate([
        dwi_t[0:512] * jnp.asarray(0.125, dwi_t.dtype), dwi_t[512:2048], dwi_t[O_CQ:O_CKV], dwi_t[O_CKV:O_MZ],
        dkrot_t, dwi_t[O_MZ:O_GA], dwi_t[O_GA:O_GB], dwi_t[O_GB:O_KR]], axis=0)
    dwp = dwq[:, 512:768] + _swap_halves(dwq[:, 768:1024], 32)
    g_uq = jnp.concatenate([dwq[:, :512].reshape(Q_RANK, N_HEADS, 64), dwp.reshape(Q_RANK, N_HEADS, 32)],
                           axis=2).reshape(Q_RANK, 768)
    g_ukv = jnp.concatenate([dwkv[:, :512].reshape(KV_RANK, N_HEADS, 64), dwkv[:, 512:].reshape(KV_RANK, N_HEADS, 64)],
                            axis=2).reshape(KV_RANK, 1024)
    return g_in_t, g_uq, g_ukv


def _swap_halves(w, group):
    r, n = w.shape
    return w.reshape(r, n // group, 2, group // 2)[:, :, ::-1, :].reshape(r, n)


def _pack_shards(parts):
    return jnp.concatenate([p.reshape(-1, LANES) for p in parts], axis=0)


def _unpack_small(gw):
    offs = [0]
    for r in SMALL_ROWS:
        offs.append(offs[-1] + r)

    def cols(i, rows, shard_cols):
        blk = gw[:, offs[i]:offs[i + 1]].reshape(N_CHIPS, rows, shard_cols)
        return blk.transpose(1, 0, 2).reshape(rows, N_CHIPS * shard_cols)

    return (cols(0, Q_RANK, 192), cols(1, KV_RANK, 256), cols(2, 512, 256), cols(3, 512, 256),
            gw[:, offs[4]:offs[5]].reshape(D_MODEL, D_MODEL))


def _chip_major(g, shard_cols):
    r = g.shape[0]
    return g.reshape(r, N_CHIPS, shard_cols).transpose(1, 0, 2).reshape(N_CHIPS, -1, LANES)


def kernel(x, c, positions, w_ada, b_ada, norm_gain, w_in, q_norm_gain, w_uq, kv_norm_gain, w_ukv, w_branch_a, w_branch_b, w_out, final_norm_gain, loss_target, m_w_ada, m_b_ada, m_norm_gain, m_w_in, m_q_norm_gain, m_w_uq, m_kv_norm_gain, m_w_ukv, m_w_branch_a, m_w_branch_b, m_w_out, m_final_norm_gain, v_w_ada, v_b_ada, v_norm_gain, v_w_in, v_q_norm_gain, v_w_uq, v_kv_norm_gain, v_w_ukv, v_w_branch_a, v_w_branch_b, v_w_out, v_final_norm_gain):
    ix, iy, ic = lax.axis_index("x"), lax.axis_index("y"), lax.axis_index("c")
    me = 4 * ix + 2 * iy + ic
    chip = 2 * ix + iy
    xs = x[0]
    tgt = loss_target[0]
    s_len = xs.shape[0]

    w_in_t = jnp.swapaxes(w_in[0], 0, 1)
    w_in_tb = w_in_t.astype(BF16)
    pack_in = jnp.stack([w_in_tb[:, :HALF_D], w_in_tb[:, HALF_D:]], axis=0)
    small_shards = (w_uq[0], w_ukv[0], w_branch_a[0], w_branch_b[0], w_out[0])
    pack_small = _pack_shards([s.astype(BF16) for s in small_shards]).reshape(2, SMALL_HALF, LANES)
    mg, call, gw_in, gw_small = _gather_call(c, w_ada[0], pack_in, pack_small)
    mod = mg.transpose(1, 0, 2).reshape(N_DEV, 3 * D_MODEL) + b_ada
    mod_me = lax.dynamic_slice_in_dim(mod, me, 1, axis=0)
    shift, scale, gate = mod_me[:, :D_MODEL], mod_me[:, D_MODEL:2 * D_MODEL], mod_me[:, 2 * D_MODEL:]

    f_in_t = jnp.concatenate([gw_in[:, 0], gw_in[:, 1]], axis=2).reshape(IN_WIDTH, D_MODEL)
    f_uq, f_ukv, f_a, f_b, f_out = _unpack_small(gw_small.reshape(N_CHIPS, SMALL_TOTAL, LANES))
    w_int_t, w_q, w_kv = _internal_weights(f_in_t, f_uq, f_ukv)

    inv_freq = ROPE_BASE ** (-jnp.arange(0, ROPE_DIM, 2, dtype=F32) / ROPE_DIM)
    ang = positions[0].astype(F32)[:, None] * inv_freq
    cs, sn = jnp.cos(ang), jnp.sin(ang)
    cos256 = jnp.tile(jnp.concatenate([cs, cs], axis=1), (1, 8))
    sin256 = jnp.tile(jnp.concatenate([-sn, sn], axis=1), (1, 8))

    (h, sq, sk, sv, sz, cq, ckv, mz, ga, gb, kpt, qn, qp, kn, vv) = _inproj_call(
        xs, shift, scale, norm_gain, w_int_t, w_q, w_kv, q_norm_gain, kv_norm_gain, cos256, sin256)
    oa, lt = _sb_fwd_call(sq, sk, sv)
    ob, lse = _mla_fwd_call(qn, qp, kn, kpt, vv)

    gf = final_norm_gain.reshape(1, D_MODEL)
    (dx2, doa, dob, dsz, dmz, dga, dgb, dwo, dwa, dwb, dgf, dgate, loss_p) = _post_call(
        xs, tgt, oa, ob, sz, mz, ga, gb, gate, gf, f_a, f_b, f_out, f_a.T, f_b.T, f_out.T)

    dsq, dsk_t, dsv_t = _sb_bwd_call(sq, sk, sv, doa, lt)
    dqn, dqp, dkn_t, dkpt_t, dvv_t = _mla_bwd_call(qn, qp, kn, kpt, vv, ob, dob, lse)

    dproj, dwq, dwkv, dqg, dkvg = _bwdprep_call(
        dsq, dsk_t, dsv_t, dsz, dqn, dqp, dkn_t, dvv_t, dkpt_t, dmz, dga, dgb, cq, ckv, cos256, sin256,
        q_norm_gain, kv_norm_gain, w_q.T, w_kv.T)
    grad_x, dshift, dscale, dg1 = _dh_call(dproj, w_int_t, xs, dx2, scale, norm_gain)
    dwi_t = _dwin_call(h, dproj)
    g_in_t, g_uq, g_ukv = _true_weight_grads(dwi_t, dwq, dwkv)

    g_in_c = g_in_t.reshape(N_CHIPS, IN_SHARD, D_MODEL)
    g_in_pieces = jnp.stack([g_in_c[:, :, :HALF_D], g_in_c[:, :, HALF_D:]], axis=1).reshape(N_DEV, IN_SHARD, HALF_D)
    g_small = jnp.concatenate([
        _chip_major(g_uq, 192), _chip_major(g_ukv, 256), _chip_major(dwa, 256), _chip_major(dwb, 256),
        dwo.reshape(N_CHIPS, -1, LANES)], axis=1).astype(BF16).reshape(N_DEV, SMALL_HALF, LANES)
    small = jnp.concatenate([
        dshift, dscale, dgate, dg1, dqg, dkvg, dgf, loss_p,
        jnp.zeros((1, 8 * SV_COLS - 5888), F32)], axis=1).reshape(8, SV_COLS)
    full_in, full_small, svg = _reduce_call(g_in_pieces, g_small, small)
    gs_in_t = jnp.concatenate([full_in[0], full_in[1]], axis=1)
    full = full_small.reshape(SMALL_TOTAL, LANES)
    offs = [0]
    for r in SMALL_ROWS:
        offs.append(offs[-1] + r)
    gs_uq = full[offs[0]:offs[1]].reshape(Q_RANK, 192)
    gs_ukv = full[offs[1]:offs[2]].reshape(KV_RANK, 256)
    gs_a = full[offs[2]:offs[3]].reshape(512, 256)
    gs_b = full[offs[3]:offs[4]].reshape(512, 256)
    gs_out = full[offs[4]:offs[5]].reshape(256, D_MODEL)

    svm = svg.reshape(N_DEV, 8 * SV_COLS)
    dmod_sh = lax.dynamic_slice_in_dim(svm[:, :3 * D_MODEL], chip * 768, 768, axis=1)
    tot, gs_ada = _small_call(svm, call.T, dmod_sh)
    g_bada = tot[:, 0:3072]
    g_g1 = tot[:, 3072:4096]
    g_qg = tot[:, 4096:4480]
    g_kvg = tot[:, 4480:4736]
    g_gf = tot[:, 4736:5760]
    loss = tot[0, 5760]

    names = ["w_ada", "b_ada", "norm_gain", "w_in", "q_norm_gain", "w_uq", "kv_norm_gain", "w_ukv",
             "w_branch_a", "w_branch_b", "w_out", "final_norm_gain"]
    ws = [w_ada[0], b_ada, norm_gain, w_in_t, q_norm_gain, w_uq[0], kv_norm_gain, w_ukv[0],
          w_branch_a[0], w_branch_b[0], w_out[0], final_norm_gain.reshape(1, D_MODEL)]
    gs = [gs_ada, g_bada, g_g1, gs_in_t, g_qg, gs_uq, g_kvg, gs_ukv, gs_a, gs_b, gs_out, g_gf]
    ms = [m_w_ada[0], m_b_ada, m_norm_gain, jnp.swapaxes(m_w_in[0], 0, 1), m_q_norm_gain, m_w_uq[0],
          m_kv_norm_gain, m_w_ukv[0], m_w_branch_a[0], m_w_branch_b[0], m_w_out[0],
          m_final_norm_gain.reshape(1, D_MODEL)]
    vs = [v_w_ada[0], v_b_ada, v_norm_gain, jnp.swapaxes(v_w_in[0], 0, 1), v_q_norm_gain, v_w_uq[0],
          v_kv_norm_gain, v_w_ukv[0], v_w_branch_a[0], v_w_branch_b[0], v_w_out[0],
          v_final_norm_gain.reshape(1, D_MODEL)]
    refs = [w_ada, b_ada, norm_gain, w_in, q_norm_gain, w_uq, kv_norm_gain, w_ukv,
            w_branch_a, w_branch_b, w_out, final_norm_gain]
    grads, deltas, new_ms, new_vs = [], [], [], []
    for n, w_, g_, m_, v_, ref in zip(names, ws, gs, ms, vs, refs):
        outs = (g_,) + _adamw_call(n, w_, g_, m_, v_)
        if n == "w_in":
            outs = tuple(jnp.swapaxes(o_, 0, 1) for o_ in outs)
        for lst, o_ in zip((grads, deltas, new_ms, new_vs), outs):
            lst.append(o_.reshape(ref.shape))

    return (loss, grad_x.reshape(x.shape), *grads, *deltas, *new_ms, *new_vs)
```

```python
import math

import jax
import jax.numpy as jnp
from jax import lax
from jax.experimental import pallas as pl
from jax.experimental.pallas import tpu as pltpu

F32 = jnp.float32
BF16 = jnp.bfloat16

D_MODEL = 1024
SB_WIDTH = 512
MLA_WIDTH = 512
Q_RANK = 384
KV_RANK = 256
ROPE_DIM = 32
N_HEADS = 8
IN_WIDTH = 5280
EPS = 1e-6
ROPE_BASE = 10000.0
MLA_SCALE = 1.0 / math.sqrt(96.0)

ADAM_LR = 0.001
ADAM_B1 = 0.9
ADAM_B2 = 0.999
ADAM_EPS = 1e-08
ADAM_WD = 0.01
ADAM_STEP = 10

O_SQ, O_SK, O_SV, O_SZ, O_CQ, O_CKV, O_MZ, O_GA, O_GB, O_KR, O_END = (
    0, 512, 1024, 1536, 2048, 2432, 2688, 3200, 4224, 5248, 5504)
W_INT = 5632

N_CHIPS = 4
N_DEV = 8
LANES = 128
SV_COLS = 768

ROW_TILE = 256
ATT_TILE = 256
ATT_Q_TILES = 2
FWD_Q_TILES = 4
MLA_KEY_TILE = 512
VMEM_LIMIT = 56 * 1024 * 1024

MESH = pl.DeviceIdType.MESH


def _dot(a, b):
    return lax.dot_general(a, b, (((1,), (0,)), ((), ())), preferred_element_type=F32)


def _dot_nt(a, b):
    return lax.dot_general(a, b, (((1,), (1,)), ((), ())), preferred_element_type=F32)


def _dot_tn(a, b):
    return lax.dot_general(a, b, (((0,), (0,)), ((), ())), preferred_element_type=F32)


def _sigmoid(z):
    return 1.0 / (1.0 + jnp.exp(-z))


def _params(sem=None):
    if sem is None:
        return pltpu.CompilerParams(vmem_limit_bytes=VMEM_LIMIT)
    return pltpu.CompilerParams(dimension_semantics=sem, vmem_limit_bytes=VMEM_LIMIT)


def _rows(tm, n):
    return pl.BlockSpec((tm, n), lambda i: (i, 0))


def _cols(n, tm):
    return pl.BlockSpec((n, tm), lambda i: (0, i))


def _whole(shape):
    nd = len(shape)
    return pl.BlockSpec(shape, lambda i: (0,) * nd)


def _sds(shape, dtype):
    return jax.ShapeDtypeStruct(shape, dtype)


def _flip(v, d):
    return 1 - v if d else v


def _inproj_call(x, shift, scale, g1, w_int, w_q, w_kv, qg, kvg, cos256, sin256):
    s_len = x.shape[0]
    tm = min(ROW_TILE, s_len)

    def body(x_ref, sh_ref, sc_ref, g1_ref, w_ref, wq_ref, wkv_ref, qg_ref, kvg_ref, cos_ref, sin_ref,
             h_ref, sq_ref, sk_ref, sv_ref, sz_ref, cq_ref, ckv_ref, mz_ref, ga_ref, gb_ref, kpt_ref,
             qn_ref, qp_ref, kn_ref, vv_ref):
        xt = x_ref[...]
        r = lax.rsqrt(jnp.mean(xt * xt, axis=-1, keepdims=True) + EPS)
        h = (xt * r * g1_ref[...]) * (1.0 + sc_ref[...]) + sh_ref[...]
        hb = h.astype(BF16)
        h_ref[...] = hb

        def seg(a, b):
            return _dot_nt(hb, w_ref[a:b, :])

        sq_ref[...] = seg(O_SQ, O_SK).astype(BF16)
        sk_ref[...] = seg(O_SK, O_SV).astype(BF16)
        sv_ref[...] = seg(O_SV, O_SZ).astype(BF16)
        sz_ref[...] = seg(O_SZ, O_CQ)
        mz_ref[...] = seg(O_MZ, O_GA)
        ga_ref[...] = seg(O_GA, O_GB)
        gb_ref[...] = seg(O_GB, O_KR)
        cos = cos_ref[...]
        sin = sin_ref[...]
        kr = seg(O_KR, O_END)
        kpt_ref[...] = (kr[:, :128] * cos[:, :128] + kr[:, 128:] * sin[:, :128]).astype(BF16)

        cq = seg(O_CQ, O_CKV)
        cq_ref[...] = cq
        rq = lax.rsqrt(jnp.mean(cq * cq, axis=-1, keepdims=True) + EPS)
        cqn = (cq * rq * qg_ref[...]).astype(BF16)
        qa = _dot(cqn, wq_ref[...])
        qn_ref[...] = qa[:, :512].astype(BF16)
        qp_ref[...] = (qa[:, 512:768] * cos + qa[:, 768:] * sin).astype(BF16)

        ckv = seg(O_CKV, O_MZ)
        ckv_ref[...] = ckv
        rk = lax.rsqrt(jnp.mean(ckv * ckv, axis=-1, keepdims=True) + EPS)
        ckvn = (ckv * rk * kvg_ref[...]).astype(BF16)
        kva = _dot(ckvn, wkv_ref[...])
        kn_ref[...] = kva[:, :512].astype(BF16)
        vv_ref[...] = kva[:, 512:].astype(BF16)

    outs = [
        (D_MODEL, BF16), (512, BF16), (512, BF16), (512, BF16), (512, F32), (Q_RANK, F32), (KV_RANK, F32),
        (512, F32), (D_MODEL, F32), (D_MODEL, F32), (128, BF16), (512, BF16), (256, BF16), (512, BF16), (512, BF16),
    ]
    return pl.pallas_call(
        body, name="inproj", grid=(s_len // tm,),
        out_shape=tuple(_sds((s_len, n), dt) for n, dt in outs),
        in_specs=[_rows(tm, D_MODEL), _whole((1, D_MODEL)), _whole((1, D_MODEL)), _whole((1, D_MODEL)),
                  _whole((W_INT, D_MODEL)), _whole((Q_RANK, 1024)), _whole((KV_RANK, 1024)),
                  _whole((1, Q_RANK)), _whole((1, KV_RANK)), _rows(tm, 256), _rows(tm, 256)],
        out_specs=tuple(_rows(tm, n) for n, _ in outs),
        compiler_params=_params(("parallel",)),
    )(x, shift, scale, g1, w_int, w_q, w_kv, qg, kvg, cos256, sin256)


Z_CLAMP = 80.0
RUN_CUTOFF = 110.0


def _softplus_clamped(z):
    zc = jnp.minimum(z, Z_CLAMP)
    return zc, jnp.log(1.0 + jnp.exp(zc))


def _tri_sum(a, tri):
    return _dot(a.astype(BF16), tri)


def _sb_fwd_call(q, k, v):
    s_len = q.shape[0]
    tk = min(ATT_TILE, s_len)
    tq = min(FWD_Q_TILES * ATT_TILE, s_len)
    r = tq // tk
    nq = s_len // tq

    def body(q_ref, k_ref, v_ref, o_ref, lt_ref, first_ref):
        i = pl.program_id(1)
        q2 = q_ref[...]
        lane = lax.broadcasted_iota(jnp.int32, (1, 256), 1)
        krow = lax.broadcasted_iota(jnp.int32, (tk, tk), 0)
        kcol = lax.broadcasted_iota(jnp.int32, (tk, tk), 1)
        row = lax.broadcasted_iota(jnp.int32, (tq, tk), 0)
        col = lax.broadcasted_iota(jnp.int32, (tq, tk), 1)
        later = (krow > kcol).astype(BF16)
        valids = [col + u * tk < row for u in range(r)]
        hms = [(lane // 64) == hh for hh in range(4)]
        qms = [jnp.where(hm, q2, jnp.zeros_like(q2)) for hm in hms]

        def block(j, carry, valid):
            runs, acc = list(carry[:4]), carry[4]
            off = pl.multiple_of(j * tk, tk)
            kb = k_ref[pl.ds(off, tk), :]
            vb = v_ref[pl.ds(off, tk), :]
            ws = []
            for hh in range(4):
                zc, sp = _softplus_clamped(_dot_nt(qms[hh], kb))
                lm = jnp.where(valid, sp, 0.0) if valid is not None else sp
                suf = _tri_sum(lm, later)
                w = jnp.exp(zc - sp - suf - runs[hh])
                if valid is not None:
                    w = jnp.where(valid, w, 0.0)
                ws.append(w.astype(BF16))
                runs[hh] = runs[hh] + jnp.sum(lm, axis=1, keepdims=True)
            vstack = jnp.concatenate([jnp.where(hm, vb, jnp.zeros_like(vb)) for hm in hms], axis=0)
            acc = acc + _dot(jnp.concatenate(ws, axis=1), vstack)
            return (*runs, acc)

        zero = jnp.zeros((tq, 1), F32)
        carry = (zero, zero, zero, zero, jnp.zeros((tq, 256), F32))
        for u in reversed(range(r)):
            carry = block(i * r + u, carry, valids[u])

        def least_run(runs):
            return jnp.min(jnp.minimum(jnp.minimum(runs[0], runs[1]), jnp.minimum(runs[2], runs[3])))

        n_full = i * r

        def unfinished(state):
            return jnp.logical_and(state[0] < n_full, state[1] <= RUN_CUTOFF)

        def visit(state):
            cr = block(n_full - 1 - state[0], state[2:], None)
            return (state[0] + 1, least_run(cr[:4]), *cr)

        state = lax.while_loop(unfinished, visit, (jnp.int32(0), least_run(carry[:4]), *carry))
        carry = state[2:]
        first_ref[...] = jnp.full(first_ref.shape, n_full - state[0], jnp.int32)
        for hh in range(4):
            lt_ref[0, :, hh:hh + 1] = carry[hh]
        o_ref[...] = carry[4]

    return pl.pallas_call(
        body, name="sb_fwd", grid=(2, nq),
        out_shape=(_sds((s_len, SB_WIDTH), F32), _sds((2, s_len, 4), F32), _sds((2, nq, 8, 128), jnp.int32)),
        in_specs=[pl.BlockSpec((tq, 256), lambda g, i: (i, g)),
                  pl.BlockSpec((s_len, 256), lambda g, i: (0, g)),
                  pl.BlockSpec((s_len, 256), lambda g, i: (0, g))],
        out_specs=(pl.BlockSpec((tq, 256), lambda g, i: (i, g)),
                   pl.BlockSpec((1, tq, 4), lambda g, i: (g, i, 0)),
                   pl.BlockSpec((1, 1, 8, 128), lambda g, i: (g, i, 0, 0))),
        compiler_params=_params(("parallel", "parallel")),
    )(q, k, v)


def _sb_bwd_call(first, q, k, v, do, lt):
    s_len = q.shape[0]
    tk = min(ATT_TILE, s_len)
    tq = min(ATT_Q_TILES * tk, s_len)
    r = tq // tk
    nq = s_len // tq
    nq_fwd = first.shape[0] // 2
    per_fwd = nq // nq_fwd

    def body(first_ref, q_ref, k_ref, v_ref, do_ref, lt_ref, dq_ref, dk_ref, dv_ref):
        g = pl.program_id(0)
        i = pl.program_id(1)

        @pl.when(i == 0)
        def _():
            dk_ref[...] = jnp.zeros_like(dk_ref)
            dv_ref[...] = jnp.zeros_like(dv_ref)

        q2 = q_ref[...]
        do2 = do_ref[...].astype(BF16)
        lane = lax.broadcasted_iota(jnp.int32, (1, 256), 1)
        krow = lax.broadcasted_iota(jnp.int32, (tk, tk), 0)
        kcol = lax.broadcasted_iota(jnp.int32, (tk, tk), 1)
        row = lax.broadcasted_iota(jnp.int32, (tq, tk), 0)
        col = lax.broadcasted_iota(jnp.int32, (tq, tk), 1)
        earlier = (krow < kcol).astype(BF16)
        later = (krow > kcol).astype(BF16)
        valids = [col + u * tk < row for u in range(r)]
        hms = [(lane // 64) == hh for hh in range(4)]
        qms = [jnp.where(hm, q2, jnp.zeros_like(q2)) for hm in hms]
        doms = [jnp.where(hm, do2, jnp.zeros_like(do2)) for hm in hms]
        ltots = [lt_ref[0, :, hh:hh + 1] for hh in range(4)]
        q2t = jnp.transpose(q2.astype(F32))
        do2t = jnp.transpose(do_ref[...])
        subl = lax.broadcasted_iota(jnp.int32, (256, 1), 0)
        qtstack = jnp.concatenate(
            [jnp.where((subl // 64) == hh, q2t, 0.0).astype(BF16) for hh in range(4)], axis=1)
        dotstack = jnp.concatenate(
            [jnp.where((subl // 64) == hh, do2t, 0.0).astype(BF16) for hh in range(4)], axis=1)

        def block(j, carry, valid):
            lpre, ppre, dq = list(carry[0:4]), list(carry[4:8]), carry[8]
            off = pl.multiple_of(j * tk, tk)
            kb = k_ref[pl.ds(off, tk), :]
            vb = v_ref[pl.ds(off, tk), :]
            dzs, avs = [], []
            for hh in range(4):
                zc, sp = _softplus_clamped(_dot_nt(qms[hh], kb))
                lsig = zc - sp
                lm = jnp.where(valid, sp, 0.0) if valid is not None else sp
                rowsum = jnp.sum(lm, axis=1, keepdims=True)
                between = _tri_sum(lm, later) + ((ltots[hh] - lpre[hh]) - rowsum)
                a = jnp.exp(lsig - between)
                if valid is not None:
                    a = jnp.where(valid, a, 0.0)
                p = a * _dot_nt(doms[hh], vb)
                pbefore = ppre[hh] + _tri_sum(p, earlier)
                dz = p - jnp.exp(lsig) * (p + pbefore)
                if valid is not None:
                    dz = jnp.where(valid, dz, 0.0)
                dzs.append(dz.astype(BF16))
                avs.append(a.astype(BF16))
                lpre[hh] = lpre[hh] + rowsum
                ppre[hh] = ppre[hh] + jnp.sum(p, axis=1, keepdims=True)
            kstack = jnp.concatenate([jnp.where(hm, kb, jnp.zeros_like(kb)) for hm in hms], axis=0)
            dq = dq + _dot(jnp.concatenate(dzs, axis=1), kstack)
            dk_ref[:, pl.ds(off, tk)] += _dot(qtstack, jnp.concatenate(dzs, axis=0))
            dv_ref[:, pl.ds(off, tk)] += _dot(dotstack, jnp.concatenate(avs, axis=0))
            return (*lpre, *ppre, dq)

        zero = jnp.zeros((tq, 1), F32)
        start = jnp.minimum(first_ref[g * nq_fwd + i // per_fwd], i * r)
        carry = lax.fori_loop(start, i * r, lambda j, cr: block(j, cr, None),
                              (zero,) * 8 + (jnp.zeros((tq, 256), F32),))
        for u in range(r):
            carry = block(i * r + u, carry, valids[u])
        dq_ref[...] = carry[8].astype(BF16)

    return pl.pallas_call(
        body, name="sb_bwd",
        out_shape=(_sds((s_len, SB_WIDTH), BF16), _sds((SB_WIDTH, s_len), F32), _sds((SB_WIDTH, s_len), F32)),
        grid_spec=pltpu.PrefetchScalarGridSpec(
            num_scalar_prefetch=1, grid=(2, nq),
            in_specs=[pl.BlockSpec((tq, 256), lambda g, i, f: (i, g)),
                      pl.BlockSpec((s_len, 256), lambda g, i, f: (0, g)),
                      pl.BlockSpec((s_len, 256), lambda g, i, f: (0, g)),
                      pl.BlockSpec((tq, 256), lambda g, i, f: (i, g)),
                      pl.BlockSpec((1, tq, 4), lambda g, i, f: (g, i, 0))],
            out_specs=(pl.BlockSpec((tq, 256), lambda g, i, f: (i, g)),
                       pl.BlockSpec((256, s_len), lambda g, i, f: (g, 0)),
                       pl.BlockSpec((256, s_len), lambda g, i, f: (g, 0)))),
        compiler_params=_params(("parallel", "arbitrary")),
    )(first, q, k, v, do, lt)


def _mla_fwd_call(qn, qp, kn, kpt, v):
    s_len = qn.shape[0]
    tk = min(MLA_KEY_TILE, s_len)
    tq = min(FWD_Q_TILES * ATT_TILE, s_len)
    r = tq // tk
    nq = s_len // tq

    def body(qn_ref, qp_ref, kn_ref, kpt_ref, v_ref, o_ref, lse_ref):
        i = pl.program_id(1)
        qn2 = qn_ref[...]
        qp2 = qp_ref[...]
        lane256 = lax.broadcasted_iota(jnp.int32, (1, 256), 1)
        lane128 = lax.broadcasted_iota(jnp.int32, (1, 128), 1)
        krow = lax.broadcasted_iota(jnp.int32, (tk, tk), 0)
        kcol = lax.broadcasted_iota(jnp.int32, (tk, tk), 1)
        row = lax.broadcasted_iota(jnp.int32, (tq, tk), 0)
        col = lax.broadcasted_iota(jnp.int32, (tq, tk), 1)
        valids = [col + u * tk <= row for u in range(r)]
        m64s = [(lane256 // 64) == hh for hh in range(4)]
        half = [(lane128 // 64) == u for u in range(2)]
        m32s = [(lane128 // 32) == hh for hh in range(4)]
        qcs = []
        for hh in range(4):
            qpair = qn2[:, 128 * (hh // 2):128 * (hh // 2) + 128]
            qcs.append(jnp.concatenate([jnp.where(half[hh % 2], qpair, jnp.zeros_like(qpair)),
                                        jnp.where(m32s[hh], qp2, jnp.zeros_like(qp2))], axis=1))

        def by_head(vals):
            return jnp.where(m64s[0], vals[0], jnp.where(m64s[1], vals[1], jnp.where(m64s[2], vals[2], vals[3])))

        def block(j, carry, valid):
            ms, ls, acc = list(carry[0:4]), list(carry[4:8]), carry[8]
            off = pl.multiple_of(j * tk, tk)
            knb = kn_ref[pl.ds(off, tk), :]
            kpb = kpt_ref[pl.ds(off, tk), :]
            vb = v_ref[pl.ds(off, tk), :]
            kcs = [jnp.concatenate([knb[:, 128 * pp:128 * pp + 128], kpb], axis=1) for pp in range(2)]
            ps, alphas = [], []
            for hh in range(4):
                s = _dot_nt(qcs[hh], kcs[hh // 2]) * MLA_SCALE
                if valid is not None:
                    s = jnp.where(valid, s, -1e30)
                mn = jnp.maximum(ms[hh], jnp.max(s, axis=1, keepdims=True))
                p = jnp.exp(s - mn)
                alpha = jnp.exp(ms[hh] - mn)
                ls[hh] = alpha * ls[hh] + jnp.sum(p, axis=1, keepdims=True)
                ms[hh] = mn
                ps.append(p.astype(BF16))
                alphas.append(alpha)
            pvs = []
            for pp in range(2):
                vpair = vb[:, 128 * pp:128 * pp + 128]
                vstack = jnp.concatenate([jnp.where(hf, vpair, jnp.zeros_like(vpair)) for hf in half], axis=0)
                pvs.append(_dot(jnp.concatenate(ps[2 * pp:2 * pp + 2], axis=1), vstack))
            acc = by_head(alphas) * acc + jnp.concatenate(pvs, axis=1)
            return (*ms, *ls, acc)

        neg = jnp.full((tq, 1), -1e30, F32)
        zero = jnp.zeros((tq, 1), F32)
        carry = lax.fori_loop(0, i * r, lambda j, cr: block(j, cr, None),
                              (neg,) * 4 + (zero,) * 4 + (jnp.zeros((tq, 256), F32),))
        for u in range(r):
            carry = block(i * r + u, carry, valids[u])
        o_ref[...] = carry[8] / by_head(list(carry[4:8]))
        for hh in range(4):
            lse_ref[0, :, hh:hh + 1] = carry[hh] + jnp.log(carry[4 + hh])

    return pl.pallas_call(
        body, name="mla_fwd", grid=(2, nq),
        out_shape=(_sds((s_len, MLA_WIDTH), F32), _sds((2, s_len, 4), F32)),
        in_specs=[pl.BlockSpec((tq, 256), lambda g, i: (i, g)),
                  pl.BlockSpec((tq, 128), lambda g, i: (i, g)),
                  pl.BlockSpec((s_len, 256), lambda g, i: (0, g)),
                  pl.BlockSpec((s_len, 128), lambda g, i: (0, 0)),
                  pl.BlockSpec((s_len, 256), lambda g, i: (0, g))],
        out_specs=(pl.BlockSpec((tq, 256), lambda g, i: (i, g)),
                   pl.BlockSpec((1, tq, 4), lambda g, i: (g, i, 0))),
        compiler_params=_params(("parallel", "parallel")),
    )(qn, qp, kn, kpt, v)


def _mla_bwd_call(qn, qp, kn, kpt, v, o, do, lse):
    s_len = qn.shape[0]
    tk = min(MLA_KEY_TILE, s_len)
    tq = min(ATT_Q_TILES * ATT_TILE, s_len)
    r = tq // tk
    nq = s_len // tq

    def body(qn_ref, qp_ref, kn_ref, kpt_ref, v_ref, o_ref, do_ref, lse_ref,
             dqn_ref, dqp_ref, dkn_ref, dkpt_ref, dv_ref):
        g = pl.program_id(0)
        i = pl.program_id(1)

        @pl.when(i == 0)
        def _():
            dkn_ref[...] = jnp.zeros_like(dkn_ref)
            dv_ref[...] = jnp.zeros_like(dv_ref)

        @pl.when((i == 0) & (g == 0))
        def _():
            dkpt_ref[...] = jnp.zeros_like(dkpt_ref)

        qn2 = qn_ref[...]
        qp2 = qp_ref[...]
        dof = do_ref[...]
        dob = dof.astype(BF16)
        prod = dof * o_ref[...]
        lane256 = lax.broadcasted_iota(jnp.int32, (1, 256), 1)
        lane128 = lax.broadcasted_iota(jnp.int32, (1, 128), 1)
        krow = lax.broadcasted_iota(jnp.int32, (tk, tk), 0)
        kcol = lax.broadcasted_iota(jnp.int32, (tk, tk), 1)
        row = lax.broadcasted_iota(jnp.int32, (tq, tk), 0)
        col = lax.broadcasted_iota(jnp.int32, (tq, tk), 1)
        valids = [col + u * tk <= row for u in range(r)]
        m64s = [(lane256 // 64) == hh for hh in range(4)]
        half = [(lane128 // 64) == u for u in range(2)]
        m32s = [(lane128 // 32) == hh for hh in range(4)]
        qcs, doms = [], []
        for hh in range(4):
            sl = slice(128 * (hh // 2), 128 * (hh // 2) + 128)
            qpair = qn2[:, sl]
            dpair = dob[:, sl]
            qcs.append(jnp.concatenate([jnp.where(half[hh % 2], qpair, jnp.zeros_like(qpair)),
                                        jnp.where(m32s[hh], qp2, jnp.zeros_like(qp2))], axis=1))
            doms.append(jnp.where(half[hh % 2], dpair, jnp.zeros_like(dpair)))
        dsums = [jnp.sum(jnp.where(m64, prod, 0.0), axis=1, keepdims=True) * MLA_SCALE for m64 in m64s]
        lses = [lse_ref[0, :, hh:hh + 1] for hh in range(4)]
        qn2t = jnp.transpose(qn2.astype(F32))
        qp2t = jnp.transpose(qp2.astype(F32))
        do2t = jnp.transpose(dof)
        sub128 = lax.broadcasted_iota(jnp.int32, (128, 1), 0)
        qtstacks, dotstacks = [], []
        for pp in range(2):
            qts, dts = [], []
            for u in range(2):
                hh = 2 * pp + u
                qts.append(jnp.concatenate(
                    [jnp.where((sub128 // 64) == u, qn2t[128 * pp:128 * pp + 128, :], 0.0),
                     jnp.where((sub128 // 32) == hh, qp2t, 0.0)], axis=0).astype(BF16))
                dts.append(jnp.where((sub128 // 64) == u, do2t[128 * pp:128 * pp + 128, :], 0.0).astype(BF16))
            qtstacks.append(jnp.concatenate(qts, axis=1))
            dotstacks.append(jnp.concatenate(dts, axis=1))

        def block(j, carry, valid):
            dqn, dqp = carry
            off = pl.multiple_of(j * tk, tk)
            knb = kn_ref[pl.ds(off, tk), :]
            kpb = kpt_ref[pl.ds(off, tk), :]
            vb = v_ref[pl.ds(off, tk), :]
            dqn_parts = []
            dkp = None
            for pp in range(2):
                sl = slice(128 * pp, 128 * pp + 128)
                knp = knb[:, sl]
                vpair = vb[:, sl]
                kc = jnp.concatenate([knp, kpb], axis=1)
                dss, pbs, kcms = [], [], []
                for u in range(2):
                    hh = 2 * pp + u
                    s = _dot_nt(qcs[hh], kc) * MLA_SCALE
                    if valid is not None:
                        s = jnp.where(valid, s, -1e30)
                    p = jnp.exp(s - lses[hh])
                    ds = p * (_dot_nt(doms[hh], vpair) * MLA_SCALE - dsums[hh])
                    dss.append(ds.astype(BF16))
                    pbs.append(p.astype(BF16))
                    kcms.append(jnp.concatenate([jnp.where(half[u], knp, jnp.zeros_like(knp)),
                                                 jnp.where(m32s[hh], kpb, jnp.zeros_like(kpb))], axis=1))
                dqc = _dot(jnp.concatenate(dss, axis=1), jnp.concatenate(kcms, axis=0))
                dqn_parts.append(dqc[:, :128])
                dqp = dqp + dqc[:, 128:]
                dkc = _dot(qtstacks[pp], jnp.concatenate(dss, axis=0))
                dkn_ref[128 * pp:128 * pp + 128, pl.ds(off, tk)] += dkc[:128, :]
                dkp = dkc[128:, :] if dkp is None else dkp + dkc[128:, :]
                dv_ref[128 * pp:128 * pp + 128, pl.ds(off, tk)] += _dot(dotstacks[pp], jnp.concatenate(pbs, axis=0))
            dqn = dqn + jnp.concatenate(dqn_parts, axis=1)
            dkpt_ref[:, pl.ds(off, tk)] += dkp
            return dqn, dqp

        carry = lax.fori_loop(0, i * r, lambda j, cr: block(j, cr, None),
                              (jnp.zeros((tq, 256), F32), jnp.zeros((tq, 128), F32)))
        for u in range(r):
            carry = block(i * r + u, carry, valids[u])
        dqn, dqp = carry
        dqn_ref[...] = dqn.astype(BF16)
        dqp_ref[...] = dqp.astype(BF16)

    return pl.pallas_call(
        body, name="mla_bwd", grid=(2, nq),
        out_shape=(_sds((s_len, 512), BF16), _sds((s_len, 256), BF16), _sds((512, s_len), F32),
                   _sds((128, s_len), F32), _sds((512, s_len), F32)),
        in_specs=[pl.BlockSpec((tq, 256), lambda g, i: (i, g)),
                  pl.BlockSpec((tq, 128), lambda g, i: (i, g)),
                  pl.BlockSpec((s_len, 256), lambda g, i: (0, g)),
                  pl.BlockSpec((s_len, 128), lambda g, i: (0, 0)),
                  pl.BlockSpec((s_len, 256), lambda g, i: (0, g)),
                  pl.BlockSpec((tq, 256), lambda g, i: (i, g)),
                  pl.BlockSpec((tq, 256), lambda g, i: (i, g)),
                  pl.BlockSpec((1, tq, 4), lambda g, i: (g, i, 0))],
        out_specs=(pl.BlockSpec((tq, 256), lambda g, i: (i, g)),
                   pl.BlockSpec((tq, 128), lambda g, i: (i, g)),
                   pl.BlockSpec((256, s_len), lambda g, i: (g, 0)),
                   pl.BlockSpec((128, s_len), lambda g, i: (0, 0)),
                   pl.BlockSpec((256, s_len), lambda g, i: (g, 0))),
        compiler_params=_params(("arbitrary", "arbitrary")),
    )(qn, qp, kn, kpt, v, o, do, lse)


def _post_call(x, tgt, oa, ob, sz, mz, ga, gb, gate, gf, wa, wb, wo, wat, wbt, wot):
    s_len = x.shape[0]
    tm = min(ROW_TILE, s_len)

    def body(x_ref, t_ref, oa_ref, ob_ref, sz_ref, mz_ref, ga_ref, gb_ref, gate_ref, gf_ref,
             wa_ref, wb_ref, wo_ref, wat_ref, wbt_ref, wot_ref,
             dx2_ref, doa_ref, dob_ref, dsz_ref, dmz_ref, dga_ref, dgb_ref,
             dwo_ref, dwa_ref, dwb_ref, dgf_ref, dgate_ref, loss_ref):
        @pl.when(pl.program_id(0) == 0)
        def _():
            dwo_ref[...] = jnp.zeros_like(dwo_ref)
            dwa_ref[...] = jnp.zeros_like(dwa_ref)
            dwb_ref[...] = jnp.zeros_like(dwb_ref)
            dgf_ref[...] = jnp.zeros_like(dgf_ref)
            dgate_ref[...] = jnp.zeros_like(dgate_ref)
            loss_ref[...] = jnp.zeros_like(loss_ref)

        gate = gate_ref[...]
        gf = gf_ref[...]
        oa = oa_ref[...]
        ob = ob_ref[...]
        sz = sz_ref[...]
        mz = mz_ref[...]
        sa = _sigmoid(sz)
        sb = _sigmoid(mz)
        silu_a = sz * sa
        silu_b = mz * sb
        ua = (oa * silu_a).astype(BF16)
        ub = (ob * silu_b).astype(BF16)
        ya = _dot(ua, wa_ref[...])
        yb = _dot(ub, wb_ref[...])
        sga = _sigmoid(ga_ref[...])
        sgb = _sigmoid(gb_ref[...])
        merged = (sga * ya + sgb * yb).astype(BF16)
        out = _dot(merged, wo_ref[...])
        x2 = x_ref[...] + gate * out
        r2 = lax.rsqrt(jnp.mean(x2 * x2, axis=-1, keepdims=True) + EPS)
        xhat = x2 * r2
        err = xhat * gf - t_ref[...]
        loss_ref[...] += 0.5 * jnp.sum(jnp.sum(err * err, axis=1, keepdims=True), axis=0, keepdims=True) / D_MODEL
        dy = err * (1.0 / D_MODEL)
        dgf_ref[...] += jnp.sum(dy * xhat, axis=0, keepdims=True)
        dxhat = dy * gf
        dx2 = r2 * (dxhat - xhat * jnp.mean(dxhat * xhat, axis=-1, keepdims=True))
        dx2_ref[...] = dx2
        dgate_ref[...] += jnp.sum(dx2 * out, axis=0, keepdims=True)
        dout = (dx2 * gate).astype(BF16)
        dmerged = _dot(dout, wot_ref[...])
        dwo_ref[...] += _dot_tn(merged, dout)
        dya = dmerged * sga
        dyb = dmerged * sgb
        dga_ref[...] = (dya * ya * (1.0 - sga)).astype(BF16)
        dgb_ref[...] = (dyb * yb * (1.0 - sgb)).astype(BF16)
        dyab = dya.astype(BF16)
        dybb = dyb.astype(BF16)
        dua = _dot(dyab, wat_ref[...])
        dub = _dot(dybb, wbt_ref[...])
        dwa_ref[...] += _dot_tn(ua, dyab)
        dwb_ref[...] += _dot_tn(ub, dybb)
        doa_ref[...] = dua * silu_a
        dob_ref[...] = dub * silu_b
        dsz_ref[...] = (dua * oa * (sa * (1.0 + sz * (1.0 - sa)))).astype(BF16)
        dmz_ref[...] = (dub * ob * (sb * (1.0 + mz * (1.0 - sb)))).astype(BF16)

    return pl.pallas_call(
        body, name="post", grid=(s_len // tm,),
        out_shape=(_sds((s_len, D_MODEL), F32), _sds((s_len, 512), F32), _sds((s_len, 512), F32),
                   _sds((s_len, 512), BF16), _sds((s_len, 512), BF16),
                   _sds((s_len, D_MODEL), BF16), _sds((s_len, D_MODEL), BF16),
                   _sds((D_MODEL, D_MODEL), F32), _sds((512, D_MODEL), F32), _sds((512, D_MODEL), F32),
                   _sds((1, D_MODEL), F32), _sds((1, D_MODEL), F32), _sds((1, 128), F32)),
        in_specs=[_rows(tm, D_MODEL), _rows(tm, D_MODEL), _rows(tm, 512), _rows(tm, 512), _rows(tm, 512),
                  _rows(tm, 512), _rows(tm, D_MODEL), _rows(tm, D_MODEL), _whole((1, D_MODEL)), _whole((1, D_MODEL)),
                  _whole((512, D_MODEL)), _whole((512, D_MODEL)), _whole((D_MODEL, D_MODEL)),
                  _whole((D_MODEL, 512)), _whole((D_MODEL, 512)), _whole((D_MODEL, D_MODEL))],
        out_specs=(_rows(tm, D_MODEL), _rows(tm, 512), _rows(tm, 512), _rows(tm, 512), _rows(tm, 512),
                   _rows(tm, D_MODEL), _rows(tm, D_MODEL),
                   _whole((D_MODEL, D_MODEL)), _whole((512, D_MODEL)), _whole((512, D_MODEL)),
                   _whole((1, D_MODEL)), _whole((1, D_MODEL)), _whole((1, 128))),
        compiler_params=_params(("arbitrary",)),
    )(x, tgt, oa, ob, sz, mz, ga, gb, gate, gf, wa, wb, wo, wat, wbt, wot)


def _bwdprep_call(dsq, dsk, dsv, dsz, dqn, dqp, dkn, dvv, dkpt, dmz, dga, dgb, cq, ckv, cos256, sin256,
                  qg, kvg, wqt, wkvt):
    s_len = cq.shape[0]
    tm = min(ROW_TILE, s_len)

    def body(dsq_ref, dsk_ref, dsv_ref, dsz_ref, dqn_ref, dqp_ref, dkn_ref, dvv_ref, dkpt_ref, dmz_ref,
             dga_ref, dgb_ref, cq_ref, ckv_ref, cos_ref, sin_ref, qg_ref, kvg_ref, wqt_ref, wkvt_ref,
             dp_ref, dwq_ref, dwkv_ref, dqg_ref, dkvg_ref):
        @pl.when(pl.program_id(0) == 0)
        def _():
            dwq_ref[...] = jnp.zeros_like(dwq_ref)
            dwkv_ref[...] = jnp.zeros_like(dwkv_ref)
            dqg_ref[...] = jnp.zeros_like(dqg_ref)
            dkvg_ref[...] = jnp.zeros_like(dkvg_ref)

        cos = cos_ref[...]
        sin = sin_ref[...]
        dp_ref[:, O_SQ:O_SK] = dsq_ref[...]
        dp_ref[:, O_SK:O_SV] = jnp.transpose(dsk_ref[...]).astype(BF16)
        dp_ref[:, O_SV:O_SZ] = jnp.transpose(dsv_ref[...]).astype(BF16)
        dp_ref[:, O_SZ:O_CQ] = dsz_ref[...]
        dp_ref[:, O_MZ:O_GA] = dmz_ref[...]
        dp_ref[:, O_GA:O_GB] = dga_ref[...]
        dp_ref[:, O_GB:O_KR] = dgb_ref[...]
        dkp = jnp.transpose(dkpt_ref[...])
        dp_ref[:, O_KR:O_KR + 128] = (dkp * cos[:, :128]).astype(BF16)
        dp_ref[:, O_KR + 128:O_END] = (dkp * sin[:, :128]).astype(BF16)
        dp_ref[:, O_END:W_INT] = jnp.zeros((tm, W_INT - O_END), BF16)

        cq = cq_ref[...]
        rq = lax.rsqrt(jnp.mean(cq * cq, axis=-1, keepdims=True) + EPS)
        cqh = cq * rq
        qg = qg_ref[...]
        cqn = (cqh * qg).astype(BF16)
        dqp = dqp_ref[...].astype(F32)
        dqa = jnp.concatenate([dqn_ref[...], (dqp * cos).astype(BF16), (dqp * sin).astype(BF16)], axis=1)
        dcqn = _dot(dqa, wqt_ref[...])
        dwq_ref[...] += _dot_tn(cqn, dqa)
        dqg_ref[...] += jnp.sum(dcqn * cqh, axis=0, keepdims=True)
        dh = dcqn * qg
        dcq = rq * (dh - cqh * jnp.mean(dh * cqh, axis=-1, keepdims=True))
        dp_ref[:, O_CQ:O_CKV] = dcq.astype(BF16)

        ckv = ckv_ref[...]
        rk = lax.rsqrt(jnp.mean(ckv * ckv, axis=-1, keepdims=True) + EPS)
        ckh = ckv * rk
        kvg = kvg_ref[...]
        ckvn = (ckh * kvg).astype(BF16)
        dkva = jnp.concatenate([jnp.transpose(dkn_ref[...]).astype(BF16),
                                jnp.transpose(dvv_ref[...]).astype(BF16)], axis=1)
        dckvn = _dot(dkva, wkvt_ref[...])
        dwkv_ref[...] += _dot_tn(ckvn, dkva)
        dkvg_ref[...] += jnp.sum(dckvn * ckh, axis=0, keepdims=True)
        dh2 = dckvn * kvg
        dckv = rk * (dh2 - ckh * jnp.mean(dh2 * ckh, axis=-1, keepdims=True))
        dp_ref[:, O_CKV:O_MZ] = dckv.astype(BF16)

    return pl.pallas_call(
        body, name="bwdprep", grid=(s_len // tm,),
        out_shape=(_sds((s_len, W_INT), BF16), _sds((Q_RANK, 1024), F32), _sds((KV_RANK, 1024), F32),
                   _sds((1, Q_RANK), F32), _sds((1, KV_RANK), F32)),
        in_specs=[_rows(tm, 512), _cols(512, tm), _cols(512, tm), _rows(tm, 512), _rows(tm, 512), _rows(tm, 256),
                  _cols(512, tm), _cols(512, tm), _cols(128, tm), _rows(tm, 512), _rows(tm, D_MODEL),
                  _rows(tm, D_MODEL), _rows(tm, Q_RANK), _rows(tm, KV_RANK), _rows(tm, 256), _rows(tm, 256),
                  _whole((1, Q_RANK)), _whole((1, KV_RANK)), _whole((1024, Q_RANK)), _whole((1024, KV_RANK))],
        out_specs=(_rows(tm, W_INT), _whole((Q_RANK, 1024)), _whole((KV_RANK, 1024)),
                   _whole((1, Q_RANK)), _whole((1, KV_RANK))),
        compiler_params=_params(("arbitrary",)),
    )(dsq, dsk, dsv, dsz, dqn, dqp, dkn, dvv, dkpt, dmz, dga, dgb, cq, ckv, cos256, sin256, qg, kvg, wqt, wkvt)


def _dh_call(dproj, w_int_t, x, dx2, scale, g1):
    s_len = x.shape[0]
    tm = min(2 * ROW_TILE, s_len)

    def body(dp_ref, wt_ref, x_ref, dx2_ref, sc_ref, g1_ref, gx_ref, dsh_ref, dsc_ref, dg1_ref):
        @pl.when(pl.program_id(0) == 0)
        def _():
            dsh_ref[...] = jnp.zeros_like(dsh_ref)
            dsc_ref[...] = jnp.zeros_like(dsc_ref)
            dg1_ref[...] = jnp.zeros_like(dg1_ref)

        dh = _dot(dp_ref[...], wt_ref[...])
        xt = x_ref[...]
        r = lax.rsqrt(jnp.mean(xt * xt, axis=-1, keepdims=True) + EPS)
        xh = xt * r
        g1 = g1_ref[...]
        xg = xh * g1
        dsh_ref[...] += jnp.sum(dh, axis=0, keepdims=True)
        dsc_ref[...] += jnp.sum(dh * xg, axis=0, keepdims=True)
        dxg = dh * (1.0 + sc_ref[...])
        dg1_ref[...] += jnp.sum(dxg * xh, axis=0, keepdims=True)
        dxh = dxg * g1
        gx_ref[...] = dx2_ref[...] + r * (dxh - xh * jnp.mean(dxh * xh, axis=-1, keepdims=True))

    return pl.pallas_call(
        body, name="dh", grid=(s_len // tm,),
        out_shape=(_sds((s_len, D_MODEL), F32), _sds((1, D_MODEL), F32), _sds((1, D_MODEL), F32),
                   _sds((1, D_MODEL), F32)),
        in_specs=[_rows(tm, W_INT), _whole((W_INT, D_MODEL)), _rows(tm, D_MODEL), _rows(tm, D_MODEL),
                  _whole((1, D_MODEL)), _whole((1, D_MODEL))],
        out_specs=(_rows(tm, D_MODEL), _whole((1, D_MODEL)), _whole((1, D_MODEL)), _whole((1, D_MODEL))),
        compiler_params=_params(("arbitrary",)),
    )(dproj, w_int_t, x, dx2, scale, g1)


def _small_call(svg, ct, dmod_sh):
    def body(sv_ref, ct_ref, dm_ref, tot_ref, gwada_ref):
        acc = sv_ref[0:1, :]
        for d in range(1, N_DEV):
            acc = acc + sv_ref[d:d + 1, :]
        tot_ref[...] = acc
        gwada_ref[...] = lax.dot_general(ct_ref[...], dm_ref[...], (((1,), (0,)), ((), ())),
                                         precision=lax.Precision.HIGHEST, preferred_element_type=F32)

    vmem = pl.BlockSpec(memory_space=pltpu.VMEM)
    return pl.pallas_call(
        body, name="small_grads",
        out_shape=(_sds((1, 8 * SV_COLS), F32), _sds((D_MODEL, 768), F32)),
        in_specs=[vmem, vmem, vmem], out_specs=(vmem, vmem),
        compiler_params=_params(),
    )(svg, ct, dmod_sh)


def _adamw_tile_rows(rows, cols):
    budget = 2 << 20
    if rows * cols * 4 <= budget or rows % 8:
        return rows
    best = 8
    for tr in range(8, rows + 1, 8):
        if rows % tr == 0 and tr * cols * 4 <= budget:
            best = tr
    return best


def _adamw_call(name, w, g, m, v):
    rows, cols = w.shape
    tr = _adamw_tile_rows(rows, cols)

    def body(w_ref, g_ref, m_ref, v_ref, d_ref, nm_ref, nv_ref):
        gg = g_ref[...]
        m2 = ADAM_B1 * m_ref[...] + (1.0 - ADAM_B1) * gg
        v2 = ADAM_B2 * v_ref[...] + (1.0 - ADAM_B2) * (gg * gg)
        m_hat = m2 / (1.0 - ADAM_B1 ** ADAM_STEP)
        v_hat = v2 / (1.0 - ADAM_B2 ** ADAM_STEP)
        d_ref[...] = -ADAM_LR * (m_hat / (jnp.sqrt(v_hat) + ADAM_EPS) + ADAM_WD * w_ref[...])
        nm_ref[...] = m2
        nv_ref[...] = v2

    spec = pl.BlockSpec((tr, cols), lambda i: (i, 0))
    return pl.pallas_call(
        body, name="adamw_" + name, grid=(rows // tr,),
        out_shape=(_sds((rows, cols), F32),) * 3,
        in_specs=[spec] * 4, out_specs=(spec,) * 3,
        compiler_params=_params(("parallel",)),
    )(w, g, m, v)


IN_SHARD = IN_WIDTH // N_CHIPS
HALF_D = D_MODEL // 2
SMALL_ROWS = (576, 512, 1024, 1024, 2048)
SMALL_TOTAL = sum(SMALL_ROWS)
SMALL_HALF = SMALL_TOTAL // 2
SMALL_SUM_ROWS = 432


def _gather_call(c_row, w_ada_sh, pack_in, pack_small):
    def body(c_ref, wada_ref, pki_ref, pks_ref, mg_ref, cg_ref, gwi_ref, gws_ref,
             cv, ssem_c, rsem_c, ssem_m, rsem_m, ssem_w, rsem_w, ssem_f, rsem_f, lsem):
        x, y, c = lax.axis_index("x"), lax.axis_index("y"), lax.axis_index("c")
        me = 4 * x + 2 * y + c
        chip = 2 * x + y
        rel3 = [(1, 0), (0, 1), (1, 1)]
        packs = [(pki_ref, gwi_ref), (pks_ref, gws_ref)]

        sends = []
        for j, (dx, dy) in enumerate(rel3):
            for a, (pk, gw) in enumerate(packs):
                cp = pltpu.make_async_remote_copy(
                    src_ref=pk.at[c], dst_ref=gw.at[chip, c], send_sem=ssem_w.at[j, a], recv_sem=rsem_w.at[j, a],
                    device_id=(_flip(x, dx), _flip(y, dy), c), device_id_type=MESH)
                cp.start()
                sends.append(cp)
        owns = []
        for a, (pk, gw) in enumerate(packs):
            own = pltpu.make_async_copy(pk, gw.at[chip], lsem.at[a])
            own.start()
            owns.append(own)

        cv[me] = c_ref[...]
        for r in range(1, N_DEV):
            dx, dy, dc = (r >> 2) & 1, (r >> 1) & 1, r & 1
            cp = pltpu.make_async_remote_copy(
                src_ref=c_ref, dst_ref=cv.at[me], send_sem=ssem_c.at[r - 1], recv_sem=rsem_c.at[r - 1],
                device_id=(_flip(x, dx), _flip(y, dy), _flip(c, dc)), device_id_type=MESH)
            cp.start()
            sends.append(cp)
        for r in range(1, N_DEV):
            dx, dy, dc = (r >> 2) & 1, (r >> 1) & 1, r & 1
            src = 4 * _flip(x, dx) + 2 * _flip(y, dy) + _flip(c, dc)
            pltpu.make_async_remote_copy(
                src_ref=c_ref, dst_ref=cv.at[src], send_sem=ssem_c.at[r - 1], recv_sem=rsem_c.at[r - 1],
                device_id=(x, y, c), device_id_type=MESH).wait_recv()
        rows = lax.broadcasted_iota(jnp.int32, (N_DEV, D_MODEL), 0)
        call = jnp.zeros((N_DEV, D_MODEL), F32)
        for b in range(N_DEV):
            call = jnp.where(rows == b, jnp.broadcast_to(cv[b], (N_DEV, D_MODEL)), call)
        cg_ref[...] = call

        mg_ref[chip] = lax.dot_general(call, wada_ref[...], (((1,), (0,)), ((), ())),
                                       precision=lax.Precision.HIGHEST, preferred_element_type=F32)
        for j, (dx, dy) in enumerate(rel3):
            cp = pltpu.make_async_remote_copy(
                src_ref=mg_ref.at[chip], dst_ref=mg_ref.at[chip], send_sem=ssem_m.at[j], recv_sem=rsem_m.at[j],
                device_id=(_flip(x, dx), _flip(y, dy), c), device_id_type=MESH)
            cp.start()
            sends.append(cp)
        for j, (dx, dy) in enumerate(rel3):
            src_chip = 2 * _flip(x, dx) + _flip(y, dy)
            pltpu.make_async_remote_copy(
                src_ref=mg_ref.at[src_chip], dst_ref=mg_ref.at[src_chip], send_sem=ssem_m.at[j],
                recv_sem=rsem_m.at[j], device_id=(x, y, c), device_id_type=MESH).wait_recv()
        for j, (dx, dy) in enumerate(rel3):
            src_chip = 2 * _flip(x, dx) + _flip(y, dy)
            for a, (pk, gw) in enumerate(packs):
                pltpu.make_async_remote_copy(
                    src_ref=pk.at[c], dst_ref=gw.at[src_chip, c], send_sem=ssem_w.at[j, a],
                    recv_sem=rsem_w.at[j, a], device_id=(x, y, c), device_id_type=MESH).wait_recv()
                cp = pltpu.make_async_remote_copy(
                    src_ref=gw.at[src_chip, c], dst_ref=gw.at[src_chip, c], send_sem=ssem_f.at[j, a],
                    recv_sem=rsem_f.at[j, a], device_id=(x, y, 1 - c), device_id_type=MESH)
                cp.start()
                sends.append(cp)
        for j, (dx, dy) in enumerate(rel3):
            src_chip = 2 * _flip(x, dx) + _flip(y, dy)
            for a, (pk, gw) in enumerate(packs):
                pltpu.make_async_remote_copy(
                    src_ref=pk.at[c], dst_ref=gw.at[src_chip, 1 - c], send_sem=ssem_f.at[j, a],
                    recv_sem=rsem_f.at[j, a], device_id=(x, y, c), device_id_type=MESH).wait_recv()
        for cp in sends:
            cp.wait_send()
        for own in owns:
            own.wait()

    vmem = pl.BlockSpec(memory_space=pltpu.VMEM)
    return pl.pallas_call(
        body, name="gather_fwd",
        out_shape=(_sds((N_CHIPS, N_DEV, 768), F32), _sds((N_DEV, D_MODEL), F32),
                   _sds((N_CHIPS, 2, IN_SHARD, HALF_D), BF16), _sds((N_CHIPS, 2, SMALL_HALF, LANES), BF16)),
        in_specs=[vmem, vmem, vmem, vmem], out_specs=(vmem, vmem, vmem, vmem),
        scratch_shapes=[
            pltpu.VMEM((N_DEV, 1, D_MODEL), F32),
            pltpu.SemaphoreType.DMA((N_DEV - 1,)), pltpu.SemaphoreType.DMA((N_DEV - 1,)),
            pltpu.SemaphoreType.DMA((3,)), pltpu.SemaphoreType.DMA((3,)),
            pltpu.SemaphoreType.DMA((3, 2)), pltpu.SemaphoreType.DMA((3, 2)),
            pltpu.SemaphoreType.DMA((3, 2)), pltpu.SemaphoreType.DMA((3, 2)),
            pltpu.SemaphoreType.DMA((2,)),
        ],
        compiler_params=_params(),
    )(c_row, w_ada_sh, pack_in, pack_small)


def _reduce_call(g_in, g_small, sv):
    def body(gi_ref, gs_ref, sv_ref, fi_ref, fs_ref, svg_ref, pair_i, pair_s, send_i, send_s, land_i, land_s,
             ssem_p, rsem_p, ssem_g, rsem_g, ssem_s, rsem_s, ssem_x, rsem_x):
        x, y, c = lax.axis_index("x"), lax.axis_index("y"), lax.axis_index("c")
        me = 4 * x + 2 * y + c
        chip = 2 * x + y
        rel3 = [(1, 0), (0, 1), (1, 1)]
        payloads = [(gi_ref, pair_i, send_i, land_i, fi_ref), (gs_ref, pair_s, send_s, land_s, fs_ref)]
        copies = []

        for k in range(N_CHIPS):
            for a, (g, pair, _, _, _) in enumerate(payloads):
                cp = pltpu.make_async_remote_copy(
                    src_ref=g.at[2 * k + 1 - c], dst_ref=pair.at[k], send_sem=ssem_p.at[k, a],
                    recv_sem=rsem_p.at[k, a], device_id=(x, y, 1 - c), device_id_type=MESH)
                cp.start()
                copies.append(cp)

        for r in range(1, N_DEV):
            dx, dy, dc = (r >> 2) & 1, (r >> 1) & 1, r & 1
            cp = pltpu.make_async_remote_copy(
                src_ref=sv_ref, dst_ref=svg_ref.at[me], send_sem=ssem_s.at[r - 1], recv_sem=rsem_s.at[r - 1],
                device_id=(_flip(x, dx), _flip(y, dy), _flip(c, dc)), device_id_type=MESH)
            cp.start()
            copies.append(cp)
        svg_ref[me] = sv_ref[...]

        for k in range(N_CHIPS):
            for a, (g, pair, _, _, _) in enumerate(payloads):
                pltpu.make_async_remote_copy(
                    src_ref=g.at[2 * k + c], dst_ref=pair.at[k], send_sem=ssem_p.at[k, a],
                    recv_sem=rsem_p.at[k, a], device_id=(x, y, c), device_id_type=MESH).wait_recv()

        def pair_sum(k, store_in, store_small):
            for qd in range(HALF_D // LANES):
                sl = slice(LANES * qd, LANES * qd + LANES)
                store_in(sl, gi_ref[2 * k + c, :, sl].astype(F32) + pair_i[k, :, sl].astype(F32))

            def rows(i, carry):
                sl = pl.ds(pl.multiple_of(i * SMALL_SUM_ROWS, 16), SMALL_SUM_ROWS)
                store_small(sl, gs_ref[2 * k + c, sl, :].astype(F32) + pair_s[k, sl, :].astype(F32))
                return carry

            lax.fori_loop(0, SMALL_HALF // SMALL_SUM_ROWS, rows, 0)

        for j, (dx, dy) in enumerate(rel3):
            tx, ty = _flip(x, dx), _flip(y, dy)

            def put_in(sl, val, j=j):
                send_i[j, :, sl] = val.astype(BF16)

            def put_small(sl, val, j=j):
                send_s[j, sl, :] = val.astype(BF16)

            pair_sum(2 * tx + ty, put_in, put_small)
            for a, (_, _, send, land, _) in enumerate(payloads):
                cp = pltpu.make_async_remote_copy(
                    src_ref=send.at[j], dst_ref=land.at[j], send_sem=ssem_g.at[j, a], recv_sem=rsem_g.at[j, a],
                    device_id=(tx, ty, c), device_id_type=MESH)
                cp.start()
                copies.append(cp)

        def own_in(sl, val):
            fi_ref[c, :, sl] = val

        def own_small(sl, val):
            fs_ref[c, sl, :] = val

        pair_sum(chip, own_in, own_small)
        for j in range(3):
            for a, (_, _, send, land, _) in enumerate(payloads):
                pltpu.make_async_remote_copy(
                    src_ref=send.at[j], dst_ref=land.at[j], send_sem=ssem_g.at[j, a], recv_sem=rsem_g.at[j, a],
                    device_id=(x, y, c), device_id_type=MESH).wait_recv()
            for qd in range(HALF_D // LANES):
                sl = slice(LANES * qd, LANES * qd + LANES)
                fi_ref[c, :, sl] += land_i[j, :, sl].astype(F32)

            def add_rows(i, carry, j=j):
                sl = pl.ds(pl.multiple_of(i * SMALL_SUM_ROWS, 16), SMALL_SUM_ROWS)
                fs_ref[c, sl, :] += land_s[j, sl, :].astype(F32)
                return carry

            lax.fori_loop(0, SMALL_HALF // SMALL_SUM_ROWS, add_rows, 0)

        for a, f in enumerate((fi_ref, fs_ref)):
            cp = pltpu.make_async_remote_copy(
                src_ref=f.at[c], dst_ref=f.at[c], send_sem=ssem_x.at[a], recv_sem=rsem_x.at[a],
                device_id=(x, y, 1 - c), device_id_type=MESH)
            cp.start()
            copies.append(cp)
        for a, f in enumerate((fi_ref, fs_ref)):
            pltpu.make_async_remote_copy(
                src_ref=f.at[c], dst_ref=f.at[1 - c], send_sem=ssem_x.at[a], recv_sem=rsem_x.at[a],
                device_id=(x, y, c), device_id_type=MESH).wait_recv()
        for r in range(1, N_DEV):
            dx, dy, dc = (r >> 2) & 1, (r >> 1) & 1, r & 1
            src = 4 * _flip(x, dx) + 2 * _flip(y, dy) + _flip(c, dc)
            pltpu.make_async_remote_copy(
                src_ref=sv_ref, dst_ref=svg_ref.at[src], send_sem=ssem_s.at[r - 1],
                recv_sem=rsem_s.at[r - 1], device_id=(x, y, c), device_id_type=MESH).wait_recv()
        for cp in copies:
            cp.wait_send()

    vmem = pl.BlockSpec(memory_space=pltpu.VMEM)
    return pl.pallas_call(
        body, name="grad_reduce",
        out_shape=(_sds((2, IN_SHARD, HALF_D), F32), _sds((2, SMALL_HALF, LANES), F32),
                   _sds((N_DEV, 8, SV_COLS), F32)),
        in_specs=[vmem, vmem, vmem], out_specs=(vmem, vmem, vmem),
        scratch_shapes=[
            pltpu.VMEM((N_CHIPS, IN_SHARD, HALF_D), BF16), pltpu.VMEM((N_CHIPS, SMALL_HALF, LANES), BF16),
            pltpu.VMEM((3, IN_SHARD, HALF_D), BF16), pltpu.VMEM((3, SMALL_HALF, LANES), BF16),
            pltpu.VMEM((3, IN_SHARD, HALF_D), BF16), pltpu.VMEM((3, SMALL_HALF, LANES), BF16),
            pltpu.SemaphoreType.DMA((N_CHIPS, 2)), pltpu.SemaphoreType.DMA((N_CHIPS, 2)),
            pltpu.SemaphoreType.DMA((3, 2)), pltpu.SemaphoreType.DMA((3, 2)),
            pltpu.SemaphoreType.DMA((N_DEV - 1,)), pltpu.SemaphoreType.DMA((N_DEV - 1,)),
            pltpu.SemaphoreType.DMA((2,)), pltpu.SemaphoreType.DMA((2,)),
        ],
        compiler_params=_params(),
    )(g_in, g_small, sv)


def _dwin_call(h, dproj):
    s_len = h.shape[0]
    tm = min(4 * ROW_TILE, s_len)
    nrow = s_len // tm
    nc = 4
    chunk = W_INT // nc

    def body(h_ref, dp_ref, dw_ref, acc):
        i = pl.program_id(1)

        @pl.when(i == 0)
        def _():
            acc[...] = jnp.zeros_like(acc)

        acc[...] += _dot_tn(dp_ref[...], h_ref[...])

        @pl.when(i == nrow - 1)
        def _():
            dw_ref[...] = acc[...].astype(BF16)

    return pl.pallas_call(
        body, name="dwin", grid=(nc, nrow),
        out_shape=_sds((W_INT, D_MODEL), BF16),
        in_specs=[pl.BlockSpec((tm, D_MODEL), lambda c, i: (i, 0)),
                  pl.BlockSpec((tm, chunk), lambda c, i: (i, c))],
        out_specs=pl.BlockSpec((chunk, D_MODEL), lambda c, i: (c, 0)),
        scratch_shapes=[pltpu.VMEM((chunk, D_MODEL), F32)],
        compiler_params=_params(("parallel", "arbitrary")),
    )(h, dproj)


def _swap_rows(w, group):
    r, n = w.shape
    return w.reshape(r // group, 2, group // 2, n)[:, ::-1].reshape(r, n)


def _internal_weights(w_in_t, w_uq, w_ukv):
    krot_t = w_in_t[2688:2720]
    w_int_t = jnp.concatenate([
        w_in_t[0:512] * jnp.asarray(0.125, w_in_t.dtype), w_in_t[512:2048],
        w_in_t[2048:2432], w_in_t[2432:2688], w_in_t[2720:3232], w_in_t[3232:4256], w_in_t[4256:5280],
        jnp.tile(krot_t, (4, 1)), jnp.tile(_swap_rows(krot_t, 32), (4, 1)),
        jnp.zeros((W_INT - O_END, D_MODEL), w_in_t.dtype)], axis=0)
    uq = w_uq.reshape(Q_RANK, N_HEADS, 96)
    wp = uq[:, :, 64:].reshape(Q_RANK, 256)
    w_q = jnp.concatenate([uq[:, :, :64].reshape(Q_RANK, 512), wp, _swap_halves(wp, 32)], axis=1)
    ukv = w_ukv.reshape(KV_RANK, N_HEADS, 128)
    w_kv = jnp.concatenate([ukv[:, :, :64].reshape(KV_RANK, 512), ukv[:, :, 64:].reshape(KV_RANK, 512)], axis=1)
    return w_int_t, w_q, w_kv


def _true_weight_grads(dwi_t, dwq, dwkv):
    dkr = dwi_t[O_KR:O_KR + 128].astype(F32).reshape(4, 32, D_MODEL).sum(axis=0)
    dkr_sw = dwi_t[O_KR + 128:O_END].astype(F32).reshape(4, 32, D_MODEL).sum(axis=0)
    dkrot_t = (dkr + _swap_rows(dkr_sw, 32)).astype(dwi_t.dtype)
    g_in_t = jnp.concatenate([
        dwi_t[0:512] * jnp.asarray(0.125, dwi_t.dtype), dwi_t[512:2048], dwi_t[O_CQ:O_CKV], dwi_t[O_CKV:O_MZ],
        dkrot_t, dwi_t[O_MZ:O_GA], dwi_t[O_GA:O_GB], dwi_t[O_GB:O_KR]], axis=0)
    dwp = dwq[:, 512:768] + _swap_halves(dwq[:, 768:1024], 32)
    g_uq = jnp.concatenate([dwq[:, :512].reshape(Q_RANK, N_HEADS, 64), dwp.reshape(Q_RANK, N_HEADS, 32)],
                           axis=2).reshape(Q_RANK, 768)
    g_ukv = jnp.concatenate([dwkv[:, :512].reshape(KV_RANK, N_HEADS, 64), dwkv[:, 512:].reshape(KV_RANK, N_HEADS, 64)],
                            axis=2).reshape(KV_RANK, 1024)
    return g_in_t, g_uq, g_ukv


def _swap_halves(w, group):
    r, n = w.shape
    return w.reshape(r, n // group, 2, group // 2)[:, :, ::-1, :].reshape(r, n)


def _pack_shards(parts):
    return jnp.concatenate([p.reshape(-1, LANES) for p in parts], axis=0)


def _unpack_small(gw):
    offs = [0]
    for r in SMALL_ROWS:
        offs.append(offs[-1] + r)

    def cols(i, rows, shard_cols):
        blk = gw[:, offs[i]:offs[i + 1]].reshape(N_CHIPS, rows, shard_cols)
        return blk.transpose(1, 0, 2).reshape(rows, N_CHIPS * shard_cols)

    return (cols(0, Q_RANK, 192), cols(1, KV_RANK, 256), cols(2, 512, 256), cols(3, 512, 256),
            gw[:, offs[4]:offs[5]].reshape(D_MODEL, D_MODEL))


def _chip_major(g, shard_cols):
    r = g.shape[0]
    return g.reshape(r, N_CHIPS, shard_cols).transpose(1, 0, 2).reshape(N_CHIPS, -1, LANES)


def kernel(x, c, positions, w_ada, b_ada, norm_gain, w_in, q_norm_gain, w_uq, kv_norm_gain, w_ukv, w_branch_a, w_branch_b, w_out, final_norm_gain, loss_target, m_w_ada, m_b_ada, m_norm_gain, m_w_in, m_q_norm_gain, m_w_uq, m_kv_norm_gain, m_w_ukv, m_w_branch_a, m_w_branch_b, m_w_out, m_final_norm_gain, v_w_ada, v_b_ada, v_norm_gain, v_w_in, v_q_norm_gain, v_w_uq, v_kv_norm_gain, v_w_ukv, v_w_branch_a, v_w_branch_b, v_w_out, v_final_norm_gain):
    ix, iy, ic = lax.axis_index("x"), lax.axis_index("y"), lax.axis_index("c")
    me = 4 * ix + 2 * iy + ic
    chip = 2 * ix + iy
    xs = x[0]
    tgt = loss_target[0]
    s_len = xs.shape[0]

    w_in_t = jnp.swapaxes(w_in[0], 0, 1)
    w_in_tb = w_in_t.astype(BF16)
    pack_in = jnp.stack([w_in_tb[:, :HALF_D], w_in_tb[:, HALF_D:]], axis=0)
    small_shards = (w_uq[0], w_ukv[0], w_branch_a[0], w_branch_b[0], w_out[0])
    pack_small = _pack_shards([s.astype(BF16) for s in small_shards]).reshape(2, SMALL_HALF, LANES)
    mg, call, gw_in, gw_small = _gather_call(c, w_ada[0], pack_in, pack_small)
    mod = mg.transpose(1, 0, 2).reshape(N_DEV, 3 * D_MODEL) + b_ada
    mod_me = lax.dynamic_slice_in_dim(mod, me, 1, axis=0)
    shift, scale, gate = mod_me[:, :D_MODEL], mod_me[:, D_MODEL:2 * D_MODEL], mod_me[:, 2 * D_MODEL:]

    f_in_t = jnp.concatenate([gw_in[:, 0], gw_in[:, 1]], axis=2).reshape(IN_WIDTH, D_MODEL)
    f_uq, f_ukv, f_a, f_b, f_out = _unpack_small(gw_small.reshape(N_CHIPS, SMALL_TOTAL, LANES))
    w_int_t, w_q, w_kv = _internal_weights(f_in_t, f_uq, f_ukv)

    inv_freq = ROPE_BASE ** (-jnp.arange(0, ROPE_DIM, 2, dtype=F32) / ROPE_DIM)
    ang = positions[0].astype(F32)[:, None] * inv_freq
    cs, sn = jnp.cos(ang), jnp.sin(ang)
    cos256 = jnp.tile(jnp.concatenate([cs, cs], axis=1), (1, 8))
    sin256 = jnp.tile(jnp.concatenate([-sn, sn], axis=1), (1, 8))

    (h, sq, sk, sv, sz, cq, ckv, mz, ga, gb, kpt, qn, qp, kn, vv) = _inproj_call(
        xs, shift, scale, norm_gain, w_int_t, w_q, w_kv, q_norm_gain, kv_norm_gain, cos256, sin256)
    oa, lt, first = _sb_fwd_call(sq, sk, sv)
    ob, lse = _mla_fwd_call(qn, qp, kn, kpt, vv)

    gf = final_norm_gain.reshape(1, D_MODEL)
    (dx2, doa, dob, dsz, dmz, dga, dgb, dwo, dwa, dwb, dgf, dgate, loss_p) = _post_call(
        xs, tgt, oa, ob, sz, mz, ga, gb, gate, gf, f_a, f_b, f_out, f_a.T, f_b.T, f_out.T)

    dsq, dsk_t, dsv_t = _sb_bwd_call(first[:, :, 0, 0].reshape(-1), sq, sk, sv, doa, lt)
    dqn, dqp, dkn_t, dkpt_t, dvv_t = _mla_bwd_call(qn, qp, kn, kpt, vv, ob, dob, lse)

    dproj, dwq, dwkv, dqg, dkvg = _bwdprep_call(
        dsq, dsk_t, dsv_t, dsz, dqn, dqp, dkn_t, dvv_t, dkpt_t, dmz, dga, dgb, cq, ckv, cos256, sin256,
        q_norm_gain, kv_norm_gain, w_q.T, w_kv.T)
    grad_x, dshift, dscale, dg1 = _dh_call(dproj, w_int_t, xs, dx2, scale, norm_gain)
    dwi_t = _dwin_call(h, dproj)
    g_in_t, g_uq, g_ukv = _true_weight_grads(dwi_t, dwq, dwkv)

    g_in_c = g_in_t.reshape(N_CHIPS, IN_SHARD, D_MODEL)
    g_in_pieces = jnp.stack([g_in_c[:, :, :HALF_D], g_in_c[:, :, HALF_D:]], axis=1).reshape(N_DEV, IN_SHARD, HALF_D)
    g_small = jnp.concatenate([
        _chip_major(g_uq, 192), _chip_major(g_ukv, 256), _chip_major(dwa, 256), _chip_major(dwb, 256),
        dwo.reshape(N_CHIPS, -1, LANES)], axis=1).astype(BF16).reshape(N_DEV, SMALL_HALF, LANES)
    small = jnp.concatenate([
        dshift, dscale, dgate, dg1, dqg, dkvg, dgf, loss_p,
        jnp.zeros((1, 8 * SV_COLS - 5888), F32)], axis=1).reshape(8, SV_COLS)
    full_in, full_small, svg = _reduce_call(g_in_pieces, g_small, small)
    gs_in_t = jnp.concatenate([full_in[0], full_in[1]], axis=1)
    full = full_small.reshape(SMALL_TOTAL, LANES)
    offs = [0]
    for r in SMALL_ROWS:
        offs.append(offs[-1] + r)
    gs_uq = full[offs[0]:offs[1]].reshape(Q_RANK, 192)
    gs_ukv = full[offs[1]:offs[2]].reshape(KV_RANK, 256)
    gs_a = full[offs[2]:offs[3]].reshape(512, 256)
    gs_b = full[offs[3]:offs[4]].reshape(512, 256)
    gs_out = full[offs[4]:offs[5]].reshape(256, D_MODEL)

    svm = svg.reshape(N_DEV, 8 * SV_COLS)
    dmod_sh = lax.dynamic_slice_in_dim(svm[:, :3 * D_MODEL], chip * 768, 768, axis=1)
    tot, gs_ada = _small_call(svm, call.T, dmod_sh)
    g_bada = tot[:, 0:3072]
    g_g1 = tot[:, 3072:4096]
    g_qg = tot[:, 4096:4480]
    g_kvg = tot[:, 4480:4736]
    g_gf = tot[:, 4736:5760]
    loss = tot[0, 5760]

    names = ["w_ada", "b_ada", "norm_gain", "w_in", "q_norm_gain", "w_uq", "kv_norm_gain", "w_ukv",
             "w_branch_a", "w_branch_b", "w_out", "final_norm_gain"]
    ws = [w_ada[0], b_ada, norm_gain, w_in_t, q_norm_gain, w_uq[0], kv_norm_gain, w_ukv[0],
          w_branch_a[0], w_branch_b[0], w_out[0], final_norm_gain.reshape(1, D_MODEL)]
    gs = [gs_ada, g_bada, g_g1, gs_in_t, g_qg, gs_uq, g_kvg, gs_ukv, gs_a, gs_b, gs_out, g_gf]
    ms = [m_w_ada[0], m_b_ada, m_norm_gain, jnp.swapaxes(m_w_in[0], 0, 1), m_q_norm_gain, m_w_uq[0],
          m_kv_norm_gain, m_w_ukv[0], m_w_branch_a[0], m_w_branch_b[0], m_w_out[0],
          m_final_norm_gain.reshape(1, D_MODEL)]
    vs = [v_w_ada[0], v_b_ada, v_norm_gain, jnp.swapaxes(v_w_in[0], 0, 1), v_q_norm_gain, v_w_uq[0],
          v_kv_norm_gain, v_w_ukv[0], v_w_branch_a[0], v_w_branch_b[0], v_w_out[0],
          v_final_norm_gain.reshape(1, D_MODEL)]
    refs = [w_ada, b_ada, norm_gain, w_in, q_norm_gain, w_uq, kv_norm_gain, w_ukv,
            w_branch_a, w_branch_b, w_out, final_norm_gain]
    grads, deltas, new_ms, new_vs = [], [], [], []
    for n, w_, g_, m_, v_, ref in zip(names, ws, gs, ms, vs, refs):
        outs = (g_,) + _adamw_call(n, w_, g_, m_, v_)
        if n == "w_in":
            outs = tuple(jnp.swapaxes(o_, 0, 1) for o_ in outs)
        for lst, o_ in zip((grads, deltas, new_ms, new_vs), outs):
            lst.append(o_.reshape(ref.shape))

    return (loss, grad_x.reshape(x.shape), *grads, *deltas, *new_ms, *new_vs)
```

```python
import math

import jax
import jax.numpy as jnp
from jax import lax
from jax.experimental import pallas as pl
from jax.experimental.pallas import tpu as pltpu

F32 = jnp.float32
BF16 = jnp.bfloat16

D_MODEL = 1024
SB_WIDTH = 512
MLA_WIDTH = 512
Q_RANK = 384
KV_RANK = 256
ROPE_DIM = 32
N_HEADS = 8
IN_WIDTH = 5280
EPS = 1e-6
ROPE_BASE = 10000.0
MLA_SCALE = 1.0 / math.sqrt(96.0)

ADAM_LR = 0.001
ADAM_B1 = 0.9
ADAM_B2 = 0.999
ADAM_EPS = 1e-08
ADAM_WD = 0.01
ADAM_STEP = 10

O_SQ, O_SK, O_SV, O_SZ, O_CQ, O_CKV, O_MZ, O_GA, O_GB, O_KR, O_END = (
    0, 512, 1024, 1536, 2048, 2432, 2688, 3200, 4224, 5248, 5504)
W_INT = 5632

N_CHIPS = 4
N_DEV = 8
LANES = 128
SV_COLS = 768

ROW_TILE = 256
ATT_TILE = 256
ATT_Q_TILES = 2
FWD_Q_TILES = 4
SB_Q_TILES = 1
MLA_KEY_TILE = 512
VMEM_LIMIT = 56 * 1024 * 1024

MESH = pl.DeviceIdType.MESH


def _dot(a, b):
    return lax.dot_general(a, b, (((1,), (0,)), ((), ())), preferred_element_type=F32)


def _dot_nt(a, b):
    return lax.dot_general(a, b, (((1,), (1,)), ((), ())), preferred_element_type=F32)


def _dot_tn(a, b):
    return lax.dot_general(a, b, (((0,), (0,)), ((), ())), preferred_element_type=F32)


def _sigmoid(z):
    return 1.0 / (1.0 + jnp.exp(-z))


def _params(sem=None):
    if sem is None:
        return pltpu.CompilerParams(vmem_limit_bytes=VMEM_LIMIT)
    return pltpu.CompilerParams(dimension_semantics=sem, vmem_limit_bytes=VMEM_LIMIT)


def _rows(tm, n):
    return pl.BlockSpec((tm, n), lambda i: (i, 0))


def _cols(n, tm):
    return pl.BlockSpec((n, tm), lambda i: (0, i))


def _whole(shape):
    nd = len(shape)
    return pl.BlockSpec(shape, lambda i: (0,) * nd)


def _sds(shape, dtype):
    return jax.ShapeDtypeStruct(shape, dtype)


def _flip(v, d):
    return 1 - v if d else v


def _inproj_call(x, shift, scale, g1, w_int, w_q, w_kv, qg, kvg, cos256, sin256):
    s_len = x.shape[0]
    tm = min(ROW_TILE, s_len)

    def body(x_ref, sh_ref, sc_ref, g1_ref, w_ref, wq_ref, wkv_ref, qg_ref, kvg_ref, cos_ref, sin_ref,
             h_ref, sq_ref, sk_ref, sv_ref, sz_ref, cq_ref, ckv_ref, mz_ref, ga_ref, gb_ref, kpt_ref,
             qn_ref, qp_ref, kn_ref, vv_ref):
        xt = x_ref[...]
        r = lax.rsqrt(jnp.mean(xt * xt, axis=-1, keepdims=True) + EPS)
        h = (xt * r * g1_ref[...]) * (1.0 + sc_ref[...]) + sh_ref[...]
        hb = h.astype(BF16)
        h_ref[...] = hb

        def seg(a, b):
            return _dot_nt(hb, w_ref[a:b, :])

        sq_ref[...] = seg(O_SQ, O_SK).astype(BF16)
        sk_ref[...] = seg(O_SK, O_SV).astype(BF16)
        sv_ref[...] = seg(O_SV, O_SZ).astype(BF16)
        sz_ref[...] = seg(O_SZ, O_CQ)
        mz_ref[...] = seg(O_MZ, O_GA)
        ga_ref[...] = seg(O_GA, O_GB)
        gb_ref[...] = seg(O_GB, O_KR)
        cos = cos_ref[...]
        sin = sin_ref[...]
        kr = seg(O_KR, O_END)
        kpt_ref[...] = (kr[:, :128] * cos[:, :128] + kr[:, 128:] * sin[:, :128]).astype(BF16)

        cq = seg(O_CQ, O_CKV)
        cq_ref[...] = cq
        rq = lax.rsqrt(jnp.mean(cq * cq, axis=-1, keepdims=True) + EPS)
        cqn = (cq * rq * qg_ref[...]).astype(BF16)
        qa = _dot(cqn, wq_ref[...])
        qn_ref[...] = qa[:, :512].astype(BF16)
        qp_ref[...] = (qa[:, 512:768] * cos + qa[:, 768:] * sin).astype(BF16)

        ckv = seg(O_CKV, O_MZ)
        ckv_ref[...] = ckv
        rk = lax.rsqrt(jnp.mean(ckv * ckv, axis=-1, keepdims=True) + EPS)
        ckvn = (ckv * rk * kvg_ref[...]).astype(BF16)
        kva = _dot(ckvn, wkv_ref[...])
        kn_ref[...] = kva[:, :512].astype(BF16)
        vv_ref[...] = kva[:, 512:].astype(BF16)

    outs = [
        (D_MODEL, BF16), (512, BF16), (512, BF16), (512, BF16), (512, F32), (Q_RANK, F32), (KV_RANK, F32),
        (512, F32), (D_MODEL, F32), (D_MODEL, F32), (128, BF16), (512, BF16), (256, BF16), (512, BF16), (512, BF16),
    ]
    return pl.pallas_call(
        body, name="inproj", grid=(s_len // tm,),
        out_shape=tuple(_sds((s_len, n), dt) for n, dt in outs),
        in_specs=[_rows(tm, D_MODEL), _whole((1, D_MODEL)), _whole((1, D_MODEL)), _whole((1, D_MODEL)),
                  _whole((W_INT, D_MODEL)), _whole((Q_RANK, 1024)), _whole((KV_RANK, 1024)),
                  _whole((1, Q_RANK)), _whole((1, KV_RANK)), _rows(tm, 256), _rows(tm, 256)],
        out_specs=tuple(_rows(tm, n) for n, _ in outs),
        compiler_params=_params(("parallel",)),
    )(x, shift, scale, g1, w_int, w_q, w_kv, qg, kvg, cos256, sin256)


Z_CLAMP = 80.0
RUN_CUTOFF = 110.0


def _softplus_clamped(z):
    zc = jnp.minimum(z, Z_CLAMP)
    return zc, jnp.log(1.0 + jnp.exp(zc))


def _tri_sum(a, tri):
    return _dot(a.astype(BF16), tri)


def _sb_fwd_call(q, k, v):
    s_len = q.shape[0]
    tk = min(ATT_TILE, s_len)
    tq = min(SB_Q_TILES * ATT_TILE, s_len)
    r = tq // tk
    nq = s_len // tq

    def body(q_ref, k_ref, v_ref, o_ref, lt_ref, first_ref):
        i = pl.program_id(1)
        q2 = q_ref[...]
        lane = lax.broadcasted_iota(jnp.int32, (1, 256), 1)
        krow = lax.broadcasted_iota(jnp.int32, (tk, tk), 0)
        kcol = lax.broadcasted_iota(jnp.int32, (tk, tk), 1)
        row = lax.broadcasted_iota(jnp.int32, (tq, tk), 0)
        col = lax.broadcasted_iota(jnp.int32, (tq, tk), 1)
        later = (krow > kcol).astype(BF16)
        valids = [col + u * tk < row for u in range(r)]
        hms = [(lane // 64) == hh for hh in range(4)]
        qms = [jnp.where(hm, q2, jnp.zeros_like(q2)) for hm in hms]

        def block(j, carry, valid):
            runs, acc = list(carry[:4]), carry[4]
            off = pl.multiple_of(j * tk, tk)
            kb = k_ref[pl.ds(off, tk), :]
            vb = v_ref[pl.ds(off, tk), :]
            ws = []
            for hh in range(4):
                zc, sp = _softplus_clamped(_dot_nt(qms[hh], kb))
                lm = jnp.where(valid, sp, 0.0) if valid is not None else sp
                suf = _tri_sum(lm, later)
                w = jnp.exp(zc - sp - suf - runs[hh])
                if valid is not None:
                    w = jnp.where(valid, w, 0.0)
                ws.append(w.astype(BF16))
                runs[hh] = runs[hh] + jnp.sum(lm, axis=1, keepdims=True)
            vstack = jnp.concatenate([jnp.where(hm, vb, jnp.zeros_like(vb)) for hm in hms], axis=0)
            acc = acc + _dot(jnp.concatenate(ws, axis=1), vstack)
            return (*runs, acc)

        zero = jnp.zeros((tq, 1), F32)
        carry = (zero, zero, zero, zero, jnp.zeros((tq, 256), F32))
        for u in reversed(range(r)):
            carry = block(i * r + u, carry, valids[u])

        def least_run(runs):
            return jnp.min(jnp.minimum(jnp.minimum(runs[0], runs[1]), jnp.minimum(runs[2], runs[3])))

        n_full = i * r

        def unfinished(state):
            return jnp.logical_and(state[0] < n_full, state[1] <= RUN_CUTOFF)

        def visit(state):
            cr = block(n_full - 1 - state[0], state[2:], None)
            return (state[0] + 1, least_run(cr[:4]), *cr)

        state = lax.while_loop(unfinished, visit, (jnp.int32(0), least_run(carry[:4]), *carry))
        carry = state[2:]
        first_ref[...] = jnp.full(first_ref.shape, n_full - state[0], jnp.int32)
        for hh in range(4):
            lt_ref[0, :, hh:hh + 1] = carry[hh]
        o_ref[...] = carry[4]

    return pl.pallas_call(
        body, name="sb_fwd", grid=(2, nq),
        out_shape=(_sds((s_len, SB_WIDTH), F32), _sds((2, s_len, 4), F32), _sds((2, nq, 8, 128), jnp.int32)),
        in_specs=[pl.BlockSpec((tq, 256), lambda g, i: (i, g)),
                  pl.BlockSpec((s_len, 256), lambda g, i: (0, g)),
                  pl.BlockSpec((s_len, 256), lambda g, i: (0, g))],
        out_specs=(pl.BlockSpec((tq, 256), lambda g, i: (i, g)),
                   pl.BlockSpec((1, tq, 4), lambda g, i: (g, i, 0)),
                   pl.BlockSpec((1, 1, 8, 128), lambda g, i: (g, i, 0, 0))),
        compiler_params=_params(("parallel", "parallel")),
    )(q, k, v)


def _sb_bwd_call(first, q, k, v, do, lt):
    s_len = q.shape[0]
    tk = min(ATT_TILE, s_len)
    tq = min(SB_Q_TILES * ATT_TILE, s_len)
    r = tq // tk
    nq = s_len // tq
    nq_fwd = first.shape[0] // 2
    per_fwd = nq // nq_fwd

    def body(first_ref, q_ref, k_ref, v_ref, do_ref, lt_ref, dq_ref, dk_ref, dv_ref):
        g = pl.program_id(0)
        i = pl.program_id(1)

        @pl.when(i == 0)
        def _():
            dk_ref[...] = jnp.zeros_like(dk_ref)
            dv_ref[...] = jnp.zeros_like(dv_ref)

        q2 = q_ref[...]
        do2 = do_ref[...].astype(BF16)
        lane = lax.broadcasted_iota(jnp.int32, (1, 256), 1)
        krow = lax.broadcasted_iota(jnp.int32, (tk, tk), 0)
        kcol = lax.broadcasted_iota(jnp.int32, (tk, tk), 1)
        row = lax.broadcasted_iota(jnp.int32, (tq, tk), 0)
        col = lax.broadcasted_iota(jnp.int32, (tq, tk), 1)
        earlier = (krow < kcol).astype(BF16)
        later = (krow > kcol).astype(BF16)
        valids = [col + u * tk < row for u in range(r)]
        hms = [(lane // 64) == hh for hh in range(4)]
        qms = [jnp.where(hm, q2, jnp.zeros_like(q2)) for hm in hms]
        doms = [jnp.where(hm, do2, jnp.zeros_like(do2)) for hm in hms]
        ltots = [lt_ref[0, :, hh:hh + 1] for hh in range(4)]
        q2t = jnp.transpose(q2.astype(F32))
        do2t = jnp.transpose(do_ref[...])
        subl = lax.broadcasted_iota(jnp.int32, (256, 1), 0)
        qtstack = jnp.concatenate(
            [jnp.where((subl // 64) == hh, q2t, 0.0).astype(BF16) for hh in range(4)], axis=1)
        dotstack = jnp.concatenate(
            [jnp.where((subl // 64) == hh, do2t, 0.0).astype(BF16) for hh in range(4)], axis=1)

        def block(j, carry, valid):
            lpre, ppre, dq = list(carry[0:4]), list(carry[4:8]), carry[8]
            off = pl.multiple_of(j * tk, tk)
            kb = k_ref[pl.ds(off, tk), :]
            vb = v_ref[pl.ds(off, tk), :]
            dzs, avs = [], []
            for hh in range(4):
                zc, sp = _softplus_clamped(_dot_nt(qms[hh], kb))
                lsig = zc - sp
                lm = jnp.where(valid, sp, 0.0) if valid is not None else sp
                rowsum = jnp.sum(lm, axis=1, keepdims=True)
                between = _tri_sum(lm, later) + ((ltots[hh] - lpre[hh]) - rowsum)
                a = jnp.exp(lsig - between)
                if valid is not None:
                    a = jnp.where(valid, a, 0.0)
                p = a * _dot_nt(doms[hh], vb)
                pbefore = ppre[hh] + _tri_sum(p, earlier)
                dz = p - jnp.exp(lsig) * (p + pbefore)
                if valid is not None:
                    dz = jnp.where(valid, dz, 0.0)
                dzs.append(dz.astype(BF16))
                avs.append(a.astype(BF16))
                lpre[hh] = lpre[hh] + rowsum
                ppre[hh] = ppre[hh] + jnp.sum(p, axis=1, keepdims=True)
            kstack = jnp.concatenate([jnp.where(hm, kb, jnp.zeros_like(kb)) for hm in hms], axis=0)
            dq = dq + _dot(jnp.concatenate(dzs, axis=1), kstack)
            dk_ref[:, pl.ds(off, tk)] += _dot(qtstack, jnp.concatenate(dzs, axis=0))
            dv_ref[:, pl.ds(off, tk)] += _dot(dotstack, jnp.concatenate(avs, axis=0))
            return (*lpre, *ppre, dq)

        zero = jnp.zeros((tq, 1), F32)
        start = jnp.minimum(first_ref[g * nq_fwd + i // per_fwd], i * r)
        carry = lax.fori_loop(start, i * r, lambda j, cr: block(j, cr, None),
                              (zero,) * 8 + (jnp.zeros((tq, 256), F32),))
        for u in range(r):
            carry = block(i * r + u, carry, valids[u])
        dq_ref[...] = carry[8].astype(BF16)

    return pl.pallas_call(
        body, name="sb_bwd",
        out_shape=(_sds((s_len, SB_WIDTH), BF16), _sds((SB_WIDTH, s_len), F32), _sds((SB_WIDTH, s_len), F32)),
        grid_spec=pltpu.PrefetchScalarGridSpec(
            num_scalar_prefetch=1, grid=(2, nq),
            in_specs=[pl.BlockSpec((tq, 256), lambda g, i, f: (i, g)),
                      pl.BlockSpec((s_len, 256), lambda g, i, f: (0, g)),
                      pl.BlockSpec((s_len, 256), lambda g, i, f: (0, g)),
                      pl.BlockSpec((tq, 256), lambda g, i, f: (i, g)),
                      pl.BlockSpec((1, tq, 4), lambda g, i, f: (g, i, 0))],
            out_specs=(pl.BlockSpec((tq, 256), lambda g, i, f: (i, g)),
                       pl.BlockSpec((256, s_len), lambda g, i, f: (g, 0)),
                       pl.BlockSpec((256, s_len), lambda g, i, f: (g, 0)))),
        compiler_params=_params(("parallel", "arbitrary")),
    )(first, q, k, v, do, lt)


def _mla_fwd_call(qn, qp, kn, kpt, v):
    s_len = qn.shape[0]
    tk = min(MLA_KEY_TILE, s_len)
    tq = min(FWD_Q_TILES * ATT_TILE, s_len)
    r = tq // tk
    nq = s_len // tq

    def body(qn_ref, qp_ref, kn_ref, kpt_ref, v_ref, o_ref, lse_ref):
        i = pl.program_id(1)
        qn2 = qn_ref[...]
        qp2 = qp_ref[...]
        lane256 = lax.broadcasted_iota(jnp.int32, (1, 256), 1)
        lane128 = lax.broadcasted_iota(jnp.int32, (1, 128), 1)
        krow = lax.broadcasted_iota(jnp.int32, (tk, tk), 0)
        kcol = lax.broadcasted_iota(jnp.int32, (tk, tk), 1)
        row = lax.broadcasted_iota(jnp.int32, (tq, tk), 0)
        col = lax.broadcasted_iota(jnp.int32, (tq, tk), 1)
        valids = [col + u * tk <= row for u in range(r)]
        m64s = [(lane256 // 64) == hh for hh in range(4)]
        half = [(lane128 // 64) == u for u in range(2)]
        m32s = [(lane128 // 32) == hh for hh in range(4)]
        qcs = []
        for hh in range(4):
            qpair = qn2[:, 128 * (hh // 2):128 * (hh // 2) + 128]
            qcs.append(jnp.concatenate([jnp.where(half[hh % 2], qpair, jnp.zeros_like(qpair)),
                                        jnp.where(m32s[hh], qp2, jnp.zeros_like(qp2))], axis=1))

        def by_head(vals):
            return jnp.where(m64s[0], vals[0], jnp.where(m64s[1], vals[1], jnp.where(m64s[2], vals[2], vals[3])))

        def block(j, carry, valid):
            ms, ls, acc = list(carry[0:4]), list(carry[4:8]), carry[8]
            off = pl.multiple_of(j * tk, tk)
            knb = kn_ref[pl.ds(off, tk), :]
            kpb = kpt_ref[pl.ds(off, tk), :]
            vb = v_ref[pl.ds(off, tk), :]
            kcs = [jnp.concatenate([knb[:, 128 * pp:128 * pp + 128], kpb], axis=1) for pp in range(2)]
            ps, alphas = [], []
            for hh in range(4):
                s = _dot_nt(qcs[hh], kcs[hh // 2]) * MLA_SCALE
                if valid is not None:
                    s = jnp.where(valid, s, -1e30)
                mn = jnp.maximum(ms[hh], jnp.max(s, axis=1, keepdims=True))
                p = jnp.exp(s - mn)
                alpha = jnp.exp(ms[hh] - mn)
                ls[hh] = alpha * ls[hh] + jnp.sum(p, axis=1, keepdims=True)
                ms[hh] = mn
                ps.append(p.astype(BF16))
                alphas.append(alpha)
            pvs = []
            for pp in range(2):
                vpair = vb[:, 128 * pp:128 * pp + 128]
                vstack = jnp.concatenate([jnp.where(hf, vpair, jnp.zeros_like(vpair)) for hf in half], axis=0)
                pvs.append(_dot(jnp.concatenate(ps[2 * pp:2 * pp + 2], axis=1), vstack))
            acc = by_head(alphas) * acc + jnp.concatenate(pvs, axis=1)
            return (*ms, *ls, acc)

        neg = jnp.full((tq, 1), -1e30, F32)
        zero = jnp.zeros((tq, 1), F32)
        carry = lax.fori_loop(0, i * r, lambda j, cr: block(j, cr, None),
                              (neg,) * 4 + (zero,) * 4 + (jnp.zeros((tq, 256), F32),))
        for u in range(r):
            carry = block(i * r + u, carry, valids[u])
        o_ref[...] = carry[8] / by_head(list(carry[4:8]))
        for hh in range(4):
            lse_ref[0, :, hh:hh + 1] = carry[hh] + jnp.log(carry[4 + hh])

    return pl.pallas_call(
        body, name="mla_fwd", grid=(2, nq),
        out_shape=(_sds((s_len, MLA_WIDTH), F32), _sds((2, s_len, 4), F32)),
        in_specs=[pl.BlockSpec((tq, 256), lambda g, i: (i, g)),
                  pl.BlockSpec((tq, 128), lambda g, i: (i, g)),
                  pl.BlockSpec((s_len, 256), lambda g, i: (0, g)),
                  pl.BlockSpec((s_len, 128), lambda g, i: (0, 0)),
                  pl.BlockSpec((s_len, 256), lambda g, i: (0, g))],
        out_specs=(pl.BlockSpec((tq, 256), lambda g, i: (i, g)),
                   pl.BlockSpec((1, tq, 4), lambda g, i: (g, i, 0))),
        compiler_params=_params(("parallel", "parallel")),
    )(qn, qp, kn, kpt, v)


def _mla_bwd_call(qn, qp, kn, kpt, v, o, do, lse):
    s_len = qn.shape[0]
    tk = min(MLA_KEY_TILE, s_len)
    tq = min(ATT_Q_TILES * ATT_TILE, s_len)
    r = tq // tk
    nq = s_len // tq

    def body(qn_ref, qp_ref, kn_ref, kpt_ref, v_ref, o_ref, do_ref, lse_ref,
             dqn_ref, dqp_ref, dkn_ref, dkpt_ref, dv_ref):
        g = pl.program_id(0)
        i = pl.program_id(1)

        @pl.when(i == 0)
        def _():
            dkn_ref[...] = jnp.zeros_like(dkn_ref)
            dv_ref[...] = jnp.zeros_like(dv_ref)

        @pl.when((i == 0) & (g == 0))
        def _():
            dkpt_ref[...] = jnp.zeros_like(dkpt_ref)

        qn2 = qn_ref[...]
        qp2 = qp_ref[...]
        dof = do_ref[...]
        dob = dof.astype(BF16)
        prod = dof * o_ref[...]
        lane256 = lax.broadcasted_iota(jnp.int32, (1, 256), 1)
        lane128 = lax.broadcasted_iota(jnp.int32, (1, 128), 1)
        krow = lax.broadcasted_iota(jnp.int32, (tk, tk), 0)
        kcol = lax.broadcasted_iota(jnp.int32, (tk, tk), 1)
        row = lax.broadcasted_iota(jnp.int32, (tq, tk), 0)
        col = lax.broadcasted_iota(jnp.int32, (tq, tk), 1)
        valids = [col + u * tk <= row for u in range(r)]
        m64s = [(lane256 // 64) == hh for hh in range(4)]
        half = [(lane128 // 64) == u for u in range(2)]
        m32s = [(lane128 // 32) == hh for hh in range(4)]
        qcs, doms = [], []
        for hh in range(4):
            sl = slice(128 * (hh // 2), 128 * (hh // 2) + 128)
            qpair = qn2[:, sl]
            dpair = dob[:, sl]
            qcs.append(jnp.concatenate([jnp.where(half[hh % 2], qpair, jnp.zeros_like(qpair)),
                                        jnp.where(m32s[hh], qp2, jnp.zeros_like(qp2))], axis=1))
            doms.append(jnp.where(half[hh % 2], dpair, jnp.zeros_like(dpair)))
        dsums = [jnp.sum(jnp.where(m64, prod, 0.0), axis=1, keepdims=True) * MLA_SCALE for m64 in m64s]
        lses = [lse_ref[0, :, hh:hh + 1] for hh in range(4)]
        qn2t = jnp.transpose(qn2.astype(F32))
        qp2t = jnp.transpose(qp2.astype(F32))
        do2t = jnp.transpose(dof)
        sub128 = lax.broadcasted_iota(jnp.int32, (128, 1), 0)
        qtstacks, dotstacks = [], []
        for pp in range(2):
            qts, dts = [], []
            for u in range(2):
                hh = 2 * pp + u
                qts.append(jnp.concatenate(
                    [jnp.where((sub128 // 64) == u, qn2t[128 * pp:128 * pp + 128, :], 0.0),
                     jnp.where((sub128 // 32) == hh, qp2t, 0.0)], axis=0).astype(BF16))
                dts.append(jnp.where((sub128 // 64) == u, do2t[128 * pp:128 * pp + 128, :], 0.0).astype(BF16))
            qtstacks.append(jnp.concatenate(qts, axis=1))
            dotstacks.append(jnp.concatenate(dts, axis=1))

        def block(j, carry, valid):
            dqn, dqp = carry
            off = pl.multiple_of(j * tk, tk)
            knb = kn_ref[pl.ds(off, tk), :]
            kpb = kpt_ref[pl.ds(off, tk), :]
            vb = v_ref[pl.ds(off, tk), :]
            dqn_parts = []
            dkp = None
            for pp in range(2):
                sl = slice(128 * pp, 128 * pp + 128)
                knp = knb[:, sl]
                vpair = vb[:, sl]
                kc = jnp.concatenate([knp, kpb], axis=1)
                dss, pbs, kcms = [], [], []
                for u in range(2):
                    hh = 2 * pp + u
                    s = _dot_nt(qcs[hh], kc) * MLA_SCALE
                    if valid is not None:
                        s = jnp.where(valid, s, -1e30)
                    p = jnp.exp(s - lses[hh])
                    ds = p * (_dot_nt(doms[hh], vpair) * MLA_SCALE - dsums[hh])
                    dss.append(ds.astype(BF16))
                    pbs.append(p.astype(BF16))
                    kcms.append(jnp.concatenate([jnp.where(half[u], knp, jnp.zeros_like(knp)),
                                                 jnp.where(m32s[hh], kpb, jnp.zeros_like(kpb))], axis=1))
                dqc = _dot(jnp.concatenate(dss, axis=1), jnp.concatenate(kcms, axis=0))
                dqn_parts.append(dqc[:, :128])
                dqp = dqp + dqc[:, 128:]
                dkc = _dot(qtstacks[pp], jnp.concatenate(dss, axis=0))
                dkn_ref[128 * pp:128 * pp + 128, pl.ds(off, tk)] += dkc[:128, :]
                dkp = dkc[128:, :] if dkp is None else dkp + dkc[128:, :]
                dv_ref[128 * pp:128 * pp + 128, pl.ds(off, tk)] += _dot(dotstacks[pp], jnp.concatenate(pbs, axis=0))
            dqn = dqn + jnp.concatenate(dqn_parts, axis=1)
            dkpt_ref[:, pl.ds(off, tk)] += dkp
            return dqn, dqp

        carry = lax.fori_loop(0, i * r, lambda j, cr: block(j, cr, None),
                              (jnp.zeros((tq, 256), F32), jnp.zeros((tq, 128), F32)))
        for u in range(r):
            carry = block(i * r + u, carry, valids[u])
        dqn, dqp = carry
        dqn_ref[...] = dqn.astype(BF16)
        dqp_ref[...] = dqp.astype(BF16)

    return pl.pallas_call(
        body, name="mla_bwd", grid=(2, nq),
        out_shape=(_sds((s_len, 512), BF16), _sds((s_len, 256), BF16), _sds((512, s_len), F32),
                   _sds((128, s_len), F32), _sds((512, s_len), F32)),
        in_specs=[pl.BlockSpec((tq, 256), lambda g, i: (i, g)),
                  pl.BlockSpec((tq, 128), lambda g, i: (i, g)),
                  pl.BlockSpec((s_len, 256), lambda g, i: (0, g)),
                  pl.BlockSpec((s_len, 128), lambda g, i: (0, 0)),
                  pl.BlockSpec((s_len, 256), lambda g, i: (0, g)),
                  pl.BlockSpec((tq, 256), lambda g, i: (i, g)),
                  pl.BlockSpec((tq, 256), lambda g, i: (i, g)),
                  pl.BlockSpec((1, tq, 4), lambda g, i: (g, i, 0))],
        out_specs=(pl.BlockSpec((tq, 256), lambda g, i: (i, g)),
                   pl.BlockSpec((tq, 128), lambda g, i: (i, g)),
                   pl.BlockSpec((256, s_len), lambda g, i: (g, 0)),
                   pl.BlockSpec((128, s_len), lambda g, i: (0, 0)),
                   pl.BlockSpec((256, s_len), lambda g, i: (g, 0))),
        compiler_params=_params(("arbitrary", "arbitrary")),
    )(qn, qp, kn, kpt, v, o, do, lse)


def _post_call(x, tgt, oa, ob, sz, mz, ga, gb, gate, gf, wa, wb, wo, wat, wbt, wot):
    s_len = x.shape[0]
    tm = min(ROW_TILE, s_len)

    def body(x_ref, t_ref, oa_ref, ob_ref, sz_ref, mz_ref, ga_ref, gb_ref, gate_ref, gf_ref,
             wa_ref, wb_ref, wo_ref, wat_ref, wbt_ref, wot_ref,
             dx2_ref, doa_ref, dob_ref, dsz_ref, dmz_ref, dga_ref, dgb_ref,
             dwo_ref, dwa_ref, dwb_ref, dgf_ref, dgate_ref, loss_ref):
        @pl.when(pl.program_id(0) == 0)
        def _():
            dwo_ref[...] = jnp.zeros_like(dwo_ref)
            dwa_ref[...] = jnp.zeros_like(dwa_ref)
            dwb_ref[...] = jnp.zeros_like(dwb_ref)
            dgf_ref[...] = jnp.zeros_like(dgf_ref)
            dgate_ref[...] = jnp.zeros_like(dgate_ref)
            loss_ref[...] = jnp.zeros_like(loss_ref)

        gate = gate_ref[...]
        gf = gf_ref[...]
        oa = oa_ref[...]
        ob = ob_ref[...]
        sz = sz_ref[...]
        mz = mz_ref[...]
        sa = _sigmoid(sz)
        sb = _sigmoid(mz)
        silu_a = sz * sa
        silu_b = mz * sb
        ua = (oa * silu_a).astype(BF16)
        ub = (ob * silu_b).astype(BF16)
        ya = _dot(ua, wa_ref[...])
        yb = _dot(ub, wb_ref[...])
        sga = _sigmoid(ga_ref[...])
        sgb = _sigmoid(gb_ref[...])
        merged = (sga * ya + sgb * yb).astype(BF16)
        out = _dot(merged, wo_ref[...])
        x2 = x_ref[...] + gate * out
        r2 = lax.rsqrt(jnp.mean(x2 * x2, axis=-1, keepdims=True) + EPS)
        xhat = x2 * r2
        err = xhat * gf - t_ref[...]
        loss_ref[...] += 0.5 * jnp.sum(jnp.sum(err * err, axis=1, keepdims=True), axis=0, keepdims=True) / D_MODEL
        dy = err * (1.0 / D_MODEL)
        dgf_ref[...] += jnp.sum(dy * xhat, axis=0, keepdims=True)
        dxhat = dy * gf
        dx2 = r2 * (dxhat - xhat * jnp.mean(dxhat * xhat, axis=-1, keepdims=True))
        dx2_ref[...] = dx2
        dgate_ref[...] += jnp.sum(dx2 * out, axis=0, keepdims=True)
        dout = (dx2 * gate).astype(BF16)
        dmerged = _dot(dout, wot_ref[...])
        dwo_ref[...] += _dot_tn(merged, dout)
        dya = dmerged * sga
        dyb = dmerged * sgb
        dga_ref[...] = (dya * ya * (1.0 - sga)).astype(BF16)
        dgb_ref[...] = (dyb * yb * (1.0 - sgb)).astype(BF16)
        dyab = dya.astype(BF16)
        dybb = dyb.astype(BF16)
        dua = _dot(dyab, wat_ref[...])
        dub = _dot(dybb, wbt_ref[...])
        dwa_ref[...] += _dot_tn(ua, dyab)
        dwb_ref[...] += _dot_tn(ub, dybb)
        doa_ref[...] = dua * silu_a
        dob_ref[...] = dub * silu_b
        dsz_ref[...] = (dua * oa * (sa * (1.0 + sz * (1.0 - sa)))).astype(BF16)
        dmz_ref[...] = (dub * ob * (sb * (1.0 + mz * (1.0 - sb)))).astype(BF16)

    return pl.pallas_call(
        body, name="post", grid=(s_len // tm,),
        out_shape=(_sds((s_len, D_MODEL), F32), _sds((s_len, 512), F32), _sds((s_len, 512), F32),
                   _sds((s_len, 512), BF16), _sds((s_len, 512), BF16),
                   _sds((s_len, D_MODEL), BF16), _sds((s_len, D_MODEL), BF16),
                   _sds((D_MODEL, D_MODEL), F32), _sds((512, D_MODEL), F32), _sds((512, D_MODEL), F32),
                   _sds((1, D_MODEL), F32), _sds((1, D_MODEL), F32), _sds((1, 128), F32)),
        in_specs=[_rows(tm, D_MODEL), _rows(tm, D_MODEL), _rows(tm, 512), _rows(tm, 512), _rows(tm, 512),
                  _rows(tm, 512), _rows(tm, D_MODEL), _rows(tm, D_MODEL), _whole((1, D_MODEL)), _whole((1, D_MODEL)),
                  _whole((512, D_MODEL)), _whole((512, D_MODEL)), _whole((D_MODEL, D_MODEL)),
                  _whole((D_MODEL, 512)), _whole((D_MODEL, 512)), _whole((D_MODEL, D_MODEL))],
        out_specs=(_rows(tm, D_MODEL), _rows(tm, 512), _rows(tm, 512), _rows(tm, 512), _rows(tm, 512),
                   _rows(tm, D_MODEL), _rows(tm, D_MODEL),
                   _whole((D_MODEL, D_MODEL)), _whole((512, D_MODEL)), _whole((512, D_MODEL)),
                   _whole((1, D_MODEL)), _whole((1, D_MODEL)), _whole((1, 128))),
        compiler_params=_params(("arbitrary",)),
    )(x, tgt, oa, ob, sz, mz, ga, gb, gate, gf, wa, wb, wo, wat, wbt, wot)


def _bwdprep_call(dsq, dsk, dsv, dsz, dqn, dqp, dkn, dvv, dkpt, dmz, dga, dgb, cq, ckv, cos256, sin256,
                  qg, kvg, wqt, wkvt):
    s_len = cq.shape[0]
    tm = min(ROW_TILE, s_len)

    def body(dsq_ref, dsk_ref, dsv_ref, dsz_ref, dqn_ref, dqp_ref, dkn_ref, dvv_ref, dkpt_ref, dmz_ref,
             dga_ref, dgb_ref, cq_ref, ckv_ref, cos_ref, sin_ref, qg_ref, kvg_ref, wqt_ref, wkvt_ref,
             dp_ref, dwq_ref, dwkv_ref, dqg_ref, dkvg_ref):
        @pl.when(pl.program_id(0) == 0)
        def _():
            dwq_ref[...] = jnp.zeros_like(dwq_ref)
            dwkv_ref[...] = jnp.zeros_like(dwkv_ref)
            dqg_ref[...] = jnp.zeros_like(dqg_ref)
            dkvg_ref[...] = jnp.zeros_like(dkvg_ref)

        cos = cos_ref[...]
        sin = sin_ref[...]
        dp_ref[:, O_SQ:O_SK] = dsq_ref[...]
        dp_ref[:, O_SK:O_SV] = jnp.transpose(dsk_ref[...]).astype(BF16)
        dp_ref[:, O_SV:O_SZ] = jnp.transpose(dsv_ref[...]).astype(BF16)
        dp_ref[:, O_SZ:O_CQ] = dsz_ref[...]
        dp_ref[:, O_MZ:O_GA] = dmz_ref[...]
        dp_ref[:, O_GA:O_GB] = dga_ref[...]
        dp_ref[:, O_GB:O_KR] = dgb_ref[...]
        dkp = jnp.transpose(dkpt_ref[...])
        dp_ref[:, O_KR:O_KR + 128] = (dkp * cos[:, :128]).astype(BF16)
        dp_ref[:, O_KR + 128:O_END] = (dkp * sin[:, :128]).astype(BF16)
        dp_ref[:, O_END:W_INT] = jnp.zeros((tm, W_INT - O_END), BF16)

        cq = cq_ref[...]
        rq = lax.rsqrt(jnp.mean(cq * cq, axis=-1, keepdims=True) + EPS)
        cqh = cq * rq
        qg = qg_ref[...]
        cqn = (cqh * qg).astype(BF16)
        dqp = dqp_ref[...].astype(F32)
        dqa = jnp.concatenate([dqn_ref[...], (dqp * cos).astype(BF16), (dqp * sin).astype(BF16)], axis=1)
        dcqn = _dot(dqa, wqt_ref[...])
        dwq_ref[...] += _dot_tn(cqn, dqa)
        dqg_ref[...] += jnp.sum(dcqn * cqh, axis=0, keepdims=True)
        dh = dcqn * qg
        dcq = rq * (dh - cqh * jnp.mean(dh * cqh, axis=-1, keepdims=True))
        dp_ref[:, O_CQ:O_CKV] = dcq.astype(BF16)

        ckv = ckv_ref[...]
        rk = lax.rsqrt(jnp.mean(ckv * ckv, axis=-1, keepdims=True) + EPS)
        ckh = ckv * rk
        kvg = kvg_ref[...]
        ckvn = (ckh * kvg).astype(BF16)
        dkva = jnp.concatenate([jnp.transpose(dkn_ref[...]).astype(BF16),
                                jnp.transpose(dvv_ref[...]).astype(BF16)], axis=1)
        dckvn = _dot(dkva, wkvt_ref[...])
        dwkv_ref[...] += _dot_tn(ckvn, dkva)
        dkvg_ref[...] += jnp.sum(dckvn * ckh, axis=0, keepdims=True)
        dh2 = dckvn * kvg
        dckv = rk * (dh2 - ckh * jnp.mean(dh2 * ckh, axis=-1, keepdims=True))
        dp_ref[:, O_CKV:O_MZ] = dckv.astype(BF16)

    return pl.pallas_call(
        body, name="bwdprep", grid=(s_len // tm,),
        out_shape=(_sds((s_len, W_INT), BF16), _sds((Q_RANK, 1024), F32), _sds((KV_RANK, 1024), F32),
                   _sds((1, Q_RANK), F32), _sds((1, KV_RANK), F32)),
        in_specs=[_rows(tm, 512), _cols(512, tm), _cols(512, tm), _rows(tm, 512), _rows(tm, 512), _rows(tm, 256),
                  _cols(512, tm), _cols(512, tm), _cols(128, tm), _rows(tm, 512), _rows(tm, D_MODEL),
                  _rows(tm, D_MODEL), _rows(tm, Q_RANK), _rows(tm, KV_RANK), _rows(tm, 256), _rows(tm, 256),
                  _whole((1, Q_RANK)), _whole((1, KV_RANK)), _whole((1024, Q_RANK)), _whole((1024, KV_RANK))],
        out_specs=(_rows(tm, W_INT), _whole((Q_RANK, 1024)), _whole((KV_RANK, 1024)),
                   _whole((1, Q_RANK)), _whole((1, KV_RANK))),
        compiler_params=_params(("arbitrary",)),
    )(dsq, dsk, dsv, dsz, dqn, dqp, dkn, dvv, dkpt, dmz, dga, dgb, cq, ckv, cos256, sin256, qg, kvg, wqt, wkvt)


def _dh_call(dproj, w_int_t, x, dx2, scale, g1):
    s_len = x.shape[0]
    tm = min(2 * ROW_TILE, s_len)

    def body(dp_ref, wt_ref, x_ref, dx2_ref, sc_ref, g1_ref, gx_ref, dsh_ref, dsc_ref, dg1_ref):
        @pl.when(pl.program_id(0) == 0)
        def _():
            dsh_ref[...] = jnp.zeros_like(dsh_ref)
            dsc_ref[...] = jnp.zeros_like(dsc_ref)
            dg1_ref[...] = jnp.zeros_like(dg1_ref)

        dh = _dot(dp_ref[...], wt_ref[...])
        xt = x_ref[...]
        r = lax.rsqrt(jnp.mean(xt * xt, axis=-1, keepdims=True) + EPS)
        xh = xt * r
        g1 = g1_ref[...]
        xg = xh * g1
        dsh_ref[...] += jnp.sum(dh, axis=0, keepdims=True)
        dsc_ref[...] += jnp.sum(dh * xg, axis=0, keepdims=True)
        dxg = dh * (1.0 + sc_ref[...])
        dg1_ref[...] += jnp.sum(dxg * xh, axis=0, keepdims=True)
        dxh = dxg * g1
        gx_ref[...] = dx2_ref[...] + r * (dxh - xh * jnp.mean(dxh * xh, axis=-1, keepdims=True))

    return pl.pallas_call(
        body, name="dh", grid=(s_len // tm,),
        out_shape=(_sds((s_len, D_MODEL), F32), _sds((1, D_MODEL), F32), _sds((1, D_MODEL), F32),
                   _sds((1, D_MODEL), F32)),
        in_specs=[_rows(tm, W_INT), _whole((W_INT, D_MODEL)), _rows(tm, D_MODEL), _rows(tm, D_MODEL),
                  _whole((1, D_MODEL)), _whole((1, D_MODEL))],
        out_specs=(_rows(tm, D_MODEL), _whole((1, D_MODEL)), _whole((1, D_MODEL)), _whole((1, D_MODEL))),
        compiler_params=_params(("arbitrary",)),
    )(dproj, w_int_t, x, dx2, scale, g1)


def _small_call(svg, ct, dmod_sh):
    def body(sv_ref, ct_ref, dm_ref, tot_ref, gwada_ref):
        acc = sv_ref[0:1, :]
        for d in range(1, N_DEV):
            acc = acc + sv_ref[d:d + 1, :]
        tot_ref[...] = acc
        gwada_ref[...] = lax.dot_general(ct_ref[...], dm_ref[...], (((1,), (0,)), ((), ())),
                                         precision=lax.Precision.HIGHEST, preferred_element_type=F32)

    vmem = pl.BlockSpec(memory_space=pltpu.VMEM)
    return pl.pallas_call(
        body, name="small_grads",
        out_shape=(_sds((1, 8 * SV_COLS), F32), _sds((D_MODEL, 768), F32)),
        in_specs=[vmem, vmem, vmem], out_specs=(vmem, vmem),
        compiler_params=_params(),
    )(svg, ct, dmod_sh)


def _adamw_tile_rows(rows, cols):
    budget = 2 << 20
    if rows * cols * 4 <= budget or rows % 8:
        return rows
    best = 8
    for tr in range(8, rows + 1, 8):
        if rows % tr == 0 and tr * cols * 4 <= budget:
            best = tr
    return best


def _adamw_call(name, w, g, m, v):
    rows, cols = w.shape
    tr = _adamw_tile_rows(rows, cols)

    def body(w_ref, g_ref, m_ref, v_ref, d_ref, nm_ref, nv_ref):
        gg = g_ref[...]
        m2 = ADAM_B1 * m_ref[...] + (1.0 - ADAM_B1) * gg
        v2 = ADAM_B2 * v_ref[...] + (1.0 - ADAM_B2) * (gg * gg)
        m_hat = m2 / (1.0 - ADAM_B1 ** ADAM_STEP)
        v_hat = v2 / (1.0 - ADAM_B2 ** ADAM_STEP)
        d_ref[...] = -ADAM_LR * (m_hat / (jnp.sqrt(v_hat) + ADAM_EPS) + ADAM_WD * w_ref[...])
        nm_ref[...] = m2
        nv_ref[...] = v2

    spec = pl.BlockSpec((tr, cols), lambda i: (i, 0))
    return pl.pallas_call(
        body, name="adamw_" + name, grid=(rows // tr,),
        out_shape=(_sds((rows, cols), F32),) * 3,
        in_specs=[spec] * 4, out_specs=(spec,) * 3,
        compiler_params=_params(("parallel",)),
    )(w, g, m, v)


IN_SHARD = IN_WIDTH // N_CHIPS
HALF_D = D_MODEL // 2
SMALL_ROWS = (576, 512, 1024, 1024, 2048)
SMALL_TOTAL = sum(SMALL_ROWS)
SMALL_HALF = SMALL_TOTAL // 2
SMALL_SUM_ROWS = 432


def _gather_call(c_row, w_ada_sh, pack_in, pack_small):
    def body(c_ref, wada_ref, pki_ref, pks_ref, mg_ref, cg_ref, gwi_ref, gws_ref,
             cv, ssem_c, rsem_c, ssem_m, rsem_m, ssem_w, rsem_w, ssem_f, rsem_f, lsem):
        x, y, c = lax.axis_index("x"), lax.axis_index("y"), lax.axis_index("c")
        me = 4 * x + 2 * y + c
        chip = 2 * x + y
        rel3 = [(1, 0), (0, 1), (1, 1)]
        packs = [(pki_ref, gwi_ref), (pks_ref, gws_ref)]

        sends = []
        for j, (dx, dy) in enumerate(rel3):
            for a, (pk, gw) in enumerate(packs):
                cp = pltpu.make_async_remote_copy(
                    src_ref=pk.at[c], dst_ref=gw.at[chip, c], send_sem=ssem_w.at[j, a], recv_sem=rsem_w.at[j, a],
                    device_id=(_flip(x, dx), _flip(y, dy), c), device_id_type=MESH)
                cp.start()
                sends.append(cp)
        owns = []
        for a, (pk, gw) in enumerate(packs):
            own = pltpu.make_async_copy(pk, gw.at[chip], lsem.at[a])
            own.start()
            owns.append(own)

        cv[me] = c_ref[...]
        for r in range(1, N_DEV):
            dx, dy, dc = (r >> 2) & 1, (r >> 1) & 1, r & 1
            cp = pltpu.make_async_remote_copy(
                src_ref=c_ref, dst_ref=cv.at[me], send_sem=ssem_c.at[r - 1], recv_sem=rsem_c.at[r - 1],
                device_id=(_flip(x, dx), _flip(y, dy), _flip(c, dc)), device_id_type=MESH)
            cp.start()
            sends.append(cp)
        for r in range(1, N_DEV):
            dx, dy, dc = (r >> 2) & 1, (r >> 1) & 1, r & 1
            src = 4 * _flip(x, dx) + 2 * _flip(y, dy) + _flip(c, dc)
            pltpu.make_async_remote_copy(
                src_ref=c_ref, dst_ref=cv.at[src], send_sem=ssem_c.at[r - 1], recv_sem=rsem_c.at[r - 1],
                device_id=(x, y, c), device_id_type=MESH).wait_recv()
        rows = lax.broadcasted_iota(jnp.int32, (N_DEV, D_MODEL), 0)
        call = jnp.zeros((N_DEV, D_MODEL), F32)
        for b in range(N_DEV):
            call = jnp.where(rows == b, jnp.broadcast_to(cv[b], (N_DEV, D_MODEL)), call)
        cg_ref[...] = call

        mg_ref[chip] = lax.dot_general(call, wada_ref[...], (((1,), (0,)), ((), ())),
                                       precision=lax.Precision.HIGHEST, preferred_element_type=F32)
        for j, (dx, dy) in enumerate(rel3):
            cp = pltpu.make_async_remote_copy(
                src_ref=mg_ref.at[chip], dst_ref=mg_ref.at[chip], send_sem=ssem_m.at[j], recv_sem=rsem_m.at[j],
                device_id=(_flip(x, dx), _flip(y, dy), c), device_id_type=MESH)
            cp.start()
            sends.append(cp)
        for j, (dx, dy) in enumerate(rel3):
            src_chip = 2 * _flip(x, dx) + _flip(y, dy)
            pltpu.make_async_remote_copy(
                src_ref=mg_ref.at[src_chip], dst_ref=mg_ref.at[src_chip], send_sem=ssem_m.at[j],
                recv_sem=rsem_m.at[j], device_id=(x, y, c), device_id_type=MESH).wait_recv()
        for j, (dx, dy) in enumerate(rel3):
            src_chip = 2 * _flip(x, dx) + _flip(y, dy)
            for a, (pk, gw) in enumerate(packs):
                pltpu.make_async_remote_copy(
                    src_ref=pk.at[c], dst_ref=gw.at[src_chip, c], send_sem=ssem_w.at[j, a],
                    recv_sem=rsem_w.at[j, a], device_id=(x, y, c), device_id_type=MESH).wait_recv()
                cp = pltpu.make_async_remote_copy(
                    src_ref=gw.at[src_chip, c], dst_ref=gw.at[src_chip, c], send_sem=ssem_f.at[j, a],
                    recv_sem=rsem_f.at[j, a], device_id=(x, y, 1 - c), device_id_type=MESH)
                cp.start()
                sends.append(cp)
        for j, (dx, dy) in enumerate(rel3):
            src_chip = 2 * _flip(x, dx) + _flip(y, dy)
            for a, (pk, gw) in enumerate(packs):
                pltpu.make_async_remote_copy(
                    src_ref=pk.at[c], dst_ref=gw.at[src_chip, 1 - c], send_sem=ssem_f.at[j, a],
                    recv_sem=rsem_f.at[j, a], device_id=(x, y, c), device_id_type=MESH).wait_recv()
        for cp in sends:
            cp.wait_send()
        for own in owns:
            own.wait()

    vmem = pl.BlockSpec(memory_space=pltpu.VMEM)
    return pl.pallas_call(
        body, name="gather_fwd",
        out_shape=(_sds((N_CHIPS, N_DEV, 768), F32), _sds((N_DEV, D_MODEL), F32),
                   _sds((N_CHIPS, 2, IN_SHARD, HALF_D), BF16), _sds((N_CHIPS, 2, SMALL_HALF, LANES), BF16)),
        in_specs=[vmem, vmem, vmem, vmem], out_specs=(vmem, vmem, vmem, vmem),
        scratch_shapes=[
            pltpu.VMEM((N_DEV, 1, D_MODEL), F32),
            pltpu.SemaphoreType.DMA((N_DEV - 1,)), pltpu.SemaphoreType.DMA((N_DEV - 1,)),
            pltpu.SemaphoreType.DMA((3,)), pltpu.SemaphoreType.DMA((3,)),
            pltpu.SemaphoreType.DMA((3, 2)), pltpu.SemaphoreType.DMA((3, 2)),
            pltpu.SemaphoreType.DMA((3, 2)), pltpu.SemaphoreType.DMA((3, 2)),
            pltpu.SemaphoreType.DMA((2,)),
        ],
        compiler_params=_params(),
    )(c_row, w_ada_sh, pack_in, pack_small)


def _reduce_call(g_in, g_small, sv):
    def body(gi_ref, gs_ref, sv_ref, fi_ref, fs_ref, svg_ref, pair_i, pair_s, send_i, send_s, land_i, land_s,
             ssem_p, rsem_p, ssem_g, rsem_g, ssem_s, rsem_s, ssem_x, rsem_x):
        x, y, c = lax.axis_index("x"), lax.axis_index("y"), lax.axis_index("c")
        me = 4 * x + 2 * y + c
        chip = 2 * x + y
        rel3 = [(1, 0), (0, 1), (1, 1)]
        payloads = [(gi_ref, pair_i, send_i, land_i, fi_ref), (gs_ref, pair_s, send_s, land_s, fs_ref)]
        copies = []

        for k in range(N_CHIPS):
            for a, (g, pair, _, _, _) in enumerate(payloads):
                cp = pltpu.make_async_remote_copy(
                    src_ref=g.at[2 * k + 1 - c], dst_ref=pair.at[k], send_sem=ssem_p.at[k, a],
                    recv_sem=rsem_p.at[k, a], device_id=(x, y, 1 - c), device_id_type=MESH)
                cp.start()
                copies.append(cp)

        for r in range(1, N_DEV):
            dx, dy, dc = (r >> 2) & 1, (r >> 1) & 1, r & 1
            cp = pltpu.make_async_remote_copy(
                src_ref=sv_ref, dst_ref=svg_ref.at[me], send_sem=ssem_s.at[r - 1], recv_sem=rsem_s.at[r - 1],
                device_id=(_flip(x, dx), _flip(y, dy), _flip(c, dc)), device_id_type=MESH)
            cp.start()
            copies.append(cp)
        svg_ref[me] = sv_ref[...]

        for k in range(N_CHIPS):
            for a, (g, pair, _, _, _) in enumerate(payloads):
                pltpu.make_async_remote_copy(
                    src_ref=g.at[2 * k + c], dst_ref=pair.at[k], send_sem=ssem_p.at[k, a],
                    recv_sem=rsem_p.at[k, a], device_id=(x, y, c), device_id_type=MESH).wait_recv()

        def pair_sum(k, store_in, store_small):
            for qd in range(HALF_D // LANES):
                sl = slice(LANES * qd, LANES * qd + LANES)
                store_in(sl, gi_ref[2 * k + c, :, sl].astype(F32) + pair_i[k, :, sl].astype(F32))

            def rows(i, carry):
                sl = pl.ds(pl.multiple_of(i * SMALL_SUM_ROWS, 16), SMALL_SUM_ROWS)
                store_small(sl, gs_ref[2 * k + c, sl, :].astype(F32) + pair_s[k, sl, :].astype(F32))
                return carry

            lax.fori_loop(0, SMALL_HALF // SMALL_SUM_ROWS, rows, 0)

        for j, (dx, dy) in enumerate(rel3):
            tx, ty = _flip(x, dx), _flip(y, dy)

            def put_in(sl, val, j=j):
                send_i[j, :, sl] = val.astype(BF16)

            def put_small(sl, val, j=j):
                send_s[j, sl, :] = val.astype(BF16)

            pair_sum(2 * tx + ty, put_in, put_small)
            for a, (_, _, send, land, _) in enumerate(payloads):
                cp = pltpu.make_async_remote_copy(
                    src_ref=send.at[j], dst_ref=land.at[j], send_sem=ssem_g.at[j, a], recv_sem=rsem_g.at[j, a],
                    device_id=(tx, ty, c), device_id_type=MESH)
                cp.start()
                copies.append(cp)

        def own_in(sl, val):
            fi_ref[c, :, sl] = val

        def own_small(sl, val):
            fs_ref[c, sl, :] = val

        pair_sum(chip, own_in, own_small)
        for j in range(3):
            for a, (_, _, send, land, _) in enumerate(payloads):
                pltpu.make_async_remote_copy(
                    src_ref=send.at[j], dst_ref=land.at[j], send_sem=ssem_g.at[j, a], recv_sem=rsem_g.at[j, a],
                    device_id=(x, y, c), device_id_type=MESH).wait_recv()
            for qd in range(HALF_D // LANES):
                sl = slice(LANES * qd, LANES * qd + LANES)
                fi_ref[c, :, sl] += land_i[j, :, sl].astype(F32)

            def add_rows(i, carry, j=j):
                sl = pl.ds(pl.multiple_of(i * SMALL_SUM_ROWS, 16), SMALL_SUM_ROWS)
                fs_ref[c, sl, :] += land_s[j, sl, :].astype(F32)
                return carry

            lax.fori_loop(0, SMALL_HALF // SMALL_SUM_ROWS, add_rows, 0)

        for a, f in enumerate((fi_ref, fs_ref)):
            cp = pltpu.make_async_remote_copy(
                src_ref=f.at[c], dst_ref=f.at[c], send_sem=ssem_x.at[a], recv_sem=rsem_x.at[a],
                device_id=(x, y, 1 - c), device_id_type=MESH)
            cp.start()
            copies.append(cp)
        for a, f in enumerate((fi_ref, fs_ref)):
            pltpu.make_async_remote_copy(
                src_ref=f.at[c], dst_ref=f.at[1 - c], send_sem=ssem_x.at[a], recv_sem=rsem_x.at[a],
                device_id=(x, y, c), device_id_type=MESH).wait_recv()
        for r in range(1, N_DEV):
            dx, dy, dc = (r >> 2) & 1, (r >> 1) & 1, r & 1
            src = 4 * _flip(x, dx) + 2 * _flip(y, dy) + _flip(c, dc)
            pltpu.make_async_remote_copy(
                src_ref=sv_ref, dst_ref=svg_ref.at[src], send_sem=ssem_s.at[r - 1],
                recv_sem=rsem_s.at[r - 1], device_id=(x, y, c), device_id_type=MESH).wait_recv()
        for cp in copies:
            cp.wait_send()

    vmem = pl.BlockSpec(memory_space=pltpu.VMEM)
    return pl.pallas_call(
        body, name="grad_reduce",
        out_shape=(_sds((2, IN_SHARD, HALF_D), F32), _sds((2, SMALL_HALF, LANES), F32),
                   _sds((N_DEV, 8, SV_COLS), F32)),
        in_specs=[vmem, vmem, vmem], out_specs=(vmem, vmem, vmem),
        scratch_shapes=[
            pltpu.VMEM((N_CHIPS, IN_SHARD, HALF_D), BF16), pltpu.VMEM((N_CHIPS, SMALL_HALF, LANES), BF16),
            pltpu.VMEM((3, IN_SHARD, HALF_D), BF16), pltpu.VMEM((3, SMALL_HALF, LANES), BF16),
            pltpu.VMEM((3, IN_SHARD, HALF_D), BF16), pltpu.VMEM((3, SMALL_HALF, LANES), BF16),
            pltpu.SemaphoreType.DMA((N_CHIPS, 2)), pltpu.SemaphoreType.DMA((N_CHIPS, 2)),
            pltpu.SemaphoreType.DMA((3, 2)), pltpu.SemaphoreType.DMA((3, 2)),
            pltpu.SemaphoreType.DMA((N_DEV - 1,)), pltpu.SemaphoreType.DMA((N_DEV - 1,)),
            pltpu.SemaphoreType.DMA((2,)), pltpu.SemaphoreType.DMA((2,)),
        ],
        compiler_params=_params(),
    )(g_in, g_small, sv)


def _dwin_call(h, dproj):
    s_len = h.shape[0]
    tm = min(4 * ROW_TILE, s_len)
    nrow = s_len // tm
    nc = 4
    chunk = W_INT // nc

    def body(h_ref, dp_ref, dw_ref, acc):
        i = pl.program_id(1)

        @pl.when(i == 0)
        def _():
            acc[...] = jnp.zeros_like(acc)

        acc[...] += _dot_tn(dp_ref[...], h_ref[...])

        @pl.when(i == nrow - 1)
        def _():
            dw_ref[...] = acc[...].astype(BF16)

    return pl.pallas_call(
        body, name="dwin", grid=(nc, nrow),
        out_shape=_sds((W_INT, D_MODEL), BF16),
        in_specs=[pl.BlockSpec((tm, D_MODEL), lambda c, i: (i, 0)),
                  pl.BlockSpec((tm, chunk), lambda c, i: (i, c))],
        out_specs=pl.BlockSpec((chunk, D_MODEL), lambda c, i: (c, 0)),
        scratch_shapes=[pltpu.VMEM((chunk, D_MODEL), F32)],
        compiler_params=_params(("parallel", "arbitrary")),
    )(h, dproj)


def _swap_rows(w, group):
    r, n = w.shape
    return w.reshape(r // group, 2, group // 2, n)[:, ::-1].reshape(r, n)


def _internal_weights(w_in_t, w_uq, w_ukv):
    krot_t = w_in_t[2688:2720]
    w_int_t = jnp.concatenate([
        w_in_t[0:512] * jnp.asarray(0.125, w_in_t.dtype), w_in_t[512:2048],
        w_in_t[2048:2432], w_in_t[2432:2688], w_in_t[2720:3232], w_in_t[3232:4256], w_in_t[4256:5280],
        jnp.tile(krot_t, (4, 1)), jnp.tile(_swap_rows(krot_t, 32), (4, 1)),
        jnp.zeros((W_INT - O_END, D_MODEL), w_in_t.dtype)], axis=0)
    uq = w_uq.reshape(Q_RANK, N_HEADS, 96)
    wp = uq[:, :, 64:].reshape(Q_RANK, 256)
    w_q = jnp.concatenate([uq[:, :, :64].reshape(Q_RANK, 512), wp, _swap_halves(wp, 32)], axis=1)
    ukv = w_ukv.reshape(KV_RANK, N_HEADS, 128)
    w_kv = jnp.concatenate([ukv[:, :, :64].reshape(KV_RANK, 512), ukv[:, :, 64:].reshape(KV_RANK, 512)], axis=1)
    return w_int_t, w_q, w_kv


def _true_weight_grads(dwi_t, dwq, dwkv):
    dkr = dwi_t[O_KR:O_KR + 128].astype(F32).reshape(4, 32, D_MODEL).sum(axis=0)
    dkr_sw = dwi_t[O_KR + 128:O_END].astype(F32).reshape(4, 32, D_MODEL).sum(axis=0)
    dkrot_t = (dkr + _swap_rows(dkr_sw, 32)).astype(dwi_t.dtype)
    g_in_t = jnp.concatenate([
        dwi_t[0:512] * jnp.asarray(0.125, dwi_t.dtype), dwi_t[512:2048], dwi_t[O_CQ:O_CKV], dwi_t[O_CKV:O_MZ],
        dkrot_t, dwi_t[O_MZ:O_GA], dwi_t[O_GA:O_GB], dwi_t[O_GB:O_KR]], axis=0)
    dwp = dwq[:, 512:768] + _swap_halves(dwq[:, 768:1024], 32)
    g_uq = jnp.concatenate([dwq[:, :512].reshape(Q_RANK, N_HEADS, 64), dwp.reshape(Q_RANK, N_HEADS, 32)],
                           axis=2).reshape(Q_RANK, 768)
    g_ukv = jnp.concatenate([dwkv[:, :512].reshape(KV_RANK, N_HEADS, 64), dwkv[:, 512:].reshape(KV_RANK, N_HEADS, 64)],
                            axis=2).reshape(KV_RANK, 1024)
    return g_in_t, g_uq, g_ukv


def _swap_halves(w, group):
    r, n = w.shape
    return w.reshape(r, n // group, 2, group // 2)[:, :, ::-1, :].reshape(r, n)


def _pack_shards(parts):
    return jnp.concatenate([p.reshape(-1, LANES) for p in parts], axis=0)


def _unpack_small(gw):
    offs = [0]
    for r in SMALL_ROWS:
        offs.append(offs[-1] + r)

    def cols(i, rows, shard_cols):
        blk = gw[:, offs[i]:offs[i + 1]].reshape(N_CHIPS, rows, shard_cols)
        return blk.transpose(1, 0, 2).reshape(rows, N_CHIPS * shard_cols)

    return (cols(0, Q_RANK, 192), cols(1, KV_RANK, 256), cols(2, 512, 256), cols(3, 512, 256),
            gw[:, offs[4]:offs[5]].reshape(D_MODEL, D_MODEL))


def _chip_major(g, shard_cols):
    r = g.shape[0]
    return g.reshape(r, N_CHIPS, shard_cols).transpose(1, 0, 2).reshape(N_CHIPS, -1, LANES)


def kernel(x, c, positions, w_ada, b_ada, norm_gain, w_in, q_norm_gain, w_uq, kv_norm_gain, w_ukv, w_branch_a, w_branch_b, w_out, final_norm_gain, loss_target, m_w_ada, m_b_ada, m_norm_gain, m_w_in, m_q_norm_gain, m_w_uq, m_kv_norm_gain, m_w_ukv, m_w_branch_a, m_w_branch_b, m_w_out, m_final_norm_gain, v_w_ada, v_b_ada, v_norm_gain, v_w_in, v_q_norm_gain, v_w_uq, v_kv_norm_gain, v_w_ukv, v_w_branch_a, v_w_branch_b, v_w_out, v_final_norm_gain):
    ix, iy, ic = lax.axis_index("x"), lax.axis_index("y"), lax.axis_index("c")
    me = 4 * ix + 2 * iy + ic
    chip = 2 * ix + iy
    xs = x[0]
    tgt = loss_target[0]
    s_len = xs.shape[0]

    w_in_t = jnp.swapaxes(w_in[0], 0, 1)
    w_in_tb = w_in_t.astype(BF16)
    pack_in = jnp.stack([w_in_tb[:, :HALF_D], w_in_tb[:, HALF_D:]], axis=0)
    small_shards = (w_uq[0], w_ukv[0], w_branch_a[0], w_branch_b[0], w_out[0])
    pack_small = _pack_shards([s.astype(BF16) for s in small_shards]).reshape(2, SMALL_HALF, LANES)
    mg, call, gw_in, gw_small = _gather_call(c, w_ada[0], pack_in, pack_small)
    mod = mg.transpose(1, 0, 2).reshape(N_DEV, 3 * D_MODEL) + b_ada
    mod_me = lax.dynamic_slice_in_dim(mod, me, 1, axis=0)
    shift, scale, gate = mod_me[:, :D_MODEL], mod_me[:, D_MODEL:2 * D_MODEL], mod_me[:, 2 * D_MODEL:]

    f_in_t = jnp.concatenate([gw_in[:, 0], gw_in[:, 1]], axis=2).reshape(IN_WIDTH, D_MODEL)
    f_uq, f_ukv, f_a, f_b, f_out = _unpack_small(gw_small.reshape(N_CHIPS, SMALL_TOTAL, LANES))
    w_int_t, w_q, w_kv = _internal_weights(f_in_t, f_uq, f_ukv)

    inv_freq = ROPE_BASE ** (-jnp.arange(0, ROPE_DIM, 2, dtype=F32) / ROPE_DIM)
    ang = positions[0].astype(F32)[:, None] * inv_freq
    cs, sn = jnp.cos(ang), jnp.sin(ang)
    cos256 = jnp.tile(jnp.concatenate([cs, cs], axis=1), (1, 8))
    sin256 = jnp.tile(jnp.concatenate([-sn, sn], axis=1), (1, 8))

    (h, sq, sk, sv, sz, cq, ckv, mz, ga, gb, kpt, qn, qp, kn, vv) = _inproj_call(
        xs, shift, scale, norm_gain, w_int_t, w_q, w_kv, q_norm_gain, kv_norm_gain, cos256, sin256)
    oa, lt, first = _sb_fwd_call(sq, sk, sv)
    ob, lse = _mla_fwd_call(qn, qp, kn, kpt, vv)

    gf = final_norm_gain.reshape(1, D_MODEL)
    (dx2, doa, dob, dsz, dmz, dga, dgb, dwo, dwa, dwb, dgf, dgate, loss_p) = _post_call(
        xs, tgt, oa, ob, sz, mz, ga, gb, gate, gf, f_a, f_b, f_out, f_a.T, f_b.T, f_out.T)

    dsq, dsk_t, dsv_t = _sb_bwd_call(first[:, :, 0, 0].reshape(-1), sq, sk, sv, doa, lt)
    dqn, dqp, dkn_t, dkpt_t, dvv_t = _mla_bwd_call(qn, qp, kn, kpt, vv, ob, dob, lse)

    dproj, dwq, dwkv, dqg, dkvg = _bwdprep_call(
        dsq, dsk_t, dsv_t, dsz, dqn, dqp, dkn_t, dvv_t, dkpt_t, dmz, dga, dgb, cq, ckv, cos256, sin256,
        q_norm_gain, kv_norm_gain, w_q.T, w_kv.T)
    grad_x, dshift, dscale, dg1 = _dh_call(dproj, w_int_t, xs, dx2, scale, norm_gain)
    dwi_t = _dwin_call(h, dproj)
    g_in_t, g_uq, g_ukv = _true_weight_grads(dwi_t, dwq, dwkv)

    g_in_c = g_in_t.reshape(N_CHIPS, IN_SHARD, D_MODEL)
    g_in_pieces = jnp.stack([g_in_c[:, :, :HALF_D], g_in_c[:, :, HALF_D:]], axis=1).reshape(N_DEV, IN_SHARD, HALF_D)
    g_small = jnp.concatenate([
        _chip_major(g_uq, 192), _chip_major(g_ukv, 256), _chip_major(dwa, 256), _chip_major(dwb, 256),
        dwo.reshape(N_CHIPS, -1, LANES)], axis=1).astype(BF16).reshape(N_DEV, SMALL_HALF, LANES)
    small = jnp.concatenate([
        dshift, dscale, dgate, dg1, dqg, dkvg, dgf, loss_p,
        jnp.zeros((1, 8 * SV_COLS - 5888), F32)], axis=1).reshape(8, SV_COLS)
    full_in, full_small, svg = _reduce_call(g_in_pieces, g_small, small)
    gs_in_t = jnp.concatenate([full_in[0], full_in[1]], axis=1)
    full = full_small.reshape(SMALL_TOTAL, LANES)
    offs = [0]
    for r in SMALL_ROWS:
        offs.append(offs[-1] + r)
    gs_uq = full[offs[0]:offs[1]].reshape(Q_RANK, 192)
    gs_ukv = full[offs[1]:offs[2]].reshape(KV_RANK, 256)
    gs_a = full[offs[2]:offs[3]].reshape(512, 256)
    gs_b = full[offs[3]:offs[4]].reshape(512, 256)
    gs_out = full[offs[4]:offs[5]].reshape(256, D_MODEL)

    svm = svg.reshape(N_DEV, 8 * SV_COLS)
    dmod_sh = lax.dynamic_slice_in_dim(svm[:, :3 * D_MODEL], chip * 768, 768, axis=1)
    tot, gs_ada = _small_call(svm, call.T, dmod_sh)
    g_bada = tot[:, 0:3072]
    g_g1 = tot[:, 3072:4096]
    g_qg = tot[:, 4096:4480]
    g_kvg = tot[:, 4480:4736]
    g_gf = tot[:, 4736:5760]
    loss = tot[0, 5760]

    names = ["w_ada", "b_ada", "norm_gain", "w_in", "q_norm_gain", "w_uq", "kv_norm_gain", "w_ukv",
             "w_branch_a", "w_branch_b", "w_out", "final_norm_gain"]
    ws = [w_ada[0], b_ada, norm_gain, w_in_t, q_norm_gain, w_uq[0], kv_norm_gain, w_ukv[0],
          w_branch_a[0], w_branch_b[0], w_out[0], final_norm_gain.reshape(1, D_MODEL)]
    gs = [gs_ada, g_bada, g_g1, gs_in_t, g_qg, gs_uq, g_kvg, gs_ukv, gs_a, gs_b, gs_out, g_gf]
    ms = [m_w_ada[0], m_b_ada, m_norm_gain, jnp.swapaxes(m_w_in[0], 0, 1), m_q_norm_gain, m_w_uq[0],
          m_kv_norm_gain, m_w_ukv[0], m_w_branch_a[0], m_w_branch_b[0], m_w_out[0],
          m_final_norm_gain.reshape(1, D_MODEL)]
    vs = [v_w_ada[0], v_b_ada, v_norm_gain, jnp.swapaxes(v_w_in[0], 0, 1), v_q_norm_gain, v_w_uq[0],
          v_kv_norm_gain, v_w_ukv[0], v_w_branch_a[0], v_w_branch_b[0], v_w_out[0],
          v_final_norm_gain.reshape(1, D_MODEL)]
    refs = [w_ada, b_ada, norm_gain, w_in, q_norm_gain, w_uq, kv_norm_gain, w_ukv,
            w_branch_a, w_branch_b, w_out, final_norm_gain]
    grads, deltas, new_ms, new_vs = [], [], [], []
    for n, w_, g_, m_, v_, ref in zip(names, ws, gs, ms, vs, refs):
        outs = (g_,) + _adamw_call(n, w_, g_, m_, v_)
        if n == "w_in":
            outs = tuple(jnp.swapaxes(o_, 0, 1) for o_ in outs)
        for lst, o_ in zip((grads, deltas, new_ms, new_vs), outs):
            lst.append(o_.reshape(ref.shape))

    return (loss, grad_x.reshape(x.shape), *grads, *deltas, *new_ms, *new_vs)
```

```python
import math

import jax
import jax.numpy as jnp
from jax import lax
from jax.experimental import pallas as pl
from jax.experimental.pallas import tpu as pltpu

F32 = jnp.float32
BF16 = jnp.bfloat16

D_MODEL = 1024
SB_WIDTH = 512
MLA_WIDTH = 512
Q_RANK = 384
KV_RANK = 256
ROPE_DIM = 32
N_HEADS = 8
IN_WIDTH = 5280
EPS = 1e-6
ROPE_BASE = 10000.0
MLA_SCALE = 1.0 / math.sqrt(96.0)
SB_SCALE = 0.125

ADAM_LR = 0.001
ADAM_B1 = 0.9
ADAM_B2 = 0.999
ADAM_EPS = 1e-08
ADAM_WD = 0.01
ADAM_STEP = 10

O_SQ, O_SK, O_SV, O_SZ, O_CQ, O_CKV, O_MZ, O_GA, O_GB, O_KR, O_END = (
    0, 512, 1024, 1536, 2048, 2432, 2688, 3200, 4224, 5248, 5504)
W_INT = 5632

N_CHIPS = 4
N_DEV = 8
LANES = 128
SV_COLS = 768

ROW_TILE = 256
ATT_TILE = 256
ATT_Q_TILES = 2
FWD_Q_TILES = 4
SB_Q_TILES = 1
MLA_KEY_TILE = 512
VMEM_LIMIT = 56 * 1024 * 1024

MESH = pl.DeviceIdType.MESH


def _dot(a, b):
    return lax.dot_general(a, b, (((1,), (0,)), ((), ())), preferred_element_type=F32)


def _dot_nt(a, b):
    return lax.dot_general(a, b, (((1,), (1,)), ((), ())), preferred_element_type=F32)


def _dot_tn(a, b):
    return lax.dot_general(a, b, (((0,), (0,)), ((), ())), preferred_element_type=F32)


def _sigmoid(z):
    return 1.0 / (1.0 + jnp.exp(-z))


def _params(sem=None):
    if sem is None:
        return pltpu.CompilerParams(vmem_limit_bytes=VMEM_LIMIT)
    return pltpu.CompilerParams(dimension_semantics=sem, vmem_limit_bytes=VMEM_LIMIT)


def _rows(tm, n):
    return pl.BlockSpec((tm, n), lambda i: (i, 0))


def _cols(n, tm):
    return pl.BlockSpec((n, tm), lambda i: (0, i))


def _whole(shape):
    nd = len(shape)
    return pl.BlockSpec(shape, lambda i: (0,) * nd)


def _sds(shape, dtype):
    return jax.ShapeDtypeStruct(shape, dtype)


def _flip(v, d):
    return 1 - v if d else v


def _inproj_call(x, shift, scale, g1, w_int, w_q, w_kv, qg, kvg, cos256, sin256):
    s_len = x.shape[0]
    tm = min(ROW_TILE, s_len)

    def body(x_ref, sh_ref, sc_ref, g1_ref, w_ref, wq_ref, wkv_ref, qg_ref, kvg_ref, cos_ref, sin_ref,
             h_ref, sq_ref, sk_ref, sv_ref, sz_ref, cq_ref, ckv_ref, mz_ref, ga_ref, gb_ref, kpt_ref,
             qn_ref, qp_ref, kn_ref, vv_ref):
        xt = x_ref[...]
        r = lax.rsqrt(jnp.mean(xt * xt, axis=-1, keepdims=True) + EPS)
        h = (xt * r * g1_ref[...]) * (1.0 + sc_ref[...]) + sh_ref[...]
        hb = h.astype(BF16)
        h_ref[...] = hb

        def seg(a, b):
            return _dot_nt(hb, w_ref[a:b, :])

        sq_ref[...] = (seg(O_SQ, O_SK) * SB_SCALE).astype(BF16)
        sk_ref[...] = seg(O_SK, O_SV).astype(BF16)
        sv_ref[...] = seg(O_SV, O_SZ).astype(BF16)
        sz_ref[...] = seg(O_SZ, O_CQ)
        mz_ref[...] = seg(O_MZ, O_GA)
        ga_ref[...] = seg(O_GA, O_GB)
        gb_ref[...] = seg(O_GB, O_KR)
        cos = cos_ref[...]
        sin = sin_ref[...]
        kr = seg(O_KR, O_END)
        kpt_ref[...] = (kr[:, :128] * cos[:, :128] + kr[:, 128:] * sin[:, :128]).astype(BF16)

        cq = seg(O_CQ, O_CKV)
        cq_ref[...] = cq
        rq = lax.rsqrt(jnp.mean(cq * cq, axis=-1, keepdims=True) + EPS)
        cqn = (cq * rq * qg_ref[...]).astype(BF16)
        qa = _dot(cqn, wq_ref[...])
        qn_ref[...] = qa[:, :512].astype(BF16)
        qp_ref[...] = (qa[:, 512:768] * cos + qa[:, 768:] * sin).astype(BF16)

        ckv = seg(O_CKV, O_MZ)
        ckv_ref[...] = ckv
        rk = lax.rsqrt(jnp.mean(ckv * ckv, axis=-1, keepdims=True) + EPS)
        ckvn = (ckv * rk * kvg_ref[...]).astype(BF16)
        kva = _dot(ckvn, wkv_ref[...])
        kn_ref[...] = kva[:, :512].astype(BF16)
        vv_ref[...] = kva[:, 512:].astype(BF16)

    outs = [
        (D_MODEL, BF16), (512, BF16), (512, BF16), (512, BF16), (512, F32), (Q_RANK, F32), (KV_RANK, F32),
        (512, F32), (D_MODEL, F32), (D_MODEL, F32), (128, BF16), (512, BF16), (256, BF16), (512, BF16), (512, BF16),
    ]
    return pl.pallas_call(
        body, name="inproj", grid=(s_len // tm,),
        out_shape=tuple(_sds((s_len, n), dt) for n, dt in outs),
        in_specs=[_rows(tm, D_MODEL), _whole((1, D_MODEL)), _whole((1, D_MODEL)), _whole((1, D_MODEL)),
                  _whole((W_INT, D_MODEL)), _whole((Q_RANK, 1024)), _whole((KV_RANK, 1024)),
                  _whole((1, Q_RANK)), _whole((1, KV_RANK)), _rows(tm, 256), _rows(tm, 256)],
        out_specs=tuple(_rows(tm, n) for n, _ in outs),
        compiler_params=_params(("parallel",)),
    )(x, shift, scale, g1, w_int, w_q, w_kv, qg, kvg, cos256, sin256)


Z_CLAMP = 80.0
RUN_CUTOFF = 110.0


def _softplus_clamped(z):
    zc = jnp.minimum(z, Z_CLAMP)
    return zc, jnp.log(1.0 + jnp.exp(zc))


def _tri_sum(a, tri):
    return _dot(a.astype(BF16), tri)


def _sb_fwd_call(q, k, v):
    s_len = q.shape[0]
    tk = min(ATT_TILE, s_len)
    tq = min(SB_Q_TILES * ATT_TILE, s_len)
    r = tq // tk
    nq = s_len // tq

    def body(q_ref, k_ref, v_ref, o_ref, lt_ref, first_ref):
        i = pl.program_id(1)
        q2 = q_ref[...]
        lane = lax.broadcasted_iota(jnp.int32, (1, 256), 1)
        krow = lax.broadcasted_iota(jnp.int32, (tk, tk), 0)
        kcol = lax.broadcasted_iota(jnp.int32, (tk, tk), 1)
        row = lax.broadcasted_iota(jnp.int32, (tq, tk), 0)
        col = lax.broadcasted_iota(jnp.int32, (tq, tk), 1)
        later = (krow > kcol).astype(BF16)
        valids = [col + u * tk < row for u in range(r)]
        hms = [(lane // 64) == hh for hh in range(4)]
        qms = [jnp.where(hm, q2, jnp.zeros_like(q2)) for hm in hms]

        def block(j, carry, valid):
            runs, acc = list(carry[:4]), carry[4]
            off = pl.multiple_of(j * tk, tk)
            kb = k_ref[pl.ds(off, tk), :]
            vb = v_ref[pl.ds(off, tk), :]
            ws = []
            for hh in range(4):
                zc, sp = _softplus_clamped(_dot_nt(qms[hh], kb))
                lm = jnp.where(valid, sp, 0.0) if valid is not None else sp
                suf = _tri_sum(lm, later)
                w = jnp.exp(zc - sp - suf - runs[hh])
                if valid is not None:
                    w = jnp.where(valid, w, 0.0)
                ws.append(w.astype(BF16))
                runs[hh] = runs[hh] + jnp.sum(lm, axis=1, keepdims=True)
            vstack = jnp.concatenate([jnp.where(hm, vb, jnp.zeros_like(vb)) for hm in hms], axis=0)
            acc = acc + _dot(jnp.concatenate(ws, axis=1), vstack)
            return (*runs, acc)

        zero = jnp.zeros((tq, 1), F32)
        carry = (zero, zero, zero, zero, jnp.zeros((tq, 256), F32))
        for u in reversed(range(r)):
            carry = block(i * r + u, carry, valids[u])

        def least_run(runs):
            return jnp.min(jnp.minimum(jnp.minimum(runs[0], runs[1]), jnp.minimum(runs[2], runs[3])))

        n_full = i * r

        def unfinished(state):
            return jnp.logical_and(state[0] < n_full, state[1] <= RUN_CUTOFF)

        def visit(state):
            cr = block(n_full - 1 - state[0], state[2:], None)
            return (state[0] + 1, least_run(cr[:4]), *cr)

        state = lax.while_loop(unfinished, visit, (jnp.int32(0), least_run(carry[:4]), *carry))
        carry = state[2:]
        first_ref[...] = jnp.full(first_ref.shape, n_full - state[0], jnp.int32)
        for hh in range(4):
            lt_ref[0, :, hh:hh + 1] = carry[hh]
        o_ref[...] = carry[4]

    return pl.pallas_call(
        body, name="sb_fwd", grid=(2, nq),
        out_shape=(_sds((s_len, SB_WIDTH), F32), _sds((2, s_len, 4), F32), _sds((2, nq, 8, 128), jnp.int32)),
        in_specs=[pl.BlockSpec((tq, 256), lambda g, i: (i, g)),
                  pl.BlockSpec((s_len, 256), lambda g, i: (0, g)),
                  pl.BlockSpec((s_len, 256), lambda g, i: (0, g))],
        out_specs=(pl.BlockSpec((tq, 256), lambda g, i: (i, g)),
                   pl.BlockSpec((1, tq, 4), lambda g, i: (g, i, 0)),
                   pl.BlockSpec((1, 1, 8, 128), lambda g, i: (g, i, 0, 0))),
        compiler_params=_params(("parallel", "parallel")),
    )(q, k, v)


def _sb_bwd_call(first, q, k, v, do, lt):
    s_len = q.shape[0]
    tk = min(ATT_TILE, s_len)
    tq = min(SB_Q_TILES * ATT_TILE, s_len)
    r = tq // tk
    nq = s_len // tq
    nq_fwd = first.shape[0] // 2
    per_fwd = nq // nq_fwd

    def body(first_ref, q_ref, k_ref, v_ref, do_ref, lt_ref, dq_ref, dk_ref, dv_ref):
        g = pl.program_id(0)
        i = pl.program_id(1)

        @pl.when(i == 0)
        def _():
            dk_ref[...] = jnp.zeros_like(dk_ref)
            dv_ref[...] = jnp.zeros_like(dv_ref)

        q2 = q_ref[...]
        do2 = do_ref[...].astype(BF16)
        lane = lax.broadcasted_iota(jnp.int32, (1, 256), 1)
        krow = lax.broadcasted_iota(jnp.int32, (tk, tk), 0)
        kcol = lax.broadcasted_iota(jnp.int32, (tk, tk), 1)
        row = lax.broadcasted_iota(jnp.int32, (tq, tk), 0)
        col = lax.broadcasted_iota(jnp.int32, (tq, tk), 1)
        earlier = (krow < kcol).astype(BF16)
        later = (krow > kcol).astype(BF16)
        valids = [col + u * tk < row for u in range(r)]
        hms = [(lane // 64) == hh for hh in range(4)]
        qms = [jnp.where(hm, q2, jnp.zeros_like(q2)) for hm in hms]
        doms = [jnp.where(hm, do2, jnp.zeros_like(do2)) for hm in hms]
        ltots = [lt_ref[0, :, hh:hh + 1] for hh in range(4)]
        q2t = jnp.transpose(q2.astype(F32))
        do2t = jnp.transpose(do_ref[...])
        subl = lax.broadcasted_iota(jnp.int32, (256, 1), 0)
        qtstack = jnp.concatenate(
            [jnp.where((subl // 64) == hh, q2t, 0.0).astype(BF16) for hh in range(4)], axis=1)
        dotstack = jnp.concatenate(
            [jnp.where((subl // 64) == hh, do2t, 0.0).astype(BF16) for hh in range(4)], axis=1)

        def block(j, carry, valid):
            lpre, ppre, dq = list(carry[0:4]), list(carry[4:8]), carry[8]
            off = pl.multiple_of(j * tk, tk)
            kb = k_ref[pl.ds(off, tk), :]
            vb = v_ref[pl.ds(off, tk), :]
            dzs, avs = [], []
            for hh in range(4):
                zc, sp = _softplus_clamped(_dot_nt(qms[hh], kb))
                lsig = zc - sp
                lm = jnp.where(valid, sp, 0.0) if valid is not None else sp
                rowsum = jnp.sum(lm, axis=1, keepdims=True)
                between = _tri_sum(lm, later) + ((ltots[hh] - lpre[hh]) - rowsum)
                a = jnp.exp(lsig - between)
                if valid is not None:
                    a = jnp.where(valid, a, 0.0)
                p = a * _dot_nt(doms[hh], vb)
                pbefore = ppre[hh] + _tri_sum(p, earlier)
                dz = p - jnp.exp(lsig) * (p + pbefore)
                if valid is not None:
                    dz = jnp.where(valid, dz, 0.0)
                dzs.append(dz.astype(BF16))
                avs.append(a.astype(BF16))
                lpre[hh] = lpre[hh] + rowsum
                ppre[hh] = ppre[hh] + jnp.sum(p, axis=1, keepdims=True)
            kstack = jnp.concatenate([jnp.where(hm, kb, jnp.zeros_like(kb)) for hm in hms], axis=0)
            dq = dq + _dot(jnp.concatenate(dzs, axis=1), kstack)
            dk_ref[:, pl.ds(off, tk)] += _dot(qtstack, jnp.concatenate(dzs, axis=0))
            dv_ref[:, pl.ds(off, tk)] += _dot(dotstack, jnp.concatenate(avs, axis=0))
            return (*lpre, *ppre, dq)

        zero = jnp.zeros((tq, 1), F32)
        start = jnp.minimum(first_ref[g * nq_fwd + i // per_fwd], i * r)
        carry = lax.fori_loop(start, i * r, lambda j, cr: block(j, cr, None),
                              (zero,) * 8 + (jnp.zeros((tq, 256), F32),))
        for u in range(r):
            carry = block(i * r + u, carry, valids[u])
        dq_ref[...] = carry[8].astype(BF16)

    return pl.pallas_call(
        body, name="sb_bwd",
        out_shape=(_sds((s_len, SB_WIDTH), BF16), _sds((SB_WIDTH, s_len), F32), _sds((SB_WIDTH, s_len), F32)),
        grid_spec=pltpu.PrefetchScalarGridSpec(
            num_scalar_prefetch=1, grid=(2, nq),
            in_specs=[pl.BlockSpec((tq, 256), lambda g, i, f: (i, g)),
                      pl.BlockSpec((s_len, 256), lambda g, i, f: (0, g)),
                      pl.BlockSpec((s_len, 256), lambda g, i, f: (0, g)),
                      pl.BlockSpec((tq, 256), lambda g, i, f: (i, g)),
                      pl.BlockSpec((1, tq, 4), lambda g, i, f: (g, i, 0))],
            out_specs=(pl.BlockSpec((tq, 256), lambda g, i, f: (i, g)),
                       pl.BlockSpec((256, s_len), lambda g, i, f: (g, 0)),
                       pl.BlockSpec((256, s_len), lambda g, i, f: (g, 0)))),
        compiler_params=_params(("parallel", "arbitrary")),
    )(first, q, k, v, do, lt)


def _mla_fwd_call(qn, qp, kn, kpt, v):
    s_len = qn.shape[0]
    tk = min(MLA_KEY_TILE, s_len)
    tq = min(FWD_Q_TILES * ATT_TILE, s_len)
    r = tq // tk
    nq = s_len // tq

    def body(qn_ref, qp_ref, kn_ref, kpt_ref, v_ref, o_ref, lse_ref):
        i = pl.program_id(1)
        qn2 = qn_ref[...]
        qp2 = qp_ref[...]
        lane256 = lax.broadcasted_iota(jnp.int32, (1, 256), 1)
        lane128 = lax.broadcasted_iota(jnp.int32, (1, 128), 1)
        krow = lax.broadcasted_iota(jnp.int32, (tk, tk), 0)
        kcol = lax.broadcasted_iota(jnp.int32, (tk, tk), 1)
        row = lax.broadcasted_iota(jnp.int32, (tq, tk), 0)
        col = lax.broadcasted_iota(jnp.int32, (tq, tk), 1)
        valids = [col + u * tk <= row for u in range(r)]
        m64s = [(lane256 // 64) == hh for hh in range(4)]
        half = [(lane128 // 64) == u for u in range(2)]
        m32s = [(lane128 // 32) == hh for hh in range(4)]
        qcs = []
        for hh in range(4):
            qpair = qn2[:, 128 * (hh // 2):128 * (hh // 2) + 128]
            qcs.append(jnp.concatenate([jnp.where(half[hh % 2], qpair, jnp.zeros_like(qpair)),
                                        jnp.where(m32s[hh], qp2, jnp.zeros_like(qp2))], axis=1))

        def by_head(vals):
            return jnp.where(m64s[0], vals[0], jnp.where(m64s[1], vals[1], jnp.where(m64s[2], vals[2], vals[3])))

        def block(j, carry, valid):
            ms, ls, acc = list(carry[0:4]), list(carry[4:8]), carry[8]
            off = pl.multiple_of(j * tk, tk)
            knb = kn_ref[pl.ds(off, tk), :]
            kpb = kpt_ref[pl.ds(off, tk), :]
            vb = v_ref[pl.ds(off, tk), :]
            kcs = [jnp.concatenate([knb[:, 128 * pp:128 * pp + 128], kpb], axis=1) for pp in range(2)]
            ps, alphas = [], []
            for hh in range(4):
                s = _dot_nt(qcs[hh], kcs[hh // 2]) * MLA_SCALE
                if valid is not None:
                    s = jnp.where(valid, s, -1e30)
                mn = jnp.maximum(ms[hh], jnp.max(s, axis=1, keepdims=True))
                p = jnp.exp(s - mn)
                alpha = jnp.exp(ms[hh] - mn)
                ls[hh] = alpha * ls[hh] + jnp.sum(p, axis=1, keepdims=True)
                ms[hh] = mn
                ps.append(p.astype(BF16))
                alphas.append(alpha)
            pvs = []
            for pp in range(2):
                vpair = vb[:, 128 * pp:128 * pp + 128]
                vstack = jnp.concatenate([jnp.where(hf, vpair, jnp.zeros_like(vpair)) for hf in half], axis=0)
                pvs.append(_dot(jnp.concatenate(ps[2 * pp:2 * pp + 2], axis=1), vstack))
            acc = by_head(alphas) * acc + jnp.concatenate(pvs, axis=1)
            return (*ms, *ls, acc)

        neg = jnp.full((tq, 1), -1e30, F32)
        zero = jnp.zeros((tq, 1), F32)
        carry = lax.fori_loop(0, i * r, lambda j, cr: block(j, cr, None),
                              (neg,) * 4 + (zero,) * 4 + (jnp.zeros((tq, 256), F32),))
        for u in range(r):
            carry = block(i * r + u, carry, valids[u])
        o_ref[...] = carry[8] / by_head(list(carry[4:8]))
        for hh in range(4):
            lse_ref[0, :, hh:hh + 1] = carry[hh] + jnp.log(carry[4 + hh])

    return pl.pallas_call(
        body, name="mla_fwd", grid=(2, nq),
        out_shape=(_sds((s_len, MLA_WIDTH), F32), _sds((2, s_len, 4), F32)),
        in_specs=[pl.BlockSpec((tq, 256), lambda g, i: (i, g)),
                  pl.BlockSpec((tq, 128), lambda g, i: (i, g)),
                  pl.BlockSpec((s_len, 256), lambda g, i: (0, g)),
                  pl.BlockSpec((s_len, 128), lambda g, i: (0, 0)),
                  pl.BlockSpec((s_len, 256), lambda g, i: (0, g))],
        out_specs=(pl.BlockSpec((tq, 256), lambda g, i: (i, g)),
                   pl.BlockSpec((1, tq, 4), lambda g, i: (g, i, 0))),
        compiler_params=_params(("parallel", "parallel")),
    )(qn, qp, kn, kpt, v)


def _mla_bwd_call(qn, qp, kn, kpt, v, o, do, lse):
    s_len = qn.shape[0]
    tk = min(MLA_KEY_TILE, s_len)
    tq = min(ATT_Q_TILES * ATT_TILE, s_len)
    r = tq // tk
    nq = s_len // tq

    def body(qn_ref, qp_ref, kn_ref, kpt_ref, v_ref, o_ref, do_ref, lse_ref,
             dqn_ref, dqp_ref, dkn_ref, dkpt_ref, dv_ref):
        g = pl.program_id(0)
        i = pl.program_id(1)

        @pl.when(i == 0)
        def _():
            dkn_ref[...] = jnp.zeros_like(dkn_ref)
            dv_ref[...] = jnp.zeros_like(dv_ref)

        @pl.when((i == 0) & (g == 0))
        def _():
            dkpt_ref[...] = jnp.zeros_like(dkpt_ref)

        qn2 = qn_ref[...]
        qp2 = qp_ref[...]
        dof = do_ref[...]
        dob = dof.astype(BF16)
        prod = dof * o_ref[...]
        lane256 = lax.broadcasted_iota(jnp.int32, (1, 256), 1)
        lane128 = lax.broadcasted_iota(jnp.int32, (1, 128), 1)
        krow = lax.broadcasted_iota(jnp.int32, (tk, tk), 0)
        kcol = lax.broadcasted_iota(jnp.int32, (tk, tk), 1)
        row = lax.broadcasted_iota(jnp.int32, (tq, tk), 0)
        col = lax.broadcasted_iota(jnp.int32, (tq, tk), 1)
        valids = [col + u * tk <= row for u in range(r)]
        m64s = [(lane256 // 64) == hh for hh in range(4)]
        half = [(lane128 // 64) == u for u in range(2)]
        m32s = [(lane128 // 32) == hh for hh in range(4)]
        qcs, doms = [], []
        for hh in range(4):
            sl = slice(128 * (hh // 2), 128 * (hh // 2) + 128)
            qpair = qn2[:, sl]
            dpair = dob[:, sl]
            qcs.append(jnp.concatenate([jnp.where(half[hh % 2], qpair, jnp.zeros_like(qpair)),
                                        jnp.where(m32s[hh], qp2, jnp.zeros_like(qp2))], axis=1))
            doms.append(jnp.where(half[hh % 2], dpair, jnp.zeros_like(dpair)))
        dsums = [jnp.sum(jnp.where(m64, prod, 0.0), axis=1, keepdims=True) * MLA_SCALE for m64 in m64s]
        lses = [lse_ref[0, :, hh:hh + 1] for hh in range(4)]
        qn2t = jnp.transpose(qn2.astype(F32))
        qp2t = jnp.transpose(qp2.astype(F32))
        do2t = jnp.transpose(dof)
        sub128 = lax.broadcasted_iota(jnp.int32, (128, 1), 0)
        qtstacks, dotstacks = [], []
        for pp in range(2):
            qts, dts = [], []
            for u in range(2):
                hh = 2 * pp + u
                qts.append(jnp.concatenate(
                    [jnp.where((sub128 // 64) == u, qn2t[128 * pp:128 * pp + 128, :], 0.0),
                     jnp.where((sub128 // 32) == hh, qp2t, 0.0)], axis=0).astype(BF16))
                dts.append(jnp.where((sub128 // 64) == u, do2t[128 * pp:128 * pp + 128, :], 0.0).astype(BF16))
            qtstacks.append(jnp.concatenate(qts, axis=1))
            dotstacks.append(jnp.concatenate(dts, axis=1))

        def block(j, carry, valid):
            dqn, dqp = carry
            off = pl.multiple_of(j * tk, tk)
            knb = kn_ref[pl.ds(off, tk), :]
            kpb = kpt_ref[pl.ds(off, tk), :]
            vb = v_ref[pl.ds(off, tk), :]
            dqn_parts = []
            dkp = None
            for pp in range(2):
                sl = slice(128 * pp, 128 * pp + 128)
                knp = knb[:, sl]
                vpair = vb[:, sl]
                kc = jnp.concatenate([knp, kpb], axis=1)
                dss, pbs, kcms = [], [], []
                for u in range(2):
                    hh = 2 * pp + u
                    s = _dot_nt(qcs[hh], kc) * MLA_SCALE
                    if valid is not None:
                        s = jnp.where(valid, s, -1e30)
                    p = jnp.exp(s - lses[hh])
                    ds = p * (_dot_nt(doms[hh], vpair) * MLA_SCALE - dsums[hh])
                    dss.append(ds.astype(BF16))
                    pbs.append(p.astype(BF16))
                    kcms.append(jnp.concatenate([jnp.where(half[u], knp, jnp.zeros_like(knp)),
                                                 jnp.where(m32s[hh], kpb, jnp.zeros_like(kpb))], axis=1))
                dqc = _dot(jnp.concatenate(dss, axis=1), jnp.concatenate(kcms, axis=0))
                dqn_parts.append(dqc[:, :128])
                dqp = dqp + dqc[:, 128:]
                dkc = _dot(qtstacks[pp], jnp.concatenate(dss, axis=0))
                dkn_ref[128 * pp:128 * pp + 128, pl.ds(off, tk)] += dkc[:128, :]
                dkp = dkc[128:, :] if dkp is None else dkp + dkc[128:, :]
                dv_ref[128 * pp:128 * pp + 128, pl.ds(off, tk)] += _dot(dotstacks[pp], jnp.concatenate(pbs, axis=0))
            dqn = dqn + jnp.concatenate(dqn_parts, axis=1)
            dkpt_ref[:, pl.ds(off, tk)] += dkp
            return dqn, dqp

        carry = lax.fori_loop(0, i * r, lambda j, cr: block(j, cr, None),
                              (jnp.zeros((tq, 256), F32), jnp.zeros((tq, 128), F32)))
        for u in range(r):
            carry = block(i * r + u, carry, valids[u])
        dqn, dqp = carry
        dqn_ref[...] = dqn.astype(BF16)
        dqp_ref[...] = dqp.astype(BF16)

    return pl.pallas_call(
        body, name="mla_bwd", grid=(2, nq),
        out_shape=(_sds((s_len, 512), BF16), _sds((s_len, 256), BF16), _sds((512, s_len), F32),
                   _sds((128, s_len), F32), _sds((512, s_len), F32)),
        in_specs=[pl.BlockSpec((tq, 256), lambda g, i: (i, g)),
                  pl.BlockSpec((tq, 128), lambda g, i: (i, g)),
                  pl.BlockSpec((s_len, 256), lambda g, i: (0, g)),
                  pl.BlockSpec((s_len, 128), lambda g, i: (0, 0)),
                  pl.BlockSpec((s_len, 256), lambda g, i: (0, g)),
                  pl.BlockSpec((tq, 256), lambda g, i: (i, g)),
                  pl.BlockSpec((tq, 256), lambda g, i: (i, g)),
                  pl.BlockSpec((1, tq, 4), lambda g, i: (g, i, 0))],
        out_specs=(pl.BlockSpec((tq, 256), lambda g, i: (i, g)),
                   pl.BlockSpec((tq, 128), lambda g, i: (i, g)),
                   pl.BlockSpec((256, s_len), lambda g, i: (g, 0)),
                   pl.BlockSpec((128, s_len), lambda g, i: (0, 0)),
                   pl.BlockSpec((256, s_len), lambda g, i: (g, 0))),
        compiler_params=_params(("arbitrary", "arbitrary")),
    )(qn, qp, kn, kpt, v, o, do, lse)


def _post_call(x, tgt, oa, ob, sz, mz, ga, gb, gate, gf, wa, wb, wo, wat, wbt, wot):
    s_len = x.shape[0]
    tm = min(ROW_TILE, s_len)

    def body(x_ref, t_ref, oa_ref, ob_ref, sz_ref, mz_ref, ga_ref, gb_ref, gate_ref, gf_ref,
             wa_ref, wb_ref, wo_ref, wat_ref, wbt_ref, wot_ref,
             dx2_ref, doa_ref, dob_ref, dsz_ref, dmz_ref, dga_ref, dgb_ref,
             dwo_ref, dwa_ref, dwb_ref, dgf_ref, dgate_ref, loss_ref):
        @pl.when(pl.program_id(0) == 0)
        def _():
            dwo_ref[...] = jnp.zeros_like(dwo_ref)
            dwa_ref[...] = jnp.zeros_like(dwa_ref)
            dwb_ref[...] = jnp.zeros_like(dwb_ref)
            dgf_ref[...] = jnp.zeros_like(dgf_ref)
            dgate_ref[...] = jnp.zeros_like(dgate_ref)
            loss_ref[...] = jnp.zeros_like(loss_ref)

        gate = gate_ref[...]
        gf = gf_ref[...]
        oa = oa_ref[...]
        ob = ob_ref[...]
        sz = sz_ref[...]
        mz = mz_ref[...]
        sa = _sigmoid(sz)
        sb = _sigmoid(mz)
        silu_a = sz * sa
        silu_b = mz * sb
        ua = (oa * silu_a).astype(BF16)
        ub = (ob * silu_b).astype(BF16)
        ya = _dot(ua, wa_ref[...])
        yb = _dot(ub, wb_ref[...])
        sga = _sigmoid(ga_ref[...])
        sgb = _sigmoid(gb_ref[...])
        merged = (sga * ya + sgb * yb).astype(BF16)
        out = _dot(merged, wo_ref[...])
        x2 = x_ref[...] + gate * out
        r2 = lax.rsqrt(jnp.mean(x2 * x2, axis=-1, keepdims=True) + EPS)
        xhat = x2 * r2
        err = xhat * gf - t_ref[...]
        loss_ref[...] += 0.5 * jnp.sum(jnp.sum(err * err, axis=1, keepdims=True), axis=0, keepdims=True) / D_MODEL
        dy = err * (1.0 / D_MODEL)
        dgf_ref[...] += jnp.sum(dy * xhat, axis=0, keepdims=True)
        dxhat = dy * gf
        dx2 = r2 * (dxhat - xhat * jnp.mean(dxhat * xhat, axis=-1, keepdims=True))
        dx2_ref[...] = dx2
        dgate_ref[...] += jnp.sum(dx2 * out, axis=0, keepdims=True)
        dout = (dx2 * gate).astype(BF16)
        dmerged = _dot(dout, wot_ref[...])
        dwo_ref[...] += _dot_tn(merged, dout)
        dya = dmerged * sga
        dyb = dmerged * sgb
        dga_ref[...] = (dya * ya * (1.0 - sga)).astype(BF16)
        dgb_ref[...] = (dyb * yb * (1.0 - sgb)).astype(BF16)
        dyab = dya.astype(BF16)
        dybb = dyb.astype(BF16)
        dua = _dot(dyab, wat_ref[...])
        dub = _dot(dybb, wbt_ref[...])
        dwa_ref[...] += _dot_tn(ua, dyab)
        dwb_ref[...] += _dot_tn(ub, dybb)
        doa_ref[...] = dua * silu_a
        dob_ref[...] = dub * silu_b
        dsz_ref[...] = (dua * oa * (sa * (1.0 + sz * (1.0 - sa)))).astype(BF16)
        dmz_ref[...] = (dub * ob * (sb * (1.0 + mz * (1.0 - sb)))).astype(BF16)

    return pl.pallas_call(
        body, name="post", grid=(s_len // tm,),
        out_shape=(_sds((s_len, D_MODEL), F32), _sds((s_len, 512), F32), _sds((s_len, 512), F32),
                   _sds((s_len, 512), BF16), _sds((s_len, 512), BF16),
                   _sds((s_len, D_MODEL), BF16), _sds((s_len, D_MODEL), BF16),
                   _sds((D_MODEL, D_MODEL), F32), _sds((512, D_MODEL), F32), _sds((512, D_MODEL), F32),
                   _sds((1, D_MODEL), F32), _sds((1, D_MODEL), F32), _sds((1, 128), F32)),
        in_specs=[_rows(tm, D_MODEL), _rows(tm, D_MODEL), _rows(tm, 512), _rows(tm, 512), _rows(tm, 512),
                  _rows(tm, 512), _rows(tm, D_MODEL), _rows(tm, D_MODEL), _whole((1, D_MODEL)), _whole((1, D_MODEL)),
                  _whole((512, D_MODEL)), _whole((512, D_MODEL)), _whole((D_MODEL, D_MODEL)),
                  _whole((D_MODEL, 512)), _whole((D_MODEL, 512)), _whole((D_MODEL, D_MODEL))],
        out_specs=(_rows(tm, D_MODEL), _rows(tm, 512), _rows(tm, 512), _rows(tm, 512), _rows(tm, 512),
                   _rows(tm, D_MODEL), _rows(tm, D_MODEL),
                   _whole((D_MODEL, D_MODEL)), _whole((512, D_MODEL)), _whole((512, D_MODEL)),
                   _whole((1, D_MODEL)), _whole((1, D_MODEL)), _whole((1, 128))),
        compiler_params=_params(("arbitrary",)),
    )(x, tgt, oa, ob, sz, mz, ga, gb, gate, gf, wa, wb, wo, wat, wbt, wot)


def _bwdprep_call(dsq, dsk, dsv, dsz, dqn, dqp, dkn, dvv, dkpt, dmz, dga, dgb, cq, ckv, cos256, sin256,
                  qg, kvg, wqt, wkvt):
    s_len = cq.shape[0]
    tm = min(ROW_TILE, s_len)

    def body(dsq_ref, dsk_ref, dsv_ref, dsz_ref, dqn_ref, dqp_ref, dkn_ref, dvv_ref, dkpt_ref, dmz_ref,
             dga_ref, dgb_ref, cq_ref, ckv_ref, cos_ref, sin_ref, qg_ref, kvg_ref, wqt_ref, wkvt_ref,
             dp_ref, dwq_ref, dwkv_ref, dqg_ref, dkvg_ref):
        @pl.when(pl.program_id(0) == 0)
        def _():
            dwq_ref[...] = jnp.zeros_like(dwq_ref)
            dwkv_ref[...] = jnp.zeros_like(dwkv_ref)
            dqg_ref[...] = jnp.zeros_like(dqg_ref)
            dkvg_ref[...] = jnp.zeros_like(dkvg_ref)

        cos = cos_ref[...]
        sin = sin_ref[...]
        dp_ref[:, O_SQ:O_SK] = dsq_ref[...] * jnp.asarray(SB_SCALE, BF16)
        dp_ref[:, O_SK:O_SV] = jnp.transpose(dsk_ref[...]).astype(BF16)
        dp_ref[:, O_SV:O_SZ] = jnp.transpose(dsv_ref[...]).astype(BF16)
        dp_ref[:, O_SZ:O_CQ] = dsz_ref[...]
        dp_ref[:, O_MZ:O_GA] = dmz_ref[...]
        dp_ref[:, O_GA:O_GB] = dga_ref[...]
        dp_ref[:, O_GB:O_KR] = dgb_ref[...]
        dkp = jnp.transpose(dkpt_ref[...])
        dp_ref[:, O_KR:O_KR + 128] = (dkp * cos[:, :128]).astype(BF16)
        dp_ref[:, O_KR + 128:O_END] = (dkp * sin[:, :128]).astype(BF16)
        dp_ref[:, O_END:W_INT] = jnp.zeros((tm, W_INT - O_END), BF16)

        cq = cq_ref[...]
        rq = lax.rsqrt(jnp.mean(cq * cq, axis=-1, keepdims=True) + EPS)
        cqh = cq * rq
        qg = qg_ref[...]
        cqn = (cqh * qg).astype(BF16)
        dqp = dqp_ref[...].astype(F32)
        dqa = jnp.concatenate([dqn_ref[...], (dqp * cos).astype(BF16), (dqp * sin).astype(BF16)], axis=1)
        dcqn = _dot(dqa, wqt_ref[...])
        dwq_ref[...] += _dot_tn(cqn, dqa)
        dqg_ref[...] += jnp.sum(dcqn * cqh, axis=0, keepdims=True)
        dh = dcqn * qg
        dcq = rq * (dh - cqh * jnp.mean(dh * cqh, axis=-1, keepdims=True))
        dp_ref[:, O_CQ:O_CKV] = dcq.astype(BF16)

        ckv = ckv_ref[...]
        rk = lax.rsqrt(jnp.mean(ckv * ckv, axis=-1, keepdims=True) + EPS)
        ckh = ckv * rk
        kvg = kvg_ref[...]
        ckvn = (ckh * kvg).astype(BF16)
        dkva = jnp.concatenate([jnp.transpose(dkn_ref[...]).astype(BF16),
                                jnp.transpose(dvv_ref[...]).astype(BF16)], axis=1)
        dckvn = _dot(dkva, wkvt_ref[...])
        dwkv_ref[...] += _dot_tn(ckvn, dkva)
        dkvg_ref[...] += jnp.sum(dckvn * ckh, axis=0, keepdims=True)
        dh2 = dckvn * kvg
        dckv = rk * (dh2 - ckh * jnp.mean(dh2 * ckh, axis=-1, keepdims=True))
        dp_ref[:, O_CKV:O_MZ] = dckv.astype(BF16)

    return pl.pallas_call(
        body, name="bwdprep", grid=(s_len // tm,),
        out_shape=(_sds((s_len, W_INT), BF16), _sds((Q_RANK, 1024), F32), _sds((KV_RANK, 1024), F32),
                   _sds((1, Q_RANK), F32), _sds((1, KV_RANK), F32)),
        in_specs=[_rows(tm, 512), _cols(512, tm), _cols(512, tm), _rows(tm, 512), _rows(tm, 512), _rows(tm, 256),
                  _cols(512, tm), _cols(512, tm), _cols(128, tm), _rows(tm, 512), _rows(tm, D_MODEL),
                  _rows(tm, D_MODEL), _rows(tm, Q_RANK), _rows(tm, KV_RANK), _rows(tm, 256), _rows(tm, 256),
                  _whole((1, Q_RANK)), _whole((1, KV_RANK)), _whole((1024, Q_RANK)), _whole((1024, KV_RANK))],
        out_specs=(_rows(tm, W_INT), _whole((Q_RANK, 1024)), _whole((KV_RANK, 1024)),
                   _whole((1, Q_RANK)), _whole((1, KV_RANK))),
        compiler_params=_params(("arbitrary",)),
    )(dsq, dsk, dsv, dsz, dqn, dqp, dkn, dvv, dkpt, dmz, dga, dgb, cq, ckv, cos256, sin256, qg, kvg, wqt, wkvt)


def _dh_call(dproj, w_int_t, x, dx2, scale, g1):
    s_len = x.shape[0]
    tm = min(2 * ROW_TILE, s_len)

    def body(dp_ref, wt_ref, x_ref, dx2_ref, sc_ref, g1_ref, gx_ref, dsh_ref, dsc_ref, dg1_ref):
        @pl.when(pl.program_id(0) == 0)
        def _():
            dsh_ref[...] = jnp.zeros_like(dsh_ref)
            dsc_ref[...] = jnp.zeros_like(dsc_ref)
            dg1_ref[...] = jnp.zeros_like(dg1_ref)

        dh = _dot(dp_ref[...], wt_ref[...])
        xt = x_ref[...]
        r = lax.rsqrt(jnp.mean(xt * xt, axis=-1, keepdims=True) + EPS)
        xh = xt * r
        g1 = g1_ref[...]
        xg = xh * g1
        dsh_ref[...] += jnp.sum(dh, axis=0, keepdims=True)
        dsc_ref[...] += jnp.sum(dh * xg, axis=0, keepdims=True)
        dxg = dh * (1.0 + sc_ref[...])
        dg1_ref[...] += jnp.sum(dxg * xh, axis=0, keepdims=True)
        dxh = dxg * g1
        gx_ref[...] = dx2_ref[...] + r * (dxh - xh * jnp.mean(dxh * xh, axis=-1, keepdims=True))

    return pl.pallas_call(
        body, name="dh", grid=(s_len // tm,),
        out_shape=(_sds((s_len, D_MODEL), F32), _sds((1, D_MODEL), F32), _sds((1, D_MODEL), F32),
                   _sds((1, D_MODEL), F32)),
        in_specs=[_rows(tm, W_INT), _whole((W_INT, D_MODEL)), _rows(tm, D_MODEL), _rows(tm, D_MODEL),
                  _whole((1, D_MODEL)), _whole((1, D_MODEL))],
        out_specs=(_rows(tm, D_MODEL), _whole((1, D_MODEL)), _whole((1, D_MODEL)), _whole((1, D_MODEL))),
        compiler_params=_params(("arbitrary",)),
    )(dproj, w_int_t, x, dx2, scale, g1)


def _small_call(svg, ct, dmod_sh):
    def body(sv_ref, ct_ref, dm_ref, tot_ref, gwada_ref):
        acc = sv_ref[0:1, :]
        for d in range(1, N_DEV):
            acc = acc + sv_ref[d:d + 1, :]
        tot_ref[...] = acc
        gwada_ref[...] = lax.dot_general(ct_ref[...], dm_ref[...], (((1,), (0,)), ((), ())),
                                         precision=lax.Precision.HIGHEST, preferred_element_type=F32)

    vmem = pl.BlockSpec(memory_space=pltpu.VMEM)
    return pl.pallas_call(
        body, name="small_grads",
        out_shape=(_sds((1, 8 * SV_COLS), F32), _sds((D_MODEL, 768), F32)),
        in_specs=[vmem, vmem, vmem], out_specs=(vmem, vmem),
        compiler_params=_params(),
    )(svg, ct, dmod_sh)


def _adamw_tile_rows(rows, cols):
    budget = 2 << 20
    if rows * cols * 4 <= budget or rows % 8:
        return rows
    best = 8
    for tr in range(8, rows + 1, 8):
        if rows % tr == 0 and tr * cols * 4 <= budget:
            best = tr
    return best


def _adamw_call(name, w, g, m, v):
    rows, cols = w.shape
    tr = _adamw_tile_rows(rows, cols)

    def body(w_ref, g_ref, m_ref, v_ref, d_ref, nm_ref, nv_ref):
        gg = g_ref[...]
        m2 = ADAM_B1 * m_ref[...] + (1.0 - ADAM_B1) * gg
        v2 = ADAM_B2 * v_ref[...] + (1.0 - ADAM_B2) * (gg * gg)
        m_hat = m2 / (1.0 - ADAM_B1 ** ADAM_STEP)
        v_hat = v2 / (1.0 - ADAM_B2 ** ADAM_STEP)
        d_ref[...] = -ADAM_LR * (m_hat / (jnp.sqrt(v_hat) + ADAM_EPS) + ADAM_WD * w_ref[...])
        nm_ref[...] = m2
        nv_ref[...] = v2

    spec = pl.BlockSpec((tr, cols), lambda i: (i, 0))
    return pl.pallas_call(
        body, name="adamw_" + name, grid=(rows // tr,),
        out_shape=(_sds((rows, cols), F32),) * 3,
        in_specs=[spec] * 4, out_specs=(spec,) * 3,
        compiler_params=_params(("parallel",)),
    )(w, g, m, v)


IN_SHARD = IN_WIDTH // N_CHIPS
HALF_D = D_MODEL // 2
SMALL_ROWS = (576, 512, 1024, 1024, 2048)
SMALL_TOTAL = sum(SMALL_ROWS)
SMALL_HALF = SMALL_TOTAL // 2
SMALL_SUM_ROWS = 432


def _gather_call(c_row, w_ada_sh, pack_in, pack_small):
    def body(c_ref, wada_ref, pki_ref, pks_ref, mg_ref, cg_ref, gwi_ref, gws_ref,
             cv, ssem_c, rsem_c, ssem_m, rsem_m, ssem_w, rsem_w, ssem_f, rsem_f, lsem):
        x, y, c = lax.axis_index("x"), lax.axis_index("y"), lax.axis_index("c")
        me = 4 * x + 2 * y + c
        chip = 2 * x + y
        rel3 = [(1, 0), (0, 1), (1, 1)]
        packs = [(pki_ref, gwi_ref), (pks_ref, gws_ref)]

        sends = []
        for j, (dx, dy) in enumerate(rel3):
            for a, (pk, gw) in enumerate(packs):
                cp = pltpu.make_async_remote_copy(
                    src_ref=pk.at[c], dst_ref=gw.at[chip, c], send_sem=ssem_w.at[j, a], recv_sem=rsem_w.at[j, a],
                    device_id=(_flip(x, dx), _flip(y, dy), c), device_id_type=MESH)
                cp.start()
                sends.append(cp)
        owns = []
        for a, (pk, gw) in enumerate(packs):
            own = pltpu.make_async_copy(pk, gw.at[chip], lsem.at[a])
            own.start()
            owns.append(own)

        cv[me] = c_ref[...]
        for r in range(1, N_DEV):
            dx, dy, dc = (r >> 2) & 1, (r >> 1) & 1, r & 1
            cp = pltpu.make_async_remote_copy(
                src_ref=c_ref, dst_ref=cv.at[me], send_sem=ssem_c.at[r - 1], recv_sem=rsem_c.at[r - 1],
                device_id=(_flip(x, dx), _flip(y, dy), _flip(c, dc)), device_id_type=MESH)
            cp.start()
            sends.append(cp)
        for r in range(1, N_DEV):
            dx, dy, dc = (r >> 2) & 1, (r >> 1) & 1, r & 1
            src = 4 * _flip(x, dx) + 2 * _flip(y, dy) + _flip(c, dc)
            pltpu.make_async_remote_copy(
                src_ref=c_ref, dst_ref=cv.at[src], send_sem=ssem_c.at[r - 1], recv_sem=rsem_c.at[r - 1],
                device_id=(x, y, c), device_id_type=MESH).wait_recv()
        rows = lax.broadcasted_iota(jnp.int32, (N_DEV, D_MODEL), 0)
        call = jnp.zeros((N_DEV, D_MODEL), F32)
        for b in range(N_DEV):
            call = jnp.where(rows == b, jnp.broadcast_to(cv[b], (N_DEV, D_MODEL)), call)
        cg_ref[...] = call

        mg_ref[chip] = lax.dot_general(call, wada_ref[...], (((1,), (0,)), ((), ())),
                                       precision=lax.Precision.HIGHEST, preferred_element_type=F32)
        for j, (dx, dy) in enumerate(rel3):
            cp = pltpu.make_async_remote_copy(
                src_ref=mg_ref.at[chip], dst_ref=mg_ref.at[chip], send_sem=ssem_m.at[j], recv_sem=rsem_m.at[j],
                device_id=(_flip(x, dx), _flip(y, dy), c), device_id_type=MESH)
            cp.start()
            sends.append(cp)
        for j, (dx, dy) in enumerate(rel3):
            src_chip = 2 * _flip(x, dx) + _flip(y, dy)
            pltpu.make_async_remote_copy(
                src_ref=mg_ref.at[src_chip], dst_ref=mg_ref.at[src_chip], send_sem=ssem_m.at[j],
                recv_sem=rsem_m.at[j], device_id=(x, y, c), device_id_type=MESH).wait_recv()
        for j, (dx, dy) in enumerate(rel3):
            src_chip = 2 * _flip(x, dx) + _flip(y, dy)
            for a, (pk, gw) in enumerate(packs):
                pltpu.make_async_remote_copy(
                    src_ref=pk.at[c], dst_ref=gw.at[src_chip, c], send_sem=ssem_w.at[j, a],
                    recv_sem=rsem_w.at[j, a], device_id=(x, y, c), device_id_type=MESH).wait_recv()
                cp = pltpu.make_async_remote_copy(
                    src_ref=gw.at[src_chip, c], dst_ref=gw.at[src_chip, c], send_sem=ssem_f.at[j, a],
                    recv_sem=rsem_f.at[j, a], device_id=(x, y, 1 - c), device_id_type=MESH)
                cp.start()
                sends.append(cp)
        for j, (dx, dy) in enumerate(rel3):
            src_chip = 2 * _flip(x, dx) + _flip(y, dy)
            for a, (pk, gw) in enumerate(packs):
                pltpu.make_async_remote_copy(
                    src_ref=pk.at[c], dst_ref=gw.at[src_chip, 1 - c], send_sem=ssem_f.at[j, a],
                    recv_sem=rsem_f.at[j, a], device_id=(x, y, c), device_id_type=MESH).wait_recv()
        for cp in sends:
            cp.wait_send()
        for own in owns:
            own.wait()

    vmem = pl.BlockSpec(memory_space=pltpu.VMEM)
    return pl.pallas_call(
        body, name="gather_fwd",
        out_shape=(_sds((N_CHIPS, N_DEV, 768), F32), _sds((N_DEV, D_MODEL), F32),
                   _sds((N_CHIPS, 2, IN_SHARD, HALF_D), BF16), _sds((N_CHIPS, 2, SMALL_HALF, LANES), BF16)),
        in_specs=[vmem, vmem, vmem, vmem], out_specs=(vmem, vmem, vmem, vmem),
        scratch_shapes=[
            pltpu.VMEM((N_DEV, 1, D_MODEL), F32),
            pltpu.SemaphoreType.DMA((N_DEV - 1,)), pltpu.SemaphoreType.DMA((N_DEV - 1,)),
            pltpu.SemaphoreType.DMA((3,)), pltpu.SemaphoreType.DMA((3,)),
            pltpu.SemaphoreType.DMA((3, 2)), pltpu.SemaphoreType.DMA((3, 2)),
            pltpu.SemaphoreType.DMA((3, 2)), pltpu.SemaphoreType.DMA((3, 2)),
            pltpu.SemaphoreType.DMA((2,)),
        ],
        compiler_params=_params(),
    )(c_row, w_ada_sh, pack_in, pack_small)


def _reduce_call(g_in, g_small, sv):
    def body(gi_ref, gs_ref, sv_ref, fi_ref, fs_ref, svg_ref, pair_i, pair_s, send_i, send_s, land_i, land_s,
             ssem_p, rsem_p, ssem_g, rsem_g, ssem_s, rsem_s, ssem_x, rsem_x):
        x, y, c = lax.axis_index("x"), lax.axis_index("y"), lax.axis_index("c")
        me = 4 * x + 2 * y + c
        chip = 2 * x + y
        rel3 = [(1, 0), (0, 1), (1, 1)]
        payloads = [(gi_ref, pair_i, send_i, land_i, fi_ref), (gs_ref, pair_s, send_s, land_s, fs_ref)]
        copies = []

        for k in range(N_CHIPS):
            for a, (g, pair, _, _, _) in enumerate(payloads):
                cp = pltpu.make_async_remote_copy(
                    src_ref=g.at[2 * k + 1 - c], dst_ref=pair.at[k], send_sem=ssem_p.at[k, a],
                    recv_sem=rsem_p.at[k, a], device_id=(x, y, 1 - c), device_id_type=MESH)
                cp.start()
                copies.append(cp)

        for r in range(1, N_DEV):
            dx, dy, dc = (r >> 2) & 1, (r >> 1) & 1, r & 1
            cp = pltpu.make_async_remote_copy(
                src_ref=sv_ref, dst_ref=svg_ref.at[me], send_sem=ssem_s.at[r - 1], recv_sem=rsem_s.at[r - 1],
                device_id=(_flip(x, dx), _flip(y, dy), _flip(c, dc)), device_id_type=MESH)
            cp.start()
            copies.append(cp)
        svg_ref[me] = sv_ref[...]

        def pair_sum(k, store_in, store_small):
            for a, (g, pair, _, _, _) in enumerate(payloads):
                pltpu.make_async_remote_copy(
                    src_ref=g.at[2 * k + c], dst_ref=pair.at[k], send_sem=ssem_p.at[k, a],
                    recv_sem=rsem_p.at[k, a], device_id=(x, y, c), device_id_type=MESH).wait_recv()
            for qd in range(HALF_D // LANES):
                sl = slice(LANES * qd, LANES * qd + LANES)
                store_in(sl, gi_ref[2 * k + c, :, sl].astype(F32) + pair_i[k, :, sl].astype(F32))

            def rows(i, carry):
                sl = pl.ds(pl.multiple_of(i * SMALL_SUM_ROWS, 16), SMALL_SUM_ROWS)
                store_small(sl, gs_ref[2 * k + c, sl, :].astype(F32) + pair_s[k, sl, :].astype(F32))
                return carry

            lax.fori_loop(0, SMALL_HALF // SMALL_SUM_ROWS, rows, 0)

        for j, (dx, dy) in enumerate(rel3):
            tx, ty = _flip(x, dx), _flip(y, dy)

            def put_in(sl, val, j=j):
                send_i[j, :, sl] = val.astype(BF16)

            def put_small(sl, val, j=j):
                send_s[j, sl, :] = val.astype(BF16)

            pair_sum(2 * tx + ty, put_in, put_small)
            for a, (_, _, send, land, _) in enumerate(payloads):
                cp = pltpu.make_async_remote_copy(
                    src_ref=send.at[j], dst_ref=land.at[j], send_sem=ssem_g.at[j, a], recv_sem=rsem_g.at[j, a],
                    device_id=(tx, ty, c), device_id_type=MESH)
                cp.start()
                copies.append(cp)

        def own_in(sl, val):
            fi_ref[c, :, sl] = val

        def own_small(sl, val):
            fs_ref[c, sl, :] = val

        pair_sum(chip, own_in, own_small)
        for j in range(3):
            for a, (_, _, send, land, _) in enumerate(payloads):
                pltpu.make_async_remote_copy(
                    src_ref=send.at[j], dst_ref=land.at[j], send_sem=ssem_g.at[j, a], recv_sem=rsem_g.at[j, a],
                    device_id=(x, y, c), device_id_type=MESH).wait_recv()
            for qd in range(HALF_D // LANES):
                sl = slice(LANES * qd, LANES * qd + LANES)
                fi_ref[c, :, sl] += land_i[j, :, sl].astype(F32)

            def add_rows(i, carry, j=j):
                sl = pl.ds(pl.multiple_of(i * SMALL_SUM_ROWS, 16), SMALL_SUM_ROWS)
                fs_ref[c, sl, :] += land_s[j, sl, :].astype(F32)
                return carry

            lax.fori_loop(0, SMALL_HALF // SMALL_SUM_ROWS, add_rows, 0)

        for a, f in enumerate((fi_ref, fs_ref)):
            cp = pltpu.make_async_remote_copy(
                src_ref=f.at[c], dst_ref=f.at[c], send_sem=ssem_x.at[a], recv_sem=rsem_x.at[a],
                device_id=(x, y, 1 - c), device_id_type=MESH)
            cp.start()
            copies.append(cp)
        for a, f in enumerate((fi_ref, fs_ref)):
            pltpu.make_async_remote_copy(
                src_ref=f.at[c], dst_ref=f.at[1 - c], send_sem=ssem_x.at[a], recv_sem=rsem_x.at[a],
                device_id=(x, y, c), device_id_type=MESH).wait_recv()
        for r in range(1, N_DEV):
            dx, dy, dc = (r >> 2) & 1, (r >> 1) & 1, r & 1
            src = 4 * _flip(x, dx) + 2 * _flip(y, dy) + _flip(c, dc)
            pltpu.make_async_remote_copy(
                src_ref=sv_ref, dst_ref=svg_ref.at[src], send_sem=ssem_s.at[r - 1],
                recv_sem=rsem_s.at[r - 1], device_id=(x, y, c), device_id_type=MESH).wait_recv()
        for cp in copies:
            cp.wait_send()

    vmem = pl.BlockSpec(memory_space=pltpu.VMEM)
    return pl.pallas_call(
        body, name="grad_reduce",
        out_shape=(_sds((2, IN_SHARD, HALF_D), F32), _sds((2, SMALL_HALF, LANES), F32),
                   _sds((N_DEV, 8, SV_COLS), F32)),
        in_specs=[vmem, vmem, vmem], out_specs=(vmem, vmem, vmem),
        scratch_shapes=[
            pltpu.VMEM((N_CHIPS, IN_SHARD, HALF_D), BF16), pltpu.VMEM((N_CHIPS, SMALL_HALF, LANES), BF16),
            pltpu.VMEM((3, IN_SHARD, HALF_D), BF16), pltpu.VMEM((3, SMALL_HALF, LANES), BF16),
            pltpu.VMEM((3, IN_SHARD, HALF_D), BF16), pltpu.VMEM((3, SMALL_HALF, LANES), BF16),
            pltpu.SemaphoreType.DMA((N_CHIPS, 2)), pltpu.SemaphoreType.DMA((N_CHIPS, 2)),
            pltpu.SemaphoreType.DMA((3, 2)), pltpu.SemaphoreType.DMA((3, 2)),
            pltpu.SemaphoreType.DMA((N_DEV - 1,)), pltpu.SemaphoreType.DMA((N_DEV - 1,)),
            pltpu.SemaphoreType.DMA((2,)), pltpu.SemaphoreType.DMA((2,)),
        ],
        compiler_params=_params(),
    )(g_in, g_small, sv)


def _dwin_call(h, dproj):
    s_len = h.shape[0]
    tm = min(4 * ROW_TILE, s_len)
    nrow = s_len // tm
    nc = 4
    chunk = W_INT // nc

    def body(h_ref, dp_ref, dw_ref, acc):
        i = pl.program_id(1)

        @pl.when(i == 0)
        def _():
            acc[...] = jnp.zeros_like(acc)

        acc[...] += _dot_tn(dp_ref[...], h_ref[...])

        @pl.when(i == nrow - 1)
        def _():
            dw_ref[...] = acc[...].astype(BF16)

    return pl.pallas_call(
        body, name="dwin", grid=(nc, nrow),
        out_shape=_sds((W_INT, D_MODEL), BF16),
        in_specs=[pl.BlockSpec((tm, D_MODEL), lambda c, i: (i, 0)),
                  pl.BlockSpec((tm, chunk), lambda c, i: (i, c))],
        out_specs=pl.BlockSpec((chunk, D_MODEL), lambda c, i: (c, 0)),
        scratch_shapes=[pltpu.VMEM((chunk, D_MODEL), F32)],
        compiler_params=_params(("parallel", "arbitrary")),
    )(h, dproj)


def _swap_rows(w, group):
    r, n = w.shape
    return w.reshape(r // group, 2, group // 2, n)[:, ::-1].reshape(r, n)


def _internal_weights(w_in_t, w_uq, w_ukv):
    krot_t = w_in_t[2688:2720]
    w_int_t = jnp.concatenate([
        w_in_t[0:2688], w_in_t[2720:5280],
        jnp.tile(krot_t, (4, 1)), jnp.tile(_swap_rows(krot_t, 32), (4, 1)),
        jnp.zeros((W_INT - O_END, D_MODEL), w_in_t.dtype)], axis=0)
    uq = w_uq.reshape(Q_RANK, N_HEADS, 96)
    wp = uq[:, :, 64:].reshape(Q_RANK, 256)
    w_q = jnp.concatenate([uq[:, :, :64].reshape(Q_RANK, 512), wp, _swap_halves(wp, 32)], axis=1)
    ukv = w_ukv.reshape(KV_RANK, N_HEADS, 128)
    w_kv = jnp.concatenate([ukv[:, :, :64].reshape(KV_RANK, 512), ukv[:, :, 64:].reshape(KV_RANK, 512)], axis=1)
    return w_int_t, w_q, w_kv


def _true_weight_grads(dwi_t, dwq, dwkv):
    dkr = dwi_t[O_KR:O_KR + 128].astype(F32).reshape(4, 32, D_MODEL).sum(axis=0)
    dkr_sw = dwi_t[O_KR + 128:O_END].astype(F32).reshape(4, 32, D_MODEL).sum(axis=0)
    dkrot_t = (dkr + _swap_rows(dkr_sw, 32)).astype(dwi_t.dtype)
    g_in_t = jnp.concatenate([dwi_t[0:O_MZ], dkrot_t, dwi_t[O_MZ:O_KR]], axis=0)
    dwp = dwq[:, 512:768] + _swap_halves(dwq[:, 768:1024], 32)
    g_uq = jnp.concatenate([dwq[:, :512].reshape(Q_RANK, N_HEADS, 64), dwp.reshape(Q_RANK, N_HEADS, 32)],
                           axis=2).reshape(Q_RANK, 768)
    g_ukv = jnp.concatenate([dwkv[:, :512].reshape(KV_RANK, N_HEADS, 64), dwkv[:, 512:].reshape(KV_RANK, N_HEADS, 64)],
                            axis=2).reshape(KV_RANK, 1024)
    return g_in_t, g_uq, g_ukv


def _swap_halves(w, group):
    r, n = w.shape
    return w.reshape(r, n // group, 2, group // 2)[:, :, ::-1, :].reshape(r, n)


def _pack_shards(parts):
    return jnp.concatenate([p.reshape(-1, LANES) for p in parts], axis=0)


def _unpack_small(gw):
    offs = [0]
    for r in SMALL_ROWS:
        offs.append(offs[-1] + r)

    def cols(i, rows, shard_cols):
        blk = gw[:, offs[i]:offs[i + 1]].reshape(N_CHIPS, rows, shard_cols)
        return blk.transpose(1, 0, 2).reshape(rows, N_CHIPS * shard_cols)

    return (cols(0, Q_RANK, 192), cols(1, KV_RANK, 256), cols(2, 512, 256), cols(3, 512, 256),
            gw[:, offs[4]:offs[5]].reshape(D_MODEL, D_MODEL))


def _chip_major(g, shard_cols):
    r = g.shape[0]
    return g.reshape(r, N_CHIPS, shard_cols).transpose(1, 0, 2).reshape(N_CHIPS, -1, LANES)


def kernel(x, c, positions, w_ada, b_ada, norm_gain, w_in, q_norm_gain, w_uq, kv_norm_gain, w_ukv, w_branch_a, w_branch_b, w_out, final_norm_gain, loss_target, m_w_ada, m_b_ada, m_norm_gain, m_w_in, m_q_norm_gain, m_w_uq, m_kv_norm_gain, m_w_ukv, m_w_branch_a, m_w_branch_b, m_w_out, m_final_norm_gain, v_w_ada, v_b_ada, v_norm_gain, v_w_in, v_q_norm_gain, v_w_uq, v_kv_norm_gain, v_w_ukv, v_w_branch_a, v_w_branch_b, v_w_out, v_final_norm_gain):
    ix, iy, ic = lax.axis_index("x"), lax.axis_index("y"), lax.axis_index("c")
    me = 4 * ix + 2 * iy + ic
    chip = 2 * ix + iy
    xs = x[0]
    tgt = loss_target[0]
    s_len = xs.shape[0]

    w_in_t = jnp.swapaxes(w_in[0], 0, 1)
    w_in_tb = w_in_t.astype(BF16)
    pack_in = jnp.stack([w_in_tb[:, :HALF_D], w_in_tb[:, HALF_D:]], axis=0)
    small_shards = (w_uq[0], w_ukv[0], w_branch_a[0], w_branch_b[0], w_out[0])
    pack_small = _pack_shards([s.astype(BF16) for s in small_shards]).reshape(2, SMALL_HALF, LANES)
    mg, call, gw_in, gw_small = _gather_call(c, w_ada[0], pack_in, pack_small)
    mod = mg.transpose(1, 0, 2).reshape(N_DEV, 3 * D_MODEL) + b_ada
    mod_me = lax.dynamic_slice_in_dim(mod, me, 1, axis=0)
    shift, scale, gate = mod_me[:, :D_MODEL], mod_me[:, D_MODEL:2 * D_MODEL], mod_me[:, 2 * D_MODEL:]

    f_in_t = jnp.concatenate([gw_in[:, 0], gw_in[:, 1]], axis=2).reshape(IN_WIDTH, D_MODEL)
    f_uq, f_ukv, f_a, f_b, f_out = _unpack_small(gw_small.reshape(N_CHIPS, SMALL_TOTAL, LANES))
    w_int_t, w_q, w_kv = _internal_weights(f_in_t, f_uq, f_ukv)

    inv_freq = ROPE_BASE ** (-jnp.arange(0, ROPE_DIM, 2, dtype=F32) / ROPE_DIM)
    ang = positions[0].astype(F32)[:, None] * inv_freq
    cs, sn = jnp.cos(ang), jnp.sin(ang)
    cos256 = jnp.tile(jnp.concatenate([cs, cs], axis=1), (1, 8))
    sin256 = jnp.tile(jnp.concatenate([-sn, sn], axis=1), (1, 8))

    (h, sq, sk, sv, sz, cq, ckv, mz, ga, gb, kpt, qn, qp, kn, vv) = _inproj_call(
        xs, shift, scale, norm_gain, w_int_t, w_q, w_kv, q_norm_gain, kv_norm_gain, cos256, sin256)
    oa, lt, first = _sb_fwd_call(sq, sk, sv)
    ob, lse = _mla_fwd_call(qn, qp, kn, kpt, vv)

    gf = final_norm_gain.reshape(1, D_MODEL)
    (dx2, doa, dob, dsz, dmz, dga, dgb, dwo, dwa, dwb, dgf, dgate, loss_p) = _post_call(
        xs, tgt, oa, ob, sz, mz, ga, gb, gate, gf, f_a, f_b, f_out, f_a.T, f_b.T, f_out.T)

    dsq, dsk_t, dsv_t = _sb_bwd_call(first[:, :, 0, 0].reshape(-1), sq, sk, sv, doa, lt)
    dqn, dqp, dkn_t, dkpt_t, dvv_t = _mla_bwd_call(qn, qp, kn, kpt, vv, ob, dob, lse)

    dproj, dwq, dwkv, dqg, dkvg = _bwdprep_call(
        dsq, dsk_t, dsv_t, dsz, dqn, dqp, dkn_t, dvv_t, dkpt_t, dmz, dga, dgb, cq, ckv, cos256, sin256,
        q_norm_gain, kv_norm_gain, w_q.T, w_kv.T)
    grad_x, dshift, dscale, dg1 = _dh_call(dproj, w_int_t, xs, dx2, scale, norm_gain)
    dwi_t = _dwin_call(h, dproj)
    g_in_t, g_uq, g_ukv = _true_weight_grads(dwi_t, dwq, dwkv)

    g_in_c = g_in_t.reshape(N_CHIPS, IN_SHARD, D_MODEL)
    g_in_pieces = jnp.stack([g_in_c[:, :, :HALF_D], g_in_c[:, :, HALF_D:]], axis=1).reshape(N_DEV, IN_SHARD, HALF_D)
    g_small = jnp.concatenate([
        _chip_major(g_uq, 192), _chip_major(g_ukv, 256), _chip_major(dwa, 256), _chip_major(dwb, 256),
        dwo.reshape(N_CHIPS, -1, LANES)], axis=1).astype(BF16).reshape(N_DEV, SMALL_HALF, LANES)
    small = jnp.concatenate([
        dshift, dscale, dgate, dg1, dqg, dkvg, dgf, loss_p,
        jnp.zeros((1, 8 * SV_COLS - 5888), F32)], axis=1).reshape(8, SV_COLS)
    full_in, full_small, svg = _reduce_call(g_in_pieces, g_small, small)
    gs_in_t = jnp.concatenate([full_in[0], full_in[1]], axis=1)
    full = full_small.reshape(SMALL_TOTAL, LANES)
    offs = [0]
    for r in SMALL_ROWS:
        offs.append(offs[-1] + r)
    gs_uq = full[offs[0]:offs[1]].reshape(Q_RANK, 192)
    gs_ukv = full[offs[1]:offs[2]].reshape(KV_RANK, 256)
    gs_a = full[offs[2]:offs[3]].reshape(512, 256)
    gs_b = full[offs[3]:offs[4]].reshape(512, 256)
    gs_out = full[offs[4]:offs[5]].reshape(256, D_MODEL)

    svm = svg.reshape(N_DEV, 8 * SV_COLS)
    dmod_sh = lax.dynamic_slice_in_dim(svm[:, :3 * D_MODEL], chip * 768, 768, axis=1)
    tot, gs_ada = _small_call(svm, call.T, dmod_sh)
    g_bada = tot[:, 0:3072]
    g_g1 = tot[:, 3072:4096]
    g_qg = tot[:, 4096:4480]
    g_kvg = tot[:, 4480:4736]
    g_gf = tot[:, 4736:5760]
    loss = tot[0, 5760]

    names = ["w_ada", "b_ada", "norm_gain", "w_in", "q_norm_gain", "w_uq", "kv_norm_gain", "w_ukv",
             "w_branch_a", "w_branch_b", "w_out", "final_norm_gain"]
    ws = [w_ada[0], b_ada, norm_gain, w_in_t, q_norm_gain, w_uq[0], kv_norm_gain, w_ukv[0],
          w_branch_a[0], w_branch_b[0], w_out[0], final_norm_gain.reshape(1, D_MODEL)]
    gs = [gs_ada, g_bada, g_g1, gs_in_t, g_qg, gs_uq, g_kvg, gs_ukv, gs_a, gs_b, gs_out, g_gf]
    ms = [m_w_ada[0], m_b_ada, m_norm_gain, jnp.swapaxes(m_w_in[0], 0, 1), m_q_norm_gain, m_w_uq[0],
          m_kv_norm_gain, m_w_ukv[0], m_w_branch_a[0], m_w_branch_b[0], m_w_out[0],
          m_final_norm_gain.reshape(1, D_MODEL)]
    vs = [v_w_ada[0], v_b_ada, v_norm_gain, jnp.swapaxes(v_w_in[0], 0, 1), v_q_norm_gain, v_w_uq[0],
          v_kv_norm_gain, v_w_ukv[0], v_w_branch_a[0], v_w_branch_b[0], v_w_out[0],
          v_final_norm_gain.reshape(1, D_MODEL)]
    refs = [w_ada, b_ada, norm_gain, w_in, q_norm_gain, w_uq, kv_norm_gain, w_ukv,
            w_branch_a, w_branch_b, w_out, final_norm_gain]
    grads, deltas, new_ms, new_vs = [], [], [], []
    for n, w_, g_, m_, v_, ref in zip(names, ws, gs, ms, vs, refs):
        outs = (g_,) + _adamw_call(n, w_, g_, m_, v_)
        if n == "w_in":
            outs = tuple(jnp.swapaxes(o_, 0, 1) for o_ in outs)
        for lst, o_ in zip((grads, deltas, new_ms, new_vs), outs):
            lst.append(o_.reshape(ref.shape))

    return (loss, grad_x.reshape(x.shape), *grads, *deltas, *new_ms, *new_vs)
```

```python
import math

import jax
import jax.numpy as jnp
from jax import lax
from jax.experimental import pallas as pl
from jax.experimental.pallas import tpu as pltpu

F32 = jnp.float32
BF16 = jnp.bfloat16

D_MODEL = 1024
SB_WIDTH = 512
MLA_WIDTH = 512
Q_RANK = 384
KV_RANK = 256
ROPE_DIM = 32
N_HEADS = 8
IN_WIDTH = 5280
EPS = 1e-6
ROPE_BASE = 10000.0
MLA_SCALE = 1.0 / math.sqrt(96.0)
SB_SCALE = 0.125
LOG2E = 1.4426950408889634

ADAM_LR = 0.001
ADAM_B1 = 0.9
ADAM_B2 = 0.999
ADAM_EPS = 1e-08
ADAM_WD = 0.01
ADAM_STEP = 10

O_SQ, O_SK, O_SV, O_SZ, O_CQ, O_CKV, O_MZ, O_GA, O_GB, O_KR, O_END = (
    0, 512, 1024, 1536, 2048, 2432, 2688, 3200, 4224, 5248, 5504)
W_INT = 5632

N_CHIPS = 4
N_DEV = 8
LANES = 128
SV_COLS = 768

ROW_TILE = 256
ATT_TILE = 256
ATT_Q_TILES = 2
FWD_Q_TILES = 4
SB_Q_TILES = 1
MLA_KEY_TILE = 512
VMEM_LIMIT = 56 * 1024 * 1024

MESH = pl.DeviceIdType.MESH


def _dot(a, b):
    return lax.dot_general(a, b, (((1,), (0,)), ((), ())), preferred_element_type=F32)


def _dot_nt(a, b):
    return lax.dot_general(a, b, (((1,), (1,)), ((), ())), preferred_element_type=F32)


def _dot_tn(a, b):
    return lax.dot_general(a, b, (((0,), (0,)), ((), ())), preferred_element_type=F32)


def _sigmoid(z):
    return 1.0 / (1.0 + jnp.exp(-z))


def _params(sem=None):
    if sem is None:
        return pltpu.CompilerParams(vmem_limit_bytes=VMEM_LIMIT)
    return pltpu.CompilerParams(dimension_semantics=sem, vmem_limit_bytes=VMEM_LIMIT)


def _rows(tm, n):
    return pl.BlockSpec((tm, n), lambda i: (i, 0))


def _cols(n, tm):
    return pl.BlockSpec((n, tm), lambda i: (0, i))


def _whole(shape):
    nd = len(shape)
    return pl.BlockSpec(shape, lambda i: (0,) * nd)


def _sds(shape, dtype):
    return jax.ShapeDtypeStruct(shape, dtype)


def _flip(v, d):
    return 1 - v if d else v


def _inproj_call(x, shift, scale, g1, w_int, w_q, w_kv, qg, kvg, cos256, sin256):
    s_len = x.shape[0]
    tm = min(ROW_TILE, s_len)

    def body(x_ref, sh_ref, sc_ref, g1_ref, w_ref, wq_ref, wkv_ref, qg_ref, kvg_ref, cos_ref, sin_ref,
             h_ref, sq_ref, sk_ref, sv_ref, sz_ref, cq_ref, ckv_ref, mz_ref, ga_ref, gb_ref, kpt_ref,
             qn_ref, qp_ref, kn_ref, vv_ref):
        xt = x_ref[...]
        r = lax.rsqrt(jnp.mean(xt * xt, axis=-1, keepdims=True) + EPS)
        h = (xt * r * g1_ref[...]) * (1.0 + sc_ref[...]) + sh_ref[...]
        hb = h.astype(BF16)
        h_ref[...] = hb

        def seg(a, b):
            return _dot_nt(hb, w_ref[a:b, :])

        sq_ref[...] = (seg(O_SQ, O_SK) * SB_SCALE).astype(BF16)
        sk_ref[...] = seg(O_SK, O_SV).astype(BF16)
        sv_ref[...] = seg(O_SV, O_SZ).astype(BF16)
        sz_ref[...] = seg(O_SZ, O_CQ)
        mz_ref[...] = seg(O_MZ, O_GA)
        ga_ref[...] = seg(O_GA, O_GB)
        gb_ref[...] = seg(O_GB, O_KR)
        cos = cos_ref[...]
        sin = sin_ref[...]
        kr = seg(O_KR, O_END)
        kpt_ref[...] = (kr[:, :128] * cos[:, :128] + kr[:, 128:] * sin[:, :128]).astype(BF16)

        cq = seg(O_CQ, O_CKV)
        cq_ref[...] = cq
        rq = lax.rsqrt(jnp.mean(cq * cq, axis=-1, keepdims=True) + EPS)
        cqn = (cq * rq * qg_ref[...]).astype(BF16)
        qa = _dot(cqn, wq_ref[...])
        qn_ref[...] = qa[:, :512].astype(BF16)
        qp_ref[...] = (qa[:, 512:768] * cos + qa[:, 768:] * sin).astype(BF16)

        ckv = seg(O_CKV, O_MZ)
        ckv_ref[...] = ckv
        rk = lax.rsqrt(jnp.mean(ckv * ckv, axis=-1, keepdims=True) + EPS)
        ckvn = (ckv * rk * kvg_ref[...]).astype(BF16)
        kva = _dot(ckvn, wkv_ref[...])
        kn_ref[...] = kva[:, :512].astype(BF16)
        vv_ref[...] = kva[:, 512:].astype(BF16)

    outs = [
        (D_MODEL, BF16), (512, BF16), (512, BF16), (512, BF16), (512, F32), (Q_RANK, F32), (KV_RANK, F32),
        (512, F32), (D_MODEL, F32), (D_MODEL, F32), (128, BF16), (512, BF16), (256, BF16), (512, BF16), (512, BF16),
    ]
    return pl.pallas_call(
        body, name="inproj", grid=(s_len // tm,),
        out_shape=tuple(_sds((s_len, n), dt) for n, dt in outs),
        in_specs=[_rows(tm, D_MODEL), _whole((1, D_MODEL)), _whole((1, D_MODEL)), _whole((1, D_MODEL)),
                  _whole((W_INT, D_MODEL)), _whole((Q_RANK, 1024)), _whole((KV_RANK, 1024)),
                  _whole((1, Q_RANK)), _whole((1, KV_RANK)), _rows(tm, 256), _rows(tm, 256)],
        out_specs=tuple(_rows(tm, n) for n, _ in outs),
        compiler_params=_params(("parallel",)),
    )(x, shift, scale, g1, w_int, w_q, w_kv, qg, kvg, cos256, sin256)


Z_CLAMP = 80.0 * LOG2E
RUN_CUTOFF = 110.0 * LOG2E


def _softplus_clamped(z):
    zc = jnp.minimum(z * LOG2E, Z_CLAMP)
    return zc, jnp.log2(1.0 + jnp.exp2(zc))


def _tri_sum(a, tri):
    return _dot(a.astype(BF16), tri)


def _sb_fwd_call(q, k, v):
    s_len = q.shape[0]
    tk = min(ATT_TILE, s_len)
    tq = min(SB_Q_TILES * ATT_TILE, s_len)
    r = tq // tk
    nq = s_len // tq

    def body(q_ref, k_ref, v_ref, o_ref, lt_ref, first_ref):
        i = pl.program_id(1)
        q2 = q_ref[...]
        lane = lax.broadcasted_iota(jnp.int32, (1, 256), 1)
        krow = lax.broadcasted_iota(jnp.int32, (tk, tk), 0)
        kcol = lax.broadcasted_iota(jnp.int32, (tk, tk), 1)
        row = lax.broadcasted_iota(jnp.int32, (tq, tk), 0)
        col = lax.broadcasted_iota(jnp.int32, (tq, tk), 1)
        later = (krow > kcol).astype(BF16)
        valids = [col + u * tk < row for u in range(r)]
        hms = [(lane // 64) == hh for hh in range(4)]
        qms = [jnp.where(hm, q2, jnp.zeros_like(q2)) for hm in hms]

        def block(j, carry, valid):
            runs, acc = list(carry[:4]), carry[4]
            off = pl.multiple_of(j * tk, tk)
            kb = k_ref[pl.ds(off, tk), :]
            vb = v_ref[pl.ds(off, tk), :]
            ws = []
            for hh in range(4):
                zc, sp = _softplus_clamped(_dot_nt(qms[hh], kb))
                lm = jnp.where(valid, sp, 0.0) if valid is not None else sp
                suf = _tri_sum(lm, later)
                w = jnp.exp2(zc - sp - suf - runs[hh])
                if valid is not None:
                    w = jnp.where(valid, w, 0.0)
                ws.append(w.astype(BF16))
                runs[hh] = runs[hh] + jnp.sum(lm, axis=1, keepdims=True)
            vstack = jnp.concatenate([jnp.where(hm, vb, jnp.zeros_like(vb)) for hm in hms], axis=0)
            acc = acc + _dot(jnp.concatenate(ws, axis=1), vstack)
            return (*runs, acc)

        zero = jnp.zeros((tq, 1), F32)
        carry = (zero, zero, zero, zero, jnp.zeros((tq, 256), F32))
        for u in reversed(range(r)):
            carry = block(i * r + u, carry, valids[u])

        def least_run(runs):
            return jnp.min(jnp.minimum(jnp.minimum(runs[0], runs[1]), jnp.minimum(runs[2], runs[3])))

        n_full = i * r

        def unfinished(state):
            return jnp.logical_and(state[0] < n_full, state[1] <= RUN_CUTOFF)

        def visit(state):
            cr = block(n_full - 1 - state[0], state[2:], None)
            return (state[0] + 1, least_run(cr[:4]), *cr)

        state = lax.while_loop(unfinished, visit, (jnp.int32(0), least_run(carry[:4]), *carry))
        carry = state[2:]
        first_ref[...] = jnp.full(first_ref.shape, n_full - state[0], jnp.int32)
        for hh in range(4):
            lt_ref[0, :, hh:hh + 1] = carry[hh]
        o_ref[...] = carry[4]

    return pl.pallas_call(
        body, name="sb_fwd", grid=(2, nq),
        out_shape=(_sds((s_len, SB_WIDTH), F32), _sds((2, s_len, 4), F32), _sds((2, nq, 8, 128), jnp.int32)),
        in_specs=[pl.BlockSpec((tq, 256), lambda g, i: (i, g)),
                  pl.BlockSpec((s_len, 256), lambda g, i: (0, g)),
                  pl.BlockSpec((s_len, 256), lambda g, i: (0, g))],
        out_specs=(pl.BlockSpec((tq, 256), lambda g, i: (i, g)),
                   pl.BlockSpec((1, tq, 4), lambda g, i: (g, i, 0)),
                   pl.BlockSpec((1, 1, 8, 128), lambda g, i: (g, i, 0, 0))),
        compiler_params=_params(("parallel", "parallel")),
    )(q, k, v)


def _sb_bwd_call(first, q, k, v, do, lt):
    s_len = q.shape[0]
    tk = min(ATT_TILE, s_len)
    tq = min(SB_Q_TILES * ATT_TILE, s_len)
    r = tq // tk
    nq = s_len // tq
    nq_fwd = first.shape[0] // 2
    per_fwd = nq // nq_fwd

    def body(first_ref, q_ref, k_ref, v_ref, do_ref, lt_ref, dq_ref, dk_ref, dv_ref):
        g = pl.program_id(0)
        i = pl.program_id(1)

        @pl.when(i == 0)
        def _():
            dk_ref[...] = jnp.zeros_like(dk_ref)
            dv_ref[...] = jnp.zeros_like(dv_ref)

        q2 = q_ref[...]
        do2 = do_ref[...].astype(BF16)
        lane = lax.broadcasted_iota(jnp.int32, (1, 256), 1)
        krow = lax.broadcasted_iota(jnp.int32, (tk, tk), 0)
        kcol = lax.broadcasted_iota(jnp.int32, (tk, tk), 1)
        row = lax.broadcasted_iota(jnp.int32, (tq, tk), 0)
        col = lax.broadcasted_iota(jnp.int32, (tq, tk), 1)
        earlier = (krow < kcol).astype(BF16)
        later = (krow > kcol).astype(BF16)
        valids = [col + u * tk < row for u in range(r)]
        hms = [(lane // 64) == hh for hh in range(4)]
        qms = [jnp.where(hm, q2, jnp.zeros_like(q2)) for hm in hms]
        doms = [jnp.where(hm, do2, jnp.zeros_like(do2)) for hm in hms]
        ltots = [lt_ref[0, :, hh:hh + 1] for hh in range(4)]
        q2t = jnp.transpose(q2.astype(F32))
        do2t = jnp.transpose(do_ref[...])
        subl = lax.broadcasted_iota(jnp.int32, (256, 1), 0)
        qtstack = jnp.concatenate(
            [jnp.where((subl // 64) == hh, q2t, 0.0).astype(BF16) for hh in range(4)], axis=1)
        dotstack = jnp.concatenate(
            [jnp.where((subl // 64) == hh, do2t, 0.0).astype(BF16) for hh in range(4)], axis=1)

        def block(j, carry, valid):
            lpre, ppre, dq = list(carry[0:4]), list(carry[4:8]), carry[8]
            off = pl.multiple_of(j * tk, tk)
            kb = k_ref[pl.ds(off, tk), :]
            vb = v_ref[pl.ds(off, tk), :]
            dzs, avs = [], []
            for hh in range(4):
                zc, sp = _softplus_clamped(_dot_nt(qms[hh], kb))
                lsig = zc - sp
                lm = jnp.where(valid, sp, 0.0) if valid is not None else sp
                rowsum = jnp.sum(lm, axis=1, keepdims=True)
                between = _tri_sum(lm, later) + ((ltots[hh] - lpre[hh]) - rowsum)
                a = jnp.exp2(lsig - between)
                if valid is not None:
                    a = jnp.where(valid, a, 0.0)
                p = a * _dot_nt(doms[hh], vb)
                pbefore = ppre[hh] + _tri_sum(p, earlier)
                dz = p - jnp.exp2(lsig) * (p + pbefore)
                if valid is not None:
                    dz = jnp.where(valid, dz, 0.0)
                dzs.append(dz.astype(BF16))
                avs.append(a.astype(BF16))
                lpre[hh] = lpre[hh] + rowsum
                ppre[hh] = ppre[hh] + jnp.sum(p, axis=1, keepdims=True)
            kstack = jnp.concatenate([jnp.where(hm, kb, jnp.zeros_like(kb)) for hm in hms], axis=0)
            dq = dq + _dot(jnp.concatenate(dzs, axis=1), kstack)
            dk_ref[:, pl.ds(off, tk)] += _dot(qtstack, jnp.concatenate(dzs, axis=0))
            dv_ref[:, pl.ds(off, tk)] += _dot(dotstack, jnp.concatenate(avs, axis=0))
            return (*lpre, *ppre, dq)

        zero = jnp.zeros((tq, 1), F32)
        start = jnp.minimum(first_ref[g * nq_fwd + i // per_fwd], i * r)
        carry = lax.fori_loop(start, i * r, lambda j, cr: block(j, cr, None),
                              (zero,) * 8 + (jnp.zeros((tq, 256), F32),))
        for u in range(r):
            carry = block(i * r + u, carry, valids[u])
        dq_ref[...] = carry[8].astype(BF16)

    return pl.pallas_call(
        body, name="sb_bwd",
        out_shape=(_sds((s_len, SB_WIDTH), BF16), _sds((SB_WIDTH, s_len), F32), _sds((SB_WIDTH, s_len), F32)),
        grid_spec=pltpu.PrefetchScalarGridSpec(
            num_scalar_prefetch=1, grid=(2, nq),
            in_specs=[pl.BlockSpec((tq, 256), lambda g, i, f: (i, g)),
                      pl.BlockSpec((s_len, 256), lambda g, i, f: (0, g)),
                      pl.BlockSpec((s_len, 256), lambda g, i, f: (0, g)),
                      pl.BlockSpec((tq, 256), lambda g, i, f: (i, g)),
                      pl.BlockSpec((1, tq, 4), lambda g, i, f: (g, i, 0))],
            out_specs=(pl.BlockSpec((tq, 256), lambda g, i, f: (i, g)),
                       pl.BlockSpec((256, s_len), lambda g, i, f: (g, 0)),
                       pl.BlockSpec((256, s_len), lambda g, i, f: (g, 0)))),
        compiler_params=_params(("parallel", "arbitrary")),
    )(first, q, k, v, do, lt)


def _mla_fwd_call(qn, qp, kn, kpt, v):
    s_len = qn.shape[0]
    tk = min(MLA_KEY_TILE, s_len)
    tq = min(FWD_Q_TILES * ATT_TILE, s_len)
    r = tq // tk
    nq = s_len // tq

    def body(qn_ref, qp_ref, kn_ref, kpt_ref, v_ref, o_ref, lse_ref):
        i = pl.program_id(1)
        qn2 = qn_ref[...]
        qp2 = qp_ref[...]
        lane256 = lax.broadcasted_iota(jnp.int32, (1, 256), 1)
        lane128 = lax.broadcasted_iota(jnp.int32, (1, 128), 1)
        krow = lax.broadcasted_iota(jnp.int32, (tk, tk), 0)
        kcol = lax.broadcasted_iota(jnp.int32, (tk, tk), 1)
        row = lax.broadcasted_iota(jnp.int32, (tq, tk), 0)
        col = lax.broadcasted_iota(jnp.int32, (tq, tk), 1)
        valids = [col + u * tk <= row for u in range(r)]
        m64s = [(lane256 // 64) == hh for hh in range(4)]
        half = [(lane128 // 64) == u for u in range(2)]
        m32s = [(lane128 // 32) == hh for hh in range(4)]
        qcs = []
        for hh in range(4):
            qpair = qn2[:, 128 * (hh // 2):128 * (hh // 2) + 128]
            qcs.append(jnp.concatenate([jnp.where(half[hh % 2], qpair, jnp.zeros_like(qpair)),
                                        jnp.where(m32s[hh], qp2, jnp.zeros_like(qp2))], axis=1))

        def by_head(vals):
            return jnp.where(m64s[0], vals[0], jnp.where(m64s[1], vals[1], jnp.where(m64s[2], vals[2], vals[3])))

        def block(j, carry, valid):
            ms, ls, acc = list(carry[0:4]), list(carry[4:8]), carry[8]
            off = pl.multiple_of(j * tk, tk)
            knb = kn_ref[pl.ds(off, tk), :]
            kpb = kpt_ref[pl.ds(off, tk), :]
            vb = v_ref[pl.ds(off, tk), :]
            kcs = [jnp.concatenate([knb[:, 128 * pp:128 * pp + 128], kpb], axis=1) for pp in range(2)]
            ps, alphas = [], []
            for hh in range(4):
                s = _dot_nt(qcs[hh], kcs[hh // 2]) * (MLA_SCALE * LOG2E)
                if valid is not None:
                    s = jnp.where(valid, s, -1e30)
                mn = jnp.maximum(ms[hh], jnp.max(s, axis=1, keepdims=True))
                p = jnp.exp2(s - mn)
                alpha = jnp.exp2(ms[hh] - mn)
                ls[hh] = alpha * ls[hh] + jnp.sum(p, axis=1, keepdims=True)
                ms[hh] = mn
                ps.append(p.astype(BF16))
                alphas.append(alpha)
            pvs = []
            for pp in range(2):
                vpair = vb[:, 128 * pp:128 * pp + 128]
                vstack = jnp.concatenate([jnp.where(hf, vpair, jnp.zeros_like(vpair)) for hf in half], axis=0)
                pvs.append(_dot(jnp.concatenate(ps[2 * pp:2 * pp + 2], axis=1), vstack))
            acc = by_head(alphas) * acc + jnp.concatenate(pvs, axis=1)
            return (*ms, *ls, acc)

        neg = jnp.full((tq, 1), -1e30, F32)
        zero = jnp.zeros((tq, 1), F32)
        carry = lax.fori_loop(0, i * r, lambda j, cr: block(j, cr, None),
                              (neg,) * 4 + (zero,) * 4 + (jnp.zeros((tq, 256), F32),))
        for u in range(r):
            carry = block(i * r + u, carry, valids[u])
        o_ref[...] = carry[8] / by_head(list(carry[4:8]))
        for hh in range(4):
            lse_ref[0, :, hh:hh + 1] = (carry[hh] + jnp.log2(carry[4 + hh])) * (1.0 / LOG2E)

    return pl.pallas_call(
        body, name="mla_fwd", grid=(2, nq),
        out_shape=(_sds((s_len, MLA_WIDTH), F32), _sds((2, s_len, 4), F32)),
        in_specs=[pl.BlockSpec((tq, 256), lambda g, i: (i, g)),
                  pl.BlockSpec((tq, 128), lambda g, i: (i, g)),
                  pl.BlockSpec((s_len, 256), lambda g, i: (0, g)),
                  pl.BlockSpec((s_len, 128), lambda g, i: (0, 0)),
                  pl.BlockSpec((s_len, 256), lambda g, i: (0, g))],
        out_specs=(pl.BlockSpec((tq, 256), lambda g, i: (i, g)),
                   pl.BlockSpec((1, tq, 4), lambda g, i: (g, i, 0))),
        compiler_params=_params(("parallel", "parallel")),
    )(qn, qp, kn, kpt, v)


def _mla_bwd_call(qn, qp, kn, kpt, v, o, do, lse):
    s_len = qn.shape[0]
    tk = min(MLA_KEY_TILE, s_len)
    tq = min(ATT_Q_TILES * ATT_TILE, s_len)
    r = tq // tk
    nq = s_len // tq

    def body(qn_ref, qp_ref, kn_ref, kpt_ref, v_ref, o_ref, do_ref, lse_ref,
             dqn_ref, dqp_ref, dkn_ref, dkpt_ref, dv_ref):
        g = pl.program_id(0)
        i = pl.program_id(1)

        @pl.when(i == 0)
        def _():
            dkn_ref[...] = jnp.zeros_like(dkn_ref)
            dv_ref[...] = jnp.zeros_like(dv_ref)

        @pl.when((i == 0) & (g == 0))
        def _():
            dkpt_ref[...] = jnp.zeros_like(dkpt_ref)

        qn2 = qn_ref[...]
        qp2 = qp_ref[...]
        dof = do_ref[...]
        dob = dof.astype(BF16)
        prod = dof * o_ref[...]
        lane256 = lax.broadcasted_iota(jnp.int32, (1, 256), 1)
        lane128 = lax.broadcasted_iota(jnp.int32, (1, 128), 1)
        krow = lax.broadcasted_iota(jnp.int32, (tk, tk), 0)
        kcol = lax.broadcasted_iota(jnp.int32, (tk, tk), 1)
        row = lax.broadcasted_iota(jnp.int32, (tq, tk), 0)
        col = lax.broadcasted_iota(jnp.int32, (tq, tk), 1)
        valids = [col + u * tk <= row for u in range(r)]
        m64s = [(lane256 // 64) == hh for hh in range(4)]
        half = [(lane128 // 64) == u for u in range(2)]
        m32s = [(lane128 // 32) == hh for hh in range(4)]
        qcs, doms = [], []
        for hh in range(4):
            sl = slice(128 * (hh // 2), 128 * (hh // 2) + 128)
            qpair = qn2[:, sl]
            dpair = dob[:, sl]
            qcs.append(jnp.concatenate([jnp.where(half[hh % 2], qpair, jnp.zeros_like(qpair)),
                                        jnp.where(m32s[hh], qp2, jnp.zeros_like(qp2))], axis=1))
            doms.append(jnp.where(half[hh % 2], dpair, jnp.zeros_like(dpair)))
        dsums = [jnp.sum(jnp.where(m64, prod, 0.0), axis=1, keepdims=True) * MLA_SCALE for m64 in m64s]
        lses = [lse_ref[0, :, hh:hh + 1] * LOG2E for hh in range(4)]
        qn2t = jnp.transpose(qn2.astype(F32))
        qp2t = jnp.transpose(qp2.astype(F32))
        do2t = jnp.transpose(dof)
        sub128 = lax.broadcasted_iota(jnp.int32, (128, 1), 0)
        qtstacks, dotstacks = [], []
        for pp in range(2):
            qts, dts = [], []
            for u in range(2):
                hh = 2 * pp + u
                qts.append(jnp.concatenate(
                    [jnp.where((sub128 // 64) == u, qn2t[128 * pp:128 * pp + 128, :], 0.0),
                     jnp.where((sub128 // 32) == hh, qp2t, 0.0)], axis=0).astype(BF16))
                dts.append(jnp.where((sub128 // 64) == u, do2t[128 * pp:128 * pp + 128, :], 0.0).astype(BF16))
            qtstacks.append(jnp.concatenate(qts, axis=1))
            dotstacks.append(jnp.concatenate(dts, axis=1))

        def block(j, carry, valid):
            dqn, dqp = carry
            off = pl.multiple_of(j * tk, tk)
            knb = kn_ref[pl.ds(off, tk), :]
            kpb = kpt_ref[pl.ds(off, tk), :]
            vb = v_ref[pl.ds(off, tk), :]
            dqn_parts = []
            dkp = None
            for pp in range(2):
                sl = slice(128 * pp, 128 * pp + 128)
                knp = knb[:, sl]
                vpair = vb[:, sl]
                kc = jnp.concatenate([knp, kpb], axis=1)
                dss, pbs, kcms = [], [], []
                for u in range(2):
                    hh = 2 * pp + u
                    s = _dot_nt(qcs[hh], kc) * (MLA_SCALE * LOG2E)
                    if valid is not None:
                        s = jnp.where(valid, s, -1e30)
                    p = jnp.exp2(s - lses[hh])
                    ds = p * (_dot_nt(doms[hh], vpair) * MLA_SCALE - dsums[hh])
                    dss.append(ds.astype(BF16))
                    pbs.append(p.astype(BF16))
                    kcms.append(jnp.concatenate([jnp.where(half[u], knp, jnp.zeros_like(knp)),
                                                 jnp.where(m32s[hh], kpb, jnp.zeros_like(kpb))], axis=1))
                dqc = _dot(jnp.concatenate(dss, axis=1), jnp.concatenate(kcms, axis=0))
                dqn_parts.append(dqc[:, :128])
                dqp = dqp + dqc[:, 128:]
                dkc = _dot(qtstacks[pp], jnp.concatenate(dss, axis=0))
                dkn_ref[128 * pp:128 * pp + 128, pl.ds(off, tk)] += dkc[:128, :]
                dkp = dkc[128:, :] if dkp is None else dkp + dkc[128:, :]
                dv_ref[128 * pp:128 * pp + 128, pl.ds(off, tk)] += _dot(dotstacks[pp], jnp.concatenate(pbs, axis=0))
            dqn = dqn + jnp.concatenate(dqn_parts, axis=1)
            dkpt_ref[:, pl.ds(off, tk)] += dkp
            return dqn, dqp

        carry = lax.fori_loop(0, i * r, lambda j, cr: block(j, cr, None),
                              (jnp.zeros((tq, 256), F32), jnp.zeros((tq, 128), F32)))
        for u in range(r):
            carry = block(i * r + u, carry, valids[u])
        dqn, dqp = carry
        dqn_ref[...] = dqn.astype(BF16)
        dqp_ref[...] = dqp.astype(BF16)

    return pl.pallas_call(
        body, name="mla_bwd", grid=(2, nq),
        out_shape=(_sds((s_len, 512), BF16), _sds((s_len, 256), BF16), _sds((512, s_len), F32),
                   _sds((128, s_len), F32), _sds((512, s_len), F32)),
        in_specs=[pl.BlockSpec((tq, 256), lambda g, i: (i, g)),
                  pl.BlockSpec((tq, 128), lambda g, i: (i, g)),
                  pl.BlockSpec((s_len, 256), lambda g, i: (0, g)),
                  pl.BlockSpec((s_len, 128), lambda g, i: (0, 0)),
                  pl.BlockSpec((s_len, 256), lambda g, i: (0, g)),
                  pl.BlockSpec((tq, 256), lambda g, i: (i, g)),
                  pl.BlockSpec((tq, 256), lambda g, i: (i, g)),
                  pl.BlockSpec((1, tq, 4), lambda g, i: (g, i, 0))],
        out_specs=(pl.BlockSpec((tq, 256), lambda g, i: (i, g)),
                   pl.BlockSpec((tq, 128), lambda g, i: (i, g)),
                   pl.BlockSpec((256, s_len), lambda g, i: (g, 0)),
                   pl.BlockSpec((128, s_len), lambda g, i: (0, 0)),
                   pl.BlockSpec((256, s_len), lambda g, i: (g, 0))),
        compiler_params=_params(("arbitrary", "arbitrary")),
    )(qn, qp, kn, kpt, v, o, do, lse)


def _post_call(x, tgt, oa, ob, sz, mz, ga, gb, gate, gf, wa, wb, wo, wat, wbt, wot):
    s_len = x.shape[0]
    tm = min(ROW_TILE, s_len)

    def body(x_ref, t_ref, oa_ref, ob_ref, sz_ref, mz_ref, ga_ref, gb_ref, gate_ref, gf_ref,
             wa_ref, wb_ref, wo_ref, wat_ref, wbt_ref, wot_ref,
             dx2_ref, doa_ref, dob_ref, dsz_ref, dmz_ref, dga_ref, dgb_ref,
             dwo_ref, dwa_ref, dwb_ref, dgf_ref, dgate_ref, loss_ref):
        @pl.when(pl.program_id(0) == 0)
        def _():
            dwo_ref[...] = jnp.zeros_like(dwo_ref)
            dwa_ref[...] = jnp.zeros_like(dwa_ref)
            dwb_ref[...] = jnp.zeros_like(dwb_ref)
            dgf_ref[...] = jnp.zeros_like(dgf_ref)
            dgate_ref[...] = jnp.zeros_like(dgate_ref)
            loss_ref[...] = jnp.zeros_like(loss_ref)

        gate = gate_ref[...]
        gf = gf_ref[...]
        oa = oa_ref[...]
        ob = ob_ref[...]
        sz = sz_ref[...]
        mz = mz_ref[...]
        sa = _sigmoid(sz)
        sb = _sigmoid(mz)
        silu_a = sz * sa
        silu_b = mz * sb
        ua = (oa * silu_a).astype(BF16)
        ub = (ob * silu_b).astype(BF16)
        ya = _dot(ua, wa_ref[...])
        yb = _dot(ub, wb_ref[...])
        sga = _sigmoid(ga_ref[...])
        sgb = _sigmoid(gb_ref[...])
        merged = (sga * ya + sgb * yb).astype(BF16)
        out = _dot(merged, wo_ref[...])
        x2 = x_ref[...] + gate * out
        r2 = lax.rsqrt(jnp.mean(x2 * x2, axis=-1, keepdims=True) + EPS)
        xhat = x2 * r2
        err = xhat * gf - t_ref[...]
        loss_ref[...] += 0.5 * jnp.sum(jnp.sum(err * err, axis=1, keepdims=True), axis=0, keepdims=True) / D_MODEL
        dy = err * (1.0 / D_MODEL)
        dgf_ref[...] += jnp.sum(dy * xhat, axis=0, keepdims=True)
        dxhat = dy * gf
        dx2 = r2 * (dxhat - xhat * jnp.mean(dxhat * xhat, axis=-1, keepdims=True))
        dx2_ref[...] = dx2
        dgate_ref[...] += jnp.sum(dx2 * out, axis=0, keepdims=True)
        dout = (dx2 * gate).astype(BF16)
        dmerged = _dot(dout, wot_ref[...])
        dwo_ref[...] += _dot_tn(merged, dout)
        dya = dmerged * sga
        dyb = dmerged * sgb
        dga_ref[...] = (dya * ya * (1.0 - sga)).astype(BF16)
        dgb_ref[...] = (dyb * yb * (1.0 - sgb)).astype(BF16)
        dyab = dya.astype(BF16)
        dybb = dyb.astype(BF16)
        dua = _dot(dyab, wat_ref[...])
        dub = _dot(dybb, wbt_ref[...])
        dwa_ref[...] += _dot_tn(ua, dyab)
        dwb_ref[...] += _dot_tn(ub, dybb)
        doa_ref[...] = dua * silu_a
        dob_ref[...] = dub * silu_b
        dsz_ref[...] = (dua * oa * (sa * (1.0 + sz * (1.0 - sa)))).astype(BF16)
        dmz_ref[...] = (dub * ob * (sb * (1.0 + mz * (1.0 - sb)))).astype(BF16)

    return pl.pallas_call(
        body, name="post", grid=(s_len // tm,),
        out_shape=(_sds((s_len, D_MODEL), F32), _sds((s_len, 512), F32), _sds((s_len, 512), F32),
                   _sds((s_len, 512), BF16), _sds((s_len, 512), BF16),
                   _sds((s_len, D_MODEL), BF16), _sds((s_len, D_MODEL), BF16),
                   _sds((D_MODEL, D_MODEL), F32), _sds((512, D_MODEL), F32), _sds((512, D_MODEL), F32),
                   _sds((1, D_MODEL), F32), _sds((1, D_MODEL), F32), _sds((1, 128), F32)),
        in_specs=[_rows(tm, D_MODEL), _rows(tm, D_MODEL), _rows(tm, 512), _rows(tm, 512), _rows(tm, 512),
                  _rows(tm, 512), _rows(tm, D_MODEL), _rows(tm, D_MODEL), _whole((1, D_MODEL)), _whole((1, D_MODEL)),
                  _whole((512, D_MODEL)), _whole((512, D_MODEL)), _whole((D_MODEL, D_MODEL)),
                  _whole((D_MODEL, 512)), _whole((D_MODEL, 512)), _whole((D_MODEL, D_MODEL))],
        out_specs=(_rows(tm, D_MODEL), _rows(tm, 512), _rows(tm, 512), _rows(tm, 512), _rows(tm, 512),
                   _rows(tm, D_MODEL), _rows(tm, D_MODEL),
                   _whole((D_MODEL, D_MODEL)), _whole((512, D_MODEL)), _whole((512, D_MODEL)),
                   _whole((1, D_MODEL)), _whole((1, D_MODEL)), _whole((1, 128))),
        compiler_params=_params(("arbitrary",)),
    )(x, tgt, oa, ob, sz, mz, ga, gb, gate, gf, wa, wb, wo, wat, wbt, wot)


def _bwdprep_call(dsq, dsk, dsv, dsz, dqn, dqp, dkn, dvv, dkpt, dmz, dga, dgb, cq, ckv, cos256, sin256,
                  qg, kvg, wqt, wkvt):
    s_len = cq.shape[0]
    tm = min(ROW_TILE, s_len)

    def body(dsq_ref, dsk_ref, dsv_ref, dsz_ref, dqn_ref, dqp_ref, dkn_ref, dvv_ref, dkpt_ref, dmz_ref,
             dga_ref, dgb_ref, cq_ref, ckv_ref, cos_ref, sin_ref, qg_ref, kvg_ref, wqt_ref, wkvt_ref,
             dp_ref, dwq_ref, dwkv_ref, dqg_ref, dkvg_ref):
        @pl.when(pl.program_id(0) == 0)
        def _():
            dwq_ref[...] = jnp.zeros_like(dwq_ref)
            dwkv_ref[...] = jnp.zeros_like(dwkv_ref)
            dqg_ref[...] = jnp.zeros_like(dqg_ref)
            dkvg_ref[...] = jnp.zeros_like(dkvg_ref)

        cos = cos_ref[...]
        sin = sin_ref[...]
        dp_ref[:, O_SQ:O_SK] = dsq_ref[...] * jnp.asarray(SB_SCALE, BF16)
        dp_ref[:, O_SK:O_SV] = jnp.transpose(dsk_ref[...]).astype(BF16)
        dp_ref[:, O_SV:O_SZ] = jnp.transpose(dsv_ref[...]).astype(BF16)
        dp_ref[:, O_SZ:O_CQ] = dsz_ref[...]
        dp_ref[:, O_MZ:O_GA] = dmz_ref[...]
        dp_ref[:, O_GA:O_GB] = dga_ref[...]
        dp_ref[:, O_GB:O_KR] = dgb_ref[...]
        dkp = jnp.transpose(dkpt_ref[...])
        dp_ref[:, O_KR:O_KR + 128] = (dkp * cos[:, :128]).astype(BF16)
        dp_ref[:, O_KR + 128:O_END] = (dkp * sin[:, :128]).astype(BF16)
        dp_ref[:, O_END:W_INT] = jnp.zeros((tm, W_INT - O_END), BF16)

        cq = cq_ref[...]
        rq = lax.rsqrt(jnp.mean(cq * cq, axis=-1, keepdims=True) + EPS)
        cqh = cq * rq
        qg = qg_ref[...]
        cqn = (cqh * qg).astype(BF16)
        dqp = dqp_ref[...].astype(F32)
        dqa = jnp.concatenate([dqn_ref[...], (dqp * cos).astype(BF16), (dqp * sin).astype(BF16)], axis=1)
        dcqn = _dot(dqa, wqt_ref[...])
        dwq_ref[...] += _dot_tn(cqn, dqa)
        dqg_ref[...] += jnp.sum(dcqn * cqh, axis=0, keepdims=True)
        dh = dcqn * qg
        dcq = rq * (dh - cqh * jnp.mean(dh * cqh, axis=-1, keepdims=True))
        dp_ref[:, O_CQ:O_CKV] = dcq.astype(BF16)

        ckv = ckv_ref[...]
        rk = lax.rsqrt(jnp.mean(ckv * ckv, axis=-1, keepdims=True) + EPS)
        ckh = ckv * rk
        kvg = kvg_ref[...]
        ckvn = (ckh * kvg).astype(BF16)
        dkva = jnp.concatenate([jnp.transpose(dkn_ref[...]).astype(BF16),
                                jnp.transpose(dvv_ref[...]).astype(BF16)], axis=1)
        dckvn = _dot(dkva, wkvt_ref[...])
        dwkv_ref[...] += _dot_tn(ckvn, dkva)
        dkvg_ref[...] += jnp.sum(dckvn * ckh, axis=0, keepdims=True)
        dh2 = dckvn * kvg
        dckv = rk * (dh2 - ckh * jnp.mean(dh2 * ckh, axis=-1, keepdims=True))
        dp_ref[:, O_CKV:O_MZ] = dckv.astype(BF16)

    return pl.pallas_call(
        body, name="bwdprep", grid=(s_len // tm,),
        out_shape=(_sds((s_len, W_INT), BF16), _sds((Q_RANK, 1024), F32), _sds((KV_RANK, 1024), F32),
                   _sds((1, Q_RANK), F32), _sds((1, KV_RANK), F32)),
        in_specs=[_rows(tm, 512), _cols(512, tm), _cols(512, tm), _rows(tm, 512), _rows(tm, 512), _rows(tm, 256),
                  _cols(512, tm), _cols(512, tm), _cols(128, tm), _rows(tm, 512), _rows(tm, D_MODEL),
                  _rows(tm, D_MODEL), _rows(tm, Q_RANK), _rows(tm, KV_RANK), _rows(tm, 256), _rows(tm, 256),
                  _whole((1, Q_RANK)), _whole((1, KV_RANK)), _whole((1024, Q_RANK)), _whole((1024, KV_RANK))],
        out_specs=(_rows(tm, W_INT), _whole((Q_RANK, 1024)), _whole((KV_RANK, 1024)),
                   _whole((1, Q_RANK)), _whole((1, KV_RANK))),
        compiler_params=_params(("arbitrary",)),
    )(dsq, dsk, dsv, dsz, dqn, dqp, dkn, dvv, dkpt, dmz, dga, dgb, cq, ckv, cos256, sin256, qg, kvg, wqt, wkvt)


def _dh_call(dproj, w_int_t, x, dx2, scale, g1):
    s_len = x.shape[0]
    tm = min(2 * ROW_TILE, s_len)

    def body(dp_ref, wt_ref, x_ref, dx2_ref, sc_ref, g1_ref, gx_ref, dsh_ref, dsc_ref, dg1_ref):
        @pl.when(pl.program_id(0) == 0)
        def _():
            dsh_ref[...] = jnp.zeros_like(dsh_ref)
            dsc_ref[...] = jnp.zeros_like(dsc_ref)
            dg1_ref[...] = jnp.zeros_like(dg1_ref)

        dh = _dot(dp_ref[...], wt_ref[...])
        xt = x_ref[...]
        r = lax.rsqrt(jnp.mean(xt * xt, axis=-1, keepdims=True) + EPS)
        xh = xt * r
        g1 = g1_ref[...]
        xg = xh * g1
        dsh_ref[...] += jnp.sum(dh, axis=0, keepdims=True)
        dsc_ref[...] += jnp.sum(dh * xg, axis=0, keepdims=True)
        dxg = dh * (1.0 + sc_ref[...])
        dg1_ref[...] += jnp.sum(dxg * xh, axis=0, keepdims=True)
        dxh = dxg * g1
        gx_ref[...] = dx2_ref[...] + r * (dxh - xh * jnp.mean(dxh * xh, axis=-1, keepdims=True))

    return pl.pallas_call(
        body, name="dh", grid=(s_len // tm,),
        out_shape=(_sds((s_len, D_MODEL), F32), _sds((1, D_MODEL), F32), _sds((1, D_MODEL), F32),
                   _sds((1, D_MODEL), F32)),
        in_specs=[_rows(tm, W_INT), _whole((W_INT, D_MODEL)), _rows(tm, D_MODEL), _rows(tm, D_MODEL),
                  _whole((1, D_MODEL)), _whole((1, D_MODEL))],
        out_specs=(_rows(tm, D_MODEL), _whole((1, D_MODEL)), _whole((1, D_MODEL)), _whole((1, D_MODEL))),
        compiler_params=_params(("arbitrary",)),
    )(dproj, w_int_t, x, dx2, scale, g1)


def _small_call(svg, ct, dmod_sh):
    def body(sv_ref, ct_ref, dm_ref, tot_ref, gwada_ref):
        acc = sv_ref[0:1, :]
        for d in range(1, N_DEV):
            acc = acc + sv_ref[d:d + 1, :]
        tot_ref[...] = acc
        gwada_ref[...] = lax.dot_general(ct_ref[...], dm_ref[...], (((1,), (0,)), ((), ())),
                                         precision=lax.Precision.HIGHEST, preferred_element_type=F32)

    vmem = pl.BlockSpec(memory_space=pltpu.VMEM)
    return pl.pallas_call(
        body, name="small_grads",
        out_shape=(_sds((1, 8 * SV_COLS), F32), _sds((D_MODEL, 768), F32)),
        in_specs=[vmem, vmem, vmem], out_specs=(vmem, vmem),
        compiler_params=_params(),
    )(svg, ct, dmod_sh)


def _adamw_tile_rows(rows, cols):
    budget = 2 << 20
    if rows * cols * 4 <= budget or rows % 8:
        return rows
    best = 8
    for tr in range(8, rows + 1, 8):
        if rows % tr == 0 and tr * cols * 4 <= budget:
            best = tr
    return best


def _adamw_call(name, w, g, m, v):
    rows, cols = w.shape
    tr = _adamw_tile_rows(rows, cols)

    def body(w_ref, g_ref, m_ref, v_ref, d_ref, nm_ref, nv_ref):
        gg = g_ref[...]
        m2 = ADAM_B1 * m_ref[...] + (1.0 - ADAM_B1) * gg
        v2 = ADAM_B2 * v_ref[...] + (1.0 - ADAM_B2) * (gg * gg)
        m_hat = m2 / (1.0 - ADAM_B1 ** ADAM_STEP)
        v_hat = v2 / (1.0 - ADAM_B2 ** ADAM_STEP)
        d_ref[...] = -ADAM_LR * (m_hat / (jnp.sqrt(v_hat) + ADAM_EPS) + ADAM_WD * w_ref[...])
        nm_ref[...] = m2
        nv_ref[...] = v2

    spec = pl.BlockSpec((tr, cols), lambda i: (i, 0))
    return pl.pallas_call(
        body, name="adamw_" + name, grid=(rows // tr,),
        out_shape=(_sds((rows, cols), F32),) * 3,
        in_specs=[spec] * 4, out_specs=(spec,) * 3,
        compiler_params=_params(("parallel",)),
    )(w, g, m, v)


IN_SHARD = IN_WIDTH // N_CHIPS
HALF_D = D_MODEL // 2
SMALL_ROWS = (576, 512, 1024, 1024, 2048)
SMALL_TOTAL = sum(SMALL_ROWS)
SMALL_HALF = SMALL_TOTAL // 2
SMALL_SUM_ROWS = 432


def _gather_call(c_row, w_ada_sh, pack_in, pack_small):
    def body(c_ref, wada_ref, pki_ref, pks_ref, mg_ref, cg_ref, gwi_ref, gws_ref,
             cv, ssem_c, rsem_c, ssem_m, rsem_m, ssem_w, rsem_w, ssem_f, rsem_f, lsem):
        x, y, c = lax.axis_index("x"), lax.axis_index("y"), lax.axis_index("c")
        me = 4 * x + 2 * y + c
        chip = 2 * x + y
        rel3 = [(1, 0), (0, 1), (1, 1)]
        packs = [(pki_ref, gwi_ref), (pks_ref, gws_ref)]

        sends = []
        for j, (dx, dy) in enumerate(rel3):
            for a, (pk, gw) in enumerate(packs):
                cp = pltpu.make_async_remote_copy(
                    src_ref=pk.at[c], dst_ref=gw.at[chip, c], send_sem=ssem_w.at[j, a], recv_sem=rsem_w.at[j, a],
                    device_id=(_flip(x, dx), _flip(y, dy), c), device_id_type=MESH)
                cp.start()
                sends.append(cp)
        owns = []
        for a, (pk, gw) in enumerate(packs):
            own = pltpu.make_async_copy(pk, gw.at[chip], lsem.at[a])
            own.start()
            owns.append(own)

        cv[me] = c_ref[...]
        for r in range(1, N_DEV):
            dx, dy, dc = (r >> 2) & 1, (r >> 1) & 1, r & 1
            cp = pltpu.make_async_remote_copy(
                src_ref=c_ref, dst_ref=cv.at[me], send_sem=ssem_c.at[r - 1], recv_sem=rsem_c.at[r - 1],
                device_id=(_flip(x, dx), _flip(y, dy), _flip(c, dc)), device_id_type=MESH)
            cp.start()
            sends.append(cp)
        for r in range(1, N_DEV):
            dx, dy, dc = (r >> 2) & 1, (r >> 1) & 1, r & 1
            src = 4 * _flip(x, dx) + 2 * _flip(y, dy) + _flip(c, dc)
            pltpu.make_async_remote_copy(
                src_ref=c_ref, dst_ref=cv.at[src], send_sem=ssem_c.at[r - 1], recv_sem=rsem_c.at[r - 1],
                device_id=(x, y, c), device_id_type=MESH).wait_recv()
        rows = lax.broadcasted_iota(jnp.int32, (N_DEV, D_MODEL), 0)
        call = jnp.zeros((N_DEV, D_MODEL), F32)
        for b in range(N_DEV):
            call = jnp.where(rows == b, jnp.broadcast_to(cv[b], (N_DEV, D_MODEL)), call)
        cg_ref[...] = call

        mg_ref[chip] = lax.dot_general(call, wada_ref[...], (((1,), (0,)), ((), ())),
                                       precision=lax.Precision.HIGHEST, preferred_element_type=F32)
        for j, (dx, dy) in enumerate(rel3):
            cp = pltpu.make_async_remote_copy(
                src_ref=mg_ref.at[chip], dst_ref=mg_ref.at[chip], send_sem=ssem_m.at[j], recv_sem=rsem_m.at[j],
                device_id=(_flip(x, dx), _flip(y, dy), c), device_id_type=MESH)
            cp.start()
            sends.append(cp)
        for j, (dx, dy) in enumerate(rel3):
            src_chip = 2 * _flip(x, dx) + _flip(y, dy)
            pltpu.make_async_remote_copy(
                src_ref=mg_ref.at[src_chip], dst_ref=mg_ref.at[src_chip], send_sem=ssem_m.at[j],
                recv_sem=rsem_m.at[j], device_id=(x, y, c), device_id_type=MESH).wait_recv()
        for j, (dx, dy) in enumerate(rel3):
            src_chip = 2 * _flip(x, dx) + _flip(y, dy)
            for a, (pk, gw) in enumerate(packs):
                pltpu.make_async_remote_copy(
                    src_ref=pk.at[c], dst_ref=gw.at[src_chip, c], send_sem=ssem_w.at[j, a],
                    recv_sem=rsem_w.at[j, a], device_id=(x, y, c), device_id_type=MESH).wait_recv()
                cp = pltpu.make_async_remote_copy(
                    src_ref=gw.at[src_chip, c], dst_ref=gw.at[src_chip, c], send_sem=ssem_f.at[j, a],
                    recv_sem=rsem_f.at[j, a], device_id=(x, y, 1 - c), device_id_type=MESH)
                cp.start()
                sends.append(cp)
        for j, (dx, dy) in enumerate(rel3):
            src_chip = 2 * _flip(x, dx) + _flip(y, dy)
            for a, (pk, gw) in enumerate(packs):
                pltpu.make_async_remote_copy(
                    src_ref=pk.at[c], dst_ref=gw.at[src_chip, 1 - c], send_sem=ssem_f.at[j, a],
                    recv_sem=rsem_f.at[j, a], device_id=(x, y, c), device_id_type=MESH).wait_recv()
        for cp in sends:
            cp.wait_send()
        for own in owns:
            own.wait()

    vmem = pl.BlockSpec(memory_space=pltpu.VMEM)
    return pl.pallas_call(
        body, name="gather_fwd",
        out_shape=(_sds((N_CHIPS, N_DEV, 768), F32), _sds((N_DEV, D_MODEL), F32),
                   _sds((N_CHIPS, 2, IN_SHARD, HALF_D), BF16), _sds((N_CHIPS, 2, SMALL_HALF, LANES), BF16)),
        in_specs=[vmem, vmem, vmem, vmem], out_specs=(vmem, vmem, vmem, vmem),
        scratch_shapes=[
            pltpu.VMEM((N_DEV, 1, D_MODEL), F32),
            pltpu.SemaphoreType.DMA((N_DEV - 1,)), pltpu.SemaphoreType.DMA((N_DEV - 1,)),
            pltpu.SemaphoreType.DMA((3,)), pltpu.SemaphoreType.DMA((3,)),
            pltpu.SemaphoreType.DMA((3, 2)), pltpu.SemaphoreType.DMA((3, 2)),
            pltpu.SemaphoreType.DMA((3, 2)), pltpu.SemaphoreType.DMA((3, 2)),
            pltpu.SemaphoreType.DMA((2,)),
        ],
        compiler_params=_params(),
    )(c_row, w_ada_sh, pack_in, pack_small)


def _reduce_call(g_in, g_small, sv):
    def body(gi_ref, gs_ref, sv_ref, fi_ref, fs_ref, svg_ref, pair_i, pair_s, send_i, send_s, land_i, land_s,
             ssem_p, rsem_p, ssem_g, rsem_g, ssem_s, rsem_s, ssem_x, rsem_x):
        x, y, c = lax.axis_index("x"), lax.axis_index("y"), lax.axis_index("c")
        me = 4 * x + 2 * y + c
        chip = 2 * x + y
        rel3 = [(1, 0), (0, 1), (1, 1)]
        payloads = [(gi_ref, pair_i, send_i, land_i, fi_ref), (gs_ref, pair_s, send_s, land_s, fs_ref)]
        copies = []

        for k in range(N_CHIPS):
            for a, (g, pair, _, _, _) in enumerate(payloads):
                cp = pltpu.make_async_remote_copy(
                    src_ref=g.at[2 * k + 1 - c], dst_ref=pair.at[k], send_sem=ssem_p.at[k, a],
                    recv_sem=rsem_p.at[k, a], device_id=(x, y, 1 - c), device_id_type=MESH)
                cp.start()
                copies.append(cp)

        for r in range(1, N_DEV):
            dx, dy, dc = (r >> 2) & 1, (r >> 1) & 1, r & 1
            cp = pltpu.make_async_remote_copy(
                src_ref=sv_ref, dst_ref=svg_ref.at[me], send_sem=ssem_s.at[r - 1], recv_sem=rsem_s.at[r - 1],
                device_id=(_flip(x, dx), _flip(y, dy), _flip(c, dc)), device_id_type=MESH)
            cp.start()
            copies.append(cp)
        svg_ref[me] = sv_ref[...]

        def pair_sum(k, store_in, store_small):
            for a, (g, pair, _, _, _) in enumerate(payloads):
                pltpu.make_async_remote_copy(
                    src_ref=g.at[2 * k + c], dst_ref=pair.at[k], send_sem=ssem_p.at[k, a],
                    recv_sem=rsem_p.at[k, a], device_id=(x, y, c), device_id_type=MESH).wait_recv()
            for qd in range(HALF_D // LANES):
                sl = slice(LANES * qd, LANES * qd + LANES)
                store_in(sl, gi_ref[2 * k + c, :, sl].astype(F32) + pair_i[k, :, sl].astype(F32))

            def rows(i, carry):
                sl = pl.ds(pl.multiple_of(i * SMALL_SUM_ROWS, 16), SMALL_SUM_ROWS)
                store_small(sl, gs_ref[2 * k + c, sl, :].astype(F32) + pair_s[k, sl, :].astype(F32))
                return carry

            lax.fori_loop(0, SMALL_HALF // SMALL_SUM_ROWS, rows, 0)

        for j, (dx, dy) in enumerate(rel3):
            tx, ty = _flip(x, dx), _flip(y, dy)

            def put_in(sl, val, j=j):
                send_i[j, :, sl] = val.astype(BF16)

            def put_small(sl, val, j=j):
                send_s[j, sl, :] = val.astype(BF16)

            pair_sum(2 * tx + ty, put_in, put_small)
            for a, (_, _, send, land, _) in enumerate(payloads):
                cp = pltpu.make_async_remote_copy(
                    src_ref=send.at[j], dst_ref=land.at[j], send_sem=ssem_g.at[j, a], recv_sem=rsem_g.at[j, a],
                    device_id=(tx, ty, c), device_id_type=MESH)
                cp.start()
                copies.append(cp)

        def own_in(sl, val):
            fi_ref[c, :, sl] = val

        def own_small(sl, val):
            fs_ref[c, sl, :] = val

        pair_sum(chip, own_in, own_small)
        for j in range(3):
            for a, (_, _, send, land, _) in enumerate(payloads):
                pltpu.make_async_remote_copy(
                    src_ref=send.at[j], dst_ref=land.at[j], send_sem=ssem_g.at[j, a], recv_sem=rsem_g.at[j, a],
                    device_id=(x, y, c), device_id_type=MESH).wait_recv()
            for qd in range(HALF_D // LANES):
                sl = slice(LANES * qd, LANES * qd + LANES)
                fi_ref[c, :, sl] += land_i[j, :, sl].astype(F32)

            def add_rows(i, carry, j=j):
                sl = pl.ds(pl.multiple_of(i * SMALL_SUM_ROWS, 16), SMALL_SUM_ROWS)
                fs_ref[c, sl, :] += land_s[j, sl, :].astype(F32)
                return carry

            lax.fori_loop(0, SMALL_HALF // SMALL_SUM_ROWS, add_rows, 0)

        for a, f in enumerate((fi_ref, fs_ref)):
            cp = pltpu.make_async_remote_copy(
                src_ref=f.at[c], dst_ref=f.at[c], send_sem=ssem_x.at[a], recv_sem=rsem_x.at[a],
                device_id=(x, y, 1 - c), device_id_type=MESH)
            cp.start()
            copies.append(cp)
        for a, f in enumerate((fi_ref, fs_ref)):
            pltpu.make_async_remote_copy(
                src_ref=f.at[c], dst_ref=f.at[1 - c], send_sem=ssem_x.at[a], recv_sem=rsem_x.at[a],
                device_id=(x, y, c), device_id_type=MESH).wait_recv()
        for r in range(1, N_DEV):
            dx, dy, dc = (r >> 2) & 1, (r >> 1) & 1, r & 1
            src = 4 * _flip(x, dx) + 2 * _flip(y, dy) + _flip(c, dc)
            pltpu.make_async_remote_copy(
                src_ref=sv_ref, dst_ref=svg_ref.at[src], send_sem=ssem_s.at[r - 1],
                recv_sem=rsem_s.at[r - 1], device_id=(x, y, c), device_id_type=MESH).wait_recv()
        for cp in copies:
            cp.wait_send()

    vmem = pl.BlockSpec(memory_space=pltpu.VMEM)
    return pl.pallas_call(
        body, name="grad_reduce",
        out_shape=(_sds((2, IN_SHARD, HALF_D), F32), _sds((2, SMALL_HALF, LANES), F32),
                   _sds((N_DEV, 8, SV_COLS), F32)),
        in_specs=[vmem, vmem, vmem], out_specs=(vmem, vmem, vmem),
        scratch_shapes=[
            pltpu.VMEM((N_CHIPS, IN_SHARD, HALF_D), BF16), pltpu.VMEM((N_CHIPS, SMALL_HALF, LANES), BF16),
            pltpu.VMEM((3, IN_SHARD, HALF_D), BF16), pltpu.VMEM((3, SMALL_HALF, LANES), BF16),
            pltpu.VMEM((3, IN_SHARD, HALF_D), BF16), pltpu.VMEM((3, SMALL_HALF, LANES), BF16),
            pltpu.SemaphoreType.DMA((N_CHIPS, 2)), pltpu.SemaphoreType.DMA((N_CHIPS, 2)),
            pltpu.SemaphoreType.DMA((3, 2)), pltpu.SemaphoreType.DMA((3, 2)),
            pltpu.SemaphoreType.DMA((N_DEV - 1,)), pltpu.SemaphoreType.DMA((N_DEV - 1,)),
            pltpu.SemaphoreType.DMA((2,)), pltpu.SemaphoreType.DMA((2,)),
        ],
        compiler_params=_params(),
    )(g_in, g_small, sv)


def _dwin_call(h, dproj):
    s_len = h.shape[0]
    tm = min(4 * ROW_TILE, s_len)
    nrow = s_len // tm
    nc = 4
    chunk = W_INT // nc

    def body(h_ref, dp_ref, dw_ref, acc):
        i = pl.program_id(1)

        @pl.when(i == 0)
        def _():
            acc[...] = jnp.zeros_like(acc)

        acc[...] += _dot_tn(dp_ref[...], h_ref[...])

        @pl.when(i == nrow - 1)
        def _():
            dw_ref[...] = acc[...].astype(BF16)

    return pl.pallas_call(
        body, name="dwin", grid=(nc, nrow),
        out_shape=_sds((W_INT, D_MODEL), BF16),
        in_specs=[pl.BlockSpec((tm, D_MODEL), lambda c, i: (i, 0)),
                  pl.BlockSpec((tm, chunk), lambda c, i: (i, c))],
        out_specs=pl.BlockSpec((chunk, D_MODEL), lambda c, i: (c, 0)),
        scratch_shapes=[pltpu.VMEM((chunk, D_MODEL), F32)],
        compiler_params=_params(("parallel", "arbitrary")),
    )(h, dproj)


def _swap_rows(w, group):
    r, n = w.shape
    return w.reshape(r // group, 2, group // 2, n)[:, ::-1].reshape(r, n)


def _internal_weights(w_in_t, w_uq, w_ukv):
    krot_t = w_in_t[2688:2720]
    w_int_t = jnp.concatenate([
        w_in_t[0:2688], w_in_t[2720:5280],
        jnp.tile(krot_t, (4, 1)), jnp.tile(_swap_rows(krot_t, 32), (4, 1)),
        jnp.zeros((W_INT - O_END, D_MODEL), w_in_t.dtype)], axis=0)
    uq = w_uq.reshape(Q_RANK, N_HEADS, 96)
    wp = uq[:, :, 64:].reshape(Q_RANK, 256)
    w_q = jnp.concatenate([uq[:, :, :64].reshape(Q_RANK, 512), wp, _swap_halves(wp, 32)], axis=1)
    ukv = w_ukv.reshape(KV_RANK, N_HEADS, 128)
    w_kv = jnp.concatenate([ukv[:, :, :64].reshape(KV_RANK, 512), ukv[:, :, 64:].reshape(KV_RANK, 512)], axis=1)
    return w_int_t, w_q, w_kv


def _true_weight_grads(dwi_t, dwq, dwkv):
    dkr = dwi_t[O_KR:O_KR + 128].astype(F32).reshape(4, 32, D_MODEL).sum(axis=0)
    dkr_sw = dwi_t[O_KR + 128:O_END].astype(F32).reshape(4, 32, D_MODEL).sum(axis=0)
    dkrot_t = (dkr + _swap_rows(dkr_sw, 32)).astype(dwi_t.dtype)
    g_in_t = jnp.concatenate([dwi_t[0:O_MZ], dkrot_t, dwi_t[O_MZ:O_KR]], axis=0)
    dwp = dwq[:, 512:768] + _swap_halves(dwq[:, 768:1024], 32)
    g_uq = jnp.concatenate([dwq[:, :512].reshape(Q_RANK, N_HEADS, 64), dwp.reshape(Q_RANK, N_HEADS, 32)],
                           axis=2).reshape(Q_RANK, 768)
    g_ukv = jnp.concatenate([dwkv[:, :512].reshape(KV_RANK, N_HEADS, 64), dwkv[:, 512:].reshape(KV_RANK, N_HEADS, 64)],
                            axis=2).reshape(KV_RANK, 1024)
    return g_in_t, g_uq, g_ukv


def _swap_halves(w, group):
    r, n = w.shape
    return w.reshape(r, n // group, 2, group // 2)[:, :, ::-1, :].reshape(r, n)


def _pack_shards(parts):
    return jnp.concatenate([p.reshape(-1, LANES) for p in parts], axis=0)


def _unpack_small(gw):
    offs = [0]
    for r in SMALL_ROWS:
        offs.append(offs[-1] + r)

    def cols(i, rows, shard_cols):
        blk = gw[:, offs[i]:offs[i + 1]].reshape(N_CHIPS, rows, shard_cols)
        return blk.transpose(1, 0, 2).reshape(rows, N_CHIPS * shard_cols)

    return (cols(0, Q_RANK, 192), cols(1, KV_RANK, 256), cols(2, 512, 256), cols(3, 512, 256),
            gw[:, offs[4]:offs[5]].reshape(D_MODEL, D_MODEL))


def _chip_major(g, shard_cols):
    r = g.shape[0]
    return g.reshape(r, N_CHIPS, shard_cols).transpose(1, 0, 2).reshape(N_CHIPS, -1, LANES)


def kernel(x, c, positions, w_ada, b_ada, norm_gain, w_in, q_norm_gain, w_uq, kv_norm_gain, w_ukv, w_branch_a, w_branch_b, w_out, final_norm_gain, loss_target, m_w_ada, m_b_ada, m_norm_gain, m_w_in, m_q_norm_gain, m_w_uq, m_kv_norm_gain, m_w_ukv, m_w_branch_a, m_w_branch_b, m_w_out, m_final_norm_gain, v_w_ada, v_b_ada, v_norm_gain, v_w_in, v_q_norm_gain, v_w_uq, v_kv_norm_gain, v_w_ukv, v_w_branch_a, v_w_branch_b, v_w_out, v_final_norm_gain):
    ix, iy, ic = lax.axis_index("x"), lax.axis_index("y"), lax.axis_index("c")
    me = 4 * ix + 2 * iy + ic
    chip = 2 * ix + iy
    xs = x[0]
    tgt = loss_target[0]
    s_len = xs.shape[0]

    w_in_t = jnp.swapaxes(w_in[0], 0, 1)
    w_in_tb = w_in_t.astype(BF16)
    pack_in = jnp.stack([w_in_tb[:, :HALF_D], w_in_tb[:, HALF_D:]], axis=0)
    small_shards = (w_uq[0], w_ukv[0], w_branch_a[0], w_branch_b[0], w_out[0])
    pack_small = _pack_shards([s.astype(BF16) for s in small_shards]).reshape(2, SMALL_HALF, LANES)
    mg, call, gw_in, gw_small = _gather_call(c, w_ada[0], pack_in, pack_small)
    mod = mg.transpose(1, 0, 2).reshape(N_DEV, 3 * D_MODEL) + b_ada
    mod_me = lax.dynamic_slice_in_dim(mod, me, 1, axis=0)
    shift, scale, gate = mod_me[:, :D_MODEL], mod_me[:, D_MODEL:2 * D_MODEL], mod_me[:, 2 * D_MODEL:]

    f_in_t = jnp.concatenate([gw_in[:, 0], gw_in[:, 1]], axis=2).reshape(IN_WIDTH, D_MODEL)
    f_uq, f_ukv, f_a, f_b, f_out = _unpack_small(gw_small.reshape(N_CHIPS, SMALL_TOTAL, LANES))
    w_int_t, w_q, w_kv = _internal_weights(f_in_t, f_uq, f_ukv)

    inv_freq = ROPE_BASE ** (-jnp.arange(0, ROPE_DIM, 2, dtype=F32) / ROPE_DIM)
    ang = positions[0].astype(F32)[:, None] * inv_freq
    cs, sn = jnp.cos(ang), jnp.sin(ang)
    cos256 = jnp.tile(jnp.concatenate([cs, cs], axis=1), (1, 8))
    sin256 = jnp.tile(jnp.concatenate([-sn, sn], axis=1), (1, 8))

    (h, sq, sk, sv, sz, cq, ckv, mz, ga, gb, kpt, qn, qp, kn, vv) = _inproj_call(
        xs, shift, scale, norm_gain, w_int_t, w_q, w_kv, q_norm_gain, kv_norm_gain, cos256, sin256)
    oa, lt, first = _sb_fwd_call(sq, sk, sv)
    ob, lse = _mla_fwd_call(qn, qp, kn, kpt, vv)

    gf = final_norm_gain.reshape(1, D_MODEL)
    (dx2, doa, dob, dsz, dmz, dga, dgb, dwo, dwa, dwb, dgf, dgate, loss_p) = _post_call(
        xs, tgt, oa, ob, sz, mz, ga, gb, gate, gf, f_a, f_b, f_out, f_a.T, f_b.T, f_out.T)

    dsq, dsk_t, dsv_t = _sb_bwd_call(first[:, :, 0, 0].reshape(-1), sq, sk, sv, doa, lt)
    dqn, dqp, dkn_t, dkpt_t, dvv_t = _mla_bwd_call(qn, qp, kn, kpt, vv, ob, dob, lse)

    dproj, dwq, dwkv, dqg, dkvg = _bwdprep_call(
        dsq, dsk_t, dsv_t, dsz, dqn, dqp, dkn_t, dvv_t, dkpt_t, dmz, dga, dgb, cq, ckv, cos256, sin256,
        q_norm_gain, kv_norm_gain, w_q.T, w_kv.T)
    grad_x, dshift, dscale, dg1 = _dh_call(dproj, w_int_t, xs, dx2, scale, norm_gain)
    dwi_t = _dwin_call(h, dproj)
    g_in_t, g_uq, g_ukv = _true_weight_grads(dwi_t, dwq, dwkv)

    g_in_c = g_in_t.reshape(N_CHIPS, IN_SHARD, D_MODEL)
    g_in_pieces = jnp.stack([g_in_c[:, :, :HALF_D], g_in_c[:, :, HALF_D:]], axis=1).reshape(N_DEV, IN_SHARD, HALF_D)
    g_small = jnp.concatenate([
        _chip_major(g_uq, 192), _chip_major(g_ukv, 256), _chip_major(dwa, 256), _chip_major(dwb, 256),
        dwo.reshape(N_CHIPS, -1, LANES)], axis=1).astype(BF16).reshape(N_DEV, SMALL_HALF, LANES)
    small = jnp.concatenate([
        dshift, dscale, dgate, dg1, dqg, dkvg, dgf, loss_p,
        jnp.zeros((1, 8 * SV_COLS - 5888), F32)], axis=1).reshape(8, SV_COLS)
    full_in, full_small, svg = _reduce_call(g_in_pieces, g_small, small)
    gs_in_t = jnp.concatenate([full_in[0], full_in[1]], axis=1)
    full = full_small.reshape(SMALL_TOTAL, LANES)
    offs = [0]
    for r in SMALL_ROWS:
        offs.append(offs[-1] + r)
    gs_uq = full[offs[0]:offs[1]].reshape(Q_RANK, 192)
    gs_ukv = full[offs[1]:offs[2]].reshape(KV_RANK, 256)
    gs_a = full[offs[2]:offs[3]].reshape(512, 256)
    gs_b = full[offs[3]:offs[4]].reshape(512, 256)
    gs_out = full[offs[4]:offs[5]].reshape(256, D_MODEL)

    svm = svg.reshape(N_DEV, 8 * SV_COLS)
    dmod_sh = lax.dynamic_slice_in_dim(svm[:, :3 * D_MODEL], chip * 768, 768, axis=1)
    tot, gs_ada = _small_call(svm, call.T, dmod_sh)
    g_bada = tot[:, 0:3072]
    g_g1 = tot[:, 3072:4096]
    g_qg = tot[:, 4096:4480]
    g_kvg = tot[:, 4480:4736]
    g_gf = tot[:, 4736:5760]
    loss = tot[0, 5760]

    names = ["w_ada", "b_ada", "norm_gain", "w_in", "q_norm_gain", "w_uq", "kv_norm_gain", "w_ukv",
             "w_branch_a", "w_branch_b", "w_out", "final_norm_gain"]
    ws = [w_ada[0], b_ada, norm_gain, w_in_t, q_norm_gain, w_uq[0], kv_norm_gain, w_ukv[0],
          w_branch_a[0], w_branch_b[0], w_out[0], final_norm_gain.reshape(1, D_MODEL)]
    gs = [gs_ada, g_bada, g_g1, gs_in_t, g_qg, gs_uq, g_kvg, gs_ukv, gs_a, gs_b, gs_out, g_gf]
    ms = [m_w_ada[0], m_b_ada, m_norm_gain, jnp.swapaxes(m_w_in[0], 0, 1), m_q_norm_gain, m_w_uq[0],
          m_kv_norm_gain, m_w_ukv[0], m_w_branch_a[0], m_w_branch_b[0], m_w_out[0],
          m_final_norm_gain.reshape(1, D_MODEL)]
    vs = [v_w_ada[0], v_b_ada, v_norm_gain, jnp.swapaxes(v_w_in[0], 0, 1), v_q_norm_gain, v_w_uq[0],
          v_kv_norm_gain, v_w_ukv[0], v_w_branch_a[0], v_w_branch_b[0], v_w_out[0],
          v_final_norm_gain.reshape(1, D_MODEL)]
    refs = [w_ada, b_ada, norm_gain, w_in, q_norm_gain, w_uq, kv_norm_gain, w_ukv,
            w_branch_a, w_branch_b, w_out, final_norm_gain]
    grads, deltas, new_ms, new_vs = [], [], [], []
    for n, w_, g_, m_, v_, ref in zip(names, ws, gs, ms, vs, refs):
        outs = (g_,) + _adamw_call(n, w_, g_, m_, v_)
        if n == "w_in":
            outs = tuple(jnp.swapaxes(o_, 0, 1) for o_ in outs)
        for lst, o_ in zip((grads, deltas, new_ms, new_vs), outs):
            lst.append(o_.reshape(ref.shape))

    return (loss, grad_x.reshape(x.shape), *grads, *deltas, *new_ms, *new_vs)
```

```python
import math

import jax
import jax.numpy as jnp
from jax import lax
from jax.experimental import pallas as pl
from jax.experimental.pallas import tpu as pltpu

F32 = jnp.float32
BF16 = jnp.bfloat16

D_MODEL = 1024
SB_WIDTH = 512
MLA_WIDTH = 512
Q_RANK = 384
KV_RANK = 256
ROPE_DIM = 32
N_HEADS = 8
IN_WIDTH = 5280
EPS = 1e-6
ROPE_BASE = 10000.0
MLA_SCALE = 1.0 / math.sqrt(96.0)
SB_SCALE = 0.125
LOG2E = 1.4426950408889634

ADAM_LR = 0.001
ADAM_B1 = 0.9
ADAM_B2 = 0.999
ADAM_EPS = 1e-08
ADAM_WD = 0.01
ADAM_STEP = 10

O_SQ, O_SK, O_SV, O_SZ, O_CQ, O_CKV, O_MZ, O_GA, O_GB, O_KR, O_END = (
    0, 512, 1024, 1536, 2048, 2432, 2688, 3200, 4224, 5248, 5504)
W_INT = 5632

N_CHIPS = 4
N_DEV = 8
LANES = 128
SV_COLS = 768

ROW_TILE = 256
ATT_TILE = 256
ATT_Q_TILES = 2
FWD_Q_TILES = 4
SB_Q_TILES = 1
MLA_KEY_TILE = 512
VMEM_LIMIT = 56 * 1024 * 1024

MESH = pl.DeviceIdType.MESH


def _dot(a, b):
    return lax.dot_general(a, b, (((1,), (0,)), ((), ())), preferred_element_type=F32)


def _dot_nt(a, b):
    return lax.dot_general(a, b, (((1,), (1,)), ((), ())), preferred_element_type=F32)


def _dot_tn(a, b):
    return lax.dot_general(a, b, (((0,), (0,)), ((), ())), preferred_element_type=F32)


def _sigmoid(z):
    return 1.0 / (1.0 + jnp.exp2(z * (-LOG2E)))


def _params(sem=None):
    if sem is None:
        return pltpu.CompilerParams(vmem_limit_bytes=VMEM_LIMIT)
    return pltpu.CompilerParams(dimension_semantics=sem, vmem_limit_bytes=VMEM_LIMIT)


def _rows(tm, n):
    return pl.BlockSpec((tm, n), lambda i: (i, 0))


def _cols(n, tm):
    return pl.BlockSpec((n, tm), lambda i: (0, i))


def _whole(shape):
    nd = len(shape)
    return pl.BlockSpec(shape, lambda i: (0,) * nd)


def _sds(shape, dtype):
    return jax.ShapeDtypeStruct(shape, dtype)


def _flip(v, d):
    return 1 - v if d else v


def _inproj_call(x, shift, scale, g1, w_int, w_q, w_kv, qg, kvg, cos256, sin256):
    s_len = x.shape[0]
    tm = min(ROW_TILE, s_len)

    def body(x_ref, sh_ref, sc_ref, g1_ref, w_ref, wq_ref, wkv_ref, qg_ref, kvg_ref, cos_ref, sin_ref,
             h_ref, sq_ref, sk_ref, sv_ref, sz_ref, cq_ref, ckv_ref, mz_ref, ga_ref, gb_ref, kpt_ref,
             qn_ref, qp_ref, kn_ref, vv_ref):
        xt = x_ref[...]
        r = lax.rsqrt(jnp.mean(xt * xt, axis=-1, keepdims=True) + EPS)
        h = (xt * r * g1_ref[...]) * (1.0 + sc_ref[...]) + sh_ref[...]
        hb = h.astype(BF16)
        h_ref[...] = hb

        def seg(a, b):
            return _dot_nt(hb, w_ref[a:b, :])

        sq_ref[...] = (seg(O_SQ, O_SK) * SB_SCALE).astype(BF16)
        sk_ref[...] = seg(O_SK, O_SV).astype(BF16)
        sv_ref[...] = seg(O_SV, O_SZ).astype(BF16)
        sz_ref[...] = seg(O_SZ, O_CQ)
        mz_ref[...] = seg(O_MZ, O_GA)
        ga_ref[...] = seg(O_GA, O_GB)
        gb_ref[...] = seg(O_GB, O_KR)
        cos = cos_ref[...]
        sin = sin_ref[...]
        kr = seg(O_KR, O_END)
        kpt_ref[...] = (kr[:, :128] * cos[:, :128] + kr[:, 128:] * sin[:, :128]).astype(BF16)

        cq = seg(O_CQ, O_CKV)
        cq_ref[...] = cq
        rq = lax.rsqrt(jnp.mean(cq * cq, axis=-1, keepdims=True) + EPS)
        cqn = (cq * rq * qg_ref[...]).astype(BF16)
        qa = _dot(cqn, wq_ref[...])
        qn_ref[...] = qa[:, :512].astype(BF16)
        qp_ref[...] = (qa[:, 512:768] * cos + qa[:, 768:] * sin).astype(BF16)

        ckv = seg(O_CKV, O_MZ)
        ckv_ref[...] = ckv
        rk = lax.rsqrt(jnp.mean(ckv * ckv, axis=-1, keepdims=True) + EPS)
        ckvn = (ckv * rk * kvg_ref[...]).astype(BF16)
        kva = _dot(ckvn, wkv_ref[...])
        kn_ref[...] = kva[:, :512].astype(BF16)
        vv_ref[...] = kva[:, 512:].astype(BF16)

    outs = [
        (D_MODEL, BF16), (512, BF16), (512, BF16), (512, BF16), (512, F32), (Q_RANK, F32), (KV_RANK, F32),
        (512, F32), (D_MODEL, F32), (D_MODEL, F32), (128, BF16), (512, BF16), (256, BF16), (512, BF16), (512, BF16),
    ]
    return pl.pallas_call(
        body, name="inproj", grid=(s_len // tm,),
        out_shape=tuple(_sds((s_len, n), dt) for n, dt in outs),
        in_specs=[_rows(tm, D_MODEL), _whole((1, D_MODEL)), _whole((1, D_MODEL)), _whole((1, D_MODEL)),
                  _whole((W_INT, D_MODEL)), _whole((Q_RANK, 1024)), _whole((KV_RANK, 1024)),
                  _whole((1, Q_RANK)), _whole((1, KV_RANK)), _rows(tm, 256), _rows(tm, 256)],
        out_specs=tuple(_rows(tm, n) for n, _ in outs),
        compiler_params=_params(("parallel",)),
    )(x, shift, scale, g1, w_int, w_q, w_kv, qg, kvg, cos256, sin256)


Z_CLAMP = 80.0 * LOG2E
RUN_CUTOFF = 110.0 * LOG2E


def _softplus_clamped(z):
    zc = jnp.minimum(z * LOG2E, Z_CLAMP)
    return zc, jnp.log2(1.0 + jnp.exp2(zc))


def _tri_sum(a, tri):
    return _dot(a.astype(BF16), tri)


def _sb_fwd_call(q, k, v):
    s_len = q.shape[0]
    tk = min(ATT_TILE, s_len)
    tq = min(SB_Q_TILES * ATT_TILE, s_len)
    r = tq // tk
    nq = s_len // tq

    def body(q_ref, k_ref, v_ref, o_ref, lt_ref, first_ref):
        i = pl.program_id(1)
        q2 = q_ref[...]
        lane = lax.broadcasted_iota(jnp.int32, (1, 256), 1)
        krow = lax.broadcasted_iota(jnp.int32, (tk, tk), 0)
        kcol = lax.broadcasted_iota(jnp.int32, (tk, tk), 1)
        row = lax.broadcasted_iota(jnp.int32, (tq, tk), 0)
        col = lax.broadcasted_iota(jnp.int32, (tq, tk), 1)
        later = (krow > kcol).astype(BF16)
        valids = [col + u * tk < row for u in range(r)]
        hms = [(lane // 64) == hh for hh in range(4)]
        qms = [jnp.where(hm, q2, jnp.zeros_like(q2)) for hm in hms]

        def block(j, carry, valid):
            runs, acc = list(carry[:4]), carry[4]
            off = pl.multiple_of(j * tk, tk)
            kb = k_ref[pl.ds(off, tk), :]
            vb = v_ref[pl.ds(off, tk), :]
            ws = []
            for hh in range(4):
                zc, sp = _softplus_clamped(_dot_nt(qms[hh], kb))
                lm = jnp.where(valid, sp, 0.0) if valid is not None else sp
                suf = _tri_sum(lm, later)
                w = jnp.exp2(zc - sp - suf - runs[hh])
                if valid is not None:
                    w = jnp.where(valid, w, 0.0)
                ws.append(w.astype(BF16))
                runs[hh] = runs[hh] + jnp.sum(lm, axis=1, keepdims=True)
            vstack = jnp.concatenate([jnp.where(hm, vb, jnp.zeros_like(vb)) for hm in hms], axis=0)
            acc = acc + _dot(jnp.concatenate(ws, axis=1), vstack)
            return (*runs, acc)

        zero = jnp.zeros((tq, 1), F32)
        carry = (zero, zero, zero, zero, jnp.zeros((tq, 256), F32))
        for u in reversed(range(r)):
            carry = block(i * r + u, carry, valids[u])

        def least_run(runs):
            return jnp.min(jnp.minimum(jnp.minimum(runs[0], runs[1]), jnp.minimum(runs[2], runs[3])))

        n_full = i * r

        def unfinished(state):
            return jnp.logical_and(state[0] < n_full, state[1] <= RUN_CUTOFF)

        def visit(state):
            cr = block(n_full - 1 - state[0], state[2:], None)
            return (state[0] + 1, least_run(cr[:4]), *cr)

        state = lax.while_loop(unfinished, visit, (jnp.int32(0), least_run(carry[:4]), *carry))
        carry = state[2:]
        first_ref[...] = jnp.full(first_ref.shape, n_full - state[0], jnp.int32)
        for hh in range(4):
            lt_ref[0, :, hh:hh + 1] = carry[hh]
        o_ref[...] = carry[4]

    return pl.pallas_call(
        body, name="sb_fwd", grid=(2, nq),
        out_shape=(_sds((s_len, SB_WIDTH), F32), _sds((2, s_len, 4), F32), _sds((2, nq, 8, 128), jnp.int32)),
        in_specs=[pl.BlockSpec((tq, 256), lambda g, i: (i, g)),
                  pl.BlockSpec((s_len, 256), lambda g, i: (0, g)),
                  pl.BlockSpec((s_len, 256), lambda g, i: (0, g))],
        out_specs=(pl.BlockSpec((tq, 256), lambda g, i: (i, g)),
                   pl.BlockSpec((1, tq, 4), lambda g, i: (g, i, 0)),
                   pl.BlockSpec((1, 1, 8, 128), lambda g, i: (g, i, 0, 0))),
        compiler_params=_params(("parallel", "parallel")),
    )(q, k, v)


def _sb_bwd_call(first, q, k, v, do, lt):
    s_len = q.shape[0]
    tk = min(ATT_TILE, s_len)
    tq = min(SB_Q_TILES * ATT_TILE, s_len)
    r = tq // tk
    nq = s_len // tq
    nq_fwd = first.shape[0] // 2
    per_fwd = nq // nq_fwd

    def body(first_ref, q_ref, k_ref, v_ref, do_ref, lt_ref, dq_ref, dk_ref, dv_ref):
        g = pl.program_id(0)
        i = pl.program_id(1)

        @pl.when(i == 0)
        def _():
            dk_ref[...] = jnp.zeros_like(dk_ref)
            dv_ref[...] = jnp.zeros_like(dv_ref)

        q2 = q_ref[...]
        do2 = do_ref[...].astype(BF16)
        lane = lax.broadcasted_iota(jnp.int32, (1, 256), 1)
        krow = lax.broadcasted_iota(jnp.int32, (tk, tk), 0)
        kcol = lax.broadcasted_iota(jnp.int32, (tk, tk), 1)
        row = lax.broadcasted_iota(jnp.int32, (tq, tk), 0)
        col = lax.broadcasted_iota(jnp.int32, (tq, tk), 1)
        earlier = (krow < kcol).astype(BF16)
        later = (krow > kcol).astype(BF16)
        valids = [col + u * tk < row for u in range(r)]
        hms = [(lane // 64) == hh for hh in range(4)]
        qms = [jnp.where(hm, q2, jnp.zeros_like(q2)) for hm in hms]
        doms = [jnp.where(hm, do2, jnp.zeros_like(do2)) for hm in hms]
        ltots = [lt_ref[0, :, hh:hh + 1] for hh in range(4)]
        q2t = jnp.transpose(q2.astype(F32))
        do2t = jnp.transpose(do_ref[...])
        subl = lax.broadcasted_iota(jnp.int32, (256, 1), 0)
        qtstack = jnp.concatenate(
            [jnp.where((subl // 64) == hh, q2t, 0.0).astype(BF16) for hh in range(4)], axis=1)
        dotstack = jnp.concatenate(
            [jnp.where((subl // 64) == hh, do2t, 0.0).astype(BF16) for hh in range(4)], axis=1)

        def block(j, carry, valid):
            lpre, ppre, dq = list(carry[0:4]), list(carry[4:8]), carry[8]
            off = pl.multiple_of(j * tk, tk)
            kb = k_ref[pl.ds(off, tk), :]
            vb = v_ref[pl.ds(off, tk), :]
            dzs, avs = [], []
            for hh in range(4):
                zc, sp = _softplus_clamped(_dot_nt(qms[hh], kb))
                lsig = zc - sp
                lm = jnp.where(valid, sp, 0.0) if valid is not None else sp
                rowsum = jnp.sum(lm, axis=1, keepdims=True)
                between = _tri_sum(lm, later) + ((ltots[hh] - lpre[hh]) - rowsum)
                a = jnp.exp2(lsig - between)
                if valid is not None:
                    a = jnp.where(valid, a, 0.0)
                p = a * _dot_nt(doms[hh], vb)
                pbefore = ppre[hh] + _tri_sum(p, earlier)
                dz = p - jnp.exp2(lsig) * (p + pbefore)
                if valid is not None:
                    dz = jnp.where(valid, dz, 0.0)
                dzs.append(dz.astype(BF16))
                avs.append(a.astype(BF16))
                lpre[hh] = lpre[hh] + rowsum
                ppre[hh] = ppre[hh] + jnp.sum(p, axis=1, keepdims=True)
            kstack = jnp.concatenate([jnp.where(hm, kb, jnp.zeros_like(kb)) for hm in hms], axis=0)
            dq = dq + _dot(jnp.concatenate(dzs, axis=1), kstack)
            dk_ref[:, pl.ds(off, tk)] += _dot(qtstack, jnp.concatenate(dzs, axis=0))
            dv_ref[:, pl.ds(off, tk)] += _dot(dotstack, jnp.concatenate(avs, axis=0))
            return (*lpre, *ppre, dq)

        zero = jnp.zeros((tq, 1), F32)
        start = jnp.minimum(first_ref[g * nq_fwd + i // per_fwd], i * r)
        carry = lax.fori_loop(start, i * r, lambda j, cr: block(j, cr, None),
                              (zero,) * 8 + (jnp.zeros((tq, 256), F32),))
        for u in range(r):
            carry = block(i * r + u, carry, valids[u])
        dq_ref[...] = carry[8].astype(BF16)

    return pl.pallas_call(
        body, name="sb_bwd",
        out_shape=(_sds((s_len, SB_WIDTH), BF16), _sds((SB_WIDTH, s_len), F32), _sds((SB_WIDTH, s_len), F32)),
        grid_spec=pltpu.PrefetchScalarGridSpec(
            num_scalar_prefetch=1, grid=(2, nq),
            in_specs=[pl.BlockSpec((tq, 256), lambda g, i, f: (i, g)),
                      pl.BlockSpec((s_len, 256), lambda g, i, f: (0, g)),
                      pl.BlockSpec((s_len, 256), lambda g, i, f: (0, g)),
                      pl.BlockSpec((tq, 256), lambda g, i, f: (i, g)),
                      pl.BlockSpec((1, tq, 4), lambda g, i, f: (g, i, 0))],
            out_specs=(pl.BlockSpec((tq, 256), lambda g, i, f: (i, g)),
                       pl.BlockSpec((256, s_len), lambda g, i, f: (g, 0)),
                       pl.BlockSpec((256, s_len), lambda g, i, f: (g, 0)))),
        compiler_params=_params(("parallel", "arbitrary")),
    )(first, q, k, v, do, lt)


def _mla_fwd_call(qn, qp, kn, kpt, v):
    s_len = qn.shape[0]
    tk = min(MLA_KEY_TILE, s_len)
    tq = min(FWD_Q_TILES * ATT_TILE, s_len)
    r = tq // tk
    nq = s_len // tq

    def body(qn_ref, qp_ref, kn_ref, kpt_ref, v_ref, o_ref, lse_ref):
        i = pl.program_id(1)
        qn2 = qn_ref[...]
        qp2 = qp_ref[...]
        lane256 = lax.broadcasted_iota(jnp.int32, (1, 256), 1)
        lane128 = lax.broadcasted_iota(jnp.int32, (1, 128), 1)
        krow = lax.broadcasted_iota(jnp.int32, (tk, tk), 0)
        kcol = lax.broadcasted_iota(jnp.int32, (tk, tk), 1)
        row = lax.broadcasted_iota(jnp.int32, (tq, tk), 0)
        col = lax.broadcasted_iota(jnp.int32, (tq, tk), 1)
        valids = [col + u * tk <= row for u in range(r)]
        m64s = [(lane256 // 64) == hh for hh in range(4)]
        half = [(lane128 // 64) == u for u in range(2)]
        m32s = [(lane128 // 32) == hh for hh in range(4)]
        qcs = []
        for hh in range(4):
            qpair = qn2[:, 128 * (hh // 2):128 * (hh // 2) + 128]
            qcs.append(jnp.concatenate([jnp.where(half[hh % 2], qpair, jnp.zeros_like(qpair)),
                                        jnp.where(m32s[hh], qp2, jnp.zeros_like(qp2))], axis=1))

        def by_head(vals):
            return jnp.where(m64s[0], vals[0], jnp.where(m64s[1], vals[1], jnp.where(m64s[2], vals[2], vals[3])))

        def block(j, carry, valid):
            ms, ls, acc = list(carry[0:4]), list(carry[4:8]), carry[8]
            off = pl.multiple_of(j * tk, tk)
            knb = kn_ref[pl.ds(off, tk), :]
            kpb = kpt_ref[pl.ds(off, tk), :]
            vb = v_ref[pl.ds(off, tk), :]
            kcs = [jnp.concatenate([knb[:, 128 * pp:128 * pp + 128], kpb], axis=1) for pp in range(2)]
            ps, alphas = [], []
            for hh in range(4):
                s = _dot_nt(qcs[hh], kcs[hh // 2]) * (MLA_SCALE * LOG2E)
                if valid is not None:
                    s = jnp.where(valid, s, -1e30)
                mn = jnp.maximum(ms[hh], jnp.max(s, axis=1, keepdims=True))
                p = jnp.exp2(s - mn)
                alpha = jnp.exp2(ms[hh] - mn)
                ls[hh] = alpha * ls[hh] + jnp.sum(p, axis=1, keepdims=True)
                ms[hh] = mn
                ps.append(p.astype(BF16))
                alphas.append(alpha)
            pvs = []
            for pp in range(2):
                vpair = vb[:, 128 * pp:128 * pp + 128]
                vstack = jnp.concatenate([jnp.where(hf, vpair, jnp.zeros_like(vpair)) for hf in half], axis=0)
                pvs.append(_dot(jnp.concatenate(ps[2 * pp:2 * pp + 2], axis=1), vstack))
            acc = by_head(alphas) * acc + jnp.concatenate(pvs, axis=1)
            return (*ms, *ls, acc)

        neg = jnp.full((tq, 1), -1e30, F32)
        zero = jnp.zeros((tq, 1), F32)
        carry = lax.fori_loop(0, i * r, lambda j, cr: block(j, cr, None),
                              (neg,) * 4 + (zero,) * 4 + (jnp.zeros((tq, 256), F32),))
        for u in range(r):
            carry = block(i * r + u, carry, valids[u])
        o_ref[...] = carry[8] / by_head(list(carry[4:8]))
        for hh in range(4):
            lse_ref[0, :, hh:hh + 1] = (carry[hh] + jnp.log2(carry[4 + hh])) * (1.0 / LOG2E)

    return pl.pallas_call(
        body, name="mla_fwd", grid=(2, nq),
        out_shape=(_sds((s_len, MLA_WIDTH), F32), _sds((2, s_len, 4), F32)),
        in_specs=[pl.BlockSpec((tq, 256), lambda g, i: (i, g)),
                  pl.BlockSpec((tq, 128), lambda g, i: (i, g)),
                  pl.BlockSpec((s_len, 256), lambda g, i: (0, g)),
                  pl.BlockSpec((s_len, 128), lambda g, i: (0, 0)),
                  pl.BlockSpec((s_len, 256), lambda g, i: (0, g))],
        out_specs=(pl.BlockSpec((tq, 256), lambda g, i: (i, g)),
                   pl.BlockSpec((1, tq, 4), lambda g, i: (g, i, 0))),
        compiler_params=_params(("parallel", "parallel")),
    )(qn, qp, kn, kpt, v)


def _mla_bwd_call(qn, qp, kn, kpt, v, o, do, lse):
    s_len = qn.shape[0]
    tk = min(MLA_KEY_TILE, s_len)
    tq = min(ATT_Q_TILES * ATT_TILE, s_len)
    r = tq // tk
    nq = s_len // tq

    def body(qn_ref, qp_ref, kn_ref, kpt_ref, v_ref, o_ref, do_ref, lse_ref,
             dqn_ref, dqp_ref, dkn_ref, dkpt_ref, dv_ref):
        g = pl.program_id(0)
        i = pl.program_id(1)

        @pl.when(i == 0)
        def _():
            dkn_ref[...] = jnp.zeros_like(dkn_ref)
            dv_ref[...] = jnp.zeros_like(dv_ref)

        @pl.when((i == 0) & (g == 0))
        def _():
            dkpt_ref[...] = jnp.zeros_like(dkpt_ref)

        qn2 = qn_ref[...]
        qp2 = qp_ref[...]
        dof = do_ref[...]
        dob = dof.astype(BF16)
        prod = dof * o_ref[...]
        lane256 = lax.broadcasted_iota(jnp.int32, (1, 256), 1)
        lane128 = lax.broadcasted_iota(jnp.int32, (1, 128), 1)
        krow = lax.broadcasted_iota(jnp.int32, (tk, tk), 0)
        kcol = lax.broadcasted_iota(jnp.int32, (tk, tk), 1)
        row = lax.broadcasted_iota(jnp.int32, (tq, tk), 0)
        col = lax.broadcasted_iota(jnp.int32, (tq, tk), 1)
        valids = [col + u * tk <= row for u in range(r)]
        m64s = [(lane256 // 64) == hh for hh in range(4)]
        half = [(lane128 // 64) == u for u in range(2)]
        m32s = [(lane128 // 32) == hh for hh in range(4)]
        qcs, doms = [], []
        for hh in range(4):
            sl = slice(128 * (hh // 2), 128 * (hh // 2) + 128)
            qpair = qn2[:, sl]
            dpair = dob[:, sl]
            qcs.append(jnp.concatenate([jnp.where(half[hh % 2], qpair, jnp.zeros_like(qpair)),
                                        jnp.where(m32s[hh], qp2, jnp.zeros_like(qp2))], axis=1))
            doms.append(jnp.where(half[hh % 2], dpair, jnp.zeros_like(dpair)))
        dsums = [jnp.sum(jnp.where(m64, prod, 0.0), axis=1, keepdims=True) * MLA_SCALE for m64 in m64s]
        lses = [lse_ref[0, :, hh:hh + 1] * LOG2E for hh in range(4)]
        qn2t = jnp.transpose(qn2.astype(F32))
        qp2t = jnp.transpose(qp2.astype(F32))
        do2t = jnp.transpose(dof)
        sub128 = lax.broadcasted_iota(jnp.int32, (128, 1), 0)
        qtstacks, dotstacks = [], []
        for pp in range(2):
            qts, dts = [], []
            for u in range(2):
                hh = 2 * pp + u
                qts.append(jnp.concatenate(
                    [jnp.where((sub128 // 64) == u, qn2t[128 * pp:128 * pp + 128, :], 0.0),
                     jnp.where((sub128 // 32) == hh, qp2t, 0.0)], axis=0).astype(BF16))
                dts.append(jnp.where((sub128 // 64) == u, do2t[128 * pp:128 * pp + 128, :], 0.0).astype(BF16))
            qtstacks.append(jnp.concatenate(qts, axis=1))
            dotstacks.append(jnp.concatenate(dts, axis=1))

        def block(j, carry, valid):
            dqn, dqp = carry
            off = pl.multiple_of(j * tk, tk)
            knb = kn_ref[pl.ds(off, tk), :]
            kpb = kpt_ref[pl.ds(off, tk), :]
            vb = v_ref[pl.ds(off, tk), :]
            dqn_parts = []
            dkp = None
            for pp in range(2):
                sl = slice(128 * pp, 128 * pp + 128)
                knp = knb[:, sl]
                vpair = vb[:, sl]
                kc = jnp.concatenate([knp, kpb], axis=1)
                dss, pbs, kcms = [], [], []
                for u in range(2):
                    hh = 2 * pp + u
                    s = _dot_nt(qcs[hh], kc) * (MLA_SCALE * LOG2E)
                    if valid is not None:
                        s = jnp.where(valid, s, -1e30)
                    p = jnp.exp2(s - lses[hh])
                    ds = p * (_dot_nt(doms[hh], vpair) * MLA_SCALE - dsums[hh])
                    dss.append(ds.astype(BF16))
                    pbs.append(p.astype(BF16))
                    kcms.append(jnp.concatenate([jnp.where(half[u], knp, jnp.zeros_like(knp)),
                                                 jnp.where(m32s[hh], kpb, jnp.zeros_like(kpb))], axis=1))
                dqc = _dot(jnp.concatenate(dss, axis=1), jnp.concatenate(kcms, axis=0))
                dqn_parts.append(dqc[:, :128])
                dqp = dqp + dqc[:, 128:]
                dkc = _dot(qtstacks[pp], jnp.concatenate(dss, axis=0))
                dkn_ref[128 * pp:128 * pp + 128, pl.ds(off, tk)] += dkc[:128, :]
                dkp = dkc[128:, :] if dkp is None else dkp + dkc[128:, :]
                dv_ref[128 * pp:128 * pp + 128, pl.ds(off, tk)] += _dot(dotstacks[pp], jnp.concatenate(pbs, axis=0))
            dqn = dqn + jnp.concatenate(dqn_parts, axis=1)
            dkpt_ref[:, pl.ds(off, tk)] += dkp
            return dqn, dqp

        carry = lax.fori_loop(0, i * r, lambda j, cr: block(j, cr, None),
                              (jnp.zeros((tq, 256), F32), jnp.zeros((tq, 128), F32)))
        for u in range(r):
            carry = block(i * r + u, carry, valids[u])
        dqn, dqp = carry
        dqn_ref[...] = dqn.astype(BF16)
        dqp_ref[...] = dqp.astype(BF16)

    return pl.pallas_call(
        body, name="mla_bwd", grid=(2, nq),
        out_shape=(_sds((s_len, 512), BF16), _sds((s_len, 256), BF16), _sds((512, s_len), F32),
                   _sds((128, s_len), F32), _sds((512, s_len), F32)),
        in_specs=[pl.BlockSpec((tq, 256), lambda g, i: (i, g)),
                  pl.BlockSpec((tq, 128), lambda g, i: (i, g)),
                  pl.BlockSpec((s_len, 256), lambda g, i: (0, g)),
                  pl.BlockSpec((s_len, 128), lambda g, i: (0, 0)),
                  pl.BlockSpec((s_len, 256), lambda g, i: (0, g)),
                  pl.BlockSpec((tq, 256), lambda g, i: (i, g)),
                  pl.BlockSpec((tq, 256), lambda g, i: (i, g)),
                  pl.BlockSpec((1, tq, 4), lambda g, i: (g, i, 0))],
        out_specs=(pl.BlockSpec((tq, 256), lambda g, i: (i, g)),
                   pl.BlockSpec((tq, 128), lambda g, i: (i, g)),
                   pl.BlockSpec((256, s_len), lambda g, i: (g, 0)),
                   pl.BlockSpec((128, s_len), lambda g, i: (0, 0)),
                   pl.BlockSpec((256, s_len), lambda g, i: (g, 0))),
        compiler_params=_params(("arbitrary", "arbitrary")),
    )(qn, qp, kn, kpt, v, o, do, lse)


def _post_call(x, tgt, oa, ob, sz, mz, ga, gb, gate, gf, wa, wb, wo, wat, wbt, wot):
    s_len = x.shape[0]
    tm = min(ROW_TILE, s_len)

    def body(x_ref, t_ref, oa_ref, ob_ref, sz_ref, mz_ref, ga_ref, gb_ref, gate_ref, gf_ref,
             wa_ref, wb_ref, wo_ref, wat_ref, wbt_ref, wot_ref,
             dx2_ref, doa_ref, dob_ref, dsz_ref, dmz_ref, dga_ref, dgb_ref,
             dwo_ref, dwa_ref, dwb_ref, dgf_ref, dgate_ref, loss_ref):
        @pl.when(pl.program_id(0) == 0)
        def _():
            dwo_ref[...] = jnp.zeros_like(dwo_ref)
            dwa_ref[...] = jnp.zeros_like(dwa_ref)
            dwb_ref[...] = jnp.zeros_like(dwb_ref)
            dgf_ref[...] = jnp.zeros_like(dgf_ref)
            dgate_ref[...] = jnp.zeros_like(dgate_ref)
            loss_ref[...] = jnp.zeros_like(loss_ref)

        gate = gate_ref[...]
        gf = gf_ref[...]
        oa = oa_ref[...]
        ob = ob_ref[...]
        sz = sz_ref[...]
        mz = mz_ref[...]
        sa = _sigmoid(sz)
        sb = _sigmoid(mz)
        silu_a = sz * sa
        silu_b = mz * sb
        ua = (oa * silu_a).astype(BF16)
        ub = (ob * silu_b).astype(BF16)
        ya = _dot(ua, wa_ref[...])
        yb = _dot(ub, wb_ref[...])
        sga = _sigmoid(ga_ref[...])
        sgb = _sigmoid(gb_ref[...])
        merged = (sga * ya + sgb * yb).astype(BF16)
        out = _dot(merged, wo_ref[...])
        x2 = x_ref[...] + gate * out
        r2 = lax.rsqrt(jnp.mean(x2 * x2, axis=-1, keepdims=True) + EPS)
        xhat = x2 * r2
        err = xhat * gf - t_ref[...]
        loss_ref[...] += 0.5 * jnp.sum(jnp.sum(err * err, axis=1, keepdims=True), axis=0, keepdims=True) / D_MODEL
        dy = err * (1.0 / D_MODEL)
        dgf_ref[...] += jnp.sum(dy * xhat, axis=0, keepdims=True)
        dxhat = dy * gf
        dx2 = r2 * (dxhat - xhat * jnp.mean(dxhat * xhat, axis=-1, keepdims=True))
        dx2_ref[...] = dx2
        dgate_ref[...] += jnp.sum(dx2 * out, axis=0, keepdims=True)
        dout = (dx2 * gate).astype(BF16)
        dmerged = _dot(dout, wot_ref[...])
        dwo_ref[...] += _dot_tn(merged, dout)
        dya = dmerged * sga
        dyb = dmerged * sgb
        dga_ref[...] = (dya * ya * (1.0 - sga)).astype(BF16)
        dgb_ref[...] = (dyb * yb * (1.0 - sgb)).astype(BF16)
        dyab = dya.astype(BF16)
        dybb = dyb.astype(BF16)
        dua = _dot(dyab, wat_ref[...])
        dub = _dot(dybb, wbt_ref[...])
        dwa_ref[...] += _dot_tn(ua, dyab)
        dwb_ref[...] += _dot_tn(ub, dybb)
        doa_ref[...] = dua * silu_a
        dob_ref[...] = dub * silu_b
        dsz_ref[...] = (dua * oa * (sa * (1.0 + sz * (1.0 - sa)))).astype(BF16)
        dmz_ref[...] = (dub * ob * (sb * (1.0 + mz * (1.0 - sb)))).astype(BF16)

    return pl.pallas_call(
        body, name="post", grid=(s_len // tm,),
        out_shape=(_sds((s_len, D_MODEL), F32), _sds((s_len, 512), F32), _sds((s_len, 512), F32),
                   _sds((s_len, 512), BF16), _sds((s_len, 512), BF16),
                   _sds((s_len, D_MODEL), BF16), _sds((s_len, D_MODEL), BF16),
                   _sds((D_MODEL, D_MODEL), F32), _sds((512, D_MODEL), F32), _sds((512, D_MODEL), F32),
                   _sds((1, D_MODEL), F32), _sds((1, D_MODEL), F32), _sds((1, 128), F32)),
        in_specs=[_rows(tm, D_MODEL), _rows(tm, D_MODEL), _rows(tm, 512), _rows(tm, 512), _rows(tm, 512),
                  _rows(tm, 512), _rows(tm, D_MODEL), _rows(tm, D_MODEL), _whole((1, D_MODEL)), _whole((1, D_MODEL)),
                  _whole((512, D_MODEL)), _whole((512, D_MODEL)), _whole((D_MODEL, D_MODEL)),
                  _whole((D_MODEL, 512)), _whole((D_MODEL, 512)), _whole((D_MODEL, D_MODEL))],
        out_specs=(_rows(tm, D_MODEL), _rows(tm, 512), _rows(tm, 512), _rows(tm, 512), _rows(tm, 512),
                   _rows(tm, D_MODEL), _rows(tm, D_MODEL),
                   _whole((D_MODEL, D_MODEL)), _whole((512, D_MODEL)), _whole((512, D_MODEL)),
                   _whole((1, D_MODEL)), _whole((1, D_MODEL)), _whole((1, 128))),
        compiler_params=_params(("arbitrary",)),
    )(x, tgt, oa, ob, sz, mz, ga, gb, gate, gf, wa, wb, wo, wat, wbt, wot)


def _bwdprep_call(dsq, dsk, dsv, dsz, dqn, dqp, dkn, dvv, dkpt, dmz, dga, dgb, cq, ckv, cos256, sin256,
                  qg, kvg, wqt, wkvt):
    s_len = cq.shape[0]
    tm = min(ROW_TILE, s_len)

    def body(dsq_ref, dsk_ref, dsv_ref, dsz_ref, dqn_ref, dqp_ref, dkn_ref, dvv_ref, dkpt_ref, dmz_ref,
             dga_ref, dgb_ref, cq_ref, ckv_ref, cos_ref, sin_ref, qg_ref, kvg_ref, wqt_ref, wkvt_ref,
             dp_ref, dwq_ref, dwkv_ref, dqg_ref, dkvg_ref):
        @pl.when(pl.program_id(0) == 0)
        def _():
            dwq_ref[...] = jnp.zeros_like(dwq_ref)
            dwkv_ref[...] = jnp.zeros_like(dwkv_ref)
            dqg_ref[...] = jnp.zeros_like(dqg_ref)
            dkvg_ref[...] = jnp.zeros_like(dkvg_ref)

        cos = cos_ref[...]
        sin = sin_ref[...]
        dp_ref[:, O_SQ:O_SK] = dsq_ref[...] * jnp.asarray(SB_SCALE, BF16)
        dp_ref[:, O_SK:O_SV] = jnp.transpose(dsk_ref[...]).astype(BF16)
        dp_ref[:, O_SV:O_SZ] = jnp.transpose(dsv_ref[...]).astype(BF16)
        dp_ref[:, O_SZ:O_CQ] = dsz_ref[...]
        dp_ref[:, O_MZ:O_GA] = dmz_ref[...]
        dp_ref[:, O_GA:O_GB] = dga_ref[...]
        dp_ref[:, O_GB:O_KR] = dgb_ref[...]
        dkp = jnp.transpose(dkpt_ref[...])
        dp_ref[:, O_KR:O_KR + 128] = (dkp * cos[:, :128]).astype(BF16)
        dp_ref[:, O_KR + 128:O_END] = (dkp * sin[:, :128]).astype(BF16)
        dp_ref[:, O_END:W_INT] = jnp.zeros((tm, W_INT - O_END), BF16)

        cq = cq_ref[...]
        rq = lax.rsqrt(jnp.mean(cq * cq, axis=-1, keepdims=True) + EPS)
        cqh = cq * rq
        qg = qg_ref[...]
        cqn = (cqh * qg).astype(BF16)
        dqp = dqp_ref[...].astype(F32)
        dqa = jnp.concatenate([dqn_ref[...], (dqp * cos).astype(BF16), (dqp * sin).astype(BF16)], axis=1)
        dcqn = _dot(dqa, wqt_ref[...])
        dwq_ref[...] += _dot_tn(cqn, dqa)
        dqg_ref[...] += jnp.sum(dcqn * cqh, axis=0, keepdims=True)
        dh = dcqn * qg
        dcq = rq * (dh - cqh * jnp.mean(dh * cqh, axis=-1, keepdims=True))
        dp_ref[:, O_CQ:O_CKV] = dcq.astype(BF16)

        ckv = ckv_ref[...]
        rk = lax.rsqrt(jnp.mean(ckv * ckv, axis=-1, keepdims=True) + EPS)
        ckh = ckv * rk
        kvg = kvg_ref[...]
        ckvn = (ckh * kvg).astype(BF16)
        dkva = jnp.concatenate([jnp.transpose(dkn_ref[...]).astype(BF16),
                                jnp.transpose(dvv_ref[...]).astype(BF16)], axis=1)
        dckvn = _dot(dkva, wkvt_ref[...])
        dwkv_ref[...] += _dot_tn(ckvn, dkva)
        dkvg_ref[...] += jnp.sum(dckvn * ckh, axis=0, keepdims=True)
        dh2 = dckvn * kvg
        dckv = rk * (dh2 - ckh * jnp.mean(dh2 * ckh, axis=-1, keepdims=True))
        dp_ref[:, O_CKV:O_MZ] = dckv.astype(BF16)

    return pl.pallas_call(
        body, name="bwdprep", grid=(s_len // tm,),
        out_shape=(_sds((s_len, W_INT), BF16), _sds((Q_RANK, 1024), F32), _sds((KV_RANK, 1024), F32),
                   _sds((1, Q_RANK), F32), _sds((1, KV_RANK), F32)),
        in_specs=[_rows(tm, 512), _cols(512, tm), _cols(512, tm), _rows(tm, 512), _rows(tm, 512), _rows(tm, 256),
                  _cols(512, tm), _cols(512, tm), _cols(128, tm), _rows(tm, 512), _rows(tm, D_MODEL),
                  _rows(tm, D_MODEL), _rows(tm, Q_RANK), _rows(tm, KV_RANK), _rows(tm, 256), _rows(tm, 256),
                  _whole((1, Q_RANK)), _whole((1, KV_RANK)), _whole((1024, Q_RANK)), _whole((1024, KV_RANK))],
        out_specs=(_rows(tm, W_INT), _whole((Q_RANK, 1024)), _whole((KV_RANK, 1024)),
                   _whole((1, Q_RANK)), _whole((1, KV_RANK))),
        compiler_params=_params(("arbitrary",)),
    )(dsq, dsk, dsv, dsz, dqn, dqp, dkn, dvv, dkpt, dmz, dga, dgb, cq, ckv, cos256, sin256, qg, kvg, wqt, wkvt)


def _dh_call(dproj, w_int_t, x, dx2, scale, g1):
    s_len = x.shape[0]
    tm = min(2 * ROW_TILE, s_len)

    def body(dp_ref, wt_ref, x_ref, dx2_ref, sc_ref, g1_ref, gx_ref, dsh_ref, dsc_ref, dg1_ref):
        @pl.when(pl.program_id(0) == 0)
        def _():
            dsh_ref[...] = jnp.zeros_like(dsh_ref)
            dsc_ref[...] = jnp.zeros_like(dsc_ref)
            dg1_ref[...] = jnp.zeros_like(dg1_ref)

        dh = _dot(dp_ref[...], wt_ref[...])
        xt = x_ref[...]
        r = lax.rsqrt(jnp.mean(xt * xt, axis=-1, keepdims=True) + EPS)
        xh = xt * r
        g1 = g1_ref[...]
        xg = xh * g1
        dsh_ref[...] += jnp.sum(dh, axis=0, keepdims=True)
        dsc_ref[...] += jnp.sum(dh * xg, axis=0, keepdims=True)
        dxg = dh * (1.0 + sc_ref[...])
        dg1_ref[...] += jnp.sum(dxg * xh, axis=0, keepdims=True)
        dxh = dxg * g1
        gx_ref[...] = dx2_ref[...] + r * (dxh - xh * jnp.mean(dxh * xh, axis=-1, keepdims=True))

    return pl.pallas_call(
        body, name="dh", grid=(s_len // tm,),
        out_shape=(_sds((s_len, D_MODEL), F32), _sds((1, D_MODEL), F32), _sds((1, D_MODEL), F32),
                   _sds((1, D_MODEL), F32)),
        in_specs=[_rows(tm, W_INT), _whole((W_INT, D_MODEL)), _rows(tm, D_MODEL), _rows(tm, D_MODEL),
                  _whole((1, D_MODEL)), _whole((1, D_MODEL))],
        out_specs=(_rows(tm, D_MODEL), _whole((1, D_MODEL)), _whole((1, D_MODEL)), _whole((1, D_MODEL))),
        compiler_params=_params(("arbitrary",)),
    )(dproj, w_int_t, x, dx2, scale, g1)


def _small_call(svg, ct, dmod_sh):
    def body(sv_ref, ct_ref, dm_ref, tot_ref, gwada_ref):
        acc = sv_ref[0:1, :]
        for d in range(1, N_DEV):
            acc = acc + sv_ref[d:d + 1, :]
        tot_ref[...] = acc
        gwada_ref[...] = lax.dot_general(ct_ref[...], dm_ref[...], (((1,), (0,)), ((), ())),
                                         precision=lax.Precision.HIGHEST, preferred_element_type=F32)

    vmem = pl.BlockSpec(memory_space=pltpu.VMEM)
    return pl.pallas_call(
        body, name="small_grads",
        out_shape=(_sds((1, 8 * SV_COLS), F32), _sds((D_MODEL, 768), F32)),
        in_specs=[vmem, vmem, vmem], out_specs=(vmem, vmem),
        compiler_params=_params(),
    )(svg, ct, dmod_sh)


def _adamw_tile_rows(rows, cols):
    budget = 2 << 20
    if rows * cols * 4 <= budget or rows % 8:
        return rows
    best = 8
    for tr in range(8, rows + 1, 8):
        if rows % tr == 0 and tr * cols * 4 <= budget:
            best = tr
    return best


def _adamw_call(name, w, g, m, v):
    rows, cols = w.shape
    tr = _adamw_tile_rows(rows, cols)

    def body(w_ref, g_ref, m_ref, v_ref, d_ref, nm_ref, nv_ref):
        gg = g_ref[...]
        m2 = ADAM_B1 * m_ref[...] + (1.0 - ADAM_B1) * gg
        v2 = ADAM_B2 * v_ref[...] + (1.0 - ADAM_B2) * (gg * gg)
        m_hat = m2 / (1.0 - ADAM_B1 ** ADAM_STEP)
        v_hat = v2 / (1.0 - ADAM_B2 ** ADAM_STEP)
        d_ref[...] = -ADAM_LR * (m_hat / (jnp.sqrt(v_hat) + ADAM_EPS) + ADAM_WD * w_ref[...])
        nm_ref[...] = m2
        nv_ref[...] = v2

    spec = pl.BlockSpec((tr, cols), lambda i: (i, 0))
    return pl.pallas_call(
        body, name="adamw_" + name, grid=(rows // tr,),
        out_shape=(_sds((rows, cols), F32),) * 3,
        in_specs=[spec] * 4, out_specs=(spec,) * 3,
        compiler_params=_params(("parallel",)),
    )(w, g, m, v)


IN_SHARD = IN_WIDTH // N_CHIPS
HALF_D = D_MODEL // 2
SMALL_ROWS = (576, 512, 1024, 1024, 2048)
SMALL_TOTAL = sum(SMALL_ROWS)
SMALL_HALF = SMALL_TOTAL // 2
SMALL_SUM_ROWS = 432


def _gather_call(c_row, w_ada_sh, pack_in, pack_small):
    def body(c_ref, wada_ref, pki_ref, pks_ref, mg_ref, cg_ref, gwi_ref, gws_ref,
             cv, ssem_c, rsem_c, ssem_m, rsem_m, ssem_w, rsem_w, ssem_f, rsem_f, lsem):
        x, y, c = lax.axis_index("x"), lax.axis_index("y"), lax.axis_index("c")
        me = 4 * x + 2 * y + c
        chip = 2 * x + y
        rel3 = [(1, 0), (0, 1), (1, 1)]
        packs = [(pki_ref, gwi_ref), (pks_ref, gws_ref)]

        sends = []
        for j, (dx, dy) in enumerate(rel3):
            for a, (pk, gw) in enumerate(packs):
                cp = pltpu.make_async_remote_copy(
                    src_ref=pk.at[c], dst_ref=gw.at[chip, c], send_sem=ssem_w.at[j, a], recv_sem=rsem_w.at[j, a],
                    device_id=(_flip(x, dx), _flip(y, dy), c), device_id_type=MESH)
                cp.start()
                sends.append(cp)
        owns = []
        for a, (pk, gw) in enumerate(packs):
            own = pltpu.make_async_copy(pk, gw.at[chip], lsem.at[a])
            own.start()
            owns.append(own)

        cv[me] = c_ref[...]
        for r in range(1, N_DEV):
            dx, dy, dc = (r >> 2) & 1, (r >> 1) & 1, r & 1
            cp = pltpu.make_async_remote_copy(
                src_ref=c_ref, dst_ref=cv.at[me], send_sem=ssem_c.at[r - 1], recv_sem=rsem_c.at[r - 1],
                device_id=(_flip(x, dx), _flip(y, dy), _flip(c, dc)), device_id_type=MESH)
            cp.start()
            sends.append(cp)
        for r in range(1, N_DEV):
            dx, dy, dc = (r >> 2) & 1, (r >> 1) & 1, r & 1
            src = 4 * _flip(x, dx) + 2 * _flip(y, dy) + _flip(c, dc)
            pltpu.make_async_remote_copy(
                src_ref=c_ref, dst_ref=cv.at[src], send_sem=ssem_c.at[r - 1], recv_sem=rsem_c.at[r - 1],
                device_id=(x, y, c), device_id_type=MESH).wait_recv()
        rows = lax.broadcasted_iota(jnp.int32, (N_DEV, D_MODEL), 0)
        call = jnp.zeros((N_DEV, D_MODEL), F32)
        for b in range(N_DEV):
            call = jnp.where(rows == b, jnp.broadcast_to(cv[b], (N_DEV, D_MODEL)), call)
        cg_ref[...] = call

        mg_ref[chip] = lax.dot_general(call, wada_ref[...], (((1,), (0,)), ((), ())),
                                       precision=lax.Precision.HIGHEST, preferred_element_type=F32)
        for j, (dx, dy) in enumerate(rel3):
            cp = pltpu.make_async_remote_copy(
                src_ref=mg_ref.at[chip], dst_ref=mg_ref.at[chip], send_sem=ssem_m.at[j], recv_sem=rsem_m.at[j],
                device_id=(_flip(x, dx), _flip(y, dy), c), device_id_type=MESH)
            cp.start()
            sends.append(cp)
        for j, (dx, dy) in enumerate(rel3):
            src_chip = 2 * _flip(x, dx) + _flip(y, dy)
            pltpu.make_async_remote_copy(
                src_ref=mg_ref.at[src_chip], dst_ref=mg_ref.at[src_chip], send_sem=ssem_m.at[j],
                recv_sem=rsem_m.at[j], device_id=(x, y, c), device_id_type=MESH).wait_recv()
        for j, (dx, dy) in enumerate(rel3):
            src_chip = 2 * _flip(x, dx) + _flip(y, dy)
            for a, (pk, gw) in enumerate(packs):
                pltpu.make_async_remote_copy(
                    src_ref=pk.at[c], dst_ref=gw.at[src_chip, c], send_sem=ssem_w.at[j, a],
                    recv_sem=rsem_w.at[j, a], device_id=(x, y, c), device_id_type=MESH).wait_recv()
                cp = pltpu.make_async_remote_copy(
                    src_ref=gw.at[src_chip, c], dst_ref=gw.at[src_chip, c], send_sem=ssem_f.at[j, a],
                    recv_sem=rsem_f.at[j, a], device_id=(x, y, 1 - c), device_id_type=MESH)
                cp.start()
                sends.append(cp)
        for j, (dx, dy) in enumerate(rel3):
            src_chip = 2 * _flip(x, dx) + _flip(y, dy)
            for a, (pk, gw) in enumerate(packs):
                pltpu.make_async_remote_copy(
                    src_ref=pk.at[c], dst_ref=gw.at[src_chip, 1 - c], send_sem=ssem_f.at[j, a],
                    recv_sem=rsem_f.at[j, a], device_id=(x, y, c), device_id_type=MESH).wait_recv()
        for cp in sends:
            cp.wait_send()
        for own in owns:
            own.wait()

    vmem = pl.BlockSpec(memory_space=pltpu.VMEM)
    return pl.pallas_call(
        body, name="gather_fwd",
        out_shape=(_sds((N_CHIPS, N_DEV, 768), F32), _sds((N_DEV, D_MODEL), F32),
                   _sds((N_CHIPS, 2, IN_SHARD, HALF_D), BF16), _sds((N_CHIPS, 2, SMALL_HALF, LANES), BF16)),
        in_specs=[vmem, vmem, vmem, vmem], out_specs=(vmem, vmem, vmem, vmem),
        scratch_shapes=[
            pltpu.VMEM((N_DEV, 1, D_MODEL), F32),
            pltpu.SemaphoreType.DMA((N_DEV - 1,)), pltpu.SemaphoreType.DMA((N_DEV - 1,)),
            pltpu.SemaphoreType.DMA((3,)), pltpu.SemaphoreType.DMA((3,)),
            pltpu.SemaphoreType.DMA((3, 2)), pltpu.SemaphoreType.DMA((3, 2)),
            pltpu.SemaphoreType.DMA((3, 2)), pltpu.SemaphoreType.DMA((3, 2)),
            pltpu.SemaphoreType.DMA((2,)),
        ],
        compiler_params=_params(),
    )(c_row, w_ada_sh, pack_in, pack_small)


def _reduce_call(g_in, g_small, sv):
    def body(gi_ref, gs_ref, sv_ref, fi_ref, fs_ref, svg_ref, pair_i, pair_s, send_i, send_s, land_i, land_s,
             ssem_p, rsem_p, ssem_g, rsem_g, ssem_s, rsem_s, ssem_x, rsem_x):
        x, y, c = lax.axis_index("x"), lax.axis_index("y"), lax.axis_index("c")
        me = 4 * x + 2 * y + c
        chip = 2 * x + y
        rel3 = [(1, 0), (0, 1), (1, 1)]
        payloads = [(gi_ref, pair_i, send_i, land_i, fi_ref), (gs_ref, pair_s, send_s, land_s, fs_ref)]
        copies = []

        for k in range(N_CHIPS):
            for a, (g, pair, _, _, _) in enumerate(payloads):
                cp = pltpu.make_async_remote_copy(
                    src_ref=g.at[2 * k + 1 - c], dst_ref=pair.at[k], send_sem=ssem_p.at[k, a],
                    recv_sem=rsem_p.at[k, a], device_id=(x, y, 1 - c), device_id_type=MESH)
                cp.start()
                copies.append(cp)

        for r in range(1, N_DEV):
            dx, dy, dc = (r >> 2) & 1, (r >> 1) & 1, r & 1
            cp = pltpu.make_async_remote_copy(
                src_ref=sv_ref, dst_ref=svg_ref.at[me], send_sem=ssem_s.at[r - 1], recv_sem=rsem_s.at[r - 1],
                device_id=(_flip(x, dx), _flip(y, dy), _flip(c, dc)), device_id_type=MESH)
            cp.start()
            copies.append(cp)
        svg_ref[me] = sv_ref[...]

        def pair_sum(k, store_in, store_small):
            for a, (g, pair, _, _, _) in enumerate(payloads):
                pltpu.make_async_remote_copy(
                    src_ref=g.at[2 * k + c], dst_ref=pair.at[k], send_sem=ssem_p.at[k, a],
                    recv_sem=rsem_p.at[k, a], device_id=(x, y, c), device_id_type=MESH).wait_recv()
            for qd in range(HALF_D // LANES):
                sl = slice(LANES * qd, LANES * qd + LANES)
                store_in(sl, gi_ref[2 * k + c, :, sl].astype(F32) + pair_i[k, :, sl].astype(F32))

            def rows(i, carry):
                sl = pl.ds(pl.multiple_of(i * SMALL_SUM_ROWS, 16), SMALL_SUM_ROWS)
                store_small(sl, gs_ref[2 * k + c, sl, :].astype(F32) + pair_s[k, sl, :].astype(F32))
                return carry

            lax.fori_loop(0, SMALL_HALF // SMALL_SUM_ROWS, rows, 0)

        for j, (dx, dy) in enumerate(rel3):
            tx, ty = _flip(x, dx), _flip(y, dy)

            def put_in(sl, val, j=j):
                send_i[j, :, sl] = val.astype(BF16)

            def put_small(sl, val, j=j):
                send_s[j, sl, :] = val.astype(BF16)

            pair_sum(2 * tx + ty, put_in, put_small)
            for a, (_, _, send, land, _) in enumerate(payloads):
                cp = pltpu.make_async_remote_copy(
                    src_ref=send.at[j], dst_ref=land.at[j], send_sem=ssem_g.at[j, a], recv_sem=rsem_g.at[j, a],
                    device_id=(tx, ty, c), device_id_type=MESH)
                cp.start()
                copies.append(cp)

        def own_in(sl, val):
            fi_ref[c, :, sl] = val

        def own_small(sl, val):
            fs_ref[c, sl, :] = val

        pair_sum(chip, own_in, own_small)
        for j in range(3):
            for a, (_, _, send, land, _) in enumerate(payloads):
                pltpu.make_async_remote_copy(
                    src_ref=send.at[j], dst_ref=land.at[j], send_sem=ssem_g.at[j, a], recv_sem=rsem_g.at[j, a],
                    device_id=(x, y, c), device_id_type=MESH).wait_recv()
            for qd in range(HALF_D // LANES):
                sl = slice(LANES * qd, LANES * qd + LANES)
                fi_ref[c, :, sl] += land_i[j, :, sl].astype(F32)

            def add_rows(i, carry, j=j):
                sl = pl.ds(pl.multiple_of(i * SMALL_SUM_ROWS, 16), SMALL_SUM_ROWS)
                fs_ref[c, sl, :] += land_s[j, sl, :].astype(F32)
                return carry

            lax.fori_loop(0, SMALL_HALF // SMALL_SUM_ROWS, add_rows, 0)

        for a, f in enumerate((fi_ref, fs_ref)):
            cp = pltpu.make_async_remote_copy(
                src_ref=f.at[c], dst_ref=f.at[c], send_sem=ssem_x.at[a], recv_sem=rsem_x.at[a],
                device_id=(x, y, 1 - c), device_id_type=MESH)
            cp.start()
            copies.append(cp)
        for a, f in enumerate((fi_ref, fs_ref)):
            pltpu.make_async_remote_copy(
                src_ref=f.at[c], dst_ref=f.at[1 - c], send_sem=ssem_x.at[a], recv_sem=rsem_x.at[a],
                device_id=(x, y, c), device_id_type=MESH).wait_recv()
        for r in range(1, N_DEV):
            dx, dy, dc = (r >> 2) & 1, (r >> 1) & 1, r & 1
            src = 4 * _flip(x, dx) + 2 * _flip(y, dy) + _flip(c, dc)
            pltpu.make_async_remote_copy(
                src_ref=sv_ref, dst_ref=svg_ref.at[src], send_sem=ssem_s.at[r - 1],
                recv_sem=rsem_s.at[r - 1], device_id=(x, y, c), device_id_type=MESH).wait_recv()
        for cp in copies:
            cp.wait_send()

    vmem = pl.BlockSpec(memory_space=pltpu.VMEM)
    return pl.pallas_call(
        body, name="grad_reduce",
        out_shape=(_sds((2, IN_SHARD, HALF_D), F32), _sds((2, SMALL_HALF, LANES), F32),
                   _sds((N_DEV, 8, SV_COLS), F32)),
        in_specs=[vmem, vmem, vmem], out_specs=(vmem, vmem, vmem),
        scratch_shapes=[
            pltpu.VMEM((N_CHIPS, IN_SHARD, HALF_D), BF16), pltpu.VMEM((N_CHIPS, SMALL_HALF, LANES), BF16),
            pltpu.VMEM((3, IN_SHARD, HALF_D), BF16), pltpu.VMEM((3, SMALL_HALF, LANES), BF16),
            pltpu.VMEM((3, IN_SHARD, HALF_D), BF16), pltpu.VMEM((3, SMALL_HALF, LANES), BF16),
            pltpu.SemaphoreType.DMA((N_CHIPS, 2)), pltpu.SemaphoreType.DMA((N_CHIPS, 2)),
            pltpu.SemaphoreType.DMA((3, 2)), pltpu.SemaphoreType.DMA((3, 2)),
            pltpu.SemaphoreType.DMA((N_DEV - 1,)), pltpu.SemaphoreType.DMA((N_DEV - 1,)),
            pltpu.SemaphoreType.DMA((2,)), pltpu.SemaphoreType.DMA((2,)),
        ],
        compiler_params=_params(),
    )(g_in, g_small, sv)


def _dwin_call(h, dproj):
    s_len = h.shape[0]
    tm = min(4 * ROW_TILE, s_len)
    nrow = s_len // tm
    nc = 4
    chunk = W_INT // nc

    def body(h_ref, dp_ref, dw_ref, acc):
        i = pl.program_id(1)

        @pl.when(i == 0)
        def _():
            acc[...] = jnp.zeros_like(acc)

        acc[...] += _dot_tn(dp_ref[...], h_ref[...])

        @pl.when(i == nrow - 1)
        def _():
            dw_ref[...] = acc[...].astype(BF16)

    return pl.pallas_call(
        body, name="dwin", grid=(nc, nrow),
        out_shape=_sds((W_INT, D_MODEL), BF16),
        in_specs=[pl.BlockSpec((tm, D_MODEL), lambda c, i: (i, 0)),
                  pl.BlockSpec((tm, chunk), lambda c, i: (i, c))],
        out_specs=pl.BlockSpec((chunk, D_MODEL), lambda c, i: (c, 0)),
        scratch_shapes=[pltpu.VMEM((chunk, D_MODEL), F32)],
        compiler_params=_params(("parallel", "arbitrary")),
    )(h, dproj)


def _swap_rows(w, group):
    r, n = w.shape
    return w.reshape(r // group, 2, group // 2, n)[:, ::-1].reshape(r, n)


def _internal_weights(w_in_t, w_uq, w_ukv):
    krot_t = w_in_t[2688:2720]
    w_int_t = jnp.concatenate([
        w_in_t[0:2688], w_in_t[2720:5280],
        jnp.tile(krot_t, (4, 1)), jnp.tile(_swap_rows(krot_t, 32), (4, 1)),
        jnp.zeros((W_INT - O_END, D_MODEL), w_in_t.dtype)], axis=0)
    uq = w_uq.reshape(Q_RANK, N_HEADS, 96)
    wp = uq[:, :, 64:].reshape(Q_RANK, 256)
    w_q = jnp.concatenate([uq[:, :, :64].reshape(Q_RANK, 512), wp, _swap_halves(wp, 32)], axis=1)
    ukv = w_ukv.reshape(KV_RANK, N_HEADS, 128)
    w_kv = jnp.concatenate([ukv[:, :, :64].reshape(KV_RANK, 512), ukv[:, :, 64:].reshape(KV_RANK, 512)], axis=1)
    return w_int_t, w_q, w_kv


def _true_weight_grads(dwi_t, dwq, dwkv):
    dkr = dwi_t[O_KR:O_KR + 128].astype(F32).reshape(4, 32, D_MODEL).sum(axis=0)
    dkr_sw = dwi_t[O_KR + 128:O_END].astype(F32).reshape(4, 32, D_MODEL).sum(axis=0)
    dkrot_t = (dkr + _swap_rows(dkr_sw, 32)).astype(dwi_t.dtype)
    g_in_t = jnp.concatenate([dwi_t[0:O_MZ], dkrot_t, dwi_t[O_MZ:O_KR]], axis=0)
    dwp = dwq[:, 512:768] + _swap_halves(dwq[:, 768:1024], 32)
    g_uq = jnp.concatenate([dwq[:, :512].reshape(Q_RANK, N_HEADS, 64), dwp.reshape(Q_RANK, N_HEADS, 32)],
                           axis=2).reshape(Q_RANK, 768)
    g_ukv = jnp.concatenate([dwkv[:, :512].reshape(KV_RANK, N_HEADS, 64), dwkv[:, 512:].reshape(KV_RANK, N_HEADS, 64)],
                            axis=2).reshape(KV_RANK, 1024)
    return g_in_t, g_uq, g_ukv


def _swap_halves(w, group):
    r, n = w.shape
    return w.reshape(r, n // group, 2, group // 2)[:, :, ::-1, :].reshape(r, n)


def _pack_shards(parts):
    return jnp.concatenate([p.reshape(-1, LANES) for p in parts], axis=0)


def _unpack_small(gw):
    offs = [0]
    for r in SMALL_ROWS:
        offs.append(offs[-1] + r)

    def cols(i, rows, shard_cols):
        blk = gw[:, offs[i]:offs[i + 1]].reshape(N_CHIPS, rows, shard_cols)
        return blk.transpose(1, 0, 2).reshape(rows, N_CHIPS * shard_cols)

    return (cols(0, Q_RANK, 192), cols(1, KV_RANK, 256), cols(2, 512, 256), cols(3, 512, 256),
            gw[:, offs[4]:offs[5]].reshape(D_MODEL, D_MODEL))


def _chip_major(g, shard_cols):
    r = g.shape[0]
    return g.reshape(r, N_CHIPS, shard_cols).transpose(1, 0, 2).reshape(N_CHIPS, -1, LANES)


def kernel(x, c, positions, w_ada, b_ada, norm_gain, w_in, q_norm_gain, w_uq, kv_norm_gain, w_ukv, w_branch_a, w_branch_b, w_out, final_norm_gain, loss_target, m_w_ada, m_b_ada, m_norm_gain, m_w_in, m_q_norm_gain, m_w_uq, m_kv_norm_gain, m_w_ukv, m_w_branch_a, m_w_branch_b, m_w_out, m_final_norm_gain, v_w_ada, v_b_ada, v_norm_gain, v_w_in, v_q_norm_gain, v_w_uq, v_kv_norm_gain, v_w_ukv, v_w_branch_a, v_w_branch_b, v_w_out, v_final_norm_gain):
    ix, iy, ic = lax.axis_index("x"), lax.axis_index("y"), lax.axis_index("c")
    me = 4 * ix + 2 * iy + ic
    chip = 2 * ix + iy
    xs = x[0]
    tgt = loss_target[0]
    s_len = xs.shape[0]

    w_in_t = jnp.swapaxes(w_in[0], 0, 1)
    w_in_tb = w_in_t.astype(BF16)
    pack_in = jnp.stack([w_in_tb[:, :HALF_D], w_in_tb[:, HALF_D:]], axis=0)
    small_shards = (w_uq[0], w_ukv[0], w_branch_a[0], w_branch_b[0], w_out[0])
    pack_small = _pack_shards([s.astype(BF16) for s in small_shards]).reshape(2, SMALL_HALF, LANES)
    mg, call, gw_in, gw_small = _gather_call(c, w_ada[0], pack_in, pack_small)
    mod = mg.transpose(1, 0, 2).reshape(N_DEV, 3 * D_MODEL) + b_ada
    mod_me = lax.dynamic_slice_in_dim(mod, me, 1, axis=0)
    shift, scale, gate = mod_me[:, :D_MODEL], mod_me[:, D_MODEL:2 * D_MODEL], mod_me[:, 2 * D_MODEL:]

    f_in_t = jnp.concatenate([gw_in[:, 0], gw_in[:, 1]], axis=2).reshape(IN_WIDTH, D_MODEL)
    f_uq, f_ukv, f_a, f_b, f_out = _unpack_small(gw_small.reshape(N_CHIPS, SMALL_TOTAL, LANES))
    w_int_t, w_q, w_kv = _internal_weights(f_in_t, f_uq, f_ukv)

    inv_freq = ROPE_BASE ** (-jnp.arange(0, ROPE_DIM, 2, dtype=F32) / ROPE_DIM)
    ang = positions[0].astype(F32)[:, None] * inv_freq
    cs, sn = jnp.cos(ang), jnp.sin(ang)
    cos256 = jnp.tile(jnp.concatenate([cs, cs], axis=1), (1, 8))
    sin256 = jnp.tile(jnp.concatenate([-sn, sn], axis=1), (1, 8))

    (h, sq, sk, sv, sz, cq, ckv, mz, ga, gb, kpt, qn, qp, kn, vv) = _inproj_call(
        xs, shift, scale, norm_gain, w_int_t, w_q, w_kv, q_norm_gain, kv_norm_gain, cos256, sin256)
    oa, lt, first = _sb_fwd_call(sq, sk, sv)
    ob, lse = _mla_fwd_call(qn, qp, kn, kpt, vv)

    gf = final_norm_gain.reshape(1, D_MODEL)
    (dx2, doa, dob, dsz, dmz, dga, dgb, dwo, dwa, dwb, dgf, dgate, loss_p) = _post_call(
        xs, tgt, oa, ob, sz, mz, ga, gb, gate, gf, f_a, f_b, f_out, f_a.T, f_b.T, f_out.T)

    dsq, dsk_t, dsv_t = _sb_bwd_call(first[:, :, 0, 0].reshape(-1), sq, sk, sv, doa, lt)
    dqn, dqp, dkn_t, dkpt_t, dvv_t = _mla_bwd_call(qn, qp, kn, kpt, vv, ob, dob, lse)

    dproj, dwq, dwkv, dqg, dkvg = _bwdprep_call(
        dsq, dsk_t, dsv_t, dsz, dqn, dqp, dkn_t, dvv_t, dkpt_t, dmz, dga, dgb, cq, ckv, cos256, sin256,
        q_norm_gain, kv_norm_gain, w_q.T, w_kv.T)
    grad_x, dshift, dscale, dg1 = _dh_call(dproj, w_int_t, xs, dx2, scale, norm_gain)
    dwi_t = _dwin_call(h, dproj)
    g_in_t, g_uq, g_ukv = _true_weight_grads(dwi_t, dwq, dwkv)

    g_in_c = g_in_t.reshape(N_CHIPS, IN_SHARD, D_MODEL)
    g_in_pieces = jnp.stack([g_in_c[:, :, :HALF_D], g_in_c[:, :, HALF_D:]], axis=1).reshape(N_DEV, IN_SHARD, HALF_D)
    g_small = jnp.concatenate([
        _chip_major(g_uq, 192), _chip_major(g_ukv, 256), _chip_major(dwa, 256), _chip_major(dwb, 256),
        dwo.reshape(N_CHIPS, -1, LANES)], axis=1).astype(BF16).reshape(N_DEV, SMALL_HALF, LANES)
    small = jnp.concatenate([
        dshift, dscale, dgate, dg1, dqg, dkvg, dgf, loss_p,
        jnp.zeros((1, 8 * SV_COLS - 5888), F32)], axis=1).reshape(8, SV_COLS)
    full_in, full_small, svg = _reduce_call(g_in_pieces, g_small, small)
    gs_in_t = jnp.concatenate([full_in[0], full_in[1]], axis=1)
    full = full_small.reshape(SMALL_TOTAL, LANES)
    offs = [0]
    for r in SMALL_ROWS:
        offs.append(offs[-1] + r)
    gs_uq = full[offs[0]:offs[1]].reshape(Q_RANK, 192)
    gs_ukv = full[offs[1]:offs[2]].reshape(KV_RANK, 256)
    gs_a = full[offs[2]:offs[3]].reshape(512, 256)
    gs_b = full[offs[3]:offs[4]].reshape(512, 256)
    gs_out = full[offs[4]:offs[5]].reshape(256, D_MODEL)

    svm = svg.reshape(N_DEV, 8 * SV_COLS)
    dmod_sh = lax.dynamic_slice_in_dim(svm[:, :3 * D_MODEL], chip * 768, 768, axis=1)
    tot, gs_ada = _small_call(svm, call.T, dmod_sh)
    g_bada = tot[:, 0:3072]
    g_g1 = tot[:, 3072:4096]
    g_qg = tot[:, 4096:4480]
    g_kvg = tot[:, 4480:4736]
    g_gf = tot[:, 4736:5760]
    loss = tot[0, 5760]

    names = ["w_ada", "b_ada", "norm_gain", "w_in", "q_norm_gain", "w_uq", "kv_norm_gain", "w_ukv",
             "w_branch_a", "w_branch_b", "w_out", "final_norm_gain"]
    ws = [w_ada[0], b_ada, norm_gain, w_in_t, q_norm_gain, w_uq[0], kv_norm_gain, w_ukv[0],
          w_branch_a[0], w_branch_b[0], w_out[0], final_norm_gain.reshape(1, D_MODEL)]
    gs = [gs_ada, g_bada, g_g1, gs_in_t, g_qg, gs_uq, g_kvg, gs_ukv, gs_a, gs_b, gs_out, g_gf]
    ms = [m_w_ada[0], m_b_ada, m_norm_gain, jnp.swapaxes(m_w_in[0], 0, 1), m_q_norm_gain, m_w_uq[0],
          m_kv_norm_gain, m_w_ukv[0], m_w_branch_a[0], m_w_branch_b[0], m_w_out[0],
          m_final_norm_gain.reshape(1, D_MODEL)]
    vs = [v_w_ada[0], v_b_ada, v_norm_gain, jnp.swapaxes(v_w_in[0], 0, 1), v_q_norm_gain, v_w_uq[0],
          v_kv_norm_gain, v_w_ukv[0], v_w_branch_a[0], v_w_branch_b[0], v_w_out[0],
          v_final_norm_gain.reshape(1, D_MODEL)]
    refs = [w_ada, b_ada, norm_gain, w_in, q_norm_gain, w_uq, kv_norm_gain, w_ukv,
            w_branch_a, w_branch_b, w_out, final_norm_gain]
    grads, deltas, new_ms, new_vs = [], [], [], []
    for n, w_, g_, m_, v_, ref in zip(names, ws, gs, ms, vs, refs):
        outs = (g_,) + _adamw_call(n, w_, g_, m_, v_)
        if n == "w_in":
            outs = tuple(jnp.swapaxes(o_, 0, 1) for o_ in outs)
        for lst, o_ in zip((grads, deltas, new_ms, new_vs), outs):
            lst.append(o_.reshape(ref.shape))

    return (loss, grad_x.reshape(x.shape), *grads, *deltas, *new_ms, *new_vs)
```

```python
import math

import jax
import jax.numpy as jnp
from jax import lax
from jax.experimental import pallas as pl
from jax.experimental.pallas import tpu as pltpu

F32 = jnp.float32
BF16 = jnp.bfloat16

D_MODEL = 1024
SB_WIDTH = 512
MLA_WIDTH = 512
Q_RANK = 384
KV_RANK = 256
ROPE_DIM = 32
N_HEADS = 8
IN_WIDTH = 5280
EPS = 1e-6
ROPE_BASE = 10000.0
MLA_SCALE = 1.0 / math.sqrt(96.0)
SB_SCALE = 0.125
LOG2E = 1.4426950408889634

ADAM_LR = 0.001
ADAM_B1 = 0.9
ADAM_B2 = 0.999
ADAM_EPS = 1e-08
ADAM_WD = 0.01
ADAM_STEP = 10

O_SQ, O_SK, O_SV, O_SZ, O_CQ, O_CKV, O_MZ, O_GA, O_GB, O_KR, O_END = (
    0, 512, 1024, 1536, 2048, 2432, 2688, 3200, 4224, 5248, 5504)
W_INT = 5632

N_CHIPS = 4
N_DEV = 8
LANES = 128
SV_COLS = 768

ROW_TILE = 256
ATT_TILE = 256
ATT_Q_TILES = 2
FWD_Q_TILES = 4
SB_Q_TILES = 1
MLA_KEY_TILE = 512
VMEM_LIMIT = 56 * 1024 * 1024

MESH = pl.DeviceIdType.MESH


def _dot(a, b):
    return lax.dot_general(a, b, (((1,), (0,)), ((), ())), preferred_element_type=F32)


def _dot_nt(a, b):
    return lax.dot_general(a, b, (((1,), (1,)), ((), ())), preferred_element_type=F32)


def _dot_tn(a, b):
    return lax.dot_general(a, b, (((0,), (0,)), ((), ())), preferred_element_type=F32)


def _sigmoid(z):
    return 1.0 / (1.0 + jnp.exp2(z * (-LOG2E)))


def _params(sem=None):
    if sem is None:
        return pltpu.CompilerParams(vmem_limit_bytes=VMEM_LIMIT)
    return pltpu.CompilerParams(dimension_semantics=sem, vmem_limit_bytes=VMEM_LIMIT)


def _rows(tm, n):
    return pl.BlockSpec((tm, n), lambda i: (i, 0))


def _cols(n, tm):
    return pl.BlockSpec((n, tm), lambda i: (0, i))


def _whole(shape):
    nd = len(shape)
    return pl.BlockSpec(shape, lambda i: (0,) * nd)


def _sds(shape, dtype):
    return jax.ShapeDtypeStruct(shape, dtype)


def _flip(v, d):
    return 1 - v if d else v


def _inproj_call(x, shift, scale, g1, w_int, w_q, w_kv, qg, kvg, cos256, sin256):
    s_len = x.shape[0]
    tm = min(ROW_TILE, s_len)

    def body(x_ref, sh_ref, sc_ref, g1_ref, w_ref, wq_ref, wkv_ref, qg_ref, kvg_ref, cos_ref, sin_ref,
             h_ref, sq_ref, sk_ref, sv_ref, sz_ref, cq_ref, ckv_ref, mz_ref, ga_ref, gb_ref, kpt_ref,
             qn_ref, qp_ref, kn_ref, vv_ref):
        xt = x_ref[...]
        r = lax.rsqrt(jnp.mean(xt * xt, axis=-1, keepdims=True) + EPS)
        h = (xt * r * g1_ref[...]) * (1.0 + sc_ref[...]) + sh_ref[...]
        hb = h.astype(BF16)
        h_ref[...] = hb

        def seg(a, b):
            return _dot_nt(hb, w_ref[a:b, :])

        sq_ref[...] = (seg(O_SQ, O_SK) * SB_SCALE).astype(BF16)
        sk_ref[...] = seg(O_SK, O_SV).astype(BF16)
        sv_ref[...] = seg(O_SV, O_SZ).astype(BF16)
        sz_ref[...] = seg(O_SZ, O_CQ)
        mz_ref[...] = seg(O_MZ, O_GA)
        ga_ref[...] = seg(O_GA, O_GB)
        gb_ref[...] = seg(O_GB, O_KR)
        cos = cos_ref[...]
        sin = sin_ref[...]
        kr = seg(O_KR, O_END)
        kpt_ref[...] = (kr[:, :128] * cos[:, :128] + kr[:, 128:] * sin[:, :128]).astype(BF16)

        cq = seg(O_CQ, O_CKV)
        cq_ref[...] = cq
        rq = lax.rsqrt(jnp.mean(cq * cq, axis=-1, keepdims=True) + EPS)
        cqn = (cq * rq * qg_ref[...]).astype(BF16)
        qa = _dot(cqn, wq_ref[...])
        qn_ref[...] = qa[:, :512].astype(BF16)
        qp_ref[...] = (qa[:, 512:768] * cos + qa[:, 768:] * sin).astype(BF16)

        ckv = seg(O_CKV, O_MZ)
        ckv_ref[...] = ckv
        rk = lax.rsqrt(jnp.mean(ckv * ckv, axis=-1, keepdims=True) + EPS)
        ckvn = (ckv * rk * kvg_ref[...]).astype(BF16)
        kva = _dot(ckvn, wkv_ref[...])
        kn_ref[...] = kva[:, :512].astype(BF16)
        vv_ref[...] = kva[:, 512:].astype(BF16)

    outs = [
        (D_MODEL, BF16), (512, BF16), (512, BF16), (512, BF16), (512, F32), (Q_RANK, F32), (KV_RANK, F32),
        (512, F32), (D_MODEL, F32), (D_MODEL, F32), (128, BF16), (512, BF16), (256, BF16), (512, BF16), (512, BF16),
    ]
    return pl.pallas_call(
        body, name="inproj", grid=(s_len // tm,),
        out_shape=tuple(_sds((s_len, n), dt) for n, dt in outs),
        in_specs=[_rows(tm, D_MODEL), _whole((1, D_MODEL)), _whole((1, D_MODEL)), _whole((1, D_MODEL)),
                  _whole((W_INT, D_MODEL)), _whole((Q_RANK, 1024)), _whole((KV_RANK, 1024)),
                  _whole((1, Q_RANK)), _whole((1, KV_RANK)), _rows(tm, 256), _rows(tm, 256)],
        out_specs=tuple(_rows(tm, n) for n, _ in outs),
        compiler_params=_params(("parallel",)),
    )(x, shift, scale, g1, w_int, w_q, w_kv, qg, kvg, cos256, sin256)


Z_CLAMP = 80.0 * LOG2E
RUN_CUTOFF = 110.0 * LOG2E


def _softplus_clamped(z):
    zc = jnp.minimum(z * LOG2E, Z_CLAMP)
    return zc, jnp.log2(1.0 + jnp.exp2(zc))


def _tri_sum(a, tri):
    return _dot(a.astype(BF16), tri)


def _sb_fwd_call(q, k, v):
    s_len = q.shape[0]
    tk = min(ATT_TILE, s_len)
    tq = min(SB_Q_TILES * ATT_TILE, s_len)
    r = tq // tk
    nq = s_len // tq

    def body(q_ref, k_ref, v_ref, o_ref, lt_ref, first_ref):
        i = pl.program_id(1)
        q2 = q_ref[...]
        lane = lax.broadcasted_iota(jnp.int32, (1, 256), 1)
        krow = lax.broadcasted_iota(jnp.int32, (tk, tk), 0)
        kcol = lax.broadcasted_iota(jnp.int32, (tk, tk), 1)
        row = lax.broadcasted_iota(jnp.int32, (tq, tk), 0)
        col = lax.broadcasted_iota(jnp.int32, (tq, tk), 1)
        later = (krow > kcol).astype(BF16)
        valids = [col + u * tk < row for u in range(r)]
        hms = [(lane // 64) == hh for hh in range(4)]
        qms = [jnp.where(hm, q2, jnp.zeros_like(q2)) for hm in hms]

        def block(j, carry, valid):
            runs, acc = list(carry[:4]), carry[4]
            off = pl.multiple_of(j * tk, tk)
            kb = k_ref[pl.ds(off, tk), :]
            vb = v_ref[pl.ds(off, tk), :]
            ws = []
            for hh in range(4):
                zc, sp = _softplus_clamped(_dot_nt(qms[hh], kb))
                lm = jnp.where(valid, sp, 0.0) if valid is not None else sp
                suf = _tri_sum(lm, later)
                w = jnp.exp2(zc - sp - suf - runs[hh])
                if valid is not None:
                    w = jnp.where(valid, w, 0.0)
                ws.append(w.astype(BF16))
                runs[hh] = runs[hh] + jnp.sum(lm, axis=1, keepdims=True)
            vstack = jnp.concatenate([jnp.where(hm, vb, jnp.zeros_like(vb)) for hm in hms], axis=0)
            acc = acc + _dot(jnp.concatenate(ws, axis=1), vstack)
            return (*runs, acc)

        zero = jnp.zeros((tq, 1), F32)
        carry = (zero, zero, zero, zero, jnp.zeros((tq, 256), F32))
        for u in reversed(range(r)):
            carry = block(i * r + u, carry, valids[u])

        def least_run(runs):
            return jnp.min(jnp.minimum(jnp.minimum(runs[0], runs[1]), jnp.minimum(runs[2], runs[3])))

        n_full = i * r

        def unfinished(state):
            return jnp.logical_and(state[0] < n_full, state[1] <= RUN_CUTOFF)

        def visit(state):
            cr = block(n_full - 1 - state[0], state[2:], None)
            return (state[0] + 1, least_run(cr[:4]), *cr)

        state = lax.while_loop(unfinished, visit, (jnp.int32(0), least_run(carry[:4]), *carry))
        carry = state[2:]
        first_ref[...] = jnp.full(first_ref.shape, n_full - state[0], jnp.int32)
        for hh in range(4):
            lt_ref[0, :, hh:hh + 1] = carry[hh]
        o_ref[...] = carry[4]

    return pl.pallas_call(
        body, name="sb_fwd", grid=(2, nq),
        out_shape=(_sds((s_len, SB_WIDTH), F32), _sds((2, s_len, 4), F32), _sds((2, nq, 8, 128), jnp.int32)),
        in_specs=[pl.BlockSpec((tq, 256), lambda g, i: (i, g)),
                  pl.BlockSpec((s_len, 256), lambda g, i: (0, g)),
                  pl.BlockSpec((s_len, 256), lambda g, i: (0, g))],
        out_specs=(pl.BlockSpec((tq, 256), lambda g, i: (i, g)),
                   pl.BlockSpec((1, tq, 4), lambda g, i: (g, i, 0)),
                   pl.BlockSpec((1, 1, 8, 128), lambda g, i: (g, i, 0, 0))),
        compiler_params=_params(("parallel", "parallel")),
    )(q, k, v)


def _sb_bwd_call(first, q, k, v, do, lt):
    s_len = q.shape[0]
    tk = min(ATT_TILE, s_len)
    tq = min(SB_Q_TILES * ATT_TILE, s_len)
    r = tq // tk
    nq = s_len // tq
    nq_fwd = first.shape[0] // 2
    per_fwd = nq // nq_fwd

    def body(first_ref, q_ref, k_ref, v_ref, do_ref, lt_ref, dq_ref, dk_ref, dv_ref):
        g = pl.program_id(0)
        i = pl.program_id(1)

        @pl.when(i == 0)
        def _():
            dk_ref[...] = jnp.zeros_like(dk_ref)
            dv_ref[...] = jnp.zeros_like(dv_ref)

        q2 = q_ref[...]
        do2 = do_ref[...].astype(BF16)
        lane = lax.broadcasted_iota(jnp.int32, (1, 256), 1)
        krow = lax.broadcasted_iota(jnp.int32, (tk, tk), 0)
        kcol = lax.broadcasted_iota(jnp.int32, (tk, tk), 1)
        row = lax.broadcasted_iota(jnp.int32, (tq, tk), 0)
        col = lax.broadcasted_iota(jnp.int32, (tq, tk), 1)
        earlier = (krow < kcol).astype(BF16)
        later = (krow > kcol).astype(BF16)
        valids = [col + u * tk < row for u in range(r)]
        hms = [(lane // 64) == hh for hh in range(4)]
        qms = [jnp.where(hm, q2, jnp.zeros_like(q2)) for hm in hms]
        doms = [jnp.where(hm, do2, jnp.zeros_like(do2)) for hm in hms]
        ltots = [lt_ref[0, :, hh:hh + 1] for hh in range(4)]
        q2t = jnp.transpose(q2.astype(F32))
        do2t = jnp.transpose(do_ref[...])
        subl = lax.broadcasted_iota(jnp.int32, (256, 1), 0)
        qtstack = jnp.concatenate(
            [jnp.where((subl // 64) == hh, q2t, 0.0).astype(BF16) for hh in range(4)], axis=1)
        dotstack = jnp.concatenate(
            [jnp.where((subl // 64) == hh, do2t, 0.0).astype(BF16) for hh in range(4)], axis=1)

        def block(j, carry, valid):
            lpre, ppre, dq = list(carry[0:4]), list(carry[4:8]), carry[8]
            off = pl.multiple_of(j * tk, tk)
            kb = k_ref[pl.ds(off, tk), :]
            vb = v_ref[pl.ds(off, tk), :]
            dzs, avs = [], []
            for hh in range(4):
                zc, sp = _softplus_clamped(_dot_nt(qms[hh], kb))
                lsig = zc - sp
                lm = jnp.where(valid, sp, 0.0) if valid is not None else sp
                rowsum = jnp.sum(lm, axis=1, keepdims=True)
                between = _tri_sum(lm, later) + ((ltots[hh] - lpre[hh]) - rowsum)
                a = jnp.exp2(lsig - between)
                if valid is not None:
                    a = jnp.where(valid, a, 0.0)
                p = a * _dot_nt(doms[hh], vb)
                pbefore = ppre[hh] + _tri_sum(p, earlier)
                dz = p - jnp.exp2(lsig) * (p + pbefore)
                if valid is not None:
                    dz = jnp.where(valid, dz, 0.0)
                dzs.append(dz.astype(BF16))
                avs.append(a.astype(BF16))
                lpre[hh] = lpre[hh] + rowsum
                ppre[hh] = ppre[hh] + jnp.sum(p, axis=1, keepdims=True)
            kstack = jnp.concatenate([jnp.where(hm, kb, jnp.zeros_like(kb)) for hm in hms], axis=0)
            dq = dq + _dot(jnp.concatenate(dzs, axis=1), kstack)
            dk_ref[:, pl.ds(off, tk)] += _dot(qtstack, jnp.concatenate(dzs, axis=0))
            dv_ref[:, pl.ds(off, tk)] += _dot(dotstack, jnp.concatenate(avs, axis=0))
            return (*lpre, *ppre, dq)

        zero = jnp.zeros((tq, 1), F32)
        start = jnp.minimum(first_ref[g * nq_fwd + i // per_fwd], i * r)
        carry = lax.fori_loop(start, i * r, lambda j, cr: block(j, cr, None),
                              (zero,) * 8 + (jnp.zeros((tq, 256), F32),))
        for u in range(r):
            carry = block(i * r + u, carry, valids[u])
        dq_ref[...] = carry[8].astype(BF16)

    return pl.pallas_call(
        body, name="sb_bwd",
        out_shape=(_sds((s_len, SB_WIDTH), BF16), _sds((SB_WIDTH, s_len), F32), _sds((SB_WIDTH, s_len), F32)),
        grid_spec=pltpu.PrefetchScalarGridSpec(
            num_scalar_prefetch=1, grid=(2, nq),
            in_specs=[pl.BlockSpec((tq, 256), lambda g, i, f: (i, g)),
                      pl.BlockSpec((s_len, 256), lambda g, i, f: (0, g)),
                      pl.BlockSpec((s_len, 256), lambda g, i, f: (0, g)),
                      pl.BlockSpec((tq, 256), lambda g, i, f: (i, g)),
                      pl.BlockSpec((1, tq, 4), lambda g, i, f: (g, i, 0))],
            out_specs=(pl.BlockSpec((tq, 256), lambda g, i, f: (i, g)),
                       pl.BlockSpec((256, s_len), lambda g, i, f: (g, 0)),
                       pl.BlockSpec((256, s_len), lambda g, i, f: (g, 0)))),
        compiler_params=_params(("parallel", "arbitrary")),
    )(first, q, k, v, do, lt)


def _mla_fwd_call(qn, qp, kn, kpt, v):
    s_len = qn.shape[0]
    tk = min(MLA_KEY_TILE, s_len)
    tq = min(FWD_Q_TILES * ATT_TILE, s_len)
    r = tq // tk
    nq = s_len // tq

    def body(qn_ref, qp_ref, kn_ref, kpt_ref, v_ref, o_ref, lse_ref):
        i = pl.program_id(1)
        qn2 = qn_ref[...]
        qp2 = qp_ref[...]
        lane256 = lax.broadcasted_iota(jnp.int32, (1, 256), 1)
        lane128 = lax.broadcasted_iota(jnp.int32, (1, 128), 1)
        krow = lax.broadcasted_iota(jnp.int32, (tk, tk), 0)
        kcol = lax.broadcasted_iota(jnp.int32, (tk, tk), 1)
        row = lax.broadcasted_iota(jnp.int32, (tq, tk), 0)
        col = lax.broadcasted_iota(jnp.int32, (tq, tk), 1)
        valids = [col + u * tk <= row for u in range(r)]
        m64s = [(lane256 // 64) == hh for hh in range(4)]
        half = [(lane128 // 64) == u for u in range(2)]
        m32s = [(lane128 // 32) == hh for hh in range(4)]
        qcs = []
        for hh in range(4):
            qpair = qn2[:, 128 * (hh // 2):128 * (hh // 2) + 128]
            qcs.append(jnp.concatenate([jnp.where(half[hh % 2], qpair, jnp.zeros_like(qpair)),
                                        jnp.where(m32s[hh], qp2, jnp.zeros_like(qp2))], axis=1))

        def by_head(vals):
            return jnp.where(m64s[0], vals[0], jnp.where(m64s[1], vals[1], jnp.where(m64s[2], vals[2], vals[3])))

        def block(j, carry, valid):
            ms, ls, acc = list(carry[0:4]), list(carry[4:8]), carry[8]
            off = pl.multiple_of(j * tk, tk)
            knb = kn_ref[pl.ds(off, tk), :]
            kpb = kpt_ref[pl.ds(off, tk), :]
            vb = v_ref[pl.ds(off, tk), :]
            kcs = [jnp.concatenate([knb[:, 128 * pp:128 * pp + 128], kpb], axis=1) for pp in range(2)]
            ps, alphas = [], []
            for hh in range(4):
                s = _dot_nt(qcs[hh], kcs[hh // 2]) * (MLA_SCALE * LOG2E)
                if valid is not None:
                    s = jnp.where(valid, s, -1e30)
                mn = jnp.maximum(ms[hh], jnp.max(s, axis=1, keepdims=True))
                p = jnp.exp2(s - mn)
                alpha = jnp.exp2(ms[hh] - mn)
                ls[hh] = alpha * ls[hh] + jnp.sum(p, axis=1, keepdims=True)
                ms[hh] = mn
                ps.append(p.astype(BF16))
                alphas.append(alpha)
            pvs = []
            for pp in range(2):
                vpair = vb[:, 128 * pp:128 * pp + 128]
                vstack = jnp.concatenate([jnp.where(hf, vpair, jnp.zeros_like(vpair)) for hf in half], axis=0)
                pvs.append(_dot(jnp.concatenate(ps[2 * pp:2 * pp + 2], axis=1), vstack))
            acc = by_head(alphas) * acc + jnp.concatenate(pvs, axis=1)
            return (*ms, *ls, acc)

        neg = jnp.full((tq, 1), -1e30, F32)
        zero = jnp.zeros((tq, 1), F32)
        carry = lax.fori_loop(0, i * r, lambda j, cr: block(j, cr, None),
                              (neg,) * 4 + (zero,) * 4 + (jnp.zeros((tq, 256), F32),))
        for u in range(r):
            carry = block(i * r + u, carry, valids[u])
        o_ref[...] = carry[8] / by_head(list(carry[4:8]))
        for hh in range(4):
            lse_ref[0, :, hh:hh + 1] = (carry[hh] + jnp.log2(carry[4 + hh])) * (1.0 / LOG2E)

    return pl.pallas_call(
        body, name="mla_fwd", grid=(2, nq),
        out_shape=(_sds((s_len, MLA_WIDTH), F32), _sds((2, s_len, 4), F32)),
        in_specs=[pl.BlockSpec((tq, 256), lambda g, i: (i, g)),
                  pl.BlockSpec((tq, 128), lambda g, i: (i, g)),
                  pl.BlockSpec((s_len, 256), lambda g, i: (0, g)),
                  pl.BlockSpec((s_len, 128), lambda g, i: (0, 0)),
                  pl.BlockSpec((s_len, 256), lambda g, i: (0, g))],
        out_specs=(pl.BlockSpec((tq, 256), lambda g, i: (i, g)),
                   pl.BlockSpec((1, tq, 4), lambda g, i: (g, i, 0))),
        compiler_params=_params(("parallel", "parallel")),
    )(qn, qp, kn, kpt, v)


def _mla_bwd_call(qn, qp, kn, kpt, v, o, do, lse):
    s_len = qn.shape[0]
    tk = min(MLA_KEY_TILE, s_len)
    tq = min(ATT_Q_TILES * ATT_TILE, s_len)
    r = tq // tk
    nq = s_len // tq

    def body(qn_ref, qp_ref, kn_ref, kpt_ref, v_ref, o_ref, do_ref, lse_ref,
             dqn_ref, dqp_ref, dkn_ref, dkpt_ref, dv_ref):
        g = pl.program_id(0)
        i = pl.program_id(1)

        @pl.when(i == 0)
        def _():
            dkn_ref[...] = jnp.zeros_like(dkn_ref)
            dv_ref[...] = jnp.zeros_like(dv_ref)

        @pl.when((i == 0) & (g == 0))
        def _():
            dkpt_ref[...] = jnp.zeros_like(dkpt_ref)

        qn2 = qn_ref[...]
        qp2 = qp_ref[...]
        dof = do_ref[...]
        dob = dof.astype(BF16)
        prod = dof * o_ref[...]
        lane256 = lax.broadcasted_iota(jnp.int32, (1, 256), 1)
        lane128 = lax.broadcasted_iota(jnp.int32, (1, 128), 1)
        krow = lax.broadcasted_iota(jnp.int32, (tk, tk), 0)
        kcol = lax.broadcasted_iota(jnp.int32, (tk, tk), 1)
        row = lax.broadcasted_iota(jnp.int32, (tq, tk), 0)
        col = lax.broadcasted_iota(jnp.int32, (tq, tk), 1)
        valids = [col + u * tk <= row for u in range(r)]
        m64s = [(lane256 // 64) == hh for hh in range(4)]
        half = [(lane128 // 64) == u for u in range(2)]
        m32s = [(lane128 // 32) == hh for hh in range(4)]
        qcs, doms = [], []
        for hh in range(4):
            sl = slice(128 * (hh // 2), 128 * (hh // 2) + 128)
            qpair = qn2[:, sl]
            dpair = dob[:, sl]
            qcs.append(jnp.concatenate([jnp.where(half[hh % 2], qpair, jnp.zeros_like(qpair)),
                                        jnp.where(m32s[hh], qp2, jnp.zeros_like(qp2))], axis=1))
            doms.append(jnp.where(half[hh % 2], dpair, jnp.zeros_like(dpair)))
        dsums = [jnp.sum(jnp.where(m64, prod, 0.0), axis=1, keepdims=True) * MLA_SCALE for m64 in m64s]
        lses = [lse_ref[0, :, hh:hh + 1] * LOG2E for hh in range(4)]
        qn2t = jnp.transpose(qn2.astype(F32))
        qp2t = jnp.transpose(qp2.astype(F32))
        do2t = jnp.transpose(dof)
        sub128 = lax.broadcasted_iota(jnp.int32, (128, 1), 0)
        qtstacks, dotstacks = [], []
        for pp in range(2):
            qts, dts = [], []
            for u in range(2):
                hh = 2 * pp + u
                qts.append(jnp.concatenate(
                    [jnp.where((sub128 // 64) == u, qn2t[128 * pp:128 * pp + 128, :], 0.0),
                     jnp.where((sub128 // 32) == hh, qp2t, 0.0)], axis=0).astype(BF16))
                dts.append(jnp.where((sub128 // 64) == u, do2t[128 * pp:128 * pp + 128, :], 0.0).astype(BF16))
            qtstacks.append(jnp.concatenate(qts, axis=1))
            dotstacks.append(jnp.concatenate(dts, axis=1))

        def block(j, carry, valid):
            dqn, dqp = carry
            off = pl.multiple_of(j * tk, tk)
            knb = kn_ref[pl.ds(off, tk), :]
            kpb = kpt_ref[pl.ds(off, tk), :]
            vb = v_ref[pl.ds(off, tk), :]
            dqn_parts = []
            dkp = None
            for pp in range(2):
                sl = slice(128 * pp, 128 * pp + 128)
                knp = knb[:, sl]
                vpair = vb[:, sl]
                kc = jnp.concatenate([knp, kpb], axis=1)
                dss, pbs, kcms = [], [], []
                for u in range(2):
                    hh = 2 * pp + u
                    s = _dot_nt(qcs[hh], kc) * (MLA_SCALE * LOG2E)
                    if valid is not None:
                        s = jnp.where(valid, s, -1e30)
                    p = jnp.exp2(s - lses[hh])
                    ds = p * (_dot_nt(doms[hh], vpair) * MLA_SCALE - dsums[hh])
                    dss.append(ds.astype(BF16))
                    pbs.append(p.astype(BF16))
                    kcms.append(jnp.concatenate([jnp.where(half[u], knp, jnp.zeros_like(knp)),
                                                 jnp.where(m32s[hh], kpb, jnp.zeros_like(kpb))], axis=1))
                dqc = _dot(jnp.concatenate(dss, axis=1), jnp.concatenate(kcms, axis=0))
                dqn_parts.append(dqc[:, :128])
                dqp = dqp + dqc[:, 128:]
                dkc = _dot(qtstacks[pp], jnp.concatenate(dss, axis=0))
                dkn_ref[128 * pp:128 * pp + 128, pl.ds(off, tk)] += dkc[:128, :]
                dkp = dkc[128:, :] if dkp is None else dkp + dkc[128:, :]
                dv_ref[128 * pp:128 * pp + 128, pl.ds(off, tk)] += _dot(dotstacks[pp], jnp.concatenate(pbs, axis=0))
            dqn = dqn + jnp.concatenate(dqn_parts, axis=1)
            dkpt_ref[:, pl.ds(off, tk)] += dkp
            return dqn, dqp

        carry = lax.fori_loop(0, i * r, lambda j, cr: block(j, cr, None),
                              (jnp.zeros((tq, 256), F32), jnp.zeros((tq, 128), F32)))
        for u in range(r):
            carry = block(i * r + u, carry, valids[u])
        dqn, dqp = carry
        dqn_ref[...] = dqn.astype(BF16)
        dqp_ref[...] = dqp.astype(BF16)

    return pl.pallas_call(
        body, name="mla_bwd", grid=(2, nq),
        out_shape=(_sds((s_len, 512), BF16), _sds((s_len, 256), BF16), _sds((512, s_len), F32),
                   _sds((128, s_len), F32), _sds((512, s_len), F32)),
        in_specs=[pl.BlockSpec((tq, 256), lambda g, i: (i, g)),
                  pl.BlockSpec((tq, 128), lambda g, i: (i, g)),
                  pl.BlockSpec((s_len, 256), lambda g, i: (0, g)),
                  pl.BlockSpec((s_len, 128), lambda g, i: (0, 0)),
                  pl.BlockSpec((s_len, 256), lambda g, i: (0, g)),
                  pl.BlockSpec((tq, 256), lambda g, i: (i, g)),
                  pl.BlockSpec((tq, 256), lambda g, i: (i, g)),
                  pl.BlockSpec((1, tq, 4), lambda g, i: (g, i, 0))],
        out_specs=(pl.BlockSpec((tq, 256), lambda g, i: (i, g)),
                   pl.BlockSpec((tq, 128), lambda g, i: (i, g)),
                   pl.BlockSpec((256, s_len), lambda g, i: (g, 0)),
                   pl.BlockSpec((128, s_len), lambda g, i: (0, 0)),
                   pl.BlockSpec((256, s_len), lambda g, i: (g, 0))),
        compiler_params=_params(("arbitrary", "arbitrary")),
    )(qn, qp, kn, kpt, v, o, do, lse)


def _post_call(x, tgt, oa, ob, sz, mz, ga, gb, gate, gf, wa, wb, wo):
    s_len = x.shape[0]
    tm = min(ROW_TILE, s_len)

    def body(x_ref, t_ref, oa_ref, ob_ref, sz_ref, mz_ref, ga_ref, gb_ref, gate_ref, gf_ref,
             wa_ref, wb_ref, wo_ref,
             dx2_ref, doa_ref, dob_ref, dsz_ref, dmz_ref, dga_ref, dgb_ref,
             dwo_ref, dwa_ref, dwb_ref, dgf_ref, dgate_ref, loss_ref):
        @pl.when(pl.program_id(0) == 0)
        def _():
            dwo_ref[...] = jnp.zeros_like(dwo_ref)
            dwa_ref[...] = jnp.zeros_like(dwa_ref)
            dwb_ref[...] = jnp.zeros_like(dwb_ref)
            dgf_ref[...] = jnp.zeros_like(dgf_ref)
            dgate_ref[...] = jnp.zeros_like(dgate_ref)
            loss_ref[...] = jnp.zeros_like(loss_ref)

        gate = gate_ref[...]
        gf = gf_ref[...]
        oa = oa_ref[...]
        ob = ob_ref[...]
        sz = sz_ref[...]
        mz = mz_ref[...]
        sa = _sigmoid(sz)
        sb = _sigmoid(mz)
        silu_a = sz * sa
        silu_b = mz * sb
        ua = (oa * silu_a).astype(BF16)
        ub = (ob * silu_b).astype(BF16)
        ya = _dot(ua, wa_ref[...])
        yb = _dot(ub, wb_ref[...])
        sga = _sigmoid(ga_ref[...])
        sgb = _sigmoid(gb_ref[...])
        merged = (sga * ya + sgb * yb).astype(BF16)
        out = _dot(merged, wo_ref[...])
        x2 = x_ref[...] + gate * out
        r2 = lax.rsqrt(jnp.mean(x2 * x2, axis=-1, keepdims=True) + EPS)
        xhat = x2 * r2
        err = xhat * gf - t_ref[...]
        loss_ref[...] += 0.5 * jnp.sum(jnp.sum(err * err, axis=1, keepdims=True), axis=0, keepdims=True) / D_MODEL
        dy = err * (1.0 / D_MODEL)
        dgf_ref[...] += jnp.sum(dy * xhat, axis=0, keepdims=True)
        dxhat = dy * gf
        dx2 = r2 * (dxhat - xhat * jnp.mean(dxhat * xhat, axis=-1, keepdims=True))
        dx2_ref[...] = dx2
        dgate_ref[...] += jnp.sum(dx2 * out, axis=0, keepdims=True)
        dout = (dx2 * gate).astype(BF16)
        dmerged = _dot_nt(dout, wo_ref[...])
        dwo_ref[...] += _dot_tn(merged, dout)
        dya = dmerged * sga
        dyb = dmerged * sgb
        dga_ref[...] = (dya * ya * (1.0 - sga)).astype(BF16)
        dgb_ref[...] = (dyb * yb * (1.0 - sgb)).astype(BF16)
        dyab = dya.astype(BF16)
        dybb = dyb.astype(BF16)
        dua = _dot_nt(dyab, wa_ref[...])
        dub = _dot_nt(dybb, wb_ref[...])
        dwa_ref[...] += _dot_tn(ua, dyab)
        dwb_ref[...] += _dot_tn(ub, dybb)
        doa_ref[...] = dua * silu_a
        dob_ref[...] = dub * silu_b
        dsz_ref[...] = (dua * oa * (sa * (1.0 + sz * (1.0 - sa)))).astype(BF16)
        dmz_ref[...] = (dub * ob * (sb * (1.0 + mz * (1.0 - sb)))).astype(BF16)

    return pl.pallas_call(
        body, name="post", grid=(s_len // tm,),
        out_shape=(_sds((s_len, D_MODEL), F32), _sds((s_len, 512), F32), _sds((s_len, 512), F32),
                   _sds((s_len, 512), BF16), _sds((s_len, 512), BF16),
                   _sds((s_len, D_MODEL), BF16), _sds((s_len, D_MODEL), BF16),
                   _sds((D_MODEL, D_MODEL), F32), _sds((512, D_MODEL), F32), _sds((512, D_MODEL), F32),
                   _sds((1, D_MODEL), F32), _sds((1, D_MODEL), F32), _sds((1, 128), F32)),
        in_specs=[_rows(tm, D_MODEL), _rows(tm, D_MODEL), _rows(tm, 512), _rows(tm, 512), _rows(tm, 512),
                  _rows(tm, 512), _rows(tm, D_MODEL), _rows(tm, D_MODEL), _whole((1, D_MODEL)), _whole((1, D_MODEL)),
                  _whole((512, D_MODEL)), _whole((512, D_MODEL)), _whole((D_MODEL, D_MODEL))],
        out_specs=(_rows(tm, D_MODEL), _rows(tm, 512), _rows(tm, 512), _rows(tm, 512), _rows(tm, 512),
                   _rows(tm, D_MODEL), _rows(tm, D_MODEL),
                   _whole((D_MODEL, D_MODEL)), _whole((512, D_MODEL)), _whole((512, D_MODEL)),
                   _whole((1, D_MODEL)), _whole((1, D_MODEL)), _whole((1, 128))),
        compiler_params=_params(("arbitrary",)),
    )(x, tgt, oa, ob, sz, mz, ga, gb, gate, gf, wa, wb, wo)


def _bwdprep_call(dsq, dsk, dsv, dsz, dqn, dqp, dkn, dvv, dkpt, dmz, dga, dgb, cq, ckv, cos256, sin256,
                  qg, kvg, w_q, w_kv):
    s_len = cq.shape[0]
    tm = min(ROW_TILE, s_len)

    def body(dsq_ref, dsk_ref, dsv_ref, dsz_ref, dqn_ref, dqp_ref, dkn_ref, dvv_ref, dkpt_ref, dmz_ref,
             dga_ref, dgb_ref, cq_ref, ckv_ref, cos_ref, sin_ref, qg_ref, kvg_ref, wq_ref, wkv_ref,
             dp_ref, dwq_ref, dwkv_ref, dqg_ref, dkvg_ref):
        @pl.when(pl.program_id(0) == 0)
        def _():
            dwq_ref[...] = jnp.zeros_like(dwq_ref)
            dwkv_ref[...] = jnp.zeros_like(dwkv_ref)
            dqg_ref[...] = jnp.zeros_like(dqg_ref)
            dkvg_ref[...] = jnp.zeros_like(dkvg_ref)

        cos = cos_ref[...]
        sin = sin_ref[...]
        dp_ref[:, O_SQ:O_SK] = dsq_ref[...] * jnp.asarray(SB_SCALE, BF16)
        dp_ref[:, O_SK:O_SV] = jnp.transpose(dsk_ref[...]).astype(BF16)
        dp_ref[:, O_SV:O_SZ] = jnp.transpose(dsv_ref[...]).astype(BF16)
        dp_ref[:, O_SZ:O_CQ] = dsz_ref[...]
        dp_ref[:, O_MZ:O_GA] = dmz_ref[...]
        dp_ref[:, O_GA:O_GB] = dga_ref[...]
        dp_ref[:, O_GB:O_KR] = dgb_ref[...]
        dkp = jnp.transpose(dkpt_ref[...])
        dp_ref[:, O_KR:O_KR + 128] = (dkp * cos[:, :128]).astype(BF16)
        dp_ref[:, O_KR + 128:O_END] = (dkp * sin[:, :128]).astype(BF16)
        dp_ref[:, O_END:W_INT] = jnp.zeros((tm, W_INT - O_END), BF16)

        cq = cq_ref[...]
        rq = lax.rsqrt(jnp.mean(cq * cq, axis=-1, keepdims=True) + EPS)
        cqh = cq * rq
        qg = qg_ref[...]
        cqn = (cqh * qg).astype(BF16)
        dqp = dqp_ref[...].astype(F32)
        dqa = jnp.concatenate([dqn_ref[...], (dqp * cos).astype(BF16), (dqp * sin).astype(BF16)], axis=1)
        dcqn = _dot_nt(dqa, wq_ref[...])
        dwq_ref[...] += _dot_tn(cqn, dqa)
        dqg_ref[...] += jnp.sum(dcqn * cqh, axis=0, keepdims=True)
        dh = dcqn * qg
        dcq = rq * (dh - cqh * jnp.mean(dh * cqh, axis=-1, keepdims=True))
        dp_ref[:, O_CQ:O_CKV] = dcq.astype(BF16)

        ckv = ckv_ref[...]
        rk = lax.rsqrt(jnp.mean(ckv * ckv, axis=-1, keepdims=True) + EPS)
        ckh = ckv * rk
        kvg = kvg_ref[...]
        ckvn = (ckh * kvg).astype(BF16)
        dkva = jnp.concatenate([jnp.transpose(dkn_ref[...]).astype(BF16),
                                jnp.transpose(dvv_ref[...]).astype(BF16)], axis=1)
        dckvn = _dot_nt(dkva, wkv_ref[...])
        dwkv_ref[...] += _dot_tn(ckvn, dkva)
        dkvg_ref[...] += jnp.sum(dckvn * ckh, axis=0, keepdims=True)
        dh2 = dckvn * kvg
        dckv = rk * (dh2 - ckh * jnp.mean(dh2 * ckh, axis=-1, keepdims=True))
        dp_ref[:, O_CKV:O_MZ] = dckv.astype(BF16)

    return pl.pallas_call(
        body, name="bwdprep", grid=(s_len // tm,),
        out_shape=(_sds((s_len, W_INT), BF16), _sds((Q_RANK, 1024), F32), _sds((KV_RANK, 1024), F32),
                   _sds((1, Q_RANK), F32), _sds((1, KV_RANK), F32)),
        in_specs=[_rows(tm, 512), _cols(512, tm), _cols(512, tm), _rows(tm, 512), _rows(tm, 512), _rows(tm, 256),
                  _cols(512, tm), _cols(512, tm), _cols(128, tm), _rows(tm, 512), _rows(tm, D_MODEL),
                  _rows(tm, D_MODEL), _rows(tm, Q_RANK), _rows(tm, KV_RANK), _rows(tm, 256), _rows(tm, 256),
                  _whole((1, Q_RANK)), _whole((1, KV_RANK)), _whole((Q_RANK, 1024)), _whole((KV_RANK, 1024))],
        out_specs=(_rows(tm, W_INT), _whole((Q_RANK, 1024)), _whole((KV_RANK, 1024)),
                   _whole((1, Q_RANK)), _whole((1, KV_RANK))),
        compiler_params=_params(("arbitrary",)),
    )(dsq, dsk, dsv, dsz, dqn, dqp, dkn, dvv, dkpt, dmz, dga, dgb, cq, ckv, cos256, sin256, qg, kvg, w_q, w_kv)


def _dh_call(dproj, w_int_t, x, dx2, scale, g1):
    s_len = x.shape[0]
    tm = min(2 * ROW_TILE, s_len)

    def body(dp_ref, wt_ref, x_ref, dx2_ref, sc_ref, g1_ref, gx_ref, dsh_ref, dsc_ref, dg1_ref):
        @pl.when(pl.program_id(0) == 0)
        def _():
            dsh_ref[...] = jnp.zeros_like(dsh_ref)
            dsc_ref[...] = jnp.zeros_like(dsc_ref)
            dg1_ref[...] = jnp.zeros_like(dg1_ref)

        dh = _dot(dp_ref[...], wt_ref[...])
        xt = x_ref[...]
        r = lax.rsqrt(jnp.mean(xt * xt, axis=-1, keepdims=True) + EPS)
        xh = xt * r
        g1 = g1_ref[...]
        xg = xh * g1
        dsh_ref[...] += jnp.sum(dh, axis=0, keepdims=True)
        dsc_ref[...] += jnp.sum(dh * xg, axis=0, keepdims=True)
        dxg = dh * (1.0 + sc_ref[...])
        dg1_ref[...] += jnp.sum(dxg * xh, axis=0, keepdims=True)
        dxh = dxg * g1
        gx_ref[...] = dx2_ref[...] + r * (dxh - xh * jnp.mean(dxh * xh, axis=-1, keepdims=True))

    return pl.pallas_call(
        body, name="dh", grid=(s_len // tm,),
        out_shape=(_sds((s_len, D_MODEL), F32), _sds((1, D_MODEL), F32), _sds((1, D_MODEL), F32),
                   _sds((1, D_MODEL), F32)),
        in_specs=[_rows(tm, W_INT), _whole((W_INT, D_MODEL)), _rows(tm, D_MODEL), _rows(tm, D_MODEL),
                  _whole((1, D_MODEL)), _whole((1, D_MODEL))],
        out_specs=(_rows(tm, D_MODEL), _whole((1, D_MODEL)), _whole((1, D_MODEL)), _whole((1, D_MODEL))),
        compiler_params=_params(("arbitrary",)),
    )(dproj, w_int_t, x, dx2, scale, g1)


def _small_call(svg, ct, dmod_sh):
    def body(sv_ref, ct_ref, dm_ref, tot_ref, gwada_ref):
        acc = sv_ref[0:1, :]
        for d in range(1, N_DEV):
            acc = acc + sv_ref[d:d + 1, :]
        tot_ref[...] = acc
        gwada_ref[...] = lax.dot_general(ct_ref[...], dm_ref[...], (((1,), (0,)), ((), ())),
                                         precision=lax.Precision.HIGHEST, preferred_element_type=F32)

    vmem = pl.BlockSpec(memory_space=pltpu.VMEM)
    return pl.pallas_call(
        body, name="small_grads",
        out_shape=(_sds((1, 8 * SV_COLS), F32), _sds((D_MODEL, 768), F32)),
        in_specs=[vmem, vmem, vmem], out_specs=(vmem, vmem),
        compiler_params=_params(),
    )(svg, ct, dmod_sh)


def _adamw_tile_rows(rows, cols):
    budget = 2 << 20
    if rows * cols * 4 <= budget or rows % 8:
        return rows
    best = 8
    for tr in range(8, rows + 1, 8):
        if rows % tr == 0 and tr * cols * 4 <= budget:
            best = tr
    return best


def _adamw_call(name, w, g, m, v):
    rows, cols = w.shape
    tr = _adamw_tile_rows(rows, cols)

    def body(w_ref, g_ref, m_ref, v_ref, d_ref, nm_ref, nv_ref):
        gg = g_ref[...]
        m2 = ADAM_B1 * m_ref[...] + (1.0 - ADAM_B1) * gg
        v2 = ADAM_B2 * v_ref[...] + (1.0 - ADAM_B2) * (gg * gg)
        m_hat = m2 / (1.0 - ADAM_B1 ** ADAM_STEP)
        v_hat = v2 / (1.0 - ADAM_B2 ** ADAM_STEP)
        d_ref[...] = -ADAM_LR * (m_hat / (jnp.sqrt(v_hat) + ADAM_EPS) + ADAM_WD * w_ref[...])
        nm_ref[...] = m2
        nv_ref[...] = v2

    spec = pl.BlockSpec((tr, cols), lambda i: (i, 0))
    return pl.pallas_call(
        body, name="adamw_" + name, grid=(rows // tr,),
        out_shape=(_sds((rows, cols), F32),) * 3,
        in_specs=[spec] * 4, out_specs=(spec,) * 3,
        compiler_params=_params(("parallel",)),
    )(w, g, m, v)


IN_SHARD = IN_WIDTH // N_CHIPS
HALF_D = D_MODEL // 2
SMALL_ROWS = (576, 512, 1024, 1024, 2048)
SMALL_TOTAL = sum(SMALL_ROWS)
SMALL_HALF = SMALL_TOTAL // 2
SMALL_SUM_ROWS = 432


def _gather_call(c_row, w_ada_sh, pack_in, pack_small):
    def body(c_ref, wada_ref, pki_ref, pks_ref, mg_ref, cg_ref, gwi_ref, gws_ref,
             cv, ssem_c, rsem_c, ssem_m, rsem_m, ssem_w, rsem_w, ssem_f, rsem_f, lsem):
        x, y, c = lax.axis_index("x"), lax.axis_index("y"), lax.axis_index("c")
        me = 4 * x + 2 * y + c
        chip = 2 * x + y
        rel3 = [(1, 0), (0, 1), (1, 1)]
        packs = [(pki_ref, gwi_ref), (pks_ref, gws_ref)]

        sends = []
        for j, (dx, dy) in enumerate(rel3):
            for a, (pk, gw) in enumerate(packs):
                cp = pltpu.make_async_remote_copy(
                    src_ref=pk.at[c], dst_ref=gw.at[chip, c], send_sem=ssem_w.at[j, a], recv_sem=rsem_w.at[j, a],
                    device_id=(_flip(x, dx), _flip(y, dy), c), device_id_type=MESH)
                cp.start()
                sends.append(cp)
        owns = []
        for a, (pk, gw) in enumerate(packs):
            own = pltpu.make_async_copy(pk, gw.at[chip], lsem.at[a])
            own.start()
            owns.append(own)

        cv[me] = c_ref[...]
        for r in range(1, N_DEV):
            dx, dy, dc = (r >> 2) & 1, (r >> 1) & 1, r & 1
            cp = pltpu.make_async_remote_copy(
                src_ref=c_ref, dst_ref=cv.at[me], send_sem=ssem_c.at[r - 1], recv_sem=rsem_c.at[r - 1],
                device_id=(_flip(x, dx), _flip(y, dy), _flip(c, dc)), device_id_type=MESH)
            cp.start()
            sends.append(cp)
        for r in range(1, N_DEV):
            dx, dy, dc = (r >> 2) & 1, (r >> 1) & 1, r & 1
            src = 4 * _flip(x, dx) + 2 * _flip(y, dy) + _flip(c, dc)
            pltpu.make_async_remote_copy(
                src_ref=c_ref, dst_ref=cv.at[src], send_sem=ssem_c.at[r - 1], recv_sem=rsem_c.at[r - 1],
                device_id=(x, y, c), device_id_type=MESH).wait_recv()
        rows = lax.broadcasted_iota(jnp.int32, (N_DEV, D_MODEL), 0)
        call = jnp.zeros((N_DEV, D_MODEL), F32)
        for b in range(N_DEV):
            call = jnp.where(rows == b, jnp.broadcast_to(cv[b], (N_DEV, D_MODEL)), call)
        cg_ref[...] = call

        mg_ref[chip] = lax.dot_general(call, wada_ref[...], (((1,), (0,)), ((), ())),
                                       precision=lax.Precision.HIGHEST, preferred_element_type=F32)
        for j, (dx, dy) in enumerate(rel3):
            cp = pltpu.make_async_remote_copy(
                src_ref=mg_ref.at[chip], dst_ref=mg_ref.at[chip], send_sem=ssem_m.at[j], recv_sem=rsem_m.at[j],
                device_id=(_flip(x, dx), _flip(y, dy), c), device_id_type=MESH)
            cp.start()
            sends.append(cp)
        for j, (dx, dy) in enumerate(rel3):
            src_chip = 2 * _flip(x, dx) + _flip(y, dy)
            pltpu.make_async_remote_copy(
                src_ref=mg_ref.at[src_chip], dst_ref=mg_ref.at[src_chip], send_sem=ssem_m.at[j],
                recv_sem=rsem_m.at[j], device_id=(x, y, c), device_id_type=MESH).wait_recv()
        for j, (dx, dy) in enumerate(rel3):
            src_chip = 2 * _flip(x, dx) + _flip(y, dy)
            for a, (pk, gw) in enumerate(packs):
                pltpu.make_async_remote_copy(
                    src_ref=pk.at[c], dst_ref=gw.at[src_chip, c], send_sem=ssem_w.at[j, a],
                    recv_sem=rsem_w.at[j, a], device_id=(x, y, c), device_id_type=MESH).wait_recv()
                cp = pltpu.make_async_remote_copy(
                    src_ref=gw.at[src_chip, c], dst_ref=gw.at[src_chip, c], send_sem=ssem_f.at[j, a],
                    recv_sem=rsem_f.at[j, a], device_id=(x, y, 1 - c), device_id_type=MESH)
                cp.start()
                sends.append(cp)
        for j, (dx, dy) in enumerate(rel3):
            src_chip = 2 * _flip(x, dx) + _flip(y, dy)
            for a, (pk, gw) in enumerate(packs):
                pltpu.make_async_remote_copy(
                    src_ref=pk.at[c], dst_ref=gw.at[src_chip, 1 - c], send_sem=ssem_f.at[j, a],
                    recv_sem=rsem_f.at[j, a], device_id=(x, y, c), device_id_type=MESH).wait_recv()
        for cp in sends:
            cp.wait_send()
        for own in owns:
            own.wait()

    vmem = pl.BlockSpec(memory_space=pltpu.VMEM)
    return pl.pallas_call(
        body, name="gather_fwd",
        out_shape=(_sds((N_CHIPS, N_DEV, 768), F32), _sds((N_DEV, D_MODEL), F32),
                   _sds((N_CHIPS, 2, IN_SHARD, HALF_D), BF16), _sds((N_CHIPS, 2, SMALL_HALF, LANES), BF16)),
        in_specs=[vmem, vmem, vmem, vmem], out_specs=(vmem, vmem, vmem, vmem),
        scratch_shapes=[
            pltpu.VMEM((N_DEV, 1, D_MODEL), F32),
            pltpu.SemaphoreType.DMA((N_DEV - 1,)), pltpu.SemaphoreType.DMA((N_DEV - 1,)),
            pltpu.SemaphoreType.DMA((3,)), pltpu.SemaphoreType.DMA((3,)),
            pltpu.SemaphoreType.DMA((3, 2)), pltpu.SemaphoreType.DMA((3, 2)),
            pltpu.SemaphoreType.DMA((3, 2)), pltpu.SemaphoreType.DMA((3, 2)),
            pltpu.SemaphoreType.DMA((2,)),
        ],
        compiler_params=_params(),
    )(c_row, w_ada_sh, pack_in, pack_small)


def _reduce_call(g_in, g_small, sv):
    def body(gi_ref, gs_ref, sv_ref, fi_ref, fs_ref, svg_ref, pair_i, pair_s, send_i, send_s, land_i, land_s,
             ssem_p, rsem_p, ssem_g, rsem_g, ssem_s, rsem_s, ssem_x, rsem_x):
        x, y, c = lax.axis_index("x"), lax.axis_index("y"), lax.axis_index("c")
        me = 4 * x + 2 * y + c
        chip = 2 * x + y
        rel3 = [(1, 0), (0, 1), (1, 1)]
        payloads = [(gi_ref, pair_i, send_i, land_i, fi_ref), (gs_ref, pair_s, send_s, land_s, fs_ref)]
        copies = []

        for k in range(N_CHIPS):
            for a, (g, pair, _, _, _) in enumerate(payloads):
                cp = pltpu.make_async_remote_copy(
                    src_ref=g.at[2 * k + 1 - c], dst_ref=pair.at[k], send_sem=ssem_p.at[k, a],
                    recv_sem=rsem_p.at[k, a], device_id=(x, y, 1 - c), device_id_type=MESH)
                cp.start()
                copies.append(cp)

        for r in range(1, N_DEV):
            dx, dy, dc = (r >> 2) & 1, (r >> 1) & 1, r & 1
            cp = pltpu.make_async_remote_copy(
                src_ref=sv_ref, dst_ref=svg_ref.at[me], send_sem=ssem_s.at[r - 1], recv_sem=rsem_s.at[r - 1],
                device_id=(_flip(x, dx), _flip(y, dy), _flip(c, dc)), device_id_type=MESH)
            cp.start()
            copies.append(cp)
        svg_ref[me] = sv_ref[...]

        def pair_sum(k, store_in, store_small):
            for a, (g, pair, _, _, _) in enumerate(payloads):
                pltpu.make_async_remote_copy(
                    src_ref=g.at[2 * k + c], dst_ref=pair.at[k], send_sem=ssem_p.at[k, a],
                    recv_sem=rsem_p.at[k, a], device_id=(x, y, c), device_id_type=MESH).wait_recv()
            for qd in range(HALF_D // LANES):
                sl = slice(LANES * qd, LANES * qd + LANES)
                store_in(sl, gi_ref[2 * k + c, :, sl].astype(F32) + pair_i[k, :, sl].astype(F32))

            def rows(i, carry):
                sl = pl.ds(pl.multiple_of(i * SMALL_SUM_ROWS, 16), SMALL_SUM_ROWS)
                store_small(sl, gs_ref[2 * k + c, sl, :].astype(F32) + pair_s[k, sl, :].astype(F32))
                return carry

            lax.fori_loop(0, SMALL_HALF // SMALL_SUM_ROWS, rows, 0)

        for j, (dx, dy) in enumerate(rel3):
            tx, ty = _flip(x, dx), _flip(y, dy)

            def put_in(sl, val, j=j):
                send_i[j, :, sl] = val.astype(BF16)

            def put_small(sl, val, j=j):
                send_s[j, sl, :] = val.astype(BF16)

            pair_sum(2 * tx + ty, put_in, put_small)
            for a, (_, _, send, land, _) in enumerate(payloads):
                cp = pltpu.make_async_remote_copy(
                    src_ref=send.at[j], dst_ref=land.at[j], send_sem=ssem_g.at[j, a], recv_sem=rsem_g.at[j, a],
                    device_id=(tx, ty, c), device_id_type=MESH)
                cp.start()
                copies.append(cp)

        def own_in(sl, val):
            fi_ref[c, :, sl] = val

        def own_small(sl, val):
            fs_ref[c, sl, :] = val

        pair_sum(chip, own_in, own_small)
        for j in range(3):
            for a, (_, _, send, land, _) in enumerate(payloads):
                pltpu.make_async_remote_copy(
                    src_ref=send.at[j], dst_ref=land.at[j], send_sem=ssem_g.at[j, a], recv_sem=rsem_g.at[j, a],
                    device_id=(x, y, c), device_id_type=MESH).wait_recv()
            for qd in range(HALF_D // LANES):
                sl = slice(LANES * qd, LANES * qd + LANES)
                fi_ref[c, :, sl] += land_i[j, :, sl].astype(F32)

            def add_rows(i, carry, j=j):
                sl = pl.ds(pl.multiple_of(i * SMALL_SUM_ROWS, 16), SMALL_SUM_ROWS)
                fs_ref[c, sl, :] += land_s[j, sl, :].astype(F32)
                return carry

            lax.fori_loop(0, SMALL_HALF // SMALL_SUM_ROWS, add_rows, 0)

        for a, f in enumerate((fi_ref, fs_ref)):
            cp = pltpu.make_async_remote_copy(
                src_ref=f.at[c], dst_ref=f.at[c], send_sem=ssem_x.at[a], recv_sem=rsem_x.at[a],
                device_id=(x, y, 1 - c), device_id_type=MESH)
            cp.start()
            copies.append(cp)
        for a, f in enumerate((fi_ref, fs_ref)):
            pltpu.make_async_remote_copy(
                src_ref=f.at[c], dst_ref=f.at[1 - c], send_sem=ssem_x.at[a], recv_sem=rsem_x.at[a],
                device_id=(x, y, c), device_id_type=MESH).wait_recv()
        for r in range(1, N_DEV):
            dx, dy, dc = (r >> 2) & 1, (r >> 1) & 1, r & 1
            src = 4 * _flip(x, dx) + 2 * _flip(y, dy) + _flip(c, dc)
            pltpu.make_async_remote_copy(
                src_ref=sv_ref, dst_ref=svg_ref.at[src], send_sem=ssem_s.at[r - 1],
                recv_sem=rsem_s.at[r - 1], device_id=(x, y, c), device_id_type=MESH).wait_recv()
        for cp in copies:
            cp.wait_send()

    vmem = pl.BlockSpec(memory_space=pltpu.VMEM)
    return pl.pallas_call(
        body, name="grad_reduce",
        out_shape=(_sds((2, IN_SHARD, HALF_D), F32), _sds((2, SMALL_HALF, LANES), F32),
                   _sds((N_DEV, 8, SV_COLS), F32)),
        in_specs=[vmem, vmem, vmem], out_specs=(vmem, vmem, vmem),
        scratch_shapes=[
            pltpu.VMEM((N_CHIPS, IN_SHARD, HALF_D), BF16), pltpu.VMEM((N_CHIPS, SMALL_HALF, LANES), BF16),
            pltpu.VMEM((3, IN_SHARD, HALF_D), BF16), pltpu.VMEM((3, SMALL_HALF, LANES), BF16),
            pltpu.VMEM((3, IN_SHARD, HALF_D), BF16), pltpu.VMEM((3, SMALL_HALF, LANES), BF16),
            pltpu.SemaphoreType.DMA((N_CHIPS, 2)), pltpu.SemaphoreType.DMA((N_CHIPS, 2)),
            pltpu.SemaphoreType.DMA((3, 2)), pltpu.SemaphoreType.DMA((3, 2)),
            pltpu.SemaphoreType.DMA((N_DEV - 1,)), pltpu.SemaphoreType.DMA((N_DEV - 1,)),
            pltpu.SemaphoreType.DMA((2,)), pltpu.SemaphoreType.DMA((2,)),
        ],
        compiler_params=_params(),
    )(g_in, g_small, sv)


def _dwin_call(h, dproj):
    s_len = h.shape[0]
    tm = min(4 * ROW_TILE, s_len)
    nrow = s_len // tm
    nc = 4
    chunk = W_INT // nc

    def body(h_ref, dp_ref, dw_ref, acc):
        i = pl.program_id(1)

        @pl.when(i == 0)
        def _():
            acc[...] = jnp.zeros_like(acc)

        acc[...] += _dot_tn(dp_ref[...], h_ref[...])

        @pl.when(i == nrow - 1)
        def _():
            dw_ref[...] = acc[...].astype(BF16)

    return pl.pallas_call(
        body, name="dwin", grid=(nc, nrow),
        out_shape=_sds((W_INT, D_MODEL), BF16),
        in_specs=[pl.BlockSpec((tm, D_MODEL), lambda c, i: (i, 0)),
                  pl.BlockSpec((tm, chunk), lambda c, i: (i, c))],
        out_specs=pl.BlockSpec((chunk, D_MODEL), lambda c, i: (c, 0)),
        scratch_shapes=[pltpu.VMEM((chunk, D_MODEL), F32)],
        compiler_params=_params(("parallel", "arbitrary")),
    )(h, dproj)


def _swap_rows(w, group):
    r, n = w.shape
    return w.reshape(r // group, 2, group // 2, n)[:, ::-1].reshape(r, n)


def _internal_weights(w_in_t, w_uq, w_ukv):
    krot_t = w_in_t[2688:2720]
    w_int_t = jnp.concatenate([
        w_in_t[0:2688], w_in_t[2720:5280],
        jnp.tile(krot_t, (4, 1)), jnp.tile(_swap_rows(krot_t, 32), (4, 1)),
        jnp.zeros((W_INT - O_END, D_MODEL), w_in_t.dtype)], axis=0)
    uq = w_uq.reshape(Q_RANK, N_HEADS, 96)
    wp = uq[:, :, 64:].reshape(Q_RANK, 256)
    w_q = jnp.concatenate([uq[:, :, :64].reshape(Q_RANK, 512), wp, _swap_halves(wp, 32)], axis=1)
    ukv = w_ukv.reshape(KV_RANK, N_HEADS, 128)
    w_kv = jnp.concatenate([ukv[:, :, :64].reshape(KV_RANK, 512), ukv[:, :, 64:].reshape(KV_RANK, 512)], axis=1)
    return w_int_t, w_q, w_kv


def _true_weight_grads(dwi_t, dwq, dwkv):
    dkr = dwi_t[O_KR:O_KR + 128].astype(F32).reshape(4, 32, D_MODEL).sum(axis=0)
    dkr_sw = dwi_t[O_KR + 128:O_END].astype(F32).reshape(4, 32, D_MODEL).sum(axis=0)
    dkrot_t = (dkr + _swap_rows(dkr_sw, 32)).astype(dwi_t.dtype)
    g_in_t = jnp.concatenate([dwi_t[0:O_MZ], dkrot_t, dwi_t[O_MZ:O_KR]], axis=0)
    dwp = dwq[:, 512:768] + _swap_halves(dwq[:, 768:1024], 32)
    g_uq = jnp.concatenate([dwq[:, :512].reshape(Q_RANK, N_HEADS, 64), dwp.reshape(Q_RANK, N_HEADS, 32)],
                           axis=2).reshape(Q_RANK, 768)
    g_ukv = jnp.concatenate([dwkv[:, :512].reshape(KV_RANK, N_HEADS, 64), dwkv[:, 512:].reshape(KV_RANK, N_HEADS, 64)],
                            axis=2).reshape(KV_RANK, 1024)
    return g_in_t, g_uq, g_ukv


def _swap_halves(w, group):
    r, n = w.shape
    return w.reshape(r, n // group, 2, group // 2)[:, :, ::-1, :].reshape(r, n)


def _pack_shards(parts):
    return jnp.concatenate([p.reshape(-1, LANES) for p in parts], axis=0)


def _unpack_small(gw):
    offs = [0]
    for r in SMALL_ROWS:
        offs.append(offs[-1] + r)

    def cols(i, rows, shard_cols):
        blk = gw[:, offs[i]:offs[i + 1]].reshape(N_CHIPS, rows, shard_cols)
        return blk.transpose(1, 0, 2).reshape(rows, N_CHIPS * shard_cols)

    return (cols(0, Q_RANK, 192), cols(1, KV_RANK, 256), cols(2, 512, 256), cols(3, 512, 256),
            gw[:, offs[4]:offs[5]].reshape(D_MODEL, D_MODEL))


def _chip_major(g, shard_cols):
    r = g.shape[0]
    return g.reshape(r, N_CHIPS, shard_cols).transpose(1, 0, 2).reshape(N_CHIPS, -1, LANES)


def kernel(x, c, positions, w_ada, b_ada, norm_gain, w_in, q_norm_gain, w_uq, kv_norm_gain, w_ukv, w_branch_a, w_branch_b, w_out, final_norm_gain, loss_target, m_w_ada, m_b_ada, m_norm_gain, m_w_in, m_q_norm_gain, m_w_uq, m_kv_norm_gain, m_w_ukv, m_w_branch_a, m_w_branch_b, m_w_out, m_final_norm_gain, v_w_ada, v_b_ada, v_norm_gain, v_w_in, v_q_norm_gain, v_w_uq, v_kv_norm_gain, v_w_ukv, v_w_branch_a, v_w_branch_b, v_w_out, v_final_norm_gain):
    ix, iy, ic = lax.axis_index("x"), lax.axis_index("y"), lax.axis_index("c")
    me = 4 * ix + 2 * iy + ic
    chip = 2 * ix + iy
    xs = x[0]
    tgt = loss_target[0]
    s_len = xs.shape[0]

    w_in_t = jnp.swapaxes(w_in[0], 0, 1)
    w_in_tb = w_in_t.astype(BF16)
    pack_in = jnp.stack([w_in_tb[:, :HALF_D], w_in_tb[:, HALF_D:]], axis=0)
    small_shards = (w_uq[0], w_ukv[0], w_branch_a[0], w_branch_b[0], w_out[0])
    pack_small = _pack_shards([s.astype(BF16) for s in small_shards]).reshape(2, SMALL_HALF, LANES)
    mg, call, gw_in, gw_small = _gather_call(c, w_ada[0], pack_in, pack_small)
    mod = mg.transpose(1, 0, 2).reshape(N_DEV, 3 * D_MODEL) + b_ada
    mod_me = lax.dynamic_slice_in_dim(mod, me, 1, axis=0)
    shift, scale, gate = mod_me[:, :D_MODEL], mod_me[:, D_MODEL:2 * D_MODEL], mod_me[:, 2 * D_MODEL:]

    f_in_t = jnp.concatenate([gw_in[:, 0], gw_in[:, 1]], axis=2).reshape(IN_WIDTH, D_MODEL)
    f_uq, f_ukv, f_a, f_b, f_out = _unpack_small(gw_small.reshape(N_CHIPS, SMALL_TOTAL, LANES))
    w_int_t, w_q, w_kv = _internal_weights(f_in_t, f_uq, f_ukv)

    inv_freq = ROPE_BASE ** (-jnp.arange(0, ROPE_DIM, 2, dtype=F32) / ROPE_DIM)
    ang = positions[0].astype(F32)[:, None] * inv_freq
    cs, sn = jnp.cos(ang), jnp.sin(ang)
    cos256 = jnp.tile(jnp.concatenate([cs, cs], axis=1), (1, 8))
    sin256 = jnp.tile(jnp.concatenate([-sn, sn], axis=1), (1, 8))

    (h, sq, sk, sv, sz, cq, ckv, mz, ga, gb, kpt, qn, qp, kn, vv) = _inproj_call(
        xs, shift, scale, norm_gain, w_int_t, w_q, w_kv, q_norm_gain, kv_norm_gain, cos256, sin256)
    oa, lt, first = _sb_fwd_call(sq, sk, sv)
    ob, lse = _mla_fwd_call(qn, qp, kn, kpt, vv)

    gf = final_norm_gain.reshape(1, D_MODEL)
    (dx2, doa, dob, dsz, dmz, dga, dgb, dwo, dwa, dwb, dgf, dgate, loss_p) = _post_call(
        xs, tgt, oa, ob, sz, mz, ga, gb, gate, gf, f_a, f_b, f_out)

    dsq, dsk_t, dsv_t = _sb_bwd_call(first[:, :, 0, 0].reshape(-1), sq, sk, sv, doa, lt)
    dqn, dqp, dkn_t, dkpt_t, dvv_t = _mla_bwd_call(qn, qp, kn, kpt, vv, ob, dob, lse)

    dproj, dwq, dwkv, dqg, dkvg = _bwdprep_call(
        dsq, dsk_t, dsv_t, dsz, dqn, dqp, dkn_t, dvv_t, dkpt_t, dmz, dga, dgb, cq, ckv, cos256, sin256,
        q_norm_gain, kv_norm_gain, w_q, w_kv)
    grad_x, dshift, dscale, dg1 = _dh_call(dproj, w_int_t, xs, dx2, scale, norm_gain)
    dwi_t = _dwin_call(h, dproj)
    g_in_t, g_uq, g_ukv = _true_weight_grads(dwi_t, dwq, dwkv)

    g_in_c = g_in_t.reshape(N_CHIPS, IN_SHARD, D_MODEL)
    g_in_pieces = jnp.stack([g_in_c[:, :, :HALF_D], g_in_c[:, :, HALF_D:]], axis=1).reshape(N_DEV, IN_SHARD, HALF_D)
    g_small = jnp.concatenate([
        _chip_major(g_uq, 192), _chip_major(g_ukv, 256), _chip_major(dwa, 256), _chip_major(dwb, 256),
        dwo.reshape(N_CHIPS, -1, LANES)], axis=1).astype(BF16).reshape(N_DEV, SMALL_HALF, LANES)
    small = jnp.concatenate([
        dshift, dscale, dgate, dg1, dqg, dkvg, dgf, loss_p,
        jnp.zeros((1, 8 * SV_COLS - 5888), F32)], axis=1).reshape(8, SV_COLS)
    full_in, full_small, svg = _reduce_call(g_in_pieces, g_small, small)
    gs_in_t = jnp.concatenate([full_in[0], full_in[1]], axis=1)
    full = full_small.reshape(SMALL_TOTAL, LANES)
    offs = [0]
    for r in SMALL_ROWS:
        offs.append(offs[-1] + r)
    gs_uq = full[offs[0]:offs[1]].reshape(Q_RANK, 192)
    gs_ukv = full[offs[1]:offs[2]].reshape(KV_RANK, 256)
    gs_a = full[offs[2]:offs[3]].reshape(512, 256)
    gs_b = full[offs[3]:offs[4]].reshape(512, 256)
    gs_out = full[offs[4]:offs[5]].reshape(256, D_MODEL)

    svm = svg.reshape(N_DEV, 8 * SV_COLS)
    dmod_sh = lax.dynamic_slice_in_dim(svm[:, :3 * D_MODEL], chip * 768, 768, axis=1)
    tot, gs_ada = _small_call(svm, call.T, dmod_sh)
    g_bada = tot[:, 0:3072]
    g_g1 = tot[:, 3072:4096]
    g_qg = tot[:, 4096:4480]
    g_kvg = tot[:, 4480:4736]
    g_gf = tot[:, 4736:5760]
    loss = tot[0, 5760]

    names = ["w_ada", "b_ada", "norm_gain", "w_in", "q_norm_gain", "w_uq", "kv_norm_gain", "w_ukv",
             "w_branch_a", "w_branch_b", "w_out", "final_norm_gain"]
    ws = [w_ada[0], b_ada, norm_gain, w_in_t, q_norm_gain, w_uq[0], kv_norm_gain, w_ukv[0],
          w_branch_a[0], w_branch_b[0], w_out[0], final_norm_gain.reshape(1, D_MODEL)]
    gs = [gs_ada, g_bada, g_g1, gs_in_t, g_qg, gs_uq, g_kvg, gs_ukv, gs_a, gs_b, gs_out, g_gf]
    ms = [m_w_ada[0], m_b_ada, m_norm_gain, jnp.swapaxes(m_w_in[0], 0, 1), m_q_norm_gain, m_w_uq[0],
          m_kv_norm_gain, m_w_ukv[0], m_w_branch_a[0], m_w_branch_b[0], m_w_out[0],
          m_final_norm_gain.reshape(1, D_MODEL)]
    vs = [v_w_ada[0], v_b_ada, v_norm_gain, jnp.swapaxes(v_w_in[0], 0, 1), v_q_norm_gain, v_w_uq[0],
          v_kv_norm_gain, v_w_ukv[0], v_w_branch_a[0], v_w_branch_b[0], v_w_out[0],
          v_final_norm_gain.reshape(1, D_MODEL)]
    refs = [w_ada, b_ada, norm_gain, w_in, q_norm_gain, w_uq, kv_norm_gain, w_ukv,
            w_branch_a, w_branch_b, w_out, final_norm_gain]
    grads, deltas, new_ms, new_vs = [], [], [], []
    for n, w_, g_, m_, v_, ref in zip(names, ws, gs, ms, vs, refs):
        outs = (g_,) + _adamw_call(n, w_, g_, m_, v_)
        if n == "w_in":
            outs = tuple(jnp.swapaxes(o_, 0, 1) for o_ in outs)
        for lst, o_ in zip((grads, deltas, new_ms, new_vs), outs):
            lst.append(o_.reshape(ref.shape))

    return (loss, grad_x.reshape(x.shape), *grads, *deltas, *new_ms, *new_vs)
```

```python
import math

import jax
import jax.numpy as jnp
from jax import lax
from jax.experimental import pallas as pl
from jax.experimental.pallas import tpu as pltpu

F32 = jnp.float32
BF16 = jnp.bfloat16

D_MODEL = 1024
SB_WIDTH = 512
MLA_WIDTH = 512
Q_RANK = 384
KV_RANK = 256
ROPE_DIM = 32
N_HEADS = 8
IN_WIDTH = 5280
EPS = 1e-6
ROPE_BASE = 10000.0
MLA_SCALE = 1.0 / math.sqrt(96.0)
SB_SCALE = 0.125
LOG2E = 1.4426950408889634

ADAM_LR = 0.001
ADAM_B1 = 0.9
ADAM_B2 = 0.999
ADAM_EPS = 1e-08
ADAM_WD = 0.01
ADAM_STEP = 10

O_SQ, O_SK, O_SV, O_SZ, O_CQ, O_CKV, O_MZ, O_GA, O_GB, O_KR, O_END = (
    0, 512, 1024, 1536, 2048, 2432, 2688, 3200, 4224, 5248, 5504)
W_INT = 5632

N_CHIPS = 4
N_DEV = 8
LANES = 128
SV_COLS = 768

ROW_TILE = 256
ATT_TILE = 256
ATT_Q_TILES = 2
FWD_Q_TILES = 4
SB_Q_TILES = 1
MLA_KEY_TILE = 512
VMEM_LIMIT = 56 * 1024 * 1024

MESH = pl.DeviceIdType.MESH


def _dot(a, b):
    return lax.dot_general(a, b, (((1,), (0,)), ((), ())), preferred_element_type=F32)


def _dot_nt(a, b):
    return lax.dot_general(a, b, (((1,), (1,)), ((), ())), preferred_element_type=F32)


def _dot_tn(a, b):
    return lax.dot_general(a, b, (((0,), (0,)), ((), ())), preferred_element_type=F32)


def _sigmoid(z):
    return 1.0 / (1.0 + jnp.exp2(z * (-LOG2E)))


def _params(sem=None):
    if sem is None:
        return pltpu.CompilerParams(vmem_limit_bytes=VMEM_LIMIT)
    return pltpu.CompilerParams(dimension_semantics=sem, vmem_limit_bytes=VMEM_LIMIT)


def _rows(tm, n):
    return pl.BlockSpec((tm, n), lambda i: (i, 0))


def _cols(n, tm):
    return pl.BlockSpec((n, tm), lambda i: (0, i))


def _whole(shape):
    nd = len(shape)
    return pl.BlockSpec(shape, lambda i: (0,) * nd)


def _sds(shape, dtype):
    return jax.ShapeDtypeStruct(shape, dtype)


def _flip(v, d):
    return 1 - v if d else v


def _inproj_call(x, shift, scale, g1, w_int, w_q, w_kv, qg, kvg, cos256, sin256):
    s_len = x.shape[0]
    tm = min(ROW_TILE, s_len)

    def body(x_ref, sh_ref, sc_ref, g1_ref, w_ref, wq_ref, wkv_ref, qg_ref, kvg_ref, cos_ref, sin_ref,
             h_ref, sq_ref, sk_ref, sv_ref, sz_ref, cq_ref, ckv_ref, mz_ref, ga_ref, gb_ref, kpt_ref,
             qn_ref, qp_ref, kn_ref, vv_ref):
        xt = x_ref[...]
        r = lax.rsqrt(jnp.mean(xt * xt, axis=-1, keepdims=True) + EPS)
        h = (xt * r * g1_ref[...]) * (1.0 + sc_ref[...]) + sh_ref[...]
        hb = h.astype(BF16)
        h_ref[...] = hb

        def seg(a, b):
            return _dot_nt(hb, w_ref[a:b, :])

        sq_ref[...] = (seg(O_SQ, O_SK) * SB_SCALE).astype(BF16)
        sk_ref[...] = seg(O_SK, O_SV).astype(BF16)
        sv_ref[...] = seg(O_SV, O_SZ).astype(BF16)
        sz_ref[...] = seg(O_SZ, O_CQ)
        mz_ref[...] = seg(O_MZ, O_GA)
        ga_ref[...] = seg(O_GA, O_GB)
        gb_ref[...] = seg(O_GB, O_KR)
        cos = cos_ref[...]
        sin = sin_ref[...]
        kr = seg(O_KR, O_END)
        kpt_ref[...] = (kr[:, :128] * cos[:, :128] + kr[:, 128:] * sin[:, :128]).astype(BF16)

        cq = seg(O_CQ, O_CKV)
        cq_ref[...] = cq
        rq = lax.rsqrt(jnp.mean(cq * cq, axis=-1, keepdims=True) + EPS)
        cqn = (cq * rq * qg_ref[...]).astype(BF16)
        qa = _dot(cqn, wq_ref[...])
        qn_ref[...] = qa[:, :512].astype(BF16)
        qp_ref[...] = (qa[:, 512:768] * cos + qa[:, 768:] * sin).astype(BF16)

        ckv = seg(O_CKV, O_MZ)
        ckv_ref[...] = ckv
        rk = lax.rsqrt(jnp.mean(ckv * ckv, axis=-1, keepdims=True) + EPS)
        ckvn = (ckv * rk * kvg_ref[...]).astype(BF16)
        kva = _dot(ckvn, wkv_ref[...])
        kn_ref[...] = kva[:, :512].astype(BF16)
        vv_ref[...] = kva[:, 512:].astype(BF16)

    outs = [
        (D_MODEL, BF16), (512, BF16), (512, BF16), (512, BF16), (512, F32), (Q_RANK, F32), (KV_RANK, F32),
        (512, F32), (D_MODEL, F32), (D_MODEL, F32), (128, BF16), (512, BF16), (256, BF16), (512, BF16), (512, BF16),
    ]
    return pl.pallas_call(
        body, name="inproj", grid=(s_len // tm,),
        out_shape=tuple(_sds((s_len, n), dt) for n, dt in outs),
        in_specs=[_rows(tm, D_MODEL), _whole((1, D_MODEL)), _whole((1, D_MODEL)), _whole((1, D_MODEL)),
                  _whole((W_INT, D_MODEL)), _whole((Q_RANK, 1024)), _whole((KV_RANK, 1024)),
                  _whole((1, Q_RANK)), _whole((1, KV_RANK)), _rows(tm, 256), _rows(tm, 256)],
        out_specs=tuple(_rows(tm, n) for n, _ in outs),
        compiler_params=_params(("parallel",)),
    )(x, shift, scale, g1, w_int, w_q, w_kv, qg, kvg, cos256, sin256)


Z_CLAMP = 80.0 * LOG2E
RUN_CUTOFF = 110.0 * LOG2E


def _softplus_clamped(z):
    zc = jnp.minimum(z * LOG2E, Z_CLAMP)
    return zc, jnp.log2(1.0 + jnp.exp2(zc))


def _tri_sum(a, tri):
    return _dot(a.astype(BF16), tri)


def _sb_fwd_call(q, k, v):
    s_len = q.shape[0]
    tk = min(ATT_TILE, s_len)
    tq = min(SB_Q_TILES * ATT_TILE, s_len)
    r = tq // tk
    nq = s_len // tq

    def body(q_ref, k_ref, v_ref, o_ref, lt_ref, first_ref):
        i = pl.program_id(1)
        q2 = q_ref[...]
        lane = lax.broadcasted_iota(jnp.int32, (1, 256), 1)
        krow = lax.broadcasted_iota(jnp.int32, (tk, tk), 0)
        kcol = lax.broadcasted_iota(jnp.int32, (tk, tk), 1)
        row = lax.broadcasted_iota(jnp.int32, (tq, tk), 0)
        col = lax.broadcasted_iota(jnp.int32, (tq, tk), 1)
        later = (krow > kcol).astype(BF16)
        valids = [col + u * tk < row for u in range(r)]
        hms = [(lane // 64) == hh for hh in range(4)]
        qms = [jnp.where(hm, q2, jnp.zeros_like(q2)) for hm in hms]

        def block(j, carry, valid):
            runs, acc = list(carry[:4]), carry[4]
            off = pl.multiple_of(j * tk, tk)
            kb = k_ref[pl.ds(off, tk), :]
            vb = v_ref[pl.ds(off, tk), :]
            ws = []
            for hh in range(4):
                zc, sp = _softplus_clamped(_dot_nt(qms[hh], kb))
                lm = jnp.where(valid, sp, 0.0) if valid is not None else sp
                suf = _tri_sum(lm, later)
                w = jnp.exp2(zc - sp - suf - runs[hh])
                if valid is not None:
                    w = jnp.where(valid, w, 0.0)
                ws.append(w.astype(BF16))
                runs[hh] = runs[hh] + jnp.sum(lm, axis=1, keepdims=True)
            vstack = jnp.concatenate([jnp.where(hm, vb, jnp.zeros_like(vb)) for hm in hms], axis=0)
            acc = acc + _dot(jnp.concatenate(ws, axis=1), vstack)
            return (*runs, acc)

        zero = jnp.zeros((tq, 1), F32)
        carry = (zero, zero, zero, zero, jnp.zeros((tq, 256), F32))
        for u in reversed(range(r)):
            carry = block(i * r + u, carry, valids[u])

        def least_run(runs):
            return jnp.min(jnp.minimum(jnp.minimum(runs[0], runs[1]), jnp.minimum(runs[2], runs[3])))

        n_full = i * r

        def unfinished(state):
            return jnp.logical_and(state[0] < n_full, state[1] <= RUN_CUTOFF)

        def visit(state):
            cr = block(n_full - 1 - state[0], state[2:], None)
            return (state[0] + 1, least_run(cr[:4]), *cr)

        state = lax.while_loop(unfinished, visit, (jnp.int32(0), least_run(carry[:4]), *carry))
        carry = state[2:]
        first_ref[...] = jnp.full(first_ref.shape, n_full - state[0], jnp.int32)
        for hh in range(4):
            lt_ref[0, :, hh:hh + 1] = carry[hh]
        o_ref[...] = carry[4]

    return pl.pallas_call(
        body, name="sb_fwd", grid=(2, nq),
        out_shape=(_sds((s_len, SB_WIDTH), F32), _sds((2, s_len, 4), F32), _sds((2, nq, 8, 128), jnp.int32)),
        in_specs=[pl.BlockSpec((tq, 256), lambda g, i: (i, g)),
                  pl.BlockSpec((s_len, 256), lambda g, i: (0, g)),
                  pl.BlockSpec((s_len, 256), lambda g, i: (0, g))],
        out_specs=(pl.BlockSpec((tq, 256), lambda g, i: (i, g)),
                   pl.BlockSpec((1, tq, 4), lambda g, i: (g, i, 0)),
                   pl.BlockSpec((1, 1, 8, 128), lambda g, i: (g, i, 0, 0))),
        compiler_params=_params(("parallel", "parallel")),
    )(q, k, v)


def _sb_bwd_call(first, q, k, v, do, lt):
    s_len = q.shape[0]
    tk = min(ATT_TILE, s_len)
    tq = min(SB_Q_TILES * ATT_TILE, s_len)
    r = tq // tk
    nq = s_len // tq
    nq_fwd = first.shape[0] // 2
    per_fwd = nq // nq_fwd

    def body(first_ref, q_ref, k_ref, v_ref, do_ref, lt_ref, dq_ref, dk_ref, dv_ref):
        g = pl.program_id(0)
        i = pl.program_id(1)

        @pl.when(i == 0)
        def _():
            dk_ref[...] = jnp.zeros_like(dk_ref)
            dv_ref[...] = jnp.zeros_like(dv_ref)

        q2 = q_ref[...]
        do2 = do_ref[...].astype(BF16)
        lane = lax.broadcasted_iota(jnp.int32, (1, 256), 1)
        krow = lax.broadcasted_iota(jnp.int32, (tk, tk), 0)
        kcol = lax.broadcasted_iota(jnp.int32, (tk, tk), 1)
        row = lax.broadcasted_iota(jnp.int32, (tq, tk), 0)
        col = lax.broadcasted_iota(jnp.int32, (tq, tk), 1)
        earlier = (krow < kcol).astype(BF16)
        later = (krow > kcol).astype(BF16)
        valids = [col + u * tk < row for u in range(r)]
        hms = [(lane // 64) == hh for hh in range(4)]
        qms = [jnp.where(hm, q2, jnp.zeros_like(q2)) for hm in hms]
        doms = [jnp.where(hm, do2, jnp.zeros_like(do2)) for hm in hms]
        ltots = [lt_ref[0, :, hh:hh + 1] for hh in range(4)]
        q2t = jnp.transpose(q2.astype(F32))
        do2t = jnp.transpose(do_ref[...])
        subl = lax.broadcasted_iota(jnp.int32, (256, 1), 0)
        qtstack = jnp.concatenate(
            [jnp.where((subl // 64) == hh, q2t, 0.0).astype(BF16) for hh in range(4)], axis=1)
        dotstack = jnp.concatenate(
            [jnp.where((subl // 64) == hh, do2t, 0.0).astype(BF16) for hh in range(4)], axis=1)

        def block(j, carry, valid):
            lpre, ppre, dq = list(carry[0:4]), list(carry[4:8]), carry[8]
            off = pl.multiple_of(j * tk, tk)
            kb = k_ref[pl.ds(off, tk), :]
            vb = v_ref[pl.ds(off, tk), :]
            dzs, avs = [], []
            for hh in range(4):
                zc, sp = _softplus_clamped(_dot_nt(qms[hh], kb))
                lsig = zc - sp
                lm = jnp.where(valid, sp, 0.0) if valid is not None else sp
                rowsum = jnp.sum(lm, axis=1, keepdims=True)
                between = _tri_sum(lm, later) + ((ltots[hh] - lpre[hh]) - rowsum)
                a = jnp.exp2(lsig - between)
                if valid is not None:
                    a = jnp.where(valid, a, 0.0)
                p = a * _dot_nt(doms[hh], vb)
                pbefore = ppre[hh] + _tri_sum(p, earlier)
                dz = p - jnp.exp2(lsig) * (p + pbefore)
                if valid is not None:
                    dz = jnp.where(valid, dz, 0.0)
                dzs.append(dz.astype(BF16))
                avs.append(a.astype(BF16))
                lpre[hh] = lpre[hh] + rowsum
                ppre[hh] = ppre[hh] + jnp.sum(p, axis=1, keepdims=True)
            kstack = jnp.concatenate([jnp.where(hm, kb, jnp.zeros_like(kb)) for hm in hms], axis=0)
            dq = dq + _dot(jnp.concatenate(dzs, axis=1), kstack)
            dk_ref[:, pl.ds(off, tk)] += _dot(qtstack, jnp.concatenate(dzs, axis=0))
            dv_ref[:, pl.ds(off, tk)] += _dot(dotstack, jnp.concatenate(avs, axis=0))
            return (*lpre, *ppre, dq)

        zero = jnp.zeros((tq, 1), F32)
        start = jnp.minimum(first_ref[g * nq_fwd + i // per_fwd], i * r)
        carry = lax.fori_loop(start, i * r, lambda j, cr: block(j, cr, None),
                              (zero,) * 8 + (jnp.zeros((tq, 256), F32),))
        for u in range(r):
            carry = block(i * r + u, carry, valids[u])
        dq_ref[...] = carry[8].astype(BF16)

    return pl.pallas_call(
        body, name="sb_bwd",
        out_shape=(_sds((s_len, SB_WIDTH), BF16), _sds((SB_WIDTH, s_len), F32), _sds((SB_WIDTH, s_len), F32)),
        grid_spec=pltpu.PrefetchScalarGridSpec(
            num_scalar_prefetch=1, grid=(2, nq),
            in_specs=[pl.BlockSpec((tq, 256), lambda g, i, f: (i, g)),
                      pl.BlockSpec((s_len, 256), lambda g, i, f: (0, g)),
                      pl.BlockSpec((s_len, 256), lambda g, i, f: (0, g)),
                      pl.BlockSpec((tq, 256), lambda g, i, f: (i, g)),
                      pl.BlockSpec((1, tq, 4), lambda g, i, f: (g, i, 0))],
            out_specs=(pl.BlockSpec((tq, 256), lambda g, i, f: (i, g)),
                       pl.BlockSpec((256, s_len), lambda g, i, f: (g, 0)),
                       pl.BlockSpec((256, s_len), lambda g, i, f: (g, 0)))),
        compiler_params=_params(("parallel", "arbitrary")),
    )(first, q, k, v, do, lt)


def _mla_fwd_call(qn, qp, kn, kpt, v):
    s_len = qn.shape[0]
    tk = min(MLA_KEY_TILE, s_len)
    tq = min(FWD_Q_TILES * ATT_TILE, s_len)
    r = tq // tk
    nq = s_len // tq

    def body(qn_ref, qp_ref, kn_ref, kpt_ref, v_ref, o_ref, lse_ref):
        i = pl.program_id(1)
        qn2 = qn_ref[...]
        qp2 = qp_ref[...]
        lane256 = lax.broadcasted_iota(jnp.int32, (1, 256), 1)
        lane128 = lax.broadcasted_iota(jnp.int32, (1, 128), 1)
        krow = lax.broadcasted_iota(jnp.int32, (tk, tk), 0)
        kcol = lax.broadcasted_iota(jnp.int32, (tk, tk), 1)
        row = lax.broadcasted_iota(jnp.int32, (tq, tk), 0)
        col = lax.broadcasted_iota(jnp.int32, (tq, tk), 1)
        valids = [col + u * tk <= row for u in range(r)]
        m64s = [(lane256 // 64) == hh for hh in range(4)]
        half = [(lane128 // 64) == u for u in range(2)]
        m32s = [(lane128 // 32) == hh for hh in range(4)]
        qcs = []
        for hh in range(4):
            qpair = qn2[:, 128 * (hh // 2):128 * (hh // 2) + 128]
            qcs.append(jnp.concatenate([jnp.where(half[hh % 2], qpair, jnp.zeros_like(qpair)),
                                        jnp.where(m32s[hh], qp2, jnp.zeros_like(qp2))], axis=1))

        def by_head(vals):
            return jnp.where(m64s[0], vals[0], jnp.where(m64s[1], vals[1], jnp.where(m64s[2], vals[2], vals[3])))

        def block(j, carry, valid):
            ms, ls, acc = list(carry[0:4]), list(carry[4:8]), carry[8]
            off = pl.multiple_of(j * tk, tk)
            knb = kn_ref[pl.ds(off, tk), :]
            kpb = kpt_ref[pl.ds(off, tk), :]
            vb = v_ref[pl.ds(off, tk), :]
            kcs = [jnp.concatenate([knb[:, 128 * pp:128 * pp + 128], kpb], axis=1) for pp in range(2)]
            ps, alphas = [], []
            for hh in range(4):
                s = _dot_nt(qcs[hh], kcs[hh // 2]) * (MLA_SCALE * LOG2E)
                if valid is not None:
                    s = jnp.where(valid, s, -1e30)
                mn = jnp.maximum(ms[hh], jnp.max(s, axis=1, keepdims=True))
                p = jnp.exp2(s - mn)
                alpha = jnp.exp2(ms[hh] - mn)
                ls[hh] = alpha * ls[hh] + jnp.sum(p, axis=1, keepdims=True)
                ms[hh] = mn
                ps.append(p.astype(BF16))
                alphas.append(alpha)
            pvs = []
            for pp in range(2):
                vpair = vb[:, 128 * pp:128 * pp + 128]
                vstack = jnp.concatenate([jnp.where(hf, vpair, jnp.zeros_like(vpair)) for hf in half], axis=0)
                pvs.append(_dot(jnp.concatenate(ps[2 * pp:2 * pp + 2], axis=1), vstack))
            acc = by_head(alphas) * acc + jnp.concatenate(pvs, axis=1)
            return (*ms, *ls, acc)

        neg = jnp.full((tq, 1), -1e30, F32)
        zero = jnp.zeros((tq, 1), F32)
        carry = lax.fori_loop(0, i * r, lambda j, cr: block(j, cr, None),
                              (neg,) * 4 + (zero,) * 4 + (jnp.zeros((tq, 256), F32),))
        for u in range(r):
            carry = block(i * r + u, carry, valids[u])
        o_ref[...] = carry[8] / by_head(list(carry[4:8]))
        for hh in range(4):
            lse_ref[0, :, hh:hh + 1] = (carry[hh] + jnp.log2(carry[4 + hh])) * (1.0 / LOG2E)

    return pl.pallas_call(
        body, name="mla_fwd", grid=(2, nq),
        out_shape=(_sds((s_len, MLA_WIDTH), F32), _sds((2, s_len, 4), F32)),
        in_specs=[pl.BlockSpec((tq, 256), lambda g, i: (i, g)),
                  pl.BlockSpec((tq, 128), lambda g, i: (i, g)),
                  pl.BlockSpec((s_len, 256), lambda g, i: (0, g)),
                  pl.BlockSpec((s_len, 128), lambda g, i: (0, 0)),
                  pl.BlockSpec((s_len, 256), lambda g, i: (0, g))],
        out_specs=(pl.BlockSpec((tq, 256), lambda g, i: (i, g)),
                   pl.BlockSpec((1, tq, 4), lambda g, i: (g, i, 0))),
        compiler_params=_params(("parallel", "parallel")),
    )(qn, qp, kn, kpt, v)


def _mla_bwd_call(qn, qp, kn, kpt, v, o, do, lse):
    s_len = qn.shape[0]
    tk = min(MLA_KEY_TILE, s_len)
    tq = min(ATT_Q_TILES * ATT_TILE, s_len)
    r = tq // tk
    nq = s_len // tq

    def body(qn_ref, qp_ref, kn_ref, kpt_ref, v_ref, o_ref, do_ref, lse_ref,
             dqn_ref, dqp_ref, dkn_ref, dkpt_ref, dv_ref):
        g = pl.program_id(0)
        i = pl.program_id(1)

        @pl.when(i == 0)
        def _():
            dkn_ref[...] = jnp.zeros_like(dkn_ref)
            dv_ref[...] = jnp.zeros_like(dv_ref)

        @pl.when((i == 0) & (g == 0))
        def _():
            dkpt_ref[...] = jnp.zeros_like(dkpt_ref)

        qn2 = qn_ref[...]
        qp2 = qp_ref[...]
        dof = do_ref[...]
        dob = dof.astype(BF16)
        prod = dof * o_ref[...]
        lane256 = lax.broadcasted_iota(jnp.int32, (1, 256), 1)
        lane128 = lax.broadcasted_iota(jnp.int32, (1, 128), 1)
        krow = lax.broadcasted_iota(jnp.int32, (tk, tk), 0)
        kcol = lax.broadcasted_iota(jnp.int32, (tk, tk), 1)
        row = lax.broadcasted_iota(jnp.int32, (tq, tk), 0)
        col = lax.broadcasted_iota(jnp.int32, (tq, tk), 1)
        valids = [col + u * tk <= row for u in range(r)]
        m64s = [(lane256 // 64) == hh for hh in range(4)]
        half = [(lane128 // 64) == u for u in range(2)]
        m32s = [(lane128 // 32) == hh for hh in range(4)]
        qcs, doms = [], []
        for hh in range(4):
            sl = slice(128 * (hh // 2), 128 * (hh // 2) + 128)
            qpair = qn2[:, sl]
            dpair = dob[:, sl]
            qcs.append(jnp.concatenate([jnp.where(half[hh % 2], qpair, jnp.zeros_like(qpair)),
                                        jnp.where(m32s[hh], qp2, jnp.zeros_like(qp2))], axis=1))
            doms.append(jnp.where(half[hh % 2], dpair, jnp.zeros_like(dpair)))
        dsums = [jnp.sum(jnp.where(m64, prod, 0.0), axis=1, keepdims=True) * MLA_SCALE for m64 in m64s]
        lses = [lse_ref[0, :, hh:hh + 1] * LOG2E for hh in range(4)]
        qn2t = jnp.transpose(qn2.astype(F32))
        qp2t = jnp.transpose(qp2.astype(F32))
        do2t = jnp.transpose(dof)
        sub128 = lax.broadcasted_iota(jnp.int32, (128, 1), 0)
        qtstacks, dotstacks = [], []
        for pp in range(2):
            qts, dts = [], []
            for u in range(2):
                hh = 2 * pp + u
                qts.append(jnp.concatenate(
                    [jnp.where((sub128 // 64) == u, qn2t[128 * pp:128 * pp + 128, :], 0.0),
                     jnp.where((sub128 // 32) == hh, qp2t, 0.0)], axis=0).astype(BF16))
                dts.append(jnp.where((sub128 // 64) == u, do2t[128 * pp:128 * pp + 128, :], 0.0).astype(BF16))
            qtstacks.append(jnp.concatenate(qts, axis=1))
            dotstacks.append(jnp.concatenate(dts, axis=1))

        def block(j, carry, valid):
            dqn, dqp = carry
            off = pl.multiple_of(j * tk, tk)
            knb = kn_ref[pl.ds(off, tk), :]
            kpb = kpt_ref[pl.ds(off, tk), :]
            vb = v_ref[pl.ds(off, tk), :]
            dqn_parts = []
            dkp = None
            for pp in range(2):
                sl = slice(128 * pp, 128 * pp + 128)
                knp = knb[:, sl]
                vpair = vb[:, sl]
                kc = jnp.concatenate([knp, kpb], axis=1)
                dss, pbs, kcms = [], [], []
                for u in range(2):
                    hh = 2 * pp + u
                    s = _dot_nt(qcs[hh], kc) * (MLA_SCALE * LOG2E)
                    if valid is not None:
                        s = jnp.where(valid, s, -1e30)
                    p = jnp.exp2(s - lses[hh])
                    ds = p * (_dot_nt(doms[hh], vpair) * MLA_SCALE - dsums[hh])
                    dss.append(ds.astype(BF16))
                    pbs.append(p.astype(BF16))
                    kcms.append(jnp.concatenate([jnp.where(half[u], knp, jnp.zeros_like(knp)),
                                                 jnp.where(m32s[hh], kpb, jnp.zeros_like(kpb))], axis=1))
                dqc = _dot(jnp.concatenate(dss, axis=1), jnp.concatenate(kcms, axis=0))
                dqn_parts.append(dqc[:, :128])
                dqp = dqp + dqc[:, 128:]
                dkc = _dot(qtstacks[pp], jnp.concatenate(dss, axis=0))
                dkn_ref[128 * pp:128 * pp + 128, pl.ds(off, tk)] += dkc[:128, :]
                dkp = dkc[128:, :] if dkp is None else dkp + dkc[128:, :]
                dv_ref[128 * pp:128 * pp + 128, pl.ds(off, tk)] += _dot(dotstacks[pp], jnp.concatenate(pbs, axis=0))
            dqn = dqn + jnp.concatenate(dqn_parts, axis=1)
            dkpt_ref[:, pl.ds(off, tk)] += dkp
            return dqn, dqp

        carry = lax.fori_loop(0, i * r, lambda j, cr: block(j, cr, None),
                              (jnp.zeros((tq, 256), F32), jnp.zeros((tq, 128), F32)))
        for u in range(r):
            carry = block(i * r + u, carry, valids[u])
        dqn, dqp = carry
        dqn_ref[...] = dqn.astype(BF16)
        dqp_ref[...] = dqp.astype(BF16)

    return pl.pallas_call(
        body, name="mla_bwd", grid=(2, nq),
        out_shape=(_sds((s_len, 512), BF16), _sds((s_len, 256), BF16), _sds((512, s_len), F32),
                   _sds((128, s_len), F32), _sds((512, s_len), F32)),
        in_specs=[pl.BlockSpec((tq, 256), lambda g, i: (i, g)),
                  pl.BlockSpec((tq, 128), lambda g, i: (i, g)),
                  pl.BlockSpec((s_len, 256), lambda g, i: (0, g)),
                  pl.BlockSpec((s_len, 128), lambda g, i: (0, 0)),
                  pl.BlockSpec((s_len, 256), lambda g, i: (0, g)),
                  pl.BlockSpec((tq, 256), lambda g, i: (i, g)),
                  pl.BlockSpec((tq, 256), lambda g, i: (i, g)),
                  pl.BlockSpec((1, tq, 4), lambda g, i: (g, i, 0))],
        out_specs=(pl.BlockSpec((tq, 256), lambda g, i: (i, g)),
                   pl.BlockSpec((tq, 128), lambda g, i: (i, g)),
                   pl.BlockSpec((256, s_len), lambda g, i: (g, 0)),
                   pl.BlockSpec((128, s_len), lambda g, i: (0, 0)),
                   pl.BlockSpec((256, s_len), lambda g, i: (g, 0))),
        compiler_params=_params(("arbitrary", "arbitrary")),
    )(qn, qp, kn, kpt, v, o, do, lse)


def _post_call(x, tgt, oa, ob, sz, mz, ga, gb, gate, gf, wa, wb, wo):
    s_len = x.shape[0]
    tm = min(ROW_TILE, s_len)

    def body(x_ref, t_ref, oa_ref, ob_ref, sz_ref, mz_ref, ga_ref, gb_ref, gate_ref, gf_ref,
             wa_ref, wb_ref, wo_ref,
             dx2_ref, doa_ref, dob_ref, dsz_ref, dmz_ref, dga_ref, dgb_ref,
             dwo_ref, dwa_ref, dwb_ref, dgf_ref, dgate_ref, loss_ref):
        @pl.when(pl.program_id(0) == 0)
        def _():
            dwo_ref[...] = jnp.zeros_like(dwo_ref)
            dwa_ref[...] = jnp.zeros_like(dwa_ref)
            dwb_ref[...] = jnp.zeros_like(dwb_ref)
            dgf_ref[...] = jnp.zeros_like(dgf_ref)
            dgate_ref[...] = jnp.zeros_like(dgate_ref)
            loss_ref[...] = jnp.zeros_like(loss_ref)

        gate = gate_ref[...]
        gf = gf_ref[...]
        oa = oa_ref[...]
        ob = ob_ref[...]
        sz = sz_ref[...]
        mz = mz_ref[...]
        sa = _sigmoid(sz)
        sb = _sigmoid(mz)
        silu_a = sz * sa
        silu_b = mz * sb
        ua = (oa * silu_a).astype(BF16)
        ub = (ob * silu_b).astype(BF16)
        ya = _dot(ua, wa_ref[...])
        yb = _dot(ub, wb_ref[...])
        sga = _sigmoid(ga_ref[...])
        sgb = _sigmoid(gb_ref[...])
        merged = (sga * ya + sgb * yb).astype(BF16)
        out = _dot(merged, wo_ref[...])
        x2 = x_ref[...] + gate * out
        r2 = lax.rsqrt(jnp.mean(x2 * x2, axis=-1, keepdims=True) + EPS)
        xhat = x2 * r2
        err = xhat * gf - t_ref[...]
        loss_ref[...] += 0.5 * jnp.sum(jnp.sum(err * err, axis=1, keepdims=True), axis=0, keepdims=True) / D_MODEL
        dy = err * (1.0 / D_MODEL)
        dgf_ref[...] += jnp.sum(dy * xhat, axis=0, keepdims=True)
        dxhat = dy * gf
        dx2 = r2 * (dxhat - xhat * jnp.mean(dxhat * xhat, axis=-1, keepdims=True))
        dx2_ref[...] = dx2
        dgate_ref[...] += jnp.sum(dx2 * out, axis=0, keepdims=True)
        dout = (dx2 * gate).astype(BF16)
        dmerged = _dot_nt(dout, wo_ref[...])
        dwo_ref[...] += _dot_tn(merged, dout)
        dya = dmerged * sga
        dyb = dmerged * sgb
        dga_ref[...] = (dya * ya * (1.0 - sga)).astype(BF16)
        dgb_ref[...] = (dyb * yb * (1.0 - sgb)).astype(BF16)
        dyab = dya.astype(BF16)
        dybb = dyb.astype(BF16)
        dua = _dot_nt(dyab, wa_ref[...])
        dub = _dot_nt(dybb, wb_ref[...])
        dwa_ref[...] += _dot_tn(ua, dyab)
        dwb_ref[...] += _dot_tn(ub, dybb)
        doa_ref[...] = dua * silu_a
        dob_ref[...] = dub * silu_b
        dsz_ref[...] = (dua * oa * (sa * (1.0 + sz * (1.0 - sa)))).astype(BF16)
        dmz_ref[...] = (dub * ob * (sb * (1.0 + mz * (1.0 - sb)))).astype(BF16)

    return pl.pallas_call(
        body, name="post", grid=(s_len // tm,),
        out_shape=(_sds((s_len, D_MODEL), F32), _sds((s_len, 512), F32), _sds((s_len, 512), F32),
                   _sds((s_len, 512), BF16), _sds((s_len, 512), BF16),
                   _sds((s_len, D_MODEL), BF16), _sds((s_len, D_MODEL), BF16),
                   _sds((D_MODEL, D_MODEL), F32), _sds((512, D_MODEL), F32), _sds((512, D_MODEL), F32),
                   _sds((1, D_MODEL), F32), _sds((1, D_MODEL), F32), _sds((1, 128), F32)),
        in_specs=[_rows(tm, D_MODEL), _rows(tm, D_MODEL), _rows(tm, 512), _rows(tm, 512), _rows(tm, 512),
                  _rows(tm, 512), _rows(tm, D_MODEL), _rows(tm, D_MODEL), _whole((1, D_MODEL)), _whole((1, D_MODEL)),
                  _whole((512, D_MODEL)), _whole((512, D_MODEL)), _whole((D_MODEL, D_MODEL))],
        out_specs=(_rows(tm, D_MODEL), _rows(tm, 512), _rows(tm, 512), _rows(tm, 512), _rows(tm, 512),
                   _rows(tm, D_MODEL), _rows(tm, D_MODEL),
                   _whole((D_MODEL, D_MODEL)), _whole((512, D_MODEL)), _whole((512, D_MODEL)),
                   _whole((1, D_MODEL)), _whole((1, D_MODEL)), _whole((1, 128))),
        compiler_params=_params(("arbitrary",)),
    )(x, tgt, oa, ob, sz, mz, ga, gb, gate, gf, wa, wb, wo)


def _bwdprep_call(dsq, dsk, dsv, dsz, dqn, dqp, dkn, dvv, dkpt, dmz, dga, dgb, cq, ckv, cos256, sin256,
                  qg, kvg, w_q, w_kv):
    s_len = cq.shape[0]
    tm = min(ROW_TILE, s_len)

    def body(dsq_ref, dsk_ref, dsv_ref, dsz_ref, dqn_ref, dqp_ref, dkn_ref, dvv_ref, dkpt_ref, dmz_ref,
             dga_ref, dgb_ref, cq_ref, ckv_ref, cos_ref, sin_ref, qg_ref, kvg_ref, wq_ref, wkv_ref,
             dp_ref, dwq_ref, dwkv_ref, dqg_ref, dkvg_ref):
        @pl.when(pl.program_id(0) == 0)
        def _():
            dwq_ref[...] = jnp.zeros_like(dwq_ref)
            dwkv_ref[...] = jnp.zeros_like(dwkv_ref)
            dqg_ref[...] = jnp.zeros_like(dqg_ref)
            dkvg_ref[...] = jnp.zeros_like(dkvg_ref)

        cos = cos_ref[...]
        sin = sin_ref[...]
        dp_ref[:, O_SQ:O_SK] = dsq_ref[...] * jnp.asarray(SB_SCALE, BF16)
        dp_ref[:, O_SK:O_SV] = jnp.transpose(dsk_ref[...]).astype(BF16)
        dp_ref[:, O_SV:O_SZ] = jnp.transpose(dsv_ref[...]).astype(BF16)
        dp_ref[:, O_SZ:O_CQ] = dsz_ref[...]
        dp_ref[:, O_MZ:O_GA] = dmz_ref[...]
        dp_ref[:, O_GA:O_GB] = dga_ref[...]
        dp_ref[:, O_GB:O_KR] = dgb_ref[...]
        dkp = jnp.transpose(dkpt_ref[...])
        dp_ref[:, O_KR:O_KR + 128] = (dkp * cos[:, :128]).astype(BF16)
        dp_ref[:, O_KR + 128:O_END] = (dkp * sin[:, :128]).astype(BF16)
        dp_ref[:, O_END:W_INT] = jnp.zeros((tm, W_INT - O_END), BF16)

        cq = cq_ref[...]
        rq = lax.rsqrt(jnp.mean(cq * cq, axis=-1, keepdims=True) + EPS)
        cqh = cq * rq
        qg = qg_ref[...]
        cqn = (cqh * qg).astype(BF16)
        dqp = dqp_ref[...].astype(F32)
        dqa = jnp.concatenate([dqn_ref[...], (dqp * cos).astype(BF16), (dqp * sin).astype(BF16)], axis=1)
        dcqn = _dot_nt(dqa, wq_ref[...])
        dwq_ref[...] += _dot_tn(cqn, dqa)
        dqg_ref[...] += jnp.sum(dcqn * cqh, axis=0, keepdims=True)
        dh = dcqn * qg
        dcq = rq * (dh - cqh * jnp.mean(dh * cqh, axis=-1, keepdims=True))
        dp_ref[:, O_CQ:O_CKV] = dcq.astype(BF16)

        ckv = ckv_ref[...]
        rk = lax.rsqrt(jnp.mean(ckv * ckv, axis=-1, keepdims=True) + EPS)
        ckh = ckv * rk
        kvg = kvg_ref[...]
        ckvn = (ckh * kvg).astype(BF16)
        dkva = jnp.concatenate([jnp.transpose(dkn_ref[...]).astype(BF16),
                                jnp.transpose(dvv_ref[...]).astype(BF16)], axis=1)
        dckvn = _dot_nt(dkva, wkv_ref[...])
        dwkv_ref[...] += _dot_tn(ckvn, dkva)
        dkvg_ref[...] += jnp.sum(dckvn * ckh, axis=0, keepdims=True)
        dh2 = dckvn * kvg
        dckv = rk * (dh2 - ckh * jnp.mean(dh2 * ckh, axis=-1, keepdims=True))
        dp_ref[:, O_CKV:O_MZ] = dckv.astype(BF16)

    return pl.pallas_call(
        body, name="bwdprep", grid=(s_len // tm,),
        out_shape=(_sds((s_len, W_INT), BF16), _sds((Q_RANK, 1024), F32), _sds((KV_RANK, 1024), F32),
                   _sds((1, Q_RANK), F32), _sds((1, KV_RANK), F32)),
        in_specs=[_rows(tm, 512), _cols(512, tm), _cols(512, tm), _rows(tm, 512), _rows(tm, 512), _rows(tm, 256),
                  _cols(512, tm), _cols(512, tm), _cols(128, tm), _rows(tm, 512), _rows(tm, D_MODEL),
                  _rows(tm, D_MODEL), _rows(tm, Q_RANK), _rows(tm, KV_RANK), _rows(tm, 256), _rows(tm, 256),
                  _whole((1, Q_RANK)), _whole((1, KV_RANK)), _whole((Q_RANK, 1024)), _whole((KV_RANK, 1024))],
        out_specs=(_rows(tm, W_INT), _whole((Q_RANK, 1024)), _whole((KV_RANK, 1024)),
                   _whole((1, Q_RANK)), _whole((1, KV_RANK))),
        compiler_params=_params(("arbitrary",)),
    )(dsq, dsk, dsv, dsz, dqn, dqp, dkn, dvv, dkpt, dmz, dga, dgb, cq, ckv, cos256, sin256, qg, kvg, w_q, w_kv)


def _dh_call(dproj, w_int_t, x, dx2, scale, g1):
    s_len = x.shape[0]
    tm = min(2 * ROW_TILE, s_len)

    def body(dp_ref, wt_ref, x_ref, dx2_ref, sc_ref, g1_ref, gx_ref, dsh_ref, dsc_ref, dg1_ref):
        @pl.when(pl.program_id(0) == 0)
        def _():
            dsh_ref[...] = jnp.zeros_like(dsh_ref)
            dsc_ref[...] = jnp.zeros_like(dsc_ref)
            dg1_ref[...] = jnp.zeros_like(dg1_ref)

        dh = _dot(dp_ref[...], wt_ref[...])
        xt = x_ref[...]
        r = lax.rsqrt(jnp.mean(xt * xt, axis=-1, keepdims=True) + EPS)
        xh = xt * r
        g1 = g1_ref[...]
        xg = xh * g1
        dsh_ref[...] += jnp.sum(dh, axis=0, keepdims=True)
        dsc_ref[...] += jnp.sum(dh * xg, axis=0, keepdims=True)
        dxg = dh * (1.0 + sc_ref[...])
        dg1_ref[...] += jnp.sum(dxg * xh, axis=0, keepdims=True)
        dxh = dxg * g1
        gx_ref[...] = dx2_ref[...] + r * (dxh - xh * jnp.mean(dxh * xh, axis=-1, keepdims=True))

    return pl.pallas_call(
        body, name="dh", grid=(s_len // tm,),
        out_shape=(_sds((s_len, D_MODEL), F32), _sds((1, D_MODEL), F32), _sds((1, D_MODEL), F32),
                   _sds((1, D_MODEL), F32)),
        in_specs=[_rows(tm, W_INT), _whole((W_INT, D_MODEL)), _rows(tm, D_MODEL), _rows(tm, D_MODEL),
                  _whole((1, D_MODEL)), _whole((1, D_MODEL))],
        out_specs=(_rows(tm, D_MODEL), _whole((1, D_MODEL)), _whole((1, D_MODEL)), _whole((1, D_MODEL))),
        compiler_params=_params(("arbitrary",)),
    )(dproj, w_int_t, x, dx2, scale, g1)


def _small_call(svg, ct, dmod_sh):
    def body(sv_ref, ct_ref, dm_ref, tot_ref, gwada_ref):
        acc = sv_ref[0:1, :]
        for d in range(1, N_DEV):
            acc = acc + sv_ref[d:d + 1, :]
        tot_ref[...] = acc
        gwada_ref[...] = lax.dot_general(ct_ref[...], dm_ref[...], (((1,), (0,)), ((), ())),
                                         precision=lax.Precision.HIGHEST, preferred_element_type=F32)

    vmem = pl.BlockSpec(memory_space=pltpu.VMEM)
    return pl.pallas_call(
        body, name="small_grads",
        out_shape=(_sds((1, 8 * SV_COLS), F32), _sds((D_MODEL, 768), F32)),
        in_specs=[vmem, vmem, vmem], out_specs=(vmem, vmem),
        compiler_params=_params(),
    )(svg, ct, dmod_sh)


def _adamw_tile_rows(rows, cols):
    budget = 2 << 20
    if rows * cols * 4 <= budget or rows % 8:
        return rows
    best = 8
    for tr in range(8, rows + 1, 8):
        if rows % tr == 0 and tr * cols * 4 <= budget:
            best = tr
    return best


def _adamw_call(name, w, g, m, v):
    rows, cols = w.shape
    tr = _adamw_tile_rows(rows, cols)

    def body(w_ref, g_ref, m_ref, v_ref, d_ref, nm_ref, nv_ref):
        gg = g_ref[...]
        m2 = ADAM_B1 * m_ref[...] + (1.0 - ADAM_B1) * gg
        v2 = ADAM_B2 * v_ref[...] + (1.0 - ADAM_B2) * (gg * gg)
        m_hat = m2 / (1.0 - ADAM_B1 ** ADAM_STEP)
        v_hat = v2 / (1.0 - ADAM_B2 ** ADAM_STEP)
        d_ref[...] = -ADAM_LR * (m_hat / (jnp.sqrt(v_hat) + ADAM_EPS) + ADAM_WD * w_ref[...])
        nm_ref[...] = m2
        nv_ref[...] = v2

    spec = pl.BlockSpec((tr, cols), lambda i: (i, 0))
    return pl.pallas_call(
        body, name="adamw_" + name, grid=(rows // tr,),
        out_shape=(_sds((rows, cols), F32),) * 3,
        in_specs=[spec] * 4, out_specs=(spec,) * 3,
        compiler_params=_params(("parallel",)),
    )(w, g, m, v)


IN_SHARD = IN_WIDTH // N_CHIPS
HALF_D = D_MODEL // 2
SMALL_ROWS = (576, 512, 1024, 1024, 2048)
SMALL_TOTAL = sum(SMALL_ROWS)
SMALL_HALF = SMALL_TOTAL // 2
SMALL_SUM_ROWS = 432


def _gather_call(c_row, w_ada_sh, pack_in, pack_small):
    def body(c_ref, wada_ref, pki_ref, pks_ref, mg_ref, cg_ref, gwi_ref, gws_ref,
             cv, ssem_c, rsem_c, ssem_m, rsem_m, ssem_w, rsem_w, ssem_f, rsem_f, lsem):
        x, y, c = lax.axis_index("x"), lax.axis_index("y"), lax.axis_index("c")
        me = 4 * x + 2 * y + c
        chip = 2 * x + y
        rel3 = [(1, 0), (0, 1), (1, 1)]
        packs = [(pki_ref, gwi_ref), (pks_ref, gws_ref)]

        sends = []
        for j, (dx, dy) in enumerate(rel3):
            for a, (pk, gw) in enumerate(packs):
                cp = pltpu.make_async_remote_copy(
                    src_ref=pk.at[c], dst_ref=gw.at[chip, c], send_sem=ssem_w.at[j, a], recv_sem=rsem_w.at[j, a],
                    device_id=(_flip(x, dx), _flip(y, dy), c), device_id_type=MESH)
                cp.start()
                sends.append(cp)
        owns = []
        for a, (pk, gw) in enumerate(packs):
            own = pltpu.make_async_copy(pk, gw.at[chip], lsem.at[a])
            own.start()
            owns.append(own)

        cv[me] = c_ref[...]
        for r in range(1, N_DEV):
            dx, dy, dc = (r >> 2) & 1, (r >> 1) & 1, r & 1
            cp = pltpu.make_async_remote_copy(
                src_ref=c_ref, dst_ref=cv.at[me], send_sem=ssem_c.at[r - 1], recv_sem=rsem_c.at[r - 1],
                device_id=(_flip(x, dx), _flip(y, dy), _flip(c, dc)), device_id_type=MESH)
            cp.start()
            sends.append(cp)
        for r in range(1, N_DEV):
            dx, dy, dc = (r >> 2) & 1, (r >> 1) & 1, r & 1
            src = 4 * _flip(x, dx) + 2 * _flip(y, dy) + _flip(c, dc)
            pltpu.make_async_remote_copy(
                src_ref=c_ref, dst_ref=cv.at[src], send_sem=ssem_c.at[r - 1], recv_sem=rsem_c.at[r - 1],
                device_id=(x, y, c), device_id_type=MESH).wait_recv()
        rows = lax.broadcasted_iota(jnp.int32, (N_DEV, D_MODEL), 0)
        call = jnp.zeros((N_DEV, D_MODEL), F32)
        for b in range(N_DEV):
            call = jnp.where(rows == b, jnp.broadcast_to(cv[b], (N_DEV, D_MODEL)), call)
        cg_ref[...] = call

        mg_ref[chip] = lax.dot_general(call, wada_ref[...], (((1,), (0,)), ((), ())),
                                       precision=lax.Precision.HIGHEST, preferred_element_type=F32)
        for j, (dx, dy) in enumerate(rel3):
            cp = pltpu.make_async_remote_copy(
                src_ref=mg_ref.at[chip], dst_ref=mg_ref.at[chip], send_sem=ssem_m.at[j], recv_sem=rsem_m.at[j],
                device_id=(_flip(x, dx), _flip(y, dy), c), device_id_type=MESH)
            cp.start()
            sends.append(cp)
        for j, (dx, dy) in enumerate(rel3):
            src_chip = 2 * _flip(x, dx) + _flip(y, dy)
            pltpu.make_async_remote_copy(
                src_ref=mg_ref.at[src_chip], dst_ref=mg_ref.at[src_chip], send_sem=ssem_m.at[j],
                recv_sem=rsem_m.at[j], device_id=(x, y, c), device_id_type=MESH).wait_recv()
        for j, (dx, dy) in enumerate(rel3):
            src_chip = 2 * _flip(x, dx) + _flip(y, dy)
            for a, (pk, gw) in enumerate(packs):
                pltpu.make_async_remote_copy(
                    src_ref=pk.at[c], dst_ref=gw.at[src_chip, c], send_sem=ssem_w.at[j, a],
                    recv_sem=rsem_w.at[j, a], device_id=(x, y, c), device_id_type=MESH).wait_recv()
                cp = pltpu.make_async_remote_copy(
                    src_ref=gw.at[src_chip, c], dst_ref=gw.at[src_chip, c], send_sem=ssem_f.at[j, a],
                    recv_sem=rsem_f.at[j, a], device_id=(x, y, 1 - c), device_id_type=MESH)
                cp.start()
                sends.append(cp)
        for j, (dx, dy) in enumerate(rel3):
            src_chip = 2 * _flip(x, dx) + _flip(y, dy)
            for a, (pk, gw) in enumerate(packs):
                pltpu.make_async_remote_copy(
                    src_ref=pk.at[c], dst_ref=gw.at[src_chip, 1 - c], send_sem=ssem_f.at[j, a],
                    recv_sem=rsem_f.at[j, a], device_id=(x, y, c), device_id_type=MESH).wait_recv()
        for cp in sends:
            cp.wait_send()
        for own in owns:
            own.wait()

    vmem = pl.BlockSpec(memory_space=pltpu.VMEM)
    return pl.pallas_call(
        body, name="gather_fwd",
        out_shape=(_sds((N_CHIPS, N_DEV, 768), F32), _sds((N_DEV, D_MODEL), F32),
                   _sds((N_CHIPS, 2, IN_SHARD, HALF_D), BF16), _sds((N_CHIPS, 2, SMALL_HALF, LANES), BF16)),
        in_specs=[vmem, vmem, vmem, vmem], out_specs=(vmem, vmem, vmem, vmem),
        scratch_shapes=[
            pltpu.VMEM((N_DEV, 1, D_MODEL), F32),
            pltpu.SemaphoreType.DMA((N_DEV - 1,)), pltpu.SemaphoreType.DMA((N_DEV - 1,)),
            pltpu.SemaphoreType.DMA((3,)), pltpu.SemaphoreType.DMA((3,)),
            pltpu.SemaphoreType.DMA((3, 2)), pltpu.SemaphoreType.DMA((3, 2)),
            pltpu.SemaphoreType.DMA((3, 2)), pltpu.SemaphoreType.DMA((3, 2)),
            pltpu.SemaphoreType.DMA((2,)),
        ],
        compiler_params=_params(),
    )(c_row, w_ada_sh, pack_in, pack_small)


def _reduce_call(g_in, g_small, sv):
    def body(gi_ref, gs_ref, sv_ref, fi_ref, fs_ref, svg_ref, pair_i, pair_s, send_i, send_s, land_i, land_s,
             ssem_p, rsem_p, ssem_g, rsem_g, ssem_s, rsem_s, ssem_x, rsem_x):
        x, y, c = lax.axis_index("x"), lax.axis_index("y"), lax.axis_index("c")
        me = 4 * x + 2 * y + c
        chip = 2 * x + y
        rel3 = [(1, 0), (0, 1), (1, 1)]
        payloads = [(gi_ref, pair_i, send_i, land_i, fi_ref), (gs_ref, pair_s, send_s, land_s, fs_ref)]
        where = [lambda k, h: N_CHIPS * h + k, lambda k, h: 2 * k + h]
        copies = []

        for k in range(N_CHIPS):
            for a, (g, pair, _, _, _) in enumerate(payloads):
                cp = pltpu.make_async_remote_copy(
                    src_ref=g.at[where[a](k, 1 - c)], dst_ref=pair.at[k], send_sem=ssem_p.at[k, a],
                    recv_sem=rsem_p.at[k, a], device_id=(x, y, 1 - c), device_id_type=MESH)
                cp.start()
                copies.append(cp)

        for r in range(1, N_DEV):
            dx, dy, dc = (r >> 2) & 1, (r >> 1) & 1, r & 1
            cp = pltpu.make_async_remote_copy(
                src_ref=sv_ref, dst_ref=svg_ref.at[me], send_sem=ssem_s.at[r - 1], recv_sem=rsem_s.at[r - 1],
                device_id=(_flip(x, dx), _flip(y, dy), _flip(c, dc)), device_id_type=MESH)
            cp.start()
            copies.append(cp)
        svg_ref[me] = sv_ref[...]

        def pair_sum(k, store_in, store_small):
            for a, (g, pair, _, _, _) in enumerate(payloads):
                pltpu.make_async_remote_copy(
                    src_ref=g.at[where[a](k, c)], dst_ref=pair.at[k], send_sem=ssem_p.at[k, a],
                    recv_sem=rsem_p.at[k, a], device_id=(x, y, c), device_id_type=MESH).wait_recv()
            for qd in range(HALF_D // LANES):
                sl = slice(LANES * qd, LANES * qd + LANES)
                store_in(sl, gi_ref[where[0](k, c), :, sl].astype(F32) + pair_i[k, :, sl].astype(F32))

            def rows(i, carry):
                sl = pl.ds(pl.multiple_of(i * SMALL_SUM_ROWS, 16), SMALL_SUM_ROWS)
                store_small(sl, gs_ref[where[1](k, c), sl, :].astype(F32) + pair_s[k, sl, :].astype(F32))
                return carry

            lax.fori_loop(0, SMALL_HALF // SMALL_SUM_ROWS, rows, 0)

        for j, (dx, dy) in enumerate(rel3):
            tx, ty = _flip(x, dx), _flip(y, dy)

            def put_in(sl, val, j=j):
                send_i[j, :, sl] = val.astype(BF16)

            def put_small(sl, val, j=j):
                send_s[j, sl, :] = val.astype(BF16)

            pair_sum(2 * tx + ty, put_in, put_small)
            for a, (_, _, send, land, _) in enumerate(payloads):
                cp = pltpu.make_async_remote_copy(
                    src_ref=send.at[j], dst_ref=land.at[j], send_sem=ssem_g.at[j, a], recv_sem=rsem_g.at[j, a],
                    device_id=(tx, ty, c), device_id_type=MESH)
                cp.start()
                copies.append(cp)

        def own_in(sl, val):
            fi_ref[c, :, sl] = val

        def own_small(sl, val):
            fs_ref[c, sl, :] = val

        pair_sum(chip, own_in, own_small)
        for j in range(3):
            for a, (_, _, send, land, _) in enumerate(payloads):
                pltpu.make_async_remote_copy(
                    src_ref=send.at[j], dst_ref=land.at[j], send_sem=ssem_g.at[j, a], recv_sem=rsem_g.at[j, a],
                    device_id=(x, y, c), device_id_type=MESH).wait_recv()
            for qd in range(HALF_D // LANES):
                sl = slice(LANES * qd, LANES * qd + LANES)
                fi_ref[c, :, sl] += land_i[j, :, sl].astype(F32)

            def add_rows(i, carry, j=j):
                sl = pl.ds(pl.multiple_of(i * SMALL_SUM_ROWS, 16), SMALL_SUM_ROWS)
                fs_ref[c, sl, :] += land_s[j, sl, :].astype(F32)
                return carry

            lax.fori_loop(0, SMALL_HALF // SMALL_SUM_ROWS, add_rows, 0)

        for a, f in enumerate((fi_ref, fs_ref)):
            cp = pltpu.make_async_remote_copy(
                src_ref=f.at[c], dst_ref=f.at[c], send_sem=ssem_x.at[a], recv_sem=rsem_x.at[a],
                device_id=(x, y, 1 - c), device_id_type=MESH)
            cp.start()
            copies.append(cp)
        for a, f in enumerate((fi_ref, fs_ref)):
            pltpu.make_async_remote_copy(
                src_ref=f.at[c], dst_ref=f.at[1 - c], send_sem=ssem_x.at[a], recv_sem=rsem_x.at[a],
                device_id=(x, y, c), device_id_type=MESH).wait_recv()
        for r in range(1, N_DEV):
            dx, dy, dc = (r >> 2) & 1, (r >> 1) & 1, r & 1
            src = 4 * _flip(x, dx) + 2 * _flip(y, dy) + _flip(c, dc)
            pltpu.make_async_remote_copy(
                src_ref=sv_ref, dst_ref=svg_ref.at[src], send_sem=ssem_s.at[r - 1],
                recv_sem=rsem_s.at[r - 1], device_id=(x, y, c), device_id_type=MESH).wait_recv()
        for cp in copies:
            cp.wait_send()

    vmem = pl.BlockSpec(memory_space=pltpu.VMEM)
    return pl.pallas_call(
        body, name="grad_reduce",
        out_shape=(_sds((2, IN_SHARD, HALF_D), F32), _sds((2, SMALL_HALF, LANES), F32),
                   _sds((N_DEV, 8, SV_COLS), F32)),
        in_specs=[vmem, vmem, vmem], out_specs=(vmem, vmem, vmem),
        scratch_shapes=[
            pltpu.VMEM((N_CHIPS, IN_SHARD, HALF_D), BF16), pltpu.VMEM((N_CHIPS, SMALL_HALF, LANES), BF16),
            pltpu.VMEM((3, IN_SHARD, HALF_D), BF16), pltpu.VMEM((3, SMALL_HALF, LANES), BF16),
            pltpu.VMEM((3, IN_SHARD, HALF_D), BF16), pltpu.VMEM((3, SMALL_HALF, LANES), BF16),
            pltpu.SemaphoreType.DMA((N_CHIPS, 2)), pltpu.SemaphoreType.DMA((N_CHIPS, 2)),
            pltpu.SemaphoreType.DMA((3, 2)), pltpu.SemaphoreType.DMA((3, 2)),
            pltpu.SemaphoreType.DMA((N_DEV - 1,)), pltpu.SemaphoreType.DMA((N_DEV - 1,)),
            pltpu.SemaphoreType.DMA((2,)), pltpu.SemaphoreType.DMA((2,)),
        ],
        compiler_params=_params(),
    )(g_in, g_small, sv)


def _dwin_call(h, dproj):
    s_len = h.shape[0]
    tm = min(4 * ROW_TILE, s_len)
    nrow = s_len // tm
    nc = 4
    chunk = W_INT // nc

    def body(h_ref, dp_ref, dw_ref, acc):
        i = pl.program_id(1)

        @pl.when(i == 0)
        def _():
            acc[...] = jnp.zeros_like(acc)

        acc[...] += _dot_tn(dp_ref[...], h_ref[...])

        @pl.when(i == nrow - 1)
        def _():
            dw_ref[0] = acc[:, :HALF_D].astype(BF16)
            dw_ref[1] = acc[:, HALF_D:].astype(BF16)

    return pl.pallas_call(
        body, name="dwin", grid=(nc, nrow),
        out_shape=_sds((2, W_INT, HALF_D), BF16),
        in_specs=[pl.BlockSpec((tm, D_MODEL), lambda c, i: (i, 0)),
                  pl.BlockSpec((tm, chunk), lambda c, i: (i, c))],
        out_specs=pl.BlockSpec((2, chunk, HALF_D), lambda c, i: (0, c, 0)),
        scratch_shapes=[pltpu.VMEM((chunk, D_MODEL), F32)],
        compiler_params=_params(("parallel", "arbitrary")),
    )(h, dproj)


def _swap_rows(w, group):
    r, n = w.shape
    return w.reshape(r // group, 2, group // 2, n)[:, ::-1].reshape(r, n)


def _internal_weights(w_in_t, w_uq, w_ukv):
    krot_t = w_in_t[2688:2720]
    w_int_t = jnp.concatenate([
        w_in_t[0:2688], w_in_t[2720:5280],
        jnp.tile(krot_t, (4, 1)), jnp.tile(_swap_rows(krot_t, 32), (4, 1)),
        jnp.zeros((W_INT - O_END, D_MODEL), w_in_t.dtype)], axis=0)
    uq = w_uq.reshape(Q_RANK, N_HEADS, 96)
    wp = uq[:, :, 64:].reshape(Q_RANK, 256)
    w_q = jnp.concatenate([uq[:, :, :64].reshape(Q_RANK, 512), wp, _swap_halves(wp, 32)], axis=1)
    ukv = w_ukv.reshape(KV_RANK, N_HEADS, 128)
    w_kv = jnp.concatenate([ukv[:, :, :64].reshape(KV_RANK, 512), ukv[:, :, 64:].reshape(KV_RANK, 512)], axis=1)
    return w_int_t, w_q, w_kv


def _true_weight_grads(dwi_t, dwq, dwkv):
    dkr = dwi_t[:, O_KR:O_KR + 128].astype(F32).reshape(2, 4, 32, HALF_D).sum(axis=1)
    dkr_sw = dwi_t[:, O_KR + 128:O_END].astype(F32).reshape(2, 4, 32, HALF_D).sum(axis=1)
    dkr_sw = dkr_sw.reshape(2, 2, 16, HALF_D)[:, ::-1].reshape(2, 32, HALF_D)
    dkrot_t = (dkr + dkr_sw).astype(dwi_t.dtype)
    g_in_t = jnp.concatenate([dwi_t[:, 0:O_MZ], dkrot_t, dwi_t[:, O_MZ:O_KR]], axis=1)
    dwp = dwq[:, 512:768] + _swap_halves(dwq[:, 768:1024], 32)
    g_uq = jnp.concatenate([dwq[:, :512].reshape(Q_RANK, N_HEADS, 64), dwp.reshape(Q_RANK, N_HEADS, 32)],
                           axis=2).reshape(Q_RANK, 768)
    g_ukv = jnp.concatenate([dwkv[:, :512].reshape(KV_RANK, N_HEADS, 64), dwkv[:, 512:].reshape(KV_RANK, N_HEADS, 64)],
                            axis=2).reshape(KV_RANK, 1024)
    return g_in_t, g_uq, g_ukv


def _swap_halves(w, group):
    r, n = w.shape
    return w.reshape(r, n // group, 2, group // 2)[:, :, ::-1, :].reshape(r, n)


def _pack_shards(parts):
    return jnp.concatenate([p.reshape(-1, LANES) for p in parts], axis=0)


def _unpack_small(gw):
    offs = [0]
    for r in SMALL_ROWS:
        offs.append(offs[-1] + r)

    def cols(i, rows, shard_cols):
        blk = gw[:, offs[i]:offs[i + 1]].reshape(N_CHIPS, rows, shard_cols)
        return blk.transpose(1, 0, 2).reshape(rows, N_CHIPS * shard_cols)

    return (cols(0, Q_RANK, 192), cols(1, KV_RANK, 256), cols(2, 512, 256), cols(3, 512, 256),
            gw[:, offs[4]:offs[5]].reshape(D_MODEL, D_MODEL))


def _chip_major(g, shard_cols):
    r = g.shape[0]
    return g.reshape(r, N_CHIPS, shard_cols).transpose(1, 0, 2).reshape(N_CHIPS, -1, LANES)


def kernel(x, c, positions, w_ada, b_ada, norm_gain, w_in, q_norm_gain, w_uq, kv_norm_gain, w_ukv, w_branch_a, w_branch_b, w_out, final_norm_gain, loss_target, m_w_ada, m_b_ada, m_norm_gain, m_w_in, m_q_norm_gain, m_w_uq, m_kv_norm_gain, m_w_ukv, m_w_branch_a, m_w_branch_b, m_w_out, m_final_norm_gain, v_w_ada, v_b_ada, v_norm_gain, v_w_in, v_q_norm_gain, v_w_uq, v_kv_norm_gain, v_w_ukv, v_w_branch_a, v_w_branch_b, v_w_out, v_final_norm_gain):
    ix, iy, ic = lax.axis_index("x"), lax.axis_index("y"), lax.axis_index("c")
    me = 4 * ix + 2 * iy + ic
    chip = 2 * ix + iy
    xs = x[0]
    tgt = loss_target[0]
    s_len = xs.shape[0]

    w_in_t = jnp.swapaxes(w_in[0], 0, 1)
    w_in_tb = w_in_t.astype(BF16)
    pack_in = jnp.stack([w_in_tb[:, :HALF_D], w_in_tb[:, HALF_D:]], axis=0)
    small_shards = (w_uq[0], w_ukv[0], w_branch_a[0], w_branch_b[0], w_out[0])
    pack_small = _pack_shards([s.astype(BF16) for s in small_shards]).reshape(2, SMALL_HALF, LANES)
    mg, call, gw_in, gw_small = _gather_call(c, w_ada[0], pack_in, pack_small)
    mod = mg.transpose(1, 0, 2).reshape(N_DEV, 3 * D_MODEL) + b_ada
    mod_me = lax.dynamic_slice_in_dim(mod, me, 1, axis=0)
    shift, scale, gate = mod_me[:, :D_MODEL], mod_me[:, D_MODEL:2 * D_MODEL], mod_me[:, 2 * D_MODEL:]

    f_in_t = jnp.concatenate([gw_in[:, 0], gw_in[:, 1]], axis=2).reshape(IN_WIDTH, D_MODEL)
    f_uq, f_ukv, f_a, f_b, f_out = _unpack_small(gw_small.reshape(N_CHIPS, SMALL_TOTAL, LANES))
    w_int_t, w_q, w_kv = _internal_weights(f_in_t, f_uq, f_ukv)

    inv_freq = ROPE_BASE ** (-jnp.arange(0, ROPE_DIM, 2, dtype=F32) / ROPE_DIM)
    ang = positions[0].astype(F32)[:, None] * inv_freq
    cs, sn = jnp.cos(ang), jnp.sin(ang)
    cos256 = jnp.tile(jnp.concatenate([cs, cs], axis=1), (1, 8))
    sin256 = jnp.tile(jnp.concatenate([-sn, sn], axis=1), (1, 8))

    (h, sq, sk, sv, sz, cq, ckv, mz, ga, gb, kpt, qn, qp, kn, vv) = _inproj_call(
        xs, shift, scale, norm_gain, w_int_t, w_q, w_kv, q_norm_gain, kv_norm_gain, cos256, sin256)
    oa, lt, first = _sb_fwd_call(sq, sk, sv)
    ob, lse = _mla_fwd_call(qn, qp, kn, kpt, vv)

    gf = final_norm_gain.reshape(1, D_MODEL)
    (dx2, doa, dob, dsz, dmz, dga, dgb, dwo, dwa, dwb, dgf, dgate, loss_p) = _post_call(
        xs, tgt, oa, ob, sz, mz, ga, gb, gate, gf, f_a, f_b, f_out)

    dsq, dsk_t, dsv_t = _sb_bwd_call(first[:, :, 0, 0].reshape(-1), sq, sk, sv, doa, lt)
    dqn, dqp, dkn_t, dkpt_t, dvv_t = _mla_bwd_call(qn, qp, kn, kpt, vv, ob, dob, lse)

    dproj, dwq, dwkv, dqg, dkvg = _bwdprep_call(
        dsq, dsk_t, dsv_t, dsz, dqn, dqp, dkn_t, dvv_t, dkpt_t, dmz, dga, dgb, cq, ckv, cos256, sin256,
        q_norm_gain, kv_norm_gain, w_q, w_kv)
    grad_x, dshift, dscale, dg1 = _dh_call(dproj, w_int_t, xs, dx2, scale, norm_gain)
    dwi_t = _dwin_call(h, dproj)
    g_in_t, g_uq, g_ukv = _true_weight_grads(dwi_t, dwq, dwkv)

    g_in_pieces = g_in_t.reshape(N_DEV, IN_SHARD, HALF_D)
    g_small = jnp.concatenate([
        _chip_major(g_uq, 192), _chip_major(g_ukv, 256), _chip_major(dwa, 256), _chip_major(dwb, 256),
        dwo.reshape(N_CHIPS, -1, LANES)], axis=1).astype(BF16).reshape(N_DEV, SMALL_HALF, LANES)
    small = jnp.concatenate([
        dshift, dscale, dgate, dg1, dqg, dkvg, dgf, loss_p,
        jnp.zeros((1, 8 * SV_COLS - 5888), F32)], axis=1).reshape(8, SV_COLS)
    full_in, full_small, svg = _reduce_call(g_in_pieces, g_small, small)
    gs_in_t = jnp.concatenate([full_in[0], full_in[1]], axis=1)
    full = full_small.reshape(SMALL_TOTAL, LANES)
    offs = [0]
    for r in SMALL_ROWS:
        offs.append(offs[-1] + r)
    gs_uq = full[offs[0]:offs[1]].reshape(Q_RANK, 192)
    gs_ukv = full[offs[1]:offs[2]].reshape(KV_RANK, 256)
    gs_a = full[offs[2]:offs[3]].reshape(512, 256)
    gs_b = full[offs[3]:offs[4]].reshape(512, 256)
    gs_out = full[offs[4]:offs[5]].reshape(256, D_MODEL)

    svm = svg.reshape(N_DEV, 8 * SV_COLS)
    dmod_sh = lax.dynamic_slice_in_dim(svm[:, :3 * D_MODEL], chip * 768, 768, axis=1)
    tot, gs_ada = _small_call(svm, call.T, dmod_sh)
    g_bada = tot[:, 0:3072]
    g_g1 = tot[:, 3072:4096]
    g_qg = tot[:, 4096:4480]
    g_kvg = tot[:, 4480:4736]
    g_gf = tot[:, 4736:5760]
    loss = tot[0, 5760]

    names = ["w_ada", "b_ada", "norm_gain", "w_in", "q_norm_gain", "w_uq", "kv_norm_gain", "w_ukv",
             "w_branch_a", "w_branch_b", "w_out", "final_norm_gain"]
    ws = [w_ada[0], b_ada, norm_gain, w_in_t, q_norm_gain, w_uq[0], kv_norm_gain, w_ukv[0],
          w_branch_a[0], w_branch_b[0], w_out[0], final_norm_gain.reshape(1, D_MODEL)]
    gs = [gs_ada, g_bada, g_g1, gs_in_t, g_qg, gs_uq, g_kvg, gs_ukv, gs_a, gs_b, gs_out, g_gf]
    ms = [m_w_ada[0], m_b_ada, m_norm_gain, jnp.swapaxes(m_w_in[0], 0, 1), m_q_norm_gain, m_w_uq[0],
          m_kv_norm_gain, m_w_ukv[0], m_w_branch_a[0], m_w_branch_b[0], m_w_out[0],
          m_final_norm_gain.reshape(1, D_MODEL)]
    vs = [v_w_ada[0], v_b_ada, v_norm_gain, jnp.swapaxes(v_w_in[0], 0, 1), v_q_norm_gain, v_w_uq[0],
          v_kv_norm_gain, v_w_ukv[0], v_w_branch_a[0], v_w_branch_b[0], v_w_out[0],
          v_final_norm_gain.reshape(1, D_MODEL)]
    refs = [w_ada, b_ada, norm_gain, w_in, q_norm_gain, w_uq, kv_norm_gain, w_ukv,
            w_branch_a, w_branch_b, w_out, final_norm_gain]
    grads, deltas, new_ms, new_vs = [], [], [], []
    for n, w_, g_, m_, v_, ref in zip(names, ws, gs, ms, vs, refs):
        outs = (g_,) + _adamw_call(n, w_, g_, m_, v_)
        if n == "w_in":
            outs = tuple(jnp.swapaxes(o_, 0, 1) for o_ in outs)
        for lst, o_ in zip((grads, deltas, new_ms, new_vs), outs):
            lst.append(o_.reshape(ref.shape))

    return (loss, grad_x.reshape(x.shape), *grads, *deltas, *new_ms, *new_vs)
```

```python
import math

import jax
import jax.numpy as jnp
from jax import lax
from jax.experimental import pallas as pl
from jax.experimental.pallas import tpu as pltpu

F32 = jnp.float32
BF16 = jnp.bfloat16

D_MODEL = 1024
SB_WIDTH = 512
MLA_WIDTH = 512
Q_RANK = 384
KV_RANK = 256
ROPE_DIM = 32
N_HEADS = 8
IN_WIDTH = 5280
EPS = 1e-6
ROPE_BASE = 10000.0
MLA_SCALE = 1.0 / math.sqrt(96.0)
SB_SCALE = 0.125
LOG2E = 1.4426950408889634

ADAM_LR = 0.001
ADAM_B1 = 0.9
ADAM_B2 = 0.999
ADAM_EPS = 1e-08
ADAM_WD = 0.01
ADAM_STEP = 10

O_SQ, O_SK, O_SV, O_SZ, O_CQ, O_CKV, O_MZ, O_GA, O_GB, O_KR, O_END = (
    0, 512, 1024, 1536, 2048, 2432, 2688, 3200, 4224, 5248, 5504)
W_INT = 5632

N_CHIPS = 4
N_DEV = 8
LANES = 128
SV_COLS = 768

ROW_TILE = 256
ATT_TILE = 256
ATT_Q_TILES = 2
FWD_Q_TILES = 4
SB_Q_TILES = 1
MLA_KEY_TILE = 512
VMEM_LIMIT = 56 * 1024 * 1024

MESH = pl.DeviceIdType.MESH


def _dot(a, b):
    return lax.dot_general(a, b, (((1,), (0,)), ((), ())), preferred_element_type=F32)


def _dot_nt(a, b):
    return lax.dot_general(a, b, (((1,), (1,)), ((), ())), preferred_element_type=F32)


def _dot_tn(a, b):
    return lax.dot_general(a, b, (((0,), (0,)), ((), ())), preferred_element_type=F32)


def _sigmoid(z):
    return 1.0 / (1.0 + jnp.exp2(z * (-LOG2E)))


def _params(sem=None):
    if sem is None:
        return pltpu.CompilerParams(vmem_limit_bytes=VMEM_LIMIT)
    return pltpu.CompilerParams(dimension_semantics=sem, vmem_limit_bytes=VMEM_LIMIT)


def _rows(tm, n):
    return pl.BlockSpec((tm, n), lambda i: (i, 0))


def _cols(n, tm):
    return pl.BlockSpec((n, tm), lambda i: (0, i))


def _whole(shape):
    nd = len(shape)
    return pl.BlockSpec(shape, lambda i: (0,) * nd)


def _sds(shape, dtype):
    return jax.ShapeDtypeStruct(shape, dtype)


def _flip(v, d):
    return 1 - v if d else v


def _inproj_call(x, shift, scale, g1, w_int, w_q, w_kv, qg, kvg, cos256, sin256):
    s_len = x.shape[0]
    tm = min(ROW_TILE, s_len)

    def body(x_ref, sh_ref, sc_ref, g1_ref, w_ref, wq_ref, wkv_ref, qg_ref, kvg_ref, cos_ref, sin_ref,
             h_ref, sq_ref, sk_ref, sv_ref, sz_ref, cq_ref, ckv_ref, mz_ref, ga_ref, gb_ref, kpt_ref,
             qn_ref, qp_ref, kn_ref, vv_ref):
        xt = x_ref[...]
        r = lax.rsqrt(jnp.mean(xt * xt, axis=-1, keepdims=True) + EPS)
        h = (xt * r * g1_ref[...]) * (1.0 + sc_ref[...]) + sh_ref[...]
        hb = h.astype(BF16)
        h_ref[...] = hb

        def seg(a, b):
            return _dot_nt(hb[:, :HALF_D], w_ref[0, a:b, :]) + _dot_nt(hb[:, HALF_D:], w_ref[1, a:b, :])

        sq_ref[...] = (seg(O_SQ, O_SK) * SB_SCALE).astype(BF16)
        sk_ref[...] = seg(O_SK, O_SV).astype(BF16)
        sv_ref[...] = seg(O_SV, O_SZ).astype(BF16)
        sz_ref[...] = seg(O_SZ, O_CQ)
        mz_ref[...] = seg(O_MZ, O_GA)
        ga_ref[...] = seg(O_GA, O_GB)
        gb_ref[...] = seg(O_GB, O_KR)
        cos = cos_ref[...]
        sin = sin_ref[...]
        kr = seg(O_KR, O_END)
        kpt_ref[...] = (kr[:, :128] * cos[:, :128] + kr[:, 128:] * sin[:, :128]).astype(BF16)

        cq = seg(O_CQ, O_CKV)
        cq_ref[...] = cq
        rq = lax.rsqrt(jnp.mean(cq * cq, axis=-1, keepdims=True) + EPS)
        cqn = (cq * rq * qg_ref[...]).astype(BF16)
        qa = _dot(cqn, wq_ref[...])
        qn_ref[...] = qa[:, :512].astype(BF16)
        qp_ref[...] = (qa[:, 512:768] * cos + qa[:, 768:] * sin).astype(BF16)

        ckv = seg(O_CKV, O_MZ)
        ckv_ref[...] = ckv
        rk = lax.rsqrt(jnp.mean(ckv * ckv, axis=-1, keepdims=True) + EPS)
        ckvn = (ckv * rk * kvg_ref[...]).astype(BF16)
        kva = _dot(ckvn, wkv_ref[...])
        kn_ref[...] = kva[:, :512].astype(BF16)
        vv_ref[...] = kva[:, 512:].astype(BF16)

    outs = [
        (D_MODEL, BF16), (512, BF16), (512, BF16), (512, BF16), (512, F32), (Q_RANK, F32), (KV_RANK, F32),
        (512, F32), (D_MODEL, F32), (D_MODEL, F32), (128, BF16), (512, BF16), (256, BF16), (512, BF16), (512, BF16),
    ]
    return pl.pallas_call(
        body, name="inproj", grid=(s_len // tm,),
        out_shape=tuple(_sds((s_len, n), dt) for n, dt in outs),
        in_specs=[_rows(tm, D_MODEL), _whole((1, D_MODEL)), _whole((1, D_MODEL)), _whole((1, D_MODEL)),
                  _whole((2, W_INT, HALF_D)), _whole((Q_RANK, 1024)), _whole((KV_RANK, 1024)),
                  _whole((1, Q_RANK)), _whole((1, KV_RANK)), _rows(tm, 256), _rows(tm, 256)],
        out_specs=tuple(_rows(tm, n) for n, _ in outs),
        compiler_params=_params(("parallel",)),
    )(x, shift, scale, g1, w_int, w_q, w_kv, qg, kvg, cos256, sin256)


Z_CLAMP = 80.0 * LOG2E
RUN_CUTOFF = 110.0 * LOG2E


def _softplus_clamped(z):
    zc = jnp.minimum(z * LOG2E, Z_CLAMP)
    return zc, jnp.log2(1.0 + jnp.exp2(zc))


def _tri_sum(a, tri):
    return _dot(a.astype(BF16), tri)


def _sb_fwd_call(q, k, v):
    s_len = q.shape[0]
    tk = min(ATT_TILE, s_len)
    tq = min(SB_Q_TILES * ATT_TILE, s_len)
    r = tq // tk
    nq = s_len // tq

    def body(q_ref, k_ref, v_ref, o_ref, lt_ref, first_ref):
        i = pl.program_id(1)
        q2 = q_ref[...]
        lane = lax.broadcasted_iota(jnp.int32, (1, 256), 1)
        krow = lax.broadcasted_iota(jnp.int32, (tk, tk), 0)
        kcol = lax.broadcasted_iota(jnp.int32, (tk, tk), 1)
        row = lax.broadcasted_iota(jnp.int32, (tq, tk), 0)
        col = lax.broadcasted_iota(jnp.int32, (tq, tk), 1)
        later = (krow > kcol).astype(BF16)
        valids = [col + u * tk < row for u in range(r)]
        hms = [(lane // 64) == hh for hh in range(4)]
        qms = [jnp.where(hm, q2, jnp.zeros_like(q2)) for hm in hms]

        def block(j, carry, valid):
            runs, acc = list(carry[:4]), carry[4]
            off = pl.multiple_of(j * tk, tk)
            kb = k_ref[pl.ds(off, tk), :]
            vb = v_ref[pl.ds(off, tk), :]
            ws = []
            for hh in range(4):
                zc, sp = _softplus_clamped(_dot_nt(qms[hh], kb))
                lm = jnp.where(valid, sp, 0.0) if valid is not None else sp
                suf = _tri_sum(lm, later)
                w = jnp.exp2(zc - sp - suf - runs[hh])
                if valid is not None:
                    w = jnp.where(valid, w, 0.0)
                ws.append(w.astype(BF16))
                runs[hh] = runs[hh] + jnp.sum(lm, axis=1, keepdims=True)
            vstack = jnp.concatenate([jnp.where(hm, vb, jnp.zeros_like(vb)) for hm in hms], axis=0)
            acc = acc + _dot(jnp.concatenate(ws, axis=1), vstack)
            return (*runs, acc)

        zero = jnp.zeros((tq, 1), F32)
        carry = (zero, zero, zero, zero, jnp.zeros((tq, 256), F32))
        for u in reversed(range(r)):
            carry = block(i * r + u, carry, valids[u])

        def least_run(runs):
            return jnp.min(jnp.minimum(jnp.minimum(runs[0], runs[1]), jnp.minimum(runs[2], runs[3])))

        n_full = i * r

        def unfinished(state):
            return jnp.logical_and(state[0] < n_full, state[1] <= RUN_CUTOFF)

        def visit(state):
            cr = block(n_full - 1 - state[0], state[2:], None)
            return (state[0] + 1, least_run(cr[:4]), *cr)

        state = lax.while_loop(unfinished, visit, (jnp.int32(0), least_run(carry[:4]), *carry))
        carry = state[2:]
        first_ref[...] = jnp.full(first_ref.shape, n_full - state[0], jnp.int32)
        for hh in range(4):
            lt_ref[0, :, hh:hh + 1] = carry[hh]
        o_ref[...] = carry[4]

    return pl.pallas_call(
        body, name="sb_fwd", grid=(2, nq),
        out_shape=(_sds((s_len, SB_WIDTH), F32), _sds((2, s_len, 4), F32), _sds((2, nq, 8, 128), jnp.int32)),
        in_specs=[pl.BlockSpec((tq, 256), lambda g, i: (i, g)),
                  pl.BlockSpec((s_len, 256), lambda g, i: (0, g)),
                  pl.BlockSpec((s_len, 256), lambda g, i: (0, g))],
        out_specs=(pl.BlockSpec((tq, 256), lambda g, i: (i, g)),
                   pl.BlockSpec((1, tq, 4), lambda g, i: (g, i, 0)),
                   pl.BlockSpec((1, 1, 8, 128), lambda g, i: (g, i, 0, 0))),
        compiler_params=_params(("parallel", "parallel")),
    )(q, k, v)


def _sb_bwd_call(first, q, k, v, do, lt):
    s_len = q.shape[0]
    tk = min(ATT_TILE, s_len)
    tq = min(SB_Q_TILES * ATT_TILE, s_len)
    r = tq // tk
    nq = s_len // tq
    nq_fwd = first.shape[0] // 2
    per_fwd = nq // nq_fwd

    def body(first_ref, q_ref, k_ref, v_ref, do_ref, lt_ref, dq_ref, dk_ref, dv_ref):
        g = pl.program_id(0)
        i = pl.program_id(1)

        @pl.when(i == 0)
        def _():
            dk_ref[...] = jnp.zeros_like(dk_ref)
            dv_ref[...] = jnp.zeros_like(dv_ref)

        q2 = q_ref[...]
        do2 = do_ref[...].astype(BF16)
        lane = lax.broadcasted_iota(jnp.int32, (1, 256), 1)
        krow = lax.broadcasted_iota(jnp.int32, (tk, tk), 0)
        kcol = lax.broadcasted_iota(jnp.int32, (tk, tk), 1)
        row = lax.broadcasted_iota(jnp.int32, (tq, tk), 0)
        col = lax.broadcasted_iota(jnp.int32, (tq, tk), 1)
        earlier = (krow < kcol).astype(BF16)
        later = (krow > kcol).astype(BF16)
        valids = [col + u * tk < row for u in range(r)]
        hms = [(lane // 64) == hh for hh in range(4)]
        qms = [jnp.where(hm, q2, jnp.zeros_like(q2)) for hm in hms]
        doms = [jnp.where(hm, do2, jnp.zeros_like(do2)) for hm in hms]
        ltots = [lt_ref[0, :, hh:hh + 1] for hh in range(4)]
        q2t = jnp.transpose(q2.astype(F32))
        do2t = jnp.transpose(do_ref[...])
        subl = lax.broadcasted_iota(jnp.int32, (256, 1), 0)
        qtstack = jnp.concatenate(
            [jnp.where((subl // 64) == hh, q2t, 0.0).astype(BF16) for hh in range(4)], axis=1)
        dotstack = jnp.concatenate(
            [jnp.where((subl // 64) == hh, do2t, 0.0).astype(BF16) for hh in range(4)], axis=1)

        def block(j, carry, valid):
            lpre, ppre, dq = list(carry[0:4]), list(carry[4:8]), carry[8]
            off = pl.multiple_of(j * tk, tk)
            kb = k_ref[pl.ds(off, tk), :]
            vb = v_ref[pl.ds(off, tk), :]
            dzs, avs = [], []
            for hh in range(4):
                zc, sp = _softplus_clamped(_dot_nt(qms[hh], kb))
                lsig = zc - sp
                lm = jnp.where(valid, sp, 0.0) if valid is not None else sp
                rowsum = jnp.sum(lm, axis=1, keepdims=True)
                between = _tri_sum(lm, later) + ((ltots[hh] - lpre[hh]) - rowsum)
                a = jnp.exp2(lsig - between)
                if valid is not None:
                    a = jnp.where(valid, a, 0.0)
                p = a * _dot_nt(doms[hh], vb)
                pbefore = ppre[hh] + _tri_sum(p, earlier)
                dz = p - jnp.exp2(lsig) * (p + pbefore)
                if valid is not None:
                    dz = jnp.where(valid, dz, 0.0)
                dzs.append(dz.astype(BF16))
                avs.append(a.astype(BF16))
                lpre[hh] = lpre[hh] + rowsum
                ppre[hh] = ppre[hh] + jnp.sum(p, axis=1, keepdims=True)
            kstack = jnp.concatenate([jnp.where(hm, kb, jnp.zeros_like(kb)) for hm in hms], axis=0)
            dq = dq + _dot(jnp.concatenate(dzs, axis=1), kstack)
            dk_ref[:, pl.ds(off, tk)] += _dot(qtstack, jnp.concatenate(dzs, axis=0))
            dv_ref[:, pl.ds(off, tk)] += _dot(dotstack, jnp.concatenate(avs, axis=0))
            return (*lpre, *ppre, dq)

        zero = jnp.zeros((tq, 1), F32)
        start = jnp.minimum(first_ref[g * nq_fwd + i // per_fwd], i * r)
        carry = lax.fori_loop(start, i * r, lambda j, cr: block(j, cr, None),
                              (zero,) * 8 + (jnp.zeros((tq, 256), F32),))
        for u in range(r):
            carry = block(i * r + u, carry, valids[u])
        dq_ref[...] = carry[8].astype(BF16)

    return pl.pallas_call(
        body, name="sb_bwd",
        out_shape=(_sds((s_len, SB_WIDTH), BF16), _sds((SB_WIDTH, s_len), F32), _sds((SB_WIDTH, s_len), F32)),
        grid_spec=pltpu.PrefetchScalarGridSpec(
            num_scalar_prefetch=1, grid=(2, nq),
            in_specs=[pl.BlockSpec((tq, 256), lambda g, i, f: (i, g)),
                      pl.BlockSpec((s_len, 256), lambda g, i, f: (0, g)),
                      pl.BlockSpec((s_len, 256), lambda g, i, f: (0, g)),
                      pl.BlockSpec((tq, 256), lambda g, i, f: (i, g)),
                      pl.BlockSpec((1, tq, 4), lambda g, i, f: (g, i, 0))],
            out_specs=(pl.BlockSpec((tq, 256), lambda g, i, f: (i, g)),
                       pl.BlockSpec((256, s_len), lambda g, i, f: (g, 0)),
                       pl.BlockSpec((256, s_len), lambda g, i, f: (g, 0)))),
        compiler_params=_params(("parallel", "arbitrary")),
    )(first, q, k, v, do, lt)


def _mla_fwd_call(qn, qp, kn, kpt, v):
    s_len = qn.shape[0]
    tk = min(MLA_KEY_TILE, s_len)
    tq = min(FWD_Q_TILES * ATT_TILE, s_len)
    r = tq // tk
    nq = s_len // tq

    def body(qn_ref, qp_ref, kn_ref, kpt_ref, v_ref, o_ref, lse_ref):
        i = pl.program_id(1)
        qn2 = qn_ref[...]
        qp2 = qp_ref[...]
        lane256 = lax.broadcasted_iota(jnp.int32, (1, 256), 1)
        lane128 = lax.broadcasted_iota(jnp.int32, (1, 128), 1)
        krow = lax.broadcasted_iota(jnp.int32, (tk, tk), 0)
        kcol = lax.broadcasted_iota(jnp.int32, (tk, tk), 1)
        row = lax.broadcasted_iota(jnp.int32, (tq, tk), 0)
        col = lax.broadcasted_iota(jnp.int32, (tq, tk), 1)
        valids = [col + u * tk <= row for u in range(r)]
        m64s = [(lane256 // 64) == hh for hh in range(4)]
        half = [(lane128 // 64) == u for u in range(2)]
        m32s = [(lane128 // 32) == hh for hh in range(4)]
        qcs = []
        for hh in range(4):
            qpair = qn2[:, 128 * (hh // 2):128 * (hh // 2) + 128]
            qcs.append(jnp.concatenate([jnp.where(half[hh % 2], qpair, jnp.zeros_like(qpair)),
                                        jnp.where(m32s[hh], qp2, jnp.zeros_like(qp2))], axis=1))

        def by_head(vals):
            return jnp.where(m64s[0], vals[0], jnp.where(m64s[1], vals[1], jnp.where(m64s[2], vals[2], vals[3])))

        def block(j, carry, valid):
            ms, ls, acc = list(carry[0:4]), list(carry[4:8]), carry[8]
            off = pl.multiple_of(j * tk, tk)
            knb = kn_ref[pl.ds(off, tk), :]
            kpb = kpt_ref[pl.ds(off, tk), :]
            vb = v_ref[pl.ds(off, tk), :]
            kcs = [jnp.concatenate([knb[:, 128 * pp:128 * pp + 128], kpb], axis=1) for pp in range(2)]
            ps, alphas = [], []
            for hh in range(4):
                s = _dot_nt(qcs[hh], kcs[hh // 2]) * (MLA_SCALE * LOG2E)
                if valid is not None:
                    s = jnp.where(valid, s, -1e30)
                mn = jnp.maximum(ms[hh], jnp.max(s, axis=1, keepdims=True))
                p = jnp.exp2(s - mn)
                alpha = jnp.exp2(ms[hh] - mn)
                ls[hh] = alpha * ls[hh] + jnp.sum(p, axis=1, keepdims=True)
                ms[hh] = mn
                ps.append(p.astype(BF16))
                alphas.append(alpha)
            pvs = []
            for pp in range(2):
                vpair = vb[:, 128 * pp:128 * pp + 128]
                vstack = jnp.concatenate([jnp.where(hf, vpair, jnp.zeros_like(vpair)) for hf in half], axis=0)
                pvs.append(_dot(jnp.concatenate(ps[2 * pp:2 * pp + 2], axis=1), vstack))
            acc = by_head(alphas) * acc + jnp.concatenate(pvs, axis=1)
            return (*ms, *ls, acc)

        neg = jnp.full((tq, 1), -1e30, F32)
        zero = jnp.zeros((tq, 1), F32)
        carry = lax.fori_loop(0, i * r, lambda j, cr: block(j, cr, None),
                              (neg,) * 4 + (zero,) * 4 + (jnp.zeros((tq, 256), F32),))
        for u in range(r):
            carry = block(i * r + u, carry, valids[u])
        o_ref[...] = carry[8] / by_head(list(carry[4:8]))
        for hh in range(4):
            lse_ref[0, :, hh:hh + 1] = (carry[hh] + jnp.log2(carry[4 + hh])) * (1.0 / LOG2E)

    return pl.pallas_call(
        body, name="mla_fwd", grid=(2, nq),
        out_shape=(_sds((s_len, MLA_WIDTH), F32), _sds((2, s_len, 4), F32)),
        in_specs=[pl.BlockSpec((tq, 256), lambda g, i: (i, g)),
                  pl.BlockSpec((tq, 128), lambda g, i: (i, g)),
                  pl.BlockSpec((s_len, 256), lambda g, i: (0, g)),
                  pl.BlockSpec((s_len, 128), lambda g, i: (0, 0)),
                  pl.BlockSpec((s_len, 256), lambda g, i: (0, g))],
        out_specs=(pl.BlockSpec((tq, 256), lambda g, i: (i, g)),
                   pl.BlockSpec((1, tq, 4), lambda g, i: (g, i, 0))),
        compiler_params=_params(("parallel", "parallel")),
    )(qn, qp, kn, kpt, v)


def _mla_bwd_call(qn, qp, kn, kpt, v, o, do, lse):
    s_len = qn.shape[0]
    tk = min(MLA_KEY_TILE, s_len)
    tq = min(ATT_Q_TILES * ATT_TILE, s_len)
    r = tq // tk
    nq = s_len // tq

    def body(qn_ref, qp_ref, kn_ref, kpt_ref, v_ref, o_ref, do_ref, lse_ref,
             dqn_ref, dqp_ref, dkn_ref, dkpt_ref, dv_ref):
        g = pl.program_id(0)
        i = pl.program_id(1)

        @pl.when(i == 0)
        def _():
            dkn_ref[...] = jnp.zeros_like(dkn_ref)
            dv_ref[...] = jnp.zeros_like(dv_ref)

        @pl.when((i == 0) & (g == 0))
        def _():
            dkpt_ref[...] = jnp.zeros_like(dkpt_ref)

        qn2 = qn_ref[...]
        qp2 = qp_ref[...]
        dof = do_ref[...]
        dob = dof.astype(BF16)
        prod = dof * o_ref[...]
        lane256 = lax.broadcasted_iota(jnp.int32, (1, 256), 1)
        lane128 = lax.broadcasted_iota(jnp.int32, (1, 128), 1)
        krow = lax.broadcasted_iota(jnp.int32, (tk, tk), 0)
        kcol = lax.broadcasted_iota(jnp.int32, (tk, tk), 1)
        row = lax.broadcasted_iota(jnp.int32, (tq, tk), 0)
        col = lax.broadcasted_iota(jnp.int32, (tq, tk), 1)
        valids = [col + u * tk <= row for u in range(r)]
        m64s = [(lane256 // 64) == hh for hh in range(4)]
        half = [(lane128 // 64) == u for u in range(2)]
        m32s = [(lane128 // 32) == hh for hh in range(4)]
        qcs, doms = [], []
        for hh in range(4):
            sl = slice(128 * (hh // 2), 128 * (hh // 2) + 128)
            qpair = qn2[:, sl]
            dpair = dob[:, sl]
            qcs.append(jnp.concatenate([jnp.where(half[hh % 2], qpair, jnp.zeros_like(qpair)),
                                        jnp.where(m32s[hh], qp2, jnp.zeros_like(qp2))], axis=1))
            doms.append(jnp.where(half[hh % 2], dpair, jnp.zeros_like(dpair)))
        dsums = [jnp.sum(jnp.where(m64, prod, 0.0), axis=1, keepdims=True) * MLA_SCALE for m64 in m64s]
        lses = [lse_ref[0, :, hh:hh + 1] * LOG2E for hh in range(4)]
        qn2t = jnp.transpose(qn2.astype(F32))
        qp2t = jnp.transpose(qp2.astype(F32))
        do2t = jnp.transpose(dof)
        sub128 = lax.broadcasted_iota(jnp.int32, (128, 1), 0)
        qtstacks, dotstacks = [], []
        for pp in range(2):
            qts, dts = [], []
            for u in range(2):
                hh = 2 * pp + u
                qts.append(jnp.concatenate(
                    [jnp.where((sub128 // 64) == u, qn2t[128 * pp:128 * pp + 128, :], 0.0),
                     jnp.where((sub128 // 32) == hh, qp2t, 0.0)], axis=0).astype(BF16))
                dts.append(jnp.where((sub128 // 64) == u, do2t[128 * pp:128 * pp + 128, :], 0.0).astype(BF16))
            qtstacks.append(jnp.concatenate(qts, axis=1))
            dotstacks.append(jnp.concatenate(dts, axis=1))

        def block(j, carry, valid):
            dqn, dqp = carry
            off = pl.multiple_of(j * tk, tk)
            knb = kn_ref[pl.ds(off, tk), :]
            kpb = kpt_ref[pl.ds(off, tk), :]
            vb = v_ref[pl.ds(off, tk), :]
            dqn_parts = []
            dkp = None
            for pp in range(2):
                sl = slice(128 * pp, 128 * pp + 128)
                knp = knb[:, sl]
                vpair = vb[:, sl]
                kc = jnp.concatenate([knp, kpb], axis=1)
                dss, pbs, kcms = [], [], []
                for u in range(2):
                    hh = 2 * pp + u
                    s = _dot_nt(qcs[hh], kc) * (MLA_SCALE * LOG2E)
                    if valid is not None:
                        s = jnp.where(valid, s, -1e30)
                    p = jnp.exp2(s - lses[hh])
                    ds = p * (_dot_nt(doms[hh], vpair) * MLA_SCALE - dsums[hh])
                    dss.append(ds.astype(BF16))
                    pbs.append(p.astype(BF16))
                    kcms.append(jnp.concatenate([jnp.where(half[u], knp, jnp.zeros_like(knp)),
                                                 jnp.where(m32s[hh], kpb, jnp.zeros_like(kpb))], axis=1))
                dqc = _dot(jnp.concatenate(dss, axis=1), jnp.concatenate(kcms, axis=0))
                dqn_parts.append(dqc[:, :128])
                dqp = dqp + dqc[:, 128:]
                dkc = _dot(qtstacks[pp], jnp.concatenate(dss, axis=0))
                dkn_ref[128 * pp:128 * pp + 128, pl.ds(off, tk)] += dkc[:128, :]
                dkp = dkc[128:, :] if dkp is None else dkp + dkc[128:, :]
                dv_ref[128 * pp:128 * pp + 128, pl.ds(off, tk)] += _dot(dotstacks[pp], jnp.concatenate(pbs, axis=0))
            dqn = dqn + jnp.concatenate(dqn_parts, axis=1)
            dkpt_ref[:, pl.ds(off, tk)] += dkp
            return dqn, dqp

        carry = lax.fori_loop(0, i * r, lambda j, cr: block(j, cr, None),
                              (jnp.zeros((tq, 256), F32), jnp.zeros((tq, 128), F32)))
        for u in range(r):
            carry = block(i * r + u, carry, valids[u])
        dqn, dqp = carry
        dqn_ref[...] = dqn.astype(BF16)
        dqp_ref[...] = dqp.astype(BF16)

    return pl.pallas_call(
        body, name="mla_bwd", grid=(2, nq),
        out_shape=(_sds((s_len, 512), BF16), _sds((s_len, 256), BF16), _sds((512, s_len), F32),
                   _sds((128, s_len), F32), _sds((512, s_len), F32)),
        in_specs=[pl.BlockSpec((tq, 256), lambda g, i: (i, g)),
                  pl.BlockSpec((tq, 128), lambda g, i: (i, g)),
                  pl.BlockSpec((s_len, 256), lambda g, i: (0, g)),
                  pl.BlockSpec((s_len, 128), lambda g, i: (0, 0)),
                  pl.BlockSpec((s_len, 256), lambda g, i: (0, g)),
                  pl.BlockSpec((tq, 256), lambda g, i: (i, g)),
                  pl.BlockSpec((tq, 256), lambda g, i: (i, g)),
                  pl.BlockSpec((1, tq, 4), lambda g, i: (g, i, 0))],
        out_specs=(pl.BlockSpec((tq, 256), lambda g, i: (i, g)),
                   pl.BlockSpec((tq, 128), lambda g, i: (i, g)),
                   pl.BlockSpec((256, s_len), lambda g, i: (g, 0)),
                   pl.BlockSpec((128, s_len), lambda g, i: (0, 0)),
                   pl.BlockSpec((256, s_len), lambda g, i: (g, 0))),
        compiler_params=_params(("arbitrary", "arbitrary")),
    )(qn, qp, kn, kpt, v, o, do, lse)


def _post_call(x, tgt, oa, ob, sz, mz, ga, gb, gate, gf, wa, wb, wo):
    s_len = x.shape[0]
    tm = min(ROW_TILE, s_len)

    def body(x_ref, t_ref, oa_ref, ob_ref, sz_ref, mz_ref, ga_ref, gb_ref, gate_ref, gf_ref,
             wa_ref, wb_ref, wo_ref,
             dx2_ref, doa_ref, dob_ref, dsz_ref, dmz_ref, dga_ref, dgb_ref,
             dwo_ref, dwa_ref, dwb_ref, dgf_ref, dgate_ref, loss_ref):
        @pl.when(pl.program_id(0) == 0)
        def _():
            dwo_ref[...] = jnp.zeros_like(dwo_ref)
            dwa_ref[...] = jnp.zeros_like(dwa_ref)
            dwb_ref[...] = jnp.zeros_like(dwb_ref)
            dgf_ref[...] = jnp.zeros_like(dgf_ref)
            dgate_ref[...] = jnp.zeros_like(dgate_ref)
            loss_ref[...] = jnp.zeros_like(loss_ref)

        gate = gate_ref[...]
        gf = gf_ref[...]
        oa = oa_ref[...]
        ob = ob_ref[...]
        sz = sz_ref[...]
        mz = mz_ref[...]
        sa = _sigmoid(sz)
        sb = _sigmoid(mz)
        silu_a = sz * sa
        silu_b = mz * sb
        ua = (oa * silu_a).astype(BF16)
        ub = (ob * silu_b).astype(BF16)
        ya = _dot(ua, wa_ref[...])
        yb = _dot(ub, wb_ref[...])
        sga = _sigmoid(ga_ref[...])
        sgb = _sigmoid(gb_ref[...])
        merged = (sga * ya + sgb * yb).astype(BF16)
        out = _dot(merged, wo_ref[...])
        x2 = x_ref[...] + gate * out
        r2 = lax.rsqrt(jnp.mean(x2 * x2, axis=-1, keepdims=True) + EPS)
        xhat = x2 * r2
        err = xhat * gf - t_ref[...]
        loss_ref[...] += 0.5 * jnp.sum(jnp.sum(err * err, axis=1, keepdims=True), axis=0, keepdims=True) / D_MODEL
        dy = err * (1.0 / D_MODEL)
        dgf_ref[...] += jnp.sum(dy * xhat, axis=0, keepdims=True)
        dxhat = dy * gf
        dx2 = r2 * (dxhat - xhat * jnp.mean(dxhat * xhat, axis=-1, keepdims=True))
        dx2_ref[...] = dx2
        dgate_ref[...] += jnp.sum(dx2 * out, axis=0, keepdims=True)
        dout = (dx2 * gate).astype(BF16)
        dmerged = _dot_nt(dout, wo_ref[...])
        dwo_ref[...] += _dot_tn(merged, dout)
        dya = dmerged * sga
        dyb = dmerged * sgb
        dga_ref[...] = (dya * ya * (1.0 - sga)).astype(BF16)
        dgb_ref[...] = (dyb * yb * (1.0 - sgb)).astype(BF16)
        dyab = dya.astype(BF16)
        dybb = dyb.astype(BF16)
        dua = _dot_nt(dyab, wa_ref[...])
        dub = _dot_nt(dybb, wb_ref[...])
        dwa_ref[...] += _dot_tn(ua, dyab)
        dwb_ref[...] += _dot_tn(ub, dybb)
        doa_ref[...] = dua * silu_a
        dob_ref[...] = dub * silu_b
        dsz_ref[...] = (dua * oa * (sa * (1.0 + sz * (1.0 - sa)))).astype(BF16)
        dmz_ref[...] = (dub * ob * (sb * (1.0 + mz * (1.0 - sb)))).astype(BF16)

    return pl.pallas_call(
        body, name="post", grid=(s_len // tm,),
        out_shape=(_sds((s_len, D_MODEL), F32), _sds((s_len, 512), F32), _sds((s_len, 512), F32),
                   _sds((s_len, 512), BF16), _sds((s_len, 512), BF16),
                   _sds((s_len, D_MODEL), BF16), _sds((s_len, D_MODEL), BF16),
                   _sds((D_MODEL, D_MODEL), F32), _sds((512, D_MODEL), F32), _sds((512, D_MODEL), F32),
                   _sds((1, D_MODEL), F32), _sds((1, D_MODEL), F32), _sds((1, 128), F32)),
        in_specs=[_rows(tm, D_MODEL), _rows(tm, D_MODEL), _rows(tm, 512), _rows(tm, 512), _rows(tm, 512),
                  _rows(tm, 512), _rows(tm, D_MODEL), _rows(tm, D_MODEL), _whole((1, D_MODEL)), _whole((1, D_MODEL)),
                  _whole((512, D_MODEL)), _whole((512, D_MODEL)), _whole((D_MODEL, D_MODEL))],
        out_specs=(_rows(tm, D_MODEL), _rows(tm, 512), _rows(tm, 512), _rows(tm, 512), _rows(tm, 512),
                   _rows(tm, D_MODEL), _rows(tm, D_MODEL),
                   _whole((D_MODEL, D_MODEL)), _whole((512, D_MODEL)), _whole((512, D_MODEL)),
                   _whole((1, D_MODEL)), _whole((1, D_MODEL)), _whole((1, 128))),
        compiler_params=_params(("arbitrary",)),
    )(x, tgt, oa, ob, sz, mz, ga, gb, gate, gf, wa, wb, wo)


def _bwdprep_call(dsq, dsk, dsv, dsz, dqn, dqp, dkn, dvv, dkpt, dmz, dga, dgb, cq, ckv, cos256, sin256,
                  qg, kvg, w_q, w_kv):
    s_len = cq.shape[0]
    tm = min(ROW_TILE, s_len)

    def body(dsq_ref, dsk_ref, dsv_ref, dsz_ref, dqn_ref, dqp_ref, dkn_ref, dvv_ref, dkpt_ref, dmz_ref,
             dga_ref, dgb_ref, cq_ref, ckv_ref, cos_ref, sin_ref, qg_ref, kvg_ref, wq_ref, wkv_ref,
             dp_ref, dwq_ref, dwkv_ref, dqg_ref, dkvg_ref):
        @pl.when(pl.program_id(0) == 0)
        def _():
            dwq_ref[...] = jnp.zeros_like(dwq_ref)
            dwkv_ref[...] = jnp.zeros_like(dwkv_ref)
            dqg_ref[...] = jnp.zeros_like(dqg_ref)
            dkvg_ref[...] = jnp.zeros_like(dkvg_ref)

        cos = cos_ref[...]
        sin = sin_ref[...]
        dp_ref[:, O_SQ:O_SK] = dsq_ref[...] * jnp.asarray(SB_SCALE, BF16)
        dp_ref[:, O_SK:O_SV] = jnp.transpose(dsk_ref[...]).astype(BF16)
        dp_ref[:, O_SV:O_SZ] = jnp.transpose(dsv_ref[...]).astype(BF16)
        dp_ref[:, O_SZ:O_CQ] = dsz_ref[...]
        dp_ref[:, O_MZ:O_GA] = dmz_ref[...]
        dp_ref[:, O_GA:O_GB] = dga_ref[...]
        dp_ref[:, O_GB:O_KR] = dgb_ref[...]
        dkp = jnp.transpose(dkpt_ref[...])
        dp_ref[:, O_KR:O_KR + 128] = (dkp * cos[:, :128]).astype(BF16)
        dp_ref[:, O_KR + 128:O_END] = (dkp * sin[:, :128]).astype(BF16)
        dp_ref[:, O_END:W_INT] = jnp.zeros((tm, W_INT - O_END), BF16)

        cq = cq_ref[...]
        rq = lax.rsqrt(jnp.mean(cq * cq, axis=-1, keepdims=True) + EPS)
        cqh = cq * rq
        qg = qg_ref[...]
        cqn = (cqh * qg).astype(BF16)
        dqp = dqp_ref[...].astype(F32)
        dqa = jnp.concatenate([dqn_ref[...], (dqp * cos).astype(BF16), (dqp * sin).astype(BF16)], axis=1)
        dcqn = _dot_nt(dqa, wq_ref[...])
        dwq_ref[...] += _dot_tn(cqn, dqa)
        dqg_ref[...] += jnp.sum(dcqn * cqh, axis=0, keepdims=True)
        dh = dcqn * qg
        dcq = rq * (dh - cqh * jnp.mean(dh * cqh, axis=-1, keepdims=True))
        dp_ref[:, O_CQ:O_CKV] = dcq.astype(BF16)

        ckv = ckv_ref[...]
        rk = lax.rsqrt(jnp.mean(ckv * ckv, axis=-1, keepdims=True) + EPS)
        ckh = ckv * rk
        kvg = kvg_ref[...]
        ckvn = (ckh * kvg).astype(BF16)
        dkva = jnp.concatenate([jnp.transpose(dkn_ref[...]).astype(BF16),
                                jnp.transpose(dvv_ref[...]).astype(BF16)], axis=1)
        dckvn = _dot_nt(dkva, wkv_ref[...])
        dwkv_ref[...] += _dot_tn(ckvn, dkva)
        dkvg_ref[...] += jnp.sum(dckvn * ckh, axis=0, keepdims=True)
        dh2 = dckvn * kvg
        dckv = rk * (dh2 - ckh * jnp.mean(dh2 * ckh, axis=-1, keepdims=True))
        dp_ref[:, O_CKV:O_MZ] = dckv.astype(BF16)

    return pl.pallas_call(
        body, name="bwdprep", grid=(s_len // tm,),
        out_shape=(_sds((s_len, W_INT), BF16), _sds((Q_RANK, 1024), F32), _sds((KV_RANK, 1024), F32),
                   _sds((1, Q_RANK), F32), _sds((1, KV_RANK), F32)),
        in_specs=[_rows(tm, 512), _cols(512, tm), _cols(512, tm), _rows(tm, 512), _rows(tm, 512), _rows(tm, 256),
                  _cols(512, tm), _cols(512, tm), _cols(128, tm), _rows(tm, 512), _rows(tm, D_MODEL),
                  _rows(tm, D_MODEL), _rows(tm, Q_RANK), _rows(tm, KV_RANK), _rows(tm, 256), _rows(tm, 256),
                  _whole((1, Q_RANK)), _whole((1, KV_RANK)), _whole((Q_RANK, 1024)), _whole((KV_RANK, 1024))],
        out_specs=(_rows(tm, W_INT), _whole((Q_RANK, 1024)), _whole((KV_RANK, 1024)),
                   _whole((1, Q_RANK)), _whole((1, KV_RANK))),
        compiler_params=_params(("arbitrary",)),
    )(dsq, dsk, dsv, dsz, dqn, dqp, dkn, dvv, dkpt, dmz, dga, dgb, cq, ckv, cos256, sin256, qg, kvg, w_q, w_kv)


def _dh_call(dproj, w_int_t, x, dx2, scale, g1):
    s_len = x.shape[0]
    tm = min(2 * ROW_TILE, s_len)

    def body(dp_ref, wt_ref, x_ref, dx2_ref, sc_ref, g1_ref, gx_ref, dsh_ref, dsc_ref, dg1_ref):
        @pl.when(pl.program_id(0) == 0)
        def _():
            dsh_ref[...] = jnp.zeros_like(dsh_ref)
            dsc_ref[...] = jnp.zeros_like(dsc_ref)
            dg1_ref[...] = jnp.zeros_like(dg1_ref)

        dp = dp_ref[...]
        dh = jnp.concatenate([_dot(dp, wt_ref[0]), _dot(dp, wt_ref[1])], axis=1)
        xt = x_ref[...]
        r = lax.rsqrt(jnp.mean(xt * xt, axis=-1, keepdims=True) + EPS)
        xh = xt * r
        g1 = g1_ref[...]
        xg = xh * g1
        dsh_ref[...] += jnp.sum(dh, axis=0, keepdims=True)
        dsc_ref[...] += jnp.sum(dh * xg, axis=0, keepdims=True)
        dxg = dh * (1.0 + sc_ref[...])
        dg1_ref[...] += jnp.sum(dxg * xh, axis=0, keepdims=True)
        dxh = dxg * g1
        gx_ref[...] = dx2_ref[...] + r * (dxh - xh * jnp.mean(dxh * xh, axis=-1, keepdims=True))

    return pl.pallas_call(
        body, name="dh", grid=(s_len // tm,),
        out_shape=(_sds((s_len, D_MODEL), F32), _sds((1, D_MODEL), F32), _sds((1, D_MODEL), F32),
                   _sds((1, D_MODEL), F32)),
        in_specs=[_rows(tm, W_INT), _whole((2, W_INT, HALF_D)), _rows(tm, D_MODEL), _rows(tm, D_MODEL),
                  _whole((1, D_MODEL)), _whole((1, D_MODEL))],
        out_specs=(_rows(tm, D_MODEL), _whole((1, D_MODEL)), _whole((1, D_MODEL)), _whole((1, D_MODEL))),
        compiler_params=_params(("arbitrary",)),
    )(dproj, w_int_t, x, dx2, scale, g1)


def _small_call(svg, ct, dmod_sh):
    def body(sv_ref, ct_ref, dm_ref, tot_ref, gwada_ref):
        acc = sv_ref[0:1, :]
        for d in range(1, N_DEV):
            acc = acc + sv_ref[d:d + 1, :]
        tot_ref[...] = acc
        gwada_ref[...] = lax.dot_general(ct_ref[...], dm_ref[...], (((1,), (0,)), ((), ())),
                                         precision=lax.Precision.HIGHEST, preferred_element_type=F32)

    vmem = pl.BlockSpec(memory_space=pltpu.VMEM)
    return pl.pallas_call(
        body, name="small_grads",
        out_shape=(_sds((1, 8 * SV_COLS), F32), _sds((D_MODEL, 768), F32)),
        in_specs=[vmem, vmem, vmem], out_specs=(vmem, vmem),
        compiler_params=_params(),
    )(svg, ct, dmod_sh)


def _adamw_tile_rows(rows, cols):
    budget = 2 << 20
    if rows * cols * 4 <= budget or rows % 8:
        return rows
    best = 8
    for tr in range(8, rows + 1, 8):
        if rows % tr == 0 and tr * cols * 4 <= budget:
            best = tr
    return best


def _adamw_call(name, w, g, m, v):
    rows, cols = w.shape
    tr = _adamw_tile_rows(rows, cols)

    def body(w_ref, g_ref, m_ref, v_ref, d_ref, nm_ref, nv_ref):
        gg = g_ref[...]
        m2 = ADAM_B1 * m_ref[...] + (1.0 - ADAM_B1) * gg
        v2 = ADAM_B2 * v_ref[...] + (1.0 - ADAM_B2) * (gg * gg)
        m_hat = m2 / (1.0 - ADAM_B1 ** ADAM_STEP)
        v_hat = v2 / (1.0 - ADAM_B2 ** ADAM_STEP)
        d_ref[...] = -ADAM_LR * (m_hat / (jnp.sqrt(v_hat) + ADAM_EPS) + ADAM_WD * w_ref[...])
        nm_ref[...] = m2
        nv_ref[...] = v2

    spec = pl.BlockSpec((tr, cols), lambda i: (i, 0))
    return pl.pallas_call(
        body, name="adamw_" + name, grid=(rows // tr,),
        out_shape=(_sds((rows, cols), F32),) * 3,
        in_specs=[spec] * 4, out_specs=(spec,) * 3,
        compiler_params=_params(("parallel",)),
    )(w, g, m, v)


IN_SHARD = IN_WIDTH // N_CHIPS
HALF_D = D_MODEL // 2
SMALL_ROWS = (576, 512, 1024, 1024, 2048)
SMALL_TOTAL = sum(SMALL_ROWS)
SMALL_HALF = SMALL_TOTAL // 2
SMALL_SUM_ROWS = 432


def _gather_call(c_row, w_ada_sh, pack_in, pack_small):
    def body(c_ref, wada_ref, pki_ref, pks_ref, mg_ref, cg_ref, gwi_ref, gws_ref,
             cv, ssem_c, rsem_c, ssem_m, rsem_m, ssem_w, rsem_w, ssem_f, rsem_f, lsem):
        x, y, c = lax.axis_index("x"), lax.axis_index("y"), lax.axis_index("c")
        me = 4 * x + 2 * y + c
        chip = 2 * x + y
        rel3 = [(1, 0), (0, 1), (1, 1)]
        packs = [(pki_ref, gwi_ref), (pks_ref, gws_ref)]

        def slot(a, gw, k, h):
            return gw.at[h, k] if a == 0 else gw.at[k, h]

        sends = []
        for j, (dx, dy) in enumerate(rel3):
            for a, (pk, gw) in enumerate(packs):
                cp = pltpu.make_async_remote_copy(
                    src_ref=pk.at[c], dst_ref=slot(a, gw, chip, c), send_sem=ssem_w.at[j, a],
                    recv_sem=rsem_w.at[j, a], device_id=(_flip(x, dx), _flip(y, dy), c), device_id_type=MESH)
                cp.start()
                sends.append(cp)
        owns = []
        for a, (pk, gw) in enumerate(packs):
            for h in range(2):
                own = pltpu.make_async_copy(pk.at[h], slot(a, gw, chip, h), lsem.at[a, h])
                own.start()
                owns.append(own)

        cv[me] = c_ref[...]
        for r in range(1, N_DEV):
            dx, dy, dc = (r >> 2) & 1, (r >> 1) & 1, r & 1
            cp = pltpu.make_async_remote_copy(
                src_ref=c_ref, dst_ref=cv.at[me], send_sem=ssem_c.at[r - 1], recv_sem=rsem_c.at[r - 1],
                device_id=(_flip(x, dx), _flip(y, dy), _flip(c, dc)), device_id_type=MESH)
            cp.start()
            sends.append(cp)
        for r in range(1, N_DEV):
            dx, dy, dc = (r >> 2) & 1, (r >> 1) & 1, r & 1
            src = 4 * _flip(x, dx) + 2 * _flip(y, dy) + _flip(c, dc)
            pltpu.make_async_remote_copy(
                src_ref=c_ref, dst_ref=cv.at[src], send_sem=ssem_c.at[r - 1], recv_sem=rsem_c.at[r - 1],
                device_id=(x, y, c), device_id_type=MESH).wait_recv()
        rows = lax.broadcasted_iota(jnp.int32, (N_DEV, D_MODEL), 0)
        call = jnp.zeros((N_DEV, D_MODEL), F32)
        for b in range(N_DEV):
            call = jnp.where(rows == b, jnp.broadcast_to(cv[b], (N_DEV, D_MODEL)), call)
        cg_ref[...] = call

        mg_ref[chip] = lax.dot_general(call, wada_ref[...], (((1,), (0,)), ((), ())),
                                       precision=lax.Precision.HIGHEST, preferred_element_type=F32)
        for j, (dx, dy) in enumerate(rel3):
            cp = pltpu.make_async_remote_copy(
                src_ref=mg_ref.at[chip], dst_ref=mg_ref.at[chip], send_sem=ssem_m.at[j], recv_sem=rsem_m.at[j],
                device_id=(_flip(x, dx), _flip(y, dy), c), device_id_type=MESH)
            cp.start()
            sends.append(cp)
        for j, (dx, dy) in enumerate(rel3):
            src_chip = 2 * _flip(x, dx) + _flip(y, dy)
            pltpu.make_async_remote_copy(
                src_ref=mg_ref.at[src_chip], dst_ref=mg_ref.at[src_chip], send_sem=ssem_m.at[j],
                recv_sem=rsem_m.at[j], device_id=(x, y, c), device_id_type=MESH).wait_recv()
        for j, (dx, dy) in enumerate(rel3):
            src_chip = 2 * _flip(x, dx) + _flip(y, dy)
            for a, (pk, gw) in enumerate(packs):
                pltpu.make_async_remote_copy(
                    src_ref=pk.at[c], dst_ref=slot(a, gw, src_chip, c), send_sem=ssem_w.at[j, a],
                    recv_sem=rsem_w.at[j, a], device_id=(x, y, c), device_id_type=MESH).wait_recv()
                cp = pltpu.make_async_remote_copy(
                    src_ref=slot(a, gw, src_chip, c), dst_ref=slot(a, gw, src_chip, c), send_sem=ssem_f.at[j, a],
                    recv_sem=rsem_f.at[j, a], device_id=(x, y, 1 - c), device_id_type=MESH)
                cp.start()
                sends.append(cp)
        for j, (dx, dy) in enumerate(rel3):
            src_chip = 2 * _flip(x, dx) + _flip(y, dy)
            for a, (pk, gw) in enumerate(packs):
                pltpu.make_async_remote_copy(
                    src_ref=pk.at[c], dst_ref=slot(a, gw, src_chip, 1 - c), send_sem=ssem_f.at[j, a],
                    recv_sem=rsem_f.at[j, a], device_id=(x, y, c), device_id_type=MESH).wait_recv()
        for cp in sends:
            cp.wait_send()
        for own in owns:
            own.wait()

    vmem = pl.BlockSpec(memory_space=pltpu.VMEM)
    return pl.pallas_call(
        body, name="gather_fwd",
        out_shape=(_sds((N_CHIPS, N_DEV, 768), F32), _sds((N_DEV, D_MODEL), F32),
                   _sds((2, N_CHIPS, IN_SHARD, HALF_D), BF16), _sds((N_CHIPS, 2, SMALL_HALF, LANES), BF16)),
        in_specs=[vmem, vmem, vmem, vmem], out_specs=(vmem, vmem, vmem, vmem),
        scratch_shapes=[
            pltpu.VMEM((N_DEV, 1, D_MODEL), F32),
            pltpu.SemaphoreType.DMA((N_DEV - 1,)), pltpu.SemaphoreType.DMA((N_DEV - 1,)),
            pltpu.SemaphoreType.DMA((3,)), pltpu.SemaphoreType.DMA((3,)),
            pltpu.SemaphoreType.DMA((3, 2)), pltpu.SemaphoreType.DMA((3, 2)),
            pltpu.SemaphoreType.DMA((3, 2)), pltpu.SemaphoreType.DMA((3, 2)),
            pltpu.SemaphoreType.DMA((2, 2)),
        ],
        compiler_params=_params(),
    )(c_row, w_ada_sh, pack_in, pack_small)


def _reduce_call(g_in, g_small, sv):
    def body(gi_ref, gs_ref, sv_ref, fi_ref, fs_ref, svg_ref, pair_i, pair_s, send_i, send_s, land_i, land_s,
             ssem_p, rsem_p, ssem_g, rsem_g, ssem_s, rsem_s, ssem_x, rsem_x):
        x, y, c = lax.axis_index("x"), lax.axis_index("y"), lax.axis_index("c")
        me = 4 * x + 2 * y + c
        chip = 2 * x + y
        rel3 = [(1, 0), (0, 1), (1, 1)]
        payloads = [(gi_ref, pair_i, send_i, land_i, fi_ref), (gs_ref, pair_s, send_s, land_s, fs_ref)]
        where = [lambda k, h: N_CHIPS * h + k, lambda k, h: 2 * k + h]
        copies = []

        for k in range(N_CHIPS):
            for a, (g, pair, _, _, _) in enumerate(payloads):
                cp = pltpu.make_async_remote_copy(
                    src_ref=g.at[where[a](k, 1 - c)], dst_ref=pair.at[k], send_sem=ssem_p.at[k, a],
                    recv_sem=rsem_p.at[k, a], device_id=(x, y, 1 - c), device_id_type=MESH)
                cp.start()
                copies.append(cp)

        for r in range(1, N_DEV):
            dx, dy, dc = (r >> 2) & 1, (r >> 1) & 1, r & 1
            cp = pltpu.make_async_remote_copy(
                src_ref=sv_ref, dst_ref=svg_ref.at[me], send_sem=ssem_s.at[r - 1], recv_sem=rsem_s.at[r - 1],
                device_id=(_flip(x, dx), _flip(y, dy), _flip(c, dc)), device_id_type=MESH)
            cp.start()
            copies.append(cp)
        svg_ref[me] = sv_ref[...]

        def pair_sum(k, store_in, store_small):
            for a, (g, pair, _, _, _) in enumerate(payloads):
                pltpu.make_async_remote_copy(
                    src_ref=g.at[where[a](k, c)], dst_ref=pair.at[k], send_sem=ssem_p.at[k, a],
                    recv_sem=rsem_p.at[k, a], device_id=(x, y, c), device_id_type=MESH).wait_recv()
            for qd in range(HALF_D // LANES):
                sl = slice(LANES * qd, LANES * qd + LANES)
                store_in(sl, gi_ref[where[0](k, c), :, sl].astype(F32) + pair_i[k, :, sl].astype(F32))

            def rows(i, carry):
                sl = pl.ds(pl.multiple_of(i * SMALL_SUM_ROWS, 16), SMALL_SUM_ROWS)
                store_small(sl, gs_ref[where[1](k, c), sl, :].astype(F32) + pair_s[k, sl, :].astype(F32))
                return carry

            lax.fori_loop(0, SMALL_HALF // SMALL_SUM_ROWS, rows, 0)

        for j, (dx, dy) in enumerate(rel3):
            tx, ty = _flip(x, dx), _flip(y, dy)

            def put_in(sl, val, j=j):
                send_i[j, :, sl] = val.astype(BF16)

            def put_small(sl, val, j=j):
                send_s[j, sl, :] = val.astype(BF16)

            pair_sum(2 * tx + ty, put_in, put_small)
            for a, (_, _, send, land, _) in enumerate(payloads):
                cp = pltpu.make_async_remote_copy(
                    src_ref=send.at[j], dst_ref=land.at[j], send_sem=ssem_g.at[j, a], recv_sem=rsem_g.at[j, a],
                    device_id=(tx, ty, c), device_id_type=MESH)
                cp.start()
                copies.append(cp)

        def own_in(sl, val):
            fi_ref[c, :, sl] = val

        def own_small(sl, val):
            fs_ref[c, sl, :] = val

        pair_sum(chip, own_in, own_small)
        for j in range(3):
            for a, (_, _, send, land, _) in enumerate(payloads):
                pltpu.make_async_remote_copy(
                    src_ref=send.at[j], dst_ref=land.at[j], send_sem=ssem_g.at[j, a], recv_sem=rsem_g.at[j, a],
                    device_id=(x, y, c), device_id_type=MESH).wait_recv()
            for qd in range(HALF_D // LANES):
                sl = slice(LANES * qd, LANES * qd + LANES)
                fi_ref[c, :, sl] += land_i[j, :, sl].astype(F32)

            def add_rows(i, carry, j=j):
                sl = pl.ds(pl.multiple_of(i * SMALL_SUM_ROWS, 16), SMALL_SUM_ROWS)
                fs_ref[c, sl, :] += land_s[j, sl, :].astype(F32)
                return carry

            lax.fori_loop(0, SMALL_HALF // SMALL_SUM_ROWS, add_rows, 0)

        for a, f in enumerate((fi_ref, fs_ref)):
            cp = pltpu.make_async_remote_copy(
                src_ref=f.at[c], dst_ref=f.at[c], send_sem=ssem_x.at[a], recv_sem=rsem_x.at[a],
                device_id=(x, y, 1 - c), device_id_type=MESH)
            cp.start()
            copies.append(cp)
        for a, f in enumerate((fi_ref, fs_ref)):
            pltpu.make_async_remote_copy(
                src_ref=f.at[c], dst_ref=f.at[1 - c], send_sem=ssem_x.at[a], recv_sem=rsem_x.at[a],
                device_id=(x, y, c), device_id_type=MESH).wait_recv()
        for r in range(1, N_DEV):
            dx, dy, dc = (r >> 2) & 1, (r >> 1) & 1, r & 1
            src = 4 * _flip(x, dx) + 2 * _flip(y, dy) + _flip(c, dc)
            pltpu.make_async_remote_copy(
                src_ref=sv_ref, dst_ref=svg_ref.at[src], send_sem=ssem_s.at[r - 1],
                recv_sem=rsem_s.at[r - 1], device_id=(x, y, c), device_id_type=MESH).wait_recv()
        for cp in copies:
            cp.wait_send()

    vmem = pl.BlockSpec(memory_space=pltpu.VMEM)
    return pl.pallas_call(
        body, name="grad_reduce",
        out_shape=(_sds((2, IN_SHARD, HALF_D), F32), _sds((2, SMALL_HALF, LANES), F32),
                   _sds((N_DEV, 8, SV_COLS), F32)),
        in_specs=[vmem, vmem, vmem], out_specs=(vmem, vmem, vmem),
        scratch_shapes=[
            pltpu.VMEM((N_CHIPS, IN_SHARD, HALF_D), BF16), pltpu.VMEM((N_CHIPS, SMALL_HALF, LANES), BF16),
            pltpu.VMEM((3, IN_SHARD, HALF_D), BF16), pltpu.VMEM((3, SMALL_HALF, LANES), BF16),
            pltpu.VMEM((3, IN_SHARD, HALF_D), BF16), pltpu.VMEM((3, SMALL_HALF, LANES), BF16),
            pltpu.SemaphoreType.DMA((N_CHIPS, 2)), pltpu.SemaphoreType.DMA((N_CHIPS, 2)),
            pltpu.SemaphoreType.DMA((3, 2)), pltpu.SemaphoreType.DMA((3, 2)),
            pltpu.SemaphoreType.DMA((N_DEV - 1,)), pltpu.SemaphoreType.DMA((N_DEV - 1,)),
            pltpu.SemaphoreType.DMA((2,)), pltpu.SemaphoreType.DMA((2,)),
        ],
        compiler_params=_params(),
    )(g_in, g_small, sv)


def _dwin_call(h, dproj):
    s_len = h.shape[0]
    tm = min(4 * ROW_TILE, s_len)
    nrow = s_len // tm
    nc = 4
    chunk = W_INT // nc

    def body(h_ref, dp_ref, dw_ref, acc):
        i = pl.program_id(1)

        @pl.when(i == 0)
        def _():
            acc[...] = jnp.zeros_like(acc)

        acc[...] += _dot_tn(dp_ref[...], h_ref[...])

        @pl.when(i == nrow - 1)
        def _():
            dw_ref[0] = acc[:, :HALF_D].astype(BF16)
            dw_ref[1] = acc[:, HALF_D:].astype(BF16)

    return pl.pallas_call(
        body, name="dwin", grid=(nc, nrow),
        out_shape=_sds((2, W_INT, HALF_D), BF16),
        in_specs=[pl.BlockSpec((tm, D_MODEL), lambda c, i: (i, 0)),
                  pl.BlockSpec((tm, chunk), lambda c, i: (i, c))],
        out_specs=pl.BlockSpec((2, chunk, HALF_D), lambda c, i: (0, c, 0)),
        scratch_shapes=[pltpu.VMEM((chunk, D_MODEL), F32)],
        compiler_params=_params(("parallel", "arbitrary")),
    )(h, dproj)


def _internal_weights(w_in_t, w_uq, w_ukv):
    krot_t = w_in_t[:, 2688:2720]
    krot_sw = krot_t.reshape(2, 2, 16, HALF_D)[:, ::-1].reshape(2, 32, HALF_D)
    w_int_t = jnp.concatenate([
        w_in_t[:, 0:2688], w_in_t[:, 2720:5280], jnp.tile(krot_t, (1, 4, 1)), jnp.tile(krot_sw, (1, 4, 1)),
        jnp.zeros((2, W_INT - O_END, HALF_D), w_in_t.dtype)], axis=1)
    uq = w_uq.reshape(Q_RANK, N_HEADS, 96)
    wp = uq[:, :, 64:].reshape(Q_RANK, 256)
    w_q = jnp.concatenate([uq[:, :, :64].reshape(Q_RANK, 512), wp, _swap_halves(wp, 32)], axis=1)
    ukv = w_ukv.reshape(KV_RANK, N_HEADS, 128)
    w_kv = jnp.concatenate([ukv[:, :, :64].reshape(KV_RANK, 512), ukv[:, :, 64:].reshape(KV_RANK, 512)], axis=1)
    return w_int_t, w_q, w_kv


def _true_weight_grads(dwi_t, dwq, dwkv):
    dkr = dwi_t[:, O_KR:O_KR + 128].astype(F32).reshape(2, 4, 32, HALF_D).sum(axis=1)
    dkr_sw = dwi_t[:, O_KR + 128:O_END].astype(F32).reshape(2, 4, 32, HALF_D).sum(axis=1)
    dkr_sw = dkr_sw.reshape(2, 2, 16, HALF_D)[:, ::-1].reshape(2, 32, HALF_D)
    dkrot_t = (dkr + dkr_sw).astype(dwi_t.dtype)
    g_in_t = jnp.concatenate([dwi_t[:, 0:O_MZ], dkrot_t, dwi_t[:, O_MZ:O_KR]], axis=1)
    dwp = dwq[:, 512:768] + _swap_halves(dwq[:, 768:1024], 32)
    g_uq = jnp.concatenate([dwq[:, :512].reshape(Q_RANK, N_HEADS, 64), dwp.reshape(Q_RANK, N_HEADS, 32)],
                           axis=2).reshape(Q_RANK, 768)
    g_ukv = jnp.concatenate([dwkv[:, :512].reshape(KV_RANK, N_HEADS, 64), dwkv[:, 512:].reshape(KV_RANK, N_HEADS, 64)],
                            axis=2).reshape(KV_RANK, 1024)
    return g_in_t, g_uq, g_ukv


def _swap_halves(w, group):
    r, n = w.shape
    return w.reshape(r, n // group, 2, group // 2)[:, :, ::-1, :].reshape(r, n)


def _pack_shards(parts):
    return jnp.concatenate([p.reshape(-1, LANES) for p in parts], axis=0)


def _unpack_small(gw):
    offs = [0]
    for r in SMALL_ROWS:
        offs.append(offs[-1] + r)

    def cols(i, rows, shard_cols):
        blk = gw[:, offs[i]:offs[i + 1]].reshape(N_CHIPS, rows, shard_cols)
        return blk.transpose(1, 0, 2).reshape(rows, N_CHIPS * shard_cols)

    return (cols(0, Q_RANK, 192), cols(1, KV_RANK, 256), cols(2, 512, 256), cols(3, 512, 256),
            gw[:, offs[4]:offs[5]].reshape(D_MODEL, D_MODEL))


def _chip_major(g, shard_cols):
    r = g.shape[0]
    return g.reshape(r, N_CHIPS, shard_cols).transpose(1, 0, 2).reshape(N_CHIPS, -1, LANES)


def kernel(x, c, positions, w_ada, b_ada, norm_gain, w_in, q_norm_gain, w_uq, kv_norm_gain, w_ukv, w_branch_a, w_branch_b, w_out, final_norm_gain, loss_target, m_w_ada, m_b_ada, m_norm_gain, m_w_in, m_q_norm_gain, m_w_uq, m_kv_norm_gain, m_w_ukv, m_w_branch_a, m_w_branch_b, m_w_out, m_final_norm_gain, v_w_ada, v_b_ada, v_norm_gain, v_w_in, v_q_norm_gain, v_w_uq, v_kv_norm_gain, v_w_ukv, v_w_branch_a, v_w_branch_b, v_w_out, v_final_norm_gain):
    ix, iy, ic = lax.axis_index("x"), lax.axis_index("y"), lax.axis_index("c")
    me = 4 * ix + 2 * iy + ic
    chip = 2 * ix + iy
    xs = x[0]
    tgt = loss_target[0]
    s_len = xs.shape[0]

    w_in_t = jnp.swapaxes(w_in[0], 0, 1)
    w_in_tb = w_in_t.astype(BF16)
    pack_in = jnp.stack([w_in_tb[:, :HALF_D], w_in_tb[:, HALF_D:]], axis=0)
    small_shards = (w_uq[0], w_ukv[0], w_branch_a[0], w_branch_b[0], w_out[0])
    pack_small = _pack_shards([s.astype(BF16) for s in small_shards]).reshape(2, SMALL_HALF, LANES)
    mg, call, gw_in, gw_small = _gather_call(c, w_ada[0], pack_in, pack_small)
    mod = mg.transpose(1, 0, 2).reshape(N_DEV, 3 * D_MODEL) + b_ada
    mod_me = lax.dynamic_slice_in_dim(mod, me, 1, axis=0)
    shift, scale, gate = mod_me[:, :D_MODEL], mod_me[:, D_MODEL:2 * D_MODEL], mod_me[:, 2 * D_MODEL:]

    f_in_t = gw_in.reshape(2, IN_WIDTH, HALF_D)
    f_uq, f_ukv, f_a, f_b, f_out = _unpack_small(gw_small.reshape(N_CHIPS, SMALL_TOTAL, LANES))
    w_int_t, w_q, w_kv = _internal_weights(f_in_t, f_uq, f_ukv)

    inv_freq = ROPE_BASE ** (-jnp.arange(0, ROPE_DIM, 2, dtype=F32) / ROPE_DIM)
    ang = positions[0].astype(F32)[:, None] * inv_freq
    cs, sn = jnp.cos(ang), jnp.sin(ang)
    cos256 = jnp.tile(jnp.concatenate([cs, cs], axis=1), (1, 8))
    sin256 = jnp.tile(jnp.concatenate([-sn, sn], axis=1), (1, 8))

    (h, sq, sk, sv, sz, cq, ckv, mz, ga, gb, kpt, qn, qp, kn, vv) = _inproj_call(
        xs, shift, scale, norm_gain, w_int_t, w_q, w_kv, q_norm_gain, kv_norm_gain, cos256, sin256)
    oa, lt, first = _sb_fwd_call(sq, sk, sv)
    ob, lse = _mla_fwd_call(qn, qp, kn, kpt, vv)

    gf = final_norm_gain.reshape(1, D_MODEL)
    (dx2, doa, dob, dsz, dmz, dga, dgb, dwo, dwa, dwb, dgf, dgate, loss_p) = _post_call(
        xs, tgt, oa, ob, sz, mz, ga, gb, gate, gf, f_a, f_b, f_out)

    dsq, dsk_t, dsv_t = _sb_bwd_call(first[:, :, 0, 0].reshape(-1), sq, sk, sv, doa, lt)
    dqn, dqp, dkn_t, dkpt_t, dvv_t = _mla_bwd_call(qn, qp, kn, kpt, vv, ob, dob, lse)

    dproj, dwq, dwkv, dqg, dkvg = _bwdprep_call(
        dsq, dsk_t, dsv_t, dsz, dqn, dqp, dkn_t, dvv_t, dkpt_t, dmz, dga, dgb, cq, ckv, cos256, sin256,
        q_norm_gain, kv_norm_gain, w_q, w_kv)
    grad_x, dshift, dscale, dg1 = _dh_call(dproj, w_int_t, xs, dx2, scale, norm_gain)
    dwi_t = _dwin_call(h, dproj)
    g_in_t, g_uq, g_ukv = _true_weight_grads(dwi_t, dwq, dwkv)

    g_in_pieces = g_in_t.reshape(N_DEV, IN_SHARD, HALF_D)
    g_small = jnp.concatenate([
        _chip_major(g_uq, 192), _chip_major(g_ukv, 256), _chip_major(dwa, 256), _chip_major(dwb, 256),
        dwo.reshape(N_CHIPS, -1, LANES)], axis=1).astype(BF16).reshape(N_DEV, SMALL_HALF, LANES)
    small = jnp.concatenate([
        dshift, dscale, dgate, dg1, dqg, dkvg, dgf, loss_p,
        jnp.zeros((1, 8 * SV_COLS - 5888), F32)], axis=1).reshape(8, SV_COLS)
    full_in, full_small, svg = _reduce_call(g_in_pieces, g_small, small)
    gs_in_t = jnp.concatenate([full_in[0], full_in[1]], axis=1)
    full = full_small.reshape(SMALL_TOTAL, LANES)
    offs = [0]
    for r in SMALL_ROWS:
        offs.append(offs[-1] + r)
    gs_uq = full[offs[0]:offs[1]].reshape(Q_RANK, 192)
    gs_ukv = full[offs[1]:offs[2]].reshape(KV_RANK, 256)
    gs_a = full[offs[2]:offs[3]].reshape(512, 256)
    gs_b = full[offs[3]:offs[4]].reshape(512, 256)
    gs_out = full[offs[4]:offs[5]].reshape(256, D_MODEL)

    svm = svg.reshape(N_DEV, 8 * SV_COLS)
    dmod_sh = lax.dynamic_slice_in_dim(svm[:, :3 * D_MODEL], chip * 768, 768, axis=1)
    tot, gs_ada = _small_call(svm, call.T, dmod_sh)
    g_bada = tot[:, 0:3072]
    g_g1 = tot[:, 3072:4096]
    g_qg = tot[:, 4096:4480]
    g_kvg = tot[:, 4480:4736]
    g_gf = tot[:, 4736:5760]
    loss = tot[0, 5760]

    names = ["w_ada", "b_ada", "norm_gain", "w_in", "q_norm_gain", "w_uq", "kv_norm_gain", "w_ukv",
             "w_branch_a", "w_branch_b", "w_out", "final_norm_gain"]
    ws = [w_ada[0], b_ada, norm_gain, w_in_t, q_norm_gain, w_uq[0], kv_norm_gain, w_ukv[0],
          w_branch_a[0], w_branch_b[0], w_out[0], final_norm_gain.reshape(1, D_MODEL)]
    gs = [gs_ada, g_bada, g_g1, gs_in_t, g_qg, gs_uq, g_kvg, gs_ukv, gs_a, gs_b, gs_out, g_gf]
    ms = [m_w_ada[0], m_b_ada, m_norm_gain, jnp.swapaxes(m_w_in[0], 0, 1), m_q_norm_gain, m_w_uq[0],
          m_kv_norm_gain, m_w_ukv[0], m_w_branch_a[0], m_w_branch_b[0], m_w_out[0],
          m_final_norm_gain.reshape(1, D_MODEL)]
    vs = [v_w_ada[0], v_b_ada, v_norm_gain, jnp.swapaxes(v_w_in[0], 0, 1), v_q_norm_gain, v_w_uq[0],
          v_kv_norm_gain, v_w_ukv[0], v_w_branch_a[0], v_w_branch_b[0], v_w_out[0],
          v_final_norm_gain.reshape(1, D_MODEL)]
    refs = [w_ada, b_ada, norm_gain, w_in, q_norm_gain, w_uq, kv_norm_gain, w_ukv,
            w_branch_a, w_branch_b, w_out, final_norm_gain]
    grads, deltas, new_ms, new_vs = [], [], [], []
    for n, w_, g_, m_, v_, ref in zip(names, ws, gs, ms, vs, refs):
        outs = (g_,) + _adamw_call(n, w_, g_, m_, v_)
        if n == "w_in":
            outs = tuple(jnp.swapaxes(o_, 0, 1) for o_ in outs)
        for lst, o_ in zip((grads, deltas, new_ms, new_vs), outs):
            lst.append(o_.reshape(ref.shape))

    return (loss, grad_x.reshape(x.shape), *grads, *deltas, *new_ms, *new_vs)
```

```python
import math

import jax
import jax.numpy as jnp
from jax import lax
from jax.experimental import pallas as pl
from jax.experimental.pallas import tpu as pltpu

F32 = jnp.float32
BF16 = jnp.bfloat16

D_MODEL = 1024
SB_WIDTH = 512
MLA_WIDTH = 512
Q_RANK = 384
KV_RANK = 256
ROPE_DIM = 32
N_HEADS = 8
IN_WIDTH = 5280
EPS = 1e-6
ROPE_BASE = 10000.0
MLA_SCALE = 1.0 / math.sqrt(96.0)
SB_SCALE = 0.125
LOG2E = 1.4426950408889634

ADAM_LR = 0.001
ADAM_B1 = 0.9
ADAM_B2 = 0.999
ADAM_EPS = 1e-08
ADAM_WD = 0.01
ADAM_STEP = 10

O_SQ, O_SK, O_SV, O_SZ, O_CQ, O_CKV, O_MZ, O_GA, O_GB, O_KR, O_END = (
    0, 512, 1024, 1536, 2048, 2432, 2688, 3200, 4224, 5248, 5504)
W_INT = 5632

N_CHIPS = 4
N_DEV = 8
LANES = 128
SV_COLS = 768

ROW_TILE = 256
ATT_TILE = 256
ATT_Q_TILES = 2
FWD_Q_TILES = 4
SB_Q_TILES = 1
MLA_KEY_TILE = 512
VMEM_LIMIT = 56 * 1024 * 1024

MESH = pl.DeviceIdType.MESH


def _dot(a, b):
    return lax.dot_general(a, b, (((1,), (0,)), ((), ())), preferred_element_type=F32)


def _dot_nt(a, b):
    return lax.dot_general(a, b, (((1,), (1,)), ((), ())), preferred_element_type=F32)


def _dot_tn(a, b):
    return lax.dot_general(a, b, (((0,), (0,)), ((), ())), preferred_element_type=F32)


def _sigmoid(z):
    return 1.0 / (1.0 + jnp.exp2(z * (-LOG2E)))


def _params(sem=None):
    if sem is None:
        return pltpu.CompilerParams(vmem_limit_bytes=VMEM_LIMIT)
    return pltpu.CompilerParams(dimension_semantics=sem, vmem_limit_bytes=VMEM_LIMIT)


def _rows(tm, n):
    return pl.BlockSpec((tm, n), lambda i: (i, 0))


def _cols(n, tm):
    return pl.BlockSpec((n, tm), lambda i: (0, i))


def _whole(shape):
    nd = len(shape)
    return pl.BlockSpec(shape, lambda i: (0,) * nd)


def _sds(shape, dtype):
    return jax.ShapeDtypeStruct(shape, dtype)


def _flip(v, d):
    return 1 - v if d else v


def _inproj_call(x, shift, scale, g1, w_int, w_q, w_kv, qg, kvg, cos256, sin256):
    s_len = x.shape[0]
    tm = min(ROW_TILE, s_len)

    def body(x_ref, sh_ref, sc_ref, g1_ref, w_ref, wq_ref, wkv_ref, qg_ref, kvg_ref, cos_ref, sin_ref,
             h_ref, sq_ref, sk_ref, sv_ref, sz_ref, cq_ref, ckv_ref, mz_ref, ga_ref, gb_ref, kpt_ref,
             qn_ref, qp_ref, kn_ref, vv_ref):
        xt = x_ref[...]
        r = lax.rsqrt(jnp.mean(xt * xt, axis=-1, keepdims=True) + EPS)
        h = (xt * r * g1_ref[...]) * (1.0 + sc_ref[...]) + sh_ref[...]
        hb = h.astype(BF16)
        h_ref[...] = hb

        def seg(a, b):
            return _dot_nt(hb[:, :HALF_D], w_ref[0, a:b, :]) + _dot_nt(hb[:, HALF_D:], w_ref[1, a:b, :])

        sq_ref[...] = (seg(O_SQ, O_SK) * SB_SCALE).astype(BF16)
        sk_ref[...] = seg(O_SK, O_SV).astype(BF16)
        sv_ref[...] = seg(O_SV, O_SZ).astype(BF16)
        sz_ref[...] = seg(O_SZ, O_CQ)
        mz_ref[...] = seg(O_MZ, O_GA)
        ga_ref[...] = seg(O_GA, O_GB)
        gb_ref[...] = seg(O_GB, O_KR)
        cos = cos_ref[...]
        sin = sin_ref[...]
        kr = seg(O_KR, O_END)
        kpt_ref[...] = (kr[:, :128] * cos[:, :128] + kr[:, 128:] * sin[:, :128]).astype(BF16)

        cq = seg(O_CQ, O_CKV)
        cq_ref[...] = cq
        rq = lax.rsqrt(jnp.mean(cq * cq, axis=-1, keepdims=True) + EPS)
        cqn = (cq * rq * qg_ref[...]).astype(BF16)
        qa = _dot(cqn, wq_ref[...])
        qn_ref[...] = qa[:, :512].astype(BF16)
        qp_ref[...] = (qa[:, 512:768] * cos + qa[:, 768:] * sin).astype(BF16)

        ckv = seg(O_CKV, O_MZ)
        ckv_ref[...] = ckv
        rk = lax.rsqrt(jnp.mean(ckv * ckv, axis=-1, keepdims=True) + EPS)
        ckvn = (ckv * rk * kvg_ref[...]).astype(BF16)
        kva = _dot(ckvn, wkv_ref[...])
        kn_ref[...] = kva[:, :512].astype(BF16)
        vv_ref[...] = kva[:, 512:].astype(BF16)

    outs = [
        (D_MODEL, BF16), (512, BF16), (512, BF16), (512, BF16), (512, F32), (Q_RANK, F32), (KV_RANK, F32),
        (512, F32), (D_MODEL, F32), (D_MODEL, F32), (128, BF16), (512, BF16), (256, BF16), (512, BF16), (512, BF16),
    ]
    return pl.pallas_call(
        body, name="inproj", grid=(s_len // tm,),
        out_shape=tuple(_sds((s_len, n), dt) for n, dt in outs),
        in_specs=[_rows(tm, D_MODEL), _whole((1, D_MODEL)), _whole((1, D_MODEL)), _whole((1, D_MODEL)),
                  _whole((2, W_INT, HALF_D)), _whole((Q_RANK, 1024)), _whole((KV_RANK, 1024)),
                  _whole((1, Q_RANK)), _whole((1, KV_RANK)), _rows(tm, 256), _rows(tm, 256)],
        out_specs=tuple(_rows(tm, n) for n, _ in outs),
        compiler_params=_params(("parallel",)),
    )(x, shift, scale, g1, w_int, w_q, w_kv, qg, kvg, cos256, sin256)


Z_CLAMP = 80.0 * LOG2E
RUN_CUTOFF = 110.0 * LOG2E


def _softplus_clamped(z):
    zc = jnp.minimum(z * LOG2E, Z_CLAMP)
    return zc, jnp.log2(1.0 + jnp.exp2(zc))


def _tri_sum(a, tri):
    return _dot(a.astype(BF16), tri)


def _sb_fwd_call(q, k, v):
    s_len = q.shape[0]
    tk = min(ATT_TILE, s_len)
    tq = min(SB_Q_TILES * ATT_TILE, s_len)
    r = tq // tk
    nq = s_len // tq

    def body(q_ref, k_ref, v_ref, o_ref, lt_ref, first_ref):
        i = pl.program_id(1)
        q2 = q_ref[...]
        lane = lax.broadcasted_iota(jnp.int32, (1, 256), 1)
        krow = lax.broadcasted_iota(jnp.int32, (tk, tk), 0)
        kcol = lax.broadcasted_iota(jnp.int32, (tk, tk), 1)
        row = lax.broadcasted_iota(jnp.int32, (tq, tk), 0)
        col = lax.broadcasted_iota(jnp.int32, (tq, tk), 1)
        later = (krow > kcol).astype(BF16)
        valids = [col + u * tk < row for u in range(r)]
        hms = [(lane // 64) == hh for hh in range(4)]
        qms = [jnp.where(hm, q2, jnp.zeros_like(q2)) for hm in hms]

        def block(j, carry, valid):
            runs, acc = list(carry[:4]), carry[4]
            off = pl.multiple_of(j * tk, tk)
            kb = k_ref[pl.ds(off, tk), :]
            vb = v_ref[pl.ds(off, tk), :]
            ws = []
            for hh in range(4):
                zc, sp = _softplus_clamped(_dot_nt(qms[hh], kb))
                lm = jnp.where(valid, sp, 0.0) if valid is not None else sp
                suf = _tri_sum(lm, later)
                w = jnp.exp2(zc - sp - suf - runs[hh])
                if valid is not None:
                    w = jnp.where(valid, w, 0.0)
                ws.append(w.astype(BF16))
                runs[hh] = runs[hh] + jnp.sum(lm, axis=1, keepdims=True)
            vstack = jnp.concatenate([jnp.where(hm, vb, jnp.zeros_like(vb)) for hm in hms], axis=0)
            acc = acc + _dot(jnp.concatenate(ws, axis=1), vstack)
            return (*runs, acc)

        zero = jnp.zeros((tq, 1), F32)
        carry = (zero, zero, zero, zero, jnp.zeros((tq, 256), F32))
        for u in reversed(range(r)):
            carry = block(i * r + u, carry, valids[u])

        def least_run(runs):
            return jnp.min(jnp.minimum(jnp.minimum(runs[0], runs[1]), jnp.minimum(runs[2], runs[3])))

        n_full = i * r

        def unfinished(state):
            return jnp.logical_and(state[0] < n_full, state[1] <= RUN_CUTOFF)

        def visit(state):
            cr = block(n_full - 1 - state[0], state[2:], None)
            return (state[0] + 1, least_run(cr[:4]), *cr)

        state = lax.while_loop(unfinished, visit, (jnp.int32(0), least_run(carry[:4]), *carry))
        carry = state[2:]
        first_ref[...] = jnp.full(first_ref.shape, n_full - state[0], jnp.int32)
        for hh in range(4):
            lt_ref[0, :, hh:hh + 1] = carry[hh]
        o_ref[...] = carry[4]

    return pl.pallas_call(
        body, name="sb_fwd", grid=(2, nq),
        out_shape=(_sds((s_len, SB_WIDTH), F32), _sds((2, s_len, 4), F32), _sds((2, nq, 8, 128), jnp.int32)),
        in_specs=[pl.BlockSpec((tq, 256), lambda g, i: (i, g)),
                  pl.BlockSpec((s_len, 256), lambda g, i: (0, g)),
                  pl.BlockSpec((s_len, 256), lambda g, i: (0, g))],
        out_specs=(pl.BlockSpec((tq, 256), lambda g, i: (i, g)),
                   pl.BlockSpec((1, tq, 4), lambda g, i: (g, i, 0)),
                   pl.BlockSpec((1, 1, 8, 128), lambda g, i: (g, i, 0, 0))),
        compiler_params=_params(("parallel", "parallel")),
    )(q, k, v)


def _sb_bwd_call(first, q, k, v, do, lt):
    s_len = q.shape[0]
    tk = min(ATT_TILE, s_len)
    tq = min(SB_Q_TILES * ATT_TILE, s_len)
    r = tq // tk
    nq = s_len // tq
    nq_fwd = first.shape[0] // 2
    per_fwd = nq // nq_fwd

    def body(first_ref, q_ref, k_ref, v_ref, do_ref, lt_ref, dq_ref, dk_ref, dv_ref):
        g = pl.program_id(0)
        i = pl.program_id(1)

        @pl.when(i == 0)
        def _():
            dk_ref[...] = jnp.zeros_like(dk_ref)
            dv_ref[...] = jnp.zeros_like(dv_ref)

        q2 = q_ref[...]
        do2 = do_ref[...].astype(BF16)
        lane = lax.broadcasted_iota(jnp.int32, (1, 256), 1)
        krow = lax.broadcasted_iota(jnp.int32, (tk, tk), 0)
        kcol = lax.broadcasted_iota(jnp.int32, (tk, tk), 1)
        row = lax.broadcasted_iota(jnp.int32, (tq, tk), 0)
        col = lax.broadcasted_iota(jnp.int32, (tq, tk), 1)
        earlier = (krow < kcol).astype(BF16)
        later = (krow > kcol).astype(BF16)
        valids = [col + u * tk < row for u in range(r)]
        hms = [(lane // 64) == hh for hh in range(4)]
        qms = [jnp.where(hm, q2, jnp.zeros_like(q2)) for hm in hms]
        doms = [jnp.where(hm, do2, jnp.zeros_like(do2)) for hm in hms]
        ltots = [lt_ref[0, :, hh:hh + 1] for hh in range(4)]
        q2t = jnp.transpose(q2.astype(F32))
        do2t = jnp.transpose(do_ref[...])
        subl = lax.broadcasted_iota(jnp.int32, (256, 1), 0)
        qtstack = jnp.concatenate(
            [jnp.where((subl // 64) == hh, q2t, 0.0).astype(BF16) for hh in range(4)], axis=1)
        dotstack = jnp.concatenate(
            [jnp.where((subl // 64) == hh, do2t, 0.0).astype(BF16) for hh in range(4)], axis=1)

        def block(j, carry, valid):
            lpre, ppre, dq = list(carry[0:4]), list(carry[4:8]), carry[8]
            off = pl.multiple_of(j * tk, tk)
            kb = k_ref[pl.ds(off, tk), :]
            vb = v_ref[pl.ds(off, tk), :]
            dzs, avs = [], []
            for hh in range(4):
                zc, sp = _softplus_clamped(_dot_nt(qms[hh], kb))
                lsig = zc - sp
                lm = jnp.where(valid, sp, 0.0) if valid is not None else sp
                rowsum = jnp.sum(lm, axis=1, keepdims=True)
                between = _tri_sum(lm, later) + ((ltots[hh] - lpre[hh]) - rowsum)
                a = jnp.exp2(lsig - between)
                if valid is not None:
                    a = jnp.where(valid, a, 0.0)
                p = a * _dot_nt(doms[hh], vb)
                pbefore = ppre[hh] + _tri_sum(p, earlier)
                dz = p - jnp.exp2(lsig) * (p + pbefore)
                if valid is not None:
                    dz = jnp.where(valid, dz, 0.0)
                dzs.append(dz.astype(BF16))
                avs.append(a.astype(BF16))
                lpre[hh] = lpre[hh] + rowsum
                ppre[hh] = ppre[hh] + jnp.sum(p, axis=1, keepdims=True)
            kstack = jnp.concatenate([jnp.where(hm, kb, jnp.zeros_like(kb)) for hm in hms], axis=0)
            dq = dq + _dot(jnp.concatenate(dzs, axis=1), kstack)
            dk_ref[:, pl.ds(off, tk)] += _dot(qtstack, jnp.concatenate(dzs, axis=0))
            dv_ref[:, pl.ds(off, tk)] += _dot(dotstack, jnp.concatenate(avs, axis=0))
            return (*lpre, *ppre, dq)

        zero = jnp.zeros((tq, 1), F32)
        start = jnp.minimum(first_ref[g * nq_fwd + i // per_fwd], i * r)
        carry = lax.fori_loop(start, i * r, lambda j, cr: block(j, cr, None),
                              (zero,) * 8 + (jnp.zeros((tq, 256), F32),))
        for u in range(r):
            carry = block(i * r + u, carry, valids[u])
        dq_ref[...] = carry[8].astype(BF16)

    return pl.pallas_call(
        body, name="sb_bwd",
        out_shape=(_sds((s_len, SB_WIDTH), BF16), _sds((SB_WIDTH, s_len), F32), _sds((SB_WIDTH, s_len), F32)),
        grid_spec=pltpu.PrefetchScalarGridSpec(
            num_scalar_prefetch=1, grid=(2, nq),
            in_specs=[pl.BlockSpec((tq, 256), lambda g, i, f: (i, g)),
                      pl.BlockSpec((s_len, 256), lambda g, i, f: (0, g)),
                      pl.BlockSpec((s_len, 256), lambda g, i, f: (0, g)),
                      pl.BlockSpec((tq, 256), lambda g, i, f: (i, g)),
                      pl.BlockSpec((1, tq, 4), lambda g, i, f: (g, i, 0))],
            out_specs=(pl.BlockSpec((tq, 256), lambda g, i, f: (i, g)),
                       pl.BlockSpec((256, s_len), lambda g, i, f: (g, 0)),
                       pl.BlockSpec((256, s_len), lambda g, i, f: (g, 0)))),
        compiler_params=_params(("parallel", "arbitrary")),
    )(first, q, k, v, do, lt)


def _mla_fwd_call(qn, qp, kn, kpt, v):
    s_len = qn.shape[0]
    tk = min(MLA_KEY_TILE, s_len)
    tq = min(FWD_Q_TILES * ATT_TILE, s_len)
    r = tq // tk
    nq = s_len // tq

    def body(qn_ref, qp_ref, kn_ref, kpt_ref, v_ref, o_ref, lse_ref):
        i = pl.program_id(1)
        qn2 = qn_ref[...]
        qp2 = qp_ref[...]
        lane256 = lax.broadcasted_iota(jnp.int32, (1, 256), 1)
        lane128 = lax.broadcasted_iota(jnp.int32, (1, 128), 1)
        krow = lax.broadcasted_iota(jnp.int32, (tk, tk), 0)
        kcol = lax.broadcasted_iota(jnp.int32, (tk, tk), 1)
        row = lax.broadcasted_iota(jnp.int32, (tq, tk), 0)
        col = lax.broadcasted_iota(jnp.int32, (tq, tk), 1)
        valids = [col + u * tk <= row for u in range(r)]
        m64s = [(lane256 // 64) == hh for hh in range(4)]
        half = [(lane128 // 64) == u for u in range(2)]
        m32s = [(lane128 // 32) == hh for hh in range(4)]
        qcs = []
        for hh in range(4):
            qpair = qn2[:, 128 * (hh // 2):128 * (hh // 2) + 128]
            qcs.append(jnp.concatenate([jnp.where(half[hh % 2], qpair, jnp.zeros_like(qpair)),
                                        jnp.where(m32s[hh], qp2, jnp.zeros_like(qp2))], axis=1))

        def by_head(vals):
            return jnp.where(m64s[0], vals[0], jnp.where(m64s[1], vals[1], jnp.where(m64s[2], vals[2], vals[3])))

        def block(j, carry, valid):
            ms, ls, acc = list(carry[0:4]), list(carry[4:8]), carry[8]
            off = pl.multiple_of(j * tk, tk)
            knb = kn_ref[pl.ds(off, tk), :]
            kpb = kpt_ref[pl.ds(off, tk), :]
            vb = v_ref[pl.ds(off, tk), :]
            kcs = [jnp.concatenate([knb[:, 128 * pp:128 * pp + 128], kpb], axis=1) for pp in range(2)]
            ps, alphas = [], []
            for hh in range(4):
                s = _dot_nt(qcs[hh], kcs[hh // 2]) * (MLA_SCALE * LOG2E)
                if valid is not None:
                    s = jnp.where(valid, s, -1e30)
                mn = jnp.maximum(ms[hh], jnp.max(s, axis=1, keepdims=True))
                p = jnp.exp2(s - mn)
                alpha = jnp.exp2(ms[hh] - mn)
                ls[hh] = alpha * ls[hh] + jnp.sum(p, axis=1, keepdims=True)
                ms[hh] = mn
                ps.append(p.astype(BF16))
                alphas.append(alpha)
            pvs = []
            for pp in range(2):
                vpair = vb[:, 128 * pp:128 * pp + 128]
                vstack = jnp.concatenate([jnp.where(hf, vpair, jnp.zeros_like(vpair)) for hf in half], axis=0)
                pvs.append(_dot(jnp.concatenate(ps[2 * pp:2 * pp + 2], axis=1), vstack))
            acc = by_head(alphas) * acc + jnp.concatenate(pvs, axis=1)
            return (*ms, *ls, acc)

        neg = jnp.full((tq, 1), -1e30, F32)
        zero = jnp.zeros((tq, 1), F32)
        carry = lax.fori_loop(0, i * r, lambda j, cr: block(j, cr, None),
                              (neg,) * 4 + (zero,) * 4 + (jnp.zeros((tq, 256), F32),))
        for u in range(r):
            carry = block(i * r + u, carry, valids[u])
        o_ref[...] = carry[8] / by_head(list(carry[4:8]))
        for hh in range(4):
            lse_ref[0, :, hh:hh + 1] = (carry[hh] + jnp.log2(carry[4 + hh])) * (1.0 / LOG2E)

    return pl.pallas_call(
        body, name="mla_fwd", grid=(2, nq),
        out_shape=(_sds((s_len, MLA_WIDTH), F32), _sds((2, s_len, 4), F32)),
        in_specs=[pl.BlockSpec((tq, 256), lambda g, i: (i, g)),
                  pl.BlockSpec((tq, 128), lambda g, i: (i, g)),
                  pl.BlockSpec((s_len, 256), lambda g, i: (0, g)),
                  pl.BlockSpec((s_len, 128), lambda g, i: (0, 0)),
                  pl.BlockSpec((s_len, 256), lambda g, i: (0, g))],
        out_specs=(pl.BlockSpec((tq, 256), lambda g, i: (i, g)),
                   pl.BlockSpec((1, tq, 4), lambda g, i: (g, i, 0))),
        compiler_params=_params(("parallel", "parallel")),
    )(qn, qp, kn, kpt, v)


def _mla_bwd_call(qn, qp, kn, kpt, v, o, do, lse):
    s_len = qn.shape[0]
    tk = min(MLA_KEY_TILE, s_len)
    tq = min(ATT_Q_TILES * ATT_TILE, s_len)
    r = tq // tk
    nq = s_len // tq

    def body(qn_ref, qp_ref, kn_ref, kpt_ref, v_ref, o_ref, do_ref, lse_ref,
             dqn_ref, dqp_ref, dkn_ref, dkpt_ref, dv_ref):
        g = pl.program_id(0)
        i = pl.program_id(1)

        @pl.when(i == 0)
        def _():
            dkn_ref[...] = jnp.zeros_like(dkn_ref)
            dv_ref[...] = jnp.zeros_like(dv_ref)

        @pl.when((i == 0) & (g == 0))
        def _():
            dkpt_ref[...] = jnp.zeros_like(dkpt_ref)

        qn2 = qn_ref[...]
        qp2 = qp_ref[...]
        dof = do_ref[...]
        dob = dof.astype(BF16)
        prod = dof * o_ref[...]
        lane256 = lax.broadcasted_iota(jnp.int32, (1, 256), 1)
        lane128 = lax.broadcasted_iota(jnp.int32, (1, 128), 1)
        krow = lax.broadcasted_iota(jnp.int32, (tk, tk), 0)
        kcol = lax.broadcasted_iota(jnp.int32, (tk, tk), 1)
        row = lax.broadcasted_iota(jnp.int32, (tq, tk), 0)
        col = lax.broadcasted_iota(jnp.int32, (tq, tk), 1)
        valids = [col + u * tk <= row for u in range(r)]
        m64s = [(lane256 // 64) == hh for hh in range(4)]
        half = [(lane128 // 64) == u for u in range(2)]
        m32s = [(lane128 // 32) == hh for hh in range(4)]
        qcs, doms = [], []
        for hh in range(4):
            sl = slice(128 * (hh // 2), 128 * (hh // 2) + 128)
            qpair = qn2[:, sl]
            dpair = dob[:, sl]
            qcs.append(jnp.concatenate([jnp.where(half[hh % 2], qpair, jnp.zeros_like(qpair)),
                                        jnp.where(m32s[hh], qp2, jnp.zeros_like(qp2))], axis=1))
            doms.append(jnp.where(half[hh % 2], dpair, jnp.zeros_like(dpair)))
        dsums = [jnp.sum(jnp.where(m64, prod, 0.0), axis=1, keepdims=True) * MLA_SCALE for m64 in m64s]
        lses = [lse_ref[0, :, hh:hh + 1] * LOG2E for hh in range(4)]
        qn2t = jnp.transpose(qn2.astype(F32))
        qp2t = jnp.transpose(qp2.astype(F32))
        do2t = jnp.transpose(dof)
        sub128 = lax.broadcasted_iota(jnp.int32, (128, 1), 0)
        qtstacks, dotstacks = [], []
        for pp in range(2):
            qts, dts = [], []
            for u in range(2):
                hh = 2 * pp + u
                qts.append(jnp.concatenate(
                    [jnp.where((sub128 // 64) == u, qn2t[128 * pp:128 * pp + 128, :], 0.0),
                     jnp.where((sub128 // 32) == hh, qp2t, 0.0)], axis=0).astype(BF16))
                dts.append(jnp.where((sub128 // 64) == u, do2t[128 * pp:128 * pp + 128, :], 0.0).astype(BF16))
            qtstacks.append(jnp.concatenate(qts, axis=1))
            dotstacks.append(jnp.concatenate(dts, axis=1))

        def block(j, carry, valid):
            dqn, dqp = carry
            off = pl.multiple_of(j * tk, tk)
            knb = kn_ref[pl.ds(off, tk), :]
            kpb = kpt_ref[pl.ds(off, tk), :]
            vb = v_ref[pl.ds(off, tk), :]
            dqn_parts = []
            dkp = None
            for pp in range(2):
                sl = slice(128 * pp, 128 * pp + 128)
                knp = knb[:, sl]
                vpair = vb[:, sl]
                kc = jnp.concatenate([knp, kpb], axis=1)
                dss, pbs, kcms = [], [], []
                for u in range(2):
                    hh = 2 * pp + u
                    s = _dot_nt(qcs[hh], kc) * (MLA_SCALE * LOG2E)
                    if valid is not None:
                        s = jnp.where(valid, s, -1e30)
                    p = jnp.exp2(s - lses[hh])
                    ds = p * (_dot_nt(doms[hh], vpair) * MLA_SCALE - dsums[hh])
                    dss.append(ds.astype(BF16))
                    pbs.append(p.astype(BF16))
                    kcms.append(jnp.concatenate([jnp.where(half[u], knp, jnp.zeros_like(knp)),
                                                 jnp.where(m32s[hh], kpb, jnp.zeros_like(kpb))], axis=1))
                dqc = _dot(jnp.concatenate(dss, axis=1), jnp.concatenate(kcms, axis=0))
                dqn_parts.append(dqc[:, :128])
                dqp = dqp + dqc[:, 128:]
                dkc = _dot(qtstacks[pp], jnp.concatenate(dss, axis=0))
                dkn_ref[128 * pp:128 * pp + 128, pl.ds(off, tk)] += dkc[:128, :]
                dkp = dkc[128:, :] if dkp is None else dkp + dkc[128:, :]
                dv_ref[128 * pp:128 * pp + 128, pl.ds(off, tk)] += _dot(dotstacks[pp], jnp.concatenate(pbs, axis=0))
            dqn = dqn + jnp.concatenate(dqn_parts, axis=1)
            dkpt_ref[:, pl.ds(off, tk)] += dkp
            return dqn, dqp

        carry = lax.fori_loop(0, i * r, lambda j, cr: block(j, cr, None),
                              (jnp.zeros((tq, 256), F32), jnp.zeros((tq, 128), F32)))
        for u in range(r):
            carry = block(i * r + u, carry, valids[u])
        dqn, dqp = carry
        dqn_ref[...] = dqn.astype(BF16)
        dqp_ref[...] = dqp.astype(BF16)

    return pl.pallas_call(
        body, name="mla_bwd", grid=(2, nq),
        out_shape=(_sds((s_len, 512), BF16), _sds((s_len, 256), BF16), _sds((512, s_len), F32),
                   _sds((128, s_len), F32), _sds((512, s_len), F32)),
        in_specs=[pl.BlockSpec((tq, 256), lambda g, i: (i, g)),
                  pl.BlockSpec((tq, 128), lambda g, i: (i, g)),
                  pl.BlockSpec((s_len, 256), lambda g, i: (0, g)),
                  pl.BlockSpec((s_len, 128), lambda g, i: (0, 0)),
                  pl.BlockSpec((s_len, 256), lambda g, i: (0, g)),
                  pl.BlockSpec((tq, 256), lambda g, i: (i, g)),
                  pl.BlockSpec((tq, 256), lambda g, i: (i, g)),
                  pl.BlockSpec((1, tq, 4), lambda g, i: (g, i, 0))],
        out_specs=(pl.BlockSpec((tq, 256), lambda g, i: (i, g)),
                   pl.BlockSpec((tq, 128), lambda g, i: (i, g)),
                   pl.BlockSpec((256, s_len), lambda g, i: (g, 0)),
                   pl.BlockSpec((128, s_len), lambda g, i: (0, 0)),
                   pl.BlockSpec((256, s_len), lambda g, i: (g, 0))),
        compiler_params=_params(("arbitrary", "arbitrary")),
    )(qn, qp, kn, kpt, v, o, do, lse)


def _post_call(x, tgt, oa, ob, sz, mz, ga, gb, gate, gf, wa, wb, wo):
    s_len = x.shape[0]
    tm = min(ROW_TILE, s_len)

    def body(x_ref, t_ref, oa_ref, ob_ref, sz_ref, mz_ref, ga_ref, gb_ref, gate_ref, gf_ref,
             wa_ref, wb_ref, wo_ref,
             dx2_ref, doa_ref, dob_ref, dsz_ref, dmz_ref, dga_ref, dgb_ref,
             dwo_ref, dwa_ref, dwb_ref, dgf_ref, dgate_ref, loss_ref):
        @pl.when(pl.program_id(0) == 0)
        def _():
            dwo_ref[...] = jnp.zeros_like(dwo_ref)
            dwa_ref[...] = jnp.zeros_like(dwa_ref)
            dwb_ref[...] = jnp.zeros_like(dwb_ref)
            dgf_ref[...] = jnp.zeros_like(dgf_ref)
            dgate_ref[...] = jnp.zeros_like(dgate_ref)
            loss_ref[...] = jnp.zeros_like(loss_ref)

        gate = gate_ref[...]
        gf = gf_ref[...]
        oa = oa_ref[...]
        ob = ob_ref[...]
        sz = sz_ref[...]
        mz = mz_ref[...]
        sa = _sigmoid(sz)
        sb = _sigmoid(mz)
        silu_a = sz * sa
        silu_b = mz * sb
        ua = (oa * silu_a).astype(BF16)
        ub = (ob * silu_b).astype(BF16)
        ya = _dot(ua, wa_ref[...])
        yb = _dot(ub, wb_ref[...])
        sga = _sigmoid(ga_ref[...])
        sgb = _sigmoid(gb_ref[...])
        merged = (sga * ya + sgb * yb).astype(BF16)
        out = _dot(merged, wo_ref[...])
        x2 = x_ref[...] + gate * out
        r2 = lax.rsqrt(jnp.mean(x2 * x2, axis=-1, keepdims=True) + EPS)
        xhat = x2 * r2
        err = xhat * gf - t_ref[...]
        loss_ref[...] += 0.5 * jnp.sum(jnp.sum(err * err, axis=1, keepdims=True), axis=0, keepdims=True) / D_MODEL
        dy = err * (1.0 / D_MODEL)
        dgf_ref[...] += jnp.sum(dy * xhat, axis=0, keepdims=True)
        dxhat = dy * gf
        dx2 = r2 * (dxhat - xhat * jnp.mean(dxhat * xhat, axis=-1, keepdims=True))
        dx2_ref[...] = dx2
        dgate_ref[...] += jnp.sum(dx2 * out, axis=0, keepdims=True)
        dout = (dx2 * gate).astype(BF16)
        dmerged = _dot_nt(dout, wo_ref[...])
        dwo_ref[...] += _dot_tn(merged, dout)
        dya = dmerged * sga
        dyb = dmerged * sgb
        dga_ref[...] = (dya * ya * (1.0 - sga)).astype(BF16)
        dgb_ref[...] = (dyb * yb * (1.0 - sgb)).astype(BF16)
        dyab = dya.astype(BF16)
        dybb = dyb.astype(BF16)
        dua = _dot_nt(dyab, wa_ref[...])
        dub = _dot_nt(dybb, wb_ref[...])
        dwa_ref[...] += _dot_tn(ua, dyab)
        dwb_ref[...] += _dot_tn(ub, dybb)
        doa_ref[...] = dua * silu_a
        dob_ref[...] = dub * silu_b
        dsz_ref[...] = (dua * oa * (sa * (1.0 + sz * (1.0 - sa)))).astype(BF16)
        dmz_ref[...] = (dub * ob * (sb * (1.0 + mz * (1.0 - sb)))).astype(BF16)

    return pl.pallas_call(
        body, name="post", grid=(s_len // tm,),
        out_shape=(_sds((s_len, D_MODEL), F32), _sds((s_len, 512), F32), _sds((s_len, 512), F32),
                   _sds((s_len, 512), BF16), _sds((s_len, 512), BF16),
                   _sds((s_len, D_MODEL), BF16), _sds((s_len, D_MODEL), BF16),
                   _sds((D_MODEL, D_MODEL), F32), _sds((512, D_MODEL), F32), _sds((512, D_MODEL), F32),
                   _sds((1, D_MODEL), F32), _sds((1, D_MODEL), F32), _sds((1, 128), F32)),
        in_specs=[_rows(tm, D_MODEL), _rows(tm, D_MODEL), _rows(tm, 512), _rows(tm, 512), _rows(tm, 512),
                  _rows(tm, 512), _rows(tm, D_MODEL), _rows(tm, D_MODEL), _whole((1, D_MODEL)), _whole((1, D_MODEL)),
                  _whole((512, D_MODEL)), _whole((512, D_MODEL)), _whole((D_MODEL, D_MODEL))],
        out_specs=(_rows(tm, D_MODEL), _rows(tm, 512), _rows(tm, 512), _rows(tm, 512), _rows(tm, 512),
                   _rows(tm, D_MODEL), _rows(tm, D_MODEL),
                   _whole((D_MODEL, D_MODEL)), _whole((512, D_MODEL)), _whole((512, D_MODEL)),
                   _whole((1, D_MODEL)), _whole((1, D_MODEL)), _whole((1, 128))),
        compiler_params=_params(("arbitrary",)),
    )(x, tgt, oa, ob, sz, mz, ga, gb, gate, gf, wa, wb, wo)


def _bwdprep_call(dsq, dsk, dsv, dsz, dqn, dqp, dkn, dvv, dkpt, dmz, dga, dgb, cq, ckv, cos256, sin256,
                  qg, kvg, w_q, w_kv):
    s_len = cq.shape[0]
    tm = min(ROW_TILE, s_len)

    def body(dsq_ref, dsk_ref, dsv_ref, dsz_ref, dqn_ref, dqp_ref, dkn_ref, dvv_ref, dkpt_ref, dmz_ref,
             dga_ref, dgb_ref, cq_ref, ckv_ref, cos_ref, sin_ref, qg_ref, kvg_ref, wq_ref, wkv_ref,
             dp_ref, dwq_ref, dwkv_ref, dqg_ref, dkvg_ref):
        @pl.when(pl.program_id(0) == 0)
        def _():
            dwq_ref[...] = jnp.zeros_like(dwq_ref)
            dwkv_ref[...] = jnp.zeros_like(dwkv_ref)
            dqg_ref[...] = jnp.zeros_like(dqg_ref)
            dkvg_ref[...] = jnp.zeros_like(dkvg_ref)

        cos = cos_ref[...]
        sin = sin_ref[...]
        dp_ref[:, O_SQ:O_SK] = dsq_ref[...] * jnp.asarray(SB_SCALE, BF16)
        dp_ref[:, O_SK:O_SV] = jnp.transpose(dsk_ref[...]).astype(BF16)
        dp_ref[:, O_SV:O_SZ] = jnp.transpose(dsv_ref[...]).astype(BF16)
        dp_ref[:, O_SZ:O_CQ] = dsz_ref[...]
        dp_ref[:, O_MZ:O_GA] = dmz_ref[...]
        dp_ref[:, O_GA:O_GB] = dga_ref[...]
        dp_ref[:, O_GB:O_KR] = dgb_ref[...]
        dkp = jnp.transpose(dkpt_ref[...])
        dp_ref[:, O_KR:O_KR + 128] = (dkp * cos[:, :128]).astype(BF16)
        dp_ref[:, O_KR + 128:O_END] = (dkp * sin[:, :128]).astype(BF16)
        dp_ref[:, O_END:W_INT] = jnp.zeros((tm, W_INT - O_END), BF16)

        cq = cq_ref[...]
        rq = lax.rsqrt(jnp.mean(cq * cq, axis=-1, keepdims=True) + EPS)
        cqh = cq * rq
        qg = qg_ref[...]
        cqn = (cqh * qg).astype(BF16)
        dqp = dqp_ref[...].astype(F32)
        dqa = jnp.concatenate([dqn_ref[...], (dqp * cos).astype(BF16), (dqp * sin).astype(BF16)], axis=1)
        dcqn = _dot_nt(dqa, wq_ref[...])
        dwq_ref[...] += _dot_tn(cqn, dqa)
        dqg_ref[...] += jnp.sum(dcqn * cqh, axis=0, keepdims=True)
        dh = dcqn * qg
        dcq = rq * (dh - cqh * jnp.mean(dh * cqh, axis=-1, keepdims=True))
        dp_ref[:, O_CQ:O_CKV] = dcq.astype(BF16)

        ckv = ckv_ref[...]
        rk = lax.rsqrt(jnp.mean(ckv * ckv, axis=-1, keepdims=True) + EPS)
        ckh = ckv * rk
        kvg = kvg_ref[...]
        ckvn = (ckh * kvg).astype(BF16)
        dkva = jnp.concatenate([jnp.transpose(dkn_ref[...]).astype(BF16),
                                jnp.transpose(dvv_ref[...]).astype(BF16)], axis=1)
        dckvn = _dot_nt(dkva, wkv_ref[...])
        dwkv_ref[...] += _dot_tn(ckvn, dkva)
        dkvg_ref[...] += jnp.sum(dckvn * ckh, axis=0, keepdims=True)
        dh2 = dckvn * kvg
        dckv = rk * (dh2 - ckh * jnp.mean(dh2 * ckh, axis=-1, keepdims=True))
        dp_ref[:, O_CKV:O_MZ] = dckv.astype(BF16)

    return pl.pallas_call(
        body, name="bwdprep", grid=(s_len // tm,),
        out_shape=(_sds((s_len, W_INT), BF16), _sds((Q_RANK, 1024), F32), _sds((KV_RANK, 1024), F32),
                   _sds((1, Q_RANK), F32), _sds((1, KV_RANK), F32)),
        in_specs=[_rows(tm, 512), _cols(512, tm), _cols(512, tm), _rows(tm, 512), _rows(tm, 512), _rows(tm, 256),
                  _cols(512, tm), _cols(512, tm), _cols(128, tm), _rows(tm, 512), _rows(tm, D_MODEL),
                  _rows(tm, D_MODEL), _rows(tm, Q_RANK), _rows(tm, KV_RANK), _rows(tm, 256), _rows(tm, 256),
                  _whole((1, Q_RANK)), _whole((1, KV_RANK)), _whole((Q_RANK, 1024)), _whole((KV_RANK, 1024))],
        out_specs=(_rows(tm, W_INT), _whole((Q_RANK, 1024)), _whole((KV_RANK, 1024)),
                   _whole((1, Q_RANK)), _whole((1, KV_RANK))),
        compiler_params=_params(("arbitrary",)),
    )(dsq, dsk, dsv, dsz, dqn, dqp, dkn, dvv, dkpt, dmz, dga, dgb, cq, ckv, cos256, sin256, qg, kvg, w_q, w_kv)


def _dh_call(dproj, w_int_t, x, dx2, scale, g1):
    s_len = x.shape[0]
    tm = min(2 * ROW_TILE, s_len)

    def body(dp_ref, wt_ref, x_ref, dx2_ref, sc_ref, g1_ref, gx_ref, dsh_ref, dsc_ref, dg1_ref):
        @pl.when(pl.program_id(0) == 0)
        def _():
            dsh_ref[...] = jnp.zeros_like(dsh_ref)
            dsc_ref[...] = jnp.zeros_like(dsc_ref)
            dg1_ref[...] = jnp.zeros_like(dg1_ref)

        dp = dp_ref[...]
        dh = jnp.concatenate([_dot(dp, wt_ref[0]), _dot(dp, wt_ref[1])], axis=1)
        xt = x_ref[...]
        r = lax.rsqrt(jnp.mean(xt * xt, axis=-1, keepdims=True) + EPS)
        xh = xt * r
        g1 = g1_ref[...]
        xg = xh * g1
        dsh_ref[...] += jnp.sum(dh, axis=0, keepdims=True)
        dsc_ref[...] += jnp.sum(dh * xg, axis=0, keepdims=True)
        dxg = dh * (1.0 + sc_ref[...])
        dg1_ref[...] += jnp.sum(dxg * xh, axis=0, keepdims=True)
        dxh = dxg * g1
        gx_ref[...] = dx2_ref[...] + r * (dxh - xh * jnp.mean(dxh * xh, axis=-1, keepdims=True))

    return pl.pallas_call(
        body, name="dh", grid=(s_len // tm,),
        out_shape=(_sds((s_len, D_MODEL), F32), _sds((1, D_MODEL), F32), _sds((1, D_MODEL), F32),
                   _sds((1, D_MODEL), F32)),
        in_specs=[_rows(tm, W_INT), _whole((2, W_INT, HALF_D)), _rows(tm, D_MODEL), _rows(tm, D_MODEL),
                  _whole((1, D_MODEL)), _whole((1, D_MODEL))],
        out_specs=(_rows(tm, D_MODEL), _whole((1, D_MODEL)), _whole((1, D_MODEL)), _whole((1, D_MODEL))),
        compiler_params=_params(("arbitrary",)),
    )(dproj, w_int_t, x, dx2, scale, g1)


def _small_call(svg, ct, dmod_sh):
    def body(sv_ref, ct_ref, dm_ref, tot_ref, gwada_ref):
        acc = sv_ref[0:1, :]
        for d in range(1, N_DEV):
            acc = acc + sv_ref[d:d + 1, :]
        tot_ref[...] = acc
        gwada_ref[...] = lax.dot_general(ct_ref[...], dm_ref[...], (((1,), (0,)), ((), ())),
                                         precision=lax.Precision.HIGHEST, preferred_element_type=F32)

    vmem = pl.BlockSpec(memory_space=pltpu.VMEM)
    return pl.pallas_call(
        body, name="small_grads",
        out_shape=(_sds((1, 8 * SV_COLS), F32), _sds((D_MODEL, 768), F32)),
        in_specs=[vmem, vmem, vmem], out_specs=(vmem, vmem),
        compiler_params=_params(),
    )(svg, ct, dmod_sh)


def _adamw_tile_rows(rows, cols):
    budget = 2 << 20
    if rows * cols * 4 <= budget or rows % 8:
        return rows
    best = 8
    for tr in range(8, rows + 1, 8):
        if rows % tr == 0 and tr * cols * 4 <= budget:
            best = tr
    return best


def _adamw_math(w, gg, m, v):
    m2 = ADAM_B1 * m + (1.0 - ADAM_B1) * gg
    v2 = ADAM_B2 * v + (1.0 - ADAM_B2) * (gg * gg)
    m_hat = m2 / (1.0 - ADAM_B1 ** ADAM_STEP)
    v_hat = v2 / (1.0 - ADAM_B2 ** ADAM_STEP)
    return -ADAM_LR * (m_hat / (jnp.sqrt(v_hat) + ADAM_EPS) + ADAM_WD * w), m2, v2


def _adamw_call(name, w, g, m, v):
    rows, cols = w.shape
    tr = _adamw_tile_rows(rows, cols)
    halves = g.ndim == 3

    def body(w_ref, g_ref, m_ref, v_ref, *out_refs):
        if halves:
            gg = jnp.concatenate([g_ref[0], g_ref[1]], axis=1)
            out_refs[0][...] = gg
        else:
            gg = g_ref[...]
        d_ref, nm_ref, nv_ref = out_refs[-3:]
        d_ref[...], nm_ref[...], nv_ref[...] = _adamw_math(w_ref[...], gg, m_ref[...], v_ref[...])

    spec = pl.BlockSpec((tr, cols), lambda i: (i, 0))
    g_spec = pl.BlockSpec((2, tr, cols // 2), lambda i: (0, i, 0)) if halves else spec
    n_out = 4 if halves else 3
    outs = pl.pallas_call(
        body, name="adamw_" + name, grid=(rows // tr,),
        out_shape=(_sds((rows, cols), F32),) * n_out,
        in_specs=[spec, g_spec, spec, spec], out_specs=(spec,) * n_out,
        compiler_params=_params(("parallel",)),
    )(w, g, m, v)
    return tuple(outs) if halves else (g,) + tuple(outs)


IN_SHARD = IN_WIDTH // N_CHIPS
HALF_D = D_MODEL // 2
SMALL_ROWS = (576, 512, 1024, 1024, 2048)
SMALL_TOTAL = sum(SMALL_ROWS)
SMALL_HALF = SMALL_TOTAL // 2
SMALL_SUM_ROWS = 432


def _gather_call(c_row, w_ada_sh, pack_in, pack_small):
    def body(c_ref, wada_ref, pki_ref, pks_ref, mg_ref, cg_ref, gwi_ref, gws_ref,
             cv, ssem_c, rsem_c, ssem_m, rsem_m, ssem_w, rsem_w, ssem_f, rsem_f, lsem):
        x, y, c = lax.axis_index("x"), lax.axis_index("y"), lax.axis_index("c")
        me = 4 * x + 2 * y + c
        chip = 2 * x + y
        rel3 = [(1, 0), (0, 1), (1, 1)]
        packs = [(pki_ref, gwi_ref), (pks_ref, gws_ref)]

        def slot(a, gw, k, h):
            return gw.at[h, k] if a == 0 else gw.at[k, h]

        sends = []
        for j, (dx, dy) in enumerate(rel3):
            for a, (pk, gw) in enumerate(packs):
                cp = pltpu.make_async_remote_copy(
                    src_ref=pk.at[c], dst_ref=slot(a, gw, chip, c), send_sem=ssem_w.at[j, a],
                    recv_sem=rsem_w.at[j, a], device_id=(_flip(x, dx), _flip(y, dy), c), device_id_type=MESH)
                cp.start()
                sends.append(cp)
        owns = []
        for a, (pk, gw) in enumerate(packs):
            for h in range(2):
                own = pltpu.make_async_copy(pk.at[h], slot(a, gw, chip, h), lsem.at[a, h])
                own.start()
                owns.append(own)

        cv[me] = c_ref[...]
        for r in range(1, N_DEV):
            dx, dy, dc = (r >> 2) & 1, (r >> 1) & 1, r & 1
            cp = pltpu.make_async_remote_copy(
                src_ref=c_ref, dst_ref=cv.at[me], send_sem=ssem_c.at[r - 1], recv_sem=rsem_c.at[r - 1],
                device_id=(_flip(x, dx), _flip(y, dy), _flip(c, dc)), device_id_type=MESH)
            cp.start()
            sends.append(cp)
        for r in range(1, N_DEV):
            dx, dy, dc = (r >> 2) & 1, (r >> 1) & 1, r & 1
            src = 4 * _flip(x, dx) + 2 * _flip(y, dy) + _flip(c, dc)
            pltpu.make_async_remote_copy(
                src_ref=c_ref, dst_ref=cv.at[src], send_sem=ssem_c.at[r - 1], recv_sem=rsem_c.at[r - 1],
                device_id=(x, y, c), device_id_type=MESH).wait_recv()
        rows = lax.broadcasted_iota(jnp.int32, (N_DEV, D_MODEL), 0)
        call = jnp.zeros((N_DEV, D_MODEL), F32)
        for b in range(N_DEV):
            call = jnp.where(rows == b, jnp.broadcast_to(cv[b], (N_DEV, D_MODEL)), call)
        cg_ref[...] = call

        mg_ref[chip] = lax.dot_general(call, wada_ref[...], (((1,), (0,)), ((), ())),
                                       precision=lax.Precision.HIGHEST, preferred_element_type=F32)
        for j, (dx, dy) in enumerate(rel3):
            cp = pltpu.make_async_remote_copy(
                src_ref=mg_ref.at[chip], dst_ref=mg_ref.at[chip], send_sem=ssem_m.at[j], recv_sem=rsem_m.at[j],
                device_id=(_flip(x, dx), _flip(y, dy), c), device_id_type=MESH)
            cp.start()
            sends.append(cp)
        for j, (dx, dy) in enumerate(rel3):
            src_chip = 2 * _flip(x, dx) + _flip(y, dy)
            pltpu.make_async_remote_copy(
                src_ref=mg_ref.at[src_chip], dst_ref=mg_ref.at[src_chip], send_sem=ssem_m.at[j],
                recv_sem=rsem_m.at[j], device_id=(x, y, c), device_id_type=MESH).wait_recv()
        for j, (dx, dy) in enumerate(rel3):
            src_chip = 2 * _flip(x, dx) + _flip(y, dy)
            for a, (pk, gw) in enumerate(packs):
                pltpu.make_async_remote_copy(
                    src_ref=pk.at[c], dst_ref=slot(a, gw, src_chip, c), send_sem=ssem_w.at[j, a],
                    recv_sem=rsem_w.at[j, a], device_id=(x, y, c), device_id_type=MESH).wait_recv()
                cp = pltpu.make_async_remote_copy(
                    src_ref=slot(a, gw, src_chip, c), dst_ref=slot(a, gw, src_chip, c), send_sem=ssem_f.at[j, a],
                    recv_sem=rsem_f.at[j, a], device_id=(x, y, 1 - c), device_id_type=MESH)
                cp.start()
                sends.append(cp)
        for j, (dx, dy) in enumerate(rel3):
            src_chip = 2 * _flip(x, dx) + _flip(y, dy)
            for a, (pk, gw) in enumerate(packs):
                pltpu.make_async_remote_copy(
                    src_ref=pk.at[c], dst_ref=slot(a, gw, src_chip, 1 - c), send_sem=ssem_f.at[j, a],
                    recv_sem=rsem_f.at[j, a], device_id=(x, y, c), device_id_type=MESH).wait_recv()
        for cp in sends:
            cp.wait_send()
        for own in owns:
            own.wait()

    vmem = pl.BlockSpec(memory_space=pltpu.VMEM)
    return pl.pallas_call(
        body, name="gather_fwd",
        out_shape=(_sds((N_CHIPS, N_DEV, 768), F32), _sds((N_DEV, D_MODEL), F32),
                   _sds((2, N_CHIPS, IN_SHARD, HALF_D), BF16), _sds((N_CHIPS, 2, SMALL_HALF, LANES), BF16)),
        in_specs=[vmem, vmem, vmem, vmem], out_specs=(vmem, vmem, vmem, vmem),
        scratch_shapes=[
            pltpu.VMEM((N_DEV, 1, D_MODEL), F32),
            pltpu.SemaphoreType.DMA((N_DEV - 1,)), pltpu.SemaphoreType.DMA((N_DEV - 1,)),
            pltpu.SemaphoreType.DMA((3,)), pltpu.SemaphoreType.DMA((3,)),
            pltpu.SemaphoreType.DMA((3, 2)), pltpu.SemaphoreType.DMA((3, 2)),
            pltpu.SemaphoreType.DMA((3, 2)), pltpu.SemaphoreType.DMA((3, 2)),
            pltpu.SemaphoreType.DMA((2, 2)),
        ],
        compiler_params=_params(),
    )(c_row, w_ada_sh, pack_in, pack_small)


def _reduce_call(g_in, g_small, sv):
    def body(gi_ref, gs_ref, sv_ref, fi_ref, fs_ref, svg_ref, pair_i, pair_s, send_i, send_s, land_i, land_s,
             ssem_p, rsem_p, ssem_g, rsem_g, ssem_s, rsem_s, ssem_x, rsem_x):
        x, y, c = lax.axis_index("x"), lax.axis_index("y"), lax.axis_index("c")
        me = 4 * x + 2 * y + c
        chip = 2 * x + y
        rel3 = [(1, 0), (0, 1), (1, 1)]
        payloads = [(gi_ref, pair_i, send_i, land_i, fi_ref), (gs_ref, pair_s, send_s, land_s, fs_ref)]
        where = [lambda k, h: N_CHIPS * h + k, lambda k, h: 2 * k + h]
        copies = []

        for k in range(N_CHIPS):
            for a, (g, pair, _, _, _) in enumerate(payloads):
                cp = pltpu.make_async_remote_copy(
                    src_ref=g.at[where[a](k, 1 - c)], dst_ref=pair.at[k], send_sem=ssem_p.at[k, a],
                    recv_sem=rsem_p.at[k, a], device_id=(x, y, 1 - c), device_id_type=MESH)
                cp.start()
                copies.append(cp)

        for r in range(1, N_DEV):
            dx, dy, dc = (r >> 2) & 1, (r >> 1) & 1, r & 1
            cp = pltpu.make_async_remote_copy(
                src_ref=sv_ref, dst_ref=svg_ref.at[me], send_sem=ssem_s.at[r - 1], recv_sem=rsem_s.at[r - 1],
                device_id=(_flip(x, dx), _flip(y, dy), _flip(c, dc)), device_id_type=MESH)
            cp.start()
            copies.append(cp)
        svg_ref[me] = sv_ref[...]

        def pair_sum(k, store_in, store_small):
            for a, (g, pair, _, _, _) in enumerate(payloads):
                pltpu.make_async_remote_copy(
                    src_ref=g.at[where[a](k, c)], dst_ref=pair.at[k], send_sem=ssem_p.at[k, a],
                    recv_sem=rsem_p.at[k, a], device_id=(x, y, c), device_id_type=MESH).wait_recv()
            for qd in range(HALF_D // LANES):
                sl = slice(LANES * qd, LANES * qd + LANES)
                store_in(sl, gi_ref[where[0](k, c), :, sl].astype(F32) + pair_i[k, :, sl].astype(F32))

            def rows(i, carry):
                sl = pl.ds(pl.multiple_of(i * SMALL_SUM_ROWS, 16), SMALL_SUM_ROWS)
                store_small(sl, gs_ref[where[1](k, c), sl, :].astype(F32) + pair_s[k, sl, :].astype(F32))
                return carry

            lax.fori_loop(0, SMALL_HALF // SMALL_SUM_ROWS, rows, 0)

        for j, (dx, dy) in enumerate(rel3):
            tx, ty = _flip(x, dx), _flip(y, dy)

            def put_in(sl, val, j=j):
                send_i[j, :, sl] = val.astype(BF16)

            def put_small(sl, val, j=j):
                send_s[j, sl, :] = val.astype(BF16)

            pair_sum(2 * tx + ty, put_in, put_small)
            for a, (_, _, send, land, _) in enumerate(payloads):
                cp = pltpu.make_async_remote_copy(
                    src_ref=send.at[j], dst_ref=land.at[j], send_sem=ssem_g.at[j, a], recv_sem=rsem_g.at[j, a],
                    device_id=(tx, ty, c), device_id_type=MESH)
                cp.start()
                copies.append(cp)

        def own_in(sl, val):
            fi_ref[c, :, sl] = val

        def own_small(sl, val):
            fs_ref[c, sl, :] = val

        pair_sum(chip, own_in, own_small)
        for j in range(3):
            for a, (_, _, send, land, _) in enumerate(payloads):
                pltpu.make_async_remote_copy(
                    src_ref=send.at[j], dst_ref=land.at[j], send_sem=ssem_g.at[j, a], recv_sem=rsem_g.at[j, a],
                    device_id=(x, y, c), device_id_type=MESH).wait_recv()
            for qd in range(HALF_D // LANES):
                sl = slice(LANES * qd, LANES * qd + LANES)
                fi_ref[c, :, sl] += land_i[j, :, sl].astype(F32)

            def add_rows(i, carry, j=j):
                sl = pl.ds(pl.multiple_of(i * SMALL_SUM_ROWS, 16), SMALL_SUM_ROWS)
                fs_ref[c, sl, :] += land_s[j, sl, :].astype(F32)
                return carry

            lax.fori_loop(0, SMALL_HALF // SMALL_SUM_ROWS, add_rows, 0)

        for a, f in enumerate((fi_ref, fs_ref)):
            cp = pltpu.make_async_remote_copy(
                src_ref=f.at[c], dst_ref=f.at[c], send_sem=ssem_x.at[a], recv_sem=rsem_x.at[a],
                device_id=(x, y, 1 - c), device_id_type=MESH)
            cp.start()
            copies.append(cp)
        for a, f in enumerate((fi_ref, fs_ref)):
            pltpu.make_async_remote_copy(
                src_ref=f.at[c], dst_ref=f.at[1 - c], send_sem=ssem_x.at[a], recv_sem=rsem_x.at[a],
                device_id=(x, y, c), device_id_type=MESH).wait_recv()
        for r in range(1, N_DEV):
            dx, dy, dc = (r >> 2) & 1, (r >> 1) & 1, r & 1
            src = 4 * _flip(x, dx) + 2 * _flip(y, dy) + _flip(c, dc)
            pltpu.make_async_remote_copy(
                src_ref=sv_ref, dst_ref=svg_ref.at[src], send_sem=ssem_s.at[r - 1],
                recv_sem=rsem_s.at[r - 1], device_id=(x, y, c), device_id_type=MESH).wait_recv()
        for cp in copies:
            cp.wait_send()

    vmem = pl.BlockSpec(memory_space=pltpu.VMEM)
    return pl.pallas_call(
        body, name="grad_reduce",
        out_shape=(_sds((2, IN_SHARD, HALF_D), F32), _sds((2, SMALL_HALF, LANES), F32),
                   _sds((N_DEV, 8, SV_COLS), F32)),
        in_specs=[vmem, vmem, vmem], out_specs=(vmem, vmem, vmem),
        scratch_shapes=[
            pltpu.VMEM((N_CHIPS, IN_SHARD, HALF_D), BF16), pltpu.VMEM((N_CHIPS, SMALL_HALF, LANES), BF16),
            pltpu.VMEM((3, IN_SHARD, HALF_D), BF16), pltpu.VMEM((3, SMALL_HALF, LANES), BF16),
            pltpu.VMEM((3, IN_SHARD, HALF_D), BF16), pltpu.VMEM((3, SMALL_HALF, LANES), BF16),
            pltpu.SemaphoreType.DMA((N_CHIPS, 2)), pltpu.SemaphoreType.DMA((N_CHIPS, 2)),
            pltpu.SemaphoreType.DMA((3, 2)), pltpu.SemaphoreType.DMA((3, 2)),
            pltpu.SemaphoreType.DMA((N_DEV - 1,)), pltpu.SemaphoreType.DMA((N_DEV - 1,)),
            pltpu.SemaphoreType.DMA((2,)), pltpu.SemaphoreType.DMA((2,)),
        ],
        compiler_params=_params(),
    )(g_in, g_small, sv)


def _dwin_call(h, dproj):
    s_len = h.shape[0]
    tm = min(4 * ROW_TILE, s_len)
    nrow = s_len // tm
    nc = 4
    chunk = W_INT // nc

    def body(h_ref, dp_ref, dw_ref, acc):
        i = pl.program_id(1)

        @pl.when(i == 0)
        def _():
            acc[...] = jnp.zeros_like(acc)

        acc[...] += _dot_tn(dp_ref[...], h_ref[...])

        @pl.when(i == nrow - 1)
        def _():
            dw_ref[0] = acc[:, :HALF_D].astype(BF16)
            dw_ref[1] = acc[:, HALF_D:].astype(BF16)

    return pl.pallas_call(
        body, name="dwin", grid=(nc, nrow),
        out_shape=_sds((2, W_INT, HALF_D), BF16),
        in_specs=[pl.BlockSpec((tm, D_MODEL), lambda c, i: (i, 0)),
                  pl.BlockSpec((tm, chunk), lambda c, i: (i, c))],
        out_specs=pl.BlockSpec((2, chunk, HALF_D), lambda c, i: (0, c, 0)),
        scratch_shapes=[pltpu.VMEM((chunk, D_MODEL), F32)],
        compiler_params=_params(("parallel", "arbitrary")),
    )(h, dproj)


def _internal_weights(w_in_t, w_uq, w_ukv):
    krot_t = w_in_t[:, 2688:2720]
    krot_sw = krot_t.reshape(2, 2, 16, HALF_D)[:, ::-1].reshape(2, 32, HALF_D)
    w_int_t = jnp.concatenate([
        w_in_t[:, 0:2688], w_in_t[:, 2720:5280], jnp.tile(krot_t, (1, 4, 1)), jnp.tile(krot_sw, (1, 4, 1)),
        jnp.zeros((2, W_INT - O_END, HALF_D), w_in_t.dtype)], axis=1)
    uq = w_uq.reshape(Q_RANK, N_HEADS, 96)
    wp = uq[:, :, 64:].reshape(Q_RANK, 256)
    w_q = jnp.concatenate([uq[:, :, :64].reshape(Q_RANK, 512), wp, _swap_halves(wp, 32)], axis=1)
    ukv = w_ukv.reshape(KV_RANK, N_HEADS, 128)
    w_kv = jnp.concatenate([ukv[:, :, :64].reshape(KV_RANK, 512), ukv[:, :, 64:].reshape(KV_RANK, 512)], axis=1)
    return w_int_t, w_q, w_kv


def _true_weight_grads(dwi_t, dwq, dwkv):
    dkr = dwi_t[:, O_KR:O_KR + 128].astype(F32).reshape(2, 4, 32, HALF_D).sum(axis=1)
    dkr_sw = dwi_t[:, O_KR + 128:O_END].astype(F32).reshape(2, 4, 32, HALF_D).sum(axis=1)
    dkr_sw = dkr_sw.reshape(2, 2, 16, HALF_D)[:, ::-1].reshape(2, 32, HALF_D)
    dkrot_t = (dkr + dkr_sw).astype(dwi_t.dtype)
    g_in_t = jnp.concatenate([dwi_t[:, 0:O_MZ], dkrot_t, dwi_t[:, O_MZ:O_KR]], axis=1)
    dwp = dwq[:, 512:768] + _swap_halves(dwq[:, 768:1024], 32)
    g_uq = jnp.concatenate([dwq[:, :512].reshape(Q_RANK, N_HEADS, 64), dwp.reshape(Q_RANK, N_HEADS, 32)],
                           axis=2).reshape(Q_RANK, 768)
    g_ukv = jnp.concatenate([dwkv[:, :512].reshape(KV_RANK, N_HEADS, 64), dwkv[:, 512:].reshape(KV_RANK, N_HEADS, 64)],
                            axis=2).reshape(KV_RANK, 1024)
    return g_in_t, g_uq, g_ukv


def _swap_halves(w, group):
    r, n = w.shape
    return w.reshape(r, n // group, 2, group // 2)[:, :, ::-1, :].reshape(r, n)


def _pack_shards(parts):
    return jnp.concatenate([p.reshape(-1, LANES) for p in parts], axis=0)


def _unpack_small(gw):
    offs = [0]
    for r in SMALL_ROWS:
        offs.append(offs[-1] + r)

    def cols(i, rows, shard_cols):
        blk = gw[:, offs[i]:offs[i + 1]].reshape(N_CHIPS, rows, shard_cols)
        return blk.transpose(1, 0, 2).reshape(rows, N_CHIPS * shard_cols)

    return (cols(0, Q_RANK, 192), cols(1, KV_RANK, 256), cols(2, 512, 256), cols(3, 512, 256),
            gw[:, offs[4]:offs[5]].reshape(D_MODEL, D_MODEL))


def _chip_major(g, shard_cols):
    r = g.shape[0]
    return g.reshape(r, N_CHIPS, shard_cols).transpose(1, 0, 2).reshape(N_CHIPS, -1, LANES)


def kernel(x, c, positions, w_ada, b_ada, norm_gain, w_in, q_norm_gain, w_uq, kv_norm_gain, w_ukv, w_branch_a, w_branch_b, w_out, final_norm_gain, loss_target, m_w_ada, m_b_ada, m_norm_gain, m_w_in, m_q_norm_gain, m_w_uq, m_kv_norm_gain, m_w_ukv, m_w_branch_a, m_w_branch_b, m_w_out, m_final_norm_gain, v_w_ada, v_b_ada, v_norm_gain, v_w_in, v_q_norm_gain, v_w_uq, v_kv_norm_gain, v_w_ukv, v_w_branch_a, v_w_branch_b, v_w_out, v_final_norm_gain):
    ix, iy, ic = lax.axis_index("x"), lax.axis_index("y"), lax.axis_index("c")
    me = 4 * ix + 2 * iy + ic
    chip = 2 * ix + iy
    xs = x[0]
    tgt = loss_target[0]
    s_len = xs.shape[0]

    w_in_t = jnp.swapaxes(w_in[0], 0, 1)
    w_in_tb = w_in_t.astype(BF16)
    pack_in = jnp.stack([w_in_tb[:, :HALF_D], w_in_tb[:, HALF_D:]], axis=0)
    small_shards = (w_uq[0], w_ukv[0], w_branch_a[0], w_branch_b[0], w_out[0])
    pack_small = _pack_shards([s.astype(BF16) for s in small_shards]).reshape(2, SMALL_HALF, LANES)
    mg, call, gw_in, gw_small = _gather_call(c, w_ada[0], pack_in, pack_small)
    mod = mg.transpose(1, 0, 2).reshape(N_DEV, 3 * D_MODEL) + b_ada
    mod_me = lax.dynamic_slice_in_dim(mod, me, 1, axis=0)
    shift, scale, gate = mod_me[:, :D_MODEL], mod_me[:, D_MODEL:2 * D_MODEL], mod_me[:, 2 * D_MODEL:]

    f_in_t = gw_in.reshape(2, IN_WIDTH, HALF_D)
    f_uq, f_ukv, f_a, f_b, f_out = _unpack_small(gw_small.reshape(N_CHIPS, SMALL_TOTAL, LANES))
    w_int_t, w_q, w_kv = _internal_weights(f_in_t, f_uq, f_ukv)

    inv_freq = ROPE_BASE ** (-jnp.arange(0, ROPE_DIM, 2, dtype=F32) / ROPE_DIM)
    ang = positions[0].astype(F32)[:, None] * inv_freq
    cs, sn = jnp.cos(ang), jnp.sin(ang)
    cos256 = jnp.tile(jnp.concatenate([cs, cs], axis=1), (1, 8))
    sin256 = jnp.tile(jnp.concatenate([-sn, sn], axis=1), (1, 8))

    (h, sq, sk, sv, sz, cq, ckv, mz, ga, gb, kpt, qn, qp, kn, vv) = _inproj_call(
        xs, shift, scale, norm_gain, w_int_t, w_q, w_kv, q_norm_gain, kv_norm_gain, cos256, sin256)
    oa, lt, first = _sb_fwd_call(sq, sk, sv)
    ob, lse = _mla_fwd_call(qn, qp, kn, kpt, vv)

    gf = final_norm_gain.reshape(1, D_MODEL)
    (dx2, doa, dob, dsz, dmz, dga, dgb, dwo, dwa, dwb, dgf, dgate, loss_p) = _post_call(
        xs, tgt, oa, ob, sz, mz, ga, gb, gate, gf, f_a, f_b, f_out)

    dsq, dsk_t, dsv_t = _sb_bwd_call(first[:, :, 0, 0].reshape(-1), sq, sk, sv, doa, lt)
    dqn, dqp, dkn_t, dkpt_t, dvv_t = _mla_bwd_call(qn, qp, kn, kpt, vv, ob, dob, lse)

    dproj, dwq, dwkv, dqg, dkvg = _bwdprep_call(
        dsq, dsk_t, dsv_t, dsz, dqn, dqp, dkn_t, dvv_t, dkpt_t, dmz, dga, dgb, cq, ckv, cos256, sin256,
        q_norm_gain, kv_norm_gain, w_q, w_kv)
    grad_x, dshift, dscale, dg1 = _dh_call(dproj, w_int_t, xs, dx2, scale, norm_gain)
    dwi_t = _dwin_call(h, dproj)
    g_in_t, g_uq, g_ukv = _true_weight_grads(dwi_t, dwq, dwkv)

    g_in_pieces = g_in_t.reshape(N_DEV, IN_SHARD, HALF_D)
    g_small = jnp.concatenate([
        _chip_major(g_uq, 192), _chip_major(g_ukv, 256), _chip_major(dwa, 256), _chip_major(dwb, 256),
        dwo.reshape(N_CHIPS, -1, LANES)], axis=1).astype(BF16).reshape(N_DEV, SMALL_HALF, LANES)
    small = jnp.concatenate([
        dshift, dscale, dgate, dg1, dqg, dkvg, dgf, loss_p,
        jnp.zeros((1, 8 * SV_COLS - 5888), F32)], axis=1).reshape(8, SV_COLS)
    full_in, full_small, svg = _reduce_call(g_in_pieces, g_small, small)
    full = full_small.reshape(SMALL_TOTAL, LANES)
    offs = [0]
    for r in SMALL_ROWS:
        offs.append(offs[-1] + r)
    gs_uq = full[offs[0]:offs[1]].reshape(Q_RANK, 192)
    gs_ukv = full[offs[1]:offs[2]].reshape(KV_RANK, 256)
    gs_a = full[offs[2]:offs[3]].reshape(512, 256)
    gs_b = full[offs[3]:offs[4]].reshape(512, 256)
    gs_out = full[offs[4]:offs[5]].reshape(256, D_MODEL)

    svm = svg.reshape(N_DEV, 8 * SV_COLS)
    dmod_sh = lax.dynamic_slice_in_dim(svm[:, :3 * D_MODEL], chip * 768, 768, axis=1)
    tot, gs_ada = _small_call(svm, call.T, dmod_sh)
    g_bada = tot[:, 0:3072]
    g_g1 = tot[:, 3072:4096]
    g_qg = tot[:, 4096:4480]
    g_kvg = tot[:, 4480:4736]
    g_gf = tot[:, 4736:5760]
    loss = tot[0, 5760]

    names = ["w_ada", "b_ada", "norm_gain", "w_in", "q_norm_gain", "w_uq", "kv_norm_gain", "w_ukv",
             "w_branch_a", "w_branch_b", "w_out", "final_norm_gain"]
    ws = [w_ada[0], b_ada, norm_gain, w_in_t, q_norm_gain, w_uq[0], kv_norm_gain, w_ukv[0],
          w_branch_a[0], w_branch_b[0], w_out[0], final_norm_gain.reshape(1, D_MODEL)]
    gs = [gs_ada, g_bada, g_g1, full_in, g_qg, gs_uq, g_kvg, gs_ukv, gs_a, gs_b, gs_out, g_gf]
    ms = [m_w_ada[0], m_b_ada, m_norm_gain, jnp.swapaxes(m_w_in[0], 0, 1), m_q_norm_gain, m_w_uq[0],
          m_kv_norm_gain, m_w_ukv[0], m_w_branch_a[0], m_w_branch_b[0], m_w_out[0],
          m_final_norm_gain.reshape(1, D_MODEL)]
    vs = [v_w_ada[0], v_b_ada, v_norm_gain, jnp.swapaxes(v_w_in[0], 0, 1), v_q_norm_gain, v_w_uq[0],
          v_kv_norm_gain, v_w_ukv[0], v_w_branch_a[0], v_w_branch_b[0], v_w_out[0],
          v_final_norm_gain.reshape(1, D_MODEL)]
    refs = [w_ada, b_ada, norm_gain, w_in, q_norm_gain, w_uq, kv_norm_gain, w_ukv,
            w_branch_a, w_branch_b, w_out, final_norm_gain]
    grads, deltas, new_ms, new_vs = [], [], [], []
    for n, w_, g_, m_, v_, ref in zip(names, ws, gs, ms, vs, refs):
        outs = _adamw_call(n, w_, g_, m_, v_)
        if n == "w_in":
            outs = tuple(jnp.swapaxes(o_, 0, 1) for o_ in outs)
        for lst, o_ in zip((grads, deltas, new_ms, new_vs), outs):
            lst.append(o_.reshape(ref.shape))

    return (loss, grad_x.reshape(x.shape), *grads, *deltas, *new_ms, *new_vs)
```

```python
import math

import jax
import jax.numpy as jnp
from jax import lax
from jax.experimental import pallas as pl
from jax.experimental.pallas import tpu as pltpu

F32 = jnp.float32
BF16 = jnp.bfloat16

D_MODEL = 1024
SB_WIDTH = 512
MLA_WIDTH = 512
Q_RANK = 384
KV_RANK = 256
ROPE_DIM = 32
N_HEADS = 8
IN_WIDTH = 5280
EPS = 1e-6
ROPE_BASE = 10000.0
MLA_SCALE = 1.0 / math.sqrt(96.0)
SB_SCALE = 0.125
LOG2E = 1.4426950408889634

ADAM_LR = 0.001
ADAM_B1 = 0.9
ADAM_B2 = 0.999
ADAM_EPS = 1e-08
ADAM_WD = 0.01
ADAM_STEP = 10

O_SQ, O_SK, O_SV, O_SZ, O_CQ, O_CKV, O_MZ, O_GA, O_GB, O_KR, O_END = (
    0, 512, 1024, 1536, 2048, 2432, 2688, 3200, 4224, 5248, 5504)
W_INT = 5632

N_CHIPS = 4
N_DEV = 8
LANES = 128
SV_COLS = 768

ROW_TILE = 256
ATT_TILE = 256
ATT_Q_TILES = 2
FWD_Q_TILES = 4
SB_Q_TILES = 1
MLA_KEY_TILE = 512
VMEM_LIMIT = 56 * 1024 * 1024

MESH = pl.DeviceIdType.MESH


def _dot(a, b):
    return lax.dot_general(a, b, (((1,), (0,)), ((), ())), preferred_element_type=F32)


def _dot_nt(a, b):
    return lax.dot_general(a, b, (((1,), (1,)), ((), ())), preferred_element_type=F32)


def _dot_tn(a, b):
    return lax.dot_general(a, b, (((0,), (0,)), ((), ())), preferred_element_type=F32)


def _sigmoid(z):
    return 1.0 / (1.0 + jnp.exp2(z * (-LOG2E)))


def _params(sem=None):
    if sem is None:
        return pltpu.CompilerParams(vmem_limit_bytes=VMEM_LIMIT)
    return pltpu.CompilerParams(dimension_semantics=sem, vmem_limit_bytes=VMEM_LIMIT)


def _rows(tm, n):
    return pl.BlockSpec((tm, n), lambda i: (i, 0))


def _cols(n, tm):
    return pl.BlockSpec((n, tm), lambda i: (0, i))


def _whole(shape):
    nd = len(shape)
    return pl.BlockSpec(shape, lambda i: (0,) * nd)


def _sds(shape, dtype):
    return jax.ShapeDtypeStruct(shape, dtype)


def _flip(v, d):
    return 1 - v if d else v


def _inproj_call(x, shift, scale, g1, w_int, w_q, w_kv, qg, kvg, cos256, sin256):
    s_len = x.shape[0]
    tm = min(ROW_TILE, s_len)

    def body(x_ref, sh_ref, sc_ref, g1_ref, w_ref, wq_ref, wkv_ref, qg_ref, kvg_ref, cos_ref, sin_ref,
             h_ref, sq_ref, sk_ref, sv_ref, sz_ref, cq_ref, ckv_ref, mz_ref, ga_ref, gb_ref, kpt_ref,
             qn_ref, qp_ref, kn_ref, vv_ref):
        xt = x_ref[...]
        r = lax.rsqrt(jnp.mean(xt * xt, axis=-1, keepdims=True) + EPS)
        h = (xt * r * g1_ref[...]) * (1.0 + sc_ref[...]) + sh_ref[...]
        hb = h.astype(BF16)
        h_ref[...] = hb

        def seg(a, b):
            return _dot_nt(hb[:, :HALF_D], w_ref[0, a:b, :]) + _dot_nt(hb[:, HALF_D:], w_ref[1, a:b, :])

        sq_ref[...] = (seg(O_SQ, O_SK) * SB_SCALE).astype(BF16)
        sk_ref[...] = seg(O_SK, O_SV).astype(BF16)
        sv_ref[...] = seg(O_SV, O_SZ).astype(BF16)
        sz_ref[...] = seg(O_SZ, O_CQ)
        mz_ref[...] = seg(O_MZ, O_GA)
        ga_ref[...] = seg(O_GA, O_GB)
        gb_ref[...] = seg(O_GB, O_KR)
        cos = cos_ref[...]
        sin = sin_ref[...]
        kr = seg(O_KR, O_END)
        kpt_ref[...] = (kr[:, :128] * cos[:, :128] + kr[:, 128:] * sin[:, :128]).astype(BF16)

        cq = seg(O_CQ, O_CKV)
        cq_ref[...] = cq
        rq = lax.rsqrt(jnp.mean(cq * cq, axis=-1, keepdims=True) + EPS)
        cqn = (cq * rq * qg_ref[...]).astype(BF16)
        qa = _dot(cqn, wq_ref[...])
        qn_ref[...] = qa[:, :512].astype(BF16)
        qp_ref[...] = (qa[:, 512:768] * cos + qa[:, 768:] * sin).astype(BF16)

        ckv = seg(O_CKV, O_MZ)
        ckv_ref[...] = ckv
        rk = lax.rsqrt(jnp.mean(ckv * ckv, axis=-1, keepdims=True) + EPS)
        ckvn = (ckv * rk * kvg_ref[...]).astype(BF16)
        kva = _dot(ckvn, wkv_ref[...])
        kn_ref[...] = kva[:, :512].astype(BF16)
        vv_ref[...] = kva[:, 512:].astype(BF16)

    outs = [
        (D_MODEL, BF16), (512, BF16), (512, BF16), (512, BF16), (512, F32), (Q_RANK, F32), (KV_RANK, F32),
        (512, F32), (D_MODEL, F32), (D_MODEL, F32), (128, BF16), (512, BF16), (256, BF16), (512, BF16), (512, BF16),
    ]
    return pl.pallas_call(
        body, name="inproj", grid=(s_len // tm,),
        out_shape=tuple(_sds((s_len, n), dt) for n, dt in outs),
        in_specs=[_rows(tm, D_MODEL), _whole((1, D_MODEL)), _whole((1, D_MODEL)), _whole((1, D_MODEL)),
                  _whole((2, W_INT, HALF_D)), _whole((Q_RANK, 1024)), _whole((KV_RANK, 1024)),
                  _whole((1, Q_RANK)), _whole((1, KV_RANK)), _rows(tm, 256), _rows(tm, 256)],
        out_specs=tuple(_rows(tm, n) for n, _ in outs),
        compiler_params=_params(("parallel",)),
    )(x, shift, scale, g1, w_int, w_q, w_kv, qg, kvg, cos256, sin256)


Z_CLAMP = 80.0 * LOG2E
RUN_CUTOFF = 110.0 * LOG2E


def _softplus_clamped(z):
    zc = jnp.minimum(z * LOG2E, Z_CLAMP)
    return zc, jnp.log2(1.0 + jnp.exp2(zc))


def _tri_sum(a, tri):
    return _dot(a.astype(BF16), tri)


def _sb_fwd_call(q, k, v):
    s_len = q.shape[0]
    tk = min(ATT_TILE, s_len)
    tq = min(SB_Q_TILES * ATT_TILE, s_len)
    r = tq // tk
    nq = s_len // tq

    def body(q_ref, k_ref, v_ref, o_ref, lt_ref, first_ref):
        i = pl.program_id(1)
        q2 = q_ref[...]
        lane = lax.broadcasted_iota(jnp.int32, (1, 256), 1)
        krow = lax.broadcasted_iota(jnp.int32, (tk, tk), 0)
        kcol = lax.broadcasted_iota(jnp.int32, (tk, tk), 1)
        row = lax.broadcasted_iota(jnp.int32, (tq, tk), 0)
        col = lax.broadcasted_iota(jnp.int32, (tq, tk), 1)
        later = (krow > kcol).astype(BF16)
        valids = [col + u * tk < row for u in range(r)]
        hms = [(lane // 64) == hh for hh in range(4)]
        qms = [jnp.where(hm, q2, jnp.zeros_like(q2)) for hm in hms]

        def block(j, carry, valid):
            runs, acc = list(carry[:4]), carry[4]
            off = pl.multiple_of(j * tk, tk)
            kb = k_ref[pl.ds(off, tk), :]
            vb = v_ref[pl.ds(off, tk), :]
            ws = []
            for hh in range(4):
                zc, sp = _softplus_clamped(_dot_nt(qms[hh], kb))
                lm = jnp.where(valid, sp, 0.0) if valid is not None else sp
                suf = _tri_sum(lm, later)
                w = jnp.exp2(zc - sp - suf - runs[hh])
                if valid is not None:
                    w = jnp.where(valid, w, 0.0)
                ws.append(w.astype(BF16))
                runs[hh] = runs[hh] + jnp.sum(lm, axis=1, keepdims=True)
            vstack = jnp.concatenate([jnp.where(hm, vb, jnp.zeros_like(vb)) for hm in hms], axis=0)
            acc = acc + _dot(jnp.concatenate(ws, axis=1), vstack)
            return (*runs, acc)

        zero = jnp.zeros((tq, 1), F32)
        carry = (zero, zero, zero, zero, jnp.zeros((tq, 256), F32))
        for u in reversed(range(r)):
            carry = block(i * r + u, carry, valids[u])

        def least_run(runs):
            return jnp.min(jnp.minimum(jnp.minimum(runs[0], runs[1]), jnp.minimum(runs[2], runs[3])))

        n_full = i * r

        def unfinished(state):
            return jnp.logical_and(state[0] < n_full, state[1] <= RUN_CUTOFF)

        def visit(state):
            cr = block(n_full - 1 - state[0], state[2:], None)
            return (state[0] + 1, least_run(cr[:4]), *cr)

        state = lax.while_loop(unfinished, visit, (jnp.int32(0), least_run(carry[:4]), *carry))
        carry = state[2:]
        first_ref[...] = jnp.full(first_ref.shape, n_full - state[0], jnp.int32)
        for hh in range(4):
            lt_ref[0, :, hh:hh + 1] = carry[hh]
        o_ref[...] = carry[4]

    return pl.pallas_call(
        body, name="sb_fwd", grid=(2, nq),
        out_shape=(_sds((s_len, SB_WIDTH), F32), _sds((2, s_len, 4), F32), _sds((2, nq, 8, 128), jnp.int32)),
        in_specs=[pl.BlockSpec((tq, 256), lambda g, i: (i, g)),
                  pl.BlockSpec((s_len, 256), lambda g, i: (0, g)),
                  pl.BlockSpec((s_len, 256), lambda g, i: (0, g))],
        out_specs=(pl.BlockSpec((tq, 256), lambda g, i: (i, g)),
                   pl.BlockSpec((1, tq, 4), lambda g, i: (g, i, 0)),
                   pl.BlockSpec((1, 1, 8, 128), lambda g, i: (g, i, 0, 0))),
        compiler_params=_params(("parallel", "parallel")),
    )(q, k, v)


def _sb_bwd_call(first, q, k, v, do, lt):
    s_len = q.shape[0]
    tk = min(ATT_TILE, s_len)
    tq = min(SB_Q_TILES * ATT_TILE, s_len)
    r = tq // tk
    nq = s_len // tq
    nq_fwd = first.shape[0] // 2
    per_fwd = nq // nq_fwd

    def body(first_ref, q_ref, k_ref, v_ref, do_ref, lt_ref, dq_ref, dk_ref, dv_ref):
        g = pl.program_id(0)
        i = pl.program_id(1)

        @pl.when(i == 0)
        def _():
            dk_ref[...] = jnp.zeros_like(dk_ref)
            dv_ref[...] = jnp.zeros_like(dv_ref)

        q2 = q_ref[...]
        do2 = do_ref[...].astype(BF16)
        lane = lax.broadcasted_iota(jnp.int32, (1, 256), 1)
        krow = lax.broadcasted_iota(jnp.int32, (tk, tk), 0)
        kcol = lax.broadcasted_iota(jnp.int32, (tk, tk), 1)
        row = lax.broadcasted_iota(jnp.int32, (tq, tk), 0)
        col = lax.broadcasted_iota(jnp.int32, (tq, tk), 1)
        earlier = (krow < kcol).astype(BF16)
        later = (krow > kcol).astype(BF16)
        valids = [col + u * tk < row for u in range(r)]
        hms = [(lane // 64) == hh for hh in range(4)]
        qms = [jnp.where(hm, q2, jnp.zeros_like(q2)) for hm in hms]
        doms = [jnp.where(hm, do2, jnp.zeros_like(do2)) for hm in hms]
        ltots = [lt_ref[0, :, hh:hh + 1] for hh in range(4)]
        q2t = jnp.transpose(q2.astype(F32))
        do2t = jnp.transpose(do_ref[...])
        subl = lax.broadcasted_iota(jnp.int32, (256, 1), 0)
        qtstack = jnp.concatenate(
            [jnp.where((subl // 64) == hh, q2t, 0.0).astype(BF16) for hh in range(4)], axis=1)
        dotstack = jnp.concatenate(
            [jnp.where((subl // 64) == hh, do2t, 0.0).astype(BF16) for hh in range(4)], axis=1)

        def block(j, carry, valid):
            lpre, ppre, dq = list(carry[0:4]), list(carry[4:8]), carry[8]
            off = pl.multiple_of(j * tk, tk)
            kb = k_ref[pl.ds(off, tk), :]
            vb = v_ref[pl.ds(off, tk), :]
            dzs, avs = [], []
            for hh in range(4):
                zc, sp = _softplus_clamped(_dot_nt(qms[hh], kb))
                lsig = zc - sp
                lm = jnp.where(valid, sp, 0.0) if valid is not None else sp
                rowsum = jnp.sum(lm, axis=1, keepdims=True)
                between = _tri_sum(lm, later) + ((ltots[hh] - lpre[hh]) - rowsum)
                a = jnp.exp2(lsig - between)
                if valid is not None:
                    a = jnp.where(valid, a, 0.0)
                p = a * _dot_nt(doms[hh], vb)
                pbefore = ppre[hh] + _tri_sum(p, earlier)
                dz = p - jnp.exp2(lsig) * (p + pbefore)
                if valid is not None:
                    dz = jnp.where(valid, dz, 0.0)
                dzs.append(dz.astype(BF16))
                avs.append(a.astype(BF16))
                lpre[hh] = lpre[hh] + rowsum
                ppre[hh] = ppre[hh] + jnp.sum(p, axis=1, keepdims=True)
            kstack = jnp.concatenate([jnp.where(hm, kb, jnp.zeros_like(kb)) for hm in hms], axis=0)
            dq = dq + _dot(jnp.concatenate(dzs, axis=1), kstack)
            dk_ref[:, pl.ds(off, tk)] += _dot(qtstack, jnp.concatenate(dzs, axis=0))
            dv_ref[:, pl.ds(off, tk)] += _dot(dotstack, jnp.concatenate(avs, axis=0))
            return (*lpre, *ppre, dq)

        zero = jnp.zeros((tq, 1), F32)
        start = jnp.minimum(first_ref[g * nq_fwd + i // per_fwd], i * r)
        carry = lax.fori_loop(start, i * r, lambda j, cr: block(j, cr, None),
                              (zero,) * 8 + (jnp.zeros((tq, 256), F32),))
        for u in range(r):
            carry = block(i * r + u, carry, valids[u])
        dq_ref[...] = carry[8].astype(BF16)

    return pl.pallas_call(
        body, name="sb_bwd",
        out_shape=(_sds((s_len, SB_WIDTH), BF16), _sds((SB_WIDTH, s_len), F32), _sds((SB_WIDTH, s_len), F32)),
        grid_spec=pltpu.PrefetchScalarGridSpec(
            num_scalar_prefetch=1, grid=(2, nq),
            in_specs=[pl.BlockSpec((tq, 256), lambda g, i, f: (i, g)),
                      pl.BlockSpec((s_len, 256), lambda g, i, f: (0, g)),
                      pl.BlockSpec((s_len, 256), lambda g, i, f: (0, g)),
                      pl.BlockSpec((tq, 256), lambda g, i, f: (i, g)),
                      pl.BlockSpec((1, tq, 4), lambda g, i, f: (g, i, 0))],
            out_specs=(pl.BlockSpec((tq, 256), lambda g, i, f: (i, g)),
                       pl.BlockSpec((256, s_len), lambda g, i, f: (g, 0)),
                       pl.BlockSpec((256, s_len), lambda g, i, f: (g, 0)))),
        compiler_params=_params(("parallel", "arbitrary")),
    )(first, q, k, v, do, lt)


def _mla_fwd_call(qn, qp, kn, kpt, v):
    s_len = qn.shape[0]
    tk = min(MLA_KEY_TILE, s_len)
    tq = min(FWD_Q_TILES * ATT_TILE, s_len)
    r = tq // tk
    nq = s_len // tq

    def body(qn_ref, qp_ref, kn_ref, kpt_ref, v_ref, o_ref, lse_ref):
        i = pl.program_id(1)
        qn2 = qn_ref[...]
        qp2 = qp_ref[...]
        lane256 = lax.broadcasted_iota(jnp.int32, (1, 256), 1)
        lane128 = lax.broadcasted_iota(jnp.int32, (1, 128), 1)
        krow = lax.broadcasted_iota(jnp.int32, (tk, tk), 0)
        kcol = lax.broadcasted_iota(jnp.int32, (tk, tk), 1)
        row = lax.broadcasted_iota(jnp.int32, (tq, tk), 0)
        col = lax.broadcasted_iota(jnp.int32, (tq, tk), 1)
        valids = [col + u * tk <= row for u in range(r)]
        m64s = [(lane256 // 64) == hh for hh in range(4)]
        half = [(lane128 // 64) == u for u in range(2)]
        m32s = [(lane128 // 32) == hh for hh in range(4)]
        qcs = []
        for hh in range(4):
            qpair = qn2[:, 128 * (hh // 2):128 * (hh // 2) + 128]
            qcs.append(jnp.concatenate([jnp.where(half[hh % 2], qpair, jnp.zeros_like(qpair)),
                                        jnp.where(m32s[hh], qp2, jnp.zeros_like(qp2))], axis=1))

        def by_head(vals):
            return jnp.where(m64s[0], vals[0], jnp.where(m64s[1], vals[1], jnp.where(m64s[2], vals[2], vals[3])))

        def block(j, carry, valid):
            ms, ls, acc = list(carry[0:4]), list(carry[4:8]), carry[8]
            off = pl.multiple_of(j * tk, tk)
            knb = kn_ref[pl.ds(off, tk), :]
            kpb = kpt_ref[pl.ds(off, tk), :]
            vb = v_ref[pl.ds(off, tk), :]
            kcs = [jnp.concatenate([knb[:, 128 * pp:128 * pp + 128], kpb], axis=1) for pp in range(2)]
            ps, alphas = [], []
            for hh in range(4):
                s = _dot_nt(qcs[hh], kcs[hh // 2]) * (MLA_SCALE * LOG2E)
                if valid is not None:
                    s = jnp.where(valid, s, -1e30)
                mn = jnp.maximum(ms[hh], jnp.max(s, axis=1, keepdims=True))
                p = jnp.exp2(s - mn)
                alpha = jnp.exp2(ms[hh] - mn)
                ls[hh] = alpha * ls[hh] + jnp.sum(p, axis=1, keepdims=True)
                ms[hh] = mn
                ps.append(p.astype(BF16))
                alphas.append(alpha)
            pvs = []
            for pp in range(2):
                vpair = vb[:, 128 * pp:128 * pp + 128]
                vstack = jnp.concatenate([jnp.where(hf, vpair, jnp.zeros_like(vpair)) for hf in half], axis=0)
                pvs.append(_dot(jnp.concatenate(ps[2 * pp:2 * pp + 2], axis=1), vstack))
            acc = by_head(alphas) * acc + jnp.concatenate(pvs, axis=1)
            return (*ms, *ls, acc)

        neg = jnp.full((tq, 1), -1e30, F32)
        zero = jnp.zeros((tq, 1), F32)
        carry = lax.fori_loop(0, i * r, lambda j, cr: block(j, cr, None),
                              (neg,) * 4 + (zero,) * 4 + (jnp.zeros((tq, 256), F32),))
        for u in range(r):
            carry = block(i * r + u, carry, valids[u])
        o_ref[...] = carry[8] / by_head(list(carry[4:8]))
        for hh in range(4):
            lse_ref[0, :, hh:hh + 1] = (carry[hh] + jnp.log2(carry[4 + hh])) * (1.0 / LOG2E)

    return pl.pallas_call(
        body, name="mla_fwd", grid=(2, nq),
        out_shape=(_sds((s_len, MLA_WIDTH), F32), _sds((2, s_len, 4), F32)),
        in_specs=[pl.BlockSpec((tq, 256), lambda g, i: (i, g)),
                  pl.BlockSpec((tq, 128), lambda g, i: (i, g)),
                  pl.BlockSpec((s_len, 256), lambda g, i: (0, g)),
                  pl.BlockSpec((s_len, 128), lambda g, i: (0, 0)),
                  pl.BlockSpec((s_len, 256), lambda g, i: (0, g))],
        out_specs=(pl.BlockSpec((tq, 256), lambda g, i: (i, g)),
                   pl.BlockSpec((1, tq, 4), lambda g, i: (g, i, 0))),
        compiler_params=_params(("parallel", "parallel")),
    )(qn, qp, kn, kpt, v)


def _mla_bwd_call(qn, qp, kn, kpt, v, o, do, lse):
    s_len = qn.shape[0]
    tk = min(MLA_KEY_TILE, s_len)
    tq = min(ATT_Q_TILES * ATT_TILE, s_len)
    r = tq // tk
    nq = s_len // tq

    def body(qn_ref, qp_ref, kn_ref, kpt_ref, v_ref, o_ref, do_ref, lse_ref,
             dqn_ref, dqp_ref, dkn_ref, dkpt_ref, dv_ref):
        g = pl.program_id(0)
        i = pl.program_id(1)

        @pl.when(i == 0)
        def _():
            dkn_ref[...] = jnp.zeros_like(dkn_ref)
            dv_ref[...] = jnp.zeros_like(dv_ref)

        @pl.when((i == 0) & (g == 0))
        def _():
            dkpt_ref[...] = jnp.zeros_like(dkpt_ref)

        qn2 = qn_ref[...]
        qp2 = qp_ref[...]
        dof = do_ref[...]
        dob = dof.astype(BF16)
        prod = dof * o_ref[...]
        lane256 = lax.broadcasted_iota(jnp.int32, (1, 256), 1)
        lane128 = lax.broadcasted_iota(jnp.int32, (1, 128), 1)
        krow = lax.broadcasted_iota(jnp.int32, (tk, tk), 0)
        kcol = lax.broadcasted_iota(jnp.int32, (tk, tk), 1)
        row = lax.broadcasted_iota(jnp.int32, (tq, tk), 0)
        col = lax.broadcasted_iota(jnp.int32, (tq, tk), 1)
        valids = [col + u * tk <= row for u in range(r)]
        m64s = [(lane256 // 64) == hh for hh in range(4)]
        half = [(lane128 // 64) == u for u in range(2)]
        m32s = [(lane128 // 32) == hh for hh in range(4)]
        qcs, doms = [], []
        for hh in range(4):
            sl = slice(128 * (hh // 2), 128 * (hh // 2) + 128)
            qpair = qn2[:, sl]
            dpair = dob[:, sl]
            qcs.append(jnp.concatenate([jnp.where(half[hh % 2], qpair, jnp.zeros_like(qpair)),
                                        jnp.where(m32s[hh], qp2, jnp.zeros_like(qp2))], axis=1))
            doms.append(jnp.where(half[hh % 2], dpair, jnp.zeros_like(dpair)))
        dsums = [jnp.sum(jnp.where(m64, prod, 0.0), axis=1, keepdims=True) * MLA_SCALE for m64 in m64s]
        lses = [lse_ref[0, :, hh:hh + 1] * LOG2E for hh in range(4)]
        qn2t = jnp.transpose(qn2.astype(F32))
        qp2t = jnp.transpose(qp2.astype(F32))
        do2t = jnp.transpose(dof)
        sub128 = lax.broadcasted_iota(jnp.int32, (128, 1), 0)
        qtstacks, dotstacks = [], []
        for pp in range(2):
            qts, dts = [], []
            for u in range(2):
                hh = 2 * pp + u
                qts.append(jnp.concatenate(
                    [jnp.where((sub128 // 64) == u, qn2t[128 * pp:128 * pp + 128, :], 0.0),
                     jnp.where((sub128 // 32) == hh, qp2t, 0.0)], axis=0).astype(BF16))
                dts.append(jnp.where((sub128 // 64) == u, do2t[128 * pp:128 * pp + 128, :], 0.0).astype(BF16))
            qtstacks.append(jnp.concatenate(qts, axis=1))
            dotstacks.append(jnp.concatenate(dts, axis=1))

        def block(j, carry, valid):
            dqn, dqp = carry
            off = pl.multiple_of(j * tk, tk)
            knb = kn_ref[pl.ds(off, tk), :]
            kpb = kpt_ref[pl.ds(off, tk), :]
            vb = v_ref[pl.ds(off, tk), :]
            dqn_parts = []
            dkp = None
            for pp in range(2):
                sl = slice(128 * pp, 128 * pp + 128)
                knp = knb[:, sl]
                vpair = vb[:, sl]
                kc = jnp.concatenate([knp, kpb], axis=1)
                dss, pbs, kcms = [], [], []
                for u in range(2):
                    hh = 2 * pp + u
                    s = _dot_nt(qcs[hh], kc) * (MLA_SCALE * LOG2E)
                    if valid is not None:
                        s = jnp.where(valid, s, -1e30)
                    p = jnp.exp2(s - lses[hh])
                    ds = p * (_dot_nt(doms[hh], vpair) * MLA_SCALE - dsums[hh])
                    dss.append(ds.astype(BF16))
                    pbs.append(p.astype(BF16))
                    kcms.append(jnp.concatenate([jnp.where(half[u], knp, jnp.zeros_like(knp)),
                                                 jnp.where(m32s[hh], kpb, jnp.zeros_like(kpb))], axis=1))
                dqc = _dot(jnp.concatenate(dss, axis=1), jnp.concatenate(kcms, axis=0))
                dqn_parts.append(dqc[:, :128])
                dqp = dqp + dqc[:, 128:]
                dkc = _dot(qtstacks[pp], jnp.concatenate(dss, axis=0))
                dkn_ref[128 * pp:128 * pp + 128, pl.ds(off, tk)] += dkc[:128, :]
                dkp = dkc[128:, :] if dkp is None else dkp + dkc[128:, :]
                dv_ref[128 * pp:128 * pp + 128, pl.ds(off, tk)] += _dot(dotstacks[pp], jnp.concatenate(pbs, axis=0))
            dqn = dqn + jnp.concatenate(dqn_parts, axis=1)
            dkpt_ref[:, pl.ds(off, tk)] += dkp
            return dqn, dqp

        carry = lax.fori_loop(0, i * r, lambda j, cr: block(j, cr, None),
                              (jnp.zeros((tq, 256), F32), jnp.zeros((tq, 128), F32)))
        for u in range(r):
            carry = block(i * r + u, carry, valids[u])
        dqn, dqp = carry
        dqn_ref[...] = dqn.astype(BF16)
        dqp_ref[...] = dqp.astype(BF16)

    return pl.pallas_call(
        body, name="mla_bwd", grid=(2, nq),
        out_shape=(_sds((s_len, 512), BF16), _sds((s_len, 256), BF16), _sds((512, s_len), F32),
                   _sds((128, s_len), F32), _sds((512, s_len), F32)),
        in_specs=[pl.BlockSpec((tq, 256), lambda g, i: (i, g)),
                  pl.BlockSpec((tq, 128), lambda g, i: (i, g)),
                  pl.BlockSpec((s_len, 256), lambda g, i: (0, g)),
                  pl.BlockSpec((s_len, 128), lambda g, i: (0, 0)),
                  pl.BlockSpec((s_len, 256), lambda g, i: (0, g)),
                  pl.BlockSpec((tq, 256), lambda g, i: (i, g)),
                  pl.BlockSpec((tq, 256), lambda g, i: (i, g)),
                  pl.BlockSpec((1, tq, 4), lambda g, i: (g, i, 0))],
        out_specs=(pl.BlockSpec((tq, 256), lambda g, i: (i, g)),
                   pl.BlockSpec((tq, 128), lambda g, i: (i, g)),
                   pl.BlockSpec((256, s_len), lambda g, i: (g, 0)),
                   pl.BlockSpec((128, s_len), lambda g, i: (0, 0)),
                   pl.BlockSpec((256, s_len), lambda g, i: (g, 0))),
        compiler_params=_params(("arbitrary", "arbitrary")),
    )(qn, qp, kn, kpt, v, o, do, lse)


def _post_call(x, tgt, oa, ob, sz, mz, ga, gb, gate, gf, wa, wb, wo):
    s_len = x.shape[0]
    tm = min(ROW_TILE, s_len)
    nstep = s_len // tm

    def body(x_ref, t_ref, oa_ref, ob_ref, sz_ref, mz_ref, ga_ref, gb_ref, gate_ref, gf_ref,
             wa_ref, wb_ref, wo_ref,
             dx2_ref, doa_ref, dob_ref, dsz_ref, dmz_ref, dga_ref, dgb_ref,
             dwo_out, dwa_out, dwb_out, dgf_ref, dgate_ref, loss_ref, dwo_ref, dwa_ref, dwb_ref):
        @pl.when(pl.program_id(0) == 0)
        def _():
            dwo_ref[...] = jnp.zeros_like(dwo_ref)
            dwa_ref[...] = jnp.zeros_like(dwa_ref)
            dwb_ref[...] = jnp.zeros_like(dwb_ref)
            dgf_ref[...] = jnp.zeros_like(dgf_ref)
            dgate_ref[...] = jnp.zeros_like(dgate_ref)
            loss_ref[...] = jnp.zeros_like(loss_ref)

        gate = gate_ref[...]
        gf = gf_ref[...]
        oa = oa_ref[...]
        ob = ob_ref[...]
        sz = sz_ref[...]
        mz = mz_ref[...]
        sa = _sigmoid(sz)
        sb = _sigmoid(mz)
        silu_a = sz * sa
        silu_b = mz * sb
        ua = (oa * silu_a).astype(BF16)
        ub = (ob * silu_b).astype(BF16)
        ya = _dot(ua, wa_ref[...])
        yb = _dot(ub, wb_ref[...])
        sga = _sigmoid(ga_ref[...])
        sgb = _sigmoid(gb_ref[...])
        merged = (sga * ya + sgb * yb).astype(BF16)
        out = _dot(merged, wo_ref[...])
        x2 = x_ref[...] + gate * out
        r2 = lax.rsqrt(jnp.mean(x2 * x2, axis=-1, keepdims=True) + EPS)
        xhat = x2 * r2
        err = xhat * gf - t_ref[...]
        loss_ref[...] += 0.5 * jnp.sum(jnp.sum(err * err, axis=1, keepdims=True), axis=0, keepdims=True) / D_MODEL
        dy = err * (1.0 / D_MODEL)
        dgf_ref[...] += jnp.sum(dy * xhat, axis=0, keepdims=True)
        dxhat = dy * gf
        dx2 = r2 * (dxhat - xhat * jnp.mean(dxhat * xhat, axis=-1, keepdims=True))
        dx2_ref[...] = dx2
        dgate_ref[...] += jnp.sum(dx2 * out, axis=0, keepdims=True)
        dout = (dx2 * gate).astype(BF16)
        dmerged = _dot_nt(dout, wo_ref[...])
        dwo_ref[...] += _dot_tn(merged, dout)
        dya = dmerged * sga
        dyb = dmerged * sgb
        dga_ref[...] = (dya * ya * (1.0 - sga)).astype(BF16)
        dgb_ref[...] = (dyb * yb * (1.0 - sgb)).astype(BF16)
        dyab = dya.astype(BF16)
        dybb = dyb.astype(BF16)
        dua = _dot_nt(dyab, wa_ref[...])
        dub = _dot_nt(dybb, wb_ref[...])
        dwa_ref[...] += _dot_tn(ua, dyab)
        dwb_ref[...] += _dot_tn(ub, dybb)
        doa_ref[...] = dua * silu_a
        dob_ref[...] = dub * silu_b
        dsz_ref[...] = (dua * oa * (sa * (1.0 + sz * (1.0 - sa)))).astype(BF16)
        dmz_ref[...] = (dub * ob * (sb * (1.0 + mz * (1.0 - sb)))).astype(BF16)

        @pl.when(pl.program_id(0) == nstep - 1)
        def _():
            dwo_out[...] = dwo_ref[...].astype(BF16)
            for k in range(N_CHIPS):
                dwa_out[k] = dwa_ref[:, 256 * k:256 * k + 256].astype(BF16)
                dwb_out[k] = dwb_ref[:, 256 * k:256 * k + 256].astype(BF16)

    return pl.pallas_call(
        body, name="post", grid=(nstep,),
        out_shape=(_sds((s_len, D_MODEL), F32), _sds((s_len, 512), F32), _sds((s_len, 512), F32),
                   _sds((s_len, 512), BF16), _sds((s_len, 512), BF16),
                   _sds((s_len, D_MODEL), BF16), _sds((s_len, D_MODEL), BF16),
                   _sds((D_MODEL, D_MODEL), BF16), _sds((N_CHIPS, 512, 256), BF16), _sds((N_CHIPS, 512, 256), BF16),
                   _sds((1, D_MODEL), F32), _sds((1, D_MODEL), F32), _sds((1, 128), F32)),
        in_specs=[_rows(tm, D_MODEL), _rows(tm, D_MODEL), _rows(tm, 512), _rows(tm, 512), _rows(tm, 512),
                  _rows(tm, 512), _rows(tm, D_MODEL), _rows(tm, D_MODEL), _whole((1, D_MODEL)), _whole((1, D_MODEL)),
                  _whole((512, D_MODEL)), _whole((512, D_MODEL)), _whole((D_MODEL, D_MODEL))],
        out_specs=(_rows(tm, D_MODEL), _rows(tm, 512), _rows(tm, 512), _rows(tm, 512), _rows(tm, 512),
                   _rows(tm, D_MODEL), _rows(tm, D_MODEL),
                   _whole((D_MODEL, D_MODEL)), _whole((N_CHIPS, 512, 256)), _whole((N_CHIPS, 512, 256)),
                   _whole((1, D_MODEL)), _whole((1, D_MODEL)), _whole((1, 128))),
        scratch_shapes=[pltpu.VMEM((D_MODEL, D_MODEL), F32), pltpu.VMEM((512, D_MODEL), F32),
                        pltpu.VMEM((512, D_MODEL), F32)],
        compiler_params=_params(("arbitrary",)),
    )(x, tgt, oa, ob, sz, mz, ga, gb, gate, gf, wa, wb, wo)


def _bwdprep_call(dsq, dsk, dsv, dsz, dqn, dqp, dkn, dvv, dkpt, dmz, dga, dgb, cq, ckv, cos256, sin256,
                  qg, kvg, w_q, w_kv):
    s_len = cq.shape[0]
    tm = min(ROW_TILE, s_len)

    def body(dsq_ref, dsk_ref, dsv_ref, dsz_ref, dqn_ref, dqp_ref, dkn_ref, dvv_ref, dkpt_ref, dmz_ref,
             dga_ref, dgb_ref, cq_ref, ckv_ref, cos_ref, sin_ref, qg_ref, kvg_ref, wq_ref, wkv_ref,
             dp_ref, dwq_ref, dwkv_ref, dqg_ref, dkvg_ref):
        @pl.when(pl.program_id(0) == 0)
        def _():
            dwq_ref[...] = jnp.zeros_like(dwq_ref)
            dwkv_ref[...] = jnp.zeros_like(dwkv_ref)
            dqg_ref[...] = jnp.zeros_like(dqg_ref)
            dkvg_ref[...] = jnp.zeros_like(dkvg_ref)

        cos = cos_ref[...]
        sin = sin_ref[...]
        dp_ref[:, O_SQ:O_SK] = dsq_ref[...] * jnp.asarray(SB_SCALE, BF16)
        dp_ref[:, O_SK:O_SV] = jnp.transpose(dsk_ref[...]).astype(BF16)
        dp_ref[:, O_SV:O_SZ] = jnp.transpose(dsv_ref[...]).astype(BF16)
        dp_ref[:, O_SZ:O_CQ] = dsz_ref[...]
        dp_ref[:, O_MZ:O_GA] = dmz_ref[...]
        dp_ref[:, O_GA:O_GB] = dga_ref[...]
        dp_ref[:, O_GB:O_KR] = dgb_ref[...]
        dkp = jnp.transpose(dkpt_ref[...])
        dp_ref[:, O_KR:O_KR + 128] = (dkp * cos[:, :128]).astype(BF16)
        dp_ref[:, O_KR + 128:O_END] = (dkp * sin[:, :128]).astype(BF16)
        dp_ref[:, O_END:W_INT] = jnp.zeros((tm, W_INT - O_END), BF16)

        cq = cq_ref[...]
        rq = lax.rsqrt(jnp.mean(cq * cq, axis=-1, keepdims=True) + EPS)
        cqh = cq * rq
        qg = qg_ref[...]
        cqn = (cqh * qg).astype(BF16)
        dqp = dqp_ref[...].astype(F32)
        dqa = jnp.concatenate([dqn_ref[...], (dqp * cos).astype(BF16), (dqp * sin).astype(BF16)], axis=1)
        dcqn = _dot_nt(dqa, wq_ref[...])
        dwq_ref[...] += _dot_tn(cqn, dqa)
        dqg_ref[...] += jnp.sum(dcqn * cqh, axis=0, keepdims=True)
        dh = dcqn * qg
        dcq = rq * (dh - cqh * jnp.mean(dh * cqh, axis=-1, keepdims=True))
        dp_ref[:, O_CQ:O_CKV] = dcq.astype(BF16)

        ckv = ckv_ref[...]
        rk = lax.rsqrt(jnp.mean(ckv * ckv, axis=-1, keepdims=True) + EPS)
        ckh = ckv * rk
        kvg = kvg_ref[...]
        ckvn = (ckh * kvg).astype(BF16)
        dkva = jnp.concatenate([jnp.transpose(dkn_ref[...]).astype(BF16),
                                jnp.transpose(dvv_ref[...]).astype(BF16)], axis=1)
        dckvn = _dot_nt(dkva, wkv_ref[...])
        dwkv_ref[...] += _dot_tn(ckvn, dkva)
        dkvg_ref[...] += jnp.sum(dckvn * ckh, axis=0, keepdims=True)
        dh2 = dckvn * kvg
        dckv = rk * (dh2 - ckh * jnp.mean(dh2 * ckh, axis=-1, keepdims=True))
        dp_ref[:, O_CKV:O_MZ] = dckv.astype(BF16)

    return pl.pallas_call(
        body, name="bwdprep", grid=(s_len // tm,),
        out_shape=(_sds((s_len, W_INT), BF16), _sds((Q_RANK, 1024), F32), _sds((KV_RANK, 1024), F32),
                   _sds((1, Q_RANK), F32), _sds((1, KV_RANK), F32)),
        in_specs=[_rows(tm, 512), _cols(512, tm), _cols(512, tm), _rows(tm, 512), _rows(tm, 512), _rows(tm, 256),
                  _cols(512, tm), _cols(512, tm), _cols(128, tm), _rows(tm, 512), _rows(tm, D_MODEL),
                  _rows(tm, D_MODEL), _rows(tm, Q_RANK), _rows(tm, KV_RANK), _rows(tm, 256), _rows(tm, 256),
                  _whole((1, Q_RANK)), _whole((1, KV_RANK)), _whole((Q_RANK, 1024)), _whole((KV_RANK, 1024))],
        out_specs=(_rows(tm, W_INT), _whole((Q_RANK, 1024)), _whole((KV_RANK, 1024)),
                   _whole((1, Q_RANK)), _whole((1, KV_RANK))),
        compiler_params=_params(("arbitrary",)),
    )(dsq, dsk, dsv, dsz, dqn, dqp, dkn, dvv, dkpt, dmz, dga, dgb, cq, ckv, cos256, sin256, qg, kvg, w_q, w_kv)


def _dh_call(dproj, w_int_t, x, dx2, scale, g1):
    s_len = x.shape[0]
    tm = min(2 * ROW_TILE, s_len)

    def body(dp_ref, wt_ref, x_ref, dx2_ref, sc_ref, g1_ref, gx_ref, dsh_ref, dsc_ref, dg1_ref):
        @pl.when(pl.program_id(0) == 0)
        def _():
            dsh_ref[...] = jnp.zeros_like(dsh_ref)
            dsc_ref[...] = jnp.zeros_like(dsc_ref)
            dg1_ref[...] = jnp.zeros_like(dg1_ref)

        dp = dp_ref[...]
        dh = jnp.concatenate([_dot(dp, wt_ref[0]), _dot(dp, wt_ref[1])], axis=1)
        xt = x_ref[...]
        r = lax.rsqrt(jnp.mean(xt * xt, axis=-1, keepdims=True) + EPS)
        xh = xt * r
        g1 = g1_ref[...]
        xg = xh * g1
        dsh_ref[...] += jnp.sum(dh, axis=0, keepdims=True)
        dsc_ref[...] += jnp.sum(dh * xg, axis=0, keepdims=True)
        dxg = dh * (1.0 + sc_ref[...])
        dg1_ref[...] += jnp.sum(dxg * xh, axis=0, keepdims=True)
        dxh = dxg * g1
        gx_ref[...] = dx2_ref[...] + r * (dxh - xh * jnp.mean(dxh * xh, axis=-1, keepdims=True))

    return pl.pallas_call(
        body, name="dh", grid=(s_len // tm,),
        out_shape=(_sds((s_len, D_MODEL), F32), _sds((1, D_MODEL), F32), _sds((1, D_MODEL), F32),
                   _sds((1, D_MODEL), F32)),
        in_specs=[_rows(tm, W_INT), _whole((2, W_INT, HALF_D)), _rows(tm, D_MODEL), _rows(tm, D_MODEL),
                  _whole((1, D_MODEL)), _whole((1, D_MODEL))],
        out_specs=(_rows(tm, D_MODEL), _whole((1, D_MODEL)), _whole((1, D_MODEL)), _whole((1, D_MODEL))),
        compiler_params=_params(("arbitrary",)),
    )(dproj, w_int_t, x, dx2, scale, g1)


def _small_call(svg, ct, dmod_sh):
    def body(sv_ref, ct_ref, dm_ref, tot_ref, gwada_ref):
        acc = sv_ref[0:1, :]
        for d in range(1, N_DEV):
            acc = acc + sv_ref[d:d + 1, :]
        tot_ref[...] = acc
        gwada_ref[...] = lax.dot_general(ct_ref[...], dm_ref[...], (((1,), (0,)), ((), ())),
                                         precision=lax.Precision.HIGHEST, preferred_element_type=F32)

    vmem = pl.BlockSpec(memory_space=pltpu.VMEM)
    return pl.pallas_call(
        body, name="small_grads",
        out_shape=(_sds((1, 8 * SV_COLS), F32), _sds((D_MODEL, 768), F32)),
        in_specs=[vmem, vmem, vmem], out_specs=(vmem, vmem),
        compiler_params=_params(),
    )(svg, ct, dmod_sh)


def _adamw_tile_rows(rows, cols):
    budget = 2 << 20
    if rows * cols * 4 <= budget or rows % 8:
        return rows
    best = 8
    for tr in range(8, rows + 1, 8):
        if rows % tr == 0 and tr * cols * 4 <= budget:
            best = tr
    return best


def _adamw_math(w, gg, m, v):
    m2 = ADAM_B1 * m + (1.0 - ADAM_B1) * gg
    v2 = ADAM_B2 * v + (1.0 - ADAM_B2) * (gg * gg)
    m_hat = m2 / (1.0 - ADAM_B1 ** ADAM_STEP)
    v_hat = v2 / (1.0 - ADAM_B2 ** ADAM_STEP)
    return -ADAM_LR * (m_hat / (jnp.sqrt(v_hat) + ADAM_EPS) + ADAM_WD * w), m2, v2


def _adamw_call(name, w, g, m, v):
    rows, cols = w.shape
    tr = _adamw_tile_rows(rows, cols)
    halves = g.ndim == 3

    def body(w_ref, g_ref, m_ref, v_ref, *out_refs):
        if halves:
            gg = jnp.concatenate([g_ref[0], g_ref[1]], axis=1)
            out_refs[0][...] = gg
        else:
            gg = g_ref[...]
        d_ref, nm_ref, nv_ref = out_refs[-3:]
        d_ref[...], nm_ref[...], nv_ref[...] = _adamw_math(w_ref[...], gg, m_ref[...], v_ref[...])

    spec = pl.BlockSpec((tr, cols), lambda i: (i, 0))
    g_spec = pl.BlockSpec((2, tr, cols // 2), lambda i: (0, i, 0)) if halves else spec
    n_out = 4 if halves else 3
    outs = pl.pallas_call(
        body, name="adamw_" + name, grid=(rows // tr,),
        out_shape=(_sds((rows, cols), F32),) * n_out,
        in_specs=[spec, g_spec, spec, spec], out_specs=(spec,) * n_out,
        compiler_params=_params(("parallel",)),
    )(w, g, m, v)
    return tuple(outs) if halves else (g,) + tuple(outs)


IN_SHARD = IN_WIDTH // N_CHIPS
HALF_D = D_MODEL // 2
SMALL_ROWS = (576, 512, 1024, 1024, 2048)
SMALL_TOTAL = sum(SMALL_ROWS)
SMALL_HALF = SMALL_TOTAL // 2
SMALL_SUM_ROWS = 432


def _gather_call(c_row, w_ada_sh, pack_in, pack_small):
    def body(c_ref, wada_ref, pki_ref, pks_ref, mg_ref, cg_ref, gwi_ref, gws_ref,
             cv, ssem_c, rsem_c, ssem_m, rsem_m, ssem_w, rsem_w, ssem_f, rsem_f, lsem):
        x, y, c = lax.axis_index("x"), lax.axis_index("y"), lax.axis_index("c")
        me = 4 * x + 2 * y + c
        chip = 2 * x + y
        rel3 = [(1, 0), (0, 1), (1, 1)]
        packs = [(pki_ref, gwi_ref), (pks_ref, gws_ref)]

        def slot(a, gw, k, h):
            return gw.at[h, k] if a == 0 else gw.at[k, h]

        sends = []
        for j, (dx, dy) in enumerate(rel3):
            for a, (pk, gw) in enumerate(packs):
                cp = pltpu.make_async_remote_copy(
                    src_ref=pk.at[c], dst_ref=slot(a, gw, chip, c), send_sem=ssem_w.at[j, a],
                    recv_sem=rsem_w.at[j, a], device_id=(_flip(x, dx), _flip(y, dy), c), device_id_type=MESH)
                cp.start()
                sends.append(cp)
        owns = []
        for a, (pk, gw) in enumerate(packs):
            for h in range(2):
                own = pltpu.make_async_copy(pk.at[h], slot(a, gw, chip, h), lsem.at[a, h])
                own.start()
                owns.append(own)

        cv[me] = c_ref[...]
        for r in range(1, N_DEV):
            dx, dy, dc = (r >> 2) & 1, (r >> 1) & 1, r & 1
            cp = pltpu.make_async_remote_copy(
                src_ref=c_ref, dst_ref=cv.at[me], send_sem=ssem_c.at[r - 1], recv_sem=rsem_c.at[r - 1],
                device_id=(_flip(x, dx), _flip(y, dy), _flip(c, dc)), device_id_type=MESH)
            cp.start()
            sends.append(cp)
        for r in range(1, N_DEV):
            dx, dy, dc = (r >> 2) & 1, (r >> 1) & 1, r & 1
            src = 4 * _flip(x, dx) + 2 * _flip(y, dy) + _flip(c, dc)
            pltpu.make_async_remote_copy(
                src_ref=c_ref, dst_ref=cv.at[src], send_sem=ssem_c.at[r - 1], recv_sem=rsem_c.at[r - 1],
                device_id=(x, y, c), device_id_type=MESH).wait_recv()
        rows = lax.broadcasted_iota(jnp.int32, (N_DEV, D_MODEL), 0)
        call = jnp.zeros((N_DEV, D_MODEL), F32)
        for b in range(N_DEV):
            call = jnp.where(rows == b, jnp.broadcast_to(cv[b], (N_DEV, D_MODEL)), call)
        cg_ref[...] = call

        mg_ref[chip] = lax.dot_general(call, wada_ref[...], (((1,), (0,)), ((), ())),
                                       precision=lax.Precision.HIGHEST, preferred_element_type=F32)
        for j, (dx, dy) in enumerate(rel3):
            cp = pltpu.make_async_remote_copy(
                src_ref=mg_ref.at[chip], dst_ref=mg_ref.at[chip], send_sem=ssem_m.at[j], recv_sem=rsem_m.at[j],
                device_id=(_flip(x, dx), _flip(y, dy), c), device_id_type=MESH)
            cp.start()
            sends.append(cp)
        for j, (dx, dy) in enumerate(rel3):
            src_chip = 2 * _flip(x, dx) + _flip(y, dy)
            pltpu.make_async_remote_copy(
                src_ref=mg_ref.at[src_chip], dst_ref=mg_ref.at[src_chip], send_sem=ssem_m.at[j],
                recv_sem=rsem_m.at[j], device_id=(x, y, c), device_id_type=MESH).wait_recv()
        for j, (dx, dy) in enumerate(rel3):
            src_chip = 2 * _flip(x, dx) + _flip(y, dy)
            for a, (pk, gw) in enumerate(packs):
                pltpu.make_async_remote_copy(
                    src_ref=pk.at[c], dst_ref=slot(a, gw, src_chip, c), send_sem=ssem_w.at[j, a],
                    recv_sem=rsem_w.at[j, a], device_id=(x, y, c), device_id_type=MESH).wait_recv()
                cp = pltpu.make_async_remote_copy(
                    src_ref=slot(a, gw, src_chip, c), dst_ref=slot(a, gw, src_chip, c), send_sem=ssem_f.at[j, a],
                    recv_sem=rsem_f.at[j, a], device_id=(x, y, 1 - c), device_id_type=MESH)
                cp.start()
                sends.append(cp)
        for j, (dx, dy) in enumerate(rel3):
            src_chip = 2 * _flip(x, dx) + _flip(y, dy)
            for a, (pk, gw) in enumerate(packs):
                pltpu.make_async_remote_copy(
                    src_ref=pk.at[c], dst_ref=slot(a, gw, src_chip, 1 - c), send_sem=ssem_f.at[j, a],
                    recv_sem=rsem_f.at[j, a], device_id=(x, y, c), device_id_type=MESH).wait_recv()
        for cp in sends:
            cp.wait_send()
        for own in owns:
            own.wait()

    vmem = pl.BlockSpec(memory_space=pltpu.VMEM)
    return pl.pallas_call(
        body, name="gather_fwd",
        out_shape=(_sds((N_CHIPS, N_DEV, 768), F32), _sds((N_DEV, D_MODEL), F32),
                   _sds((2, N_CHIPS, IN_SHARD, HALF_D), BF16), _sds((N_CHIPS, 2, SMALL_HALF, LANES), BF16)),
        in_specs=[vmem, vmem, vmem, vmem], out_specs=(vmem, vmem, vmem, vmem),
        scratch_shapes=[
            pltpu.VMEM((N_DEV, 1, D_MODEL), F32),
            pltpu.SemaphoreType.DMA((N_DEV - 1,)), pltpu.SemaphoreType.DMA((N_DEV - 1,)),
            pltpu.SemaphoreType.DMA((3,)), pltpu.SemaphoreType.DMA((3,)),
            pltpu.SemaphoreType.DMA((3, 2)), pltpu.SemaphoreType.DMA((3, 2)),
            pltpu.SemaphoreType.DMA((3, 2)), pltpu.SemaphoreType.DMA((3, 2)),
            pltpu.SemaphoreType.DMA((2, 2)),
        ],
        compiler_params=_params(),
    )(c_row, w_ada_sh, pack_in, pack_small)


def _reduce_call(g_in, g_small, sv):
    def body(gi_ref, gs_ref, sv_ref, fi_ref, fs_ref, svg_ref, pair_i, pair_s, send_i, send_s, land_i, land_s,
             ssem_p, rsem_p, ssem_g, rsem_g, ssem_s, rsem_s, ssem_x, rsem_x):
        x, y, c = lax.axis_index("x"), lax.axis_index("y"), lax.axis_index("c")
        me = 4 * x + 2 * y + c
        chip = 2 * x + y
        rel3 = [(1, 0), (0, 1), (1, 1)]
        payloads = [(gi_ref, pair_i, send_i, land_i, fi_ref), (gs_ref, pair_s, send_s, land_s, fs_ref)]
        where = [lambda k, h: N_CHIPS * h + k, lambda k, h: 2 * k + h]
        copies = []

        for k in range(N_CHIPS):
            for a, (g, pair, _, _, _) in enumerate(payloads):
                cp = pltpu.make_async_remote_copy(
                    src_ref=g.at[where[a](k, 1 - c)], dst_ref=pair.at[k], send_sem=ssem_p.at[k, a],
                    recv_sem=rsem_p.at[k, a], device_id=(x, y, 1 - c), device_id_type=MESH)
                cp.start()
                copies.append(cp)

        for r in range(1, N_DEV):
            dx, dy, dc = (r >> 2) & 1, (r >> 1) & 1, r & 1
            cp = pltpu.make_async_remote_copy(
                src_ref=sv_ref, dst_ref=svg_ref.at[me], send_sem=ssem_s.at[r - 1], recv_sem=rsem_s.at[r - 1],
                device_id=(_flip(x, dx), _flip(y, dy), _flip(c, dc)), device_id_type=MESH)
            cp.start()
            copies.append(cp)
        svg_ref[me] = sv_ref[...]

        def pair_sum(k, store_in, store_small):
            for a, (g, pair, _, _, _) in enumerate(payloads):
                pltpu.make_async_remote_copy(
                    src_ref=g.at[where[a](k, c)], dst_ref=pair.at[k], send_sem=ssem_p.at[k, a],
                    recv_sem=rsem_p.at[k, a], device_id=(x, y, c), device_id_type=MESH).wait_recv()
            for qd in range(HALF_D // LANES):
                sl = slice(LANES * qd, LANES * qd + LANES)
                store_in(sl, gi_ref[where[0](k, c), :, sl].astype(F32) + pair_i[k, :, sl].astype(F32))

            def rows(i, carry):
                sl = pl.ds(pl.multiple_of(i * SMALL_SUM_ROWS, 16), SMALL_SUM_ROWS)
                store_small(sl, gs_ref[where[1](k, c), sl, :].astype(F32) + pair_s[k, sl, :].astype(F32))
                return carry

            lax.fori_loop(0, SMALL_HALF // SMALL_SUM_ROWS, rows, 0)

        for j, (dx, dy) in enumerate(rel3):
            tx, ty = _flip(x, dx), _flip(y, dy)

            def put_in(sl, val, j=j):
                send_i[j, :, sl] = val.astype(BF16)

            def put_small(sl, val, j=j):
                send_s[j, sl, :] = val.astype(BF16)

            pair_sum(2 * tx + ty, put_in, put_small)
            for a, (_, _, send, land, _) in enumerate(payloads):
                cp = pltpu.make_async_remote_copy(
                    src_ref=send.at[j], dst_ref=land.at[j], send_sem=ssem_g.at[j, a], recv_sem=rsem_g.at[j, a],
                    device_id=(tx, ty, c), device_id_type=MESH)
                cp.start()
                copies.append(cp)

        def own_in(sl, val):
            fi_ref[c, :, sl] = val

        def own_small(sl, val):
            fs_ref[c, sl, :] = val

        pair_sum(chip, own_in, own_small)
        for j in range(3):
            for a, (_, _, send, land, _) in enumerate(payloads):
                pltpu.make_async_remote_copy(
                    src_ref=send.at[j], dst_ref=land.at[j], send_sem=ssem_g.at[j, a], recv_sem=rsem_g.at[j, a],
                    device_id=(x, y, c), device_id_type=MESH).wait_recv()
            for qd in range(HALF_D // LANES):
                sl = slice(LANES * qd, LANES * qd + LANES)
                fi_ref[c, :, sl] += land_i[j, :, sl].astype(F32)

            def add_rows(i, carry, j=j):
                sl = pl.ds(pl.multiple_of(i * SMALL_SUM_ROWS, 16), SMALL_SUM_ROWS)
                fs_ref[c, sl, :] += land_s[j, sl, :].astype(F32)
                return carry

            lax.fori_loop(0, SMALL_HALF // SMALL_SUM_ROWS, add_rows, 0)

        for a, f in enumerate((fi_ref, fs_ref)):
            cp = pltpu.make_async_remote_copy(
                src_ref=f.at[c], dst_ref=f.at[c], send_sem=ssem_x.at[a], recv_sem=rsem_x.at[a],
                device_id=(x, y, 1 - c), device_id_type=MESH)
            cp.start()
            copies.append(cp)
        for a, f in enumerate((fi_ref, fs_ref)):
            pltpu.make_async_remote_copy(
                src_ref=f.at[c], dst_ref=f.at[1 - c], send_sem=ssem_x.at[a], recv_sem=rsem_x.at[a],
                device_id=(x, y, c), device_id_type=MESH).wait_recv()
        for r in range(1, N_DEV):
            dx, dy, dc = (r >> 2) & 1, (r >> 1) & 1, r & 1
            src = 4 * _flip(x, dx) + 2 * _flip(y, dy) + _flip(c, dc)
            pltpu.make_async_remote_copy(
                src_ref=sv_ref, dst_ref=svg_ref.at[src], send_sem=ssem_s.at[r - 1],
                recv_sem=rsem_s.at[r - 1], device_id=(x, y, c), device_id_type=MESH).wait_recv()
        for cp in copies:
            cp.wait_send()

    vmem = pl.BlockSpec(memory_space=pltpu.VMEM)
    return pl.pallas_call(
        body, name="grad_reduce",
        out_shape=(_sds((2, IN_SHARD, HALF_D), F32), _sds((2, SMALL_HALF, LANES), F32),
                   _sds((N_DEV, 8, SV_COLS), F32)),
        in_specs=[vmem, vmem, vmem], out_specs=(vmem, vmem, vmem),
        scratch_shapes=[
            pltpu.VMEM((N_CHIPS, IN_SHARD, HALF_D), BF16), pltpu.VMEM((N_CHIPS, SMALL_HALF, LANES), BF16),
            pltpu.VMEM((3, IN_SHARD, HALF_D), BF16), pltpu.VMEM((3, SMALL_HALF, LANES), BF16),
            pltpu.VMEM((3, IN_SHARD, HALF_D), BF16), pltpu.VMEM((3, SMALL_HALF, LANES), BF16),
            pltpu.SemaphoreType.DMA((N_CHIPS, 2)), pltpu.SemaphoreType.DMA((N_CHIPS, 2)),
            pltpu.SemaphoreType.DMA((3, 2)), pltpu.SemaphoreType.DMA((3, 2)),
            pltpu.SemaphoreType.DMA((N_DEV - 1,)), pltpu.SemaphoreType.DMA((N_DEV - 1,)),
            pltpu.SemaphoreType.DMA((2,)), pltpu.SemaphoreType.DMA((2,)),
        ],
        compiler_params=_params(),
    )(g_in, g_small, sv)


def _dwin_call(h, dproj):
    s_len = h.shape[0]
    tm = min(4 * ROW_TILE, s_len)
    nrow = s_len // tm
    nc = 4
    chunk = W_INT // nc

    def body(h_ref, dp_ref, dw_ref, acc):
        i = pl.program_id(1)

        @pl.when(i == 0)
        def _():
            acc[...] = jnp.zeros_like(acc)

        acc[...] += _dot_tn(dp_ref[...], h_ref[...])

        @pl.when(i == nrow - 1)
        def _():
            dw_ref[0] = acc[:, :HALF_D].astype(BF16)
            dw_ref[1] = acc[:, HALF_D:].astype(BF16)

    return pl.pallas_call(
        body, name="dwin", grid=(nc, nrow),
        out_shape=_sds((2, W_INT, HALF_D), BF16),
        in_specs=[pl.BlockSpec((tm, D_MODEL), lambda c, i: (i, 0)),
                  pl.BlockSpec((tm, chunk), lambda c, i: (i, c))],
        out_specs=pl.BlockSpec((2, chunk, HALF_D), lambda c, i: (0, c, 0)),
        scratch_shapes=[pltpu.VMEM((chunk, D_MODEL), F32)],
        compiler_params=_params(("parallel", "arbitrary")),
    )(h, dproj)


def _internal_weights(w_in_t, w_uq, w_ukv):
    krot_t = w_in_t[:, 2688:2720]
    krot_sw = krot_t.reshape(2, 2, 16, HALF_D)[:, ::-1].reshape(2, 32, HALF_D)
    w_int_t = jnp.concatenate([
        w_in_t[:, 0:2688], w_in_t[:, 2720:5280], jnp.tile(krot_t, (1, 4, 1)), jnp.tile(krot_sw, (1, 4, 1)),
        jnp.zeros((2, W_INT - O_END, HALF_D), w_in_t.dtype)], axis=1)
    uq = w_uq.reshape(Q_RANK, N_HEADS, 96)
    wp = uq[:, :, 64:].reshape(Q_RANK, 256)
    w_q = jnp.concatenate([uq[:, :, :64].reshape(Q_RANK, 512), wp, _swap_halves(wp, 32)], axis=1)
    ukv = w_ukv.reshape(KV_RANK, N_HEADS, 128)
    w_kv = jnp.concatenate([ukv[:, :, :64].reshape(KV_RANK, 512), ukv[:, :, 64:].reshape(KV_RANK, 512)], axis=1)
    return w_int_t, w_q, w_kv


def _true_weight_grads(dwi_t, dwq, dwkv):
    dkr = dwi_t[:, O_KR:O_KR + 128].astype(F32).reshape(2, 4, 32, HALF_D).sum(axis=1)
    dkr_sw = dwi_t[:, O_KR + 128:O_END].astype(F32).reshape(2, 4, 32, HALF_D).sum(axis=1)
    dkr_sw = dkr_sw.reshape(2, 2, 16, HALF_D)[:, ::-1].reshape(2, 32, HALF_D)
    dkrot_t = (dkr + dkr_sw).astype(dwi_t.dtype)
    g_in_t = jnp.concatenate([dwi_t[:, 0:O_MZ], dkrot_t, dwi_t[:, O_MZ:O_KR]], axis=1)
    dwp = dwq[:, 512:768] + _swap_halves(dwq[:, 768:1024], 32)
    g_uq = jnp.concatenate([dwq[:, :512].reshape(Q_RANK, N_HEADS, 64), dwp.reshape(Q_RANK, N_HEADS, 32)],
                           axis=2).reshape(Q_RANK, 768)
    g_ukv = jnp.concatenate([dwkv[:, :512].reshape(KV_RANK, N_HEADS, 64), dwkv[:, 512:].reshape(KV_RANK, N_HEADS, 64)],
                            axis=2).reshape(KV_RANK, 1024)
    return g_in_t, g_uq, g_ukv


def _swap_halves(w, group):
    r, n = w.shape
    return w.reshape(r, n // group, 2, group // 2)[:, :, ::-1, :].reshape(r, n)


def _pack_shards(parts):
    return jnp.concatenate([p.reshape(-1, LANES) for p in parts], axis=0)


def _unpack_small(gw):
    offs = [0]
    for r in SMALL_ROWS:
        offs.append(offs[-1] + r)

    def cols(i, rows, shard_cols):
        blk = gw[:, offs[i]:offs[i + 1]].reshape(N_CHIPS, rows, shard_cols)
        return blk.transpose(1, 0, 2).reshape(rows, N_CHIPS * shard_cols)

    return (cols(0, Q_RANK, 192), cols(1, KV_RANK, 256), cols(2, 512, 256), cols(3, 512, 256),
            gw[:, offs[4]:offs[5]].reshape(D_MODEL, D_MODEL))


def _chip_major(g, shard_cols):
    r = g.shape[0]
    return g.reshape(r, N_CHIPS, shard_cols).transpose(1, 0, 2).reshape(N_CHIPS, -1, LANES)


def kernel(x, c, positions, w_ada, b_ada, norm_gain, w_in, q_norm_gain, w_uq, kv_norm_gain, w_ukv, w_branch_a, w_branch_b, w_out, final_norm_gain, loss_target, m_w_ada, m_b_ada, m_norm_gain, m_w_in, m_q_norm_gain, m_w_uq, m_kv_norm_gain, m_w_ukv, m_w_branch_a, m_w_branch_b, m_w_out, m_final_norm_gain, v_w_ada, v_b_ada, v_norm_gain, v_w_in, v_q_norm_gain, v_w_uq, v_kv_norm_gain, v_w_ukv, v_w_branch_a, v_w_branch_b, v_w_out, v_final_norm_gain):
    ix, iy, ic = lax.axis_index("x"), lax.axis_index("y"), lax.axis_index("c")
    me = 4 * ix + 2 * iy + ic
    chip = 2 * ix + iy
    xs = x[0]
    tgt = loss_target[0]
    s_len = xs.shape[0]

    w_in_t = jnp.swapaxes(w_in[0], 0, 1)
    w_in_tb = w_in_t.astype(BF16)
    pack_in = jnp.stack([w_in_tb[:, :HALF_D], w_in_tb[:, HALF_D:]], axis=0)
    small_shards = (w_uq[0], w_ukv[0], w_branch_a[0], w_branch_b[0], w_out[0])
    pack_small = _pack_shards([s.astype(BF16) for s in small_shards]).reshape(2, SMALL_HALF, LANES)
    mg, call, gw_in, gw_small = _gather_call(c, w_ada[0], pack_in, pack_small)
    mod = mg.transpose(1, 0, 2).reshape(N_DEV, 3 * D_MODEL) + b_ada
    mod_me = lax.dynamic_slice_in_dim(mod, me, 1, axis=0)
    shift, scale, gate = mod_me[:, :D_MODEL], mod_me[:, D_MODEL:2 * D_MODEL], mod_me[:, 2 * D_MODEL:]

    f_in_t = gw_in.reshape(2, IN_WIDTH, HALF_D)
    f_uq, f_ukv, f_a, f_b, f_out = _unpack_small(gw_small.reshape(N_CHIPS, SMALL_TOTAL, LANES))
    w_int_t, w_q, w_kv = _internal_weights(f_in_t, f_uq, f_ukv)

    inv_freq = ROPE_BASE ** (-jnp.arange(0, ROPE_DIM, 2, dtype=F32) / ROPE_DIM)
    ang = positions[0].astype(F32)[:, None] * inv_freq
    cs, sn = jnp.cos(ang), jnp.sin(ang)
    cos256 = jnp.tile(jnp.concatenate([cs, cs], axis=1), (1, 8))
    sin256 = jnp.tile(jnp.concatenate([-sn, sn], axis=1), (1, 8))

    (h, sq, sk, sv, sz, cq, ckv, mz, ga, gb, kpt, qn, qp, kn, vv) = _inproj_call(
        xs, shift, scale, norm_gain, w_int_t, w_q, w_kv, q_norm_gain, kv_norm_gain, cos256, sin256)
    oa, lt, first = _sb_fwd_call(sq, sk, sv)
    ob, lse = _mla_fwd_call(qn, qp, kn, kpt, vv)

    gf = final_norm_gain.reshape(1, D_MODEL)
    (dx2, doa, dob, dsz, dmz, dga, dgb, dwo, dwa, dwb, dgf, dgate, loss_p) = _post_call(
        xs, tgt, oa, ob, sz, mz, ga, gb, gate, gf, f_a, f_b, f_out)

    dsq, dsk_t, dsv_t = _sb_bwd_call(first[:, :, 0, 0].reshape(-1), sq, sk, sv, doa, lt)
    dqn, dqp, dkn_t, dkpt_t, dvv_t = _mla_bwd_call(qn, qp, kn, kpt, vv, ob, dob, lse)

    dproj, dwq, dwkv, dqg, dkvg = _bwdprep_call(
        dsq, dsk_t, dsv_t, dsz, dqn, dqp, dkn_t, dvv_t, dkpt_t, dmz, dga, dgb, cq, ckv, cos256, sin256,
        q_norm_gain, kv_norm_gain, w_q, w_kv)
    grad_x, dshift, dscale, dg1 = _dh_call(dproj, w_int_t, xs, dx2, scale, norm_gain)
    dwi_t = _dwin_call(h, dproj)
    g_in_t, g_uq, g_ukv = _true_weight_grads(dwi_t, dwq, dwkv)

    g_in_pieces = g_in_t.reshape(N_DEV, IN_SHARD, HALF_D)
    g_small = jnp.concatenate([
        _chip_major(g_uq, 192).astype(BF16), _chip_major(g_ukv, 256).astype(BF16),
        dwa.reshape(N_CHIPS, -1, LANES), dwb.reshape(N_CHIPS, -1, LANES),
        dwo.reshape(N_CHIPS, -1, LANES)], axis=1).reshape(N_DEV, SMALL_HALF, LANES)
    small = jnp.concatenate([
        dshift, dscale, dgate, dg1, dqg, dkvg, dgf, loss_p,
        jnp.zeros((1, 8 * SV_COLS - 5888), F32)], axis=1).reshape(8, SV_COLS)
    full_in, full_small, svg = _reduce_call(g_in_pieces, g_small, small)
    full = full_small.reshape(SMALL_TOTAL, LANES)
    offs = [0]
    for r in SMALL_ROWS:
        offs.append(offs[-1] + r)
    gs_uq = full[offs[0]:offs[1]].reshape(Q_RANK, 192)
    gs_ukv = full[offs[1]:offs[2]].reshape(KV_RANK, 256)
    gs_a = full[offs[2]:offs[3]].reshape(512, 256)
    gs_b = full[offs[3]:offs[4]].reshape(512, 256)
    gs_out = full[offs[4]:offs[5]].reshape(256, D_MODEL)

    svm = svg.reshape(N_DEV, 8 * SV_COLS)
    dmod_sh = lax.dynamic_slice_in_dim(svm[:, :3 * D_MODEL], chip * 768, 768, axis=1)
    tot, gs_ada = _small_call(svm, call.T, dmod_sh)
    g_bada = tot[:, 0:3072]
    g_g1 = tot[:, 3072:4096]
    g_qg = tot[:, 4096:4480]
    g_kvg = tot[:, 4480:4736]
    g_gf = tot[:, 4736:5760]
    loss = tot[0, 5760]

    names = ["w_ada", "b_ada", "norm_gain", "w_in", "q_norm_gain", "w_uq", "kv_norm_gain", "w_ukv",
             "w_branch_a", "w_branch_b", "w_out", "final_norm_gain"]
    ws = [w_ada[0], b_ada, norm_gain, w_in_t, q_norm_gain, w_uq[0], kv_norm_gain, w_ukv[0],
          w_branch_a[0], w_branch_b[0], w_out[0], final_norm_gain.reshape(1, D_MODEL)]
    gs = [gs_ada, g_bada, g_g1, full_in, g_qg, gs_uq, g_kvg, gs_ukv, gs_a, gs_b, gs_out, g_gf]
    ms = [m_w_ada[0], m_b_ada, m_norm_gain, jnp.swapaxes(m_w_in[0], 0, 1), m_q_norm_gain, m_w_uq[0],
          m_kv_norm_gain, m_w_ukv[0], m_w_branch_a[0], m_w_branch_b[0], m_w_out[0],
          m_final_norm_gain.reshape(1, D_MODEL)]
    vs = [v_w_ada[0], v_b_ada, v_norm_gain, jnp.swapaxes(v_w_in[0], 0, 1), v_q_norm_gain, v_w_uq[0],
          v_kv_norm_gain, v_w_ukv[0], v_w_branch_a[0], v_w_branch_b[0], v_w_out[0],
          v_final_norm_gain.reshape(1, D_MODEL)]
    refs = [w_ada, b_ada, norm_gain, w_in, q_norm_gain, w_uq, kv_norm_gain, w_ukv,
            w_branch_a, w_branch_b, w_out, final_norm_gain]
    grads, deltas, new_ms, new_vs = [], [], [], []
    for n, w_, g_, m_, v_, ref in zip(names, ws, gs, ms, vs, refs):
        outs = _adamw_call(n, w_, g_, m_, v_)
        if n == "w_in":
            outs = tuple(jnp.swapaxes(o_, 0, 1) for o_ in outs)
        for lst, o_ in zip((grads, deltas, new_ms, new_vs), outs):
            lst.append(o_.reshape(ref.shape))

    return (loss, grad_x.reshape(x.shape), *grads, *deltas, *new_ms, *new_vs)
```

```python
import math

import jax
import jax.numpy as jnp
from jax import lax
from jax.experimental import pallas as pl
from jax.experimental.pallas import tpu as pltpu

F32 = jnp.float32
BF16 = jnp.bfloat16

D_MODEL = 1024
SB_WIDTH = 512
MLA_WIDTH = 512
Q_RANK = 384
KV_RANK = 256
ROPE_DIM = 32
N_HEADS = 8
IN_WIDTH = 5280
EPS = 1e-6
ROPE_BASE = 10000.0
MLA_SCALE = 1.0 / math.sqrt(96.0)
SB_SCALE = 0.125
LOG2E = 1.4426950408889634

ADAM_LR = 0.001
ADAM_B1 = 0.9
ADAM_B2 = 0.999
ADAM_EPS = 1e-08
ADAM_WD = 0.01
ADAM_STEP = 10

O_SQ, O_SK, O_SV, O_SZ, O_CQ, O_CKV, O_MZ, O_GA, O_GB, O_KR, O_END = (
    0, 512, 1024, 1536, 2048, 2432, 2688, 3200, 4224, 5248, 5504)
W_INT = 5632

N_CHIPS = 4
N_DEV = 8
LANES = 128
SV_COLS = 768

ROW_TILE = 256
ATT_TILE = 256
ATT_Q_TILES = 2
FWD_Q_TILES = 4
SB_Q_TILES = 1
MLA_KEY_TILE = 512
VMEM_LIMIT = 56 * 1024 * 1024

MESH = pl.DeviceIdType.MESH


def _dot(a, b):
    return lax.dot_general(a, b, (((1,), (0,)), ((), ())), preferred_element_type=F32)


def _dot_nt(a, b):
    return lax.dot_general(a, b, (((1,), (1,)), ((), ())), preferred_element_type=F32)


def _dot_tn(a, b):
    return lax.dot_general(a, b, (((0,), (0,)), ((), ())), preferred_element_type=F32)


def _sigmoid(z):
    return 1.0 / (1.0 + jnp.exp2(z * (-LOG2E)))


def _params(sem=None):
    if sem is None:
        return pltpu.CompilerParams(vmem_limit_bytes=VMEM_LIMIT)
    return pltpu.CompilerParams(dimension_semantics=sem, vmem_limit_bytes=VMEM_LIMIT)


def _rows(tm, n):
    return pl.BlockSpec((tm, n), lambda i: (i, 0))


def _cols(n, tm):
    return pl.BlockSpec((n, tm), lambda i: (0, i))


def _whole(shape):
    nd = len(shape)
    return pl.BlockSpec(shape, lambda i: (0,) * nd)


def _sds(shape, dtype):
    return jax.ShapeDtypeStruct(shape, dtype)


def _flip(v, d):
    return 1 - v if d else v


def _inproj_call(x, shift, scale, g1, w_int, w_q, w_kv, qg, kvg, cos128, sin128):
    s_len = x.shape[0]
    tm = min(ROW_TILE, s_len)

    def body(x_ref, sh_ref, sc_ref, g1_ref, w_ref, wq_ref, wkv_ref, qg_ref, kvg_ref, cos_ref, sin_ref,
             h_ref, sq_ref, sk_ref, sv_ref, sz_ref, cq_ref, ckv_ref, mz_ref, ga_ref, gb_ref, kpt_ref,
             qn_ref, qp_ref, kn_ref, vv_ref):
        xt = x_ref[...]
        r = lax.rsqrt(jnp.mean(xt * xt, axis=-1, keepdims=True) + EPS)
        h = (xt * r * g1_ref[...]) * (1.0 + sc_ref[...]) + sh_ref[...]
        hb = h.astype(BF16)
        h_ref[...] = hb

        def seg(a, b):
            return _dot_nt(hb[:, :HALF_D], w_ref[0, a:b, :]) + _dot_nt(hb[:, HALF_D:], w_ref[1, a:b, :])

        sq_ref[...] = (seg(O_SQ, O_SK) * SB_SCALE).astype(BF16)
        sk_ref[...] = seg(O_SK, O_SV).astype(BF16)
        sv_ref[...] = seg(O_SV, O_SZ).astype(BF16)
        sz_ref[...] = seg(O_SZ, O_CQ)
        mz_ref[...] = seg(O_MZ, O_GA)
        ga_ref[...] = seg(O_GA, O_GB)
        gb_ref[...] = seg(O_GB, O_KR)
        cos = cos_ref[...]
        sin = sin_ref[...]
        kr = seg(O_KR, O_END)
        kpt_ref[...] = (kr[:, :128] * cos + kr[:, 128:] * sin).astype(BF16)

        cq = seg(O_CQ, O_CKV)
        cq_ref[...] = cq
        rq = lax.rsqrt(jnp.mean(cq * cq, axis=-1, keepdims=True) + EPS)
        cqn = (cq * rq * qg_ref[...]).astype(BF16)
        qa = _dot(cqn, wq_ref[...])
        qn_ref[...] = qa[:, :512].astype(BF16)
        qp_ref[...] = (qa[:, 512:768] * jnp.tile(cos, (1, 2)) + qa[:, 768:] * jnp.tile(sin, (1, 2))).astype(BF16)

        ckv = seg(O_CKV, O_MZ)
        ckv_ref[...] = ckv
        rk = lax.rsqrt(jnp.mean(ckv * ckv, axis=-1, keepdims=True) + EPS)
        ckvn = (ckv * rk * kvg_ref[...]).astype(BF16)
        kva = _dot(ckvn, wkv_ref[...])
        kn_ref[...] = kva[:, :512].astype(BF16)
        vv_ref[...] = kva[:, 512:].astype(BF16)

    outs = [
        (D_MODEL, BF16), (512, BF16), (512, BF16), (512, BF16), (512, F32), (Q_RANK, F32), (KV_RANK, F32),
        (512, F32), (D_MODEL, F32), (D_MODEL, F32), (128, BF16), (512, BF16), (256, BF16), (512, BF16), (512, BF16),
    ]
    return pl.pallas_call(
        body, name="inproj", grid=(s_len // tm,),
        out_shape=tuple(_sds((s_len, n), dt) for n, dt in outs),
        in_specs=[_rows(tm, D_MODEL), _whole((1, D_MODEL)), _whole((1, D_MODEL)), _whole((1, D_MODEL)),
                  _whole((2, W_INT, HALF_D)), _whole((Q_RANK, 1024)), _whole((KV_RANK, 1024)),
                  _whole((1, Q_RANK)), _whole((1, KV_RANK)), _rows(tm, LANES), _rows(tm, LANES)],
        out_specs=tuple(_rows(tm, n) for n, _ in outs),
        compiler_params=_params(("parallel",)),
    )(x, shift, scale, g1, w_int, w_q, w_kv, qg, kvg, cos128, sin128)


Z_CLAMP = 80.0 * LOG2E
RUN_CUTOFF = 110.0 * LOG2E


def _softplus_clamped(z):
    zc = jnp.minimum(z * LOG2E, Z_CLAMP)
    return zc, jnp.log2(1.0 + jnp.exp2(zc))


def _tri_sum(a, tri):
    return _dot(a.astype(BF16), tri)


def _sb_fwd_call(q, k, v):
    s_len = q.shape[0]
    tk = min(ATT_TILE, s_len)
    tq = min(SB_Q_TILES * ATT_TILE, s_len)
    r = tq // tk
    nq = s_len // tq

    def body(q_ref, k_ref, v_ref, o_ref, lt_ref, first_ref):
        i = pl.program_id(1)
        q2 = q_ref[...]
        lane = lax.broadcasted_iota(jnp.int32, (1, 256), 1)
        krow = lax.broadcasted_iota(jnp.int32, (tk, tk), 0)
        kcol = lax.broadcasted_iota(jnp.int32, (tk, tk), 1)
        row = lax.broadcasted_iota(jnp.int32, (tq, tk), 0)
        col = lax.broadcasted_iota(jnp.int32, (tq, tk), 1)
        later = (krow > kcol).astype(BF16)
        valids = [col + u * tk < row for u in range(r)]
        hms = [(lane // 64) == hh for hh in range(4)]
        qms = [jnp.where(hm, q2, jnp.zeros_like(q2)) for hm in hms]

        def block(j, carry, valid):
            runs, acc = list(carry[:4]), carry[4]
            off = pl.multiple_of(j * tk, tk)
            kb = k_ref[pl.ds(off, tk), :]
            vb = v_ref[pl.ds(off, tk), :]
            ws = []
            for hh in range(4):
                zc, sp = _softplus_clamped(_dot_nt(qms[hh], kb))
                lm = jnp.where(valid, sp, 0.0) if valid is not None else sp
                suf = _tri_sum(lm, later)
                w = jnp.exp2(zc - sp - suf - runs[hh])
                if valid is not None:
                    w = jnp.where(valid, w, 0.0)
                ws.append(w.astype(BF16))
                runs[hh] = runs[hh] + jnp.sum(lm, axis=1, keepdims=True)
            vstack = jnp.concatenate([jnp.where(hm, vb, jnp.zeros_like(vb)) for hm in hms], axis=0)
            acc = acc + _dot(jnp.concatenate(ws, axis=1), vstack)
            return (*runs, acc)

        zero = jnp.zeros((tq, 1), F32)
        carry = (zero, zero, zero, zero, jnp.zeros((tq, 256), F32))
        for u in reversed(range(r)):
            carry = block(i * r + u, carry, valids[u])

        def least_run(runs):
            return jnp.min(jnp.minimum(jnp.minimum(runs[0], runs[1]), jnp.minimum(runs[2], runs[3])))

        n_full = i * r

        def unfinished(state):
            return jnp.logical_and(state[0] < n_full, state[1] <= RUN_CUTOFF)

        def visit(state):
            cr = block(n_full - 1 - state[0], state[2:], None)
            return (state[0] + 1, least_run(cr[:4]), *cr)

        state = lax.while_loop(unfinished, visit, (jnp.int32(0), least_run(carry[:4]), *carry))
        carry = state[2:]
        first_ref[...] = jnp.full(first_ref.shape, n_full - state[0], jnp.int32)
        for hh in range(4):
            lt_ref[0, :, hh:hh + 1] = carry[hh]
        o_ref[...] = carry[4]

    return pl.pallas_call(
        body, name="sb_fwd", grid=(2, nq),
        out_shape=(_sds((s_len, SB_WIDTH), F32), _sds((2, s_len, 4), F32), _sds((2, nq, 8, 128), jnp.int32)),
        in_specs=[pl.BlockSpec((tq, 256), lambda g, i: (i, g)),
                  pl.BlockSpec((s_len, 256), lambda g, i: (0, g)),
                  pl.BlockSpec((s_len, 256), lambda g, i: (0, g))],
        out_specs=(pl.BlockSpec((tq, 256), lambda g, i: (i, g)),
                   pl.BlockSpec((1, tq, 4), lambda g, i: (g, i, 0)),
                   pl.BlockSpec((1, 1, 8, 128), lambda g, i: (g, i, 0, 0))),
        compiler_params=_params(("parallel", "parallel")),
    )(q, k, v)


def _sb_bwd_call(first, q, k, v, do, lt):
    s_len = q.shape[0]
    tk = min(ATT_TILE, s_len)
    tq = min(SB_Q_TILES * ATT_TILE, s_len)
    r = tq // tk
    nq = s_len // tq
    nq_fwd = first.shape[0] // 2
    per_fwd = nq // nq_fwd

    def body(first_ref, q_ref, k_ref, v_ref, do_ref, lt_ref, dq_ref, dk_ref, dv_ref):
        g = pl.program_id(0)
        i = pl.program_id(1)

        @pl.when(i == 0)
        def _():
            dk_ref[...] = jnp.zeros_like(dk_ref)
            dv_ref[...] = jnp.zeros_like(dv_ref)

        q2 = q_ref[...]
        do2 = do_ref[...].astype(BF16)
        lane = lax.broadcasted_iota(jnp.int32, (1, 256), 1)
        krow = lax.broadcasted_iota(jnp.int32, (tk, tk), 0)
        kcol = lax.broadcasted_iota(jnp.int32, (tk, tk), 1)
        row = lax.broadcasted_iota(jnp.int32, (tq, tk), 0)
        col = lax.broadcasted_iota(jnp.int32, (tq, tk), 1)
        earlier = (krow < kcol).astype(BF16)
        later = (krow > kcol).astype(BF16)
        valids = [col + u * tk < row for u in range(r)]
        hms = [(lane // 64) == hh for hh in range(4)]
        qms = [jnp.where(hm, q2, jnp.zeros_like(q2)) for hm in hms]
        doms = [jnp.where(hm, do2, jnp.zeros_like(do2)) for hm in hms]
        ltots = [lt_ref[0, :, hh:hh + 1] for hh in range(4)]
        q2t = jnp.transpose(q2.astype(F32))
        do2t = jnp.transpose(do_ref[...])
        subl = lax.broadcasted_iota(jnp.int32, (256, 1), 0)
        qtstack = jnp.concatenate(
            [jnp.where((subl // 64) == hh, q2t, 0.0).astype(BF16) for hh in range(4)], axis=1)
        dotstack = jnp.concatenate(
            [jnp.where((subl // 64) == hh, do2t, 0.0).astype(BF16) for hh in range(4)], axis=1)

        def block(j, carry, valid):
            lpre, ppre, dq = list(carry[0:4]), list(carry[4:8]), carry[8]
            off = pl.multiple_of(j * tk, tk)
            kb = k_ref[pl.ds(off, tk), :]
            vb = v_ref[pl.ds(off, tk), :]
            dzs, avs = [], []
            for hh in range(4):
                zc, sp = _softplus_clamped(_dot_nt(qms[hh], kb))
                lsig = zc - sp
                lm = jnp.where(valid, sp, 0.0) if valid is not None else sp
                rowsum = jnp.sum(lm, axis=1, keepdims=True)
                between = _tri_sum(lm, later) + ((ltots[hh] - lpre[hh]) - rowsum)
                a = jnp.exp2(lsig - between)
                if valid is not None:
                    a = jnp.where(valid, a, 0.0)
                p = a * _dot_nt(doms[hh], vb)
                pbefore = ppre[hh] + _tri_sum(p, earlier)
                dz = p - jnp.exp2(lsig) * (p + pbefore)
                if valid is not None:
                    dz = jnp.where(valid, dz, 0.0)
                dzs.append(dz.astype(BF16))
                avs.append(a.astype(BF16))
                lpre[hh] = lpre[hh] + rowsum
                ppre[hh] = ppre[hh] + jnp.sum(p, axis=1, keepdims=True)
            kstack = jnp.concatenate([jnp.where(hm, kb, jnp.zeros_like(kb)) for hm in hms], axis=0)
            dq = dq + _dot(jnp.concatenate(dzs, axis=1), kstack)
            dk_ref[:, pl.ds(off, tk)] += _dot(qtstack, jnp.concatenate(dzs, axis=0))
            dv_ref[:, pl.ds(off, tk)] += _dot(dotstack, jnp.concatenate(avs, axis=0))
            return (*lpre, *ppre, dq)

        zero = jnp.zeros((tq, 1), F32)
        start = jnp.minimum(first_ref[g * nq_fwd + i // per_fwd], i * r)
        carry = lax.fori_loop(start, i * r, lambda j, cr: block(j, cr, None),
                              (zero,) * 8 + (jnp.zeros((tq, 256), F32),))
        for u in range(r):
            carry = block(i * r + u, carry, valids[u])
        dq_ref[...] = carry[8].astype(BF16)

    return pl.pallas_call(
        body, name="sb_bwd",
        out_shape=(_sds((s_len, SB_WIDTH), BF16), _sds((SB_WIDTH, s_len), F32), _sds((SB_WIDTH, s_len), F32)),
        grid_spec=pltpu.PrefetchScalarGridSpec(
            num_scalar_prefetch=1, grid=(2, nq),
            in_specs=[pl.BlockSpec((tq, 256), lambda g, i, f: (i, g)),
                      pl.BlockSpec((s_len, 256), lambda g, i, f: (0, g)),
                      pl.BlockSpec((s_len, 256), lambda g, i, f: (0, g)),
                      pl.BlockSpec((tq, 256), lambda g, i, f: (i, g)),
                      pl.BlockSpec((1, tq, 4), lambda g, i, f: (g, i, 0))],
            out_specs=(pl.BlockSpec((tq, 256), lambda g, i, f: (i, g)),
                       pl.BlockSpec((256, s_len), lambda g, i, f: (g, 0)),
                       pl.BlockSpec((256, s_len), lambda g, i, f: (g, 0)))),
        compiler_params=_params(("parallel", "arbitrary")),
    )(first, q, k, v, do, lt)


def _mla_fwd_call(qn, qp, kn, kpt, v):
    s_len = qn.shape[0]
    tk = min(MLA_KEY_TILE, s_len)
    tq = min(FWD_Q_TILES * ATT_TILE, s_len)
    r = tq // tk
    nq = s_len // tq

    def body(qn_ref, qp_ref, kn_ref, kpt_ref, v_ref, o_ref, lse_ref):
        i = pl.program_id(1)
        qn2 = qn_ref[...]
        qp2 = qp_ref[...]
        lane256 = lax.broadcasted_iota(jnp.int32, (1, 256), 1)
        lane128 = lax.broadcasted_iota(jnp.int32, (1, 128), 1)
        krow = lax.broadcasted_iota(jnp.int32, (tk, tk), 0)
        kcol = lax.broadcasted_iota(jnp.int32, (tk, tk), 1)
        row = lax.broadcasted_iota(jnp.int32, (tq, tk), 0)
        col = lax.broadcasted_iota(jnp.int32, (tq, tk), 1)
        valids = [col + u * tk <= row for u in range(r)]
        m64s = [(lane256 // 64) == hh for hh in range(4)]
        half = [(lane128 // 64) == u for u in range(2)]
        m32s = [(lane128 // 32) == hh for hh in range(4)]
        qcs = []
        for hh in range(4):
            qpair = qn2[:, 128 * (hh // 2):128 * (hh // 2) + 128]
            qcs.append(jnp.concatenate([jnp.where(half[hh % 2], qpair, jnp.zeros_like(qpair)),
                                        jnp.where(m32s[hh], qp2, jnp.zeros_like(qp2))], axis=1))

        def by_head(vals):
            return jnp.where(m64s[0], vals[0], jnp.where(m64s[1], vals[1], jnp.where(m64s[2], vals[2], vals[3])))

        def block(j, carry, valid):
            ms, ls, acc = list(carry[0:4]), list(carry[4:8]), carry[8]
            off = pl.multiple_of(j * tk, tk)
            knb = kn_ref[pl.ds(off, tk), :]
            kpb = kpt_ref[pl.ds(off, tk), :]
            vb = v_ref[pl.ds(off, tk), :]
            kcs = [jnp.concatenate([knb[:, 128 * pp:128 * pp + 128], kpb], axis=1) for pp in range(2)]
            ps, alphas = [], []
            for hh in range(4):
                s = _dot_nt(qcs[hh], kcs[hh // 2]) * (MLA_SCALE * LOG2E)
                if valid is not None:
                    s = jnp.where(valid, s, -1e30)
                mn = jnp.maximum(ms[hh], jnp.max(s, axis=1, keepdims=True))
                p = jnp.exp2(s - mn)
                alpha = jnp.exp2(ms[hh] - mn)
                ls[hh] = alpha * ls[hh] + jnp.sum(p, axis=1, keepdims=True)
                ms[hh] = mn
                ps.append(p.astype(BF16))
                alphas.append(alpha)
            pvs = []
            for pp in range(2):
                vpair = vb[:, 128 * pp:128 * pp + 128]
                vstack = jnp.concatenate([jnp.where(hf, vpair, jnp.zeros_like(vpair)) for hf in half], axis=0)
                pvs.append(_dot(jnp.concatenate(ps[2 * pp:2 * pp + 2], axis=1), vstack))
            acc = by_head(alphas) * acc + jnp.concatenate(pvs, axis=1)
            return (*ms, *ls, acc)

        neg = jnp.full((tq, 1), -1e30, F32)
        zero = jnp.zeros((tq, 1), F32)
        carry = lax.fori_loop(0, i * r, lambda j, cr: block(j, cr, None),
                              (neg,) * 4 + (zero,) * 4 + (jnp.zeros((tq, 256), F32),))
        for u in range(r):
            carry = block(i * r + u, carry, valids[u])
        o_ref[...] = carry[8] / by_head(list(carry[4:8]))
        for hh in range(4):
            lse_ref[0, :, hh:hh + 1] = (carry[hh] + jnp.log2(carry[4 + hh])) * (1.0 / LOG2E)

    return pl.pallas_call(
        body, name="mla_fwd", grid=(2, nq),
        out_shape=(_sds((s_len, MLA_WIDTH), F32), _sds((2, s_len, 4), F32)),
        in_specs=[pl.BlockSpec((tq, 256), lambda g, i: (i, g)),
                  pl.BlockSpec((tq, 128), lambda g, i: (i, g)),
                  pl.BlockSpec((s_len, 256), lambda g, i: (0, g)),
                  pl.BlockSpec((s_len, 128), lambda g, i: (0, 0)),
                  pl.BlockSpec((s_len, 256), lambda g, i: (0, g))],
        out_specs=(pl.BlockSpec((tq, 256), lambda g, i: (i, g)),
                   pl.BlockSpec((1, tq, 4), lambda g, i: (g, i, 0))),
        compiler_params=_params(("parallel", "parallel")),
    )(qn, qp, kn, kpt, v)


def _mla_bwd_call(qn, qp, kn, kpt, v, o, do, lse):
    s_len = qn.shape[0]
    tk = min(MLA_KEY_TILE, s_len)
    tq = min(ATT_Q_TILES * ATT_TILE, s_len)
    r = tq // tk
    nq = s_len // tq

    def body(qn_ref, qp_ref, kn_ref, kpt_ref, v_ref, o_ref, do_ref, lse_ref,
             dqn_ref, dqp_ref, dkn_ref, dkpt_ref, dv_ref):
        g = pl.program_id(0)
        i = pl.program_id(1)

        @pl.when(i == 0)
        def _():
            dkn_ref[...] = jnp.zeros_like(dkn_ref)
            dv_ref[...] = jnp.zeros_like(dv_ref)

        @pl.when((i == 0) & (g == 0))
        def _():
            dkpt_ref[...] = jnp.zeros_like(dkpt_ref)

        qn2 = qn_ref[...]
        qp2 = qp_ref[...]
        dof = do_ref[...]
        dob = dof.astype(BF16)
        prod = dof * o_ref[...]
        lane256 = lax.broadcasted_iota(jnp.int32, (1, 256), 1)
        lane128 = lax.broadcasted_iota(jnp.int32, (1, 128), 1)
        krow = lax.broadcasted_iota(jnp.int32, (tk, tk), 0)
        kcol = lax.broadcasted_iota(jnp.int32, (tk, tk), 1)
        row = lax.broadcasted_iota(jnp.int32, (tq, tk), 0)
        col = lax.broadcasted_iota(jnp.int32, (tq, tk), 1)
        valids = [col + u * tk <= row for u in range(r)]
        m64s = [(lane256 // 64) == hh for hh in range(4)]
        half = [(lane128 // 64) == u for u in range(2)]
        m32s = [(lane128 // 32) == hh for hh in range(4)]
        qcs, doms = [], []
        for hh in range(4):
            sl = slice(128 * (hh // 2), 128 * (hh // 2) + 128)
            qpair = qn2[:, sl]
            dpair = dob[:, sl]
            qcs.append(jnp.concatenate([jnp.where(half[hh % 2], qpair, jnp.zeros_like(qpair)),
                                        jnp.where(m32s[hh], qp2, jnp.zeros_like(qp2))], axis=1))
            doms.append(jnp.where(half[hh % 2], dpair, jnp.zeros_like(dpair)))
        dsums = [jnp.sum(jnp.where(m64, prod, 0.0), axis=1, keepdims=True) * MLA_SCALE for m64 in m64s]
        lses = [lse_ref[0, :, hh:hh + 1] * LOG2E for hh in range(4)]
        qn2t = jnp.transpose(qn2.astype(F32))
        qp2t = jnp.transpose(qp2.astype(F32))
        do2t = jnp.transpose(dof)
        sub128 = lax.broadcasted_iota(jnp.int32, (128, 1), 0)
        qtstacks, dotstacks = [], []
        for pp in range(2):
            qts, dts = [], []
            for u in range(2):
                hh = 2 * pp + u
                qts.append(jnp.concatenate(
                    [jnp.where((sub128 // 64) == u, qn2t[128 * pp:128 * pp + 128, :], 0.0),
                     jnp.where((sub128 // 32) == hh, qp2t, 0.0)], axis=0).astype(BF16))
                dts.append(jnp.where((sub128 // 64) == u, do2t[128 * pp:128 * pp + 128, :], 0.0).astype(BF16))
            qtstacks.append(jnp.concatenate(qts, axis=1))
            dotstacks.append(jnp.concatenate(dts, axis=1))

        def block(j, carry, valid):
            dqn, dqp = carry
            off = pl.multiple_of(j * tk, tk)
            knb = kn_ref[pl.ds(off, tk), :]
            kpb = kpt_ref[pl.ds(off, tk), :]
            vb = v_ref[pl.ds(off, tk), :]
            dqn_parts = []
            dkp = None
            for pp in range(2):
                sl = slice(128 * pp, 128 * pp + 128)
                knp = knb[:, sl]
                vpair = vb[:, sl]
                kc = jnp.concatenate([knp, kpb], axis=1)
                dss, pbs, kcms = [], [], []
                for u in range(2):
                    hh = 2 * pp + u
                    s = _dot_nt(qcs[hh], kc) * (MLA_SCALE * LOG2E)
                    if valid is not None:
                        s = jnp.where(valid, s, -1e30)
                    p = jnp.exp2(s - lses[hh])
                    ds = p * (_dot_nt(doms[hh], vpair) * MLA_SCALE - dsums[hh])
                    dss.append(ds.astype(BF16))
                    pbs.append(p.astype(BF16))
                    kcms.append(jnp.concatenate([jnp.where(half[u], knp, jnp.zeros_like(knp)),
                                                 jnp.where(m32s[hh], kpb, jnp.zeros_like(kpb))], axis=1))
                dqc = _dot(jnp.concatenate(dss, axis=1), jnp.concatenate(kcms, axis=0))
                dqn_parts.append(dqc[:, :128])
                dqp = dqp + dqc[:, 128:]
                dkc = _dot(qtstacks[pp], jnp.concatenate(dss, axis=0))
                dkn_ref[128 * pp:128 * pp + 128, pl.ds(off, tk)] += dkc[:128, :]
                dkp = dkc[128:, :] if dkp is None else dkp + dkc[128:, :]
                dv_ref[128 * pp:128 * pp + 128, pl.ds(off, tk)] += _dot(dotstacks[pp], jnp.concatenate(pbs, axis=0))
            dqn = dqn + jnp.concatenate(dqn_parts, axis=1)
            dkpt_ref[:, pl.ds(off, tk)] += dkp
            return dqn, dqp

        carry = lax.fori_loop(0, i * r, lambda j, cr: block(j, cr, None),
                              (jnp.zeros((tq, 256), F32), jnp.zeros((tq, 128), F32)))
        for u in range(r):
            carry = block(i * r + u, carry, valids[u])
        dqn, dqp = carry
        dqn_ref[...] = dqn.astype(BF16)
        dqp_ref[...] = dqp.astype(BF16)

    return pl.pallas_call(
        body, name="mla_bwd", grid=(2, nq),
        out_shape=(_sds((s_len, 512), BF16), _sds((s_len, 256), BF16), _sds((512, s_len), F32),
                   _sds((128, s_len), F32), _sds((512, s_len), F32)),
        in_specs=[pl.BlockSpec((tq, 256), lambda g, i: (i, g)),
                  pl.BlockSpec((tq, 128), lambda g, i: (i, g)),
                  pl.BlockSpec((s_len, 256), lambda g, i: (0, g)),
                  pl.BlockSpec((s_len, 128), lambda g, i: (0, 0)),
                  pl.BlockSpec((s_len, 256), lambda g, i: (0, g)),
                  pl.BlockSpec((tq, 256), lambda g, i: (i, g)),
                  pl.BlockSpec((tq, 256), lambda g, i: (i, g)),
                  pl.BlockSpec((1, tq, 4), lambda g, i: (g, i, 0))],
        out_specs=(pl.BlockSpec((tq, 256), lambda g, i: (i, g)),
                   pl.BlockSpec((tq, 128), lambda g, i: (i, g)),
                   pl.BlockSpec((256, s_len), lambda g, i: (g, 0)),
                   pl.BlockSpec((128, s_len), lambda g, i: (0, 0)),
                   pl.BlockSpec((256, s_len), lambda g, i: (g, 0))),
        compiler_params=_params(("arbitrary", "arbitrary")),
    )(qn, qp, kn, kpt, v, o, do, lse)


def _post_call(x, tgt, oa, ob, sz, mz, ga, gb, gate, gf, wa, wb, wo):
    s_len = x.shape[0]
    tm = min(ROW_TILE, s_len)
    nstep = s_len // tm

    def body(x_ref, t_ref, oa_ref, ob_ref, sz_ref, mz_ref, ga_ref, gb_ref, gate_ref, gf_ref,
             wa_ref, wb_ref, wo_ref,
             dx2_ref, doa_ref, dob_ref, dsz_ref, dmz_ref, dga_ref, dgb_ref,
             dwo_out, dwa_out, dwb_out, dgf_ref, dgate_ref, loss_ref, dwo_ref, dwa_ref, dwb_ref):
        @pl.when(pl.program_id(0) == 0)
        def _():
            dwo_ref[...] = jnp.zeros_like(dwo_ref)
            dwa_ref[...] = jnp.zeros_like(dwa_ref)
            dwb_ref[...] = jnp.zeros_like(dwb_ref)
            dgf_ref[...] = jnp.zeros_like(dgf_ref)
            dgate_ref[...] = jnp.zeros_like(dgate_ref)
            loss_ref[...] = jnp.zeros_like(loss_ref)

        gate = gate_ref[...]
        gf = gf_ref[...]
        oa = oa_ref[...]
        ob = ob_ref[...]
        sz = sz_ref[...]
        mz = mz_ref[...]
        sa = _sigmoid(sz)
        sb = _sigmoid(mz)
        silu_a = sz * sa
        silu_b = mz * sb
        ua = (oa * silu_a).astype(BF16)
        ub = (ob * silu_b).astype(BF16)
        ya = _dot(ua, wa_ref[...])
        yb = _dot(ub, wb_ref[...])
        sga = _sigmoid(ga_ref[...])
        sgb = _sigmoid(gb_ref[...])
        merged = (sga * ya + sgb * yb).astype(BF16)
        out = _dot(merged, wo_ref[...])
        x2 = x_ref[...] + gate * out
        r2 = lax.rsqrt(jnp.mean(x2 * x2, axis=-1, keepdims=True) + EPS)
        xhat = x2 * r2
        err = xhat * gf - t_ref[...]
        loss_ref[...] += 0.5 * jnp.sum(jnp.sum(err * err, axis=1, keepdims=True), axis=0, keepdims=True) / D_MODEL
        dy = err * (1.0 / D_MODEL)
        dgf_ref[...] += jnp.sum(dy * xhat, axis=0, keepdims=True)
        dxhat = dy * gf
        dx2 = r2 * (dxhat - xhat * jnp.mean(dxhat * xhat, axis=-1, keepdims=True))
        dx2_ref[...] = dx2
        dgate_ref[...] += jnp.sum(dx2 * out, axis=0, keepdims=True)
        dout = (dx2 * gate).astype(BF16)
        dmerged = _dot_nt(dout, wo_ref[...])
        dwo_ref[...] += _dot_tn(merged, dout)
        dya = dmerged * sga
        dyb = dmerged * sgb
        dga_ref[...] = (dya * ya * (1.0 - sga)).astype(BF16)
        dgb_ref[...] = (dyb * yb * (1.0 - sgb)).astype(BF16)
        dyab = dya.astype(BF16)
        dybb = dyb.astype(BF16)
        dua = _dot_nt(dyab, wa_ref[...])
        dub = _dot_nt(dybb, wb_ref[...])
        dwa_ref[...] += _dot_tn(ua, dyab)
        dwb_ref[...] += _dot_tn(ub, dybb)
        doa_ref[...] = dua * silu_a
        dob_ref[...] = dub * silu_b
        dsz_ref[...] = (dua * oa * (sa * (1.0 + sz * (1.0 - sa)))).astype(BF16)
        dmz_ref[...] = (dub * ob * (sb * (1.0 + mz * (1.0 - sb)))).astype(BF16)

        @pl.when(pl.program_id(0) == nstep - 1)
        def _():
            dwo_out[...] = dwo_ref[...].astype(BF16)
            for k in range(N_CHIPS):
                dwa_out[k] = dwa_ref[:, 256 * k:256 * k + 256].astype(BF16)
                dwb_out[k] = dwb_ref[:, 256 * k:256 * k + 256].astype(BF16)

    return pl.pallas_call(
        body, name="post", grid=(nstep,),
        out_shape=(_sds((s_len, D_MODEL), F32), _sds((s_len, 512), F32), _sds((s_len, 512), F32),
                   _sds((s_len, 512), BF16), _sds((s_len, 512), BF16),
                   _sds((s_len, D_MODEL), BF16), _sds((s_len, D_MODEL), BF16),
                   _sds((D_MODEL, D_MODEL), BF16), _sds((N_CHIPS, 512, 256), BF16), _sds((N_CHIPS, 512, 256), BF16),
                   _sds((1, D_MODEL), F32), _sds((1, D_MODEL), F32), _sds((1, 128), F32)),
        in_specs=[_rows(tm, D_MODEL), _rows(tm, D_MODEL), _rows(tm, 512), _rows(tm, 512), _rows(tm, 512),
                  _rows(tm, 512), _rows(tm, D_MODEL), _rows(tm, D_MODEL), _whole((1, D_MODEL)), _whole((1, D_MODEL)),
                  _whole((512, D_MODEL)), _whole((512, D_MODEL)), _whole((D_MODEL, D_MODEL))],
        out_specs=(_rows(tm, D_MODEL), _rows(tm, 512), _rows(tm, 512), _rows(tm, 512), _rows(tm, 512),
                   _rows(tm, D_MODEL), _rows(tm, D_MODEL),
                   _whole((D_MODEL, D_MODEL)), _whole((N_CHIPS, 512, 256)), _whole((N_CHIPS, 512, 256)),
                   _whole((1, D_MODEL)), _whole((1, D_MODEL)), _whole((1, 128))),
        scratch_shapes=[pltpu.VMEM((D_MODEL, D_MODEL), F32), pltpu.VMEM((512, D_MODEL), F32),
                        pltpu.VMEM((512, D_MODEL), F32)],
        compiler_params=_params(("arbitrary",)),
    )(x, tgt, oa, ob, sz, mz, ga, gb, gate, gf, wa, wb, wo)


def _bwdprep_call(dsq, dsk, dsv, dsz, dqn, dqp, dkn, dvv, dkpt, dmz, dga, dgb, cq, ckv, cos128, sin128,
                  qg, kvg, w_q, w_kv):
    s_len = cq.shape[0]
    tm = min(ROW_TILE, s_len)

    def body(dsq_ref, dsk_ref, dsv_ref, dsz_ref, dqn_ref, dqp_ref, dkn_ref, dvv_ref, dkpt_ref, dmz_ref,
             dga_ref, dgb_ref, cq_ref, ckv_ref, cos_ref, sin_ref, qg_ref, kvg_ref, wq_ref, wkv_ref,
             dp_ref, dwq_ref, dwkv_ref, dqg_ref, dkvg_ref):
        @pl.when(pl.program_id(0) == 0)
        def _():
            dwq_ref[...] = jnp.zeros_like(dwq_ref)
            dwkv_ref[...] = jnp.zeros_like(dwkv_ref)
            dqg_ref[...] = jnp.zeros_like(dqg_ref)
            dkvg_ref[...] = jnp.zeros_like(dkvg_ref)

        cos = cos_ref[...]
        sin = sin_ref[...]
        dp_ref[:, O_SQ:O_SK] = dsq_ref[...] * jnp.asarray(SB_SCALE, BF16)
        dp_ref[:, O_SK:O_SV] = jnp.transpose(dsk_ref[...]).astype(BF16)
        dp_ref[:, O_SV:O_SZ] = jnp.transpose(dsv_ref[...]).astype(BF16)
        dp_ref[:, O_SZ:O_CQ] = dsz_ref[...]
        dp_ref[:, O_MZ:O_GA] = dmz_ref[...]
        dp_ref[:, O_GA:O_GB] = dga_ref[...]
        dp_ref[:, O_GB:O_KR] = dgb_ref[...]
        dkp = jnp.transpose(dkpt_ref[...])
        dp_ref[:, O_KR:O_KR + 128] = (dkp * cos).astype(BF16)
        dp_ref[:, O_KR + 128:O_END] = (dkp * sin).astype(BF16)
        dp_ref[:, O_END:W_INT] = jnp.zeros((tm, W_INT - O_END), BF16)

        cq = cq_ref[...]
        rq = lax.rsqrt(jnp.mean(cq * cq, axis=-1, keepdims=True) + EPS)
        cqh = cq * rq
        qg = qg_ref[...]
        cqn = (cqh * qg).astype(BF16)
        dqp = dqp_ref[...].astype(F32)
        dqa = jnp.concatenate([dqn_ref[...], (dqp * jnp.tile(cos, (1, 2))).astype(BF16),
                               (dqp * jnp.tile(sin, (1, 2))).astype(BF16)], axis=1)
        dcqn = _dot_nt(dqa, wq_ref[...])
        dwq_ref[...] += _dot_tn(cqn, dqa)
        dqg_ref[...] += jnp.sum(dcqn * cqh, axis=0, keepdims=True)
        dh = dcqn * qg
        dcq = rq * (dh - cqh * jnp.mean(dh * cqh, axis=-1, keepdims=True))
        dp_ref[:, O_CQ:O_CKV] = dcq.astype(BF16)

        ckv = ckv_ref[...]
        rk = lax.rsqrt(jnp.mean(ckv * ckv, axis=-1, keepdims=True) + EPS)
        ckh = ckv * rk
        kvg = kvg_ref[...]
        ckvn = (ckh * kvg).astype(BF16)
        dkva = jnp.concatenate([jnp.transpose(dkn_ref[...]).astype(BF16),
                                jnp.transpose(dvv_ref[...]).astype(BF16)], axis=1)
        dckvn = _dot_nt(dkva, wkv_ref[...])
        dwkv_ref[...] += _dot_tn(ckvn, dkva)
        dkvg_ref[...] += jnp.sum(dckvn * ckh, axis=0, keepdims=True)
        dh2 = dckvn * kvg
        dckv = rk * (dh2 - ckh * jnp.mean(dh2 * ckh, axis=-1, keepdims=True))
        dp_ref[:, O_CKV:O_MZ] = dckv.astype(BF16)

    return pl.pallas_call(
        body, name="bwdprep", grid=(s_len // tm,),
        out_shape=(_sds((s_len, W_INT), BF16), _sds((Q_RANK, 1024), F32), _sds((KV_RANK, 1024), F32),
                   _sds((1, Q_RANK), F32), _sds((1, KV_RANK), F32)),
        in_specs=[_rows(tm, 512), _cols(512, tm), _cols(512, tm), _rows(tm, 512), _rows(tm, 512), _rows(tm, 256),
                  _cols(512, tm), _cols(512, tm), _cols(128, tm), _rows(tm, 512), _rows(tm, D_MODEL),
                  _rows(tm, D_MODEL), _rows(tm, Q_RANK), _rows(tm, KV_RANK), _rows(tm, LANES), _rows(tm, LANES),
                  _whole((1, Q_RANK)), _whole((1, KV_RANK)), _whole((Q_RANK, 1024)), _whole((KV_RANK, 1024))],
        out_specs=(_rows(tm, W_INT), _whole((Q_RANK, 1024)), _whole((KV_RANK, 1024)),
                   _whole((1, Q_RANK)), _whole((1, KV_RANK))),
        compiler_params=_params(("arbitrary",)),
    )(dsq, dsk, dsv, dsz, dqn, dqp, dkn, dvv, dkpt, dmz, dga, dgb, cq, ckv, cos128, sin128, qg, kvg, w_q, w_kv)


def _dh_call(dproj, w_int_t, x, dx2, scale, g1):
    s_len = x.shape[0]
    tm = min(2 * ROW_TILE, s_len)

    def body(dp_ref, wt_ref, x_ref, dx2_ref, sc_ref, g1_ref, gx_ref, dsh_ref, dsc_ref, dg1_ref):
        @pl.when(pl.program_id(0) == 0)
        def _():
            dsh_ref[...] = jnp.zeros_like(dsh_ref)
            dsc_ref[...] = jnp.zeros_like(dsc_ref)
            dg1_ref[...] = jnp.zeros_like(dg1_ref)

        dp = dp_ref[...]
        dh = jnp.concatenate([_dot(dp, wt_ref[0]), _dot(dp, wt_ref[1])], axis=1)
        xt = x_ref[...]
        r = lax.rsqrt(jnp.mean(xt * xt, axis=-1, keepdims=True) + EPS)
        xh = xt * r
        g1 = g1_ref[...]
        xg = xh * g1
        dsh_ref[...] += jnp.sum(dh, axis=0, keepdims=True)
        dsc_ref[...] += jnp.sum(dh * xg, axis=0, keepdims=True)
        dxg = dh * (1.0 + sc_ref[...])
        dg1_ref[...] += jnp.sum(dxg * xh, axis=0, keepdims=True)
        dxh = dxg * g1
        gx_ref[...] = dx2_ref[...] + r * (dxh - xh * jnp.mean(dxh * xh, axis=-1, keepdims=True))

    return pl.pallas_call(
        body, name="dh", grid=(s_len // tm,),
        out_shape=(_sds((s_len, D_MODEL), F32), _sds((1, D_MODEL), F32), _sds((1, D_MODEL), F32),
                   _sds((1, D_MODEL), F32)),
        in_specs=[_rows(tm, W_INT), _whole((2, W_INT, HALF_D)), _rows(tm, D_MODEL), _rows(tm, D_MODEL),
                  _whole((1, D_MODEL)), _whole((1, D_MODEL))],
        out_specs=(_rows(tm, D_MODEL), _whole((1, D_MODEL)), _whole((1, D_MODEL)), _whole((1, D_MODEL))),
        compiler_params=_params(("arbitrary",)),
    )(dproj, w_int_t, x, dx2, scale, g1)


def _small_call(svg, ct, dmod_sh):
    def body(sv_ref, ct_ref, dm_ref, tot_ref, gwada_ref):
        acc = sv_ref[0:1, :]
        for d in range(1, N_DEV):
            acc = acc + sv_ref[d:d + 1, :]
        tot_ref[...] = acc
        gwada_ref[...] = lax.dot_general(ct_ref[...], dm_ref[...], (((1,), (0,)), ((), ())),
                                         precision=lax.Precision.HIGHEST, preferred_element_type=F32)

    vmem = pl.BlockSpec(memory_space=pltpu.VMEM)
    return pl.pallas_call(
        body, name="small_grads",
        out_shape=(_sds((1, 8 * SV_COLS), F32), _sds((D_MODEL, 768), F32)),
        in_specs=[vmem, vmem, vmem], out_specs=(vmem, vmem),
        compiler_params=_params(),
    )(svg, ct, dmod_sh)


def _adamw_tile_rows(rows, cols):
    budget = 2 << 20
    if rows * cols * 4 <= budget or rows % 8:
        return rows
    best = 8
    for tr in range(8, rows + 1, 8):
        if rows % tr == 0 and tr * cols * 4 <= budget:
            best = tr
    return best


def _adamw_math(w, gg, m, v):
    m2 = ADAM_B1 * m + (1.0 - ADAM_B1) * gg
    v2 = ADAM_B2 * v + (1.0 - ADAM_B2) * (gg * gg)
    m_hat = m2 / (1.0 - ADAM_B1 ** ADAM_STEP)
    v_hat = v2 / (1.0 - ADAM_B2 ** ADAM_STEP)
    return -ADAM_LR * (m_hat / (jnp.sqrt(v_hat) + ADAM_EPS) + ADAM_WD * w), m2, v2


def _adamw_call(name, w, g, m, v):
    rows, cols = w.shape
    tr = _adamw_tile_rows(rows, cols)
    halves = g.ndim == 3

    def body(w_ref, g_ref, m_ref, v_ref, *out_refs):
        if halves:
            gg = jnp.concatenate([g_ref[0], g_ref[1]], axis=1)
            out_refs[0][...] = gg
        else:
            gg = g_ref[...]
        d_ref, nm_ref, nv_ref = out_refs[-3:]
        d_ref[...], nm_ref[...], nv_ref[...] = _adamw_math(w_ref[...], gg, m_ref[...], v_ref[...])

    spec = pl.BlockSpec((tr, cols), lambda i: (i, 0))
    g_spec = pl.BlockSpec((2, tr, cols // 2), lambda i: (0, i, 0)) if halves else spec
    n_out = 4 if halves else 3
    outs = pl.pallas_call(
        body, name="adamw_" + name, grid=(rows // tr,),
        out_shape=(_sds((rows, cols), F32),) * n_out,
        in_specs=[spec, g_spec, spec, spec], out_specs=(spec,) * n_out,
        compiler_params=_params(("parallel",)),
    )(w, g, m, v)
    return tuple(outs) if halves else (g,) + tuple(outs)


IN_SHARD = IN_WIDTH // N_CHIPS
HALF_D = D_MODEL // 2
SMALL_ROWS = (576, 512, 1024, 1024, 2048)
SMALL_TOTAL = sum(SMALL_ROWS)
SMALL_HALF = SMALL_TOTAL // 2
SMALL_SUM_ROWS = 432


def _gather_call(c_row, w_ada_sh, pack_in, pack_small):
    def body(c_ref, wada_ref, pki_ref, pks_ref, mg_ref, cg_ref, gwi_ref, gws_ref,
             cv, ssem_c, rsem_c, ssem_m, rsem_m, ssem_w, rsem_w, ssem_f, rsem_f, lsem):
        x, y, c = lax.axis_index("x"), lax.axis_index("y"), lax.axis_index("c")
        me = 4 * x + 2 * y + c
        chip = 2 * x + y
        rel3 = [(1, 0), (0, 1), (1, 1)]
        packs = [(pki_ref, gwi_ref), (pks_ref, gws_ref)]

        def slot(a, gw, k, h):
            return gw.at[h, k] if a == 0 else gw.at[k, h]

        sends = []
        for j, (dx, dy) in enumerate(rel3):
            for a, (pk, gw) in enumerate(packs):
                cp = pltpu.make_async_remote_copy(
                    src_ref=pk.at[c], dst_ref=slot(a, gw, chip, c), send_sem=ssem_w.at[j, a],
                    recv_sem=rsem_w.at[j, a], device_id=(_flip(x, dx), _flip(y, dy), c), device_id_type=MESH)
                cp.start()
                sends.append(cp)
        owns = []
        for a, (pk, gw) in enumerate(packs):
            for h in range(2):
                own = pltpu.make_async_copy(pk.at[h], slot(a, gw, chip, h), lsem.at[a, h])
                own.start()
                owns.append(own)

        cv[me] = c_ref[...]
        for r in range(1, N_DEV):
            dx, dy, dc = (r >> 2) & 1, (r >> 1) & 1, r & 1
            cp = pltpu.make_async_remote_copy(
                src_ref=c_ref, dst_ref=cv.at[me], send_sem=ssem_c.at[r - 1], recv_sem=rsem_c.at[r - 1],
                device_id=(_flip(x, dx), _flip(y, dy), _flip(c, dc)), device_id_type=MESH)
            cp.start()
            sends.append(cp)
        for r in range(1, N_DEV):
            dx, dy, dc = (r >> 2) & 1, (r >> 1) & 1, r & 1
            src = 4 * _flip(x, dx) + 2 * _flip(y, dy) + _flip(c, dc)
            pltpu.make_async_remote_copy(
                src_ref=c_ref, dst_ref=cv.at[src], send_sem=ssem_c.at[r - 1], recv_sem=rsem_c.at[r - 1],
                device_id=(x, y, c), device_id_type=MESH).wait_recv()
        rows = lax.broadcasted_iota(jnp.int32, (N_DEV, D_MODEL), 0)
        call = jnp.zeros((N_DEV, D_MODEL), F32)
        for b in range(N_DEV):
            call = jnp.where(rows == b, jnp.broadcast_to(cv[b], (N_DEV, D_MODEL)), call)
        cg_ref[...] = call

        mg_ref[chip] = lax.dot_general(call, wada_ref[...], (((1,), (0,)), ((), ())),
                                       precision=lax.Precision.HIGHEST, preferred_element_type=F32)
        for j, (dx, dy) in enumerate(rel3):
            cp = pltpu.make_async_remote_copy(
                src_ref=mg_ref.at[chip], dst_ref=mg_ref.at[chip], send_sem=ssem_m.at[j], recv_sem=rsem_m.at[j],
                device_id=(_flip(x, dx), _flip(y, dy), c), device_id_type=MESH)
            cp.start()
            sends.append(cp)
        for j, (dx, dy) in enumerate(rel3):
            src_chip = 2 * _flip(x, dx) + _flip(y, dy)
            pltpu.make_async_remote_copy(
                src_ref=mg_ref.at[src_chip], dst_ref=mg_ref.at[src_chip], send_sem=ssem_m.at[j],
                recv_sem=rsem_m.at[j], device_id=(x, y, c), device_id_type=MESH).wait_recv()
        for j, (dx, dy) in enumerate(rel3):
            src_chip = 2 * _flip(x, dx) + _flip(y, dy)
            for a, (pk, gw) in enumerate(packs):
                pltpu.make_async_remote_copy(
                    src_ref=pk.at[c], dst_ref=slot(a, gw, src_chip, c), send_sem=ssem_w.at[j, a],
                    recv_sem=rsem_w.at[j, a], device_id=(x, y, c), device_id_type=MESH).wait_recv()
                cp = pltpu.make_async_remote_copy(
                    src_ref=slot(a, gw, src_chip, c), dst_ref=slot(a, gw, src_chip, c), send_sem=ssem_f.at[j, a],
                    recv_sem=rsem_f.at[j, a], device_id=(x, y, 1 - c), device_id_type=MESH)
                cp.start()
                sends.append(cp)
        for j, (dx, dy) in enumerate(rel3):
            src_chip = 2 * _flip(x, dx) + _flip(y, dy)
            for a, (pk, gw) in enumerate(packs):
                pltpu.make_async_remote_copy(
                    src_ref=pk.at[c], dst_ref=slot(a, gw, src_chip, 1 - c), send_sem=ssem_f.at[j, a],
                    recv_sem=rsem_f.at[j, a], device_id=(x, y, c), device_id_type=MESH).wait_recv()
        for cp in sends:
            cp.wait_send()
        for own in owns:
            own.wait()

    vmem = pl.BlockSpec(memory_space=pltpu.VMEM)
    return pl.pallas_call(
        body, name="gather_fwd",
        out_shape=(_sds((N_CHIPS, N_DEV, 768), F32), _sds((N_DEV, D_MODEL), F32),
                   _sds((2, N_CHIPS, IN_SHARD, HALF_D), BF16), _sds((N_CHIPS, 2, SMALL_HALF, LANES), BF16)),
        in_specs=[vmem, vmem, vmem, vmem], out_specs=(vmem, vmem, vmem, vmem),
        scratch_shapes=[
            pltpu.VMEM((N_DEV, 1, D_MODEL), F32),
            pltpu.SemaphoreType.DMA((N_DEV - 1,)), pltpu.SemaphoreType.DMA((N_DEV - 1,)),
            pltpu.SemaphoreType.DMA((3,)), pltpu.SemaphoreType.DMA((3,)),
            pltpu.SemaphoreType.DMA((3, 2)), pltpu.SemaphoreType.DMA((3, 2)),
            pltpu.SemaphoreType.DMA((3, 2)), pltpu.SemaphoreType.DMA((3, 2)),
            pltpu.SemaphoreType.DMA((2, 2)),
        ],
        compiler_params=_params(),
    )(c_row, w_ada_sh, pack_in, pack_small)


def _reduce_call(g_in, g_small, sv):
    def body(gi_ref, gs_ref, sv_ref, fi_ref, fs_ref, svg_ref, pair_i, pair_s, send_i, send_s, land_i, land_s,
             ssem_p, rsem_p, ssem_g, rsem_g, ssem_s, rsem_s, ssem_x, rsem_x):
        x, y, c = lax.axis_index("x"), lax.axis_index("y"), lax.axis_index("c")
        me = 4 * x + 2 * y + c
        chip = 2 * x + y
        rel3 = [(1, 0), (0, 1), (1, 1)]
        payloads = [(gi_ref, pair_i, send_i, land_i, fi_ref), (gs_ref, pair_s, send_s, land_s, fs_ref)]
        where = [lambda k, h: N_CHIPS * h + k, lambda k, h: 2 * k + h]
        copies = []

        for k in range(N_CHIPS):
            for a, (g, pair, _, _, _) in enumerate(payloads):
                cp = pltpu.make_async_remote_copy(
                    src_ref=g.at[where[a](k, 1 - c)], dst_ref=pair.at[k], send_sem=ssem_p.at[k, a],
                    recv_sem=rsem_p.at[k, a], device_id=(x, y, 1 - c), device_id_type=MESH)
                cp.start()
                copies.append(cp)

        for r in range(1, N_DEV):
            dx, dy, dc = (r >> 2) & 1, (r >> 1) & 1, r & 1
            cp = pltpu.make_async_remote_copy(
                src_ref=sv_ref, dst_ref=svg_ref.at[me], send_sem=ssem_s.at[r - 1], recv_sem=rsem_s.at[r - 1],
                device_id=(_flip(x, dx), _flip(y, dy), _flip(c, dc)), device_id_type=MESH)
            cp.start()
            copies.append(cp)
        svg_ref[me] = sv_ref[...]

        def pair_sum(k, store_in, store_small):
            for a, (g, pair, _, _, _) in enumerate(payloads):
                pltpu.make_async_remote_copy(
                    src_ref=g.at[where[a](k, c)], dst_ref=pair.at[k], send_sem=ssem_p.at[k, a],
                    recv_sem=rsem_p.at[k, a], device_id=(x, y, c), device_id_type=MESH).wait_recv()
            for qd in range(HALF_D // LANES):
                sl = slice(LANES * qd, LANES * qd + LANES)
                store_in(sl, gi_ref[where[0](k, c), :, sl].astype(F32) + pair_i[k, :, sl].astype(F32))

            def rows(i, carry):
                sl = pl.ds(pl.multiple_of(i * SMALL_SUM_ROWS, 16), SMALL_SUM_ROWS)
                store_small(sl, gs_ref[where[1](k, c), sl, :].astype(F32) + pair_s[k, sl, :].astype(F32))
                return carry

            lax.fori_loop(0, SMALL_HALF // SMALL_SUM_ROWS, rows, 0)

        for j, (dx, dy) in enumerate(rel3):
            tx, ty = _flip(x, dx), _flip(y, dy)

            def put_in(sl, val, j=j):
                send_i[j, :, sl] = val.astype(BF16)

            def put_small(sl, val, j=j):
                send_s[j, sl, :] = val.astype(BF16)

            pair_sum(2 * tx + ty, put_in, put_small)
            for a, (_, _, send, land, _) in enumerate(payloads):
                cp = pltpu.make_async_remote_copy(
                    src_ref=send.at[j], dst_ref=land.at[j], send_sem=ssem_g.at[j, a], recv_sem=rsem_g.at[j, a],
                    device_id=(tx, ty, c), device_id_type=MESH)
                cp.start()
                copies.append(cp)

        def own_in(sl, val):
            fi_ref[c, :, sl] = val

        def own_small(sl, val):
            fs_ref[c, sl, :] = val

        pair_sum(chip, own_in, own_small)
        for j in range(3):
            for a, (_, _, send, land, _) in enumerate(payloads):
                pltpu.make_async_remote_copy(
                    src_ref=send.at[j], dst_ref=land.at[j], send_sem=ssem_g.at[j, a], recv_sem=rsem_g.at[j, a],
                    device_id=(x, y, c), device_id_type=MESH).wait_recv()
            for qd in range(HALF_D // LANES):
                sl = slice(LANES * qd, LANES * qd + LANES)
                fi_ref[c, :, sl] += land_i[j, :, sl].astype(F32)

            def add_rows(i, carry, j=j):
                sl = pl.ds(pl.multiple_of(i * SMALL_SUM_ROWS, 16), SMALL_SUM_ROWS)
                fs_ref[c, sl, :] += land_s[j, sl, :].astype(F32)
                return carry

            lax.fori_loop(0, SMALL_HALF // SMALL_SUM_ROWS, add_rows, 0)

        for a, f in enumerate((fi_ref, fs_ref)):
            cp = pltpu.make_async_remote_copy(
                src_ref=f.at[c], dst_ref=f.at[c], send_sem=ssem_x.at[a], recv_sem=rsem_x.at[a],
                device_id=(x, y, 1 - c), device_id_type=MESH)
            cp.start()
            copies.append(cp)
        for a, f in enumerate((fi_ref, fs_ref)):
            pltpu.make_async_remote_copy(
                src_ref=f.at[c], dst_ref=f.at[1 - c], send_sem=ssem_x.at[a], recv_sem=rsem_x.at[a],
                device_id=(x, y, c), device_id_type=MESH).wait_recv()
        for r in range(1, N_DEV):
            dx, dy, dc = (r >> 2) & 1, (r >> 1) & 1, r & 1
            src = 4 * _flip(x, dx) + 2 * _flip(y, dy) + _flip(c, dc)
            pltpu.make_async_remote_copy(
                src_ref=sv_ref, dst_ref=svg_ref.at[src], send_sem=ssem_s.at[r - 1],
                recv_sem=rsem_s.at[r - 1], device_id=(x, y, c), device_id_type=MESH).wait_recv()
        for cp in copies:
            cp.wait_send()

    vmem = pl.BlockSpec(memory_space=pltpu.VMEM)
    return pl.pallas_call(
        body, name="grad_reduce",
        out_shape=(_sds((2, IN_SHARD, HALF_D), F32), _sds((2, SMALL_HALF, LANES), F32),
                   _sds((N_DEV, 8, SV_COLS), F32)),
        in_specs=[vmem, vmem, vmem], out_specs=(vmem, vmem, vmem),
        scratch_shapes=[
            pltpu.VMEM((N_CHIPS, IN_SHARD, HALF_D), BF16), pltpu.VMEM((N_CHIPS, SMALL_HALF, LANES), BF16),
            pltpu.VMEM((3, IN_SHARD, HALF_D), BF16), pltpu.VMEM((3, SMALL_HALF, LANES), BF16),
            pltpu.VMEM((3, IN_SHARD, HALF_D), BF16), pltpu.VMEM((3, SMALL_HALF, LANES), BF16),
            pltpu.SemaphoreType.DMA((N_CHIPS, 2)), pltpu.SemaphoreType.DMA((N_CHIPS, 2)),
            pltpu.SemaphoreType.DMA((3, 2)), pltpu.SemaphoreType.DMA((3, 2)),
            pltpu.SemaphoreType.DMA((N_DEV - 1,)), pltpu.SemaphoreType.DMA((N_DEV - 1,)),
            pltpu.SemaphoreType.DMA((2,)), pltpu.SemaphoreType.DMA((2,)),
        ],
        compiler_params=_params(),
    )(g_in, g_small, sv)


def _dwin_call(h, dproj):
    s_len = h.shape[0]
    tm = min(4 * ROW_TILE, s_len)
    nrow = s_len // tm
    nc = 4
    chunk = W_INT // nc

    def body(h_ref, dp_ref, dw_ref, acc):
        i = pl.program_id(1)

        @pl.when(i == 0)
        def _():
            acc[...] = jnp.zeros_like(acc)

        acc[...] += _dot_tn(dp_ref[...], h_ref[...])

        @pl.when(i == nrow - 1)
        def _():
            dw_ref[0] = acc[:, :HALF_D].astype(BF16)
            dw_ref[1] = acc[:, HALF_D:].astype(BF16)

    return pl.pallas_call(
        body, name="dwin", grid=(nc, nrow),
        out_shape=_sds((2, W_INT, HALF_D), BF16),
        in_specs=[pl.BlockSpec((tm, D_MODEL), lambda c, i: (i, 0)),
                  pl.BlockSpec((tm, chunk), lambda c, i: (i, c))],
        out_specs=pl.BlockSpec((2, chunk, HALF_D), lambda c, i: (0, c, 0)),
        scratch_shapes=[pltpu.VMEM((chunk, D_MODEL), F32)],
        compiler_params=_params(("parallel", "arbitrary")),
    )(h, dproj)


def _internal_weights(w_in_t, w_uq, w_ukv):
    krot_t = w_in_t[:, 2688:2720]
    krot_sw = krot_t.reshape(2, 2, 16, HALF_D)[:, ::-1].reshape(2, 32, HALF_D)
    w_int_t = jnp.concatenate([
        w_in_t[:, 0:2688], w_in_t[:, 2720:5280], jnp.tile(krot_t, (1, 4, 1)), jnp.tile(krot_sw, (1, 4, 1)),
        jnp.zeros((2, W_INT - O_END, HALF_D), w_in_t.dtype)], axis=1)
    uq = w_uq.reshape(Q_RANK, N_HEADS, 96)
    wp = uq[:, :, 64:].reshape(Q_RANK, 256)
    w_q = jnp.concatenate([uq[:, :, :64].reshape(Q_RANK, 512), wp, _swap_halves(wp, 32)], axis=1)
    ukv = w_ukv.reshape(KV_RANK, N_HEADS, 128)
    w_kv = jnp.concatenate([ukv[:, :, :64].reshape(KV_RANK, 512), ukv[:, :, 64:].reshape(KV_RANK, 512)], axis=1)
    return w_int_t, w_q, w_kv


def _true_weight_grads(dwi_t, dwq, dwkv):
    dkr = dwi_t[:, O_KR:O_KR + 128].astype(F32).reshape(2, 4, 32, HALF_D).sum(axis=1)
    dkr_sw = dwi_t[:, O_KR + 128:O_END].astype(F32).reshape(2, 4, 32, HALF_D).sum(axis=1)
    dkr_sw = dkr_sw.reshape(2, 2, 16, HALF_D)[:, ::-1].reshape(2, 32, HALF_D)
    dkrot_t = (dkr + dkr_sw).astype(dwi_t.dtype)
    g_in_t = jnp.concatenate([dwi_t[:, 0:O_MZ], dkrot_t, dwi_t[:, O_MZ:O_KR]], axis=1)
    dwp = dwq[:, 512:768] + _swap_halves(dwq[:, 768:1024], 32)
    g_uq = jnp.concatenate([dwq[:, :512].reshape(Q_RANK, N_HEADS, 64), dwp.reshape(Q_RANK, N_HEADS, 32)],
                           axis=2).reshape(Q_RANK, 768)
    g_ukv = jnp.concatenate([dwkv[:, :512].reshape(KV_RANK, N_HEADS, 64), dwkv[:, 512:].reshape(KV_RANK, N_HEADS, 64)],
                            axis=2).reshape(KV_RANK, 1024)
    return g_in_t, g_uq, g_ukv


def _swap_halves(w, group):
    r, n = w.shape
    return w.reshape(r, n // group, 2, group // 2)[:, :, ::-1, :].reshape(r, n)


def _pack_shards(parts):
    return jnp.concatenate([p.reshape(-1, LANES) for p in parts], axis=0)


def _unpack_small(gw):
    offs = [0]
    for r in SMALL_ROWS:
        offs.append(offs[-1] + r)

    def cols(i, rows, shard_cols):
        blk = gw[:, offs[i]:offs[i + 1]].reshape(N_CHIPS, rows, shard_cols)
        return blk.transpose(1, 0, 2).reshape(rows, N_CHIPS * shard_cols)

    return (cols(0, Q_RANK, 192), cols(1, KV_RANK, 256), cols(2, 512, 256), cols(3, 512, 256),
            gw[:, offs[4]:offs[5]].reshape(D_MODEL, D_MODEL))


def _chip_major(g, shard_cols):
    r = g.shape[0]
    return g.reshape(r, N_CHIPS, shard_cols).transpose(1, 0, 2).reshape(N_CHIPS, -1, LANES)


def kernel(x, c, positions, w_ada, b_ada, norm_gain, w_in, q_norm_gain, w_uq, kv_norm_gain, w_ukv, w_branch_a, w_branch_b, w_out, final_norm_gain, loss_target, m_w_ada, m_b_ada, m_norm_gain, m_w_in, m_q_norm_gain, m_w_uq, m_kv_norm_gain, m_w_ukv, m_w_branch_a, m_w_branch_b, m_w_out, m_final_norm_gain, v_w_ada, v_b_ada, v_norm_gain, v_w_in, v_q_norm_gain, v_w_uq, v_kv_norm_gain, v_w_ukv, v_w_branch_a, v_w_branch_b, v_w_out, v_final_norm_gain):
    ix, iy, ic = lax.axis_index("x"), lax.axis_index("y"), lax.axis_index("c")
    me = 4 * ix + 2 * iy + ic
    chip = 2 * ix + iy
    xs = x[0]
    tgt = loss_target[0]
    s_len = xs.shape[0]

    w_in_t = jnp.swapaxes(w_in[0], 0, 1)
    w_in_tb = w_in_t.astype(BF16)
    pack_in = jnp.stack([w_in_tb[:, :HALF_D], w_in_tb[:, HALF_D:]], axis=0)
    small_shards = (w_uq[0], w_ukv[0], w_branch_a[0], w_branch_b[0], w_out[0])
    pack_small = _pack_shards([s.astype(BF16) for s in small_shards]).reshape(2, SMALL_HALF, LANES)
    mg, call, gw_in, gw_small = _gather_call(c, w_ada[0], pack_in, pack_small)
    mod = mg.transpose(1, 0, 2).reshape(N_DEV, 3 * D_MODEL) + b_ada
    mod_me = lax.dynamic_slice_in_dim(mod, me, 1, axis=0)
    shift, scale, gate = mod_me[:, :D_MODEL], mod_me[:, D_MODEL:2 * D_MODEL], mod_me[:, 2 * D_MODEL:]

    f_in_t = gw_in.reshape(2, IN_WIDTH, HALF_D)
    f_uq, f_ukv, f_a, f_b, f_out = _unpack_small(gw_small.reshape(N_CHIPS, SMALL_TOTAL, LANES))
    w_int_t, w_q, w_kv = _internal_weights(f_in_t, f_uq, f_ukv)

    inv_freq = ROPE_BASE ** (-jnp.arange(0, ROPE_DIM, 2, dtype=F32) / ROPE_DIM)
    ang = positions[0].astype(F32)[:, None] * inv_freq
    cs, sn = jnp.cos(ang), jnp.sin(ang)
    cos128 = jnp.tile(jnp.concatenate([cs, cs], axis=1), (1, 4))
    sin128 = jnp.tile(jnp.concatenate([-sn, sn], axis=1), (1, 4))

    (h, sq, sk, sv, sz, cq, ckv, mz, ga, gb, kpt, qn, qp, kn, vv) = _inproj_call(
        xs, shift, scale, norm_gain, w_int_t, w_q, w_kv, q_norm_gain, kv_norm_gain, cos128, sin128)
    oa, lt, first = _sb_fwd_call(sq, sk, sv)
    ob, lse = _mla_fwd_call(qn, qp, kn, kpt, vv)

    gf = final_norm_gain.reshape(1, D_MODEL)
    (dx2, doa, dob, dsz, dmz, dga, dgb, dwo, dwa, dwb, dgf, dgate, loss_p) = _post_call(
        xs, tgt, oa, ob, sz, mz, ga, gb, gate, gf, f_a, f_b, f_out)

    dsq, dsk_t, dsv_t = _sb_bwd_call(first[:, :, 0, 0].reshape(-1), sq, sk, sv, doa, lt)
    dqn, dqp, dkn_t, dkpt_t, dvv_t = _mla_bwd_call(qn, qp, kn, kpt, vv, ob, dob, lse)

    dproj, dwq, dwkv, dqg, dkvg = _bwdprep_call(
        dsq, dsk_t, dsv_t, dsz, dqn, dqp, dkn_t, dvv_t, dkpt_t, dmz, dga, dgb, cq, ckv, cos128, sin128,
        q_norm_gain, kv_norm_gain, w_q, w_kv)
    grad_x, dshift, dscale, dg1 = _dh_call(dproj, w_int_t, xs, dx2, scale, norm_gain)
    dwi_t = _dwin_call(h, dproj)
    g_in_t, g_uq, g_ukv = _true_weight_grads(dwi_t, dwq, dwkv)

    g_in_pieces = g_in_t.reshape(N_DEV, IN_SHARD, HALF_D)
    g_small = jnp.concatenate([
        _chip_major(g_uq, 192).astype(BF16), _chip_major(g_ukv, 256).astype(BF16),
        dwa.reshape(N_CHIPS, -1, LANES), dwb.reshape(N_CHIPS, -1, LANES),
        dwo.reshape(N_CHIPS, -1, LANES)], axis=1).reshape(N_DEV, SMALL_HALF, LANES)
    small = jnp.concatenate([
        dshift, dscale, dgate, dg1, dqg, dkvg, dgf, loss_p,
        jnp.zeros((1, 8 * SV_COLS - 5888), F32)], axis=1).reshape(8, SV_COLS)
    full_in, full_small, svg = _reduce_call(g_in_pieces, g_small, small)
    full = full_small.reshape(SMALL_TOTAL, LANES)
    offs = [0]
    for r in SMALL_ROWS:
        offs.append(offs[-1] + r)
    gs_uq = full[offs[0]:offs[1]].reshape(Q_RANK, 192)
    gs_ukv = full[offs[1]:offs[2]].reshape(KV_RANK, 256)
    gs_a = full[offs[2]:offs[3]].reshape(512, 256)
    gs_b = full[offs[3]:offs[4]].reshape(512, 256)
    gs_out = full[offs[4]:offs[5]].reshape(256, D_MODEL)

    svm = svg.reshape(N_DEV, 8 * SV_COLS)
    dmod_sh = lax.dynamic_slice_in_dim(svm[:, :3 * D_MODEL], chip * 768, 768, axis=1)
    tot, gs_ada = _small_call(svm, call.T, dmod_sh)
    g_bada = tot[:, 0:3072]
    g_g1 = tot[:, 3072:4096]
    g_qg = tot[:, 4096:4480]
    g_kvg = tot[:, 4480:4736]
    g_gf = tot[:, 4736:5760]
    loss = tot[0, 5760]

    names = ["w_ada", "b_ada", "norm_gain", "w_in", "q_norm_gain", "w_uq", "kv_norm_gain", "w_ukv",
             "w_branch_a", "w_branch_b", "w_out", "final_norm_gain"]
    ws = [w_ada[0], b_ada, norm_gain, w_in_t, q_norm_gain, w_uq[0], kv_norm_gain, w_ukv[0],
          w_branch_a[0], w_branch_b[0], w_out[0], final_norm_gain.reshape(1, D_MODEL)]
    gs = [gs_ada, g_bada, g_g1, full_in, g_qg, gs_uq, g_kvg, gs_ukv, gs_a, gs_b, gs_out, g_gf]
    ms = [m_w_ada[0], m_b_ada, m_norm_gain, jnp.swapaxes(m_w_in[0], 0, 1), m_q_norm_gain, m_w_uq[0],
          m_kv_norm_gain, m_w_ukv[0], m_w_branch_a[0], m_w_branch_b[0], m_w_out[0],
          m_final_norm_gain.reshape(1, D_MODEL)]
    vs = [v_w_ada[0], v_b_ada, v_norm_gain, jnp.swapaxes(v_w_in[0], 0, 1), v_q_norm_gain, v_w_uq[0],
          v_kv_norm_gain, v_w_ukv[0], v_w_branch_a[0], v_w_branch_b[0], v_w_out[0],
          v_final_norm_gain.reshape(1, D_MODEL)]
    refs = [w_ada, b_ada, norm_gain, w_in, q_norm_gain, w_uq, kv_norm_gain, w_ukv,
            w_branch_a, w_branch_b, w_out, final_norm_gain]
    grads, deltas, new_ms, new_vs = [], [], [], []
    for n, w_, g_, m_, v_, ref in zip(names, ws, gs, ms, vs, refs):
        outs = _adamw_call(n, w_, g_, m_, v_)
        if n == "w_in":
            outs = tuple(jnp.swapaxes(o_, 0, 1) for o_ in outs)
        for lst, o_ in zip((grads, deltas, new_ms, new_vs), outs):
            lst.append(o_.reshape(ref.shape))

    return (loss, grad_x.reshape(x.shape), *grads, *deltas, *new_ms, *new_vs)
```

```python
import math

import jax
import jax.numpy as jnp
from jax import lax
from jax.experimental import pallas as pl
from jax.experimental.pallas import tpu as pltpu

F32 = jnp.float32
BF16 = jnp.bfloat16

D_MODEL = 1024
SB_WIDTH = 512
MLA_WIDTH = 512
Q_RANK = 384
KV_RANK = 256
ROPE_DIM = 32
N_HEADS = 8
IN_WIDTH = 5280
EPS = 1e-6
ROPE_BASE = 10000.0
MLA_SCALE = 1.0 / math.sqrt(96.0)
SB_SCALE = 0.125
LOG2E = 1.4426950408889634

ADAM_LR = 0.001
ADAM_B1 = 0.9
ADAM_B2 = 0.999
ADAM_EPS = 1e-08
ADAM_WD = 0.01
ADAM_STEP = 10

O_SQ, O_SK, O_SV, O_SZ, O_CQ, O_CKV, O_MZ, O_GA, O_GB, O_KR, O_END = (
    0, 512, 1024, 1536, 2048, 2432, 2688, 3200, 4224, 5248, 5504)
W_INT = 5632

N_CHIPS = 4
N_DEV = 8
LANES = 128
SV_COLS = 768

ROW_TILE = 256
ATT_TILE = 256
ATT_Q_TILES = 2
FWD_Q_TILES = 4
SB_Q_TILES = 1
MLA_KEY_TILE = 512
VMEM_LIMIT = 56 * 1024 * 1024

MESH = pl.DeviceIdType.MESH


def _dot(a, b):
    return lax.dot_general(a, b, (((1,), (0,)), ((), ())), preferred_element_type=F32)


def _dot_nt(a, b):
    return lax.dot_general(a, b, (((1,), (1,)), ((), ())), preferred_element_type=F32)


def _dot_tn(a, b):
    return lax.dot_general(a, b, (((0,), (0,)), ((), ())), preferred_element_type=F32)


def _sigmoid(z):
    return 1.0 / (1.0 + jnp.exp2(z * (-LOG2E)))


def _params(sem=None):
    if sem is None:
        return pltpu.CompilerParams(vmem_limit_bytes=VMEM_LIMIT)
    return pltpu.CompilerParams(dimension_semantics=sem, vmem_limit_bytes=VMEM_LIMIT)


def _rows(tm, n):
    return pl.BlockSpec((tm, n), lambda i: (i, 0))


def _cols(n, tm):
    return pl.BlockSpec((n, tm), lambda i: (0, i))


def _whole(shape):
    nd = len(shape)
    return pl.BlockSpec(shape, lambda i: (0,) * nd)


def _sds(shape, dtype):
    return jax.ShapeDtypeStruct(shape, dtype)


def _flip(v, d):
    return 1 - v if d else v


def _inproj_call(x, shift, scale, g1, w_int, w_q, w_kv, qg, kvg, cos128, sin128):
    s_len = x.shape[0]
    tm = min(ROW_TILE, s_len)

    def body(x_ref, sh_ref, sc_ref, g1_ref, w_ref, wq_ref, wkv_ref, qg_ref, kvg_ref, cos_ref, sin_ref,
             h_ref, sq_ref, sk_ref, sv_ref, sz_ref, cq_ref, ckv_ref, mz_ref, ga_ref, gb_ref, kpt_ref,
             qn_ref, qp_ref, kn_ref, vv_ref):
        xt = x_ref[...]
        r = lax.rsqrt(jnp.mean(xt * xt, axis=-1, keepdims=True) + EPS)
        h = (xt * r * g1_ref[...]) * (1.0 + sc_ref[...]) + sh_ref[...]
        hb = h.astype(BF16)
        h_ref[...] = hb

        def seg(a, b):
            return _dot_nt(hb[:, :HALF_D], w_ref[0, a:b, :]) + _dot_nt(hb[:, HALF_D:], w_ref[1, a:b, :])

        sq_ref[...] = (seg(O_SQ, O_SK) * SB_SCALE).astype(BF16)
        sk_ref[...] = seg(O_SK, O_SV).astype(BF16)
        sv_ref[...] = seg(O_SV, O_SZ).astype(BF16)
        sz_ref[...] = seg(O_SZ, O_CQ)
        mz_ref[...] = seg(O_MZ, O_GA)
        ga_ref[...] = seg(O_GA, O_GB)
        gb_ref[...] = seg(O_GB, O_KR)
        cos = cos_ref[...]
        sin = sin_ref[...]
        kr = seg(O_KR, O_END)
        kpt_ref[...] = (kr[:, :128] * cos + kr[:, 128:] * sin).astype(BF16)

        cq = seg(O_CQ, O_CKV)
        cq_ref[...] = cq
        rq = lax.rsqrt(jnp.mean(cq * cq, axis=-1, keepdims=True) + EPS)
        cqn = (cq * rq * qg_ref[...]).astype(BF16)
        qa = _dot(cqn, wq_ref[...])
        qn_ref[...] = qa[:, :512].astype(BF16)
        qp_ref[...] = (qa[:, 512:768] * jnp.tile(cos, (1, 2)) + qa[:, 768:] * jnp.tile(sin, (1, 2))).astype(BF16)

        ckv = seg(O_CKV, O_MZ)
        ckv_ref[...] = ckv
        rk = lax.rsqrt(jnp.mean(ckv * ckv, axis=-1, keepdims=True) + EPS)
        ckvn = (ckv * rk * kvg_ref[...]).astype(BF16)
        kva = _dot(ckvn, wkv_ref[...])
        kn_ref[...] = kva[:, :512].astype(BF16)
        vv_ref[...] = kva[:, 512:].astype(BF16)

    outs = [
        (D_MODEL, BF16), (512, BF16), (512, BF16), (512, BF16), (512, F32), (Q_RANK, F32), (KV_RANK, F32),
        (512, F32), (D_MODEL, F32), (D_MODEL, F32), (128, BF16), (512, BF16), (256, BF16), (512, BF16), (512, BF16),
    ]
    return pl.pallas_call(
        body, name="inproj", grid=(s_len // tm,),
        out_shape=tuple(_sds((s_len, n), dt) for n, dt in outs),
        in_specs=[_rows(tm, D_MODEL), _whole((1, D_MODEL)), _whole((1, D_MODEL)), _whole((1, D_MODEL)),
                  _whole((2, W_INT, HALF_D)), _whole((Q_RANK, 1024)), _whole((KV_RANK, 1024)),
                  _whole((1, Q_RANK)), _whole((1, KV_RANK)), _rows(tm, LANES), _rows(tm, LANES)],
        out_specs=tuple(_rows(tm, n) for n, _ in outs),
        compiler_params=_params(("parallel",)),
    )(x, shift, scale, g1, w_int, w_q, w_kv, qg, kvg, cos128, sin128)


Z_CLAMP = 80.0 * LOG2E
RUN_CUTOFF = 110.0 * LOG2E


def _softplus_clamped(z):
    zc = jnp.minimum(z * LOG2E, Z_CLAMP)
    return zc, jnp.log2(1.0 + jnp.exp2(zc))


def _tri_sum(a, tri):
    return _dot(a.astype(BF16), tri)


def _sb_fwd_call(q, k, v):
    s_len = q.shape[0]
    tk = min(ATT_TILE, s_len)
    tq = min(SB_Q_TILES * ATT_TILE, s_len)
    r = tq // tk
    nq = s_len // tq

    def body(q_ref, k_ref, v_ref, o_ref, lt_ref, first_ref):
        i = pl.program_id(1)
        q2 = q_ref[...]
        lane = lax.broadcasted_iota(jnp.int32, (1, 256), 1)
        krow = lax.broadcasted_iota(jnp.int32, (tk, tk), 0)
        kcol = lax.broadcasted_iota(jnp.int32, (tk, tk), 1)
        row = lax.broadcasted_iota(jnp.int32, (tq, tk), 0)
        col = lax.broadcasted_iota(jnp.int32, (tq, tk), 1)
        later = (krow > kcol).astype(BF16)
        valids = [col + u * tk < row for u in range(r)]
        hms = [(lane // 64) == hh for hh in range(4)]
        qms = [jnp.where(hm, q2, jnp.zeros_like(q2)) for hm in hms]

        def block(j, carry, valid):
            runs, acc = list(carry[:4]), carry[4]
            off = pl.multiple_of(j * tk, tk)
            kb = k_ref[pl.ds(off, tk), :]
            vb = v_ref[pl.ds(off, tk), :]
            ws = []
            for hh in range(4):
                zc, sp = _softplus_clamped(_dot_nt(qms[hh], kb))
                lm = jnp.where(valid, sp, 0.0) if valid is not None else sp
                suf = _tri_sum(lm, later)
                w = jnp.exp2(zc - sp - suf - runs[hh])
                if valid is not None:
                    w = jnp.where(valid, w, 0.0)
                ws.append(w.astype(BF16))
                runs[hh] = runs[hh] + jnp.sum(lm, axis=1, keepdims=True)
            vstack = jnp.concatenate([jnp.where(hm, vb, jnp.zeros_like(vb)) for hm in hms], axis=0)
            acc = acc + _dot(jnp.concatenate(ws, axis=1), vstack)
            return (*runs, acc)

        zero = jnp.zeros((tq, 1), F32)
        carry = (zero, zero, zero, zero, jnp.zeros((tq, 256), F32))
        for u in reversed(range(r)):
            carry = block(i * r + u, carry, valids[u])

        def least_run(runs):
            return jnp.min(jnp.minimum(jnp.minimum(runs[0], runs[1]), jnp.minimum(runs[2], runs[3])))

        n_full = i * r

        def unfinished(state):
            return jnp.logical_and(state[0] < n_full, state[1] <= RUN_CUTOFF)

        def visit(state):
            cr = block(n_full - 1 - state[0], state[2:], None)
            return (state[0] + 1, least_run(cr[:4]), *cr)

        state = lax.while_loop(unfinished, visit, (jnp.int32(0), least_run(carry[:4]), *carry))
        carry = state[2:]
        first_ref[...] = jnp.full(first_ref.shape, n_full - state[0], jnp.int32)
        for hh in range(4):
            lt_ref[0, :, hh:hh + 1] = carry[hh]
        o_ref[...] = carry[4]

    return pl.pallas_call(
        body, name="sb_fwd", grid=(2, nq),
        out_shape=(_sds((s_len, SB_WIDTH), F32), _sds((2, s_len, 4), F32), _sds((2, nq, 8, 128), jnp.int32)),
        in_specs=[pl.BlockSpec((tq, 256), lambda g, i: (i, g)),
                  pl.BlockSpec((s_len, 256), lambda g, i: (0, g)),
                  pl.BlockSpec((s_len, 256), lambda g, i: (0, g))],
        out_specs=(pl.BlockSpec((tq, 256), lambda g, i: (i, g)),
                   pl.BlockSpec((1, tq, 4), lambda g, i: (g, i, 0)),
                   pl.BlockSpec((1, 1, 8, 128), lambda g, i: (g, i, 0, 0))),
        compiler_params=_params(("parallel", "parallel")),
    )(q, k, v)


def _sb_bwd_call(first, q, k, v, do, lt):
    s_len = q.shape[0]
    tk = min(ATT_TILE, s_len)
    tq = min(SB_Q_TILES * ATT_TILE, s_len)
    r = tq // tk
    nq = s_len // tq
    nq_fwd = first.shape[0] // 2
    per_fwd = nq // nq_fwd

    def body(first_ref, q_ref, k_ref, v_ref, do_ref, lt_ref, dq_ref, dk_ref, dv_ref):
        g = pl.program_id(0)
        i = pl.program_id(1)

        @pl.when(i == 0)
        def _():
            dk_ref[...] = jnp.zeros_like(dk_ref)
            dv_ref[...] = jnp.zeros_like(dv_ref)

        q2 = q_ref[...]
        do2 = do_ref[...].astype(BF16)
        lane = lax.broadcasted_iota(jnp.int32, (1, 256), 1)
        krow = lax.broadcasted_iota(jnp.int32, (tk, tk), 0)
        kcol = lax.broadcasted_iota(jnp.int32, (tk, tk), 1)
        row = lax.broadcasted_iota(jnp.int32, (tq, tk), 0)
        col = lax.broadcasted_iota(jnp.int32, (tq, tk), 1)
        earlier = (krow < kcol).astype(BF16)
        later = (krow > kcol).astype(BF16)
        valids = [col + u * tk < row for u in range(r)]
        hms = [(lane // 64) == hh for hh in range(4)]
        qms = [jnp.where(hm, q2, jnp.zeros_like(q2)) for hm in hms]
        doms = [jnp.where(hm, do2, jnp.zeros_like(do2)) for hm in hms]
        ltots = [lt_ref[0, :, hh:hh + 1] for hh in range(4)]
        q2t = jnp.transpose(q2.astype(F32))
        do2t = jnp.transpose(do_ref[...])
        subl = lax.broadcasted_iota(jnp.int32, (256, 1), 0)
        qtstack = jnp.concatenate(
            [jnp.where((subl // 64) == hh, q2t, 0.0).astype(BF16) for hh in range(4)], axis=1)
        dotstack = jnp.concatenate(
            [jnp.where((subl // 64) == hh, do2t, 0.0).astype(BF16) for hh in range(4)], axis=1)

        def block(j, carry, valid):
            lpre, ppre, dq = list(carry[0:4]), list(carry[4:8]), carry[8]
            off = pl.multiple_of(j * tk, tk)
            kb = k_ref[pl.ds(off, tk), :]
            vb = v_ref[pl.ds(off, tk), :]
            dzs, avs = [], []
            for hh in range(4):
                zc, sp = _softplus_clamped(_dot_nt(qms[hh], kb))
                lsig = zc - sp
                lm = jnp.where(valid, sp, 0.0) if valid is not None else sp
                rowsum = jnp.sum(lm, axis=1, keepdims=True)
                between = _tri_sum(lm, later) + ((ltots[hh] - lpre[hh]) - rowsum)
                a = jnp.exp2(lsig - between)
                if valid is not None:
                    a = jnp.where(valid, a, 0.0)
                p = a * _dot_nt(doms[hh], vb)
                pbefore = ppre[hh] + _tri_sum(p, earlier)
                dz = p - jnp.exp2(lsig) * (p + pbefore)
                if valid is not None:
                    dz = jnp.where(valid, dz, 0.0)
                dzs.append(dz.astype(BF16))
                avs.append(a.astype(BF16))
                lpre[hh] = lpre[hh] + rowsum
                ppre[hh] = ppre[hh] + jnp.sum(p, axis=1, keepdims=True)
            kstack = jnp.concatenate([jnp.where(hm, kb, jnp.zeros_like(kb)) for hm in hms], axis=0)
            dq = dq + _dot(jnp.concatenate(dzs, axis=1), kstack)
            dk_ref[:, pl.ds(off, tk)] += _dot(qtstack, jnp.concatenate(dzs, axis=0))
            dv_ref[:, pl.ds(off, tk)] += _dot(dotstack, jnp.concatenate(avs, axis=0))
            return (*lpre, *ppre, dq)

        zero = jnp.zeros((tq, 1), F32)
        start = jnp.minimum(first_ref[g * nq_fwd + i // per_fwd], i * r)
        carry = lax.fori_loop(start, i * r, lambda j, cr: block(j, cr, None),
                              (zero,) * 8 + (jnp.zeros((tq, 256), F32),))
        for u in range(r):
            carry = block(i * r + u, carry, valids[u])
        dq_ref[...] = carry[8].astype(BF16)

    return pl.pallas_call(
        body, name="sb_bwd",
        out_shape=(_sds((s_len, SB_WIDTH), BF16), _sds((SB_WIDTH, s_len), F32), _sds((SB_WIDTH, s_len), F32)),
        grid_spec=pltpu.PrefetchScalarGridSpec(
            num_scalar_prefetch=1, grid=(2, nq),
            in_specs=[pl.BlockSpec((tq, 256), lambda g, i, f: (i, g)),
                      pl.BlockSpec((s_len, 256), lambda g, i, f: (0, g)),
                      pl.BlockSpec((s_len, 256), lambda g, i, f: (0, g)),
                      pl.BlockSpec((tq, 256), lambda g, i, f: (i, g)),
                      pl.BlockSpec((1, tq, 4), lambda g, i, f: (g, i, 0))],
            out_specs=(pl.BlockSpec((tq, 256), lambda g, i, f: (i, g)),
                       pl.BlockSpec((256, s_len), lambda g, i, f: (g, 0)),
                       pl.BlockSpec((256, s_len), lambda g, i, f: (g, 0)))),
        compiler_params=_params(("parallel", "arbitrary")),
    )(first, q, k, v, do, lt)


def _mla_fwd_call(qn, qp, kn, kpt, v):
    s_len = qn.shape[0]
    tk = min(MLA_KEY_TILE, s_len)
    tq = min(FWD_Q_TILES * ATT_TILE, s_len)
    r = tq // tk
    nq = s_len // tq

    def body(qn_ref, qp_ref, kn_ref, kpt_ref, v_ref, o_ref, lse_ref):
        i = pl.program_id(1)
        qn2 = qn_ref[...]
        qp2 = qp_ref[...]
        lane256 = lax.broadcasted_iota(jnp.int32, (1, 256), 1)
        lane128 = lax.broadcasted_iota(jnp.int32, (1, 128), 1)
        krow = lax.broadcasted_iota(jnp.int32, (tk, tk), 0)
        kcol = lax.broadcasted_iota(jnp.int32, (tk, tk), 1)
        row = lax.broadcasted_iota(jnp.int32, (tq, tk), 0)
        col = lax.broadcasted_iota(jnp.int32, (tq, tk), 1)
        valids = [col + u * tk <= row for u in range(r)]
        m64s = [(lane256 // 64) == hh for hh in range(4)]
        half = [(lane128 // 64) == u for u in range(2)]
        m32s = [(lane128 // 32) == hh for hh in range(4)]
        qcs = []
        for hh in range(4):
            qpair = qn2[:, 128 * (hh // 2):128 * (hh // 2) + 128]
            qcs.append(jnp.concatenate([jnp.where(half[hh % 2], qpair, jnp.zeros_like(qpair)),
                                        jnp.where(m32s[hh], qp2, jnp.zeros_like(qp2))], axis=1))

        def by_head(vals):
            return jnp.where(m64s[0], vals[0], jnp.where(m64s[1], vals[1], jnp.where(m64s[2], vals[2], vals[3])))

        def block(j, carry, valid):
            ms, ls, acc = list(carry[0:4]), list(carry[4:8]), carry[8]
            off = pl.multiple_of(j * tk, tk)
            knb = kn_ref[pl.ds(off, tk), :]
            kpb = kpt_ref[pl.ds(off, tk), :]
            vb = v_ref[pl.ds(off, tk), :]
            kcs = [jnp.concatenate([knb[:, 128 * pp:128 * pp + 128], kpb], axis=1) for pp in range(2)]
            ps, alphas = [], []
            for hh in range(4):
                s = _dot_nt(qcs[hh], kcs[hh // 2]) * (MLA_SCALE * LOG2E)
                if valid is not None:
                    s = jnp.where(valid, s, -1e30)
                mn = jnp.maximum(ms[hh], jnp.max(s, axis=1, keepdims=True))
                p = jnp.exp2(s - mn)
                alpha = jnp.exp2(ms[hh] - mn)
                ls[hh] = alpha * ls[hh] + jnp.sum(p, axis=1, keepdims=True)
                ms[hh] = mn
                ps.append(p.astype(BF16))
                alphas.append(alpha)
            pvs = []
            for pp in range(2):
                vpair = vb[:, 128 * pp:128 * pp + 128]
                vstack = jnp.concatenate([jnp.where(hf, vpair, jnp.zeros_like(vpair)) for hf in half], axis=0)
                pvs.append(_dot(jnp.concatenate(ps[2 * pp:2 * pp + 2], axis=1), vstack))
            acc = by_head(alphas) * acc + jnp.concatenate(pvs, axis=1)
            return (*ms, *ls, acc)

        neg = jnp.full((tq, 1), -1e30, F32)
        zero = jnp.zeros((tq, 1), F32)
        carry = lax.fori_loop(0, i * r, lambda j, cr: block(j, cr, None),
                              (neg,) * 4 + (zero,) * 4 + (jnp.zeros((tq, 256), F32),))
        for u in range(r):
            carry = block(i * r + u, carry, valids[u])
        o_ref[...] = carry[8] / by_head(list(carry[4:8]))
        for hh in range(4):
            lse_ref[0, :, hh:hh + 1] = (carry[hh] + jnp.log2(carry[4 + hh])) * (1.0 / LOG2E)

    return pl.pallas_call(
        body, name="mla_fwd", grid=(2, nq),
        out_shape=(_sds((s_len, MLA_WIDTH), F32), _sds((2, s_len, 4), F32)),
        in_specs=[pl.BlockSpec((tq, 256), lambda g, i: (i, g)),
                  pl.BlockSpec((tq, 128), lambda g, i: (i, g)),
                  pl.BlockSpec((s_len, 256), lambda g, i: (0, g)),
                  pl.BlockSpec((s_len, 128), lambda g, i: (0, 0)),
                  pl.BlockSpec((s_len, 256), lambda g, i: (0, g))],
        out_specs=(pl.BlockSpec((tq, 256), lambda g, i: (i, g)),
                   pl.BlockSpec((1, tq, 4), lambda g, i: (g, i, 0))),
        compiler_params=_params(("parallel", "parallel")),
    )(qn, qp, kn, kpt, v)


def _mla_bwd_call(qn, qp, kn, kpt, v, o, do, lse):
    s_len = qn.shape[0]
    tk = min(MLA_KEY_TILE, s_len)
    tq = min(ATT_Q_TILES * ATT_TILE, s_len)
    r = tq // tk
    nq = s_len // tq

    def body(qn_ref, qp_ref, kn_ref, kpt_ref, v_ref, o_ref, do_ref, lse_ref,
             dqn_ref, dqp_ref, dkn_ref, dkpt_ref, dv_ref):
        g = pl.program_id(0)
        i = pl.program_id(1)

        @pl.when(i == 0)
        def _():
            dkn_ref[...] = jnp.zeros_like(dkn_ref)
            dv_ref[...] = jnp.zeros_like(dv_ref)

        @pl.when((i == 0) & (g == 0))
        def _():
            dkpt_ref[...] = jnp.zeros_like(dkpt_ref)

        qn2 = qn_ref[...]
        qp2 = qp_ref[...]
        dof = do_ref[...]
        dob = dof.astype(BF16)
        prod = dof * o_ref[...]
        lane256 = lax.broadcasted_iota(jnp.int32, (1, 256), 1)
        lane128 = lax.broadcasted_iota(jnp.int32, (1, 128), 1)
        krow = lax.broadcasted_iota(jnp.int32, (tk, tk), 0)
        kcol = lax.broadcasted_iota(jnp.int32, (tk, tk), 1)
        row = lax.broadcasted_iota(jnp.int32, (tq, tk), 0)
        col = lax.broadcasted_iota(jnp.int32, (tq, tk), 1)
        valids = [col + u * tk <= row for u in range(r)]
        m64s = [(lane256 // 64) == hh for hh in range(4)]
        half = [(lane128 // 64) == u for u in range(2)]
        m32s = [(lane128 // 32) == hh for hh in range(4)]
        qcs, doms = [], []
        for hh in range(4):
            sl = slice(128 * (hh // 2), 128 * (hh // 2) + 128)
            qpair = qn2[:, sl]
            dpair = dob[:, sl]
            qcs.append(jnp.concatenate([jnp.where(half[hh % 2], qpair, jnp.zeros_like(qpair)),
                                        jnp.where(m32s[hh], qp2, jnp.zeros_like(qp2))], axis=1))
            doms.append(jnp.where(half[hh % 2], dpair, jnp.zeros_like(dpair)))
        dsums = [jnp.sum(jnp.where(m64, prod, 0.0), axis=1, keepdims=True) * MLA_SCALE for m64 in m64s]
        lses = [lse_ref[0, :, hh:hh + 1] * LOG2E for hh in range(4)]
        qn2t = jnp.transpose(qn2.astype(F32))
        qp2t = jnp.transpose(qp2.astype(F32))
        do2t = jnp.transpose(dof)
        sub128 = lax.broadcasted_iota(jnp.int32, (128, 1), 0)
        qtstacks, dotstacks = [], []
        for pp in range(2):
            qts, dts = [], []
            for u in range(2):
                hh = 2 * pp + u
                qts.append(jnp.concatenate(
                    [jnp.where((sub128 // 64) == u, qn2t[128 * pp:128 * pp + 128, :], 0.0),
                     jnp.where((sub128 // 32) == hh, qp2t, 0.0)], axis=0).astype(BF16))
                dts.append(jnp.where((sub128 // 64) == u, do2t[128 * pp:128 * pp + 128, :], 0.0).astype(BF16))
            qtstacks.append(jnp.concatenate(qts, axis=1))
            dotstacks.append(jnp.concatenate(dts, axis=1))

        def block(j, carry, valid):
            dqn, dqp = carry
            off = pl.multiple_of(j * tk, tk)
            knb = kn_ref[pl.ds(off, tk), :]
            kpb = kpt_ref[pl.ds(off, tk), :]
            vb = v_ref[pl.ds(off, tk), :]
            dqn_parts = []
            dkp = None
            for pp in range(2):
                sl = slice(128 * pp, 128 * pp + 128)
                knp = knb[:, sl]
                vpair = vb[:, sl]
                kc = jnp.concatenate([knp, kpb], axis=1)
                dss, pbs, kcms = [], [], []
                for u in range(2):
                    hh = 2 * pp + u
                    s = _dot_nt(qcs[hh], kc) * (MLA_SCALE * LOG2E)
                    if valid is not None:
                        s = jnp.where(valid, s, -1e30)
                    p = jnp.exp2(s - lses[hh])
                    ds = p * (_dot_nt(doms[hh], vpair) * MLA_SCALE - dsums[hh])
                    dss.append(ds.astype(BF16))
                    pbs.append(p.astype(BF16))
                    kcms.append(jnp.concatenate([jnp.where(half[u], knp, jnp.zeros_like(knp)),
                                                 jnp.where(m32s[hh], kpb, jnp.zeros_like(kpb))], axis=1))
                dqc = _dot(jnp.concatenate(dss, axis=1), jnp.concatenate(kcms, axis=0))
                dqn_parts.append(dqc[:, :128])
                dqp = dqp + dqc[:, 128:]
                dkc = _dot(qtstacks[pp], jnp.concatenate(dss, axis=0))
                dkn_ref[128 * pp:128 * pp + 128, pl.ds(off, tk)] += dkc[:128, :]
                dkp = dkc[128:, :] if dkp is None else dkp + dkc[128:, :]
                dv_ref[128 * pp:128 * pp + 128, pl.ds(off, tk)] += _dot(dotstacks[pp], jnp.concatenate(pbs, axis=0))
            dqn = dqn + jnp.concatenate(dqn_parts, axis=1)
            dkpt_ref[:, pl.ds(off, tk)] += dkp
            return dqn, dqp

        carry = lax.fori_loop(0, i * r, lambda j, cr: block(j, cr, None),
                              (jnp.zeros((tq, 256), F32), jnp.zeros((tq, 128), F32)))
        for u in range(r):
            carry = block(i * r + u, carry, valids[u])
        dqn, dqp = carry
        dqn_ref[...] = dqn.astype(BF16)
        dqp_ref[...] = dqp.astype(BF16)

    return pl.pallas_call(
        body, name="mla_bwd", grid=(2, nq),
        out_shape=(_sds((s_len, 512), BF16), _sds((s_len, 256), BF16), _sds((512, s_len), F32),
                   _sds((128, s_len), F32), _sds((512, s_len), F32)),
        in_specs=[pl.BlockSpec((tq, 256), lambda g, i: (i, g)),
                  pl.BlockSpec((tq, 128), lambda g, i: (i, g)),
                  pl.BlockSpec((s_len, 256), lambda g, i: (0, g)),
                  pl.BlockSpec((s_len, 128), lambda g, i: (0, 0)),
                  pl.BlockSpec((s_len, 256), lambda g, i: (0, g)),
                  pl.BlockSpec((tq, 256), lambda g, i: (i, g)),
                  pl.BlockSpec((tq, 256), lambda g, i: (i, g)),
                  pl.BlockSpec((1, tq, 4), lambda g, i: (g, i, 0))],
        out_specs=(pl.BlockSpec((tq, 256), lambda g, i: (i, g)),
                   pl.BlockSpec((tq, 128), lambda g, i: (i, g)),
                   pl.BlockSpec((256, s_len), lambda g, i: (g, 0)),
                   pl.BlockSpec((128, s_len), lambda g, i: (0, 0)),
                   pl.BlockSpec((256, s_len), lambda g, i: (g, 0))),
        compiler_params=_params(("arbitrary", "arbitrary")),
    )(qn, qp, kn, kpt, v, o, do, lse)


def _post_call(x, tgt, oa, ob, sz, mz, ga, gb, gate, gf, wa, wb, wo):
    s_len = x.shape[0]
    tm = min(ROW_TILE, s_len)
    nstep = s_len // tm

    def body(x_ref, t_ref, oa_ref, ob_ref, sz_ref, mz_ref, ga_ref, gb_ref, gate_ref, gf_ref,
             wa_ref, wb_ref, wo_ref,
             dx2_ref, doa_ref, dob_ref, dsz_ref, dmz_ref, dga_ref, dgb_ref,
             dwo_out, dwa_out, dwb_out, dgf_ref, dgate_ref, loss_ref, dwo_ref, dwa_ref, dwb_ref):
        @pl.when(pl.program_id(0) == 0)
        def _():
            dwo_ref[...] = jnp.zeros_like(dwo_ref)
            dwa_ref[...] = jnp.zeros_like(dwa_ref)
            dwb_ref[...] = jnp.zeros_like(dwb_ref)
            dgf_ref[...] = jnp.zeros_like(dgf_ref)
            dgate_ref[...] = jnp.zeros_like(dgate_ref)
            loss_ref[...] = jnp.zeros_like(loss_ref)

        gate = gate_ref[...]
        gf = gf_ref[...]
        oa = oa_ref[...]
        ob = ob_ref[...]
        sz = sz_ref[...]
        mz = mz_ref[...]
        sa = _sigmoid(sz)
        sb = _sigmoid(mz)
        silu_a = sz * sa
        silu_b = mz * sb
        ua = (oa * silu_a).astype(BF16)
        ub = (ob * silu_b).astype(BF16)
        ya = _dot(ua, wa_ref[...])
        yb = _dot(ub, wb_ref[...])
        sga = _sigmoid(ga_ref[...])
        sgb = _sigmoid(gb_ref[...])
        merged = (sga * ya + sgb * yb).astype(BF16)
        out = _dot(merged, wo_ref[...])
        x2 = x_ref[...] + gate * out
        r2 = lax.rsqrt(jnp.mean(x2 * x2, axis=-1, keepdims=True) + EPS)
        xhat = x2 * r2
        err = xhat * gf - t_ref[...]
        loss_ref[...] += 0.5 * jnp.sum(jnp.sum(err * err, axis=1, keepdims=True), axis=0, keepdims=True) / D_MODEL
        dy = err * (1.0 / D_MODEL)
        dgf_ref[...] += jnp.sum(dy * xhat, axis=0, keepdims=True)
        dxhat = dy * gf
        dx2 = r2 * (dxhat - xhat * jnp.mean(dxhat * xhat, axis=-1, keepdims=True))
        dx2_ref[...] = dx2
        dgate_ref[...] += jnp.sum(dx2 * out, axis=0, keepdims=True)
        dout = (dx2 * gate).astype(BF16)
        dmerged = _dot_nt(dout, wo_ref[...])
        dwo_ref[...] += _dot_tn(merged, dout)
        dya = dmerged * sga
        dyb = dmerged * sgb
        dga_ref[...] = (dya * ya * (1.0 - sga)).astype(BF16)
        dgb_ref[...] = (dyb * yb * (1.0 - sgb)).astype(BF16)
        dyab = dya.astype(BF16)
        dybb = dyb.astype(BF16)
        dua = _dot_nt(dyab, wa_ref[...])
        dub = _dot_nt(dybb, wb_ref[...])
        dwa_ref[...] += _dot_tn(ua, dyab)
        dwb_ref[...] += _dot_tn(ub, dybb)
        doa_ref[...] = dua * silu_a
        dob_ref[...] = dub * silu_b
        dsz_ref[...] = (dua * oa * (sa * (1.0 + sz * (1.0 - sa)))).astype(BF16)
        dmz_ref[...] = (dub * ob * (sb * (1.0 + mz * (1.0 - sb)))).astype(BF16)

        @pl.when(pl.program_id(0) == nstep - 1)
        def _():
            dwo_out[...] = dwo_ref[...].astype(BF16)
            for k in range(N_CHIPS):
                dwa_out[k] = dwa_ref[:, 256 * k:256 * k + 256].astype(BF16)
                dwb_out[k] = dwb_ref[:, 256 * k:256 * k + 256].astype(BF16)

    return pl.pallas_call(
        body, name="post", grid=(nstep,),
        out_shape=(_sds((s_len, D_MODEL), F32), _sds((s_len, 512), F32), _sds((s_len, 512), F32),
                   _sds((s_len, 512), BF16), _sds((s_len, 512), BF16),
                   _sds((s_len, D_MODEL), BF16), _sds((s_len, D_MODEL), BF16),
                   _sds((D_MODEL, D_MODEL), BF16), _sds((N_CHIPS, 512, 256), BF16), _sds((N_CHIPS, 512, 256), BF16),
                   _sds((1, D_MODEL), F32), _sds((1, D_MODEL), F32), _sds((1, 128), F32)),
        in_specs=[_rows(tm, D_MODEL), _rows(tm, D_MODEL), _rows(tm, 512), _rows(tm, 512), _rows(tm, 512),
                  _rows(tm, 512), _rows(tm, D_MODEL), _rows(tm, D_MODEL), _whole((1, D_MODEL)), _whole((1, D_MODEL)),
                  _whole((512, D_MODEL)), _whole((512, D_MODEL)), _whole((D_MODEL, D_MODEL))],
        out_specs=(_rows(tm, D_MODEL), _rows(tm, 512), _rows(tm, 512), _rows(tm, 512), _rows(tm, 512),
                   _rows(tm, D_MODEL), _rows(tm, D_MODEL),
                   _whole((D_MODEL, D_MODEL)), _whole((N_CHIPS, 512, 256)), _whole((N_CHIPS, 512, 256)),
                   _whole((1, D_MODEL)), _whole((1, D_MODEL)), _whole((1, 128))),
        scratch_shapes=[pltpu.VMEM((D_MODEL, D_MODEL), F32), pltpu.VMEM((512, D_MODEL), F32),
                        pltpu.VMEM((512, D_MODEL), F32)],
        compiler_params=_params(("arbitrary",)),
    )(x, tgt, oa, ob, sz, mz, ga, gb, gate, gf, wa, wb, wo)


def _bwdprep_call(dsq, dsk, dsv, dsz, dqn, dqp, dkn, dvv, dkpt, dmz, dga, dgb, cq, ckv, cos128, sin128,
                  qg, kvg, w_q, w_kv):
    s_len = cq.shape[0]
    tm = min(ROW_TILE, s_len)

    def body(dsq_ref, dsk_ref, dsv_ref, dsz_ref, dqn_ref, dqp_ref, dkn_ref, dvv_ref, dkpt_ref, dmz_ref,
             dga_ref, dgb_ref, cq_ref, ckv_ref, cos_ref, sin_ref, qg_ref, kvg_ref, wq_ref, wkv_ref,
             dp_ref, dwq_ref, dwkv_ref, dqg_ref, dkvg_ref):
        @pl.when(pl.program_id(0) == 0)
        def _():
            dwq_ref[...] = jnp.zeros_like(dwq_ref)
            dwkv_ref[...] = jnp.zeros_like(dwkv_ref)
            dqg_ref[...] = jnp.zeros_like(dqg_ref)
            dkvg_ref[...] = jnp.zeros_like(dkvg_ref)

        cos = cos_ref[...]
        sin = sin_ref[...]
        dp_ref[:, O_SQ:O_SK] = dsq_ref[...] * jnp.asarray(SB_SCALE, BF16)
        dp_ref[:, O_SK:O_SV] = jnp.transpose(dsk_ref[...]).astype(BF16)
        dp_ref[:, O_SV:O_SZ] = jnp.transpose(dsv_ref[...]).astype(BF16)
        dp_ref[:, O_SZ:O_CQ] = dsz_ref[...]
        dp_ref[:, O_MZ:O_GA] = dmz_ref[...]
        dp_ref[:, O_GA:O_GB] = dga_ref[...]
        dp_ref[:, O_GB:O_KR] = dgb_ref[...]
        dkp = jnp.transpose(dkpt_ref[...])
        dp_ref[:, O_KR:O_KR + 128] = (dkp * cos).astype(BF16)
        dp_ref[:, O_KR + 128:O_END] = (dkp * sin).astype(BF16)
        dp_ref[:, O_END:W_INT] = jnp.zeros((tm, W_INT - O_END), BF16)

        cq = cq_ref[...]
        rq = lax.rsqrt(jnp.mean(cq * cq, axis=-1, keepdims=True) + EPS)
        cqh = cq * rq
        qg = qg_ref[...]
        cqn = (cqh * qg).astype(BF16)
        dqp = dqp_ref[...].astype(F32)
        dqa = jnp.concatenate([dqn_ref[...], (dqp * jnp.tile(cos, (1, 2))).astype(BF16),
                               (dqp * jnp.tile(sin, (1, 2))).astype(BF16)], axis=1)
        dcqn = _dot_nt(dqa, wq_ref[...])
        dwq_ref[...] += _dot_tn(cqn, dqa)
        dqg_ref[...] += jnp.sum(dcqn * cqh, axis=0, keepdims=True)
        dh = dcqn * qg
        dcq = rq * (dh - cqh * jnp.mean(dh * cqh, axis=-1, keepdims=True))
        dp_ref[:, O_CQ:O_CKV] = dcq.astype(BF16)

        ckv = ckv_ref[...]
        rk = lax.rsqrt(jnp.mean(ckv * ckv, axis=-1, keepdims=True) + EPS)
        ckh = ckv * rk
        kvg = kvg_ref[...]
        ckvn = (ckh * kvg).astype(BF16)
        dkva = jnp.concatenate([jnp.transpose(dkn_ref[...]).astype(BF16),
                                jnp.transpose(dvv_ref[...]).astype(BF16)], axis=1)
        dckvn = _dot_nt(dkva, wkv_ref[...])
        dwkv_ref[...] += _dot_tn(ckvn, dkva)
        dkvg_ref[...] += jnp.sum(dckvn * ckh, axis=0, keepdims=True)
        dh2 = dckvn * kvg
        dckv = rk * (dh2 - ckh * jnp.mean(dh2 * ckh, axis=-1, keepdims=True))
        dp_ref[:, O_CKV:O_MZ] = dckv.astype(BF16)

    return pl.pallas_call(
        body, name="bwdprep", grid=(s_len // tm,),
        out_shape=(_sds((s_len, W_INT), BF16), _sds((Q_RANK, 1024), F32), _sds((KV_RANK, 1024), F32),
                   _sds((1, Q_RANK), F32), _sds((1, KV_RANK), F32)),
        in_specs=[_rows(tm, 512), _cols(512, tm), _cols(512, tm), _rows(tm, 512), _rows(tm, 512), _rows(tm, 256),
                  _cols(512, tm), _cols(512, tm), _cols(128, tm), _rows(tm, 512), _rows(tm, D_MODEL),
                  _rows(tm, D_MODEL), _rows(tm, Q_RANK), _rows(tm, KV_RANK), _rows(tm, LANES), _rows(tm, LANES),
                  _whole((1, Q_RANK)), _whole((1, KV_RANK)), _whole((Q_RANK, 1024)), _whole((KV_RANK, 1024))],
        out_specs=(_rows(tm, W_INT), _whole((Q_RANK, 1024)), _whole((KV_RANK, 1024)),
                   _whole((1, Q_RANK)), _whole((1, KV_RANK))),
        compiler_params=_params(("arbitrary",)),
    )(dsq, dsk, dsv, dsz, dqn, dqp, dkn, dvv, dkpt, dmz, dga, dgb, cq, ckv, cos128, sin128, qg, kvg, w_q, w_kv)


def _dh_call(dproj, w_int_t, x, dx2, scale, g1):
    s_len = x.shape[0]
    tm = min(2 * ROW_TILE, s_len)

    def body(dp_ref, wt_ref, x_ref, dx2_ref, sc_ref, g1_ref, gx_ref, dsh_ref, dsc_ref, dg1_ref):
        @pl.when(pl.program_id(0) == 0)
        def _():
            dsh_ref[...] = jnp.zeros_like(dsh_ref)
            dsc_ref[...] = jnp.zeros_like(dsc_ref)
            dg1_ref[...] = jnp.zeros_like(dg1_ref)

        dp = dp_ref[...]
        dh = jnp.concatenate([_dot(dp, wt_ref[0]), _dot(dp, wt_ref[1])], axis=1)
        xt = x_ref[...]
        r = lax.rsqrt(jnp.mean(xt * xt, axis=-1, keepdims=True) + EPS)
        xh = xt * r
        g1 = g1_ref[...]
        xg = xh * g1
        dsh_ref[...] += jnp.sum(dh, axis=0, keepdims=True)
        dsc_ref[...] += jnp.sum(dh * xg, axis=0, keepdims=True)
        dxg = dh * (1.0 + sc_ref[...])
        dg1_ref[...] += jnp.sum(dxg * xh, axis=0, keepdims=True)
        dxh = dxg * g1
        gx_ref[...] = dx2_ref[...] + r * (dxh - xh * jnp.mean(dxh * xh, axis=-1, keepdims=True))

    return pl.pallas_call(
        body, name="dh", grid=(s_len // tm,),
        out_shape=(_sds((s_len, D_MODEL), F32), _sds((1, D_MODEL), F32), _sds((1, D_MODEL), F32),
                   _sds((1, D_MODEL), F32)),
        in_specs=[_rows(tm, W_INT), _whole((2, W_INT, HALF_D)), _rows(tm, D_MODEL), _rows(tm, D_MODEL),
                  _whole((1, D_MODEL)), _whole((1, D_MODEL))],
        out_specs=(_rows(tm, D_MODEL), _whole((1, D_MODEL)), _whole((1, D_MODEL)), _whole((1, D_MODEL))),
        compiler_params=_params(("arbitrary",)),
    )(dproj, w_int_t, x, dx2, scale, g1)


def _small_call(svg, ct, dmod_sh):
    def body(sv_ref, ct_ref, dm_ref, tot_ref, gwada_ref):
        acc = sv_ref[0:1, :]
        for d in range(1, N_DEV):
            acc = acc + sv_ref[d:d + 1, :]
        tot_ref[...] = acc
        gwada_ref[...] = lax.dot_general(ct_ref[...], dm_ref[...], (((1,), (0,)), ((), ())),
                                         precision=lax.Precision.HIGHEST, preferred_element_type=F32)

    vmem = pl.BlockSpec(memory_space=pltpu.VMEM)
    return pl.pallas_call(
        body, name="small_grads",
        out_shape=(_sds((1, 8 * SV_COLS), F32), _sds((D_MODEL, 768), F32)),
        in_specs=[vmem, vmem, vmem], out_specs=(vmem, vmem),
        compiler_params=_params(),
    )(svg, ct, dmod_sh)


def _adamw_tile_rows(rows, cols):
    budget = 2 << 20
    if rows * cols * 4 <= budget or rows % 8:
        return rows
    best = 8
    for tr in range(8, rows + 1, 8):
        if rows % tr == 0 and tr * cols * 4 <= budget:
            best = tr
    return best


def _adamw_math(w, gg, m, v):
    m2 = ADAM_B1 * m + (1.0 - ADAM_B1) * gg
    v2 = ADAM_B2 * v + (1.0 - ADAM_B2) * (gg * gg)
    m_hat = m2 / (1.0 - ADAM_B1 ** ADAM_STEP)
    v_hat = v2 / (1.0 - ADAM_B2 ** ADAM_STEP)
    return -ADAM_LR * (m_hat / (jnp.sqrt(v_hat) + ADAM_EPS) + ADAM_WD * w), m2, v2


def _adamw_call(name, w, g, m, v):
    rows, cols = w.shape
    tr = _adamw_tile_rows(rows, cols)
    halves = g.ndim == 3

    def body(w_ref, g_ref, m_ref, v_ref, *out_refs):
        if halves:
            gg = jnp.concatenate([g_ref[0], g_ref[1]], axis=1)
            out_refs[0][...] = gg
        else:
            gg = g_ref[...]
        d_ref, nm_ref, nv_ref = out_refs[-3:]
        d_ref[...], nm_ref[...], nv_ref[...] = _adamw_math(w_ref[...], gg, m_ref[...], v_ref[...])

    spec = pl.BlockSpec((tr, cols), lambda i: (i, 0))
    g_spec = pl.BlockSpec((2, tr, cols // 2), lambda i: (0, i, 0)) if halves else spec
    n_out = 4 if halves else 3
    outs = pl.pallas_call(
        body, name="adamw_" + name, grid=(rows // tr,),
        out_shape=(_sds((rows, cols), F32),) * n_out,
        in_specs=[spec, g_spec, spec, spec], out_specs=(spec,) * n_out,
        compiler_params=_params(("parallel",)),
    )(w, g, m, v)
    return tuple(outs) if halves else (g,) + tuple(outs)


IN_SHARD = IN_WIDTH // N_CHIPS
HALF_D = D_MODEL // 2
SMALL_ROWS = (576, 512, 1024, 1024, 2048)
SMALL_TOTAL = sum(SMALL_ROWS)
SMALL_HALF = SMALL_TOTAL // 2
SMALL_SUM_ROWS = 432


def _gather_call(c_row, w_ada_sh, pack_in, pack_small):
    def body(c_ref, wada_ref, pki_ref, pks_ref, mg_ref, cg_ref, gwi_ref, gws_ref,
             cv, ssem_c, rsem_c, ssem_m, rsem_m, ssem_w, rsem_w, ssem_f, rsem_f, lsem):
        x, y, c = lax.axis_index("x"), lax.axis_index("y"), lax.axis_index("c")
        me = 4 * x + 2 * y + c
        chip = 2 * x + y
        rel3 = [(1, 0), (0, 1), (1, 1)]
        packs = [(pki_ref, gwi_ref), (pks_ref, gws_ref)]

        def slot(a, gw, k, h):
            return gw.at[h, k] if a == 0 else gw.at[k, h]

        sends = []
        for j, (dx, dy) in enumerate(rel3):
            for a, (pk, gw) in enumerate(packs):
                cp = pltpu.make_async_remote_copy(
                    src_ref=pk.at[c], dst_ref=slot(a, gw, chip, c), send_sem=ssem_w.at[j, a],
                    recv_sem=rsem_w.at[j, a], device_id=(_flip(x, dx), _flip(y, dy), c), device_id_type=MESH)
                cp.start()
                sends.append(cp)
        owns = []
        for a, (pk, gw) in enumerate(packs):
            for h in range(2):
                own = pltpu.make_async_copy(pk.at[h], slot(a, gw, chip, h), lsem.at[a, h])
                own.start()
                owns.append(own)

        cv[me] = c_ref[...]
        for r in range(1, N_DEV):
            dx, dy, dc = (r >> 2) & 1, (r >> 1) & 1, r & 1
            cp = pltpu.make_async_remote_copy(
                src_ref=c_ref, dst_ref=cv.at[me], send_sem=ssem_c.at[r - 1], recv_sem=rsem_c.at[r - 1],
                device_id=(_flip(x, dx), _flip(y, dy), _flip(c, dc)), device_id_type=MESH)
            cp.start()
            sends.append(cp)
        for r in range(1, N_DEV):
            dx, dy, dc = (r >> 2) & 1, (r >> 1) & 1, r & 1
            src = 4 * _flip(x, dx) + 2 * _flip(y, dy) + _flip(c, dc)
            pltpu.make_async_remote_copy(
                src_ref=c_ref, dst_ref=cv.at[src], send_sem=ssem_c.at[r - 1], recv_sem=rsem_c.at[r - 1],
                device_id=(x, y, c), device_id_type=MESH).wait_recv()
        rows = lax.broadcasted_iota(jnp.int32, (N_DEV, D_MODEL), 0)
        call = jnp.zeros((N_DEV, D_MODEL), F32)
        for b in range(N_DEV):
            call = jnp.where(rows == b, jnp.broadcast_to(cv[b], (N_DEV, D_MODEL)), call)
        cg_ref[...] = call

        mg_ref[chip] = lax.dot_general(call, wada_ref[...], (((1,), (0,)), ((), ())),
                                       precision=lax.Precision.HIGHEST, preferred_element_type=F32)
        for j, (dx, dy) in enumerate(rel3):
            cp = pltpu.make_async_remote_copy(
                src_ref=mg_ref.at[chip], dst_ref=mg_ref.at[chip], send_sem=ssem_m.at[j], recv_sem=rsem_m.at[j],
                device_id=(_flip(x, dx), _flip(y, dy), c), device_id_type=MESH)
            cp.start()
            sends.append(cp)
        for j, (dx, dy) in enumerate(rel3):
            src_chip = 2 * _flip(x, dx) + _flip(y, dy)
            pltpu.make_async_remote_copy(
                src_ref=mg_ref.at[src_chip], dst_ref=mg_ref.at[src_chip], send_sem=ssem_m.at[j],
                recv_sem=rsem_m.at[j], device_id=(x, y, c), device_id_type=MESH).wait_recv()
        for j, (dx, dy) in enumerate(rel3):
            src_chip = 2 * _flip(x, dx) + _flip(y, dy)
            for a, (pk, gw) in enumerate(packs):
                pltpu.make_async_remote_copy(
                    src_ref=pk.at[c], dst_ref=slot(a, gw, src_chip, c), send_sem=ssem_w.at[j, a],
                    recv_sem=rsem_w.at[j, a], device_id=(x, y, c), device_id_type=MESH).wait_recv()
                cp = pltpu.make_async_remote_copy(
                    src_ref=slot(a, gw, src_chip, c), dst_ref=slot(a, gw, src_chip, c), send_sem=ssem_f.at[j, a],
                    recv_sem=rsem_f.at[j, a], device_id=(x, y, 1 - c), device_id_type=MESH)
                cp.start()
                sends.append(cp)
        for j, (dx, dy) in enumerate(rel3):
            src_chip = 2 * _flip(x, dx) + _flip(y, dy)
            for a, (pk, gw) in enumerate(packs):
                pltpu.make_async_remote_copy(
                    src_ref=pk.at[c], dst_ref=slot(a, gw, src_chip, 1 - c), send_sem=ssem_f.at[j, a],
                    recv_sem=rsem_f.at[j, a], device_id=(x, y, c), device_id_type=MESH).wait_recv()
        for cp in sends:
            cp.wait_send()
        for own in owns:
            own.wait()

    vmem = pl.BlockSpec(memory_space=pltpu.VMEM)
    return pl.pallas_call(
        body, name="gather_fwd",
        out_shape=(_sds((N_CHIPS, N_DEV, 768), F32), _sds((N_DEV, D_MODEL), F32),
                   _sds((2, N_CHIPS, IN_SHARD, HALF_D), BF16), _sds((N_CHIPS, 2, SMALL_HALF, LANES), BF16)),
        in_specs=[vmem, vmem, vmem, vmem], out_specs=(vmem, vmem, vmem, vmem),
        scratch_shapes=[
            pltpu.VMEM((N_DEV, 1, D_MODEL), F32),
            pltpu.SemaphoreType.DMA((N_DEV - 1,)), pltpu.SemaphoreType.DMA((N_DEV - 1,)),
            pltpu.SemaphoreType.DMA((3,)), pltpu.SemaphoreType.DMA((3,)),
            pltpu.SemaphoreType.DMA((3, 2)), pltpu.SemaphoreType.DMA((3, 2)),
            pltpu.SemaphoreType.DMA((3, 2)), pltpu.SemaphoreType.DMA((3, 2)),
            pltpu.SemaphoreType.DMA((2, 2)),
        ],
        compiler_params=_params(),
    )(c_row, w_ada_sh, pack_in, pack_small)


def _reduce_call(g_in, g_small, sv):
    def body(gi_ref, gs_ref, sv_ref, fi_ref, fs_ref, svg_ref, pair_i, pair_s, send_i, send_s, land_i, land_s,
             ssem_p, rsem_p, ssem_g, rsem_g, ssem_s, rsem_s, ssem_x, rsem_x):
        x, y, c = lax.axis_index("x"), lax.axis_index("y"), lax.axis_index("c")
        me = 4 * x + 2 * y + c
        chip = 2 * x + y
        rel3 = [(1, 0), (0, 1), (1, 1)]
        payloads = [(gi_ref, pair_i, send_i, land_i, fi_ref), (gs_ref, pair_s, send_s, land_s, fs_ref)]
        where = [lambda k, h: N_CHIPS * h + k, lambda k, h: 2 * k + h]
        copies = []

        for k in range(N_CHIPS):
            for a, (g, pair, _, _, _) in enumerate(payloads):
                cp = pltpu.make_async_remote_copy(
                    src_ref=g.at[where[a](k, 1 - c)], dst_ref=pair.at[k], send_sem=ssem_p.at[k, a],
                    recv_sem=rsem_p.at[k, a], device_id=(x, y, 1 - c), device_id_type=MESH)
                cp.start()
                copies.append(cp)

        for r in range(1, N_DEV):
            dx, dy, dc = (r >> 2) & 1, (r >> 1) & 1, r & 1
            cp = pltpu.make_async_remote_copy(
                src_ref=sv_ref, dst_ref=svg_ref.at[me], send_sem=ssem_s.at[r - 1], recv_sem=rsem_s.at[r - 1],
                device_id=(_flip(x, dx), _flip(y, dy), _flip(c, dc)), device_id_type=MESH)
            cp.start()
            copies.append(cp)
        svg_ref[me] = sv_ref[...]

        def pair_sum(k, store_in, store_small):
            for a, (g, pair, _, _, _) in enumerate(payloads):
                pltpu.make_async_remote_copy(
                    src_ref=g.at[where[a](k, c)], dst_ref=pair.at[k], send_sem=ssem_p.at[k, a],
                    recv_sem=rsem_p.at[k, a], device_id=(x, y, c), device_id_type=MESH).wait_recv()
            for qd in range(HALF_D // LANES):
                sl = slice(LANES * qd, LANES * qd + LANES)
                store_in(sl, gi_ref[where[0](k, c), :, sl].astype(F32) + pair_i[k, :, sl].astype(F32))

            def rows(i, carry):
                sl = pl.ds(pl.multiple_of(i * SMALL_SUM_ROWS, 16), SMALL_SUM_ROWS)
                store_small(sl, gs_ref[where[1](k, c), sl, :].astype(F32) + pair_s[k, sl, :].astype(F32))
                return carry

            lax.fori_loop(0, SMALL_HALF // SMALL_SUM_ROWS, rows, 0)

        for j, (dx, dy) in enumerate(rel3):
            tx, ty = _flip(x, dx), _flip(y, dy)

            def put_in(sl, val, j=j):
                send_i[j, :, sl] = val.astype(BF16)

            def put_small(sl, val, j=j):
                send_s[j, sl, :] = val.astype(BF16)

            pair_sum(2 * tx + ty, put_in, put_small)
            for a, (_, _, send, land, _) in enumerate(payloads):
                cp = pltpu.make_async_remote_copy(
                    src_ref=send.at[j], dst_ref=land.at[j], send_sem=ssem_g.at[j, a], recv_sem=rsem_g.at[j, a],
                    device_id=(tx, ty, c), device_id_type=MESH)
                cp.start()
                copies.append(cp)

        def own_in(sl, val):
            fi_ref[c, :, sl] = val

        def own_small(sl, val):
            fs_ref[c, sl, :] = val

        pair_sum(chip, own_in, own_small)
        for j in range(3):
            for a, (_, _, send, land, _) in enumerate(payloads):
                pltpu.make_async_remote_copy(
                    src_ref=send.at[j], dst_ref=land.at[j], send_sem=ssem_g.at[j, a], recv_sem=rsem_g.at[j, a],
                    device_id=(x, y, c), device_id_type=MESH).wait_recv()
            for qd in range(HALF_D // LANES):
                sl = slice(LANES * qd, LANES * qd + LANES)
                fi_ref[c, :, sl] += land_i[j, :, sl].astype(F32)

            def add_rows(i, carry, j=j):
                sl = pl.ds(pl.multiple_of(i * SMALL_SUM_ROWS, 16), SMALL_SUM_ROWS)
                fs_ref[c, sl, :] += land_s[j, sl, :].astype(F32)
                return carry

            lax.fori_loop(0, SMALL_HALF // SMALL_SUM_ROWS, add_rows, 0)

        for a, f in enumerate((fi_ref, fs_ref)):
            cp = pltpu.make_async_remote_copy(
                src_ref=f.at[c], dst_ref=f.at[c], send_sem=ssem_x.at[a], recv_sem=rsem_x.at[a],
                device_id=(x, y, 1 - c), device_id_type=MESH)
            cp.start()
            copies.append(cp)
        for a, f in enumerate((fi_ref, fs_ref)):
            pltpu.make_async_remote_copy(
                src_ref=f.at[c], dst_ref=f.at[1 - c], send_sem=ssem_x.at[a], recv_sem=rsem_x.at[a],
                device_id=(x, y, c), device_id_type=MESH).wait_recv()
        for r in range(1, N_DEV):
            dx, dy, dc = (r >> 2) & 1, (r >> 1) & 1, r & 1
            src = 4 * _flip(x, dx) + 2 * _flip(y, dy) + _flip(c, dc)
            pltpu.make_async_remote_copy(
                src_ref=sv_ref, dst_ref=svg_ref.at[src], send_sem=ssem_s.at[r - 1],
                recv_sem=rsem_s.at[r - 1], device_id=(x, y, c), device_id_type=MESH).wait_recv()
        for cp in copies:
            cp.wait_send()

    vmem = pl.BlockSpec(memory_space=pltpu.VMEM)
    return pl.pallas_call(
        body, name="grad_reduce",
        out_shape=(_sds((2, IN_SHARD, HALF_D), F32), _sds((2, SMALL_HALF, LANES), F32),
                   _sds((N_DEV, 8, SV_COLS), F32)),
        in_specs=[vmem, vmem, vmem], out_specs=(vmem, vmem, vmem),
        scratch_shapes=[
            pltpu.VMEM((N_CHIPS, IN_SHARD, HALF_D), BF16), pltpu.VMEM((N_CHIPS, SMALL_HALF, LANES), BF16),
            pltpu.VMEM((3, IN_SHARD, HALF_D), BF16), pltpu.VMEM((3, SMALL_HALF, LANES), BF16),
            pltpu.VMEM((3, IN_SHARD, HALF_D), BF16), pltpu.VMEM((3, SMALL_HALF, LANES), BF16),
            pltpu.SemaphoreType.DMA((N_CHIPS, 2)), pltpu.SemaphoreType.DMA((N_CHIPS, 2)),
            pltpu.SemaphoreType.DMA((3, 2)), pltpu.SemaphoreType.DMA((3, 2)),
            pltpu.SemaphoreType.DMA((N_DEV - 1,)), pltpu.SemaphoreType.DMA((N_DEV - 1,)),
            pltpu.SemaphoreType.DMA((2,)), pltpu.SemaphoreType.DMA((2,)),
        ],
        compiler_params=_params(),
    )(g_in, g_small, sv)


def _dwin_call(h, dproj):
    s_len = h.shape[0]
    tm = min(4 * ROW_TILE, s_len)
    nrow = s_len // tm
    nc = 4
    chunk = W_INT // nc

    def body(h_ref, dp_ref, dw_ref, acc):
        c, i = pl.program_id(0), pl.program_id(1)

        @pl.when(i == 0)
        def _():
            acc[...] = jnp.zeros_like(acc)

        acc[...] += _dot_tn(dp_ref[...], h_ref[...])

        def put(lo, hi, dst):
            for half in range(2):
                dw_ref[half, dst:dst + hi - lo, :] = acc[lo:hi, half * HALF_D:(half + 1) * HALF_D].astype(BF16)

        for cc in range(nc):
            @pl.when((c == cc) & (i == nrow - 1))
            def _(cc=cc):
                base = cc * chunk
                lo, hi = base, min(base + chunk, O_MZ)
                if lo < hi:
                    put(lo - base, hi - base, lo)
                lo, hi = max(base, O_MZ), min(base + chunk, O_KR)
                if lo < hi:
                    put(lo - base, hi - base, lo + ROPE_DIM)
                if base <= O_KR and O_END <= base + chunk:
                    r = O_KR - base
                    half_r = ROPE_DIM // 2
                    kr = sum(acc[r + ROPE_DIM * j:r + ROPE_DIM * (j + 1), :] for j in range(4))
                    sw = sum(acc[r + 128 + ROPE_DIM * j:r + 128 + ROPE_DIM * (j + 1), :] for j in range(4))
                    tot = kr + jnp.concatenate([sw[half_r:], sw[:half_r]], axis=0)
                    for half in range(2):
                        dw_ref[half, O_MZ:O_MZ + ROPE_DIM, :] = (
                            tot[:, half * HALF_D:(half + 1) * HALF_D].astype(BF16))

    assert O_KR >= (nc - 1) * chunk
    return pl.pallas_call(
        body, name="dwin", grid=(nc, nrow),
        out_shape=_sds((2, IN_WIDTH, HALF_D), BF16),
        in_specs=[pl.BlockSpec((tm, D_MODEL), lambda c, i: (i, 0)),
                  pl.BlockSpec((tm, chunk), lambda c, i: (i, c))],
        out_specs=pl.BlockSpec((2, IN_WIDTH, HALF_D), lambda c, i: (0, 0, 0)),
        scratch_shapes=[pltpu.VMEM((chunk, D_MODEL), F32)],
        compiler_params=_params(("arbitrary", "arbitrary")),
    )(h, dproj)


def _internal_weights(w_in_t, w_uq, w_ukv):
    krot_t = w_in_t[:, 2688:2720]
    krot_sw = krot_t.reshape(2, 2, 16, HALF_D)[:, ::-1].reshape(2, 32, HALF_D)
    w_int_t = jnp.concatenate([
        w_in_t[:, 0:2688], w_in_t[:, 2720:5280], jnp.tile(krot_t, (1, 4, 1)), jnp.tile(krot_sw, (1, 4, 1)),
        jnp.zeros((2, W_INT - O_END, HALF_D), w_in_t.dtype)], axis=1)
    uq = w_uq.reshape(Q_RANK, N_HEADS, 96)
    wp = uq[:, :, 64:].reshape(Q_RANK, 256)
    w_q = jnp.concatenate([uq[:, :, :64].reshape(Q_RANK, 512), wp, _swap_halves(wp, 32)], axis=1)
    ukv = w_ukv.reshape(KV_RANK, N_HEADS, 128)
    w_kv = jnp.concatenate([ukv[:, :, :64].reshape(KV_RANK, 512), ukv[:, :, 64:].reshape(KV_RANK, 512)], axis=1)
    return w_int_t, w_q, w_kv


def _true_weight_grads(dwq, dwkv):
    dwp = dwq[:, 512:768] + _swap_halves(dwq[:, 768:1024], 32)
    g_uq = jnp.concatenate([dwq[:, :512].reshape(Q_RANK, N_HEADS, 64), dwp.reshape(Q_RANK, N_HEADS, 32)],
                           axis=2).reshape(Q_RANK, 768)
    g_ukv = jnp.concatenate([dwkv[:, :512].reshape(KV_RANK, N_HEADS, 64), dwkv[:, 512:].reshape(KV_RANK, N_HEADS, 64)],
                            axis=2).reshape(KV_RANK, 1024)
    return g_uq, g_ukv


def _swap_halves(w, group):
    r, n = w.shape
    return w.reshape(r, n // group, 2, group // 2)[:, :, ::-1, :].reshape(r, n)


def _pack_shards(parts):
    return jnp.concatenate([p.reshape(-1, LANES) for p in parts], axis=0)


def _unpack_small(gw):
    offs = [0]
    for r in SMALL_ROWS:
        offs.append(offs[-1] + r)

    def cols(i, rows, shard_cols):
        blk = gw[:, offs[i]:offs[i + 1]].reshape(N_CHIPS, rows, shard_cols)
        return blk.transpose(1, 0, 2).reshape(rows, N_CHIPS * shard_cols)

    return (cols(0, Q_RANK, 192), cols(1, KV_RANK, 256), cols(2, 512, 256), cols(3, 512, 256),
            gw[:, offs[4]:offs[5]].reshape(D_MODEL, D_MODEL))


def _chip_major(g, shard_cols):
    r = g.shape[0]
    return g.reshape(r, N_CHIPS, shard_cols).transpose(1, 0, 2).reshape(N_CHIPS, -1, LANES)


def kernel(x, c, positions, w_ada, b_ada, norm_gain, w_in, q_norm_gain, w_uq, kv_norm_gain, w_ukv, w_branch_a, w_branch_b, w_out, final_norm_gain, loss_target, m_w_ada, m_b_ada, m_norm_gain, m_w_in, m_q_norm_gain, m_w_uq, m_kv_norm_gain, m_w_ukv, m_w_branch_a, m_w_branch_b, m_w_out, m_final_norm_gain, v_w_ada, v_b_ada, v_norm_gain, v_w_in, v_q_norm_gain, v_w_uq, v_kv_norm_gain, v_w_ukv, v_w_branch_a, v_w_branch_b, v_w_out, v_final_norm_gain):
    ix, iy, ic = lax.axis_index("x"), lax.axis_index("y"), lax.axis_index("c")
    me = 4 * ix + 2 * iy + ic
    chip = 2 * ix + iy
    xs = x[0]
    tgt = loss_target[0]
    s_len = xs.shape[0]

    w_in_t = jnp.swapaxes(w_in[0], 0, 1)
    w_in_tb = w_in_t.astype(BF16)
    pack_in = jnp.stack([w_in_tb[:, :HALF_D], w_in_tb[:, HALF_D:]], axis=0)
    small_shards = (w_uq[0], w_ukv[0], w_branch_a[0], w_branch_b[0], w_out[0])
    pack_small = _pack_shards([s.astype(BF16) for s in small_shards]).reshape(2, SMALL_HALF, LANES)
    mg, call, gw_in, gw_small = _gather_call(c, w_ada[0], pack_in, pack_small)
    mod = mg.transpose(1, 0, 2).reshape(N_DEV, 3 * D_MODEL) + b_ada
    mod_me = lax.dynamic_slice_in_dim(mod, me, 1, axis=0)
    shift, scale, gate = mod_me[:, :D_MODEL], mod_me[:, D_MODEL:2 * D_MODEL], mod_me[:, 2 * D_MODEL:]

    f_in_t = gw_in.reshape(2, IN_WIDTH, HALF_D)
    f_uq, f_ukv, f_a, f_b, f_out = _unpack_small(gw_small.reshape(N_CHIPS, SMALL_TOTAL, LANES))
    w_int_t, w_q, w_kv = _internal_weights(f_in_t, f_uq, f_ukv)

    inv_freq = ROPE_BASE ** (-jnp.arange(0, ROPE_DIM, 2, dtype=F32) / ROPE_DIM)
    ang = positions[0].astype(F32)[:, None] * inv_freq
    cs, sn = jnp.cos(ang), jnp.sin(ang)
    cos128 = jnp.tile(jnp.concatenate([cs, cs], axis=1), (1, 4))
    sin128 = jnp.tile(jnp.concatenate([-sn, sn], axis=1), (1, 4))

    (h, sq, sk, sv, sz, cq, ckv, mz, ga, gb, kpt, qn, qp, kn, vv) = _inproj_call(
        xs, shift, scale, norm_gain, w_int_t, w_q, w_kv, q_norm_gain, kv_norm_gain, cos128, sin128)
    oa, lt, first = _sb_fwd_call(sq, sk, sv)
    ob, lse = _mla_fwd_call(qn, qp, kn, kpt, vv)

    gf = final_norm_gain.reshape(1, D_MODEL)
    (dx2, doa, dob, dsz, dmz, dga, dgb, dwo, dwa, dwb, dgf, dgate, loss_p) = _post_call(
        xs, tgt, oa, ob, sz, mz, ga, gb, gate, gf, f_a, f_b, f_out)

    dsq, dsk_t, dsv_t = _sb_bwd_call(first[:, :, 0, 0].reshape(-1), sq, sk, sv, doa, lt)
    dqn, dqp, dkn_t, dkpt_t, dvv_t = _mla_bwd_call(qn, qp, kn, kpt, vv, ob, dob, lse)

    dproj, dwq, dwkv, dqg, dkvg = _bwdprep_call(
        dsq, dsk_t, dsv_t, dsz, dqn, dqp, dkn_t, dvv_t, dkpt_t, dmz, dga, dgb, cq, ckv, cos128, sin128,
        q_norm_gain, kv_norm_gain, w_q, w_kv)
    grad_x, dshift, dscale, dg1 = _dh_call(dproj, w_int_t, xs, dx2, scale, norm_gain)
    g_in_t = _dwin_call(h, dproj)
    g_uq, g_ukv = _true_weight_grads(dwq, dwkv)

    g_in_pieces = g_in_t.reshape(N_DEV, IN_SHARD, HALF_D)
    g_small = jnp.concatenate([
        _chip_major(g_uq, 192).astype(BF16), _chip_major(g_ukv, 256).astype(BF16),
        dwa.reshape(N_CHIPS, -1, LANES), dwb.reshape(N_CHIPS, -1, LANES),
        dwo.reshape(N_CHIPS, -1, LANES)], axis=1).reshape(N_DEV, SMALL_HALF, LANES)
    small = jnp.concatenate([
        dshift, dscale, dgate, dg1, dqg, dkvg, dgf, loss_p,
        jnp.zeros((1, 8 * SV_COLS - 5888), F32)], axis=1).reshape(8, SV_COLS)
    full_in, full_small, svg = _reduce_call(g_in_pieces, g_small, small)
    full = full_small.reshape(SMALL_TOTAL, LANES)
    offs = [0]
    for r in SMALL_ROWS:
        offs.append(offs[-1] + r)
    gs_uq = full[offs[0]:offs[1]].reshape(Q_RANK, 192)
    gs_ukv = full[offs[1]:offs[2]].reshape(KV_RANK, 256)
    gs_a = full[offs[2]:offs[3]].reshape(512, 256)
    gs_b = full[offs[3]:offs[4]].reshape(512, 256)
    gs_out = full[offs[4]:offs[5]].reshape(256, D_MODEL)

    svm = svg.reshape(N_DEV, 8 * SV_COLS)
    dmod_sh = lax.dynamic_slice_in_dim(svm[:, :3 * D_MODEL], chip * 768, 768, axis=1)
    tot, gs_ada = _small_call(svm, call.T, dmod_sh)
    g_bada = tot[:, 0:3072]
    g_g1 = tot[:, 3072:4096]
    g_qg = tot[:, 4096:4480]
    g_kvg = tot[:, 4480:4736]
    g_gf = tot[:, 4736:5760]
    loss = tot[0, 5760]

    names = ["w_ada", "b_ada", "norm_gain", "w_in", "q_norm_gain", "w_uq", "kv_norm_gain", "w_ukv",
             "w_branch_a", "w_branch_b", "w_out", "final_norm_gain"]
    ws = [w_ada[0], b_ada, norm_gain, w_in_t, q_norm_gain, w_uq[0], kv_norm_gain, w_ukv[0],
          w_branch_a[0], w_branch_b[0], w_out[0], final_norm_gain.reshape(1, D_MODEL)]
    gs = [gs_ada, g_bada, g_g1, full_in, g_qg, gs_uq, g_kvg, gs_ukv, gs_a, gs_b, gs_out, g_gf]
    ms = [m_w_ada[0], m_b_ada, m_norm_gain, jnp.swapaxes(m_w_in[0], 0, 1), m_q_norm_gain, m_w_uq[0],
          m_kv_norm_gain, m_w_ukv[0], m_w_branch_a[0], m_w_branch_b[0], m_w_out[0],
          m_final_norm_gain.reshape(1, D_MODEL)]
    vs = [v_w_ada[0], v_b_ada, v_norm_gain, jnp.swapaxes(v_w_in[0], 0, 1), v_q_norm_gain, v_w_uq[0],
          v_kv_norm_gain, v_w_ukv[0], v_w_branch_a[0], v_w_branch_b[0], v_w_out[0],
          v_final_norm_gain.reshape(1, D_MODEL)]
    refs = [w_ada, b_ada, norm_gain, w_in, q_norm_gain, w_uq, kv_norm_gain, w_ukv,
            w_branch_a, w_branch_b, w_out, final_norm_gain]
    grads, deltas, new_ms, new_vs = [], [], [], []
    for n, w_, g_, m_, v_, ref in zip(names, ws, gs, ms, vs, refs):
        outs = _adamw_call(n, w_, g_, m_, v_)
        if n == "w_in":
            outs = tuple(jnp.swapaxes(o_, 0, 1) for o_ in outs)
        for lst, o_ in zip((grads, deltas, new_ms, new_vs), outs):
            lst.append(o_.reshape(ref.shape))

    return (loss, grad_x.reshape(x.shape), *grads, *deltas, *new_ms, *new_vs)
```

```python
import math

import jax
import jax.numpy as jnp
from jax import lax
from jax.experimental import pallas as pl
from jax.experimental.pallas import tpu as pltpu

F32 = jnp.float32
BF16 = jnp.bfloat16

D_MODEL = 1024
SB_WIDTH = 512
MLA_WIDTH = 512
Q_RANK = 384
KV_RANK = 256
ROPE_DIM = 32
N_HEADS = 8
IN_WIDTH = 5280
EPS = 1e-6
ROPE_BASE = 10000.0
MLA_SCALE = 1.0 / math.sqrt(96.0)
SB_SCALE = 0.125
LOG2E = 1.4426950408889634

ADAM_LR = 0.001
ADAM_B1 = 0.9
ADAM_B2 = 0.999
ADAM_EPS = 1e-08
ADAM_WD = 0.01
ADAM_STEP = 10

O_SQ, O_SK, O_SV, O_SZ, O_CQ, O_CKV, O_MZ, O_GA, O_GB, O_KR, O_END = (
    0, 512, 1024, 1536, 2048, 2432, 2688, 3200, 4224, 5248, 5504)
W_INT = 5632

N_CHIPS = 4
N_DEV = 8
LANES = 128
SV_COLS = 768

ROW_TILE = 256
ATT_TILE = 256
ATT_Q_TILES = 2
FWD_Q_TILES = 4
SB_Q_TILES = 1
MLA_KEY_TILE = 512
VMEM_LIMIT = 56 * 1024 * 1024

MESH = pl.DeviceIdType.MESH


def _dot(a, b):
    return lax.dot_general(a, b, (((1,), (0,)), ((), ())), preferred_element_type=F32)


def _dot_nt(a, b):
    return lax.dot_general(a, b, (((1,), (1,)), ((), ())), preferred_element_type=F32)


def _dot_tn(a, b):
    return lax.dot_general(a, b, (((0,), (0,)), ((), ())), preferred_element_type=F32)


def _sigmoid(z):
    return 1.0 / (1.0 + jnp.exp2(z * (-LOG2E)))


def _params(sem=None):
    if sem is None:
        return pltpu.CompilerParams(vmem_limit_bytes=VMEM_LIMIT)
    return pltpu.CompilerParams(dimension_semantics=sem, vmem_limit_bytes=VMEM_LIMIT)


def _rows(tm, n):
    return pl.BlockSpec((tm, n), lambda i: (i, 0))


def _cols(n, tm):
    return pl.BlockSpec((n, tm), lambda i: (0, i))


def _whole(shape):
    nd = len(shape)
    return pl.BlockSpec(shape, lambda i: (0,) * nd)


def _sds(shape, dtype):
    return jax.ShapeDtypeStruct(shape, dtype)


def _flip(v, d):
    return 1 - v if d else v


def _weight_rows(a, b):
    if a >= O_KR:
        return True, slice(a - O_KR, b - O_KR)
    shift = ROPE_DIM if a >= O_MZ else 0
    return False, slice(a + shift, b + shift)


def _inproj_call(x, shift, scale, g1, w_in_t, w_kr, w_q, w_kv, qg, kvg, cos128, sin128):
    s_len = x.shape[0]
    tm = min(ROW_TILE, s_len)

    def body(x_ref, sh_ref, sc_ref, g1_ref, win_ref, wkr_ref, wq_ref, wkv_ref, qg_ref, kvg_ref, cos_ref, sin_ref,
             h_ref, sq_ref, sk_ref, sv_ref, sz_ref, cq_ref, ckv_ref, mz_ref, ga_ref, gb_ref, kpt_ref,
             qn_ref, qp_ref, kn_ref, vv_ref):
        xt = x_ref[...]
        r = lax.rsqrt(jnp.mean(xt * xt, axis=-1, keepdims=True) + EPS)
        h = (xt * r * g1_ref[...]) * (1.0 + sc_ref[...]) + sh_ref[...]
        hb = h.astype(BF16)
        h_ref[...] = hb

        def seg(a, b):
            from_kr, rows = _weight_rows(a, b)
            w_ref = wkr_ref if from_kr else win_ref
            return _dot_nt(hb[:, :HALF_D], w_ref[0, rows, :]) + _dot_nt(hb[:, HALF_D:], w_ref[1, rows, :])

        sq_ref[...] = (seg(O_SQ, O_SK) * SB_SCALE).astype(BF16)
        sk_ref[...] = seg(O_SK, O_SV).astype(BF16)
        sv_ref[...] = seg(O_SV, O_SZ).astype(BF16)
        sz_ref[...] = seg(O_SZ, O_CQ)
        mz_ref[...] = seg(O_MZ, O_GA)
        ga_ref[...] = seg(O_GA, O_GB)
        gb_ref[...] = seg(O_GB, O_KR)
        cos = cos_ref[...]
        sin = sin_ref[...]
        kr = seg(O_KR, O_END)
        kpt_ref[...] = (kr[:, :128] * cos + kr[:, 128:] * sin).astype(BF16)

        cq = seg(O_CQ, O_CKV)
        cq_ref[...] = cq
        rq = lax.rsqrt(jnp.mean(cq * cq, axis=-1, keepdims=True) + EPS)
        cqn = (cq * rq * qg_ref[...]).astype(BF16)
        qa = _dot(cqn, wq_ref[...])
        qn_ref[...] = qa[:, :512].astype(BF16)
        qp_ref[...] = (qa[:, 512:768] * jnp.tile(cos, (1, 2)) + qa[:, 768:] * jnp.tile(sin, (1, 2))).astype(BF16)

        ckv = seg(O_CKV, O_MZ)
        ckv_ref[...] = ckv
        rk = lax.rsqrt(jnp.mean(ckv * ckv, axis=-1, keepdims=True) + EPS)
        ckvn = (ckv * rk * kvg_ref[...]).astype(BF16)
        kva = _dot(ckvn, wkv_ref[...])
        kn_ref[...] = kva[:, :512].astype(BF16)
        vv_ref[...] = kva[:, 512:].astype(BF16)

    outs = [
        (D_MODEL, BF16), (512, BF16), (512, BF16), (512, BF16), (512, F32), (Q_RANK, F32), (KV_RANK, F32),
        (512, F32), (D_MODEL, F32), (D_MODEL, F32), (128, BF16), (512, BF16), (256, BF16), (512, BF16), (512, BF16),
    ]
    return pl.pallas_call(
        body, name="inproj", grid=(s_len // tm,),
        out_shape=tuple(_sds((s_len, n), dt) for n, dt in outs),
        in_specs=[_rows(tm, D_MODEL), _whole((1, D_MODEL)), _whole((1, D_MODEL)), _whole((1, D_MODEL)),
                  _whole((2, IN_WIDTH, HALF_D)), _whole((2, O_END - O_KR, HALF_D)),
                  _whole((Q_RANK, 1024)), _whole((KV_RANK, 1024)),
                  _whole((1, Q_RANK)), _whole((1, KV_RANK)), _rows(tm, LANES), _rows(tm, LANES)],
        out_specs=tuple(_rows(tm, n) for n, _ in outs),
        compiler_params=_params(("parallel",)),
    )(x, shift, scale, g1, w_in_t, w_kr, w_q, w_kv, qg, kvg, cos128, sin128)


Z_CLAMP = 80.0 * LOG2E
RUN_CUTOFF = 110.0 * LOG2E


def _softplus_clamped(z):
    zc = jnp.minimum(z * LOG2E, Z_CLAMP)
    return zc, jnp.log2(1.0 + jnp.exp2(zc))


def _tri_sum(a, tri):
    return _dot(a.astype(BF16), tri)


def _sb_fwd_call(q, k, v):
    s_len = q.shape[0]
    tk = min(ATT_TILE, s_len)
    tq = min(SB_Q_TILES * ATT_TILE, s_len)
    r = tq // tk
    nq = s_len // tq

    def body(q_ref, k_ref, v_ref, o_ref, lt_ref, first_ref):
        i = pl.program_id(1)
        q2 = q_ref[...]
        lane = lax.broadcasted_iota(jnp.int32, (1, 256), 1)
        krow = lax.broadcasted_iota(jnp.int32, (tk, tk), 0)
        kcol = lax.broadcasted_iota(jnp.int32, (tk, tk), 1)
        row = lax.broadcasted_iota(jnp.int32, (tq, tk), 0)
        col = lax.broadcasted_iota(jnp.int32, (tq, tk), 1)
        later = (krow > kcol).astype(BF16)
        valids = [col + u * tk < row for u in range(r)]
        hms = [(lane // 64) == hh for hh in range(4)]
        qms = [jnp.where(hm, q2, jnp.zeros_like(q2)) for hm in hms]

        def block(j, carry, valid):
            runs, acc = list(carry[:4]), carry[4]
            off = pl.multiple_of(j * tk, tk)
            kb = k_ref[pl.ds(off, tk), :]
            vb = v_ref[pl.ds(off, tk), :]
            ws = []
            for hh in range(4):
                zc, sp = _softplus_clamped(_dot_nt(qms[hh], kb))
                lm = jnp.where(valid, sp, 0.0) if valid is not None else sp
                suf = _tri_sum(lm, later)
                w = jnp.exp2(zc - sp - suf - runs[hh])
                if valid is not None:
                    w = jnp.where(valid, w, 0.0)
                ws.append(w.astype(BF16))
                runs[hh] = runs[hh] + jnp.sum(lm, axis=1, keepdims=True)
            vstack = jnp.concatenate([jnp.where(hm, vb, jnp.zeros_like(vb)) for hm in hms], axis=0)
            acc = acc + _dot(jnp.concatenate(ws, axis=1), vstack)
            return (*runs, acc)

        zero = jnp.zeros((tq, 1), F32)
        carry = (zero, zero, zero, zero, jnp.zeros((tq, 256), F32))
        for u in reversed(range(r)):
            carry = block(i * r + u, carry, valids[u])

        def least_run(runs):
            return jnp.min(jnp.minimum(jnp.minimum(runs[0], runs[1]), jnp.minimum(runs[2], runs[3])))

        n_full = i * r

        def unfinished(state):
            return jnp.logical_and(state[0] < n_full, state[1] <= RUN_CUTOFF)

        def visit(state):
            cr = block(n_full - 1 - state[0], state[2:], None)
            return (state[0] + 1, least_run(cr[:4]), *cr)

        state = lax.while_loop(unfinished, visit, (jnp.int32(0), least_run(carry[:4]), *carry))
        carry = state[2:]
        first_ref[...] = jnp.full(first_ref.shape, n_full - state[0], jnp.int32)
        for hh in range(4):
            lt_ref[0, :, hh:hh + 1] = carry[hh]
        o_ref[...] = carry[4]

    return pl.pallas_call(
        body, name="sb_fwd", grid=(2, nq),
        out_shape=(_sds((s_len, SB_WIDTH), F32), _sds((2, s_len, 4), F32), _sds((2, nq, 8, 128), jnp.int32)),
        in_specs=[pl.BlockSpec((tq, 256), lambda g, i: (i, g)),
                  pl.BlockSpec((s_len, 256), lambda g, i: (0, g)),
                  pl.BlockSpec((s_len, 256), lambda g, i: (0, g))],
        out_specs=(pl.BlockSpec((tq, 256), lambda g, i: (i, g)),
                   pl.BlockSpec((1, tq, 4), lambda g, i: (g, i, 0)),
                   pl.BlockSpec((1, 1, 8, 128), lambda g, i: (g, i, 0, 0))),
        compiler_params=_params(("parallel", "parallel")),
    )(q, k, v)


def _sb_bwd_call(first, q, k, v, do, lt):
    s_len = q.shape[0]
    tk = min(ATT_TILE, s_len)
    tq = min(SB_Q_TILES * ATT_TILE, s_len)
    r = tq // tk
    nq = s_len // tq
    nq_fwd = first.shape[0] // 2
    per_fwd = nq // nq_fwd

    def body(first_ref, q_ref, k_ref, v_ref, do_ref, lt_ref, dq_ref, dk_ref, dv_ref):
        g = pl.program_id(0)
        i = pl.program_id(1)

        @pl.when(i == 0)
        def _():
            dk_ref[...] = jnp.zeros_like(dk_ref)
            dv_ref[...] = jnp.zeros_like(dv_ref)

        q2 = q_ref[...]
        do2 = do_ref[...].astype(BF16)
        lane = lax.broadcasted_iota(jnp.int32, (1, 256), 1)
        krow = lax.broadcasted_iota(jnp.int32, (tk, tk), 0)
        kcol = lax.broadcasted_iota(jnp.int32, (tk, tk), 1)
        row = lax.broadcasted_iota(jnp.int32, (tq, tk), 0)
        col = lax.broadcasted_iota(jnp.int32, (tq, tk), 1)
        earlier = (krow < kcol).astype(BF16)
        later = (krow > kcol).astype(BF16)
        valids = [col + u * tk < row for u in range(r)]
        hms = [(lane // 64) == hh for hh in range(4)]
        qms = [jnp.where(hm, q2, jnp.zeros_like(q2)) for hm in hms]
        doms = [jnp.where(hm, do2, jnp.zeros_like(do2)) for hm in hms]
        ltots = [lt_ref[0, :, hh:hh + 1] for hh in range(4)]
        q2t = jnp.transpose(q2.astype(F32))
        do2t = jnp.transpose(do_ref[...])
        subl = lax.broadcasted_iota(jnp.int32, (256, 1), 0)
        qtstack = jnp.concatenate(
            [jnp.where((subl // 64) == hh, q2t, 0.0).astype(BF16) for hh in range(4)], axis=1)
        dotstack = jnp.concatenate(
            [jnp.where((subl // 64) == hh, do2t, 0.0).astype(BF16) for hh in range(4)], axis=1)

        def block(j, carry, valid):
            lpre, ppre, dq = list(carry[0:4]), list(carry[4:8]), carry[8]
            off = pl.multiple_of(j * tk, tk)
            kb = k_ref[pl.ds(off, tk), :]
            vb = v_ref[pl.ds(off, tk), :]
            dzs, avs = [], []
            for hh in range(4):
                zc, sp = _softplus_clamped(_dot_nt(qms[hh], kb))
                lsig = zc - sp
                lm = jnp.where(valid, sp, 0.0) if valid is not None else sp
                rowsum = jnp.sum(lm, axis=1, keepdims=True)
                between = _tri_sum(lm, later) + ((ltots[hh] - lpre[hh]) - rowsum)
                a = jnp.exp2(lsig - between)
                if valid is not None:
                    a = jnp.where(valid, a, 0.0)
                p = a * _dot_nt(doms[hh], vb)
                pbefore = ppre[hh] + _tri_sum(p, earlier)
                dz = p - jnp.exp2(lsig) * (p + pbefore)
                if valid is not None:
                    dz = jnp.where(valid, dz, 0.0)
                dzs.append(dz.astype(BF16))
                avs.append(a.astype(BF16))
                lpre[hh] = lpre[hh] + rowsum
                ppre[hh] = ppre[hh] + jnp.sum(p, axis=1, keepdims=True)
            kstack = jnp.concatenate([jnp.where(hm, kb, jnp.zeros_like(kb)) for hm in hms], axis=0)
            dq = dq + _dot(jnp.concatenate(dzs, axis=1), kstack)
            dk_ref[:, pl.ds(off, tk)] += _dot(qtstack, jnp.concatenate(dzs, axis=0))
            dv_ref[:, pl.ds(off, tk)] += _dot(dotstack, jnp.concatenate(avs, axis=0))
            return (*lpre, *ppre, dq)

        zero = jnp.zeros((tq, 1), F32)
        start = jnp.minimum(first_ref[g * nq_fwd + i // per_fwd], i * r)
        carry = lax.fori_loop(start, i * r, lambda j, cr: block(j, cr, None),
                              (zero,) * 8 + (jnp.zeros((tq, 256), F32),))
        for u in range(r):
            carry = block(i * r + u, carry, valids[u])
        dq_ref[...] = carry[8].astype(BF16)

    return pl.pallas_call(
        body, name="sb_bwd",
        out_shape=(_sds((s_len, SB_WIDTH), BF16), _sds((SB_WIDTH, s_len), F32), _sds((SB_WIDTH, s_len), F32)),
        grid_spec=pltpu.PrefetchScalarGridSpec(
            num_scalar_prefetch=1, grid=(2, nq),
            in_specs=[pl.BlockSpec((tq, 256), lambda g, i, f: (i, g)),
                      pl.BlockSpec((s_len, 256), lambda g, i, f: (0, g)),
                      pl.BlockSpec((s_len, 256), lambda g, i, f: (0, g)),
                      pl.BlockSpec((tq, 256), lambda g, i, f: (i, g)),
                      pl.BlockSpec((1, tq, 4), lambda g, i, f: (g, i, 0))],
            out_specs=(pl.BlockSpec((tq, 256), lambda g, i, f: (i, g)),
                       pl.BlockSpec((256, s_len), lambda g, i, f: (g, 0)),
                       pl.BlockSpec((256, s_len), lambda g, i, f: (g, 0)))),
        compiler_params=_params(("parallel", "arbitrary")),
    )(first, q, k, v, do, lt)


def _mla_fwd_call(qn, qp, kn, kpt, v):
    s_len = qn.shape[0]
    tk = min(MLA_KEY_TILE, s_len)
    tq = min(FWD_Q_TILES * ATT_TILE, s_len)
    r = tq // tk
    nq = s_len // tq

    def body(qn_ref, qp_ref, kn_ref, kpt_ref, v_ref, o_ref, lse_ref):
        i = pl.program_id(1)
        qn2 = qn_ref[...]
        qp2 = qp_ref[...]
        lane256 = lax.broadcasted_iota(jnp.int32, (1, 256), 1)
        lane128 = lax.broadcasted_iota(jnp.int32, (1, 128), 1)
        krow = lax.broadcasted_iota(jnp.int32, (tk, tk), 0)
        kcol = lax.broadcasted_iota(jnp.int32, (tk, tk), 1)
        row = lax.broadcasted_iota(jnp.int32, (tq, tk), 0)
        col = lax.broadcasted_iota(jnp.int32, (tq, tk), 1)
        valids = [col + u * tk <= row for u in range(r)]
        m64s = [(lane256 // 64) == hh for hh in range(4)]
        half = [(lane128 // 64) == u for u in range(2)]
        m32s = [(lane128 // 32) == hh for hh in range(4)]
        qcs = []
        for hh in range(4):
            qpair = qn2[:, 128 * (hh // 2):128 * (hh // 2) + 128]
            qcs.append(jnp.concatenate([jnp.where(half[hh % 2], qpair, jnp.zeros_like(qpair)),
                                        jnp.where(m32s[hh], qp2, jnp.zeros_like(qp2))], axis=1))

        def by_head(vals):
            return jnp.where(m64s[0], vals[0], jnp.where(m64s[1], vals[1], jnp.where(m64s[2], vals[2], vals[3])))

        def block(j, carry, valid):
            ms, ls, acc = list(carry[0:4]), list(carry[4:8]), carry[8]
            off = pl.multiple_of(j * tk, tk)
            knb = kn_ref[pl.ds(off, tk), :]
            kpb = kpt_ref[pl.ds(off, tk), :]
            vb = v_ref[pl.ds(off, tk), :]
            kcs = [jnp.concatenate([knb[:, 128 * pp:128 * pp + 128], kpb], axis=1) for pp in range(2)]
            ps, alphas = [], []
            for hh in range(4):
                s = _dot_nt(qcs[hh], kcs[hh // 2]) * (MLA_SCALE * LOG2E)
                if valid is not None:
                    s = jnp.where(valid, s, -1e30)
                mn = jnp.maximum(ms[hh], jnp.max(s, axis=1, keepdims=True))
                p = jnp.exp2(s - mn)
                alpha = jnp.exp2(ms[hh] - mn)
                ls[hh] = alpha * ls[hh] + jnp.sum(p, axis=1, keepdims=True)
                ms[hh] = mn
                ps.append(p.astype(BF16))
                alphas.append(alpha)
            pvs = []
            for pp in range(2):
                vpair = vb[:, 128 * pp:128 * pp + 128]
                vstack = jnp.concatenate([jnp.where(hf, vpair, jnp.zeros_like(vpair)) for hf in half], axis=0)
                pvs.append(_dot(jnp.concatenate(ps[2 * pp:2 * pp + 2], axis=1), vstack))
            acc = by_head(alphas) * acc + jnp.concatenate(pvs, axis=1)
            return (*ms, *ls, acc)

        neg = jnp.full((tq, 1), -1e30, F32)
        zero = jnp.zeros((tq, 1), F32)
        carry = lax.fori_loop(0, i * r, lambda j, cr: block(j, cr, None),
                              (neg,) * 4 + (zero,) * 4 + (jnp.zeros((tq, 256), F32),))
        for u in range(r):
            carry = block(i * r + u, carry, valids[u])
        o_ref[...] = carry[8] / by_head(list(carry[4:8]))
        for hh in range(4):
            lse_ref[0, :, hh:hh + 1] = (carry[hh] + jnp.log2(carry[4 + hh])) * (1.0 / LOG2E)

    return pl.pallas_call(
        body, name="mla_fwd", grid=(2, nq),
        out_shape=(_sds((s_len, MLA_WIDTH), F32), _sds((2, s_len, 4), F32)),
        in_specs=[pl.BlockSpec((tq, 256), lambda g, i: (i, g)),
                  pl.BlockSpec((tq, 128), lambda g, i: (i, g)),
                  pl.BlockSpec((s_len, 256), lambda g, i: (0, g)),
                  pl.BlockSpec((s_len, 128), lambda g, i: (0, 0)),
                  pl.BlockSpec((s_len, 256), lambda g, i: (0, g))],
        out_specs=(pl.BlockSpec((tq, 256), lambda g, i: (i, g)),
                   pl.BlockSpec((1, tq, 4), lambda g, i: (g, i, 0))),
        compiler_params=_params(("parallel", "parallel")),
    )(qn, qp, kn, kpt, v)


def _mla_bwd_call(qn, qp, kn, kpt, v, o, do, lse):
    s_len = qn.shape[0]
    tk = min(MLA_KEY_TILE, s_len)
    tq = min(ATT_Q_TILES * ATT_TILE, s_len)
    r = tq // tk
    nq = s_len // tq

    def body(qn_ref, qp_ref, kn_ref, kpt_ref, v_ref, o_ref, do_ref, lse_ref,
             dqn_ref, dqp_ref, dkn_ref, dkpt_ref, dv_ref):
        g = pl.program_id(0)
        i = pl.program_id(1)

        @pl.when(i == 0)
        def _():
            dkn_ref[...] = jnp.zeros_like(dkn_ref)
            dv_ref[...] = jnp.zeros_like(dv_ref)

        @pl.when((i == 0) & (g == 0))
        def _():
            dkpt_ref[...] = jnp.zeros_like(dkpt_ref)

        qn2 = qn_ref[...]
        qp2 = qp_ref[...]
        dof = do_ref[...]
        dob = dof.astype(BF16)
        prod = dof * o_ref[...]
        lane256 = lax.broadcasted_iota(jnp.int32, (1, 256), 1)
        lane128 = lax.broadcasted_iota(jnp.int32, (1, 128), 1)
        krow = lax.broadcasted_iota(jnp.int32, (tk, tk), 0)
        kcol = lax.broadcasted_iota(jnp.int32, (tk, tk), 1)
        row = lax.broadcasted_iota(jnp.int32, (tq, tk), 0)
        col = lax.broadcasted_iota(jnp.int32, (tq, tk), 1)
        valids = [col + u * tk <= row for u in range(r)]
        m64s = [(lane256 // 64) == hh for hh in range(4)]
        half = [(lane128 // 64) == u for u in range(2)]
        m32s = [(lane128 // 32) == hh for hh in range(4)]
        qcs, doms = [], []
        for hh in range(4):
            sl = slice(128 * (hh // 2), 128 * (hh // 2) + 128)
            qpair = qn2[:, sl]
            dpair = dob[:, sl]
            qcs.append(jnp.concatenate([jnp.where(half[hh % 2], qpair, jnp.zeros_like(qpair)),
                                        jnp.where(m32s[hh], qp2, jnp.zeros_like(qp2))], axis=1))
            doms.append(jnp.where(half[hh % 2], dpair, jnp.zeros_like(dpair)))
        dsums = [jnp.sum(jnp.where(m64, prod, 0.0), axis=1, keepdims=True) * MLA_SCALE for m64 in m64s]
        lses = [lse_ref[0, :, hh:hh + 1] * LOG2E for hh in range(4)]
        qn2t = jnp.transpose(qn2.astype(F32))
        qp2t = jnp.transpose(qp2.astype(F32))
        do2t = jnp.transpose(dof)
        sub128 = lax.broadcasted_iota(jnp.int32, (128, 1), 0)
        qtstacks, dotstacks = [], []
        for pp in range(2):
            qts, dts = [], []
            for u in range(2):
                hh = 2 * pp + u
                qts.append(jnp.concatenate(
                    [jnp.where((sub128 // 64) == u, qn2t[128 * pp:128 * pp + 128, :], 0.0),
                     jnp.where((sub128 // 32) == hh, qp2t, 0.0)], axis=0).astype(BF16))
                dts.append(jnp.where((sub128 // 64) == u, do2t[128 * pp:128 * pp + 128, :], 0.0).astype(BF16))
            qtstacks.append(jnp.concatenate(qts, axis=1))
            dotstacks.append(jnp.concatenate(dts, axis=1))

        def block(j, carry, valid):
            dqn, dqp = carry
            off = pl.multiple_of(j * tk, tk)
            knb = kn_ref[pl.ds(off, tk), :]
            kpb = kpt_ref[pl.ds(off, tk), :]
            vb = v_ref[pl.ds(off, tk), :]
            dqn_parts = []
            dkp = None
            for pp in range(2):
                sl = slice(128 * pp, 128 * pp + 128)
                knp = knb[:, sl]
                vpair = vb[:, sl]
                kc = jnp.concatenate([knp, kpb], axis=1)
                dss, pbs, kcms = [], [], []
                for u in range(2):
                    hh = 2 * pp + u
                    s = _dot_nt(qcs[hh], kc) * (MLA_SCALE * LOG2E)
                    if valid is not None:
                        s = jnp.where(valid, s, -1e30)
                    p = jnp.exp2(s - lses[hh])
                    ds = p * (_dot_nt(doms[hh], vpair) * MLA_SCALE - dsums[hh])
                    dss.append(ds.astype(BF16))
                    pbs.append(p.astype(BF16))
                    kcms.append(jnp.concatenate([jnp.where(half[u], knp, jnp.zeros_like(knp)),
                                                 jnp.where(m32s[hh], kpb, jnp.zeros_like(kpb))], axis=1))
                dqc = _dot(jnp.concatenate(dss, axis=1), jnp.concatenate(kcms, axis=0))
                dqn_parts.append(dqc[:, :128])
                dqp = dqp + dqc[:, 128:]
                dkc = _dot(qtstacks[pp], jnp.concatenate(dss, axis=0))
                dkn_ref[128 * pp:128 * pp + 128, pl.ds(off, tk)] += dkc[:128, :]
                dkp = dkc[128:, :] if dkp is None else dkp + dkc[128:, :]
                dv_ref[128 * pp:128 * pp + 128, pl.ds(off, tk)] += _dot(dotstacks[pp], jnp.concatenate(pbs, axis=0))
            dqn = dqn + jnp.concatenate(dqn_parts, axis=1)
            dkpt_ref[:, pl.ds(off, tk)] += dkp
            return dqn, dqp

        carry = lax.fori_loop(0, i * r, lambda j, cr: block(j, cr, None),
                              (jnp.zeros((tq, 256), F32), jnp.zeros((tq, 128), F32)))
        for u in range(r):
            carry = block(i * r + u, carry, valids[u])
        dqn, dqp = carry
        dqn_ref[...] = dqn.astype(BF16)
        dqp_ref[...] = dqp.astype(BF16)

    return pl.pallas_call(
        body, name="mla_bwd", grid=(2, nq),
        out_shape=(_sds((s_len, 512), BF16), _sds((s_len, 256), BF16), _sds((512, s_len), F32),
                   _sds((128, s_len), F32), _sds((512, s_len), F32)),
        in_specs=[pl.BlockSpec((tq, 256), lambda g, i: (i, g)),
                  pl.BlockSpec((tq, 128), lambda g, i: (i, g)),
                  pl.BlockSpec((s_len, 256), lambda g, i: (0, g)),
                  pl.BlockSpec((s_len, 128), lambda g, i: (0, 0)),
                  pl.BlockSpec((s_len, 256), lambda g, i: (0, g)),
                  pl.BlockSpec((tq, 256), lambda g, i: (i, g)),
                  pl.BlockSpec((tq, 256), lambda g, i: (i, g)),
                  pl.BlockSpec((1, tq, 4), lambda g, i: (g, i, 0))],
        out_specs=(pl.BlockSpec((tq, 256), lambda g, i: (i, g)),
                   pl.BlockSpec((tq, 128), lambda g, i: (i, g)),
                   pl.BlockSpec((256, s_len), lambda g, i: (g, 0)),
                   pl.BlockSpec((128, s_len), lambda g, i: (0, 0)),
                   pl.BlockSpec((256, s_len), lambda g, i: (g, 0))),
        compiler_params=_params(("arbitrary", "arbitrary")),
    )(qn, qp, kn, kpt, v, o, do, lse)


def _post_call(x, tgt, oa, ob, sz, mz, ga, gb, gate, gf, wa, wb, wo):
    s_len = x.shape[0]
    tm = min(ROW_TILE, s_len)
    nstep = s_len // tm

    def body(x_ref, t_ref, oa_ref, ob_ref, sz_ref, mz_ref, ga_ref, gb_ref, gate_ref, gf_ref,
             wa_ref, wb_ref, wo_ref,
             dx2_ref, doa_ref, dob_ref, dsz_ref, dmz_ref, dga_ref, dgb_ref,
             dwo_out, dwa_out, dwb_out, dgf_ref, dgate_ref, loss_ref, dwo_ref, dwa_ref, dwb_ref):
        @pl.when(pl.program_id(0) == 0)
        def _():
            dwo_ref[...] = jnp.zeros_like(dwo_ref)
            dwa_ref[...] = jnp.zeros_like(dwa_ref)
            dwb_ref[...] = jnp.zeros_like(dwb_ref)
            dgf_ref[...] = jnp.zeros_like(dgf_ref)
            dgate_ref[...] = jnp.zeros_like(dgate_ref)
            loss_ref[...] = jnp.zeros_like(loss_ref)

        gate = gate_ref[...]
        gf = gf_ref[...]
        oa = oa_ref[...]
        ob = ob_ref[...]
        sz = sz_ref[...]
        mz = mz_ref[...]
        sa = _sigmoid(sz)
        sb = _sigmoid(mz)
        silu_a = sz * sa
        silu_b = mz * sb
        ua = (oa * silu_a).astype(BF16)
        ub = (ob * silu_b).astype(BF16)
        ya = _dot(ua, wa_ref[...])
        yb = _dot(ub, wb_ref[...])
        sga = _sigmoid(ga_ref[...])
        sgb = _sigmoid(gb_ref[...])
        merged = (sga * ya + sgb * yb).astype(BF16)
        out = _dot(merged, wo_ref[...])
        x2 = x_ref[...] + gate * out
        r2 = lax.rsqrt(jnp.mean(x2 * x2, axis=-1, keepdims=True) + EPS)
        xhat = x2 * r2
        err = xhat * gf - t_ref[...]
        loss_ref[...] += 0.5 * jnp.sum(jnp.sum(err * err, axis=1, keepdims=True), axis=0, keepdims=True) / D_MODEL
        dy = err * (1.0 / D_MODEL)
        dgf_ref[...] += jnp.sum(dy * xhat, axis=0, keepdims=True)
        dxhat = dy * gf
        dx2 = r2 * (dxhat - xhat * jnp.mean(dxhat * xhat, axis=-1, keepdims=True))
        dx2_ref[...] = dx2
        dgate_ref[...] += jnp.sum(dx2 * out, axis=0, keepdims=True)
        dout = (dx2 * gate).astype(BF16)
        dmerged = _dot_nt(dout, wo_ref[...])
        dwo_ref[...] += _dot_tn(merged, dout)
        dya = dmerged * sga
        dyb = dmerged * sgb
        dga_ref[...] = (dya * ya * (1.0 - sga)).astype(BF16)
        dgb_ref[...] = (dyb * yb * (1.0 - sgb)).astype(BF16)
        dyab = dya.astype(BF16)
        dybb = dyb.astype(BF16)
        dua = _dot_nt(dyab, wa_ref[...])
        dub = _dot_nt(dybb, wb_ref[...])
        dwa_ref[...] += _dot_tn(ua, dyab)
        dwb_ref[...] += _dot_tn(ub, dybb)
        doa_ref[...] = dua * silu_a
        dob_ref[...] = dub * silu_b
        dsz_ref[...] = (dua * oa * (sa * (1.0 + sz * (1.0 - sa)))).astype(BF16)
        dmz_ref[...] = (dub * ob * (sb * (1.0 + mz * (1.0 - sb)))).astype(BF16)

        @pl.when(pl.program_id(0) == nstep - 1)
        def _():
            dwo_out[...] = dwo_ref[...].astype(BF16)
            for k in range(N_CHIPS):
                dwa_out[k] = dwa_ref[:, 256 * k:256 * k + 256].astype(BF16)
                dwb_out[k] = dwb_ref[:, 256 * k:256 * k + 256].astype(BF16)

    return pl.pallas_call(
        body, name="post", grid=(nstep,),
        out_shape=(_sds((s_len, D_MODEL), F32), _sds((s_len, 512), F32), _sds((s_len, 512), F32),
                   _sds((s_len, 512), BF16), _sds((s_len, 512), BF16),
                   _sds((s_len, D_MODEL), BF16), _sds((s_len, D_MODEL), BF16),
                   _sds((D_MODEL, D_MODEL), BF16), _sds((N_CHIPS, 512, 256), BF16), _sds((N_CHIPS, 512, 256), BF16),
                   _sds((1, D_MODEL), F32), _sds((1, D_MODEL), F32), _sds((1, 128), F32)),
        in_specs=[_rows(tm, D_MODEL), _rows(tm, D_MODEL), _rows(tm, 512), _rows(tm, 512), _rows(tm, 512),
                  _rows(tm, 512), _rows(tm, D_MODEL), _rows(tm, D_MODEL), _whole((1, D_MODEL)), _whole((1, D_MODEL)),
                  _whole((512, D_MODEL)), _whole((512, D_MODEL)), _whole((D_MODEL, D_MODEL))],
        out_specs=(_rows(tm, D_MODEL), _rows(tm, 512), _rows(tm, 512), _rows(tm, 512), _rows(tm, 512),
                   _rows(tm, D_MODEL), _rows(tm, D_MODEL),
                   _whole((D_MODEL, D_MODEL)), _whole((N_CHIPS, 512, 256)), _whole((N_CHIPS, 512, 256)),
                   _whole((1, D_MODEL)), _whole((1, D_MODEL)), _whole((1, 128))),
        scratch_shapes=[pltpu.VMEM((D_MODEL, D_MODEL), F32), pltpu.VMEM((512, D_MODEL), F32),
                        pltpu.VMEM((512, D_MODEL), F32)],
        compiler_params=_params(("arbitrary",)),
    )(x, tgt, oa, ob, sz, mz, ga, gb, gate, gf, wa, wb, wo)


def _bwdprep_call(dsq, dsk, dsv, dsz, dqn, dqp, dkn, dvv, dkpt, dmz, dga, dgb, cq, ckv, cos128, sin128,
                  qg, kvg, w_q, w_kv):
    s_len = cq.shape[0]
    tm = min(ROW_TILE, s_len)

    def body(dsq_ref, dsk_ref, dsv_ref, dsz_ref, dqn_ref, dqp_ref, dkn_ref, dvv_ref, dkpt_ref, dmz_ref,
             dga_ref, dgb_ref, cq_ref, ckv_ref, cos_ref, sin_ref, qg_ref, kvg_ref, wq_ref, wkv_ref,
             dp_ref, dwq_ref, dwkv_ref, dqg_ref, dkvg_ref):
        @pl.when(pl.program_id(0) == 0)
        def _():
            dwq_ref[...] = jnp.zeros_like(dwq_ref)
            dwkv_ref[...] = jnp.zeros_like(dwkv_ref)
            dqg_ref[...] = jnp.zeros_like(dqg_ref)
            dkvg_ref[...] = jnp.zeros_like(dkvg_ref)

        cos = cos_ref[...]
        sin = sin_ref[...]
        dp_ref[:, O_SQ:O_SK] = dsq_ref[...] * jnp.asarray(SB_SCALE, BF16)
        dp_ref[:, O_SK:O_SV] = jnp.transpose(dsk_ref[...]).astype(BF16)
        dp_ref[:, O_SV:O_SZ] = jnp.transpose(dsv_ref[...]).astype(BF16)
        dp_ref[:, O_SZ:O_CQ] = dsz_ref[...]
        dp_ref[:, O_MZ:O_GA] = dmz_ref[...]
        dp_ref[:, O_GA:O_GB] = dga_ref[...]
        dp_ref[:, O_GB:O_KR] = dgb_ref[...]
        dkp = jnp.transpose(dkpt_ref[...])
        dp_ref[:, O_KR:O_KR + 128] = (dkp * cos).astype(BF16)
        dp_ref[:, O_KR + 128:O_END] = (dkp * sin).astype(BF16)
        dp_ref[:, O_END:W_INT] = jnp.zeros((tm, W_INT - O_END), BF16)

        cq = cq_ref[...]
        rq = lax.rsqrt(jnp.mean(cq * cq, axis=-1, keepdims=True) + EPS)
        cqh = cq * rq
        qg = qg_ref[...]
        cqn = (cqh * qg).astype(BF16)
        dqp = dqp_ref[...].astype(F32)
        dqa = jnp.concatenate([dqn_ref[...], (dqp * jnp.tile(cos, (1, 2))).astype(BF16),
                               (dqp * jnp.tile(sin, (1, 2))).astype(BF16)], axis=1)
        dcqn = _dot_nt(dqa, wq_ref[...])
        dwq_ref[...] += _dot_tn(cqn, dqa)
        dqg_ref[...] += jnp.sum(dcqn * cqh, axis=0, keepdims=True)
        dh = dcqn * qg
        dcq = rq * (dh - cqh * jnp.mean(dh * cqh, axis=-1, keepdims=True))
        dp_ref[:, O_CQ:O_CKV] = dcq.astype(BF16)

        ckv = ckv_ref[...]
        rk = lax.rsqrt(jnp.mean(ckv * ckv, axis=-1, keepdims=True) + EPS)
        ckh = ckv * rk
        kvg = kvg_ref[...]
        ckvn = (ckh * kvg).astype(BF16)
        dkva = jnp.concatenate([jnp.transpose(dkn_ref[...]).astype(BF16),
                                jnp.transpose(dvv_ref[...]).astype(BF16)], axis=1)
        dckvn = _dot_nt(dkva, wkv_ref[...])
        dwkv_ref[...] += _dot_tn(ckvn, dkva)
        dkvg_ref[...] += jnp.sum(dckvn * ckh, axis=0, keepdims=True)
        dh2 = dckvn * kvg
        dckv = rk * (dh2 - ckh * jnp.mean(dh2 * ckh, axis=-1, keepdims=True))
        dp_ref[:, O_CKV:O_MZ] = dckv.astype(BF16)

    return pl.pallas_call(
        body, name="bwdprep", grid=(s_len // tm,),
        out_shape=(_sds((s_len, W_INT), BF16), _sds((Q_RANK, 1024), F32), _sds((KV_RANK, 1024), F32),
                   _sds((1, Q_RANK), F32), _sds((1, KV_RANK), F32)),
        in_specs=[_rows(tm, 512), _cols(512, tm), _cols(512, tm), _rows(tm, 512), _rows(tm, 512), _rows(tm, 256),
                  _cols(512, tm), _cols(512, tm), _cols(128, tm), _rows(tm, 512), _rows(tm, D_MODEL),
                  _rows(tm, D_MODEL), _rows(tm, Q_RANK), _rows(tm, KV_RANK), _rows(tm, LANES), _rows(tm, LANES),
                  _whole((1, Q_RANK)), _whole((1, KV_RANK)), _whole((Q_RANK, 1024)), _whole((KV_RANK, 1024))],
        out_specs=(_rows(tm, W_INT), _whole((Q_RANK, 1024)), _whole((KV_RANK, 1024)),
                   _whole((1, Q_RANK)), _whole((1, KV_RANK))),
        compiler_params=_params(("arbitrary",)),
    )(dsq, dsk, dsv, dsz, dqn, dqp, dkn, dvv, dkpt, dmz, dga, dgb, cq, ckv, cos128, sin128, qg, kvg, w_q, w_kv)


def _dh_call(dproj, w_in_t, w_kr, x, dx2, scale, g1):
    s_len = x.shape[0]
    tm = min(2 * ROW_TILE, s_len)
    parts = [(a, b) + _weight_rows(a, b) for a, b in ((0, O_MZ), (O_MZ, O_KR), (O_KR, O_END))]

    def body(dp_ref, win_ref, wkr_ref, x_ref, dx2_ref, sc_ref, g1_ref, gx_ref, dsh_ref, dsc_ref, dg1_ref):
        @pl.when(pl.program_id(0) == 0)
        def _():
            dsh_ref[...] = jnp.zeros_like(dsh_ref)
            dsc_ref[...] = jnp.zeros_like(dsc_ref)
            dg1_ref[...] = jnp.zeros_like(dg1_ref)

        def half(k):
            return sum(_dot(dp_ref[:, a:b], (wkr_ref if from_kr else win_ref)[k, rows, :])
                       for a, b, from_kr, rows in parts)

        dh = jnp.concatenate([half(0), half(1)], axis=1)
        xt = x_ref[...]
        r = lax.rsqrt(jnp.mean(xt * xt, axis=-1, keepdims=True) + EPS)
        xh = xt * r
        g1 = g1_ref[...]
        xg = xh * g1
        dsh_ref[...] += jnp.sum(dh, axis=0, keepdims=True)
        dsc_ref[...] += jnp.sum(dh * xg, axis=0, keepdims=True)
        dxg = dh * (1.0 + sc_ref[...])
        dg1_ref[...] += jnp.sum(dxg * xh, axis=0, keepdims=True)
        dxh = dxg * g1
        gx_ref[...] = dx2_ref[...] + r * (dxh - xh * jnp.mean(dxh * xh, axis=-1, keepdims=True))

    return pl.pallas_call(
        body, name="dh", grid=(s_len // tm,),
        out_shape=(_sds((s_len, D_MODEL), F32), _sds((1, D_MODEL), F32), _sds((1, D_MODEL), F32),
                   _sds((1, D_MODEL), F32)),
        in_specs=[_rows(tm, W_INT), _whole((2, IN_WIDTH, HALF_D)), _whole((2, O_END - O_KR, HALF_D)),
                  _rows(tm, D_MODEL), _rows(tm, D_MODEL), _whole((1, D_MODEL)), _whole((1, D_MODEL))],
        out_specs=(_rows(tm, D_MODEL), _whole((1, D_MODEL)), _whole((1, D_MODEL)), _whole((1, D_MODEL))),
        compiler_params=_params(("arbitrary",)),
    )(dproj, w_in_t, w_kr, x, dx2, scale, g1)


def _small_call(svg, ct, dmod_sh):
    def body(sv_ref, ct_ref, dm_ref, tot_ref, gwada_ref):
        acc = sv_ref[0:1, :]
        for d in range(1, N_DEV):
            acc = acc + sv_ref[d:d + 1, :]
        tot_ref[...] = acc
        gwada_ref[...] = lax.dot_general(ct_ref[...], dm_ref[...], (((1,), (0,)), ((), ())),
                                         precision=lax.Precision.HIGHEST, preferred_element_type=F32)

    vmem = pl.BlockSpec(memory_space=pltpu.VMEM)
    return pl.pallas_call(
        body, name="small_grads",
        out_shape=(_sds((1, 8 * SV_COLS), F32), _sds((D_MODEL, 768), F32)),
        in_specs=[vmem, vmem, vmem], out_specs=(vmem, vmem),
        compiler_params=_params(),
    )(svg, ct, dmod_sh)


def _adamw_tile_rows(rows, cols):
    budget = 2 << 20
    if rows * cols * 4 <= budget or rows % 8:
        return rows
    best = 8
    for tr in range(8, rows + 1, 8):
        if rows % tr == 0 and tr * cols * 4 <= budget:
            best = tr
    return best


def _adamw_math(w, gg, m, v):
    m2 = ADAM_B1 * m + (1.0 - ADAM_B1) * gg
    v2 = ADAM_B2 * v + (1.0 - ADAM_B2) * (gg * gg)
    m_hat = m2 / (1.0 - ADAM_B1 ** ADAM_STEP)
    v_hat = v2 / (1.0 - ADAM_B2 ** ADAM_STEP)
    return -ADAM_LR * (m_hat / (jnp.sqrt(v_hat) + ADAM_EPS) + ADAM_WD * w), m2, v2


def _adamw_call(name, w, g, m, v):
    rows, cols = w.shape
    tr = _adamw_tile_rows(rows, cols)
    halves = g.ndim == 3

    def body(w_ref, g_ref, m_ref, v_ref, *out_refs):
        if halves:
            gg = jnp.concatenate([g_ref[0], g_ref[1]], axis=1)
            out_refs[0][...] = gg
        else:
            gg = g_ref[...]
        d_ref, nm_ref, nv_ref = out_refs[-3:]
        d_ref[...], nm_ref[...], nv_ref[...] = _adamw_math(w_ref[...], gg, m_ref[...], v_ref[...])

    spec = pl.BlockSpec((tr, cols), lambda i: (i, 0))
    g_spec = pl.BlockSpec((2, tr, cols // 2), lambda i: (0, i, 0)) if halves else spec
    n_out = 4 if halves else 3
    outs = pl.pallas_call(
        body, name="adamw_" + name, grid=(rows // tr,),
        out_shape=(_sds((rows, cols), F32),) * n_out,
        in_specs=[spec, g_spec, spec, spec], out_specs=(spec,) * n_out,
        compiler_params=_params(("parallel",)),
    )(w, g, m, v)
    return tuple(outs) if halves else (g,) + tuple(outs)


IN_SHARD = IN_WIDTH // N_CHIPS
HALF_D = D_MODEL // 2
SMALL_ROWS = (576, 512, 1024, 1024, 2048)
SMALL_TOTAL = sum(SMALL_ROWS)
SMALL_HALF = SMALL_TOTAL // 2
SMALL_SUM_ROWS = 432


def _gather_call(c_row, w_ada_sh, pack_in, pack_small):
    def body(c_ref, wada_ref, pki_ref, pks_ref, mg_ref, cg_ref, gwi_ref, gws_ref,
             cv, ssem_c, rsem_c, ssem_m, rsem_m, ssem_w, rsem_w, ssem_f, rsem_f, lsem):
        x, y, c = lax.axis_index("x"), lax.axis_index("y"), lax.axis_index("c")
        me = 4 * x + 2 * y + c
        chip = 2 * x + y
        rel3 = [(1, 0), (0, 1), (1, 1)]
        packs = [(pki_ref, gwi_ref), (pks_ref, gws_ref)]

        def slot(a, gw, k, h):
            return gw.at[h, k] if a == 0 else gw.at[k, h]

        sends = []
        for j, (dx, dy) in enumerate(rel3):
            for a, (pk, gw) in enumerate(packs):
                cp = pltpu.make_async_remote_copy(
                    src_ref=pk.at[c], dst_ref=slot(a, gw, chip, c), send_sem=ssem_w.at[j, a],
                    recv_sem=rsem_w.at[j, a], device_id=(_flip(x, dx), _flip(y, dy), c), device_id_type=MESH)
                cp.start()
                sends.append(cp)
        owns = []
        for a, (pk, gw) in enumerate(packs):
            for h in range(2):
                own = pltpu.make_async_copy(pk.at[h], slot(a, gw, chip, h), lsem.at[a, h])
                own.start()
                owns.append(own)

        cv[me] = c_ref[...]
        for r in range(1, N_DEV):
            dx, dy, dc = (r >> 2) & 1, (r >> 1) & 1, r & 1
            cp = pltpu.make_async_remote_copy(
                src_ref=c_ref, dst_ref=cv.at[me], send_sem=ssem_c.at[r - 1], recv_sem=rsem_c.at[r - 1],
                device_id=(_flip(x, dx), _flip(y, dy), _flip(c, dc)), device_id_type=MESH)
            cp.start()
            sends.append(cp)
        for r in range(1, N_DEV):
            dx, dy, dc = (r >> 2) & 1, (r >> 1) & 1, r & 1
            src = 4 * _flip(x, dx) + 2 * _flip(y, dy) + _flip(c, dc)
            pltpu.make_async_remote_copy(
                src_ref=c_ref, dst_ref=cv.at[src], send_sem=ssem_c.at[r - 1], recv_sem=rsem_c.at[r - 1],
                device_id=(x, y, c), device_id_type=MESH).wait_recv()
        rows = lax.broadcasted_iota(jnp.int32, (N_DEV, D_MODEL), 0)
        call = jnp.zeros((N_DEV, D_MODEL), F32)
        for b in range(N_DEV):
            call = jnp.where(rows == b, jnp.broadcast_to(cv[b], (N_DEV, D_MODEL)), call)
        cg_ref[...] = call

        mg_ref[chip] = lax.dot_general(call, wada_ref[...], (((1,), (0,)), ((), ())),
                                       precision=lax.Precision.HIGHEST, preferred_element_type=F32)
        for j, (dx, dy) in enumerate(rel3):
            cp = pltpu.make_async_remote_copy(
                src_ref=mg_ref.at[chip], dst_ref=mg_ref.at[chip], send_sem=ssem_m.at[j], recv_sem=rsem_m.at[j],
                device_id=(_flip(x, dx), _flip(y, dy), c), device_id_type=MESH)
            cp.start()
            sends.append(cp)
        for j, (dx, dy) in enumerate(rel3):
            src_chip = 2 * _flip(x, dx) + _flip(y, dy)
            pltpu.make_async_remote_copy(
                src_ref=mg_ref.at[src_chip], dst_ref=mg_ref.at[src_chip], send_sem=ssem_m.at[j],
                recv_sem=rsem_m.at[j], device_id=(x, y, c), device_id_type=MESH).wait_recv()
        for j, (dx, dy) in enumerate(rel3):
            src_chip = 2 * _flip(x, dx) + _flip(y, dy)
            for a, (pk, gw) in enumerate(packs):
                pltpu.make_async_remote_copy(
                    src_ref=pk.at[c], dst_ref=slot(a, gw, src_chip, c), send_sem=ssem_w.at[j, a],
                    recv_sem=rsem_w.at[j, a], device_id=(x, y, c), device_id_type=MESH).wait_recv()
                cp = pltpu.make_async_remote_copy(
                    src_ref=slot(a, gw, src_chip, c), dst_ref=slot(a, gw, src_chip, c), send_sem=ssem_f.at[j, a],
                    recv_sem=rsem_f.at[j, a], device_id=(x, y, 1 - c), device_id_type=MESH)
                cp.start()
                sends.append(cp)
        for j, (dx, dy) in enumerate(rel3):
            src_chip = 2 * _flip(x, dx) + _flip(y, dy)
            for a, (pk, gw) in enumerate(packs):
                pltpu.make_async_remote_copy(
                    src_ref=pk.at[c], dst_ref=slot(a, gw, src_chip, 1 - c), send_sem=ssem_f.at[j, a],
                    recv_sem=rsem_f.at[j, a], device_id=(x, y, c), device_id_type=MESH).wait_recv()
        for cp in sends:
            cp.wait_send()
        for own in owns:
            own.wait()

    vmem = pl.BlockSpec(memory_space=pltpu.VMEM)
    return pl.pallas_call(
        body, name="gather_fwd",
        out_shape=(_sds((N_CHIPS, N_DEV, 768), F32), _sds((N_DEV, D_MODEL), F32),
                   _sds((2, N_CHIPS, IN_SHARD, HALF_D), BF16), _sds((N_CHIPS, 2, SMALL_HALF, LANES), BF16)),
        in_specs=[vmem, vmem, vmem, vmem], out_specs=(vmem, vmem, vmem, vmem),
        scratch_shapes=[
            pltpu.VMEM((N_DEV, 1, D_MODEL), F32),
            pltpu.SemaphoreType.DMA((N_DEV - 1,)), pltpu.SemaphoreType.DMA((N_DEV - 1,)),
            pltpu.SemaphoreType.DMA((3,)), pltpu.SemaphoreType.DMA((3,)),
            pltpu.SemaphoreType.DMA((3, 2)), pltpu.SemaphoreType.DMA((3, 2)),
            pltpu.SemaphoreType.DMA((3, 2)), pltpu.SemaphoreType.DMA((3, 2)),
            pltpu.SemaphoreType.DMA((2, 2)),
        ],
        compiler_params=_params(),
    )(c_row, w_ada_sh, pack_in, pack_small)


def _reduce_call(g_in, g_small, sv):
    def body(gi_ref, gs_ref, sv_ref, fi_ref, fs_ref, svg_ref, pair_i, pair_s, send_i, send_s, land_i, land_s,
             ssem_p, rsem_p, ssem_g, rsem_g, ssem_s, rsem_s, ssem_x, rsem_x):
        x, y, c = lax.axis_index("x"), lax.axis_index("y"), lax.axis_index("c")
        me = 4 * x + 2 * y + c
        chip = 2 * x + y
        rel3 = [(1, 0), (0, 1), (1, 1)]
        payloads = [(gi_ref, pair_i, send_i, land_i, fi_ref), (gs_ref, pair_s, send_s, land_s, fs_ref)]
        where = [lambda k, h: N_CHIPS * h + k, lambda k, h: 2 * k + h]
        copies = []

        for k in range(N_CHIPS):
            for a, (g, pair, _, _, _) in enumerate(payloads):
                cp = pltpu.make_async_remote_copy(
                    src_ref=g.at[where[a](k, 1 - c)], dst_ref=pair.at[k], send_sem=ssem_p.at[k, a],
                    recv_sem=rsem_p.at[k, a], device_id=(x, y, 1 - c), device_id_type=MESH)
                cp.start()
                copies.append(cp)

        for r in range(1, N_DEV):
            dx, dy, dc = (r >> 2) & 1, (r >> 1) & 1, r & 1
            cp = pltpu.make_async_remote_copy(
                src_ref=sv_ref, dst_ref=svg_ref.at[me], send_sem=ssem_s.at[r - 1], recv_sem=rsem_s.at[r - 1],
                device_id=(_flip(x, dx), _flip(y, dy), _flip(c, dc)), device_id_type=MESH)
            cp.start()
            copies.append(cp)
        svg_ref[me] = sv_ref[...]

        def pair_sum(k, store_in, store_small):
            for a, (g, pair, _, _, _) in enumerate(payloads):
                pltpu.make_async_remote_copy(
                    src_ref=g.at[where[a](k, c)], dst_ref=pair.at[k], send_sem=ssem_p.at[k, a],
                    recv_sem=rsem_p.at[k, a], device_id=(x, y, c), device_id_type=MESH).wait_recv()
            for qd in range(HALF_D // LANES):
                sl = slice(LANES * qd, LANES * qd + LANES)
                store_in(sl, gi_ref[where[0](k, c), :, sl].astype(F32) + pair_i[k, :, sl].astype(F32))

            def rows(i, carry):
                sl = pl.ds(pl.multiple_of(i * SMALL_SUM_ROWS, 16), SMALL_SUM_ROWS)
                store_small(sl, gs_ref[where[1](k, c), sl, :].astype(F32) + pair_s[k, sl, :].astype(F32))
                return carry

            lax.fori_loop(0, SMALL_HALF // SMALL_SUM_ROWS, rows, 0)

        for j, (dx, dy) in enumerate(rel3):
            tx, ty = _flip(x, dx), _flip(y, dy)

            def put_in(sl, val, j=j):
                send_i[j, :, sl] = val.astype(BF16)

            def put_small(sl, val, j=j):
                send_s[j, sl, :] = val.astype(BF16)

            pair_sum(2 * tx + ty, put_in, put_small)
            for a, (_, _, send, land, _) in enumerate(payloads):
                cp = pltpu.make_async_remote_copy(
                    src_ref=send.at[j], dst_ref=land.at[j], send_sem=ssem_g.at[j, a], recv_sem=rsem_g.at[j, a],
                    device_id=(tx, ty, c), device_id_type=MESH)
                cp.start()
                copies.append(cp)

        def own_in(sl, val):
            fi_ref[c, :, sl] = val

        def own_small(sl, val):
            fs_ref[c, sl, :] = val

        pair_sum(chip, own_in, own_small)
        for j in range(3):
            for a, (_, _, send, land, _) in enumerate(payloads):
                pltpu.make_async_remote_copy(
                    src_ref=send.at[j], dst_ref=land.at[j], send_sem=ssem_g.at[j, a], recv_sem=rsem_g.at[j, a],
                    device_id=(x, y, c), device_id_type=MESH).wait_recv()
            for qd in range(HALF_D // LANES):
                sl = slice(LANES * qd, LANES * qd + LANES)
                fi_ref[c, :, sl] += land_i[j, :, sl].astype(F32)

            def add_rows(i, carry, j=j):
                sl = pl.ds(pl.multiple_of(i * SMALL_SUM_ROWS, 16), SMALL_SUM_ROWS)
                fs_ref[c, sl, :] += land_s[j, sl, :].astype(F32)
                return carry

            lax.fori_loop(0, SMALL_HALF // SMALL_SUM_ROWS, add_rows, 0)

        for a, f in enumerate((fi_ref, fs_ref)):
            cp = pltpu.make_async_remote_copy(
                src_ref=f.at[c], dst_ref=f.at[c], send_sem=ssem_x.at[a], recv_sem=rsem_x.at[a],
                device_id=(x, y, 1 - c), device_id_type=MESH)
            cp.start()
            copies.append(cp)
        for a, f in enumerate((fi_ref, fs_ref)):
            pltpu.make_async_remote_copy(
                src_ref=f.at[c], dst_ref=f.at[1 - c], send_sem=ssem_x.at[a], recv_sem=rsem_x.at[a],
                device_id=(x, y, c), device_id_type=MESH).wait_recv()
        for r in range(1, N_DEV):
            dx, dy, dc = (r >> 2) & 1, (r >> 1) & 1, r & 1
            src = 4 * _flip(x, dx) + 2 * _flip(y, dy) + _flip(c, dc)
            pltpu.make_async_remote_copy(
                src_ref=sv_ref, dst_ref=svg_ref.at[src], send_sem=ssem_s.at[r - 1],
                recv_sem=rsem_s.at[r - 1], device_id=(x, y, c), device_id_type=MESH).wait_recv()
        for cp in copies:
            cp.wait_send()

    vmem = pl.BlockSpec(memory_space=pltpu.VMEM)
    return pl.pallas_call(
        body, name="grad_reduce",
        out_shape=(_sds((2, IN_SHARD, HALF_D), F32), _sds((2, SMALL_HALF, LANES), F32),
                   _sds((N_DEV, 8, SV_COLS), F32)),
        in_specs=[vmem, vmem, vmem], out_specs=(vmem, vmem, vmem),
        scratch_shapes=[
            pltpu.VMEM((N_CHIPS, IN_SHARD, HALF_D), BF16), pltpu.VMEM((N_CHIPS, SMALL_HALF, LANES), BF16),
            pltpu.VMEM((3, IN_SHARD, HALF_D), BF16), pltpu.VMEM((3, SMALL_HALF, LANES), BF16),
            pltpu.VMEM((3, IN_SHARD, HALF_D), BF16), pltpu.VMEM((3, SMALL_HALF, LANES), BF16),
            pltpu.SemaphoreType.DMA((N_CHIPS, 2)), pltpu.SemaphoreType.DMA((N_CHIPS, 2)),
            pltpu.SemaphoreType.DMA((3, 2)), pltpu.SemaphoreType.DMA((3, 2)),
            pltpu.SemaphoreType.DMA((N_DEV - 1,)), pltpu.SemaphoreType.DMA((N_DEV - 1,)),
            pltpu.SemaphoreType.DMA((2,)), pltpu.SemaphoreType.DMA((2,)),
        ],
        compiler_params=_params(),
    )(g_in, g_small, sv)


def _dwin_call(h, dproj):
    s_len = h.shape[0]
    tm = min(4 * ROW_TILE, s_len)
    nrow = s_len // tm
    nc = 4
    chunk = W_INT // nc

    def body(h_ref, dp_ref, dw_ref, acc):
        c, i = pl.program_id(0), pl.program_id(1)

        @pl.when(i == 0)
        def _():
            acc[...] = jnp.zeros_like(acc)

        acc[...] += _dot_tn(dp_ref[...], h_ref[...])

        def put(lo, hi, dst):
            for half in range(2):
                dw_ref[half, dst:dst + hi - lo, :] = acc[lo:hi, half * HALF_D:(half + 1) * HALF_D].astype(BF16)

        for cc in range(nc):
            @pl.when((c == cc) & (i == nrow - 1))
            def _(cc=cc):
                base = cc * chunk
                lo, hi = base, min(base + chunk, O_MZ)
                if lo < hi:
                    put(lo - base, hi - base, lo)
                lo, hi = max(base, O_MZ), min(base + chunk, O_KR)
                if lo < hi:
                    put(lo - base, hi - base, lo + ROPE_DIM)
                if base <= O_KR and O_END <= base + chunk:
                    r = O_KR - base
                    half_r = ROPE_DIM // 2
                    kr = sum(acc[r + ROPE_DIM * j:r + ROPE_DIM * (j + 1), :] for j in range(4))
                    sw = sum(acc[r + 128 + ROPE_DIM * j:r + 128 + ROPE_DIM * (j + 1), :] for j in range(4))
                    tot = kr + jnp.concatenate([sw[half_r:], sw[:half_r]], axis=0)
                    for half in range(2):
                        dw_ref[half, O_MZ:O_MZ + ROPE_DIM, :] = (
                            tot[:, half * HALF_D:(half + 1) * HALF_D].astype(BF16))

    assert O_KR >= (nc - 1) * chunk
    return pl.pallas_call(
        body, name="dwin", grid=(nc, nrow),
        out_shape=_sds((2, IN_WIDTH, HALF_D), BF16),
        in_specs=[pl.BlockSpec((tm, D_MODEL), lambda c, i: (i, 0)),
                  pl.BlockSpec((tm, chunk), lambda c, i: (i, c))],
        out_specs=pl.BlockSpec((2, IN_WIDTH, HALF_D), lambda c, i: (0, 0, 0)),
        scratch_shapes=[pltpu.VMEM((chunk, D_MODEL), F32)],
        compiler_params=_params(("arbitrary", "arbitrary")),
    )(h, dproj)


def _internal_weights(w_in_t, w_uq, w_ukv):
    krot_t = w_in_t[:, O_MZ:O_MZ + ROPE_DIM]
    krot_sw = krot_t.reshape(2, 2, ROPE_DIM // 2, HALF_D)[:, ::-1].reshape(2, ROPE_DIM, HALF_D)
    w_kr = jnp.concatenate([jnp.tile(krot_t, (1, 4, 1)), jnp.tile(krot_sw, (1, 4, 1))], axis=1)
    uq = w_uq.reshape(Q_RANK, N_HEADS, 96)
    wp = uq[:, :, 64:].reshape(Q_RANK, 256)
    w_q = jnp.concatenate([uq[:, :, :64].reshape(Q_RANK, 512), wp, _swap_halves(wp, 32)], axis=1)
    ukv = w_ukv.reshape(KV_RANK, N_HEADS, 128)
    w_kv = jnp.concatenate([ukv[:, :, :64].reshape(KV_RANK, 512), ukv[:, :, 64:].reshape(KV_RANK, 512)], axis=1)
    return w_kr, w_q, w_kv


def _true_weight_grads(dwq, dwkv):
    dwp = dwq[:, 512:768] + _swap_halves(dwq[:, 768:1024], 32)
    g_uq = jnp.concatenate([dwq[:, :512].reshape(Q_RANK, N_HEADS, 64), dwp.reshape(Q_RANK, N_HEADS, 32)],
                           axis=2).reshape(Q_RANK, 768)
    g_ukv = jnp.concatenate([dwkv[:, :512].reshape(KV_RANK, N_HEADS, 64), dwkv[:, 512:].reshape(KV_RANK, N_HEADS, 64)],
                            axis=2).reshape(KV_RANK, 1024)
    return g_uq, g_ukv


def _swap_halves(w, group):
    r, n = w.shape
    return w.reshape(r, n // group, 2, group // 2)[:, :, ::-1, :].reshape(r, n)


def _pack_shards(parts):
    return jnp.concatenate([p.reshape(-1, LANES) for p in parts], axis=0)


def _unpack_small(gw):
    offs = [0]
    for r in SMALL_ROWS:
        offs.append(offs[-1] + r)

    def cols(i, rows, shard_cols):
        blk = gw[:, offs[i]:offs[i + 1]].reshape(N_CHIPS, rows, shard_cols)
        return blk.transpose(1, 0, 2).reshape(rows, N_CHIPS * shard_cols)

    return (cols(0, Q_RANK, 192), cols(1, KV_RANK, 256), cols(2, 512, 256), cols(3, 512, 256),
            gw[:, offs[4]:offs[5]].reshape(D_MODEL, D_MODEL))


def _chip_major(g, shard_cols):
    r = g.shape[0]
    return g.reshape(r, N_CHIPS, shard_cols).transpose(1, 0, 2).reshape(N_CHIPS, -1, LANES)


def kernel(x, c, positions, w_ada, b_ada, norm_gain, w_in, q_norm_gain, w_uq, kv_norm_gain, w_ukv, w_branch_a, w_branch_b, w_out, final_norm_gain, loss_target, m_w_ada, m_b_ada, m_norm_gain, m_w_in, m_q_norm_gain, m_w_uq, m_kv_norm_gain, m_w_ukv, m_w_branch_a, m_w_branch_b, m_w_out, m_final_norm_gain, v_w_ada, v_b_ada, v_norm_gain, v_w_in, v_q_norm_gain, v_w_uq, v_kv_norm_gain, v_w_ukv, v_w_branch_a, v_w_branch_b, v_w_out, v_final_norm_gain):
    ix, iy, ic = lax.axis_index("x"), lax.axis_index("y"), lax.axis_index("c")
    me = 4 * ix + 2 * iy + ic
    chip = 2 * ix + iy
    xs = x[0]
    tgt = loss_target[0]
    s_len = xs.shape[0]

    w_in_t = jnp.swapaxes(w_in[0], 0, 1)
    w_in_tb = w_in_t.astype(BF16)
    pack_in = jnp.stack([w_in_tb[:, :HALF_D], w_in_tb[:, HALF_D:]], axis=0)
    small_shards = (w_uq[0], w_ukv[0], w_branch_a[0], w_branch_b[0], w_out[0])
    pack_small = _pack_shards([s.astype(BF16) for s in small_shards]).reshape(2, SMALL_HALF, LANES)
    mg, call, gw_in, gw_small = _gather_call(c, w_ada[0], pack_in, pack_small)
    mod = mg.transpose(1, 0, 2).reshape(N_DEV, 3 * D_MODEL) + b_ada
    mod_me = lax.dynamic_slice_in_dim(mod, me, 1, axis=0)
    shift, scale, gate = mod_me[:, :D_MODEL], mod_me[:, D_MODEL:2 * D_MODEL], mod_me[:, 2 * D_MODEL:]

    f_in_t = gw_in.reshape(2, IN_WIDTH, HALF_D)
    f_uq, f_ukv, f_a, f_b, f_out = _unpack_small(gw_small.reshape(N_CHIPS, SMALL_TOTAL, LANES))
    w_kr, w_q, w_kv = _internal_weights(f_in_t, f_uq, f_ukv)

    inv_freq = ROPE_BASE ** (-jnp.arange(0, ROPE_DIM, 2, dtype=F32) / ROPE_DIM)
    ang = positions[0].astype(F32)[:, None] * inv_freq
    cs, sn = jnp.cos(ang), jnp.sin(ang)
    cos128 = jnp.tile(jnp.concatenate([cs, cs], axis=1), (1, 4))
    sin128 = jnp.tile(jnp.concatenate([-sn, sn], axis=1), (1, 4))

    (h, sq, sk, sv, sz, cq, ckv, mz, ga, gb, kpt, qn, qp, kn, vv) = _inproj_call(
        xs, shift, scale, norm_gain, f_in_t, w_kr, w_q, w_kv, q_norm_gain, kv_norm_gain, cos128, sin128)
    oa, lt, first = _sb_fwd_call(sq, sk, sv)
    ob, lse = _mla_fwd_call(qn, qp, kn, kpt, vv)

    gf = final_norm_gain.reshape(1, D_MODEL)
    (dx2, doa, dob, dsz, dmz, dga, dgb, dwo, dwa, dwb, dgf, dgate, loss_p) = _post_call(
        xs, tgt, oa, ob, sz, mz, ga, gb, gate, gf, f_a, f_b, f_out)

    dsq, dsk_t, dsv_t = _sb_bwd_call(first[:, :, 0, 0].reshape(-1), sq, sk, sv, doa, lt)
    dqn, dqp, dkn_t, dkpt_t, dvv_t = _mla_bwd_call(qn, qp, kn, kpt, vv, ob, dob, lse)

    dproj, dwq, dwkv, dqg, dkvg = _bwdprep_call(
        dsq, dsk_t, dsv_t, dsz, dqn, dqp, dkn_t, dvv_t, dkpt_t, dmz, dga, dgb, cq, ckv, cos128, sin128,
        q_norm_gain, kv_norm_gain, w_q, w_kv)
    grad_x, dshift, dscale, dg1 = _dh_call(dproj, f_in_t, w_kr, xs, dx2, scale, norm_gain)
    g_in_t = _dwin_call(h, dproj)
    g_uq, g_ukv = _true_weight_grads(dwq, dwkv)

    g_in_pieces = g_in_t.reshape(N_DEV, IN_SHARD, HALF_D)
    g_small = jnp.concatenate([
        _chip_major(g_uq, 192).astype(BF16), _chip_major(g_ukv, 256).astype(BF16),
        dwa.reshape(N_CHIPS, -1, LANES), dwb.reshape(N_CHIPS, -1, LANES),
        dwo.reshape(N_CHIPS, -1, LANES)], axis=1).reshape(N_DEV, SMALL_HALF, LANES)
    small = jnp.concatenate([
        dshift, dscale, dgate, dg1, dqg, dkvg, dgf, loss_p,
        jnp.zeros((1, 8 * SV_COLS - 5888), F32)], axis=1).reshape(8, SV_COLS)
    full_in, full_small, svg = _reduce_call(g_in_pieces, g_small, small)
    full = full_small.reshape(SMALL_TOTAL, LANES)
    offs = [0]
    for r in SMALL_ROWS:
        offs.append(offs[-1] + r)
    gs_uq = full[offs[0]:offs[1]].reshape(Q_RANK, 192)
    gs_ukv = full[offs[1]:offs[2]].reshape(KV_RANK, 256)
    gs_a = full[offs[2]:offs[3]].reshape(512, 256)
    gs_b = full[offs[3]:offs[4]].reshape(512, 256)
    gs_out = full[offs[4]:offs[5]].reshape(256, D_MODEL)

    svm = svg.reshape(N_DEV, 8 * SV_COLS)
    dmod_sh = lax.dynamic_slice_in_dim(svm[:, :3 * D_MODEL], chip * 768, 768, axis=1)
    tot, gs_ada = _small_call(svm, call.T, dmod_sh)
    g_bada = tot[:, 0:3072]
    g_g1 = tot[:, 3072:4096]
    g_qg = tot[:, 4096:4480]
    g_kvg = tot[:, 4480:4736]
    g_gf = tot[:, 4736:5760]
    loss = tot[0, 5760]

    names = ["w_ada", "b_ada", "norm_gain", "w_in", "q_norm_gain", "w_uq", "kv_norm_gain", "w_ukv",
             "w_branch_a", "w_branch_b", "w_out", "final_norm_gain"]
    ws = [w_ada[0], b_ada, norm_gain, w_in_t, q_norm_gain, w_uq[0], kv_norm_gain, w_ukv[0],
          w_branch_a[0], w_branch_b[0], w_out[0], final_norm_gain.reshape(1, D_MODEL)]
    gs = [gs_ada, g_bada, g_g1, full_in, g_qg, gs_uq, g_kvg, gs_ukv, gs_a, gs_b, gs_out, g_gf]
    ms = [m_w_ada[0], m_b_ada, m_norm_gain, jnp.swapaxes(m_w_in[0], 0, 1), m_q_norm_gain, m_w_uq[0],
          m_kv_norm_gain, m_w_ukv[0], m_w_branch_a[0], m_w_branch_b[0], m_w_out[0],
          m_final_norm_gain.reshape(1, D_MODEL)]
    vs = [v_w_ada[0], v_b_ada, v_norm_gain, jnp.swapaxes(v_w_in[0], 0, 1), v_q_norm_gain, v_w_uq[0],
          v_kv_norm_gain, v_w_ukv[0], v_w_branch_a[0], v_w_branch_b[0], v_w_out[0],
          v_final_norm_gain.reshape(1, D_MODEL)]
    refs = [w_ada, b_ada, norm_gain, w_in, q_norm_gain, w_uq, kv_norm_gain, w_ukv,
            w_branch_a, w_branch_b, w_out, final_norm_gain]
    grads, deltas, new_ms, new_vs = [], [], [], []
    for n, w_, g_, m_, v_, ref in zip(names, ws, gs, ms, vs, refs):
        outs = _adamw_call(n, w_, g_, m_, v_)
        if n == "w_in":
            outs = tuple(jnp.swapaxes(o_, 0, 1) for o_ in outs)
        for lst, o_ in zip((grads, deltas, new_ms, new_vs), outs):
            lst.append(o_.reshape(ref.shape))

    return (loss, grad_x.reshape(x.shape), *grads, *deltas, *new_ms, *new_vs)
```

```python
import math

import jax
import jax.numpy as jnp
from jax import lax
from jax.experimental import pallas as pl
from jax.experimental.pallas import tpu as pltpu

F32 = jnp.float32
BF16 = jnp.bfloat16

D_MODEL = 1024
SB_WIDTH = 512
MLA_WIDTH = 512
Q_RANK = 384
KV_RANK = 256
ROPE_DIM = 32
N_HEADS = 8
IN_WIDTH = 5280
EPS = 1e-6
ROPE_BASE = 10000.0
MLA_SCALE = 1.0 / math.sqrt(96.0)
SB_SCALE = 0.125
LOG2E = 1.4426950408889634

ADAM_LR = 0.001
ADAM_B1 = 0.9
ADAM_B2 = 0.999
ADAM_EPS = 1e-08
ADAM_WD = 0.01
ADAM_STEP = 10

O_SQ, O_SK, O_SV, O_SZ, O_CQ, O_CKV, O_MZ, O_GA, O_GB, O_KR, O_END = (
    0, 512, 1024, 1536, 2048, 2432, 2688, 3200, 4224, 5248, 5504)
W_INT = 5632

N_CHIPS = 4
N_DEV = 8
LANES = 128
SV_COLS = 768

ROW_TILE = 256
ATT_TILE = 256
ATT_Q_TILES = 2
FWD_Q_TILES = 4
SB_Q_TILES = 1
MLA_KEY_TILE = 512
VMEM_LIMIT = 56 * 1024 * 1024

MESH = pl.DeviceIdType.MESH


def _dot(a, b):
    return lax.dot_general(a, b, (((1,), (0,)), ((), ())), preferred_element_type=F32)


def _dot_nt(a, b):
    return lax.dot_general(a, b, (((1,), (1,)), ((), ())), preferred_element_type=F32)


def _dot_tn(a, b):
    return lax.dot_general(a, b, (((0,), (0,)), ((), ())), preferred_element_type=F32)


def _sigmoid(z):
    return 1.0 / (1.0 + jnp.exp2(z * (-LOG2E)))


def _params(sem=None):
    if sem is None:
        return pltpu.CompilerParams(vmem_limit_bytes=VMEM_LIMIT)
    return pltpu.CompilerParams(dimension_semantics=sem, vmem_limit_bytes=VMEM_LIMIT)


def _rows(tm, n):
    return pl.BlockSpec((tm, n), lambda i: (i, 0))


def _cols(n, tm):
    return pl.BlockSpec((n, tm), lambda i: (0, i))


def _whole(shape):
    nd = len(shape)
    return pl.BlockSpec(shape, lambda i: (0,) * nd)


def _sds(shape, dtype):
    return jax.ShapeDtypeStruct(shape, dtype)


def _flip(v, d):
    return 1 - v if d else v


def _weight_rows(a, b):
    if a >= O_KR:
        return True, slice(a - O_KR, b - O_KR)
    shift = ROPE_DIM if a >= O_MZ else 0
    return False, slice(a + shift, b + shift)


def _inproj_call(x, shift, scale, g1, w_in_t, w_kr, w_q, w_kv, qg, kvg, cos128, sin128):
    s_len = x.shape[0]
    tm = min(ROW_TILE, s_len)

    def body(x_ref, sh_ref, sc_ref, g1_ref, win_ref, wkr_ref, wq_ref, wkv_ref, qg_ref, kvg_ref, cos_ref, sin_ref,
             h_ref, sq_ref, sk_ref, sv_ref, sz_ref, cq_ref, ckv_ref, mz_ref, ga_ref, gb_ref, kpt_ref,
             qn_ref, qp_ref, kn_ref, vv_ref):
        xt = x_ref[...]
        r = lax.rsqrt(jnp.mean(xt * xt, axis=-1, keepdims=True) + EPS)
        h = (xt * r * g1_ref[...]) * (1.0 + sc_ref[...]) + sh_ref[...]
        hb = h.astype(BF16)
        h_ref[...] = hb

        def seg(a, b):
            from_kr, rows = _weight_rows(a, b)
            w_ref = wkr_ref if from_kr else win_ref
            return _dot_nt(hb[:, :HALF_D], w_ref[0, rows, :]) + _dot_nt(hb[:, HALF_D:], w_ref[1, rows, :])

        sq_ref[...] = (seg(O_SQ, O_SK) * SB_SCALE).astype(BF16)
        sk_ref[...] = seg(O_SK, O_SV).astype(BF16)
        sv_ref[...] = seg(O_SV, O_SZ).astype(BF16)
        sz_ref[...] = seg(O_SZ, O_CQ)
        mz_ref[...] = seg(O_MZ, O_GA)
        ga_ref[...] = seg(O_GA, O_GB)
        gb_ref[...] = seg(O_GB, O_KR)
        cos = cos_ref[...]
        sin = sin_ref[...]
        kr = seg(O_KR, O_END)
        kpt_ref[...] = (kr[:, :128] * cos + kr[:, 128:] * sin).astype(BF16)

        cq = seg(O_CQ, O_CKV)
        cq_ref[...] = cq
        rq = lax.rsqrt(jnp.mean(cq * cq, axis=-1, keepdims=True) + EPS)
        cqn = (cq * rq * qg_ref[...]).astype(BF16)
        qa = _dot(cqn, wq_ref[...])
        qn_ref[...] = qa[:, :512].astype(BF16)
        qp_ref[...] = (qa[:, 512:768] * jnp.tile(cos, (1, 2)) + qa[:, 768:] * jnp.tile(sin, (1, 2))).astype(BF16)

        ckv = seg(O_CKV, O_MZ)
        ckv_ref[...] = ckv
        rk = lax.rsqrt(jnp.mean(ckv * ckv, axis=-1, keepdims=True) + EPS)
        ckvn = (ckv * rk * kvg_ref[...]).astype(BF16)
        kva = _dot(ckvn, wkv_ref[...])
        kn_ref[...] = kva[:, :512].astype(BF16)
        vv_ref[...] = kva[:, 512:].astype(BF16)

    outs = [
        (D_MODEL, BF16), (512, BF16), (512, BF16), (512, BF16), (512, F32), (Q_RANK, F32), (KV_RANK, F32),
        (512, F32), (D_MODEL, F32), (D_MODEL, F32), (128, BF16), (512, BF16), (256, BF16), (512, BF16), (512, BF16),
    ]
    return pl.pallas_call(
        body, name="inproj", grid=(s_len // tm,),
        out_shape=tuple(_sds((s_len, n), dt) for n, dt in outs),
        in_specs=[_rows(tm, D_MODEL), _whole((1, D_MODEL)), _whole((1, D_MODEL)), _whole((1, D_MODEL)),
                  _whole((2, IN_WIDTH, HALF_D)), _whole((2, O_END - O_KR, HALF_D)),
                  _whole((Q_RANK, 1024)), _whole((KV_RANK, 1024)),
                  _whole((1, Q_RANK)), _whole((1, KV_RANK)), _rows(tm, LANES), _rows(tm, LANES)],
        out_specs=tuple(_rows(tm, n) for n, _ in outs),
        compiler_params=_params(("parallel",)),
    )(x, shift, scale, g1, w_in_t, w_kr, w_q, w_kv, qg, kvg, cos128, sin128)


Z_CLAMP = 80.0 * LOG2E
RUN_CUTOFF = 110.0 * LOG2E


def _softplus_clamped(z):
    zc = jnp.minimum(z * LOG2E, Z_CLAMP)
    return zc, jnp.log2(1.0 + jnp.exp2(zc))


def _tri_sum(a, tri):
    return _dot(a.astype(BF16), tri)


def _sb_fwd_call(q, k, v):
    s_len = q.shape[0]
    tk = min(ATT_TILE, s_len)
    tq = min(SB_Q_TILES * ATT_TILE, s_len)
    r = tq // tk
    nq = s_len // tq

    def body(q_ref, k_ref, v_ref, o_ref, lt_ref, first_ref):
        i = pl.program_id(1)
        q2 = q_ref[...]
        lane = lax.broadcasted_iota(jnp.int32, (1, 256), 1)
        krow = lax.broadcasted_iota(jnp.int32, (tk, tk), 0)
        kcol = lax.broadcasted_iota(jnp.int32, (tk, tk), 1)
        row = lax.broadcasted_iota(jnp.int32, (tq, tk), 0)
        col = lax.broadcasted_iota(jnp.int32, (tq, tk), 1)
        later = (krow > kcol).astype(BF16)
        valids = [col + u * tk < row for u in range(r)]
        hms = [(lane // 64) == hh for hh in range(4)]
        qms = [jnp.where(hm, q2, jnp.zeros_like(q2)) for hm in hms]

        def block(j, carry, valid):
            runs, acc = list(carry[:4]), carry[4]
            off = pl.multiple_of(j * tk, tk)
            kb = k_ref[pl.ds(off, tk), :]
            vb = v_ref[pl.ds(off, tk), :]
            ws = []
            for hh in range(4):
                zc, sp = _softplus_clamped(_dot_nt(qms[hh], kb))
                lm = jnp.where(valid, sp, 0.0) if valid is not None else sp
                suf = _tri_sum(lm, later)
                w = jnp.exp2(zc - sp - suf - runs[hh])
                if valid is not None:
                    w = jnp.where(valid, w, 0.0)
                ws.append(w.astype(BF16))
                runs[hh] = runs[hh] + jnp.sum(lm, axis=1, keepdims=True)
            vstack = jnp.concatenate([jnp.where(hm, vb, jnp.zeros_like(vb)) for hm in hms], axis=0)
            acc = acc + _dot(jnp.concatenate(ws, axis=1), vstack)
            return (*runs, acc)

        zero = jnp.zeros((tq, 1), F32)
        carry = (zero, zero, zero, zero, jnp.zeros((tq, 256), F32))
        for u in reversed(range(r)):
            carry = block(i * r + u, carry, valids[u])

        def least_run(runs):
            return jnp.min(jnp.minimum(jnp.minimum(runs[0], runs[1]), jnp.minimum(runs[2], runs[3])))

        n_full = i * r

        def unfinished(state):
            return jnp.logical_and(state[0] < n_full, state[1] <= RUN_CUTOFF)

        def visit(state):
            cr = block(n_full - 1 - state[0], state[2:], None)
            return (state[0] + 1, least_run(cr[:4]), *cr)

        state = lax.while_loop(unfinished, visit, (jnp.int32(0), least_run(carry[:4]), *carry))
        carry = state[2:]
        first_ref[...] = jnp.full(first_ref.shape, n_full - state[0], jnp.int32)
        for hh in range(4):
            lt_ref[0, :, hh:hh + 1] = carry[hh]
        o_ref[...] = carry[4]

    return pl.pallas_call(
        body, name="sb_fwd", grid=(2, nq),
        out_shape=(_sds((s_len, SB_WIDTH), F32), _sds((2, s_len, 4), F32), _sds((2, nq, 8, 128), jnp.int32)),
        in_specs=[pl.BlockSpec((tq, 256), lambda g, i: (i, g)),
                  pl.BlockSpec((s_len, 256), lambda g, i: (0, g)),
                  pl.BlockSpec((s_len, 256), lambda g, i: (0, g))],
        out_specs=(pl.BlockSpec((tq, 256), lambda g, i: (i, g)),
                   pl.BlockSpec((1, tq, 4), lambda g, i: (g, i, 0)),
                   pl.BlockSpec((1, 1, 8, 128), lambda g, i: (g, i, 0, 0))),
        compiler_params=_params(("parallel", "parallel")),
    )(q, k, v)


def _sb_bwd_call(first, q, k, v, do, lt):
    s_len = q.shape[0]
    tk = min(ATT_TILE, s_len)
    tq = min(SB_Q_TILES * ATT_TILE, s_len)
    r = tq // tk
    nq = s_len // tq
    nq_fwd = first.shape[0] // 2
    per_fwd = nq // nq_fwd

    def body(first_ref, q_ref, k_ref, v_ref, do_ref, lt_ref, dq_ref, dk_ref, dv_ref):
        g = pl.program_id(0)
        i = pl.program_id(1)

        @pl.when(i == 0)
        def _():
            dk_ref[...] = jnp.zeros_like(dk_ref)
            dv_ref[...] = jnp.zeros_like(dv_ref)

        q2 = q_ref[...]
        do2 = do_ref[...].astype(BF16)
        lane = lax.broadcasted_iota(jnp.int32, (1, 256), 1)
        krow = lax.broadcasted_iota(jnp.int32, (tk, tk), 0)
        kcol = lax.broadcasted_iota(jnp.int32, (tk, tk), 1)
        row = lax.broadcasted_iota(jnp.int32, (tq, tk), 0)
        col = lax.broadcasted_iota(jnp.int32, (tq, tk), 1)
        earlier = (krow < kcol).astype(BF16)
        later = (krow > kcol).astype(BF16)
        valids = [col + u * tk < row for u in range(r)]
        hms = [(lane // 64) == hh for hh in range(4)]
        qms = [jnp.where(hm, q2, jnp.zeros_like(q2)) for hm in hms]
        doms = [jnp.where(hm, do2, jnp.zeros_like(do2)) for hm in hms]
        ltots = [lt_ref[0, :, hh:hh + 1] for hh in range(4)]
        q2t = jnp.transpose(q2.astype(F32))
        do2t = jnp.transpose(do_ref[...])
        subl = lax.broadcasted_iota(jnp.int32, (256, 1), 0)
        qtstack = jnp.concatenate(
            [jnp.where((subl // 64) == hh, q2t, 0.0).astype(BF16) for hh in range(4)], axis=1)
        dotstack = jnp.concatenate(
            [jnp.where((subl // 64) == hh, do2t, 0.0).astype(BF16) for hh in range(4)], axis=1)

        def block(j, carry, valid):
            lpre, ppre, dq = list(carry[0:4]), list(carry[4:8]), carry[8]
            off = pl.multiple_of(j * tk, tk)
            kb = k_ref[pl.ds(off, tk), :]
            vb = v_ref[pl.ds(off, tk), :]
            dzs, avs = [], []
            for hh in range(4):
                zc, sp = _softplus_clamped(_dot_nt(qms[hh], kb))
                lsig = zc - sp
                lm = jnp.where(valid, sp, 0.0) if valid is not None else sp
                rowsum = jnp.sum(lm, axis=1, keepdims=True)
                between = _tri_sum(lm, later) + ((ltots[hh] - lpre[hh]) - rowsum)
                a = jnp.exp2(lsig - between)
                if valid is not None:
                    a = jnp.where(valid, a, 0.0)
                p = a * _dot_nt(doms[hh], vb)
                pbefore = ppre[hh] + _tri_sum(p, earlier)
                dz = p - jnp.exp2(lsig) * (p + pbefore)
                if valid is not None:
                    dz = jnp.where(valid, dz, 0.0)
                dzs.append(dz.astype(BF16))
                avs.append(a.astype(BF16))
                lpre[hh] = lpre[hh] + rowsum
                ppre[hh] = ppre[hh] + jnp.sum(p, axis=1, keepdims=True)
            kstack = jnp.concatenate([jnp.where(hm, kb, jnp.zeros_like(kb)) for hm in hms], axis=0)
            dq = dq + _dot(jnp.concatenate(dzs, axis=1), kstack)
            dk_ref[:, pl.ds(off, tk)] += _dot(qtstack, jnp.concatenate(dzs, axis=0))
            dv_ref[:, pl.ds(off, tk)] += _dot(dotstack, jnp.concatenate(avs, axis=0))
            return (*lpre, *ppre, dq)

        zero = jnp.zeros((tq, 1), F32)
        start = jnp.minimum(first_ref[g * nq_fwd + i // per_fwd], i * r)
        carry = lax.fori_loop(start, i * r, lambda j, cr: block(j, cr, None),
                              (zero,) * 8 + (jnp.zeros((tq, 256), F32),))
        for u in range(r):
            carry = block(i * r + u, carry, valids[u])
        dq_ref[...] = carry[8].astype(BF16)

    return pl.pallas_call(
        body, name="sb_bwd",
        out_shape=(_sds((s_len, SB_WIDTH), BF16), _sds((SB_WIDTH, s_len), F32), _sds((SB_WIDTH, s_len), F32)),
        grid_spec=pltpu.PrefetchScalarGridSpec(
            num_scalar_prefetch=1, grid=(2, nq),
            in_specs=[pl.BlockSpec((tq, 256), lambda g, i, f: (i, g)),
                      pl.BlockSpec((s_len, 256), lambda g, i, f: (0, g)),
                      pl.BlockSpec((s_len, 256), lambda g, i, f: (0, g)),
                      pl.BlockSpec((tq, 256), lambda g, i, f: (i, g)),
                      pl.BlockSpec((1, tq, 4), lambda g, i, f: (g, i, 0))],
            out_specs=(pl.BlockSpec((tq, 256), lambda g, i, f: (i, g)),
                       pl.BlockSpec((256, s_len), lambda g, i, f: (g, 0)),
                       pl.BlockSpec((256, s_len), lambda g, i, f: (g, 0)))),
        compiler_params=_params(("parallel", "arbitrary")),
    )(first, q, k, v, do, lt)


def _mla_fwd_call(qn, qp, kn, kpt, v):
    s_len = qn.shape[0]
    tk = min(MLA_KEY_TILE, s_len)
    tq = min(FWD_Q_TILES * ATT_TILE, s_len)
    r = tq // tk
    nq = s_len // tq

    def body(qn_ref, qp_ref, kn_ref, kpt_ref, v_ref, o_ref, lse_ref):
        i = pl.program_id(1)
        qn2 = qn_ref[...]
        qp2 = qp_ref[...]
        lane256 = lax.broadcasted_iota(jnp.int32, (1, 256), 1)
        lane128 = lax.broadcasted_iota(jnp.int32, (1, 128), 1)
        krow = lax.broadcasted_iota(jnp.int32, (tk, tk), 0)
        kcol = lax.broadcasted_iota(jnp.int32, (tk, tk), 1)
        row = lax.broadcasted_iota(jnp.int32, (tq, tk), 0)
        col = lax.broadcasted_iota(jnp.int32, (tq, tk), 1)
        valids = [col + u * tk <= row for u in range(r)]
        m64s = [(lane256 // 64) == hh for hh in range(4)]
        half = [(lane128 // 64) == u for u in range(2)]
        m32s = [(lane128 // 32) == hh for hh in range(4)]
        qcs = []
        for hh in range(4):
            qpair = qn2[:, 128 * (hh // 2):128 * (hh // 2) + 128]
            qcs.append(jnp.concatenate([jnp.where(half[hh % 2], qpair, jnp.zeros_like(qpair)),
                                        jnp.where(m32s[hh], qp2, jnp.zeros_like(qp2))], axis=1))

        def by_head(vals):
            return jnp.where(m64s[0], vals[0], jnp.where(m64s[1], vals[1], jnp.where(m64s[2], vals[2], vals[3])))

        def block(j, carry, valid):
            ms, ls, acc = list(carry[0:4]), list(carry[4:8]), carry[8]
            off = pl.multiple_of(j * tk, tk)
            knb = kn_ref[pl.ds(off, tk), :]
            kpb = kpt_ref[pl.ds(off, tk), :]
            vb = v_ref[pl.ds(off, tk), :]
            kcs = [jnp.concatenate([knb[:, 128 * pp:128 * pp + 128], kpb], axis=1) for pp in range(2)]
            ps, alphas = [], []
            for hh in range(4):
                s = _dot_nt(qcs[hh], kcs[hh // 2]) * (MLA_SCALE * LOG2E)
                if valid is not None:
                    s = jnp.where(valid, s, -1e30)
                mn = jnp.maximum(ms[hh], jnp.max(s, axis=1, keepdims=True))
                p = jnp.exp2(s - mn)
                alpha = jnp.exp2(ms[hh] - mn)
                ls[hh] = alpha * ls[hh] + jnp.sum(p, axis=1, keepdims=True)
                ms[hh] = mn
                ps.append(p.astype(BF16))
                alphas.append(alpha)
            pvs = []
            for pp in range(2):
                vpair = vb[:, 128 * pp:128 * pp + 128]
                vstack = jnp.concatenate([jnp.where(hf, vpair, jnp.zeros_like(vpair)) for hf in half], axis=0)
                pvs.append(_dot(jnp.concatenate(ps[2 * pp:2 * pp + 2], axis=1), vstack))
            acc = by_head(alphas) * acc + jnp.concatenate(pvs, axis=1)
            return (*ms, *ls, acc)

        neg = jnp.full((tq, 1), -1e30, F32)
        zero = jnp.zeros((tq, 1), F32)
        carry = lax.fori_loop(0, i * r, lambda j, cr: block(j, cr, None),
                              (neg,) * 4 + (zero,) * 4 + (jnp.zeros((tq, 256), F32),))
        for u in range(r):
            carry = block(i * r + u, carry, valids[u])
        o_ref[...] = carry[8] / by_head(list(carry[4:8]))
        for hh in range(4):
            lse_ref[0, :, hh:hh + 1] = (carry[hh] + jnp.log2(carry[4 + hh])) * (1.0 / LOG2E)

    return pl.pallas_call(
        body, name="mla_fwd", grid=(2, nq),
        out_shape=(_sds((s_len, MLA_WIDTH), F32), _sds((2, s_len, 4), F32)),
        in_specs=[pl.BlockSpec((tq, 256), lambda g, i: (i, g)),
                  pl.BlockSpec((tq, 128), lambda g, i: (i, g)),
                  pl.BlockSpec((s_len, 256), lambda g, i: (0, g)),
                  pl.BlockSpec((s_len, 128), lambda g, i: (0, 0)),
                  pl.BlockSpec((s_len, 256), lambda g, i: (0, g))],
        out_specs=(pl.BlockSpec((tq, 256), lambda g, i: (i, g)),
                   pl.BlockSpec((1, tq, 4), lambda g, i: (g, i, 0))),
        compiler_params=_params(("parallel", "parallel")),
    )(qn, qp, kn, kpt, v)


def _mla_bwd_call(qn, qp, kn, kpt, v, o, do, lse):
    s_len = qn.shape[0]
    tk = min(MLA_KEY_TILE, s_len)
    tq = min(ATT_Q_TILES * ATT_TILE, s_len)
    r = tq // tk
    nq = s_len // tq

    def body(qn_ref, qp_ref, kn_ref, kpt_ref, v_ref, o_ref, do_ref, lse_ref,
             dqn_ref, dqp_ref, dkn_ref, dkpt_ref, dv_ref):
        g = pl.program_id(0)
        i = pl.program_id(1)

        @pl.when(i == 0)
        def _():
            dkn_ref[...] = jnp.zeros_like(dkn_ref)
            dv_ref[...] = jnp.zeros_like(dv_ref)

        @pl.when((i == 0) & (g == 0))
        def _():
            dkpt_ref[...] = jnp.zeros_like(dkpt_ref)

        qn2 = qn_ref[...]
        qp2 = qp_ref[...]
        dof = do_ref[...]
        dob = dof.astype(BF16)
        prod = dof * o_ref[...]
        lane256 = lax.broadcasted_iota(jnp.int32, (1, 256), 1)
        lane128 = lax.broadcasted_iota(jnp.int32, (1, 128), 1)
        krow = lax.broadcasted_iota(jnp.int32, (tk, tk), 0)
        kcol = lax.broadcasted_iota(jnp.int32, (tk, tk), 1)
        row = lax.broadcasted_iota(jnp.int32, (tq, tk), 0)
        col = lax.broadcasted_iota(jnp.int32, (tq, tk), 1)
        valids = [col + u * tk <= row for u in range(r)]
        m64s = [(lane256 // 64) == hh for hh in range(4)]
        half = [(lane128 // 64) == u for u in range(2)]
        m32s = [(lane128 // 32) == hh for hh in range(4)]
        qcs, doms = [], []
        for hh in range(4):
            sl = slice(128 * (hh // 2), 128 * (hh // 2) + 128)
            qpair = qn2[:, sl]
            dpair = dob[:, sl]
            qcs.append(jnp.concatenate([jnp.where(half[hh % 2], qpair, jnp.zeros_like(qpair)),
                                        jnp.where(m32s[hh], qp2, jnp.zeros_like(qp2))], axis=1))
            doms.append(jnp.where(half[hh % 2], dpair, jnp.zeros_like(dpair)))
        dsums = [jnp.sum(jnp.where(m64, prod, 0.0), axis=1, keepdims=True) * MLA_SCALE for m64 in m64s]
        lses = [lse_ref[0, :, hh:hh + 1] * LOG2E for hh in range(4)]
        qn2t = jnp.transpose(qn2.astype(F32))
        qp2t = jnp.transpose(qp2.astype(F32))
        do2t = jnp.transpose(dof)
        sub128 = lax.broadcasted_iota(jnp.int32, (128, 1), 0)
        qtstacks, dotstacks = [], []
        for pp in range(2):
            qts, dts = [], []
            for u in range(2):
                hh = 2 * pp + u
                qts.append(jnp.concatenate(
                    [jnp.where((sub128 // 64) == u, qn2t[128 * pp:128 * pp + 128, :], 0.0),
                     jnp.where((sub128 // 32) == hh, qp2t, 0.0)], axis=0).astype(BF16))
                dts.append(jnp.where((sub128 // 64) == u, do2t[128 * pp:128 * pp + 128, :], 0.0).astype(BF16))
            qtstacks.append(jnp.concatenate(qts, axis=1))
            dotstacks.append(jnp.concatenate(dts, axis=1))

        def block(j, carry, valid):
            dqn, dqp = carry
            off = pl.multiple_of(j * tk, tk)
            knb = kn_ref[pl.ds(off, tk), :]
            kpb = kpt_ref[pl.ds(off, tk), :]
            vb = v_ref[pl.ds(off, tk), :]
            dqn_parts = []
            dkp = None
            for pp in range(2):
                sl = slice(128 * pp, 128 * pp + 128)
                knp = knb[:, sl]
                vpair = vb[:, sl]
                kc = jnp.concatenate([knp, kpb], axis=1)
                dss, pbs, kcms = [], [], []
                for u in range(2):
                    hh = 2 * pp + u
                    s = _dot_nt(qcs[hh], kc) * (MLA_SCALE * LOG2E)
                    if valid is not None:
                        s = jnp.where(valid, s, -1e30)
                    p = jnp.exp2(s - lses[hh])
                    ds = p * (_dot_nt(doms[hh], vpair) * MLA_SCALE - dsums[hh])
                    dss.append(ds.astype(BF16))
                    pbs.append(p.astype(BF16))
                    kcms.append(jnp.concatenate([jnp.where(half[u], knp, jnp.zeros_like(knp)),
                                                 jnp.where(m32s[hh], kpb, jnp.zeros_like(kpb))], axis=1))
                dqc = _dot(jnp.concatenate(dss, axis=1), jnp.concatenate(kcms, axis=0))
                dqn_parts.append(dqc[:, :128])
                dqp = dqp + dqc[:, 128:]
                dkc = _dot(qtstacks[pp], jnp.concatenate(dss, axis=0))
                dkn_ref[128 * pp:128 * pp + 128, pl.ds(off, tk)] += dkc[:128, :]
                dkp = dkc[128:, :] if dkp is None else dkp + dkc[128:, :]
                dv_ref[128 * pp:128 * pp + 128, pl.ds(off, tk)] += _dot(dotstacks[pp], jnp.concatenate(pbs, axis=0))
            dqn = dqn + jnp.concatenate(dqn_parts, axis=1)
            dkpt_ref[:, pl.ds(off, tk)] += dkp
            return dqn, dqp

        carry = lax.fori_loop(0, i * r, lambda j, cr: block(j, cr, None),
                              (jnp.zeros((tq, 256), F32), jnp.zeros((tq, 128), F32)))
        for u in range(r):
            carry = block(i * r + u, carry, valids[u])
        dqn, dqp = carry
        dqn_ref[...] = dqn.astype(BF16)
        dqp_ref[...] = dqp.astype(BF16)

    return pl.pallas_call(
        body, name="mla_bwd", grid=(2, nq),
        out_shape=(_sds((s_len, 512), BF16), _sds((s_len, 256), BF16), _sds((512, s_len), F32),
                   _sds((128, s_len), F32), _sds((512, s_len), F32)),
        in_specs=[pl.BlockSpec((tq, 256), lambda g, i: (i, g)),
                  pl.BlockSpec((tq, 128), lambda g, i: (i, g)),
                  pl.BlockSpec((s_len, 256), lambda g, i: (0, g)),
                  pl.BlockSpec((s_len, 128), lambda g, i: (0, 0)),
                  pl.BlockSpec((s_len, 256), lambda g, i: (0, g)),
                  pl.BlockSpec((tq, 256), lambda g, i: (i, g)),
                  pl.BlockSpec((tq, 256), lambda g, i: (i, g)),
                  pl.BlockSpec((1, tq, 4), lambda g, i: (g, i, 0))],
        out_specs=(pl.BlockSpec((tq, 256), lambda g, i: (i, g)),
                   pl.BlockSpec((tq, 128), lambda g, i: (i, g)),
                   pl.BlockSpec((256, s_len), lambda g, i: (g, 0)),
                   pl.BlockSpec((128, s_len), lambda g, i: (0, 0)),
                   pl.BlockSpec((256, s_len), lambda g, i: (g, 0))),
        compiler_params=_params(("arbitrary", "arbitrary")),
    )(qn, qp, kn, kpt, v, o, do, lse)


def _post_call(x, tgt, oa, ob, sz, mz, ga, gb, gate, gf, wa, wb, wo):
    s_len = x.shape[0]
    tm = min(ROW_TILE, s_len)
    nstep = s_len // tm

    def body(x_ref, t_ref, oa_ref, ob_ref, sz_ref, mz_ref, ga_ref, gb_ref, gate_ref, gf_ref,
             wa_ref, wb_ref, wo_ref,
             dx2_ref, doa_ref, dob_ref, dsz_ref, dmz_ref, dga_ref, dgb_ref,
             dwo_out, dwa_out, dwb_out, dgf_ref, dgate_ref, loss_ref, dwo_ref, dwa_ref, dwb_ref):
        @pl.when(pl.program_id(0) == 0)
        def _():
            dwo_ref[...] = jnp.zeros_like(dwo_ref)
            dwa_ref[...] = jnp.zeros_like(dwa_ref)
            dwb_ref[...] = jnp.zeros_like(dwb_ref)
            dgf_ref[...] = jnp.zeros_like(dgf_ref)
            dgate_ref[...] = jnp.zeros_like(dgate_ref)
            loss_ref[...] = jnp.zeros_like(loss_ref)

        gate = gate_ref[...]
        gf = gf_ref[...]
        oa = oa_ref[...]
        ob = ob_ref[...]
        sz = sz_ref[...]
        mz = mz_ref[...]
        sa = _sigmoid(sz)
        sb = _sigmoid(mz)
        silu_a = sz * sa
        silu_b = mz * sb
        ua = (oa * silu_a).astype(BF16)
        ub = (ob * silu_b).astype(BF16)
        ya = _dot(ua, wa_ref[...])
        yb = _dot(ub, wb_ref[...])
        sga = _sigmoid(ga_ref[...])
        sgb = _sigmoid(gb_ref[...])
        merged = (sga * ya + sgb * yb).astype(BF16)
        out = _dot(merged, wo_ref[...])
        x2 = x_ref[...] + gate * out
        r2 = lax.rsqrt(jnp.mean(x2 * x2, axis=-1, keepdims=True) + EPS)
        xhat = x2 * r2
        err = xhat * gf - t_ref[...]
        loss_ref[...] += 0.5 * jnp.sum(jnp.sum(err * err, axis=1, keepdims=True), axis=0, keepdims=True) / D_MODEL
        dy = err * (1.0 / D_MODEL)
        dgf_ref[...] += jnp.sum(dy * xhat, axis=0, keepdims=True)
        dxhat = dy * gf
        dx2 = r2 * (dxhat - xhat * jnp.mean(dxhat * xhat, axis=-1, keepdims=True))
        dx2_ref[...] = dx2
        dgate_ref[...] += jnp.sum(dx2 * out, axis=0, keepdims=True)
        dout = (dx2 * gate).astype(BF16)
        dmerged = _dot_nt(dout, wo_ref[...])
        dwo_ref[...] += _dot_tn(merged, dout)
        dya = dmerged * sga
        dyb = dmerged * sgb
        dga_ref[...] = (dya * ya * (1.0 - sga)).astype(BF16)
        dgb_ref[...] = (dyb * yb * (1.0 - sgb)).astype(BF16)
        dyab = dya.astype(BF16)
        dybb = dyb.astype(BF16)
        dua = _dot_nt(dyab, wa_ref[...])
        dub = _dot_nt(dybb, wb_ref[...])
        dwa_ref[...] += _dot_tn(ua, dyab)
        dwb_ref[...] += _dot_tn(ub, dybb)
        doa_ref[...] = dua * silu_a
        dob_ref[...] = dub * silu_b
        dsz_ref[...] = (dua * oa * (sa * (1.0 + sz * (1.0 - sa)))).astype(BF16)
        dmz_ref[...] = (dub * ob * (sb * (1.0 + mz * (1.0 - sb)))).astype(BF16)

        @pl.when(pl.program_id(0) == nstep - 1)
        def _():
            dwo_out[...] = dwo_ref[...].astype(BF16)
            for k in range(N_CHIPS):
                dwa_out[k] = dwa_ref[:, 256 * k:256 * k + 256].astype(BF16)
                dwb_out[k] = dwb_ref[:, 256 * k:256 * k + 256].astype(BF16)

    return pl.pallas_call(
        body, name="post", grid=(nstep,),
        out_shape=(_sds((s_len, D_MODEL), F32), _sds((s_len, 512), F32), _sds((s_len, 512), F32),
                   _sds((s_len, 512), BF16), _sds((s_len, 512), BF16),
                   _sds((s_len, D_MODEL), BF16), _sds((s_len, D_MODEL), BF16),
                   _sds((D_MODEL, D_MODEL), BF16), _sds((N_CHIPS, 512, 256), BF16), _sds((N_CHIPS, 512, 256), BF16),
                   _sds((1, D_MODEL), F32), _sds((1, D_MODEL), F32), _sds((1, 128), F32)),
        in_specs=[_rows(tm, D_MODEL), _rows(tm, D_MODEL), _rows(tm, 512), _rows(tm, 512), _rows(tm, 512),
                  _rows(tm, 512), _rows(tm, D_MODEL), _rows(tm, D_MODEL), _whole((1, D_MODEL)), _whole((1, D_MODEL)),
                  _whole((512, D_MODEL)), _whole((512, D_MODEL)), _whole((D_MODEL, D_MODEL))],
        out_specs=(_rows(tm, D_MODEL), _rows(tm, 512), _rows(tm, 512), _rows(tm, 512), _rows(tm, 512),
                   _rows(tm, D_MODEL), _rows(tm, D_MODEL),
                   _whole((D_MODEL, D_MODEL)), _whole((N_CHIPS, 512, 256)), _whole((N_CHIPS, 512, 256)),
                   _whole((1, D_MODEL)), _whole((1, D_MODEL)), _whole((1, 128))),
        scratch_shapes=[pltpu.VMEM((D_MODEL, D_MODEL), F32), pltpu.VMEM((512, D_MODEL), F32),
                        pltpu.VMEM((512, D_MODEL), F32)],
        compiler_params=_params(("arbitrary",)),
    )(x, tgt, oa, ob, sz, mz, ga, gb, gate, gf, wa, wb, wo)


def _bwdprep_call(dsq, dsk, dsv, dsz, dqn, dqp, dkn, dvv, dkpt, dmz, dga, dgb, cq, ckv, cos128, sin128,
                  qg, kvg, w_q, w_kv):
    s_len = cq.shape[0]
    tm = min(ROW_TILE, s_len)

    def body(dsq_ref, dsk_ref, dsv_ref, dsz_ref, dqn_ref, dqp_ref, dkn_ref, dvv_ref, dkpt_ref, dmz_ref,
             dga_ref, dgb_ref, cq_ref, ckv_ref, cos_ref, sin_ref, qg_ref, kvg_ref, wq_ref, wkv_ref,
             dp_ref, dwq_ref, dwkv_ref, dqg_ref, dkvg_ref):
        @pl.when(pl.program_id(0) == 0)
        def _():
            dwq_ref[...] = jnp.zeros_like(dwq_ref)
            dwkv_ref[...] = jnp.zeros_like(dwkv_ref)
            dqg_ref[...] = jnp.zeros_like(dqg_ref)
            dkvg_ref[...] = jnp.zeros_like(dkvg_ref)

        cos = cos_ref[...]
        sin = sin_ref[...]
        dp_ref[:, O_SQ:O_SK] = dsq_ref[...] * jnp.asarray(SB_SCALE, BF16)
        dp_ref[:, O_SK:O_SV] = jnp.transpose(dsk_ref[...]).astype(BF16)
        dp_ref[:, O_SV:O_SZ] = jnp.transpose(dsv_ref[...]).astype(BF16)
        dp_ref[:, O_SZ:O_CQ] = dsz_ref[...]
        dp_ref[:, O_MZ:O_GA] = dmz_ref[...]
        dp_ref[:, O_GA:O_GB] = dga_ref[...]
        dp_ref[:, O_GB:O_KR] = dgb_ref[...]
        dkp = jnp.transpose(dkpt_ref[...])
        dp_ref[:, O_KR:O_KR + 128] = (dkp * cos).astype(BF16)
        dp_ref[:, O_KR + 128:O_END] = (dkp * sin).astype(BF16)
        dp_ref[:, O_END:W_INT] = jnp.zeros((tm, W_INT - O_END), BF16)

        cq = cq_ref[...]
        rq = lax.rsqrt(jnp.mean(cq * cq, axis=-1, keepdims=True) + EPS)
        cqh = cq * rq
        qg = qg_ref[...]
        cqn = (cqh * qg).astype(BF16)
        dqp = dqp_ref[...].astype(F32)
        dqa = jnp.concatenate([dqn_ref[...], (dqp * jnp.tile(cos, (1, 2))).astype(BF16),
                               (dqp * jnp.tile(sin, (1, 2))).astype(BF16)], axis=1)
        dcqn = _dot_nt(dqa, wq_ref[...])
        dwq_ref[...] += _dot_tn(cqn, dqa)
        dqg_ref[...] += jnp.sum(dcqn * cqh, axis=0, keepdims=True)
        dh = dcqn * qg
        dcq = rq * (dh - cqh * jnp.mean(dh * cqh, axis=-1, keepdims=True))
        dp_ref[:, O_CQ:O_CKV] = dcq.astype(BF16)

        ckv = ckv_ref[...]
        rk = lax.rsqrt(jnp.mean(ckv * ckv, axis=-1, keepdims=True) + EPS)
        ckh = ckv * rk
        kvg = kvg_ref[...]
        ckvn = (ckh * kvg).astype(BF16)
        dkva = jnp.concatenate([jnp.transpose(dkn_ref[...]).astype(BF16),
                                jnp.transpose(dvv_ref[...]).astype(BF16)], axis=1)
        dckvn = _dot_nt(dkva, wkv_ref[...])
        dwkv_ref[...] += _dot_tn(ckvn, dkva)
        dkvg_ref[...] += jnp.sum(dckvn * ckh, axis=0, keepdims=True)
        dh2 = dckvn * kvg
        dckv = rk * (dh2 - ckh * jnp.mean(dh2 * ckh, axis=-1, keepdims=True))
        dp_ref[:, O_CKV:O_MZ] = dckv.astype(BF16)

    return pl.pallas_call(
        body, name="bwdprep", grid=(s_len // tm,),
        out_shape=(_sds((s_len, W_INT), BF16), _sds((Q_RANK, 1024), F32), _sds((KV_RANK, 1024), F32),
                   _sds((1, Q_RANK), F32), _sds((1, KV_RANK), F32)),
        in_specs=[_rows(tm, 512), _cols(512, tm), _cols(512, tm), _rows(tm, 512), _rows(tm, 512), _rows(tm, 256),
                  _cols(512, tm), _cols(512, tm), _cols(128, tm), _rows(tm, 512), _rows(tm, D_MODEL),
                  _rows(tm, D_MODEL), _rows(tm, Q_RANK), _rows(tm, KV_RANK), _rows(tm, LANES), _rows(tm, LANES),
                  _whole((1, Q_RANK)), _whole((1, KV_RANK)), _whole((Q_RANK, 1024)), _whole((KV_RANK, 1024))],
        out_specs=(_rows(tm, W_INT), _whole((Q_RANK, 1024)), _whole((KV_RANK, 1024)),
                   _whole((1, Q_RANK)), _whole((1, KV_RANK))),
        compiler_params=_params(("arbitrary",)),
    )(dsq, dsk, dsv, dsz, dqn, dqp, dkn, dvv, dkpt, dmz, dga, dgb, cq, ckv, cos128, sin128, qg, kvg, w_q, w_kv)


def _dh_call(dproj, w_in_t, w_kr, x, dx2, scale, g1):
    s_len = x.shape[0]
    tm = min(2 * ROW_TILE, s_len)
    parts = [(a, b) + _weight_rows(a, b) for a, b in ((0, O_MZ), (O_MZ, O_KR), (O_KR, O_END))]

    def body(dp_ref, win_ref, wkr_ref, x_ref, dx2_ref, sc_ref, g1_ref, gx_ref, dsh_ref, dsc_ref, dg1_ref):
        @pl.when(pl.program_id(0) == 0)
        def _():
            dsh_ref[...] = jnp.zeros_like(dsh_ref)
            dsc_ref[...] = jnp.zeros_like(dsc_ref)
            dg1_ref[...] = jnp.zeros_like(dg1_ref)

        def half(k):
            return sum(_dot(dp_ref[:, a:b], (wkr_ref if from_kr else win_ref)[k, rows, :])
                       for a, b, from_kr, rows in parts)

        dh = jnp.concatenate([half(0), half(1)], axis=1)
        xt = x_ref[...]
        r = lax.rsqrt(jnp.mean(xt * xt, axis=-1, keepdims=True) + EPS)
        xh = xt * r
        g1 = g1_ref[...]
        xg = xh * g1
        dsh_ref[...] += jnp.sum(dh, axis=0, keepdims=True)
        dsc_ref[...] += jnp.sum(dh * xg, axis=0, keepdims=True)
        dxg = dh * (1.0 + sc_ref[...])
        dg1_ref[...] += jnp.sum(dxg * xh, axis=0, keepdims=True)
        dxh = dxg * g1
        gx_ref[...] = dx2_ref[...] + r * (dxh - xh * jnp.mean(dxh * xh, axis=-1, keepdims=True))

    return pl.pallas_call(
        body, name="dh", grid=(s_len // tm,),
        out_shape=(_sds((s_len, D_MODEL), F32), _sds((1, D_MODEL), F32), _sds((1, D_MODEL), F32),
                   _sds((1, D_MODEL), F32)),
        in_specs=[_rows(tm, W_INT), _whole((2, IN_WIDTH, HALF_D)), _whole((2, O_END - O_KR, HALF_D)),
                  _rows(tm, D_MODEL), _rows(tm, D_MODEL), _whole((1, D_MODEL)), _whole((1, D_MODEL))],
        out_specs=(_rows(tm, D_MODEL), _whole((1, D_MODEL)), _whole((1, D_MODEL)), _whole((1, D_MODEL))),
        compiler_params=_params(("arbitrary",)),
    )(dproj, w_in_t, w_kr, x, dx2, scale, g1)


def _small_call(svg, ct, dmod_sh):
    def body(sv_ref, ct_ref, dm_ref, tot_ref, gwada_ref):
        acc = sv_ref[0:1, :]
        for d in range(1, N_DEV):
            acc = acc + sv_ref[d:d + 1, :]
        tot_ref[...] = acc
        gwada_ref[...] = lax.dot_general(ct_ref[...], dm_ref[...], (((1,), (0,)), ((), ())),
                                         precision=lax.Precision.HIGHEST, preferred_element_type=F32)

    vmem = pl.BlockSpec(memory_space=pltpu.VMEM)
    return pl.pallas_call(
        body, name="small_grads",
        out_shape=(_sds((1, 8 * SV_COLS), F32), _sds((D_MODEL, 768), F32)),
        in_specs=[vmem, vmem, vmem], out_specs=(vmem, vmem),
        compiler_params=_params(),
    )(svg, ct, dmod_sh)


def _adamw_tile_rows(rows, cols):
    budget = 2 << 20
    if rows * cols * 4 <= budget or rows % 8:
        return rows
    best = 8
    for tr in range(8, rows + 1, 8):
        if rows % tr == 0 and tr * cols * 4 <= budget:
            best = tr
    return best


def _adamw_math(w, gg, m, v):
    m2 = ADAM_B1 * m + (1.0 - ADAM_B1) * gg
    v2 = ADAM_B2 * v + (1.0 - ADAM_B2) * (gg * gg)
    m_hat = m2 / (1.0 - ADAM_B1 ** ADAM_STEP)
    v_hat = v2 / (1.0 - ADAM_B2 ** ADAM_STEP)
    return -ADAM_LR * (m_hat / (jnp.sqrt(v_hat) + ADAM_EPS) + ADAM_WD * w), m2, v2


def _adamw_call(name, w, g, m, v):
    rows, cols = w.shape
    tr = _adamw_tile_rows(rows, cols)
    halves = g.ndim == 3

    def body(w_ref, g_ref, m_ref, v_ref, *out_refs):
        if halves:
            gg = jnp.concatenate([g_ref[0], g_ref[1]], axis=1)
            out_refs[0][...] = gg
        else:
            gg = g_ref[...]
        d_ref, nm_ref, nv_ref = out_refs[-3:]
        d_ref[...], nm_ref[...], nv_ref[...] = _adamw_math(w_ref[...], gg, m_ref[...], v_ref[...])

    spec = pl.BlockSpec((tr, cols), lambda i: (i, 0))
    g_spec = pl.BlockSpec((2, tr, cols // 2), lambda i: (0, i, 0)) if halves else spec
    n_out = 4 if halves else 3
    outs = pl.pallas_call(
        body, name="adamw_" + name, grid=(rows // tr,),
        out_shape=(_sds((rows, cols), F32),) * n_out,
        in_specs=[spec, g_spec, spec, spec], out_specs=(spec,) * n_out,
        compiler_params=_params(("parallel",)),
    )(w, g, m, v)
    return tuple(outs) if halves else (g,) + tuple(outs)


def _adamw_vectors_call(tot, offsets, ws, ms, vs):
    nvec = len(ws)

    def body(tot_ref, *refs):
        ins, outs = refs[:3 * nvec], refs[3 * nvec:]
        for k in range(nvec):
            w_ref, m_ref, v_ref = ins[3 * k:3 * k + 3]
            g_ref, d_ref, nm_ref, nv_ref = outs[4 * k:4 * k + 4]
            gg = tot_ref[:, offsets[k]:offsets[k] + w_ref.shape[1]]
            g_ref[...] = gg
            d_ref[...], nm_ref[...], nv_ref[...] = _adamw_math(w_ref[...], gg, m_ref[...], v_ref[...])

    vmem = pl.BlockSpec(memory_space=pltpu.VMEM)
    outs = pl.pallas_call(
        body, name="adamw_vectors",
        out_shape=tuple(_sds(w.shape, F32) for w in ws for _ in range(4)),
        in_specs=[vmem] * (1 + 3 * nvec), out_specs=(vmem,) * (4 * nvec),
        compiler_params=_params(),
    )(tot, *[a for w, m, v in zip(ws, ms, vs) for a in (w, m, v)])
    return [tuple(outs[4 * k:4 * k + 4]) for k in range(nvec)]


IN_SHARD = IN_WIDTH // N_CHIPS
HALF_D = D_MODEL // 2
SMALL_ROWS = (576, 512, 1024, 1024, 2048)
SMALL_TOTAL = sum(SMALL_ROWS)
SMALL_HALF = SMALL_TOTAL // 2
SMALL_SUM_ROWS = 432


def _gather_call(c_row, w_ada_sh, pack_in, pack_small):
    def body(c_ref, wada_ref, pki_ref, pks_ref, mg_ref, cg_ref, gwi_ref, gws_ref,
             cv, ssem_c, rsem_c, ssem_m, rsem_m, ssem_w, rsem_w, ssem_f, rsem_f, lsem):
        x, y, c = lax.axis_index("x"), lax.axis_index("y"), lax.axis_index("c")
        me = 4 * x + 2 * y + c
        chip = 2 * x + y
        rel3 = [(1, 0), (0, 1), (1, 1)]
        packs = [(pki_ref, gwi_ref), (pks_ref, gws_ref)]

        def slot(a, gw, k, h):
            return gw.at[h, k] if a == 0 else gw.at[k, h]

        sends = []
        for j, (dx, dy) in enumerate(rel3):
            for a, (pk, gw) in enumerate(packs):
                cp = pltpu.make_async_remote_copy(
                    src_ref=pk.at[c], dst_ref=slot(a, gw, chip, c), send_sem=ssem_w.at[j, a],
                    recv_sem=rsem_w.at[j, a], device_id=(_flip(x, dx), _flip(y, dy), c), device_id_type=MESH)
                cp.start()
                sends.append(cp)
        owns = []
        for a, (pk, gw) in enumerate(packs):
            for h in range(2):
                own = pltpu.make_async_copy(pk.at[h], slot(a, gw, chip, h), lsem.at[a, h])
                own.start()
                owns.append(own)

        cv[me] = c_ref[...]
        for r in range(1, N_DEV):
            dx, dy, dc = (r >> 2) & 1, (r >> 1) & 1, r & 1
            cp = pltpu.make_async_remote_copy(
                src_ref=c_ref, dst_ref=cv.at[me], send_sem=ssem_c.at[r - 1], recv_sem=rsem_c.at[r - 1],
                device_id=(_flip(x, dx), _flip(y, dy), _flip(c, dc)), device_id_type=MESH)
            cp.start()
            sends.append(cp)
        for r in range(1, N_DEV):
            dx, dy, dc = (r >> 2) & 1, (r >> 1) & 1, r & 1
            src = 4 * _flip(x, dx) + 2 * _flip(y, dy) + _flip(c, dc)
            pltpu.make_async_remote_copy(
                src_ref=c_ref, dst_ref=cv.at[src], send_sem=ssem_c.at[r - 1], recv_sem=rsem_c.at[r - 1],
                device_id=(x, y, c), device_id_type=MESH).wait_recv()
        rows = lax.broadcasted_iota(jnp.int32, (N_DEV, D_MODEL), 0)
        call = jnp.zeros((N_DEV, D_MODEL), F32)
        for b in range(N_DEV):
            call = jnp.where(rows == b, jnp.broadcast_to(cv[b], (N_DEV, D_MODEL)), call)
        cg_ref[...] = call

        mg_ref[chip] = lax.dot_general(call, wada_ref[...], (((1,), (0,)), ((), ())),
                                       precision=lax.Precision.HIGHEST, preferred_element_type=F32)
        for j, (dx, dy) in enumerate(rel3):
            cp = pltpu.make_async_remote_copy(
                src_ref=mg_ref.at[chip], dst_ref=mg_ref.at[chip], send_sem=ssem_m.at[j], recv_sem=rsem_m.at[j],
                device_id=(_flip(x, dx), _flip(y, dy), c), device_id_type=MESH)
            cp.start()
            sends.append(cp)
        for j, (dx, dy) in enumerate(rel3):
            src_chip = 2 * _flip(x, dx) + _flip(y, dy)
            pltpu.make_async_remote_copy(
                src_ref=mg_ref.at[src_chip], dst_ref=mg_ref.at[src_chip], send_sem=ssem_m.at[j],
                recv_sem=rsem_m.at[j], device_id=(x, y, c), device_id_type=MESH).wait_recv()
        for j, (dx, dy) in enumerate(rel3):
            src_chip = 2 * _flip(x, dx) + _flip(y, dy)
            for a, (pk, gw) in enumerate(packs):
                pltpu.make_async_remote_copy(
                    src_ref=pk.at[c], dst_ref=slot(a, gw, src_chip, c), send_sem=ssem_w.at[j, a],
                    recv_sem=rsem_w.at[j, a], device_id=(x, y, c), device_id_type=MESH).wait_recv()
                cp = pltpu.make_async_remote_copy(
                    src_ref=slot(a, gw, src_chip, c), dst_ref=slot(a, gw, src_chip, c), send_sem=ssem_f.at[j, a],
                    recv_sem=rsem_f.at[j, a], device_id=(x, y, 1 - c), device_id_type=MESH)
                cp.start()
                sends.append(cp)
        for j, (dx, dy) in enumerate(rel3):
            src_chip = 2 * _flip(x, dx) + _flip(y, dy)
            for a, (pk, gw) in enumerate(packs):
                pltpu.make_async_remote_copy(
                    src_ref=pk.at[c], dst_ref=slot(a, gw, src_chip, 1 - c), send_sem=ssem_f.at[j, a],
                    recv_sem=rsem_f.at[j, a], device_id=(x, y, c), device_id_type=MESH).wait_recv()
        for cp in sends:
            cp.wait_send()
        for own in owns:
            own.wait()

    vmem = pl.BlockSpec(memory_space=pltpu.VMEM)
    return pl.pallas_call(
        body, name="gather_fwd",
        out_shape=(_sds((N_CHIPS, N_DEV, 768), F32), _sds((N_DEV, D_MODEL), F32),
                   _sds((2, N_CHIPS, IN_SHARD, HALF_D), BF16), _sds((N_CHIPS, 2, SMALL_HALF, LANES), BF16)),
        in_specs=[vmem, vmem, vmem, vmem], out_specs=(vmem, vmem, vmem, vmem),
        scratch_shapes=[
            pltpu.VMEM((N_DEV, 1, D_MODEL), F32),
            pltpu.SemaphoreType.DMA((N_DEV - 1,)), pltpu.SemaphoreType.DMA((N_DEV - 1,)),
            pltpu.SemaphoreType.DMA((3,)), pltpu.SemaphoreType.DMA((3,)),
            pltpu.SemaphoreType.DMA((3, 2)), pltpu.SemaphoreType.DMA((3, 2)),
            pltpu.SemaphoreType.DMA((3, 2)), pltpu.SemaphoreType.DMA((3, 2)),
            pltpu.SemaphoreType.DMA((2, 2)),
        ],
        compiler_params=_params(),
    )(c_row, w_ada_sh, pack_in, pack_small)


def _reduce_call(g_in, g_small, sv):
    def body(gi_ref, gs_ref, sv_ref, fi_ref, fs_ref, svg_ref, pair_i, pair_s, send_i, send_s, land_i, land_s,
             ssem_p, rsem_p, ssem_g, rsem_g, ssem_s, rsem_s, ssem_x, rsem_x):
        x, y, c = lax.axis_index("x"), lax.axis_index("y"), lax.axis_index("c")
        me = 4 * x + 2 * y + c
        chip = 2 * x + y
        rel3 = [(1, 0), (0, 1), (1, 1)]
        payloads = [(gi_ref, pair_i, send_i, land_i, fi_ref), (gs_ref, pair_s, send_s, land_s, fs_ref)]
        where = [lambda k, h: N_CHIPS * h + k, lambda k, h: 2 * k + h]
        copies = []

        for k in range(N_CHIPS):
            for a, (g, pair, _, _, _) in enumerate(payloads):
                cp = pltpu.make_async_remote_copy(
                    src_ref=g.at[where[a](k, 1 - c)], dst_ref=pair.at[k], send_sem=ssem_p.at[k, a],
                    recv_sem=rsem_p.at[k, a], device_id=(x, y, 1 - c), device_id_type=MESH)
                cp.start()
                copies.append(cp)

        for r in range(1, N_DEV):
            dx, dy, dc = (r >> 2) & 1, (r >> 1) & 1, r & 1
            cp = pltpu.make_async_remote_copy(
                src_ref=sv_ref, dst_ref=svg_ref.at[me], send_sem=ssem_s.at[r - 1], recv_sem=rsem_s.at[r - 1],
                device_id=(_flip(x, dx), _flip(y, dy), _flip(c, dc)), device_id_type=MESH)
            cp.start()
            copies.append(cp)
        svg_ref[me] = sv_ref[...]

        def pair_sum(k, store_in, store_small):
            for a, (g, pair, _, _, _) in enumerate(payloads):
                pltpu.make_async_remote_copy(
                    src_ref=g.at[where[a](k, c)], dst_ref=pair.at[k], send_sem=ssem_p.at[k, a],
                    recv_sem=rsem_p.at[k, a], device_id=(x, y, c), device_id_type=MESH).wait_recv()
            for qd in range(HALF_D // LANES):
                sl = slice(LANES * qd, LANES * qd + LANES)
                store_in(sl, gi_ref[where[0](k, c), :, sl].astype(F32) + pair_i[k, :, sl].astype(F32))

            def rows(i, carry):
                sl = pl.ds(pl.multiple_of(i * SMALL_SUM_ROWS, 16), SMALL_SUM_ROWS)
                store_small(sl, gs_ref[where[1](k, c), sl, :].astype(F32) + pair_s[k, sl, :].astype(F32))
                return carry

            lax.fori_loop(0, SMALL_HALF // SMALL_SUM_ROWS, rows, 0)

        for j, (dx, dy) in enumerate(rel3):
            tx, ty = _flip(x, dx), _flip(y, dy)

            def put_in(sl, val, j=j):
                send_i[j, :, sl] = val.astype(BF16)

            def put_small(sl, val, j=j):
                send_s[j, sl, :] = val.astype(BF16)

            pair_sum(2 * tx + ty, put_in, put_small)
            for a, (_, _, send, land, _) in enumerate(payloads):
                cp = pltpu.make_async_remote_copy(
                    src_ref=send.at[j], dst_ref=land.at[j], send_sem=ssem_g.at[j, a], recv_sem=rsem_g.at[j, a],
                    device_id=(tx, ty, c), device_id_type=MESH)
                cp.start()
                copies.append(cp)

        def own_in(sl, val):
            fi_ref[c, :, sl] = val

        def own_small(sl, val):
            fs_ref[c, sl, :] = val

        pair_sum(chip, own_in, own_small)
        for j in range(3):
            for a, (_, _, send, land, _) in enumerate(payloads):
                pltpu.make_async_remote_copy(
                    src_ref=send.at[j], dst_ref=land.at[j], send_sem=ssem_g.at[j, a], recv_sem=rsem_g.at[j, a],
                    device_id=(x, y, c), device_id_type=MESH).wait_recv()
            for qd in range(HALF_D // LANES):
                sl = slice(LANES * qd, LANES * qd + LANES)
                fi_ref[c, :, sl] += land_i[j, :, sl].astype(F32)

            def add_rows(i, carry, j=j):
                sl = pl.ds(pl.multiple_of(i * SMALL_SUM_ROWS, 16), SMALL_SUM_ROWS)
                fs_ref[c, sl, :] += land_s[j, sl, :].astype(F32)
                return carry

            lax.fori_loop(0, SMALL_HALF // SMALL_SUM_ROWS, add_rows, 0)

        for a, f in enumerate((fi_ref, fs_ref)):
            cp = pltpu.make_async_remote_copy(
                src_ref=f.at[c], dst_ref=f.at[c], send_sem=ssem_x.at[a], recv_sem=rsem_x.at[a],
                device_id=(x, y, 1 - c), device_id_type=MESH)
            cp.start()
            copies.append(cp)
        for a, f in enumerate((fi_ref, fs_ref)):
            pltpu.make_async_remote_copy(
                src_ref=f.at[c], dst_ref=f.at[1 - c], send_sem=ssem_x.at[a], recv_sem=rsem_x.at[a],
                device_id=(x, y, c), device_id_type=MESH).wait_recv()
        for r in range(1, N_DEV):
            dx, dy, dc = (r >> 2) & 1, (r >> 1) & 1, r & 1
            src = 4 * _flip(x, dx) + 2 * _flip(y, dy) + _flip(c, dc)
            pltpu.make_async_remote_copy(
                src_ref=sv_ref, dst_ref=svg_ref.at[src], send_sem=ssem_s.at[r - 1],
                recv_sem=rsem_s.at[r - 1], device_id=(x, y, c), device_id_type=MESH).wait_recv()
        for cp in copies:
            cp.wait_send()

    vmem = pl.BlockSpec(memory_space=pltpu.VMEM)
    return pl.pallas_call(
        body, name="grad_reduce",
        out_shape=(_sds((2, IN_SHARD, HALF_D), F32), _sds((2, SMALL_HALF, LANES), F32),
                   _sds((N_DEV, 8, SV_COLS), F32)),
        in_specs=[vmem, vmem, vmem], out_specs=(vmem, vmem, vmem),
        scratch_shapes=[
            pltpu.VMEM((N_CHIPS, IN_SHARD, HALF_D), BF16), pltpu.VMEM((N_CHIPS, SMALL_HALF, LANES), BF16),
            pltpu.VMEM((3, IN_SHARD, HALF_D), BF16), pltpu.VMEM((3, SMALL_HALF, LANES), BF16),
            pltpu.VMEM((3, IN_SHARD, HALF_D), BF16), pltpu.VMEM((3, SMALL_HALF, LANES), BF16),
            pltpu.SemaphoreType.DMA((N_CHIPS, 2)), pltpu.SemaphoreType.DMA((N_CHIPS, 2)),
            pltpu.SemaphoreType.DMA((3, 2)), pltpu.SemaphoreType.DMA((3, 2)),
            pltpu.SemaphoreType.DMA((N_DEV - 1,)), pltpu.SemaphoreType.DMA((N_DEV - 1,)),
            pltpu.SemaphoreType.DMA((2,)), pltpu.SemaphoreType.DMA((2,)),
        ],
        compiler_params=_params(),
    )(g_in, g_small, sv)


def _dwin_call(h, dproj):
    s_len = h.shape[0]
    tm = min(4 * ROW_TILE, s_len)
    nrow = s_len // tm
    nc = 4
    chunk = W_INT // nc

    def body(h_ref, dp_ref, dw_ref, acc):
        c, i = pl.program_id(0), pl.program_id(1)

        @pl.when(i == 0)
        def _():
            acc[...] = jnp.zeros_like(acc)

        acc[...] += _dot_tn(dp_ref[...], h_ref[...])

        def put(lo, hi, dst):
            for half in range(2):
                dw_ref[half, dst:dst + hi - lo, :] = acc[lo:hi, half * HALF_D:(half + 1) * HALF_D].astype(BF16)

        for cc in range(nc):
            @pl.when((c == cc) & (i == nrow - 1))
            def _(cc=cc):
                base = cc * chunk
                lo, hi = base, min(base + chunk, O_MZ)
                if lo < hi:
                    put(lo - base, hi - base, lo)
                lo, hi = max(base, O_MZ), min(base + chunk, O_KR)
                if lo < hi:
                    put(lo - base, hi - base, lo + ROPE_DIM)
                if base <= O_KR and O_END <= base + chunk:
                    r = O_KR - base
                    half_r = ROPE_DIM // 2
                    kr = sum(acc[r + ROPE_DIM * j:r + ROPE_DIM * (j + 1), :] for j in range(4))
                    sw = sum(acc[r + 128 + ROPE_DIM * j:r + 128 + ROPE_DIM * (j + 1), :] for j in range(4))
                    tot = kr + jnp.concatenate([sw[half_r:], sw[:half_r]], axis=0)
                    for half in range(2):
                        dw_ref[half, O_MZ:O_MZ + ROPE_DIM, :] = (
                            tot[:, half * HALF_D:(half + 1) * HALF_D].astype(BF16))

    assert O_KR >= (nc - 1) * chunk
    return pl.pallas_call(
        body, name="dwin", grid=(nc, nrow),
        out_shape=_sds((2, IN_WIDTH, HALF_D), BF16),
        in_specs=[pl.BlockSpec((tm, D_MODEL), lambda c, i: (i, 0)),
                  pl.BlockSpec((tm, chunk), lambda c, i: (i, c))],
        out_specs=pl.BlockSpec((2, IN_WIDTH, HALF_D), lambda c, i: (0, 0, 0)),
        scratch_shapes=[pltpu.VMEM((chunk, D_MODEL), F32)],
        compiler_params=_params(("arbitrary", "arbitrary")),
    )(h, dproj)


def _internal_weights(w_in_t, w_uq, w_ukv):
    krot_t = w_in_t[:, O_MZ:O_MZ + ROPE_DIM]
    krot_sw = krot_t.reshape(2, 2, ROPE_DIM // 2, HALF_D)[:, ::-1].reshape(2, ROPE_DIM, HALF_D)
    w_kr = jnp.concatenate([jnp.tile(krot_t, (1, 4, 1)), jnp.tile(krot_sw, (1, 4, 1))], axis=1)
    uq = w_uq.reshape(Q_RANK, N_HEADS, 96)
    wp = uq[:, :, 64:].reshape(Q_RANK, 256)
    w_q = jnp.concatenate([uq[:, :, :64].reshape(Q_RANK, 512), wp, _swap_halves(wp, 32)], axis=1)
    ukv = w_ukv.reshape(KV_RANK, N_HEADS, 128)
    w_kv = jnp.concatenate([ukv[:, :, :64].reshape(KV_RANK, 512), ukv[:, :, 64:].reshape(KV_RANK, 512)], axis=1)
    return w_kr, w_q, w_kv


def _true_weight_grads(dwq, dwkv):
    dwp = dwq[:, 512:768] + _swap_halves(dwq[:, 768:1024], 32)
    g_uq = jnp.concatenate([dwq[:, :512].reshape(Q_RANK, N_HEADS, 64), dwp.reshape(Q_RANK, N_HEADS, 32)],
                           axis=2).reshape(Q_RANK, 768)
    g_ukv = jnp.concatenate([dwkv[:, :512].reshape(KV_RANK, N_HEADS, 64), dwkv[:, 512:].reshape(KV_RANK, N_HEADS, 64)],
                            axis=2).reshape(KV_RANK, 1024)
    return g_uq, g_ukv


def _swap_halves(w, group):
    r, n = w.shape
    return w.reshape(r, n // group, 2, group // 2)[:, :, ::-1, :].reshape(r, n)


def _pack_shards(parts):
    return jnp.concatenate([p.reshape(-1, LANES) for p in parts], axis=0)


def _unpack_small(gw):
    offs = [0]
    for r in SMALL_ROWS:
        offs.append(offs[-1] + r)

    def cols(i, rows, shard_cols):
        blk = gw[:, offs[i]:offs[i + 1]].reshape(N_CHIPS, rows, shard_cols)
        return blk.transpose(1, 0, 2).reshape(rows, N_CHIPS * shard_cols)

    return (cols(0, Q_RANK, 192), cols(1, KV_RANK, 256), cols(2, 512, 256), cols(3, 512, 256),
            gw[:, offs[4]:offs[5]].reshape(D_MODEL, D_MODEL))


def _chip_major(g, shard_cols):
    r = g.shape[0]
    return g.reshape(r, N_CHIPS, shard_cols).transpose(1, 0, 2).reshape(N_CHIPS, -1, LANES)


def kernel(x, c, positions, w_ada, b_ada, norm_gain, w_in, q_norm_gain, w_uq, kv_norm_gain, w_ukv, w_branch_a, w_branch_b, w_out, final_norm_gain, loss_target, m_w_ada, m_b_ada, m_norm_gain, m_w_in, m_q_norm_gain, m_w_uq, m_kv_norm_gain, m_w_ukv, m_w_branch_a, m_w_branch_b, m_w_out, m_final_norm_gain, v_w_ada, v_b_ada, v_norm_gain, v_w_in, v_q_norm_gain, v_w_uq, v_kv_norm_gain, v_w_ukv, v_w_branch_a, v_w_branch_b, v_w_out, v_final_norm_gain):
    ix, iy, ic = lax.axis_index("x"), lax.axis_index("y"), lax.axis_index("c")
    me = 4 * ix + 2 * iy + ic
    chip = 2 * ix + iy
    xs = x[0]
    tgt = loss_target[0]
    s_len = xs.shape[0]

    w_in_t = jnp.swapaxes(w_in[0], 0, 1)
    w_in_tb = w_in_t.astype(BF16)
    pack_in = jnp.stack([w_in_tb[:, :HALF_D], w_in_tb[:, HALF_D:]], axis=0)
    small_shards = (w_uq[0], w_ukv[0], w_branch_a[0], w_branch_b[0], w_out[0])
    pack_small = _pack_shards([s.astype(BF16) for s in small_shards]).reshape(2, SMALL_HALF, LANES)
    mg, call, gw_in, gw_small = _gather_call(c, w_ada[0], pack_in, pack_small)
    mod = mg.transpose(1, 0, 2).reshape(N_DEV, 3 * D_MODEL) + b_ada
    mod_me = lax.dynamic_slice_in_dim(mod, me, 1, axis=0)
    shift, scale, gate = mod_me[:, :D_MODEL], mod_me[:, D_MODEL:2 * D_MODEL], mod_me[:, 2 * D_MODEL:]

    f_in_t = gw_in.reshape(2, IN_WIDTH, HALF_D)
    f_uq, f_ukv, f_a, f_b, f_out = _unpack_small(gw_small.reshape(N_CHIPS, SMALL_TOTAL, LANES))
    w_kr, w_q, w_kv = _internal_weights(f_in_t, f_uq, f_ukv)

    inv_freq = ROPE_BASE ** (-jnp.arange(0, ROPE_DIM, 2, dtype=F32) / ROPE_DIM)
    ang = positions[0].astype(F32)[:, None] * inv_freq
    cs, sn = jnp.cos(ang), jnp.sin(ang)
    cos128 = jnp.tile(jnp.concatenate([cs, cs], axis=1), (1, 4))
    sin128 = jnp.tile(jnp.concatenate([-sn, sn], axis=1), (1, 4))

    (h, sq, sk, sv, sz, cq, ckv, mz, ga, gb, kpt, qn, qp, kn, vv) = _inproj_call(
        xs, shift, scale, norm_gain, f_in_t, w_kr, w_q, w_kv, q_norm_gain, kv_norm_gain, cos128, sin128)
    oa, lt, first = _sb_fwd_call(sq, sk, sv)
    ob, lse = _mla_fwd_call(qn, qp, kn, kpt, vv)

    gf = final_norm_gain.reshape(1, D_MODEL)
    (dx2, doa, dob, dsz, dmz, dga, dgb, dwo, dwa, dwb, dgf, dgate, loss_p) = _post_call(
        xs, tgt, oa, ob, sz, mz, ga, gb, gate, gf, f_a, f_b, f_out)

    dsq, dsk_t, dsv_t = _sb_bwd_call(first[:, :, 0, 0].reshape(-1), sq, sk, sv, doa, lt)
    dqn, dqp, dkn_t, dkpt_t, dvv_t = _mla_bwd_call(qn, qp, kn, kpt, vv, ob, dob, lse)

    dproj, dwq, dwkv, dqg, dkvg = _bwdprep_call(
        dsq, dsk_t, dsv_t, dsz, dqn, dqp, dkn_t, dvv_t, dkpt_t, dmz, dga, dgb, cq, ckv, cos128, sin128,
        q_norm_gain, kv_norm_gain, w_q, w_kv)
    grad_x, dshift, dscale, dg1 = _dh_call(dproj, f_in_t, w_kr, xs, dx2, scale, norm_gain)
    g_in_t = _dwin_call(h, dproj)
    g_uq, g_ukv = _true_weight_grads(dwq, dwkv)

    g_in_pieces = g_in_t.reshape(N_DEV, IN_SHARD, HALF_D)
    g_small = jnp.concatenate([
        _chip_major(g_uq, 192).astype(BF16), _chip_major(g_ukv, 256).astype(BF16),
        dwa.reshape(N_CHIPS, -1, LANES), dwb.reshape(N_CHIPS, -1, LANES),
        dwo.reshape(N_CHIPS, -1, LANES)], axis=1).reshape(N_DEV, SMALL_HALF, LANES)
    small = jnp.concatenate([
        dshift, dscale, dgate, dg1, dqg, dkvg, dgf, loss_p,
        jnp.zeros((1, 8 * SV_COLS - 5888), F32)], axis=1).reshape(8, SV_COLS)
    full_in, full_small, svg = _reduce_call(g_in_pieces, g_small, small)
    full = full_small.reshape(SMALL_TOTAL, LANES)
    offs = [0]
    for r in SMALL_ROWS:
        offs.append(offs[-1] + r)
    gs_uq = full[offs[0]:offs[1]].reshape(Q_RANK, 192)
    gs_ukv = full[offs[1]:offs[2]].reshape(KV_RANK, 256)
    gs_a = full[offs[2]:offs[3]].reshape(512, 256)
    gs_b = full[offs[3]:offs[4]].reshape(512, 256)
    gs_out = full[offs[4]:offs[5]].reshape(256, D_MODEL)

    svm = svg.reshape(N_DEV, 8 * SV_COLS)
    dmod_sh = lax.dynamic_slice_in_dim(svm[:, :3 * D_MODEL], chip * 768, 768, axis=1)
    tot, gs_ada = _small_call(svm, call.T, dmod_sh)
    vec_offsets = {"b_ada": 0, "norm_gain": 3072, "q_norm_gain": 4096, "kv_norm_gain": 4480, "final_norm_gain": 4736}
    loss = tot[0, 5760]

    names = ["w_ada", "b_ada", "norm_gain", "w_in", "q_norm_gain", "w_uq", "kv_norm_gain", "w_ukv",
             "w_branch_a", "w_branch_b", "w_out", "final_norm_gain"]
    ws = [w_ada[0], b_ada, norm_gain, w_in_t, q_norm_gain, w_uq[0], kv_norm_gain, w_ukv[0],
          w_branch_a[0], w_branch_b[0], w_out[0], final_norm_gain.reshape(1, D_MODEL)]
    gs = [gs_ada, None, None, full_in, None, gs_uq, None, gs_ukv, gs_a, gs_b, gs_out, None]
    ms = [m_w_ada[0], m_b_ada, m_norm_gain, jnp.swapaxes(m_w_in[0], 0, 1), m_q_norm_gain, m_w_uq[0],
          m_kv_norm_gain, m_w_ukv[0], m_w_branch_a[0], m_w_branch_b[0], m_w_out[0],
          m_final_norm_gain.reshape(1, D_MODEL)]
    vs = [v_w_ada[0], v_b_ada, v_norm_gain, jnp.swapaxes(v_w_in[0], 0, 1), v_q_norm_gain, v_w_uq[0],
          v_kv_norm_gain, v_w_ukv[0], v_w_branch_a[0], v_w_branch_b[0], v_w_out[0],
          v_final_norm_gain.reshape(1, D_MODEL)]
    refs = [w_ada, b_ada, norm_gain, w_in, q_norm_gain, w_uq, kv_norm_gain, w_ukv,
            w_branch_a, w_branch_b, w_out, final_norm_gain]
    vec_ids = [k for k, n in enumerate(names) if n in vec_offsets]
    vec_outs = dict(zip(vec_ids, _adamw_vectors_call(
        tot, [vec_offsets[names[k]] for k in vec_ids], [ws[k] for k in vec_ids], [ms[k] for k in vec_ids],
        [vs[k] for k in vec_ids])))
    grads, deltas, new_ms, new_vs = [], [], [], []
    for k, (n, w_, g_, m_, v_, ref) in enumerate(zip(names, ws, gs, ms, vs, refs)):
        outs = vec_outs[k] if k in vec_outs else _adamw_call(n, w_, g_, m_, v_)
        if n == "w_in":
            outs = tuple(jnp.swapaxes(o_, 0, 1) for o_ in outs)
        for lst, o_ in zip((grads, deltas, new_ms, new_vs), outs):
            lst.append(o_.reshape(ref.shape))

    return (loss, grad_x.reshape(x.shape), *grads, *deltas, *new_ms, *new_vs)
```

```python
import math

import jax
import jax.numpy as jnp
from jax import lax
from jax.experimental import pallas as pl
from jax.experimental.pallas import tpu as pltpu

F32 = jnp.float32
BF16 = jnp.bfloat16

D_MODEL = 1024
SB_WIDTH = 512
MLA_WIDTH = 512
Q_RANK = 384
KV_RANK = 256
ROPE_DIM = 32
N_HEADS = 8
IN_WIDTH = 5280
EPS = 1e-6
ROPE_BASE = 10000.0
MLA_SCALE = 1.0 / math.sqrt(96.0)
SB_SCALE = 0.125
LOG2E = 1.4426950408889634

ADAM_LR = 0.001
ADAM_B1 = 0.9
ADAM_B2 = 0.999
ADAM_EPS = 1e-08
ADAM_WD = 0.01
ADAM_STEP = 10

O_SQ, O_SK, O_SV, O_SZ, O_CQ, O_CKV, O_MZ, O_GA, O_GB, O_KR, O_END = (
    0, 512, 1024, 1536, 2048, 2432, 2688, 3200, 4224, 5248, 5504)
W_INT = 5632

N_CHIPS = 4
N_DEV = 8
LANES = 128
SV_COLS = 768

ROW_TILE = 256
ATT_TILE = 256
ATT_Q_TILES = 2
FWD_Q_TILES = 4
SB_Q_TILES = 1
MLA_KEY_TILE = 512
VMEM_LIMIT = 56 * 1024 * 1024

MESH = pl.DeviceIdType.MESH


def _dot(a, b):
    return lax.dot_general(a, b, (((1,), (0,)), ((), ())), preferred_element_type=F32)


def _dot_nt(a, b):
    return lax.dot_general(a, b, (((1,), (1,)), ((), ())), preferred_element_type=F32)


def _dot_tn(a, b):
    return lax.dot_general(a, b, (((0,), (0,)), ((), ())), preferred_element_type=F32)


def _sigmoid(z):
    return 1.0 / (1.0 + jnp.exp2(z * (-LOG2E)))


def _params(sem=None):
    if sem is None:
        return pltpu.CompilerParams(vmem_limit_bytes=VMEM_LIMIT)
    return pltpu.CompilerParams(dimension_semantics=sem, vmem_limit_bytes=VMEM_LIMIT)


def _rows(tm, n):
    return pl.BlockSpec((tm, n), lambda i: (i, 0))


def _cols(n, tm):
    return pl.BlockSpec((n, tm), lambda i: (0, i))


def _whole(shape):
    nd = len(shape)
    return pl.BlockSpec(shape, lambda i: (0,) * nd)


def _sds(shape, dtype):
    return jax.ShapeDtypeStruct(shape, dtype)


def _flip(v, d):
    return 1 - v if d else v


def _weight_rows(a, b):
    if a >= O_KR:
        return True, slice(a - O_KR, b - O_KR)
    shift = ROPE_DIM if a >= O_MZ else 0
    return False, slice(a + shift, b + shift)


def _inproj_call(x, shift, scale, g1, w_in_t, w_kr, w_q, w_kv, qg, kvg, cos128, sin128):
    s_len = x.shape[0]
    tm = min(ROW_TILE, s_len)

    def body(x_ref, sh_ref, sc_ref, g1_ref, win_ref, wkr_ref, wq_ref, wkv_ref, qg_ref, kvg_ref, cos_ref, sin_ref,
             h_ref, sq_ref, sk_ref, sv_ref, sz_ref, cq_ref, ckv_ref, mz_ref, ga_ref, gb_ref, kpt_ref,
             qn_ref, qp_ref, kn_ref, vv_ref):
        xt = x_ref[...]
        r = lax.rsqrt(jnp.mean(xt * xt, axis=-1, keepdims=True) + EPS)
        h = (xt * r * g1_ref[...]) * (1.0 + sc_ref[...]) + sh_ref[...]
        hb = h.astype(BF16)
        h_ref[...] = hb

        def seg(a, b):
            from_kr, rows = _weight_rows(a, b)
            w_ref = wkr_ref if from_kr else win_ref
            return _dot_nt(hb[:, :HALF_D], w_ref[0, rows, :]) + _dot_nt(hb[:, HALF_D:], w_ref[1, rows, :])

        sq_ref[...] = (seg(O_SQ, O_SK) * SB_SCALE).astype(BF16)
        sk_ref[...] = seg(O_SK, O_SV).astype(BF16)
        sv_ref[...] = seg(O_SV, O_SZ).astype(BF16)
        sz_ref[...] = seg(O_SZ, O_CQ)
        mz_ref[...] = seg(O_MZ, O_GA)
        ga_ref[...] = seg(O_GA, O_GB)
        gb_ref[...] = seg(O_GB, O_KR)
        cos = cos_ref[...]
        sin = sin_ref[...]
        kr = seg(O_KR, O_END)
        kpt_ref[...] = (kr[:, :128] * cos + kr[:, 128:] * sin).astype(BF16)

        cq = seg(O_CQ, O_CKV)
        cq_ref[...] = cq
        rq = lax.rsqrt(jnp.mean(cq * cq, axis=-1, keepdims=True) + EPS)
        cqn = (cq * rq * qg_ref[...]).astype(BF16)
        qa = _dot(cqn, wq_ref[...])
        qn_ref[...] = qa[:, :512].astype(BF16)
        qp_ref[...] = (qa[:, 512:768] * jnp.tile(cos, (1, 2)) + qa[:, 768:] * jnp.tile(sin, (1, 2))).astype(BF16)

        ckv = seg(O_CKV, O_MZ)
        ckv_ref[...] = ckv
        rk = lax.rsqrt(jnp.mean(ckv * ckv, axis=-1, keepdims=True) + EPS)
        ckvn = (ckv * rk * kvg_ref[...]).astype(BF16)
        kva = _dot(ckvn, wkv_ref[...])
        kn_ref[...] = kva[:, :512].astype(BF16)
        vv_ref[...] = kva[:, 512:].astype(BF16)

    outs = [
        (D_MODEL, BF16), (512, BF16), (512, BF16), (512, BF16), (512, F32), (Q_RANK, F32), (KV_RANK, F32),
        (512, F32), (D_MODEL, F32), (D_MODEL, F32), (128, BF16), (512, BF16), (256, BF16), (512, BF16), (512, BF16),
    ]
    return pl.pallas_call(
        body, name="inproj", grid=(s_len // tm,),
        out_shape=tuple(_sds((s_len, n), dt) for n, dt in outs),
        in_specs=[_rows(tm, D_MODEL), _whole((1, D_MODEL)), _whole((1, D_MODEL)), _whole((1, D_MODEL)),
                  _whole((2, IN_WIDTH, HALF_D)), _whole((2, O_END - O_KR, HALF_D)),
                  _whole((Q_RANK, 1024)), _whole((KV_RANK, 1024)),
                  _whole((1, Q_RANK)), _whole((1, KV_RANK)), _rows(tm, LANES), _rows(tm, LANES)],
        out_specs=tuple(_rows(tm, n) for n, _ in outs),
        compiler_params=_params(("parallel",)),
    )(x, shift, scale, g1, w_in_t, w_kr, w_q, w_kv, qg, kvg, cos128, sin128)


Z_CLAMP = 80.0 * LOG2E
RUN_CUTOFF = 110.0 * LOG2E


def _softplus_clamped(z):
    zc = jnp.minimum(z * LOG2E, Z_CLAMP)
    return zc, jnp.log2(1.0 + jnp.exp2(zc))


def _tri_sum(a, tri):
    return _dot(a.astype(BF16), tri)


def _sb_fwd_call(q, k, v):
    s_len = q.shape[0]
    tk = min(ATT_TILE, s_len)
    tq = min(SB_Q_TILES * ATT_TILE, s_len)
    r = tq // tk
    nq = s_len // tq

    def body(q_ref, k_ref, v_ref, o_ref, lt_ref, first_ref):
        i = pl.program_id(1)
        q2 = q_ref[...]
        lane = lax.broadcasted_iota(jnp.int32, (1, 256), 1)
        krow = lax.broadcasted_iota(jnp.int32, (tk, tk), 0)
        kcol = lax.broadcasted_iota(jnp.int32, (tk, tk), 1)
        row = lax.broadcasted_iota(jnp.int32, (tq, tk), 0)
        col = lax.broadcasted_iota(jnp.int32, (tq, tk), 1)
        later = (krow > kcol).astype(BF16)
        valids = [col + u * tk < row for u in range(r)]
        hms = [(lane // 64) == hh for hh in range(4)]
        qms = [jnp.where(hm, q2, jnp.zeros_like(q2)) for hm in hms]

        def block(j, carry, valid):
            runs, acc = list(carry[:4]), carry[4]
            off = pl.multiple_of(j * tk, tk)
            kb = k_ref[pl.ds(off, tk), :]
            vb = v_ref[pl.ds(off, tk), :]
            ws = []
            for hh in range(4):
                zc, sp = _softplus_clamped(_dot_nt(qms[hh], kb))
                lm = jnp.where(valid, sp, 0.0) if valid is not None else sp
                suf = _tri_sum(lm, later)
                w = jnp.exp2(zc - sp - suf - runs[hh])
                if valid is not None:
                    w = jnp.where(valid, w, 0.0)
                ws.append(w.astype(BF16))
                runs[hh] = runs[hh] + jnp.sum(lm, axis=1, keepdims=True)
            vstack = jnp.concatenate([jnp.where(hm, vb, jnp.zeros_like(vb)) for hm in hms], axis=0)
            acc = acc + _dot(jnp.concatenate(ws, axis=1), vstack)
            return (*runs, acc)

        zero = jnp.zeros((tq, 1), F32)
        carry = (zero, zero, zero, zero, jnp.zeros((tq, 256), F32))
        for u in reversed(range(r)):
            carry = block(i * r + u, carry, valids[u])

        def least_run(runs):
            return jnp.min(jnp.minimum(jnp.minimum(runs[0], runs[1]), jnp.minimum(runs[2], runs[3])))

        n_full = i * r

        def unfinished(state):
            return jnp.logical_and(state[0] < n_full, state[1] <= RUN_CUTOFF)

        def visit(state):
            cr = block(n_full - 1 - state[0], state[2:], None)
            return (state[0] + 1, least_run(cr[:4]), *cr)

        state = lax.while_loop(unfinished, visit, (jnp.int32(0), least_run(carry[:4]), *carry))
        carry = state[2:]
        first_ref[...] = jnp.full(first_ref.shape, n_full - state[0], jnp.int32)
        for hh in range(4):
            lt_ref[0, :, hh:hh + 1] = carry[hh]
        o_ref[...] = carry[4]

    return pl.pallas_call(
        body, name="sb_fwd", grid=(2, nq),
        out_shape=(_sds((s_len, SB_WIDTH), F32), _sds((2, s_len, 4), F32), _sds((2, nq, 8, 128), jnp.int32)),
        in_specs=[pl.BlockSpec((tq, 256), lambda g, i: (i, g)),
                  pl.BlockSpec((s_len, 256), lambda g, i: (0, g)),
                  pl.BlockSpec((s_len, 256), lambda g, i: (0, g))],
        out_specs=(pl.BlockSpec((tq, 256), lambda g, i: (i, g)),
                   pl.BlockSpec((1, tq, 4), lambda g, i: (g, i, 0)),
                   pl.BlockSpec((1, 1, 8, 128), lambda g, i: (g, i, 0, 0))),
        compiler_params=_params(("parallel", "parallel")),
    )(q, k, v)


def _sb_bwd_call(first, q, k, v, do, lt):
    s_len = q.shape[0]
    tk = min(ATT_TILE, s_len)
    tq = min(SB_Q_TILES * ATT_TILE, s_len)
    r = tq // tk
    nq = s_len // tq
    nq_fwd = first.shape[0] // 2
    per_fwd = nq // nq_fwd

    def body(first_ref, q_ref, k_ref, v_ref, do_ref, lt_ref, dq_ref, dk_ref, dv_ref):
        g = pl.program_id(0)
        i = pl.program_id(1)

        @pl.when(i == 0)
        def _():
            dk_ref[...] = jnp.zeros_like(dk_ref)
            dv_ref[...] = jnp.zeros_like(dv_ref)

        q2 = q_ref[...]
        do2 = do_ref[...].astype(BF16)
        lane = lax.broadcasted_iota(jnp.int32, (1, 256), 1)
        krow = lax.broadcasted_iota(jnp.int32, (tk, tk), 0)
        kcol = lax.broadcasted_iota(jnp.int32, (tk, tk), 1)
        row = lax.broadcasted_iota(jnp.int32, (tq, tk), 0)
        col = lax.broadcasted_iota(jnp.int32, (tq, tk), 1)
        earlier = (krow < kcol).astype(BF16)
        later = (krow > kcol).astype(BF16)
        valids = [col + u * tk < row for u in range(r)]
        hms = [(lane // 64) == hh for hh in range(4)]
        qms = [jnp.where(hm, q2, jnp.zeros_like(q2)) for hm in hms]
        doms = [jnp.where(hm, do2, jnp.zeros_like(do2)) for hm in hms]
        ltots = [lt_ref[0, :, hh:hh + 1] for hh in range(4)]
        q2t = jnp.transpose(q2.astype(F32))
        do2t = jnp.transpose(do_ref[...])
        subl = lax.broadcasted_iota(jnp.int32, (256, 1), 0)
        qtstack = jnp.concatenate(
            [jnp.where((subl // 64) == hh, q2t, 0.0).astype(BF16) for hh in range(4)], axis=1)
        dotstack = jnp.concatenate(
            [jnp.where((subl // 64) == hh, do2t, 0.0).astype(BF16) for hh in range(4)], axis=1)

        def block(j, carry, valid):
            lpre, ppre, dq = list(carry[0:4]), list(carry[4:8]), carry[8]
            off = pl.multiple_of(j * tk, tk)
            kb = k_ref[pl.ds(off, tk), :]
            vb = v_ref[pl.ds(off, tk), :]
            dzs, avs = [], []
            for hh in range(4):
                zc, sp = _softplus_clamped(_dot_nt(qms[hh], kb))
                lsig = zc - sp
                lm = jnp.where(valid, sp, 0.0) if valid is not None else sp
                rowsum = jnp.sum(lm, axis=1, keepdims=True)
                between = _tri_sum(lm, later) + ((ltots[hh] - lpre[hh]) - rowsum)
                a = jnp.exp2(lsig - between)
                if valid is not None:
                    a = jnp.where(valid, a, 0.0)
                p = a * _dot_nt(doms[hh], vb)
                pbefore = ppre[hh] + _tri_sum(p, earlier)
                dz = p - jnp.exp2(lsig) * (p + pbefore)
                if valid is not None:
                    dz = jnp.where(valid, dz, 0.0)
                dzs.append(dz.astype(BF16))
                avs.append(a.astype(BF16))
                lpre[hh] = lpre[hh] + rowsum
                ppre[hh] = ppre[hh] + jnp.sum(p, axis=1, keepdims=True)
            kstack = jnp.concatenate([jnp.where(hm, kb, jnp.zeros_like(kb)) for hm in hms], axis=0)
            dq = dq + _dot(jnp.concatenate(dzs, axis=1), kstack)
            dk_ref[:, pl.ds(off, tk)] += _dot(qtstack, jnp.concatenate(dzs, axis=0))
            dv_ref[:, pl.ds(off, tk)] += _dot(dotstack, jnp.concatenate(avs, axis=0))
            return (*lpre, *ppre, dq)

        zero = jnp.zeros((tq, 1), F32)
        start = jnp.minimum(first_ref[g * nq_fwd + i // per_fwd], i * r)
        carry = lax.fori_loop(start, i * r, lambda j, cr: block(j, cr, None),
                              (zero,) * 8 + (jnp.zeros((tq, 256), F32),))
        for u in range(r):
            carry = block(i * r + u, carry, valids[u])
        dq_ref[...] = carry[8].astype(BF16)

    return pl.pallas_call(
        body, name="sb_bwd",
        out_shape=(_sds((s_len, SB_WIDTH), BF16), _sds((SB_WIDTH, s_len), F32), _sds((SB_WIDTH, s_len), F32)),
        grid_spec=pltpu.PrefetchScalarGridSpec(
            num_scalar_prefetch=1, grid=(2, nq),
            in_specs=[pl.BlockSpec((tq, 256), lambda g, i, f: (i, g)),
                      pl.BlockSpec((s_len, 256), lambda g, i, f: (0, g)),
                      pl.BlockSpec((s_len, 256), lambda g, i, f: (0, g)),
                      pl.BlockSpec((tq, 256), lambda g, i, f: (i, g)),
                      pl.BlockSpec((1, tq, 4), lambda g, i, f: (g, i, 0))],
            out_specs=(pl.BlockSpec((tq, 256), lambda g, i, f: (i, g)),
                       pl.BlockSpec((256, s_len), lambda g, i, f: (g, 0)),
                       pl.BlockSpec((256, s_len), lambda g, i, f: (g, 0)))),
        compiler_params=_params(("parallel", "arbitrary")),
    )(first, q, k, v, do, lt)


def _mla_fwd_call(qn, qp, kn, kpt, v):
    s_len = qn.shape[0]
    tk = min(MLA_KEY_TILE, s_len)
    tq = min(FWD_Q_TILES * ATT_TILE, s_len)
    r = tq // tk
    nq = s_len // tq

    def body(qn_ref, qp_ref, kn_ref, kpt_ref, v_ref, o_ref, lse_ref):
        i = pl.program_id(1)
        qn2 = qn_ref[...]
        qp2 = qp_ref[...]
        lane256 = lax.broadcasted_iota(jnp.int32, (1, 256), 1)
        lane128 = lax.broadcasted_iota(jnp.int32, (1, 128), 1)
        krow = lax.broadcasted_iota(jnp.int32, (tk, tk), 0)
        kcol = lax.broadcasted_iota(jnp.int32, (tk, tk), 1)
        row = lax.broadcasted_iota(jnp.int32, (tq, tk), 0)
        col = lax.broadcasted_iota(jnp.int32, (tq, tk), 1)
        valids = [col + u * tk <= row for u in range(r)]
        m64s = [(lane256 // 64) == hh for hh in range(4)]
        half = [(lane128 // 64) == u for u in range(2)]
        m32s = [(lane128 // 32) == hh for hh in range(4)]
        qcs = []
        for hh in range(4):
            qpair = qn2[:, 128 * (hh // 2):128 * (hh // 2) + 128]
            qcs.append(jnp.concatenate([jnp.where(half[hh % 2], qpair, jnp.zeros_like(qpair)),
                                        jnp.where(m32s[hh], qp2, jnp.zeros_like(qp2))], axis=1))

        def by_head(vals):
            return jnp.where(m64s[0], vals[0], jnp.where(m64s[1], vals[1], jnp.where(m64s[2], vals[2], vals[3])))

        def block(j, carry, valid):
            ms, ls, acc = list(carry[0:4]), list(carry[4:8]), carry[8]
            off = pl.multiple_of(j * tk, tk)
            knb = kn_ref[pl.ds(off, tk), :]
            kpb = kpt_ref[pl.ds(off, tk), :]
            vb = v_ref[pl.ds(off, tk), :]
            kcs = [jnp.concatenate([knb[:, 128 * pp:128 * pp + 128], kpb], axis=1) for pp in range(2)]
            ps, alphas = [], []
            for hh in range(4):
                s = _dot_nt(qcs[hh], kcs[hh // 2]) * (MLA_SCALE * LOG2E)
                if valid is not None:
                    s = jnp.where(valid, s, -1e30)
                mn = jnp.maximum(ms[hh], jnp.max(s, axis=1, keepdims=True))
                p = jnp.exp2(s - mn)
                alpha = jnp.exp2(ms[hh] - mn)
                ls[hh] = alpha * ls[hh] + jnp.sum(p, axis=1, keepdims=True)
                ms[hh] = mn
                ps.append(p.astype(BF16))
                alphas.append(alpha)
            pvs = []
            for pp in range(2):
                vpair = vb[:, 128 * pp:128 * pp + 128]
                vstack = jnp.concatenate([jnp.where(hf, vpair, jnp.zeros_like(vpair)) for hf in half], axis=0)
                pvs.append(_dot(jnp.concatenate(ps[2 * pp:2 * pp + 2], axis=1), vstack))
            acc = by_head(alphas) * acc + jnp.concatenate(pvs, axis=1)
            return (*ms, *ls, acc)

        neg = jnp.full((tq, 1), -1e30, F32)
        zero = jnp.zeros((tq, 1), F32)
        carry = lax.fori_loop(0, i * r, lambda j, cr: block(j, cr, None),
                              (neg,) * 4 + (zero,) * 4 + (jnp.zeros((tq, 256), F32),))
        for u in range(r):
            carry = block(i * r + u, carry, valids[u])
        o_ref[...] = carry[8] / by_head(list(carry[4:8]))
        for hh in range(4):
            lse_ref[0, :, hh:hh + 1] = (carry[hh] + jnp.log2(carry[4 + hh])) * (1.0 / LOG2E)

    return pl.pallas_call(
        body, name="mla_fwd", grid=(2, nq),
        out_shape=(_sds((s_len, MLA_WIDTH), F32), _sds((2, s_len, 4), F32)),
        in_specs=[pl.BlockSpec((tq, 256), lambda g, i: (i, g)),
                  pl.BlockSpec((tq, 128), lambda g, i: (i, g)),
                  pl.BlockSpec((s_len, 256), lambda g, i: (0, g)),
                  pl.BlockSpec((s_len, 128), lambda g, i: (0, 0)),
                  pl.BlockSpec((s_len, 256), lambda g, i: (0, g))],
        out_specs=(pl.BlockSpec((tq, 256), lambda g, i: (i, g)),
                   pl.BlockSpec((1, tq, 4), lambda g, i: (g, i, 0))),
        compiler_params=_params(("parallel", "parallel")),
    )(qn, qp, kn, kpt, v)


def _mla_bwd_call(qn, qp, kn, kpt, v, o, do, lse):
    s_len = qn.shape[0]
    tk = min(MLA_KEY_TILE, s_len)
    tq = min(ATT_Q_TILES * ATT_TILE, s_len)
    r = tq // tk
    nq = s_len // tq

    def body(qn_ref, qp_ref, kn_ref, kpt_ref, v_ref, o_ref, do_ref, lse_ref,
             dqn_ref, dqp_ref, dkn_ref, dkpt_ref, dv_ref):
        g = pl.program_id(0)
        i = pl.program_id(1)

        @pl.when(i == 0)
        def _():
            dkn_ref[...] = jnp.zeros_like(dkn_ref)
            dv_ref[...] = jnp.zeros_like(dv_ref)

        @pl.when((i == 0) & (g == 0))
        def _():
            dkpt_ref[...] = jnp.zeros_like(dkpt_ref)

        qn2 = qn_ref[...]
        qp2 = qp_ref[...]
        dof = do_ref[...]
        dob = dof.astype(BF16)
        prod = dof * o_ref[...]
        lane256 = lax.broadcasted_iota(jnp.int32, (1, 256), 1)
        lane128 = lax.broadcasted_iota(jnp.int32, (1, 128), 1)
        krow = lax.broadcasted_iota(jnp.int32, (tk, tk), 0)
        kcol = lax.broadcasted_iota(jnp.int32, (tk, tk), 1)
        row = lax.broadcasted_iota(jnp.int32, (tq, tk), 0)
        col = lax.broadcasted_iota(jnp.int32, (tq, tk), 1)
        valids = [col + u * tk <= row for u in range(r)]
        m64s = [(lane256 // 64) == hh for hh in range(4)]
        half = [(lane128 // 64) == u for u in range(2)]
        m32s = [(lane128 // 32) == hh for hh in range(4)]
        qcs, doms = [], []
        for hh in range(4):
            sl = slice(128 * (hh // 2), 128 * (hh // 2) + 128)
            qpair = qn2[:, sl]
            dpair = dob[:, sl]
            qcs.append(jnp.concatenate([jnp.where(half[hh % 2], qpair, jnp.zeros_like(qpair)),
                                        jnp.where(m32s[hh], qp2, jnp.zeros_like(qp2))], axis=1))
            doms.append(jnp.where(half[hh % 2], dpair, jnp.zeros_like(dpair)))
        dsums = [jnp.sum(jnp.where(m64, prod, 0.0), axis=1, keepdims=True) * MLA_SCALE for m64 in m64s]
        lses = [lse_ref[0, :, hh:hh + 1] * LOG2E for hh in range(4)]
        qn2t = jnp.transpose(qn2.astype(F32))
        qp2t = jnp.transpose(qp2.astype(F32))
        do2t = jnp.transpose(dof)
        sub128 = lax.broadcasted_iota(jnp.int32, (128, 1), 0)
        qtstacks, dotstacks = [], []
        for pp in range(2):
            qts, dts = [], []
            for u in range(2):
                hh = 2 * pp + u
                qts.append(jnp.concatenate(
                    [jnp.where((sub128 // 64) == u, qn2t[128 * pp:128 * pp + 128, :], 0.0),
                     jnp.where((sub128 // 32) == hh, qp2t, 0.0)], axis=0).astype(BF16))
                dts.append(jnp.where((sub128 // 64) == u, do2t[128 * pp:128 * pp + 128, :], 0.0).astype(BF16))
            qtstacks.append(jnp.concatenate(qts, axis=1))
            dotstacks.append(jnp.concatenate(dts, axis=1))

        def block(j, carry, valid):
            dqn, dqp = carry
            off = pl.multiple_of(j * tk, tk)
            knb = kn_ref[pl.ds(off, tk), :]
            kpb = kpt_ref[pl.ds(off, tk), :]
            vb = v_ref[pl.ds(off, tk), :]
            dqn_parts = []
            dkp = None
            for pp in range(2):
                sl = slice(128 * pp, 128 * pp + 128)
                knp = knb[:, sl]
                vpair = vb[:, sl]
                kc = jnp.concatenate([knp, kpb], axis=1)
                dss, pbs, kcms = [], [], []
                for u in range(2):
                    hh = 2 * pp + u
                    s = _dot_nt(qcs[hh], kc) * (MLA_SCALE * LOG2E)
                    if valid is not None:
                        s = jnp.where(valid, s, -1e30)
                    p = jnp.exp2(s - lses[hh])
                    ds = p * (_dot_nt(doms[hh], vpair) * MLA_SCALE - dsums[hh])
                    dss.append(ds.astype(BF16))
                    pbs.append(p.astype(BF16))
                    kcms.append(jnp.concatenate([jnp.where(half[u], knp, jnp.zeros_like(knp)),
                                                 jnp.where(m32s[hh], kpb, jnp.zeros_like(kpb))], axis=1))
                dqc = _dot(jnp.concatenate(dss, axis=1), jnp.concatenate(kcms, axis=0))
                dqn_parts.append(dqc[:, :128])
                dqp = dqp + dqc[:, 128:]
                dkc = _dot(qtstacks[pp], jnp.concatenate(dss, axis=0))
                dkn_ref[128 * pp:128 * pp + 128, pl.ds(off, tk)] += dkc[:128, :]
                dkp = dkc[128:, :] if dkp is None else dkp + dkc[128:, :]
                dv_ref[128 * pp:128 * pp + 128, pl.ds(off, tk)] += _dot(dotstacks[pp], jnp.concatenate(pbs, axis=0))
            dqn = dqn + jnp.concatenate(dqn_parts, axis=1)
            dkpt_ref[:, pl.ds(off, tk)] += dkp
            return dqn, dqp

        carry = lax.fori_loop(0, i * r, lambda j, cr: block(j, cr, None),
                              (jnp.zeros((tq, 256), F32), jnp.zeros((tq, 128), F32)))
        for u in range(r):
            carry = block(i * r + u, carry, valids[u])
        dqn, dqp = carry
        dqn_ref[...] = dqn.astype(BF16)
        dqp_ref[...] = dqp.astype(BF16)

    return pl.pallas_call(
        body, name="mla_bwd", grid=(2, nq),
        out_shape=(_sds((s_len, 512), BF16), _sds((s_len, 256), BF16), _sds((512, s_len), F32),
                   _sds((128, s_len), F32), _sds((512, s_len), F32)),
        in_specs=[pl.BlockSpec((tq, 256), lambda g, i: (i, g)),
                  pl.BlockSpec((tq, 128), lambda g, i: (i, g)),
                  pl.BlockSpec((s_len, 256), lambda g, i: (0, g)),
                  pl.BlockSpec((s_len, 128), lambda g, i: (0, 0)),
                  pl.BlockSpec((s_len, 256), lambda g, i: (0, g)),
                  pl.BlockSpec((tq, 256), lambda g, i: (i, g)),
                  pl.BlockSpec((tq, 256), lambda g, i: (i, g)),
                  pl.BlockSpec((1, tq, 4), lambda g, i: (g, i, 0))],
        out_specs=(pl.BlockSpec((tq, 256), lambda g, i: (i, g)),
                   pl.BlockSpec((tq, 128), lambda g, i: (i, g)),
                   pl.BlockSpec((256, s_len), lambda g, i: (g, 0)),
                   pl.BlockSpec((128, s_len), lambda g, i: (0, 0)),
                   pl.BlockSpec((256, s_len), lambda g, i: (g, 0))),
        compiler_params=_params(("arbitrary", "arbitrary")),
    )(qn, qp, kn, kpt, v, o, do, lse)


def _post_call(x, tgt, oa, ob, sz, mz, ga, gb, gate, gf, wa, wb, wo):
    s_len = x.shape[0]
    tm = min(ROW_TILE, s_len)
    nstep = s_len // tm

    def body(x_ref, t_ref, oa_ref, ob_ref, sz_ref, mz_ref, ga_ref, gb_ref, gate_ref, gf_ref,
             wa_ref, wb_ref, wo_ref,
             dx2_ref, doa_ref, dob_ref, dsz_ref, dmz_ref, dga_ref, dgb_ref,
             dwo_out, dwa_out, dwb_out, dgf_ref, dgate_ref, loss_ref, dwo_ref, dwa_ref, dwb_ref):
        @pl.when(pl.program_id(0) == 0)
        def _():
            dwo_ref[...] = jnp.zeros_like(dwo_ref)
            dwa_ref[...] = jnp.zeros_like(dwa_ref)
            dwb_ref[...] = jnp.zeros_like(dwb_ref)
            dgf_ref[...] = jnp.zeros_like(dgf_ref)
            dgate_ref[...] = jnp.zeros_like(dgate_ref)
            loss_ref[...] = jnp.zeros_like(loss_ref)

        gate = gate_ref[...]
        gf = gf_ref[...]
        oa = oa_ref[...]
        ob = ob_ref[...]
        sz = sz_ref[...]
        mz = mz_ref[...]
        sa = _sigmoid(sz)
        sb = _sigmoid(mz)
        silu_a = sz * sa
        silu_b = mz * sb
        ua = (oa * silu_a).astype(BF16)
        ub = (ob * silu_b).astype(BF16)
        ya = _dot(ua, wa_ref[...])
        yb = _dot(ub, wb_ref[...])
        sga = _sigmoid(ga_ref[...])
        sgb = _sigmoid(gb_ref[...])
        merged = (sga * ya + sgb * yb).astype(BF16)
        out = _dot(merged, wo_ref[...])
        x2 = x_ref[...] + gate * out
        r2 = lax.rsqrt(jnp.mean(x2 * x2, axis=-1, keepdims=True) + EPS)
        xhat = x2 * r2
        err = xhat * gf - t_ref[...]
        loss_ref[...] += 0.5 * jnp.sum(jnp.sum(err * err, axis=1, keepdims=True), axis=0, keepdims=True) / D_MODEL
        dy = err * (1.0 / D_MODEL)
        dgf_ref[...] += jnp.sum(dy * xhat, axis=0, keepdims=True)
        dxhat = dy * gf
        dx2 = r2 * (dxhat - xhat * jnp.mean(dxhat * xhat, axis=-1, keepdims=True))
        dx2_ref[...] = dx2
        dgate_ref[...] += jnp.sum(dx2 * out, axis=0, keepdims=True)
        dout = (dx2 * gate).astype(BF16)
        dmerged = _dot_nt(dout, wo_ref[...])
        dwo_ref[...] += _dot_tn(merged, dout)
        dya = dmerged * sga
        dyb = dmerged * sgb
        dga_ref[...] = (dya * ya * (1.0 - sga)).astype(BF16)
        dgb_ref[...] = (dyb * yb * (1.0 - sgb)).astype(BF16)
        dyab = dya.astype(BF16)
        dybb = dyb.astype(BF16)
        dua = _dot_nt(dyab, wa_ref[...])
        dub = _dot_nt(dybb, wb_ref[...])
        dwa_ref[...] += _dot_tn(ua, dyab)
        dwb_ref[...] += _dot_tn(ub, dybb)
        doa_ref[...] = dua * silu_a
        dob_ref[...] = dub * silu_b
        dsz_ref[...] = (dua * oa * (sa * (1.0 + sz * (1.0 - sa)))).astype(BF16)
        dmz_ref[...] = (dub * ob * (sb * (1.0 + mz * (1.0 - sb)))).astype(BF16)

        @pl.when(pl.program_id(0) == nstep - 1)
        def _():
            dwo_out[...] = dwo_ref[...].astype(BF16)
            for k in range(N_CHIPS):
                dwa_out[k] = dwa_ref[:, 256 * k:256 * k + 256].astype(BF16)
                dwb_out[k] = dwb_ref[:, 256 * k:256 * k + 256].astype(BF16)

    return pl.pallas_call(
        body, name="post", grid=(nstep,),
        out_shape=(_sds((s_len, D_MODEL), F32), _sds((s_len, 512), F32), _sds((s_len, 512), F32),
                   _sds((s_len, 512), BF16), _sds((s_len, 512), BF16),
                   _sds((s_len, D_MODEL), BF16), _sds((s_len, D_MODEL), BF16),
                   _sds((D_MODEL, D_MODEL), BF16), _sds((N_CHIPS, 512, 256), BF16), _sds((N_CHIPS, 512, 256), BF16),
                   _sds((1, D_MODEL), F32), _sds((1, D_MODEL), F32), _sds((1, 128), F32)),
        in_specs=[_rows(tm, D_MODEL), _rows(tm, D_MODEL), _rows(tm, 512), _rows(tm, 512), _rows(tm, 512),
                  _rows(tm, 512), _rows(tm, D_MODEL), _rows(tm, D_MODEL), _whole((1, D_MODEL)), _whole((1, D_MODEL)),
                  _whole((512, D_MODEL)), _whole((512, D_MODEL)), _whole((D_MODEL, D_MODEL))],
        out_specs=(_rows(tm, D_MODEL), _rows(tm, 512), _rows(tm, 512), _rows(tm, 512), _rows(tm, 512),
                   _rows(tm, D_MODEL), _rows(tm, D_MODEL),
                   _whole((D_MODEL, D_MODEL)), _whole((N_CHIPS, 512, 256)), _whole((N_CHIPS, 512, 256)),
                   _whole((1, D_MODEL)), _whole((1, D_MODEL)), _whole((1, 128))),
        scratch_shapes=[pltpu.VMEM((D_MODEL, D_MODEL), F32), pltpu.VMEM((512, D_MODEL), F32),
                        pltpu.VMEM((512, D_MODEL), F32)],
        compiler_params=_params(("arbitrary",)),
    )(x, tgt, oa, ob, sz, mz, ga, gb, gate, gf, wa, wb, wo)


def _bwdprep_call(dsq, dsk, dsv, dsz, dqn, dqp, dkn, dvv, dkpt, dmz, dga, dgb, cq, ckv, cos128, sin128,
                  qg, kvg, w_q, w_kv):
    s_len = cq.shape[0]
    tm = min(ROW_TILE, s_len)

    def body(dsq_ref, dsk_ref, dsv_ref, dsz_ref, dqn_ref, dqp_ref, dkn_ref, dvv_ref, dkpt_ref, dmz_ref,
             dga_ref, dgb_ref, cq_ref, ckv_ref, cos_ref, sin_ref, qg_ref, kvg_ref, wq_ref, wkv_ref,
             dp_ref, dwq_ref, dwkv_ref, dqg_ref, dkvg_ref):
        @pl.when(pl.program_id(0) == 0)
        def _():
            dwq_ref[...] = jnp.zeros_like(dwq_ref)
            dwkv_ref[...] = jnp.zeros_like(dwkv_ref)
            dqg_ref[...] = jnp.zeros_like(dqg_ref)
            dkvg_ref[...] = jnp.zeros_like(dkvg_ref)

        cos = cos_ref[...]
        sin = sin_ref[...]
        dp_ref[:, O_SQ:O_SK] = dsq_ref[...] * jnp.asarray(SB_SCALE, BF16)
        dp_ref[:, O_SK:O_SV] = jnp.transpose(dsk_ref[...]).astype(BF16)
        dp_ref[:, O_SV:O_SZ] = jnp.transpose(dsv_ref[...]).astype(BF16)
        dp_ref[:, O_SZ:O_CQ] = dsz_ref[...]
        dp_ref[:, O_MZ:O_GA] = dmz_ref[...]
        dp_ref[:, O_GA:O_GB] = dga_ref[...]
        dp_ref[:, O_GB:O_KR] = dgb_ref[...]
        dkp = jnp.transpose(dkpt_ref[...])
        dp_ref[:, O_KR:O_KR + 128] = (dkp * cos).astype(BF16)
        dp_ref[:, O_KR + 128:O_END] = (dkp * sin).astype(BF16)
        dp_ref[:, O_END:W_INT] = jnp.zeros((tm, W_INT - O_END), BF16)

        cq = cq_ref[...]
        rq = lax.rsqrt(jnp.mean(cq * cq, axis=-1, keepdims=True) + EPS)
        cqh = cq * rq
        qg = qg_ref[...]
        cqn = (cqh * qg).astype(BF16)
        dqp = dqp_ref[...].astype(F32)
        dqa = jnp.concatenate([dqn_ref[...], (dqp * jnp.tile(cos, (1, 2))).astype(BF16),
                               (dqp * jnp.tile(sin, (1, 2))).astype(BF16)], axis=1)
        dcqn = _dot_nt(dqa, wq_ref[...])
        dwq_ref[...] += _dot_tn(cqn, dqa)
        dqg_ref[...] += jnp.sum(dcqn * cqh, axis=0, keepdims=True)
        dh = dcqn * qg
        dcq = rq * (dh - cqh * jnp.mean(dh * cqh, axis=-1, keepdims=True))
        dp_ref[:, O_CQ:O_CKV] = dcq.astype(BF16)

        ckv = ckv_ref[...]
        rk = lax.rsqrt(jnp.mean(ckv * ckv, axis=-1, keepdims=True) + EPS)
        ckh = ckv * rk
        kvg = kvg_ref[...]
        ckvn = (ckh * kvg).astype(BF16)
        dkva = jnp.concatenate([jnp.transpose(dkn_ref[...]).astype(BF16),
                                jnp.transpose(dvv_ref[...]).astype(BF16)], axis=1)
        dckvn = _dot_nt(dkva, wkv_ref[...])
        dwkv_ref[...] += _dot_tn(ckvn, dkva)
        dkvg_ref[...] += jnp.sum(dckvn * ckh, axis=0, keepdims=True)
        dh2 = dckvn * kvg
        dckv = rk * (dh2 - ckh * jnp.mean(dh2 * ckh, axis=-1, keepdims=True))
        dp_ref[:, O_CKV:O_MZ] = dckv.astype(BF16)

    return pl.pallas_call(
        body, name="bwdprep", grid=(s_len // tm,),
        out_shape=(_sds((s_len, W_INT), BF16), _sds((Q_RANK, 1024), F32), _sds((KV_RANK, 1024), F32),
                   _sds((1, Q_RANK), F32), _sds((1, KV_RANK), F32)),
        in_specs=[_rows(tm, 512), _cols(512, tm), _cols(512, tm), _rows(tm, 512), _rows(tm, 512), _rows(tm, 256),
                  _cols(512, tm), _cols(512, tm), _cols(128, tm), _rows(tm, 512), _rows(tm, D_MODEL),
                  _rows(tm, D_MODEL), _rows(tm, Q_RANK), _rows(tm, KV_RANK), _rows(tm, LANES), _rows(tm, LANES),
                  _whole((1, Q_RANK)), _whole((1, KV_RANK)), _whole((Q_RANK, 1024)), _whole((KV_RANK, 1024))],
        out_specs=(_rows(tm, W_INT), _whole((Q_RANK, 1024)), _whole((KV_RANK, 1024)),
                   _whole((1, Q_RANK)), _whole((1, KV_RANK))),
        compiler_params=_params(("arbitrary",)),
    )(dsq, dsk, dsv, dsz, dqn, dqp, dkn, dvv, dkpt, dmz, dga, dgb, cq, ckv, cos128, sin128, qg, kvg, w_q, w_kv)


def _dh_call(dproj, w_in_t, w_kr, x, dx2, scale, g1):
    s_len = x.shape[0]
    tm = min(2 * ROW_TILE, s_len)
    parts = [(a, b) + _weight_rows(a, b) for a, b in ((0, O_MZ), (O_MZ, O_KR), (O_KR, O_END))]

    def body(dp_ref, win_ref, wkr_ref, x_ref, dx2_ref, sc_ref, g1_ref, gx_ref, dsh_ref, dsc_ref, dg1_ref):
        @pl.when(pl.program_id(0) == 0)
        def _():
            dsh_ref[...] = jnp.zeros_like(dsh_ref)
            dsc_ref[...] = jnp.zeros_like(dsc_ref)
            dg1_ref[...] = jnp.zeros_like(dg1_ref)

        def half(k):
            return sum(_dot(dp_ref[:, a:b], (wkr_ref if from_kr else win_ref)[k, rows, :])
                       for a, b, from_kr, rows in parts)

        dh = jnp.concatenate([half(0), half(1)], axis=1)
        xt = x_ref[...]
        r = lax.rsqrt(jnp.mean(xt * xt, axis=-1, keepdims=True) + EPS)
        xh = xt * r
        g1 = g1_ref[...]
        xg = xh * g1
        dsh_ref[...] += jnp.sum(dh, axis=0, keepdims=True)
        dsc_ref[...] += jnp.sum(dh * xg, axis=0, keepdims=True)
        dxg = dh * (1.0 + sc_ref[...])
        dg1_ref[...] += jnp.sum(dxg * xh, axis=0, keepdims=True)
        dxh = dxg * g1
        gx_ref[...] = dx2_ref[...] + r * (dxh - xh * jnp.mean(dxh * xh, axis=-1, keepdims=True))

    return pl.pallas_call(
        body, name="dh", grid=(s_len // tm,),
        out_shape=(_sds((s_len, D_MODEL), F32), _sds((1, D_MODEL), F32), _sds((1, D_MODEL), F32),
                   _sds((1, D_MODEL), F32)),
        in_specs=[_rows(tm, W_INT), _whole((2, IN_WIDTH, HALF_D)), _whole((2, O_END - O_KR, HALF_D)),
                  _rows(tm, D_MODEL), _rows(tm, D_MODEL), _whole((1, D_MODEL)), _whole((1, D_MODEL))],
        out_specs=(_rows(tm, D_MODEL), _whole((1, D_MODEL)), _whole((1, D_MODEL)), _whole((1, D_MODEL))),
        compiler_params=_params(("arbitrary",)),
    )(dproj, w_in_t, w_kr, x, dx2, scale, g1)


def _small_call(svg, ct, dmod_sh):
    def body(sv_ref, ct_ref, dm_ref, tot_ref, gwada_ref):
        acc = sv_ref[0:1, :]
        for d in range(1, N_DEV):
            acc = acc + sv_ref[d:d + 1, :]
        tot_ref[...] = acc
        gwada_ref[...] = lax.dot_general(ct_ref[...], dm_ref[...], (((1,), (0,)), ((), ())),
                                         precision=lax.Precision.HIGHEST, preferred_element_type=F32)

    vmem = pl.BlockSpec(memory_space=pltpu.VMEM)
    return pl.pallas_call(
        body, name="small_grads",
        out_shape=(_sds((1, 8 * SV_COLS), F32), _sds((D_MODEL, 768), F32)),
        in_specs=[vmem, vmem, vmem], out_specs=(vmem, vmem),
        compiler_params=_params(),
    )(svg, ct, dmod_sh)


def _adamw_tile_rows(rows, cols):
    budget = 2 << 20
    if rows * cols * 4 <= budget or rows % 8:
        return rows
    best = 8
    for tr in range(8, rows + 1, 8):
        if rows % tr == 0 and tr * cols * 4 <= budget:
            best = tr
    return best


def _adamw_math(w, gg, m, v):
    m2 = ADAM_B1 * m + (1.0 - ADAM_B1) * gg
    v2 = ADAM_B2 * v + (1.0 - ADAM_B2) * (gg * gg)
    m_hat = m2 / (1.0 - ADAM_B1 ** ADAM_STEP)
    v_hat = v2 / (1.0 - ADAM_B2 ** ADAM_STEP)
    return -ADAM_LR * (m_hat / (jnp.sqrt(v_hat) + ADAM_EPS) + ADAM_WD * w), m2, v2


def _adamw_call(name, w, g, m, v):
    rows, cols = w.shape
    tr = _adamw_tile_rows(rows, cols)
    halves = g.ndim == 3

    def body(w_ref, g_ref, m_ref, v_ref, *out_refs):
        if halves:
            gg = jnp.concatenate([g_ref[0], g_ref[1]], axis=1)
            out_refs[0][...] = gg
        else:
            gg = g_ref[...]
        d_ref, nm_ref, nv_ref = out_refs[-3:]
        d_ref[...], nm_ref[...], nv_ref[...] = _adamw_math(w_ref[...], gg, m_ref[...], v_ref[...])

    spec = pl.BlockSpec((tr, cols), lambda i: (i, 0))
    g_spec = pl.BlockSpec((2, tr, cols // 2), lambda i: (0, i, 0)) if halves else spec
    n_out = 4 if halves else 3
    outs = pl.pallas_call(
        body, name="adamw_" + name, grid=(rows // tr,),
        out_shape=(_sds((rows, cols), F32),) * n_out,
        in_specs=[spec, g_spec, spec, spec], out_specs=(spec,) * n_out,
        compiler_params=_params(("parallel",)),
    )(w, g, m, v)
    return tuple(outs) if halves else (g,) + tuple(outs)


def _pack_vectors_call(vectors):
    used = sum(v.shape[1] for v in vectors)

    def body(*refs):
        out_ref = refs[-1]
        row = jnp.concatenate([r[...] for r in refs[:-1]] + [jnp.zeros((1, 8 * SV_COLS - used), F32)], axis=1)
        for r in range(8):
            out_ref[r:r + 1, :] = row[:, SV_COLS * r:SV_COLS * (r + 1)]

    vmem = pl.BlockSpec(memory_space=pltpu.VMEM)
    return pl.pallas_call(
        body, name="pack_vectors", out_shape=_sds((8, SV_COLS), F32),
        in_specs=[vmem] * len(vectors), out_specs=vmem, compiler_params=_params(),
    )(*vectors)


def _adamw_vectors_call(tot, offsets, ws, ms, vs):
    nvec = len(ws)

    def body(tot_ref, *refs):
        ins, outs = refs[:3 * nvec], refs[3 * nvec:]
        for k in range(nvec):
            w_ref, m_ref, v_ref = ins[3 * k:3 * k + 3]
            g_ref, d_ref, nm_ref, nv_ref = outs[4 * k:4 * k + 4]
            gg = tot_ref[:, offsets[k]:offsets[k] + w_ref.shape[1]]
            g_ref[...] = gg
            d_ref[...], nm_ref[...], nv_ref[...] = _adamw_math(w_ref[...], gg, m_ref[...], v_ref[...])

    vmem = pl.BlockSpec(memory_space=pltpu.VMEM)
    outs = pl.pallas_call(
        body, name="adamw_vectors",
        out_shape=tuple(_sds(w.shape, F32) for w in ws for _ in range(4)),
        in_specs=[vmem] * (1 + 3 * nvec), out_specs=(vmem,) * (4 * nvec),
        compiler_params=_params(),
    )(tot, *[a for w, m, v in zip(ws, ms, vs) for a in (w, m, v)])
    return [tuple(outs[4 * k:4 * k + 4]) for k in range(nvec)]


IN_SHARD = IN_WIDTH // N_CHIPS
HALF_D = D_MODEL // 2
SMALL_ROWS = (576, 512, 1024, 1024, 2048)
SMALL_TOTAL = sum(SMALL_ROWS)
SMALL_HALF = SMALL_TOTAL // 2
SMALL_SUM_ROWS = 432


def _gather_call(c_row, w_ada_sh, pack_in, pack_small):
    def body(c_ref, wada_ref, pki_ref, pks_ref, mg_ref, cg_ref, gwi_ref, gws_ref,
             cv, ssem_c, rsem_c, ssem_m, rsem_m, ssem_w, rsem_w, ssem_f, rsem_f, lsem):
        x, y, c = lax.axis_index("x"), lax.axis_index("y"), lax.axis_index("c")
        me = 4 * x + 2 * y + c
        chip = 2 * x + y
        rel3 = [(1, 0), (0, 1), (1, 1)]
        packs = [(pki_ref, gwi_ref), (pks_ref, gws_ref)]

        def slot(a, gw, k, h):
            return gw.at[h, k] if a == 0 else gw.at[k, h]

        sends = []
        for j, (dx, dy) in enumerate(rel3):
            for a, (pk, gw) in enumerate(packs):
                cp = pltpu.make_async_remote_copy(
                    src_ref=pk.at[c], dst_ref=slot(a, gw, chip, c), send_sem=ssem_w.at[j, a],
                    recv_sem=rsem_w.at[j, a], device_id=(_flip(x, dx), _flip(y, dy), c), device_id_type=MESH)
                cp.start()
                sends.append(cp)
        owns = []
        for a, (pk, gw) in enumerate(packs):
            for h in range(2):
                own = pltpu.make_async_copy(pk.at[h], slot(a, gw, chip, h), lsem.at[a, h])
                own.start()
                owns.append(own)

        cv[me] = c_ref[...]
        for r in range(1, N_DEV):
            dx, dy, dc = (r >> 2) & 1, (r >> 1) & 1, r & 1
            cp = pltpu.make_async_remote_copy(
                src_ref=c_ref, dst_ref=cv.at[me], send_sem=ssem_c.at[r - 1], recv_sem=rsem_c.at[r - 1],
                device_id=(_flip(x, dx), _flip(y, dy), _flip(c, dc)), device_id_type=MESH)
            cp.start()
            sends.append(cp)
        for r in range(1, N_DEV):
            dx, dy, dc = (r >> 2) & 1, (r >> 1) & 1, r & 1
            src = 4 * _flip(x, dx) + 2 * _flip(y, dy) + _flip(c, dc)
            pltpu.make_async_remote_copy(
                src_ref=c_ref, dst_ref=cv.at[src], send_sem=ssem_c.at[r - 1], recv_sem=rsem_c.at[r - 1],
                device_id=(x, y, c), device_id_type=MESH).wait_recv()
        rows = lax.broadcasted_iota(jnp.int32, (N_DEV, D_MODEL), 0)
        call = jnp.zeros((N_DEV, D_MODEL), F32)
        for b in range(N_DEV):
            call = jnp.where(rows == b, jnp.broadcast_to(cv[b], (N_DEV, D_MODEL)), call)
        cg_ref[...] = call

        mg_ref[chip] = lax.dot_general(call, wada_ref[...], (((1,), (0,)), ((), ())),
                                       precision=lax.Precision.HIGHEST, preferred_element_type=F32)
        for j, (dx, dy) in enumerate(rel3):
            cp = pltpu.make_async_remote_copy(
                src_ref=mg_ref.at[chip], dst_ref=mg_ref.at[chip], send_sem=ssem_m.at[j], recv_sem=rsem_m.at[j],
                device_id=(_flip(x, dx), _flip(y, dy), c), device_id_type=MESH)
            cp.start()
            sends.append(cp)
        for j, (dx, dy) in enumerate(rel3):
            src_chip = 2 * _flip(x, dx) + _flip(y, dy)
            pltpu.make_async_remote_copy(
                src_ref=mg_ref.at[src_chip], dst_ref=mg_ref.at[src_chip], send_sem=ssem_m.at[j],
                recv_sem=rsem_m.at[j], device_id=(x, y, c), device_id_type=MESH).wait_recv()
        for j, (dx, dy) in enumerate(rel3):
            src_chip = 2 * _flip(x, dx) + _flip(y, dy)
            for a, (pk, gw) in enumerate(packs):
                pltpu.make_async_remote_copy(
                    src_ref=pk.at[c], dst_ref=slot(a, gw, src_chip, c), send_sem=ssem_w.at[j, a],
                    recv_sem=rsem_w.at[j, a], device_id=(x, y, c), device_id_type=MESH).wait_recv()
                cp = pltpu.make_async_remote_copy(
                    src_ref=slot(a, gw, src_chip, c), dst_ref=slot(a, gw, src_chip, c), send_sem=ssem_f.at[j, a],
                    recv_sem=rsem_f.at[j, a], device_id=(x, y, 1 - c), device_id_type=MESH)
                cp.start()
                sends.append(cp)
        for j, (dx, dy) in enumerate(rel3):
            src_chip = 2 * _flip(x, dx) + _flip(y, dy)
            for a, (pk, gw) in enumerate(packs):
                pltpu.make_async_remote_copy(
                    src_ref=pk.at[c], dst_ref=slot(a, gw, src_chip, 1 - c), send_sem=ssem_f.at[j, a],
                    recv_sem=rsem_f.at[j, a], device_id=(x, y, c), device_id_type=MESH).wait_recv()
        for cp in sends:
            cp.wait_send()
        for own in owns:
            own.wait()

    vmem = pl.BlockSpec(memory_space=pltpu.VMEM)
    return pl.pallas_call(
        body, name="gather_fwd",
        out_shape=(_sds((N_CHIPS, N_DEV, 768), F32), _sds((N_DEV, D_MODEL), F32),
                   _sds((2, N_CHIPS, IN_SHARD, HALF_D), BF16), _sds((N_CHIPS, 2, SMALL_HALF, LANES), BF16)),
        in_specs=[vmem, vmem, vmem, vmem], out_specs=(vmem, vmem, vmem, vmem),
        scratch_shapes=[
            pltpu.VMEM((N_DEV, 1, D_MODEL), F32),
            pltpu.SemaphoreType.DMA((N_DEV - 1,)), pltpu.SemaphoreType.DMA((N_DEV - 1,)),
            pltpu.SemaphoreType.DMA((3,)), pltpu.SemaphoreType.DMA((3,)),
            pltpu.SemaphoreType.DMA((3, 2)), pltpu.SemaphoreType.DMA((3, 2)),
            pltpu.SemaphoreType.DMA((3, 2)), pltpu.SemaphoreType.DMA((3, 2)),
            pltpu.SemaphoreType.DMA((2, 2)),
        ],
        compiler_params=_params(),
    )(c_row, w_ada_sh, pack_in, pack_small)


def _reduce_call(g_in, g_small, sv):
    def body(gi_ref, gs_ref, sv_ref, fi_ref, fs_ref, svg_ref, pair_i, pair_s, send_i, send_s, land_i, land_s,
             ssem_p, rsem_p, ssem_g, rsem_g, ssem_s, rsem_s, ssem_x, rsem_x):
        x, y, c = lax.axis_index("x"), lax.axis_index("y"), lax.axis_index("c")
        me = 4 * x + 2 * y + c
        chip = 2 * x + y
        rel3 = [(1, 0), (0, 1), (1, 1)]
        payloads = [(gi_ref, pair_i, send_i, land_i, fi_ref), (gs_ref, pair_s, send_s, land_s, fs_ref)]
        where = [lambda k, h: N_CHIPS * h + k, lambda k, h: 2 * k + h]
        copies = []

        for k in range(N_CHIPS):
            for a, (g, pair, _, _, _) in enumerate(payloads):
                cp = pltpu.make_async_remote_copy(
                    src_ref=g.at[where[a](k, 1 - c)], dst_ref=pair.at[k], send_sem=ssem_p.at[k, a],
                    recv_sem=rsem_p.at[k, a], device_id=(x, y, 1 - c), device_id_type=MESH)
                cp.start()
                copies.append(cp)

        for r in range(1, N_DEV):
            dx, dy, dc = (r >> 2) & 1, (r >> 1) & 1, r & 1
            cp = pltpu.make_async_remote_copy(
                src_ref=sv_ref, dst_ref=svg_ref.at[me], send_sem=ssem_s.at[r - 1], recv_sem=rsem_s.at[r - 1],
                device_id=(_flip(x, dx), _flip(y, dy), _flip(c, dc)), device_id_type=MESH)
            cp.start()
            copies.append(cp)
        svg_ref[me] = sv_ref[...]

        def pair_sum(k, store_in, store_small):
            for a, (g, pair, _, _, _) in enumerate(payloads):
                pltpu.make_async_remote_copy(
                    src_ref=g.at[where[a](k, c)], dst_ref=pair.at[k], send_sem=ssem_p.at[k, a],
                    recv_sem=rsem_p.at[k, a], device_id=(x, y, c), device_id_type=MESH).wait_recv()
            for qd in range(HALF_D // LANES):
                sl = slice(LANES * qd, LANES * qd + LANES)
                store_in(sl, gi_ref[where[0](k, c), :, sl].astype(F32) + pair_i[k, :, sl].astype(F32))

            def rows(i, carry):
                sl = pl.ds(pl.multiple_of(i * SMALL_SUM_ROWS, 16), SMALL_SUM_ROWS)
                store_small(sl, gs_ref[where[1](k, c), sl, :].astype(F32) + pair_s[k, sl, :].astype(F32))
                return carry

            lax.fori_loop(0, SMALL_HALF // SMALL_SUM_ROWS, rows, 0)

        for j, (dx, dy) in enumerate(rel3):
            tx, ty = _flip(x, dx), _flip(y, dy)

            def put_in(sl, val, j=j):
                send_i[j, :, sl] = val.astype(BF16)

            def put_small(sl, val, j=j):
                send_s[j, sl, :] = val.astype(BF16)

            pair_sum(2 * tx + ty, put_in, put_small)
            for a, (_, _, send, land, _) in enumerate(payloads):
                cp = pltpu.make_async_remote_copy(
                    src_ref=send.at[j], dst_ref=land.at[j], send_sem=ssem_g.at[j, a], recv_sem=rsem_g.at[j, a],
                    device_id=(tx, ty, c), device_id_type=MESH)
                cp.start()
                copies.append(cp)

        def own_in(sl, val):
            fi_ref[c, :, sl] = val

        def own_small(sl, val):
            fs_ref[c, sl, :] = val

        pair_sum(chip, own_in, own_small)
        for j in range(3):
            for a, (_, _, send, land, _) in enumerate(payloads):
                pltpu.make_async_remote_copy(
                    src_ref=send.at[j], dst_ref=land.at[j], send_sem=ssem_g.at[j, a], recv_sem=rsem_g.at[j, a],
                    device_id=(x, y, c), device_id_type=MESH).wait_recv()
            for qd in range(HALF_D // LANES):
                sl = slice(LANES * qd, LANES * qd + LANES)
                fi_ref[c, :, sl] += land_i[j, :, sl].astype(F32)

            def add_rows(i, carry, j=j):
                sl = pl.ds(pl.multiple_of(i * SMALL_SUM_ROWS, 16), SMALL_SUM_ROWS)
                fs_ref[c, sl, :] += land_s[j, sl, :].astype(F32)
                return carry

            lax.fori_loop(0, SMALL_HALF // SMALL_SUM_ROWS, add_rows, 0)

        for a, f in enumerate((fi_ref, fs_ref)):
            cp = pltpu.make_async_remote_copy(
                src_ref=f.at[c], dst_ref=f.at[c], send_sem=ssem_x.at[a], recv_sem=rsem_x.at[a],
                device_id=(x, y, 1 - c), device_id_type=MESH)
            cp.start()
            copies.append(cp)
        for a, f in enumerate((fi_ref, fs_ref)):
            pltpu.make_async_remote_copy(
                src_ref=f.at[c], dst_ref=f.at[1 - c], send_sem=ssem_x.at[a], recv_sem=rsem_x.at[a],
                device_id=(x, y, c), device_id_type=MESH).wait_recv()
        for r in range(1, N_DEV):
            dx, dy, dc = (r >> 2) & 1, (r >> 1) & 1, r & 1
            src = 4 * _flip(x, dx) + 2 * _flip(y, dy) + _flip(c, dc)
            pltpu.make_async_remote_copy(
                src_ref=sv_ref, dst_ref=svg_ref.at[src], send_sem=ssem_s.at[r - 1],
                recv_sem=rsem_s.at[r - 1], device_id=(x, y, c), device_id_type=MESH).wait_recv()
        for cp in copies:
            cp.wait_send()

    vmem = pl.BlockSpec(memory_space=pltpu.VMEM)
    return pl.pallas_call(
        body, name="grad_reduce",
        out_shape=(_sds((2, IN_SHARD, HALF_D), F32), _sds((2, SMALL_HALF, LANES), F32),
                   _sds((N_DEV, 8, SV_COLS), F32)),
        in_specs=[vmem, vmem, vmem], out_specs=(vmem, vmem, vmem),
        scratch_shapes=[
            pltpu.VMEM((N_CHIPS, IN_SHARD, HALF_D), BF16), pltpu.VMEM((N_CHIPS, SMALL_HALF, LANES), BF16),
            pltpu.VMEM((3, IN_SHARD, HALF_D), BF16), pltpu.VMEM((3, SMALL_HALF, LANES), BF16),
            pltpu.VMEM((3, IN_SHARD, HALF_D), BF16), pltpu.VMEM((3, SMALL_HALF, LANES), BF16),
            pltpu.SemaphoreType.DMA((N_CHIPS, 2)), pltpu.SemaphoreType.DMA((N_CHIPS, 2)),
            pltpu.SemaphoreType.DMA((3, 2)), pltpu.SemaphoreType.DMA((3, 2)),
            pltpu.SemaphoreType.DMA((N_DEV - 1,)), pltpu.SemaphoreType.DMA((N_DEV - 1,)),
            pltpu.SemaphoreType.DMA((2,)), pltpu.SemaphoreType.DMA((2,)),
        ],
        compiler_params=_params(),
    )(g_in, g_small, sv)


def _dwin_call(h, dproj):
    s_len = h.shape[0]
    tm = min(4 * ROW_TILE, s_len)
    nrow = s_len // tm
    nc = 4
    chunk = W_INT // nc

    def body(h_ref, dp_ref, dw_ref, acc):
        c, i = pl.program_id(0), pl.program_id(1)

        @pl.when(i == 0)
        def _():
            acc[...] = jnp.zeros_like(acc)

        acc[...] += _dot_tn(dp_ref[...], h_ref[...])

        def put(lo, hi, dst):
            for half in range(2):
                dw_ref[half, dst:dst + hi - lo, :] = acc[lo:hi, half * HALF_D:(half + 1) * HALF_D].astype(BF16)

        for cc in range(nc):
            @pl.when((c == cc) & (i == nrow - 1))
            def _(cc=cc):
                base = cc * chunk
                lo, hi = base, min(base + chunk, O_MZ)
                if lo < hi:
                    put(lo - base, hi - base, lo)
                lo, hi = max(base, O_MZ), min(base + chunk, O_KR)
                if lo < hi:
                    put(lo - base, hi - base, lo + ROPE_DIM)
                if base <= O_KR and O_END <= base + chunk:
                    r = O_KR - base
                    half_r = ROPE_DIM // 2
                    kr = sum(acc[r + ROPE_DIM * j:r + ROPE_DIM * (j + 1), :] for j in range(4))
                    sw = sum(acc[r + 128 + ROPE_DIM * j:r + 128 + ROPE_DIM * (j + 1), :] for j in range(4))
                    tot = kr + jnp.concatenate([sw[half_r:], sw[:half_r]], axis=0)
                    for half in range(2):
                        dw_ref[half, O_MZ:O_MZ + ROPE_DIM, :] = (
                            tot[:, half * HALF_D:(half + 1) * HALF_D].astype(BF16))

    assert O_KR >= (nc - 1) * chunk
    return pl.pallas_call(
        body, name="dwin", grid=(nc, nrow),
        out_shape=_sds((2, IN_WIDTH, HALF_D), BF16),
        in_specs=[pl.BlockSpec((tm, D_MODEL), lambda c, i: (i, 0)),
                  pl.BlockSpec((tm, chunk), lambda c, i: (i, c))],
        out_specs=pl.BlockSpec((2, IN_WIDTH, HALF_D), lambda c, i: (0, 0, 0)),
        scratch_shapes=[pltpu.VMEM((chunk, D_MODEL), F32)],
        compiler_params=_params(("arbitrary", "arbitrary")),
    )(h, dproj)


def _internal_weights(w_in_t, w_uq, w_ukv):
    krot_t = w_in_t[:, O_MZ:O_MZ + ROPE_DIM]
    krot_sw = krot_t.reshape(2, 2, ROPE_DIM // 2, HALF_D)[:, ::-1].reshape(2, ROPE_DIM, HALF_D)
    w_kr = jnp.concatenate([jnp.tile(krot_t, (1, 4, 1)), jnp.tile(krot_sw, (1, 4, 1))], axis=1)
    uq = w_uq.reshape(Q_RANK, N_HEADS, 96)
    wp = uq[:, :, 64:].reshape(Q_RANK, 256)
    w_q = jnp.concatenate([uq[:, :, :64].reshape(Q_RANK, 512), wp, _swap_halves(wp, 32)], axis=1)
    ukv = w_ukv.reshape(KV_RANK, N_HEADS, 128)
    w_kv = jnp.concatenate([ukv[:, :, :64].reshape(KV_RANK, 512), ukv[:, :, 64:].reshape(KV_RANK, 512)], axis=1)
    return w_kr, w_q, w_kv


def _true_weight_grads(dwq, dwkv):
    dwp = dwq[:, 512:768] + _swap_halves(dwq[:, 768:1024], 32)
    g_uq = jnp.concatenate([dwq[:, :512].reshape(Q_RANK, N_HEADS, 64), dwp.reshape(Q_RANK, N_HEADS, 32)],
                           axis=2).reshape(Q_RANK, 768)
    g_ukv = jnp.concatenate([dwkv[:, :512].reshape(KV_RANK, N_HEADS, 64), dwkv[:, 512:].reshape(KV_RANK, N_HEADS, 64)],
                            axis=2).reshape(KV_RANK, 1024)
    return g_uq, g_ukv


def _swap_halves(w, group):
    r, n = w.shape
    return w.reshape(r, n // group, 2, group // 2)[:, :, ::-1, :].reshape(r, n)


def _pack_shards(parts):
    return jnp.concatenate([p.reshape(-1, LANES) for p in parts], axis=0)


def _unpack_small(gw):
    offs = [0]
    for r in SMALL_ROWS:
        offs.append(offs[-1] + r)

    def cols(i, rows, shard_cols):
        blk = gw[:, offs[i]:offs[i + 1]].reshape(N_CHIPS, rows, shard_cols)
        return blk.transpose(1, 0, 2).reshape(rows, N_CHIPS * shard_cols)

    return (cols(0, Q_RANK, 192), cols(1, KV_RANK, 256), cols(2, 512, 256), cols(3, 512, 256),
            gw[:, offs[4]:offs[5]].reshape(D_MODEL, D_MODEL))


def _chip_major(g, shard_cols):
    r = g.shape[0]
    return g.reshape(r, N_CHIPS, shard_cols).transpose(1, 0, 2).reshape(N_CHIPS, -1, LANES)


def kernel(x, c, positions, w_ada, b_ada, norm_gain, w_in, q_norm_gain, w_uq, kv_norm_gain, w_ukv, w_branch_a, w_branch_b, w_out, final_norm_gain, loss_target, m_w_ada, m_b_ada, m_norm_gain, m_w_in, m_q_norm_gain, m_w_uq, m_kv_norm_gain, m_w_ukv, m_w_branch_a, m_w_branch_b, m_w_out, m_final_norm_gain, v_w_ada, v_b_ada, v_norm_gain, v_w_in, v_q_norm_gain, v_w_uq, v_kv_norm_gain, v_w_ukv, v_w_branch_a, v_w_branch_b, v_w_out, v_final_norm_gain):
    ix, iy, ic = lax.axis_index("x"), lax.axis_index("y"), lax.axis_index("c")
    me = 4 * ix + 2 * iy + ic
    chip = 2 * ix + iy
    xs = x[0]
    tgt = loss_target[0]
    s_len = xs.shape[0]

    w_in_t = jnp.swapaxes(w_in[0], 0, 1)
    w_in_tb = w_in_t.astype(BF16)
    pack_in = jnp.stack([w_in_tb[:, :HALF_D], w_in_tb[:, HALF_D:]], axis=0)
    small_shards = (w_uq[0], w_ukv[0], w_branch_a[0], w_branch_b[0], w_out[0])
    pack_small = _pack_shards([s.astype(BF16) for s in small_shards]).reshape(2, SMALL_HALF, LANES)
    mg, call, gw_in, gw_small = _gather_call(c, w_ada[0], pack_in, pack_small)
    mod = mg.transpose(1, 0, 2).reshape(N_DEV, 3 * D_MODEL) + b_ada
    mod_me = lax.dynamic_slice_in_dim(mod, me, 1, axis=0)
    shift, scale, gate = mod_me[:, :D_MODEL], mod_me[:, D_MODEL:2 * D_MODEL], mod_me[:, 2 * D_MODEL:]

    f_in_t = gw_in.reshape(2, IN_WIDTH, HALF_D)
    f_uq, f_ukv, f_a, f_b, f_out = _unpack_small(gw_small.reshape(N_CHIPS, SMALL_TOTAL, LANES))
    w_kr, w_q, w_kv = _internal_weights(f_in_t, f_uq, f_ukv)

    inv_freq = ROPE_BASE ** (-jnp.arange(0, ROPE_DIM, 2, dtype=F32) / ROPE_DIM)
    ang = positions[0].astype(F32)[:, None] * inv_freq
    cs, sn = jnp.cos(ang), jnp.sin(ang)
    cos128 = jnp.tile(jnp.concatenate([cs, cs], axis=1), (1, 4))
    sin128 = jnp.tile(jnp.concatenate([-sn, sn], axis=1), (1, 4))

    (h, sq, sk, sv, sz, cq, ckv, mz, ga, gb, kpt, qn, qp, kn, vv) = _inproj_call(
        xs, shift, scale, norm_gain, f_in_t, w_kr, w_q, w_kv, q_norm_gain, kv_norm_gain, cos128, sin128)
    oa, lt, first = _sb_fwd_call(sq, sk, sv)
    ob, lse = _mla_fwd_call(qn, qp, kn, kpt, vv)

    gf = final_norm_gain.reshape(1, D_MODEL)
    (dx2, doa, dob, dsz, dmz, dga, dgb, dwo, dwa, dwb, dgf, dgate, loss_p) = _post_call(
        xs, tgt, oa, ob, sz, mz, ga, gb, gate, gf, f_a, f_b, f_out)

    dsq, dsk_t, dsv_t = _sb_bwd_call(first[:, :, 0, 0].reshape(-1), sq, sk, sv, doa, lt)
    dqn, dqp, dkn_t, dkpt_t, dvv_t = _mla_bwd_call(qn, qp, kn, kpt, vv, ob, dob, lse)

    dproj, dwq, dwkv, dqg, dkvg = _bwdprep_call(
        dsq, dsk_t, dsv_t, dsz, dqn, dqp, dkn_t, dvv_t, dkpt_t, dmz, dga, dgb, cq, ckv, cos128, sin128,
        q_norm_gain, kv_norm_gain, w_q, w_kv)
    grad_x, dshift, dscale, dg1 = _dh_call(dproj, f_in_t, w_kr, xs, dx2, scale, norm_gain)
    g_in_t = _dwin_call(h, dproj)
    g_uq, g_ukv = _true_weight_grads(dwq, dwkv)

    g_in_pieces = g_in_t.reshape(N_DEV, IN_SHARD, HALF_D)
    g_small = jnp.concatenate([
        _chip_major(g_uq, 192).astype(BF16), _chip_major(g_ukv, 256).astype(BF16),
        dwa.reshape(N_CHIPS, -1, LANES), dwb.reshape(N_CHIPS, -1, LANES),
        dwo.reshape(N_CHIPS, -1, LANES)], axis=1).reshape(N_DEV, SMALL_HALF, LANES)
    small = _pack_vectors_call([dshift, dscale, dgate, dg1, dqg, dkvg, dgf, loss_p])
    full_in, full_small, svg = _reduce_call(g_in_pieces, g_small, small)
    full = full_small.reshape(SMALL_TOTAL, LANES)
    offs = [0]
    for r in SMALL_ROWS:
        offs.append(offs[-1] + r)
    gs_uq = full[offs[0]:offs[1]].reshape(Q_RANK, 192)
    gs_ukv = full[offs[1]:offs[2]].reshape(KV_RANK, 256)
    gs_a = full[offs[2]:offs[3]].reshape(512, 256)
    gs_b = full[offs[3]:offs[4]].reshape(512, 256)
    gs_out = full[offs[4]:offs[5]].reshape(256, D_MODEL)

    svm = svg.reshape(N_DEV, 8 * SV_COLS)
    dmod_sh = lax.dynamic_slice_in_dim(svm[:, :3 * D_MODEL], chip * 768, 768, axis=1)
    tot, gs_ada = _small_call(svm, call.T, dmod_sh)
    vec_offsets = {"b_ada": 0, "norm_gain": 3072, "q_norm_gain": 4096, "kv_norm_gain": 4480, "final_norm_gain": 4736}
    loss = tot[0, 5760]

    names = ["w_ada", "b_ada", "norm_gain", "w_in", "q_norm_gain", "w_uq", "kv_norm_gain", "w_ukv",
             "w_branch_a", "w_branch_b", "w_out", "final_norm_gain"]
    ws = [w_ada[0], b_ada, norm_gain, w_in_t, q_norm_gain, w_uq[0], kv_norm_gain, w_ukv[0],
          w_branch_a[0], w_branch_b[0], w_out[0], final_norm_gain.reshape(1, D_MODEL)]
    gs = [gs_ada, None, None, full_in, None, gs_uq, None, gs_ukv, gs_a, gs_b, gs_out, None]
    ms = [m_w_ada[0], m_b_ada, m_norm_gain, jnp.swapaxes(m_w_in[0], 0, 1), m_q_norm_gain, m_w_uq[0],
          m_kv_norm_gain, m_w_ukv[0], m_w_branch_a[0], m_w_branch_b[0], m_w_out[0],
          m_final_norm_gain.reshape(1, D_MODEL)]
    vs = [v_w_ada[0], v_b_ada, v_norm_gain, jnp.swapaxes(v_w_in[0], 0, 1), v_q_norm_gain, v_w_uq[0],
          v_kv_norm_gain, v_w_ukv[0], v_w_branch_a[0], v_w_branch_b[0], v_w_out[0],
          v_final_norm_gain.reshape(1, D_MODEL)]
    refs = [w_ada, b_ada, norm_gain, w_in, q_norm_gain, w_uq, kv_norm_gain, w_ukv,
            w_branch_a, w_branch_b, w_out, final_norm_gain]
    vec_ids = [k for k, n in enumerate(names) if n in vec_offsets]
    vec_outs = dict(zip(vec_ids, _adamw_vectors_call(
        tot, [vec_offsets[names[k]] for k in vec_ids], [ws[k] for k in vec_ids], [ms[k] for k in vec_ids],
        [vs[k] for k in vec_ids])))
    grads, deltas, new_ms, new_vs = [], [], [], []
    for k, (n, w_, g_, m_, v_, ref) in enumerate(zip(names, ws, gs, ms, vs, refs)):
        outs = vec_outs[k] if k in vec_outs else _adamw_call(n, w_, g_, m_, v_)
        if n == "w_in":
            outs = tuple(jnp.swapaxes(o_, 0, 1) for o_ in outs)
        for lst, o_ in zip((grads, deltas, new_ms, new_vs), outs):
            lst.append(o_.reshape(ref.shape))

    return (loss, grad_x.reshape(x.shape), *grads, *deltas, *new_ms, *new_vs)
```

```python
import math

import jax
import jax.numpy as jnp
from jax import lax
from jax.experimental import pallas as pl
from jax.experimental.pallas import tpu as pltpu

F32 = jnp.float32
BF16 = jnp.bfloat16

D_MODEL = 1024
SB_WIDTH = 512
MLA_WIDTH = 512
Q_RANK = 384
KV_RANK = 256
ROPE_DIM = 32
N_HEADS = 8
IN_WIDTH = 5280
EPS = 1e-6
ROPE_BASE = 10000.0
MLA_SCALE = 1.0 / math.sqrt(96.0)
SB_SCALE = 0.125
LOG2E = 1.4426950408889634

ADAM_LR = 0.001
ADAM_B1 = 0.9
ADAM_B2 = 0.999
ADAM_EPS = 1e-08
ADAM_WD = 0.01
ADAM_STEP = 10

O_SQ, O_SK, O_SV, O_SZ, O_CQ, O_CKV, O_MZ, O_GA, O_GB, O_KR, O_END = (
    0, 512, 1024, 1536, 2048, 2432, 2688, 3200, 4224, 5248, 5504)
W_INT = 5632

N_CHIPS = 4
N_DEV = 8
LANES = 128
SV_COLS = 768

ROW_TILE = 256
ATT_TILE = 256
ATT_Q_TILES = 2
FWD_Q_TILES = 4
SB_Q_TILES = 1
MLA_KEY_TILE = 512
VMEM_LIMIT = 56 * 1024 * 1024

MESH = pl.DeviceIdType.MESH


def _dot(a, b):
    return lax.dot_general(a, b, (((1,), (0,)), ((), ())), preferred_element_type=F32)


def _dot_nt(a, b):
    return lax.dot_general(a, b, (((1,), (1,)), ((), ())), preferred_element_type=F32)


def _dot_tn(a, b):
    return lax.dot_general(a, b, (((0,), (0,)), ((), ())), preferred_element_type=F32)


def _sigmoid(z):
    return 1.0 / (1.0 + jnp.exp2(z * (-LOG2E)))


def _params(sem=None):
    if sem is None:
        return pltpu.CompilerParams(vmem_limit_bytes=VMEM_LIMIT)
    return pltpu.CompilerParams(dimension_semantics=sem, vmem_limit_bytes=VMEM_LIMIT)


def _rows(tm, n):
    return pl.BlockSpec((tm, n), lambda i: (i, 0))


def _cols(n, tm):
    return pl.BlockSpec((n, tm), lambda i: (0, i))


def _whole(shape):
    nd = len(shape)
    return pl.BlockSpec(shape, lambda i: (0,) * nd)


def _sds(shape, dtype):
    return jax.ShapeDtypeStruct(shape, dtype)


def _flip(v, d):
    return 1 - v if d else v


def _weight_rows(a, b):
    if a >= O_KR:
        return True, slice(a - O_KR, b - O_KR)
    shift = ROPE_DIM if a >= O_MZ else 0
    return False, slice(a + shift, b + shift)


def _inproj_call(x, shift, scale, g1, w_in_t, w_kr, w_q, w_kv, qg, kvg, cos128, sin128):
    s_len = x.shape[0]
    tm = min(ROW_TILE, s_len)

    def body(x_ref, sh_ref, sc_ref, g1_ref, win_ref, wkr_ref, wq_ref, wkv_ref, qg_ref, kvg_ref, cos_ref, sin_ref,
             h_ref, sq_ref, sk_ref, sv_ref, sz_ref, cq_ref, ckv_ref, mz_ref, ga_ref, gb_ref, kpt_ref,
             qn_ref, qp_ref, kn_ref, vv_ref):
        xt = x_ref[...]
        r = lax.rsqrt(jnp.mean(xt * xt, axis=-1, keepdims=True) + EPS)
        h = (xt * r * g1_ref[...]) * (1.0 + sc_ref[...]) + sh_ref[...]
        hb = h.astype(BF16)
        h_ref[...] = hb

        def seg(a, b):
            from_kr, rows = _weight_rows(a, b)
            w_ref = wkr_ref if from_kr else win_ref
            return _dot_nt(hb[:, :HALF_D], w_ref[0, rows, :]) + _dot_nt(hb[:, HALF_D:], w_ref[1, rows, :])

        sq_ref[...] = (seg(O_SQ, O_SK) * SB_SCALE).astype(BF16)
        sk_ref[...] = seg(O_SK, O_SV).astype(BF16)
        sv_ref[...] = seg(O_SV, O_SZ).astype(BF16)
        sz_ref[...] = seg(O_SZ, O_CQ)
        mz_ref[...] = seg(O_MZ, O_GA)
        ga_ref[...] = seg(O_GA, O_GB)
        gb_ref[...] = seg(O_GB, O_KR)
        cos = cos_ref[...]
        sin = sin_ref[...]
        kr = seg(O_KR, O_END)
        kpt_ref[...] = (kr[:, :128] * cos + kr[:, 128:] * sin).astype(BF16)

        cq = seg(O_CQ, O_CKV)
        cq_ref[...] = cq
        rq = lax.rsqrt(jnp.mean(cq * cq, axis=-1, keepdims=True) + EPS)
        cqn = (cq * rq * qg_ref[...]).astype(BF16)
        qa = _dot(cqn, wq_ref[...])
        qn_ref[...] = qa[:, :512].astype(BF16)
        qp_ref[...] = (qa[:, 512:768] * jnp.tile(cos, (1, 2)) + qa[:, 768:] * jnp.tile(sin, (1, 2))).astype(BF16)

        ckv = seg(O_CKV, O_MZ)
        ckv_ref[...] = ckv
        rk = lax.rsqrt(jnp.mean(ckv * ckv, axis=-1, keepdims=True) + EPS)
        ckvn = (ckv * rk * kvg_ref[...]).astype(BF16)
        kva = _dot(ckvn, wkv_ref[...])
        kn_ref[...] = kva[:, :512].astype(BF16)
        vv_ref[...] = kva[:, 512:].astype(BF16)

    outs = [
        (D_MODEL, BF16), (512, BF16), (512, BF16), (512, BF16), (512, F32), (Q_RANK, F32), (KV_RANK, F32),
        (512, F32), (D_MODEL, F32), (D_MODEL, F32), (128, BF16), (512, BF16), (256, BF16), (512, BF16), (512, BF16),
    ]
    return pl.pallas_call(
        body, name="inproj", grid=(s_len // tm,),
        out_shape=tuple(_sds((s_len, n), dt) for n, dt in outs),
        in_specs=[_rows(tm, D_MODEL), _whole((1, D_MODEL)), _whole((1, D_MODEL)), _whole((1, D_MODEL)),
                  _whole((2, IN_WIDTH, HALF_D)), _whole((2, O_END - O_KR, HALF_D)),
                  _whole((Q_RANK, 1024)), _whole((KV_RANK, 1024)),
                  _whole((1, Q_RANK)), _whole((1, KV_RANK)), _rows(tm, LANES), _rows(tm, LANES)],
        out_specs=tuple(_rows(tm, n) for n, _ in outs),
        compiler_params=_params(("parallel",)),
    )(x, shift, scale, g1, w_in_t, w_kr, w_q, w_kv, qg, kvg, cos128, sin128)


Z_CLAMP = 80.0 * LOG2E
RUN_CUTOFF = 110.0 * LOG2E


def _softplus_clamped(z):
    zc = jnp.minimum(z * LOG2E, Z_CLAMP)
    return zc, jnp.log2(1.0 + jnp.exp2(zc))


def _tri_sum(a, tri):
    return _dot(a.astype(BF16), tri)


def _sb_fwd_call(q, k, v):
    s_len = q.shape[0]
    tk = min(ATT_TILE, s_len)
    tq = min(SB_Q_TILES * ATT_TILE, s_len)
    r = tq // tk
    nq = s_len // tq

    def body(q_ref, k_ref, v_ref, o_ref, lt_ref, first_ref):
        i = pl.program_id(1)
        q2 = q_ref[...]
        lane = lax.broadcasted_iota(jnp.int32, (1, 256), 1)
        krow = lax.broadcasted_iota(jnp.int32, (tk, tk), 0)
        kcol = lax.broadcasted_iota(jnp.int32, (tk, tk), 1)
        row = lax.broadcasted_iota(jnp.int32, (tq, tk), 0)
        col = lax.broadcasted_iota(jnp.int32, (tq, tk), 1)
        later = (krow > kcol).astype(BF16)
        valids = [col + u * tk < row for u in range(r)]
        hms = [(lane // 64) == hh for hh in range(4)]
        qms = [jnp.where(hm, q2, jnp.zeros_like(q2)) for hm in hms]

        def block(j, carry, valid):
            runs, acc = list(carry[:4]), carry[4]
            off = pl.multiple_of(j * tk, tk)
            kb = k_ref[pl.ds(off, tk), :]
            vb = v_ref[pl.ds(off, tk), :]
            ws = []
            for hh in range(4):
                zc, sp = _softplus_clamped(_dot_nt(qms[hh], kb))
                lm = jnp.where(valid, sp, 0.0) if valid is not None else sp
                suf = _tri_sum(lm, later)
                w = jnp.exp2(zc - sp - suf - runs[hh])
                if valid is not None:
                    w = jnp.where(valid, w, 0.0)
                ws.append(w.astype(BF16))
                runs[hh] = runs[hh] + jnp.sum(lm, axis=1, keepdims=True)
            vstack = jnp.concatenate([jnp.where(hm, vb, jnp.zeros_like(vb)) for hm in hms], axis=0)
            acc = acc + _dot(jnp.concatenate(ws, axis=1), vstack)
            return (*runs, acc)

        zero = jnp.zeros((tq, 1), F32)
        carry = (zero, zero, zero, zero, jnp.zeros((tq, 256), F32))
        for u in reversed(range(r)):
            carry = block(i * r + u, carry, valids[u])

        def least_run(runs):
            return jnp.min(jnp.minimum(jnp.minimum(runs[0], runs[1]), jnp.minimum(runs[2], runs[3])))

        n_full = i * r

        def unfinished(state):
            return jnp.logical_and(state[0] < n_full, state[1] <= RUN_CUTOFF)

        def visit(state):
            cr = block(n_full - 1 - state[0], state[2:], None)
            return (state[0] + 1, least_run(cr[:4]), *cr)

        state = lax.while_loop(unfinished, visit, (jnp.int32(0), least_run(carry[:4]), *carry))
        carry = state[2:]
        first_ref[...] = jnp.full(first_ref.shape, n_full - state[0], jnp.int32)
        for hh in range(4):
            lt_ref[0, :, hh:hh + 1] = carry[hh]
        o_ref[...] = carry[4]

    return pl.pallas_call(
        body, name="sb_fwd", grid=(2, nq),
        out_shape=(_sds((s_len, SB_WIDTH), F32), _sds((2, s_len, 4), F32), _sds((2, nq, 8, 128), jnp.int32)),
        in_specs=[pl.BlockSpec((tq, 256), lambda g, i: (i, g)),
                  pl.BlockSpec((s_len, 256), lambda g, i: (0, g)),
                  pl.BlockSpec((s_len, 256), lambda g, i: (0, g))],
        out_specs=(pl.BlockSpec((tq, 256), lambda g, i: (i, g)),
                   pl.BlockSpec((1, tq, 4), lambda g, i: (g, i, 0)),
                   pl.BlockSpec((1, 1, 8, 128), lambda g, i: (g, i, 0, 0))),
        compiler_params=_params(("parallel", "parallel")),
    )(q, k, v)


def _sb_bwd_call(first, q, k, v, do, lt):
    s_len = q.shape[0]
    tk = min(ATT_TILE, s_len)
    tq = min(SB_Q_TILES * ATT_TILE, s_len)
    r = tq // tk
    nq = s_len // tq
    nq_fwd = first.shape[0] // 2
    per_fwd = nq // nq_fwd

    def body(first_ref, q_ref, k_ref, v_ref, do_ref, lt_ref, dq_ref, dk_ref, dv_ref):
        g = pl.program_id(0)
        i = pl.program_id(1)

        @pl.when(i == 0)
        def _():
            dk_ref[...] = jnp.zeros_like(dk_ref)
            dv_ref[...] = jnp.zeros_like(dv_ref)

        q2 = q_ref[...]
        do2 = do_ref[...].astype(BF16)
        lane = lax.broadcasted_iota(jnp.int32, (1, 256), 1)
        krow = lax.broadcasted_iota(jnp.int32, (tk, tk), 0)
        kcol = lax.broadcasted_iota(jnp.int32, (tk, tk), 1)
        row = lax.broadcasted_iota(jnp.int32, (tq, tk), 0)
        col = lax.broadcasted_iota(jnp.int32, (tq, tk), 1)
        earlier = (krow < kcol).astype(BF16)
        later = (krow > kcol).astype(BF16)
        valids = [col + u * tk < row for u in range(r)]
        hms = [(lane // 64) == hh for hh in range(4)]
        qms = [jnp.where(hm, q2, jnp.zeros_like(q2)) for hm in hms]
        doms = [jnp.where(hm, do2, jnp.zeros_like(do2)) for hm in hms]
        ltots = [lt_ref[0, :, hh:hh + 1] for hh in range(4)]
        q2t = jnp.transpose(q2.astype(F32))
        do2t = jnp.transpose(do_ref[...])
        subl = lax.broadcasted_iota(jnp.int32, (256, 1), 0)
        qtstack = jnp.concatenate(
            [jnp.where((subl // 64) == hh, q2t, 0.0).astype(BF16) for hh in range(4)], axis=1)
        dotstack = jnp.concatenate(
            [jnp.where((subl // 64) == hh, do2t, 0.0).astype(BF16) for hh in range(4)], axis=1)

        def block(j, carry, valid):
            lpre, ppre, dq = list(carry[0:4]), list(carry[4:8]), carry[8]
            off = pl.multiple_of(j * tk, tk)
            kb = k_ref[pl.ds(off, tk), :]
            vb = v_ref[pl.ds(off, tk), :]
            dzs, avs = [], []
            for hh in range(4):
                zc, sp = _softplus_clamped(_dot_nt(qms[hh], kb))
                lsig = zc - sp
                lm = jnp.where(valid, sp, 0.0) if valid is not None else sp
                rowsum = jnp.sum(lm, axis=1, keepdims=True)
                between = _tri_sum(lm, later) + ((ltots[hh] - lpre[hh]) - rowsum)
                a = jnp.exp2(lsig - between)
                if valid is not None:
                    a = jnp.where(valid, a, 0.0)
                p = a * _dot_nt(doms[hh], vb)
                pbefore = ppre[hh] + _tri_sum(p, earlier)
                dz = p - jnp.exp2(lsig) * (p + pbefore)
                if valid is not None:
                    dz = jnp.where(valid, dz, 0.0)
                dzs.append(dz.astype(BF16))
                avs.append(a.astype(BF16))
                lpre[hh] = lpre[hh] + rowsum
                ppre[hh] = ppre[hh] + jnp.sum(p, axis=1, keepdims=True)
            kstack = jnp.concatenate([jnp.where(hm, kb, jnp.zeros_like(kb)) for hm in hms], axis=0)
            dq = dq + _dot(jnp.concatenate(dzs, axis=1), kstack)
            dk_ref[:, pl.ds(off, tk)] += _dot(qtstack, jnp.concatenate(dzs, axis=0))
            dv_ref[:, pl.ds(off, tk)] += _dot(dotstack, jnp.concatenate(avs, axis=0))
            return (*lpre, *ppre, dq)

        zero = jnp.zeros((tq, 1), F32)
        start = jnp.minimum(first_ref[g * nq_fwd + i // per_fwd], i * r)
        carry = lax.fori_loop(start, i * r, lambda j, cr: block(j, cr, None),
                              (zero,) * 8 + (jnp.zeros((tq, 256), F32),))
        for u in range(r):
            carry = block(i * r + u, carry, valids[u])
        dq_ref[...] = carry[8].astype(BF16)

    return pl.pallas_call(
        body, name="sb_bwd",
        out_shape=(_sds((s_len, SB_WIDTH), BF16), _sds((SB_WIDTH, s_len), F32), _sds((SB_WIDTH, s_len), F32)),
        grid_spec=pltpu.PrefetchScalarGridSpec(
            num_scalar_prefetch=1, grid=(2, nq),
            in_specs=[pl.BlockSpec((tq, 256), lambda g, i, f: (i, g)),
                      pl.BlockSpec((s_len, 256), lambda g, i, f: (0, g)),
                      pl.BlockSpec((s_len, 256), lambda g, i, f: (0, g)),
                      pl.BlockSpec((tq, 256), lambda g, i, f: (i, g)),
                      pl.BlockSpec((1, tq, 4), lambda g, i, f: (g, i, 0))],
            out_specs=(pl.BlockSpec((tq, 256), lambda g, i, f: (i, g)),
                       pl.BlockSpec((256, s_len), lambda g, i, f: (g, 0)),
                       pl.BlockSpec((256, s_len), lambda g, i, f: (g, 0)))),
        compiler_params=_params(("parallel", "arbitrary")),
    )(first, q, k, v, do, lt)


def _mla_fwd_call(qn, qp, kn, kpt, v):
    s_len = qn.shape[0]
    tk = min(MLA_KEY_TILE, s_len)
    tq = min(FWD_Q_TILES * ATT_TILE, s_len)
    r = tq // tk
    nq = s_len // tq

    def body(qn_ref, qp_ref, kn_ref, kpt_ref, v_ref, o_ref, lse_ref):
        i = pl.program_id(1)
        qn2 = qn_ref[...]
        qp2 = qp_ref[...]
        lane256 = lax.broadcasted_iota(jnp.int32, (1, 256), 1)
        lane128 = lax.broadcasted_iota(jnp.int32, (1, 128), 1)
        krow = lax.broadcasted_iota(jnp.int32, (tk, tk), 0)
        kcol = lax.broadcasted_iota(jnp.int32, (tk, tk), 1)
        row = lax.broadcasted_iota(jnp.int32, (tq, tk), 0)
        col = lax.broadcasted_iota(jnp.int32, (tq, tk), 1)
        valids = [col + u * tk <= row for u in range(r)]
        m64s = [(lane256 // 64) == hh for hh in range(4)]
        half = [(lane128 // 64) == u for u in range(2)]
        m32s = [(lane128 // 32) == hh for hh in range(4)]
        qcs = []
        for hh in range(4):
            qpair = qn2[:, 128 * (hh // 2):128 * (hh // 2) + 128]
            qcs.append(jnp.concatenate([jnp.where(half[hh % 2], qpair, jnp.zeros_like(qpair)),
                                        jnp.where(m32s[hh], qp2, jnp.zeros_like(qp2))], axis=1))

        def by_head(vals):
            return jnp.where(m64s[0], vals[0], jnp.where(m64s[1], vals[1], jnp.where(m64s[2], vals[2], vals[3])))

        def block(j, carry, valid):
            ms, ls, acc = list(carry[0:4]), list(carry[4:8]), carry[8]
            off = pl.multiple_of(j * tk, tk)
            knb = kn_ref[pl.ds(off, tk), :]
            kpb = kpt_ref[pl.ds(off, tk), :]
            vb = v_ref[pl.ds(off, tk), :]
            kcs = [jnp.concatenate([knb[:, 128 * pp:128 * pp + 128], kpb], axis=1) for pp in range(2)]
            ps, alphas = [], []
            for hh in range(4):
                s = _dot_nt(qcs[hh], kcs[hh // 2]) * (MLA_SCALE * LOG2E)
                if valid is not None:
                    s = jnp.where(valid, s, -1e30)
                mn = jnp.maximum(ms[hh], jnp.max(s, axis=1, keepdims=True))
                p = jnp.exp2(s - mn)
                alpha = jnp.exp2(ms[hh] - mn)
                ls[hh] = alpha * ls[hh] + jnp.sum(p, axis=1, keepdims=True)
                ms[hh] = mn
                ps.append(p.astype(BF16))
                alphas.append(alpha)
            pvs = []
            for pp in range(2):
                vpair = vb[:, 128 * pp:128 * pp + 128]
                vstack = jnp.concatenate([jnp.where(hf, vpair, jnp.zeros_like(vpair)) for hf in half], axis=0)
                pvs.append(_dot(jnp.concatenate(ps[2 * pp:2 * pp + 2], axis=1), vstack))
            acc = by_head(alphas) * acc + jnp.concatenate(pvs, axis=1)
            return (*ms, *ls, acc)

        neg = jnp.full((tq, 1), -1e30, F32)
        zero = jnp.zeros((tq, 1), F32)
        carry = lax.fori_loop(0, i * r, lambda j, cr: block(j, cr, None),
                              (neg,) * 4 + (zero,) * 4 + (jnp.zeros((tq, 256), F32),))
        for u in range(r):
            carry = block(i * r + u, carry, valids[u])
        o_ref[...] = carry[8] / by_head(list(carry[4:8]))
        for hh in range(4):
            lse_ref[0, :, hh:hh + 1] = (carry[hh] + jnp.log2(carry[4 + hh])) * (1.0 / LOG2E)

    return pl.pallas_call(
        body, name="mla_fwd", grid=(2, nq),
        out_shape=(_sds((s_len, MLA_WIDTH), F32), _sds((2, s_len, 4), F32)),
        in_specs=[pl.BlockSpec((tq, 256), lambda g, i: (i, g)),
                  pl.BlockSpec((tq, 128), lambda g, i: (i, g)),
                  pl.BlockSpec((s_len, 256), lambda g, i: (0, g)),
                  pl.BlockSpec((s_len, 128), lambda g, i: (0, 0)),
                  pl.BlockSpec((s_len, 256), lambda g, i: (0, g))],
        out_specs=(pl.BlockSpec((tq, 256), lambda g, i: (i, g)),
                   pl.BlockSpec((1, tq, 4), lambda g, i: (g, i, 0))),
        compiler_params=_params(("parallel", "parallel")),
    )(qn, qp, kn, kpt, v)


def _mla_bwd_call(qn, qp, kn, kpt, v, o, do, lse):
    s_len = qn.shape[0]
    tk = min(MLA_KEY_TILE, s_len)
    tq = min(ATT_Q_TILES * ATT_TILE, s_len)
    r = tq // tk
    nq = s_len // tq

    def body(qn_ref, qp_ref, kn_ref, kpt_ref, v_ref, o_ref, do_ref, lse_ref,
             dqn_ref, dqp_ref, dkn_ref, dkpt_ref, dv_ref):
        g = pl.program_id(0)
        i = pl.program_id(1)

        @pl.when(i == 0)
        def _():
            dkn_ref[...] = jnp.zeros_like(dkn_ref)
            dv_ref[...] = jnp.zeros_like(dv_ref)

        @pl.when((i == 0) & (g == 0))
        def _():
            dkpt_ref[...] = jnp.zeros_like(dkpt_ref)

        qn2 = qn_ref[...]
        qp2 = qp_ref[...]
        dof = do_ref[...]
        dob = dof.astype(BF16)
        prod = dof * o_ref[...]
        lane256 = lax.broadcasted_iota(jnp.int32, (1, 256), 1)
        lane128 = lax.broadcasted_iota(jnp.int32, (1, 128), 1)
        krow = lax.broadcasted_iota(jnp.int32, (tk, tk), 0)
        kcol = lax.broadcasted_iota(jnp.int32, (tk, tk), 1)
        row = lax.broadcasted_iota(jnp.int32, (tq, tk), 0)
        col = lax.broadcasted_iota(jnp.int32, (tq, tk), 1)
        valids = [col + u * tk <= row for u in range(r)]
        m64s = [(lane256 // 64) == hh for hh in range(4)]
        half = [(lane128 // 64) == u for u in range(2)]
        m32s = [(lane128 // 32) == hh for hh in range(4)]
        qcs, doms = [], []
        for hh in range(4):
            sl = slice(128 * (hh // 2), 128 * (hh // 2) + 128)
            qpair = qn2[:, sl]
            dpair = dob[:, sl]
            qcs.append(jnp.concatenate([jnp.where(half[hh % 2], qpair, jnp.zeros_like(qpair)),
                                        jnp.where(m32s[hh], qp2, jnp.zeros_like(qp2))], axis=1))
            doms.append(jnp.where(half[hh % 2], dpair, jnp.zeros_like(dpair)))
        dsums = [jnp.sum(jnp.where(m64, prod, 0.0), axis=1, keepdims=True) * MLA_SCALE for m64 in m64s]
        lses = [lse_ref[0, :, hh:hh + 1] * LOG2E for hh in range(4)]
        qn2t = jnp.transpose(qn2.astype(F32))
        qp2t = jnp.transpose(qp2.astype(F32))
        do2t = jnp.transpose(dof)
        sub128 = lax.broadcasted_iota(jnp.int32, (128, 1), 0)
        qtstacks, dotstacks = [], []
        for pp in range(2):
            qts, dts = [], []
            for u in range(2):
                hh = 2 * pp + u
                qts.append(jnp.concatenate(
                    [jnp.where((sub128 // 64) == u, qn2t[128 * pp:128 * pp + 128, :], 0.0),
                     jnp.where((sub128 // 32) == hh, qp2t, 0.0)], axis=0).astype(BF16))
                dts.append(jnp.where((sub128 // 64) == u, do2t[128 * pp:128 * pp + 128, :], 0.0).astype(BF16))
            qtstacks.append(jnp.concatenate(qts, axis=1))
            dotstacks.append(jnp.concatenate(dts, axis=1))

        def block(j, carry, valid):
            dqn, dqp = carry
            off = pl.multiple_of(j * tk, tk)
            knb = kn_ref[pl.ds(off, tk), :]
            kpb = kpt_ref[pl.ds(off, tk), :]
            vb = v_ref[pl.ds(off, tk), :]
            dqn_parts = []
            dkp = None
            for pp in range(2):
                sl = slice(128 * pp, 128 * pp + 128)
                knp = knb[:, sl]
                vpair = vb[:, sl]
                kc = jnp.concatenate([knp, kpb], axis=1)
                dss, pbs, kcms = [], [], []
                for u in range(2):
                    hh = 2 * pp + u
                    s = _dot_nt(qcs[hh], kc) * (MLA_SCALE * LOG2E)
                    if valid is not None:
                        s = jnp.where(valid, s, -1e30)
                    p = jnp.exp2(s - lses[hh])
                    ds = p * (_dot_nt(doms[hh], vpair) * MLA_SCALE - dsums[hh])
                    dss.append(ds.astype(BF16))
                    pbs.append(p.astype(BF16))
                    kcms.append(jnp.concatenate([jnp.where(half[u], knp, jnp.zeros_like(knp)),
                                                 jnp.where(m32s[hh], kpb, jnp.zeros_like(kpb))], axis=1))
                dqc = _dot(jnp.concatenate(dss, axis=1), jnp.concatenate(kcms, axis=0))
                dqn_parts.append(dqc[:, :128])
                dqp = dqp + dqc[:, 128:]
                dkc = _dot(qtstacks[pp], jnp.concatenate(dss, axis=0))
                dkn_ref[128 * pp:128 * pp + 128, pl.ds(off, tk)] += dkc[:128, :]
                dkp = dkc[128:, :] if dkp is None else dkp + dkc[128:, :]
                dv_ref[128 * pp:128 * pp + 128, pl.ds(off, tk)] += _dot(dotstacks[pp], jnp.concatenate(pbs, axis=0))
            dqn = dqn + jnp.concatenate(dqn_parts, axis=1)
            dkpt_ref[:, pl.ds(off, tk)] += dkp
            return dqn, dqp

        carry = lax.fori_loop(0, i * r, lambda j, cr: block(j, cr, None),
                              (jnp.zeros((tq, 256), F32), jnp.zeros((tq, 128), F32)))
        for u in range(r):
            carry = block(i * r + u, carry, valids[u])
        dqn, dqp = carry
        dqn_ref[...] = dqn.astype(BF16)
        dqp_ref[...] = dqp.astype(BF16)

    return pl.pallas_call(
        body, name="mla_bwd", grid=(2, nq),
        out_shape=(_sds((s_len, 512), BF16), _sds((s_len, 256), BF16), _sds((512, s_len), F32),
                   _sds((128, s_len), F32), _sds((512, s_len), F32)),
        in_specs=[pl.BlockSpec((tq, 256), lambda g, i: (i, g)),
                  pl.BlockSpec((tq, 128), lambda g, i: (i, g)),
                  pl.BlockSpec((s_len, 256), lambda g, i: (0, g)),
                  pl.BlockSpec((s_len, 128), lambda g, i: (0, 0)),
                  pl.BlockSpec((s_len, 256), lambda g, i: (0, g)),
                  pl.BlockSpec((tq, 256), lambda g, i: (i, g)),
                  pl.BlockSpec((tq, 256), lambda g, i: (i, g)),
                  pl.BlockSpec((1, tq, 4), lambda g, i: (g, i, 0))],
        out_specs=(pl.BlockSpec((tq, 256), lambda g, i: (i, g)),
                   pl.BlockSpec((tq, 128), lambda g, i: (i, g)),
                   pl.BlockSpec((256, s_len), lambda g, i: (g, 0)),
                   pl.BlockSpec((128, s_len), lambda g, i: (0, 0)),
                   pl.BlockSpec((256, s_len), lambda g, i: (g, 0))),
        compiler_params=_params(("arbitrary", "arbitrary")),
    )(qn, qp, kn, kpt, v, o, do, lse)


def _post_call(x, tgt, oa, ob, sz, mz, ga, gb, gate, gf, wa, wb, wo):
    s_len = x.shape[0]
    tm = min(ROW_TILE, s_len)
    nstep = s_len // tm

    def body(x_ref, t_ref, oa_ref, ob_ref, sz_ref, mz_ref, ga_ref, gb_ref, gate_ref, gf_ref,
             wa_ref, wb_ref, wo_ref,
             dx2_ref, doa_ref, dob_ref, dsz_ref, dmz_ref, dga_ref, dgb_ref,
             dwo_out, dwa_out, dwb_out, dgf_ref, dgate_ref, loss_ref, dwo_ref, dwa_ref, dwb_ref):
        @pl.when(pl.program_id(0) == 0)
        def _():
            dwo_ref[...] = jnp.zeros_like(dwo_ref)
            dwa_ref[...] = jnp.zeros_like(dwa_ref)
            dwb_ref[...] = jnp.zeros_like(dwb_ref)
            dgf_ref[...] = jnp.zeros_like(dgf_ref)
            dgate_ref[...] = jnp.zeros_like(dgate_ref)
            loss_ref[...] = jnp.zeros_like(loss_ref)

        gate = gate_ref[...]
        gf = gf_ref[...]
        oa = oa_ref[...]
        ob = ob_ref[...]
        sz = sz_ref[...]
        mz = mz_ref[...]
        sa = _sigmoid(sz)
        sb = _sigmoid(mz)
        silu_a = sz * sa
        silu_b = mz * sb
        ua = (oa * silu_a).astype(BF16)
        ub = (ob * silu_b).astype(BF16)
        ya = _dot(ua, wa_ref[...])
        yb = _dot(ub, wb_ref[...])
        sga = _sigmoid(ga_ref[...])
        sgb = _sigmoid(gb_ref[...])
        merged = (sga * ya + sgb * yb).astype(BF16)
        out = _dot(merged, wo_ref[...])
        x2 = x_ref[...] + gate * out
        r2 = lax.rsqrt(jnp.mean(x2 * x2, axis=-1, keepdims=True) + EPS)
        xhat = x2 * r2
        err = xhat * gf - t_ref[...]
        loss_ref[...] += 0.5 * jnp.sum(jnp.sum(err * err, axis=1, keepdims=True), axis=0, keepdims=True) / D_MODEL
        dy = err * (1.0 / D_MODEL)
        dgf_ref[...] += jnp.sum(dy * xhat, axis=0, keepdims=True)
        dxhat = dy * gf
        dx2 = r2 * (dxhat - xhat * jnp.mean(dxhat * xhat, axis=-1, keepdims=True))
        dx2_ref[...] = dx2
        dgate_ref[...] += jnp.sum(dx2 * out, axis=0, keepdims=True)
        dout = (dx2 * gate).astype(BF16)
        dmerged = _dot_nt(dout, wo_ref[...])
        dwo_ref[...] += _dot_tn(merged, dout)
        dya = dmerged * sga
        dyb = dmerged * sgb
        dga_ref[...] = (dya * ya * (1.0 - sga)).astype(BF16)
        dgb_ref[...] = (dyb * yb * (1.0 - sgb)).astype(BF16)
        dyab = dya.astype(BF16)
        dybb = dyb.astype(BF16)
        dua = _dot_nt(dyab, wa_ref[...])
        dub = _dot_nt(dybb, wb_ref[...])
        dwa_ref[...] += _dot_tn(ua, dyab)
        dwb_ref[...] += _dot_tn(ub, dybb)
        doa_ref[...] = dua * silu_a
        dob_ref[...] = dub * silu_b
        dsz_ref[...] = (dua * oa * (sa * (1.0 + sz * (1.0 - sa)))).astype(BF16)
        dmz_ref[...] = (dub * ob * (sb * (1.0 + mz * (1.0 - sb)))).astype(BF16)

        @pl.when(pl.program_id(0) == nstep - 1)
        def _():
            dwo_out[...] = dwo_ref[...].astype(BF16)
            for k in range(N_CHIPS):
                dwa_out[k] = dwa_ref[:, 256 * k:256 * k + 256].astype(BF16)
                dwb_out[k] = dwb_ref[:, 256 * k:256 * k + 256].astype(BF16)

    return pl.pallas_call(
        body, name="post", grid=(nstep,),
        out_shape=(_sds((s_len, D_MODEL), F32), _sds((s_len, 512), F32), _sds((s_len, 512), F32),
                   _sds((s_len, 512), BF16), _sds((s_len, 512), BF16),
                   _sds((s_len, D_MODEL), BF16), _sds((s_len, D_MODEL), BF16),
                   _sds((D_MODEL, D_MODEL), BF16), _sds((N_CHIPS, 512, 256), BF16), _sds((N_CHIPS, 512, 256), BF16),
                   _sds((1, D_MODEL), F32), _sds((1, D_MODEL), F32), _sds((1, 128), F32)),
        in_specs=[_rows(tm, D_MODEL), _rows(tm, D_MODEL), _rows(tm, 512), _rows(tm, 512), _rows(tm, 512),
                  _rows(tm, 512), _rows(tm, D_MODEL), _rows(tm, D_MODEL), _whole((1, D_MODEL)), _whole((1, D_MODEL)),
                  _whole((512, D_MODEL)), _whole((512, D_MODEL)), _whole((D_MODEL, D_MODEL))],
        out_specs=(_rows(tm, D_MODEL), _rows(tm, 512), _rows(tm, 512), _rows(tm, 512), _rows(tm, 512),
                   _rows(tm, D_MODEL), _rows(tm, D_MODEL),
                   _whole((D_MODEL, D_MODEL)), _whole((N_CHIPS, 512, 256)), _whole((N_CHIPS, 512, 256)),
                   _whole((1, D_MODEL)), _whole((1, D_MODEL)), _whole((1, 128))),
        scratch_shapes=[pltpu.VMEM((D_MODEL, D_MODEL), F32), pltpu.VMEM((512, D_MODEL), F32),
                        pltpu.VMEM((512, D_MODEL), F32)],
        compiler_params=_params(("arbitrary",)),
    )(x, tgt, oa, ob, sz, mz, ga, gb, gate, gf, wa, wb, wo)


def _bwdprep_call(dsq, dsk, dsv, dsz, dqn, dqp, dkn, dvv, dkpt, dmz, dga, dgb, cq, ckv, cos128, sin128,
                  qg, kvg, w_q, w_kv):
    s_len = cq.shape[0]
    tm = min(ROW_TILE, s_len)

    def body(dsq_ref, dsk_ref, dsv_ref, dsz_ref, dqn_ref, dqp_ref, dkn_ref, dvv_ref, dkpt_ref, dmz_ref,
             dga_ref, dgb_ref, cq_ref, ckv_ref, cos_ref, sin_ref, qg_ref, kvg_ref, wq_ref, wkv_ref,
             dp_ref, dwq_ref, dwkv_ref, dqg_ref, dkvg_ref):
        @pl.when(pl.program_id(0) == 0)
        def _():
            dwq_ref[...] = jnp.zeros_like(dwq_ref)
            dwkv_ref[...] = jnp.zeros_like(dwkv_ref)
            dqg_ref[...] = jnp.zeros_like(dqg_ref)
            dkvg_ref[...] = jnp.zeros_like(dkvg_ref)

        cos = cos_ref[...]
        sin = sin_ref[...]
        dp_ref[:, O_SQ:O_SK] = dsq_ref[...] * jnp.asarray(SB_SCALE, BF16)
        dp_ref[:, O_SK:O_SV] = jnp.transpose(dsk_ref[...]).astype(BF16)
        dp_ref[:, O_SV:O_SZ] = jnp.transpose(dsv_ref[...]).astype(BF16)
        dp_ref[:, O_SZ:O_CQ] = dsz_ref[...]
        dp_ref[:, O_MZ:O_GA] = dmz_ref[...]
        dp_ref[:, O_GA:O_GB] = dga_ref[...]
        dp_ref[:, O_GB:O_KR] = dgb_ref[...]
        dkp = jnp.transpose(dkpt_ref[...])
        dp_ref[:, O_KR:O_KR + 128] = (dkp * cos).astype(BF16)
        dp_ref[:, O_KR + 128:O_END] = (dkp * sin).astype(BF16)
        dp_ref[:, O_END:W_INT] = jnp.zeros((tm, W_INT - O_END), BF16)

        cq = cq_ref[...]
        rq = lax.rsqrt(jnp.mean(cq * cq, axis=-1, keepdims=True) + EPS)
        cqh = cq * rq
        qg = qg_ref[...]
        cqn = (cqh * qg).astype(BF16)
        dqp = dqp_ref[...].astype(F32)
        dqa = jnp.concatenate([dqn_ref[...], (dqp * jnp.tile(cos, (1, 2))).astype(BF16),
                               (dqp * jnp.tile(sin, (1, 2))).astype(BF16)], axis=1)
        dcqn = _dot_nt(dqa, wq_ref[...])
        dwq_ref[...] += _dot_tn(cqn, dqa)
        dqg_ref[...] += jnp.sum(dcqn * cqh, axis=0, keepdims=True)
        dh = dcqn * qg
        dcq = rq * (dh - cqh * jnp.mean(dh * cqh, axis=-1, keepdims=True))
        dp_ref[:, O_CQ:O_CKV] = dcq.astype(BF16)

        ckv = ckv_ref[...]
        rk = lax.rsqrt(jnp.mean(ckv * ckv, axis=-1, keepdims=True) + EPS)
        ckh = ckv * rk
        kvg = kvg_ref[...]
        ckvn = (ckh * kvg).astype(BF16)
        dkva = jnp.concatenate([jnp.transpose(dkn_ref[...]).astype(BF16),
                                jnp.transpose(dvv_ref[...]).astype(BF16)], axis=1)
        dckvn = _dot_nt(dkva, wkv_ref[...])
        dwkv_ref[...] += _dot_tn(ckvn, dkva)
        dkvg_ref[...] += jnp.sum(dckvn * ckh, axis=0, keepdims=True)
        dh2 = dckvn * kvg
        dckv = rk * (dh2 - ckh * jnp.mean(dh2 * ckh, axis=-1, keepdims=True))
        dp_ref[:, O_CKV:O_MZ] = dckv.astype(BF16)

    return pl.pallas_call(
        body, name="bwdprep", grid=(s_len // tm,),
        out_shape=(_sds((s_len, W_INT), BF16), _sds((Q_RANK, 1024), F32), _sds((KV_RANK, 1024), F32),
                   _sds((1, Q_RANK), F32), _sds((1, KV_RANK), F32)),
        in_specs=[_rows(tm, 512), _cols(512, tm), _cols(512, tm), _rows(tm, 512), _rows(tm, 512), _rows(tm, 256),
                  _cols(512, tm), _cols(512, tm), _cols(128, tm), _rows(tm, 512), _rows(tm, D_MODEL),
                  _rows(tm, D_MODEL), _rows(tm, Q_RANK), _rows(tm, KV_RANK), _rows(tm, LANES), _rows(tm, LANES),
                  _whole((1, Q_RANK)), _whole((1, KV_RANK)), _whole((Q_RANK, 1024)), _whole((KV_RANK, 1024))],
        out_specs=(_rows(tm, W_INT), _whole((Q_RANK, 1024)), _whole((KV_RANK, 1024)),
                   _whole((1, Q_RANK)), _whole((1, KV_RANK))),
        compiler_params=_params(("arbitrary",)),
    )(dsq, dsk, dsv, dsz, dqn, dqp, dkn, dvv, dkpt, dmz, dga, dgb, cq, ckv, cos128, sin128, qg, kvg, w_q, w_kv)


def _dh_call(dproj, w_in_t, w_kr, x, dx2, scale, g1):
    s_len = x.shape[0]
    tm = min(2 * ROW_TILE, s_len)
    parts = [(a, b) + _weight_rows(a, b) for a, b in ((0, O_MZ), (O_MZ, O_KR), (O_KR, O_END))]

    def body(dp_ref, win_ref, wkr_ref, x_ref, dx2_ref, sc_ref, g1_ref, gx_ref, dsh_ref, dsc_ref, dg1_ref):
        @pl.when(pl.program_id(0) == 0)
        def _():
            dsh_ref[...] = jnp.zeros_like(dsh_ref)
            dsc_ref[...] = jnp.zeros_like(dsc_ref)
            dg1_ref[...] = jnp.zeros_like(dg1_ref)

        def half(k):
            return sum(_dot(dp_ref[:, a:b], (wkr_ref if from_kr else win_ref)[k, rows, :])
                       for a, b, from_kr, rows in parts)

        dh = jnp.concatenate([half(0), half(1)], axis=1)
        xt = x_ref[...]
        r = lax.rsqrt(jnp.mean(xt * xt, axis=-1, keepdims=True) + EPS)
        xh = xt * r
        g1 = g1_ref[...]
        xg = xh * g1
        dsh_ref[...] += jnp.sum(dh, axis=0, keepdims=True)
        dsc_ref[...] += jnp.sum(dh * xg, axis=0, keepdims=True)
        dxg = dh * (1.0 + sc_ref[...])
        dg1_ref[...] += jnp.sum(dxg * xh, axis=0, keepdims=True)
        dxh = dxg * g1
        gx_ref[...] = dx2_ref[...] + r * (dxh - xh * jnp.mean(dxh * xh, axis=-1, keepdims=True))

    return pl.pallas_call(
        body, name="dh", grid=(s_len // tm,),
        out_shape=(_sds((s_len, D_MODEL), F32), _sds((1, D_MODEL), F32), _sds((1, D_MODEL), F32),
                   _sds((1, D_MODEL), F32)),
        in_specs=[_rows(tm, W_INT), _whole((2, IN_WIDTH, HALF_D)), _whole((2, O_END - O_KR, HALF_D)),
                  _rows(tm, D_MODEL), _rows(tm, D_MODEL), _whole((1, D_MODEL)), _whole((1, D_MODEL))],
        out_specs=(_rows(tm, D_MODEL), _whole((1, D_MODEL)), _whole((1, D_MODEL)), _whole((1, D_MODEL))),
        compiler_params=_params(("arbitrary",)),
    )(dproj, w_in_t, w_kr, x, dx2, scale, g1)


def _small_call(svg, ct, dmod_sh):
    def body(sv_ref, ct_ref, dm_ref, tot_ref, gwada_ref):
        acc = sv_ref[0:1, :]
        for d in range(1, N_DEV):
            acc = acc + sv_ref[d:d + 1, :]
        tot_ref[...] = acc
        gwada_ref[...] = lax.dot_general(ct_ref[...], dm_ref[...], (((1,), (0,)), ((), ())),
                                         precision=lax.Precision.HIGHEST, preferred_element_type=F32)

    vmem = pl.BlockSpec(memory_space=pltpu.VMEM)
    return pl.pallas_call(
        body, name="small_grads",
        out_shape=(_sds((1, 8 * SV_COLS), F32), _sds((D_MODEL, 768), F32)),
        in_specs=[vmem, vmem, vmem], out_specs=(vmem, vmem),
        compiler_params=_params(),
    )(svg, ct, dmod_sh)


def _adamw_tile_rows(rows, cols):
    budget = 2 << 20
    if rows * cols * 4 <= budget or rows % 8:
        return rows
    best = 8
    for tr in range(8, rows + 1, 8):
        if rows % tr == 0 and tr * cols * 4 <= budget:
            best = tr
    return best


def _adamw_math(w, gg, m, v):
    m2 = ADAM_B1 * m + (1.0 - ADAM_B1) * gg
    v2 = ADAM_B2 * v + (1.0 - ADAM_B2) * (gg * gg)
    m_hat = m2 / (1.0 - ADAM_B1 ** ADAM_STEP)
    v_hat = v2 / (1.0 - ADAM_B2 ** ADAM_STEP)
    return -ADAM_LR * (m_hat / (jnp.sqrt(v_hat) + ADAM_EPS) + ADAM_WD * w), m2, v2


def _adamw_call(name, w, g, m, v):
    rows, cols = w.shape
    tr = _adamw_tile_rows(rows, cols)
    halves = g.ndim == 3

    def body(w_ref, g_ref, m_ref, v_ref, *out_refs):
        if halves:
            gg = jnp.concatenate([g_ref[0], g_ref[1]], axis=1)
            out_refs[0][...] = gg
        else:
            gg = g_ref[...]
        d_ref, nm_ref, nv_ref = out_refs[-3:]
        d_ref[...], nm_ref[...], nv_ref[...] = _adamw_math(w_ref[...], gg, m_ref[...], v_ref[...])

    spec = pl.BlockSpec((tr, cols), lambda i: (i, 0))
    g_spec = pl.BlockSpec((2, tr, cols // 2), lambda i: (0, i, 0)) if halves else spec
    n_out = 4 if halves else 3
    outs = pl.pallas_call(
        body, name="adamw_" + name, grid=(rows // tr,),
        out_shape=(_sds((rows, cols), F32),) * n_out,
        in_specs=[spec, g_spec, spec, spec], out_specs=(spec,) * n_out,
        compiler_params=_params(("parallel",)),
    )(w, g, m, v)
    return tuple(outs) if halves else (g,) + tuple(outs)


def _pack_vectors_call(vectors):
    used = sum(v.shape[1] for v in vectors)

    def body(*refs):
        out_ref = refs[-1]
        row = jnp.concatenate([r[...] for r in refs[:-1]] + [jnp.zeros((1, 8 * SV_COLS - used), F32)], axis=1)
        for r in range(8):
            out_ref[r:r + 1, :] = row[:, SV_COLS * r:SV_COLS * (r + 1)]

    vmem = pl.BlockSpec(memory_space=pltpu.VMEM)
    return pl.pallas_call(
        body, name="pack_vectors", out_shape=_sds((8, SV_COLS), F32),
        in_specs=[vmem] * len(vectors), out_specs=vmem, compiler_params=_params(),
    )(*vectors)


def _adamw_vectors_call(tot, offsets, ws, ms, vs):
    nvec = len(ws)

    def body(tot_ref, *refs):
        ins, outs = refs[:3 * nvec], refs[3 * nvec:]
        for k in range(nvec):
            w_ref, m_ref, v_ref = ins[3 * k:3 * k + 3]
            g_ref, d_ref, nm_ref, nv_ref = outs[4 * k:4 * k + 4]
            gg = tot_ref[:, offsets[k]:offsets[k] + w_ref.shape[1]]
            g_ref[...] = gg
            d_ref[...], nm_ref[...], nv_ref[...] = _adamw_math(w_ref[...], gg, m_ref[...], v_ref[...])

    vmem = pl.BlockSpec(memory_space=pltpu.VMEM)
    outs = pl.pallas_call(
        body, name="adamw_vectors",
        out_shape=tuple(_sds(w.shape, F32) for w in ws for _ in range(4)),
        in_specs=[vmem] * (1 + 3 * nvec), out_specs=(vmem,) * (4 * nvec),
        compiler_params=_params(),
    )(tot, *[a for w, m, v in zip(ws, ms, vs) for a in (w, m, v)])
    return [tuple(outs[4 * k:4 * k + 4]) for k in range(nvec)]


IN_SHARD = IN_WIDTH // N_CHIPS
HALF_D = D_MODEL // 2
SMALL_ROWS = (576, 512, 1024, 1024, 2048)
SMALL_TOTAL = sum(SMALL_ROWS)
SMALL_HALF = SMALL_TOTAL // 2
SMALL_SUM_ROWS = 432


def _gather_call(c_row, w_ada_sh, pack_in, pack_small):
    def body(c_ref, wada_ref, pki_ref, pks_ref, mg_ref, cg_ref, gwi_ref, gws_ref,
             cv, ssem_c, rsem_c, ssem_m, rsem_m, ssem_w, rsem_w, ssem_f, rsem_f, lsem):
        x, y, c = lax.axis_index("x"), lax.axis_index("y"), lax.axis_index("c")
        me = 4 * x + 2 * y + c
        chip = 2 * x + y
        rel3 = [(1, 0), (0, 1), (1, 1)]
        packs = [(pki_ref, gwi_ref), (pks_ref, gws_ref)]

        def slot(a, gw, k, h):
            return gw.at[h, k] if a == 0 else gw.at[k, h]

        sends = []
        for j, (dx, dy) in enumerate(rel3):
            for a, (pk, gw) in enumerate(packs):
                cp = pltpu.make_async_remote_copy(
                    src_ref=pk.at[c], dst_ref=slot(a, gw, chip, c), send_sem=ssem_w.at[j, a],
                    recv_sem=rsem_w.at[j, a], device_id=(_flip(x, dx), _flip(y, dy), c), device_id_type=MESH)
                cp.start()
                sends.append(cp)
        owns = []
        for a, (pk, gw) in enumerate(packs):
            for h in range(2):
                own = pltpu.make_async_copy(pk.at[h], slot(a, gw, chip, h), lsem.at[a, h])
                own.start()
                owns.append(own)

        cv[me] = c_ref[...]
        for r in range(1, N_DEV):
            dx, dy, dc = (r >> 2) & 1, (r >> 1) & 1, r & 1
            cp = pltpu.make_async_remote_copy(
                src_ref=c_ref, dst_ref=cv.at[me], send_sem=ssem_c.at[r - 1], recv_sem=rsem_c.at[r - 1],
                device_id=(_flip(x, dx), _flip(y, dy), _flip(c, dc)), device_id_type=MESH)
            cp.start()
            sends.append(cp)
        for r in range(1, N_DEV):
            dx, dy, dc = (r >> 2) & 1, (r >> 1) & 1, r & 1
            src = 4 * _flip(x, dx) + 2 * _flip(y, dy) + _flip(c, dc)
            pltpu.make_async_remote_copy(
                src_ref=c_ref, dst_ref=cv.at[src], send_sem=ssem_c.at[r - 1], recv_sem=rsem_c.at[r - 1],
                device_id=(x, y, c), device_id_type=MESH).wait_recv()
        rows = lax.broadcasted_iota(jnp.int32, (N_DEV, D_MODEL), 0)
        call = jnp.zeros((N_DEV, D_MODEL), F32)
        for b in range(N_DEV):
            call = jnp.where(rows == b, jnp.broadcast_to(cv[b], (N_DEV, D_MODEL)), call)
        cg_ref[...] = call

        mg_ref[chip] = lax.dot_general(call, wada_ref[...], (((1,), (0,)), ((), ())),
                                       precision=lax.Precision.HIGHEST, preferred_element_type=F32)
        for j, (dx, dy) in enumerate(rel3):
            cp = pltpu.make_async_remote_copy(
                src_ref=mg_ref.at[chip], dst_ref=mg_ref.at[chip], send_sem=ssem_m.at[j], recv_sem=rsem_m.at[j],
                device_id=(_flip(x, dx), _flip(y, dy), c), device_id_type=MESH)
            cp.start()
            sends.append(cp)
        for j, (dx, dy) in enumerate(rel3):
            src_chip = 2 * _flip(x, dx) + _flip(y, dy)
            pltpu.make_async_remote_copy(
                src_ref=mg_ref.at[src_chip], dst_ref=mg_ref.at[src_chip], send_sem=ssem_m.at[j],
                recv_sem=rsem_m.at[j], device_id=(x, y, c), device_id_type=MESH).wait_recv()
        for j, (dx, dy) in enumerate(rel3):
            src_chip = 2 * _flip(x, dx) + _flip(y, dy)
            for a, (pk, gw) in enumerate(packs):
                pltpu.make_async_remote_copy(
                    src_ref=pk.at[c], dst_ref=slot(a, gw, src_chip, c), send_sem=ssem_w.at[j, a],
                    recv_sem=rsem_w.at[j, a], device_id=(x, y, c), device_id_type=MESH).wait_recv()
                cp = pltpu.make_async_remote_copy(
                    src_ref=slot(a, gw, src_chip, c), dst_ref=slot(a, gw, src_chip, c), send_sem=ssem_f.at[j, a],
                    recv_sem=rsem_f.at[j, a], device_id=(x, y, 1 - c), device_id_type=MESH)
                cp.start()
                sends.append(cp)
        for j, (dx, dy) in enumerate(rel3):
            src_chip = 2 * _flip(x, dx) + _flip(y, dy)
            for a, (pk, gw) in enumerate(packs):
                pltpu.make_async_remote_copy(
                    src_ref=pk.at[c], dst_ref=slot(a, gw, src_chip, 1 - c), send_sem=ssem_f.at[j, a],
                    recv_sem=rsem_f.at[j, a], device_id=(x, y, c), device_id_type=MESH).wait_recv()
        for cp in sends:
            cp.wait_send()
        for own in owns:
            own.wait()

    vmem = pl.BlockSpec(memory_space=pltpu.VMEM)
    return pl.pallas_call(
        body, name="gather_fwd",
        out_shape=(_sds((N_CHIPS, N_DEV, 768), F32), _sds((N_DEV, D_MODEL), F32),
                   _sds((2, N_CHIPS, IN_SHARD, HALF_D), BF16), _sds((N_CHIPS, 2, SMALL_HALF, LANES), BF16)),
        in_specs=[vmem, vmem, vmem, vmem], out_specs=(vmem, vmem, vmem, vmem),
        scratch_shapes=[
            pltpu.VMEM((N_DEV, 1, D_MODEL), F32),
            pltpu.SemaphoreType.DMA((N_DEV - 1,)), pltpu.SemaphoreType.DMA((N_DEV - 1,)),
            pltpu.SemaphoreType.DMA((3,)), pltpu.SemaphoreType.DMA((3,)),
            pltpu.SemaphoreType.DMA((3, 2)), pltpu.SemaphoreType.DMA((3, 2)),
            pltpu.SemaphoreType.DMA((3, 2)), pltpu.SemaphoreType.DMA((3, 2)),
            pltpu.SemaphoreType.DMA((2, 2)),
        ],
        compiler_params=_params(),
    )(c_row, w_ada_sh, pack_in, pack_small)


def _reduce_call(g_in, g_small, sv):
    def body(gi_ref, gs_ref, sv_ref, fi_ref, fs_ref, svg_ref, pair_i, pair_s, send_i, send_s, land_i, land_s,
             ssem_p, rsem_p, ssem_g, rsem_g, ssem_s, rsem_s, ssem_x, rsem_x):
        x, y, c = lax.axis_index("x"), lax.axis_index("y"), lax.axis_index("c")
        me = 4 * x + 2 * y + c
        chip = 2 * x + y
        rel3 = [(1, 0), (0, 1), (1, 1)]
        payloads = [(gi_ref, pair_i, send_i, land_i, fi_ref), (gs_ref, pair_s, send_s, land_s, fs_ref)]
        where = [lambda k, h: N_CHIPS * h + k, lambda k, h: 2 * k + h]
        copies = []

        for k in range(N_CHIPS):
            for a, (g, pair, _, _, _) in enumerate(payloads):
                cp = pltpu.make_async_remote_copy(
                    src_ref=g.at[where[a](k, 1 - c)], dst_ref=pair.at[k], send_sem=ssem_p.at[k, a],
                    recv_sem=rsem_p.at[k, a], device_id=(x, y, 1 - c), device_id_type=MESH)
                cp.start()
                copies.append(cp)

        for r in range(1, N_DEV):
            dx, dy, dc = (r >> 2) & 1, (r >> 1) & 1, r & 1
            cp = pltpu.make_async_remote_copy(
                src_ref=sv_ref, dst_ref=svg_ref.at[me], send_sem=ssem_s.at[r - 1], recv_sem=rsem_s.at[r - 1],
                device_id=(_flip(x, dx), _flip(y, dy), _flip(c, dc)), device_id_type=MESH)
            cp.start()
            copies.append(cp)
        svg_ref[me] = sv_ref[...]

        def pair_sum(k, store_in, store_small):
            for a, (g, pair, _, _, _) in enumerate(payloads):
                pltpu.make_async_remote_copy(
                    src_ref=g.at[where[a](k, c)], dst_ref=pair.at[k], send_sem=ssem_p.at[k, a],
                    recv_sem=rsem_p.at[k, a], device_id=(x, y, c), device_id_type=MESH).wait_recv()
            for qd in range(HALF_D // LANES):
                sl = slice(LANES * qd, LANES * qd + LANES)
                store_in(sl, gi_ref[where[0](k, c), :, sl].astype(F32) + pair_i[k, :, sl].astype(F32))

            def rows(i, carry):
                sl = pl.ds(pl.multiple_of(i * SMALL_SUM_ROWS, 16), SMALL_SUM_ROWS)
                store_small(sl, gs_ref[where[1](k, c), sl, :].astype(F32) + pair_s[k, sl, :].astype(F32))
                return carry

            lax.fori_loop(0, SMALL_HALF // SMALL_SUM_ROWS, rows, 0)

        for j, (dx, dy) in enumerate(rel3):
            tx, ty = _flip(x, dx), _flip(y, dy)

            def put_in(sl, val, j=j):
                send_i[j, :, sl] = val.astype(BF16)

            def put_small(sl, val, j=j):
                send_s[j, sl, :] = val.astype(BF16)

            pair_sum(2 * tx + ty, put_in, put_small)
            for a, (_, _, send, land, _) in enumerate(payloads):
                cp = pltpu.make_async_remote_copy(
                    src_ref=send.at[j], dst_ref=land.at[j], send_sem=ssem_g.at[j, a], recv_sem=rsem_g.at[j, a],
                    device_id=(tx, ty, c), device_id_type=MESH)
                cp.start()
                copies.append(cp)

        def own_in(sl, val):
            fi_ref[c, :, sl] = val

        def own_small(sl, val):
            fs_ref[c, sl, :] = val

        pair_sum(chip, own_in, own_small)
        for j in range(3):
            for a, (_, _, send, land, _) in enumerate(payloads):
                pltpu.make_async_remote_copy(
                    src_ref=send.at[j], dst_ref=land.at[j], send_sem=ssem_g.at[j, a], recv_sem=rsem_g.at[j, a],
                    device_id=(x, y, c), device_id_type=MESH).wait_recv()
            for qd in range(HALF_D // LANES):
                sl = slice(LANES * qd, LANES * qd + LANES)
                fi_ref[c, :, sl] += land_i[j, :, sl].astype(F32)

            def add_rows(i, carry, j=j):
                sl = pl.ds(pl.multiple_of(i * SMALL_SUM_ROWS, 16), SMALL_SUM_ROWS)
                fs_ref[c, sl, :] += land_s[j, sl, :].astype(F32)
                return carry

            lax.fori_loop(0, SMALL_HALF // SMALL_SUM_ROWS, add_rows, 0)

        for a, f in enumerate((fi_ref, fs_ref)):
            cp = pltpu.make_async_remote_copy(
                src_ref=f.at[c], dst_ref=f.at[c], send_sem=ssem_x.at[a], recv_sem=rsem_x.at[a],
                device_id=(x, y, 1 - c), device_id_type=MESH)
            cp.start()
            copies.append(cp)
        for a, f in enumerate((fi_ref, fs_ref)):
            pltpu.make_async_remote_copy(
                src_ref=f.at[c], dst_ref=f.at[1 - c], send_sem=ssem_x.at[a], recv_sem=rsem_x.at[a],
                device_id=(x, y, c), device_id_type=MESH).wait_recv()
        for r in range(1, N_DEV):
            dx, dy, dc = (r >> 2) & 1, (r >> 1) & 1, r & 1
            src = 4 * _flip(x, dx) + 2 * _flip(y, dy) + _flip(c, dc)
            pltpu.make_async_remote_copy(
                src_ref=sv_ref, dst_ref=svg_ref.at[src], send_sem=ssem_s.at[r - 1],
                recv_sem=rsem_s.at[r - 1], device_id=(x, y, c), device_id_type=MESH).wait_recv()
        for cp in copies:
            cp.wait_send()

    vmem = pl.BlockSpec(memory_space=pltpu.VMEM)
    return pl.pallas_call(
        body, name="grad_reduce",
        out_shape=(_sds((2, IN_SHARD, HALF_D), F32), _sds((2, SMALL_HALF, LANES), F32),
                   _sds((N_DEV, 8, SV_COLS), F32)),
        in_specs=[vmem, vmem, vmem], out_specs=(vmem, vmem, vmem),
        scratch_shapes=[
            pltpu.VMEM((N_CHIPS, IN_SHARD, HALF_D), BF16), pltpu.VMEM((N_CHIPS, SMALL_HALF, LANES), BF16),
            pltpu.VMEM((3, IN_SHARD, HALF_D), BF16), pltpu.VMEM((3, SMALL_HALF, LANES), BF16),
            pltpu.VMEM((3, IN_SHARD, HALF_D), BF16), pltpu.VMEM((3, SMALL_HALF, LANES), BF16),
            pltpu.SemaphoreType.DMA((N_CHIPS, 2)), pltpu.SemaphoreType.DMA((N_CHIPS, 2)),
            pltpu.SemaphoreType.DMA((3, 2)), pltpu.SemaphoreType.DMA((3, 2)),
            pltpu.SemaphoreType.DMA((N_DEV - 1,)), pltpu.SemaphoreType.DMA((N_DEV - 1,)),
            pltpu.SemaphoreType.DMA((2,)), pltpu.SemaphoreType.DMA((2,)),
        ],
        compiler_params=_params(),
    )(g_in, g_small, sv)


def _dwin_call(h, dproj):
    s_len = h.shape[0]
    tm = min(4 * ROW_TILE, s_len)
    nrow = s_len // tm
    nc = 4
    chunk = W_INT // nc

    def body(h_ref, dp_ref, dw_ref, acc):
        c, i = pl.program_id(0), pl.program_id(1)

        @pl.when(i == 0)
        def _():
            acc[...] = jnp.zeros_like(acc)

        acc[...] += _dot_tn(dp_ref[...], h_ref[...])

        def put(lo, hi, dst):
            for half in range(2):
                dw_ref[half, dst:dst + hi - lo, :] = acc[lo:hi, half * HALF_D:(half + 1) * HALF_D].astype(BF16)

        for cc in range(nc):
            @pl.when((c == cc) & (i == nrow - 1))
            def _(cc=cc):
                base = cc * chunk
                lo, hi = base, min(base + chunk, O_MZ)
                if lo < hi:
                    put(lo - base, hi - base, lo)
                lo, hi = max(base, O_MZ), min(base + chunk, O_KR)
                if lo < hi:
                    put(lo - base, hi - base, lo + ROPE_DIM)
                if base <= O_KR and O_END <= base + chunk:
                    r = O_KR - base
                    half_r = ROPE_DIM // 2
                    kr = sum(acc[r + ROPE_DIM * j:r + ROPE_DIM * (j + 1), :] for j in range(4))
                    sw = sum(acc[r + 128 + ROPE_DIM * j:r + 128 + ROPE_DIM * (j + 1), :] for j in range(4))
                    tot = kr + jnp.concatenate([sw[half_r:], sw[:half_r]], axis=0)
                    for half in range(2):
                        dw_ref[half, O_MZ:O_MZ + ROPE_DIM, :] = (
                            tot[:, half * HALF_D:(half + 1) * HALF_D].astype(BF16))

    assert O_KR >= (nc - 1) * chunk
    return pl.pallas_call(
        body, name="dwin", grid=(nc, nrow),
        out_shape=_sds((2, IN_WIDTH, HALF_D), BF16),
        in_specs=[pl.BlockSpec((tm, D_MODEL), lambda c, i: (i, 0)),
                  pl.BlockSpec((tm, chunk), lambda c, i: (i, c))],
        out_specs=pl.BlockSpec((2, IN_WIDTH, HALF_D), lambda c, i: (0, 0, 0)),
        scratch_shapes=[pltpu.VMEM((chunk, D_MODEL), F32)],
        compiler_params=_params(("arbitrary", "arbitrary")),
    )(h, dproj)


def _internal_weights(w_in_t, w_uq, w_ukv):
    krot_t = w_in_t[:, O_MZ:O_MZ + ROPE_DIM]
    krot_sw = krot_t.reshape(2, 2, ROPE_DIM // 2, HALF_D)[:, ::-1].reshape(2, ROPE_DIM, HALF_D)
    w_kr = jnp.concatenate([jnp.tile(krot_t, (1, 4, 1)), jnp.tile(krot_sw, (1, 4, 1))], axis=1)
    uq = w_uq.reshape(Q_RANK, N_HEADS, 96)
    wp = uq[:, :, 64:].reshape(Q_RANK, 256)
    w_q = jnp.concatenate([uq[:, :, :64].reshape(Q_RANK, 512), wp, _swap_halves(wp, 32)], axis=1)
    ukv = w_ukv.reshape(KV_RANK, N_HEADS, 128)
    w_kv = jnp.concatenate([ukv[:, :, :64].reshape(KV_RANK, 512), ukv[:, :, 64:].reshape(KV_RANK, 512)], axis=1)
    return w_kr, w_q, w_kv


def _true_weight_grads(dwq, dwkv):
    dwp = dwq[:, 512:768] + _swap_halves(dwq[:, 768:1024], 32)
    g_uq = jnp.concatenate([dwq[:, :512].reshape(Q_RANK, N_HEADS, 64), dwp.reshape(Q_RANK, N_HEADS, 32)],
                           axis=2).reshape(Q_RANK, 768)
    g_ukv = jnp.concatenate([dwkv[:, :512].reshape(KV_RANK, N_HEADS, 64), dwkv[:, 512:].reshape(KV_RANK, N_HEADS, 64)],
                            axis=2).reshape(KV_RANK, 1024)
    return g_uq, g_ukv


def _swap_halves(w, group):
    r, n = w.shape
    return w.reshape(r, n // group, 2, group // 2)[:, :, ::-1, :].reshape(r, n)


def _pack_shards(parts):
    return jnp.concatenate([p.reshape(-1, LANES) for p in parts], axis=0)


def _unpack_small(gw):
    offs = [0]
    for r in SMALL_ROWS:
        offs.append(offs[-1] + r)

    def cols(i, rows, shard_cols):
        blk = gw[:, offs[i]:offs[i + 1]].reshape(N_CHIPS, rows, shard_cols)
        return blk.transpose(1, 0, 2).reshape(rows, N_CHIPS * shard_cols)

    return (cols(0, Q_RANK, 192), cols(1, KV_RANK, 256), cols(2, 512, 256), cols(3, 512, 256),
            gw[:, offs[4]:offs[5]].reshape(D_MODEL, D_MODEL))


def _chip_major(g, shard_cols):
    r = g.shape[0]
    return g.reshape(r, N_CHIPS, shard_cols).transpose(1, 0, 2).reshape(N_CHIPS, -1, LANES)


def kernel(x, c, positions, w_ada, b_ada, norm_gain, w_in, q_norm_gain, w_uq, kv_norm_gain, w_ukv, w_branch_a, w_branch_b, w_out, final_norm_gain, loss_target, m_w_ada, m_b_ada, m_norm_gain, m_w_in, m_q_norm_gain, m_w_uq, m_kv_norm_gain, m_w_ukv, m_w_branch_a, m_w_branch_b, m_w_out, m_final_norm_gain, v_w_ada, v_b_ada, v_norm_gain, v_w_in, v_q_norm_gain, v_w_uq, v_kv_norm_gain, v_w_ukv, v_w_branch_a, v_w_branch_b, v_w_out, v_final_norm_gain):
    ix, iy, ic = lax.axis_index("x"), lax.axis_index("y"), lax.axis_index("c")
    me = 4 * ix + 2 * iy + ic
    chip = 2 * ix + iy
    xs = x[0]
    tgt = loss_target[0]
    s_len = xs.shape[0]

    w_in_t = jnp.swapaxes(w_in[0], 0, 1)
    w_in_tb = w_in_t.astype(BF16)
    pack_in = jnp.stack([w_in_tb[:, :HALF_D], w_in_tb[:, HALF_D:]], axis=0)
    small_shards = (w_uq[0], w_ukv[0], w_branch_a[0], w_branch_b[0], w_out[0])
    pack_small = _pack_shards([s.astype(BF16) for s in small_shards]).reshape(2, SMALL_HALF, LANES)
    mg, call, gw_in, gw_small = _gather_call(c, w_ada[0], pack_in, pack_small)
    mod = mg.transpose(1, 0, 2).reshape(N_DEV, 3 * D_MODEL) + b_ada
    mod_me = lax.dynamic_slice_in_dim(mod, me, 1, axis=0)
    shift, scale, gate = mod_me[:, :D_MODEL], mod_me[:, D_MODEL:2 * D_MODEL], mod_me[:, 2 * D_MODEL:]

    f_in_t = gw_in.reshape(2, IN_WIDTH, HALF_D)
    f_uq, f_ukv, f_a, f_b, f_out = _unpack_small(gw_small.reshape(N_CHIPS, SMALL_TOTAL, LANES))
    w_kr, w_q, w_kv = _internal_weights(f_in_t, f_uq, f_ukv)

    inv_freq = ROPE_BASE ** (-jnp.arange(0, ROPE_DIM, 2, dtype=F32) / ROPE_DIM)
    ang = positions[0].astype(F32)[:, None] * jnp.tile(inv_freq, 2 * LANES // ROPE_DIM)
    sign = jnp.tile(jnp.repeat(jnp.asarray([-1.0, 1.0], F32), ROPE_DIM // 2), LANES // ROPE_DIM)
    cos128, sin128 = jnp.cos(ang), jnp.sin(ang) * sign

    (h, sq, sk, sv, sz, cq, ckv, mz, ga, gb, kpt, qn, qp, kn, vv) = _inproj_call(
        xs, shift, scale, norm_gain, f_in_t, w_kr, w_q, w_kv, q_norm_gain, kv_norm_gain, cos128, sin128)
    oa, lt, first = _sb_fwd_call(sq, sk, sv)
    ob, lse = _mla_fwd_call(qn, qp, kn, kpt, vv)

    gf = final_norm_gain.reshape(1, D_MODEL)
    (dx2, doa, dob, dsz, dmz, dga, dgb, dwo, dwa, dwb, dgf, dgate, loss_p) = _post_call(
        xs, tgt, oa, ob, sz, mz, ga, gb, gate, gf, f_a, f_b, f_out)

    dsq, dsk_t, dsv_t = _sb_bwd_call(first[:, :, 0, 0].reshape(-1), sq, sk, sv, doa, lt)
    dqn, dqp, dkn_t, dkpt_t, dvv_t = _mla_bwd_call(qn, qp, kn, kpt, vv, ob, dob, lse)

    dproj, dwq, dwkv, dqg, dkvg = _bwdprep_call(
        dsq, dsk_t, dsv_t, dsz, dqn, dqp, dkn_t, dvv_t, dkpt_t, dmz, dga, dgb, cq, ckv, cos128, sin128,
        q_norm_gain, kv_norm_gain, w_q, w_kv)
    grad_x, dshift, dscale, dg1 = _dh_call(dproj, f_in_t, w_kr, xs, dx2, scale, norm_gain)
    g_in_t = _dwin_call(h, dproj)
    g_uq, g_ukv = _true_weight_grads(dwq, dwkv)

    g_in_pieces = g_in_t.reshape(N_DEV, IN_SHARD, HALF_D)
    g_small = jnp.concatenate([
        _chip_major(g_uq, 192).astype(BF16), _chip_major(g_ukv, 256).astype(BF16),
        dwa.reshape(N_CHIPS, -1, LANES), dwb.reshape(N_CHIPS, -1, LANES),
        dwo.reshape(N_CHIPS, -1, LANES)], axis=1).reshape(N_DEV, SMALL_HALF, LANES)
    small = _pack_vectors_call([dshift, dscale, dgate, dg1, dqg, dkvg, dgf, loss_p])
    full_in, full_small, svg = _reduce_call(g_in_pieces, g_small, small)
    full = full_small.reshape(SMALL_TOTAL, LANES)
    offs = [0]
    for r in SMALL_ROWS:
        offs.append(offs[-1] + r)
    gs_uq = full[offs[0]:offs[1]].reshape(Q_RANK, 192)
    gs_ukv = full[offs[1]:offs[2]].reshape(KV_RANK, 256)
    gs_a = full[offs[2]:offs[3]].reshape(512, 256)
    gs_b = full[offs[3]:offs[4]].reshape(512, 256)
    gs_out = full[offs[4]:offs[5]].reshape(256, D_MODEL)

    svm = svg.reshape(N_DEV, 8 * SV_COLS)
    dmod_sh = lax.dynamic_slice_in_dim(svm[:, :3 * D_MODEL], chip * 768, 768, axis=1)
    tot, gs_ada = _small_call(svm, call.T, dmod_sh)
    vec_offsets = {"b_ada": 0, "norm_gain": 3072, "q_norm_gain": 4096, "kv_norm_gain": 4480, "final_norm_gain": 4736}
    loss = tot[0, 5760]

    names = ["w_ada", "b_ada", "norm_gain", "w_in", "q_norm_gain", "w_uq", "kv_norm_gain", "w_ukv",
             "w_branch_a", "w_branch_b", "w_out", "final_norm_gain"]
    ws = [w_ada[0], b_ada, norm_gain, w_in_t, q_norm_gain, w_uq[0], kv_norm_gain, w_ukv[0],
          w_branch_a[0], w_branch_b[0], w_out[0], final_norm_gain.reshape(1, D_MODEL)]
    gs = [gs_ada, None, None, full_in, None, gs_uq, None, gs_ukv, gs_a, gs_b, gs_out, None]
    ms = [m_w_ada[0], m_b_ada, m_norm_gain, jnp.swapaxes(m_w_in[0], 0, 1), m_q_norm_gain, m_w_uq[0],
          m_kv_norm_gain, m_w_ukv[0], m_w_branch_a[0], m_w_branch_b[0], m_w_out[0],
          m_final_norm_gain.reshape(1, D_MODEL)]
    vs = [v_w_ada[0], v_b_ada, v_norm_gain, jnp.swapaxes(v_w_in[0], 0, 1), v_q_norm_gain, v_w_uq[0],
          v_kv_norm_gain, v_w_ukv[0], v_w_branch_a[0], v_w_branch_b[0], v_w_out[0],
          v_final_norm_gain.reshape(1, D_MODEL)]
    refs = [w_ada, b_ada, norm_gain, w_in, q_norm_gain, w_uq, kv_norm_gain, w_ukv,
            w_branch_a, w_branch_b, w_out, final_norm_gain]
    vec_ids = [k for k, n in enumerate(names) if n in vec_offsets]
    vec_outs = dict(zip(vec_ids, _adamw_vectors_call(
        tot, [vec_offsets[names[k]] for k in vec_ids], [ws[k] for k in vec_ids], [ms[k] for k in vec_ids],
        [vs[k] for k in vec_ids])))
    grads, deltas, new_ms, new_vs = [], [], [], []
    for k, (n, w_, g_, m_, v_, ref) in enumerate(zip(names, ws, gs, ms, vs, refs)):
        outs = vec_outs[k] if k in vec_outs else _adamw_call(n, w_, g_, m_, v_)
        if n == "w_in":
            outs = tuple(jnp.swapaxes(o_, 0, 1) for o_ in outs)
        for lst, o_ in zip((grads, deltas, new_ms, new_vs), outs):
            lst.append(o_.reshape(ref.shape))

    return (loss, grad_x.reshape(x.shape), *grads, *deltas, *new_ms, *new_vs)
```

```python
import math

import jax
import jax.numpy as jnp
from jax import lax
from jax.experimental import pallas as pl
from jax.experimental.pallas import tpu as pltpu

F32 = jnp.float32
BF16 = jnp.bfloat16

D_MODEL = 1024
SB_WIDTH = 512
MLA_WIDTH = 512
Q_RANK = 384
KV_RANK = 256
ROPE_DIM = 32
N_HEADS = 8
IN_WIDTH = 5280
EPS = 1e-6
ROPE_BASE = 10000.0
MLA_SCALE = 1.0 / math.sqrt(96.0)
SB_SCALE = 0.125
LOG2E = 1.4426950408889634

ADAM_LR = 0.001
ADAM_B1 = 0.9
ADAM_B2 = 0.999
ADAM_EPS = 1e-08
ADAM_WD = 0.01
ADAM_STEP = 10

O_SQ, O_SK, O_SV, O_SZ, O_CQ, O_CKV, O_MZ, O_GA, O_GB, O_KR, O_END = (
    0, 512, 1024, 1536, 2048, 2432, 2688, 3200, 4224, 5248, 5504)
W_INT = 5632

N_CHIPS = 4
N_DEV = 8
LANES = 128
SV_COLS = 768

ROW_TILE = 256
ATT_TILE = 256
ATT_Q_TILES = 2
FWD_Q_TILES = 4
SB_Q_TILES = 1
MLA_KEY_TILE = 512
VMEM_LIMIT = 56 * 1024 * 1024

MESH = pl.DeviceIdType.MESH


def _dot(a, b):
    return lax.dot_general(a, b, (((1,), (0,)), ((), ())), preferred_element_type=F32)


def _dot_nt(a, b):
    return lax.dot_general(a, b, (((1,), (1,)), ((), ())), preferred_element_type=F32)


def _dot_tn(a, b):
    return lax.dot_general(a, b, (((0,), (0,)), ((), ())), preferred_element_type=F32)


def _sigmoid(z):
    return 1.0 / (1.0 + jnp.exp2(z * (-LOG2E)))


def _params(sem=None):
    if sem is None:
        return pltpu.CompilerParams(vmem_limit_bytes=VMEM_LIMIT)
    return pltpu.CompilerParams(dimension_semantics=sem, vmem_limit_bytes=VMEM_LIMIT)


def _rows(tm, n):
    return pl.BlockSpec((tm, n), lambda i: (i, 0))


def _cols(n, tm):
    return pl.BlockSpec((n, tm), lambda i: (0, i))


def _whole(shape):
    nd = len(shape)
    return pl.BlockSpec(shape, lambda i: (0,) * nd)


def _sds(shape, dtype):
    return jax.ShapeDtypeStruct(shape, dtype)


def _flip(v, d):
    return 1 - v if d else v


def _weight_rows(a, b):
    if a >= O_KR:
        return True, slice(a - O_KR, b - O_KR)
    shift = ROPE_DIM if a >= O_MZ else 0
    return False, slice(a + shift, b + shift)


def _inproj_call(x, shift, scale, g1, w_in_t, w_kr, w_q, w_kv, qg, kvg, cos128, sin128):
    s_len = x.shape[0]
    tm = min(ROW_TILE, s_len)

    def body(x_ref, sh_ref, sc_ref, g1_ref, win_ref, wkr_ref, wq_ref, wkv_ref, qg_ref, kvg_ref, cos_ref, sin_ref,
             h_ref, sq_ref, sk_ref, sv_ref, sz_ref, cq_ref, ckv_ref, mz_ref, ga_ref, gb_ref, kpt_ref,
             qn_ref, qp_ref, kn_ref, vv_ref):
        xt = x_ref[...]
        r = lax.rsqrt(jnp.mean(xt * xt, axis=-1, keepdims=True) + EPS)
        h = (xt * r * g1_ref[...]) * (1.0 + sc_ref[...]) + sh_ref[...]
        hb = h.astype(BF16)
        h_ref[...] = hb

        def seg(a, b):
            from_kr, rows = _weight_rows(a, b)
            w_ref = wkr_ref if from_kr else win_ref
            return _dot_nt(hb[:, :HALF_D], w_ref[0, rows, :]) + _dot_nt(hb[:, HALF_D:], w_ref[1, rows, :])

        sq_ref[...] = (seg(O_SQ, O_SK) * SB_SCALE).astype(BF16)
        sk_ref[...] = seg(O_SK, O_SV).astype(BF16)
        sv_ref[...] = seg(O_SV, O_SZ).astype(BF16)
        sz_ref[...] = seg(O_SZ, O_CQ)
        mz_ref[...] = seg(O_MZ, O_GA)
        ga_ref[...] = seg(O_GA, O_GB)
        gb_ref[...] = seg(O_GB, O_KR)
        cos = cos_ref[...]
        sin = sin_ref[...]
        kr = seg(O_KR, O_END)
        kpt_ref[...] = (kr[:, :128] * cos + kr[:, 128:] * sin).astype(BF16)

        cq = seg(O_CQ, O_CKV)
        cq_ref[...] = cq
        rq = lax.rsqrt(jnp.mean(cq * cq, axis=-1, keepdims=True) + EPS)
        cqn = (cq * rq * qg_ref[...]).astype(BF16)
        qa = _dot(cqn, wq_ref[...])
        qn_ref[...] = qa[:, :512].astype(BF16)
        qp_ref[...] = (qa[:, 512:768] * jnp.tile(cos, (1, 2)) + qa[:, 768:] * jnp.tile(sin, (1, 2))).astype(BF16)

        ckv = seg(O_CKV, O_MZ)
        ckv_ref[...] = ckv
        rk = lax.rsqrt(jnp.mean(ckv * ckv, axis=-1, keepdims=True) + EPS)
        ckvn = (ckv * rk * kvg_ref[...]).astype(BF16)
        kva = _dot(ckvn, wkv_ref[...])
        kn_ref[...] = kva[:, :512].astype(BF16)
        vv_ref[...] = kva[:, 512:].astype(BF16)

    outs = [
        (D_MODEL, BF16), (512, BF16), (512, BF16), (512, BF16), (512, F32), (Q_RANK, F32), (KV_RANK, F32),
        (512, F32), (D_MODEL, F32), (D_MODEL, F32), (128, BF16), (512, BF16), (256, BF16), (512, BF16), (512, BF16),
    ]
    return pl.pallas_call(
        body, name="inproj", grid=(s_len // tm,),
        out_shape=tuple(_sds((s_len, n), dt) for n, dt in outs),
        in_specs=[_rows(tm, D_MODEL), _whole((1, D_MODEL)), _whole((1, D_MODEL)), _whole((1, D_MODEL)),
                  _whole((2, IN_WIDTH, HALF_D)), _whole((2, O_END - O_KR, HALF_D)),
                  _whole((Q_RANK, 1024)), _whole((KV_RANK, 1024)),
                  _whole((1, Q_RANK)), _whole((1, KV_RANK)), _rows(tm, LANES), _rows(tm, LANES)],
        out_specs=tuple(_rows(tm, n) for n, _ in outs),
        compiler_params=_params(("parallel",)),
    )(x, shift, scale, g1, w_in_t, w_kr, w_q, w_kv, qg, kvg, cos128, sin128)


Z_CLAMP = 80.0 * LOG2E
RUN_CUTOFF = 110.0 * LOG2E


def _softplus_clamped(z):
    zc = jnp.minimum(z * LOG2E, Z_CLAMP)
    return zc, jnp.log2(1.0 + jnp.exp2(zc))


def _tri_sum(a, tri):
    return _dot(a.astype(BF16), tri)


def _sb_fwd_call(q, k, v):
    s_len = q.shape[0]
    tk = min(ATT_TILE, s_len)
    tq = min(SB_Q_TILES * ATT_TILE, s_len)
    r = tq // tk
    nq = s_len // tq

    def body(q_ref, k_ref, v_ref, o_ref, lt_ref, first_ref):
        i = pl.program_id(1)
        q2 = q_ref[...]
        lane = lax.broadcasted_iota(jnp.int32, (1, 256), 1)
        krow = lax.broadcasted_iota(jnp.int32, (tk, tk), 0)
        kcol = lax.broadcasted_iota(jnp.int32, (tk, tk), 1)
        row = lax.broadcasted_iota(jnp.int32, (tq, tk), 0)
        col = lax.broadcasted_iota(jnp.int32, (tq, tk), 1)
        later = (krow > kcol).astype(BF16)
        valids = [col + u * tk < row for u in range(r)]
        hms = [(lane // 64) == hh for hh in range(4)]
        qms = [jnp.where(hm, q2, jnp.zeros_like(q2)) for hm in hms]

        def block(j, carry, valid):
            runs, acc = list(carry[:4]), carry[4]
            off = pl.multiple_of(j * tk, tk)
            kb = k_ref[pl.ds(off, tk), :]
            vb = v_ref[pl.ds(off, tk), :]
            ws = []
            for hh in range(4):
                zc, sp = _softplus_clamped(_dot_nt(qms[hh], kb))
                lm = jnp.where(valid, sp, 0.0) if valid is not None else sp
                suf = _tri_sum(lm, later)
                w = jnp.exp2(zc - sp - suf - runs[hh])
                if valid is not None:
                    w = jnp.where(valid, w, 0.0)
                ws.append(w.astype(BF16))
                runs[hh] = runs[hh] + jnp.sum(lm, axis=1, keepdims=True)
            vstack = jnp.concatenate([jnp.where(hm, vb, jnp.zeros_like(vb)) for hm in hms], axis=0)
            acc = acc + _dot(jnp.concatenate(ws, axis=1), vstack)
            return (*runs, acc)

        zero = jnp.zeros((tq, 1), F32)
        carry = (zero, zero, zero, zero, jnp.zeros((tq, 256), F32))
        for u in reversed(range(r)):
            carry = block(i * r + u, carry, valids[u])

        def least_run(runs):
            return jnp.min(jnp.minimum(jnp.minimum(runs[0], runs[1]), jnp.minimum(runs[2], runs[3])))

        n_full = i * r

        def unfinished(state):
            return jnp.logical_and(state[0] < n_full, state[1] <= RUN_CUTOFF)

        def visit(state):
            cr = block(n_full - 1 - state[0], state[2:], None)
            return (state[0] + 1, least_run(cr[:4]), *cr)

        state = lax.while_loop(unfinished, visit, (jnp.int32(0), least_run(carry[:4]), *carry))
        carry = state[2:]
        first_ref[...] = jnp.full(first_ref.shape, n_full - state[0], jnp.int32)
        for hh in range(4):
            lt_ref[0, :, hh:hh + 1] = carry[hh]
        o_ref[...] = carry[4]

    return pl.pallas_call(
        body, name="sb_fwd", grid=(2, nq),
        out_shape=(_sds((s_len, SB_WIDTH), F32), _sds((2, s_len, 4), F32), _sds((2, nq, 8, 128), jnp.int32)),
        in_specs=[pl.BlockSpec((tq, 256), lambda g, i: (i, g)),
                  pl.BlockSpec((s_len, 256), lambda g, i: (0, g)),
                  pl.BlockSpec((s_len, 256), lambda g, i: (0, g))],
        out_specs=(pl.BlockSpec((tq, 256), lambda g, i: (i, g)),
                   pl.BlockSpec((1, tq, 4), lambda g, i: (g, i, 0)),
                   pl.BlockSpec((1, 1, 8, 128), lambda g, i: (g, i, 0, 0))),
        compiler_params=_params(("parallel", "parallel")),
    )(q, k, v)


def _sb_bwd_call(first, q, k, v, do, lt):
    s_len = q.shape[0]
    tk = min(ATT_TILE, s_len)
    tq = min(SB_Q_TILES * ATT_TILE, s_len)
    r = tq // tk
    nq = s_len // tq
    nq_fwd = first.shape[0] // 2
    per_fwd = nq // nq_fwd

    def body(first_ref, q_ref, k_ref, v_ref, do_ref, lt_ref, dq_ref, dk_ref, dv_ref):
        g = pl.program_id(0)
        i = pl.program_id(1)

        @pl.when(i == 0)
        def _():
            dk_ref[...] = jnp.zeros_like(dk_ref)
            dv_ref[...] = jnp.zeros_like(dv_ref)

        q2 = q_ref[...]
        do2 = do_ref[...].astype(BF16)
        lane = lax.broadcasted_iota(jnp.int32, (1, 256), 1)
        krow = lax.broadcasted_iota(jnp.int32, (tk, tk), 0)
        kcol = lax.broadcasted_iota(jnp.int32, (tk, tk), 1)
        row = lax.broadcasted_iota(jnp.int32, (tq, tk), 0)
        col = lax.broadcasted_iota(jnp.int32, (tq, tk), 1)
        earlier = (krow < kcol).astype(BF16)
        later = (krow > kcol).astype(BF16)
        valids = [col + u * tk < row for u in range(r)]
        hms = [(lane // 64) == hh for hh in range(4)]
        qms = [jnp.where(hm, q2, jnp.zeros_like(q2)) for hm in hms]
        doms = [jnp.where(hm, do2, jnp.zeros_like(do2)) for hm in hms]
        ltots = [lt_ref[0, :, hh:hh + 1] for hh in range(4)]
        q2t = jnp.transpose(q2.astype(F32))
        do2t = jnp.transpose(do_ref[...])
        subl = lax.broadcasted_iota(jnp.int32, (256, 1), 0)
        qtstack = jnp.concatenate(
            [jnp.where((subl // 64) == hh, q2t, 0.0).astype(BF16) for hh in range(4)], axis=1)
        dotstack = jnp.concatenate(
            [jnp.where((subl // 64) == hh, do2t, 0.0).astype(BF16) for hh in range(4)], axis=1)

        def block(j, carry, valid):
            lpre, ppre, dq = list(carry[0:4]), list(carry[4:8]), carry[8]
            off = pl.multiple_of(j * tk, tk)
            kb = k_ref[pl.ds(off, tk), :]
            vb = v_ref[pl.ds(off, tk), :]
            dzs, avs = [], []
            for hh in range(4):
                zc, sp = _softplus_clamped(_dot_nt(qms[hh], kb))
                lsig = zc - sp
                lm = jnp.where(valid, sp, 0.0) if valid is not None else sp
                rowsum = jnp.sum(lm, axis=1, keepdims=True)
                between = _tri_sum(lm, later) + ((ltots[hh] - lpre[hh]) - rowsum)
                a = jnp.exp2(lsig - between)
                if valid is not None:
                    a = jnp.where(valid, a, 0.0)
                p = a * _dot_nt(doms[hh], vb)
                pbefore = ppre[hh] + _tri_sum(p, earlier)
                dz = p - jnp.exp2(lsig) * (p + pbefore)
                if valid is not None:
                    dz = jnp.where(valid, dz, 0.0)
                dzs.append(dz.astype(BF16))
                avs.append(a.astype(BF16))
                lpre[hh] = lpre[hh] + rowsum
                ppre[hh] = ppre[hh] + jnp.sum(p, axis=1, keepdims=True)
            kstack = jnp.concatenate([jnp.where(hm, kb, jnp.zeros_like(kb)) for hm in hms], axis=0)
            dq = dq + _dot(jnp.concatenate(dzs, axis=1), kstack)
            dk_ref[:, pl.ds(off, tk)] += _dot(qtstack, jnp.concatenate(dzs, axis=0))
            dv_ref[:, pl.ds(off, tk)] += _dot(dotstack, jnp.concatenate(avs, axis=0))
            return (*lpre, *ppre, dq)

        zero = jnp.zeros((tq, 1), F32)
        start = jnp.minimum(first_ref[g * nq_fwd + i // per_fwd], i * r)
        carry = lax.fori_loop(start, i * r, lambda j, cr: block(j, cr, None),
                              (zero,) * 8 + (jnp.zeros((tq, 256), F32),))
        for u in range(r):
            carry = block(i * r + u, carry, valids[u])
        dq_ref[...] = carry[8].astype(BF16)

    return pl.pallas_call(
        body, name="sb_bwd",
        out_shape=(_sds((s_len, SB_WIDTH), BF16), _sds((SB_WIDTH, s_len), F32), _sds((SB_WIDTH, s_len), F32)),
        grid_spec=pltpu.PrefetchScalarGridSpec(
            num_scalar_prefetch=1, grid=(2, nq),
            in_specs=[pl.BlockSpec((tq, 256), lambda g, i, f: (i, g)),
                      pl.BlockSpec((s_len, 256), lambda g, i, f: (0, g)),
                      pl.BlockSpec((s_len, 256), lambda g, i, f: (0, g)),
                      pl.BlockSpec((tq, 256), lambda g, i, f: (i, g)),
                      pl.BlockSpec((1, tq, 4), lambda g, i, f: (g, i, 0))],
            out_specs=(pl.BlockSpec((tq, 256), lambda g, i, f: (i, g)),
                       pl.BlockSpec((256, s_len), lambda g, i, f: (g, 0)),
                       pl.BlockSpec((256, s_len), lambda g, i, f: (g, 0)))),
        compiler_params=_params(("parallel", "arbitrary")),
    )(first, q, k, v, do, lt)


def _mla_fwd_call(qn, qp, kn, kpt, v):
    s_len = qn.shape[0]
    tk = min(MLA_KEY_TILE, s_len)
    tq = min(FWD_Q_TILES * ATT_TILE, s_len)
    r = tq // tk
    nq = s_len // tq

    def body(qn_ref, qp_ref, kn_ref, kpt_ref, v_ref, o_ref, lse_ref):
        i = pl.program_id(1)
        qn2 = qn_ref[...]
        qp2 = qp_ref[...]
        lane256 = lax.broadcasted_iota(jnp.int32, (1, 256), 1)
        lane128 = lax.broadcasted_iota(jnp.int32, (1, 128), 1)
        krow = lax.broadcasted_iota(jnp.int32, (tk, tk), 0)
        kcol = lax.broadcasted_iota(jnp.int32, (tk, tk), 1)
        row = lax.broadcasted_iota(jnp.int32, (tq, tk), 0)
        col = lax.broadcasted_iota(jnp.int32, (tq, tk), 1)
        valids = [col + u * tk <= row for u in range(r)]
        m64s = [(lane256 // 64) == hh for hh in range(4)]
        half = [(lane128 // 64) == u for u in range(2)]
        m32s = [(lane128 // 32) == hh for hh in range(4)]
        qcs = []
        for hh in range(4):
            qpair = qn2[:, 128 * (hh // 2):128 * (hh // 2) + 128]
            qcs.append(jnp.concatenate([jnp.where(half[hh % 2], qpair, jnp.zeros_like(qpair)),
                                        jnp.where(m32s[hh], qp2, jnp.zeros_like(qp2))], axis=1))

        def by_head(vals):
            return jnp.where(m64s[0], vals[0], jnp.where(m64s[1], vals[1], jnp.where(m64s[2], vals[2], vals[3])))

        def block(j, carry, valid):
            ms, ls, acc = list(carry[0:4]), list(carry[4:8]), carry[8]
            off = pl.multiple_of(j * tk, tk)
            knb = kn_ref[pl.ds(off, tk), :]
            kpb = kpt_ref[pl.ds(off, tk), :]
            vb = v_ref[pl.ds(off, tk), :]
            kcs = [jnp.concatenate([knb[:, 128 * pp:128 * pp + 128], kpb], axis=1) for pp in range(2)]
            ps, alphas = [], []
            for hh in range(4):
                s = _dot_nt(qcs[hh], kcs[hh // 2]) * (MLA_SCALE * LOG2E)
                if valid is not None:
                    s = jnp.where(valid, s, -1e30)
                mn = jnp.maximum(ms[hh], jnp.max(s, axis=1, keepdims=True))
                p = jnp.exp2(s - mn)
                alpha = jnp.exp2(ms[hh] - mn)
                ls[hh] = alpha * ls[hh] + jnp.sum(p, axis=1, keepdims=True)
                ms[hh] = mn
                ps.append(p.astype(BF16))
                alphas.append(alpha)
            pvs = []
            for pp in range(2):
                vpair = vb[:, 128 * pp:128 * pp + 128]
                vstack = jnp.concatenate([jnp.where(hf, vpair, jnp.zeros_like(vpair)) for hf in half], axis=0)
                pvs.append(_dot(jnp.concatenate(ps[2 * pp:2 * pp + 2], axis=1), vstack))
            acc = by_head(alphas) * acc + jnp.concatenate(pvs, axis=1)
            return (*ms, *ls, acc)

        neg = jnp.full((tq, 1), -1e30, F32)
        zero = jnp.zeros((tq, 1), F32)
        carry = lax.fori_loop(0, i * r, lambda j, cr: block(j, cr, None),
                              (neg,) * 4 + (zero,) * 4 + (jnp.zeros((tq, 256), F32),))
        for u in range(r):
            carry = block(i * r + u, carry, valids[u])
        o_ref[...] = carry[8] / by_head(list(carry[4:8]))
        for hh in range(4):
            lse_ref[0, :, hh:hh + 1] = (carry[hh] + jnp.log2(carry[4 + hh])) * (1.0 / LOG2E)

    return pl.pallas_call(
        body, name="mla_fwd", grid=(2, nq),
        out_shape=(_sds((s_len, MLA_WIDTH), F32), _sds((2, s_len, 4), F32)),
        in_specs=[pl.BlockSpec((tq, 256), lambda g, i: (i, g)),
                  pl.BlockSpec((tq, 128), lambda g, i: (i, g)),
                  pl.BlockSpec((s_len, 256), lambda g, i: (0, g)),
                  pl.BlockSpec((s_len, 128), lambda g, i: (0, 0)),
                  pl.BlockSpec((s_len, 256), lambda g, i: (0, g))],
        out_specs=(pl.BlockSpec((tq, 256), lambda g, i: (i, g)),
                   pl.BlockSpec((1, tq, 4), lambda g, i: (g, i, 0))),
        compiler_params=_params(("parallel", "parallel")),
    )(qn, qp, kn, kpt, v)


def _mla_bwd_call(qn, qp, kn, kpt, v, o, do, lse):
    s_len = qn.shape[0]
    tk = min(MLA_KEY_TILE, s_len)
    tq = min(ATT_Q_TILES * ATT_TILE, s_len)
    r = tq // tk
    nq = s_len // tq

    def body(qn_ref, qp_ref, kn_ref, kpt_ref, v_ref, o_ref, do_ref, lse_ref,
             dqn_ref, dqp_ref, dkn_ref, dkpt_ref, dv_ref):
        g = pl.program_id(0)
        i = pl.program_id(1)

        @pl.when(i == 0)
        def _():
            dkn_ref[...] = jnp.zeros_like(dkn_ref)
            dv_ref[...] = jnp.zeros_like(dv_ref)

        @pl.when((i == 0) & (g == 0))
        def _():
            dkpt_ref[...] = jnp.zeros_like(dkpt_ref)

        qn2 = qn_ref[...]
        qp2 = qp_ref[...]
        dof = do_ref[...]
        dob = dof.astype(BF16)
        prod = dof * o_ref[...]
        lane256 = lax.broadcasted_iota(jnp.int32, (1, 256), 1)
        lane128 = lax.broadcasted_iota(jnp.int32, (1, 128), 1)
        krow = lax.broadcasted_iota(jnp.int32, (tk, tk), 0)
        kcol = lax.broadcasted_iota(jnp.int32, (tk, tk), 1)
        row = lax.broadcasted_iota(jnp.int32, (tq, tk), 0)
        col = lax.broadcasted_iota(jnp.int32, (tq, tk), 1)
        valids = [col + u * tk <= row for u in range(r)]
        m64s = [(lane256 // 64) == hh for hh in range(4)]
        half = [(lane128 // 64) == u for u in range(2)]
        m32s = [(lane128 // 32) == hh for hh in range(4)]
        qcs, doms = [], []
        for hh in range(4):
            sl = slice(128 * (hh // 2), 128 * (hh // 2) + 128)
            qpair = qn2[:, sl]
            dpair = dob[:, sl]
            qcs.append(jnp.concatenate([jnp.where(half[hh % 2], qpair, jnp.zeros_like(qpair)),
                                        jnp.where(m32s[hh], qp2, jnp.zeros_like(qp2))], axis=1))
            doms.append(jnp.where(half[hh % 2], dpair, jnp.zeros_like(dpair)))
        dsums = [jnp.sum(jnp.where(m64, prod, 0.0), axis=1, keepdims=True) * MLA_SCALE for m64 in m64s]
        lses = [lse_ref[0, :, hh:hh + 1] * LOG2E for hh in range(4)]
        qn2t = jnp.transpose(qn2.astype(F32))
        qp2t = jnp.transpose(qp2.astype(F32))
        do2t = jnp.transpose(dof)
        sub128 = lax.broadcasted_iota(jnp.int32, (128, 1), 0)
        qtstacks, dotstacks = [], []
        for pp in range(2):
            qts, dts = [], []
            for u in range(2):
                hh = 2 * pp + u
                qts.append(jnp.concatenate(
                    [jnp.where((sub128 // 64) == u, qn2t[128 * pp:128 * pp + 128, :], 0.0),
                     jnp.where((sub128 // 32) == hh, qp2t, 0.0)], axis=0).astype(BF16))
                dts.append(jnp.where((sub128 // 64) == u, do2t[128 * pp:128 * pp + 128, :], 0.0).astype(BF16))
            qtstacks.append(jnp.concatenate(qts, axis=1))
            dotstacks.append(jnp.concatenate(dts, axis=1))

        def block(j, carry, valid):
            dqn, dqp = carry
            off = pl.multiple_of(j * tk, tk)
            knb = kn_ref[pl.ds(off, tk), :]
            kpb = kpt_ref[pl.ds(off, tk), :]
            vb = v_ref[pl.ds(off, tk), :]
            dqn_parts = []
            dkp = None
            for pp in range(2):
                sl = slice(128 * pp, 128 * pp + 128)
                knp = knb[:, sl]
                vpair = vb[:, sl]
                kc = jnp.concatenate([knp, kpb], axis=1)
                dss, pbs, kcms = [], [], []
                for u in range(2):
                    hh = 2 * pp + u
                    s = _dot_nt(qcs[hh], kc) * (MLA_SCALE * LOG2E)
                    if valid is not None:
                        s = jnp.where(valid, s, -1e30)
                    p = jnp.exp2(s - lses[hh])
                    ds = p * (_dot_nt(doms[hh], vpair) * MLA_SCALE - dsums[hh])
                    dss.append(ds.astype(BF16))
                    pbs.append(p.astype(BF16))
                    kcms.append(jnp.concatenate([jnp.where(half[u], knp, jnp.zeros_like(knp)),
                                                 jnp.where(m32s[hh], kpb, jnp.zeros_like(kpb))], axis=1))
                dqc = _dot(jnp.concatenate(dss, axis=1), jnp.concatenate(kcms, axis=0))
                dqn_parts.append(dqc[:, :128])
                dqp = dqp + dqc[:, 128:]
                dkc = _dot(qtstacks[pp], jnp.concatenate(dss, axis=0))
                dkn_ref[128 * pp:128 * pp + 128, pl.ds(off, tk)] += dkc[:128, :]
                dkp = dkc[128:, :] if dkp is None else dkp + dkc[128:, :]
                dv_ref[128 * pp:128 * pp + 128, pl.ds(off, tk)] += _dot(dotstacks[pp], jnp.concatenate(pbs, axis=0))
            dqn = dqn + jnp.concatenate(dqn_parts, axis=1)
            dkpt_ref[:, pl.ds(off, tk)] += dkp
            return dqn, dqp

        carry = lax.fori_loop(0, i * r, lambda j, cr: block(j, cr, None),
                              (jnp.zeros((tq, 256), F32), jnp.zeros((tq, 128), F32)))
        for u in range(r):
            carry = block(i * r + u, carry, valids[u])
        dqn, dqp = carry
        dqn_ref[...] = dqn.astype(BF16)
        dqp_ref[...] = dqp.astype(BF16)

    return pl.pallas_call(
        body, name="mla_bwd", grid=(2, nq),
        out_shape=(_sds((s_len, 512), BF16), _sds((s_len, 256), BF16), _sds((512, s_len), F32),
                   _sds((128, s_len), F32), _sds((512, s_len), F32)),
        in_specs=[pl.BlockSpec((tq, 256), lambda g, i: (i, g)),
                  pl.BlockSpec((tq, 128), lambda g, i: (i, g)),
                  pl.BlockSpec((s_len, 256), lambda g, i: (0, g)),
                  pl.BlockSpec((s_len, 128), lambda g, i: (0, 0)),
                  pl.BlockSpec((s_len, 256), lambda g, i: (0, g)),
                  pl.BlockSpec((tq, 256), lambda g, i: (i, g)),
                  pl.BlockSpec((tq, 256), lambda g, i: (i, g)),
                  pl.BlockSpec((1, tq, 4), lambda g, i: (g, i, 0))],
        out_specs=(pl.BlockSpec((tq, 256), lambda g, i: (i, g)),
                   pl.BlockSpec((tq, 128), lambda g, i: (i, g)),
                   pl.BlockSpec((256, s_len), lambda g, i: (g, 0)),
                   pl.BlockSpec((128, s_len), lambda g, i: (0, 0)),
                   pl.BlockSpec((256, s_len), lambda g, i: (g, 0))),
        compiler_params=_params(("arbitrary", "arbitrary")),
    )(qn, qp, kn, kpt, v, o, do, lse)


def _post_call(x, tgt, oa, ob, sz, mz, ga, gb, gate, gf, wa, wb, wo):
    s_len = x.shape[0]
    tm = min(ROW_TILE, s_len)
    nstep = s_len // tm

    def body(x_ref, t_ref, oa_ref, ob_ref, sz_ref, mz_ref, ga_ref, gb_ref, gate_ref, gf_ref,
             wa_ref, wb_ref, wo_ref,
             dx2_ref, doa_ref, dob_ref, dsz_ref, dmz_ref, dga_ref, dgb_ref,
             dwo_out, dwa_out, dwb_out, dgf_ref, dgate_ref, loss_ref, dwo_ref, dwa_ref, dwb_ref):
        @pl.when(pl.program_id(0) == 0)
        def _():
            dwo_ref[...] = jnp.zeros_like(dwo_ref)
            dwa_ref[...] = jnp.zeros_like(dwa_ref)
            dwb_ref[...] = jnp.zeros_like(dwb_ref)
            dgf_ref[...] = jnp.zeros_like(dgf_ref)
            dgate_ref[...] = jnp.zeros_like(dgate_ref)
            loss_ref[...] = jnp.zeros_like(loss_ref)

        gate = gate_ref[...]
        gf = gf_ref[...]
        oa = oa_ref[...]
        ob = ob_ref[...]
        sz = sz_ref[...]
        mz = mz_ref[...]
        sa = _sigmoid(sz)
        sb = _sigmoid(mz)
        silu_a = sz * sa
        silu_b = mz * sb
        ua = (oa * silu_a).astype(BF16)
        ub = (ob * silu_b).astype(BF16)
        ya = _dot(ua, wa_ref[...])
        yb = _dot(ub, wb_ref[...])
        sga = _sigmoid(ga_ref[...])
        sgb = _sigmoid(gb_ref[...])
        merged = (sga * ya + sgb * yb).astype(BF16)
        out = _dot(merged, wo_ref[...])
        x2 = x_ref[...] + gate * out
        r2 = lax.rsqrt(jnp.mean(x2 * x2, axis=-1, keepdims=True) + EPS)
        xhat = x2 * r2
        err = xhat * gf - t_ref[...]
        loss_ref[...] += 0.5 * jnp.sum(jnp.sum(err * err, axis=1, keepdims=True), axis=0, keepdims=True) / D_MODEL
        dy = err * (1.0 / D_MODEL)
        dgf_ref[...] += jnp.sum(dy * xhat, axis=0, keepdims=True)
        dxhat = dy * gf
        dx2 = r2 * (dxhat - xhat * jnp.mean(dxhat * xhat, axis=-1, keepdims=True))
        dx2_ref[...] = dx2
        dgate_ref[...] += jnp.sum(dx2 * out, axis=0, keepdims=True)
        dout = (dx2 * gate).astype(BF16)
        dmerged = _dot_nt(dout, wo_ref[...])
        dwo_ref[...] += _dot_tn(merged, dout)
        dya = dmerged * sga
        dyb = dmerged * sgb
        dga_ref[...] = (dya * ya * (1.0 - sga)).astype(BF16)
        dgb_ref[...] = (dyb * yb * (1.0 - sgb)).astype(BF16)
        dyab = dya.astype(BF16)
        dybb = dyb.astype(BF16)
        dua = _dot_nt(dyab, wa_ref[...])
        dub = _dot_nt(dybb, wb_ref[...])
        dwa_ref[...] += _dot_tn(ua, dyab)
        dwb_ref[...] += _dot_tn(ub, dybb)
        doa_ref[...] = dua * silu_a
        dob_ref[...] = dub * silu_b
        dsz_ref[...] = (dua * oa * (sa * (1.0 + sz * (1.0 - sa)))).astype(BF16)
        dmz_ref[...] = (dub * ob * (sb * (1.0 + mz * (1.0 - sb)))).astype(BF16)

        @pl.when(pl.program_id(0) == nstep - 1)
        def _():
            dwo_out[...] = dwo_ref[...].astype(BF16)
            for k in range(N_CHIPS):
                dwa_out[k] = dwa_ref[:, 256 * k:256 * k + 256].astype(BF16)
                dwb_out[k] = dwb_ref[:, 256 * k:256 * k + 256].astype(BF16)

    return pl.pallas_call(
        body, name="post", grid=(nstep,),
        out_shape=(_sds((s_len, D_MODEL), F32), _sds((s_len, 512), F32), _sds((s_len, 512), F32),
                   _sds((s_len, 512), BF16), _sds((s_len, 512), BF16),
                   _sds((s_len, D_MODEL), BF16), _sds((s_len, D_MODEL), BF16),
                   _sds((D_MODEL, D_MODEL), BF16), _sds((N_CHIPS, 512, 256), BF16), _sds((N_CHIPS, 512, 256), BF16),
                   _sds((1, D_MODEL), F32), _sds((1, D_MODEL), F32), _sds((1, 128), F32)),
        in_specs=[_rows(tm, D_MODEL), _rows(tm, D_MODEL), _rows(tm, 512), _rows(tm, 512), _rows(tm, 512),
                  _rows(tm, 512), _rows(tm, D_MODEL), _rows(tm, D_MODEL), _whole((1, D_MODEL)), _whole((1, D_MODEL)),
                  _whole((512, D_MODEL)), _whole((512, D_MODEL)), _whole((D_MODEL, D_MODEL))],
        out_specs=(_rows(tm, D_MODEL), _rows(tm, 512), _rows(tm, 512), _rows(tm, 512), _rows(tm, 512),
                   _rows(tm, D_MODEL), _rows(tm, D_MODEL),
                   _whole((D_MODEL, D_MODEL)), _whole((N_CHIPS, 512, 256)), _whole((N_CHIPS, 512, 256)),
                   _whole((1, D_MODEL)), _whole((1, D_MODEL)), _whole((1, 128))),
        scratch_shapes=[pltpu.VMEM((D_MODEL, D_MODEL), F32), pltpu.VMEM((512, D_MODEL), F32),
                        pltpu.VMEM((512, D_MODEL), F32)],
        compiler_params=_params(("arbitrary",)),
    )(x, tgt, oa, ob, sz, mz, ga, gb, gate, gf, wa, wb, wo)


def _bwdprep_call(dsq, dsk, dsv, dsz, dqn, dqp, dkn, dvv, dkpt, dmz, dga, dgb, cq, ckv, cos128, sin128,
                  qg, kvg, w_q, w_kv):
    s_len = cq.shape[0]
    tm = min(ROW_TILE, s_len)

    def body(dsq_ref, dsk_ref, dsv_ref, dsz_ref, dqn_ref, dqp_ref, dkn_ref, dvv_ref, dkpt_ref, dmz_ref,
             dga_ref, dgb_ref, cq_ref, ckv_ref, cos_ref, sin_ref, qg_ref, kvg_ref, wq_ref, wkv_ref,
             dp_ref, dwq_ref, dwkv_ref, dqg_ref, dkvg_ref):
        @pl.when(pl.program_id(0) == 0)
        def _():
            dwq_ref[...] = jnp.zeros_like(dwq_ref)
            dwkv_ref[...] = jnp.zeros_like(dwkv_ref)
            dqg_ref[...] = jnp.zeros_like(dqg_ref)
            dkvg_ref[...] = jnp.zeros_like(dkvg_ref)

        cos = cos_ref[...]
        sin = sin_ref[...]
        dp_ref[:, O_SQ:O_SK] = dsq_ref[...] * jnp.asarray(SB_SCALE, BF16)
        dp_ref[:, O_SK:O_SV] = jnp.transpose(dsk_ref[...]).astype(BF16)
        dp_ref[:, O_SV:O_SZ] = jnp.transpose(dsv_ref[...]).astype(BF16)
        dp_ref[:, O_SZ:O_CQ] = dsz_ref[...]
        dp_ref[:, O_MZ:O_GA] = dmz_ref[...]
        dp_ref[:, O_GA:O_GB] = dga_ref[...]
        dp_ref[:, O_GB:O_KR] = dgb_ref[...]
        dkp = jnp.transpose(dkpt_ref[...])
        dp_ref[:, O_KR:O_KR + 128] = (dkp * cos).astype(BF16)
        dp_ref[:, O_KR + 128:O_END] = (dkp * sin).astype(BF16)
        dp_ref[:, O_END:W_INT] = jnp.zeros((tm, W_INT - O_END), BF16)

        cq = cq_ref[...]
        rq = lax.rsqrt(jnp.mean(cq * cq, axis=-1, keepdims=True) + EPS)
        cqh = cq * rq
        qg = qg_ref[...]
        cqn = (cqh * qg).astype(BF16)
        dqp = dqp_ref[...].astype(F32)
        dqa = jnp.concatenate([dqn_ref[...], (dqp * jnp.tile(cos, (1, 2))).astype(BF16),
                               (dqp * jnp.tile(sin, (1, 2))).astype(BF16)], axis=1)
        dcqn = _dot_nt(dqa, wq_ref[...])
        dwq_ref[...] += _dot_tn(cqn, dqa)
        dqg_ref[...] += jnp.sum(dcqn * cqh, axis=0, keepdims=True)
        dh = dcqn * qg
        dcq = rq * (dh - cqh * jnp.mean(dh * cqh, axis=-1, keepdims=True))
        dp_ref[:, O_CQ:O_CKV] = dcq.astype(BF16)

        ckv = ckv_ref[...]
        rk = lax.rsqrt(jnp.mean(ckv * ckv, axis=-1, keepdims=True) + EPS)
        ckh = ckv * rk
        kvg = kvg_ref[...]
        ckvn = (ckh * kvg).astype(BF16)
        dkva = jnp.concatenate([jnp.transpose(dkn_ref[...]).astype(BF16),
                                jnp.transpose(dvv_ref[...]).astype(BF16)], axis=1)
        dckvn = _dot_nt(dkva, wkv_ref[...])
        dwkv_ref[...] += _dot_tn(ckvn, dkva)
        dkvg_ref[...] += jnp.sum(dckvn * ckh, axis=0, keepdims=True)
        dh2 = dckvn * kvg
        dckv = rk * (dh2 - ckh * jnp.mean(dh2 * ckh, axis=-1, keepdims=True))
        dp_ref[:, O_CKV:O_MZ] = dckv.astype(BF16)

    return pl.pallas_call(
        body, name="bwdprep", grid=(s_len // tm,),
        out_shape=(_sds((s_len, W_INT), BF16), _sds((Q_RANK, 1024), F32), _sds((KV_RANK, 1024), F32),
                   _sds((1, Q_RANK), F32), _sds((1, KV_RANK), F32)),
        in_specs=[_rows(tm, 512), _cols(512, tm), _cols(512, tm), _rows(tm, 512), _rows(tm, 512), _rows(tm, 256),
                  _cols(512, tm), _cols(512, tm), _cols(128, tm), _rows(tm, 512), _rows(tm, D_MODEL),
                  _rows(tm, D_MODEL), _rows(tm, Q_RANK), _rows(tm, KV_RANK), _rows(tm, LANES), _rows(tm, LANES),
                  _whole((1, Q_RANK)), _whole((1, KV_RANK)), _whole((Q_RANK, 1024)), _whole((KV_RANK, 1024))],
        out_specs=(_rows(tm, W_INT), _whole((Q_RANK, 1024)), _whole((KV_RANK, 1024)),
                   _whole((1, Q_RANK)), _whole((1, KV_RANK))),
        compiler_params=_params(("arbitrary",)),
    )(dsq, dsk, dsv, dsz, dqn, dqp, dkn, dvv, dkpt, dmz, dga, dgb, cq, ckv, cos128, sin128, qg, kvg, w_q, w_kv)


def _dh_call(dproj, w_in_t, w_kr, x, dx2, scale, g1):
    s_len = x.shape[0]
    tm = min(2 * ROW_TILE, s_len)
    parts = [(a, b) + _weight_rows(a, b) for a, b in ((0, O_MZ), (O_MZ, O_KR), (O_KR, O_END))]

    def body(dp_ref, win_ref, wkr_ref, x_ref, dx2_ref, sc_ref, g1_ref, gx_ref, dsh_ref, dsc_ref, dg1_ref):
        @pl.when(pl.program_id(0) == 0)
        def _():
            dsh_ref[...] = jnp.zeros_like(dsh_ref)
            dsc_ref[...] = jnp.zeros_like(dsc_ref)
            dg1_ref[...] = jnp.zeros_like(dg1_ref)

        def half(k):
            return sum(_dot(dp_ref[:, a:b], (wkr_ref if from_kr else win_ref)[k, rows, :])
                       for a, b, from_kr, rows in parts)

        dh = jnp.concatenate([half(0), half(1)], axis=1)
        xt = x_ref[...]
        r = lax.rsqrt(jnp.mean(xt * xt, axis=-1, keepdims=True) + EPS)
        xh = xt * r
        g1 = g1_ref[...]
        xg = xh * g1
        dsh_ref[...] += jnp.sum(dh, axis=0, keepdims=True)
        dsc_ref[...] += jnp.sum(dh * xg, axis=0, keepdims=True)
        dxg = dh * (1.0 + sc_ref[...])
        dg1_ref[...] += jnp.sum(dxg * xh, axis=0, keepdims=True)
        dxh = dxg * g1
        gx_ref[...] = dx2_ref[...] + r * (dxh - xh * jnp.mean(dxh * xh, axis=-1, keepdims=True))

    return pl.pallas_call(
        body, name="dh", grid=(s_len // tm,),
        out_shape=(_sds((s_len, D_MODEL), F32), _sds((1, D_MODEL), F32), _sds((1, D_MODEL), F32),
                   _sds((1, D_MODEL), F32)),
        in_specs=[_rows(tm, W_INT), _whole((2, IN_WIDTH, HALF_D)), _whole((2, O_END - O_KR, HALF_D)),
                  _rows(tm, D_MODEL), _rows(tm, D_MODEL), _whole((1, D_MODEL)), _whole((1, D_MODEL))],
        out_specs=(_rows(tm, D_MODEL), _whole((1, D_MODEL)), _whole((1, D_MODEL)), _whole((1, D_MODEL))),
        compiler_params=_params(("arbitrary",)),
    )(dproj, w_in_t, w_kr, x, dx2, scale, g1)


def _small_call(svg, ct, dmod_sh):
    def body(sv_ref, ct_ref, dm_ref, tot_ref, gwada_ref):
        acc = sv_ref[0:1, :]
        for d in range(1, N_DEV):
            acc = acc + sv_ref[d:d + 1, :]
        tot_ref[...] = acc
        gwada_ref[...] = lax.dot_general(ct_ref[...], dm_ref[...], (((1,), (0,)), ((), ())),
                                         precision=lax.Precision.HIGHEST, preferred_element_type=F32)

    vmem = pl.BlockSpec(memory_space=pltpu.VMEM)
    return pl.pallas_call(
        body, name="small_grads",
        out_shape=(_sds((1, 8 * SV_COLS), F32), _sds((D_MODEL, 768), F32)),
        in_specs=[vmem, vmem, vmem], out_specs=(vmem, vmem),
        compiler_params=_params(),
    )(svg, ct, dmod_sh)


def _adamw_tile_rows(rows, cols):
    budget = 2 << 20
    if rows * cols * 4 <= budget or rows % 8:
        return rows
    best = 8
    for tr in range(8, rows + 1, 8):
        if rows % tr == 0 and tr * cols * 4 <= budget:
            best = tr
    return best


def _adamw_math(w, gg, m, v):
    m2 = ADAM_B1 * m + (1.0 - ADAM_B1) * gg
    v2 = ADAM_B2 * v + (1.0 - ADAM_B2) * (gg * gg)
    m_hat = m2 / (1.0 - ADAM_B1 ** ADAM_STEP)
    v_hat = v2 / (1.0 - ADAM_B2 ** ADAM_STEP)
    return -ADAM_LR * (m_hat / (jnp.sqrt(v_hat) + ADAM_EPS) + ADAM_WD * w), m2, v2


def _adamw_call(name, w, g, m, v):
    rows, cols = w.shape
    tr = _adamw_tile_rows(rows, cols)
    halves = g.ndim == 3

    def body(w_ref, g_ref, m_ref, v_ref, *out_refs):
        if halves:
            gg = jnp.concatenate([g_ref[0], g_ref[1]], axis=1)
            out_refs[0][...] = gg
        else:
            gg = g_ref[...]
        d_ref, nm_ref, nv_ref = out_refs[-3:]
        d_ref[...], nm_ref[...], nv_ref[...] = _adamw_math(w_ref[...], gg, m_ref[...], v_ref[...])

    spec = pl.BlockSpec((tr, cols), lambda i: (i, 0))
    g_spec = pl.BlockSpec((2, tr, cols // 2), lambda i: (0, i, 0)) if halves else spec
    n_out = 4 if halves else 3
    outs = pl.pallas_call(
        body, name="adamw_" + name, grid=(rows // tr,),
        out_shape=(_sds((rows, cols), F32),) * n_out,
        in_specs=[spec, g_spec, spec, spec], out_specs=(spec,) * n_out,
        compiler_params=_params(("parallel",)),
    )(w, g, m, v)
    return tuple(outs) if halves else (g,) + tuple(outs)


def _pack_vectors_call(vectors):
    used = sum(v.shape[1] for v in vectors)

    def body(*refs):
        out_ref = refs[-1]
        row = jnp.concatenate([r[...] for r in refs[:-1]] + [jnp.zeros((1, 8 * SV_COLS - used), F32)], axis=1)
        for r in range(8):
            out_ref[r:r + 1, :] = row[:, SV_COLS * r:SV_COLS * (r + 1)]

    vmem = pl.BlockSpec(memory_space=pltpu.VMEM)
    return pl.pallas_call(
        body, name="pack_vectors", out_shape=_sds((8, SV_COLS), F32),
        in_specs=[vmem] * len(vectors), out_specs=vmem, compiler_params=_params(),
    )(*vectors)


def _adamw_vectors_call(tot, offsets, ws, ms, vs):
    nvec = len(ws)

    def body(tot_ref, *refs):
        ins, outs = refs[:3 * nvec], refs[3 * nvec:]
        for k in range(nvec):
            w_ref, m_ref, v_ref = ins[3 * k:3 * k + 3]
            g_ref, d_ref, nm_ref, nv_ref = outs[4 * k:4 * k + 4]
            gg = tot_ref[:, offsets[k]:offsets[k] + w_ref.shape[1]]
            g_ref[...] = gg
            d_ref[...], nm_ref[...], nv_ref[...] = _adamw_math(w_ref[...], gg, m_ref[...], v_ref[...])

    vmem = pl.BlockSpec(memory_space=pltpu.VMEM)
    outs = pl.pallas_call(
        body, name="adamw_vectors",
        out_shape=tuple(_sds(w.shape, F32) for w in ws for _ in range(4)),
        in_specs=[vmem] * (1 + 3 * nvec), out_specs=(vmem,) * (4 * nvec),
        compiler_params=_params(),
    )(tot, *[a for w, m, v in zip(ws, ms, vs) for a in (w, m, v)])
    return [tuple(outs[4 * k:4 * k + 4]) for k in range(nvec)]


IN_SHARD = IN_WIDTH // N_CHIPS
HALF_D = D_MODEL // 2
SMALL_ROWS = (576, 512, 1024, 1024, 2048)
SMALL_TOTAL = sum(SMALL_ROWS)
SMALL_HALF = SMALL_TOTAL // 2
SMALL_SUM_ROWS = 432


def _gather_call(c_row, w_ada_sh, pack_in, pack_small):
    def body(c_ref, wada_ref, pki_ref, pks_ref, mg_ref, cg_ref, gwi_ref, gws_ref,
             cv, ssem_c, rsem_c, ssem_m, rsem_m, ssem_w, rsem_w, ssem_f, rsem_f, lsem):
        x, y, c = lax.axis_index("x"), lax.axis_index("y"), lax.axis_index("c")
        me = 4 * x + 2 * y + c
        chip = 2 * x + y
        rel3 = [(1, 0), (0, 1), (1, 1)]
        packs = [(pki_ref, gwi_ref), (pks_ref, gws_ref)]

        def slot(a, gw, k, h):
            return gw.at[h, k] if a == 0 else gw.at[k, h]

        sends = []
        for j, (dx, dy) in enumerate(rel3):
            for a, (pk, gw) in enumerate(packs):
                cp = pltpu.make_async_remote_copy(
                    src_ref=pk.at[c], dst_ref=slot(a, gw, chip, c), send_sem=ssem_w.at[j, a],
                    recv_sem=rsem_w.at[j, a], device_id=(_flip(x, dx), _flip(y, dy), c), device_id_type=MESH)
                cp.start()
                sends.append(cp)
        owns = []
        for a, (pk, gw) in enumerate(packs):
            for h in range(2):
                own = pltpu.make_async_copy(pk.at[h], slot(a, gw, chip, h), lsem.at[a, h])
                own.start()
                owns.append(own)

        cv[me] = c_ref[...]
        for r in range(1, N_DEV):
            dx, dy, dc = (r >> 2) & 1, (r >> 1) & 1, r & 1
            cp = pltpu.make_async_remote_copy(
                src_ref=c_ref, dst_ref=cv.at[me], send_sem=ssem_c.at[r - 1], recv_sem=rsem_c.at[r - 1],
                device_id=(_flip(x, dx), _flip(y, dy), _flip(c, dc)), device_id_type=MESH)
            cp.start()
            sends.append(cp)
        for r in range(1, N_DEV):
            dx, dy, dc = (r >> 2) & 1, (r >> 1) & 1, r & 1
            src = 4 * _flip(x, dx) + 2 * _flip(y, dy) + _flip(c, dc)
            pltpu.make_async_remote_copy(
                src_ref=c_ref, dst_ref=cv.at[src], send_sem=ssem_c.at[r - 1], recv_sem=rsem_c.at[r - 1],
                device_id=(x, y, c), device_id_type=MESH).wait_recv()
        rows = lax.broadcasted_iota(jnp.int32, (N_DEV, D_MODEL), 0)
        call = jnp.zeros((N_DEV, D_MODEL), F32)
        for b in range(N_DEV):
            call = jnp.where(rows == b, jnp.broadcast_to(cv[b], (N_DEV, D_MODEL)), call)
        cg_ref[...] = call

        mg_ref[chip] = lax.dot_general(call, wada_ref[...], (((1,), (0,)), ((), ())),
                                       precision=lax.Precision.HIGHEST, preferred_element_type=F32)
        for j, (dx, dy) in enumerate(rel3):
            cp = pltpu.make_async_remote_copy(
                src_ref=mg_ref.at[chip], dst_ref=mg_ref.at[chip], send_sem=ssem_m.at[j], recv_sem=rsem_m.at[j],
                device_id=(_flip(x, dx), _flip(y, dy), c), device_id_type=MESH)
            cp.start()
            sends.append(cp)
        for j, (dx, dy) in enumerate(rel3):
            src_chip = 2 * _flip(x, dx) + _flip(y, dy)
            pltpu.make_async_remote_copy(
                src_ref=mg_ref.at[src_chip], dst_ref=mg_ref.at[src_chip], send_sem=ssem_m.at[j],
                recv_sem=rsem_m.at[j], device_id=(x, y, c), device_id_type=MESH).wait_recv()
        for j, (dx, dy) in enumerate(rel3):
            src_chip = 2 * _flip(x, dx) + _flip(y, dy)
            for a, (pk, gw) in enumerate(packs):
                pltpu.make_async_remote_copy(
                    src_ref=pk.at[c], dst_ref=slot(a, gw, src_chip, c), send_sem=ssem_w.at[j, a],
                    recv_sem=rsem_w.at[j, a], device_id=(x, y, c), device_id_type=MESH).wait_recv()
                cp = pltpu.make_async_remote_copy(
                    src_ref=slot(a, gw, src_chip, c), dst_ref=slot(a, gw, src_chip, c), send_sem=ssem_f.at[j, a],
                    recv_sem=rsem_f.at[j, a], device_id=(x, y, 1 - c), device_id_type=MESH)
                cp.start()
                sends.append(cp)
        for j, (dx, dy) in enumerate(rel3):
            src_chip = 2 * _flip(x, dx) + _flip(y, dy)
            for a, (pk, gw) in enumerate(packs):
                pltpu.make_async_remote_copy(
                    src_ref=pk.at[c], dst_ref=slot(a, gw, src_chip, 1 - c), send_sem=ssem_f.at[j, a],
                    recv_sem=rsem_f.at[j, a], device_id=(x, y, c), device_id_type=MESH).wait_recv()
        for cp in sends:
            cp.wait_send()
        for own in owns:
            own.wait()

    vmem = pl.BlockSpec(memory_space=pltpu.VMEM)
    return pl.pallas_call(
        body, name="gather_fwd",
        out_shape=(_sds((N_CHIPS, N_DEV, 768), F32), _sds((N_DEV, D_MODEL), F32),
                   _sds((2, N_CHIPS, IN_SHARD, HALF_D), BF16), _sds((N_CHIPS, 2, SMALL_HALF, LANES), BF16)),
        in_specs=[vmem, vmem, vmem, vmem], out_specs=(vmem, vmem, vmem, vmem),
        scratch_shapes=[
            pltpu.VMEM((N_DEV, 1, D_MODEL), F32),
            pltpu.SemaphoreType.DMA((N_DEV - 1,)), pltpu.SemaphoreType.DMA((N_DEV - 1,)),
            pltpu.SemaphoreType.DMA((3,)), pltpu.SemaphoreType.DMA((3,)),
            pltpu.SemaphoreType.DMA((3, 2)), pltpu.SemaphoreType.DMA((3, 2)),
            pltpu.SemaphoreType.DMA((3, 2)), pltpu.SemaphoreType.DMA((3, 2)),
            pltpu.SemaphoreType.DMA((2, 2)),
        ],
        compiler_params=_params(),
    )(c_row, w_ada_sh, pack_in, pack_small)


def _reduce_call(g_in, g_small, sv):
    def body(gi_ref, gs_ref, sv_ref, fi_ref, fs_ref, svg_ref, pair_i, pair_s, send_i, send_s, land_i, land_s,
             ssem_p, rsem_p, ssem_g, rsem_g, ssem_s, rsem_s, ssem_x, rsem_x):
        x, y, c = lax.axis_index("x"), lax.axis_index("y"), lax.axis_index("c")
        me = 4 * x + 2 * y + c
        chip = 2 * x + y
        rel3 = [(1, 0), (0, 1), (1, 1)]
        payloads = [(gi_ref, pair_i, send_i, land_i, fi_ref), (gs_ref, pair_s, send_s, land_s, fs_ref)]
        where = [lambda k, h: N_CHIPS * h + k, lambda k, h: 2 * k + h]
        copies = []

        for k in range(N_CHIPS):
            for a, (g, pair, _, _, _) in enumerate(payloads):
                cp = pltpu.make_async_remote_copy(
                    src_ref=g.at[where[a](k, 1 - c)], dst_ref=pair.at[k], send_sem=ssem_p.at[k, a],
                    recv_sem=rsem_p.at[k, a], device_id=(x, y, 1 - c), device_id_type=MESH)
                cp.start()
                copies.append(cp)

        for r in range(1, N_DEV):
            dx, dy, dc = (r >> 2) & 1, (r >> 1) & 1, r & 1
            cp = pltpu.make_async_remote_copy(
                src_ref=sv_ref, dst_ref=svg_ref.at[me], send_sem=ssem_s.at[r - 1], recv_sem=rsem_s.at[r - 1],
                device_id=(_flip(x, dx), _flip(y, dy), _flip(c, dc)), device_id_type=MESH)
            cp.start()
            copies.append(cp)
        svg_ref[me] = sv_ref[...]

        def pair_sum(k, store_in, store_small):
            for a, (g, pair, _, _, _) in enumerate(payloads):
                pltpu.make_async_remote_copy(
                    src_ref=g.at[where[a](k, c)], dst_ref=pair.at[k], send_sem=ssem_p.at[k, a],
                    recv_sem=rsem_p.at[k, a], device_id=(x, y, c), device_id_type=MESH).wait_recv()
            for qd in range(HALF_D // LANES):
                sl = slice(LANES * qd, LANES * qd + LANES)
                store_in(sl, gi_ref[where[0](k, c), :, sl].astype(F32) + pair_i[k, :, sl].astype(F32))

            def rows(i, carry):
                sl = pl.ds(pl.multiple_of(i * SMALL_SUM_ROWS, 16), SMALL_SUM_ROWS)
                store_small(sl, gs_ref[where[1](k, c), sl, :].astype(F32) + pair_s[k, sl, :].astype(F32))
                return carry

            lax.fori_loop(0, SMALL_HALF // SMALL_SUM_ROWS, rows, 0)

        for j, (dx, dy) in enumerate(rel3):
            tx, ty = _flip(x, dx), _flip(y, dy)

            def put_in(sl, val, j=j):
                send_i[j, :, sl] = val.astype(BF16)

            def put_small(sl, val, j=j):
                send_s[j, sl, :] = val.astype(BF16)

            pair_sum(2 * tx + ty, put_in, put_small)
            for a, (_, _, send, land, _) in enumerate(payloads):
                cp = pltpu.make_async_remote_copy(
                    src_ref=send.at[j], dst_ref=land.at[j], send_sem=ssem_g.at[j, a], recv_sem=rsem_g.at[j, a],
                    device_id=(tx, ty, c), device_id_type=MESH)
                cp.start()
                copies.append(cp)

        def own_in(sl, val):
            fi_ref[c, :, sl] = val

        def own_small(sl, val):
            fs_ref[c, sl, :] = val

        pair_sum(chip, own_in, own_small)
        for j in range(3):
            for a, (_, _, send, land, _) in enumerate(payloads):
                pltpu.make_async_remote_copy(
                    src_ref=send.at[j], dst_ref=land.at[j], send_sem=ssem_g.at[j, a], recv_sem=rsem_g.at[j, a],
                    device_id=(x, y, c), device_id_type=MESH).wait_recv()
            for qd in range(HALF_D // LANES):
                sl = slice(LANES * qd, LANES * qd + LANES)
                fi_ref[c, :, sl] += land_i[j, :, sl].astype(F32)

            def add_rows(i, carry, j=j):
                sl = pl.ds(pl.multiple_of(i * SMALL_SUM_ROWS, 16), SMALL_SUM_ROWS)
                fs_ref[c, sl, :] += land_s[j, sl, :].astype(F32)
                return carry

            lax.fori_loop(0, SMALL_HALF // SMALL_SUM_ROWS, add_rows, 0)

        for a, f in enumerate((fi_ref, fs_ref)):
            cp = pltpu.make_async_remote_copy(
                src_ref=f.at[c], dst_ref=f.at[c], send_sem=ssem_x.at[a], recv_sem=rsem_x.at[a],
                device_id=(x, y, 1 - c), device_id_type=MESH)
            cp.start()
            copies.append(cp)
        for a, f in enumerate((fi_ref, fs_ref)):
            pltpu.make_async_remote_copy(
                src_ref=f.at[c], dst_ref=f.at[1 - c], send_sem=ssem_x.at[a], recv_sem=rsem_x.at[a],
                device_id=(x, y, c), device_id_type=MESH).wait_recv()
        for r in range(1, N_DEV):
            dx, dy, dc = (r >> 2) & 1, (r >> 1) & 1, r & 1
            src = 4 * _flip(x, dx) + 2 * _flip(y, dy) + _flip(c, dc)
            pltpu.make_async_remote_copy(
                src_ref=sv_ref, dst_ref=svg_ref.at[src], send_sem=ssem_s.at[r - 1],
                recv_sem=rsem_s.at[r - 1], device_id=(x, y, c), device_id_type=MESH).wait_recv()
        for cp in copies:
            cp.wait_send()

    vmem = pl.BlockSpec(memory_space=pltpu.VMEM)
    return pl.pallas_call(
        body, name="grad_reduce",
        out_shape=(_sds((2, IN_SHARD, HALF_D), F32), _sds((2, SMALL_HALF, LANES), F32),
                   _sds((N_DEV, 8, SV_COLS), F32)),
        in_specs=[vmem, vmem, vmem], out_specs=(vmem, vmem, vmem),
        scratch_shapes=[
            pltpu.VMEM((N_CHIPS, IN_SHARD, HALF_D), BF16), pltpu.VMEM((N_CHIPS, SMALL_HALF, LANES), BF16),
            pltpu.VMEM((3, IN_SHARD, HALF_D), BF16), pltpu.VMEM((3, SMALL_HALF, LANES), BF16),
            pltpu.VMEM((3, IN_SHARD, HALF_D), BF16), pltpu.VMEM((3, SMALL_HALF, LANES), BF16),
            pltpu.SemaphoreType.DMA((N_CHIPS, 2)), pltpu.SemaphoreType.DMA((N_CHIPS, 2)),
            pltpu.SemaphoreType.DMA((3, 2)), pltpu.SemaphoreType.DMA((3, 2)),
            pltpu.SemaphoreType.DMA((N_DEV - 1,)), pltpu.SemaphoreType.DMA((N_DEV - 1,)),
            pltpu.SemaphoreType.DMA((2,)), pltpu.SemaphoreType.DMA((2,)),
        ],
        compiler_params=_params(),
    )(g_in, g_small, sv)


def _dwin_call(h, dproj):
    s_len = h.shape[0]
    tm = min(4 * ROW_TILE, s_len)
    nrow = s_len // tm
    nc = 2
    chunk = W_INT // nc

    def body(h_ref, dp_ref, dw_ref, acc):
        c, i = pl.program_id(0), pl.program_id(1)

        @pl.when(i == 0)
        def _():
            acc[...] = jnp.zeros_like(acc)

        acc[...] += _dot_tn(dp_ref[...], h_ref[...])

        def put(lo, hi, dst):
            for half in range(2):
                dw_ref[half, dst:dst + hi - lo, :] = acc[lo:hi, half * HALF_D:(half + 1) * HALF_D].astype(BF16)

        for cc in range(nc):
            @pl.when((c == cc) & (i == nrow - 1))
            def _(cc=cc):
                base = cc * chunk
                lo, hi = base, min(base + chunk, O_MZ)
                if lo < hi:
                    put(lo - base, hi - base, lo)
                lo, hi = max(base, O_MZ), min(base + chunk, O_KR)
                if lo < hi:
                    put(lo - base, hi - base, lo + ROPE_DIM)
                if base <= O_KR and O_END <= base + chunk:
                    r = O_KR - base
                    half_r = ROPE_DIM // 2
                    kr = sum(acc[r + ROPE_DIM * j:r + ROPE_DIM * (j + 1), :] for j in range(4))
                    sw = sum(acc[r + 128 + ROPE_DIM * j:r + 128 + ROPE_DIM * (j + 1), :] for j in range(4))
                    tot = kr + jnp.concatenate([sw[half_r:], sw[:half_r]], axis=0)
                    for half in range(2):
                        dw_ref[half, O_MZ:O_MZ + ROPE_DIM, :] = (
                            tot[:, half * HALF_D:(half + 1) * HALF_D].astype(BF16))

    assert O_KR >= (nc - 1) * chunk
    return pl.pallas_call(
        body, name="dwin", grid=(nc, nrow),
        out_shape=_sds((2, IN_WIDTH, HALF_D), BF16),
        in_specs=[pl.BlockSpec((tm, D_MODEL), lambda c, i: (i, 0)),
                  pl.BlockSpec((tm, chunk), lambda c, i: (i, c))],
        out_specs=pl.BlockSpec((2, IN_WIDTH, HALF_D), lambda c, i: (0, 0, 0)),
        scratch_shapes=[pltpu.VMEM((chunk, D_MODEL), F32)],
        compiler_params=_params(("arbitrary", "arbitrary")),
    )(h, dproj)


def _internal_weights(w_in_t, w_uq, w_ukv):
    krot_t = w_in_t[:, O_MZ:O_MZ + ROPE_DIM]
    krot_sw = krot_t.reshape(2, 2, ROPE_DIM // 2, HALF_D)[:, ::-1].reshape(2, ROPE_DIM, HALF_D)
    w_kr = jnp.concatenate([jnp.tile(krot_t, (1, 4, 1)), jnp.tile(krot_sw, (1, 4, 1))], axis=1)
    uq = w_uq.reshape(Q_RANK, N_HEADS, 96)
    wp = uq[:, :, 64:].reshape(Q_RANK, 256)
    w_q = jnp.concatenate([uq[:, :, :64].reshape(Q_RANK, 512), wp, _swap_halves(wp, 32)], axis=1)
    ukv = w_ukv.reshape(KV_RANK, N_HEADS, 128)
    w_kv = jnp.concatenate([ukv[:, :, :64].reshape(KV_RANK, 512), ukv[:, :, 64:].reshape(KV_RANK, 512)], axis=1)
    return w_kr, w_q, w_kv


def _true_weight_grads(dwq, dwkv):
    dwp = dwq[:, 512:768] + _swap_halves(dwq[:, 768:1024], 32)
    g_uq = jnp.concatenate([dwq[:, :512].reshape(Q_RANK, N_HEADS, 64), dwp.reshape(Q_RANK, N_HEADS, 32)],
                           axis=2).reshape(Q_RANK, 768)
    g_ukv = jnp.concatenate([dwkv[:, :512].reshape(KV_RANK, N_HEADS, 64), dwkv[:, 512:].reshape(KV_RANK, N_HEADS, 64)],
                            axis=2).reshape(KV_RANK, 1024)
    return g_uq, g_ukv


def _swap_halves(w, group):
    r, n = w.shape
    return w.reshape(r, n // group, 2, group // 2)[:, :, ::-1, :].reshape(r, n)


def _pack_shards(parts):
    return jnp.concatenate([p.reshape(-1, LANES) for p in parts], axis=0)


def _unpack_small(gw):
    offs = [0]
    for r in SMALL_ROWS:
        offs.append(offs[-1] + r)

    def cols(i, rows, shard_cols):
        blk = gw[:, offs[i]:offs[i + 1]].reshape(N_CHIPS, rows, shard_cols)
        return blk.transpose(1, 0, 2).reshape(rows, N_CHIPS * shard_cols)

    return (cols(0, Q_RANK, 192), cols(1, KV_RANK, 256), cols(2, 512, 256), cols(3, 512, 256),
            gw[:, offs[4]:offs[5]].reshape(D_MODEL, D_MODEL))


def _chip_major(g, shard_cols):
    r = g.shape[0]
    return g.reshape(r, N_CHIPS, shard_cols).transpose(1, 0, 2).reshape(N_CHIPS, -1, LANES)


def kernel(x, c, positions, w_ada, b_ada, norm_gain, w_in, q_norm_gain, w_uq, kv_norm_gain, w_ukv, w_branch_a, w_branch_b, w_out, final_norm_gain, loss_target, m_w_ada, m_b_ada, m_norm_gain, m_w_in, m_q_norm_gain, m_w_uq, m_kv_norm_gain, m_w_ukv, m_w_branch_a, m_w_branch_b, m_w_out, m_final_norm_gain, v_w_ada, v_b_ada, v_norm_gain, v_w_in, v_q_norm_gain, v_w_uq, v_kv_norm_gain, v_w_ukv, v_w_branch_a, v_w_branch_b, v_w_out, v_final_norm_gain):
    ix, iy, ic = lax.axis_index("x"), lax.axis_index("y"), lax.axis_index("c")
    me = 4 * ix + 2 * iy + ic
    chip = 2 * ix + iy
    xs = x[0]
    tgt = loss_target[0]
    s_len = xs.shape[0]

    w_in_t = jnp.swapaxes(w_in[0], 0, 1)
    w_in_tb = w_in_t.astype(BF16)
    pack_in = jnp.stack([w_in_tb[:, :HALF_D], w_in_tb[:, HALF_D:]], axis=0)
    small_shards = (w_uq[0], w_ukv[0], w_branch_a[0], w_branch_b[0], w_out[0])
    pack_small = _pack_shards([s.astype(BF16) for s in small_shards]).reshape(2, SMALL_HALF, LANES)
    mg, call, gw_in, gw_small = _gather_call(c, w_ada[0], pack_in, pack_small)
    mod = mg.transpose(1, 0, 2).reshape(N_DEV, 3 * D_MODEL) + b_ada
    mod_me = lax.dynamic_slice_in_dim(mod, me, 1, axis=0)
    shift, scale, gate = mod_me[:, :D_MODEL], mod_me[:, D_MODEL:2 * D_MODEL], mod_me[:, 2 * D_MODEL:]

    f_in_t = gw_in.reshape(2, IN_WIDTH, HALF_D)
    f_uq, f_ukv, f_a, f_b, f_out = _unpack_small(gw_small.reshape(N_CHIPS, SMALL_TOTAL, LANES))
    w_kr, w_q, w_kv = _internal_weights(f_in_t, f_uq, f_ukv)

    inv_freq = ROPE_BASE ** (-jnp.arange(0, ROPE_DIM, 2, dtype=F32) / ROPE_DIM)
    ang = positions[0].astype(F32)[:, None] * inv_freq
    cs, sn = jnp.cos(ang), jnp.sin(ang)
    cos128 = jnp.tile(jnp.concatenate([cs, cs], axis=1), (1, 4))
    sin128 = jnp.tile(jnp.concatenate([-sn, sn], axis=1), (1, 4))

    (h, sq, sk, sv, sz, cq, ckv, mz, ga, gb, kpt, qn, qp, kn, vv) = _inproj_call(
        xs, shift, scale, norm_gain, f_in_t, w_kr, w_q, w_kv, q_norm_gain, kv_norm_gain, cos128, sin128)
    oa, lt, first = _sb_fwd_call(sq, sk, sv)
    ob, lse = _mla_fwd_call(qn, qp, kn, kpt, vv)

    gf = final_norm_gain.reshape(1, D_MODEL)
    (dx2, doa, dob, dsz, dmz, dga, dgb, dwo, dwa, dwb, dgf, dgate, loss_p) = _post_call(
        xs, tgt, oa, ob, sz, mz, ga, gb, gate, gf, f_a, f_b, f_out)

    dsq, dsk_t, dsv_t = _sb_bwd_call(first[:, :, 0, 0].reshape(-1), sq, sk, sv, doa, lt)
    dqn, dqp, dkn_t, dkpt_t, dvv_t = _mla_bwd_call(qn, qp, kn, kpt, vv, ob, dob, lse)

    dproj, dwq, dwkv, dqg, dkvg = _bwdprep_call(
        dsq, dsk_t, dsv_t, dsz, dqn, dqp, dkn_t, dvv_t, dkpt_t, dmz, dga, dgb, cq, ckv, cos128, sin128,
        q_norm_gain, kv_norm_gain, w_q, w_kv)
    grad_x, dshift, dscale, dg1 = _dh_call(dproj, f_in_t, w_kr, xs, dx2, scale, norm_gain)
    g_in_t = _dwin_call(h, dproj)
    g_uq, g_ukv = _true_weight_grads(dwq, dwkv)

    g_in_pieces = g_in_t.reshape(N_DEV, IN_SHARD, HALF_D)
    g_small = jnp.concatenate([
        _chip_major(g_uq, 192).astype(BF16), _chip_major(g_ukv, 256).astype(BF16),
        dwa.reshape(N_CHIPS, -1, LANES), dwb.reshape(N_CHIPS, -1, LANES),
        dwo.reshape(N_CHIPS, -1, LANES)], axis=1).reshape(N_DEV, SMALL_HALF, LANES)
    small = _pack_vectors_call([dshift, dscale, dgate, dg1, dqg, dkvg, dgf, loss_p])
    full_in, full_small, svg = _reduce_call(g_in_pieces, g_small, small)
    full = full_small.reshape(SMALL_TOTAL, LANES)
    offs = [0]
    for r in SMALL_ROWS:
        offs.append(offs[-1] + r)
    gs_uq = full[offs[0]:offs[1]].reshape(Q_RANK, 192)
    gs_ukv = full[offs[1]:offs[2]].reshape(KV_RANK, 256)
    gs_a = full[offs[2]:offs[3]].reshape(512, 256)
    gs_b = full[offs[3]:offs[4]].reshape(512, 256)
    gs_out = full[offs[4]:offs[5]].reshape(256, D_MODEL)

    svm = svg.reshape(N_DEV, 8 * SV_COLS)
    dmod_sh = lax.dynamic_slice_in_dim(svm[:, :3 * D_MODEL], chip * 768, 768, axis=1)
    tot, gs_ada = _small_call(svm, call.T, dmod_sh)
    vec_offsets = {"b_ada": 0, "norm_gain": 3072, "q_norm_gain": 4096, "kv_norm_gain": 4480, "final_norm_gain": 4736}
    loss = tot[0, 5760]

    names = ["w_ada", "b_ada", "norm_gain", "w_in", "q_norm_gain", "w_uq", "kv_norm_gain", "w_ukv",
             "w_branch_a", "w_branch_b", "w_out", "final_norm_gain"]
    ws = [w_ada[0], b_ada, norm_gain, w_in_t, q_norm_gain, w_uq[0], kv_norm_gain, w_ukv[0],
          w_branch_a[0], w_branch_b[0], w_out[0], final_norm_gain.reshape(1, D_MODEL)]
    gs = [gs_ada, None, None, full_in, None, gs_uq, None, gs_ukv, gs_a, gs_b, gs_out, None]
    ms = [m_w_ada[0], m_b_ada, m_norm_gain, jnp.swapaxes(m_w_in[0], 0, 1), m_q_norm_gain, m_w_uq[0],
          m_kv_norm_gain, m_w_ukv[0], m_w_branch_a[0], m_w_branch_b[0], m_w_out[0],
          m_final_norm_gain.reshape(1, D_MODEL)]
    vs = [v_w_ada[0], v_b_ada, v_norm_gain, jnp.swapaxes(v_w_in[0], 0, 1), v_q_norm_gain, v_w_uq[0],
          v_kv_norm_gain, v_w_ukv[0], v_w_branch_a[0], v_w_branch_b[0], v_w_out[0],
          v_final_norm_gain.reshape(1, D_MODEL)]
    refs = [w_ada, b_ada, norm_gain, w_in, q_norm_gain, w_uq, kv_norm_gain, w_ukv,
            w_branch_a, w_branch_b, w_out, final_norm_gain]
    vec_ids = [k for k, n in enumerate(names) if n in vec_offsets]
    vec_outs = dict(zip(vec_ids, _adamw_vectors_call(
        tot, [vec_offsets[names[k]] for k in vec_ids], [ws[k] for k in vec_ids], [ms[k] for k in vec_ids],
        [vs[k] for k in vec_ids])))
    grads, deltas, new_ms, new_vs = [], [], [], []
    for k, (n, w_, g_, m_, v_, ref) in enumerate(zip(names, ws, gs, ms, vs, refs)):
        outs = vec_outs[k] if k in vec_outs else _adamw_call(n, w_, g_, m_, v_)
        if n == "w_in":
            outs = tuple(jnp.swapaxes(o_, 0, 1) for o_ in outs)
        for lst, o_ in zip((grads, deltas, new_ms, new_vs), outs):
            lst.append(o_.reshape(ref.shape))

    return (loss, grad_x.reshape(x.shape), *grads, *deltas, *new_ms, *new_vs)
```

```python
import math

import jax
import jax.numpy as jnp
from jax import lax
from jax.experimental import pallas as pl
from jax.experimental.pallas import tpu as pltpu

F32 = jnp.float32
BF16 = jnp.bfloat16

D_MODEL = 1024
SB_WIDTH = 512
MLA_WIDTH = 512
Q_RANK = 384
KV_RANK = 256
ROPE_DIM = 32
N_HEADS = 8
IN_WIDTH = 5280
EPS = 1e-6
ROPE_BASE = 10000.0
MLA_SCALE = 1.0 / math.sqrt(96.0)
SB_SCALE = 0.125
LOG2E = 1.4426950408889634

ADAM_LR = 0.001
ADAM_B1 = 0.9
ADAM_B2 = 0.999
ADAM_EPS = 1e-08
ADAM_WD = 0.01
ADAM_STEP = 10

O_SQ, O_SK, O_SV, O_SZ, O_CQ, O_CKV, O_MZ, O_GA, O_GB, O_KR, O_END = (
    0, 512, 1024, 1536, 2048, 2432, 2688, 3200, 4224, 5248, 5504)
W_INT = 5632

N_CHIPS = 4
N_DEV = 8
LANES = 128
SV_COLS = 768

ROW_TILE = 256
ATT_TILE = 256
ATT_Q_TILES = 2
FWD_Q_TILES = 4
SB_Q_TILES = 1
MLA_KEY_TILE = 512
VMEM_LIMIT = 56 * 1024 * 1024

MESH = pl.DeviceIdType.MESH


def _dot(a, b):
    return lax.dot_general(a, b, (((1,), (0,)), ((), ())), preferred_element_type=F32)


def _dot_nt(a, b):
    return lax.dot_general(a, b, (((1,), (1,)), ((), ())), preferred_element_type=F32)


def _dot_tn(a, b):
    return lax.dot_general(a, b, (((0,), (0,)), ((), ())), preferred_element_type=F32)


def _sigmoid(z):
    return 1.0 / (1.0 + jnp.exp2(z * (-LOG2E)))


def _params(sem=None):
    if sem is None:
        return pltpu.CompilerParams(vmem_limit_bytes=VMEM_LIMIT)
    return pltpu.CompilerParams(dimension_semantics=sem, vmem_limit_bytes=VMEM_LIMIT)


def _rows(tm, n):
    return pl.BlockSpec((tm, n), lambda i: (i, 0))


def _cols(n, tm):
    return pl.BlockSpec((n, tm), lambda i: (0, i))


def _whole(shape):
    nd = len(shape)
    return pl.BlockSpec(shape, lambda i: (0,) * nd)


def _sds(shape, dtype):
    return jax.ShapeDtypeStruct(shape, dtype)


def _flip(v, d):
    return 1 - v if d else v


def _weight_rows(a, b):
    if a >= O_KR:
        return True, slice(a - O_KR, b - O_KR)
    shift = ROPE_DIM if a >= O_MZ else 0
    return False, slice(a + shift, b + shift)


def _inproj_call(x, shift, scale, g1, w_in_t, w_kr, w_q, w_kv, qg, kvg, cos128, sin128):
    s_len = x.shape[0]
    tm = min(ROW_TILE, s_len)

    def body(x_ref, sh_ref, sc_ref, g1_ref, win_ref, wkr_ref, wq_ref, wkv_ref, qg_ref, kvg_ref, cos_ref, sin_ref,
             h_ref, sq_ref, sk_ref, sv_ref, sz_ref, cq_ref, ckv_ref, mz_ref, ga_ref, gb_ref, kpt_ref,
             qn_ref, qp_ref, kn_ref, vv_ref):
        xt = x_ref[...]
        r = lax.rsqrt(jnp.mean(xt * xt, axis=-1, keepdims=True) + EPS)
        h = (xt * r * g1_ref[...]) * (1.0 + sc_ref[...]) + sh_ref[...]
        hb = h.astype(BF16)
        h_ref[...] = hb

        def seg(a, b):
            from_kr, rows = _weight_rows(a, b)
            w_ref = wkr_ref if from_kr else win_ref
            return _dot_nt(hb[:, :HALF_D], w_ref[0, rows, :]) + _dot_nt(hb[:, HALF_D:], w_ref[1, rows, :])

        sq_ref[...] = (seg(O_SQ, O_SK) * SB_SCALE).astype(BF16)
        sk_ref[...] = seg(O_SK, O_SV).astype(BF16)
        sv_ref[...] = seg(O_SV, O_SZ).astype(BF16)
        sz_ref[...] = seg(O_SZ, O_CQ)
        mz_ref[...] = seg(O_MZ, O_GA)
        ga_ref[...] = seg(O_GA, O_GB)
        gb_ref[...] = seg(O_GB, O_KR)
        cos = cos_ref[...]
        sin = sin_ref[...]
        kr = seg(O_KR, O_END)
        kpt_ref[...] = (kr[:, :128] * cos + kr[:, 128:] * sin).astype(BF16)

        cq = seg(O_CQ, O_CKV)
        cq_ref[...] = cq
        rq = lax.rsqrt(jnp.mean(cq * cq, axis=-1, keepdims=True) + EPS)
        cqn = (cq * rq * qg_ref[...]).astype(BF16)
        qa = _dot(cqn, wq_ref[...])
        qn_ref[...] = qa[:, :512].astype(BF16)
        qp_ref[...] = (qa[:, 512:768] * jnp.tile(cos, (1, 2)) + qa[:, 768:] * jnp.tile(sin, (1, 2))).astype(BF16)

        ckv = seg(O_CKV, O_MZ)
        ckv_ref[...] = ckv
        rk = lax.rsqrt(jnp.mean(ckv * ckv, axis=-1, keepdims=True) + EPS)
        ckvn = (ckv * rk * kvg_ref[...]).astype(BF16)
        kva = _dot(ckvn, wkv_ref[...])
        kn_ref[...] = kva[:, :512].astype(BF16)
        vv_ref[...] = kva[:, 512:].astype(BF16)

    outs = [
        (D_MODEL, BF16), (512, BF16), (512, BF16), (512, BF16), (512, F32), (Q_RANK, F32), (KV_RANK, F32),
        (512, F32), (D_MODEL, F32), (D_MODEL, F32), (128, BF16), (512, BF16), (256, BF16), (512, BF16), (512, BF16),
    ]
    return pl.pallas_call(
        body, name="inproj", grid=(s_len // tm,),
        out_shape=tuple(_sds((s_len, n), dt) for n, dt in outs),
        in_specs=[_rows(tm, D_MODEL), _whole((1, D_MODEL)), _whole((1, D_MODEL)), _whole((1, D_MODEL)),
                  _whole((2, IN_WIDTH, HALF_D)), _whole((2, O_END - O_KR, HALF_D)),
                  _whole((Q_RANK, 1024)), _whole((KV_RANK, 1024)),
                  _whole((1, Q_RANK)), _whole((1, KV_RANK)), _rows(tm, LANES), _rows(tm, LANES)],
        out_specs=tuple(_rows(tm, n) for n, _ in outs),
        compiler_params=_params(("parallel",)),
    )(x, shift, scale, g1, w_in_t, w_kr, w_q, w_kv, qg, kvg, cos128, sin128)


Z_CLAMP = 80.0 * LOG2E
RUN_CUTOFF = 110.0 * LOG2E


def _softplus_clamped(z):
    zc = jnp.minimum(z * LOG2E, Z_CLAMP)
    return zc, jnp.log2(1.0 + jnp.exp2(zc))


def _tri_sum(a, tri):
    return _dot(a.astype(BF16), tri)


def _sb_fwd_call(q, k, v):
    s_len = q.shape[0]
    tk = min(ATT_TILE, s_len)
    tq = min(SB_Q_TILES * ATT_TILE, s_len)
    r = tq // tk
    nq = s_len // tq

    def body(q_ref, k_ref, v_ref, o_ref, lt_ref, first_ref):
        i = pl.program_id(1)
        q2 = q_ref[...]
        lane = lax.broadcasted_iota(jnp.int32, (1, 256), 1)
        krow = lax.broadcasted_iota(jnp.int32, (tk, tk), 0)
        kcol = lax.broadcasted_iota(jnp.int32, (tk, tk), 1)
        row = lax.broadcasted_iota(jnp.int32, (tq, tk), 0)
        col = lax.broadcasted_iota(jnp.int32, (tq, tk), 1)
        later = (krow > kcol).astype(BF16)
        valids = [col + u * tk < row for u in range(r)]
        hms = [(lane // 64) == hh for hh in range(4)]
        qms = [jnp.where(hm, q2, jnp.zeros_like(q2)) for hm in hms]

        def block(j, carry, valid):
            runs, acc = list(carry[:4]), carry[4]
            off = pl.multiple_of(j * tk, tk)
            kb = k_ref[pl.ds(off, tk), :]
            vb = v_ref[pl.ds(off, tk), :]
            ws = []
            for hh in range(4):
                zc, sp = _softplus_clamped(_dot_nt(qms[hh], kb))
                lm = jnp.where(valid, sp, 0.0) if valid is not None else sp
                suf = _tri_sum(lm, later)
                w = jnp.exp2(zc - sp - suf - runs[hh])
                if valid is not None:
                    w = jnp.where(valid, w, 0.0)
                ws.append(w.astype(BF16))
                runs[hh] = runs[hh] + jnp.sum(lm, axis=1, keepdims=True)
            vstack = jnp.concatenate([jnp.where(hm, vb, jnp.zeros_like(vb)) for hm in hms], axis=0)
            acc = acc + _dot(jnp.concatenate(ws, axis=1), vstack)
            return (*runs, acc)

        zero = jnp.zeros((tq, 1), F32)
        carry = (zero, zero, zero, zero, jnp.zeros((tq, 256), F32))
        for u in reversed(range(r)):
            carry = block(i * r + u, carry, valids[u])

        def least_run(runs):
            return jnp.min(jnp.minimum(jnp.minimum(runs[0], runs[1]), jnp.minimum(runs[2], runs[3])))

        n_full = i * r

        def unfinished(state):
            return jnp.logical_and(state[0] < n_full, state[1] <= RUN_CUTOFF)

        def visit(state):
            cr = block(n_full - 1 - state[0], state[2:], None)
            return (state[0] + 1, least_run(cr[:4]), *cr)

        state = lax.while_loop(unfinished, visit, (jnp.int32(0), least_run(carry[:4]), *carry))
        carry = state[2:]
        first_ref[...] = jnp.full(first_ref.shape, n_full - state[0], jnp.int32)
        for hh in range(4):
            lt_ref[0, :, hh:hh + 1] = carry[hh]
        o_ref[...] = carry[4]

    return pl.pallas_call(
        body, name="sb_fwd", grid=(2, nq),
        out_shape=(_sds((s_len, SB_WIDTH), F32), _sds((2, s_len, 4), F32), _sds((2, nq, 8, 128), jnp.int32)),
        in_specs=[pl.BlockSpec((tq, 256), lambda g, i: (i, g)),
                  pl.BlockSpec((s_len, 256), lambda g, i: (0, g)),
                  pl.BlockSpec((s_len, 256), lambda g, i: (0, g))],
        out_specs=(pl.BlockSpec((tq, 256), lambda g, i: (i, g)),
                   pl.BlockSpec((1, tq, 4), lambda g, i: (g, i, 0)),
                   pl.BlockSpec((1, 1, 8, 128), lambda g, i: (g, i, 0, 0))),
        compiler_params=_params(("parallel", "parallel")),
    )(q, k, v)


def _sb_bwd_call(first, q, k, v, do, lt):
    s_len = q.shape[0]
    tk = min(ATT_TILE, s_len)
    tq = min(SB_Q_TILES * ATT_TILE, s_len)
    r = tq // tk
    nq = s_len // tq
    nq_fwd = first.shape[0] // 2
    per_fwd = nq // nq_fwd

    def body(first_ref, q_ref, k_ref, v_ref, do_ref, lt_ref, dq_ref, dk_ref, dv_ref):
        g = pl.program_id(0)
        i = pl.program_id(1)

        @pl.when(i == 0)
        def _():
            dk_ref[...] = jnp.zeros_like(dk_ref)
            dv_ref[...] = jnp.zeros_like(dv_ref)

        q2 = q_ref[...]
        do2 = do_ref[...].astype(BF16)
        lane = lax.broadcasted_iota(jnp.int32, (1, 256), 1)
        krow = lax.broadcasted_iota(jnp.int32, (tk, tk), 0)
        kcol = lax.broadcasted_iota(jnp.int32, (tk, tk), 1)
        row = lax.broadcasted_iota(jnp.int32, (tq, tk), 0)
        col = lax.broadcasted_iota(jnp.int32, (tq, tk), 1)
        earlier = (krow < kcol).astype(BF16)
        later = (krow > kcol).astype(BF16)
        valids = [col + u * tk < row for u in range(r)]
        hms = [(lane // 64) == hh for hh in range(4)]
        qms = [jnp.where(hm, q2, jnp.zeros_like(q2)) for hm in hms]
        doms = [jnp.where(hm, do2, jnp.zeros_like(do2)) for hm in hms]
        ltots = [lt_ref[0, :, hh:hh + 1] for hh in range(4)]
        q2t = jnp.transpose(q2.astype(F32))
        do2t = jnp.transpose(do_ref[...])
        subl = lax.broadcasted_iota(jnp.int32, (256, 1), 0)
        qtstack = jnp.concatenate(
            [jnp.where((subl // 64) == hh, q2t, 0.0).astype(BF16) for hh in range(4)], axis=1)
        dotstack = jnp.concatenate(
            [jnp.where((subl // 64) == hh, do2t, 0.0).astype(BF16) for hh in range(4)], axis=1)

        def block(j, carry, valid):
            lpre, ppre, dq = list(carry[0:4]), list(carry[4:8]), carry[8]
            off = pl.multiple_of(j * tk, tk)
            kb = k_ref[pl.ds(off, tk), :]
            vb = v_ref[pl.ds(off, tk), :]
            dzs, avs = [], []
            for hh in range(4):
                zc, sp = _softplus_clamped(_dot_nt(qms[hh], kb))
                lsig = zc - sp
                lm = jnp.where(valid, sp, 0.0) if valid is not None else sp
                rowsum = jnp.sum(lm, axis=1, keepdims=True)
                between = _tri_sum(lm, later) + ((ltots[hh] - lpre[hh]) - rowsum)
                a = jnp.exp2(lsig - between)
                if valid is not None:
                    a = jnp.where(valid, a, 0.0)
                p = a * _dot_nt(doms[hh], vb)
                pbefore = ppre[hh] + _tri_sum(p, earlier)
                dz = p - jnp.exp2(lsig) * (p + pbefore)
                if valid is not None:
                    dz = jnp.where(valid, dz, 0.0)
                dzs.append(dz.astype(BF16))
                avs.append(a.astype(BF16))
                lpre[hh] = lpre[hh] + rowsum
                ppre[hh] = ppre[hh] + jnp.sum(p, axis=1, keepdims=True)
            kstack = jnp.concatenate([jnp.where(hm, kb, jnp.zeros_like(kb)) for hm in hms], axis=0)
            dq = dq + _dot(jnp.concatenate(dzs, axis=1), kstack)
            dk_ref[:, pl.ds(off, tk)] += _dot(qtstack, jnp.concatenate(dzs, axis=0))
            dv_ref[:, pl.ds(off, tk)] += _dot(dotstack, jnp.concatenate(avs, axis=0))
            return (*lpre, *ppre, dq)

        zero = jnp.zeros((tq, 1), F32)
        start = jnp.minimum(first_ref[g * nq_fwd + i // per_fwd], i * r)
        carry = lax.fori_loop(start, i * r, lambda j, cr: block(j, cr, None),
                              (zero,) * 8 + (jnp.zeros((tq, 256), F32),))
        for u in range(r):
            carry = block(i * r + u, carry, valids[u])
        dq_ref[...] = carry[8].astype(BF16)

    return pl.pallas_call(
        body, name="sb_bwd",
        out_shape=(_sds((s_len, SB_WIDTH), BF16), _sds((SB_WIDTH, s_len), F32), _sds((SB_WIDTH, s_len), F32)),
        grid_spec=pltpu.PrefetchScalarGridSpec(
            num_scalar_prefetch=1, grid=(2, nq),
            in_specs=[pl.BlockSpec((tq, 256), lambda g, i, f: (i, g)),
                      pl.BlockSpec((s_len, 256), lambda g, i, f: (0, g)),
                      pl.BlockSpec((s_len, 256), lambda g, i, f: (0, g)),
                      pl.BlockSpec((tq, 256), lambda g, i, f: (i, g)),
                      pl.BlockSpec((1, tq, 4), lambda g, i, f: (g, i, 0))],
            out_specs=(pl.BlockSpec((tq, 256), lambda g, i, f: (i, g)),
                       pl.BlockSpec((256, s_len), lambda g, i, f: (g, 0)),
                       pl.BlockSpec((256, s_len), lambda g, i, f: (g, 0)))),
        compiler_params=_params(("parallel", "arbitrary")),
    )(first, q, k, v, do, lt)


def _mla_fwd_call(qn, qp, kn, kpt, v):
    s_len = qn.shape[0]
    tk = min(MLA_KEY_TILE, s_len)
    tq = min(FWD_Q_TILES * ATT_TILE, s_len)
    r = tq // tk
    nq = s_len // tq

    def body(qn_ref, qp_ref, kn_ref, kpt_ref, v_ref, o_ref, lse_ref):
        i = pl.program_id(1)
        qn2 = qn_ref[...]
        qp2 = qp_ref[...]
        lane256 = lax.broadcasted_iota(jnp.int32, (1, 256), 1)
        lane128 = lax.broadcasted_iota(jnp.int32, (1, 128), 1)
        krow = lax.broadcasted_iota(jnp.int32, (tk, tk), 0)
        kcol = lax.broadcasted_iota(jnp.int32, (tk, tk), 1)
        row = lax.broadcasted_iota(jnp.int32, (tq, tk), 0)
        col = lax.broadcasted_iota(jnp.int32, (tq, tk), 1)
        valids = [col + u * tk <= row for u in range(r)]
        m64s = [(lane256 // 64) == hh for hh in range(4)]
        half = [(lane128 // 64) == u for u in range(2)]
        m32s = [(lane128 // 32) == hh for hh in range(4)]
        qcs = []
        for hh in range(4):
            qpair = qn2[:, 128 * (hh // 2):128 * (hh // 2) + 128]
            qcs.append(jnp.concatenate([jnp.where(half[hh % 2], qpair, jnp.zeros_like(qpair)),
                                        jnp.where(m32s[hh], qp2, jnp.zeros_like(qp2))], axis=1))

        def by_head(vals):
            return jnp.where(m64s[0], vals[0], jnp.where(m64s[1], vals[1], jnp.where(m64s[2], vals[2], vals[3])))

        def block(j, carry, valid):
            ms, ls, acc = list(carry[0:4]), list(carry[4:8]), carry[8]
            off = pl.multiple_of(j * tk, tk)
            knb = kn_ref[pl.ds(off, tk), :]
            kpb = kpt_ref[pl.ds(off, tk), :]
            vb = v_ref[pl.ds(off, tk), :]
            kcs = [jnp.concatenate([knb[:, 128 * pp:128 * pp + 128], kpb], axis=1) for pp in range(2)]
            ps, alphas = [], []
            for hh in range(4):
                s = _dot_nt(qcs[hh], kcs[hh // 2]) * (MLA_SCALE * LOG2E)
                if valid is not None:
                    s = jnp.where(valid, s, -1e30)
                mn = jnp.maximum(ms[hh], jnp.max(s, axis=1, keepdims=True))
                p = jnp.exp2(s - mn)
                alpha = jnp.exp2(ms[hh] - mn)
                ls[hh] = alpha * ls[hh] + jnp.sum(p, axis=1, keepdims=True)
                ms[hh] = mn
                ps.append(p.astype(BF16))
                alphas.append(alpha)
            pvs = []
            for pp in range(2):
                vpair = vb[:, 128 * pp:128 * pp + 128]
                vstack = jnp.concatenate([jnp.where(hf, vpair, jnp.zeros_like(vpair)) for hf in half], axis=0)
                pvs.append(_dot(jnp.concatenate(ps[2 * pp:2 * pp + 2], axis=1), vstack))
            acc = by_head(alphas) * acc + jnp.concatenate(pvs, axis=1)
            return (*ms, *ls, acc)

        neg = jnp.full((tq, 1), -1e30, F32)
        zero = jnp.zeros((tq, 1), F32)
        carry = lax.fori_loop(0, i * r, lambda j, cr: block(j, cr, None),
                              (neg,) * 4 + (zero,) * 4 + (jnp.zeros((tq, 256), F32),))
        for u in range(r):
            carry = block(i * r + u, carry, valids[u])
        o_ref[...] = carry[8] / by_head(list(carry[4:8]))
        for hh in range(4):
            lse_ref[0, :, hh:hh + 1] = (carry[hh] + jnp.log2(carry[4 + hh])) * (1.0 / LOG2E)

    return pl.pallas_call(
        body, name="mla_fwd", grid=(2, nq),
        out_shape=(_sds((s_len, MLA_WIDTH), F32), _sds((2, s_len, 4), F32)),
        in_specs=[pl.BlockSpec((tq, 256), lambda g, i: (i, g)),
                  pl.BlockSpec((tq, 128), lambda g, i: (i, g)),
                  pl.BlockSpec((s_len, 256), lambda g, i: (0, g)),
                  pl.BlockSpec((s_len, 128), lambda g, i: (0, 0)),
                  pl.BlockSpec((s_len, 256), lambda g, i: (0, g))],
        out_specs=(pl.BlockSpec((tq, 256), lambda g, i: (i, g)),
                   pl.BlockSpec((1, tq, 4), lambda g, i: (g, i, 0))),
        compiler_params=_params(("parallel", "parallel")),
    )(qn, qp, kn, kpt, v)


def _mla_bwd_call(qn, qp, kn, kpt, v, o, do, lse):
    s_len = qn.shape[0]
    tk = min(MLA_KEY_TILE, s_len)
    tq = min(ATT_Q_TILES * ATT_TILE, s_len)
    r = tq // tk
    nq = s_len // tq

    def body(qn_ref, qp_ref, kn_ref, kpt_ref, v_ref, o_ref, do_ref, lse_ref,
             dqn_ref, dqp_ref, dkn_ref, dkpt_ref, dv_ref):
        g = pl.program_id(0)
        i = pl.program_id(1)

        @pl.when(i == 0)
        def _():
            dkn_ref[...] = jnp.zeros_like(dkn_ref)
            dv_ref[...] = jnp.zeros_like(dv_ref)

        @pl.when((i == 0) & (g == 0))
        def _():
            dkpt_ref[...] = jnp.zeros_like(dkpt_ref)

        qn2 = qn_ref[...]
        qp2 = qp_ref[...]
        dof = do_ref[...]
        dob = dof.astype(BF16)
        prod = dof * o_ref[...]
        lane256 = lax.broadcasted_iota(jnp.int32, (1, 256), 1)
        lane128 = lax.broadcasted_iota(jnp.int32, (1, 128), 1)
        krow = lax.broadcasted_iota(jnp.int32, (tk, tk), 0)
        kcol = lax.broadcasted_iota(jnp.int32, (tk, tk), 1)
        row = lax.broadcasted_iota(jnp.int32, (tq, tk), 0)
        col = lax.broadcasted_iota(jnp.int32, (tq, tk), 1)
        valids = [col + u * tk <= row for u in range(r)]
        m64s = [(lane256 // 64) == hh for hh in range(4)]
        half = [(lane128 // 64) == u for u in range(2)]
        m32s = [(lane128 // 32) == hh for hh in range(4)]
        qcs, doms = [], []
        for hh in range(4):
            sl = slice(128 * (hh // 2), 128 * (hh // 2) + 128)
            qpair = qn2[:, sl]
            dpair = dob[:, sl]
            qcs.append(jnp.concatenate([jnp.where(half[hh % 2], qpair, jnp.zeros_like(qpair)),
                                        jnp.where(m32s[hh], qp2, jnp.zeros_like(qp2))], axis=1))
            doms.append(jnp.where(half[hh % 2], dpair, jnp.zeros_like(dpair)))
        dsums = [jnp.sum(jnp.where(m64, prod, 0.0), axis=1, keepdims=True) * MLA_SCALE for m64 in m64s]
        lses = [lse_ref[0, :, hh:hh + 1] * LOG2E for hh in range(4)]
        qn2t = jnp.transpose(qn2.astype(F32))
        qp2t = jnp.transpose(qp2.astype(F32))
        do2t = jnp.transpose(dof)
        sub128 = lax.broadcasted_iota(jnp.int32, (128, 1), 0)
        qtstacks, dotstacks = [], []
        for pp in range(2):
            qts, dts = [], []
            for u in range(2):
                hh = 2 * pp + u
                qts.append(jnp.concatenate(
                    [jnp.where((sub128 // 64) == u, qn2t[128 * pp:128 * pp + 128, :], 0.0),
                     jnp.where((sub128 // 32) == hh, qp2t, 0.0)], axis=0).astype(BF16))
                dts.append(jnp.where((sub128 // 64) == u, do2t[128 * pp:128 * pp + 128, :], 0.0).astype(BF16))
            qtstacks.append(jnp.concatenate(qts, axis=1))
            dotstacks.append(jnp.concatenate(dts, axis=1))

        def block(j, carry, valid):
            dqn, dqp = carry
            off = pl.multiple_of(j * tk, tk)
            knb = kn_ref[pl.ds(off, tk), :]
            kpb = kpt_ref[pl.ds(off, tk), :]
            vb = v_ref[pl.ds(off, tk), :]
            dqn_parts = []
            dkp = None
            for pp in range(2):
                sl = slice(128 * pp, 128 * pp + 128)
                knp = knb[:, sl]
                vpair = vb[:, sl]
                kc = jnp.concatenate([knp, kpb], axis=1)
                dss, pbs, kcms = [], [], []
                for u in range(2):
                    hh = 2 * pp + u
                    s = _dot_nt(qcs[hh], kc) * (MLA_SCALE * LOG2E)
                    if valid is not None:
                        s = jnp.where(valid, s, -1e30)
                    p = jnp.exp2(s - lses[hh])
                    ds = p * (_dot_nt(doms[hh], vpair) * MLA_SCALE - dsums[hh])
                    dss.append(ds.astype(BF16))
                    pbs.append(p.astype(BF16))
                    kcms.append(jnp.concatenate([jnp.where(half[u], knp, jnp.zeros_like(knp)),
                                                 jnp.where(m32s[hh], kpb, jnp.zeros_like(kpb))], axis=1))
                dqc = _dot(jnp.concatenate(dss, axis=1), jnp.concatenate(kcms, axis=0))
                dqn_parts.append(dqc[:, :128])
                dqp = dqp + dqc[:, 128:]
                dkc = _dot(qtstacks[pp], jnp.concatenate(dss, axis=0))
                dkn_ref[128 * pp:128 * pp + 128, pl.ds(off, tk)] += dkc[:128, :]
                dkp = dkc[128:, :] if dkp is None else dkp + dkc[128:, :]
                dv_ref[128 * pp:128 * pp + 128, pl.ds(off, tk)] += _dot(dotstacks[pp], jnp.concatenate(pbs, axis=0))
            dqn = dqn + jnp.concatenate(dqn_parts, axis=1)
            dkpt_ref[:, pl.ds(off, tk)] += dkp
            return dqn, dqp

        carry = lax.fori_loop(0, i * r, lambda j, cr: block(j, cr, None),
                              (jnp.zeros((tq, 256), F32), jnp.zeros((tq, 128), F32)))
        for u in range(r):
            carry = block(i * r + u, carry, valids[u])
        dqn, dqp = carry
        dqn_ref[...] = dqn.astype(BF16)
        dqp_ref[...] = dqp.astype(BF16)

    return pl.pallas_call(
        body, name="mla_bwd", grid=(2, nq),
        out_shape=(_sds((s_len, 512), BF16), _sds((s_len, 256), BF16), _sds((512, s_len), F32),
                   _sds((128, s_len), F32), _sds((512, s_len), F32)),
        in_specs=[pl.BlockSpec((tq, 256), lambda g, i: (i, g)),
                  pl.BlockSpec((tq, 128), lambda g, i: (i, g)),
                  pl.BlockSpec((s_len, 256), lambda g, i: (0, g)),
                  pl.BlockSpec((s_len, 128), lambda g, i: (0, 0)),
                  pl.BlockSpec((s_len, 256), lambda g, i: (0, g)),
                  pl.BlockSpec((tq, 256), lambda g, i: (i, g)),
                  pl.BlockSpec((tq, 256), lambda g, i: (i, g)),
                  pl.BlockSpec((1, tq, 4), lambda g, i: (g, i, 0))],
        out_specs=(pl.BlockSpec((tq, 256), lambda g, i: (i, g)),
                   pl.BlockSpec((tq, 128), lambda g, i: (i, g)),
                   pl.BlockSpec((256, s_len), lambda g, i: (g, 0)),
                   pl.BlockSpec((128, s_len), lambda g, i: (0, 0)),
                   pl.BlockSpec((256, s_len), lambda g, i: (g, 0))),
        compiler_params=_params(("arbitrary", "arbitrary")),
    )(qn, qp, kn, kpt, v, o, do, lse)


def _post_call(x, tgt, oa, ob, sz, mz, ga, gb, gate, gf, wa, wb, wo):
    s_len = x.shape[0]
    tm = min(ROW_TILE, s_len)
    nstep = s_len // tm

    def body(x_ref, t_ref, oa_ref, ob_ref, sz_ref, mz_ref, ga_ref, gb_ref, gate_ref, gf_ref,
             wa_ref, wb_ref, wo_ref,
             dx2_ref, doa_ref, dob_ref, dsz_ref, dmz_ref, dga_ref, dgb_ref,
             dwo_out, dwa_out, dwb_out, dgf_ref, dgate_ref, loss_ref, dwo_ref, dwa_ref, dwb_ref):
        @pl.when(pl.program_id(0) == 0)
        def _():
            dwo_ref[...] = jnp.zeros_like(dwo_ref)
            dwa_ref[...] = jnp.zeros_like(dwa_ref)
            dwb_ref[...] = jnp.zeros_like(dwb_ref)
            dgf_ref[...] = jnp.zeros_like(dgf_ref)
            dgate_ref[...] = jnp.zeros_like(dgate_ref)
            loss_ref[...] = jnp.zeros_like(loss_ref)

        gate = gate_ref[...]
        gf = gf_ref[...]
        oa = oa_ref[...]
        ob = ob_ref[...]
        sz = sz_ref[...]
        mz = mz_ref[...]
        sa = _sigmoid(sz)
        sb = _sigmoid(mz)
        silu_a = sz * sa
        silu_b = mz * sb
        ua = (oa * silu_a).astype(BF16)
        ub = (ob * silu_b).astype(BF16)
        ya = _dot(ua, wa_ref[...])
        yb = _dot(ub, wb_ref[...])
        sga = _sigmoid(ga_ref[...])
        sgb = _sigmoid(gb_ref[...])
        merged = (sga * ya + sgb * yb).astype(BF16)
        out = _dot(merged, wo_ref[...])
        x2 = x_ref[...] + gate * out
        r2 = lax.rsqrt(jnp.mean(x2 * x2, axis=-1, keepdims=True) + EPS)
        xhat = x2 * r2
        err = xhat * gf - t_ref[...]
        loss_ref[...] += 0.5 * jnp.sum(jnp.sum(err * err, axis=1, keepdims=True), axis=0, keepdims=True) / D_MODEL
        dy = err * (1.0 / D_MODEL)
        dgf_ref[...] += jnp.sum(dy * xhat, axis=0, keepdims=True)
        dxhat = dy * gf
        dx2 = r2 * (dxhat - xhat * jnp.mean(dxhat * xhat, axis=-1, keepdims=True))
        dx2_ref[...] = dx2
        dgate_ref[...] += jnp.sum(dx2 * out, axis=0, keepdims=True)
        dout = (dx2 * gate).astype(BF16)
        dmerged = _dot_nt(dout, wo_ref[...])
        dwo_ref[...] += _dot_tn(merged, dout)
        dya = dmerged * sga
        dyb = dmerged * sgb
        dga_ref[...] = (dya * ya * (1.0 - sga)).astype(BF16)
        dgb_ref[...] = (dyb * yb * (1.0 - sgb)).astype(BF16)
        dyab = dya.astype(BF16)
        dybb = dyb.astype(BF16)
        dua = _dot_nt(dyab, wa_ref[...])
        dub = _dot_nt(dybb, wb_ref[...])
        dwa_ref[...] += _dot_tn(ua, dyab)
        dwb_ref[...] += _dot_tn(ub, dybb)
        doa_ref[...] = dua * silu_a
        dob_ref[...] = dub * silu_b
        dsz_ref[...] = (dua * oa * (sa * (1.0 + sz * (1.0 - sa)))).astype(BF16)
        dmz_ref[...] = (dub * ob * (sb * (1.0 + mz * (1.0 - sb)))).astype(BF16)

        @pl.when(pl.program_id(0) == nstep - 1)
        def _():
            dwo_out[...] = dwo_ref[...].astype(BF16)
            for k in range(N_CHIPS):
                dwa_out[k] = dwa_ref[:, 256 * k:256 * k + 256].astype(BF16)
                dwb_out[k] = dwb_ref[:, 256 * k:256 * k + 256].astype(BF16)

    return pl.pallas_call(
        body, name="post", grid=(nstep,),
        out_shape=(_sds((s_len, D_MODEL), F32), _sds((s_len, 512), F32), _sds((s_len, 512), F32),
                   _sds((s_len, 512), BF16), _sds((s_len, 512), BF16),
                   _sds((s_len, D_MODEL), BF16), _sds((s_len, D_MODEL), BF16),
                   _sds((D_MODEL, D_MODEL), BF16), _sds((N_CHIPS, 512, 256), BF16), _sds((N_CHIPS, 512, 256), BF16),
                   _sds((1, D_MODEL), F32), _sds((1, D_MODEL), F32), _sds((1, 128), F32)),
        in_specs=[_rows(tm, D_MODEL), _rows(tm, D_MODEL), _rows(tm, 512), _rows(tm, 512), _rows(tm, 512),
                  _rows(tm, 512), _rows(tm, D_MODEL), _rows(tm, D_MODEL), _whole((1, D_MODEL)), _whole((1, D_MODEL)),
                  _whole((512, D_MODEL)), _whole((512, D_MODEL)), _whole((D_MODEL, D_MODEL))],
        out_specs=(_rows(tm, D_MODEL), _rows(tm, 512), _rows(tm, 512), _rows(tm, 512), _rows(tm, 512),
                   _rows(tm, D_MODEL), _rows(tm, D_MODEL),
                   _whole((D_MODEL, D_MODEL)), _whole((N_CHIPS, 512, 256)), _whole((N_CHIPS, 512, 256)),
                   _whole((1, D_MODEL)), _whole((1, D_MODEL)), _whole((1, 128))),
        scratch_shapes=[pltpu.VMEM((D_MODEL, D_MODEL), F32), pltpu.VMEM((512, D_MODEL), F32),
                        pltpu.VMEM((512, D_MODEL), F32)],
        compiler_params=_params(("arbitrary",)),
    )(x, tgt, oa, ob, sz, mz, ga, gb, gate, gf, wa, wb, wo)


def _bwdprep_call(dsq, dsk, dsv, dsz, dqn, dqp, dkn, dvv, dkpt, dmz, dga, dgb, cq, ckv, cos128, sin128,
                  qg, kvg, w_q, w_kv):
    s_len = cq.shape[0]
    tm = min(ROW_TILE, s_len)

    def body(dsq_ref, dsk_ref, dsv_ref, dsz_ref, dqn_ref, dqp_ref, dkn_ref, dvv_ref, dkpt_ref, dmz_ref,
             dga_ref, dgb_ref, cq_ref, ckv_ref, cos_ref, sin_ref, qg_ref, kvg_ref, wq_ref, wkv_ref,
             dp_ref, dwq_ref, dwkv_ref, dqg_ref, dkvg_ref):
        @pl.when(pl.program_id(0) == 0)
        def _():
            dwq_ref[...] = jnp.zeros_like(dwq_ref)
            dwkv_ref[...] = jnp.zeros_like(dwkv_ref)
            dqg_ref[...] = jnp.zeros_like(dqg_ref)
            dkvg_ref[...] = jnp.zeros_like(dkvg_ref)

        cos = cos_ref[...]
        sin = sin_ref[...]
        dp_ref[:, O_SQ:O_SK] = dsq_ref[...] * jnp.asarray(SB_SCALE, BF16)
        dp_ref[:, O_SK:O_SV] = jnp.transpose(dsk_ref[...]).astype(BF16)
        dp_ref[:, O_SV:O_SZ] = jnp.transpose(dsv_ref[...]).astype(BF16)
        dp_ref[:, O_SZ:O_CQ] = dsz_ref[...]
        dp_ref[:, O_MZ:O_GA] = dmz_ref[...]
        dp_ref[:, O_GA:O_GB] = dga_ref[...]
        dp_ref[:, O_GB:O_KR] = dgb_ref[...]
        dkp = jnp.transpose(dkpt_ref[...])
        dp_ref[:, O_KR:O_KR + 128] = (dkp * cos).astype(BF16)
        dp_ref[:, O_KR + 128:O_END] = (dkp * sin).astype(BF16)
        dp_ref[:, O_END:W_INT] = jnp.zeros((tm, W_INT - O_END), BF16)

        cq = cq_ref[...]
        rq = lax.rsqrt(jnp.mean(cq * cq, axis=-1, keepdims=True) + EPS)
        cqh = cq * rq
        qg = qg_ref[...]
        cqn = (cqh * qg).astype(BF16)
        dqp = dqp_ref[...].astype(F32)
        dqa = jnp.concatenate([dqn_ref[...], (dqp * jnp.tile(cos, (1, 2))).astype(BF16),
                               (dqp * jnp.tile(sin, (1, 2))).astype(BF16)], axis=1)
        dcqn = _dot_nt(dqa, wq_ref[...])
        dwq_ref[...] += _dot_tn(cqn, dqa)
        dqg_ref[...] += jnp.sum(dcqn * cqh, axis=0, keepdims=True)
        dh = dcqn * qg
        dcq = rq * (dh - cqh * jnp.mean(dh * cqh, axis=-1, keepdims=True))
        dp_ref[:, O_CQ:O_CKV] = dcq.astype(BF16)

        ckv = ckv_ref[...]
        rk = lax.rsqrt(jnp.mean(ckv * ckv, axis=-1, keepdims=True) + EPS)
        ckh = ckv * rk
        kvg = kvg_ref[...]
        ckvn = (ckh * kvg).astype(BF16)
        dkva = jnp.concatenate([jnp.transpose(dkn_ref[...]).astype(BF16),
                                jnp.transpose(dvv_ref[...]).astype(BF16)], axis=1)
        dckvn = _dot_nt(dkva, wkv_ref[...])
        dwkv_ref[...] += _dot_tn(ckvn, dkva)
        dkvg_ref[...] += jnp.sum(dckvn * ckh, axis=0, keepdims=True)
        dh2 = dckvn * kvg
        dckv = rk * (dh2 - ckh * jnp.mean(dh2 * ckh, axis=-1, keepdims=True))
        dp_ref[:, O_CKV:O_MZ] = dckv.astype(BF16)

    return pl.pallas_call(
        body, name="bwdprep", grid=(s_len // tm,),
        out_shape=(_sds((s_len, W_INT), BF16), _sds((Q_RANK, 1024), F32), _sds((KV_RANK, 1024), F32),
                   _sds((1, Q_RANK), F32), _sds((1, KV_RANK), F32)),
        in_specs=[_rows(tm, 512), _cols(512, tm), _cols(512, tm), _rows(tm, 512), _rows(tm, 512), _rows(tm, 256),
                  _cols(512, tm), _cols(512, tm), _cols(128, tm), _rows(tm, 512), _rows(tm, D_MODEL),
                  _rows(tm, D_MODEL), _rows(tm, Q_RANK), _rows(tm, KV_RANK), _rows(tm, LANES), _rows(tm, LANES),
                  _whole((1, Q_RANK)), _whole((1, KV_RANK)), _whole((Q_RANK, 1024)), _whole((KV_RANK, 1024))],
        out_specs=(_rows(tm, W_INT), _whole((Q_RANK, 1024)), _whole((KV_RANK, 1024)),
                   _whole((1, Q_RANK)), _whole((1, KV_RANK))),
        compiler_params=_params(("arbitrary",)),
    )(dsq, dsk, dsv, dsz, dqn, dqp, dkn, dvv, dkpt, dmz, dga, dgb, cq, ckv, cos128, sin128, qg, kvg, w_q, w_kv)


def _dh_call(dproj, w_in_t, w_kr, x, dx2, scale, g1):
    s_len = x.shape[0]
    tm = min(2 * ROW_TILE, s_len)
    parts = [(a, b) + _weight_rows(a, b) for a, b in ((0, O_MZ), (O_MZ, O_KR), (O_KR, O_END))]

    def body(dp_ref, win_ref, wkr_ref, x_ref, dx2_ref, sc_ref, g1_ref, gx_ref, dsh_ref, dsc_ref, dg1_ref):
        @pl.when(pl.program_id(0) == 0)
        def _():
            dsh_ref[...] = jnp.zeros_like(dsh_ref)
            dsc_ref[...] = jnp.zeros_like(dsc_ref)
            dg1_ref[...] = jnp.zeros_like(dg1_ref)

        def half(k):
            return sum(_dot(dp_ref[:, a:b], (wkr_ref if from_kr else win_ref)[k, rows, :])
                       for a, b, from_kr, rows in parts)

        dh = jnp.concatenate([half(0), half(1)], axis=1)
        xt = x_ref[...]
        r = lax.rsqrt(jnp.mean(xt * xt, axis=-1, keepdims=True) + EPS)
        xh = xt * r
        g1 = g1_ref[...]
        xg = xh * g1
        dsh_ref[...] += jnp.sum(dh, axis=0, keepdims=True)
        dsc_ref[...] += jnp.sum(dh * xg, axis=0, keepdims=True)
        dxg = dh * (1.0 + sc_ref[...])
        dg1_ref[...] += jnp.sum(dxg * xh, axis=0, keepdims=True)
        dxh = dxg * g1
        gx_ref[...] = dx2_ref[...] + r * (dxh - xh * jnp.mean(dxh * xh, axis=-1, keepdims=True))

    return pl.pallas_call(
        body, name="dh", grid=(s_len // tm,),
        out_shape=(_sds((s_len, D_MODEL), F32), _sds((1, D_MODEL), F32), _sds((1, D_MODEL), F32),
                   _sds((1, D_MODEL), F32)),
        in_specs=[_rows(tm, W_INT), _whole((2, IN_WIDTH, HALF_D)), _whole((2, O_END - O_KR, HALF_D)),
                  _rows(tm, D_MODEL), _rows(tm, D_MODEL), _whole((1, D_MODEL)), _whole((1, D_MODEL))],
        out_specs=(_rows(tm, D_MODEL), _whole((1, D_MODEL)), _whole((1, D_MODEL)), _whole((1, D_MODEL))),
        compiler_params=_params(("arbitrary",)),
    )(dproj, w_in_t, w_kr, x, dx2, scale, g1)


def _small_call(svg, ct, dmod_sh):
    def body(sv_ref, ct_ref, dm_ref, tot_ref, gwada_ref):
        acc = sv_ref[0:1, :]
        for d in range(1, N_DEV):
            acc = acc + sv_ref[d:d + 1, :]
        tot_ref[...] = acc
        gwada_ref[...] = lax.dot_general(ct_ref[...], dm_ref[...], (((1,), (0,)), ((), ())),
                                         precision=lax.Precision.HIGHEST, preferred_element_type=F32)

    vmem = pl.BlockSpec(memory_space=pltpu.VMEM)
    return pl.pallas_call(
        body, name="small_grads",
        out_shape=(_sds((1, 8 * SV_COLS), F32), _sds((D_MODEL, 768), F32)),
        in_specs=[vmem, vmem, vmem], out_specs=(vmem, vmem),
        compiler_params=_params(),
    )(svg, ct, dmod_sh)


def _adamw_tile_rows(rows, cols):
    budget = 2 << 20
    if rows * cols * 4 <= budget or rows % 8:
        return rows
    best = 8
    for tr in range(8, rows + 1, 8):
        if rows % tr == 0 and tr * cols * 4 <= budget:
            best = tr
    return best


def _adamw_math(w, gg, m, v):
    m2 = ADAM_B1 * m + (1.0 - ADAM_B1) * gg
    v2 = ADAM_B2 * v + (1.0 - ADAM_B2) * (gg * gg)
    m_hat = m2 / (1.0 - ADAM_B1 ** ADAM_STEP)
    v_hat = v2 / (1.0 - ADAM_B2 ** ADAM_STEP)
    return -ADAM_LR * (m_hat / (jnp.sqrt(v_hat) + ADAM_EPS) + ADAM_WD * w), m2, v2


def _adamw_call(name, w, g, m, v):
    rows, cols = w.shape
    tr = _adamw_tile_rows(rows, cols)
    halves = g.ndim == 3

    def body(w_ref, g_ref, m_ref, v_ref, *out_refs):
        if halves:
            gg = jnp.concatenate([g_ref[0], g_ref[1]], axis=1)
            out_refs[0][...] = gg
        else:
            gg = g_ref[...]
        d_ref, nm_ref, nv_ref = out_refs[-3:]
        d_ref[...], nm_ref[...], nv_ref[...] = _adamw_math(w_ref[...], gg, m_ref[...], v_ref[...])

    spec = pl.BlockSpec((tr, cols), lambda i: (i, 0))
    g_spec = pl.BlockSpec((2, tr, cols // 2), lambda i: (0, i, 0)) if halves else spec
    n_out = 4 if halves else 3
    outs = pl.pallas_call(
        body, name="adamw_" + name, grid=(rows // tr,),
        out_shape=(_sds((rows, cols), F32),) * n_out,
        in_specs=[spec, g_spec, spec, spec], out_specs=(spec,) * n_out,
        compiler_params=_params(("parallel",)),
    )(w, g, m, v)
    return tuple(outs) if halves else (g,) + tuple(outs)


def _adamw_group_call(ws, gs, ms, vs):
    n = len(ws)

    def body(*refs):
        ins, outs = refs[:4 * n], refs[4 * n:]
        for k in range(n):
            w_ref, g_ref, m_ref, v_ref = ins[4 * k:4 * k + 4]
            d_ref, nm_ref, nv_ref = outs[3 * k:3 * k + 3]
            d_ref[...], nm_ref[...], nv_ref[...] = _adamw_math(w_ref[...], g_ref[...], m_ref[...], v_ref[...])

    vmem = pl.BlockSpec(memory_space=pltpu.VMEM)
    outs = pl.pallas_call(
        body, name="adamw_small_matrices",
        out_shape=tuple(_sds(w.shape, F32) for w in ws for _ in range(3)),
        in_specs=[vmem] * (4 * n), out_specs=(vmem,) * (3 * n),
        compiler_params=_params(),
    )(*[a for quad in zip(ws, gs, ms, vs) for a in quad])
    return [tuple(outs[3 * k:3 * k + 3]) for k in range(n)]


def _pack_vectors_call(vectors):
    used = sum(v.shape[1] for v in vectors)

    def body(*refs):
        out_ref = refs[-1]
        row = jnp.concatenate([r[...] for r in refs[:-1]] + [jnp.zeros((1, 8 * SV_COLS - used), F32)], axis=1)
        for r in range(8):
            out_ref[r:r + 1, :] = row[:, SV_COLS * r:SV_COLS * (r + 1)]

    vmem = pl.BlockSpec(memory_space=pltpu.VMEM)
    return pl.pallas_call(
        body, name="pack_vectors", out_shape=_sds((8, SV_COLS), F32),
        in_specs=[vmem] * len(vectors), out_specs=vmem, compiler_params=_params(),
    )(*vectors)


def _adamw_vectors_call(tot, offsets, ws, ms, vs):
    nvec = len(ws)

    def body(tot_ref, *refs):
        ins, outs = refs[:3 * nvec], refs[3 * nvec:]
        for k in range(nvec):
            w_ref, m_ref, v_ref = ins[3 * k:3 * k + 3]
            g_ref, d_ref, nm_ref, nv_ref = outs[4 * k:4 * k + 4]
            gg = tot_ref[:, offsets[k]:offsets[k] + w_ref.shape[1]]
            g_ref[...] = gg
            d_ref[...], nm_ref[...], nv_ref[...] = _adamw_math(w_ref[...], gg, m_ref[...], v_ref[...])

    vmem = pl.BlockSpec(memory_space=pltpu.VMEM)
    outs = pl.pallas_call(
        body, name="adamw_vectors",
        out_shape=tuple(_sds(w.shape, F32) for w in ws for _ in range(4)),
        in_specs=[vmem] * (1 + 3 * nvec), out_specs=(vmem,) * (4 * nvec),
        compiler_params=_params(),
    )(tot, *[a for w, m, v in zip(ws, ms, vs) for a in (w, m, v)])
    return [tuple(outs[4 * k:4 * k + 4]) for k in range(nvec)]


IN_SHARD = IN_WIDTH // N_CHIPS
HALF_D = D_MODEL // 2
SMALL_ROWS = (576, 512, 1024, 1024, 2048)
SMALL_TOTAL = sum(SMALL_ROWS)
SMALL_HALF = SMALL_TOTAL // 2
SMALL_SUM_ROWS = 432


def _gather_call(c_row, w_ada_sh, pack_in, pack_small):
    def body(c_ref, wada_ref, pki_ref, pks_ref, mg_ref, cg_ref, gwi_ref, gws_ref,
             cv, ssem_c, rsem_c, ssem_m, rsem_m, ssem_w, rsem_w, ssem_f, rsem_f, lsem):
        x, y, c = lax.axis_index("x"), lax.axis_index("y"), lax.axis_index("c")
        me = 4 * x + 2 * y + c
        chip = 2 * x + y
        rel3 = [(1, 0), (0, 1), (1, 1)]
        packs = [(pki_ref, gwi_ref), (pks_ref, gws_ref)]

        def slot(a, gw, k, h):
            return gw.at[h, k] if a == 0 else gw.at[k, h]

        sends = []
        for j, (dx, dy) in enumerate(rel3):
            for a, (pk, gw) in enumerate(packs):
                cp = pltpu.make_async_remote_copy(
                    src_ref=pk.at[c], dst_ref=slot(a, gw, chip, c), send_sem=ssem_w.at[j, a],
                    recv_sem=rsem_w.at[j, a], device_id=(_flip(x, dx), _flip(y, dy), c), device_id_type=MESH)
                cp.start()
                sends.append(cp)
        owns = []
        for a, (pk, gw) in enumerate(packs):
            for h in range(2):
                own = pltpu.make_async_copy(pk.at[h], slot(a, gw, chip, h), lsem.at[a, h])
                own.start()
                owns.append(own)

        cv[me] = c_ref[...]
        for r in range(1, N_DEV):
            dx, dy, dc = (r >> 2) & 1, (r >> 1) & 1, r & 1
            cp = pltpu.make_async_remote_copy(
                src_ref=c_ref, dst_ref=cv.at[me], send_sem=ssem_c.at[r - 1], recv_sem=rsem_c.at[r - 1],
                device_id=(_flip(x, dx), _flip(y, dy), _flip(c, dc)), device_id_type=MESH)
            cp.start()
            sends.append(cp)
        for r in range(1, N_DEV):
            dx, dy, dc = (r >> 2) & 1, (r >> 1) & 1, r & 1
            src = 4 * _flip(x, dx) + 2 * _flip(y, dy) + _flip(c, dc)
            pltpu.make_async_remote_copy(
                src_ref=c_ref, dst_ref=cv.at[src], send_sem=ssem_c.at[r - 1], recv_sem=rsem_c.at[r - 1],
                device_id=(x, y, c), device_id_type=MESH).wait_recv()
        rows = lax.broadcasted_iota(jnp.int32, (N_DEV, D_MODEL), 0)
        call = jnp.zeros((N_DEV, D_MODEL), F32)
        for b in range(N_DEV):
            call = jnp.where(rows == b, jnp.broadcast_to(cv[b], (N_DEV, D_MODEL)), call)
        cg_ref[...] = call

        mg_ref[chip] = lax.dot_general(call, wada_ref[...], (((1,), (0,)), ((), ())),
                                       precision=lax.Precision.HIGHEST, preferred_element_type=F32)
        for j, (dx, dy) in enumerate(rel3):
            cp = pltpu.make_async_remote_copy(
                src_ref=mg_ref.at[chip], dst_ref=mg_ref.at[chip], send_sem=ssem_m.at[j], recv_sem=rsem_m.at[j],
                device_id=(_flip(x, dx), _flip(y, dy), c), device_id_type=MESH)
            cp.start()
            sends.append(cp)
        for j, (dx, dy) in enumerate(rel3):
            src_chip = 2 * _flip(x, dx) + _flip(y, dy)
            pltpu.make_async_remote_copy(
                src_ref=mg_ref.at[src_chip], dst_ref=mg_ref.at[src_chip], send_sem=ssem_m.at[j],
                recv_sem=rsem_m.at[j], device_id=(x, y, c), device_id_type=MESH).wait_recv()
        for j, (dx, dy) in enumerate(rel3):
            src_chip = 2 * _flip(x, dx) + _flip(y, dy)
            for a, (pk, gw) in enumerate(packs):
                pltpu.make_async_remote_copy(
                    src_ref=pk.at[c], dst_ref=slot(a, gw, src_chip, c), send_sem=ssem_w.at[j, a],
                    recv_sem=rsem_w.at[j, a], device_id=(x, y, c), device_id_type=MESH).wait_recv()
                cp = pltpu.make_async_remote_copy(
                    src_ref=slot(a, gw, src_chip, c), dst_ref=slot(a, gw, src_chip, c), send_sem=ssem_f.at[j, a],
                    recv_sem=rsem_f.at[j, a], device_id=(x, y, 1 - c), device_id_type=MESH)
                cp.start()
                sends.append(cp)
        for j, (dx, dy) in enumerate(rel3):
            src_chip = 2 * _flip(x, dx) + _flip(y, dy)
            for a, (pk, gw) in enumerate(packs):
                pltpu.make_async_remote_copy(
                    src_ref=pk.at[c], dst_ref=slot(a, gw, src_chip, 1 - c), send_sem=ssem_f.at[j, a],
                    recv_sem=rsem_f.at[j, a], device_id=(x, y, c), device_id_type=MESH).wait_recv()
        for cp in sends:
            cp.wait_send()
        for own in owns:
            own.wait()

    vmem = pl.BlockSpec(memory_space=pltpu.VMEM)
    return pl.pallas_call(
        body, name="gather_fwd",
        out_shape=(_sds((N_CHIPS, N_DEV, 768), F32), _sds((N_DEV, D_MODEL), F32),
                   _sds((2, N_CHIPS, IN_SHARD, HALF_D), BF16), _sds((N_CHIPS, 2, SMALL_HALF, LANES), BF16)),
        in_specs=[vmem, vmem, vmem, vmem], out_specs=(vmem, vmem, vmem, vmem),
        scratch_shapes=[
            pltpu.VMEM((N_DEV, 1, D_MODEL), F32),
            pltpu.SemaphoreType.DMA((N_DEV - 1,)), pltpu.SemaphoreType.DMA((N_DEV - 1,)),
            pltpu.SemaphoreType.DMA((3,)), pltpu.SemaphoreType.DMA((3,)),
            pltpu.SemaphoreType.DMA((3, 2)), pltpu.SemaphoreType.DMA((3, 2)),
            pltpu.SemaphoreType.DMA((3, 2)), pltpu.SemaphoreType.DMA((3, 2)),
            pltpu.SemaphoreType.DMA((2, 2)),
        ],
        compiler_params=_params(),
    )(c_row, w_ada_sh, pack_in, pack_small)


def _reduce_call(g_in, g_small, sv):
    def body(gi_ref, gs_ref, sv_ref, fi_ref, fs_ref, svg_ref, pair_i, pair_s, send_i, send_s, land_i, land_s,
             ssem_p, rsem_p, ssem_g, rsem_g, ssem_s, rsem_s, ssem_x, rsem_x):
        x, y, c = lax.axis_index("x"), lax.axis_index("y"), lax.axis_index("c")
        me = 4 * x + 2 * y + c
        chip = 2 * x + y
        rel3 = [(1, 0), (0, 1), (1, 1)]
        payloads = [(gi_ref, pair_i, send_i, land_i, fi_ref), (gs_ref, pair_s, send_s, land_s, fs_ref)]
        where = [lambda k, h: N_CHIPS * h + k, lambda k, h: 2 * k + h]
        copies = []

        for k in range(N_CHIPS):
            for a, (g, pair, _, _, _) in enumerate(payloads):
                cp = pltpu.make_async_remote_copy(
                    src_ref=g.at[where[a](k, 1 - c)], dst_ref=pair.at[k], send_sem=ssem_p.at[k, a],
                    recv_sem=rsem_p.at[k, a], device_id=(x, y, 1 - c), device_id_type=MESH)
                cp.start()
                copies.append(cp)

        for r in range(1, N_DEV):
            dx, dy, dc = (r >> 2) & 1, (r >> 1) & 1, r & 1
            cp = pltpu.make_async_remote_copy(
                src_ref=sv_ref, dst_ref=svg_ref.at[me], send_sem=ssem_s.at[r - 1], recv_sem=rsem_s.at[r - 1],
                device_id=(_flip(x, dx), _flip(y, dy), _flip(c, dc)), device_id_type=MESH)
            cp.start()
            copies.append(cp)
        svg_ref[me] = sv_ref[...]

        def pair_sum(k, store_in, store_small):
            for a, (g, pair, _, _, _) in enumerate(payloads):
                pltpu.make_async_remote_copy(
                    src_ref=g.at[where[a](k, c)], dst_ref=pair.at[k], send_sem=ssem_p.at[k, a],
                    recv_sem=rsem_p.at[k, a], device_id=(x, y, c), device_id_type=MESH).wait_recv()
            for qd in range(HALF_D // LANES):
                sl = slice(LANES * qd, LANES * qd + LANES)
                store_in(sl, gi_ref[where[0](k, c), :, sl].astype(F32) + pair_i[k, :, sl].astype(F32))

            def rows(i, carry):
                sl = pl.ds(pl.multiple_of(i * SMALL_SUM_ROWS, 16), SMALL_SUM_ROWS)
                store_small(sl, gs_ref[where[1](k, c), sl, :].astype(F32) + pair_s[k, sl, :].astype(F32))
                return carry

            lax.fori_loop(0, SMALL_HALF // SMALL_SUM_ROWS, rows, 0)

        for j, (dx, dy) in enumerate(rel3):
            tx, ty = _flip(x, dx), _flip(y, dy)

            def put_in(sl, val, j=j):
                send_i[j, :, sl] = val.astype(BF16)

            def put_small(sl, val, j=j):
                send_s[j, sl, :] = val.astype(BF16)

            pair_sum(2 * tx + ty, put_in, put_small)
            for a, (_, _, send, land, _) in enumerate(payloads):
                cp = pltpu.make_async_remote_copy(
                    src_ref=send.at[j], dst_ref=land.at[j], send_sem=ssem_g.at[j, a], recv_sem=rsem_g.at[j, a],
                    device_id=(tx, ty, c), device_id_type=MESH)
                cp.start()
                copies.append(cp)

        def own_in(sl, val):
            fi_ref[c, :, sl] = val

        def own_small(sl, val):
            fs_ref[c, sl, :] = val

        pair_sum(chip, own_in, own_small)
        for j in range(3):
            for a, (_, _, send, land, _) in enumerate(payloads):
                pltpu.make_async_remote_copy(
                    src_ref=send.at[j], dst_ref=land.at[j], send_sem=ssem_g.at[j, a], recv_sem=rsem_g.at[j, a],
                    device_id=(x, y, c), device_id_type=MESH).wait_recv()
            for qd in range(HALF_D // LANES):
                sl = slice(LANES * qd, LANES * qd + LANES)
                fi_ref[c, :, sl] += land_i[j, :, sl].astype(F32)

            def add_rows(i, carry, j=j):
                sl = pl.ds(pl.multiple_of(i * SMALL_SUM_ROWS, 16), SMALL_SUM_ROWS)
                fs_ref[c, sl, :] += land_s[j, sl, :].astype(F32)
                return carry

            lax.fori_loop(0, SMALL_HALF // SMALL_SUM_ROWS, add_rows, 0)

        for a, f in enumerate((fi_ref, fs_ref)):
            cp = pltpu.make_async_remote_copy(
                src_ref=f.at[c], dst_ref=f.at[c], send_sem=ssem_x.at[a], recv_sem=rsem_x.at[a],
                device_id=(x, y, 1 - c), device_id_type=MESH)
            cp.start()
            copies.append(cp)
        for a, f in enumerate((fi_ref, fs_ref)):
            pltpu.make_async_remote_copy(
                src_ref=f.at[c], dst_ref=f.at[1 - c], send_sem=ssem_x.at[a], recv_sem=rsem_x.at[a],
                device_id=(x, y, c), device_id_type=MESH).wait_recv()
        for r in range(1, N_DEV):
            dx, dy, dc = (r >> 2) & 1, (r >> 1) & 1, r & 1
            src = 4 * _flip(x, dx) + 2 * _flip(y, dy) + _flip(c, dc)
            pltpu.make_async_remote_copy(
                src_ref=sv_ref, dst_ref=svg_ref.at[src], send_sem=ssem_s.at[r - 1],
                recv_sem=rsem_s.at[r - 1], device_id=(x, y, c), device_id_type=MESH).wait_recv()
        for cp in copies:
            cp.wait_send()

    vmem = pl.BlockSpec(memory_space=pltpu.VMEM)
    return pl.pallas_call(
        body, name="grad_reduce",
        out_shape=(_sds((2, IN_SHARD, HALF_D), F32), _sds((2, SMALL_HALF, LANES), F32),
                   _sds((N_DEV, 8, SV_COLS), F32)),
        in_specs=[vmem, vmem, vmem], out_specs=(vmem, vmem, vmem),
        scratch_shapes=[
            pltpu.VMEM((N_CHIPS, IN_SHARD, HALF_D), BF16), pltpu.VMEM((N_CHIPS, SMALL_HALF, LANES), BF16),
            pltpu.VMEM((3, IN_SHARD, HALF_D), BF16), pltpu.VMEM((3, SMALL_HALF, LANES), BF16),
            pltpu.VMEM((3, IN_SHARD, HALF_D), BF16), pltpu.VMEM((3, SMALL_HALF, LANES), BF16),
            pltpu.SemaphoreType.DMA((N_CHIPS, 2)), pltpu.SemaphoreType.DMA((N_CHIPS, 2)),
            pltpu.SemaphoreType.DMA((3, 2)), pltpu.SemaphoreType.DMA((3, 2)),
            pltpu.SemaphoreType.DMA((N_DEV - 1,)), pltpu.SemaphoreType.DMA((N_DEV - 1,)),
            pltpu.SemaphoreType.DMA((2,)), pltpu.SemaphoreType.DMA((2,)),
        ],
        compiler_params=_params(),
    )(g_in, g_small, sv)


def _dwin_call(h, dproj):
    s_len = h.shape[0]
    tm = min(4 * ROW_TILE, s_len)
    nrow = s_len // tm
    nc = 2
    chunk = W_INT // nc

    def body(h_ref, dp_ref, dw_ref, acc):
        c, i = pl.program_id(0), pl.program_id(1)

        @pl.when(i == 0)
        def _():
            acc[...] = jnp.zeros_like(acc)

        acc[...] += _dot_tn(dp_ref[...], h_ref[...])

        def put(lo, hi, dst):
            for half in range(2):
                dw_ref[half, dst:dst + hi - lo, :] = acc[lo:hi, half * HALF_D:(half + 1) * HALF_D].astype(BF16)

        for cc in range(nc):
            @pl.when((c == cc) & (i == nrow - 1))
            def _(cc=cc):
                base = cc * chunk
                lo, hi = base, min(base + chunk, O_MZ)
                if lo < hi:
                    put(lo - base, hi - base, lo)
                lo, hi = max(base, O_MZ), min(base + chunk, O_KR)
                if lo < hi:
                    put(lo - base, hi - base, lo + ROPE_DIM)
                if base <= O_KR and O_END <= base + chunk:
                    r = O_KR - base
                    half_r = ROPE_DIM // 2
                    kr = sum(acc[r + ROPE_DIM * j:r + ROPE_DIM * (j + 1), :] for j in range(4))
                    sw = sum(acc[r + 128 + ROPE_DIM * j:r + 128 + ROPE_DIM * (j + 1), :] for j in range(4))
                    tot = kr + jnp.concatenate([sw[half_r:], sw[:half_r]], axis=0)
                    for half in range(2):
                        dw_ref[half, O_MZ:O_MZ + ROPE_DIM, :] = (
                            tot[:, half * HALF_D:(half + 1) * HALF_D].astype(BF16))

    assert O_KR >= (nc - 1) * chunk
    return pl.pallas_call(
        body, name="dwin", grid=(nc, nrow),
        out_shape=_sds((2, IN_WIDTH, HALF_D), BF16),
        in_specs=[pl.BlockSpec((tm, D_MODEL), lambda c, i: (i, 0)),
                  pl.BlockSpec((tm, chunk), lambda c, i: (i, c))],
        out_specs=pl.BlockSpec((2, IN_WIDTH, HALF_D), lambda c, i: (0, 0, 0)),
        scratch_shapes=[pltpu.VMEM((chunk, D_MODEL), F32)],
        compiler_params=_params(("arbitrary", "arbitrary")),
    )(h, dproj)


def _internal_weights(w_in_t, w_uq, w_ukv):
    krot_t = w_in_t[:, O_MZ:O_MZ + ROPE_DIM]
    krot_sw = krot_t.reshape(2, 2, ROPE_DIM // 2, HALF_D)[:, ::-1].reshape(2, ROPE_DIM, HALF_D)
    w_kr = jnp.concatenate([jnp.tile(krot_t, (1, 4, 1)), jnp.tile(krot_sw, (1, 4, 1))], axis=1)
    uq = w_uq.reshape(Q_RANK, N_HEADS, 96)
    wp = uq[:, :, 64:].reshape(Q_RANK, 256)
    w_q = jnp.concatenate([uq[:, :, :64].reshape(Q_RANK, 512), wp, _swap_halves(wp, 32)], axis=1)
    ukv = w_ukv.reshape(KV_RANK, N_HEADS, 128)
    w_kv = jnp.concatenate([ukv[:, :, :64].reshape(KV_RANK, 512), ukv[:, :, 64:].reshape(KV_RANK, 512)], axis=1)
    return w_kr, w_q, w_kv


def _true_weight_grads(dwq, dwkv):
    dwp = dwq[:, 512:768] + _swap_halves(dwq[:, 768:1024], 32)
    g_uq = jnp.concatenate([dwq[:, :512].reshape(Q_RANK, N_HEADS, 64), dwp.reshape(Q_RANK, N_HEADS, 32)],
                           axis=2).reshape(Q_RANK, 768)
    g_ukv = jnp.concatenate([dwkv[:, :512].reshape(KV_RANK, N_HEADS, 64), dwkv[:, 512:].reshape(KV_RANK, N_HEADS, 64)],
                            axis=2).reshape(KV_RANK, 1024)
    return g_uq, g_ukv


def _swap_halves(w, group):
    r, n = w.shape
    return w.reshape(r, n // group, 2, group // 2)[:, :, ::-1, :].reshape(r, n)


def _pack_shards(parts):
    return jnp.concatenate([p.reshape(-1, LANES) for p in parts], axis=0)


def _unpack_small(gw):
    offs = [0]
    for r in SMALL_ROWS:
        offs.append(offs[-1] + r)

    def cols(i, rows, shard_cols):
        blk = gw[:, offs[i]:offs[i + 1]].reshape(N_CHIPS, rows, shard_cols)
        return blk.transpose(1, 0, 2).reshape(rows, N_CHIPS * shard_cols)

    return (cols(0, Q_RANK, 192), cols(1, KV_RANK, 256), cols(2, 512, 256), cols(3, 512, 256),
            gw[:, offs[4]:offs[5]].reshape(D_MODEL, D_MODEL))


def _chip_major(g, shard_cols):
    r = g.shape[0]
    return g.reshape(r, N_CHIPS, shard_cols).transpose(1, 0, 2).reshape(N_CHIPS, -1, LANES)


def kernel(x, c, positions, w_ada, b_ada, norm_gain, w_in, q_norm_gain, w_uq, kv_norm_gain, w_ukv, w_branch_a, w_branch_b, w_out, final_norm_gain, loss_target, m_w_ada, m_b_ada, m_norm_gain, m_w_in, m_q_norm_gain, m_w_uq, m_kv_norm_gain, m_w_ukv, m_w_branch_a, m_w_branch_b, m_w_out, m_final_norm_gain, v_w_ada, v_b_ada, v_norm_gain, v_w_in, v_q_norm_gain, v_w_uq, v_kv_norm_gain, v_w_ukv, v_w_branch_a, v_w_branch_b, v_w_out, v_final_norm_gain):
    ix, iy, ic = lax.axis_index("x"), lax.axis_index("y"), lax.axis_index("c")
    me = 4 * ix + 2 * iy + ic
    chip = 2 * ix + iy
    xs = x[0]
    tgt = loss_target[0]
    s_len = xs.shape[0]

    w_in_t = jnp.swapaxes(w_in[0], 0, 1)
    w_in_tb = w_in_t.astype(BF16)
    pack_in = jnp.stack([w_in_tb[:, :HALF_D], w_in_tb[:, HALF_D:]], axis=0)
    small_shards = (w_uq[0], w_ukv[0], w_branch_a[0], w_branch_b[0], w_out[0])
    pack_small = _pack_shards([s.astype(BF16) for s in small_shards]).reshape(2, SMALL_HALF, LANES)
    mg, call, gw_in, gw_small = _gather_call(c, w_ada[0], pack_in, pack_small)
    mod = mg.transpose(1, 0, 2).reshape(N_DEV, 3 * D_MODEL) + b_ada
    mod_me = lax.dynamic_slice_in_dim(mod, me, 1, axis=0)
    shift, scale, gate = mod_me[:, :D_MODEL], mod_me[:, D_MODEL:2 * D_MODEL], mod_me[:, 2 * D_MODEL:]

    f_in_t = gw_in.reshape(2, IN_WIDTH, HALF_D)
    f_uq, f_ukv, f_a, f_b, f_out = _unpack_small(gw_small.reshape(N_CHIPS, SMALL_TOTAL, LANES))
    w_kr, w_q, w_kv = _internal_weights(f_in_t, f_uq, f_ukv)

    inv_freq = ROPE_BASE ** (-jnp.arange(0, ROPE_DIM, 2, dtype=F32) / ROPE_DIM)
    ang = positions[0].astype(F32)[:, None] * inv_freq
    cs, sn = jnp.cos(ang), jnp.sin(ang)
    cos128 = jnp.tile(jnp.concatenate([cs, cs], axis=1), (1, 4))
    sin128 = jnp.tile(jnp.concatenate([-sn, sn], axis=1), (1, 4))

    (h, sq, sk, sv, sz, cq, ckv, mz, ga, gb, kpt, qn, qp, kn, vv) = _inproj_call(
        xs, shift, scale, norm_gain, f_in_t, w_kr, w_q, w_kv, q_norm_gain, kv_norm_gain, cos128, sin128)
    oa, lt, first = _sb_fwd_call(sq, sk, sv)
    ob, lse = _mla_fwd_call(qn, qp, kn, kpt, vv)

    gf = final_norm_gain.reshape(1, D_MODEL)
    (dx2, doa, dob, dsz, dmz, dga, dgb, dwo, dwa, dwb, dgf, dgate, loss_p) = _post_call(
        xs, tgt, oa, ob, sz, mz, ga, gb, gate, gf, f_a, f_b, f_out)

    dsq, dsk_t, dsv_t = _sb_bwd_call(first[:, :, 0, 0].reshape(-1), sq, sk, sv, doa, lt)
    dqn, dqp, dkn_t, dkpt_t, dvv_t = _mla_bwd_call(qn, qp, kn, kpt, vv, ob, dob, lse)

    dproj, dwq, dwkv, dqg, dkvg = _bwdprep_call(
        dsq, dsk_t, dsv_t, dsz, dqn, dqp, dkn_t, dvv_t, dkpt_t, dmz, dga, dgb, cq, ckv, cos128, sin128,
        q_norm_gain, kv_norm_gain, w_q, w_kv)
    grad_x, dshift, dscale, dg1 = _dh_call(dproj, f_in_t, w_kr, xs, dx2, scale, norm_gain)
    g_in_t = _dwin_call(h, dproj)
    g_uq, g_ukv = _true_weight_grads(dwq, dwkv)

    g_in_pieces = g_in_t.reshape(N_DEV, IN_SHARD, HALF_D)
    g_small = jnp.concatenate([
        _chip_major(g_uq, 192).astype(BF16), _chip_major(g_ukv, 256).astype(BF16),
        dwa.reshape(N_CHIPS, -1, LANES), dwb.reshape(N_CHIPS, -1, LANES),
        dwo.reshape(N_CHIPS, -1, LANES)], axis=1).reshape(N_DEV, SMALL_HALF, LANES)
    small = _pack_vectors_call([dshift, dscale, dgate, dg1, dqg, dkvg, dgf, loss_p])
    full_in, full_small, svg = _reduce_call(g_in_pieces, g_small, small)
    full = full_small.reshape(SMALL_TOTAL, LANES)
    offs = [0]
    for r in SMALL_ROWS:
        offs.append(offs[-1] + r)
    gs_uq = full[offs[0]:offs[1]].reshape(Q_RANK, 192)
    gs_ukv = full[offs[1]:offs[2]].reshape(KV_RANK, 256)
    gs_a = full[offs[2]:offs[3]].reshape(512, 256)
    gs_b = full[offs[3]:offs[4]].reshape(512, 256)
    gs_out = full[offs[4]:offs[5]].reshape(256, D_MODEL)

    svm = svg.reshape(N_DEV, 8 * SV_COLS)
    dmod_sh = lax.dynamic_slice_in_dim(svm[:, :3 * D_MODEL], chip * 768, 768, axis=1)
    tot, gs_ada = _small_call(svm, call.T, dmod_sh)
    vec_offsets = {"b_ada": 0, "norm_gain": 3072, "q_norm_gain": 4096, "kv_norm_gain": 4480, "final_norm_gain": 4736}
    loss = tot[0, 5760]

    names = ["w_ada", "b_ada", "norm_gain", "w_in", "q_norm_gain", "w_uq", "kv_norm_gain", "w_ukv",
             "w_branch_a", "w_branch_b", "w_out", "final_norm_gain"]
    ws = [w_ada[0], b_ada, norm_gain, w_in_t, q_norm_gain, w_uq[0], kv_norm_gain, w_ukv[0],
          w_branch_a[0], w_branch_b[0], w_out[0], final_norm_gain.reshape(1, D_MODEL)]
    gs = [gs_ada, None, None, full_in, None, gs_uq, None, gs_ukv, gs_a, gs_b, gs_out, None]
    ms = [m_w_ada[0], m_b_ada, m_norm_gain, jnp.swapaxes(m_w_in[0], 0, 1), m_q_norm_gain, m_w_uq[0],
          m_kv_norm_gain, m_w_ukv[0], m_w_branch_a[0], m_w_branch_b[0], m_w_out[0],
          m_final_norm_gain.reshape(1, D_MODEL)]
    vs = [v_w_ada[0], v_b_ada, v_norm_gain, jnp.swapaxes(v_w_in[0], 0, 1), v_q_norm_gain, v_w_uq[0],
          v_kv_norm_gain, v_w_ukv[0], v_w_branch_a[0], v_w_branch_b[0], v_w_out[0],
          v_final_norm_gain.reshape(1, D_MODEL)]
    refs = [w_ada, b_ada, norm_gain, w_in, q_norm_gain, w_uq, kv_norm_gain, w_ukv,
            w_branch_a, w_branch_b, w_out, final_norm_gain]
    vec_ids = [k for k, n in enumerate(names) if n in vec_offsets]
    vec_outs = dict(zip(vec_ids, _adamw_vectors_call(
        tot, [vec_offsets[names[k]] for k in vec_ids], [ws[k] for k in vec_ids], [ms[k] for k in vec_ids],
        [vs[k] for k in vec_ids])))
    mat_ids = [names.index(n) for n in ("w_uq", "w_ukv", "w_branch_a", "w_branch_b", "w_out")]
    for k, o_ in zip(mat_ids, _adamw_group_call(*[[lst[k] for k in mat_ids] for lst in (ws, gs, ms, vs)])):
        vec_outs[k] = (gs[k],) + o_
    grads, deltas, new_ms, new_vs = [], [], [], []
    for k, (n, w_, g_, m_, v_, ref) in enumerate(zip(names, ws, gs, ms, vs, refs)):
        outs = vec_outs[k] if k in vec_outs else _adamw_call(n, w_, g_, m_, v_)
        if n == "w_in":
            outs = tuple(jnp.swapaxes(o_, 0, 1) for o_ in outs)
        for lst, o_ in zip((grads, deltas, new_ms, new_vs), outs):
            lst.append(o_.reshape(ref.shape))

    return (loss, grad_x.reshape(x.shape), *grads, *deltas, *new_ms, *new_vs)
```

```python
import math

import jax
import jax.numpy as jnp
from jax import lax
from jax.experimental import pallas as pl
from jax.experimental.pallas import tpu as pltpu

F32 = jnp.float32
BF16 = jnp.bfloat16

D_MODEL = 1024
SB_WIDTH = 512
MLA_WIDTH = 512
Q_RANK = 384
KV_RANK = 256
ROPE_DIM = 32
N_HEADS = 8
IN_WIDTH = 5280
EPS = 1e-6
ROPE_BASE = 10000.0
MLA_SCALE = 1.0 / math.sqrt(96.0)
SB_SCALE = 0.125
LOG2E = 1.4426950408889634

ADAM_LR = 0.001
ADAM_B1 = 0.9
ADAM_B2 = 0.999
ADAM_EPS = 1e-08
ADAM_WD = 0.01
ADAM_STEP = 10

O_SQ, O_SK, O_SV, O_SZ, O_CQ, O_CKV, O_MZ, O_GA, O_GB, O_KR, O_END = (
    0, 512, 1024, 1536, 2048, 2432, 2688, 3200, 4224, 5248, 5504)
W_INT = 5632

N_CHIPS = 4
N_DEV = 8
LANES = 128
SV_COLS = 768

ROW_TILE = 256
ATT_TILE = 256
ATT_Q_TILES = 2
FWD_Q_TILES = 4
SB_Q_TILES = 1
MLA_KEY_TILE = 512
VMEM_LIMIT = 56 * 1024 * 1024

MESH = pl.DeviceIdType.MESH


def _dot(a, b):
    return lax.dot_general(a, b, (((1,), (0,)), ((), ())), preferred_element_type=F32)


def _dot_nt(a, b):
    return lax.dot_general(a, b, (((1,), (1,)), ((), ())), preferred_element_type=F32)


def _dot_tn(a, b):
    return lax.dot_general(a, b, (((0,), (0,)), ((), ())), preferred_element_type=F32)


def _sigmoid(z):
    return 1.0 / (1.0 + jnp.exp2(z * (-LOG2E)))


def _params(sem=None):
    if sem is None:
        return pltpu.CompilerParams(vmem_limit_bytes=VMEM_LIMIT)
    return pltpu.CompilerParams(dimension_semantics=sem, vmem_limit_bytes=VMEM_LIMIT)


def _rows(tm, n):
    return pl.BlockSpec((tm, n), lambda i: (i, 0))


def _cols(n, tm):
    return pl.BlockSpec((n, tm), lambda i: (0, i))


def _whole(shape):
    nd = len(shape)
    return pl.BlockSpec(shape, lambda i: (0,) * nd)


def _sds(shape, dtype):
    return jax.ShapeDtypeStruct(shape, dtype)


def _flip(v, d):
    return 1 - v if d else v


def _weight_rows(a, b):
    if a >= O_KR:
        return True, slice(a - O_KR, b - O_KR)
    shift = ROPE_DIM if a >= O_MZ else 0
    return False, slice(a + shift, b + shift)


def _inproj_call(x, shift, scale, g1, w_in_t, w_kr, w_q, w_kv, qg, kvg, cos128, sin128):
    s_len = x.shape[0]
    tm = min(ROW_TILE, s_len)

    def body(x_ref, sh_ref, sc_ref, g1_ref, win_ref, wkr_ref, wq_ref, wkv_ref, qg_ref, kvg_ref, cos_ref, sin_ref,
             h_ref, sq_ref, sk_ref, sv_ref, sz_ref, cq_ref, ckv_ref, mz_ref, ga_ref, gb_ref, kpt_ref,
             qn_ref, qp_ref, kn_ref, vv_ref):
        xt = x_ref[...]
        r = lax.rsqrt(jnp.mean(xt * xt, axis=-1, keepdims=True) + EPS)
        h = (xt * r * g1_ref[...]) * (1.0 + sc_ref[...]) + sh_ref[...]
        hb = h.astype(BF16)
        h_ref[...] = hb

        def seg(a, b):
            from_kr, rows = _weight_rows(a, b)
            w_ref = wkr_ref if from_kr else win_ref
            return _dot_nt(hb[:, :HALF_D], w_ref[0, rows, :]) + _dot_nt(hb[:, HALF_D:], w_ref[1, rows, :])

        sq_ref[...] = (seg(O_SQ, O_SK) * SB_SCALE).astype(BF16)
        sk_ref[...] = seg(O_SK, O_SV).astype(BF16)
        sv_ref[...] = seg(O_SV, O_SZ).astype(BF16)
        sz_ref[...] = seg(O_SZ, O_CQ)
        mz_ref[...] = seg(O_MZ, O_GA)
        ga_ref[...] = seg(O_GA, O_GB)
        gb_ref[...] = seg(O_GB, O_KR)
        cos = cos_ref[...]
        sin = sin_ref[...]
        kr = seg(O_KR, O_END)
        kpt_ref[...] = (kr[:, :128] * cos + kr[:, 128:] * sin).astype(BF16)

        cq = seg(O_CQ, O_CKV)
        cq_ref[...] = cq
        rq = lax.rsqrt(jnp.mean(cq * cq, axis=-1, keepdims=True) + EPS)
        cqn = (cq * rq * qg_ref[...]).astype(BF16)
        qa = _dot(cqn, wq_ref[...])
        qn_ref[...] = qa[:, :512].astype(BF16)
        qp_ref[...] = (qa[:, 512:768] * jnp.tile(cos, (1, 2)) + qa[:, 768:] * jnp.tile(sin, (1, 2))).astype(BF16)

        ckv = seg(O_CKV, O_MZ)
        ckv_ref[...] = ckv
        rk = lax.rsqrt(jnp.mean(ckv * ckv, axis=-1, keepdims=True) + EPS)
        ckvn = (ckv * rk * kvg_ref[...]).astype(BF16)
        kva = _dot(ckvn, wkv_ref[...])
        kn_ref[...] = kva[:, :512].astype(BF16)
        vv_ref[...] = kva[:, 512:].astype(BF16)

    outs = [
        (D_MODEL, BF16), (512, BF16), (512, BF16), (512, BF16), (512, F32), (Q_RANK, F32), (KV_RANK, F32),
        (512, F32), (D_MODEL, F32), (D_MODEL, F32), (128, BF16), (512, BF16), (256, BF16), (512, BF16), (512, BF16),
    ]
    return pl.pallas_call(
        body, name="inproj", grid=(s_len // tm,),
        out_shape=tuple(_sds((s_len, n), dt) for n, dt in outs),
        in_specs=[_rows(tm, D_MODEL), _whole((1, D_MODEL)), _whole((1, D_MODEL)), _whole((1, D_MODEL)),
                  _whole((2, IN_WIDTH, HALF_D)), _whole((2, O_END - O_KR, HALF_D)),
                  _whole((Q_RANK, 1024)), _whole((KV_RANK, 1024)),
                  _whole((1, Q_RANK)), _whole((1, KV_RANK)), _rows(tm, LANES), _rows(tm, LANES)],
        out_specs=tuple(_rows(tm, n) for n, _ in outs),
        compiler_params=_params(("parallel",)),
    )(x, shift, scale, g1, w_in_t, w_kr, w_q, w_kv, qg, kvg, cos128, sin128)


Z_CLAMP = 80.0 * LOG2E
RUN_CUTOFF = 110.0 * LOG2E


def _softplus_clamped(z):
    zc = jnp.minimum(z * LOG2E, Z_CLAMP)
    return zc, jnp.log2(1.0 + jnp.exp2(zc))


def _tri_sum(a, tri):
    return _dot(a.astype(BF16), tri)


def _sb_fwd_call(q, k, v):
    s_len = q.shape[0]
    tk = min(ATT_TILE, s_len)
    tq = min(SB_Q_TILES * ATT_TILE, s_len)
    r = tq // tk
    nq = s_len // tq

    def body(q_ref, k_ref, v_ref, o_ref, lt_ref, first_ref):
        i = pl.program_id(1)
        q2 = q_ref[...]
        lane = lax.broadcasted_iota(jnp.int32, (1, 256), 1)
        krow = lax.broadcasted_iota(jnp.int32, (tk, tk), 0)
        kcol = lax.broadcasted_iota(jnp.int32, (tk, tk), 1)
        row = lax.broadcasted_iota(jnp.int32, (tq, tk), 0)
        col = lax.broadcasted_iota(jnp.int32, (tq, tk), 1)
        later = (krow > kcol).astype(BF16)
        valids = [col + u * tk < row for u in range(r)]
        hms = [(lane // 64) == hh for hh in range(4)]
        qms = [jnp.where(hm, q2, jnp.zeros_like(q2)) for hm in hms]

        def block(j, carry, valid):
            runs, acc = list(carry[:4]), carry[4]
            off = pl.multiple_of(j * tk, tk)
            kb = k_ref[pl.ds(off, tk), :]
            vb = v_ref[pl.ds(off, tk), :]
            ws = []
            for hh in range(4):
                zc, sp = _softplus_clamped(_dot_nt(qms[hh], kb))
                lm = jnp.where(valid, sp, 0.0) if valid is not None else sp
                suf = _tri_sum(lm, later)
                w = jnp.exp2(zc - sp - suf - runs[hh])
                if valid is not None:
                    w = jnp.where(valid, w, 0.0)
                ws.append(w.astype(BF16))
                runs[hh] = runs[hh] + jnp.sum(lm, axis=1, keepdims=True)
            vstack = jnp.concatenate([jnp.where(hm, vb, jnp.zeros_like(vb)) for hm in hms], axis=0)
            acc = acc + _dot(jnp.concatenate(ws, axis=1), vstack)
            return (*runs, acc)

        zero = jnp.zeros((tq, 1), F32)
        carry = (zero, zero, zero, zero, jnp.zeros((tq, 256), F32))
        for u in reversed(range(r)):
            carry = block(i * r + u, carry, valids[u])

        def least_run(runs):
            return jnp.min(jnp.minimum(jnp.minimum(runs[0], runs[1]), jnp.minimum(runs[2], runs[3])))

        n_full = i * r

        def unfinished(state):
            return jnp.logical_and(state[0] < n_full, state[1] <= RUN_CUTOFF)

        def visit(state):
            cr = block(n_full - 1 - state[0], state[2:], None)
            return (state[0] + 1, least_run(cr[:4]), *cr)

        state = lax.while_loop(unfinished, visit, (jnp.int32(0), least_run(carry[:4]), *carry))
        carry = state[2:]
        first_ref[...] = jnp.full(first_ref.shape, n_full - state[0], jnp.int32)
        for hh in range(4):
            lt_ref[0, :, hh:hh + 1] = carry[hh]
        o_ref[...] = carry[4]

    return pl.pallas_call(
        body, name="sb_fwd", grid=(2, nq),
        out_shape=(_sds((s_len, SB_WIDTH), F32), _sds((2, s_len, 4), F32), _sds((2, nq, 8, 128), jnp.int32)),
        in_specs=[pl.BlockSpec((tq, 256), lambda g, i: (i, g)),
                  pl.BlockSpec((s_len, 256), lambda g, i: (0, g)),
                  pl.BlockSpec((s_len, 256), lambda g, i: (0, g))],
        out_specs=(pl.BlockSpec((tq, 256), lambda g, i: (i, g)),
                   pl.BlockSpec((1, tq, 4), lambda g, i: (g, i, 0)),
                   pl.BlockSpec((1, 1, 8, 128), lambda g, i: (g, i, 0, 0))),
        compiler_params=_params(("parallel", "parallel")),
    )(q, k, v)


def _sb_bwd_call(first, q, k, v, do, lt):
    s_len = q.shape[0]
    tk = min(ATT_TILE, s_len)
    tq = min(SB_Q_TILES * ATT_TILE, s_len)
    r = tq // tk
    nq = s_len // tq
    nq_fwd = first.shape[0] // 2
    per_fwd = nq // nq_fwd

    def body(first_ref, q_ref, k_ref, v_ref, do_ref, lt_ref, dq_ref, dk_ref, dv_ref):
        g = pl.program_id(0)
        i = pl.program_id(1)

        @pl.when(i == 0)
        def _():
            dk_ref[...] = jnp.zeros_like(dk_ref)
            dv_ref[...] = jnp.zeros_like(dv_ref)

        q2 = q_ref[...]
        do2 = do_ref[...].astype(BF16)
        lane = lax.broadcasted_iota(jnp.int32, (1, 256), 1)
        krow = lax.broadcasted_iota(jnp.int32, (tk, tk), 0)
        kcol = lax.broadcasted_iota(jnp.int32, (tk, tk), 1)
        row = lax.broadcasted_iota(jnp.int32, (tq, tk), 0)
        col = lax.broadcasted_iota(jnp.int32, (tq, tk), 1)
        earlier = (krow < kcol).astype(BF16)
        later = (krow > kcol).astype(BF16)
        valids = [col + u * tk < row for u in range(r)]
        hms = [(lane // 64) == hh for hh in range(4)]
        qms = [jnp.where(hm, q2, jnp.zeros_like(q2)) for hm in hms]
        doms = [jnp.where(hm, do2, jnp.zeros_like(do2)) for hm in hms]
        ltots = [lt_ref[0, :, hh:hh + 1] for hh in range(4)]
        q2t = jnp.transpose(q2.astype(F32))
        do2t = jnp.transpose(do_ref[...])
        subl = lax.broadcasted_iota(jnp.int32, (256, 1), 0)
        qtstack = jnp.concatenate(
            [jnp.where((subl // 64) == hh, q2t, 0.0).astype(BF16) for hh in range(4)], axis=1)
        dotstack = jnp.concatenate(
            [jnp.where((subl // 64) == hh, do2t, 0.0).astype(BF16) for hh in range(4)], axis=1)

        def block(j, carry, valid):
            lpre, ppre, dq = list(carry[0:4]), list(carry[4:8]), carry[8]
            off = pl.multiple_of(j * tk, tk)
            kb = k_ref[pl.ds(off, tk), :]
            vb = v_ref[pl.ds(off, tk), :]
            dzs, avs = [], []
            for hh in range(4):
                zc, sp = _softplus_clamped(_dot_nt(qms[hh], kb))
                lsig = zc - sp
                lm = jnp.where(valid, sp, 0.0) if valid is not None else sp
                rowsum = jnp.sum(lm, axis=1, keepdims=True)
                between = _tri_sum(lm, later) + ((ltots[hh] - lpre[hh]) - rowsum)
                a = jnp.exp2(lsig - between)
                if valid is not None:
                    a = jnp.where(valid, a, 0.0)
                p = a * _dot_nt(doms[hh], vb)
                pbefore = ppre[hh] + _tri_sum(p, earlier)
                dz = p - jnp.exp2(lsig) * (p + pbefore)
                if valid is not None:
                    dz = jnp.where(valid, dz, 0.0)
                dzs.append(dz.astype(BF16))
                avs.append(a.astype(BF16))
                lpre[hh] = lpre[hh] + rowsum
                ppre[hh] = ppre[hh] + jnp.sum(p, axis=1, keepdims=True)
            kstack = jnp.concatenate([jnp.where(hm, kb, jnp.zeros_like(kb)) for hm in hms], axis=0)
            dq = dq + _dot(jnp.concatenate(dzs, axis=1), kstack)
            dk_ref[:, pl.ds(off, tk)] += _dot(qtstack, jnp.concatenate(dzs, axis=0))
            dv_ref[:, pl.ds(off, tk)] += _dot(dotstack, jnp.concatenate(avs, axis=0))
            return (*lpre, *ppre, dq)

        zero = jnp.zeros((tq, 1), F32)
        start = jnp.minimum(first_ref[g * nq_fwd + i // per_fwd], i * r)
        carry = lax.fori_loop(start, i * r, lambda j, cr: block(j, cr, None),
                              (zero,) * 8 + (jnp.zeros((tq, 256), F32),))
        for u in range(r):
            carry = block(i * r + u, carry, valids[u])
        dq_ref[...] = carry[8].astype(BF16)

    return pl.pallas_call(
        body, name="sb_bwd",
        out_shape=(_sds((s_len, SB_WIDTH), BF16), _sds((SB_WIDTH, s_len), F32), _sds((SB_WIDTH, s_len), F32)),
        grid_spec=pltpu.PrefetchScalarGridSpec(
            num_scalar_prefetch=1, grid=(2, nq),
            in_specs=[pl.BlockSpec((tq, 256), lambda g, i, f: (i, g)),
                      pl.BlockSpec((s_len, 256), lambda g, i, f: (0, g)),
                      pl.BlockSpec((s_len, 256), lambda g, i, f: (0, g)),
                      pl.BlockSpec((tq, 256), lambda g, i, f: (i, g)),
                      pl.BlockSpec((1, tq, 4), lambda g, i, f: (g, i, 0))],
            out_specs=(pl.BlockSpec((tq, 256), lambda g, i, f: (i, g)),
                       pl.BlockSpec((256, s_len), lambda g, i, f: (g, 0)),
                       pl.BlockSpec((256, s_len), lambda g, i, f: (g, 0)))),
        compiler_params=_params(("parallel", "arbitrary")),
    )(first, q, k, v, do, lt)


def _mla_fwd_call(qn, qp, kn, kpt, v):
    s_len = qn.shape[0]
    tk = min(MLA_KEY_TILE, s_len)
    tq = min(FWD_Q_TILES * ATT_TILE, s_len)
    r = tq // tk
    nq = s_len // tq

    def body(qn_ref, qp_ref, kn_ref, kpt_ref, v_ref, o_ref, lse_ref):
        i = pl.program_id(1)
        qn2 = qn_ref[...]
        qp2 = qp_ref[...]
        lane256 = lax.broadcasted_iota(jnp.int32, (1, 256), 1)
        lane128 = lax.broadcasted_iota(jnp.int32, (1, 128), 1)
        krow = lax.broadcasted_iota(jnp.int32, (tk, tk), 0)
        kcol = lax.broadcasted_iota(jnp.int32, (tk, tk), 1)
        row = lax.broadcasted_iota(jnp.int32, (tq, tk), 0)
        col = lax.broadcasted_iota(jnp.int32, (tq, tk), 1)
        valids = [col + u * tk <= row for u in range(r)]
        m64s = [(lane256 // 64) == hh for hh in range(4)]
        half = [(lane128 // 64) == u for u in range(2)]
        m32s = [(lane128 // 32) == hh for hh in range(4)]
        qcs = []
        for hh in range(4):
            qpair = qn2[:, 128 * (hh // 2):128 * (hh // 2) + 128]
            qcs.append(jnp.concatenate([jnp.where(half[hh % 2], qpair, jnp.zeros_like(qpair)),
                                        jnp.where(m32s[hh], qp2, jnp.zeros_like(qp2))], axis=1))

        def by_head(vals):
            return jnp.where(m64s[0], vals[0], jnp.where(m64s[1], vals[1], jnp.where(m64s[2], vals[2], vals[3])))

        def block(j, carry, valid):
            ms, ls, acc = list(carry[0:4]), list(carry[4:8]), carry[8]
            off = pl.multiple_of(j * tk, tk)
            knb = kn_ref[pl.ds(off, tk), :]
            kpb = kpt_ref[pl.ds(off, tk), :]
            vb = v_ref[pl.ds(off, tk), :]
            kcs = [jnp.concatenate([knb[:, 128 * pp:128 * pp + 128], kpb], axis=1) for pp in range(2)]
            ps, alphas = [], []
            for hh in range(4):
                s = _dot_nt(qcs[hh], kcs[hh // 2]) * (MLA_SCALE * LOG2E)
                if valid is not None:
                    s = jnp.where(valid, s, -1e30)
                mn = jnp.maximum(ms[hh], jnp.max(s, axis=1, keepdims=True))
                p = jnp.exp2(s - mn)
                alpha = jnp.exp2(ms[hh] - mn)
                ls[hh] = alpha * ls[hh] + jnp.sum(p, axis=1, keepdims=True)
                ms[hh] = mn
                ps.append(p.astype(BF16))
                alphas.append(alpha)
            pvs = []
            for pp in range(2):
                vpair = vb[:, 128 * pp:128 * pp + 128]
                vstack = jnp.concatenate([jnp.where(hf, vpair, jnp.zeros_like(vpair)) for hf in half], axis=0)
                pvs.append(_dot(jnp.concatenate(ps[2 * pp:2 * pp + 2], axis=1), vstack))
            acc = by_head(alphas) * acc + jnp.concatenate(pvs, axis=1)
            return (*ms, *ls, acc)

        neg = jnp.full((tq, 1), -1e30, F32)
        zero = jnp.zeros((tq, 1), F32)
        carry = lax.fori_loop(0, i * r, lambda j, cr: block(j, cr, None),
                              (neg,) * 4 + (zero,) * 4 + (jnp.zeros((tq, 256), F32),))
        for u in range(r):
            carry = block(i * r + u, carry, valids[u])
        o_ref[...] = carry[8] / by_head(list(carry[4:8]))
        for hh in range(4):
            lse_ref[0, :, hh:hh + 1] = (carry[hh] + jnp.log2(carry[4 + hh])) * (1.0 / LOG2E)

    return pl.pallas_call(
        body, name="mla_fwd", grid=(2, nq),
        out_shape=(_sds((s_len, MLA_WIDTH), F32), _sds((2, s_len, 4), F32)),
        in_specs=[pl.BlockSpec((tq, 256), lambda g, i: (i, g)),
                  pl.BlockSpec((tq, 128), lambda g, i: (i, g)),
                  pl.BlockSpec((s_len, 256), lambda g, i: (0, g)),
                  pl.BlockSpec((s_len, 128), lambda g, i: (0, 0)),
                  pl.BlockSpec((s_len, 256), lambda g, i: (0, g))],
        out_specs=(pl.BlockSpec((tq, 256), lambda g, i: (i, g)),
                   pl.BlockSpec((1, tq, 4), lambda g, i: (g, i, 0))),
        compiler_params=_params(("parallel", "parallel")),
    )(qn, qp, kn, kpt, v)


def _mla_bwd_call(qn, qp, kn, kpt, v, o, do, lse):
    s_len = qn.shape[0]
    tk = min(MLA_KEY_TILE, s_len)
    tq = min(ATT_Q_TILES * ATT_TILE, s_len)
    r = tq // tk
    nq = s_len // tq

    def body(qn_ref, qp_ref, kn_ref, kpt_ref, v_ref, o_ref, do_ref, lse_ref,
             dqn_ref, dqp_ref, dkn_ref, dkpt_ref, dv_ref):
        g = pl.program_id(0)
        i = pl.program_id(1)

        @pl.when(i == 0)
        def _():
            dkn_ref[...] = jnp.zeros_like(dkn_ref)
            dv_ref[...] = jnp.zeros_like(dv_ref)

        @pl.when((i == 0) & (g == 0))
        def _():
            dkpt_ref[...] = jnp.zeros_like(dkpt_ref)

        qn2 = qn_ref[...]
        qp2 = qp_ref[...]
        dof = do_ref[...]
        dob = dof.astype(BF16)
        prod = dof * o_ref[...]
        lane256 = lax.broadcasted_iota(jnp.int32, (1, 256), 1)
        lane128 = lax.broadcasted_iota(jnp.int32, (1, 128), 1)
        krow = lax.broadcasted_iota(jnp.int32, (tk, tk), 0)
        kcol = lax.broadcasted_iota(jnp.int32, (tk, tk), 1)
        row = lax.broadcasted_iota(jnp.int32, (tq, tk), 0)
        col = lax.broadcasted_iota(jnp.int32, (tq, tk), 1)
        valids = [col + u * tk <= row for u in range(r)]
        m64s = [(lane256 // 64) == hh for hh in range(4)]
        half = [(lane128 // 64) == u for u in range(2)]
        m32s = [(lane128 // 32) == hh for hh in range(4)]
        qcs, doms = [], []
        for hh in range(4):
            sl = slice(128 * (hh // 2), 128 * (hh // 2) + 128)
            qpair = qn2[:, sl]
            dpair = dob[:, sl]
            qcs.append(jnp.concatenate([jnp.where(half[hh % 2], qpair, jnp.zeros_like(qpair)),
                                        jnp.where(m32s[hh], qp2, jnp.zeros_like(qp2))], axis=1))
            doms.append(jnp.where(half[hh % 2], dpair, jnp.zeros_like(dpair)))
        dsums = [jnp.sum(jnp.where(m64, prod, 0.0), axis=1, keepdims=True) * MLA_SCALE for m64 in m64s]
        lses = [lse_ref[0, :, hh:hh + 1] * LOG2E for hh in range(4)]
        qn2t = jnp.transpose(qn2.astype(F32))
        qp2t = jnp.transpose(qp2.astype(F32))
        do2t = jnp.transpose(dof)
        sub128 = lax.broadcasted_iota(jnp.int32, (128, 1), 0)
        qtstacks, dotstacks = [], []
        for pp in range(2):
            qts, dts = [], []
            for u in range(2):
                hh = 2 * pp + u
                qts.append(jnp.concatenate(
                    [jnp.where((sub128 // 64) == u, qn2t[128 * pp:128 * pp + 128, :], 0.0),
                     jnp.where((sub128 // 32) == hh, qp2t, 0.0)], axis=0).astype(BF16))
                dts.append(jnp.where((sub128 // 64) == u, do2t[128 * pp:128 * pp + 128, :], 0.0).astype(BF16))
            qtstacks.append(jnp.concatenate(qts, axis=1))
            dotstacks.append(jnp.concatenate(dts, axis=1))

        def block(j, carry, valid):
            dqn, dqp = carry
            off = pl.multiple_of(j * tk, tk)
            knb = kn_ref[pl.ds(off, tk), :]
            kpb = kpt_ref[pl.ds(off, tk), :]
            vb = v_ref[pl.ds(off, tk), :]
            dqn_parts = []
            dkp = None
            for pp in range(2):
                sl = slice(128 * pp, 128 * pp + 128)
                knp = knb[:, sl]
                vpair = vb[:, sl]
                kc = jnp.concatenate([knp, kpb], axis=1)
                dss, pbs, kcms = [], [], []
                for u in range(2):
                    hh = 2 * pp + u
                    s = _dot_nt(qcs[hh], kc) * (MLA_SCALE * LOG2E)
                    if valid is not None:
                        s = jnp.where(valid, s, -1e30)
                    p = jnp.exp2(s - lses[hh])
                    ds = p * (_dot_nt(doms[hh], vpair) * MLA_SCALE - dsums[hh])
                    dss.append(ds.astype(BF16))
                    pbs.append(p.astype(BF16))
                    kcms.append(jnp.concatenate([jnp.where(half[u], knp, jnp.zeros_like(knp)),
                                                 jnp.where(m32s[hh], kpb, jnp.zeros_like(kpb))], axis=1))
                dqc = _dot(jnp.concatenate(dss, axis=1), jnp.concatenate(kcms, axis=0))
                dqn_parts.append(dqc[:, :128])
                dqp = dqp + dqc[:, 128:]
                dkc = _dot(qtstacks[pp], jnp.concatenate(dss, axis=0))
                dkn_ref[128 * pp:128 * pp + 128, pl.ds(off, tk)] += dkc[:128, :]
                dkp = dkc[128:, :] if dkp is None else dkp + dkc[128:, :]
                dv_ref[128 * pp:128 * pp + 128, pl.ds(off, tk)] += _dot(dotstacks[pp], jnp.concatenate(pbs, axis=0))
            dqn = dqn + jnp.concatenate(dqn_parts, axis=1)
            dkpt_ref[:, pl.ds(off, tk)] += dkp
            return dqn, dqp

        carry = lax.fori_loop(0, i * r, lambda j, cr: block(j, cr, None),
                              (jnp.zeros((tq, 256), F32), jnp.zeros((tq, 128), F32)))
        for u in range(r):
            carry = block(i * r + u, carry, valids[u])
        dqn, dqp = carry
        dqn_ref[...] = dqn.astype(BF16)
        dqp_ref[...] = dqp.astype(BF16)

    return pl.pallas_call(
        body, name="mla_bwd", grid=(2, nq),
        out_shape=(_sds((s_len, 512), BF16), _sds((s_len, 256), BF16), _sds((512, s_len), F32),
                   _sds((128, s_len), F32), _sds((512, s_len), F32)),
        in_specs=[pl.BlockSpec((tq, 256), lambda g, i: (i, g)),
                  pl.BlockSpec((tq, 128), lambda g, i: (i, g)),
                  pl.BlockSpec((s_len, 256), lambda g, i: (0, g)),
                  pl.BlockSpec((s_len, 128), lambda g, i: (0, 0)),
                  pl.BlockSpec((s_len, 256), lambda g, i: (0, g)),
                  pl.BlockSpec((tq, 256), lambda g, i: (i, g)),
                  pl.BlockSpec((tq, 256), lambda g, i: (i, g)),
                  pl.BlockSpec((1, tq, 4), lambda g, i: (g, i, 0))],
        out_specs=(pl.BlockSpec((tq, 256), lambda g, i: (i, g)),
                   pl.BlockSpec((tq, 128), lambda g, i: (i, g)),
                   pl.BlockSpec((256, s_len), lambda g, i: (g, 0)),
                   pl.BlockSpec((128, s_len), lambda g, i: (0, 0)),
                   pl.BlockSpec((256, s_len), lambda g, i: (g, 0))),
        compiler_params=_params(("arbitrary", "arbitrary")),
    )(qn, qp, kn, kpt, v, o, do, lse)


def _post_call(x, tgt, oa, ob, sz, mz, ga, gb, gate, gf, wa, wb, wo):
    s_len = x.shape[0]
    tm = min(ROW_TILE, s_len)
    nstep = s_len // tm

    def body(x_ref, t_ref, oa_ref, ob_ref, sz_ref, mz_ref, ga_ref, gb_ref, gate_ref, gf_ref,
             wa_ref, wb_ref, wo_ref,
             dx2_ref, doa_ref, dob_ref, dsz_ref, dmz_ref, dga_ref, dgb_ref,
             dwo_out, dwa_out, dwb_out, dgf_ref, dgate_ref, loss_ref, dwo_ref, dwa_ref, dwb_ref):
        @pl.when(pl.program_id(0) == 0)
        def _():
            dwo_ref[...] = jnp.zeros_like(dwo_ref)
            dwa_ref[...] = jnp.zeros_like(dwa_ref)
            dwb_ref[...] = jnp.zeros_like(dwb_ref)
            dgf_ref[...] = jnp.zeros_like(dgf_ref)
            dgate_ref[...] = jnp.zeros_like(dgate_ref)
            loss_ref[...] = jnp.zeros_like(loss_ref)

        gate = gate_ref[...]
        gf = gf_ref[...]
        oa = oa_ref[...]
        ob = ob_ref[...]
        sz = sz_ref[...]
        mz = mz_ref[...]
        sa = _sigmoid(sz)
        sb = _sigmoid(mz)
        silu_a = sz * sa
        silu_b = mz * sb
        ua = (oa * silu_a).astype(BF16)
        ub = (ob * silu_b).astype(BF16)
        ya = _dot(ua, wa_ref[...])
        yb = _dot(ub, wb_ref[...])
        sga = _sigmoid(ga_ref[...])
        sgb = _sigmoid(gb_ref[...])
        merged = (sga * ya + sgb * yb).astype(BF16)
        out = _dot(merged, wo_ref[...])
        x2 = x_ref[...] + gate * out
        r2 = lax.rsqrt(jnp.mean(x2 * x2, axis=-1, keepdims=True) + EPS)
        xhat = x2 * r2
        err = xhat * gf - t_ref[...]
        loss_ref[...] += 0.5 * jnp.sum(jnp.sum(err * err, axis=1, keepdims=True), axis=0, keepdims=True) / D_MODEL
        dy = err * (1.0 / D_MODEL)
        dgf_ref[...] += jnp.sum(dy * xhat, axis=0, keepdims=True)
        dxhat = dy * gf
        dx2 = r2 * (dxhat - xhat * jnp.mean(dxhat * xhat, axis=-1, keepdims=True))
        dx2_ref[...] = dx2
        dgate_ref[...] += jnp.sum(dx2 * out, axis=0, keepdims=True)
        dout = (dx2 * gate).astype(BF16)
        dmerged = _dot_nt(dout, wo_ref[...])
        dwo_ref[...] += _dot_tn(merged, dout)
        dya = dmerged * sga
        dyb = dmerged * sgb
        dga_ref[...] = (dya * ya * (1.0 - sga)).astype(BF16)
        dgb_ref[...] = (dyb * yb * (1.0 - sgb)).astype(BF16)
        dyab = dya.astype(BF16)
        dybb = dyb.astype(BF16)
        dua = _dot_nt(dyab, wa_ref[...])
        dub = _dot_nt(dybb, wb_ref[...])
        dwa_ref[...] += _dot_tn(ua, dyab)
        dwb_ref[...] += _dot_tn(ub, dybb)
        doa_ref[...] = dua * silu_a
        dob_ref[...] = dub * silu_b
        dsz_ref[...] = (dua * oa * (sa * (1.0 + sz * (1.0 - sa)))).astype(BF16)
        dmz_ref[...] = (dub * ob * (sb * (1.0 + mz * (1.0 - sb)))).astype(BF16)

        @pl.when(pl.program_id(0) == nstep - 1)
        def _():
            dwo_out[...] = dwo_ref[...].astype(BF16)
            for k in range(N_CHIPS):
                dwa_out[k] = dwa_ref[:, 256 * k:256 * k + 256].astype(BF16)
                dwb_out[k] = dwb_ref[:, 256 * k:256 * k + 256].astype(BF16)

    return pl.pallas_call(
        body, name="post", grid=(nstep,),
        out_shape=(_sds((s_len, D_MODEL), F32), _sds((s_len, 512), F32), _sds((s_len, 512), F32),
                   _sds((s_len, 512), BF16), _sds((s_len, 512), BF16),
                   _sds((s_len, D_MODEL), BF16), _sds((s_len, D_MODEL), BF16),
                   _sds((D_MODEL, D_MODEL), BF16), _sds((N_CHIPS, 512, 256), BF16), _sds((N_CHIPS, 512, 256), BF16),
                   _sds((1, D_MODEL), F32), _sds((1, D_MODEL), F32), _sds((1, 128), F32)),
        in_specs=[_rows(tm, D_MODEL), _rows(tm, D_MODEL), _rows(tm, 512), _rows(tm, 512), _rows(tm, 512),
                  _rows(tm, 512), _rows(tm, D_MODEL), _rows(tm, D_MODEL), _whole((1, D_MODEL)), _whole((1, D_MODEL)),
                  _whole((512, D_MODEL)), _whole((512, D_MODEL)), _whole((D_MODEL, D_MODEL))],
        out_specs=(_rows(tm, D_MODEL), _rows(tm, 512), _rows(tm, 512), _rows(tm, 512), _rows(tm, 512),
                   _rows(tm, D_MODEL), _rows(tm, D_MODEL),
                   _whole((D_MODEL, D_MODEL)), _whole((N_CHIPS, 512, 256)), _whole((N_CHIPS, 512, 256)),
                   _whole((1, D_MODEL)), _whole((1, D_MODEL)), _whole((1, 128))),
        scratch_shapes=[pltpu.VMEM((D_MODEL, D_MODEL), F32), pltpu.VMEM((512, D_MODEL), F32),
                        pltpu.VMEM((512, D_MODEL), F32)],
        compiler_params=_params(("arbitrary",)),
    )(x, tgt, oa, ob, sz, mz, ga, gb, gate, gf, wa, wb, wo)


def _bwdprep_call(dsq, dsk, dsv, dsz, dqn, dqp, dkn, dvv, dkpt, dmz, dga, dgb, cq, ckv, cos128, sin128,
                  qg, kvg, w_q, w_kv):
    s_len = cq.shape[0]
    tm = min(ROW_TILE, s_len)

    def body(dsq_ref, dsk_ref, dsv_ref, dsz_ref, dqn_ref, dqp_ref, dkn_ref, dvv_ref, dkpt_ref, dmz_ref,
             dga_ref, dgb_ref, cq_ref, ckv_ref, cos_ref, sin_ref, qg_ref, kvg_ref, wq_ref, wkv_ref,
             dp_ref, dwq_ref, dwkv_ref, dqg_ref, dkvg_ref):
        @pl.when(pl.program_id(0) == 0)
        def _():
            dwq_ref[...] = jnp.zeros_like(dwq_ref)
            dwkv_ref[...] = jnp.zeros_like(dwkv_ref)
            dqg_ref[...] = jnp.zeros_like(dqg_ref)
            dkvg_ref[...] = jnp.zeros_like(dkvg_ref)

        cos = cos_ref[...]
        sin = sin_ref[...]
        dp_ref[:, O_SQ:O_SK] = dsq_ref[...] * jnp.asarray(SB_SCALE, BF16)
        dp_ref[:, O_SK:O_SV] = jnp.transpose(dsk_ref[...]).astype(BF16)
        dp_ref[:, O_SV:O_SZ] = jnp.transpose(dsv_ref[...]).astype(BF16)
        dp_ref[:, O_SZ:O_CQ] = dsz_ref[...]
        dp_ref[:, O_MZ:O_GA] = dmz_ref[...]
        dp_ref[:, O_GA:O_GB] = dga_ref[...]
        dp_ref[:, O_GB:O_KR] = dgb_ref[...]
        dkp = jnp.transpose(dkpt_ref[...])
        dp_ref[:, O_KR:O_KR + 128] = (dkp * cos).astype(BF16)
        dp_ref[:, O_KR + 128:O_END] = (dkp * sin).astype(BF16)
        dp_ref[:, O_END:W_INT] = jnp.zeros((tm, W_INT - O_END), BF16)

        cq = cq_ref[...]
        rq = lax.rsqrt(jnp.mean(cq * cq, axis=-1, keepdims=True) + EPS)
        cqh = cq * rq
        qg = qg_ref[...]
        cqn = (cqh * qg).astype(BF16)
        dqp = dqp_ref[...].astype(F32)
        dqa = jnp.concatenate([dqn_ref[...], (dqp * jnp.tile(cos, (1, 2))).astype(BF16),
                               (dqp * jnp.tile(sin, (1, 2))).astype(BF16)], axis=1)
        dcqn = _dot_nt(dqa, wq_ref[...])
        dwq_ref[...] += _dot_tn(cqn, dqa)
        dqg_ref[...] += jnp.sum(dcqn * cqh, axis=0, keepdims=True)
        dh = dcqn * qg
        dcq = rq * (dh - cqh * jnp.mean(dh * cqh, axis=-1, keepdims=True))
        dp_ref[:, O_CQ:O_CKV] = dcq.astype(BF16)

        ckv = ckv_ref[...]
        rk = lax.rsqrt(jnp.mean(ckv * ckv, axis=-1, keepdims=True) + EPS)
        ckh = ckv * rk
        kvg = kvg_ref[...]
        ckvn = (ckh * kvg).astype(BF16)
        dkva = jnp.concatenate([jnp.transpose(dkn_ref[...]).astype(BF16),
                                jnp.transpose(dvv_ref[...]).astype(BF16)], axis=1)
        dckvn = _dot_nt(dkva, wkv_ref[...])
        dwkv_ref[...] += _dot_tn(ckvn, dkva)
        dkvg_ref[...] += jnp.sum(dckvn * ckh, axis=0, keepdims=True)
        dh2 = dckvn * kvg
        dckv = rk * (dh2 - ckh * jnp.mean(dh2 * ckh, axis=-1, keepdims=True))
        dp_ref[:, O_CKV:O_MZ] = dckv.astype(BF16)

    return pl.pallas_call(
        body, name="bwdprep", grid=(s_len // tm,),
        out_shape=(_sds((s_len, W_INT), BF16), _sds((Q_RANK, 1024), F32), _sds((KV_RANK, 1024), F32),
                   _sds((1, Q_RANK), F32), _sds((1, KV_RANK), F32)),
        in_specs=[_rows(tm, 512), _cols(512, tm), _cols(512, tm), _rows(tm, 512), _rows(tm, 512), _rows(tm, 256),
                  _cols(512, tm), _cols(512, tm), _cols(128, tm), _rows(tm, 512), _rows(tm, D_MODEL),
                  _rows(tm, D_MODEL), _rows(tm, Q_RANK), _rows(tm, KV_RANK), _rows(tm, LANES), _rows(tm, LANES),
                  _whole((1, Q_RANK)), _whole((1, KV_RANK)), _whole((Q_RANK, 1024)), _whole((KV_RANK, 1024))],
        out_specs=(_rows(tm, W_INT), _whole((Q_RANK, 1024)), _whole((KV_RANK, 1024)),
                   _whole((1, Q_RANK)), _whole((1, KV_RANK))),
        compiler_params=_params(("arbitrary",)),
    )(dsq, dsk, dsv, dsz, dqn, dqp, dkn, dvv, dkpt, dmz, dga, dgb, cq, ckv, cos128, sin128, qg, kvg, w_q, w_kv)


def _dh_call(dproj, w_in_t, w_kr, x, dx2, scale, g1):
    s_len = x.shape[0]
    tm = min(2 * ROW_TILE, s_len)
    parts = [(a, b) + _weight_rows(a, b) for a, b in ((0, O_MZ), (O_MZ, O_KR), (O_KR, O_END))]

    def body(dp_ref, win_ref, wkr_ref, x_ref, dx2_ref, sc_ref, g1_ref, gx_ref, dsh_ref, dsc_ref, dg1_ref):
        @pl.when(pl.program_id(0) == 0)
        def _():
            dsh_ref[...] = jnp.zeros_like(dsh_ref)
            dsc_ref[...] = jnp.zeros_like(dsc_ref)
            dg1_ref[...] = jnp.zeros_like(dg1_ref)

        def half(k):
            return sum(_dot(dp_ref[:, a:b], (wkr_ref if from_kr else win_ref)[k, rows, :])
                       for a, b, from_kr, rows in parts)

        dh = jnp.concatenate([half(0), half(1)], axis=1)
        xt = x_ref[...]
        r = lax.rsqrt(jnp.mean(xt * xt, axis=-1, keepdims=True) + EPS)
        xh = xt * r
        g1 = g1_ref[...]
        xg = xh * g1
        dsh_ref[...] += jnp.sum(dh, axis=0, keepdims=True)
        dsc_ref[...] += jnp.sum(dh * xg, axis=0, keepdims=True)
        dxg = dh * (1.0 + sc_ref[...])
        dg1_ref[...] += jnp.sum(dxg * xh, axis=0, keepdims=True)
        dxh = dxg * g1
        gx_ref[...] = dx2_ref[...] + r * (dxh - xh * jnp.mean(dxh * xh, axis=-1, keepdims=True))

    return pl.pallas_call(
        body, name="dh", grid=(s_len // tm,),
        out_shape=(_sds((s_len, D_MODEL), F32), _sds((1, D_MODEL), F32), _sds((1, D_MODEL), F32),
                   _sds((1, D_MODEL), F32)),
        in_specs=[_rows(tm, W_INT), _whole((2, IN_WIDTH, HALF_D)), _whole((2, O_END - O_KR, HALF_D)),
                  _rows(tm, D_MODEL), _rows(tm, D_MODEL), _whole((1, D_MODEL)), _whole((1, D_MODEL))],
        out_specs=(_rows(tm, D_MODEL), _whole((1, D_MODEL)), _whole((1, D_MODEL)), _whole((1, D_MODEL))),
        compiler_params=_params(("arbitrary",)),
    )(dproj, w_in_t, w_kr, x, dx2, scale, g1)


def _small_call(svg, ct, dmod_sh, w, m, v):
    def body(sv_ref, ct_ref, dm_ref, w_ref, m_ref, v_ref, tot_ref, gwada_ref, d_ref, nm_ref, nv_ref):
        acc = sv_ref[0:1, :]
        for d in range(1, N_DEV):
            acc = acc + sv_ref[d:d + 1, :]
        tot_ref[...] = acc
        gg = lax.dot_general(ct_ref[...], dm_ref[...], (((1,), (0,)), ((), ())),
                             precision=lax.Precision.HIGHEST, preferred_element_type=F32)
        gwada_ref[...] = gg
        d_ref[...], nm_ref[...], nv_ref[...] = _adamw_math(w_ref[...], gg, m_ref[...], v_ref[...])

    vmem = pl.BlockSpec(memory_space=pltpu.VMEM)
    mat = _sds((D_MODEL, 768), F32)
    return pl.pallas_call(
        body, name="small_grads",
        out_shape=(_sds((1, 8 * SV_COLS), F32), mat, mat, mat, mat),
        in_specs=[vmem] * 6, out_specs=(vmem,) * 5,
        compiler_params=_params(),
    )(svg, ct, dmod_sh, w, m, v)


def _adamw_tile_rows(rows, cols):
    budget = 2 << 20
    if rows * cols * 4 <= budget or rows % 8:
        return rows
    best = 8
    for tr in range(8, rows + 1, 8):
        if rows % tr == 0 and tr * cols * 4 <= budget:
            best = tr
    return best


def _adamw_math(w, gg, m, v):
    m2 = ADAM_B1 * m + (1.0 - ADAM_B1) * gg
    v2 = ADAM_B2 * v + (1.0 - ADAM_B2) * (gg * gg)
    m_hat = m2 / (1.0 - ADAM_B1 ** ADAM_STEP)
    v_hat = v2 / (1.0 - ADAM_B2 ** ADAM_STEP)
    return -ADAM_LR * (m_hat / (jnp.sqrt(v_hat) + ADAM_EPS) + ADAM_WD * w), m2, v2


def _adamw_call(name, w, g, m, v):
    rows, cols = w.shape
    tr = _adamw_tile_rows(rows, cols)
    halves = g.ndim == 3

    def body(w_ref, g_ref, m_ref, v_ref, *out_refs):
        if halves:
            gg = jnp.concatenate([g_ref[0], g_ref[1]], axis=1)
            out_refs[0][...] = gg
        else:
            gg = g_ref[...]
        d_ref, nm_ref, nv_ref = out_refs[-3:]
        d_ref[...], nm_ref[...], nv_ref[...] = _adamw_math(w_ref[...], gg, m_ref[...], v_ref[...])

    spec = pl.BlockSpec((tr, cols), lambda i: (i, 0))
    g_spec = pl.BlockSpec((2, tr, cols // 2), lambda i: (0, i, 0)) if halves else spec
    n_out = 4 if halves else 3
    outs = pl.pallas_call(
        body, name="adamw_" + name, grid=(rows // tr,),
        out_shape=(_sds((rows, cols), F32),) * n_out,
        in_specs=[spec, g_spec, spec, spec], out_specs=(spec,) * n_out,
        compiler_params=_params(("parallel",)),
    )(w, g, m, v)
    return tuple(outs) if halves else (g,) + tuple(outs)


def _adamw_group_call(ws, gs, ms, vs):
    n = len(ws)

    def body(*refs):
        ins, outs = refs[:4 * n], refs[4 * n:]
        for k in range(n):
            w_ref, g_ref, m_ref, v_ref = ins[4 * k:4 * k + 4]
            d_ref, nm_ref, nv_ref = outs[3 * k:3 * k + 3]
            d_ref[...], nm_ref[...], nv_ref[...] = _adamw_math(w_ref[...], g_ref[...], m_ref[...], v_ref[...])

    vmem = pl.BlockSpec(memory_space=pltpu.VMEM)
    outs = pl.pallas_call(
        body, name="adamw_small_matrices",
        out_shape=tuple(_sds(w.shape, F32) for w in ws for _ in range(3)),
        in_specs=[vmem] * (4 * n), out_specs=(vmem,) * (3 * n),
        compiler_params=_params(),
    )(*[a for quad in zip(ws, gs, ms, vs) for a in quad])
    return [tuple(outs[3 * k:3 * k + 3]) for k in range(n)]


def _pack_vectors_call(vectors):
    used = sum(v.shape[1] for v in vectors)

    def body(*refs):
        out_ref = refs[-1]
        row = jnp.concatenate([r[...] for r in refs[:-1]] + [jnp.zeros((1, 8 * SV_COLS - used), F32)], axis=1)
        for r in range(8):
            out_ref[r:r + 1, :] = row[:, SV_COLS * r:SV_COLS * (r + 1)]

    vmem = pl.BlockSpec(memory_space=pltpu.VMEM)
    return pl.pallas_call(
        body, name="pack_vectors", out_shape=_sds((8, SV_COLS), F32),
        in_specs=[vmem] * len(vectors), out_specs=vmem, compiler_params=_params(),
    )(*vectors)


def _adamw_vectors_call(tot, offsets, ws, ms, vs):
    nvec = len(ws)

    def body(tot_ref, *refs):
        ins, outs = refs[:3 * nvec], refs[3 * nvec:]
        for k in range(nvec):
            w_ref, m_ref, v_ref = ins[3 * k:3 * k + 3]
            g_ref, d_ref, nm_ref, nv_ref = outs[4 * k:4 * k + 4]
            gg = tot_ref[:, offsets[k]:offsets[k] + w_ref.shape[1]]
            g_ref[...] = gg
            d_ref[...], nm_ref[...], nv_ref[...] = _adamw_math(w_ref[...], gg, m_ref[...], v_ref[...])

    vmem = pl.BlockSpec(memory_space=pltpu.VMEM)
    outs = pl.pallas_call(
        body, name="adamw_vectors",
        out_shape=tuple(_sds(w.shape, F32) for w in ws for _ in range(4)),
        in_specs=[vmem] * (1 + 3 * nvec), out_specs=(vmem,) * (4 * nvec),
        compiler_params=_params(),
    )(tot, *[a for w, m, v in zip(ws, ms, vs) for a in (w, m, v)])
    return [tuple(outs[4 * k:4 * k + 4]) for k in range(nvec)]


IN_SHARD = IN_WIDTH // N_CHIPS
HALF_D = D_MODEL // 2
SMALL_ROWS = (576, 512, 1024, 1024, 2048)
SMALL_TOTAL = sum(SMALL_ROWS)
SMALL_HALF = SMALL_TOTAL // 2
SMALL_SUM_ROWS = 432


def _gather_call(c_row, w_ada_sh, pack_in, pack_small):
    def body(c_ref, wada_ref, pki_ref, pks_ref, mg_ref, cg_ref, gwi_ref, gws_ref,
             cv, ssem_c, rsem_c, ssem_m, rsem_m, ssem_w, rsem_w, ssem_f, rsem_f, lsem):
        x, y, c = lax.axis_index("x"), lax.axis_index("y"), lax.axis_index("c")
        me = 4 * x + 2 * y + c
        chip = 2 * x + y
        rel3 = [(1, 0), (0, 1), (1, 1)]
        packs = [(pki_ref, gwi_ref), (pks_ref, gws_ref)]

        def slot(a, gw, k, h):
            return gw.at[h, k] if a == 0 else gw.at[k, h]

        sends = []
        for j, (dx, dy) in enumerate(rel3):
            for a, (pk, gw) in enumerate(packs):
                cp = pltpu.make_async_remote_copy(
                    src_ref=pk.at[c], dst_ref=slot(a, gw, chip, c), send_sem=ssem_w.at[j, a],
                    recv_sem=rsem_w.at[j, a], device_id=(_flip(x, dx), _flip(y, dy), c), device_id_type=MESH)
                cp.start()
                sends.append(cp)
        owns = []
        for a, (pk, gw) in enumerate(packs):
            for h in range(2):
                own = pltpu.make_async_copy(pk.at[h], slot(a, gw, chip, h), lsem.at[a, h])
                own.start()
                owns.append(own)

        cv[me] = c_ref[...]
        for r in range(1, N_DEV):
            dx, dy, dc = (r >> 2) & 1, (r >> 1) & 1, r & 1
            cp = pltpu.make_async_remote_copy(
                src_ref=c_ref, dst_ref=cv.at[me], send_sem=ssem_c.at[r - 1], recv_sem=rsem_c.at[r - 1],
                device_id=(_flip(x, dx), _flip(y, dy), _flip(c, dc)), device_id_type=MESH)
            cp.start()
            sends.append(cp)
        for r in range(1, N_DEV):
            dx, dy, dc = (r >> 2) & 1, (r >> 1) & 1, r & 1
            src = 4 * _flip(x, dx) + 2 * _flip(y, dy) + _flip(c, dc)
            pltpu.make_async_remote_copy(
                src_ref=c_ref, dst_ref=cv.at[src], send_sem=ssem_c.at[r - 1], recv_sem=rsem_c.at[r - 1],
                device_id=(x, y, c), device_id_type=MESH).wait_recv()
        rows = lax.broadcasted_iota(jnp.int32, (N_DEV, D_MODEL), 0)
        call = jnp.zeros((N_DEV, D_MODEL), F32)
        for b in range(N_DEV):
            call = jnp.where(rows == b, jnp.broadcast_to(cv[b], (N_DEV, D_MODEL)), call)
        cg_ref[...] = call

        mg_ref[chip] = lax.dot_general(call, wada_ref[...], (((1,), (0,)), ((), ())),
                                       precision=lax.Precision.HIGHEST, preferred_element_type=F32)
        for j, (dx, dy) in enumerate(rel3):
            cp = pltpu.make_async_remote_copy(
                src_ref=mg_ref.at[chip], dst_ref=mg_ref.at[chip], send_sem=ssem_m.at[j], recv_sem=rsem_m.at[j],
                device_id=(_flip(x, dx), _flip(y, dy), c), device_id_type=MESH)
            cp.start()
            sends.append(cp)
        for j, (dx, dy) in enumerate(rel3):
            src_chip = 2 * _flip(x, dx) + _flip(y, dy)
            pltpu.make_async_remote_copy(
                src_ref=mg_ref.at[src_chip], dst_ref=mg_ref.at[src_chip], send_sem=ssem_m.at[j],
                recv_sem=rsem_m.at[j], device_id=(x, y, c), device_id_type=MESH).wait_recv()
        for j, (dx, dy) in enumerate(rel3):
            src_chip = 2 * _flip(x, dx) + _flip(y, dy)
            for a, (pk, gw) in enumerate(packs):
                pltpu.make_async_remote_copy(
                    src_ref=pk.at[c], dst_ref=slot(a, gw, src_chip, c), send_sem=ssem_w.at[j, a],
                    recv_sem=rsem_w.at[j, a], device_id=(x, y, c), device_id_type=MESH).wait_recv()
                cp = pltpu.make_async_remote_copy(
                    src_ref=slot(a, gw, src_chip, c), dst_ref=slot(a, gw, src_chip, c), send_sem=ssem_f.at[j, a],
                    recv_sem=rsem_f.at[j, a], device_id=(x, y, 1 - c), device_id_type=MESH)
                cp.start()
                sends.append(cp)
        for j, (dx, dy) in enumerate(rel3):
            src_chip = 2 * _flip(x, dx) + _flip(y, dy)
            for a, (pk, gw) in enumerate(packs):
                pltpu.make_async_remote_copy(
                    src_ref=pk.at[c], dst_ref=slot(a, gw, src_chip, 1 - c), send_sem=ssem_f.at[j, a],
                    recv_sem=rsem_f.at[j, a], device_id=(x, y, c), device_id_type=MESH).wait_recv()
        for cp in sends:
            cp.wait_send()
        for own in owns:
            own.wait()

    vmem = pl.BlockSpec(memory_space=pltpu.VMEM)
    return pl.pallas_call(
        body, name="gather_fwd",
        out_shape=(_sds((N_CHIPS, N_DEV, 768), F32), _sds((N_DEV, D_MODEL), F32),
                   _sds((2, N_CHIPS, IN_SHARD, HALF_D), BF16), _sds((N_CHIPS, 2, SMALL_HALF, LANES), BF16)),
        in_specs=[vmem, vmem, vmem, vmem], out_specs=(vmem, vmem, vmem, vmem),
        scratch_shapes=[
            pltpu.VMEM((N_DEV, 1, D_MODEL), F32),
            pltpu.SemaphoreType.DMA((N_DEV - 1,)), pltpu.SemaphoreType.DMA((N_DEV - 1,)),
            pltpu.SemaphoreType.DMA((3,)), pltpu.SemaphoreType.DMA((3,)),
            pltpu.SemaphoreType.DMA((3, 2)), pltpu.SemaphoreType.DMA((3, 2)),
            pltpu.SemaphoreType.DMA((3, 2)), pltpu.SemaphoreType.DMA((3, 2)),
            pltpu.SemaphoreType.DMA((2, 2)),
        ],
        compiler_params=_params(),
    )(c_row, w_ada_sh, pack_in, pack_small)


def _reduce_call(g_in, g_small, sv):
    def body(gi_ref, gs_ref, sv_ref, fi_ref, fs_ref, svg_ref, pair_i, pair_s, send_i, send_s, land_i, land_s,
             ssem_p, rsem_p, ssem_g, rsem_g, ssem_s, rsem_s, ssem_x, rsem_x):
        x, y, c = lax.axis_index("x"), lax.axis_index("y"), lax.axis_index("c")
        me = 4 * x + 2 * y + c
        chip = 2 * x + y
        rel3 = [(1, 0), (0, 1), (1, 1)]
        payloads = [(gi_ref, pair_i, send_i, land_i, fi_ref), (gs_ref, pair_s, send_s, land_s, fs_ref)]
        where = [lambda k, h: N_CHIPS * h + k, lambda k, h: 2 * k + h]
        copies = []

        for k in range(N_CHIPS):
            for a, (g, pair, _, _, _) in enumerate(payloads):
                cp = pltpu.make_async_remote_copy(
                    src_ref=g.at[where[a](k, 1 - c)], dst_ref=pair.at[k], send_sem=ssem_p.at[k, a],
                    recv_sem=rsem_p.at[k, a], device_id=(x, y, 1 - c), device_id_type=MESH)
                cp.start()
                copies.append(cp)

        for r in range(1, N_DEV):
            dx, dy, dc = (r >> 2) & 1, (r >> 1) & 1, r & 1
            cp = pltpu.make_async_remote_copy(
                src_ref=sv_ref, dst_ref=svg_ref.at[me], send_sem=ssem_s.at[r - 1], recv_sem=rsem_s.at[r - 1],
                device_id=(_flip(x, dx), _flip(y, dy), _flip(c, dc)), device_id_type=MESH)
            cp.start()
            copies.append(cp)
        svg_ref[me] = sv_ref[...]

        def pair_sum(k, store_in, store_small):
            for a, (g, pair, _, _, _) in enumerate(payloads):
                pltpu.make_async_remote_copy(
                    src_ref=g.at[where[a](k, c)], dst_ref=pair.at[k], send_sem=ssem_p.at[k, a],
                    recv_sem=rsem_p.at[k, a], device_id=(x, y, c), device_id_type=MESH).wait_recv()
            for qd in range(HALF_D // LANES):
                sl = slice(LANES * qd, LANES * qd + LANES)
                store_in(sl, gi_ref[where[0](k, c), :, sl].astype(F32) + pair_i[k, :, sl].astype(F32))

            def rows(i, carry):
                sl = pl.ds(pl.multiple_of(i * SMALL_SUM_ROWS, 16), SMALL_SUM_ROWS)
                store_small(sl, gs_ref[where[1](k, c), sl, :].astype(F32) + pair_s[k, sl, :].astype(F32))
                return carry

            lax.fori_loop(0, SMALL_HALF // SMALL_SUM_ROWS, rows, 0)

        for j, (dx, dy) in enumerate(rel3):
            tx, ty = _flip(x, dx), _flip(y, dy)

            def put_in(sl, val, j=j):
                send_i[j, :, sl] = val.astype(BF16)

            def put_small(sl, val, j=j):
                send_s[j, sl, :] = val.astype(BF16)

            pair_sum(2 * tx + ty, put_in, put_small)
            for a, (_, _, send, land, _) in enumerate(payloads):
                cp = pltpu.make_async_remote_copy(
                    src_ref=send.at[j], dst_ref=land.at[j], send_sem=ssem_g.at[j, a], recv_sem=rsem_g.at[j, a],
                    device_id=(tx, ty, c), device_id_type=MESH)
                cp.start()
                copies.append(cp)

        def own_in(sl, val):
            fi_ref[c, :, sl] = val

        def own_small(sl, val):
            fs_ref[c, sl, :] = val

        pair_sum(chip, own_in, own_small)
        for j in range(3):
            for a, (_, _, send, land, _) in enumerate(payloads):
                pltpu.make_async_remote_copy(
                    src_ref=send.at[j], dst_ref=land.at[j], send_sem=ssem_g.at[j, a], recv_sem=rsem_g.at[j, a],
                    device_id=(x, y, c), device_id_type=MESH).wait_recv()
            for qd in range(HALF_D // LANES):
                sl = slice(LANES * qd, LANES * qd + LANES)
                fi_ref[c, :, sl] += land_i[j, :, sl].astype(F32)

            def add_rows(i, carry, j=j):
                sl = pl.ds(pl.multiple_of(i * SMALL_SUM_ROWS, 16), SMALL_SUM_ROWS)
                fs_ref[c, sl, :] += land_s[j, sl, :].astype(F32)
                return carry

            lax.fori_loop(0, SMALL_HALF // SMALL_SUM_ROWS, add_rows, 0)

        for a, f in enumerate((fi_ref, fs_ref)):
            cp = pltpu.make_async_remote_copy(
                src_ref=f.at[c], dst_ref=f.at[c], send_sem=ssem_x.at[a], recv_sem=rsem_x.at[a],
                device_id=(x, y, 1 - c), device_id_type=MESH)
            cp.start()
            copies.append(cp)
        for a, f in enumerate((fi_ref, fs_ref)):
            pltpu.make_async_remote_copy(
                src_ref=f.at[c], dst_ref=f.at[1 - c], send_sem=ssem_x.at[a], recv_sem=rsem_x.at[a],
                device_id=(x, y, c), device_id_type=MESH).wait_recv()
        for r in range(1, N_DEV):
            dx, dy, dc = (r >> 2) & 1, (r >> 1) & 1, r & 1
            src = 4 * _flip(x, dx) + 2 * _flip(y, dy) + _flip(c, dc)
            pltpu.make_async_remote_copy(
                src_ref=sv_ref, dst_ref=svg_ref.at[src], send_sem=ssem_s.at[r - 1],
                recv_sem=rsem_s.at[r - 1], device_id=(x, y, c), device_id_type=MESH).wait_recv()
        for cp in copies:
            cp.wait_send()

    vmem = pl.BlockSpec(memory_space=pltpu.VMEM)
    return pl.pallas_call(
        body, name="grad_reduce",
        out_shape=(_sds((2, IN_SHARD, HALF_D), F32), _sds((2, SMALL_HALF, LANES), F32),
                   _sds((N_DEV, 8, SV_COLS), F32)),
        in_specs=[vmem, vmem, vmem], out_specs=(vmem, vmem, vmem),
        scratch_shapes=[
            pltpu.VMEM((N_CHIPS, IN_SHARD, HALF_D), BF16), pltpu.VMEM((N_CHIPS, SMALL_HALF, LANES), BF16),
            pltpu.VMEM((3, IN_SHARD, HALF_D), BF16), pltpu.VMEM((3, SMALL_HALF, LANES), BF16),
            pltpu.VMEM((3, IN_SHARD, HALF_D), BF16), pltpu.VMEM((3, SMALL_HALF, LANES), BF16),
            pltpu.SemaphoreType.DMA((N_CHIPS, 2)), pltpu.SemaphoreType.DMA((N_CHIPS, 2)),
            pltpu.SemaphoreType.DMA((3, 2)), pltpu.SemaphoreType.DMA((3, 2)),
            pltpu.SemaphoreType.DMA((N_DEV - 1,)), pltpu.SemaphoreType.DMA((N_DEV - 1,)),
            pltpu.SemaphoreType.DMA((2,)), pltpu.SemaphoreType.DMA((2,)),
        ],
        compiler_params=_params(),
    )(g_in, g_small, sv)


def _dwin_call(h, dproj):
    s_len = h.shape[0]
    tm = min(4 * ROW_TILE, s_len)
    nrow = s_len // tm
    nc = 2
    chunk = W_INT // nc

    def body(h_ref, dp_ref, dw_ref, acc):
        c, i = pl.program_id(0), pl.program_id(1)

        @pl.when(i == 0)
        def _():
            acc[...] = jnp.zeros_like(acc)

        acc[...] += _dot_tn(dp_ref[...], h_ref[...])

        def put(lo, hi, dst):
            for half in range(2):
                dw_ref[half, dst:dst + hi - lo, :] = acc[lo:hi, half * HALF_D:(half + 1) * HALF_D].astype(BF16)

        for cc in range(nc):
            @pl.when((c == cc) & (i == nrow - 1))
            def _(cc=cc):
                base = cc * chunk
                lo, hi = base, min(base + chunk, O_MZ)
                if lo < hi:
                    put(lo - base, hi - base, lo)
                lo, hi = max(base, O_MZ), min(base + chunk, O_KR)
                if lo < hi:
                    put(lo - base, hi - base, lo + ROPE_DIM)
                if base <= O_KR and O_END <= base + chunk:
                    r = O_KR - base
                    half_r = ROPE_DIM // 2
                    kr = sum(acc[r + ROPE_DIM * j:r + ROPE_DIM * (j + 1), :] for j in range(4))
                    sw = sum(acc[r + 128 + ROPE_DIM * j:r + 128 + ROPE_DIM * (j + 1), :] for j in range(4))
                    tot = kr + jnp.concatenate([sw[half_r:], sw[:half_r]], axis=0)
                    for half in range(2):
                        dw_ref[half, O_MZ:O_MZ + ROPE_DIM, :] = (
                            tot[:, half * HALF_D:(half + 1) * HALF_D].astype(BF16))

    assert O_KR >= (nc - 1) * chunk
    return pl.pallas_call(
        body, name="dwin", grid=(nc, nrow),
        out_shape=_sds((2, IN_WIDTH, HALF_D), BF16),
        in_specs=[pl.BlockSpec((tm, D_MODEL), lambda c, i: (i, 0)),
                  pl.BlockSpec((tm, chunk), lambda c, i: (i, c))],
        out_specs=pl.BlockSpec((2, IN_WIDTH, HALF_D), lambda c, i: (0, 0, 0)),
        scratch_shapes=[pltpu.VMEM((chunk, D_MODEL), F32)],
        compiler_params=_params(("arbitrary", "arbitrary")),
    )(h, dproj)


def _internal_weights(w_in_t, w_uq, w_ukv):
    krot_t = w_in_t[:, O_MZ:O_MZ + ROPE_DIM]
    krot_sw = krot_t.reshape(2, 2, ROPE_DIM // 2, HALF_D)[:, ::-1].reshape(2, ROPE_DIM, HALF_D)
    w_kr = jnp.concatenate([jnp.tile(krot_t, (1, 4, 1)), jnp.tile(krot_sw, (1, 4, 1))], axis=1)
    uq = w_uq.reshape(Q_RANK, N_HEADS, 96)
    wp = uq[:, :, 64:].reshape(Q_RANK, 256)
    w_q = jnp.concatenate([uq[:, :, :64].reshape(Q_RANK, 512), wp, _swap_halves(wp, 32)], axis=1)
    ukv = w_ukv.reshape(KV_RANK, N_HEADS, 128)
    w_kv = jnp.concatenate([ukv[:, :, :64].reshape(KV_RANK, 512), ukv[:, :, 64:].reshape(KV_RANK, 512)], axis=1)
    return w_kr, w_q, w_kv


def _true_weight_grads(dwq, dwkv):
    dwp = dwq[:, 512:768] + _swap_halves(dwq[:, 768:1024], 32)
    g_uq = jnp.concatenate([dwq[:, :512].reshape(Q_RANK, N_HEADS, 64), dwp.reshape(Q_RANK, N_HEADS, 32)],
                           axis=2).reshape(Q_RANK, 768)
    g_ukv = jnp.concatenate([dwkv[:, :512].reshape(KV_RANK, N_HEADS, 64), dwkv[:, 512:].reshape(KV_RANK, N_HEADS, 64)],
                            axis=2).reshape(KV_RANK, 1024)
    return g_uq, g_ukv


def _swap_halves(w, group):
    r, n = w.shape
    return w.reshape(r, n // group, 2, group // 2)[:, :, ::-1, :].reshape(r, n)


def _pack_shards(parts):
    return jnp.concatenate([p.reshape(-1, LANES) for p in parts], axis=0)


def _unpack_small(gw):
    offs = [0]
    for r in SMALL_ROWS:
        offs.append(offs[-1] + r)

    def cols(i, rows, shard_cols):
        blk = gw[:, offs[i]:offs[i + 1]].reshape(N_CHIPS, rows, shard_cols)
        return blk.transpose(1, 0, 2).reshape(rows, N_CHIPS * shard_cols)

    return (cols(0, Q_RANK, 192), cols(1, KV_RANK, 256), cols(2, 512, 256), cols(3, 512, 256),
            gw[:, offs[4]:offs[5]].reshape(D_MODEL, D_MODEL))


def _chip_major(g, shard_cols):
    r = g.shape[0]
    return g.reshape(r, N_CHIPS, shard_cols).transpose(1, 0, 2).reshape(N_CHIPS, -1, LANES)


def kernel(x, c, positions, w_ada, b_ada, norm_gain, w_in, q_norm_gain, w_uq, kv_norm_gain, w_ukv, w_branch_a, w_branch_b, w_out, final_norm_gain, loss_target, m_w_ada, m_b_ada, m_norm_gain, m_w_in, m_q_norm_gain, m_w_uq, m_kv_norm_gain, m_w_ukv, m_w_branch_a, m_w_branch_b, m_w_out, m_final_norm_gain, v_w_ada, v_b_ada, v_norm_gain, v_w_in, v_q_norm_gain, v_w_uq, v_kv_norm_gain, v_w_ukv, v_w_branch_a, v_w_branch_b, v_w_out, v_final_norm_gain):
    ix, iy, ic = lax.axis_index("x"), lax.axis_index("y"), lax.axis_index("c")
    me = 4 * ix + 2 * iy + ic
    chip = 2 * ix + iy
    xs = x[0]
    tgt = loss_target[0]
    s_len = xs.shape[0]

    w_in_t = jnp.swapaxes(w_in[0], 0, 1)
    w_in_tb = w_in_t.astype(BF16)
    pack_in = jnp.stack([w_in_tb[:, :HALF_D], w_in_tb[:, HALF_D:]], axis=0)
    small_shards = (w_uq[0], w_ukv[0], w_branch_a[0], w_branch_b[0], w_out[0])
    pack_small = _pack_shards([s.astype(BF16) for s in small_shards]).reshape(2, SMALL_HALF, LANES)
    mg, call, gw_in, gw_small = _gather_call(c, w_ada[0], pack_in, pack_small)
    mod = mg.transpose(1, 0, 2).reshape(N_DEV, 3 * D_MODEL) + b_ada
    mod_me = lax.dynamic_slice_in_dim(mod, me, 1, axis=0)
    shift, scale, gate = mod_me[:, :D_MODEL], mod_me[:, D_MODEL:2 * D_MODEL], mod_me[:, 2 * D_MODEL:]

    f_in_t = gw_in.reshape(2, IN_WIDTH, HALF_D)
    f_uq, f_ukv, f_a, f_b, f_out = _unpack_small(gw_small.reshape(N_CHIPS, SMALL_TOTAL, LANES))
    w_kr, w_q, w_kv = _internal_weights(f_in_t, f_uq, f_ukv)

    inv_freq = ROPE_BASE ** (-jnp.arange(0, ROPE_DIM, 2, dtype=F32) / ROPE_DIM)
    ang = positions[0].astype(F32)[:, None] * inv_freq
    cs, sn = jnp.cos(ang), jnp.sin(ang)
    cos128 = jnp.tile(jnp.concatenate([cs, cs], axis=1), (1, 4))
    sin128 = jnp.tile(jnp.concatenate([-sn, sn], axis=1), (1, 4))

    (h, sq, sk, sv, sz, cq, ckv, mz, ga, gb, kpt, qn, qp, kn, vv) = _inproj_call(
        xs, shift, scale, norm_gain, f_in_t, w_kr, w_q, w_kv, q_norm_gain, kv_norm_gain, cos128, sin128)
    oa, lt, first = _sb_fwd_call(sq, sk, sv)
    ob, lse = _mla_fwd_call(qn, qp, kn, kpt, vv)

    gf = final_norm_gain.reshape(1, D_MODEL)
    (dx2, doa, dob, dsz, dmz, dga, dgb, dwo, dwa, dwb, dgf, dgate, loss_p) = _post_call(
        xs, tgt, oa, ob, sz, mz, ga, gb, gate, gf, f_a, f_b, f_out)

    dsq, dsk_t, dsv_t = _sb_bwd_call(first[:, :, 0, 0].reshape(-1), sq, sk, sv, doa, lt)
    dqn, dqp, dkn_t, dkpt_t, dvv_t = _mla_bwd_call(qn, qp, kn, kpt, vv, ob, dob, lse)

    dproj, dwq, dwkv, dqg, dkvg = _bwdprep_call(
        dsq, dsk_t, dsv_t, dsz, dqn, dqp, dkn_t, dvv_t, dkpt_t, dmz, dga, dgb, cq, ckv, cos128, sin128,
        q_norm_gain, kv_norm_gain, w_q, w_kv)
    grad_x, dshift, dscale, dg1 = _dh_call(dproj, f_in_t, w_kr, xs, dx2, scale, norm_gain)
    g_in_t = _dwin_call(h, dproj)
    g_uq, g_ukv = _true_weight_grads(dwq, dwkv)

    g_in_pieces = g_in_t.reshape(N_DEV, IN_SHARD, HALF_D)
    g_small = jnp.concatenate([
        _chip_major(g_uq, 192).astype(BF16), _chip_major(g_ukv, 256).astype(BF16),
        dwa.reshape(N_CHIPS, -1, LANES), dwb.reshape(N_CHIPS, -1, LANES),
        dwo.reshape(N_CHIPS, -1, LANES)], axis=1).reshape(N_DEV, SMALL_HALF, LANES)
    small = _pack_vectors_call([dshift, dscale, dgate, dg1, dqg, dkvg, dgf, loss_p])
    full_in, full_small, svg = _reduce_call(g_in_pieces, g_small, small)
    full = full_small.reshape(SMALL_TOTAL, LANES)
    offs = [0]
    for r in SMALL_ROWS:
        offs.append(offs[-1] + r)
    gs_uq = full[offs[0]:offs[1]].reshape(Q_RANK, 192)
    gs_ukv = full[offs[1]:offs[2]].reshape(KV_RANK, 256)
    gs_a = full[offs[2]:offs[3]].reshape(512, 256)
    gs_b = full[offs[3]:offs[4]].reshape(512, 256)
    gs_out = full[offs[4]:offs[5]].reshape(256, D_MODEL)

    svm = svg.reshape(N_DEV, 8 * SV_COLS)
    dmod_sh = lax.dynamic_slice_in_dim(svm[:, :3 * D_MODEL], chip * 768, 768, axis=1)
    tot, *ada_outs = _small_call(svm, call.T, dmod_sh, w_ada[0], m_w_ada[0], v_w_ada[0])
    vec_offsets = {"b_ada": 0, "norm_gain": 3072, "q_norm_gain": 4096, "kv_norm_gain": 4480, "final_norm_gain": 4736}
    loss = tot[0, 5760]

    names = ["w_ada", "b_ada", "norm_gain", "w_in", "q_norm_gain", "w_uq", "kv_norm_gain", "w_ukv",
             "w_branch_a", "w_branch_b", "w_out", "final_norm_gain"]
    ws = [w_ada[0], b_ada, norm_gain, w_in_t, q_norm_gain, w_uq[0], kv_norm_gain, w_ukv[0],
          w_branch_a[0], w_branch_b[0], w_out[0], final_norm_gain.reshape(1, D_MODEL)]
    gs = [None, None, None, full_in, None, gs_uq, None, gs_ukv, gs_a, gs_b, gs_out, None]
    ms = [m_w_ada[0], m_b_ada, m_norm_gain, jnp.swapaxes(m_w_in[0], 0, 1), m_q_norm_gain, m_w_uq[0],
          m_kv_norm_gain, m_w_ukv[0], m_w_branch_a[0], m_w_branch_b[0], m_w_out[0],
          m_final_norm_gain.reshape(1, D_MODEL)]
    vs = [v_w_ada[0], v_b_ada, v_norm_gain, jnp.swapaxes(v_w_in[0], 0, 1), v_q_norm_gain, v_w_uq[0],
          v_kv_norm_gain, v_w_ukv[0], v_w_branch_a[0], v_w_branch_b[0], v_w_out[0],
          v_final_norm_gain.reshape(1, D_MODEL)]
    refs = [w_ada, b_ada, norm_gain, w_in, q_norm_gain, w_uq, kv_norm_gain, w_ukv,
            w_branch_a, w_branch_b, w_out, final_norm_gain]
    vec_ids = [k for k, n in enumerate(names) if n in vec_offsets]
    vec_outs = dict(zip(vec_ids, _adamw_vectors_call(
        tot, [vec_offsets[names[k]] for k in vec_ids], [ws[k] for k in vec_ids], [ms[k] for k in vec_ids],
        [vs[k] for k in vec_ids])))
    mat_ids = [names.index(n) for n in ("w_uq", "w_ukv", "w_branch_a", "w_branch_b", "w_out")]
    for k, o_ in zip(mat_ids, _adamw_group_call(*[[lst[k] for k in mat_ids] for lst in (ws, gs, ms, vs)])):
        vec_outs[k] = (gs[k],) + o_
    vec_outs[names.index("w_ada")] = tuple(ada_outs)
    grads, deltas, new_ms, new_vs = [], [], [], []
    for k, (n, w_, g_, m_, v_, ref) in enumerate(zip(names, ws, gs, ms, vs, refs)):
        outs = vec_outs[k] if k in vec_outs else _adamw_call(n, w_, g_, m_, v_)
        if n == "w_in":
            outs = tuple(jnp.swapaxes(o_, 0, 1) for o_ in outs)
        for lst, o_ in zip((grads, deltas, new_ms, new_vs), outs):
            lst.append(o_.reshape(ref.shape))

    return (loss, grad_x.reshape(x.shape), *grads, *deltas, *new_ms, *new_vs)
```
